```python
import math
import jax, jax.numpy as jnp
from jax import lax
import numpy as np

D_MODEL = 1024
BATCH = 8
SEQ = 8192
DEPTH = 2

N_A_LAYERS = DEPTH // 2
N_B_LAYERS = DEPTH - N_A_LAYERS
HG_HEADS = 8
HG_DIM = D_MODEL // HG_HEADS
HG_CHUNK = 32
SW_Q_HEADS = 16
SW_KV_HEADS = 4
SW_HEAD_DIM = D_MODEL // SW_Q_HEADS
SW_GROUP = SW_Q_HEADS // SW_KV_HEADS
SW_WINDOW = 128
REL_BUCKETS = 32
REL_MAX_DIST = 128
FFN_DIM = 2816
CONV_WIDTH = 3
ALPHA = (2.0 * DEPTH) ** 0.25
BETA = (8.0 * DEPTH) ** -0.25
LN_EPS = 1e-5
RMS_EPS = 1e-6

kernel_name = "yoco_hgrn2_swa_sink_convffn"


def layer_norm(x, g, b):
    xf = x.astype(jnp.float32)
    mu = xf.mean(-1, keepdims=True)
    var = jnp.square(xf - mu).mean(-1, keepdims=True)
    y = (xf - mu) * lax.rsqrt(var + LN_EPS) * g.astype(jnp.float32) + b.astype(jnp.float32)
    return y.astype(x.dtype)


def hgrn2_chunkwise(q, k, v, log_f):
    B, S, H, Dk = q.shape
    Dv = v.shape[-1]
    n = S // HG_CHUNK

    def chunks(a):
        return a.reshape(B, n, HG_CHUNK, H, a.shape[-1]).transpose(1, 0, 3, 2, 4)

    qc, kc, vc = chunks(q), chunks(k), chunks(v)
    bc = jnp.cumsum(chunks(log_f), axis=3)
    causal = jnp.tril(jnp.ones((HG_CHUNK, HG_CHUNK), dtype=bool))[:, :, None]

    def step(state, inp):
        q_, k_, v_, b_ = inp
        o_inter = jnp.einsum('bhtk,bhkv->bhtv', q_ * jnp.exp(b_), state)
        diff = b_[:, :, :, None, :] - b_[:, :, None, :, :]
        decay = jnp.where(causal, jnp.exp(jnp.minimum(diff, 0.0)), 0.0)
        scores = jnp.einsum('bhtsk,bhsk->bhts', q_[:, :, :, None, :] * decay, k_)
        o = o_inter + jnp.einsum('bhts,bhsv->bhtv', scores, v_)
        b_last = b_[:, :, -1, :]
        k_dec = k_ * jnp.exp(b_last[:, :, None, :] - b_)
        state = jnp.exp(b_last)[..., None] * state + jnp.einsum('bhsk,bhsv->bhkv', k_dec, v_)
        return state, o

    state0 = jnp.zeros((B, H, Dk, Dv), jnp.float32)
    _, o = lax.scan(step, state0, (qc, kc, vc, bc))
    return o.transpose(1, 0, 3, 2, 4).reshape(B, S, H, Dv)


def hgrn2_mixer(h, w_in, lower_bound, g_norm_w, w_out):
    B, S, D = h.shape
    q, f, i, g = jnp.split(h @ w_in, 4, axis=-1)

    def heads(a):
        return a.reshape(B, S, HG_HEADS, HG_DIM).astype(jnp.float32)

    lb = lower_bound.astype(jnp.float32).reshape(HG_HEADS, HG_DIM)
    fg = lb + (1.0 - lb) * jax.nn.sigmoid(heads(f))
    o = hgrn2_chunkwise(jax.nn.silu(heads(q)), 1.0 - fg, heads(i), jnp.log(fg))
    o = o * lax.rsqrt(jnp.mean(jnp.square(o), -1, keepdims=True) + RMS_EPS)
    o = o * g_norm_w.astype(jnp.float32) * jax.nn.silu(heads(g))
    return o.reshape(B, S, D).astype(h.dtype) @ w_out


def t5_causal_bucket(dist):
    exact = REL_BUCKETS // 2
    d = jnp.maximum(dist, 1).astype(jnp.float32)
    log_b = exact + (jnp.log(d / exact) / math.log(REL_MAX_DIST / exact)
                     * (REL_BUCKETS - exact)).astype(jnp.int32)
    return jnp.where(dist < exact, dist, jnp.minimum(log_b, REL_BUCKETS - 1))


def banded_bias_and_mask(rel_table, n_blocks):
    t = jnp.arange(SW_WINDOW)[:, None] + SW_WINDOW
    s = jnp.arange(2 * SW_WINDOW)[None, :]
    dist = t - s
    bias = rel_table[t5_causal_bucket(jnp.maximum(dist, 0))].transpose(2, 0, 1)
    band = (dist >= 0) & (dist < SW_WINDOW)
    has_prev = (jnp.arange(n_blocks) > 0)[:, None, None]
    mask = band[None] & (has_prev | (s >= SW_WINDOW)[None])
    return bias, mask


def swa_sink_mixer(h, k, v, w_q, sinks, bias, mask, w_out):
    B, S, D = h.shape
    nb = S // SW_WINDOW
    q = (h @ w_q).reshape(B, nb, SW_WINDOW, SW_KV_HEADS, SW_GROUP, SW_HEAD_DIM)

    def with_prev(a):
        ab = a.reshape(B, nb, SW_WINDOW, SW_KV_HEADS, SW_HEAD_DIM)
        prev = jnp.pad(ab, ((0, 0), (1, 0), (0, 0), (0, 0), (0, 0)))[:, :-1]
        return jnp.concatenate([prev, ab], axis=2)

    kk, vv = with_prev(k), with_prev(v)
    scale = SW_HEAD_DIM ** -0.5
    logits = jnp.einsum('bntgrd,bnsgd->bngrts', q, kk).astype(jnp.float32) * scale
    logits = logits + bias.astype(jnp.float32).reshape(SW_KV_HEADS, SW_GROUP, SW_WINDOW, 2 * SW_WINDOW)
    logits = jnp.where(mask[None, :, None, None], logits, -jnp.inf)
    sink = sinks.astype(jnp.float32).reshape(1, 1, SW_KV_HEADS, SW_GROUP, 1)
    m = jnp.maximum(logits.max(-1), sink)
    p = jnp.exp(logits - m[..., None])
    denom = p.sum(-1) + jnp.exp(sink - m)
    o = jnp.einsum('bngrts,bnsgd->bntgrd', p, vv.astype(jnp.float32))
    o = o / jnp.moveaxis(denom, -1, 2)[..., None]
    return o.reshape(B, S, D).astype(h.dtype) @ w_out


def conv_ffn(h, w_in, conv_w, conv_b, w_out):
    u = h @ w_in
    C = u.shape[-1]
    u = lax.conv_general_dilated(u, conv_w[:, None, :], window_strides=(1,),
                                 padding=[(CONV_WIDTH - 1, 0)],
                                 dimension_numbers=('NWC', 'WIO', 'NWC'),
                                 feature_group_count=C) + conv_b
    a, b = jnp.split(u, 2, axis=-1)
    return (jax.nn.silu(a) * b) @ w_out


def _fwd_setup_inputs(seed: int = 0) -> dict:
    key = jax.random.key(seed)
    ks = jax.random.split(key, 24)
    D, F = D_MODEL, FFN_DIM
    kv_dim = SW_KV_HEADS * SW_HEAD_DIM
    nrm = jax.random.normal

    x = nrm(ks[0], (BATCH, SEQ, D), jnp.float32)

    hgrn_w_in = nrm(ks[1], (N_A_LAYERS, D, 4 * D), jnp.float32) * D ** -0.5
    hgrn_w_in = hgrn_w_in.at[..., 2 * D:3 * D].multiply(BETA)
    hgrn_lb_logits = nrm(ks[2], (N_A_LAYERS + 1, D), jnp.float32) * 0.5
    hgrn_gnorm_w = 1.0 + 0.02 * nrm(ks[3], (N_A_LAYERS, HG_DIM), jnp.float32)
    hgrn_w_out = nrm(ks[4], (N_A_LAYERS, D, D), jnp.float32) * D ** -0.5 * BETA

    swa_w_q = nrm(ks[5], (N_B_LAYERS, D, D), jnp.float32) * D ** -0.5
    swa_sinks = nrm(ks[6], (N_B_LAYERS, SW_Q_HEADS), jnp.float32) * 0.5
    swa_w_out = nrm(ks[7], (N_B_LAYERS, D, D), jnp.float32) * D ** -0.5 * BETA
    shared_w_kv = nrm(ks[8], (D, 2 * kv_dim), jnp.float32) * D ** -0.5
    shared_w_kv = shared_w_kv.at[:, kv_dim:].multiply(BETA)
    rel_bias = nrm(ks[9], (REL_BUCKETS, SW_Q_HEADS), jnp.float32) * 0.5

    ffn_w_in = nrm(ks[10], (DEPTH, D, 2 * F), jnp.float32) * D ** -0.5 * BETA
    ffn_conv_w = nrm(ks[11], (DEPTH, CONV_WIDTH, 2 * F), jnp.float32) * CONV_WIDTH ** -0.5
    ffn_conv_b = nrm(ks[12], (DEPTH, 2 * F), jnp.float32) * 0.02
    ffn_w_out = nrm(ks[13], (DEPTH, F, D), jnp.float32) * F ** -0.5 * BETA

    ln_mix_g = 1.0 + 0.02 * nrm(ks[14], (DEPTH, D), jnp.float32)
    ln_mix_b = 0.02 * nrm(ks[15], (DEPTH, D), jnp.float32)
    ln_ffn_g = 1.0 + 0.02 * nrm(ks[16], (DEPTH, D), jnp.float32)
    ln_ffn_b = 0.02 * nrm(ks[17], (DEPTH, D), jnp.float32)

    return {"x": x, "hgrn_w_in": hgrn_w_in, "hgrn_lb_logits": hgrn_lb_logits,
            "hgrn_gnorm_w": hgrn_gnorm_w, "hgrn_w_out": hgrn_w_out,
            "swa_w_q": swa_w_q, "swa_sinks": swa_sinks, "swa_w_out": swa_w_out,
            "shared_w_kv": shared_w_kv, "rel_bias": rel_bias,
            "ffn_w_in": ffn_w_in, "ffn_conv_w": ffn_conv_w, "ffn_conv_b": ffn_conv_b,
            "ffn_w_out": ffn_w_out, "ln_mix_g": ln_mix_g, "ln_mix_b": ln_mix_b,
            "ln_ffn_g": ln_ffn_g, "ln_ffn_b": ln_ffn_b}


def _fwd_reference(x, hgrn_w_in, hgrn_lb_logits, hgrn_gnorm_w, hgrn_w_out,
              swa_w_q, swa_sinks, swa_w_out, shared_w_kv, rel_bias,
              ffn_w_in, ffn_conv_w, ffn_conv_b, ffn_w_out,
              ln_mix_g, ln_mix_b, ln_ffn_g, ln_ffn_b):
    B, S, D = x.shape
    n_blocks = S // SW_WINDOW
    lower_bounds = jnp.cumsum(jax.nn.softmax(hgrn_lb_logits.astype(jnp.float32), axis=0), axis=0)
    bias, mask = banded_bias_and_mask(rel_bias, n_blocks)

    h = x
    k_shared = v_shared = None
    for layer in range(DEPTH):
        if layer < N_A_LAYERS:
            mix = hgrn2_mixer(h, hgrn_w_in[layer], lower_bounds[layer],
                              hgrn_gnorm_w[layer], hgrn_w_out[layer])
        else:
            j = layer - N_A_LAYERS
            mix = swa_sink_mixer(h, k_shared, v_shared, swa_w_q[j], swa_sinks[j],
                                 bias, mask, swa_w_out[j])
        h = layer_norm(ALPHA * h + mix, ln_mix_g[layer], ln_mix_b[layer])
        ff = conv_ffn(h, ffn_w_in[layer], ffn_conv_w[layer], ffn_conv_b[layer], ffn_w_out[layer])
        h = layer_norm(ALPHA * h + ff, ln_ffn_g[layer], ln_ffn_b[layer])
        if layer == N_A_LAYERS - 1:
            k_flat, v_flat = jnp.split(h @ shared_w_kv, 2, axis=-1)
            k_shared = k_flat.reshape(B, S, SW_KV_HEADS, SW_HEAD_DIM)
            v_shared = v_flat.reshape(B, S, SW_KV_HEADS, SW_HEAD_DIM)
    return h


import jax as _jax
import jax.numpy as _jnp

TWIN_FORMAT = 'train_step'
FWD_PARAMS = ['x', 'hgrn_w_in', 'hgrn_lb_logits', 'hgrn_gnorm_w', 'hgrn_w_out', 'swa_w_q', 'swa_sinks', 'swa_w_out', 'shared_w_kv', 'rel_bias', 'ffn_w_in', 'ffn_conv_w', 'ffn_conv_b', 'ffn_w_out', 'ln_mix_g', 'ln_mix_b', 'ln_ffn_g', 'ln_ffn_b']
TWIN_WEIGHTS = ['hgrn_w_in', 'hgrn_lb_logits', 'hgrn_gnorm_w', 'hgrn_w_out', 'swa_w_q', 'swa_sinks', 'swa_w_out', 'shared_w_kv', 'rel_bias', 'ffn_w_in', 'ffn_conv_w', 'ffn_conv_b', 'ffn_w_out', 'ln_mix_g', 'ln_mix_b', 'ln_ffn_g', 'ln_ffn_b']
TWIN_DIFF_INPUT = 'x'
TWIN_INPUTS = ['x', 'hgrn_w_in', 'hgrn_lb_logits', 'hgrn_gnorm_w', 'hgrn_w_out', 'swa_w_q', 'swa_sinks', 'swa_w_out', 'shared_w_kv', 'rel_bias', 'ffn_w_in', 'ffn_conv_w', 'ffn_conv_b', 'ffn_w_out', 'ln_mix_g', 'ln_mix_b', 'ln_ffn_g', 'ln_ffn_b', 'loss_target', 'm_hgrn_w_in', 'm_hgrn_lb_logits', 'm_hgrn_gnorm_w', 'm_hgrn_w_out', 'm_swa_w_q', 'm_swa_sinks', 'm_swa_w_out', 'm_shared_w_kv', 'm_rel_bias', 'm_ffn_w_in', 'm_ffn_conv_w', 'm_ffn_conv_b', 'm_ffn_w_out', 'm_ln_mix_g', 'm_ln_mix_b', 'm_ln_ffn_g', 'm_ln_ffn_b', 'v_hgrn_w_in', 'v_hgrn_lb_logits', 'v_hgrn_gnorm_w', 'v_hgrn_w_out', 'v_swa_w_q', 'v_swa_sinks', 'v_swa_w_out', 'v_shared_w_kv', 'v_rel_bias', 'v_ffn_w_in', 'v_ffn_conv_w', 'v_ffn_conv_b', 'v_ffn_w_out', 'v_ln_mix_g', 'v_ln_mix_b', 'v_ln_ffn_g', 'v_ln_ffn_b']
TWIN_OUTPUTS = ['loss', 'grad_x', 'grad_hgrn_w_in', 'grad_hgrn_lb_logits', 'grad_hgrn_gnorm_w', 'grad_hgrn_w_out', 'grad_swa_w_q', 'grad_swa_sinks', 'grad_swa_w_out', 'grad_shared_w_kv', 'grad_rel_bias', 'grad_ffn_w_in', 'grad_ffn_conv_w', 'grad_ffn_conv_b', 'grad_ffn_w_out', 'grad_ln_mix_g', 'grad_ln_mix_b', 'grad_ln_ffn_g', 'grad_ln_ffn_b', 'delta_hgrn_w_in', 'delta_hgrn_lb_logits', 'delta_hgrn_gnorm_w', 'delta_hgrn_w_out', 'delta_swa_w_q', 'delta_swa_sinks', 'delta_swa_w_out', 'delta_shared_w_kv', 'delta_rel_bias', 'delta_ffn_w_in', 'delta_ffn_conv_w', 'delta_ffn_conv_b', 'delta_ffn_w_out', 'delta_ln_mix_g', 'delta_ln_mix_b', 'delta_ln_ffn_g', 'delta_ln_ffn_b', 'new_m_hgrn_w_in', 'new_m_hgrn_lb_logits', 'new_m_hgrn_gnorm_w', 'new_m_hgrn_w_out', 'new_m_swa_w_q', 'new_m_swa_sinks', 'new_m_swa_w_out', 'new_m_shared_w_kv', 'new_m_rel_bias', 'new_m_ffn_w_in', 'new_m_ffn_conv_w', 'new_m_ffn_conv_b', 'new_m_ffn_w_out', 'new_m_ln_mix_g', 'new_m_ln_mix_b', 'new_m_ln_ffn_g', 'new_m_ln_ffn_b', 'new_v_hgrn_w_in', 'new_v_hgrn_lb_logits', 'new_v_hgrn_gnorm_w', 'new_v_hgrn_w_out', 'new_v_swa_w_q', 'new_v_swa_sinks', 'new_v_swa_w_out', 'new_v_shared_w_kv', 'new_v_rel_bias', 'new_v_ffn_w_in', 'new_v_ffn_conv_w', 'new_v_ffn_conv_b', 'new_v_ffn_w_out', 'new_v_ln_mix_g', 'new_v_ln_mix_b', 'new_v_ln_ffn_g', 'new_v_ln_ffn_b']
TWIN_LEAF_KINDS = {'loss': 'loss', 'grad_x': 'grad_x', 'grad_hgrn_w_in': 'grad_w', 'grad_hgrn_lb_logits': 'grad_w', 'grad_hgrn_gnorm_w': 'grad_w', 'grad_hgrn_w_out': 'grad_w', 'grad_swa_w_q': 'grad_w', 'grad_swa_sinks': 'grad_w', 'grad_swa_w_out': 'grad_w', 'grad_shared_w_kv': 'grad_w', 'grad_rel_bias': 'grad_w', 'grad_ffn_w_in': 'grad_w', 'grad_ffn_conv_w': 'grad_w', 'grad_ffn_conv_b': 'grad_w', 'grad_ffn_w_out': 'grad_w', 'grad_ln_mix_g': 'grad_w', 'grad_ln_mix_b': 'grad_w', 'grad_ln_ffn_g': 'grad_w', 'grad_ln_ffn_b': 'grad_w', 'delta_hgrn_w_in': 'delta_w', 'delta_hgrn_lb_logits': 'delta_w', 'delta_hgrn_gnorm_w': 'delta_w', 'delta_hgrn_w_out': 'delta_w', 'delta_swa_w_q': 'delta_w', 'delta_swa_sinks': 'delta_w', 'delta_swa_w_out': 'delta_w', 'delta_shared_w_kv': 'delta_w', 'delta_rel_bias': 'delta_w', 'delta_ffn_w_in': 'delta_w', 'delta_ffn_conv_w': 'delta_w', 'delta_ffn_conv_b': 'delta_w', 'delta_ffn_w_out': 'delta_w', 'delta_ln_mix_g': 'delta_w', 'delta_ln_mix_b': 'delta_w', 'delta_ln_ffn_g': 'delta_w', 'delta_ln_ffn_b': 'delta_w', 'new_m_hgrn_w_in': 'new_m', 'new_m_hgrn_lb_logits': 'new_m', 'new_m_hgrn_gnorm_w': 'new_m', 'new_m_hgrn_w_out': 'new_m', 'new_m_swa_w_q': 'new_m', 'new_m_swa_sinks': 'new_m', 'new_m_swa_w_out': 'new_m', 'new_m_shared_w_kv': 'new_m', 'new_m_rel_bias': 'new_m', 'new_m_ffn_w_in': 'new_m', 'new_m_ffn_conv_w': 'new_m', 'new_m_ffn_conv_b': 'new_m', 'new_m_ffn_w_out': 'new_m', 'new_m_ln_mix_g': 'new_m', 'new_m_ln_mix_b': 'new_m', 'new_m_ln_ffn_g': 'new_m', 'new_m_ln_ffn_b': 'new_m', 'new_v_hgrn_w_in': 'new_v', 'new_v_hgrn_lb_logits': 'new_v', 'new_v_hgrn_gnorm_w': 'new_v', 'new_v_hgrn_w_out': 'new_v', 'new_v_swa_w_q': 'new_v', 'new_v_swa_sinks': 'new_v', 'new_v_swa_w_out': 'new_v', 'new_v_shared_w_kv': 'new_v', 'new_v_rel_bias': 'new_v', 'new_v_ffn_w_in': 'new_v', 'new_v_ffn_conv_w': 'new_v', 'new_v_ffn_conv_b': 'new_v', 'new_v_ffn_w_out': 'new_v', 'new_v_ln_mix_g': 'new_v', 'new_v_ln_mix_b': 'new_v', 'new_v_ln_ffn_g': 'new_v', 'new_v_ln_ffn_b': 'new_v'}


def _forward(args):
    return _fwd_reference(*[args[k] for k in FWD_PARAMS])


def _output_shape():
    def fwd():
        inp = _fwd_setup_inputs(0)
        return _fwd_reference(*[inp[k] for k in FWD_PARAMS])
    out = _jax.eval_shape(fwd)
    return out.shape, out.dtype

N_MICROBATCH = 1
ADAM_LR = 0.001
ADAM_B1 = 0.9
ADAM_B2 = 0.999
ADAM_EPS = 1e-08
ADAM_WD = 0.01
ADAM_STEP = 10
PER_EXAMPLE_BATCH_AXIS = {'x': 0, 'loss_target': 0}
SHARED_INPUTS = []
_WEIGHT_DTYPES = {'hgrn_w_in': _jnp.float32, 'hgrn_lb_logits': _jnp.float32, 'hgrn_gnorm_w': _jnp.float32, 'hgrn_w_out': _jnp.float32, 'swa_w_q': _jnp.float32, 'swa_sinks': _jnp.float32, 'swa_w_out': _jnp.float32, 'shared_w_kv': _jnp.float32, 'rel_bias': _jnp.float32, 'ffn_w_in': _jnp.float32, 'ffn_conv_w': _jnp.float32, 'ffn_conv_b': _jnp.float32, 'ffn_w_out': _jnp.float32, 'ln_mix_g': _jnp.float32, 'ln_mix_b': _jnp.float32, 'ln_ffn_g': _jnp.float32, 'ln_ffn_b': _jnp.float32}
MOMENT_SCALE = {'hgrn_w_in': 5.862923e-02, 'hgrn_lb_logits': 4.460218e-03, 'hgrn_gnorm_w': 1.569252e-01, 'hgrn_w_out': 1.036381e-01, 'swa_w_q': 8.331079e-03, 'swa_sinks': 6.768253e-03, 'swa_w_out': 1.981512e-02, 'shared_w_kv': 3.057857e-02, 'rel_bias': 1.082330e-02, 'ffn_w_in': 1.515028e-02, 'ffn_conv_w': 7.606681e-03, 'ffn_conv_b': 1.700027e-02, 'ffn_w_out': 2.495606e-02, 'ln_mix_g': 2.246286e+00, 'ln_mix_b': 1.039676e+00, 'ln_ffn_g': 4.535608e+01, 'ln_ffn_b': 1.744940e+00}


def _to_microbatches(a, axis):
    t = _jnp.moveaxis(a, axis, 0)
    t = t.reshape((N_MICROBATCH, t.shape[0] // N_MICROBATCH) + t.shape[1:])
    return _jnp.moveaxis(t, 1, axis + 1)


def setup_inputs(seed: int = 0) -> dict:
    inp = _fwd_setup_inputs(seed)
    key = _jax.random.fold_in(_jax.random.key(seed), 7919)
    shape, _ = _output_shape()
    out = dict(inp)
    out["loss_target"] = _jax.random.normal(_jax.random.fold_in(key, 0), shape, _jnp.float32)
    for i, name in enumerate(TWIN_WEIGHTS):
        w = inp[name].astype(_jnp.float32)
        if MOMENT_SCALE is None:
            s = _jnp.sqrt(_jnp.mean(_jnp.square(w)) + 1e-30)
        else:
            s = MOMENT_SCALE[name]
        km, kv = _jax.random.split(_jax.random.fold_in(key, i + 1))
        out[name] = w
        out["m_" + name] = s * _jax.random.normal(km, w.shape, _jnp.float32)
        out["v_" + name] = (s * s) * _jax.random.uniform(kv, w.shape, _jnp.float32, 0.5, 1.5)
    if N_MICROBATCH > 1:
        for name, axis in PER_EXAMPLE_BATCH_AXIS.items():
            out[name] = _to_microbatches(out[name], axis)
    return {'x': out['x'], 'hgrn_w_in': out['hgrn_w_in'], 'hgrn_lb_logits': out['hgrn_lb_logits'], 'hgrn_gnorm_w': out['hgrn_gnorm_w'], 'hgrn_w_out': out['hgrn_w_out'], 'swa_w_q': out['swa_w_q'], 'swa_sinks': out['swa_sinks'], 'swa_w_out': out['swa_w_out'], 'shared_w_kv': out['shared_w_kv'], 'rel_bias': out['rel_bias'], 'ffn_w_in': out['ffn_w_in'], 'ffn_conv_w': out['ffn_conv_w'], 'ffn_conv_b': out['ffn_conv_b'], 'ffn_w_out': out['ffn_w_out'], 'ln_mix_g': out['ln_mix_g'], 'ln_mix_b': out['ln_mix_b'], 'ln_ffn_g': out['ln_ffn_g'], 'ln_ffn_b': out['ln_ffn_b'], 'loss_target': out['loss_target'], 'm_hgrn_w_in': out['m_hgrn_w_in'], 'm_hgrn_lb_logits': out['m_hgrn_lb_logits'], 'm_hgrn_gnorm_w': out['m_hgrn_gnorm_w'], 'm_hgrn_w_out': out['m_hgrn_w_out'], 'm_swa_w_q': out['m_swa_w_q'], 'm_swa_sinks': out['m_swa_sinks'], 'm_swa_w_out': out['m_swa_w_out'], 'm_shared_w_kv': out['m_shared_w_kv'], 'm_rel_bias': out['m_rel_bias'], 'm_ffn_w_in': out['m_ffn_w_in'], 'm_ffn_conv_w': out['m_ffn_conv_w'], 'm_ffn_conv_b': out['m_ffn_conv_b'], 'm_ffn_w_out': out['m_ffn_w_out'], 'm_ln_mix_g': out['m_ln_mix_g'], 'm_ln_mix_b': out['m_ln_mix_b'], 'm_ln_ffn_g': out['m_ln_ffn_g'], 'm_ln_ffn_b': out['m_ln_ffn_b'], 'v_hgrn_w_in': out['v_hgrn_w_in'], 'v_hgrn_lb_logits': out['v_hgrn_lb_logits'], 'v_hgrn_gnorm_w': out['v_hgrn_gnorm_w'], 'v_hgrn_w_out': out['v_hgrn_w_out'], 'v_swa_w_q': out['v_swa_w_q'], 'v_swa_sinks': out['v_swa_sinks'], 'v_swa_w_out': out['v_swa_w_out'], 'v_shared_w_kv': out['v_shared_w_kv'], 'v_rel_bias': out['v_rel_bias'], 'v_ffn_w_in': out['v_ffn_w_in'], 'v_ffn_conv_w': out['v_ffn_conv_w'], 'v_ffn_conv_b': out['v_ffn_conv_b'], 'v_ffn_w_out': out['v_ffn_w_out'], 'v_ln_mix_g': out['v_ln_mix_g'], 'v_ln_mix_b': out['v_ln_mix_b'], 'v_ln_ffn_g': out['v_ln_ffn_g'], 'v_ln_ffn_b': out['v_ln_ffn_b']}


def _loss(weights, diff, rest, loss_target):
    with _jax.named_scope("forward"):
        args = {**rest, TWIN_DIFF_INPUT: diff, **{k: w.astype(_WEIGHT_DTYPES[k]) for k, w in weights.items()}}
        y = _forward(args)
    with _jax.named_scope("loss_head"):
        err = _jnp.square(y.astype(_jnp.float32) - loss_target)
        return 0.5 * _jnp.sum(_jnp.mean(err, axis=-1)) if err.ndim else 0.5 * err


def _adamw(w, g, m, v):
    m = ADAM_B1 * m + (1.0 - ADAM_B1) * g
    v = ADAM_B2 * v + (1.0 - ADAM_B2) * _jnp.square(g)
    m_hat = m / (1.0 - ADAM_B1 ** ADAM_STEP)
    v_hat = v / (1.0 - ADAM_B2 ** ADAM_STEP)
    delta = -ADAM_LR * (m_hat / (_jnp.sqrt(v_hat) + ADAM_EPS) + ADAM_WD * w)
    return delta, m, v


def reference(x, hgrn_w_in, hgrn_lb_logits, hgrn_gnorm_w, hgrn_w_out, swa_w_q, swa_sinks, swa_w_out, shared_w_kv, rel_bias, ffn_w_in, ffn_conv_w, ffn_conv_b, ffn_w_out, ln_mix_g, ln_mix_b, ln_ffn_g, ln_ffn_b, loss_target, m_hgrn_w_in, m_hgrn_lb_logits, m_hgrn_gnorm_w, m_hgrn_w_out, m_swa_w_q, m_swa_sinks, m_swa_w_out, m_shared_w_kv, m_rel_bias, m_ffn_w_in, m_ffn_conv_w, m_ffn_conv_b, m_ffn_w_out, m_ln_mix_g, m_ln_mix_b, m_ln_ffn_g, m_ln_ffn_b, v_hgrn_w_in, v_hgrn_lb_logits, v_hgrn_gnorm_w, v_hgrn_w_out, v_swa_w_q, v_swa_sinks, v_swa_w_out, v_shared_w_kv, v_rel_bias, v_ffn_w_in, v_ffn_conv_w, v_ffn_conv_b, v_ffn_w_out, v_ln_mix_g, v_ln_mix_b, v_ln_ffn_g, v_ln_ffn_b):
    given = dict(x=x, hgrn_w_in=hgrn_w_in, hgrn_lb_logits=hgrn_lb_logits, hgrn_gnorm_w=hgrn_gnorm_w, hgrn_w_out=hgrn_w_out, swa_w_q=swa_w_q, swa_sinks=swa_sinks, swa_w_out=swa_w_out, shared_w_kv=shared_w_kv, rel_bias=rel_bias, ffn_w_in=ffn_w_in, ffn_conv_w=ffn_conv_w, ffn_conv_b=ffn_conv_b, ffn_w_out=ffn_w_out, ln_mix_g=ln_mix_g, ln_mix_b=ln_mix_b, ln_ffn_g=ln_ffn_g, ln_ffn_b=ln_ffn_b, loss_target=loss_target, m_hgrn_w_in=m_hgrn_w_in, m_hgrn_lb_logits=m_hgrn_lb_logits, m_hgrn_gnorm_w=m_hgrn_gnorm_w, m_hgrn_w_out=m_hgrn_w_out, m_swa_w_q=m_swa_w_q, m_swa_sinks=m_swa_sinks, m_swa_w_out=m_swa_w_out, m_shared_w_kv=m_shared_w_kv, m_rel_bias=m_rel_bias, m_ffn_w_in=m_ffn_w_in, m_ffn_conv_w=m_ffn_conv_w, m_ffn_conv_b=m_ffn_conv_b, m_ffn_w_out=m_ffn_w_out, m_ln_mix_g=m_ln_mix_g, m_ln_mix_b=m_ln_mix_b, m_ln_ffn_g=m_ln_ffn_g, m_ln_ffn_b=m_ln_ffn_b, v_hgrn_w_in=v_hgrn_w_in, v_hgrn_lb_logits=v_hgrn_lb_logits, v_hgrn_gnorm_w=v_hgrn_gnorm_w, v_hgrn_w_out=v_hgrn_w_out, v_swa_w_q=v_swa_w_q, v_swa_sinks=v_swa_sinks, v_swa_w_out=v_swa_w_out, v_shared_w_kv=v_shared_w_kv, v_rel_bias=v_rel_bias, v_ffn_w_in=v_ffn_w_in, v_ffn_conv_w=v_ffn_conv_w, v_ffn_conv_b=v_ffn_conv_b, v_ffn_w_out=v_ffn_w_out, v_ln_mix_g=v_ln_mix_g, v_ln_mix_b=v_ln_mix_b, v_ln_ffn_g=v_ln_ffn_g, v_ln_ffn_b=v_ln_ffn_b)
    weights = {n: given[n] for n in TWIN_WEIGHTS}
    shared = {n: given[n] for n in SHARED_INPUTS}
    per_example = {n: given[n] for n in ['x']}
    grad_fn = _jax.value_and_grad(_loss, argnums=(0, 1))

    def one_microbatch(ex, loss_target):
        ex = dict(ex)
        diff = ex.pop(TWIN_DIFF_INPUT)
        return grad_fn(weights, diff, {**shared, **ex}, loss_target)

    if N_MICROBATCH == 1:
        loss, (grad_w, grad_x) = one_microbatch(per_example, given["loss_target"])
    else:
        def body(carry, xs):
            loss_sum, grad_sum = carry
            l_k, (gw_k, gx_k) = one_microbatch(xs[0], xs[1])
            with _jax.named_scope("update"):
                return (loss_sum + l_k, _jax.tree.map(_jnp.add, grad_sum, gw_k)), gx_k

        init = (_jnp.zeros((), _jnp.float32), _jax.tree.map(_jnp.zeros_like, weights))
        (loss, grad_w), grad_x = _jax.lax.scan(body, init, (per_example, given["loss_target"]))
    with _jax.named_scope("update"):
        delta_w, new_m, new_v = {}, {}, {}
        for n in TWIN_WEIGHTS:
            delta_w[n], new_m[n], new_v[n] = _adamw(weights[n], grad_w[n], given["m_" + n], given["v_" + n])
    return (loss, grad_x, *[grad_w[n] for n in TWIN_WEIGHTS], *[delta_w[n] for n in TWIN_WEIGHTS],
            *[new_m[n] for n in TWIN_WEIGHTS], *[new_v[n] for n in TWIN_WEIGHTS])
```

```python
import functools
import math

import numpy as np
import jax
import jax.numpy as jnp
from jax import lax
from jax.experimental import pallas as pl
from jax.experimental.pallas import tpu as pltpu

F32 = jnp.float32
BF16 = jnp.bfloat16
MESH = pl.DeviceIdType.MESH

D_MODEL = 1024
DEPTH = 2
HG_HEADS = 8
HG_DIM = 128
SW_Q_HEADS = 16
SW_KV_HEADS = 4
SW_HEAD_DIM = 64
SW_GROUP = 4
SW_WINDOW = 128
REL_BUCKETS = 32
REL_MAX_DIST = 128
FFN_DIM = 2816
ALPHA = (2.0 * DEPTH) ** 0.25
LN_EPS = 1e-5
RMS_EPS = 1e-6
ADAM_LR = 0.001
ADAM_B1 = 0.9
ADAM_B2 = 0.999
ADAM_EPS = 1e-08
ADAM_WD = 0.01
ADAM_STEP = 10

VMEM_BYTES_V7X = 64 * 1024 * 1024
VMEM_LIMIT = VMEM_BYTES_V7X - 8 * 1024 * 1024
LANES = 128
SUBLANES = 8

HG_C = 64
HG_RB = 256
ROW_TILE = 256
CONV_R = 256
N_CHIPS = 4
N_DEV = 8


def _params(sem=None):
    return pltpu.CompilerParams(dimension_semantics=sem, vmem_limit_bytes=VMEM_LIMIT)


def _tile(n, pref, unit=LANES):
    if n <= pref:
        return n
    best = None
    for t in range(unit, pref + 1, unit):
        if n % t == 0:
            best = t
    assert best is not None, (n, pref, unit)
    return best


def _dot(a, b, ca, cb):
    return lax.dot_general(a.astype(BF16), b.astype(BF16), (((ca,), (cb,)), ((), ())),
                           preferred_element_type=F32)


@jax.custom_vjp
def mm(a, b):
    return _dot(a, b, 1, 0)


@jax.custom_vjp
def mm_nt(a, b):
    return _dot(a, b, 1, 1)


@jax.custom_vjp
def mm_tn(a, b):
    return _dot(a, b, 0, 0)


mm.defvjp(lambda a, b: (mm(a, b), (a, b)), lambda r, ct: (mm_nt(ct, r[1]), mm_tn(r[0], ct)))
mm_nt.defvjp(lambda a, b: (mm_nt(a, b), (a, b)), lambda r, ct: (mm(ct, r[1]), mm_tn(ct, r[0])))
mm_tn.defvjp(lambda a, b: (mm_tn(a, b), (a, b)), lambda r, ct: (mm_nt(r[1], ct), mm(r[0], ct)))


def _split2(x):
    hi = x.astype(BF16)
    return hi, (x - hi.astype(F32)).astype(BF16)


@jax.custom_vjp
def _scores(qt, kt):
    return _dot(qt, kt, 1, 1)


def _scores_bwd(r, ct):
    (qh, ql), (kh, kl) = _split2(r[0]), _split2(r[1])
    return _dot(ct, kh, 1, 0) + _dot(ct, kl, 1, 0), _dot(ct, qh, 0, 0) + _dot(ct, ql, 0, 0)


_scores.defvjp(lambda a, b: (_scores(a, b), (a, b)), _scores_bwd)


def _split3(x):
    hi = x.astype(BF16)
    r1 = x - hi.astype(F32)
    mid = r1.astype(BF16)
    lo = (r1 - mid.astype(F32)).astype(BF16)
    return hi, mid, lo


def _cumsum_impl(x):
    n = x.shape[0]
    row = lax.broadcasted_iota(jnp.int32, x.shape, 0)
    d = 1
    while d < n:
        x = x + jnp.where(row >= d, pltpu.roll(x, d, 0), 0.0)
        d *= 2
    return x


def _cumsum_rev_impl(x):
    n = x.shape[0]
    row = lax.broadcasted_iota(jnp.int32, x.shape, 0)
    d = 1
    while d < n:
        x = x + jnp.where(row < n - d, pltpu.roll(x, n - d, 0), 0.0)
        d *= 2
    return x


@jax.custom_vjp
def _cumsum(x):
    return _cumsum_impl(x)


_cumsum.defvjp(lambda x: (_cumsum_impl(x), None), lambda _, ct: (_cumsum_rev_impl(ct),))


def _matmul(a, b, *, mode, name, out_dtype=F32, add=None, add_scale=1.0, tm=512, tn=1408, tk=1408):
    if mode == "nn":
        (M, K), (K2, N) = a.shape, b.shape
    elif mode == "nt":
        (M, K), (N, K2) = a.shape, b.shape
    else:
        (K, M), (K2, N) = a.shape, b.shape
    assert K == K2, (a.shape, b.shape, mode)
    tm, tn, tk = _tile(M, tm), _tile(N, tn), _tile(K, tk)
    nk = K // tk
    ca, cb = {"nn": (1, 0), "nt": (1, 1), "tn": (0, 0)}[mode]
    a_spec = {"nn": pl.BlockSpec((tm, tk), lambda i, j, k: (i, k)),
              "nt": pl.BlockSpec((tm, tk), lambda i, j, k: (i, k)),
              "tn": pl.BlockSpec((tk, tm), lambda i, j, k: (k, i))}[mode]
    b_spec = {"nn": pl.BlockSpec((tk, tn), lambda i, j, k: (k, j)),
              "nt": pl.BlockSpec((tn, tk), lambda i, j, k: (j, k)),
              "tn": pl.BlockSpec((tk, tn), lambda i, j, k: (k, j))}[mode]
    o_spec = pl.BlockSpec((tm, tn), lambda i, j, k: (i, j))
    has_add = add is not None

    def body(*refs):
        if has_add:
            a_ref, b_ref, add_ref, o_ref, acc_ref = refs
        else:
            a_ref, b_ref, o_ref, acc_ref = refs
        k = pl.program_id(2)

        @pl.when(k == 0)
        def _():
            acc_ref[...] = jnp.zeros_like(acc_ref)

        acc_ref[...] += _dot(a_ref[...], b_ref[...], ca, cb)

        @pl.when(k == nk - 1)
        def _():
            r = acc_ref[...]
            if has_add:
                r = r + add_scale * add_ref[...]
            o_ref[...] = r.astype(out_dtype)

    in_specs = [a_spec, b_spec] + ([o_spec] if has_add else [])
    args = (a, b) + ((add,) if has_add else ())
    return pl.pallas_call(
        body, name=name, grid=(M // tm, N // tn, nk), in_specs=in_specs, out_specs=o_spec,
        out_shape=jax.ShapeDtypeStruct((M, N), out_dtype),
        scratch_shapes=[pltpu.VMEM((tm, tn), F32)],
        compiler_params=_params(("parallel", "parallel", "arbitrary")),
    )(*args)


def _ln(z, g, b):
    mu = jnp.mean(z, axis=-1, keepdims=True)
    zc = z - mu
    var = jnp.mean(zc * zc, axis=-1, keepdims=True)
    return zc * lax.rsqrt(var + LN_EPS) * g + b


def _ln_fwd(h, s, g, b, *, name):
    T, Dm = h.shape
    tr = _tile(T, ROW_TILE, SUBLANES)

    def body(h_ref, s_ref, g_ref, b_ref, y_ref, yb_ref):
        y = _ln(ALPHA * h_ref[...] + s_ref[...], g_ref[...], b_ref[...])
        y_ref[...] = y
        yb_ref[...] = y.astype(BF16)

    row = pl.BlockSpec((tr, Dm), lambda i: (i, 0))
    vec = pl.BlockSpec((1, Dm), lambda i: (0, 0))
    return pl.pallas_call(
        body, name=name, grid=(T // tr,), in_specs=[row, row, vec, vec], out_specs=[row, row],
        out_shape=[jax.ShapeDtypeStruct((T, Dm), F32), jax.ShapeDtypeStruct((T, Dm), BF16)],
        compiler_params=_params(("parallel",)),
    )(h, s, g, b)


def _ln_bwd(dy, h, s, g, b, *, name):
    T, Dm = h.shape
    tr = _tile(T, ROW_TILE, SUBLANES)

    def body(dy_ref, h_ref, s_ref, g_ref, b_ref, dz_ref, dzb_ref, dg_ref, db_ref):
        @pl.when(pl.program_id(0) == 0)
        def _():
            dg_ref[...] = jnp.zeros_like(dg_ref)
            db_ref[...] = jnp.zeros_like(db_ref)

        z = ALPHA * h_ref[...] + s_ref[...]
        _, vjp = jax.vjp(_ln, z, g_ref[...], b_ref[...])
        dz, dg, db = vjp(dy_ref[...])
        dz_ref[...] = dz
        dzb_ref[...] = dz.astype(BF16)
        dg_ref[...] += dg
        db_ref[...] += db

    row = pl.BlockSpec((tr, Dm), lambda i: (i, 0))
    vec = pl.BlockSpec((1, Dm), lambda i: (0, 0))
    return pl.pallas_call(
        body, name=name, grid=(T // tr,), in_specs=[row, row, row, vec, vec],
        out_specs=[row, row, vec, vec],
        out_shape=[jax.ShapeDtypeStruct((T, Dm), F32), jax.ShapeDtypeStruct((T, Dm), BF16),
                   jax.ShapeDtypeStruct((1, Dm), F32), jax.ShapeDtypeStruct((1, Dm), F32)],
        compiler_params=_params(("arbitrary",)),
    )(dy, h, s, g, b)


def _loss_grad(y, tgt, *, name):
    T, Dm = y.shape
    tr = _tile(T, ROW_TILE, SUBLANES)

    def body(y_ref, t_ref, dy_ref, l_ref):
        @pl.when(pl.program_id(0) == 0)
        def _():
            l_ref[...] = jnp.zeros_like(l_ref)

        e = y_ref[...] - t_ref[...]
        dy_ref[...] = e * (1.0 / Dm)
        l_ref[...] += 0.5 * jnp.sum(jnp.mean(e * e, axis=-1, keepdims=True), axis=0, keepdims=True)

    row = pl.BlockSpec((tr, Dm), lambda i: (i, 0))
    lsp = pl.BlockSpec((SUBLANES, LANES), lambda i: (0, 0))
    return pl.pallas_call(
        body, name=name, grid=(T // tr,), in_specs=[row, row], out_specs=[row, lsp],
        out_shape=[jax.ShapeDtypeStruct((T, Dm), F32), jax.ShapeDtypeStruct((SUBLANES, LANES), F32)],
        compiler_params=_params(("arbitrary",)),
    )(y, tgt)


def _hg_chunk(qr, fr, ir, gr, l0, l1, gw, st):
    C = qr.shape[0]
    row = lax.broadcasted_iota(jnp.int32, (C, HG_DIM), 0)
    lb = jax.nn.sigmoid(l0 - l1)
    fg = lb + (1.0 - lb) * jax.nn.sigmoid(fr)
    b = _cumsum(jnp.log(fg))
    q = jax.nn.silu(qr)
    k = 1.0 - fg
    bmid = lax.stop_gradient(jnp.sum(jnp.where(row == C // 2 - 1, b, 0.0), axis=0, keepdims=True))
    bl = jnp.sum(jnp.where(row == C - 1, b, 0.0), axis=0, keepdims=True)
    o = mm_nt(q * jnp.exp(b), st)
    sc = _scores(q * jnp.exp(b - bmid), k * jnp.exp(bmid - b))
    ti = lax.broadcasted_iota(jnp.int32, (C, C), 0)
    si = lax.broadcasted_iota(jnp.int32, (C, C), 1)
    sc = jnp.where(si <= ti, sc, 0.0)
    o = o + mm(sc, ir)
    st_new = st * jnp.exp(bl) + mm_tn(ir, k * jnp.exp(bl - b))
    on = o * lax.rsqrt(jnp.mean(o * o, axis=-1, keepdims=True) + RMS_EPS)
    return on * gw * jax.nn.silu(gr), st_new


def _hgrn_fwd(q, f, i, g, lbl, gw, *, name):
    T, Dm = q.shape
    rb = min(HG_RB, T)
    C = min(HG_C, rb)
    ncb = rb // C

    def body(q_ref, f_ref, i_ref, g_ref, lbl_ref, gw_ref, o_ref, st_ref, s_ref):
        @pl.when(pl.program_id(0) == 0)
        def _():
            s_ref[...] = jnp.zeros_like(s_ref)

        def chunk(ci, carry):
            r0 = pl.multiple_of(ci * C, C)
            rows = pl.ds(r0, C)
            for h in range(HG_HEADS):
                cols = slice(h * HG_DIM, (h + 1) * HG_DIM)
                st = s_ref[h]
                st_ref[ci, h] = st
                out, st_new = _hg_chunk(q_ref[rows, cols], f_ref[rows, cols], i_ref[rows, cols], g_ref[rows, cols],
                                        lbl_ref[0:1, cols], lbl_ref[1:2, cols], gw_ref[...], st)
                o_ref[rows, cols] = out.astype(BF16)
                s_ref[h] = st_new
            return carry

        lax.fori_loop(0, ncb, chunk, 0)

    row = pl.BlockSpec((rb, Dm), lambda n: (n, 0))
    return pl.pallas_call(
        body, name=name, grid=(T // rb,),
        in_specs=[row, row, row, row, pl.BlockSpec((2, Dm), lambda n: (0, 0)), pl.BlockSpec((1, HG_DIM), lambda n: (0, 0))],
        out_specs=[row, pl.BlockSpec((ncb, HG_HEADS, HG_DIM, HG_DIM), lambda n: (n, 0, 0, 0))],
        out_shape=[jax.ShapeDtypeStruct((T, Dm), BF16),
                   jax.ShapeDtypeStruct((T // C, HG_HEADS, HG_DIM, HG_DIM), F32)],
        scratch_shapes=[pltpu.VMEM((HG_HEADS, HG_DIM, HG_DIM), F32)],
        compiler_params=_params(("arbitrary",)),
    )(q, f, i, g, lbl, gw)


def _hgrn_bwd(q, f, i, g, lbl, gw, states, dout, *, name):
    T, Dm = q.shape
    rb = min(HG_RB, T)
    C = min(HG_C, rb)
    ncb = rb // C
    nb = T // rb

    def body(q_ref, f_ref, i_ref, g_ref, lbl_ref, gw_ref, st_ref, do_ref,
             dq_ref, df_ref, di_ref, dg_ref, dlbl_ref, dgw_ref, ds_ref):
        @pl.when(pl.program_id(0) == 0)
        def _():
            ds_ref[...] = jnp.zeros_like(ds_ref)
            dlbl_ref[...] = jnp.zeros_like(dlbl_ref)
            dgw_ref[...] = jnp.zeros_like(dgw_ref)

        def chunk(cj, carry):
            ci = ncb - 1 - cj
            r0 = pl.multiple_of(ci * C, C)
            rows = pl.ds(r0, C)
            for h in range(HG_HEADS):
                cols = slice(h * HG_DIM, (h + 1) * HG_DIM)
                _, vjp = jax.vjp(_hg_chunk, q_ref[rows, cols], f_ref[rows, cols], i_ref[rows, cols], g_ref[rows, cols],
                                 lbl_ref[0:1, cols], lbl_ref[1:2, cols], gw_ref[...], st_ref[ci, h])
                dq, df, di, dg, dl0, dl1, dgw, dst = vjp((do_ref[rows, cols].astype(F32), ds_ref[h]))
                dq_ref[rows, cols] = dq.astype(BF16)
                df_ref[rows, cols] = df.astype(BF16)
                di_ref[rows, cols] = di.astype(BF16)
                dg_ref[rows, cols] = dg.astype(BF16)
                dlbl_ref[0:1, cols] += dl0
                dlbl_ref[1:2, cols] += dl1
                dgw_ref[...] += dgw
                ds_ref[h] = dst
            return carry

        lax.fori_loop(0, ncb, chunk, 0)

    row = pl.BlockSpec((rb, Dm), lambda n: (nb - 1 - n, 0))
    lsp = pl.BlockSpec((2, Dm), lambda n: (0, 0))
    gsp = pl.BlockSpec((1, HG_DIM), lambda n: (0, 0))
    return pl.pallas_call(
        body, name=name, grid=(nb,),
        in_specs=[row, row, row, row, lsp, gsp,
                  pl.BlockSpec((ncb, HG_HEADS, HG_DIM, HG_DIM), lambda n: (nb - 1 - n, 0, 0, 0)), row],
        out_specs=[row, row, row, row, lsp, gsp],
        out_shape=[jax.ShapeDtypeStruct((T, Dm), BF16)] * 4
        + [jax.ShapeDtypeStruct((2, Dm), F32), jax.ShapeDtypeStruct((1, HG_DIM), F32)],
        scratch_shapes=[pltpu.VMEM((HG_HEADS, HG_DIM, HG_DIM), F32)],
        compiler_params=_params(("arbitrary",)),
    )(q, f, i, g, lbl, gw, states, dout)


def _shift_down(cur, prev8):
    big = jnp.concatenate([prev8, cur], axis=0)
    return pltpu.roll(big, 1, 0)[SUBLANES:], pltpu.roll(big, 2, 0)[SUBLANES:]


def _shift_up(cur, next8):
    n = cur.shape[0] + SUBLANES
    big = jnp.concatenate([cur, next8], axis=0)
    return pltpu.roll(big, n - 1, 0)[:cur.shape[0]], pltpu.roll(big, n - 2, 0)[:cur.shape[0]]


def _conv_rows(u_ref, w, bias, r0, R):
    cur = u_ref[pl.ds(r0, R), :]
    p0 = pl.multiple_of(jnp.maximum(r0 - SUBLANES, 0), SUBLANES)
    prev8 = jnp.where(r0 > 0, u_ref[pl.ds(p0, SUBLANES), :], 0.0)
    s1, s2 = _shift_down(cur, prev8)
    return w[0:1, :] * s2 + w[1:2, :] * s1 + w[2:3, :] * cur + bias, cur, s1, s2


def _conv_gate_fwd(ua, ub, wa, wb, ba, bb, *, name):
    T, Fd = ua.shape
    R = min(CONV_R, T)
    tc = LANES

    def body(ua_ref, ub_ref, wa_ref, wb_ref, ba_ref, bb_ref, o_ref):
        wa_, wb_, ba_, bb_ = wa_ref[...], wb_ref[...], ba_ref[...], bb_ref[...]

        def step(ri, carry):
            r0 = pl.multiple_of(ri * R, R)
            ca = _conv_rows(ua_ref, wa_, ba_, r0, R)[0]
            cb = _conv_rows(ub_ref, wb_, bb_, r0, R)[0]
            o_ref[pl.ds(r0, R), :] = (jax.nn.silu(ca) * cb).astype(BF16)
            return carry

        lax.fori_loop(0, T // R, step, 0)

    col = pl.BlockSpec((T, tc), lambda j: (0, j))
    wsp = pl.BlockSpec((3, tc), lambda j: (0, j))
    bsp = pl.BlockSpec((1, tc), lambda j: (0, j))
    return pl.pallas_call(
        body, name=name, grid=(Fd // tc,), in_specs=[col, col, wsp, wsp, bsp, bsp], out_specs=col,
        out_shape=jax.ShapeDtypeStruct((T, Fd), BF16),
        compiler_params=_params(("parallel",)),
    )(ua, ub, wa, wb, ba, bb)


def _conv_gate_bwd(ua, ub, wa, wb, ba, bb, dact, *, name):
    T, Fd = ua.shape
    R = min(CONV_R, T)
    nr = T // R
    tc = LANES

    def body(ua_ref, ub_ref, wa_ref, wb_ref, ba_ref, bb_ref, da_ref,
             dua_ref, dub_ref, dwa_ref, dwb_ref, dba_ref, dbb_ref, dca_ref, dcb_ref):
        wa_, wb_, ba_, bb_ = wa_ref[...], wb_ref[...], ba_ref[...], bb_ref[...]

        def taps(dc, cur, s1, s2):
            return jnp.concatenate([jnp.sum(dc * s2, axis=0, keepdims=True), jnp.sum(dc * s1, axis=0, keepdims=True),
                                    jnp.sum(dc * cur, axis=0, keepdims=True)], axis=0)

        def first(ri, carry):
            dwa, dwb, dba, dbb = carry
            r0 = pl.multiple_of(ri * R, R)
            ca, cura, s1a, s2a = _conv_rows(ua_ref, wa_, ba_, r0, R)
            cb, curb, s1b, s2b = _conv_rows(ub_ref, wb_, bb_, r0, R)
            dact_ = da_ref[pl.ds(r0, R), :].astype(F32)
            sg = jax.nn.sigmoid(ca)
            dca = dact_ * cb * (sg * (1.0 + ca * (1.0 - sg)))
            dcb = dact_ * (ca * sg)
            dca_ref[pl.ds(r0, R), :] = dca
            dcb_ref[pl.ds(r0, R), :] = dcb
            return (dwa + taps(dca, cura, s1a, s2a), dwb + taps(dcb, curb, s1b, s2b),
                    dba + jnp.sum(dca, axis=0, keepdims=True), dbb + jnp.sum(dcb, axis=0, keepdims=True))

        z3 = jnp.zeros((3, tc), F32)
        z1 = jnp.zeros((1, tc), F32)
        dwa, dwb, dba, dbb = lax.fori_loop(0, nr, first, (z3, z3, z1, z1))
        dwa_ref[...] = dwa
        dwb_ref[...] = dwb
        dba_ref[...] = dba
        dbb_ref[...] = dbb

        def du_rows(dc_ref, w, r0):
            cur = dc_ref[pl.ds(r0, R), :]
            n0 = pl.multiple_of(jnp.minimum(r0 + R, T - SUBLANES), SUBLANES)
            next8 = jnp.where(r0 + R < T, dc_ref[pl.ds(n0, SUBLANES), :], 0.0)
            m1, m2 = _shift_up(cur, next8)
            return w[2:3, :] * cur + w[1:2, :] * m1 + w[0:1, :] * m2

        def second(ri, carry):
            r0 = pl.multiple_of(ri * R, R)
            dua_ref[pl.ds(r0, R), :] = du_rows(dca_ref, wa_, r0).astype(BF16)
            dub_ref[pl.ds(r0, R), :] = du_rows(dcb_ref, wb_, r0).astype(BF16)
            return carry

        lax.fori_loop(0, nr, second, 0)

    col = pl.BlockSpec((T, tc), lambda j: (0, j))
    wsp = pl.BlockSpec((3, tc), lambda j: (0, j))
    bsp = pl.BlockSpec((1, tc), lambda j: (0, j))
    return pl.pallas_call(
        body, name=name, grid=(Fd // tc,), in_specs=[col, col, wsp, wsp, bsp, bsp, col],
        out_specs=[col, col, wsp, wsp, bsp, bsp],
        out_shape=[jax.ShapeDtypeStruct((T, Fd), BF16)] * 2 + [jax.ShapeDtypeStruct((3, Fd), F32)] * 2
        + [jax.ShapeDtypeStruct((1, Fd), F32)] * 2,
        scratch_shapes=[pltpu.VMEM((T, tc), F32), pltpu.VMEM((T, tc), F32)],
        compiler_params=_params(("parallel",)),
    )(ua, ub, wa, wb, ba, bb, dact)


def _bucket_index():
    t = np.arange(SW_WINDOW)[:, None] + SW_WINDOW
    s = np.arange(2 * SW_WINDOW)[None, :]
    dist = np.maximum(t - s, 0)
    exact = REL_BUCKETS // 2
    d = np.maximum(dist, 1).astype(np.float32)
    log_b = exact + (np.log(d / np.float32(exact)) / np.float32(math.log(REL_MAX_DIST / exact))
                     * np.float32(REL_BUCKETS - exact)).astype(np.int32)
    bucket = np.where(dist < exact, dist, np.minimum(log_b, REL_BUCKETS - 1))
    return bucket.astype(np.int32).reshape(1, -1)


BIAS_COLS = SW_WINDOW * 2 * SW_WINDOW
BIAS_TILE = 4096


def _bias_from_table(table, bucket, *, name):
    def body(t_ref, idx_ref, o_ref):
        onehot = (lax.broadcasted_iota(jnp.int32, (REL_BUCKETS, BIAS_TILE), 0) == idx_ref[...]).astype(BF16)
        acc = jnp.zeros((SW_Q_HEADS, BIAS_TILE), F32)
        for piece in _split3(t_ref[...]):
            acc = acc + lax.dot_general(piece, onehot, (((0,), (0,)), ((), ())), preferred_element_type=F32)
        o_ref[...] = acc

    return pl.pallas_call(
        body, name=name, grid=(BIAS_COLS // BIAS_TILE,),
        in_specs=[pl.BlockSpec((REL_BUCKETS, SW_Q_HEADS), lambda j: (0, 0)), pl.BlockSpec((1, BIAS_TILE), lambda j: (0, j))],
        out_specs=pl.BlockSpec((SW_Q_HEADS, BIAS_TILE), lambda j: (0, j)),
        out_shape=jax.ShapeDtypeStruct((SW_Q_HEADS, BIAS_COLS), F32),
        compiler_params=_params(("parallel",)),
    )(table, bucket)


def _table_grad(dbias, bucket, *, name):
    def body(d_ref, idx_ref, o_ref):
        @pl.when(pl.program_id(0) == 0)
        def _():
            o_ref[...] = jnp.zeros_like(o_ref)

        onehot = (lax.broadcasted_iota(jnp.int32, (REL_BUCKETS, BIAS_TILE), 0) == idx_ref[...]).astype(BF16)
        acc = jnp.zeros((REL_BUCKETS, SW_Q_HEADS), F32)
        for piece in _split3(d_ref[...]):
            acc = acc + lax.dot_general(onehot, piece, (((1,), (1,)), ((), ())), preferred_element_type=F32)
        o_ref[...] += acc

    return pl.pallas_call(
        body, name=name, grid=(BIAS_COLS // BIAS_TILE,),
        in_specs=[pl.BlockSpec((SW_Q_HEADS, BIAS_TILE), lambda j: (0, j)), pl.BlockSpec((1, BIAS_TILE), lambda j: (0, j))],
        out_specs=pl.BlockSpec((REL_BUCKETS, SW_Q_HEADS), lambda j: (0, 0)),
        out_shape=jax.ShapeDtypeStruct((REL_BUCKETS, SW_Q_HEADS), F32),
        compiler_params=_params(("arbitrary",)),
    )(dbias, bucket)


def _attn_head(q, kk, vv, bias, sink, mask):
    logits = mm_nt(q, kk) * (SW_HEAD_DIM ** -0.5) + bias
    logits = jnp.where(mask, logits, -jnp.inf)
    m = lax.stop_gradient(jnp.maximum(jnp.max(logits, axis=-1, keepdims=True), sink))
    p = jnp.exp(logits - m)
    denom = jnp.sum(p, axis=-1, keepdims=True) + jnp.exp(sink - m)
    return mm(p, vv) / denom


def _band_mask(n):
    t = lax.broadcasted_iota(jnp.int32, (SW_WINDOW, 2 * SW_WINDOW), 0) + SW_WINDOW
    s = lax.broadcasted_iota(jnp.int32, (SW_WINDOW, 2 * SW_WINDOW), 1)
    dist = t - s
    return (dist >= 0) & (dist < SW_WINDOW) & ((n > 0) | (s >= SW_WINDOW))


KV_DIM = SW_KV_HEADS * SW_HEAD_DIM


def _kv_pair(kvp_ref, kvc_ref, g):
    ks = slice(g * SW_HEAD_DIM, (g + 1) * SW_HEAD_DIM)
    vs = slice(KV_DIM + g * SW_HEAD_DIM, KV_DIM + (g + 1) * SW_HEAD_DIM)
    kk = jnp.concatenate([kvp_ref[:, ks], kvc_ref[:, ks]], axis=0)
    vv = jnp.concatenate([kvp_ref[:, vs], kvc_ref[:, vs]], axis=0)
    return kk, vv, ks, vs


def _attn_fwd(q1, kv, bias, sinks, *, name):
    T, Dm = q1.shape
    W = SW_WINDOW

    def body(q_ref, kvc_ref, kvp_ref, bias_ref, sink_ref, o_ref):
        mask = _band_mask(pl.program_id(0))
        for g in range(SW_KV_HEADS):
            kk, vv, _, _ = _kv_pair(kvp_ref, kvc_ref, g)
            for r in range(SW_GROUP):
                h = g * SW_GROUP + r
                hs = slice(h * SW_HEAD_DIM, (h + 1) * SW_HEAD_DIM)
                o = _attn_head(q_ref[:, hs], kk, vv, bias_ref[h], sink_ref[:, h:h + 1], mask)
                o_ref[:, hs] = o.astype(BF16)

    return pl.pallas_call(
        body, name=name, grid=(T // W,),
        in_specs=[pl.BlockSpec((W, Dm), lambda n: (n, 0)),
                  pl.BlockSpec((W, 2 * KV_DIM), lambda n: (n, 0)),
                  pl.BlockSpec((W, 2 * KV_DIM), lambda n: (jnp.maximum(n - 1, 0), 0)),
                  pl.BlockSpec((SW_Q_HEADS, W, 2 * W), lambda n: (0, 0, 0)),
                  pl.BlockSpec((1, SW_Q_HEADS), lambda n: (0, 0))],
        out_specs=pl.BlockSpec((W, Dm), lambda n: (n, 0)),
        out_shape=jax.ShapeDtypeStruct((T, Dm), BF16),
        compiler_params=_params(("parallel",)),
    )(q1, kv, kv, bias, sinks)


def _attn_bwd(q1, kv, bias, sinks, do, *, name):
    T, Dm = q1.shape
    W = SW_WINDOW
    nb = T // W

    def body(q_ref, kvc_ref, kvp_ref, bias_ref, sink_ref, do_ref,
             dq_ref, dkv_ref, dbias_ref, dsink_ref, carry_ref):
        @pl.when(pl.program_id(0) == 0)
        def _():
            carry_ref[...] = jnp.zeros_like(carry_ref)
            dbias_ref[...] = jnp.zeros_like(dbias_ref)
            dsink_ref[...] = jnp.zeros_like(dsink_ref)

        n = nb - 1 - pl.program_id(0)
        mask = _band_mask(n)
        lane = lax.broadcasted_iota(jnp.int32, (1, SW_Q_HEADS), 1)
        dsink = jnp.zeros((1, SW_Q_HEADS), F32)
        for g in range(SW_KV_HEADS):
            kk, vv, ks, vs = _kv_pair(kvp_ref, kvc_ref, g)
            dkk = jnp.zeros_like(kk)
            dvv = jnp.zeros_like(vv)
            for r in range(SW_GROUP):
                h = g * SW_GROUP + r
                hs = slice(h * SW_HEAD_DIM, (h + 1) * SW_HEAD_DIM)
                _, vjp = jax.vjp(functools.partial(_attn_head, mask=mask),
                                 q_ref[:, hs], kk, vv, bias_ref[h], sink_ref[:, h:h + 1])
                dq, dk_, dv_, db_, ds_ = vjp(do_ref[:, hs].astype(F32))
                dq_ref[:, hs] = dq.astype(BF16)
                dbias_ref[h] += db_
                dsink = dsink + jnp.where(lane == h, ds_, 0.0)
                dkk = dkk + dk_
                dvv = dvv + dv_
            dkv_ref[:, ks] = (carry_ref[:, ks] + dkk[W:]).astype(BF16)
            dkv_ref[:, vs] = (carry_ref[:, vs] + dvv[W:]).astype(BF16)
            carry_ref[:, ks] = dkk[:W]
            carry_ref[:, vs] = dvv[:W]
        dsink_ref[...] += dsink

    rev = lambda n: (nb - 1 - n, 0)
    return pl.pallas_call(
        body, name=name, grid=(nb,),
        in_specs=[pl.BlockSpec((W, Dm), rev),
                  pl.BlockSpec((W, 2 * KV_DIM), rev),
                  pl.BlockSpec((W, 2 * KV_DIM), lambda n: (jnp.maximum(nb - 2 - n, 0), 0)),
                  pl.BlockSpec((SW_Q_HEADS, W, 2 * W), lambda n: (0, 0, 0)),
                  pl.BlockSpec((1, SW_Q_HEADS), lambda n: (0, 0)),
                  pl.BlockSpec((W, Dm), rev)],
        out_specs=[pl.BlockSpec((W, Dm), rev), pl.BlockSpec((W, 2 * KV_DIM), rev),
                   pl.BlockSpec((SW_Q_HEADS, W, 2 * W), lambda n: (0, 0, 0)),
                   pl.BlockSpec((1, SW_Q_HEADS), lambda n: (0, 0))],
        out_shape=[jax.ShapeDtypeStruct((T, Dm), BF16), jax.ShapeDtypeStruct((T, 2 * KV_DIM), BF16),
                   jax.ShapeDtypeStruct((SW_Q_HEADS, W, 2 * W), F32), jax.ShapeDtypeStruct((1, SW_Q_HEADS), F32)],
        scratch_shapes=[pltpu.VMEM((W, 2 * KV_DIM), F32)],
        compiler_params=_params(("arbitrary",)),
    )(q1, kv, kv, bias, sinks, do)


def _ffn_fwd(hb, w, l):
    ua = _matmul(hb, w["ffn_in_a"][l], mode="nn", name=f"ffn{l}_up_a")
    ub = _matmul(hb, w["ffn_in_b"][l], mode="nn", name=f"ffn{l}_up_b")
    act = _conv_gate_fwd(ua, ub, w["conv_w_a"][l], w["conv_w_b"][l], w["conv_b_a"][l], w["conv_b_b"][l],
                         name=f"ffn{l}_conv_gate")
    ff = _matmul(act, w["ffn_out"][l], mode="nn", name=f"ffn{l}_down")
    return ua, ub, act, ff


def _ffn_bwd(dffb, dh_scaled, hb, ua, ub, act, w, l):
    dact = _matmul(dffb, w["ffn_out"][l], mode="nt", out_dtype=BF16, name=f"ffn{l}_down_dx")
    g_out = _matmul(act, dffb, mode="tn", name=f"ffn{l}_down_dw", tm=1408, tn=1024, tk=512)
    dua, dub, dwa, dwb, dba, dbb = _conv_gate_bwd(ua, ub, w["conv_w_a"][l], w["conv_w_b"][l], w["conv_b_a"][l],
                                                  w["conv_b_b"][l], dact, name=f"ffn{l}_conv_gate_bwd")
    dh = _matmul(dua, w["ffn_in_a"][l], mode="nt", add=dh_scaled, add_scale=ALPHA, name=f"ffn{l}_up_a_dx", tn=1024)
    dh = _matmul(dub, w["ffn_in_b"][l], mode="nt", add=dh, name=f"ffn{l}_up_b_dx", tn=1024)
    g_in_a = _matmul(hb, dua, mode="tn", name=f"ffn{l}_up_a_dw", tm=1024, tn=1408, tk=512)
    g_in_b = _matmul(hb, dub, mode="tn", name=f"ffn{l}_up_b_dw", tm=1024, tn=1408, tk=512)
    return dh, dict(ffn_out=g_out, ffn_in_a=g_in_a, ffn_in_b=g_in_b, conv_w_a=dwa, conv_w_b=dwb, conv_b_a=dba, conv_b_b=dbb)


def _local_step(x, tgt, w):
    bucket = jnp.asarray(_bucket_index())
    xb = x.astype(BF16)

    pre = [_matmul(xb, w["hg_in"][j], mode="nn", name=f"hg_in_{j}") for j in range(4)]
    og, states = _hgrn_fwd(*pre, w["lb_logits"], w["gnorm"], name="hgrn_fwd")
    mix0 = _matmul(og, w["hg_out"], mode="nn", name="hg_out")
    h1, h1b = _ln_fwd(x, mix0, w["ln_mix_g"][0], w["ln_mix_b"][0], name="ln_mix0")
    ua0, ub0, act0, ff0 = _ffn_fwd(h1b, w, 0)
    h2, h2b = _ln_fwd(h1, ff0, w["ln_ffn_g"][0], w["ln_ffn_b"][0], name="ln_ffn0")
    kv = _matmul(h2b, w["kv"], mode="nn", name="kv_proj")

    bias = _bias_from_table(w["rel_bias"], bucket, name="rel_bias_expand").reshape(SW_Q_HEADS, SW_WINDOW, 2 * SW_WINDOW)
    q1 = _matmul(h2b, w["sw_q"], mode="nn", name="sw_q")
    o1 = _attn_fwd(q1, kv, bias, w["sinks"], name="attn_fwd")
    mix1 = _matmul(o1, w["sw_out"], mode="nn", name="sw_out")
    h3, h3b = _ln_fwd(h2, mix1, w["ln_mix_g"][1], w["ln_mix_b"][1], name="ln_mix1")
    ua1, ub1, act1, ff1 = _ffn_fwd(h3b, w, 1)
    y, _ = _ln_fwd(h3, ff1, w["ln_ffn_g"][1], w["ln_ffn_b"][1], name="ln_ffn1")

    dy, loss_tile = _loss_grad(y, tgt, name="loss_grad")

    g = {}
    dz, dzb, dg_, db_ = _ln_bwd(dy, h3, ff1, w["ln_ffn_g"][1], w["ln_ffn_b"][1], name="ln_ffn1_bwd")
    g["ln_ffn_g1"], g["ln_ffn_b1"] = dg_, db_
    dh3, gf1 = _ffn_bwd(dzb, dz, h3b, ua1, ub1, act1, w, 1)
    dz, dzb, dg_, db_ = _ln_bwd(dh3, h2, mix1, w["ln_mix_g"][1], w["ln_mix_b"][1], name="ln_mix1_bwd")
    g["ln_mix_g1"], g["ln_mix_b1"] = dg_, db_
    do1 = _matmul(dzb, w["sw_out"], mode="nt", out_dtype=BF16, name="sw_out_dx")
    g["sw_out"] = _matmul(o1, dzb, mode="tn", name="sw_out_dw", tm=1024, tn=1024, tk=512)
    dq1, dkv, dbias, dsinks = _attn_bwd(q1, kv, bias, w["sinks"], do1, name="attn_bwd")
    g["sinks"] = dsinks
    g["rel_bias"] = _table_grad(dbias.reshape(SW_Q_HEADS, BIAS_COLS), bucket, name="rel_bias_grad")
    dh2 = _matmul(dq1, w["sw_q"], mode="nt", add=dz, add_scale=ALPHA, name="sw_q_dx", tn=1024)
    dh2 = _matmul(dkv, w["kv"], mode="nt", add=dh2, name="kv_dx", tn=1024)
    g["sw_q"] = _matmul(h2b, dq1, mode="tn", name="sw_q_dw", tm=1024, tn=1024, tk=512)
    g["kv"] = _matmul(h2b, dkv, mode="tn", name="kv_dw", tm=1024, tn=512, tk=512)

    dz, dzb, dg_, db_ = _ln_bwd(dh2, h1, ff0, w["ln_ffn_g"][0], w["ln_ffn_b"][0], name="ln_ffn0_bwd")
    g["ln_ffn_g0"], g["ln_ffn_b0"] = dg_, db_
    dh1, gf0 = _ffn_bwd(dzb, dz, h1b, ua0, ub0, act0, w, 0)
    dz, dzb, dg_, db_ = _ln_bwd(dh1, x, mix0, w["ln_mix_g"][0], w["ln_mix_b"][0], name="ln_mix0_bwd")
    g["ln_mix_g0"], g["ln_mix_b0"] = dg_, db_
    dog = _matmul(dzb, w["hg_out"], mode="nt", out_dtype=BF16, name="hg_out_dx")
    g["hg_out"] = _matmul(og, dzb, mode="tn", name="hg_out_dw", tm=1024, tn=1024, tk=512)
    dpre = _hgrn_bwd(*pre, w["lb_logits"], w["gnorm"], states, dog, name="hgrn_bwd")
    g["lb_logits"], g["gnorm"] = dpre[4], dpre[5]
    dx = dz
    scale = ALPHA
    for j in range(4):
        dx = _matmul(dpre[j], w["hg_in"][j], mode="nt", add=dx, add_scale=scale, name=f"hg_in_{j}_dx", tn=1024)
        scale = 1.0
    g["hg_in"] = [_matmul(xb, dpre[j], mode="tn", name=f"hg_in_{j}_dw", tm=1024, tn=1024, tk=512) for j in range(4)]
    g["ffn"] = [gf0, gf1]
    return loss_tile, dx, g


def _adamw(wt, gr, m, v, *, name):
    R, Cc = wt.shape
    tr = _tile(R, 256, SUBLANES) if R % SUBLANES == 0 else R
    c1 = 1.0 - ADAM_B1 ** ADAM_STEP
    c2 = 1.0 - ADAM_B2 ** ADAM_STEP

    def body(w_ref, g_ref, m_ref, v_ref, d_ref, nm_ref, nv_ref):
        g_ = g_ref[...]
        nm = ADAM_B1 * m_ref[...] + (1.0 - ADAM_B1) * g_
        nv = ADAM_B2 * v_ref[...] + (1.0 - ADAM_B2) * (g_ * g_)
        d_ref[...] = -ADAM_LR * ((nm / c1) / (jnp.sqrt(nv / c2) + ADAM_EPS) + ADAM_WD * w_ref[...])
        nm_ref[...] = nm
        nv_ref[...] = nv

    blk = pl.BlockSpec((tr, Cc), lambda i: (i, 0))
    return pl.pallas_call(
        body, name=name, grid=(R // tr,), in_specs=[blk] * 4, out_specs=[blk] * 3,
        out_shape=[jax.ShapeDtypeStruct((R, Cc), F32)] * 3,
        compiler_params=_params(("parallel",)),
    )(wt, gr, m, v)


def _place():
    return lax.axis_index("x"), lax.axis_index("y"), lax.axis_index("c")


def _other_chips(x, y):
    return [(1 - x, y), (x, 1 - y), (1 - x, 1 - y)]


def _sum8(v, *, name):
    r = v.shape[0]

    def body(v_ref, all_ref, o_ref, send_sems, recv_sems, local_sem):
        x, y, c = _place()
        me, sibling = (x, y, c), (x, y, 1 - c)
        chips = _other_chips(x, y)

        def rows(px, py, pc):
            return all_ref.at[pl.ds((4 * px + 2 * py + pc) * r, r), :]

        def copy(k, block, to, src=None):
            return pltpu.make_async_remote_copy(
                src_ref=rows(*block) if src is None else src, dst_ref=rows(*block),
                send_sem=send_sems.at[k], recv_sem=recv_sems.at[k], device_id=to, device_id_type=MESH)

        mine = pltpu.make_async_copy(v_ref, rows(*me), local_sem)
        mine.start()
        first = [copy(0, me, sibling, src=v_ref)]
        first += [copy(1 + j, me, (*chip, c), src=v_ref) for j, chip in enumerate(chips)]
        for cp in first:
            cp.start()
        passed = [copy(4 + j, (*chip, c), sibling) for j, chip in enumerate(chips)]
        for j, chip in enumerate(chips):
            copy(1 + j, (*chip, c), me).wait_recv()
            passed[j].start()
        copy(0, sibling, me).wait_recv()
        for j, chip in enumerate(chips):
            copy(4 + j, (*chip, 1 - c), me).wait_recv()
        for cp in first + passed:
            cp.wait_send()
        mine.wait()
        acc = all_ref[pl.ds(0, r), :]
        for d in range(1, N_DEV):
            acc = acc + all_ref[pl.ds(d * r, r), :]
        o_ref[...] = acc

    vm = pl.BlockSpec(memory_space=pltpu.VMEM)
    return pl.pallas_call(
        body, name=name, in_specs=[vm], out_specs=[vm, vm],
        out_shape=[jax.ShapeDtypeStruct((N_DEV * r, LANES), F32), jax.ShapeDtypeStruct((r, LANES), F32)],
        scratch_shapes=[pltpu.SemaphoreType.DMA((7,)), pltpu.SemaphoreType.DMA((7,)), pltpu.SemaphoreType.DMA],
        compiler_params=pltpu.CompilerParams(vmem_limit_bytes=VMEM_LIMIT),
    )(v)[1]


def _gather_chips(shard, *, name):
    R, Cc = shard.shape
    half = R // 2
    assert half * 2 == R

    def body(s_ref, o_ref, send_sems, recv_sems, local_sem):
        x, y, c = _place()
        sibling = (x, y, 1 - c)
        chips = _other_chips(x, y)

        def part(px, py, pc):
            return o_ref.at[2 * px + py, pl.ds(pc * half, half), :]

        def copy(k, block, to, src=None):
            return pltpu.make_async_remote_copy(
                src_ref=part(*block) if src is None else src, dst_ref=part(*block),
                send_sem=send_sems.at[k], recv_sem=recv_sems.at[k], device_id=to, device_id_type=MESH)

        mine = pltpu.make_async_copy(s_ref, o_ref.at[2 * x + y], local_sem)
        mine.start()
        my_half = s_ref.at[pl.ds(c * half, half), :]
        first = [copy(j, (x, y, c), (*chip, c), src=my_half) for j, chip in enumerate(chips)]
        for cp in first:
            cp.start()
        passed = [copy(3 + j, (*chip, c), sibling) for j, chip in enumerate(chips)]
        for j, chip in enumerate(chips):
            copy(j, (*chip, c), (x, y, c)).wait_recv()
            passed[j].start()
        for j, chip in enumerate(chips):
            copy(3 + j, (*chip, 1 - c), (x, y, c)).wait_recv()
        for cp in first + passed:
            cp.wait_send()
        mine.wait()

    hbm = pl.BlockSpec(memory_space=pltpu.HBM)
    return pl.pallas_call(
        body, name=name, in_specs=[hbm], out_specs=hbm,
        out_shape=jax.ShapeDtypeStruct((N_CHIPS, R, Cc), shard.dtype),
        scratch_shapes=[pltpu.SemaphoreType.DMA((6,)), pltpu.SemaphoreType.DMA((6,)), pltpu.SemaphoreType.DMA],
    )(shard)


def _swap_sibling(v, *, name):
    def body(v_ref, o_ref, send_sem, recv_sem):
        x, y, c = _place()
        cp = pltpu.make_async_remote_copy(src_ref=v_ref, dst_ref=o_ref, send_sem=send_sem, recv_sem=recv_sem,
                                          device_id=(x, y, 1 - c), device_id_type=MESH)
        cp.start()
        cp.wait()

    hbm = pl.BlockSpec(memory_space=pltpu.HBM)
    return pl.pallas_call(
        body, name=name, in_specs=[hbm], out_specs=hbm, out_shape=jax.ShapeDtypeStruct(v.shape, v.dtype),
        scratch_shapes=[pltpu.SemaphoreType.DMA, pltpu.SemaphoreType.DMA],
    )(v)


def _scatter_chips(a, *, name):
    _, R, Cc = a.shape

    def body(a_ref, o_ref, send_sems, recv_sems):
        x, y, c = _place()
        chips = _other_chips(x, y)
        cps = [pltpu.make_async_remote_copy(src_ref=a_ref.at[2 * px + py], dst_ref=o_ref.at[k],
                                            send_sem=send_sems.at[k], recv_sem=recv_sems.at[k],
                                            device_id=(px, py, c), device_id_type=MESH)
               for k, (px, py) in enumerate(chips)]
        for cp in cps:
            cp.start()
        for cp in cps:
            cp.wait()

    hbm = pl.BlockSpec(memory_space=pltpu.HBM)
    return pl.pallas_call(
        body, name=name, in_specs=[hbm], out_specs=hbm, out_shape=jax.ShapeDtypeStruct((3, R, Cc), a.dtype),
        scratch_shapes=[pltpu.SemaphoreType.DMA((3,)), pltpu.SemaphoreType.DMA((3,))],
    )(a)


def _pair_sum(keep, got, *, name):
    n, R, Cc = keep.shape
    tr = _tile(R, 256, SUBLANES)

    def body(a_ref, b_ref, o_ref, ob_ref):
        s = a_ref[...] + b_ref[...]
        o_ref[...] = s
        ob_ref[...] = s.astype(BF16)

    blk = pl.BlockSpec((1, tr, Cc), lambda j, i: (j, i, 0))
    return pl.pallas_call(
        body, name=name, grid=(n, R // tr), in_specs=[blk, blk], out_specs=[blk, blk],
        out_shape=[jax.ShapeDtypeStruct(keep.shape, F32), jax.ShapeDtypeStruct(keep.shape, BF16)],
        compiler_params=_params(("parallel", "parallel")),
    )(keep, got)


def _chip_sum(own, got, *, name):
    R, Cc = own.shape
    tr = _tile(R, 256, SUBLANES)

    def body(a_ref, g_ref, o_ref):
        o_ref[...] = ((a_ref[...] + g_ref[0].astype(F32)) + g_ref[1].astype(F32)) + g_ref[2].astype(F32)

    return pl.pallas_call(
        body, name=name, grid=(R // tr,),
        in_specs=[pl.BlockSpec((tr, Cc), lambda i: (i, 0)), pl.BlockSpec((3, tr, Cc), lambda i: (0, i, 0))],
        out_specs=pl.BlockSpec((tr, Cc), lambda i: (i, 0)),
        out_shape=jax.ShapeDtypeStruct((R, Cc), F32),
        compiler_params=_params(("parallel",)),
    )(own, got)


PACK_COLS = 1024


def _pack_rows(parts):
    return jnp.concatenate([p.reshape(-1, PACK_COLS) for p in parts], axis=0)


def _flat128(parts):
    out = []
    for p in parts:
        v = p.reshape(-1)
        pad = (-v.shape[0]) % LANES
        out.append(jnp.pad(v, (0, pad)) if pad else v)
    v = jnp.concatenate(out)
    pad = (-v.shape[0]) % (SUBLANES * LANES)
    if pad:
        v = jnp.pad(v, (0, pad))
    return v.reshape(-1, LANES)


def _unflat128(block, shapes):
    v = block.reshape(-1)
    out, off = [], 0
    for s in shapes:
        n = int(np.prod(s))
        out.append(v[off:off + n].reshape(s))
        off += n + ((-n) % LANES)
    return out


def kernel(x, hgrn_w_in, hgrn_lb_logits, hgrn_gnorm_w, hgrn_w_out, swa_w_q, swa_sinks, swa_w_out, shared_w_kv, rel_bias, ffn_w_in, ffn_conv_w, ffn_conv_b, ffn_w_out, ln_mix_g, ln_mix_b, ln_ffn_g, ln_ffn_b, loss_target, m_hgrn_w_in, m_hgrn_lb_logits, m_hgrn_gnorm_w, m_hgrn_w_out, m_swa_w_q, m_swa_sinks, m_swa_w_out, m_shared_w_kv, m_rel_bias, m_ffn_w_in, m_ffn_conv_w, m_ffn_conv_b, m_ffn_w_out, m_ln_mix_g, m_ln_mix_b, m_ln_ffn_g, m_ln_ffn_b, v_hgrn_w_in, v_hgrn_lb_logits, v_hgrn_gnorm_w, v_hgrn_w_out, v_swa_w_q, v_swa_sinks, v_swa_w_out, v_shared_w_kv, v_rel_bias, v_ffn_w_in, v_ffn_conv_w, v_ffn_conv_b, v_ffn_w_out, v_ln_mix_g, v_ln_mix_b, v_ln_ffn_g, v_ln_ffn_b):
    xi, yi, ci = _place()
    chip = 2 * xi + yi
    T = x.shape[1]
    Dm = D_MODEL
    FC = 2 * FFN_DIM // N_CHIPS
    Fo = FFN_DIM // N_CHIPS
    Dq = Dm // N_CHIPS

    big = [hgrn_w_in, hgrn_w_out, swa_w_q, swa_w_out, shared_w_kv, ffn_w_in, ffn_w_out]
    rows = [int(np.prod(p.shape)) // PACK_COLS for p in big]
    offs = np.concatenate([[0], np.cumsum(rows)]).tolist()
    RP = offs[-1]
    packed = _pack_rows([p.astype(BF16) for p in big])
    allw = _gather_chips(packed, name="gather_weights")

    def piece(k, shape):
        return allw[:, offs[k]:offs[k + 1], :].reshape((N_CHIPS,) + shape)

    w_in = piece(0, (Dm, Dm))
    w_fi = piece(5, (DEPTH, Dm, FC))
    w_fo = piece(6, (DEPTH, Fo, Dm))
    w = {
        "hg_in": [w_in[j] for j in range(4)],
        "hg_out": piece(1, (Dq, Dm)).reshape(Dm, Dm),
        "sw_q": piece(2, (Dq, Dm)).reshape(Dm, Dm),
        "sw_out": piece(3, (Dq, Dm)).reshape(Dm, Dm),
        "kv": piece(4, (Dq, 2 * KV_DIM)).reshape(Dm, 2 * KV_DIM),
        "ffn_in_a": [jnp.concatenate([w_fi[0, l], w_fi[1, l]], axis=1) for l in range(DEPTH)],
        "ffn_in_b": [jnp.concatenate([w_fi[2, l], w_fi[3, l]], axis=1) for l in range(DEPTH)],
        "ffn_out": [w_fo[:, l].reshape(FFN_DIM, Dm) for l in range(DEPTH)],
    }

    lb_full = lax.dynamic_update_slice(jnp.zeros((2, Dm), F32), hgrn_lb_logits, (0, chip * Dq))
    cw_full = lax.dynamic_update_slice(jnp.zeros((DEPTH, 3, 2 * FFN_DIM), F32), ffn_conv_w, (0, 0, chip * FC))
    only_south = (ci == 0).astype(F32)
    small_in = _sum8(_flat128([lb_full, cw_full]) * only_south, name="gather_small")
    lb_full, cw_full = _unflat128(small_in, [(2, Dm), (DEPTH, 3, 2 * FFN_DIM)])
    w.update({
        "lb_logits": lb_full, "gnorm": hgrn_gnorm_w, "sinks": swa_sinks, "rel_bias": rel_bias,
        "conv_w_a": [cw_full[l, :, :FFN_DIM] for l in range(DEPTH)],
        "conv_w_b": [cw_full[l, :, FFN_DIM:] for l in range(DEPTH)],
        "conv_b_a": [ffn_conv_b[l:l + 1, :FFN_DIM] for l in range(DEPTH)],
        "conv_b_b": [ffn_conv_b[l:l + 1, FFN_DIM:] for l in range(DEPTH)],
        "ln_mix_g": [ln_mix_g[l:l + 1] for l in range(DEPTH)], "ln_mix_b": [ln_mix_b[l:l + 1] for l in range(DEPTH)],
        "ln_ffn_g": [ln_ffn_g[l:l + 1] for l in range(DEPTH)], "ln_ffn_b": [ln_ffn_b[l:l + 1] for l in range(DEPTH)],
    })

    loss_tile, grad_x, g = _local_step(x[0], loss_target[0], w)

    gf = g["ffn"]
    g_fi = [jnp.concatenate([gf[l]["ffn_in_a"], gf[l]["ffn_in_b"]], axis=1) for l in range(DEPTH)]

    def chip_block(j):
        parts = [
            g["hg_in"][j],
            g["hg_out"][j * Dq:(j + 1) * Dq],
            g["sw_q"][j * Dq:(j + 1) * Dq],
            g["sw_out"][j * Dq:(j + 1) * Dq],
            g["kv"][j * Dq:(j + 1) * Dq],
            jnp.stack([g_fi[l][:, j * FC:(j + 1) * FC] for l in range(DEPTH)]),
            jnp.stack([gf[l]["ffn_out"][j * Fo:(j + 1) * Fo] for l in range(DEPTH)]),
        ]
        return _pack_rows(parts)

    G = jnp.stack([chip_block(j) for j in range(N_CHIPS)])
    half = RP // 2
    keep = lax.dynamic_slice_in_dim(G, ci * half, half, axis=1)
    give = lax.dynamic_slice_in_dim(G, (1 - ci) * half, half, axis=1)
    got = _swap_sibling(give, name="rs_pair_swap")
    pair, pair_b = _pair_sum(keep, got, name="rs_pair_sum")
    others = _scatter_chips(pair_b, name="rs_chip_scatter")
    own = lax.dynamic_index_in_dim(pair, chip, axis=0, keepdims=False)
    mine = _chip_sum(own, others, name="rs_chip_sum")
    theirs = _swap_sibling(mine, name="rs_half_swap")
    lo = jnp.where(ci == 0, mine, theirs)
    hi = jnp.where(ci == 0, theirs, mine)
    gsh = jnp.concatenate([lo, hi], axis=0)

    big_g = [gsh[offs[k]:offs[k + 1]].reshape(big[k].shape) for k in range(len(big))]

    small_shapes = [(SUBLANES, LANES), (2, Dm), (1, HG_DIM), (1, SW_Q_HEADS), (REL_BUCKETS, SW_Q_HEADS),
                    (DEPTH, 3, 2 * FFN_DIM), (DEPTH, 2 * FFN_DIM)] + [(DEPTH, Dm)] * 4
    conv_w_g = jnp.stack([jnp.concatenate([gf[l]["conv_w_a"], gf[l]["conv_w_b"]], axis=1) for l in range(DEPTH)])
    conv_b_g = jnp.concatenate([jnp.concatenate([gf[l]["conv_b_a"], gf[l]["conv_b_b"]], axis=1) for l in range(DEPTH)], axis=0)
    ln_g = [jnp.concatenate([g[f"{n}0"], g[f"{n}1"]], axis=0) for n in ("ln_mix_g", "ln_mix_b", "ln_ffn_g", "ln_ffn_b")]
    small_out = _sum8(_flat128([loss_tile, g["lb_logits"], g["gnorm"], g["sinks"], g["rel_bias"], conv_w_g, conv_b_g] + ln_g),
                      name="sum_small")
    (loss_t, g_lb, g_gn, g_sinks, g_rel, g_cw, g_cb, g_lmg, g_lmb, g_lfg, g_lfb) = _unflat128(small_out, small_shapes)
    loss = loss_t[0, 0]
    g_lb = lax.dynamic_slice_in_dim(g_lb, chip * Dq, Dq, axis=1)
    g_cw = lax.dynamic_slice_in_dim(g_cw, chip * FC, FC, axis=2)

    names = ["hgrn_w_in", "hgrn_lb_logits", "hgrn_gnorm_w", "hgrn_w_out", "swa_w_q", "swa_sinks", "swa_w_out",
             "shared_w_kv", "rel_bias", "ffn_w_in", "ffn_conv_w", "ffn_conv_b", "ffn_w_out",
             "ln_mix_g", "ln_mix_b", "ln_ffn_g", "ln_ffn_b"]
    wts = dict(zip(names, [hgrn_w_in, hgrn_lb_logits, hgrn_gnorm_w, hgrn_w_out, swa_w_q, swa_sinks, swa_w_out,
                           shared_w_kv, rel_bias, ffn_w_in, ffn_conv_w, ffn_conv_b, ffn_w_out,
                           ln_mix_g, ln_mix_b, ln_ffn_g, ln_ffn_b]))
    ms = dict(zip(names, [m_hgrn_w_in, m_hgrn_lb_logits, m_hgrn_gnorm_w, m_hgrn_w_out, m_swa_w_q, m_swa_sinks, m_swa_w_out,
                          m_shared_w_kv, m_rel_bias, m_ffn_w_in, m_ffn_conv_w, m_ffn_conv_b, m_ffn_w_out,
                          m_ln_mix_g, m_ln_mix_b, m_ln_ffn_g, m_ln_ffn_b]))
    vs = dict(zip(names, [v_hgrn_w_in, v_hgrn_lb_logits, v_hgrn_gnorm_w, v_hgrn_w_out, v_swa_w_q, v_swa_sinks, v_swa_w_out,
                          v_shared_w_kv, v_rel_bias, v_ffn_w_in, v_ffn_conv_w, v_ffn_conv_b, v_ffn_w_out,
                          v_ln_mix_g, v_ln_mix_b, v_ln_ffn_g, v_ln_ffn_b]))
    big_names = ["hgrn_w_in", "hgrn_w_out", "swa_w_q", "swa_w_out", "shared_w_kv", "ffn_w_in", "ffn_w_out"]
    grads = dict(zip(big_names, big_g))
    grads.update(hgrn_lb_logits=g_lb, hgrn_gnorm_w=g_gn, swa_sinks=g_sinks, rel_bias=g_rel, ffn_conv_w=g_cw,
                 ffn_conv_b=g_cb, ln_mix_g=g_lmg, ln_mix_b=g_lmb, ln_ffn_g=g_lfg, ln_ffn_b=g_lfb)

    delta, new_m, new_v = {}, {}, {}
    for n in big_names:
        shp = wts[n].shape
        two_d = (-1, shp[-1])
        d_, m_, v_ = _adamw(wts[n].reshape(two_d), grads[n].reshape(two_d), ms[n].reshape(two_d), vs[n].reshape(two_d),
                            name=f"adamw_{n}")
        delta[n], new_m[n], new_v[n] = d_.reshape(shp), m_.reshape(shp), v_.reshape(shp)
    small_names = [n for n in names if n not in big_names]
    sshapes = [wts[n].shape for n in small_names]
    d_, m_, v_ = _adamw(_flat128([wts[n] for n in small_names]), _flat128([grads[n] for n in small_names]),
                        _flat128([ms[n] for n in small_names]), _flat128([vs[n] for n in small_names]), name="adamw_small")
    for n, a, b_, c_ in zip(small_names, _unflat128(d_, sshapes), _unflat128(m_, sshapes), _unflat128(v_, sshapes)):
        delta[n], new_m[n], new_v[n] = a, b_, c_

    return (loss, grad_x[None], *[grads[n] for n in names], *[delta[n] for n in names],
            *[new_m[n] for n in names], *[new_v[n] for n in names])
```

```python
import functools
import math

import numpy as np
import jax
import jax.numpy as jnp
from jax import lax
from jax.experimental import pallas as pl
from jax.experimental.pallas import tpu as pltpu

F32 = jnp.float32
BF16 = jnp.bfloat16
MESH = pl.DeviceIdType.MESH

D_MODEL = 1024
DEPTH = 2
HG_HEADS = 8
HG_DIM = 128
SW_Q_HEADS = 16
SW_KV_HEADS = 4
SW_HEAD_DIM = 64
SW_GROUP = 4
SW_WINDOW = 128
REL_BUCKETS = 32
REL_MAX_DIST = 128
FFN_DIM = 2816
ALPHA = (2.0 * DEPTH) ** 0.25
LN_EPS = 1e-5
RMS_EPS = 1e-6
ADAM_LR = 0.001
ADAM_B1 = 0.9
ADAM_B2 = 0.999
ADAM_EPS = 1e-08
ADAM_WD = 0.01
ADAM_STEP = 10

VMEM_BYTES_V7X = 64 * 1024 * 1024
VMEM_LIMIT = VMEM_BYTES_V7X - 8 * 1024 * 1024
LANES = 128
SUBLANES = 8

HG_C = 64
HG_RB = 256
ROW_TILE = 256
CONV_R = 256
N_CHIPS = 4
N_DEV = 8

ANY_SPEC = pl.BlockSpec(memory_space=pl.ANY)


def _after(body, n_in, after):
    if after is None:
        return body, [], ()

    def wrapped(*refs):
        return body(*refs[:n_in], *refs[n_in + 1:])

    return wrapped, [ANY_SPEC], (after,)


def _params(sem=None):
    return pltpu.CompilerParams(dimension_semantics=sem, vmem_limit_bytes=VMEM_LIMIT)


def _tile(n, pref, unit=LANES):
    if n <= pref:
        return n
    best = None
    for t in range(unit, pref + 1, unit):
        if n % t == 0:
            best = t
    assert best is not None, (n, pref, unit)
    return best


def _dot(a, b, ca, cb):
    nb = a.ndim - 2
    batch = tuple(range(nb))
    return lax.dot_general(a.astype(BF16), b.astype(BF16), (((nb + ca,), (nb + cb,)), (batch, batch)),
                           preferred_element_type=F32)


@jax.custom_vjp
def mm(a, b):
    return _dot(a, b, 1, 0)


@jax.custom_vjp
def mm_nt(a, b):
    return _dot(a, b, 1, 1)


@jax.custom_vjp
def mm_tn(a, b):
    return _dot(a, b, 0, 0)


mm.defvjp(lambda a, b: (mm(a, b), (a, b)), lambda r, ct: (mm_nt(ct, r[1]), mm_tn(r[0], ct)))
mm_nt.defvjp(lambda a, b: (mm_nt(a, b), (a, b)), lambda r, ct: (mm(ct, r[1]), mm_tn(ct, r[0])))
mm_tn.defvjp(lambda a, b: (mm_tn(a, b), (a, b)), lambda r, ct: (mm_nt(r[1], ct), mm(r[0], ct)))


def _split2(x):
    hi = x.astype(BF16)
    return hi, (x - hi.astype(F32)).astype(BF16)


@jax.custom_vjp
def _scores(qt, kt):
    return _dot(qt, kt, 1, 1)


def _scores_bwd(r, ct):
    (qh, ql), (kh, kl) = _split2(r[0]), _split2(r[1])
    return _dot(ct, kh, 1, 0) + _dot(ct, kl, 1, 0), _dot(ct, qh, 0, 0) + _dot(ct, ql, 0, 0)


_scores.defvjp(lambda a, b: (_scores(a, b), (a, b)), _scores_bwd)


def _split3(x):
    hi = x.astype(BF16)
    r1 = x - hi.astype(F32)
    mid = r1.astype(BF16)
    lo = (r1 - mid.astype(F32)).astype(BF16)
    return hi, mid, lo


def _cumsum_impl(x):
    ax = x.ndim - 2
    n = x.shape[ax]
    row = lax.broadcasted_iota(jnp.int32, x.shape, ax)
    d = 1
    while d < n:
        x = x + jnp.where(row >= d, pltpu.roll(x, d, ax), 0.0)
        d *= 2
    return x


def _cumsum_rev_impl(x):
    ax = x.ndim - 2
    n = x.shape[ax]
    row = lax.broadcasted_iota(jnp.int32, x.shape, ax)
    d = 1
    while d < n:
        x = x + jnp.where(row < n - d, pltpu.roll(x, n - d, ax), 0.0)
        d *= 2
    return x


@jax.custom_vjp
def _cumsum(x):
    return _cumsum_impl(x)


_cumsum.defvjp(lambda x: (_cumsum_impl(x), None), lambda _, ct: (_cumsum_rev_impl(ct),))


def _matmul(a, b, *, mode, name, out_dtype=F32, add=None, add_scale=1.0, tm=512, tn=1408, tk=1408, after=None):
    if mode == "nn":
        (M, K), (K2, N) = a.shape, b.shape
    elif mode == "nt":
        (M, K), (N, K2) = a.shape, b.shape
    else:
        (K, M), (K2, N) = a.shape, b.shape
    assert K == K2, (a.shape, b.shape, mode)
    tm, tn, tk = _tile(M, tm), _tile(N, tn), _tile(K, tk)
    nk = K // tk
    ca, cb = {"nn": (1, 0), "nt": (1, 1), "tn": (0, 0)}[mode]
    a_spec = {"nn": pl.BlockSpec((tm, tk), lambda i, j, k: (i, k)),
              "nt": pl.BlockSpec((tm, tk), lambda i, j, k: (i, k)),
              "tn": pl.BlockSpec((tk, tm), lambda i, j, k: (k, i))}[mode]
    b_spec = {"nn": pl.BlockSpec((tk, tn), lambda i, j, k: (k, j)),
              "nt": pl.BlockSpec((tn, tk), lambda i, j, k: (j, k)),
              "tn": pl.BlockSpec((tk, tn), lambda i, j, k: (k, j))}[mode]
    o_spec = pl.BlockSpec((tm, tn), lambda i, j, k: (i, j))
    has_add = add is not None

    def body(*refs):
        if has_add:
            a_ref, b_ref, add_ref, o_ref, acc_ref = refs
        else:
            a_ref, b_ref, o_ref, acc_ref = refs
        k = pl.program_id(2)

        @pl.when(k == 0)
        def _():
            acc_ref[...] = jnp.zeros_like(acc_ref)

        acc_ref[...] += _dot(a_ref[...], b_ref[...], ca, cb)

        @pl.when(k == nk - 1)
        def _():
            r = acc_ref[...]
            if has_add:
                r = r + add_scale * add_ref[...]
            o_ref[...] = r.astype(out_dtype)

    in_specs = [a_spec, b_spec] + ([o_spec] if has_add else [])
    args = (a, b) + ((add,) if has_add else ())
    body, xs, xa = _after(body, len(args), after)
    in_specs, args = in_specs + xs, args + xa
    return pl.pallas_call(
        body, name=name, grid=(M // tm, N // tn, nk), in_specs=in_specs, out_specs=o_spec,
        out_shape=jax.ShapeDtypeStruct((M, N), out_dtype),
        scratch_shapes=[pltpu.VMEM((tm, tn), F32)],
        compiler_params=_params(("parallel", "parallel", "arbitrary")),
    )(*args)


def _ln(z, g, b):
    mu = jnp.mean(z, axis=-1, keepdims=True)
    zc = z - mu
    var = jnp.mean(zc * zc, axis=-1, keepdims=True)
    return zc * lax.rsqrt(var + LN_EPS) * g + b


def _ln_fwd(h, s, g, b, *, name):
    T, Dm = h.shape
    tr = _tile(T, ROW_TILE, SUBLANES)

    def body(h_ref, s_ref, g_ref, b_ref, y_ref, yb_ref):
        y = _ln(ALPHA * h_ref[...] + s_ref[...], g_ref[...], b_ref[...])
        y_ref[...] = y
        yb_ref[...] = y.astype(BF16)

    row = pl.BlockSpec((tr, Dm), lambda i: (i, 0))
    vec = pl.BlockSpec((1, Dm), lambda i: (0, 0))
    return pl.pallas_call(
        body, name=name, grid=(T // tr,), in_specs=[row, row, vec, vec], out_specs=[row, row],
        out_shape=[jax.ShapeDtypeStruct((T, Dm), F32), jax.ShapeDtypeStruct((T, Dm), BF16)],
        compiler_params=_params(("parallel",)),
    )(h, s, g, b)


def _ln_bwd(dy, h, s, g, b, *, name, after=None):
    T, Dm = h.shape
    tr = _tile(T, ROW_TILE, SUBLANES)

    def body(dy_ref, h_ref, s_ref, g_ref, b_ref, dz_ref, dzb_ref, dg_ref, db_ref):
        @pl.when(pl.program_id(0) == 0)
        def _():
            dg_ref[...] = jnp.zeros_like(dg_ref)
            db_ref[...] = jnp.zeros_like(db_ref)

        z = ALPHA * h_ref[...] + s_ref[...]
        _, vjp = jax.vjp(_ln, z, g_ref[...], b_ref[...])
        dz, dg, db = vjp(dy_ref[...])
        dz_ref[...] = dz
        dzb_ref[...] = dz.astype(BF16)
        dg_ref[...] += dg
        db_ref[...] += db

    row = pl.BlockSpec((tr, Dm), lambda i: (i, 0))
    vec = pl.BlockSpec((1, Dm), lambda i: (0, 0))
    body, xs, xa = _after(body, 5, after)
    return pl.pallas_call(
        body, name=name, grid=(T // tr,), in_specs=[row, row, row, vec, vec] + xs,
        out_specs=[row, row, vec, vec],
        out_shape=[jax.ShapeDtypeStruct((T, Dm), F32), jax.ShapeDtypeStruct((T, Dm), BF16),
                   jax.ShapeDtypeStruct((1, Dm), F32), jax.ShapeDtypeStruct((1, Dm), F32)],
        compiler_params=_params(("arbitrary",)),
    )(dy, h, s, g, b, *xa)


def _loss_grad(y, tgt, *, name):
    T, Dm = y.shape
    tr = _tile(T, ROW_TILE, SUBLANES)

    def body(y_ref, t_ref, dy_ref, l_ref):
        @pl.when(pl.program_id(0) == 0)
        def _():
            l_ref[...] = jnp.zeros_like(l_ref)

        e = y_ref[...] - t_ref[...]
        dy_ref[...] = e * (1.0 / Dm)
        l_ref[...] += 0.5 * jnp.sum(jnp.mean(e * e, axis=-1, keepdims=True), axis=0, keepdims=True)

    row = pl.BlockSpec((tr, Dm), lambda i: (i, 0))
    lsp = pl.BlockSpec((SUBLANES, LANES), lambda i: (0, 0))
    return pl.pallas_call(
        body, name=name, grid=(T // tr,), in_specs=[row, row], out_specs=[row, lsp],
        out_shape=[jax.ShapeDtypeStruct((T, Dm), F32), jax.ShapeDtypeStruct((SUBLANES, LANES), F32)],
        compiler_params=_params(("arbitrary",)),
    )(y, tgt)


def _hg_chunk(qr, fr, ir, gr, l0, l1, gw, st):
    C = qr.shape[-2]
    row = lax.broadcasted_iota(jnp.int32, qr.shape, qr.ndim - 2)
    lb = jax.nn.sigmoid(l0 - l1)
    fg = lb + (1.0 - lb) * jax.nn.sigmoid(fr)
    b = _cumsum(jnp.log(fg))
    q = jax.nn.silu(qr)
    k = 1.0 - fg
    bmid = lax.stop_gradient(jnp.sum(jnp.where(row == C // 2 - 1, b, 0.0), axis=-2, keepdims=True))
    bl = jnp.sum(jnp.where(row == C - 1, b, 0.0), axis=-2, keepdims=True)
    o = mm_nt(q * jnp.exp(b), st)
    sc = _scores(q * jnp.exp(b - bmid), k * jnp.exp(bmid - b))
    ti = lax.broadcasted_iota(jnp.int32, (C, C), 0)
    si = lax.broadcasted_iota(jnp.int32, (C, C), 1)
    sc = jnp.where(si <= ti, sc, 0.0)
    o = o + mm(sc, ir)
    st_new = st * jnp.exp(bl) + mm_tn(ir, k * jnp.exp(bl - b))
    on = o * lax.rsqrt(jnp.mean(o * o, axis=-1, keepdims=True) + RMS_EPS)
    return on * gw * jax.nn.silu(gr), st_new


def _heads(ref, rows):
    return jnp.stack([ref[rows, h * HG_DIM:(h + 1) * HG_DIM] for h in range(HG_HEADS)])


def _unheads(x):
    return jnp.concatenate([x[h] for h in range(HG_HEADS)], axis=-1)


def _hgrn_fwd(q, f, i, g, lbl, gw, *, name):
    T, Dm = q.shape
    rb = min(HG_RB, T)
    C = min(HG_C, rb)
    ncb = rb // C

    def body(q_ref, f_ref, i_ref, g_ref, lbl_ref, gw_ref, o_ref, st_ref, s_ref):
        @pl.when(pl.program_id(0) == 0)
        def _():
            s_ref[...] = jnp.zeros_like(s_ref)

        def chunk(ci, carry):
            r0 = pl.multiple_of(ci * C, C)
            rows = pl.ds(r0, C)
            st = s_ref[...]
            st_ref[ci] = st
            out, st_new = _hg_chunk(_heads(q_ref, rows), _heads(f_ref, rows), _heads(i_ref, rows), _heads(g_ref, rows),
                                    _heads(lbl_ref, slice(0, 1)), _heads(lbl_ref, slice(1, 2)), gw_ref[...], st)
            o_ref[rows, :] = _unheads(out).astype(BF16)
            s_ref[...] = st_new
            return carry

        lax.fori_loop(0, ncb, chunk, 0)

    row = pl.BlockSpec((rb, Dm), lambda n: (n, 0))
    return pl.pallas_call(
        body, name=name, grid=(T // rb,),
        in_specs=[row, row, row, row, pl.BlockSpec((2, Dm), lambda n: (0, 0)), pl.BlockSpec((1, HG_DIM), lambda n: (0, 0))],
        out_specs=[row, pl.BlockSpec((ncb, HG_HEADS, HG_DIM, HG_DIM), lambda n: (n, 0, 0, 0))],
        out_shape=[jax.ShapeDtypeStruct((T, Dm), BF16),
                   jax.ShapeDtypeStruct((T // C, HG_HEADS, HG_DIM, HG_DIM), F32)],
        scratch_shapes=[pltpu.VMEM((HG_HEADS, HG_DIM, HG_DIM), F32)],
        compiler_params=_params(("arbitrary",)),
    )(q, f, i, g, lbl, gw)


def _hgrn_bwd(q, f, i, g, lbl, gw, states, dout, *, name, after=None):
    T, Dm = q.shape
    rb = min(HG_RB, T)
    C = min(HG_C, rb)
    ncb = rb // C
    nb = T // rb

    def body(q_ref, f_ref, i_ref, g_ref, lbl_ref, gw_ref, st_ref, do_ref,
             dq_ref, df_ref, di_ref, dg_ref, dlbl_ref, dgw_ref, ds_ref):
        @pl.when(pl.program_id(0) == 0)
        def _():
            ds_ref[...] = jnp.zeros_like(ds_ref)
            dlbl_ref[...] = jnp.zeros_like(dlbl_ref)
            dgw_ref[...] = jnp.zeros_like(dgw_ref)

        def chunk(cj, carry):
            ci = ncb - 1 - cj
            r0 = pl.multiple_of(ci * C, C)
            rows = pl.ds(r0, C)
            _, vjp = jax.vjp(_hg_chunk, _heads(q_ref, rows), _heads(f_ref, rows), _heads(i_ref, rows), _heads(g_ref, rows),
                             _heads(lbl_ref, slice(0, 1)), _heads(lbl_ref, slice(1, 2)), gw_ref[...], st_ref[ci])
            dq, df, di, dg, dl0, dl1, dgw, dst = vjp((_heads(do_ref, rows).astype(F32), ds_ref[...]))
            dq_ref[rows, :] = _unheads(dq).astype(BF16)
            df_ref[rows, :] = _unheads(df).astype(BF16)
            di_ref[rows, :] = _unheads(di).astype(BF16)
            dg_ref[rows, :] = _unheads(dg).astype(BF16)
            dlbl_ref[0:1, :] += _unheads(dl0)
            dlbl_ref[1:2, :] += _unheads(dl1)
            dgw_ref[...] += dgw
            ds_ref[...] = dst
            return carry

        lax.fori_loop(0, ncb, chunk, 0)

    row = pl.BlockSpec((rb, Dm), lambda n: (nb - 1 - n, 0))
    lsp = pl.BlockSpec((2, Dm), lambda n: (0, 0))
    gsp = pl.BlockSpec((1, HG_DIM), lambda n: (0, 0))
    body, xs, xa = _after(body, 8, after)
    return pl.pallas_call(
        body, name=name, grid=(nb,),
        in_specs=[row, row, row, row, lsp, gsp,
                  pl.BlockSpec((ncb, HG_HEADS, HG_DIM, HG_DIM), lambda n: (nb - 1 - n, 0, 0, 0)), row] + xs,
        out_specs=[row, row, row, row, lsp, gsp],
        out_shape=[jax.ShapeDtypeStruct((T, Dm), BF16)] * 4
        + [jax.ShapeDtypeStruct((2, Dm), F32), jax.ShapeDtypeStruct((1, HG_DIM), F32)],
        scratch_shapes=[pltpu.VMEM((HG_HEADS, HG_DIM, HG_DIM), F32)],
        compiler_params=_params(("arbitrary",)),
    )(q, f, i, g, lbl, gw, states, dout, *xa)


def _shift_down(cur, prev8):
    big = jnp.concatenate([prev8, cur], axis=0)
    return pltpu.roll(big, 1, 0)[SUBLANES:], pltpu.roll(big, 2, 0)[SUBLANES:]


def _shift_up(cur, next8):
    n = cur.shape[0] + SUBLANES
    big = jnp.concatenate([cur, next8], axis=0)
    return pltpu.roll(big, n - 1, 0)[:cur.shape[0]], pltpu.roll(big, n - 2, 0)[:cur.shape[0]]


def _conv_rows(u_ref, w, bias, r0, R):
    cur = u_ref[pl.ds(r0, R), :]
    p0 = pl.multiple_of(jnp.maximum(r0 - SUBLANES, 0), SUBLANES)
    prev8 = jnp.where(r0 > 0, u_ref[pl.ds(p0, SUBLANES), :], 0.0)
    s1, s2 = _shift_down(cur, prev8)
    return w[0:1, :] * s2 + w[1:2, :] * s1 + w[2:3, :] * cur + bias, cur, s1, s2


def _conv_gate_fwd(ua, ub, wa, wb, ba, bb, *, name):
    T, Fd = ua.shape
    R = min(CONV_R, T)
    tc = LANES

    def body(ua_ref, ub_ref, wa_ref, wb_ref, ba_ref, bb_ref, o_ref):
        wa_, wb_, ba_, bb_ = wa_ref[...], wb_ref[...], ba_ref[...], bb_ref[...]

        def step(ri, carry):
            r0 = pl.multiple_of(ri * R, R)
            ca = _conv_rows(ua_ref, wa_, ba_, r0, R)[0]
            cb = _conv_rows(ub_ref, wb_, bb_, r0, R)[0]
            o_ref[pl.ds(r0, R), :] = (jax.nn.silu(ca) * cb).astype(BF16)
            return carry

        lax.fori_loop(0, T // R, step, 0)

    col = pl.BlockSpec((T, tc), lambda j: (0, j))
    wsp = pl.BlockSpec((3, tc), lambda j: (0, j))
    bsp = pl.BlockSpec((1, tc), lambda j: (0, j))
    return pl.pallas_call(
        body, name=name, grid=(Fd // tc,), in_specs=[col, col, wsp, wsp, bsp, bsp], out_specs=col,
        out_shape=jax.ShapeDtypeStruct((T, Fd), BF16),
        compiler_params=_params(("parallel",)),
    )(ua, ub, wa, wb, ba, bb)


def _conv_gate_bwd(ua, ub, wa, wb, ba, bb, dact, *, name):
    T, Fd = ua.shape
    R = min(CONV_R, T)
    nr = T // R
    tc = LANES

    def body(ua_ref, ub_ref, wa_ref, wb_ref, ba_ref, bb_ref, da_ref,
             dua_ref, dub_ref, dwa_ref, dwb_ref, dba_ref, dbb_ref, dca_ref, dcb_ref):
        wa_, wb_, ba_, bb_ = wa_ref[...], wb_ref[...], ba_ref[...], bb_ref[...]

        def taps(dc, cur, s1, s2):
            return jnp.concatenate([jnp.sum(dc * s2, axis=0, keepdims=True), jnp.sum(dc * s1, axis=0, keepdims=True),
                                    jnp.sum(dc * cur, axis=0, keepdims=True)], axis=0)

        def first(ri, carry):
            dwa, dwb, dba, dbb = carry
            r0 = pl.multiple_of(ri * R, R)
            ca, cura, s1a, s2a = _conv_rows(ua_ref, wa_, ba_, r0, R)
            cb, curb, s1b, s2b = _conv_rows(ub_ref, wb_, bb_, r0, R)
            dact_ = da_ref[pl.ds(r0, R), :].astype(F32)
            sg = jax.nn.sigmoid(ca)
            dca = dact_ * cb * (sg * (1.0 + ca * (1.0 - sg)))
            dcb = dact_ * (ca * sg)
            dca_ref[pl.ds(r0, R), :] = dca
            dcb_ref[pl.ds(r0, R), :] = dcb
            return (dwa + taps(dca, cura, s1a, s2a), dwb + taps(dcb, curb, s1b, s2b),
                    dba + jnp.sum(dca, axis=0, keepdims=True), dbb + jnp.sum(dcb, axis=0, keepdims=True))

        z3 = jnp.zeros((3, tc), F32)
        z1 = jnp.zeros((1, tc), F32)
        dwa, dwb, dba, dbb = lax.fori_loop(0, nr, first, (z3, z3, z1, z1))
        dwa_ref[...] = dwa
        dwb_ref[...] = dwb
        dba_ref[...] = dba
        dbb_ref[...] = dbb

        def du_rows(dc_ref, w, r0):
            cur = dc_ref[pl.ds(r0, R), :]
            n0 = pl.multiple_of(jnp.minimum(r0 + R, T - SUBLANES), SUBLANES)
            next8 = jnp.where(r0 + R < T, dc_ref[pl.ds(n0, SUBLANES), :], 0.0)
            m1, m2 = _shift_up(cur, next8)
            return w[2:3, :] * cur + w[1:2, :] * m1 + w[0:1, :] * m2

        def second(ri, carry):
            r0 = pl.multiple_of(ri * R, R)
            dua_ref[pl.ds(r0, R), :] = du_rows(dca_ref, wa_, r0).astype(BF16)
            dub_ref[pl.ds(r0, R), :] = du_rows(dcb_ref, wb_, r0).astype(BF16)
            return carry

        lax.fori_loop(0, nr, second, 0)

    col = pl.BlockSpec((T, tc), lambda j: (0, j))
    wsp = pl.BlockSpec((3, tc), lambda j: (0, j))
    bsp = pl.BlockSpec((1, tc), lambda j: (0, j))
    return pl.pallas_call(
        body, name=name, grid=(Fd // tc,), in_specs=[col, col, wsp, wsp, bsp, bsp, col],
        out_specs=[col, col, wsp, wsp, bsp, bsp],
        out_shape=[jax.ShapeDtypeStruct((T, Fd), BF16)] * 2 + [jax.ShapeDtypeStruct((3, Fd), F32)] * 2
        + [jax.ShapeDtypeStruct((1, Fd), F32)] * 2,
        scratch_shapes=[pltpu.VMEM((T, tc), F32), pltpu.VMEM((T, tc), F32)],
        compiler_params=_params(("parallel",)),
    )(ua, ub, wa, wb, ba, bb, dact)


def _bucket_index():
    t = np.arange(SW_WINDOW)[:, None] + SW_WINDOW
    s = np.arange(2 * SW_WINDOW)[None, :]
    dist = np.maximum(t - s, 0)
    exact = REL_BUCKETS // 2
    d = np.maximum(dist, 1).astype(np.float32)
    log_b = exact + (np.log(d / np.float32(exact)) / np.float32(math.log(REL_MAX_DIST / exact))
                     * np.float32(REL_BUCKETS - exact)).astype(np.int32)
    bucket = np.where(dist < exact, dist, np.minimum(log_b, REL_BUCKETS - 1))
    return bucket.astype(np.int32).reshape(1, -1)


BIAS_COLS = SW_WINDOW * 2 * SW_WINDOW
BIAS_TILE = 4096


def _bias_from_table(table, bucket, *, name):
    def body(t_ref, idx_ref, o_ref):
        onehot = (lax.broadcasted_iota(jnp.int32, (REL_BUCKETS, BIAS_TILE), 0) == idx_ref[...]).astype(BF16)
        acc = jnp.zeros((SW_Q_HEADS, BIAS_TILE), F32)
        for piece in _split3(t_ref[...]):
            acc = acc + lax.dot_general(piece, onehot, (((0,), (0,)), ((), ())), preferred_element_type=F32)
        o_ref[...] = acc

    return pl.pallas_call(
        body, name=name, grid=(BIAS_COLS // BIAS_TILE,),
        in_specs=[pl.BlockSpec((REL_BUCKETS, SW_Q_HEADS), lambda j: (0, 0)), pl.BlockSpec((1, BIAS_TILE), lambda j: (0, j))],
        out_specs=pl.BlockSpec((SW_Q_HEADS, BIAS_TILE), lambda j: (0, j)),
        out_shape=jax.ShapeDtypeStruct((SW_Q_HEADS, BIAS_COLS), F32),
        compiler_params=_params(("parallel",)),
    )(table, bucket)


def _table_grad(dbias, bucket, *, name):
    def body(d_ref, idx_ref, o_ref):
        @pl.when(pl.program_id(0) == 0)
        def _():
            o_ref[...] = jnp.zeros_like(o_ref)

        onehot = (lax.broadcasted_iota(jnp.int32, (REL_BUCKETS, BIAS_TILE), 0) == idx_ref[...]).astype(BF16)
        acc = jnp.zeros((REL_BUCKETS, SW_Q_HEADS), F32)
        for piece in _split3(d_ref[...]):
            acc = acc + lax.dot_general(onehot, piece, (((1,), (1,)), ((), ())), preferred_element_type=F32)
        o_ref[...] += acc

    return pl.pallas_call(
        body, name=name, grid=(BIAS_COLS // BIAS_TILE,),
        in_specs=[pl.BlockSpec((SW_Q_HEADS, BIAS_TILE), lambda j: (0, j)), pl.BlockSpec((1, BIAS_TILE), lambda j: (0, j))],
        out_specs=pl.BlockSpec((REL_BUCKETS, SW_Q_HEADS), lambda j: (0, 0)),
        out_shape=jax.ShapeDtypeStruct((REL_BUCKETS, SW_Q_HEADS), F32),
        compiler_params=_params(("arbitrary",)),
    )(dbias, bucket)


def _band_mask(n):
    rows = SW_GROUP * SW_WINDOW
    t = (lax.broadcasted_iota(jnp.int32, (rows, 2 * SW_WINDOW), 0) & (SW_WINDOW - 1)) + SW_WINDOW
    s = lax.broadcasted_iota(jnp.int32, (rows, 2 * SW_WINDOW), 1)
    dist = t - s
    return (dist >= 0) & (dist < SW_WINDOW) & ((n > 0) | (s >= SW_WINDOW))


def _head_cols(h):
    return slice(h * SW_HEAD_DIM, (h + 1) * SW_HEAD_DIM)


def _group_inputs(q_ref, bias_ref, sink_ref, g):
    heads = range(g * SW_GROUP, (g + 1) * SW_GROUP)
    q = jnp.concatenate([q_ref[:, _head_cols(h)] for h in heads], axis=0)
    sink = jnp.concatenate([jnp.broadcast_to(sink_ref[:, h:h + 1], (SW_WINDOW, 1)) for h in heads], axis=0)
    bias = bias_ref[g * SW_GROUP:(g + 1) * SW_GROUP].reshape(SW_GROUP * SW_WINDOW, 2 * SW_WINDOW)
    return heads, q, bias, sink


KV_DIM = SW_KV_HEADS * SW_HEAD_DIM


def _kv_pair(kvp_ref, kvc_ref, g):
    ks = slice(g * SW_HEAD_DIM, (g + 1) * SW_HEAD_DIM)
    vs = slice(KV_DIM + g * SW_HEAD_DIM, KV_DIM + (g + 1) * SW_HEAD_DIM)
    kk = jnp.concatenate([kvp_ref[:, ks], kvc_ref[:, ks]], axis=0)
    vv = jnp.concatenate([kvp_ref[:, vs], kvc_ref[:, vs]], axis=0)
    return kk, vv, ks, vs


def _attn_fwd(q1, kv, bias, sinks, *, name):
    T, Dm = q1.shape
    W = SW_WINDOW

    def body(q_ref, kvc_ref, kvp_ref, bias_ref, sink_ref, o_ref):
        mask = _band_mask(pl.program_id(0))
        G = range(SW_KV_HEADS)
        ins = [_group_inputs(q_ref, bias_ref, sink_ref, g) for g in G]
        kvs = [_kv_pair(kvp_ref, kvc_ref, g) for g in G]
        lg = [jnp.where(mask, mm_nt(ins[g][1], kvs[g][0]) * (SW_HEAD_DIM ** -0.5) + ins[g][2], -jnp.inf) for g in G]
        m = [jnp.maximum(jnp.max(lg[g], axis=-1, keepdims=True), ins[g][3]) for g in G]
        p = [jnp.exp(lg[g] - m[g]) for g in G]
        den = [jnp.sum(p[g], axis=-1, keepdims=True) + jnp.exp(ins[g][3] - m[g]) for g in G]
        o = [mm(p[g], kvs[g][1]) / den[g] for g in G]
        for g in G:
            for r, h in enumerate(ins[g][0]):
                o_ref[:, _head_cols(h)] = o[g][r * W:(r + 1) * W].astype(BF16)

    return pl.pallas_call(
        body, name=name, grid=(T // W,),
        in_specs=[pl.BlockSpec((W, Dm), lambda n: (n, 0)),
                  pl.BlockSpec((W, 2 * KV_DIM), lambda n: (n, 0)),
                  pl.BlockSpec((W, 2 * KV_DIM), lambda n: (jnp.maximum(n - 1, 0), 0)),
                  pl.BlockSpec((SW_Q_HEADS, W, 2 * W), lambda n: (0, 0, 0)),
                  pl.BlockSpec((1, SW_Q_HEADS), lambda n: (0, 0))],
        out_specs=pl.BlockSpec((W, Dm), lambda n: (n, 0)),
        out_shape=jax.ShapeDtypeStruct((T, Dm), BF16),
        compiler_params=_params(("parallel",)),
    )(q1, kv, kv, bias, sinks)


def _attn_bwd(q1, kv, bias, sinks, do, *, name):
    T, Dm = q1.shape
    W = SW_WINDOW
    nb = T // W

    def body(q_ref, kvc_ref, kvp_ref, bias_ref, sink_ref, do_ref,
             dq_ref, dkv_ref, dbias_ref, dsink_ref, carry_ref):
        @pl.when(pl.program_id(0) == 0)
        def _():
            carry_ref[...] = jnp.zeros_like(carry_ref)
            dbias_ref[...] = jnp.zeros_like(dbias_ref)
            dsink_ref[...] = jnp.zeros_like(dsink_ref)

        n = nb - 1 - pl.program_id(0)
        mask = _band_mask(n)
        lane = lax.broadcasted_iota(jnp.int32, (1, SW_Q_HEADS), 1)
        sc = SW_HEAD_DIM ** -0.5
        G = range(SW_KV_HEADS)
        ins = [_group_inputs(q_ref, bias_ref, sink_ref, g) for g in G]
        kvs = [_kv_pair(kvp_ref, kvc_ref, g) for g in G]
        do = [jnp.concatenate([do_ref[:, _head_cols(h)] for h in ins[g][0]], axis=0) for g in G]
        lg = [jnp.where(mask, mm_nt(ins[g][1], kvs[g][0]) * sc + ins[g][2], -jnp.inf) for g in G]
        m = [jnp.maximum(jnp.max(lg[g], axis=-1, keepdims=True), ins[g][3]) for g in G]
        p = [jnp.exp(lg[g] - m[g]) for g in G]
        ps = [jnp.exp(ins[g][3] - m[g]) for g in G]
        rden = [1.0 / (jnp.sum(p[g], axis=-1, keepdims=True) + ps[g]) for g in G]
        pn = [p[g] * rden[g] for g in G]
        dpn = [mm_nt(do[g], kvs[g][1]) for g in G]
        delta = [jnp.sum(pn[g] * dpn[g], axis=-1, keepdims=True) for g in G]
        ds = [pn[g] * (dpn[g] - delta[g]) for g in G]
        dsr = [-(ps[g] * rden[g]) * delta[g] for g in G]
        dq = [mm(ds[g], kvs[g][0]) * sc for g in G]
        dkk = [mm_tn(ds[g], ins[g][1]) * sc for g in G]
        dvv = [mm_tn(pn[g], do[g]) for g in G]
        dsink = jnp.zeros((1, SW_Q_HEADS), F32)
        for g in G:
            _, _, ks, vs = kvs[g]
            dbias_ref[g * SW_GROUP:(g + 1) * SW_GROUP] += ds[g].reshape(SW_GROUP, W, 2 * W)
            for r, h in enumerate(ins[g][0]):
                dq_ref[:, _head_cols(h)] = dq[g][r * W:(r + 1) * W].astype(BF16)
                dsink = dsink + jnp.where(lane == h, jnp.sum(dsr[g][r * W:(r + 1) * W], axis=0, keepdims=True), 0.0)
            dkv_ref[:, ks] = (carry_ref[:, ks] + dkk[g][W:]).astype(BF16)
            dkv_ref[:, vs] = (carry_ref[:, vs] + dvv[g][W:]).astype(BF16)
            carry_ref[:, ks] = dkk[g][:W]
            carry_ref[:, vs] = dvv[g][:W]
        dsink_ref[...] += dsink

    rev = lambda n: (nb - 1 - n, 0)
    return pl.pallas_call(
        body, name=name, grid=(nb,),
        in_specs=[pl.BlockSpec((W, Dm), rev),
                  pl.BlockSpec((W, 2 * KV_DIM), rev),
                  pl.BlockSpec((W, 2 * KV_DIM), lambda n: (jnp.maximum(nb - 2 - n, 0), 0)),
                  pl.BlockSpec((SW_Q_HEADS, W, 2 * W), lambda n: (0, 0, 0)),
                  pl.BlockSpec((1, SW_Q_HEADS), lambda n: (0, 0)),
                  pl.BlockSpec((W, Dm), rev)],
        out_specs=[pl.BlockSpec((W, Dm), rev), pl.BlockSpec((W, 2 * KV_DIM), rev),
                   pl.BlockSpec((SW_Q_HEADS, W, 2 * W), lambda n: (0, 0, 0)),
                   pl.BlockSpec((1, SW_Q_HEADS), lambda n: (0, 0))],
        out_shape=[jax.ShapeDtypeStruct((T, Dm), BF16), jax.ShapeDtypeStruct((T, 2 * KV_DIM), BF16),
                   jax.ShapeDtypeStruct((SW_Q_HEADS, W, 2 * W), F32), jax.ShapeDtypeStruct((1, SW_Q_HEADS), F32)],
        scratch_shapes=[pltpu.VMEM((W, 2 * KV_DIM), F32)],
        compiler_params=_params(("arbitrary",)),
    )(q1, kv, kv, bias, sinks, do)


def _ffn_fwd(hb, w, l):
    ua = _matmul(hb, w["ffn_in_a"][l], mode="nn", name=f"ffn{l}_up_a")
    ub = _matmul(hb, w["ffn_in_b"][l], mode="nn", name=f"ffn{l}_up_b")
    act = _conv_gate_fwd(ua, ub, w["conv_w_a"][l], w["conv_w_b"][l], w["conv_b_a"][l], w["conv_b_b"][l],
                         name=f"ffn{l}_conv_gate")
    ff = _matmul(act, w["ffn_out"][l], mode="nn", name=f"ffn{l}_down")
    return ua, ub, act, ff


def _ffn_bwd(dffb, dh_scaled, hb, ua, ub, act, w, l):
    dact = _matmul(dffb, w["ffn_out"][l], mode="nt", out_dtype=BF16, name=f"ffn{l}_down_dx")
    g_out = _matmul(act, dffb, mode="tn", name=f"ffn{l}_down_dw", tm=1408, tn=1024, tk=512)
    dua, dub, dwa, dwb, dba, dbb = _conv_gate_bwd(ua, ub, w["conv_w_a"][l], w["conv_w_b"][l], w["conv_b_a"][l],
                                                  w["conv_b_b"][l], dact, name=f"ffn{l}_conv_gate_bwd")
    dh = _matmul(dua, w["ffn_in_a"][l], mode="nt", add=dh_scaled, add_scale=ALPHA, name=f"ffn{l}_up_a_dx", tn=1024)
    dh = _matmul(dub, w["ffn_in_b"][l], mode="nt", add=dh, name=f"ffn{l}_up_b_dx", tn=1024)
    g_in_a = _matmul(hb, dua, mode="tn", name=f"ffn{l}_up_a_dw", tm=1024, tn=1408, tk=512)
    g_in_b = _matmul(hb, dub, mode="tn", name=f"ffn{l}_up_b_dw", tm=1024, tn=1408, tk=512)
    return dh, dict(ffn_out=g_out, ffn_in_a=g_in_a, ffn_in_b=g_in_b, conv_w_a=dwa, conv_w_b=dwb, conv_b_a=dba, conv_b_b=dbb)


def _local_step(x, tgt, w, more_weights, emit):
    bucket = jnp.asarray(_bucket_index())
    xb = x.astype(BF16)

    pre = [_matmul(xb, w["hg_in"][j], mode="nn", name=f"hg_in_{j}") for j in range(4)]
    og, states = _hgrn_fwd(*pre, w["lb_logits"], w["gnorm"], name="hgrn_fwd")
    mix0 = _matmul(og, w["hg_out"], mode="nn", name="hg_out")
    h1, h1b = _ln_fwd(x, mix0, w["ln_mix_g"][0], w["ln_mix_b"][0], name="ln_mix0")
    w = {**w, **more_weights(1, h1b)}
    ua0, ub0, act0, ff0 = _ffn_fwd(h1b, w, 0)
    h2, h2b = _ln_fwd(h1, ff0, w["ln_ffn_g"][0], w["ln_ffn_b"][0], name="ln_ffn0")
    kv = _matmul(h2b, w["kv"], mode="nn", name="kv_proj")

    bias = _bias_from_table(w["rel_bias"], bucket, name="rel_bias_expand").reshape(SW_Q_HEADS, SW_WINDOW, 2 * SW_WINDOW)
    q1 = _matmul(h2b, w["sw_q"], mode="nn", name="sw_q")
    o1 = _attn_fwd(q1, kv, bias, w["sinks"], name="attn_fwd")
    mix1 = _matmul(o1, w["sw_out"], mode="nn", name="sw_out")
    h3, h3b = _ln_fwd(h2, mix1, w["ln_mix_g"][1], w["ln_mix_b"][1], name="ln_mix1")
    w = {**w, **more_weights(2, h3b)}
    ua1, ub1, act1, ff1 = _ffn_fwd(h3b, w, 1)
    y, _ = _ln_fwd(h3, ff1, w["ln_ffn_g"][1], w["ln_ffn_b"][1], name="ln_ffn1")

    dy, loss_tile = _loss_grad(y, tgt, name="loss_grad")

    g = {}
    dz, dzb, dg_, db_ = _ln_bwd(dy, h3, ff1, w["ln_ffn_g"][1], w["ln_ffn_b"][1], name="ln_ffn1_bwd")
    g["ln_ffn_g1"], g["ln_ffn_b1"] = dg_, db_
    dh3, gf1 = _ffn_bwd(dzb, dz, h3b, ua1, ub1, act1, w, 1)
    dz, dzb, dg_, db_ = _ln_bwd(dh3, h2, mix1, w["ln_mix_g"][1], w["ln_mix_b"][1], name="ln_mix1_bwd")
    g["ln_mix_g1"], g["ln_mix_b1"] = dg_, db_
    do1 = _matmul(dzb, w["sw_out"], mode="nt", out_dtype=BF16, name="sw_out_dx")
    g_sw_out = _matmul(o1, dzb, mode="tn", name="sw_out_dw", tm=1024, tn=1024, tk=512)
    dq1, dkv, dbias, dsinks = _attn_bwd(q1, kv, bias, w["sinks"], do1, name="attn_bwd")
    g["sinks"] = dsinks
    g["rel_bias"] = _table_grad(dbias.reshape(SW_Q_HEADS, BIAS_COLS), bucket, name="rel_bias_grad")
    dh2 = _matmul(dq1, w["sw_q"], mode="nt", add=dz, add_scale=ALPHA, name="sw_q_dx", tn=1024)
    dh2 = _matmul(dkv, w["kv"], mode="nt", add=dh2, name="kv_dx", tn=1024)
    g_sw_q = _matmul(h2b, dq1, mode="tn", name="sw_q_dw", tm=1024, tn=1024, tk=512)
    g_kv = _matmul(h2b, dkv, mode="tn", name="kv_dw", tm=1024, tn=512, tk=512)
    tok = emit(1, dict(sw_q=g_sw_q, sw_out=g_sw_out, kv=g_kv, ffn_in_a=gf1["ffn_in_a"], ffn_in_b=gf1["ffn_in_b"],
                       ffn_out=gf1["ffn_out"]))

    dz, dzb, dg_, db_ = _ln_bwd(dh2, h1, ff0, w["ln_ffn_g"][0], w["ln_ffn_b"][0], name="ln_ffn0_bwd", after=tok)
    g["ln_ffn_g0"], g["ln_ffn_b0"] = dg_, db_
    dh1, gf0 = _ffn_bwd(dzb, dz, h1b, ua0, ub0, act0, w, 0)
    dz, dzb, dg_, db_ = _ln_bwd(dh1, x, mix0, w["ln_mix_g"][0], w["ln_mix_b"][0], name="ln_mix0_bwd")
    g["ln_mix_g0"], g["ln_mix_b0"] = dg_, db_
    dog = _matmul(dzb, w["hg_out"], mode="nt", out_dtype=BF16, name="hg_out_dx")
    g_hg_out = _matmul(og, dzb, mode="tn", name="hg_out_dw", tm=1024, tn=1024, tk=512)
    tok = emit(2, dict(hg_out=g_hg_out, ffn_in_a=gf0["ffn_in_a"], ffn_in_b=gf0["ffn_in_b"], ffn_out=gf0["ffn_out"]))
    dpre = _hgrn_bwd(*pre, w["lb_logits"], w["gnorm"], states, dog, name="hgrn_bwd", after=tok)
    g["lb_logits"], g["gnorm"] = dpre[4], dpre[5]
    tok = emit(3, dict(hg_in=[_matmul(xb, dpre[j], mode="tn", name=f"hg_in_{j}_dw", tm=1024, tn=1024, tk=512)
                              for j in range(4)]))
    dx = dz
    for j in range(4):
        dx = _matmul(dpre[j], w["hg_in"][j], mode="nt", add=dx, add_scale=ALPHA if j == 0 else 1.0,
                     name=f"hg_in_{j}_dx", tn=1024, after=tok if j == 0 else None)
    g["conv"] = [{k: gf[k] for k in ("conv_w_a", "conv_w_b", "conv_b_a", "conv_b_b")} for gf in (gf0, gf1)]
    return loss_tile, dx, g


def _adamw(wt, ga, gb, m, v, *, name):
    R, Cc = wt.shape
    tr = _tile(R, 256, SUBLANES) if R % SUBLANES == 0 else R
    c1 = 1.0 - ADAM_B1 ** ADAM_STEP
    c2 = 1.0 - ADAM_B2 ** ADAM_STEP
    two = gb is not None

    def body(*refs):
        if two:
            w_ref, ga_ref, gb_ref, m_ref, v_ref, g_ref, d_ref, nm_ref, nv_ref = refs
            g_ = ga_ref[...] + gb_ref[...]
        else:
            w_ref, ga_ref, m_ref, v_ref, g_ref, d_ref, nm_ref, nv_ref = refs
            g_ = ga_ref[...]
        nm = ADAM_B1 * m_ref[...] + (1.0 - ADAM_B1) * g_
        nv = ADAM_B2 * v_ref[...] + (1.0 - ADAM_B2) * (g_ * g_)
        g_ref[...] = g_
        d_ref[...] = -ADAM_LR * ((nm / c1) / (jnp.sqrt(nv / c2) + ADAM_EPS) + ADAM_WD * w_ref[...])
        nm_ref[...] = nm
        nv_ref[...] = nv

    blk = pl.BlockSpec((tr, Cc), lambda i: (i, 0))
    args = (wt, ga, gb, m, v) if two else (wt, ga, m, v)
    return pl.pallas_call(
        body, name=name, grid=(R // tr,), in_specs=[blk] * len(args), out_specs=[blk] * 4,
        out_shape=[jax.ShapeDtypeStruct((R, Cc), F32)] * 4,
        compiler_params=_params(("parallel",)),
    )(*args)


HBM_SPEC = pl.BlockSpec(memory_space=pltpu.HBM)
SEM_SPEC = pl.BlockSpec(memory_space=pltpu.SEMAPHORE)
VMEM_SPEC = pl.BlockSpec(memory_space=pltpu.VMEM)
DATAFLOW = pltpu.SideEffectType.DATAFLOW_SIDE_EFFECTING


def _in_hbm(a):
    return pltpu.with_memory_space_constraint(a, pltpu.HBM)


def _place():
    return lax.axis_index("x"), lax.axis_index("y"), lax.axis_index("c")


def _other_chips(x, y):
    return [(1 - x, y), (x, 1 - y), (1 - x, 1 - y)]


def _sum8(v, *, name):
    r = v.shape[0]

    def body(v_ref, all_ref, o_ref, send_sems, recv_sems, local_sem):
        x, y, c = _place()
        me, sibling = (x, y, c), (x, y, 1 - c)
        chips = _other_chips(x, y)

        def rows(px, py, pc):
            return all_ref.at[pl.ds((4 * px + 2 * py + pc) * r, r), :]

        def copy(k, block, to, src=None):
            return pltpu.make_async_remote_copy(
                src_ref=rows(*block) if src is None else src, dst_ref=rows(*block),
                send_sem=send_sems.at[k], recv_sem=recv_sems.at[k], device_id=to, device_id_type=MESH)

        mine = pltpu.make_async_copy(v_ref, rows(*me), local_sem)
        mine.start()
        first = [copy(0, me, sibling, src=v_ref)]
        first += [copy(1 + j, me, (*chip, c), src=v_ref) for j, chip in enumerate(chips)]
        for cp in first:
            cp.start()
        passed = [copy(4 + j, (*chip, c), sibling) for j, chip in enumerate(chips)]
        for j, chip in enumerate(chips):
            copy(1 + j, (*chip, c), me).wait_recv()
            passed[j].start()
        copy(0, sibling, me).wait_recv()
        for j, chip in enumerate(chips):
            copy(4 + j, (*chip, 1 - c), me).wait_recv()
        for cp in first + passed:
            cp.wait_send()
        mine.wait()
        acc = all_ref[pl.ds(0, r), :]
        for d in range(1, N_DEV):
            acc = acc + all_ref[pl.ds(d * r, r), :]
        o_ref[...] = acc

    return pl.pallas_call(
        body, name=name, in_specs=[VMEM_SPEC], out_specs=[VMEM_SPEC, VMEM_SPEC],
        out_shape=[jax.ShapeDtypeStruct((N_DEV * r, LANES), F32), jax.ShapeDtypeStruct((r, LANES), F32)],
        scratch_shapes=[pltpu.SemaphoreType.DMA((7,)), pltpu.SemaphoreType.DMA((7,)), pltpu.SemaphoreType.DMA],
        compiler_params=pltpu.CompilerParams(vmem_limit_bytes=VMEM_LIMIT),
    )(v)[1]


def _gather_chips(shard, *, name):
    R, Cc = shard.shape
    half = R // 2
    assert half * 2 == R

    def body(s_ref, o_ref, send_sems, recv_sems, local_sem):
        x, y, c = _place()
        sibling = (x, y, 1 - c)
        chips = _other_chips(x, y)

        def part(px, py, pc):
            return o_ref.at[2 * px + py, pl.ds(pc * half, half), :]

        def copy(k, block, to, src=None):
            return pltpu.make_async_remote_copy(
                src_ref=part(*block) if src is None else src, dst_ref=part(*block),
                send_sem=send_sems.at[k], recv_sem=recv_sems.at[k], device_id=to, device_id_type=MESH)

        mine = pltpu.make_async_copy(s_ref, o_ref.at[2 * x + y], local_sem)
        mine.start()
        my_half = s_ref.at[pl.ds(c * half, half), :]
        first = [copy(j, (x, y, c), (*chip, c), src=my_half) for j, chip in enumerate(chips)]
        for cp in first:
            cp.start()
        passed = [copy(3 + j, (*chip, c), sibling) for j, chip in enumerate(chips)]
        for j, chip in enumerate(chips):
            copy(j, (*chip, c), (x, y, c)).wait_recv()
            passed[j].start()
        for j, chip in enumerate(chips):
            copy(3 + j, (*chip, 1 - c), (x, y, c)).wait_recv()
        for cp in first + passed:
            cp.wait_send()
        mine.wait()

    return pl.pallas_call(
        body, name=name, in_specs=[HBM_SPEC], out_specs=HBM_SPEC,
        out_shape=jax.ShapeDtypeStruct((N_CHIPS, R, Cc), shard.dtype),
        scratch_shapes=[pltpu.SemaphoreType.DMA((6,)), pltpu.SemaphoreType.DMA((6,)), pltpu.SemaphoreType.DMA],
    )(shard)


def _swap_sibling(v, *, name):
    def body(v_ref, o_ref, send_sem, recv_sem):
        x, y, c = _place()
        cp = pltpu.make_async_remote_copy(src_ref=v_ref, dst_ref=o_ref, send_sem=send_sem, recv_sem=recv_sem,
                                          device_id=(x, y, 1 - c), device_id_type=MESH)
        cp.start()
        cp.wait()

    return pl.pallas_call(
        body, name=name, in_specs=[HBM_SPEC], out_specs=HBM_SPEC, out_shape=jax.ShapeDtypeStruct(v.shape, v.dtype),
        scratch_shapes=[pltpu.SemaphoreType.DMA, pltpu.SemaphoreType.DMA],
    )(v)


def _gather_start(shards, after, *, name):
    n = len(shards)

    def body(*refs):
        src, land = refs[:n], refs[n:2 * n]
        send, recv = refs[2 * n + 1:3 * n + 1], refs[3 * n + 1:4 * n + 1]
        x, y, c = _place()
        for i in range(n):
            for k, (px, py) in enumerate(_other_chips(x, y)):
                pltpu.make_async_remote_copy(src_ref=src[i], dst_ref=land[i].at[2 * x + y], send_sem=send[i].at[k],
                                             recv_sem=recv[i].at[k], device_id=(px, py, c), device_id_type=MESH).start()

    lands = [lax.empty((N_CHIPS,) + s.shape, s.dtype) for s in shards]
    sems = [pltpu.SemaphoreType.DMA((3,))] * n
    out = pl.pallas_call(
        body, name=name,
        in_specs=[HBM_SPEC] * (2 * n) + [ANY_SPEC],
        out_specs=[SEM_SPEC] * (2 * n) + [HBM_SPEC] * (2 * n),
        out_shape=sems + sems + [pltpu.HBM(s.shape, s.dtype) for s in shards] + [pltpu.HBM(l.shape, l.dtype) for l in lands],
        input_output_aliases={i: 2 * n + i for i in range(2 * n)},
        compiler_params=pltpu.CompilerParams(has_side_effects=DATAFLOW),
    )(*[_in_hbm(s) for s in shards], *[_in_hbm(l) for l in lands], after)
    return [(out[i], out[n + i], out[2 * n + i], out[3 * n + i]) for i in range(n)]


def _gather_wait(handle, after, *, name):
    send_sems, recv_sems, src, land = handle

    def body(src_ref, land_ref, send_ref, recv_ref, after_ref, src_out, land_out):
        x, y, c = _place()
        for k, (px, py) in enumerate(_other_chips(x, y)):
            cp = pltpu.make_async_remote_copy(src_ref=src_ref, dst_ref=land_ref.at[2 * px + py], send_sem=send_ref.at[k],
                                              recv_sem=recv_ref.at[k], device_id=(px, py, c), device_id_type=MESH)
            cp.wait_send()
            cp.wait_recv()

    return pl.pallas_call(
        body, name=name, in_specs=[HBM_SPEC, HBM_SPEC, SEM_SPEC, SEM_SPEC, ANY_SPEC], out_specs=[HBM_SPEC, HBM_SPEC],
        out_shape=[pltpu.HBM(src.shape, src.dtype), pltpu.HBM(land.shape, land.dtype)],
        input_output_aliases={0: 0, 1: 1},
        compiler_params=pltpu.CompilerParams(has_side_effects=DATAFLOW),
    )(src, land, send_sems, recv_sems, after)[1]


def _scatter_start(pieces, *, name):
    _, R, Cc = pieces.shape

    def body(src, land, send, recv, src_out, land_out, token):
        x, y, c = _place()
        for k, (px, py) in enumerate(_other_chips(x, y)):
            pltpu.make_async_remote_copy(src_ref=src.at[2 * px + py], dst_ref=land.at[k], send_sem=send.at[k],
                                         recv_sem=recv.at[k], device_id=(px, py, c), device_id_type=MESH).start()
        token[...] = jnp.zeros_like(token)

    land = lax.empty((3, R, Cc), pieces.dtype)
    out = pl.pallas_call(
        body, name=name, in_specs=[HBM_SPEC, HBM_SPEC],
        out_specs=[SEM_SPEC, SEM_SPEC, HBM_SPEC, HBM_SPEC, VMEM_SPEC],
        out_shape=[pltpu.SemaphoreType.DMA((3,)), pltpu.SemaphoreType.DMA((3,)), pltpu.HBM(pieces.shape, pieces.dtype),
                   pltpu.HBM(land.shape, land.dtype), jax.ShapeDtypeStruct((SUBLANES, LANES), F32)],
        input_output_aliases={0: 2, 1: 3},
        compiler_params=pltpu.CompilerParams(has_side_effects=DATAFLOW),
    )(_in_hbm(pieces), _in_hbm(land))
    return out[:4], out[4]


def _scatter_wait(handle, after, *, name):
    send_sems, recv_sems, src, land = handle

    def body(src_ref, land_ref, send_ref, recv_ref, after_ref, src_out, land_out):
        x, y, c = _place()
        for k, (px, py) in enumerate(_other_chips(x, y)):
            cp = pltpu.make_async_remote_copy(src_ref=src_ref.at[2 * px + py], dst_ref=land_ref.at[k], send_sem=send_ref.at[k],
                                              recv_sem=recv_ref.at[k], device_id=(px, py, c), device_id_type=MESH)
            cp.wait_send()
            cp.wait_recv()

    return pl.pallas_call(
        body, name=name, in_specs=[HBM_SPEC, HBM_SPEC, SEM_SPEC, SEM_SPEC, ANY_SPEC], out_specs=[HBM_SPEC, HBM_SPEC],
        out_shape=[pltpu.HBM(src.shape, src.dtype), pltpu.HBM(land.shape, land.dtype)],
        input_output_aliases={0: 0, 1: 1},
        compiler_params=pltpu.CompilerParams(has_side_effects=DATAFLOW),
    )(src, land, send_sems, recv_sems, after)[1]


def _chip_sum(own, got, *, name):
    R, Cc = own.shape
    tr = _tile(R, 256, SUBLANES)

    def body(a_ref, g_ref, o_ref):
        o_ref[...] = ((a_ref[...] + g_ref[0].astype(F32)) + g_ref[1].astype(F32)) + g_ref[2].astype(F32)

    return pl.pallas_call(
        body, name=name, grid=(R // tr,),
        in_specs=[pl.BlockSpec((tr, Cc), lambda i: (i, 0)), pl.BlockSpec((3, tr, Cc), lambda i: (0, i, 0))],
        out_specs=pl.BlockSpec((tr, Cc), lambda i: (i, 0)),
        out_shape=jax.ShapeDtypeStruct((R, Cc), F32),
        compiler_params=_params(("parallel",)),
    )(own, got)


PACK_COLS = 1024


def _pack_rows(parts):
    return jnp.concatenate([p.reshape(-1, PACK_COLS) for p in parts], axis=0)


def _unpack_rows(block, shapes):
    lead = block.shape[:-2]
    out, off = [], 0
    for s in shapes:
        r = int(np.prod(s)) // PACK_COLS
        out.append(block[..., off:off + r, :].reshape(lead + tuple(s)))
        off += r
    assert off == block.shape[-2]
    return out


def _flat128(parts):
    out = []
    for p in parts:
        v = p.reshape(-1)
        pad = (-v.shape[0]) % LANES
        out.append(jnp.pad(v, (0, pad)) if pad else v)
    v = jnp.concatenate(out)
    pad = (-v.shape[0]) % (SUBLANES * LANES)
    if pad:
        v = jnp.pad(v, (0, pad))
    return v.reshape(-1, LANES)


def _unflat128(block, shapes):
    v = block.reshape(-1)
    out, off = [], 0
    for s in shapes:
        n = int(np.prod(s))
        out.append(v[off:off + n].reshape(s))
        off += n + ((-n) % LANES)
    return out


def kernel(x, hgrn_w_in, hgrn_lb_logits, hgrn_gnorm_w, hgrn_w_out, swa_w_q, swa_sinks, swa_w_out, shared_w_kv, rel_bias, ffn_w_in, ffn_conv_w, ffn_conv_b, ffn_w_out, ln_mix_g, ln_mix_b, ln_ffn_g, ln_ffn_b, loss_target, m_hgrn_w_in, m_hgrn_lb_logits, m_hgrn_gnorm_w, m_hgrn_w_out, m_swa_w_q, m_swa_sinks, m_swa_w_out, m_shared_w_kv, m_rel_bias, m_ffn_w_in, m_ffn_conv_w, m_ffn_conv_b, m_ffn_w_out, m_ln_mix_g, m_ln_mix_b, m_ln_ffn_g, m_ln_ffn_b, v_hgrn_w_in, v_hgrn_lb_logits, v_hgrn_gnorm_w, v_hgrn_w_out, v_swa_w_q, v_swa_sinks, v_swa_w_out, v_shared_w_kv, v_rel_bias, v_ffn_w_in, v_ffn_conv_w, v_ffn_conv_b, v_ffn_w_out, v_ln_mix_g, v_ln_mix_b, v_ln_ffn_g, v_ln_ffn_b):
    xi, yi, ci = _place()
    chip = 2 * xi + yi
    Dm = D_MODEL
    FC = 2 * FFN_DIM // N_CHIPS
    Fo = FFN_DIM // N_CHIPS
    Dq = Dm // N_CHIPS
    bf = lambda a: a.astype(BF16)

    shard0 = _pack_rows([bf(hgrn_w_in), bf(hgrn_w_out)])
    shard1 = _pack_rows([bf(swa_w_q), bf(swa_w_out), bf(shared_w_kv), bf(ffn_w_in[0]), bf(ffn_w_out[0])])
    shard2 = _pack_rows([bf(ffn_w_in[1]), bf(ffn_w_out[1])])
    all0 = _gather_chips(shard0, name="gather_w0")
    handles = _gather_start([shard1, shard2], all0, name="gather_w12_start")
    w_in, w_hg_out = _unpack_rows(all0, [(Dm, Dm), (Dq, Dm)])

    def ffn_weights(w_fi, w_fo, l):
        return {"ffn_in_a": {l: jnp.concatenate([w_fi[0], w_fi[1]], axis=1)},
                "ffn_in_b": {l: jnp.concatenate([w_fi[2], w_fi[3]], axis=1)},
                "ffn_out": {l: w_fo.reshape(FFN_DIM, Dm)}}

    got = {}

    def more_weights(k, after):
        shard = (shard1, shard2)[k - 1]
        land = _gather_wait(handles[k - 1], after, name=f"gather_w{k}_wait")
        allk = lax.dynamic_update_slice(land, shard[None], (chip, 0, 0))
        if k == 1:
            w_q, w_o, w_kv, w_fi, w_fo = _unpack_rows(allk, [(Dq, Dm), (Dq, Dm), (Dq, 2 * KV_DIM), (Dm, FC), (Fo, Dm)])
            got.update(ffn_weights(w_fi, w_fo, 0))
            return {"sw_q": w_q.reshape(Dm, Dm), "sw_out": w_o.reshape(Dm, Dm), "kv": w_kv.reshape(Dm, 2 * KV_DIM), **got}
        w_fi, w_fo = _unpack_rows(allk, [(Dm, FC), (Fo, Dm)])
        new = ffn_weights(w_fi, w_fo, 1)
        return {n: {**got[n], **new[n]} for n in new}

    lb_full = lax.dynamic_update_slice(jnp.zeros((2, Dm), F32), hgrn_lb_logits, (0, chip * Dq))
    cw_full = lax.dynamic_update_slice(jnp.zeros((DEPTH, 3, 2 * FFN_DIM), F32), ffn_conv_w, (0, 0, chip * FC))
    only_south = (ci == 0).astype(F32)
    small_in = _sum8(_flat128([lb_full, cw_full]) * only_south, name="gather_small")
    lb_full, cw_full = _unflat128(small_in, [(2, Dm), (DEPTH, 3, 2 * FFN_DIM)])
    w = {
        "hg_in": [w_in[j] for j in range(4)], "hg_out": w_hg_out.reshape(Dm, Dm),
        "lb_logits": lb_full, "gnorm": hgrn_gnorm_w, "sinks": swa_sinks, "rel_bias": rel_bias,
        "conv_w_a": [cw_full[l, :, :FFN_DIM] for l in range(DEPTH)],
        "conv_w_b": [cw_full[l, :, FFN_DIM:] for l in range(DEPTH)],
        "conv_b_a": [ffn_conv_b[l:l + 1, :FFN_DIM] for l in range(DEPTH)],
        "conv_b_b": [ffn_conv_b[l:l + 1, FFN_DIM:] for l in range(DEPTH)],
        "ln_mix_g": [ln_mix_g[l:l + 1] for l in range(DEPTH)], "ln_mix_b": [ln_mix_b[l:l + 1] for l in range(DEPTH)],
        "ln_ffn_g": [ln_ffn_g[l:l + 1] for l in range(DEPTH)], "ln_ffn_b": [ln_ffn_b[l:l + 1] for l in range(DEPTH)],
    }

    sent = {}

    def ffn_blocks(gd, j):
        fi = gd["ffn_in_a"] if j < 2 else gd["ffn_in_b"]
        return [fi[:, (j % 2) * FC:(j % 2 + 1) * FC], gd["ffn_out"][j * Fo:(j + 1) * Fo]]

    def emit(k, gd):
        if k == 1:
            blocks = [[gd["sw_q"][j * Dq:(j + 1) * Dq], gd["sw_out"][j * Dq:(j + 1) * Dq], gd["kv"][j * Dq:(j + 1) * Dq]]
                      + ffn_blocks(gd, j) for j in range(N_CHIPS)]
        elif k == 2:
            blocks = [ffn_blocks(gd, j) + [gd["hg_out"][j * Dq:(j + 1) * Dq]] for j in range(N_CHIPS)]
        else:
            blocks = [[gd["hg_in"][j]] for j in range(N_CHIPS)]
        pieces = jnp.stack([_pack_rows(b) for b in blocks])
        own = lax.dynamic_index_in_dim(pieces, chip, axis=0, keepdims=False)
        handle, token = _scatter_start(pieces.astype(BF16), name=f"scatter_g{k}_start")
        sent[k] = (handle, own)
        return token

    loss_tile, grad_x, g = _local_step(x[0], loss_target[0], w, more_weights, emit)

    wts = dict(hgrn_w_in=hgrn_w_in, hgrn_lb_logits=hgrn_lb_logits, hgrn_gnorm_w=hgrn_gnorm_w, hgrn_w_out=hgrn_w_out,
               swa_w_q=swa_w_q, swa_sinks=swa_sinks, swa_w_out=swa_w_out, shared_w_kv=shared_w_kv, rel_bias=rel_bias,
               ffn_w_in=ffn_w_in, ffn_conv_w=ffn_conv_w, ffn_conv_b=ffn_conv_b, ffn_w_out=ffn_w_out,
               ln_mix_g=ln_mix_g, ln_mix_b=ln_mix_b, ln_ffn_g=ln_ffn_g, ln_ffn_b=ln_ffn_b)
    ms = dict(hgrn_w_in=m_hgrn_w_in, hgrn_lb_logits=m_hgrn_lb_logits, hgrn_gnorm_w=m_hgrn_gnorm_w, hgrn_w_out=m_hgrn_w_out,
              swa_w_q=m_swa_w_q, swa_sinks=m_swa_sinks, swa_w_out=m_swa_w_out, shared_w_kv=m_shared_w_kv, rel_bias=m_rel_bias,
              ffn_w_in=m_ffn_w_in, ffn_conv_w=m_ffn_conv_w, ffn_conv_b=m_ffn_conv_b, ffn_w_out=m_ffn_w_out,
              ln_mix_g=m_ln_mix_g, ln_mix_b=m_ln_mix_b, ln_ffn_g=m_ln_ffn_g, ln_ffn_b=m_ln_ffn_b)
    vs = dict(hgrn_w_in=v_hgrn_w_in, hgrn_lb_logits=v_hgrn_lb_logits, hgrn_gnorm_w=v_hgrn_gnorm_w, hgrn_w_out=v_hgrn_w_out,
              swa_w_q=v_swa_w_q, swa_sinks=v_swa_sinks, swa_w_out=v_swa_w_out, shared_w_kv=v_shared_w_kv, rel_bias=v_rel_bias,
              ffn_w_in=v_ffn_w_in, ffn_conv_w=v_ffn_conv_w, ffn_conv_b=v_ffn_conv_b, ffn_w_out=v_ffn_w_out,
              ln_mix_g=v_ln_mix_g, ln_mix_b=v_ln_mix_b, ln_ffn_g=v_ln_ffn_g, ln_ffn_b=v_ln_ffn_b)
    names = list(wts)
    grads, delta, new_m, new_v = {}, {}, {}, {}

    def update(n, ga, gb, sl=None):
        pick = (lambda a: a) if sl is None else (lambda a: a[sl])
        shp = pick(wts[n]).shape
        two_d = (-1, shp[-1])
        r2 = lambda a: a.reshape(two_d)
        res = _adamw(r2(pick(wts[n])), r2(ga), r2(gb), r2(pick(ms[n])), r2(pick(vs[n])),
                     name=f"adamw_{n}" + ("" if sl is None else f"_{sl}"))
        return [a.reshape(shp) for a in res]

    after = grad_x
    layer_parts = {}
    for k in (1, 2, 3):
        handle, own = sent[k]
        got3 = _scatter_wait(handle, after, name=f"scatter_g{k}_wait")
        part = _chip_sum(own, got3, name=f"scatter_g{k}_sum")
        part_sib = _swap_sibling(part, name=f"scatter_g{k}_swap")
        if k == 1:
            shapes = [(1, Dq, Dm), (1, Dq, Dm), (Dq, 2 * KV_DIM), (Dm, FC), (Fo, Dm)]
            a = _unpack_rows(part, shapes)
            b = _unpack_rows(part_sib, shapes)
            for n, ga, gb in zip(["swa_w_q", "swa_w_out", "shared_w_kv"], a[:3], b[:3]):
                grads[n], delta[n], new_m[n], new_v[n] = update(n, ga, gb)
            layer_parts[1] = [update("ffn_w_in", a[3], b[3], sl=1), update("ffn_w_out", a[4], b[4], sl=1)]
            after = layer_parts[1][1][3]
        elif k == 2:
            shapes = [(Dm, FC), (Fo, Dm), (1, Dq, Dm)]
            a = _unpack_rows(part, shapes)
            b = _unpack_rows(part_sib, shapes)
            layer_parts[0] = [update("ffn_w_in", a[0], b[0], sl=0), update("ffn_w_out", a[1], b[1], sl=0)]
            n = "hgrn_w_out"
            grads[n], delta[n], new_m[n], new_v[n] = update(n, a[2], b[2])
            after = new_v[n]
        else:
            n = "hgrn_w_in"
            grads[n], delta[n], new_m[n], new_v[n] = update(n, part.reshape(1, Dm, Dm), part_sib.reshape(1, Dm, Dm))
    for i, n in enumerate(["ffn_w_in", "ffn_w_out"]):
        grads[n], delta[n], new_m[n], new_v[n] = [jnp.stack([layer_parts[0][i][t], layer_parts[1][i][t]]) for t in range(4)]

    small_shapes = [(SUBLANES, LANES), (2, Dm), (1, HG_DIM), (1, SW_Q_HEADS), (REL_BUCKETS, SW_Q_HEADS),
                    (DEPTH, 3, 2 * FFN_DIM), (DEPTH, 2 * FFN_DIM)] + [(DEPTH, Dm)] * 4
    gc = g["conv"]
    conv_w_g = jnp.stack([jnp.concatenate([gc[l]["conv_w_a"], gc[l]["conv_w_b"]], axis=1) for l in range(DEPTH)])
    conv_b_g = jnp.concatenate([jnp.concatenate([gc[l]["conv_b_a"], gc[l]["conv_b_b"]], axis=1) for l in range(DEPTH)], axis=0)
    ln_g = [jnp.concatenate([g[f"{n}0"], g[f"{n}1"]], axis=0) for n in ("ln_mix_g", "ln_mix_b", "ln_ffn_g", "ln_ffn_b")]
    small_out = _sum8(_flat128([loss_tile, g["lb_logits"], g["gnorm"], g["sinks"], g["rel_bias"], conv_w_g, conv_b_g] + ln_g),
                      name="sum_small")
    (loss_t, g_lb, g_gn, g_sinks, g_rel, g_cw, g_cb, g_lmg, g_lmb, g_lfg, g_lfb) = _unflat128(small_out, small_shapes)
    loss = loss_t[0, 0]
    g_lb = lax.dynamic_slice_in_dim(g_lb, chip * Dq, Dq, axis=1)
    g_cw = lax.dynamic_slice_in_dim(g_cw, chip * FC, FC, axis=2)
    small_g = dict(hgrn_lb_logits=g_lb, hgrn_gnorm_w=g_gn, swa_sinks=g_sinks, rel_bias=g_rel, ffn_conv_w=g_cw,
                   ffn_conv_b=g_cb, ln_mix_g=g_lmg, ln_mix_b=g_lmb, ln_ffn_g=g_lfg, ln_ffn_b=g_lfb)
    small_names = list(small_g)
    sshapes = [wts[n].shape for n in small_names]
    _, d_, m_, v_ = _adamw(_flat128([wts[n] for n in small_names]), _flat128([small_g[n] for n in small_names]), None,
                           _flat128([ms[n] for n in small_names]), _flat128([vs[n] for n in small_names]), name="adamw_small")
    for n, a, b_, c_ in zip(small_names, _unflat128(d_, sshapes), _unflat128(m_, sshapes), _unflat128(v_, sshapes)):
        grads[n], delta[n], new_m[n], new_v[n] = small_g[n], a, b_, c_

    return (loss, grad_x[None], *[grads[n] for n in names], *[delta[n] for n in names],
            *[new_m[n] for n in names], *[new_v[n] for n in names])
```

```python
import functools
import math

import numpy as np
import jax
import jax.numpy as jnp
from jax import lax
from jax.experimental import pallas as pl
from jax.experimental.pallas import tpu as pltpu

F32 = jnp.float32
BF16 = jnp.bfloat16
MESH = pl.DeviceIdType.MESH

D_MODEL = 1024
DEPTH = 2
HG_HEADS = 8
HG_DIM = 128
SW_Q_HEADS = 16
SW_KV_HEADS = 4
SW_HEAD_DIM = 64
SW_GROUP = 4
SW_WINDOW = 128
REL_BUCKETS = 32
REL_MAX_DIST = 128
FFN_DIM = 2816
ALPHA = (2.0 * DEPTH) ** 0.25
LN_EPS = 1e-5
RMS_EPS = 1e-6
ADAM_LR = 0.001
ADAM_B1 = 0.9
ADAM_B2 = 0.999
ADAM_EPS = 1e-08
ADAM_WD = 0.01
ADAM_STEP = 10

VMEM_BYTES_V7X = 64 * 1024 * 1024
VMEM_LIMIT = VMEM_BYTES_V7X - 8 * 1024 * 1024
LANES = 128
SUBLANES = 8

HG_C = 64
HG_RB = 256
ROW_TILE = 256
CONV_R = 256
N_CHIPS = 4
N_DEV = 8

ANY_SPEC = pl.BlockSpec(memory_space=pl.ANY)


def _after(body, n_in, after):
    if after is None:
        return body, [], ()

    def wrapped(*refs):
        return body(*refs[:n_in], *refs[n_in + 1:])

    return wrapped, [ANY_SPEC], (after,)


def _params(sem=None):
    return pltpu.CompilerParams(dimension_semantics=sem, vmem_limit_bytes=VMEM_LIMIT)


def _tile(n, pref, unit=LANES):
    if n <= pref:
        return n
    best = None
    for t in range(unit, pref + 1, unit):
        if n % t == 0:
            best = t
    assert best is not None, (n, pref, unit)
    return best


def _dot(a, b, ca, cb):
    nb = a.ndim - 2
    batch = tuple(range(nb))
    return lax.dot_general(a.astype(BF16), b.astype(BF16), (((nb + ca,), (nb + cb,)), (batch, batch)),
                           preferred_element_type=F32)


@jax.custom_vjp
def mm(a, b):
    return _dot(a, b, 1, 0)


@jax.custom_vjp
def mm_nt(a, b):
    return _dot(a, b, 1, 1)


@jax.custom_vjp
def mm_tn(a, b):
    return _dot(a, b, 0, 0)


mm.defvjp(lambda a, b: (mm(a, b), (a, b)), lambda r, ct: (mm_nt(ct, r[1]), mm_tn(r[0], ct)))
mm_nt.defvjp(lambda a, b: (mm_nt(a, b), (a, b)), lambda r, ct: (mm(ct, r[1]), mm_tn(ct, r[0])))
mm_tn.defvjp(lambda a, b: (mm_tn(a, b), (a, b)), lambda r, ct: (mm_nt(r[1], ct), mm(r[0], ct)))


def _split2(x):
    hi = x.astype(BF16)
    return hi, (x - hi.astype(F32)).astype(BF16)


@jax.custom_vjp
def _scores(qt, kt):
    return _dot(qt, kt, 1, 1)


def _scores_bwd(r, ct):
    (qh, ql), (kh, kl) = _split2(r[0]), _split2(r[1])
    return _dot(ct, kh, 1, 0) + _dot(ct, kl, 1, 0), _dot(ct, qh, 0, 0) + _dot(ct, ql, 0, 0)


_scores.defvjp(lambda a, b: (_scores(a, b), (a, b)), _scores_bwd)


def _split3(x):
    hi = x.astype(BF16)
    r1 = x - hi.astype(F32)
    mid = r1.astype(BF16)
    lo = (r1 - mid.astype(F32)).astype(BF16)
    return hi, mid, lo


def _cumsum_impl(x):
    ax = x.ndim - 2
    n = x.shape[ax]
    row = lax.broadcasted_iota(jnp.int32, x.shape, ax)
    d = 1
    while d < n:
        x = x + jnp.where(row >= d, pltpu.roll(x, d, ax), 0.0)
        d *= 2
    return x


def _cumsum_rev_impl(x):
    ax = x.ndim - 2
    n = x.shape[ax]
    row = lax.broadcasted_iota(jnp.int32, x.shape, ax)
    d = 1
    while d < n:
        x = x + jnp.where(row < n - d, pltpu.roll(x, n - d, ax), 0.0)
        d *= 2
    return x


@jax.custom_vjp
def _cumsum(x):
    return _cumsum_impl(x)


_cumsum.defvjp(lambda x: (_cumsum_impl(x), None), lambda _, ct: (_cumsum_rev_impl(ct),))


def _matmul(a, b, *, mode, name, out_dtype=F32, add=None, add_scale=1.0, tm=512, tn=1408, tk=1408, after=None):
    if mode == "nn":
        (M, K), (K2, N) = a.shape, b.shape
    elif mode == "nt":
        (M, K), (N, K2) = a.shape, b.shape
    else:
        (K, M), (K2, N) = a.shape, b.shape
    assert K == K2, (a.shape, b.shape, mode)
    tm, tn, tk = _tile(M, tm), _tile(N, tn), _tile(K, tk)
    nk = K // tk
    ca, cb = {"nn": (1, 0), "nt": (1, 1), "tn": (0, 0)}[mode]
    a_spec = {"nn": pl.BlockSpec((tm, tk), lambda i, j, k: (i, k)),
              "nt": pl.BlockSpec((tm, tk), lambda i, j, k: (i, k)),
              "tn": pl.BlockSpec((tk, tm), lambda i, j, k: (k, i))}[mode]
    b_spec = {"nn": pl.BlockSpec((tk, tn), lambda i, j, k: (k, j)),
              "nt": pl.BlockSpec((tn, tk), lambda i, j, k: (j, k)),
              "tn": pl.BlockSpec((tk, tn), lambda i, j, k: (k, j))}[mode]
    o_spec = pl.BlockSpec((tm, tn), lambda i, j, k: (i, j))
    has_add = add is not None

    def finish(r, add_ref, o_ref):
        if has_add:
            r = r + add_scale * add_ref[...]
        o_ref[...] = r.astype(out_dtype)

    def body(*refs):
        a_ref, b_ref = refs[:2]
        add_ref = refs[2] if has_add else None
        o_ref = refs[3 if has_add else 2]
        if nk == 1:
            finish(_dot(a_ref[...], b_ref[...], ca, cb), add_ref, o_ref)
            return
        acc_ref = refs[-1]
        k = pl.program_id(2)

        @pl.when(k == 0)
        def _():
            acc_ref[...] = jnp.zeros_like(acc_ref)

        acc_ref[...] += _dot(a_ref[...], b_ref[...], ca, cb)

        @pl.when(k == nk - 1)
        def _():
            finish(acc_ref[...], add_ref, o_ref)

    in_specs = [a_spec, b_spec] + ([o_spec] if has_add else [])
    args = (a, b) + ((add,) if has_add else ())
    body, xs, xa = _after(body, len(args), after)
    in_specs, args = in_specs + xs, args + xa
    return pl.pallas_call(
        body, name=name, grid=(M // tm, N // tn, nk), in_specs=in_specs, out_specs=o_spec,
        out_shape=jax.ShapeDtypeStruct((M, N), out_dtype),
        scratch_shapes=[pltpu.VMEM((tm, tn), F32)] if nk > 1 else [],
        compiler_params=_params(("parallel", "parallel", "arbitrary")),
    )(*args)


def _ln(z, g, b):
    mu = jnp.mean(z, axis=-1, keepdims=True)
    zc = z - mu
    var = jnp.mean(zc * zc, axis=-1, keepdims=True)
    return zc * lax.rsqrt(var + LN_EPS) * g + b


def _ln_fwd(h, s, g, b, *, name):
    T, Dm = h.shape
    tr = _tile(T, ROW_TILE, SUBLANES)

    def body(h_ref, s_ref, g_ref, b_ref, y_ref, yb_ref):
        y = _ln(ALPHA * h_ref[...] + s_ref[...], g_ref[...], b_ref[...])
        y_ref[...] = y
        yb_ref[...] = y.astype(BF16)

    row = pl.BlockSpec((tr, Dm), lambda i: (i, 0))
    vec = pl.BlockSpec((1, Dm), lambda i: (0, 0))
    return pl.pallas_call(
        body, name=name, grid=(T // tr,), in_specs=[row, row, vec, vec], out_specs=[row, row],
        out_shape=[jax.ShapeDtypeStruct((T, Dm), F32), jax.ShapeDtypeStruct((T, Dm), BF16)],
        compiler_params=_params(("parallel",)),
    )(h, s, g, b)


def _ln_bwd(dy, h, s, g, b, *, name, after=None):
    T, Dm = h.shape
    tr = _tile(T, ROW_TILE, SUBLANES)

    def body(dy_ref, h_ref, s_ref, g_ref, b_ref, dz_ref, dzb_ref, dg_ref, db_ref):
        @pl.when(pl.program_id(0) == 0)
        def _():
            dg_ref[...] = jnp.zeros_like(dg_ref)
            db_ref[...] = jnp.zeros_like(db_ref)

        z = ALPHA * h_ref[...] + s_ref[...]
        _, vjp = jax.vjp(_ln, z, g_ref[...], b_ref[...])
        dz, dg, db = vjp(dy_ref[...])
        dz_ref[...] = dz
        dzb_ref[...] = dz.astype(BF16)
        dg_ref[...] += dg
        db_ref[...] += db

    row = pl.BlockSpec((tr, Dm), lambda i: (i, 0))
    vec = pl.BlockSpec((1, Dm), lambda i: (0, 0))
    body, xs, xa = _after(body, 5, after)
    return pl.pallas_call(
        body, name=name, grid=(T // tr,), in_specs=[row, row, row, vec, vec] + xs,
        out_specs=[row, row, vec, vec],
        out_shape=[jax.ShapeDtypeStruct((T, Dm), F32), jax.ShapeDtypeStruct((T, Dm), BF16),
                   jax.ShapeDtypeStruct((1, Dm), F32), jax.ShapeDtypeStruct((1, Dm), F32)],
        compiler_params=_params(("arbitrary",)),
    )(dy, h, s, g, b, *xa)


def _loss_grad(y, tgt, *, name):
    T, Dm = y.shape
    tr = _tile(T, ROW_TILE, SUBLANES)

    def body(y_ref, t_ref, dy_ref, l_ref):
        @pl.when(pl.program_id(0) == 0)
        def _():
            l_ref[...] = jnp.zeros_like(l_ref)

        e = y_ref[...] - t_ref[...]
        dy_ref[...] = e * (1.0 / Dm)
        l_ref[...] += 0.5 * jnp.sum(jnp.mean(e * e, axis=-1, keepdims=True), axis=0, keepdims=True)

    row = pl.BlockSpec((tr, Dm), lambda i: (i, 0))
    lsp = pl.BlockSpec((SUBLANES, LANES), lambda i: (0, 0))
    return pl.pallas_call(
        body, name=name, grid=(T // tr,), in_specs=[row, row], out_specs=[row, lsp],
        out_shape=[jax.ShapeDtypeStruct((T, Dm), F32), jax.ShapeDtypeStruct((SUBLANES, LANES), F32)],
        compiler_params=_params(("arbitrary",)),
    )(y, tgt)


def _hg_chunk(qr, fr, ir, gr, l0, l1, gw, st):
    C = qr.shape[-2]
    row = lax.broadcasted_iota(jnp.int32, qr.shape, qr.ndim - 2)
    lb = jax.nn.sigmoid(l0 - l1)
    fg = lb + (1.0 - lb) * jax.nn.sigmoid(fr)
    b = _cumsum(jnp.log(fg))
    q = jax.nn.silu(qr)
    k = 1.0 - fg
    bmid = lax.stop_gradient(jnp.sum(jnp.where(row == C // 2 - 1, b, 0.0), axis=-2, keepdims=True))
    bl = jnp.sum(jnp.where(row == C - 1, b, 0.0), axis=-2, keepdims=True)
    o = mm_nt(q * jnp.exp(b), st)
    sc = _scores(q * jnp.exp(b - bmid), k * jnp.exp(bmid - b))
    ti = lax.broadcasted_iota(jnp.int32, (C, C), 0)
    si = lax.broadcasted_iota(jnp.int32, (C, C), 1)
    sc = jnp.where(si <= ti, sc, 0.0)
    o = o + mm(sc, ir)
    st_new = st * jnp.exp(bl) + mm_tn(ir, k * jnp.exp(bl - b))
    on = o * lax.rsqrt(jnp.mean(o * o, axis=-1, keepdims=True) + RMS_EPS)
    return on * gw * jax.nn.silu(gr), st_new


def _heads(ref, rows):
    return jnp.stack([ref[rows, h * HG_DIM:(h + 1) * HG_DIM] for h in range(HG_HEADS)])


def _unheads(x):
    return jnp.concatenate([x[h] for h in range(HG_HEADS)], axis=-1)


def _hgrn_fwd(q, f, i, g, lbl, gw, *, name):
    T, Dm = q.shape
    rb = min(HG_RB, T)
    C = min(HG_C, rb)
    ncb = rb // C

    def body(q_ref, f_ref, i_ref, g_ref, lbl_ref, gw_ref, o_ref, st_ref, s_ref):
        @pl.when(pl.program_id(0) == 0)
        def _():
            s_ref[...] = jnp.zeros_like(s_ref)

        def chunk(ci, carry):
            r0 = pl.multiple_of(ci * C, C)
            rows = pl.ds(r0, C)
            st = s_ref[...]
            st_ref[ci] = st
            out, st_new = _hg_chunk(_heads(q_ref, rows), _heads(f_ref, rows), _heads(i_ref, rows), _heads(g_ref, rows),
                                    _heads(lbl_ref, slice(0, 1)), _heads(lbl_ref, slice(1, 2)), gw_ref[...], st)
            o_ref[rows, :] = _unheads(out).astype(BF16)
            s_ref[...] = st_new
            return carry

        lax.fori_loop(0, ncb, chunk, 0)

    row = pl.BlockSpec((rb, Dm), lambda n: (n, 0))
    return pl.pallas_call(
        body, name=name, grid=(T // rb,),
        in_specs=[row, row, row, row, pl.BlockSpec((2, Dm), lambda n: (0, 0)), pl.BlockSpec((1, HG_DIM), lambda n: (0, 0))],
        out_specs=[row, pl.BlockSpec((ncb, HG_HEADS, HG_DIM, HG_DIM), lambda n: (n, 0, 0, 0))],
        out_shape=[jax.ShapeDtypeStruct((T, Dm), BF16),
                   jax.ShapeDtypeStruct((T // C, HG_HEADS, HG_DIM, HG_DIM), F32)],
        scratch_shapes=[pltpu.VMEM((HG_HEADS, HG_DIM, HG_DIM), F32)],
        compiler_params=_params(("arbitrary",)),
    )(q, f, i, g, lbl, gw)


def _hgrn_bwd(q, f, i, g, lbl, gw, states, dout, *, name, after=None):
    T, Dm = q.shape
    rb = min(HG_RB, T)
    C = min(HG_C, rb)
    ncb = rb // C
    nb = T // rb

    def body(q_ref, f_ref, i_ref, g_ref, lbl_ref, gw_ref, st_ref, do_ref,
             dq_ref, df_ref, di_ref, dg_ref, dlbl_ref, dgw_ref, ds_ref):
        @pl.when(pl.program_id(0) == 0)
        def _():
            ds_ref[...] = jnp.zeros_like(ds_ref)
            dlbl_ref[...] = jnp.zeros_like(dlbl_ref)
            dgw_ref[...] = jnp.zeros_like(dgw_ref)

        def chunk(cj, carry):
            ci = ncb - 1 - cj
            r0 = pl.multiple_of(ci * C, C)
            rows = pl.ds(r0, C)
            _, vjp = jax.vjp(_hg_chunk, _heads(q_ref, rows), _heads(f_ref, rows), _heads(i_ref, rows), _heads(g_ref, rows),
                             _heads(lbl_ref, slice(0, 1)), _heads(lbl_ref, slice(1, 2)), gw_ref[...], st_ref[ci])
            dq, df, di, dg, dl0, dl1, dgw, dst = vjp((_heads(do_ref, rows).astype(F32), ds_ref[...]))
            dq_ref[rows, :] = _unheads(dq).astype(BF16)
            df_ref[rows, :] = _unheads(df).astype(BF16)
            di_ref[rows, :] = _unheads(di).astype(BF16)
            dg_ref[rows, :] = _unheads(dg).astype(BF16)
            dlbl_ref[0:1, :] += _unheads(dl0)
            dlbl_ref[1:2, :] += _unheads(dl1)
            dgw_ref[...] += dgw
            ds_ref[...] = dst
            return carry

        lax.fori_loop(0, ncb, chunk, 0)

    row = pl.BlockSpec((rb, Dm), lambda n: (nb - 1 - n, 0))
    lsp = pl.BlockSpec((2, Dm), lambda n: (0, 0))
    gsp = pl.BlockSpec((1, HG_DIM), lambda n: (0, 0))
    body, xs, xa = _after(body, 8, after)
    return pl.pallas_call(
        body, name=name, grid=(nb,),
        in_specs=[row, row, row, row, lsp, gsp,
                  pl.BlockSpec((ncb, HG_HEADS, HG_DIM, HG_DIM), lambda n: (nb - 1 - n, 0, 0, 0)), row] + xs,
        out_specs=[row, row, row, row, lsp, gsp],
        out_shape=[jax.ShapeDtypeStruct((T, Dm), BF16)] * 4
        + [jax.ShapeDtypeStruct((2, Dm), F32), jax.ShapeDtypeStruct((1, HG_DIM), F32)],
        scratch_shapes=[pltpu.VMEM((HG_HEADS, HG_DIM, HG_DIM), F32)],
        compiler_params=_params(("arbitrary",)),
    )(q, f, i, g, lbl, gw, states, dout, *xa)


def _shift_down(cur, prev):
    h = prev.shape[0]
    big = jnp.concatenate([prev, cur], axis=0)
    return pltpu.roll(big, 1, 0)[h:], pltpu.roll(big, 2, 0)[h:]


def _shift_up(cur, next8):
    n = cur.shape[0] + SUBLANES
    big = jnp.concatenate([cur, next8], axis=0)
    return pltpu.roll(big, n - 1, 0)[:cur.shape[0]], pltpu.roll(big, n - 2, 0)[:cur.shape[0]]


def _conv_rows(u_ref, w, bias, r0, R):
    halo = 2 * SUBLANES
    cur = u_ref[pl.ds(r0, R), :].astype(F32)
    p0 = pl.multiple_of(jnp.maximum(r0 - halo, 0), halo)
    prev = jnp.where(r0 > 0, u_ref[pl.ds(p0, halo), :].astype(F32), 0.0)
    s1, s2 = _shift_down(cur, prev)
    return w[0:1, :] * s2 + w[1:2, :] * s1 + w[2:3, :] * cur + bias, cur, s1, s2


def _conv_gate_fwd(ua, ub, wa, wb, ba, bb, *, name):
    T, Fd = ua.shape
    R = min(CONV_R, T)
    tc = LANES

    def body(ua_ref, ub_ref, wa_ref, wb_ref, ba_ref, bb_ref, o_ref):
        wa_, wb_, ba_, bb_ = wa_ref[...], wb_ref[...], ba_ref[...], bb_ref[...]

        def step(ri, carry):
            r0 = pl.multiple_of(ri * R, R)
            ca = _conv_rows(ua_ref, wa_, ba_, r0, R)[0]
            cb = _conv_rows(ub_ref, wb_, bb_, r0, R)[0]
            o_ref[pl.ds(r0, R), :] = (jax.nn.silu(ca) * cb).astype(BF16)
            return carry

        lax.fori_loop(0, T // R, step, 0)

    col = pl.BlockSpec((T, tc), lambda j: (0, j))
    wsp = pl.BlockSpec((3, tc), lambda j: (0, j))
    bsp = pl.BlockSpec((1, tc), lambda j: (0, j))
    return pl.pallas_call(
        body, name=name, grid=(Fd // tc,), in_specs=[col, col, wsp, wsp, bsp, bsp], out_specs=col,
        out_shape=jax.ShapeDtypeStruct((T, Fd), BF16),
        compiler_params=_params(("parallel",)),
    )(ua, ub, wa, wb, ba, bb)


def _conv_gate_bwd(ua, ub, wa, wb, ba, bb, dact, *, name):
    T, Fd = ua.shape
    R = min(CONV_R, T)
    nr = T // R
    tc = LANES

    def body(ua_ref, ub_ref, wa_ref, wb_ref, ba_ref, bb_ref, da_ref,
             dua_ref, dub_ref, dwa_ref, dwb_ref, dba_ref, dbb_ref, dca_ref, dcb_ref):
        wa_, wb_, ba_, bb_ = wa_ref[...], wb_ref[...], ba_ref[...], bb_ref[...]

        def taps(dc, cur, s1, s2):
            return jnp.concatenate([jnp.sum(dc * s2, axis=0, keepdims=True), jnp.sum(dc * s1, axis=0, keepdims=True),
                                    jnp.sum(dc * cur, axis=0, keepdims=True)], axis=0)

        def first(ri, carry):
            dwa, dwb, dba, dbb = carry
            r0 = pl.multiple_of(ri * R, R)
            ca, cura, s1a, s2a = _conv_rows(ua_ref, wa_, ba_, r0, R)
            cb, curb, s1b, s2b = _conv_rows(ub_ref, wb_, bb_, r0, R)
            dact_ = da_ref[pl.ds(r0, R), :].astype(F32)
            sg = jax.nn.sigmoid(ca)
            dca = dact_ * cb * (sg * (1.0 + ca * (1.0 - sg)))
            dcb = dact_ * (ca * sg)
            dca_ref[pl.ds(r0, R), :] = dca
            dcb_ref[pl.ds(r0, R), :] = dcb
            return (dwa + taps(dca, cura, s1a, s2a), dwb + taps(dcb, curb, s1b, s2b),
                    dba + jnp.sum(dca, axis=0, keepdims=True), dbb + jnp.sum(dcb, axis=0, keepdims=True))

        z3 = jnp.zeros((3, tc), F32)
        z1 = jnp.zeros((1, tc), F32)
        dwa, dwb, dba, dbb = lax.fori_loop(0, nr, first, (z3, z3, z1, z1))
        dwa_ref[...] = dwa
        dwb_ref[...] = dwb
        dba_ref[...] = dba
        dbb_ref[...] = dbb

        def du_rows(dc_ref, w, r0):
            cur = dc_ref[pl.ds(r0, R), :]
            n0 = pl.multiple_of(jnp.minimum(r0 + R, T - SUBLANES), SUBLANES)
            next8 = jnp.where(r0 + R < T, dc_ref[pl.ds(n0, SUBLANES), :], 0.0)
            m1, m2 = _shift_up(cur, next8)
            return w[2:3, :] * cur + w[1:2, :] * m1 + w[0:1, :] * m2

        def second(ri, carry):
            r0 = pl.multiple_of(ri * R, R)
            dua_ref[pl.ds(r0, R), :] = du_rows(dca_ref, wa_, r0).astype(BF16)
            dub_ref[pl.ds(r0, R), :] = du_rows(dcb_ref, wb_, r0).astype(BF16)
            return carry

        lax.fori_loop(0, nr, second, 0)

    col = pl.BlockSpec((T, tc), lambda j: (0, j))
    wsp = pl.BlockSpec((3, tc), lambda j: (0, j))
    bsp = pl.BlockSpec((1, tc), lambda j: (0, j))
    return pl.pallas_call(
        body, name=name, grid=(Fd // tc,), in_specs=[col, col, wsp, wsp, bsp, bsp, col],
        out_specs=[col, col, wsp, wsp, bsp, bsp],
        out_shape=[jax.ShapeDtypeStruct((T, Fd), BF16)] * 2 + [jax.ShapeDtypeStruct((3, Fd), F32)] * 2
        + [jax.ShapeDtypeStruct((1, Fd), F32)] * 2,
        scratch_shapes=[pltpu.VMEM((T, tc), F32), pltpu.VMEM((T, tc), F32)],
        compiler_params=_params(("parallel",)),
    )(ua, ub, wa, wb, ba, bb, dact)


def _bucket_index():
    t = np.arange(SW_WINDOW)[:, None] + SW_WINDOW
    s = np.arange(2 * SW_WINDOW)[None, :]
    dist = np.maximum(t - s, 0)
    exact = REL_BUCKETS // 2
    d = np.maximum(dist, 1).astype(np.float32)
    log_b = exact + (np.log(d / np.float32(exact)) / np.float32(math.log(REL_MAX_DIST / exact))
                     * np.float32(REL_BUCKETS - exact)).astype(np.int32)
    bucket = np.where(dist < exact, dist, np.minimum(log_b, REL_BUCKETS - 1))
    return bucket.astype(np.int32).reshape(1, -1)


BIAS_COLS = SW_WINDOW * 2 * SW_WINDOW
BIAS_TILE = 4096


def _bias_from_table(table, bucket, *, name):
    def body(t_ref, idx_ref, o_ref):
        onehot = (lax.broadcasted_iota(jnp.int32, (REL_BUCKETS, BIAS_TILE), 0) == idx_ref[...]).astype(BF16)
        acc = jnp.zeros((SW_Q_HEADS, BIAS_TILE), F32)
        for piece in _split3(t_ref[...]):
            acc = acc + lax.dot_general(piece, onehot, (((0,), (0,)), ((), ())), preferred_element_type=F32)
        o_ref[...] = acc

    return pl.pallas_call(
        body, name=name, grid=(BIAS_COLS // BIAS_TILE,),
        in_specs=[pl.BlockSpec((REL_BUCKETS, SW_Q_HEADS), lambda j: (0, 0)), pl.BlockSpec((1, BIAS_TILE), lambda j: (0, j))],
        out_specs=pl.BlockSpec((SW_Q_HEADS, BIAS_TILE), lambda j: (0, j)),
        out_shape=jax.ShapeDtypeStruct((SW_Q_HEADS, BIAS_COLS), F32),
        compiler_params=_params(("parallel",)),
    )(table, bucket)


def _table_grad(dbias, bucket, *, name):
    def body(d_ref, idx_ref, o_ref):
        @pl.when(pl.program_id(0) == 0)
        def _():
            o_ref[...] = jnp.zeros_like(o_ref)

        onehot = (lax.broadcasted_iota(jnp.int32, (REL_BUCKETS, BIAS_TILE), 0) == idx_ref[...]).astype(BF16)
        acc = jnp.zeros((REL_BUCKETS, SW_Q_HEADS), F32)
        for piece in _split3(d_ref[...]):
            acc = acc + lax.dot_general(onehot, piece, (((1,), (1,)), ((), ())), preferred_element_type=F32)
        o_ref[...] += acc

    return pl.pallas_call(
        body, name=name, grid=(BIAS_COLS // BIAS_TILE,),
        in_specs=[pl.BlockSpec((SW_Q_HEADS, BIAS_TILE), lambda j: (0, j)), pl.BlockSpec((1, BIAS_TILE), lambda j: (0, j))],
        out_specs=pl.BlockSpec((REL_BUCKETS, SW_Q_HEADS), lambda j: (0, 0)),
        out_shape=jax.ShapeDtypeStruct((REL_BUCKETS, SW_Q_HEADS), F32),
        compiler_params=_params(("arbitrary",)),
    )(dbias, bucket)


def _band_mask(n):
    rows = SW_GROUP * SW_WINDOW
    t = (lax.broadcasted_iota(jnp.int32, (rows, 2 * SW_WINDOW), 0) & (SW_WINDOW - 1)) + SW_WINDOW
    s = lax.broadcasted_iota(jnp.int32, (rows, 2 * SW_WINDOW), 1)
    dist = t - s
    return (dist >= 0) & (dist < SW_WINDOW) & ((n > 0) | (s >= SW_WINDOW))


def _head_cols(h):
    return slice(h * SW_HEAD_DIM, (h + 1) * SW_HEAD_DIM)


def _group_inputs(q_ref, bias_ref, sink_ref, g):
    heads = range(g * SW_GROUP, (g + 1) * SW_GROUP)
    q = jnp.concatenate([q_ref[:, _head_cols(h)] for h in heads], axis=0)
    sink = jnp.concatenate([jnp.broadcast_to(sink_ref[:, h:h + 1], (SW_WINDOW, 1)) for h in heads], axis=0)
    bias = bias_ref[g * SW_GROUP:(g + 1) * SW_GROUP].reshape(SW_GROUP * SW_WINDOW, 2 * SW_WINDOW)
    return heads, q, bias, sink


KV_DIM = SW_KV_HEADS * SW_HEAD_DIM


def _kv_pair(kvp_ref, kvc_ref, g):
    ks = slice(g * SW_HEAD_DIM, (g + 1) * SW_HEAD_DIM)
    vs = slice(KV_DIM + g * SW_HEAD_DIM, KV_DIM + (g + 1) * SW_HEAD_DIM)
    kk = jnp.concatenate([kvp_ref[:, ks], kvc_ref[:, ks]], axis=0)
    vv = jnp.concatenate([kvp_ref[:, vs], kvc_ref[:, vs]], axis=0)
    return kk, vv, ks, vs


def _attn_fwd(q1, kv, bias, sinks, *, name):
    T, Dm = q1.shape
    W = SW_WINDOW

    def body(q_ref, kvc_ref, kvp_ref, bias_ref, sink_ref, o_ref):
        mask = _band_mask(pl.program_id(0))
        G = range(SW_KV_HEADS)
        ins = [_group_inputs(q_ref, bias_ref, sink_ref, g) for g in G]
        kvs = [_kv_pair(kvp_ref, kvc_ref, g) for g in G]
        lg = [jnp.where(mask, mm_nt(ins[g][1], kvs[g][0]) * (SW_HEAD_DIM ** -0.5) + ins[g][2], -jnp.inf) for g in G]
        m = [jnp.maximum(jnp.max(lg[g], axis=-1, keepdims=True), ins[g][3]) for g in G]
        p = [jnp.exp(lg[g] - m[g]) for g in G]
        den = [jnp.sum(p[g], axis=-1, keepdims=True) + jnp.exp(ins[g][3] - m[g]) for g in G]
        o = [mm(p[g], kvs[g][1]) / den[g] for g in G]
        for g in G:
            for r, h in enumerate(ins[g][0]):
                o_ref[:, _head_cols(h)] = o[g][r * W:(r + 1) * W].astype(BF16)

    return pl.pallas_call(
        body, name=name, grid=(T // W,),
        in_specs=[pl.BlockSpec((W, Dm), lambda n: (n, 0)),
                  pl.BlockSpec((W, 2 * KV_DIM), lambda n: (n, 0)),
                  pl.BlockSpec((W, 2 * KV_DIM), lambda n: (jnp.maximum(n - 1, 0), 0)),
                  pl.BlockSpec((SW_Q_HEADS, W, 2 * W), lambda n: (0, 0, 0)),
                  pl.BlockSpec((1, SW_Q_HEADS), lambda n: (0, 0))],
        out_specs=pl.BlockSpec((W, Dm), lambda n: (n, 0)),
        out_shape=jax.ShapeDtypeStruct((T, Dm), BF16),
        compiler_params=_params(("parallel",)),
    )(q1, kv, kv, bias, sinks)


def _attn_bwd(q1, kv, bias, sinks, do, *, name):
    T, Dm = q1.shape
    W = SW_WINDOW
    nb = T // W

    def body(q_ref, kvc_ref, kvp_ref, bias_ref, sink_ref, do_ref,
             dq_ref, dkv_ref, dbias_ref, dsink_ref, carry_ref):
        @pl.when(pl.program_id(0) == 0)
        def _():
            carry_ref[...] = jnp.zeros_like(carry_ref)
            dbias_ref[...] = jnp.zeros_like(dbias_ref)
            dsink_ref[...] = jnp.zeros_like(dsink_ref)

        n = nb - 1 - pl.program_id(0)
        mask = _band_mask(n)
        lane = lax.broadcasted_iota(jnp.int32, (1, SW_Q_HEADS), 1)
        sc = SW_HEAD_DIM ** -0.5
        G = range(SW_KV_HEADS)
        ins = [_group_inputs(q_ref, bias_ref, sink_ref, g) for g in G]
        kvs = [_kv_pair(kvp_ref, kvc_ref, g) for g in G]
        do = [jnp.concatenate([do_ref[:, _head_cols(h)] for h in ins[g][0]], axis=0) for g in G]
        lg = [jnp.where(mask, mm_nt(ins[g][1], kvs[g][0]) * sc + ins[g][2], -jnp.inf) for g in G]
        m = [jnp.maximum(jnp.max(lg[g], axis=-1, keepdims=True), ins[g][3]) for g in G]
        p = [jnp.exp(lg[g] - m[g]) for g in G]
        ps = [jnp.exp(ins[g][3] - m[g]) for g in G]
        rden = [1.0 / (jnp.sum(p[g], axis=-1, keepdims=True) + ps[g]) for g in G]
        pn = [p[g] * rden[g] for g in G]
        dpn = [mm_nt(do[g], kvs[g][1]) for g in G]
        delta = [jnp.sum(pn[g] * dpn[g], axis=-1, keepdims=True) for g in G]
        ds = [pn[g] * (dpn[g] - delta[g]) for g in G]
        dsr = [-(ps[g] * rden[g]) * delta[g] for g in G]
        dq = [mm(ds[g], kvs[g][0]) * sc for g in G]
        dkk = [mm_tn(ds[g], ins[g][1]) * sc for g in G]
        dvv = [mm_tn(pn[g], do[g]) for g in G]
        dsink = jnp.zeros((1, SW_Q_HEADS), F32)
        for g in G:
            _, _, ks, vs = kvs[g]
            dbias_ref[g * SW_GROUP:(g + 1) * SW_GROUP] += ds[g].reshape(SW_GROUP, W, 2 * W)
            for r, h in enumerate(ins[g][0]):
                dq_ref[:, _head_cols(h)] = dq[g][r * W:(r + 1) * W].astype(BF16)
                dsink = dsink + jnp.where(lane == h, jnp.sum(dsr[g][r * W:(r + 1) * W], axis=0, keepdims=True), 0.0)
            dkv_ref[:, ks] = (carry_ref[:, ks] + dkk[g][W:]).astype(BF16)
            dkv_ref[:, vs] = (carry_ref[:, vs] + dvv[g][W:]).astype(BF16)
            carry_ref[:, ks] = dkk[g][:W]
            carry_ref[:, vs] = dvv[g][:W]
        dsink_ref[...] += dsink

    rev = lambda n: (nb - 1 - n, 0)
    return pl.pallas_call(
        body, name=name, grid=(nb,),
        in_specs=[pl.BlockSpec((W, Dm), rev),
                  pl.BlockSpec((W, 2 * KV_DIM), rev),
                  pl.BlockSpec((W, 2 * KV_DIM), lambda n: (jnp.maximum(nb - 2 - n, 0), 0)),
                  pl.BlockSpec((SW_Q_HEADS, W, 2 * W), lambda n: (0, 0, 0)),
                  pl.BlockSpec((1, SW_Q_HEADS), lambda n: (0, 0)),
                  pl.BlockSpec((W, Dm), rev)],
        out_specs=[pl.BlockSpec((W, Dm), rev), pl.BlockSpec((W, 2 * KV_DIM), rev),
                   pl.BlockSpec((SW_Q_HEADS, W, 2 * W), lambda n: (0, 0, 0)),
                   pl.BlockSpec((1, SW_Q_HEADS), lambda n: (0, 0))],
        out_shape=[jax.ShapeDtypeStruct((T, Dm), BF16), jax.ShapeDtypeStruct((T, 2 * KV_DIM), BF16),
                   jax.ShapeDtypeStruct((SW_Q_HEADS, W, 2 * W), F32), jax.ShapeDtypeStruct((1, SW_Q_HEADS), F32)],
        scratch_shapes=[pltpu.VMEM((W, 2 * KV_DIM), F32)],
        compiler_params=_params(("arbitrary",)),
    )(q1, kv, kv, bias, sinks, do)


def _ffn_fwd(hb, w, l, after=None):
    ua = _matmul(hb, w["ffn_in_a"][l], mode="nn", out_dtype=BF16, name=f"ffn{l}_up_a", tm=1024, after=after)
    ub = _matmul(hb, w["ffn_in_b"][l], mode="nn", out_dtype=BF16, name=f"ffn{l}_up_b", tm=1024)
    act = _conv_gate_fwd(ua, ub, w["conv_w_a"][l], w["conv_w_b"][l], w["conv_b_a"][l], w["conv_b_b"][l],
                         name=f"ffn{l}_conv_gate")
    ff = _matmul(act, w["ffn_out"][l], mode="nn", name=f"ffn{l}_down", tn=1024, tk=FFN_DIM)
    return ua, ub, act, ff


def _ffn_bwd(dffb, dh_scaled, hb, ua, ub, act, w, l):
    dact = _matmul(dffb, w["ffn_out"][l], mode="nt", out_dtype=BF16, name=f"ffn{l}_down_dx", tm=1024)
    g_out = _matmul(act, dffb, mode="tn", name=f"ffn{l}_down_dw", tm=1408, tn=1024, tk=1024)
    dua, dub, dwa, dwb, dba, dbb = _conv_gate_bwd(ua, ub, w["conv_w_a"][l], w["conv_w_b"][l], w["conv_b_a"][l],
                                                  w["conv_b_b"][l], dact, name=f"ffn{l}_conv_gate_bwd")
    dh = _matmul(dua, w["ffn_in_a"][l], mode="nt", add=dh_scaled, add_scale=ALPHA, name=f"ffn{l}_up_a_dx",
                 tn=1024, tk=FFN_DIM)
    dh = _matmul(dub, w["ffn_in_b"][l], mode="nt", add=dh, name=f"ffn{l}_up_b_dx", tn=1024, tk=FFN_DIM)
    g_in_a = _matmul(hb, dua, mode="tn", name=f"ffn{l}_up_a_dw", tm=1024, tn=1408, tk=1024)
    g_in_b = _matmul(hb, dub, mode="tn", name=f"ffn{l}_up_b_dw", tm=1024, tn=1408, tk=1024)
    return dh, dict(ffn_out=g_out, ffn_in_a=g_in_a, ffn_in_b=g_in_b, conv_w_a=dwa, conv_w_b=dwb, conv_b_a=dba, conv_b_b=dbb)


def _local_step(x, tgt, w, more_weights, emit):
    bucket = jnp.asarray(_bucket_index())
    xb = x.astype(BF16)

    pre = [_matmul(xb, w["hg_in"][j], mode="nn", name=f"hg_in_{j}", tn=1024, after=w.get("token") if j == 0 else None)
           for j in range(4)]
    og, states = _hgrn_fwd(*pre, w["lb_logits"], w["gnorm"], name="hgrn_fwd")
    mix0 = _matmul(og, w["hg_out"], mode="nn", name="hg_out", tn=1024)
    h1, h1b = _ln_fwd(x, mix0, w["ln_mix_g"][0], w["ln_mix_b"][0], name="ln_mix0")
    w = {**w, **more_weights(1, h1b)}
    ua0, ub0, act0, ff0 = _ffn_fwd(h1b, w, 0, after=w.get("token"))
    h2, h2b = _ln_fwd(h1, ff0, w["ln_ffn_g"][0], w["ln_ffn_b"][0], name="ln_ffn0")
    kv = _matmul(h2b, w["kv"], mode="nn", name="kv_proj")

    bias = _bias_from_table(w["rel_bias"], bucket, name="rel_bias_expand").reshape(SW_Q_HEADS, SW_WINDOW, 2 * SW_WINDOW)
    q1 = _matmul(h2b, w["sw_q"], mode="nn", name="sw_q")
    o1 = _attn_fwd(q1, kv, bias, w["sinks"], name="attn_fwd")
    mix1 = _matmul(o1, w["sw_out"], mode="nn", name="sw_out")
    h3, h3b = _ln_fwd(h2, mix1, w["ln_mix_g"][1], w["ln_mix_b"][1], name="ln_mix1")
    w = {**w, **more_weights(2, h3b)}
    ua1, ub1, act1, ff1 = _ffn_fwd(h3b, w, 1)
    y, _ = _ln_fwd(h3, ff1, w["ln_ffn_g"][1], w["ln_ffn_b"][1], name="ln_ffn1")

    dy, loss_tile = _loss_grad(y, tgt, name="loss_grad")

    g = {}
    dz, dzb, dg_, db_ = _ln_bwd(dy, h3, ff1, w["ln_ffn_g"][1], w["ln_ffn_b"][1], name="ln_ffn1_bwd")
    g["ln_ffn_g1"], g["ln_ffn_b1"] = dg_, db_
    dh3, gf1 = _ffn_bwd(dzb, dz, h3b, ua1, ub1, act1, w, 1)
    dz, dzb, dg_, db_ = _ln_bwd(dh3, h2, mix1, w["ln_mix_g"][1], w["ln_mix_b"][1], name="ln_mix1_bwd")
    g["ln_mix_g1"], g["ln_mix_b1"] = dg_, db_
    do1 = _matmul(dzb, w["sw_out"], mode="nt", out_dtype=BF16, name="sw_out_dx")
    g_sw_out = _matmul(o1, dzb, mode="tn", name="sw_out_dw", tm=1024, tn=1024, tk=1024)
    dq1, dkv, dbias, dsinks = _attn_bwd(q1, kv, bias, w["sinks"], do1, name="attn_bwd")
    g["sinks"] = dsinks
    g["rel_bias"] = _table_grad(dbias.reshape(SW_Q_HEADS, BIAS_COLS), bucket, name="rel_bias_grad")
    dh2 = _matmul(dq1, w["sw_q"], mode="nt", add=dz, add_scale=ALPHA, name="sw_q_dx", tn=1024)
    dh2 = _matmul(dkv, w["kv"], mode="nt", add=dh2, name="kv_dx", tn=1024)
    g_sw_q = _matmul(h2b, dq1, mode="tn", name="sw_q_dw", tm=1024, tn=1024, tk=1024)
    g_kv = _matmul(h2b, dkv, mode="tn", name="kv_dw", tm=1024, tn=512, tk=1024)
    tok = emit(1, dict(sw_q=g_sw_q, sw_out=g_sw_out, kv=g_kv, ffn_in_a=gf1["ffn_in_a"], ffn_in_b=gf1["ffn_in_b"],
                       ffn_out=gf1["ffn_out"]))

    dz, dzb, dg_, db_ = _ln_bwd(dh2, h1, ff0, w["ln_ffn_g"][0], w["ln_ffn_b"][0], name="ln_ffn0_bwd", after=tok)
    g["ln_ffn_g0"], g["ln_ffn_b0"] = dg_, db_
    dh1, gf0 = _ffn_bwd(dzb, dz, h1b, ua0, ub0, act0, w, 0)
    dz, dzb, dg_, db_ = _ln_bwd(dh1, x, mix0, w["ln_mix_g"][0], w["ln_mix_b"][0], name="ln_mix0_bwd")
    g["ln_mix_g0"], g["ln_mix_b0"] = dg_, db_
    dog = _matmul(dzb, w["hg_out"], mode="nt", out_dtype=BF16, name="hg_out_dx")
    g_hg_out = _matmul(og, dzb, mode="tn", name="hg_out_dw", tm=1024, tn=1024, tk=1024)
    tok = emit(2, dict(hg_out=g_hg_out, ffn_in_a=gf0["ffn_in_a"], ffn_in_b=gf0["ffn_in_b"], ffn_out=gf0["ffn_out"]))
    dpre = _hgrn_bwd(*pre, w["lb_logits"], w["gnorm"], states, dog, name="hgrn_bwd", after=tok)
    g["lb_logits"], g["gnorm"] = dpre[4], dpre[5]
    tok = emit(3, dict(hg_in=[_matmul(xb, dpre[j], mode="tn", name=f"hg_in_{j}_dw", tm=1024, tn=1024, tk=1024)
                              for j in range(4)]))
    dx = dz
    for j in range(4):
        dx = _matmul(dpre[j], w["hg_in"][j], mode="nt", add=dx, add_scale=ALPHA if j == 0 else 1.0,
                     name=f"hg_in_{j}_dx", tn=1024, after=tok if j == 0 else None)
    g["conv"] = [{k: gf[k] for k in ("conv_w_a", "conv_w_b", "conv_b_a", "conv_b_b")} for gf in (gf0, gf1)]
    return loss_tile, dx, g


def _adamw(wt, ga, gb, m, v, *, name):
    R, Cc = wt.shape
    tr = _tile(R, 256, SUBLANES) if R % SUBLANES == 0 else R
    c1 = 1.0 - ADAM_B1 ** ADAM_STEP
    c2 = 1.0 - ADAM_B2 ** ADAM_STEP
    two = gb is not None

    def body(*refs):
        if two:
            w_ref, ga_ref, gb_ref, m_ref, v_ref, g_ref, d_ref, nm_ref, nv_ref = refs
            g_ = ga_ref[...] + gb_ref[...]
        else:
            w_ref, ga_ref, m_ref, v_ref, g_ref, d_ref, nm_ref, nv_ref = refs
            g_ = ga_ref[...]
        nm = ADAM_B1 * m_ref[...] + (1.0 - ADAM_B1) * g_
        nv = ADAM_B2 * v_ref[...] + (1.0 - ADAM_B2) * (g_ * g_)
        g_ref[...] = g_
        d_ref[...] = -ADAM_LR * ((nm / c1) / (jnp.sqrt(nv / c2) + ADAM_EPS) + ADAM_WD * w_ref[...])
        nm_ref[...] = nm
        nv_ref[...] = nv

    blk = pl.BlockSpec((tr, Cc), lambda i: (i, 0))
    args = (wt, ga, gb, m, v) if two else (wt, ga, m, v)
    return pl.pallas_call(
        body, name=name, grid=(R // tr,), in_specs=[blk] * len(args), out_specs=[blk] * 4,
        out_shape=[jax.ShapeDtypeStruct((R, Cc), F32)] * 4,
        compiler_params=_params(("parallel",)),
    )(*args)


HBM_SPEC = pl.BlockSpec(memory_space=pltpu.HBM)
SEM_SPEC = pl.BlockSpec(memory_space=pltpu.SEMAPHORE)
VMEM_SPEC = pl.BlockSpec(memory_space=pltpu.VMEM)
DATAFLOW = pltpu.SideEffectType.DATAFLOW_SIDE_EFFECTING


def _in_hbm(a):
    return pltpu.with_memory_space_constraint(a, pltpu.HBM)


def _place():
    return lax.axis_index("x"), lax.axis_index("y"), lax.axis_index("c")


def _other_chips(x, y):
    return [(1 - x, y), (x, 1 - y), (1 - x, 1 - y)]


def _sum8(v, *, name):
    r = v.shape[0]

    def body(v_ref, all_ref, o_ref, send_sems, recv_sems, local_sem):
        x, y, c = _place()
        me, sibling = (x, y, c), (x, y, 1 - c)
        chips = _other_chips(x, y)

        def rows(px, py, pc):
            return all_ref.at[pl.ds((4 * px + 2 * py + pc) * r, r), :]

        def copy(k, block, to, src=None):
            return pltpu.make_async_remote_copy(
                src_ref=rows(*block) if src is None else src, dst_ref=rows(*block),
                send_sem=send_sems.at[k], recv_sem=recv_sems.at[k], device_id=to, device_id_type=MESH)

        mine = pltpu.make_async_copy(v_ref, rows(*me), local_sem)
        mine.start()
        first = [copy(0, me, sibling, src=v_ref)]
        first += [copy(1 + j, me, (*chip, c), src=v_ref) for j, chip in enumerate(chips)]
        for cp in first:
            cp.start()
        passed = [copy(4 + j, (*chip, c), sibling) for j, chip in enumerate(chips)]
        for j, chip in enumerate(chips):
            copy(1 + j, (*chip, c), me).wait_recv()
            passed[j].start()
        copy(0, sibling, me).wait_recv()
        for j, chip in enumerate(chips):
            copy(4 + j, (*chip, 1 - c), me).wait_recv()
        for cp in first + passed:
            cp.wait_send()
        mine.wait()
        acc = all_ref[pl.ds(0, r), :]
        for d in range(1, N_DEV):
            acc = acc + all_ref[pl.ds(d * r, r), :]
        o_ref[...] = acc

    return pl.pallas_call(
        body, name=name, in_specs=[VMEM_SPEC], out_specs=[VMEM_SPEC, VMEM_SPEC],
        out_shape=[jax.ShapeDtypeStruct((N_DEV * r, LANES), F32), jax.ShapeDtypeStruct((r, LANES), F32)],
        scratch_shapes=[pltpu.SemaphoreType.DMA((7,)), pltpu.SemaphoreType.DMA((7,)), pltpu.SemaphoreType.DMA],
        compiler_params=pltpu.CompilerParams(vmem_limit_bytes=VMEM_LIMIT),
    )(v)[1]


def _gather_chips(shard, *, name):
    R, Cc = shard.shape
    half = R // 2
    assert half * 2 == R

    def body(s_ref, o_ref, send_sems, recv_sems, local_sem):
        x, y, c = _place()
        sibling = (x, y, 1 - c)
        chips = _other_chips(x, y)

        def part(px, py, pc):
            return o_ref.at[2 * px + py, pl.ds(pc * half, half), :]

        def copy(k, block, to, src=None):
            return pltpu.make_async_remote_copy(
                src_ref=part(*block) if src is None else src, dst_ref=part(*block),
                send_sem=send_sems.at[k], recv_sem=recv_sems.at[k], device_id=to, device_id_type=MESH)

        mine = pltpu.make_async_copy(s_ref, o_ref.at[2 * x + y], local_sem)
        mine.start()
        my_half = s_ref.at[pl.ds(c * half, half), :]
        first = [copy(j, (x, y, c), (*chip, c), src=my_half) for j, chip in enumerate(chips)]
        for cp in first:
            cp.start()
        passed = [copy(3 + j, (*chip, c), sibling) for j, chip in enumerate(chips)]
        for j, chip in enumerate(chips):
            copy(j, (*chip, c), (x, y, c)).wait_recv()
            passed[j].start()
        for j, chip in enumerate(chips):
            copy(3 + j, (*chip, 1 - c), (x, y, c)).wait_recv()
        for cp in first + passed:
            cp.wait_send()
        mine.wait()

    return pl.pallas_call(
        body, name=name, in_specs=[HBM_SPEC], out_specs=HBM_SPEC,
        out_shape=jax.ShapeDtypeStruct((N_CHIPS, R, Cc), shard.dtype),
        scratch_shapes=[pltpu.SemaphoreType.DMA((6,)), pltpu.SemaphoreType.DMA((6,)), pltpu.SemaphoreType.DMA],
    )(shard)


def _swap_sibling(v, *, name):
    def body(v_ref, o_ref, send_sem, recv_sem):
        x, y, c = _place()
        cp = pltpu.make_async_remote_copy(src_ref=v_ref, dst_ref=o_ref, send_sem=send_sem, recv_sem=recv_sem,
                                          device_id=(x, y, 1 - c), device_id_type=MESH)
        cp.start()
        cp.wait()

    return pl.pallas_call(
        body, name=name, in_specs=[HBM_SPEC], out_specs=HBM_SPEC, out_shape=jax.ShapeDtypeStruct(v.shape, v.dtype),
        scratch_shapes=[pltpu.SemaphoreType.DMA, pltpu.SemaphoreType.DMA],
    )(v)


def _half(ref, j, c, half):
    return ref.at[j, pl.ds(c * half, half), :]


def _gather_start(shard, after, *, name):
    R, Cc = shard.shape
    half = R // 2

    def body(src, land, after_ref, send, recv, src_out, land_out, token):
        x, y, c = _place()
        for k, (px, py) in enumerate(_other_chips(x, y)):
            pltpu.make_async_remote_copy(src_ref=src.at[pl.ds(c * half, half), :], dst_ref=_half(land, 2 * x + y, c, half),
                                         send_sem=send.at[k], recv_sem=recv.at[k], device_id=(px, py, c),
                                         device_id_type=MESH).start()
        token[...] = jnp.zeros_like(token)

    land = lax.empty((N_CHIPS, R, Cc), shard.dtype)
    out = pl.pallas_call(
        body, name=name, in_specs=[HBM_SPEC, HBM_SPEC, ANY_SPEC],
        out_specs=[SEM_SPEC, SEM_SPEC, HBM_SPEC, HBM_SPEC, VMEM_SPEC],
        out_shape=[pltpu.SemaphoreType.DMA((3,)), pltpu.SemaphoreType.DMA((3,)), pltpu.HBM(shard.shape, shard.dtype),
                   pltpu.HBM(land.shape, land.dtype), jax.ShapeDtypeStruct((SUBLANES, LANES), F32)],
        input_output_aliases={0: 2, 1: 3},
        compiler_params=pltpu.CompilerParams(has_side_effects=DATAFLOW),
    )(_in_hbm(shard), _in_hbm(land), after)
    return out[:4], out[4]


def _gather_wait(handle, after, *, name):
    send_sems, recv_sems, src, land = handle
    half = src.shape[0] // 2

    def body(src_ref, land_ref, send_ref, recv_ref, after_ref, src_out, land_out):
        x, y, c = _place()
        for k, (px, py) in enumerate(_other_chips(x, y)):
            cp = pltpu.make_async_remote_copy(src_ref=src_ref.at[pl.ds(c * half, half), :],
                                              dst_ref=_half(land_ref, 2 * px + py, c, half), send_sem=send_ref.at[k],
                                              recv_sem=recv_ref.at[k], device_id=(px, py, c), device_id_type=MESH)
            cp.wait_send()
            cp.wait_recv()

    return pl.pallas_call(
        body, name=name, in_specs=[HBM_SPEC, HBM_SPEC, SEM_SPEC, SEM_SPEC, ANY_SPEC], out_specs=[HBM_SPEC, HBM_SPEC],
        out_shape=[pltpu.HBM(src.shape, src.dtype), pltpu.HBM(land.shape, land.dtype)],
        input_output_aliases={0: 0, 1: 1},
        compiler_params=pltpu.CompilerParams(has_side_effects=DATAFLOW),
    )(src, land, send_sems, recv_sems, after)[1]


def _fill_sibling(land, *, name):
    _, R, Cc = land.shape
    half = R // 2

    def body(in_ref, o_ref, send_sems, recv_sems):
        x, y, c = _place()
        chips = _other_chips(x, y)
        cps = [pltpu.make_async_remote_copy(src_ref=_half(in_ref, 2 * px + py, c, half),
                                            dst_ref=_half(o_ref, 2 * px + py, c, half), send_sem=send_sems.at[k],
                                            recv_sem=recv_sems.at[k], device_id=(x, y, 1 - c), device_id_type=MESH)
               for k, (px, py) in enumerate(chips)]
        for cp in cps:
            cp.start()
        for k, (px, py) in enumerate(chips):
            pltpu.make_async_remote_copy(src_ref=_half(in_ref, 2 * px + py, 1 - c, half),
                                         dst_ref=_half(o_ref, 2 * px + py, 1 - c, half), send_sem=send_sems.at[k],
                                         recv_sem=recv_sems.at[k], device_id=(x, y, 1 - c), device_id_type=MESH).wait_recv()
        for cp in cps:
            cp.wait_send()

    return pl.pallas_call(
        body, name=name, in_specs=[HBM_SPEC], out_specs=HBM_SPEC, out_shape=jax.ShapeDtypeStruct(land.shape, land.dtype),
        scratch_shapes=[pltpu.SemaphoreType.DMA((3,)), pltpu.SemaphoreType.DMA((3,))],
        input_output_aliases={0: 0},
    )(land)


def _scatter_start(pieces, *, name):
    _, R, Cc = pieces.shape

    def body(src, land, send, recv, src_out, land_out, token):
        x, y, c = _place()
        for k, (px, py) in enumerate(_other_chips(x, y)):
            pltpu.make_async_remote_copy(src_ref=src.at[2 * px + py], dst_ref=land.at[k], send_sem=send.at[k],
                                         recv_sem=recv.at[k], device_id=(px, py, c), device_id_type=MESH).start()
        token[...] = jnp.zeros_like(token)

    land = lax.empty((3, R, Cc), pieces.dtype)
    out = pl.pallas_call(
        body, name=name, in_specs=[HBM_SPEC, HBM_SPEC],
        out_specs=[SEM_SPEC, SEM_SPEC, HBM_SPEC, HBM_SPEC, VMEM_SPEC],
        out_shape=[pltpu.SemaphoreType.DMA((3,)), pltpu.SemaphoreType.DMA((3,)), pltpu.HBM(pieces.shape, pieces.dtype),
                   pltpu.HBM(land.shape, land.dtype), jax.ShapeDtypeStruct((SUBLANES, LANES), F32)],
        input_output_aliases={0: 2, 1: 3},
        compiler_params=pltpu.CompilerParams(has_side_effects=DATAFLOW),
    )(_in_hbm(pieces), _in_hbm(land))
    return out[:4], out[4]


def _scatter_wait(handle, after, *, name):
    send_sems, recv_sems, src, land = handle

    def body(src_ref, land_ref, send_ref, recv_ref, after_ref, src_out, land_out):
        x, y, c = _place()
        for k, (px, py) in enumerate(_other_chips(x, y)):
            cp = pltpu.make_async_remote_copy(src_ref=src_ref.at[2 * px + py], dst_ref=land_ref.at[k], send_sem=send_ref.at[k],
                                              recv_sem=recv_ref.at[k], device_id=(px, py, c), device_id_type=MESH)
            cp.wait_send()
            cp.wait_recv()

    return pl.pallas_call(
        body, name=name, in_specs=[HBM_SPEC, HBM_SPEC, SEM_SPEC, SEM_SPEC, ANY_SPEC], out_specs=[HBM_SPEC, HBM_SPEC],
        out_shape=[pltpu.HBM(src.shape, src.dtype), pltpu.HBM(land.shape, land.dtype)],
        input_output_aliases={0: 0, 1: 1},
        compiler_params=pltpu.CompilerParams(has_side_effects=DATAFLOW),
    )(src, land, send_sems, recv_sems, after)[1]


def _chip_sum(own, got, *, name):
    R, Cc = own.shape
    tr = _tile(R, 256, SUBLANES)

    def body(a_ref, g_ref, o_ref):
        o_ref[...] = ((a_ref[...] + g_ref[0].astype(F32)) + g_ref[1].astype(F32)) + g_ref[2].astype(F32)

    return pl.pallas_call(
        body, name=name, grid=(R // tr,),
        in_specs=[pl.BlockSpec((tr, Cc), lambda i: (i, 0)), pl.BlockSpec((3, tr, Cc), lambda i: (0, i, 0))],
        out_specs=pl.BlockSpec((tr, Cc), lambda i: (i, 0)),
        out_shape=jax.ShapeDtypeStruct((R, Cc), F32),
        compiler_params=_params(("parallel",)),
    )(own, got)


PACK_COLS = 1024


def _pack_rows(parts):
    return jnp.concatenate([p.reshape(-1, PACK_COLS) for p in parts], axis=0)


def _unpack_rows(block, shapes):
    lead = block.shape[:-2]
    out, off = [], 0
    for s in shapes:
        r = int(np.prod(s)) // PACK_COLS
        out.append(block[..., off:off + r, :].reshape(lead + tuple(s)))
        off += r
    assert off == block.shape[-2]
    return out


def _flat128(parts):
    out = []
    for p in parts:
        v = p.reshape(-1)
        pad = (-v.shape[0]) % LANES
        out.append(jnp.pad(v, (0, pad)) if pad else v)
    v = jnp.concatenate(out)
    pad = (-v.shape[0]) % (SUBLANES * LANES)
    if pad:
        v = jnp.pad(v, (0, pad))
    return v.reshape(-1, LANES)


def _unflat128(block, shapes):
    v = block.reshape(-1)
    out, off = [], 0
    for s in shapes:
        n = int(np.prod(s))
        out.append(v[off:off + n].reshape(s))
        off += n + ((-n) % LANES)
    return out


def kernel(x, hgrn_w_in, hgrn_lb_logits, hgrn_gnorm_w, hgrn_w_out, swa_w_q, swa_sinks, swa_w_out, shared_w_kv, rel_bias, ffn_w_in, ffn_conv_w, ffn_conv_b, ffn_w_out, ln_mix_g, ln_mix_b, ln_ffn_g, ln_ffn_b, loss_target, m_hgrn_w_in, m_hgrn_lb_logits, m_hgrn_gnorm_w, m_hgrn_w_out, m_swa_w_q, m_swa_sinks, m_swa_w_out, m_shared_w_kv, m_rel_bias, m_ffn_w_in, m_ffn_conv_w, m_ffn_conv_b, m_ffn_w_out, m_ln_mix_g, m_ln_mix_b, m_ln_ffn_g, m_ln_ffn_b, v_hgrn_w_in, v_hgrn_lb_logits, v_hgrn_gnorm_w, v_hgrn_w_out, v_swa_w_q, v_swa_sinks, v_swa_w_out, v_shared_w_kv, v_rel_bias, v_ffn_w_in, v_ffn_conv_w, v_ffn_conv_b, v_ffn_w_out, v_ln_mix_g, v_ln_mix_b, v_ln_ffn_g, v_ln_ffn_b):
    xi, yi, ci = _place()
    chip = 2 * xi + yi
    Dm = D_MODEL
    FC = 2 * FFN_DIM // N_CHIPS
    Fo = FFN_DIM // N_CHIPS
    Dq = Dm // N_CHIPS
    bf = lambda a: a.astype(BF16)

    shard0 = _pack_rows([bf(hgrn_w_in), bf(hgrn_w_out)])
    shard1 = _pack_rows([bf(swa_w_q), bf(swa_w_out), bf(shared_w_kv), bf(ffn_w_in[0]), bf(ffn_w_out[0])])
    shard2 = _pack_rows([bf(ffn_w_in[1]), bf(ffn_w_out[1])])
    all0 = _gather_chips(shard0, name="gather_w0")
    handle1, token1 = _gather_start(shard1, all0, name="gather_w1_start")
    w_in, w_hg_out = _unpack_rows(all0, [(Dm, Dm), (Dq, Dm)])

    def ffn_weights(w_fi, w_fo, l):
        return {"ffn_in_a": {l: jnp.concatenate([w_fi[0], w_fi[1]], axis=1)},
                "ffn_in_b": {l: jnp.concatenate([w_fi[2], w_fi[3]], axis=1)},
                "ffn_out": {l: w_fo.reshape(FFN_DIM, Dm)}}

    got = {}

    def more_weights(k, after):
        shard = (shard1, shard2)[k - 1]
        land = _gather_wait(got.pop("handle"), after, name=f"gather_w{k}_wait")
        land = _fill_sibling(land, name=f"gather_w{k}_fill")
        allk = lax.dynamic_update_slice(land, shard[None], (chip, 0, 0))
        if k == 1:
            got["handle"], token2 = _gather_start(shard2, land, name="gather_w2_start")
            w_q, w_o, w_kv, w_fi, w_fo = _unpack_rows(allk, [(Dq, Dm), (Dq, Dm), (Dq, 2 * KV_DIM), (Dm, FC), (Fo, Dm)])
            got.update(ffn_weights(w_fi, w_fo, 0))
            return {"sw_q": w_q.reshape(Dm, Dm), "sw_out": w_o.reshape(Dm, Dm), "kv": w_kv.reshape(Dm, 2 * KV_DIM),
                    "token": token2, **{n: got[n] for n in ("ffn_in_a", "ffn_in_b", "ffn_out")}}
        w_fi, w_fo = _unpack_rows(allk, [(Dm, FC), (Fo, Dm)])
        new = ffn_weights(w_fi, w_fo, 1)
        return {n: {**got[n], **new[n]} for n in new}

    got["handle"] = handle1

    lb_full = lax.dynamic_update_slice(jnp.zeros((2, Dm), F32), hgrn_lb_logits, (0, chip * Dq))
    cw_full = lax.dynamic_update_slice(jnp.zeros((DEPTH, 3, 2 * FFN_DIM), F32), ffn_conv_w, (0, 0, chip * FC))
    only_south = (ci == 0).astype(F32)
    small_in = _sum8(_flat128([lb_full, cw_full]) * only_south, name="gather_small")
    lb_full, cw_full = _unflat128(small_in, [(2, Dm), (DEPTH, 3, 2 * FFN_DIM)])
    w = {
        "hg_in": [w_in[j] for j in range(4)], "hg_out": w_hg_out.reshape(Dm, Dm), "token": token1,
        "lb_logits": lb_full, "gnorm": hgrn_gnorm_w, "sinks": swa_sinks, "rel_bias": rel_bias,
        "conv_w_a": [cw_full[l, :, :FFN_DIM] for l in range(DEPTH)],
        "conv_w_b": [cw_full[l, :, FFN_DIM:] for l in range(DEPTH)],
        "conv_b_a": [ffn_conv_b[l:l + 1, :FFN_DIM] for l in range(DEPTH)],
        "conv_b_b": [ffn_conv_b[l:l + 1, FFN_DIM:] for l in range(DEPTH)],
        "ln_mix_g": [ln_mix_g[l:l + 1] for l in range(DEPTH)], "ln_mix_b": [ln_mix_b[l:l + 1] for l in range(DEPTH)],
        "ln_ffn_g": [ln_ffn_g[l:l + 1] for l in range(DEPTH)], "ln_ffn_b": [ln_ffn_b[l:l + 1] for l in range(DEPTH)],
    }

    sent = {}

    def ffn_blocks(gd, j):
        fi = gd["ffn_in_a"] if j < 2 else gd["ffn_in_b"]
        return [fi[:, (j % 2) * FC:(j % 2 + 1) * FC], gd["ffn_out"][j * Fo:(j + 1) * Fo]]

    def emit(k, gd):
        if k == 1:
            blocks = [[gd["sw_q"][j * Dq:(j + 1) * Dq], gd["sw_out"][j * Dq:(j + 1) * Dq], gd["kv"][j * Dq:(j + 1) * Dq]]
                      + ffn_blocks(gd, j) for j in range(N_CHIPS)]
        elif k == 2:
            blocks = [ffn_blocks(gd, j) + [gd["hg_out"][j * Dq:(j + 1) * Dq]] for j in range(N_CHIPS)]
        else:
            blocks = [[gd["hg_in"][j]] for j in range(N_CHIPS)]
        pieces = jnp.stack([_pack_rows(b) for b in blocks])
        own = lax.dynamic_index_in_dim(pieces, chip, axis=0, keepdims=False)
        handle, token = _scatter_start(pieces.astype(BF16), name=f"scatter_g{k}_start")
        sent[k] = (handle, own)
        return token

    loss_tile, grad_x, g = _local_step(x[0], loss_target[0], w, more_weights, emit)

    wts = dict(hgrn_w_in=hgrn_w_in, hgrn_lb_logits=hgrn_lb_logits, hgrn_gnorm_w=hgrn_gnorm_w, hgrn_w_out=hgrn_w_out,
               swa_w_q=swa_w_q, swa_sinks=swa_sinks, swa_w_out=swa_w_out, shared_w_kv=shared_w_kv, rel_bias=rel_bias,
               ffn_w_in=ffn_w_in, ffn_conv_w=ffn_conv_w, ffn_conv_b=ffn_conv_b, ffn_w_out=ffn_w_out,
               ln_mix_g=ln_mix_g, ln_mix_b=ln_mix_b, ln_ffn_g=ln_ffn_g, ln_ffn_b=ln_ffn_b)
    ms = dict(hgrn_w_in=m_hgrn_w_in, hgrn_lb_logits=m_hgrn_lb_logits, hgrn_gnorm_w=m_hgrn_gnorm_w, hgrn_w_out=m_hgrn_w_out,
              swa_w_q=m_swa_w_q, swa_sinks=m_swa_sinks, swa_w_out=m_swa_w_out, shared_w_kv=m_shared_w_kv, rel_bias=m_rel_bias,
              ffn_w_in=m_ffn_w_in, ffn_conv_w=m_ffn_conv_w, ffn_conv_b=m_ffn_conv_b, ffn_w_out=m_ffn_w_out,
              ln_mix_g=m_ln_mix_g, ln_mix_b=m_ln_mix_b, ln_ffn_g=m_ln_ffn_g, ln_ffn_b=m_ln_ffn_b)
    vs = dict(hgrn_w_in=v_hgrn_w_in, hgrn_lb_logits=v_hgrn_lb_logits, hgrn_gnorm_w=v_hgrn_gnorm_w, hgrn_w_out=v_hgrn_w_out,
              swa_w_q=v_swa_w_q, swa_sinks=v_swa_sinks, swa_w_out=v_swa_w_out, shared_w_kv=v_shared_w_kv, rel_bias=v_rel_bias,
              ffn_w_in=v_ffn_w_in, ffn_conv_w=v_ffn_conv_w, ffn_conv_b=v_ffn_conv_b, ffn_w_out=v_ffn_w_out,
              ln_mix_g=v_ln_mix_g, ln_mix_b=v_ln_mix_b, ln_ffn_g=v_ln_ffn_g, ln_ffn_b=v_ln_ffn_b)
    names = list(wts)
    grads, delta, new_m, new_v = {}, {}, {}, {}

    def update(n, ga, gb, sl=None):
        pick = (lambda a: a) if sl is None else (lambda a: a[sl])
        shp = pick(wts[n]).shape
        two_d = (-1, shp[-1])
        r2 = lambda a: a.reshape(two_d)
        res = _adamw(r2(pick(wts[n])), r2(ga), r2(gb), r2(pick(ms[n])), r2(pick(vs[n])),
                     name=f"adamw_{n}" + ("" if sl is None else f"_{sl}"))
        return [a.reshape(shp) for a in res]

    after = grad_x
    layer_parts = {}
    for k in (1, 2, 3):
        handle, own = sent[k]
        got3 = _scatter_wait(handle, after, name=f"scatter_g{k}_wait")
        part = _chip_sum(own, got3, name=f"scatter_g{k}_sum")
        part_sib = _swap_sibling(part, name=f"scatter_g{k}_swap")
        if k == 1:
            shapes = [(1, Dq, Dm), (1, Dq, Dm), (Dq, 2 * KV_DIM), (Dm, FC), (Fo, Dm)]
            a = _unpack_rows(part, shapes)
            b = _unpack_rows(part_sib, shapes)
            for n, ga, gb in zip(["swa_w_q", "swa_w_out", "shared_w_kv"], a[:3], b[:3]):
                grads[n], delta[n], new_m[n], new_v[n] = update(n, ga, gb)
            layer_parts[1] = [update("ffn_w_in", a[3], b[3], sl=1), update("ffn_w_out", a[4], b[4], sl=1)]
            after = layer_parts[1][1][3]
        elif k == 2:
            shapes = [(Dm, FC), (Fo, Dm), (1, Dq, Dm)]
            a = _unpack_rows(part, shapes)
            b = _unpack_rows(part_sib, shapes)
            layer_parts[0] = [update("ffn_w_in", a[0], b[0], sl=0), update("ffn_w_out", a[1], b[1], sl=0)]
            n = "hgrn_w_out"
            grads[n], delta[n], new_m[n], new_v[n] = update(n, a[2], b[2])
            after = new_v[n]
        else:
            n = "hgrn_w_in"
            grads[n], delta[n], new_m[n], new_v[n] = update(n, part.reshape(1, Dm, Dm), part_sib.reshape(1, Dm, Dm))
    for i, n in enumerate(["ffn_w_in", "ffn_w_out"]):
        grads[n], delta[n], new_m[n], new_v[n] = [jnp.stack([layer_parts[0][i][t], layer_parts[1][i][t]]) for t in range(4)]

    small_shapes = [(SUBLANES, LANES), (2, Dm), (1, HG_DIM), (1, SW_Q_HEADS), (REL_BUCKETS, SW_Q_HEADS),
                    (DEPTH, 3, 2 * FFN_DIM), (DEPTH, 2 * FFN_DIM)] + [(DEPTH, Dm)] * 4
    gc = g["conv"]
    conv_w_g = jnp.stack([jnp.concatenate([gc[l]["conv_w_a"], gc[l]["conv_w_b"]], axis=1) for l in range(DEPTH)])
    conv_b_g = jnp.concatenate([jnp.concatenate([gc[l]["conv_b_a"], gc[l]["conv_b_b"]], axis=1) for l in range(DEPTH)], axis=0)
    ln_g = [jnp.concatenate([g[f"{n}0"], g[f"{n}1"]], axis=0) for n in ("ln_mix_g", "ln_mix_b", "ln_ffn_g", "ln_ffn_b")]
    small_out = _sum8(_flat128([loss_tile, g["lb_logits"], g["gnorm"], g["sinks"], g["rel_bias"], conv_w_g, conv_b_g] + ln_g),
                      name="sum_small")
    (loss_t, g_lb, g_gn, g_sinks, g_rel, g_cw, g_cb, g_lmg, g_lmb, g_lfg, g_lfb) = _unflat128(small_out, small_shapes)
    loss = loss_t[0, 0]
    g_lb = lax.dynamic_slice_in_dim(g_lb, chip * Dq, Dq, axis=1)
    g_cw = lax.dynamic_slice_in_dim(g_cw, chip * FC, FC, axis=2)
    small_g = dict(hgrn_lb_logits=g_lb, hgrn_gnorm_w=g_gn, swa_sinks=g_sinks, rel_bias=g_rel, ffn_conv_w=g_cw,
                   ffn_conv_b=g_cb, ln_mix_g=g_lmg, ln_mix_b=g_lmb, ln_ffn_g=g_lfg, ln_ffn_b=g_lfb)
    small_names = list(small_g)
    sshapes = [wts[n].shape for n in small_names]
    _, d_, m_, v_ = _adamw(_flat128([wts[n] for n in small_names]), _flat128([small_g[n] for n in small_names]), None,
                           _flat128([ms[n] for n in small_names]), _flat128([vs[n] for n in small_names]), name="adamw_small")
    for n, a, b_, c_ in zip(small_names, _unflat128(d_, sshapes), _unflat128(m_, sshapes), _unflat128(v_, sshapes)):
        grads[n], delta[n], new_m[n], new_v[n] = small_g[n], a, b_, c_

    return (loss, grad_x[None], *[grads[n] for n in names], *[delta[n] for n in names],
            *[new_m[n] for n in names], *[new_v[n] for n in names])
```

```python
import functools
import math

import numpy as np
import jax
import jax.numpy as jnp
from jax import lax
from jax.experimental import pallas as pl
from jax.experimental.pallas import tpu as pltpu

F32 = jnp.float32
BF16 = jnp.bfloat16
MESH = pl.DeviceIdType.MESH

D_MODEL = 1024
DEPTH = 2
HG_HEADS = 8
HG_DIM = 128
SW_Q_HEADS = 16
SW_KV_HEADS = 4
SW_HEAD_DIM = 64
SW_GROUP = 4
SW_WINDOW = 128
REL_BUCKETS = 32
REL_MAX_DIST = 128
FFN_DIM = 2816
ALPHA = (2.0 * DEPTH) ** 0.25
LN_EPS = 1e-5
RMS_EPS = 1e-6
ADAM_LR = 0.001
ADAM_B1 = 0.9
ADAM_B2 = 0.999
ADAM_EPS = 1e-08
ADAM_WD = 0.01
ADAM_STEP = 10

VMEM_BYTES_V7X = 64 * 1024 * 1024
VMEM_LIMIT = VMEM_BYTES_V7X - 8 * 1024 * 1024
LANES = 128
SUBLANES = 8

HG_C = 64
HG_RB = 256
ROW_TILE = 256
CONV_R = 256
N_CHIPS = 4
N_DEV = 8

ANY_SPEC = pl.BlockSpec(memory_space=pl.ANY)


def _after(body, n_in, after):
    if after is None:
        return body, [], ()

    def wrapped(*refs):
        return body(*refs[:n_in], *refs[n_in + 1:])

    return wrapped, [ANY_SPEC], (after,)


def _params(sem=None):
    return pltpu.CompilerParams(dimension_semantics=sem, vmem_limit_bytes=VMEM_LIMIT)


def _tile(n, pref, unit=LANES):
    if n <= pref:
        return n
    best = None
    for t in range(unit, pref + 1, unit):
        if n % t == 0:
            best = t
    assert best is not None, (n, pref, unit)
    return best


def _dot(a, b, ca, cb):
    nb = a.ndim - 2
    batch = tuple(range(nb))
    return lax.dot_general(a.astype(BF16), b.astype(BF16), (((nb + ca,), (nb + cb,)), (batch, batch)),
                           preferred_element_type=F32)


@jax.custom_vjp
def mm(a, b):
    return _dot(a, b, 1, 0)


@jax.custom_vjp
def mm_nt(a, b):
    return _dot(a, b, 1, 1)


@jax.custom_vjp
def mm_tn(a, b):
    return _dot(a, b, 0, 0)


mm.defvjp(lambda a, b: (mm(a, b), (a, b)), lambda r, ct: (mm_nt(ct, r[1]), mm_tn(r[0], ct)))
mm_nt.defvjp(lambda a, b: (mm_nt(a, b), (a, b)), lambda r, ct: (mm(ct, r[1]), mm_tn(ct, r[0])))
mm_tn.defvjp(lambda a, b: (mm_tn(a, b), (a, b)), lambda r, ct: (mm_nt(r[1], ct), mm(r[0], ct)))


def _split2(x):
    hi = x.astype(BF16)
    return hi, (x - hi.astype(F32)).astype(BF16)


@jax.custom_vjp
def _scores(qt, kt):
    return _dot(qt, kt, 1, 1)


def _scores_bwd(r, ct):
    (qh, ql), (kh, kl) = _split2(r[0]), _split2(r[1])
    return _dot(ct, kh, 1, 0) + _dot(ct, kl, 1, 0), _dot(ct, qh, 0, 0) + _dot(ct, ql, 0, 0)


_scores.defvjp(lambda a, b: (_scores(a, b), (a, b)), _scores_bwd)


def _split3(x):
    hi = x.astype(BF16)
    r1 = x - hi.astype(F32)
    mid = r1.astype(BF16)
    lo = (r1 - mid.astype(F32)).astype(BF16)
    return hi, mid, lo


def _cumsum_impl(x):
    ax = x.ndim - 2
    n = x.shape[ax]
    row = lax.broadcasted_iota(jnp.int32, x.shape, ax)
    d = 1
    while d < n:
        x = x + jnp.where(row >= d, pltpu.roll(x, d, ax), 0.0)
        d *= 2
    return x


def _cumsum_rev_impl(x):
    ax = x.ndim - 2
    n = x.shape[ax]
    row = lax.broadcasted_iota(jnp.int32, x.shape, ax)
    d = 1
    while d < n:
        x = x + jnp.where(row < n - d, pltpu.roll(x, n - d, ax), 0.0)
        d *= 2
    return x


@jax.custom_vjp
def _cumsum(x):
    return _cumsum_impl(x)


_cumsum.defvjp(lambda x: (_cumsum_impl(x), None), lambda _, ct: (_cumsum_rev_impl(ct),))


def _matmul(a, b, *, mode, name, out_dtype=F32, add=None, add_scale=1.0, tm=512, tn=1408, tk=1408, after=None,
            split_n=False):
    if mode == "nn":
        (M, K), (K2, N) = a.shape, b.shape
    elif mode == "nt":
        (M, K), (N, K2) = a.shape, b.shape
    else:
        (K, M), (K2, N) = a.shape, b.shape
    assert K == K2, (a.shape, b.shape, mode)
    tm, tn, tk = _tile(M, tm), _tile(N, tn), _tile(K, tk)
    nk = K // tk
    ca, cb = {"nn": (1, 0), "nt": (1, 1), "tn": (0, 0)}[mode]
    a_spec = {"nn": pl.BlockSpec((tm, tk), lambda i, j, k: (i, k)),
              "nt": pl.BlockSpec((tm, tk), lambda i, j, k: (i, k)),
              "tn": pl.BlockSpec((tk, tm), lambda i, j, k: (k, i))}[mode]
    b_spec = {"nn": pl.BlockSpec((tk, tn), lambda i, j, k: (k, j)),
              "nt": pl.BlockSpec((tn, tk), lambda i, j, k: (j, k)),
              "tn": pl.BlockSpec((tk, tn), lambda i, j, k: (k, j))}[mode]
    o_spec = pl.BlockSpec((tm, tn), lambda i, j, k: (i, j))
    has_add = add is not None

    def finish(r, add_ref, o_ref):
        if has_add:
            r = r + add_scale * add_ref[...]
        o_ref[...] = r.astype(out_dtype)

    def body(*refs):
        a_ref, b_ref = refs[:2]
        add_ref = refs[2] if has_add else None
        o_ref = refs[3 if has_add else 2]
        if nk == 1:
            finish(_dot(a_ref[...], b_ref[...], ca, cb), add_ref, o_ref)
            return
        acc_ref = refs[-1]
        k = pl.program_id(2)

        @pl.when(k == 0)
        def _():
            acc_ref[...] = jnp.zeros_like(acc_ref)

        acc_ref[...] += _dot(a_ref[...], b_ref[...], ca, cb)

        @pl.when(k == nk - 1)
        def _():
            finish(acc_ref[...], add_ref, o_ref)

    in_specs = [a_spec, b_spec] + ([o_spec] if has_add else [])
    args = (a, b) + ((add,) if has_add else ())
    body, xs, xa = _after(body, len(args), after)
    in_specs, args = in_specs + xs, args + xa
    out_shape = (M, N)
    if split_n:
        assert not has_add and M == tm
        o_spec = pl.BlockSpec((None, tm, tn), lambda i, j, k: (j, 0, 0))
        out_shape = (N // tn, M, tn)
    return pl.pallas_call(
        body, name=name, grid=(M // tm, N // tn, nk), in_specs=in_specs, out_specs=o_spec,
        out_shape=jax.ShapeDtypeStruct(out_shape, out_dtype),
        scratch_shapes=[pltpu.VMEM((tm, tn), F32)] if nk > 1 else [],
        compiler_params=_params(("parallel", "parallel", "arbitrary")),
    )(*args)


def _ln(z, g, b):
    mu = jnp.mean(z, axis=-1, keepdims=True)
    zc = z - mu
    var = jnp.mean(zc * zc, axis=-1, keepdims=True)
    return zc * lax.rsqrt(var + LN_EPS) * g + b


def _ln_fwd(h, s, g, b, *, name):
    T, Dm = h.shape
    tr = _tile(T, ROW_TILE, SUBLANES)

    def body(h_ref, s_ref, g_ref, b_ref, y_ref, yb_ref):
        y = _ln(ALPHA * h_ref[...] + s_ref[...], g_ref[...], b_ref[...])
        y_ref[...] = y
        yb_ref[...] = y.astype(BF16)

    row = pl.BlockSpec((tr, Dm), lambda i: (i, 0))
    vec = pl.BlockSpec((1, Dm), lambda i: (0, 0))
    return pl.pallas_call(
        body, name=name, grid=(T // tr,), in_specs=[row, row, vec, vec], out_specs=[row, row],
        out_shape=[jax.ShapeDtypeStruct((T, Dm), F32), jax.ShapeDtypeStruct((T, Dm), BF16)],
        compiler_params=_params(("parallel",)),
    )(h, s, g, b)


def _ln_bwd(dy, h, s, g, b, *, name, after=None):
    T, Dm = h.shape
    tr = _tile(T, ROW_TILE, SUBLANES)

    def body(dy_ref, h_ref, s_ref, g_ref, b_ref, dz_ref, dzb_ref, dg_ref, db_ref):
        @pl.when(pl.program_id(0) == 0)
        def _():
            dg_ref[...] = jnp.zeros_like(dg_ref)
            db_ref[...] = jnp.zeros_like(db_ref)

        z = ALPHA * h_ref[...] + s_ref[...]
        _, vjp = jax.vjp(_ln, z, g_ref[...], b_ref[...])
        dz, dg, db = vjp(dy_ref[...])
        dz_ref[...] = dz
        dzb_ref[...] = dz.astype(BF16)
        dg_ref[...] += dg
        db_ref[...] += db

    row = pl.BlockSpec((tr, Dm), lambda i: (i, 0))
    vec = pl.BlockSpec((1, Dm), lambda i: (0, 0))
    body, xs, xa = _after(body, 5, after)
    return pl.pallas_call(
        body, name=name, grid=(T // tr,), in_specs=[row, row, row, vec, vec] + xs,
        out_specs=[row, row, vec, vec],
        out_shape=[jax.ShapeDtypeStruct((T, Dm), F32), jax.ShapeDtypeStruct((T, Dm), BF16),
                   jax.ShapeDtypeStruct((1, Dm), F32), jax.ShapeDtypeStruct((1, Dm), F32)],
        compiler_params=_params(("arbitrary",)),
    )(dy, h, s, g, b, *xa)


def _loss_grad(y, tgt, *, name):
    T, Dm = y.shape
    tr = _tile(T, ROW_TILE, SUBLANES)

    def body(y_ref, t_ref, dy_ref, l_ref):
        @pl.when(pl.program_id(0) == 0)
        def _():
            l_ref[...] = jnp.zeros_like(l_ref)

        e = y_ref[...] - t_ref[...]
        dy_ref[...] = e * (1.0 / Dm)
        l_ref[...] += 0.5 * jnp.sum(jnp.mean(e * e, axis=-1, keepdims=True), axis=0, keepdims=True)

    row = pl.BlockSpec((tr, Dm), lambda i: (i, 0))
    lsp = pl.BlockSpec((SUBLANES, LANES), lambda i: (0, 0))
    return pl.pallas_call(
        body, name=name, grid=(T // tr,), in_specs=[row, row], out_specs=[row, lsp],
        out_shape=[jax.ShapeDtypeStruct((T, Dm), F32), jax.ShapeDtypeStruct((SUBLANES, LANES), F32)],
        compiler_params=_params(("arbitrary",)),
    )(y, tgt)


def _hg_chunk(qr, fr, ir, gr, l0, l1, gw, st):
    C = qr.shape[-2]
    row = lax.broadcasted_iota(jnp.int32, qr.shape, qr.ndim - 2)
    lb = jax.nn.sigmoid(l0 - l1)
    fg = lb + (1.0 - lb) * jax.nn.sigmoid(fr)
    b = _cumsum(jnp.log(fg))
    q = jax.nn.silu(qr)
    k = 1.0 - fg
    bmid = lax.stop_gradient(jnp.sum(jnp.where(row == C // 2 - 1, b, 0.0), axis=-2, keepdims=True))
    bl = jnp.sum(jnp.where(row == C - 1, b, 0.0), axis=-2, keepdims=True)
    o = mm_nt(q * jnp.exp(b), st)
    sc = _scores(q * jnp.exp(b - bmid), k * jnp.exp(bmid - b))
    ti = lax.broadcasted_iota(jnp.int32, (C, C), 0)
    si = lax.broadcasted_iota(jnp.int32, (C, C), 1)
    sc = jnp.where(si <= ti, sc, 0.0)
    o = o + mm(sc, ir)
    st_new = st * jnp.exp(bl) + mm_tn(ir, k * jnp.exp(bl - b))
    on = o * lax.rsqrt(jnp.mean(o * o, axis=-1, keepdims=True) + RMS_EPS)
    return on * gw * jax.nn.silu(gr), st_new


def _heads(ref, rows):
    return jnp.stack([ref[rows, h * HG_DIM:(h + 1) * HG_DIM].astype(F32) for h in range(HG_HEADS)])


def _unheads(x):
    return jnp.concatenate([x[h] for h in range(HG_HEADS)], axis=-1)


def _hgrn_fwd(q, f, i, g, lbl, gw, *, name):
    T, Dm = q.shape
    rb = min(HG_RB, T)
    C = min(HG_C, rb)
    ncb = rb // C

    def body(q_ref, f_ref, i_ref, g_ref, lbl_ref, gw_ref, o_ref, st_ref, s_ref):
        @pl.when(pl.program_id(0) == 0)
        def _():
            s_ref[...] = jnp.zeros_like(s_ref)

        def chunk(ci, carry):
            r0 = pl.multiple_of(ci * C, C)
            rows = pl.ds(r0, C)
            st = s_ref[...]
            st_ref[ci] = st
            out, st_new = _hg_chunk(_heads(q_ref, rows), _heads(f_ref, rows), _heads(i_ref, rows), _heads(g_ref, rows),
                                    _heads(lbl_ref, slice(0, 1)), _heads(lbl_ref, slice(1, 2)), gw_ref[...], st)
            o_ref[rows, :] = _unheads(out).astype(BF16)
            s_ref[...] = st_new
            return carry

        lax.fori_loop(0, ncb, chunk, 0)

    row = pl.BlockSpec((rb, Dm), lambda n: (n, 0))
    return pl.pallas_call(
        body, name=name, grid=(T // rb,),
        in_specs=[row, row, row, row, pl.BlockSpec((2, Dm), lambda n: (0, 0)), pl.BlockSpec((1, HG_DIM), lambda n: (0, 0))],
        out_specs=[row, pl.BlockSpec((ncb, HG_HEADS, HG_DIM, HG_DIM), lambda n: (n, 0, 0, 0))],
        out_shape=[jax.ShapeDtypeStruct((T, Dm), BF16),
                   jax.ShapeDtypeStruct((T // C, HG_HEADS, HG_DIM, HG_DIM), F32)],
        scratch_shapes=[pltpu.VMEM((HG_HEADS, HG_DIM, HG_DIM), F32)],
        compiler_params=_params(("arbitrary",)),
    )(q, f, i, g, lbl, gw)


def _hgrn_bwd(q, f, i, g, lbl, gw, states, dout, *, name, after=None):
    T, Dm = q.shape
    rb = min(HG_RB, T)
    C = min(HG_C, rb)
    ncb = rb // C
    nb = T // rb

    def body(q_ref, f_ref, i_ref, g_ref, lbl_ref, gw_ref, st_ref, do_ref,
             dq_ref, df_ref, di_ref, dg_ref, dlbl_ref, dgw_ref, ds_ref):
        @pl.when(pl.program_id(0) == 0)
        def _():
            ds_ref[...] = jnp.zeros_like(ds_ref)
            dlbl_ref[...] = jnp.zeros_like(dlbl_ref)
            dgw_ref[...] = jnp.zeros_like(dgw_ref)

        def chunk(cj, carry):
            ci = ncb - 1 - cj
            r0 = pl.multiple_of(ci * C, C)
            rows = pl.ds(r0, C)
            _, vjp = jax.vjp(_hg_chunk, _heads(q_ref, rows), _heads(f_ref, rows), _heads(i_ref, rows), _heads(g_ref, rows),
                             _heads(lbl_ref, slice(0, 1)), _heads(lbl_ref, slice(1, 2)), gw_ref[...], st_ref[ci])
            dq, df, di, dg, dl0, dl1, dgw, dst = vjp((_heads(do_ref, rows).astype(F32), ds_ref[...]))
            dq_ref[rows, :] = _unheads(dq).astype(BF16)
            df_ref[rows, :] = _unheads(df).astype(BF16)
            di_ref[rows, :] = _unheads(di).astype(BF16)
            dg_ref[rows, :] = _unheads(dg).astype(BF16)
            dlbl_ref[0:1, :] += _unheads(dl0)
            dlbl_ref[1:2, :] += _unheads(dl1)
            dgw_ref[...] += dgw
            ds_ref[...] = dst
            return carry

        lax.fori_loop(0, ncb, chunk, 0)

    row = pl.BlockSpec((rb, Dm), lambda n: (nb - 1 - n, 0))
    lsp = pl.BlockSpec((2, Dm), lambda n: (0, 0))
    gsp = pl.BlockSpec((1, HG_DIM), lambda n: (0, 0))
    body, xs, xa = _after(body, 8, after)
    return pl.pallas_call(
        body, name=name, grid=(nb,),
        in_specs=[row, row, row, row, lsp, gsp,
                  pl.BlockSpec((ncb, HG_HEADS, HG_DIM, HG_DIM), lambda n: (nb - 1 - n, 0, 0, 0)), row] + xs,
        out_specs=[row, row, row, row, lsp, gsp],
        out_shape=[jax.ShapeDtypeStruct((T, Dm), BF16)] * 4
        + [jax.ShapeDtypeStruct((2, Dm), F32), jax.ShapeDtypeStruct((1, HG_DIM), F32)],
        scratch_shapes=[pltpu.VMEM((HG_HEADS, HG_DIM, HG_DIM), F32)],
        compiler_params=_params(("arbitrary",)),
    )(q, f, i, g, lbl, gw, states, dout, *xa)


def _shift_down(cur, prev):
    h = prev.shape[0]
    big = jnp.concatenate([prev, cur], axis=0)
    return pltpu.roll(big, 1, 0)[h:], pltpu.roll(big, 2, 0)[h:]


def _shift_up(cur, next8):
    n = cur.shape[0] + SUBLANES
    big = jnp.concatenate([cur, next8], axis=0)
    return pltpu.roll(big, n - 1, 0)[:cur.shape[0]], pltpu.roll(big, n - 2, 0)[:cur.shape[0]]


def _conv_rows(u_ref, w, bias, r0, R):
    halo = 2 * SUBLANES
    cur = u_ref[pl.ds(r0, R), :].astype(F32)
    p0 = pl.multiple_of(jnp.maximum(r0 - halo, 0), halo)
    prev = jnp.where(r0 > 0, u_ref[pl.ds(p0, halo), :].astype(F32), 0.0)
    s1, s2 = _shift_down(cur, prev)
    return w[0:1, :] * s2 + w[1:2, :] * s1 + w[2:3, :] * cur + bias, cur, s1, s2


def _conv_gate_fwd(ua, ub, wa, wb, ba, bb, *, name):
    T, Fd = ua.shape
    R = min(CONV_R, T)
    tc = LANES

    def body(ua_ref, ub_ref, wa_ref, wb_ref, ba_ref, bb_ref, o_ref):
        wa_, wb_, ba_, bb_ = wa_ref[...], wb_ref[...], ba_ref[...], bb_ref[...]

        def step(ri, carry):
            r0 = pl.multiple_of(ri * R, R)
            ca = _conv_rows(ua_ref, wa_, ba_, r0, R)[0]
            cb = _conv_rows(ub_ref, wb_, bb_, r0, R)[0]
            o_ref[pl.ds(r0, R), :] = (jax.nn.silu(ca) * cb).astype(BF16)
            return carry

        lax.fori_loop(0, T // R, step, 0)

    col = pl.BlockSpec((T, tc), lambda j: (0, j))
    wsp = pl.BlockSpec((3, tc), lambda j: (0, j))
    bsp = pl.BlockSpec((1, tc), lambda j: (0, j))
    return pl.pallas_call(
        body, name=name, grid=(Fd // tc,), in_specs=[col, col, wsp, wsp, bsp, bsp], out_specs=col,
        out_shape=jax.ShapeDtypeStruct((T, Fd), BF16),
        compiler_params=_params(("parallel",)),
    )(ua, ub, wa, wb, ba, bb)


def _conv_gate_bwd(ua, ub, wa, wb, ba, bb, dact, *, name):
    T, Fd = ua.shape
    R = min(CONV_R, T)
    nr = T // R
    tc = LANES

    def body(ua_ref, ub_ref, wa_ref, wb_ref, ba_ref, bb_ref, da_ref,
             dua_ref, dub_ref, dwa_ref, dwb_ref, dba_ref, dbb_ref, dca_ref, dcb_ref):
        wa_, wb_, ba_, bb_ = wa_ref[...], wb_ref[...], ba_ref[...], bb_ref[...]

        def taps(dc, cur, s1, s2):
            return jnp.concatenate([jnp.sum(dc * s2, axis=0, keepdims=True), jnp.sum(dc * s1, axis=0, keepdims=True),
                                    jnp.sum(dc * cur, axis=0, keepdims=True)], axis=0)

        def first(ri, carry):
            dwa, dwb, dba, dbb = carry
            r0 = pl.multiple_of(ri * R, R)
            ca, cura, s1a, s2a = _conv_rows(ua_ref, wa_, ba_, r0, R)
            cb, curb, s1b, s2b = _conv_rows(ub_ref, wb_, bb_, r0, R)
            dact_ = da_ref[pl.ds(r0, R), :].astype(F32)
            sg = jax.nn.sigmoid(ca)
            dca = dact_ * cb * (sg * (1.0 + ca * (1.0 - sg)))
            dcb = dact_ * (ca * sg)
            dca_ref[pl.ds(r0, R), :] = dca
            dcb_ref[pl.ds(r0, R), :] = dcb
            return (dwa + taps(dca, cura, s1a, s2a), dwb + taps(dcb, curb, s1b, s2b),
                    dba + jnp.sum(dca, axis=0, keepdims=True), dbb + jnp.sum(dcb, axis=0, keepdims=True))

        z3 = jnp.zeros((3, tc), F32)
        z1 = jnp.zeros((1, tc), F32)
        dwa, dwb, dba, dbb = lax.fori_loop(0, nr, first, (z3, z3, z1, z1))
        dwa_ref[...] = dwa
        dwb_ref[...] = dwb
        dba_ref[...] = dba
        dbb_ref[...] = dbb

        def du_rows(dc_ref, w, r0):
            cur = dc_ref[pl.ds(r0, R), :]
            n0 = pl.multiple_of(jnp.minimum(r0 + R, T - SUBLANES), SUBLANES)
            next8 = jnp.where(r0 + R < T, dc_ref[pl.ds(n0, SUBLANES), :], 0.0)
            m1, m2 = _shift_up(cur, next8)
            return w[2:3, :] * cur + w[1:2, :] * m1 + w[0:1, :] * m2

        def second(ri, carry):
            r0 = pl.multiple_of(ri * R, R)
            dua_ref[pl.ds(r0, R), :] = du_rows(dca_ref, wa_, r0).astype(BF16)
            dub_ref[pl.ds(r0, R), :] = du_rows(dcb_ref, wb_, r0).astype(BF16)
            return carry

        lax.fori_loop(0, nr, second, 0)

    col = pl.BlockSpec((T, tc), lambda j: (0, j))
    wsp = pl.BlockSpec((3, tc), lambda j: (0, j))
    bsp = pl.BlockSpec((1, tc), lambda j: (0, j))
    return pl.pallas_call(
        body, name=name, grid=(Fd // tc,), in_specs=[col, col, wsp, wsp, bsp, bsp, col],
        out_specs=[col, col, wsp, wsp, bsp, bsp],
        out_shape=[jax.ShapeDtypeStruct((T, Fd), BF16)] * 2 + [jax.ShapeDtypeStruct((3, Fd), F32)] * 2
        + [jax.ShapeDtypeStruct((1, Fd), F32)] * 2,
        scratch_shapes=[pltpu.VMEM((T, tc), F32), pltpu.VMEM((T, tc), F32)],
        compiler_params=_params(("parallel",)),
    )(ua, ub, wa, wb, ba, bb, dact)


def _bucket_index():
    t = np.arange(SW_WINDOW)[:, None] + SW_WINDOW
    s = np.arange(2 * SW_WINDOW)[None, :]
    dist = np.maximum(t - s, 0)
    exact = REL_BUCKETS // 2
    d = np.maximum(dist, 1).astype(np.float32)
    log_b = exact + (np.log(d / np.float32(exact)) / np.float32(math.log(REL_MAX_DIST / exact))
                     * np.float32(REL_BUCKETS - exact)).astype(np.int32)
    bucket = np.where(dist < exact, dist, np.minimum(log_b, REL_BUCKETS - 1))
    return bucket.astype(np.int32).reshape(1, -1)


BIAS_COLS = SW_WINDOW * 2 * SW_WINDOW
BIAS_TILE = 4096


def _bias_from_table(table, bucket, *, name):
    def body(t_ref, idx_ref, o_ref):
        onehot = (lax.broadcasted_iota(jnp.int32, (REL_BUCKETS, BIAS_TILE), 0) == idx_ref[...]).astype(BF16)
        acc = jnp.zeros((SW_Q_HEADS, BIAS_TILE), F32)
        for piece in _split3(t_ref[...]):
            acc = acc + lax.dot_general(piece, onehot, (((0,), (0,)), ((), ())), preferred_element_type=F32)
        o_ref[...] = acc

    return pl.pallas_call(
        body, name=name, grid=(BIAS_COLS // BIAS_TILE,),
        in_specs=[pl.BlockSpec((REL_BUCKETS, SW_Q_HEADS), lambda j: (0, 0)), pl.BlockSpec((1, BIAS_TILE), lambda j: (0, j))],
        out_specs=pl.BlockSpec((SW_Q_HEADS, BIAS_TILE), lambda j: (0, j)),
        out_shape=jax.ShapeDtypeStruct((SW_Q_HEADS, BIAS_COLS), F32),
        compiler_params=_params(("parallel",)),
    )(table, bucket)


def _table_grad(dbias, bucket, *, name):
    def body(d_ref, idx_ref, o_ref):
        @pl.when(pl.program_id(0) == 0)
        def _():
            o_ref[...] = jnp.zeros_like(o_ref)

        onehot = (lax.broadcasted_iota(jnp.int32, (REL_BUCKETS, BIAS_TILE), 0) == idx_ref[...]).astype(BF16)
        acc = jnp.zeros((REL_BUCKETS, SW_Q_HEADS), F32)
        for piece in _split3(d_ref[...]):
            acc = acc + lax.dot_general(onehot, piece, (((1,), (1,)), ((), ())), preferred_element_type=F32)
        o_ref[...] += acc

    return pl.pallas_call(
        body, name=name, grid=(BIAS_COLS // BIAS_TILE,),
        in_specs=[pl.BlockSpec((SW_Q_HEADS, BIAS_TILE), lambda j: (0, j)), pl.BlockSpec((1, BIAS_TILE), lambda j: (0, j))],
        out_specs=pl.BlockSpec((REL_BUCKETS, SW_Q_HEADS), lambda j: (0, 0)),
        out_shape=jax.ShapeDtypeStruct((REL_BUCKETS, SW_Q_HEADS), F32),
        compiler_params=_params(("arbitrary",)),
    )(dbias, bucket)


def _band_mask(n):
    rows = SW_GROUP * SW_WINDOW
    t = (lax.broadcasted_iota(jnp.int32, (rows, 2 * SW_WINDOW), 0) & (SW_WINDOW - 1)) + SW_WINDOW
    s = lax.broadcasted_iota(jnp.int32, (rows, 2 * SW_WINDOW), 1)
    dist = t - s
    return (dist >= 0) & (dist < SW_WINDOW) & ((n > 0) | (s >= SW_WINDOW))


def _head_cols(h):
    return slice(h * SW_HEAD_DIM, (h + 1) * SW_HEAD_DIM)


def _group_inputs(q_ref, bias_ref, sink_ref, g):
    heads = range(g * SW_GROUP, (g + 1) * SW_GROUP)
    q = jnp.concatenate([q_ref[:, _head_cols(h)] for h in heads], axis=0)
    sink = jnp.concatenate([jnp.broadcast_to(sink_ref[:, h:h + 1], (SW_WINDOW, 1)) for h in heads], axis=0)
    bias = bias_ref[g * SW_GROUP:(g + 1) * SW_GROUP].reshape(SW_GROUP * SW_WINDOW, 2 * SW_WINDOW)
    return heads, q, bias, sink


KV_DIM = SW_KV_HEADS * SW_HEAD_DIM


def _kv_pair(kvp_ref, kvc_ref, g):
    ks = slice(g * SW_HEAD_DIM, (g + 1) * SW_HEAD_DIM)
    vs = slice(KV_DIM + g * SW_HEAD_DIM, KV_DIM + (g + 1) * SW_HEAD_DIM)
    kk = jnp.concatenate([kvp_ref[:, ks], kvc_ref[:, ks]], axis=0)
    vv = jnp.concatenate([kvp_ref[:, vs], kvc_ref[:, vs]], axis=0)
    return kk, vv, ks, vs


def _attn_fwd(q1, kv, bias, sinks, *, name):
    T, Dm = q1.shape
    W = SW_WINDOW

    def body(q_ref, kvc_ref, kvp_ref, bias_ref, sink_ref, o_ref):
        mask = _band_mask(pl.program_id(0))
        G = range(SW_KV_HEADS)
        ins = [_group_inputs(q_ref, bias_ref, sink_ref, g) for g in G]
        kvs = [_kv_pair(kvp_ref, kvc_ref, g) for g in G]
        lg = [jnp.where(mask, mm_nt(ins[g][1], kvs[g][0]) * (SW_HEAD_DIM ** -0.5) + ins[g][2], -jnp.inf) for g in G]
        m = [jnp.maximum(jnp.max(lg[g], axis=-1, keepdims=True), ins[g][3]) for g in G]
        p = [jnp.exp(lg[g] - m[g]) for g in G]
        den = [jnp.sum(p[g], axis=-1, keepdims=True) + jnp.exp(ins[g][3] - m[g]) for g in G]
        o = [mm(p[g], kvs[g][1]) / den[g] for g in G]
        for g in G:
            for r, h in enumerate(ins[g][0]):
                o_ref[:, _head_cols(h)] = o[g][r * W:(r + 1) * W].astype(BF16)

    return pl.pallas_call(
        body, name=name, grid=(T // W,),
        in_specs=[pl.BlockSpec((W, Dm), lambda n: (n, 0)),
                  pl.BlockSpec((W, 2 * KV_DIM), lambda n: (n, 0)),
                  pl.BlockSpec((W, 2 * KV_DIM), lambda n: (jnp.maximum(n - 1, 0), 0)),
                  pl.BlockSpec((SW_Q_HEADS, W, 2 * W), lambda n: (0, 0, 0)),
                  pl.BlockSpec((1, SW_Q_HEADS), lambda n: (0, 0))],
        out_specs=pl.BlockSpec((W, Dm), lambda n: (n, 0)),
        out_shape=jax.ShapeDtypeStruct((T, Dm), BF16),
        compiler_params=_params(("parallel",)),
    )(q1, kv, kv, bias, sinks)


def _attn_bwd(q1, kv, bias, sinks, do, *, name):
    T, Dm = q1.shape
    W = SW_WINDOW
    nb = T // W

    def body(q_ref, kvc_ref, kvp_ref, bias_ref, sink_ref, do_ref,
             dq_ref, dkv_ref, dbias_ref, dsink_ref, carry_ref):
        @pl.when(pl.program_id(0) == 0)
        def _():
            carry_ref[...] = jnp.zeros_like(carry_ref)
            dbias_ref[...] = jnp.zeros_like(dbias_ref)
            dsink_ref[...] = jnp.zeros_like(dsink_ref)

        n = nb - 1 - pl.program_id(0)
        mask = _band_mask(n)
        lane = lax.broadcasted_iota(jnp.int32, (1, SW_Q_HEADS), 1)
        sc = SW_HEAD_DIM ** -0.5
        G = range(SW_KV_HEADS)
        ins = [_group_inputs(q_ref, bias_ref, sink_ref, g) for g in G]
        kvs = [_kv_pair(kvp_ref, kvc_ref, g) for g in G]
        do = [jnp.concatenate([do_ref[:, _head_cols(h)] for h in ins[g][0]], axis=0) for g in G]
        lg = [jnp.where(mask, mm_nt(ins[g][1], kvs[g][0]) * sc + ins[g][2], -jnp.inf) for g in G]
        m = [jnp.maximum(jnp.max(lg[g], axis=-1, keepdims=True), ins[g][3]) for g in G]
        p = [jnp.exp(lg[g] - m[g]) for g in G]
        ps = [jnp.exp(ins[g][3] - m[g]) for g in G]
        rden = [1.0 / (jnp.sum(p[g], axis=-1, keepdims=True) + ps[g]) for g in G]
        pn = [p[g] * rden[g] for g in G]
        dpn = [mm_nt(do[g], kvs[g][1]) for g in G]
        delta = [jnp.sum(pn[g] * dpn[g], axis=-1, keepdims=True) for g in G]
        ds = [pn[g] * (dpn[g] - delta[g]) for g in G]
        dsr = [-(ps[g] * rden[g]) * delta[g] for g in G]
        dq = [mm(ds[g], kvs[g][0]) * sc for g in G]
        dkk = [mm_tn(ds[g], ins[g][1]) * sc for g in G]
        dvv = [mm_tn(pn[g], do[g]) for g in G]
        dsink = jnp.zeros((1, SW_Q_HEADS), F32)
        for g in G:
            _, _, ks, vs = kvs[g]
            dbias_ref[g * SW_GROUP:(g + 1) * SW_GROUP] += ds[g].reshape(SW_GROUP, W, 2 * W)
            for r, h in enumerate(ins[g][0]):
                dq_ref[:, _head_cols(h)] = dq[g][r * W:(r + 1) * W].astype(BF16)
                dsink = dsink + jnp.where(lane == h, jnp.sum(dsr[g][r * W:(r + 1) * W], axis=0, keepdims=True), 0.0)
            dkv_ref[:, ks] = (carry_ref[:, ks] + dkk[g][W:]).astype(BF16)
            dkv_ref[:, vs] = (carry_ref[:, vs] + dvv[g][W:]).astype(BF16)
            carry_ref[:, ks] = dkk[g][:W]
            carry_ref[:, vs] = dvv[g][:W]
        dsink_ref[...] += dsink

    rev = lambda n: (nb - 1 - n, 0)
    return pl.pallas_call(
        body, name=name, grid=(nb,),
        in_specs=[pl.BlockSpec((W, Dm), rev),
                  pl.BlockSpec((W, 2 * KV_DIM), rev),
                  pl.BlockSpec((W, 2 * KV_DIM), lambda n: (jnp.maximum(nb - 2 - n, 0), 0)),
                  pl.BlockSpec((SW_Q_HEADS, W, 2 * W), lambda n: (0, 0, 0)),
                  pl.BlockSpec((1, SW_Q_HEADS), lambda n: (0, 0)),
                  pl.BlockSpec((W, Dm), rev)],
        out_specs=[pl.BlockSpec((W, Dm), rev), pl.BlockSpec((W, 2 * KV_DIM), rev),
                   pl.BlockSpec((SW_Q_HEADS, W, 2 * W), lambda n: (0, 0, 0)),
                   pl.BlockSpec((1, SW_Q_HEADS), lambda n: (0, 0))],
        out_shape=[jax.ShapeDtypeStruct((T, Dm), BF16), jax.ShapeDtypeStruct((T, 2 * KV_DIM), BF16),
                   jax.ShapeDtypeStruct((SW_Q_HEADS, W, 2 * W), F32), jax.ShapeDtypeStruct((1, SW_Q_HEADS), F32)],
        scratch_shapes=[pltpu.VMEM((W, 2 * KV_DIM), F32)],
        compiler_params=_params(("arbitrary",)),
    )(q1, kv, kv, bias, sinks, do)


def _ffn_fwd(hb, w, l, after=None):
    ua = _matmul(hb, w["ffn_in_a"][l], mode="nn", out_dtype=BF16, name=f"ffn{l}_up_a", tm=1024, after=after)
    ub = _matmul(hb, w["ffn_in_b"][l], mode="nn", out_dtype=BF16, name=f"ffn{l}_up_b", tm=1024)
    act = _conv_gate_fwd(ua, ub, w["conv_w_a"][l], w["conv_w_b"][l], w["conv_b_a"][l], w["conv_b_b"][l],
                         name=f"ffn{l}_conv_gate")
    ff = _matmul(act, w["ffn_out"][l], mode="nn", name=f"ffn{l}_down", tn=1024, tk=FFN_DIM)
    return ua, ub, act, ff


def _ffn_bwd(dffb, dh_scaled, hb, ua, ub, act, w, l):
    dact = _matmul(dffb, w["ffn_out"][l], mode="nt", out_dtype=BF16, name=f"ffn{l}_down_dx", tm=1024)
    g_out = _matmul(act, dffb, mode="tn", name=f"ffn{l}_down_dw", tm=1408, tn=1024, tk=1024)
    dua, dub, dwa, dwb, dba, dbb = _conv_gate_bwd(ua, ub, w["conv_w_a"][l], w["conv_w_b"][l], w["conv_b_a"][l],
                                                  w["conv_b_b"][l], dact, name=f"ffn{l}_conv_gate_bwd")
    dh = _matmul(dua, w["ffn_in_a"][l], mode="nt", add=dh_scaled, add_scale=ALPHA, name=f"ffn{l}_up_a_dx",
                 tn=1024, tk=FFN_DIM)
    dh = _matmul(dub, w["ffn_in_b"][l], mode="nt", add=dh, name=f"ffn{l}_up_b_dx", tn=1024, tk=FFN_DIM)
    g_in_a = _matmul(hb, dua, mode="tn", name=f"ffn{l}_up_a_dw", tm=1024, tn=FFN_DIM // 2, tk=1024, split_n=True)
    g_in_b = _matmul(hb, dub, mode="tn", name=f"ffn{l}_up_b_dw", tm=1024, tn=FFN_DIM // 2, tk=1024, split_n=True)
    return dh, dict(ffn_out=g_out, ffn_in_a=g_in_a, ffn_in_b=g_in_b, conv_w_a=dwa, conv_w_b=dwb, conv_b_a=dba, conv_b_b=dbb)


def _local_step(x, tgt, w, more_weights, emit):
    bucket = jnp.asarray(_bucket_index())
    xb = x.astype(BF16)

    pre = [_matmul(xb, w["hg_in"][j], mode="nn", out_dtype=BF16, name=f"hg_in_{j}", tm=1024, tn=1024,
                   after=w.get("token") if j == 0 else None) for j in range(4)]
    og, states = _hgrn_fwd(*pre, w["lb_logits"], w["gnorm"], name="hgrn_fwd")
    mix0 = _matmul(og, w["hg_out"], mode="nn", name="hg_out", tn=1024)
    h1, h1b = _ln_fwd(x, mix0, w["ln_mix_g"][0], w["ln_mix_b"][0], name="ln_mix0")
    w = {**w, **more_weights(1, h1b)}
    ua0, ub0, act0, ff0 = _ffn_fwd(h1b, w, 0, after=w.get("token"))
    h2, h2b = _ln_fwd(h1, ff0, w["ln_ffn_g"][0], w["ln_ffn_b"][0], name="ln_ffn0")
    kv = _matmul(h2b, w["kv"], mode="nn", name="kv_proj")

    bias = _bias_from_table(w["rel_bias"], bucket, name="rel_bias_expand").reshape(SW_Q_HEADS, SW_WINDOW, 2 * SW_WINDOW)
    q1 = _matmul(h2b, w["sw_q"], mode="nn", name="sw_q")
    o1 = _attn_fwd(q1, kv, bias, w["sinks"], name="attn_fwd")
    mix1 = _matmul(o1, w["sw_out"], mode="nn", name="sw_out")
    h3, h3b = _ln_fwd(h2, mix1, w["ln_mix_g"][1], w["ln_mix_b"][1], name="ln_mix1")
    w = {**w, **more_weights(2, h3b)}
    ua1, ub1, act1, ff1 = _ffn_fwd(h3b, w, 1)
    y, _ = _ln_fwd(h3, ff1, w["ln_ffn_g"][1], w["ln_ffn_b"][1], name="ln_ffn1")

    dy, loss_tile = _loss_grad(y, tgt, name="loss_grad")

    g = {}
    dz, dzb, dg_, db_ = _ln_bwd(dy, h3, ff1, w["ln_ffn_g"][1], w["ln_ffn_b"][1], name="ln_ffn1_bwd")
    g["ln_ffn_g1"], g["ln_ffn_b1"] = dg_, db_
    dh3, gf1 = _ffn_bwd(dzb, dz, h3b, ua1, ub1, act1, w, 1)
    dz, dzb, dg_, db_ = _ln_bwd(dh3, h2, mix1, w["ln_mix_g"][1], w["ln_mix_b"][1], name="ln_mix1_bwd")
    g["ln_mix_g1"], g["ln_mix_b1"] = dg_, db_
    do1 = _matmul(dzb, w["sw_out"], mode="nt", out_dtype=BF16, name="sw_out_dx")
    g_sw_out = _matmul(o1, dzb, mode="tn", name="sw_out_dw", tm=1024, tn=1024, tk=1024)
    dq1, dkv, dbias, dsinks = _attn_bwd(q1, kv, bias, w["sinks"], do1, name="attn_bwd")
    g["sinks"] = dsinks
    g["rel_bias"] = _table_grad(dbias.reshape(SW_Q_HEADS, BIAS_COLS), bucket, name="rel_bias_grad")
    dh2 = _matmul(dq1, w["sw_q"], mode="nt", add=dz, add_scale=ALPHA, name="sw_q_dx", tn=1024)
    dh2 = _matmul(dkv, w["kv"], mode="nt", add=dh2, name="kv_dx", tn=1024)
    g_sw_q = _matmul(h2b, dq1, mode="tn", name="sw_q_dw", tm=1024, tn=1024, tk=1024)
    g_kv = _matmul(h2b, dkv, mode="tn", name="kv_dw", tm=1024, tn=512, tk=1024)
    tok = emit(1, dict(sw_q=g_sw_q, sw_out=g_sw_out, kv=g_kv, ffn_in_a=gf1["ffn_in_a"], ffn_in_b=gf1["ffn_in_b"],
                       ffn_out=gf1["ffn_out"]))

    dz, dzb, dg_, db_ = _ln_bwd(dh2, h1, ff0, w["ln_ffn_g"][0], w["ln_ffn_b"][0], name="ln_ffn0_bwd", after=tok)
    g["ln_ffn_g0"], g["ln_ffn_b0"] = dg_, db_
    dh1, gf0 = _ffn_bwd(dzb, dz, h1b, ua0, ub0, act0, w, 0)
    dz, dzb, dg_, db_ = _ln_bwd(dh1, x, mix0, w["ln_mix_g"][0], w["ln_mix_b"][0], name="ln_mix0_bwd")
    g["ln_mix_g0"], g["ln_mix_b0"] = dg_, db_
    dog = _matmul(dzb, w["hg_out"], mode="nt", out_dtype=BF16, name="hg_out_dx")
    g_hg_out = _matmul(og, dzb, mode="tn", name="hg_out_dw", tm=1024, tn=1024, tk=1024)
    tok = emit(2, dict(hg_out=g_hg_out, ffn_in_a=gf0["ffn_in_a"], ffn_in_b=gf0["ffn_in_b"], ffn_out=gf0["ffn_out"]))
    dpre = _hgrn_bwd(*pre, w["lb_logits"], w["gnorm"], states, dog, name="hgrn_bwd", after=tok)
    g["lb_logits"], g["gnorm"] = dpre[4], dpre[5]
    tok = emit(3, dict(hg_in=[_matmul(xb, dpre[j], mode="tn", name=f"hg_in_{j}_dw", tm=1024, tn=1024, tk=1024)
                              for j in range(4)]))
    dx = dz
    for j in range(4):
        dx = _matmul(dpre[j], w["hg_in"][j], mode="nt", add=dx, add_scale=ALPHA if j == 0 else 1.0,
                     name=f"hg_in_{j}_dx", tn=1024, after=tok if j == 0 else None)
    g["conv"] = [{k: gf[k] for k in ("conv_w_a", "conv_w_b", "conv_b_a", "conv_b_b")} for gf in (gf0, gf1)]
    return loss_tile, dx, g


def _adamw(wt, ga, gb, m, v, *, name, rows=None, prev=None):
    R, Cc = wt.shape
    r0, n = rows if rows is not None else (0, R)
    tr = _tile(n, 256, SUBLANES) if n % SUBLANES == 0 else n
    assert r0 % tr == 0
    c1 = 1.0 - ADAM_B1 ** ADAM_STEP
    c2 = 1.0 - ADAM_B2 ** ADAM_STEP
    two = gb is not None
    n_in = 5 if two else 4

    def body(*refs):
        if two:
            w_ref, ga_ref, gb_ref, m_ref, v_ref = refs[:5]
            g_ = ga_ref[...] + gb_ref[...]
        else:
            w_ref, ga_ref, m_ref, v_ref = refs[:4]
            g_ = ga_ref[...]
        g_ref, d_ref, nm_ref, nv_ref = refs[-4:]
        nm = ADAM_B1 * m_ref[...] + (1.0 - ADAM_B1) * g_
        nv = ADAM_B2 * v_ref[...] + (1.0 - ADAM_B2) * (g_ * g_)
        g_ref[...] = g_
        d_ref[...] = -ADAM_LR * ((nm / c1) / (jnp.sqrt(nv / c2) + ADAM_EPS) + ADAM_WD * w_ref[...])
        nm_ref[...] = nm
        nv_ref[...] = nv

    full = pl.BlockSpec((tr, Cc), lambda i: (i + r0 // tr, 0))
    part = pl.BlockSpec((tr, Cc), lambda i: (i, 0))
    args = (wt, ga, gb, m, v) if two else (wt, ga, m, v)
    in_specs = [full] + [part] * (n_in - 3) + [full, full]
    aliases = {}
    if prev is not None:
        args, in_specs = args + tuple(prev), in_specs + [ANY_SPEC] * 4
        aliases = {n_in + t: t for t in range(4)}
    return pl.pallas_call(
        body, name=name, grid=(n // tr,), in_specs=in_specs, out_specs=[full] * 4,
        out_shape=[jax.ShapeDtypeStruct((R, Cc), F32)] * 4, input_output_aliases=aliases,
        compiler_params=_params(("parallel",)),
    )(*args)


HBM_SPEC = pl.BlockSpec(memory_space=pltpu.HBM)
SEM_SPEC = pl.BlockSpec(memory_space=pltpu.SEMAPHORE)
VMEM_SPEC = pl.BlockSpec(memory_space=pltpu.VMEM)
DATAFLOW = pltpu.SideEffectType.DATAFLOW_SIDE_EFFECTING


def _in_hbm(a):
    return pltpu.with_memory_space_constraint(a, pltpu.HBM)


def _place():
    return lax.axis_index("x"), lax.axis_index("y"), lax.axis_index("c")


def _other_chips(x, y):
    return [(1 - x, y), (x, 1 - y), (1 - x, 1 - y)]


def _sum8(v, *, name):
    r = v.shape[0]

    def body(v_ref, all_ref, o_ref, send_sems, recv_sems, local_sem):
        x, y, c = _place()
        me, sibling = (x, y, c), (x, y, 1 - c)
        chips = _other_chips(x, y)

        def rows(px, py, pc):
            return all_ref.at[pl.ds((4 * px + 2 * py + pc) * r, r), :]

        def copy(k, block, to, src=None):
            return pltpu.make_async_remote_copy(
                src_ref=rows(*block) if src is None else src, dst_ref=rows(*block),
                send_sem=send_sems.at[k], recv_sem=recv_sems.at[k], device_id=to, device_id_type=MESH)

        mine = pltpu.make_async_copy(v_ref, rows(*me), local_sem)
        mine.start()
        first = [copy(0, me, sibling, src=v_ref)]
        first += [copy(1 + j, me, (*chip, c), src=v_ref) for j, chip in enumerate(chips)]
        for cp in first:
            cp.start()
        passed = [copy(4 + j, (*chip, c), sibling) for j, chip in enumerate(chips)]
        for j, chip in enumerate(chips):
            copy(1 + j, (*chip, c), me).wait_recv()
            passed[j].start()
        copy(0, sibling, me).wait_recv()
        for j, chip in enumerate(chips):
            copy(4 + j, (*chip, 1 - c), me).wait_recv()
        for cp in first + passed:
            cp.wait_send()
        mine.wait()
        acc = all_ref[pl.ds(0, r), :]
        for d in range(1, N_DEV):
            acc = acc + all_ref[pl.ds(d * r, r), :]
        o_ref[...] = acc

    return pl.pallas_call(
        body, name=name, in_specs=[VMEM_SPEC], out_specs=[VMEM_SPEC, VMEM_SPEC],
        out_shape=[jax.ShapeDtypeStruct((N_DEV * r, LANES), F32), jax.ShapeDtypeStruct((r, LANES), F32)],
        scratch_shapes=[pltpu.SemaphoreType.DMA((7,)), pltpu.SemaphoreType.DMA((7,)), pltpu.SemaphoreType.DMA],
        compiler_params=pltpu.CompilerParams(vmem_limit_bytes=VMEM_LIMIT),
    )(v)[1]


def _gather_chips(shard, *, name):
    R, Cc = shard.shape
    half = R // 2
    assert half * 2 == R

    def body(s_ref, o_ref, send_sems, recv_sems, local_sem):
        x, y, c = _place()
        sibling = (x, y, 1 - c)
        chips = _other_chips(x, y)

        def part(px, py, pc):
            return o_ref.at[2 * px + py, pl.ds(pc * half, half), :]

        def copy(k, block, to, src=None):
            return pltpu.make_async_remote_copy(
                src_ref=part(*block) if src is None else src, dst_ref=part(*block),
                send_sem=send_sems.at[k], recv_sem=recv_sems.at[k], device_id=to, device_id_type=MESH)

        mine = pltpu.make_async_copy(s_ref, o_ref.at[2 * x + y], local_sem)
        mine.start()
        my_half = s_ref.at[pl.ds(c * half, half), :]
        first = [copy(j, (x, y, c), (*chip, c), src=my_half) for j, chip in enumerate(chips)]
        for cp in first:
            cp.start()
        passed = [copy(3 + j, (*chip, c), sibling) for j, chip in enumerate(chips)]
        for j, chip in enumerate(chips):
            copy(j, (*chip, c), (x, y, c)).wait_recv()
            passed[j].start()
        for j, chip in enumerate(chips):
            copy(3 + j, (*chip, 1 - c), (x, y, c)).wait_recv()
        for cp in first + passed:
            cp.wait_send()
        mine.wait()

    return pl.pallas_call(
        body, name=name, in_specs=[HBM_SPEC], out_specs=HBM_SPEC,
        out_shape=jax.ShapeDtypeStruct((N_CHIPS, R, Cc), shard.dtype),
        scratch_shapes=[pltpu.SemaphoreType.DMA((6,)), pltpu.SemaphoreType.DMA((6,)), pltpu.SemaphoreType.DMA],
    )(shard)


def _swap_sibling(vs, *, name):
    n = len(vs)

    def body(*refs):
        src, dst, send_sems, recv_sems = refs[:n], refs[n:2 * n], refs[2 * n], refs[2 * n + 1]
        x, y, c = _place()
        cps = [pltpu.make_async_remote_copy(src_ref=src[i], dst_ref=dst[i], send_sem=send_sems.at[i],
                                            recv_sem=recv_sems.at[i], device_id=(x, y, 1 - c), device_id_type=MESH)
               for i in range(n)]
        for cp in cps:
            cp.start()
        for cp in cps:
            cp.wait()

    return pl.pallas_call(
        body, name=name, in_specs=[HBM_SPEC] * n, out_specs=[HBM_SPEC] * n,
        out_shape=[jax.ShapeDtypeStruct(v.shape, v.dtype) for v in vs],
        scratch_shapes=[pltpu.SemaphoreType.DMA((n,)), pltpu.SemaphoreType.DMA((n,))],
    )(*vs)


def _half(ref, j, c, half):
    return ref.at[j, pl.ds(c * half, half), :]


def _gather_start(shard, after, *, name):
    R, Cc = shard.shape
    half = R // 2

    def body(src, land, after_ref, send, recv, src_out, land_out, token):
        x, y, c = _place()
        for k, (px, py) in enumerate(_other_chips(x, y)):
            pltpu.make_async_remote_copy(src_ref=src.at[pl.ds(c * half, half), :], dst_ref=_half(land, 2 * x + y, c, half),
                                         send_sem=send.at[k], recv_sem=recv.at[k], device_id=(px, py, c),
                                         device_id_type=MESH).start()
        token[...] = jnp.zeros_like(token)

    land = lax.empty((N_CHIPS, R, Cc), shard.dtype)
    out = pl.pallas_call(
        body, name=name, in_specs=[HBM_SPEC, HBM_SPEC, ANY_SPEC],
        out_specs=[SEM_SPEC, SEM_SPEC, HBM_SPEC, HBM_SPEC, VMEM_SPEC],
        out_shape=[pltpu.SemaphoreType.DMA((3,)), pltpu.SemaphoreType.DMA((3,)), pltpu.HBM(shard.shape, shard.dtype),
                   pltpu.HBM(land.shape, land.dtype), jax.ShapeDtypeStruct((SUBLANES, LANES), F32)],
        input_output_aliases={0: 2, 1: 3},
        compiler_params=pltpu.CompilerParams(has_side_effects=DATAFLOW),
    )(_in_hbm(shard), _in_hbm(land), after)
    return out[:4], out[4]


def _gather_wait(handle, after, *, name):
    send_sems, recv_sems, src, land = handle
    half = src.shape[0] // 2

    def body(src_ref, land_ref, send_ref, recv_ref, after_ref, src_out, land_out):
        x, y, c = _place()
        for k, (px, py) in enumerate(_other_chips(x, y)):
            cp = pltpu.make_async_remote_copy(src_ref=src_ref.at[pl.ds(c * half, half), :],
                                              dst_ref=_half(land_ref, 2 * px + py, c, half), send_sem=send_ref.at[k],
                                              recv_sem=recv_ref.at[k], device_id=(px, py, c), device_id_type=MESH)
            cp.wait_send()
            cp.wait_recv()

    return pl.pallas_call(
        body, name=name, in_specs=[HBM_SPEC, HBM_SPEC, SEM_SPEC, SEM_SPEC, ANY_SPEC], out_specs=[HBM_SPEC, HBM_SPEC],
        out_shape=[pltpu.HBM(src.shape, src.dtype), pltpu.HBM(land.shape, land.dtype)],
        input_output_aliases={0: 0, 1: 1},
        compiler_params=pltpu.CompilerParams(has_side_effects=DATAFLOW),
    )(src, land, send_sems, recv_sems, after)[1]


def _fill_sibling(land, *, name):
    _, R, Cc = land.shape
    half = R // 2

    def body(in_ref, o_ref, send_sems, recv_sems):
        x, y, c = _place()
        chips = _other_chips(x, y)
        cps = [pltpu.make_async_remote_copy(src_ref=_half(in_ref, 2 * px + py, c, half),
                                            dst_ref=_half(o_ref, 2 * px + py, c, half), send_sem=send_sems.at[k],
                                            recv_sem=recv_sems.at[k], device_id=(x, y, 1 - c), device_id_type=MESH)
               for k, (px, py) in enumerate(chips)]
        for cp in cps:
            cp.start()
        for k, (px, py) in enumerate(chips):
            pltpu.make_async_remote_copy(src_ref=_half(in_ref, 2 * px + py, 1 - c, half),
                                         dst_ref=_half(o_ref, 2 * px + py, 1 - c, half), send_sem=send_sems.at[k],
                                         recv_sem=recv_sems.at[k], device_id=(x, y, 1 - c), device_id_type=MESH).wait_recv()
        for cp in cps:
            cp.wait_send()

    return pl.pallas_call(
        body, name=name, in_specs=[HBM_SPEC], out_specs=HBM_SPEC, out_shape=jax.ShapeDtypeStruct(land.shape, land.dtype),
        scratch_shapes=[pltpu.SemaphoreType.DMA((3,)), pltpu.SemaphoreType.DMA((3,))],
        input_output_aliases={0: 0},
    )(land)


def _scatter_copies(src, land, send, recv):
    x, y, c = _place()
    return [pltpu.make_async_remote_copy(src_ref=src[i].at[2 * px + py], dst_ref=land[i].at[k], send_sem=send.at[3 * i + k],
                                         recv_sem=recv.at[3 * i + k], device_id=(px, py, c), device_id_type=MESH)
            for i in range(len(src)) for k, (px, py) in enumerate(_other_chips(x, y))]


def _scatter_start(pieces, *, name):
    n = len(pieces)

    def body(*refs):
        src, land, send, recv, token = refs[:n], refs[n:2 * n], refs[2 * n], refs[2 * n + 1], refs[-1]
        for cp in _scatter_copies(src, land, send, recv):
            cp.start()
        token[...] = jnp.zeros_like(token)

    lands = [lax.empty((3,) + p.shape[1:], p.dtype) for p in pieces]
    sems = pltpu.SemaphoreType.DMA((3 * n,))
    out = pl.pallas_call(
        body, name=name, in_specs=[HBM_SPEC] * (2 * n),
        out_specs=[SEM_SPEC, SEM_SPEC] + [HBM_SPEC] * (2 * n) + [VMEM_SPEC],
        out_shape=[sems, sems] + [pltpu.HBM(a.shape, a.dtype) for a in pieces + lands]
        + [jax.ShapeDtypeStruct((SUBLANES, LANES), F32)],
        input_output_aliases={i: 2 + i for i in range(2 * n)},
        compiler_params=pltpu.CompilerParams(has_side_effects=DATAFLOW),
    )(*[_in_hbm(a) for a in pieces + lands])
    return (out[0], out[1], out[2:2 + n], out[2 + n:2 + 2 * n]), out[-1]


def _scatter_wait(handle, after, *, name):
    send_sems, recv_sems, srcs, lands = handle
    n = len(srcs)

    def body(*refs):
        src, land, send, recv = refs[:n], refs[n:2 * n], refs[2 * n], refs[2 * n + 1]
        for cp in _scatter_copies(src, land, send, recv):
            cp.wait_send()
            cp.wait_recv()

    both = list(srcs) + list(lands)
    out = pl.pallas_call(
        body, name=name, in_specs=[HBM_SPEC] * (2 * n) + [SEM_SPEC, SEM_SPEC, ANY_SPEC], out_specs=[HBM_SPEC] * (2 * n),
        out_shape=[pltpu.HBM(a.shape, a.dtype) for a in both],
        input_output_aliases={i: i for i in range(2 * n)},
        compiler_params=pltpu.CompilerParams(has_side_effects=DATAFLOW),
    )(*both, send_sems, recv_sems, after)
    return out[n:]


def _chip_sum(pieces, got, chip, *, name):
    _, R, Cc = pieces.shape
    tr = _tile(R, 256, SUBLANES)

    def body(chip_ref, a_ref, g_ref, o_ref):
        o_ref[...] = ((a_ref[...] + g_ref[0].astype(F32)) + g_ref[1].astype(F32)) + g_ref[2].astype(F32)

    return pl.pallas_call(
        body, name=name,
        grid_spec=pltpu.PrefetchScalarGridSpec(
            num_scalar_prefetch=1, grid=(R // tr,),
            in_specs=[pl.BlockSpec((None, tr, Cc), lambda i, ch: (ch[0], i, 0)),
                      pl.BlockSpec((3, tr, Cc), lambda i, ch: (0, i, 0))],
            out_specs=pl.BlockSpec((tr, Cc), lambda i, ch: (i, 0))),
        out_shape=jax.ShapeDtypeStruct((R, Cc), F32),
        compiler_params=_params(("parallel",)),
    )(chip, pieces, got)


PACK_COLS = 1024


def _pack_rows(parts):
    return jnp.concatenate([p.reshape(-1, PACK_COLS) for p in parts], axis=0)


def _unpack_rows(block, shapes):
    lead = block.shape[:-2]
    out, off = [], 0
    for s in shapes:
        r = int(np.prod(s)) // PACK_COLS
        out.append(block[..., off:off + r, :].reshape(lead + tuple(s)))
        off += r
    assert off == block.shape[-2]
    return out


def _flat128(parts):
    out = []
    for p in parts:
        v = p.reshape(-1)
        pad = (-v.shape[0]) % LANES
        out.append(jnp.pad(v, (0, pad)) if pad else v)
    v = jnp.concatenate(out)
    pad = (-v.shape[0]) % (SUBLANES * LANES)
    if pad:
        v = jnp.pad(v, (0, pad))
    return v.reshape(-1, LANES)


def _unflat128(block, shapes):
    v = block.reshape(-1)
    out, off = [], 0
    for s in shapes:
        n = int(np.prod(s))
        out.append(v[off:off + n].reshape(s))
        off += n + ((-n) % LANES)
    return out


def kernel(x, hgrn_w_in, hgrn_lb_logits, hgrn_gnorm_w, hgrn_w_out, swa_w_q, swa_sinks, swa_w_out, shared_w_kv, rel_bias, ffn_w_in, ffn_conv_w, ffn_conv_b, ffn_w_out, ln_mix_g, ln_mix_b, ln_ffn_g, ln_ffn_b, loss_target, m_hgrn_w_in, m_hgrn_lb_logits, m_hgrn_gnorm_w, m_hgrn_w_out, m_swa_w_q, m_swa_sinks, m_swa_w_out, m_shared_w_kv, m_rel_bias, m_ffn_w_in, m_ffn_conv_w, m_ffn_conv_b, m_ffn_w_out, m_ln_mix_g, m_ln_mix_b, m_ln_ffn_g, m_ln_ffn_b, v_hgrn_w_in, v_hgrn_lb_logits, v_hgrn_gnorm_w, v_hgrn_w_out, v_swa_w_q, v_swa_sinks, v_swa_w_out, v_shared_w_kv, v_rel_bias, v_ffn_w_in, v_ffn_conv_w, v_ffn_conv_b, v_ffn_w_out, v_ln_mix_g, v_ln_mix_b, v_ln_ffn_g, v_ln_ffn_b):
    xi, yi, ci = _place()
    chip = 2 * xi + yi
    Dm = D_MODEL
    FC = 2 * FFN_DIM // N_CHIPS
    Fo = FFN_DIM // N_CHIPS
    Dq = Dm // N_CHIPS
    bf = lambda a: a.astype(BF16)

    shard0 = _pack_rows([bf(hgrn_w_in), bf(hgrn_w_out)])
    shard1 = _pack_rows([bf(swa_w_q), bf(swa_w_out), bf(shared_w_kv), bf(ffn_w_in[0]), bf(ffn_w_out[0])])
    shard2 = _pack_rows([bf(ffn_w_in[1]), bf(ffn_w_out[1])])
    all0 = _gather_chips(shard0, name="gather_w0")
    handle1, token1 = _gather_start(shard1, all0, name="gather_w1_start")
    w_in, w_hg_out = _unpack_rows(all0, [(Dm, Dm), (Dq, Dm)])

    def ffn_weights(w_fi, w_fo, l):
        return {"ffn_in_a": {l: jnp.concatenate([w_fi[0], w_fi[1]], axis=1)},
                "ffn_in_b": {l: jnp.concatenate([w_fi[2], w_fi[3]], axis=1)},
                "ffn_out": {l: w_fo.reshape(FFN_DIM, Dm)}}

    got = {}

    def more_weights(k, after):
        shard = (shard1, shard2)[k - 1]
        land = _gather_wait(got.pop("handle"), after, name=f"gather_w{k}_wait")
        land = _fill_sibling(land, name=f"gather_w{k}_fill")
        allk = lax.dynamic_update_slice(land, shard[None], (chip, 0, 0))
        if k == 1:
            got["handle"], token2 = _gather_start(shard2, land, name="gather_w2_start")
            w_q, w_o, w_kv, w_fi, w_fo = _unpack_rows(allk, [(Dq, Dm), (Dq, Dm), (Dq, 2 * KV_DIM), (Dm, FC), (Fo, Dm)])
            got.update(ffn_weights(w_fi, w_fo, 0))
            return {"sw_q": w_q.reshape(Dm, Dm), "sw_out": w_o.reshape(Dm, Dm), "kv": w_kv.reshape(Dm, 2 * KV_DIM),
                    "token": token2, **{n: got[n] for n in ("ffn_in_a", "ffn_in_b", "ffn_out")}}
        w_fi, w_fo = _unpack_rows(allk, [(Dm, FC), (Fo, Dm)])
        new = ffn_weights(w_fi, w_fo, 1)
        return {n: {**got[n], **new[n]} for n in new}

    got["handle"] = handle1

    lb_full = lax.dynamic_update_slice(jnp.zeros((2, Dm), F32), hgrn_lb_logits, (0, chip * Dq))
    cw_full = lax.dynamic_update_slice(jnp.zeros((DEPTH, 3, 2 * FFN_DIM), F32), ffn_conv_w, (0, 0, chip * FC))
    only_south = (ci == 0).astype(F32)
    small_in = _sum8(_flat128([lb_full, cw_full]) * only_south, name="gather_small")
    lb_full, cw_full = _unflat128(small_in, [(2, Dm), (DEPTH, 3, 2 * FFN_DIM)])
    w = {
        "hg_in": [w_in[j] for j in range(4)], "hg_out": w_hg_out.reshape(Dm, Dm), "token": token1,
        "lb_logits": lb_full, "gnorm": hgrn_gnorm_w, "sinks": swa_sinks, "rel_bias": rel_bias,
        "conv_w_a": [cw_full[l, :, :FFN_DIM] for l in range(DEPTH)],
        "conv_w_b": [cw_full[l, :, FFN_DIM:] for l in range(DEPTH)],
        "conv_b_a": [ffn_conv_b[l:l + 1, :FFN_DIM] for l in range(DEPTH)],
        "conv_b_b": [ffn_conv_b[l:l + 1, FFN_DIM:] for l in range(DEPTH)],
        "ln_mix_g": [ln_mix_g[l:l + 1] for l in range(DEPTH)], "ln_mix_b": [ln_mix_b[l:l + 1] for l in range(DEPTH)],
        "ln_ffn_g": [ln_ffn_g[l:l + 1] for l in range(DEPTH)], "ln_ffn_b": [ln_ffn_b[l:l + 1] for l in range(DEPTH)],
    }

    sent = {}

    def ffn_pieces(gd):
        return [jnp.concatenate([gd["ffn_in_a"], gd["ffn_in_b"]], axis=0), gd["ffn_out"].reshape(N_CHIPS, Fo, Dm)]

    def emit(k, gd):
        rows4 = lambda a: a.reshape(N_CHIPS, Dq, a.shape[-1])
        if k == 1:
            pieces = [rows4(gd["sw_q"]), rows4(gd["sw_out"]), rows4(gd["kv"])] + ffn_pieces(gd)
        elif k == 2:
            pieces = ffn_pieces(gd) + [rows4(gd["hg_out"])]
        else:
            pieces = [jnp.stack(gd["hg_in"])]
        handle, token = _scatter_start([p.astype(BF16) for p in pieces], name=f"scatter_g{k}_start")
        sent[k] = (handle, pieces)
        return token

    loss_tile, grad_x, g = _local_step(x[0], loss_target[0], w, more_weights, emit)

    wts = dict(hgrn_w_in=hgrn_w_in, hgrn_lb_logits=hgrn_lb_logits, hgrn_gnorm_w=hgrn_gnorm_w, hgrn_w_out=hgrn_w_out,
               swa_w_q=swa_w_q, swa_sinks=swa_sinks, swa_w_out=swa_w_out, shared_w_kv=shared_w_kv, rel_bias=rel_bias,
               ffn_w_in=ffn_w_in, ffn_conv_w=ffn_conv_w, ffn_conv_b=ffn_conv_b, ffn_w_out=ffn_w_out,
               ln_mix_g=ln_mix_g, ln_mix_b=ln_mix_b, ln_ffn_g=ln_ffn_g, ln_ffn_b=ln_ffn_b)
    ms = dict(hgrn_w_in=m_hgrn_w_in, hgrn_lb_logits=m_hgrn_lb_logits, hgrn_gnorm_w=m_hgrn_gnorm_w, hgrn_w_out=m_hgrn_w_out,
              swa_w_q=m_swa_w_q, swa_sinks=m_swa_sinks, swa_w_out=m_swa_w_out, shared_w_kv=m_shared_w_kv, rel_bias=m_rel_bias,
              ffn_w_in=m_ffn_w_in, ffn_conv_w=m_ffn_conv_w, ffn_conv_b=m_ffn_conv_b, ffn_w_out=m_ffn_w_out,
              ln_mix_g=m_ln_mix_g, ln_mix_b=m_ln_mix_b, ln_ffn_g=m_ln_ffn_g, ln_ffn_b=m_ln_ffn_b)
    vs = dict(hgrn_w_in=v_hgrn_w_in, hgrn_lb_logits=v_hgrn_lb_logits, hgrn_gnorm_w=v_hgrn_gnorm_w, hgrn_w_out=v_hgrn_w_out,
              swa_w_q=v_swa_w_q, swa_sinks=v_swa_sinks, swa_w_out=v_swa_w_out, shared_w_kv=v_shared_w_kv, rel_bias=v_rel_bias,
              ffn_w_in=v_ffn_w_in, ffn_conv_w=v_ffn_conv_w, ffn_conv_b=v_ffn_conv_b, ffn_w_out=v_ffn_w_out,
              ln_mix_g=v_ln_mix_g, ln_mix_b=v_ln_mix_b, ln_ffn_g=v_ln_ffn_g, ln_ffn_b=v_ln_ffn_b)
    names = list(wts)
    grads, delta, new_m, new_v = {}, {}, {}, {}

    def update(n, ga, gb, layer=None, prev=None):
        r2 = lambda a: a.reshape(-1, a.shape[-1])
        rows = None if layer is None else (layer * ga.shape[0], ga.shape[0])
        return _adamw(r2(wts[n]), ga, gb, r2(ms[n]), r2(vs[n]), rows=rows, prev=prev,
                      name=f"adamw_{n}" + ("" if layer is None else f"_{layer}"))

    def keep(n, res):
        grads[n], delta[n], new_m[n], new_v[n] = [a.reshape(wts[n].shape) for a in res]

    chip1 = jnp.reshape(chip, (1,)).astype(jnp.int32)
    after = grad_x
    for k in (1, 2, 3):
        handle, pieces = sent[k]
        lands = _scatter_wait(handle, after, name=f"scatter_g{k}_wait")
        parts = [_chip_sum(p, l, chip1, name=f"scatter_g{k}_sum{i}") for i, (p, l) in enumerate(zip(pieces, lands))]
        sibs = _swap_sibling(parts, name=f"scatter_g{k}_swap")
        if k == 1:
            for n, ga, gb in zip(["swa_w_q", "swa_w_out", "shared_w_kv"], parts[:3], sibs[:3]):
                keep(n, update(n, ga, gb))
            ffn_in_1 = update("ffn_w_in", parts[3], sibs[3], layer=1)
            ffn_out_1 = update("ffn_w_out", parts[4], sibs[4], layer=1)
            after = ffn_out_1[3]
        elif k == 2:
            keep("ffn_w_in", update("ffn_w_in", parts[0], sibs[0], layer=0, prev=ffn_in_1))
            keep("ffn_w_out", update("ffn_w_out", parts[1], sibs[1], layer=0, prev=ffn_out_1))
            keep("hgrn_w_out", update("hgrn_w_out", parts[2], sibs[2]))
            after = new_v["hgrn_w_out"]
        else:
            keep("hgrn_w_in", update("hgrn_w_in", parts[0], sibs[0]))

    small_shapes = [(SUBLANES, LANES), (2, Dm), (1, HG_DIM), (1, SW_Q_HEADS), (REL_BUCKETS, SW_Q_HEADS),
                    (DEPTH, 3, 2 * FFN_DIM), (DEPTH, 2 * FFN_DIM)] + [(DEPTH, Dm)] * 4
    gc = g["conv"]
    conv_w_g = jnp.stack([jnp.concatenate([gc[l]["conv_w_a"], gc[l]["conv_w_b"]], axis=1) for l in range(DEPTH)])
    conv_b_g = jnp.concatenate([jnp.concatenate([gc[l]["conv_b_a"], gc[l]["conv_b_b"]], axis=1) for l in range(DEPTH)], axis=0)
    ln_g = [jnp.concatenate([g[f"{n}0"], g[f"{n}1"]], axis=0) for n in ("ln_mix_g", "ln_mix_b", "ln_ffn_g", "ln_ffn_b")]
    small_out = _sum8(_flat128([loss_tile, g["lb_logits"], g["gnorm"], g["sinks"], g["rel_bias"], conv_w_g, conv_b_g] + ln_g),
                      name="sum_small")
    (loss_t, g_lb, g_gn, g_sinks, g_rel, g_cw, g_cb, g_lmg, g_lmb, g_lfg, g_lfb) = _unflat128(small_out, small_shapes)
    loss = loss_t[0, 0]
    g_lb = lax.dynamic_slice_in_dim(g_lb, chip * Dq, Dq, axis=1)
    g_cw = lax.dynamic_slice_in_dim(g_cw, chip * FC, FC, axis=2)
    small_g = dict(hgrn_lb_logits=g_lb, hgrn_gnorm_w=g_gn, swa_sinks=g_sinks, rel_bias=g_rel, ffn_conv_w=g_cw,
                   ffn_conv_b=g_cb, ln_mix_g=g_lmg, ln_mix_b=g_lmb, ln_ffn_g=g_lfg, ln_ffn_b=g_lfb)
    small_names = list(small_g)
    sshapes = [wts[n].shape for n in small_names]
    _, d_, m_, v_ = _adamw(_flat128([wts[n] for n in small_names]), _flat128([small_g[n] for n in small_names]), None,
                           _flat128([ms[n] for n in small_names]), _flat128([vs[n] for n in small_names]), name="adamw_small")
    for n, a, b_, c_ in zip(small_names, _unflat128(d_, sshapes), _unflat128(m_, sshapes), _unflat128(v_, sshapes)):
        grads[n], delta[n], new_m[n], new_v[n] = small_g[n], a, b_, c_

    return (loss, grad_x[None], *[grads[n] for n in names], *[delta[n] for n in names],
            *[new_m[n] for n in names], *[new_v[n] for n in names])
```

```python
import functools
import math

import numpy as np
import jax
import jax.numpy as jnp
from jax import lax
from jax.experimental import pallas as pl
from jax.experimental.pallas import tpu as pltpu

F32 = jnp.float32
BF16 = jnp.bfloat16
MESH = pl.DeviceIdType.MESH

D_MODEL = 1024
DEPTH = 2
HG_HEADS = 8
HG_DIM = 128
SW_Q_HEADS = 16
SW_KV_HEADS = 4
SW_HEAD_DIM = 64
SW_GROUP = 4
SW_WINDOW = 128
REL_BUCKETS = 32
REL_MAX_DIST = 128
FFN_DIM = 2816
ALPHA = (2.0 * DEPTH) ** 0.25
LN_EPS = 1e-5
RMS_EPS = 1e-6
ADAM_LR = 0.001
ADAM_B1 = 0.9
ADAM_B2 = 0.999
ADAM_EPS = 1e-08
ADAM_WD = 0.01
ADAM_STEP = 10

VMEM_BYTES_V7X = 64 * 1024 * 1024
VMEM_LIMIT = VMEM_BYTES_V7X - 8 * 1024 * 1024
LANES = 128
SUBLANES = 8

HG_C = 64
HG_RB = 256
ROW_TILE = 256
CONV_R = 256
N_CHIPS = 4
N_DEV = 8

ANY_SPEC = pl.BlockSpec(memory_space=pl.ANY)


def _after(body, n_in, after):
    if after is None:
        return body, [], ()

    def wrapped(*refs):
        return body(*refs[:n_in], *refs[n_in + 1:])

    return wrapped, [ANY_SPEC], (after,)


def _params(sem=None):
    return pltpu.CompilerParams(dimension_semantics=sem, vmem_limit_bytes=VMEM_LIMIT)


def _tile(n, pref, unit=LANES):
    if n <= pref:
        return n
    best = None
    for t in range(unit, pref + 1, unit):
        if n % t == 0:
            best = t
    assert best is not None, (n, pref, unit)
    return best


def _dot(a, b, ca, cb):
    nb = a.ndim - 2
    batch = tuple(range(nb))
    return lax.dot_general(a.astype(BF16), b.astype(BF16), (((nb + ca,), (nb + cb,)), (batch, batch)),
                           preferred_element_type=F32)


@jax.custom_vjp
def mm(a, b):
    return _dot(a, b, 1, 0)


@jax.custom_vjp
def mm_nt(a, b):
    return _dot(a, b, 1, 1)


@jax.custom_vjp
def mm_tn(a, b):
    return _dot(a, b, 0, 0)


mm.defvjp(lambda a, b: (mm(a, b), (a, b)), lambda r, ct: (mm_nt(ct, r[1]), mm_tn(r[0], ct)))
mm_nt.defvjp(lambda a, b: (mm_nt(a, b), (a, b)), lambda r, ct: (mm(ct, r[1]), mm_tn(ct, r[0])))
mm_tn.defvjp(lambda a, b: (mm_tn(a, b), (a, b)), lambda r, ct: (mm_nt(r[1], ct), mm(r[0], ct)))


def _split2(x):
    hi = x.astype(BF16)
    return hi, (x - hi.astype(F32)).astype(BF16)


@jax.custom_vjp
def _scores(qt, kt):
    return _dot(qt, kt, 1, 1)


def _scores_bwd(r, ct):
    (qh, ql), (kh, kl) = _split2(r[0]), _split2(r[1])
    return _dot(ct, kh, 1, 0) + _dot(ct, kl, 1, 0), _dot(ct, qh, 0, 0) + _dot(ct, ql, 0, 0)


_scores.defvjp(lambda a, b: (_scores(a, b), (a, b)), _scores_bwd)


def _split3(x):
    hi = x.astype(BF16)
    r1 = x - hi.astype(F32)
    mid = r1.astype(BF16)
    lo = (r1 - mid.astype(F32)).astype(BF16)
    return hi, mid, lo


def _cumsum_impl(x):
    ax = x.ndim - 2
    n = x.shape[ax]
    row = lax.broadcasted_iota(jnp.int32, x.shape, ax)
    d = 1
    while d < n:
        x = x + jnp.where(row >= d, pltpu.roll(x, d, ax), 0.0)
        d *= 2
    return x


def _cumsum_rev_impl(x):
    ax = x.ndim - 2
    n = x.shape[ax]
    row = lax.broadcasted_iota(jnp.int32, x.shape, ax)
    d = 1
    while d < n:
        x = x + jnp.where(row < n - d, pltpu.roll(x, n - d, ax), 0.0)
        d *= 2
    return x


@jax.custom_vjp
def _cumsum(x):
    return _cumsum_impl(x)


_cumsum.defvjp(lambda x: (_cumsum_impl(x), None), lambda _, ct: (_cumsum_rev_impl(ct),))


def _matmul(a, b, *, mode, name, out_dtype=F32, add=None, add_scale=1.0, tm=512, tn=1408, tk=1408, after=None,
            split_n=False):
    if mode == "nn":
        (M, K), (K2, N) = a.shape, b.shape
    elif mode == "nt":
        (M, K), (N, K2) = a.shape, b.shape
    else:
        (K, M), (K2, N) = a.shape, b.shape
    assert K == K2, (a.shape, b.shape, mode)
    tm, tn, tk = _tile(M, tm), _tile(N, tn), _tile(K, tk)
    nk = K // tk
    ca, cb = {"nn": (1, 0), "nt": (1, 1), "tn": (0, 0)}[mode]
    a_spec = {"nn": pl.BlockSpec((tm, tk), lambda i, j, k: (i, k)),
              "nt": pl.BlockSpec((tm, tk), lambda i, j, k: (i, k)),
              "tn": pl.BlockSpec((tk, tm), lambda i, j, k: (k, i))}[mode]
    b_spec = {"nn": pl.BlockSpec((tk, tn), lambda i, j, k: (k, j)),
              "nt": pl.BlockSpec((tn, tk), lambda i, j, k: (j, k)),
              "tn": pl.BlockSpec((tk, tn), lambda i, j, k: (k, j))}[mode]
    o_spec = pl.BlockSpec((tm, tn), lambda i, j, k: (i, j))
    has_add = add is not None

    def finish(r, add_ref, o_ref):
        if has_add:
            r = r + add_scale * add_ref[...]
        o_ref[...] = r.astype(out_dtype)

    def body(*refs):
        a_ref, b_ref = refs[:2]
        add_ref = refs[2] if has_add else None
        o_ref = refs[3 if has_add else 2]
        if nk == 1:
            finish(_dot(a_ref[...], b_ref[...], ca, cb), add_ref, o_ref)
            return
        acc_ref = refs[-1]
        k = pl.program_id(2)

        @pl.when(k == 0)
        def _():
            acc_ref[...] = jnp.zeros_like(acc_ref)

        acc_ref[...] += _dot(a_ref[...], b_ref[...], ca, cb)

        @pl.when(k == nk - 1)
        def _():
            finish(acc_ref[...], add_ref, o_ref)

    in_specs = [a_spec, b_spec] + ([o_spec] if has_add else [])
    args = (a, b) + ((add,) if has_add else ())
    body, xs, xa = _after(body, len(args), after)
    in_specs, args = in_specs + xs, args + xa
    out_shape = (M, N)
    if split_n:
        assert not has_add and M == tm
        o_spec = pl.BlockSpec((None, tm, tn), lambda i, j, k: (j, 0, 0))
        out_shape = (N // tn, M, tn)
    return pl.pallas_call(
        body, name=name, grid=(M // tm, N // tn, nk), in_specs=in_specs, out_specs=o_spec,
        out_shape=jax.ShapeDtypeStruct(out_shape, out_dtype),
        scratch_shapes=[pltpu.VMEM((tm, tn), F32)] if nk > 1 else [],
        compiler_params=_params(("parallel", "parallel", "arbitrary")),
    )(*args)


def _ln(z, g, b):
    mu = jnp.mean(z, axis=-1, keepdims=True)
    zc = z - mu
    var = jnp.mean(zc * zc, axis=-1, keepdims=True)
    return zc * lax.rsqrt(var + LN_EPS) * g + b


def _matmul_ln(a, b, h, g, bias, *, name, tgt=None, tm=512):
    (T, K), (K2, Dm) = a.shape, b.shape
    assert K == K2 and h.shape == (T, Dm)
    tm = _tile(T, tm, SUBLANES)
    last = tgt is not None

    def body(*refs):
        a_ref, b_ref, h_ref, g_ref, bias_ref = refs[:5]
        z = ALPHA * h_ref[...] + _dot(a_ref[...], b_ref[...], 1, 0)
        if not last:
            z_ref, y_ref, yb_ref = refs[5:]
            y = _ln(z, g_ref[...], bias_ref[...])
            z_ref[...] = z
            y_ref[...] = y
            yb_ref[...] = y.astype(BF16)
            return
        t_ref, dz_ref, dzb_ref, dg_ref, db_ref, l_ref = refs[5:]

        @pl.when(pl.program_id(0) == 0)
        def _():
            dg_ref[...] = jnp.zeros_like(dg_ref)
            db_ref[...] = jnp.zeros_like(db_ref)
            l_ref[...] = jnp.zeros_like(l_ref)

        y, vjp = jax.vjp(_ln, z, g_ref[...], bias_ref[...])
        e = y - t_ref[...]
        dz, dg, db = vjp(e * (1.0 / Dm))
        l_ref[...] += 0.5 * jnp.sum(jnp.mean(e * e, axis=-1, keepdims=True), axis=0, keepdims=True)
        dz_ref[...] = dz
        dzb_ref[...] = dz.astype(BF16)
        dg_ref[...] += dg
        db_ref[...] += db

    row = pl.BlockSpec((tm, Dm), lambda i: (i, 0))
    vec = pl.BlockSpec((1, Dm), lambda i: (0, 0))
    in_specs = [pl.BlockSpec((tm, K), lambda i: (i, 0)), pl.BlockSpec((K, Dm), lambda i: (0, 0)), row, vec, vec]
    f32, b16 = jax.ShapeDtypeStruct((T, Dm), F32), jax.ShapeDtypeStruct((T, Dm), BF16)
    if not last:
        return pl.pallas_call(
            body, name=name, grid=(T // tm,), in_specs=in_specs, out_specs=[row, row, row], out_shape=[f32, f32, b16],
            compiler_params=_params(("parallel",)),
        )(a, b, h, g, bias)
    return pl.pallas_call(
        body, name=name, grid=(T // tm,), in_specs=in_specs + [row],
        out_specs=[row, row, vec, vec, pl.BlockSpec((SUBLANES, LANES), lambda i: (0, 0))],
        out_shape=[f32, b16, jax.ShapeDtypeStruct((1, Dm), F32), jax.ShapeDtypeStruct((1, Dm), F32),
                   jax.ShapeDtypeStruct((SUBLANES, LANES), F32)],
        compiler_params=_params(("arbitrary",)),
    )(a, b, h, g, bias, tgt)


def _ln_bwd(dy, z, g, b, *, name, after=None):
    T, Dm = z.shape
    tr = _tile(T, ROW_TILE, SUBLANES)

    def body(dy_ref, z_ref, g_ref, b_ref, dz_ref, dzb_ref, dg_ref, db_ref):
        @pl.when(pl.program_id(0) == 0)
        def _():
            dg_ref[...] = jnp.zeros_like(dg_ref)
            db_ref[...] = jnp.zeros_like(db_ref)

        _, vjp = jax.vjp(_ln, z_ref[...], g_ref[...], b_ref[...])
        dz, dg, db = vjp(dy_ref[...])
        dz_ref[...] = dz
        dzb_ref[...] = dz.astype(BF16)
        dg_ref[...] += dg
        db_ref[...] += db

    row = pl.BlockSpec((tr, Dm), lambda i: (i, 0))
    vec = pl.BlockSpec((1, Dm), lambda i: (0, 0))
    body, xs, xa = _after(body, 4, after)
    return pl.pallas_call(
        body, name=name, grid=(T // tr,), in_specs=[row, row, vec, vec] + xs,
        out_specs=[row, row, vec, vec],
        out_shape=[jax.ShapeDtypeStruct((T, Dm), F32), jax.ShapeDtypeStruct((T, Dm), BF16),
                   jax.ShapeDtypeStruct((1, Dm), F32), jax.ShapeDtypeStruct((1, Dm), F32)],
        compiler_params=_params(("arbitrary",)),
    )(dy, z, g, b, *xa)


def _hg_chunk(qr, fr, ir, gr, l0, l1, gw, st):
    C = qr.shape[-2]
    row = lax.broadcasted_iota(jnp.int32, qr.shape, qr.ndim - 2)
    lb = jax.nn.sigmoid(l0 - l1)
    fg = lb + (1.0 - lb) * jax.nn.sigmoid(fr)
    b = _cumsum(jnp.log(fg))
    q = jax.nn.silu(qr)
    k = 1.0 - fg
    bmid = lax.stop_gradient(jnp.sum(jnp.where(row == C // 2 - 1, b, 0.0), axis=-2, keepdims=True))
    bl = jnp.sum(jnp.where(row == C - 1, b, 0.0), axis=-2, keepdims=True)
    o = mm_nt(q * jnp.exp(b), st)
    sc = _scores(q * jnp.exp(b - bmid), k * jnp.exp(bmid - b))
    ti = lax.broadcasted_iota(jnp.int32, (C, C), 0)
    si = lax.broadcasted_iota(jnp.int32, (C, C), 1)
    sc = jnp.where(si <= ti, sc, 0.0)
    o = o + mm(sc, ir)
    st_new = st * jnp.exp(bl) + mm_tn(ir, k * jnp.exp(bl - b))
    on = o * lax.rsqrt(jnp.mean(o * o, axis=-1, keepdims=True) + RMS_EPS)
    return on * gw * jax.nn.silu(gr), st_new


def _heads(ref, rows):
    return jnp.stack([ref[rows, h * HG_DIM:(h + 1) * HG_DIM].astype(F32) for h in range(HG_HEADS)])


def _unheads(x):
    return jnp.concatenate([x[h] for h in range(HG_HEADS)], axis=-1)


def _hgrn_fwd(q, f, i, g, lbl, gw, *, name):
    T, Dm = q.shape
    rb = min(HG_RB, T)
    C = min(HG_C, rb)
    ncb = rb // C

    def body(q_ref, f_ref, i_ref, g_ref, lbl_ref, gw_ref, o_ref, st_ref, s_ref):
        @pl.when(pl.program_id(0) == 0)
        def _():
            s_ref[...] = jnp.zeros_like(s_ref)

        def chunk(ci, carry):
            r0 = pl.multiple_of(ci * C, C)
            rows = pl.ds(r0, C)
            st = s_ref[...]
            st_ref[ci] = st
            out, st_new = _hg_chunk(_heads(q_ref, rows), _heads(f_ref, rows), _heads(i_ref, rows), _heads(g_ref, rows),
                                    _heads(lbl_ref, slice(0, 1)), _heads(lbl_ref, slice(1, 2)), gw_ref[...], st)
            o_ref[rows, :] = _unheads(out).astype(BF16)
            s_ref[...] = st_new
            return carry

        lax.fori_loop(0, ncb, chunk, 0, unroll=True)

    row = pl.BlockSpec((rb, Dm), lambda n: (n, 0))
    return pl.pallas_call(
        body, name=name, grid=(T // rb,),
        in_specs=[row, row, row, row, pl.BlockSpec((2, Dm), lambda n: (0, 0)), pl.BlockSpec((1, HG_DIM), lambda n: (0, 0))],
        out_specs=[row, pl.BlockSpec((ncb, HG_HEADS, HG_DIM, HG_DIM), lambda n: (n, 0, 0, 0))],
        out_shape=[jax.ShapeDtypeStruct((T, Dm), BF16),
                   jax.ShapeDtypeStruct((T // C, HG_HEADS, HG_DIM, HG_DIM), F32)],
        scratch_shapes=[pltpu.VMEM((HG_HEADS, HG_DIM, HG_DIM), F32)],
        compiler_params=_params(("arbitrary",)),
    )(q, f, i, g, lbl, gw)


def _hgrn_bwd(q, f, i, g, lbl, gw, states, dout, *, name, after=None):
    T, Dm = q.shape
    rb = min(HG_RB, T)
    C = min(HG_C, rb)
    ncb = rb // C
    nb = T // rb

    def body(q_ref, f_ref, i_ref, g_ref, lbl_ref, gw_ref, st_ref, do_ref,
             dq_ref, df_ref, di_ref, dg_ref, dlbl_ref, dgw_ref, ds_ref):
        @pl.when(pl.program_id(0) == 0)
        def _():
            ds_ref[...] = jnp.zeros_like(ds_ref)
            dlbl_ref[...] = jnp.zeros_like(dlbl_ref)
            dgw_ref[...] = jnp.zeros_like(dgw_ref)

        def chunk(cj, carry):
            ci = ncb - 1 - cj
            r0 = pl.multiple_of(ci * C, C)
            rows = pl.ds(r0, C)
            _, vjp = jax.vjp(_hg_chunk, _heads(q_ref, rows), _heads(f_ref, rows), _heads(i_ref, rows), _heads(g_ref, rows),
                             _heads(lbl_ref, slice(0, 1)), _heads(lbl_ref, slice(1, 2)), gw_ref[...], st_ref[ci])
            dq, df, di, dg, dl0, dl1, dgw, dst = vjp((_heads(do_ref, rows).astype(F32), ds_ref[...]))
            dq_ref[rows, :] = _unheads(dq).astype(BF16)
            df_ref[rows, :] = _unheads(df).astype(BF16)
            di_ref[rows, :] = _unheads(di).astype(BF16)
            dg_ref[rows, :] = _unheads(dg).astype(BF16)
            dlbl_ref[0:1, :] += _unheads(dl0)
            dlbl_ref[1:2, :] += _unheads(dl1)
            dgw_ref[...] += dgw
            ds_ref[...] = dst
            return carry

        lax.fori_loop(0, ncb, chunk, 0, unroll=True)

    row = pl.BlockSpec((rb, Dm), lambda n: (nb - 1 - n, 0))
    lsp = pl.BlockSpec((2, Dm), lambda n: (0, 0))
    gsp = pl.BlockSpec((1, HG_DIM), lambda n: (0, 0))
    body, xs, xa = _after(body, 8, after)
    return pl.pallas_call(
        body, name=name, grid=(nb,),
        in_specs=[row, row, row, row, lsp, gsp,
                  pl.BlockSpec((ncb, HG_HEADS, HG_DIM, HG_DIM), lambda n: (nb - 1 - n, 0, 0, 0)), row] + xs,
        out_specs=[row, row, row, row, lsp, gsp],
        out_shape=[jax.ShapeDtypeStruct((T, Dm), BF16)] * 4
        + [jax.ShapeDtypeStruct((2, Dm), F32), jax.ShapeDtypeStruct((1, HG_DIM), F32)],
        scratch_shapes=[pltpu.VMEM((HG_HEADS, HG_DIM, HG_DIM), F32)],
        compiler_params=_params(("arbitrary",)),
    )(q, f, i, g, lbl, gw, states, dout, *xa)


def _shift_down(cur, prev):
    h = prev.shape[0]
    big = jnp.concatenate([prev, cur], axis=0)
    return pltpu.roll(big, 1, 0)[h:], pltpu.roll(big, 2, 0)[h:]


def _shift_up(cur, next8):
    n = cur.shape[0] + SUBLANES
    big = jnp.concatenate([cur, next8], axis=0)
    return pltpu.roll(big, n - 1, 0)[:cur.shape[0]], pltpu.roll(big, n - 2, 0)[:cur.shape[0]]


def _conv_rows(u_ref, w, bias, r0, R):
    halo = 2 * SUBLANES
    cur = u_ref[pl.ds(r0, R), :].astype(F32)
    p0 = pl.multiple_of(jnp.maximum(r0 - halo, 0), halo)
    prev = jnp.where(r0 > 0, u_ref[pl.ds(p0, halo), :].astype(F32), 0.0)
    s1, s2 = _shift_down(cur, prev)
    return w[0:1, :] * s2 + w[1:2, :] * s1 + w[2:3, :] * cur + bias, cur, s1, s2


def _conv_gate_fwd(ua, ub, wa, wb, ba, bb, *, name):
    T, Fd = ua.shape
    R = min(CONV_R, T)
    tc = LANES

    def body(ua_ref, ub_ref, wa_ref, wb_ref, ba_ref, bb_ref, o_ref):
        wa_, wb_, ba_, bb_ = wa_ref[...], wb_ref[...], ba_ref[...], bb_ref[...]

        def step(ri, carry):
            r0 = pl.multiple_of(ri * R, R)
            ca = _conv_rows(ua_ref, wa_, ba_, r0, R)[0]
            cb = _conv_rows(ub_ref, wb_, bb_, r0, R)[0]
            o_ref[pl.ds(r0, R), :] = (jax.nn.silu(ca) * cb).astype(BF16)
            return carry

        lax.fori_loop(0, T // R, step, 0)

    col = pl.BlockSpec((T, tc), lambda j: (0, j))
    wsp = pl.BlockSpec((3, tc), lambda j: (0, j))
    bsp = pl.BlockSpec((1, tc), lambda j: (0, j))
    return pl.pallas_call(
        body, name=name, grid=(Fd // tc,), in_specs=[col, col, wsp, wsp, bsp, bsp], out_specs=col,
        out_shape=jax.ShapeDtypeStruct((T, Fd), BF16),
        compiler_params=_params(("parallel",)),
    )(ua, ub, wa, wb, ba, bb)


def _conv_gate_bwd(ua, ub, wa, wb, ba, bb, dact, *, name):
    T, Fd = ua.shape
    R = min(CONV_R, T)
    nr = T // R
    tc = LANES

    def body(ua_ref, ub_ref, wa_ref, wb_ref, ba_ref, bb_ref, da_ref,
             dua_ref, dub_ref, dwa_ref, dwb_ref, dba_ref, dbb_ref, dca_ref, dcb_ref):
        wa_, wb_, ba_, bb_ = wa_ref[...], wb_ref[...], ba_ref[...], bb_ref[...]

        def taps(dc, cur, s1, s2):
            return jnp.concatenate([jnp.sum(dc * s2, axis=0, keepdims=True), jnp.sum(dc * s1, axis=0, keepdims=True),
                                    jnp.sum(dc * cur, axis=0, keepdims=True)], axis=0)

        def first(ri, carry):
            dwa, dwb, dba, dbb = carry
            r0 = pl.multiple_of(ri * R, R)
            ca, cura, s1a, s2a = _conv_rows(ua_ref, wa_, ba_, r0, R)
            cb, curb, s1b, s2b = _conv_rows(ub_ref, wb_, bb_, r0, R)
            dact_ = da_ref[pl.ds(r0, R), :].astype(F32)
            sg = jax.nn.sigmoid(ca)
            dca = dact_ * cb * (sg * (1.0 + ca * (1.0 - sg)))
            dcb = dact_ * (ca * sg)
            dca_ref[pl.ds(r0, R), :] = dca
            dcb_ref[pl.ds(r0, R), :] = dcb
            return (dwa + taps(dca, cura, s1a, s2a), dwb + taps(dcb, curb, s1b, s2b),
                    dba + jnp.sum(dca, axis=0, keepdims=True), dbb + jnp.sum(dcb, axis=0, keepdims=True))

        z3 = jnp.zeros((3, tc), F32)
        z1 = jnp.zeros((1, tc), F32)
        dwa, dwb, dba, dbb = lax.fori_loop(0, nr, first, (z3, z3, z1, z1))
        dwa_ref[...] = dwa
        dwb_ref[...] = dwb
        dba_ref[...] = dba
        dbb_ref[...] = dbb

        def du_rows(dc_ref, w, r0):
            cur = dc_ref[pl.ds(r0, R), :]
            n0 = pl.multiple_of(jnp.minimum(r0 + R, T - SUBLANES), SUBLANES)
            next8 = jnp.where(r0 + R < T, dc_ref[pl.ds(n0, SUBLANES), :], 0.0)
            m1, m2 = _shift_up(cur, next8)
            return w[2:3, :] * cur + w[1:2, :] * m1 + w[0:1, :] * m2

        def second(ri, carry):
            r0 = pl.multiple_of(ri * R, R)
            dua_ref[pl.ds(r0, R), :] = du_rows(dca_ref, wa_, r0).astype(BF16)
            dub_ref[pl.ds(r0, R), :] = du_rows(dcb_ref, wb_, r0).astype(BF16)
            return carry

        lax.fori_loop(0, nr, second, 0)

    col = pl.BlockSpec((T, tc), lambda j: (0, j))
    wsp = pl.BlockSpec((3, tc), lambda j: (0, j))
    bsp = pl.BlockSpec((1, tc), lambda j: (0, j))
    return pl.pallas_call(
        body, name=name, grid=(Fd // tc,), in_specs=[col, col, wsp, wsp, bsp, bsp, col],
        out_specs=[col, col, wsp, wsp, bsp, bsp],
        out_shape=[jax.ShapeDtypeStruct((T, Fd), BF16)] * 2 + [jax.ShapeDtypeStruct((3, Fd), F32)] * 2
        + [jax.ShapeDtypeStruct((1, Fd), F32)] * 2,
        scratch_shapes=[pltpu.VMEM((T, tc), F32), pltpu.VMEM((T, tc), F32)],
        compiler_params=_params(("parallel",)),
    )(ua, ub, wa, wb, ba, bb, dact)


def _bucket_index():
    t = np.arange(SW_WINDOW)[:, None] + SW_WINDOW
    s = np.arange(2 * SW_WINDOW)[None, :]
    dist = np.maximum(t - s, 0)
    exact = REL_BUCKETS // 2
    d = np.maximum(dist, 1).astype(np.float32)
    log_b = exact + (np.log(d / np.float32(exact)) / np.float32(math.log(REL_MAX_DIST / exact))
                     * np.float32(REL_BUCKETS - exact)).astype(np.int32)
    bucket = np.where(dist < exact, dist, np.minimum(log_b, REL_BUCKETS - 1))
    return bucket.astype(np.int32).reshape(1, -1)


BIAS_COLS = SW_WINDOW * 2 * SW_WINDOW
BIAS_TILE = 4096


def _bias_from_table(table, bucket, *, name):
    def body(t_ref, idx_ref, o_ref):
        onehot = (lax.broadcasted_iota(jnp.int32, (REL_BUCKETS, BIAS_TILE), 0) == idx_ref[...]).astype(BF16)
        acc = jnp.zeros((SW_Q_HEADS, BIAS_TILE), F32)
        for piece in _split3(t_ref[...]):
            acc = acc + lax.dot_general(piece, onehot, (((0,), (0,)), ((), ())), preferred_element_type=F32)
        o_ref[...] = acc

    return pl.pallas_call(
        body, name=name, grid=(BIAS_COLS // BIAS_TILE,),
        in_specs=[pl.BlockSpec((REL_BUCKETS, SW_Q_HEADS), lambda j: (0, 0)), pl.BlockSpec((1, BIAS_TILE), lambda j: (0, j))],
        out_specs=pl.BlockSpec((SW_Q_HEADS, BIAS_TILE), lambda j: (0, j)),
        out_shape=jax.ShapeDtypeStruct((SW_Q_HEADS, BIAS_COLS), F32),
        compiler_params=_params(("parallel",)),
    )(table, bucket)


def _table_grad(dbias, bucket, *, name):
    def body(d_ref, idx_ref, o_ref):
        @pl.when(pl.program_id(0) == 0)
        def _():
            o_ref[...] = jnp.zeros_like(o_ref)

        onehot = (lax.broadcasted_iota(jnp.int32, (REL_BUCKETS, BIAS_TILE), 0) == idx_ref[...]).astype(BF16)
        acc = jnp.zeros((REL_BUCKETS, SW_Q_HEADS), F32)
        for piece in _split3(d_ref[...]):
            acc = acc + lax.dot_general(onehot, piece, (((1,), (1,)), ((), ())), preferred_element_type=F32)
        o_ref[...] += acc

    return pl.pallas_call(
        body, name=name, grid=(BIAS_COLS // BIAS_TILE,),
        in_specs=[pl.BlockSpec((SW_Q_HEADS, BIAS_TILE), lambda j: (0, j)), pl.BlockSpec((1, BIAS_TILE), lambda j: (0, j))],
        out_specs=pl.BlockSpec((REL_BUCKETS, SW_Q_HEADS), lambda j: (0, 0)),
        out_shape=jax.ShapeDtypeStruct((REL_BUCKETS, SW_Q_HEADS), F32),
        compiler_params=_params(("arbitrary",)),
    )(dbias, bucket)


def _band_mask(n):
    rows = SW_GROUP * SW_WINDOW
    t = (lax.broadcasted_iota(jnp.int32, (rows, 2 * SW_WINDOW), 0) & (SW_WINDOW - 1)) + SW_WINDOW
    s = lax.broadcasted_iota(jnp.int32, (rows, 2 * SW_WINDOW), 1)
    dist = t - s
    return (dist >= 0) & (dist < SW_WINDOW) & ((n > 0) | (s >= SW_WINDOW))


def _head_cols(h):
    return slice(h * SW_HEAD_DIM, (h + 1) * SW_HEAD_DIM)


def _group_inputs(q_ref, bias_ref, sink_ref, g):
    heads = range(g * SW_GROUP, (g + 1) * SW_GROUP)
    q = jnp.concatenate([q_ref[:, _head_cols(h)] for h in heads], axis=0)
    sink = jnp.concatenate([jnp.broadcast_to(sink_ref[:, h:h + 1], (SW_WINDOW, 1)) for h in heads], axis=0)
    bias = bias_ref[g * SW_GROUP:(g + 1) * SW_GROUP].reshape(SW_GROUP * SW_WINDOW, 2 * SW_WINDOW)
    return heads, q, bias, sink


KV_DIM = SW_KV_HEADS * SW_HEAD_DIM


def _kv_pair(kvp_ref, kvc_ref, g):
    ks = slice(g * SW_HEAD_DIM, (g + 1) * SW_HEAD_DIM)
    vs = slice(KV_DIM + g * SW_HEAD_DIM, KV_DIM + (g + 1) * SW_HEAD_DIM)
    kk = jnp.concatenate([kvp_ref[:, ks], kvc_ref[:, ks]], axis=0)
    vv = jnp.concatenate([kvp_ref[:, vs], kvc_ref[:, vs]], axis=0)
    return kk, vv, ks, vs


def _attn_fwd(q1, kv, bias, sinks, *, name):
    T, Dm = q1.shape
    W = SW_WINDOW

    def body(q_ref, kvc_ref, kvp_ref, bias_ref, sink_ref, o_ref):
        mask = _band_mask(pl.program_id(0))
        G = range(SW_KV_HEADS)
        ins = [_group_inputs(q_ref, bias_ref, sink_ref, g) for g in G]
        kvs = [_kv_pair(kvp_ref, kvc_ref, g) for g in G]
        lg = [jnp.where(mask, mm_nt(ins[g][1], kvs[g][0]) * (SW_HEAD_DIM ** -0.5) + ins[g][2], -jnp.inf) for g in G]
        m = [jnp.maximum(jnp.max(lg[g], axis=-1, keepdims=True), ins[g][3]) for g in G]
        p = [jnp.exp(lg[g] - m[g]) for g in G]
        den = [jnp.sum(p[g], axis=-1, keepdims=True) + jnp.exp(ins[g][3] - m[g]) for g in G]
        o = [mm(p[g], kvs[g][1]) / den[g] for g in G]
        for g in G:
            for r, h in enumerate(ins[g][0]):
                o_ref[:, _head_cols(h)] = o[g][r * W:(r + 1) * W].astype(BF16)

    return pl.pallas_call(
        body, name=name, grid=(T // W,),
        in_specs=[pl.BlockSpec((W, Dm), lambda n: (n, 0)),
                  pl.BlockSpec((W, 2 * KV_DIM), lambda n: (n, 0)),
                  pl.BlockSpec((W, 2 * KV_DIM), lambda n: (jnp.maximum(n - 1, 0), 0)),
                  pl.BlockSpec((SW_Q_HEADS, W, 2 * W), lambda n: (0, 0, 0)),
                  pl.BlockSpec((1, SW_Q_HEADS), lambda n: (0, 0))],
        out_specs=pl.BlockSpec((W, Dm), lambda n: (n, 0)),
        out_shape=jax.ShapeDtypeStruct((T, Dm), BF16),
        compiler_params=_params(("parallel",)),
    )(q1, kv, kv, bias, sinks)


def _attn_bwd(q1, kv, bias, sinks, do, *, name):
    T, Dm = q1.shape
    W = SW_WINDOW
    nb = T // W

    def body(q_ref, kvc_ref, kvp_ref, bias_ref, sink_ref, do_ref,
             dq_ref, dkv_ref, dbias_ref, dsink_ref, carry_ref):
        @pl.when(pl.program_id(0) == 0)
        def _():
            carry_ref[...] = jnp.zeros_like(carry_ref)
            dbias_ref[...] = jnp.zeros_like(dbias_ref)
            dsink_ref[...] = jnp.zeros_like(dsink_ref)

        n = nb - 1 - pl.program_id(0)
        mask = _band_mask(n)
        lane = lax.broadcasted_iota(jnp.int32, (1, SW_Q_HEADS), 1)
        sc = SW_HEAD_DIM ** -0.5
        G = range(SW_KV_HEADS)
        ins = [_group_inputs(q_ref, bias_ref, sink_ref, g) for g in G]
        kvs = [_kv_pair(kvp_ref, kvc_ref, g) for g in G]
        do = [jnp.concatenate([do_ref[:, _head_cols(h)] for h in ins[g][0]], axis=0) for g in G]
        lg = [jnp.where(mask, mm_nt(ins[g][1], kvs[g][0]) * sc + ins[g][2], -jnp.inf) for g in G]
        m = [jnp.maximum(jnp.max(lg[g], axis=-1, keepdims=True), ins[g][3]) for g in G]
        p = [jnp.exp(lg[g] - m[g]) for g in G]
        ps = [jnp.exp(ins[g][3] - m[g]) for g in G]
        rden = [1.0 / (jnp.sum(p[g], axis=-1, keepdims=True) + ps[g]) for g in G]
        pn = [p[g] * rden[g] for g in G]
        dpn = [mm_nt(do[g], kvs[g][1]) for g in G]
        delta = [jnp.sum(pn[g] * dpn[g], axis=-1, keepdims=True) for g in G]
        ds = [pn[g] * (dpn[g] - delta[g]) for g in G]
        dsr = [-(ps[g] * rden[g]) * delta[g] for g in G]
        dq = [mm(ds[g], kvs[g][0]) * sc for g in G]
        dkk = [mm_tn(ds[g], ins[g][1]) * sc for g in G]
        dvv = [mm_tn(pn[g], do[g]) for g in G]
        dsink = jnp.zeros((1, SW_Q_HEADS), F32)
        for g in G:
            _, _, ks, vs = kvs[g]
            dbias_ref[g * SW_GROUP:(g + 1) * SW_GROUP] += ds[g].reshape(SW_GROUP, W, 2 * W)
            for r, h in enumerate(ins[g][0]):
                dq_ref[:, _head_cols(h)] = dq[g][r * W:(r + 1) * W].astype(BF16)
                dsink = dsink + jnp.where(lane == h, jnp.sum(dsr[g][r * W:(r + 1) * W], axis=0, keepdims=True), 0.0)
            dkv_ref[:, ks] = (carry_ref[:, ks] + dkk[g][W:]).astype(BF16)
            dkv_ref[:, vs] = (carry_ref[:, vs] + dvv[g][W:]).astype(BF16)
            carry_ref[:, ks] = dkk[g][:W]
            carry_ref[:, vs] = dvv[g][:W]
        dsink_ref[...] += dsink

    rev = lambda n: (nb - 1 - n, 0)
    return pl.pallas_call(
        body, name=name, grid=(nb,),
        in_specs=[pl.BlockSpec((W, Dm), rev),
                  pl.BlockSpec((W, 2 * KV_DIM), rev),
                  pl.BlockSpec((W, 2 * KV_DIM), lambda n: (jnp.maximum(nb - 2 - n, 0), 0)),
                  pl.BlockSpec((SW_Q_HEADS, W, 2 * W), lambda n: (0, 0, 0)),
                  pl.BlockSpec((1, SW_Q_HEADS), lambda n: (0, 0)),
                  pl.BlockSpec((W, Dm), rev)],
        out_specs=[pl.BlockSpec((W, Dm), rev), pl.BlockSpec((W, 2 * KV_DIM), rev),
                   pl.BlockSpec((SW_Q_HEADS, W, 2 * W), lambda n: (0, 0, 0)),
                   pl.BlockSpec((1, SW_Q_HEADS), lambda n: (0, 0))],
        out_shape=[jax.ShapeDtypeStruct((T, Dm), BF16), jax.ShapeDtypeStruct((T, 2 * KV_DIM), BF16),
                   jax.ShapeDtypeStruct((SW_Q_HEADS, W, 2 * W), F32), jax.ShapeDtypeStruct((1, SW_Q_HEADS), F32)],
        scratch_shapes=[pltpu.VMEM((W, 2 * KV_DIM), F32)],
        compiler_params=_params(("arbitrary",)),
    )(q1, kv, kv, bias, sinks, do)


def _ffn_fwd(hb, w, l, after=None):
    ua = _matmul(hb, w["ffn_in_a"][l], mode="nn", out_dtype=BF16, name=f"ffn{l}_up_a", tm=1024, after=after)
    ub = _matmul(hb, w["ffn_in_b"][l], mode="nn", out_dtype=BF16, name=f"ffn{l}_up_b", tm=1024)
    act = _conv_gate_fwd(ua, ub, w["conv_w_a"][l], w["conv_w_b"][l], w["conv_b_a"][l], w["conv_b_b"][l],
                         name=f"ffn{l}_conv_gate")
    return ua, ub, act


def _ffn_bwd(dffb, dh_scaled, hb, ua, ub, act, w, l):
    dact = _matmul(dffb, w["ffn_out"][l], mode="nt", out_dtype=BF16, name=f"ffn{l}_down_dx", tm=1024)
    g_out = _matmul(act, dffb, mode="tn", name=f"ffn{l}_down_dw", tm=1408, tn=1024, tk=1024)
    dua, dub, dwa, dwb, dba, dbb = _conv_gate_bwd(ua, ub, w["conv_w_a"][l], w["conv_w_b"][l], w["conv_b_a"][l],
                                                  w["conv_b_b"][l], dact, name=f"ffn{l}_conv_gate_bwd")
    dh = _matmul(dua, w["ffn_in_a"][l], mode="nt", add=dh_scaled, add_scale=ALPHA, name=f"ffn{l}_up_a_dx",
                 tn=1024, tk=FFN_DIM)
    dh = _matmul(dub, w["ffn_in_b"][l], mode="nt", add=dh, name=f"ffn{l}_up_b_dx", tn=1024, tk=FFN_DIM)
    g_in_a = _matmul(hb, dua, mode="tn", name=f"ffn{l}_up_a_dw", tm=1024, tn=FFN_DIM // 2, tk=1024, split_n=True)
    g_in_b = _matmul(hb, dub, mode="tn", name=f"ffn{l}_up_b_dw", tm=1024, tn=FFN_DIM // 2, tk=1024, split_n=True)
    return dh, dict(ffn_out=g_out, ffn_in_a=g_in_a, ffn_in_b=g_in_b, conv_w_a=dwa, conv_w_b=dwb, conv_b_a=dba, conv_b_b=dbb)


def _local_step(x, tgt, w, more_weights, emit):
    bucket = jnp.asarray(_bucket_index())
    xb = x.astype(BF16)

    pre = [_matmul(xb, w["hg_in"][j], mode="nn", out_dtype=BF16, name=f"hg_in_{j}", tm=1024, tn=1024,
                   after=w.get("token") if j == 0 else None) for j in range(4)]
    og, states = _hgrn_fwd(*pre, w["lb_logits"], w["gnorm"], name="hgrn_fwd")
    z1, h1, h1b = _matmul_ln(og, w["hg_out"], x, w["ln_mix_g"][0], w["ln_mix_b"][0], name="hg_out_ln")
    w = {**w, **more_weights(1, h1b)}
    ua0, ub0, act0 = _ffn_fwd(h1b, w, 0, after=w.get("token"))
    z2, h2, h2b = _matmul_ln(act0, w["ffn_out"][0], h1, w["ln_ffn_g"][0], w["ln_ffn_b"][0], name="ffn0_down_ln")
    kv = _matmul(h2b, w["kv"], mode="nn", name="kv_proj")

    bias = _bias_from_table(w["rel_bias"], bucket, name="rel_bias_expand").reshape(SW_Q_HEADS, SW_WINDOW, 2 * SW_WINDOW)
    q1 = _matmul(h2b, w["sw_q"], mode="nn", name="sw_q")
    o1 = _attn_fwd(q1, kv, bias, w["sinks"], name="attn_fwd")
    z3, h3, h3b = _matmul_ln(o1, w["sw_out"], h2, w["ln_mix_g"][1], w["ln_mix_b"][1], name="sw_out_ln")
    w = {**w, **more_weights(2, h3b)}
    ua1, ub1, act1 = _ffn_fwd(h3b, w, 1)

    g = {}
    dz, dzb, dg_, db_, loss_tile = _matmul_ln(act1, w["ffn_out"][1], h3, w["ln_ffn_g"][1], w["ln_ffn_b"][1], tgt=tgt,
                                              name="ffn1_down_ln_loss")

    g["ln_ffn_g1"], g["ln_ffn_b1"] = dg_, db_
    dh3, gf1 = _ffn_bwd(dzb, dz, h3b, ua1, ub1, act1, w, 1)
    dz, dzb, dg_, db_ = _ln_bwd(dh3, z3, w["ln_mix_g"][1], w["ln_mix_b"][1], name="ln_mix1_bwd")
    g["ln_mix_g1"], g["ln_mix_b1"] = dg_, db_
    do1 = _matmul(dzb, w["sw_out"], mode="nt", out_dtype=BF16, name="sw_out_dx")
    g_sw_out = _matmul(o1, dzb, mode="tn", name="sw_out_dw", tm=1024, tn=1024, tk=1024)
    dq1, dkv, dbias, dsinks = _attn_bwd(q1, kv, bias, w["sinks"], do1, name="attn_bwd")
    g["sinks"] = dsinks
    g["rel_bias"] = _table_grad(dbias.reshape(SW_Q_HEADS, BIAS_COLS), bucket, name="rel_bias_grad")
    dh2 = _matmul(dq1, w["sw_q"], mode="nt", add=dz, add_scale=ALPHA, name="sw_q_dx", tn=1024)
    dh2 = _matmul(dkv, w["kv"], mode="nt", add=dh2, name="kv_dx", tn=1024)
    g_sw_q = _matmul(h2b, dq1, mode="tn", name="sw_q_dw", tm=1024, tn=1024, tk=1024)
    g_kv = _matmul(h2b, dkv, mode="tn", name="kv_dw", tm=1024, tn=512, tk=1024)
    tok = emit(1, dict(sw_q=g_sw_q, sw_out=g_sw_out, kv=g_kv, ffn_in_a=gf1["ffn_in_a"], ffn_in_b=gf1["ffn_in_b"],
                       ffn_out=gf1["ffn_out"]))

    dz, dzb, dg_, db_ = _ln_bwd(dh2, z2, w["ln_ffn_g"][0], w["ln_ffn_b"][0], name="ln_ffn0_bwd", after=tok)
    g["ln_ffn_g0"], g["ln_ffn_b0"] = dg_, db_
    dh1, gf0 = _ffn_bwd(dzb, dz, h1b, ua0, ub0, act0, w, 0)
    dz, dzb, dg_, db_ = _ln_bwd(dh1, z1, w["ln_mix_g"][0], w["ln_mix_b"][0], name="ln_mix0_bwd")
    g["ln_mix_g0"], g["ln_mix_b0"] = dg_, db_
    dog = _matmul(dzb, w["hg_out"], mode="nt", out_dtype=BF16, name="hg_out_dx")
    g_hg_out = _matmul(og, dzb, mode="tn", name="hg_out_dw", tm=1024, tn=1024, tk=1024)
    tok = emit(2, dict(hg_out=g_hg_out, ffn_in_a=gf0["ffn_in_a"], ffn_in_b=gf0["ffn_in_b"], ffn_out=gf0["ffn_out"]))
    dpre = _hgrn_bwd(*pre, w["lb_logits"], w["gnorm"], states, dog, name="hgrn_bwd", after=tok)
    g["lb_logits"], g["gnorm"] = dpre[4], dpre[5]
    tok = emit(3, dict(hg_in=[_matmul(xb, dpre[j], mode="tn", name=f"hg_in_{j}_dw", tm=1024, tn=1024, tk=1024)
                              for j in range(4)]))
    dx = dz
    for j in range(4):
        dx = _matmul(dpre[j], w["hg_in"][j], mode="nt", add=dx, add_scale=ALPHA if j == 0 else 1.0,
                     name=f"hg_in_{j}_dx", tn=1024, after=tok if j == 0 else None)
    g["conv"] = [{k: gf[k] for k in ("conv_w_a", "conv_w_b", "conv_b_a", "conv_b_b")} for gf in (gf0, gf1)]
    return loss_tile, dx, g


def _adamw(wt, ga, gb, m, v, *, name, rows=None, prev=None):
    R, Cc = wt.shape
    r0, n = rows if rows is not None else (0, R)
    tr = _tile(n, 256, SUBLANES) if n % SUBLANES == 0 else n
    assert r0 % tr == 0
    c1 = 1.0 - ADAM_B1 ** ADAM_STEP
    c2 = 1.0 - ADAM_B2 ** ADAM_STEP
    two = gb is not None
    n_in = 5 if two else 4

    def body(*refs):
        if two:
            w_ref, ga_ref, gb_ref, m_ref, v_ref = refs[:5]
            g_ = ga_ref[...] + gb_ref[...]
        else:
            w_ref, ga_ref, m_ref, v_ref = refs[:4]
            g_ = ga_ref[...]
        g_ref, d_ref, nm_ref, nv_ref = refs[-4:]
        nm = ADAM_B1 * m_ref[...] + (1.0 - ADAM_B1) * g_
        nv = ADAM_B2 * v_ref[...] + (1.0 - ADAM_B2) * (g_ * g_)
        g_ref[...] = g_
        d_ref[...] = -ADAM_LR * ((nm / c1) / (jnp.sqrt(nv / c2) + ADAM_EPS) + ADAM_WD * w_ref[...])
        nm_ref[...] = nm
        nv_ref[...] = nv

    full = pl.BlockSpec((tr, Cc), lambda i: (i + r0 // tr, 0))
    part = pl.BlockSpec((tr, Cc), lambda i: (i, 0))
    args = (wt, ga, gb, m, v) if two else (wt, ga, m, v)
    in_specs = [full] + [part] * (n_in - 3) + [full, full]
    aliases = {}
    if prev is not None:
        args, in_specs = args + tuple(prev), in_specs + [ANY_SPEC] * 4
        aliases = {n_in + t: t for t in range(4)}
    return pl.pallas_call(
        body, name=name, grid=(n // tr,), in_specs=in_specs, out_specs=[full] * 4,
        out_shape=[jax.ShapeDtypeStruct((R, Cc), F32)] * 4, input_output_aliases=aliases,
        compiler_params=_params(("parallel",)),
    )(*args)


HBM_SPEC = pl.BlockSpec(memory_space=pltpu.HBM)
SEM_SPEC = pl.BlockSpec(memory_space=pltpu.SEMAPHORE)
VMEM_SPEC = pl.BlockSpec(memory_space=pltpu.VMEM)
DATAFLOW = pltpu.SideEffectType.DATAFLOW_SIDE_EFFECTING


def _in_hbm(a):
    return pltpu.with_memory_space_constraint(a, pltpu.HBM)


def _place():
    return lax.axis_index("x"), lax.axis_index("y"), lax.axis_index("c")


def _other_chips(x, y):
    return [(1 - x, y), (x, 1 - y), (1 - x, 1 - y)]


def _sum8(v, *, name):
    r = v.shape[0]

    def body(v_ref, all_ref, o_ref, send_sems, recv_sems, local_sem):
        x, y, c = _place()
        me, sibling = (x, y, c), (x, y, 1 - c)
        chips = _other_chips(x, y)

        def rows(px, py, pc):
            return all_ref.at[pl.ds((4 * px + 2 * py + pc) * r, r), :]

        def copy(k, block, to, src=None):
            return pltpu.make_async_remote_copy(
                src_ref=rows(*block) if src is None else src, dst_ref=rows(*block),
                send_sem=send_sems.at[k], recv_sem=recv_sems.at[k], device_id=to, device_id_type=MESH)

        mine = pltpu.make_async_copy(v_ref, rows(*me), local_sem)
        mine.start()
        first = [copy(0, me, sibling, src=v_ref)]
        first += [copy(1 + j, me, (*chip, c), src=v_ref) for j, chip in enumerate(chips)]
        for cp in first:
            cp.start()
        passed = [copy(4 + j, (*chip, c), sibling) for j, chip in enumerate(chips)]
        for j, chip in enumerate(chips):
            copy(1 + j, (*chip, c), me).wait_recv()
            passed[j].start()
        copy(0, sibling, me).wait_recv()
        for j, chip in enumerate(chips):
            copy(4 + j, (*chip, 1 - c), me).wait_recv()
        for cp in first + passed:
            cp.wait_send()
        mine.wait()
        acc = all_ref[pl.ds(0, r), :]
        for d in range(1, N_DEV):
            acc = acc + all_ref[pl.ds(d * r, r), :]
        o_ref[...] = acc

    return pl.pallas_call(
        body, name=name, in_specs=[VMEM_SPEC], out_specs=[VMEM_SPEC, VMEM_SPEC],
        out_shape=[jax.ShapeDtypeStruct((N_DEV * r, LANES), F32), jax.ShapeDtypeStruct((r, LANES), F32)],
        scratch_shapes=[pltpu.SemaphoreType.DMA((7,)), pltpu.SemaphoreType.DMA((7,)), pltpu.SemaphoreType.DMA],
        compiler_params=pltpu.CompilerParams(vmem_limit_bytes=VMEM_LIMIT),
    )(v)[1]


def _gather_chips(shard, *, name):
    R, Cc = shard.shape
    half = R // 2
    assert half * 2 == R

    def body(s_ref, o_ref, send_sems, recv_sems, local_sem):
        x, y, c = _place()
        sibling = (x, y, 1 - c)
        chips = _other_chips(x, y)

        def part(px, py, pc):
            return o_ref.at[2 * px + py, pl.ds(pc * half, half), :]

        def copy(k, block, to, src=None):
            return pltpu.make_async_remote_copy(
                src_ref=part(*block) if src is None else src, dst_ref=part(*block),
                send_sem=send_sems.at[k], recv_sem=recv_sems.at[k], device_id=to, device_id_type=MESH)

        mine = pltpu.make_async_copy(s_ref, o_ref.at[2 * x + y], local_sem)
        mine.start()
        my_half = s_ref.at[pl.ds(c * half, half), :]
        first = [copy(j, (x, y, c), (*chip, c), src=my_half) for j, chip in enumerate(chips)]
        for cp in first:
            cp.start()
        passed = [copy(3 + j, (*chip, c), sibling) for j, chip in enumerate(chips)]
        for j, chip in enumerate(chips):
            copy(j, (*chip, c), (x, y, c)).wait_recv()
            passed[j].start()
        for j, chip in enumerate(chips):
            copy(3 + j, (*chip, 1 - c), (x, y, c)).wait_recv()
        for cp in first + passed:
            cp.wait_send()
        mine.wait()

    return pl.pallas_call(
        body, name=name, in_specs=[HBM_SPEC], out_specs=HBM_SPEC,
        out_shape=jax.ShapeDtypeStruct((N_CHIPS, R, Cc), shard.dtype),
        scratch_shapes=[pltpu.SemaphoreType.DMA((6,)), pltpu.SemaphoreType.DMA((6,)), pltpu.SemaphoreType.DMA],
    )(shard)


def _swap_sibling(vs, *, name):
    n = len(vs)

    def body(*refs):
        src, dst, send_sems, recv_sems = refs[:n], refs[n:2 * n], refs[2 * n], refs[2 * n + 1]
        x, y, c = _place()
        cps = [pltpu.make_async_remote_copy(src_ref=src[i], dst_ref=dst[i], send_sem=send_sems.at[i],
                                            recv_sem=recv_sems.at[i], device_id=(x, y, 1 - c), device_id_type=MESH)
               for i in range(n)]
        for cp in cps:
            cp.start()
        for cp in cps:
            cp.wait()

    return pl.pallas_call(
        body, name=name, in_specs=[HBM_SPEC] * n, out_specs=[HBM_SPEC] * n,
        out_shape=[jax.ShapeDtypeStruct(v.shape, v.dtype) for v in vs],
        scratch_shapes=[pltpu.SemaphoreType.DMA((n,)), pltpu.SemaphoreType.DMA((n,))],
    )(*vs)


def _half(ref, j, c, half):
    return ref.at[j, pl.ds(c * half, half), :]


def _gather_start(shard, after, *, name):
    R, Cc = shard.shape
    half = R // 2

    def body(src, land, after_ref, send, recv, src_out, land_out, token):
        x, y, c = _place()
        for k, (px, py) in enumerate(_other_chips(x, y)):
            pltpu.make_async_remote_copy(src_ref=src.at[pl.ds(c * half, half), :], dst_ref=_half(land, 2 * x + y, c, half),
                                         send_sem=send.at[k], recv_sem=recv.at[k], device_id=(px, py, c),
                                         device_id_type=MESH).start()
        token[...] = jnp.zeros_like(token)

    land = lax.empty((N_CHIPS, R, Cc), shard.dtype)
    out = pl.pallas_call(
        body, name=name, in_specs=[HBM_SPEC, HBM_SPEC, ANY_SPEC],
        out_specs=[SEM_SPEC, SEM_SPEC, HBM_SPEC, HBM_SPEC, VMEM_SPEC],
        out_shape=[pltpu.SemaphoreType.DMA((3,)), pltpu.SemaphoreType.DMA((3,)), pltpu.HBM(shard.shape, shard.dtype),
                   pltpu.HBM(land.shape, land.dtype), jax.ShapeDtypeStruct((SUBLANES, LANES), F32)],
        input_output_aliases={0: 2, 1: 3},
        compiler_params=pltpu.CompilerParams(has_side_effects=DATAFLOW),
    )(_in_hbm(shard), _in_hbm(land), after)
    return out[:4], out[4]


def _gather_wait(handle, after, *, name):
    send_sems, recv_sems, src, land = handle
    half = src.shape[0] // 2

    def body(src_ref, land_ref, send_ref, recv_ref, after_ref, src_out, land_out):
        x, y, c = _place()
        for k, (px, py) in enumerate(_other_chips(x, y)):
            cp = pltpu.make_async_remote_copy(src_ref=src_ref.at[pl.ds(c * half, half), :],
                                              dst_ref=_half(land_ref, 2 * px + py, c, half), send_sem=send_ref.at[k],
                                              recv_sem=recv_ref.at[k], device_id=(px, py, c), device_id_type=MESH)
            cp.wait_send()
            cp.wait_recv()

    return pl.pallas_call(
        body, name=name, in_specs=[HBM_SPEC, HBM_SPEC, SEM_SPEC, SEM_SPEC, ANY_SPEC], out_specs=[HBM_SPEC, HBM_SPEC],
        out_shape=[pltpu.HBM(src.shape, src.dtype), pltpu.HBM(land.shape, land.dtype)],
        input_output_aliases={0: 0, 1: 1},
        compiler_params=pltpu.CompilerParams(has_side_effects=DATAFLOW),
    )(src, land, send_sems, recv_sems, after)[1]


def _fill_sibling(land, *, name):
    _, R, Cc = land.shape
    half = R // 2

    def body(in_ref, o_ref, send_sems, recv_sems):
        x, y, c = _place()
        chips = _other_chips(x, y)
        cps = [pltpu.make_async_remote_copy(src_ref=_half(in_ref, 2 * px + py, c, half),
                                            dst_ref=_half(o_ref, 2 * px + py, c, half), send_sem=send_sems.at[k],
                                            recv_sem=recv_sems.at[k], device_id=(x, y, 1 - c), device_id_type=MESH)
               for k, (px, py) in enumerate(chips)]
        for cp in cps:
            cp.start()
        for k, (px, py) in enumerate(chips):
            pltpu.make_async_remote_copy(src_ref=_half(in_ref, 2 * px + py, 1 - c, half),
                                         dst_ref=_half(o_ref, 2 * px + py, 1 - c, half), send_sem=send_sems.at[k],
                                         recv_sem=recv_sems.at[k], device_id=(x, y, 1 - c), device_id_type=MESH).wait_recv()
        for cp in cps:
            cp.wait_send()

    return pl.pallas_call(
        body, name=name, in_specs=[HBM_SPEC], out_specs=HBM_SPEC, out_shape=jax.ShapeDtypeStruct(land.shape, land.dtype),
        scratch_shapes=[pltpu.SemaphoreType.DMA((3,)), pltpu.SemaphoreType.DMA((3,))],
        input_output_aliases={0: 0},
    )(land)


def _scatter_copies(src, land, send, recv):
    x, y, c = _place()
    return [pltpu.make_async_remote_copy(src_ref=src[i].at[2 * px + py], dst_ref=land[i].at[k], send_sem=send.at[3 * i + k],
                                         recv_sem=recv.at[3 * i + k], device_id=(px, py, c), device_id_type=MESH)
            for i in range(len(src)) for k, (px, py) in enumerate(_other_chips(x, y))]


def _scatter_start(pieces, *, name):
    n = len(pieces)

    def body(*refs):
        src, land, send, recv, token = refs[:n], refs[n:2 * n], refs[2 * n], refs[2 * n + 1], refs[-1]
        for cp in _scatter_copies(src, land, send, recv):
            cp.start()
        token[...] = jnp.zeros_like(token)

    lands = [lax.empty((3,) + p.shape[1:], p.dtype) for p in pieces]
    sems = pltpu.SemaphoreType.DMA((3 * n,))
    out = pl.pallas_call(
        body, name=name, in_specs=[HBM_SPEC] * (2 * n),
        out_specs=[SEM_SPEC, SEM_SPEC] + [HBM_SPEC] * (2 * n) + [VMEM_SPEC],
        out_shape=[sems, sems] + [pltpu.HBM(a.shape, a.dtype) for a in pieces + lands]
        + [jax.ShapeDtypeStruct((SUBLANES, LANES), F32)],
        input_output_aliases={i: 2 + i for i in range(2 * n)},
        compiler_params=pltpu.CompilerParams(has_side_effects=DATAFLOW),
    )(*[_in_hbm(a) for a in pieces + lands])
    return (out[0], out[1], out[2:2 + n], out[2 + n:2 + 2 * n]), out[-1]


def _scatter_wait(handle, after, *, name):
    send_sems, recv_sems, srcs, lands = handle
    n = len(srcs)

    def body(*refs):
        src, land, send, recv = refs[:n], refs[n:2 * n], refs[2 * n], refs[2 * n + 1]
        for cp in _scatter_copies(src, land, send, recv):
            cp.wait_send()
            cp.wait_recv()

    both = list(srcs) + list(lands)
    out = pl.pallas_call(
        body, name=name, in_specs=[HBM_SPEC] * (2 * n) + [SEM_SPEC, SEM_SPEC, ANY_SPEC], out_specs=[HBM_SPEC] * (2 * n),
        out_shape=[pltpu.HBM(a.shape, a.dtype) for a in both],
        input_output_aliases={i: i for i in range(2 * n)},
        compiler_params=pltpu.CompilerParams(has_side_effects=DATAFLOW),
    )(*both, send_sems, recv_sems, after)
    return out[n:]


def _chip_sum(pieces, got, chip, *, name):
    _, R, Cc = pieces.shape
    tr = _tile(R, 256, SUBLANES)

    def body(chip_ref, a_ref, g_ref, o_ref):
        o_ref[...] = ((a_ref[...] + g_ref[0].astype(F32)) + g_ref[1].astype(F32)) + g_ref[2].astype(F32)

    return pl.pallas_call(
        body, name=name,
        grid_spec=pltpu.PrefetchScalarGridSpec(
            num_scalar_prefetch=1, grid=(R // tr,),
            in_specs=[pl.BlockSpec((None, tr, Cc), lambda i, ch: (ch[0], i, 0)),
                      pl.BlockSpec((3, tr, Cc), lambda i, ch: (0, i, 0))],
            out_specs=pl.BlockSpec((tr, Cc), lambda i, ch: (i, 0))),
        out_shape=jax.ShapeDtypeStruct((R, Cc), F32),
        compiler_params=_params(("parallel",)),
    )(chip, pieces, got)


PACK_COLS = 1024


def _pack_rows(parts):
    return jnp.concatenate([p.reshape(-1, PACK_COLS) for p in parts], axis=0)


def _unpack_rows(block, shapes):
    lead = block.shape[:-2]
    out, off = [], 0
    for s in shapes:
        r = int(np.prod(s)) // PACK_COLS
        out.append(block[..., off:off + r, :].reshape(lead + tuple(s)))
        off += r
    assert off == block.shape[-2]
    return out


def _flat128(parts):
    out = []
    for p in parts:
        v = p.reshape(-1)
        pad = (-v.shape[0]) % LANES
        out.append(jnp.pad(v, (0, pad)) if pad else v)
    v = jnp.concatenate(out)
    pad = (-v.shape[0]) % (SUBLANES * LANES)
    if pad:
        v = jnp.pad(v, (0, pad))
    return v.reshape(-1, LANES)


def _unflat128(block, shapes):
    v = block.reshape(-1)
    out, off = [], 0
    for s in shapes:
        n = int(np.prod(s))
        out.append(v[off:off + n].reshape(s))
        off += n + ((-n) % LANES)
    return out


def kernel(x, hgrn_w_in, hgrn_lb_logits, hgrn_gnorm_w, hgrn_w_out, swa_w_q, swa_sinks, swa_w_out, shared_w_kv, rel_bias, ffn_w_in, ffn_conv_w, ffn_conv_b, ffn_w_out, ln_mix_g, ln_mix_b, ln_ffn_g, ln_ffn_b, loss_target, m_hgrn_w_in, m_hgrn_lb_logits, m_hgrn_gnorm_w, m_hgrn_w_out, m_swa_w_q, m_swa_sinks, m_swa_w_out, m_shared_w_kv, m_rel_bias, m_ffn_w_in, m_ffn_conv_w, m_ffn_conv_b, m_ffn_w_out, m_ln_mix_g, m_ln_mix_b, m_ln_ffn_g, m_ln_ffn_b, v_hgrn_w_in, v_hgrn_lb_logits, v_hgrn_gnorm_w, v_hgrn_w_out, v_swa_w_q, v_swa_sinks, v_swa_w_out, v_shared_w_kv, v_rel_bias, v_ffn_w_in, v_ffn_conv_w, v_ffn_conv_b, v_ffn_w_out, v_ln_mix_g, v_ln_mix_b, v_ln_ffn_g, v_ln_ffn_b):
    xi, yi, ci = _place()
    chip = 2 * xi + yi
    Dm = D_MODEL
    FC = 2 * FFN_DIM // N_CHIPS
    Fo = FFN_DIM // N_CHIPS
    Dq = Dm // N_CHIPS
    bf = lambda a: a.astype(BF16)

    shard0 = _pack_rows([bf(hgrn_w_in), bf(hgrn_w_out)])
    shard1 = _pack_rows([bf(swa_w_q), bf(swa_w_out), bf(shared_w_kv), bf(ffn_w_in[0]), bf(ffn_w_out[0])])
    shard2 = _pack_rows([bf(ffn_w_in[1]), bf(ffn_w_out[1])])
    all0 = _gather_chips(shard0, name="gather_w0")
    handle1, token1 = _gather_start(shard1, all0, name="gather_w1_start")
    w_in, w_hg_out = _unpack_rows(all0, [(Dm, Dm), (Dq, Dm)])

    def ffn_weights(w_fi, w_fo, l):
        return {"ffn_in_a": {l: jnp.concatenate([w_fi[0], w_fi[1]], axis=1)},
                "ffn_in_b": {l: jnp.concatenate([w_fi[2], w_fi[3]], axis=1)},
                "ffn_out": {l: w_fo.reshape(FFN_DIM, Dm)}}

    got = {}

    def more_weights(k, after):
        shard = (shard1, shard2)[k - 1]
        land = _gather_wait(got.pop("handle"), after, name=f"gather_w{k}_wait")
        land = _fill_sibling(land, name=f"gather_w{k}_fill")
        allk = lax.dynamic_update_slice(land, shard[None], (chip, 0, 0))
        if k == 1:
            got["handle"], token2 = _gather_start(shard2, land, name="gather_w2_start")
            w_q, w_o, w_kv, w_fi, w_fo = _unpack_rows(allk, [(Dq, Dm), (Dq, Dm), (Dq, 2 * KV_DIM), (Dm, FC), (Fo, Dm)])
            got.update(ffn_weights(w_fi, w_fo, 0))
            return {"sw_q": w_q.reshape(Dm, Dm), "sw_out": w_o.reshape(Dm, Dm), "kv": w_kv.reshape(Dm, 2 * KV_DIM),
                    "token": token2, **{n: got[n] for n in ("ffn_in_a", "ffn_in_b", "ffn_out")}}
        w_fi, w_fo = _unpack_rows(allk, [(Dm, FC), (Fo, Dm)])
        new = ffn_weights(w_fi, w_fo, 1)
        return {n: {**got[n], **new[n]} for n in new}

    got["handle"] = handle1

    lb_full = lax.dynamic_update_slice(jnp.zeros((2, Dm), F32), hgrn_lb_logits, (0, chip * Dq))
    cw_full = lax.dynamic_update_slice(jnp.zeros((DEPTH, 3, 2 * FFN_DIM), F32), ffn_conv_w, (0, 0, chip * FC))
    only_south = (ci == 0).astype(F32)
    small_in = _sum8(_flat128([lb_full, cw_full]) * only_south, name="gather_small")
    lb_full, cw_full = _unflat128(small_in, [(2, Dm), (DEPTH, 3, 2 * FFN_DIM)])
    w = {
        "hg_in": [w_in[j] for j in range(4)], "hg_out": w_hg_out.reshape(Dm, Dm), "token": token1,
        "lb_logits": lb_full, "gnorm": hgrn_gnorm_w, "sinks": swa_sinks, "rel_bias": rel_bias,
        "conv_w_a": [cw_full[l, :, :FFN_DIM] for l in range(DEPTH)],
        "conv_w_b": [cw_full[l, :, FFN_DIM:] for l in range(DEPTH)],
        "conv_b_a": [ffn_conv_b[l:l + 1, :FFN_DIM] for l in range(DEPTH)],
        "conv_b_b": [ffn_conv_b[l:l + 1, FFN_DIM:] for l in range(DEPTH)],
        "ln_mix_g": [ln_mix_g[l:l + 1] for l in range(DEPTH)], "ln_mix_b": [ln_mix_b[l:l + 1] for l in range(DEPTH)],
        "ln_ffn_g": [ln_ffn_g[l:l + 1] for l in range(DEPTH)], "ln_ffn_b": [ln_ffn_b[l:l + 1] for l in range(DEPTH)],
    }

    sent = {}

    def ffn_pieces(gd):
        return [jnp.concatenate([gd["ffn_in_a"], gd["ffn_in_b"]], axis=0), gd["ffn_out"].reshape(N_CHIPS, Fo, Dm)]

    def emit(k, gd):
        rows4 = lambda a: a.reshape(N_CHIPS, Dq, a.shape[-1])
        if k == 1:
            pieces = [rows4(gd["sw_q"]), rows4(gd["sw_out"]), rows4(gd["kv"])] + ffn_pieces(gd)
        elif k == 2:
            pieces = ffn_pieces(gd) + [rows4(gd["hg_out"])]
        else:
            pieces = [jnp.stack(gd["hg_in"])]
        handle, token = _scatter_start([p.astype(BF16) for p in pieces], name=f"scatter_g{k}_start")
        sent[k] = (handle, pieces)
        return token

    loss_tile, grad_x, g = _local_step(x[0], loss_target[0], w, more_weights, emit)

    wts = dict(hgrn_w_in=hgrn_w_in, hgrn_lb_logits=hgrn_lb_logits, hgrn_gnorm_w=hgrn_gnorm_w, hgrn_w_out=hgrn_w_out,
               swa_w_q=swa_w_q, swa_sinks=swa_sinks, swa_w_out=swa_w_out, shared_w_kv=shared_w_kv, rel_bias=rel_bias,
               ffn_w_in=ffn_w_in, ffn_conv_w=ffn_conv_w, ffn_conv_b=ffn_conv_b, ffn_w_out=ffn_w_out,
               ln_mix_g=ln_mix_g, ln_mix_b=ln_mix_b, ln_ffn_g=ln_ffn_g, ln_ffn_b=ln_ffn_b)
    ms = dict(hgrn_w_in=m_hgrn_w_in, hgrn_lb_logits=m_hgrn_lb_logits, hgrn_gnorm_w=m_hgrn_gnorm_w, hgrn_w_out=m_hgrn_w_out,
              swa_w_q=m_swa_w_q, swa_sinks=m_swa_sinks, swa_w_out=m_swa_w_out, shared_w_kv=m_shared_w_kv, rel_bias=m_rel_bias,
              ffn_w_in=m_ffn_w_in, ffn_conv_w=m_ffn_conv_w, ffn_conv_b=m_ffn_conv_b, ffn_w_out=m_ffn_w_out,
              ln_mix_g=m_ln_mix_g, ln_mix_b=m_ln_mix_b, ln_ffn_g=m_ln_ffn_g, ln_ffn_b=m_ln_ffn_b)
    vs = dict(hgrn_w_in=v_hgrn_w_in, hgrn_lb_logits=v_hgrn_lb_logits, hgrn_gnorm_w=v_hgrn_gnorm_w, hgrn_w_out=v_hgrn_w_out,
              swa_w_q=v_swa_w_q, swa_sinks=v_swa_sinks, swa_w_out=v_swa_w_out, shared_w_kv=v_shared_w_kv, rel_bias=v_rel_bias,
              ffn_w_in=v_ffn_w_in, ffn_conv_w=v_ffn_conv_w, ffn_conv_b=v_ffn_conv_b, ffn_w_out=v_ffn_w_out,
              ln_mix_g=v_ln_mix_g, ln_mix_b=v_ln_mix_b, ln_ffn_g=v_ln_ffn_g, ln_ffn_b=v_ln_ffn_b)
    names = list(wts)
    grads, delta, new_m, new_v = {}, {}, {}, {}

    def update(n, ga, gb, layer=None, prev=None):
        r2 = lambda a: a.reshape(-1, a.shape[-1])
        rows = None if layer is None else (layer * ga.shape[0], ga.shape[0])
        return _adamw(r2(wts[n]), ga, gb, r2(ms[n]), r2(vs[n]), rows=rows, prev=prev,
                      name=f"adamw_{n}" + ("" if layer is None else f"_{layer}"))

    def keep(n, res):
        grads[n], delta[n], new_m[n], new_v[n] = [a.reshape(wts[n].shape) for a in res]

    chip1 = jnp.reshape(chip, (1,)).astype(jnp.int32)
    after = grad_x
    for k in (1, 2, 3):
        handle, pieces = sent[k]
        lands = _scatter_wait(handle, after, name=f"scatter_g{k}_wait")
        parts = [_chip_sum(p, l, chip1, name=f"scatter_g{k}_sum{i}") for i, (p, l) in enumerate(zip(pieces, lands))]
        sibs = _swap_sibling(parts, name=f"scatter_g{k}_swap")
        if k == 1:
            for n, ga, gb in zip(["swa_w_q", "swa_w_out", "shared_w_kv"], parts[:3], sibs[:3]):
                keep(n, update(n, ga, gb))
            ffn_in_1 = update("ffn_w_in", parts[3], sibs[3], layer=1)
            ffn_out_1 = update("ffn_w_out", parts[4], sibs[4], layer=1)
            after = ffn_out_1[3]
        elif k == 2:
            keep("ffn_w_in", update("ffn_w_in", parts[0], sibs[0], layer=0, prev=ffn_in_1))
            keep("ffn_w_out", update("ffn_w_out", parts[1], sibs[1], layer=0, prev=ffn_out_1))
            keep("hgrn_w_out", update("hgrn_w_out", parts[2], sibs[2]))
            after = new_v["hgrn_w_out"]
        else:
            keep("hgrn_w_in", update("hgrn_w_in", parts[0], sibs[0]))

    small_shapes = [(SUBLANES, LANES), (2, Dm), (1, HG_DIM), (1, SW_Q_HEADS), (REL_BUCKETS, SW_Q_HEADS),
                    (DEPTH, 3, 2 * FFN_DIM), (DEPTH, 2 * FFN_DIM)] + [(DEPTH, Dm)] * 4
    gc = g["conv"]
    conv_w_g = jnp.stack([jnp.concatenate([gc[l]["conv_w_a"], gc[l]["conv_w_b"]], axis=1) for l in range(DEPTH)])
    conv_b_g = jnp.concatenate([jnp.concatenate([gc[l]["conv_b_a"], gc[l]["conv_b_b"]], axis=1) for l in range(DEPTH)], axis=0)
    ln_g = [jnp.concatenate([g[f"{n}0"], g[f"{n}1"]], axis=0) for n in ("ln_mix_g", "ln_mix_b", "ln_ffn_g", "ln_ffn_b")]
    small_out = _sum8(_flat128([loss_tile, g["lb_logits"], g["gnorm"], g["sinks"], g["rel_bias"], conv_w_g, conv_b_g] + ln_g),
                      name="sum_small")
    (loss_t, g_lb, g_gn, g_sinks, g_rel, g_cw, g_cb, g_lmg, g_lmb, g_lfg, g_lfb) = _unflat128(small_out, small_shapes)
    loss = loss_t[0, 0]
    g_lb = lax.dynamic_slice_in_dim(g_lb, chip * Dq, Dq, axis=1)
    g_cw = lax.dynamic_slice_in_dim(g_cw, chip * FC, FC, axis=2)
    small_g = dict(hgrn_lb_logits=g_lb, hgrn_gnorm_w=g_gn, swa_sinks=g_sinks, rel_bias=g_rel, ffn_conv_w=g_cw,
                   ffn_conv_b=g_cb, ln_mix_g=g_lmg, ln_mix_b=g_lmb, ln_ffn_g=g_lfg, ln_ffn_b=g_lfb)
    small_names = list(small_g)
    sshapes = [wts[n].shape for n in small_names]
    _, d_, m_, v_ = _adamw(_flat128([wts[n] for n in small_names]), _flat128([small_g[n] for n in small_names]), None,
                           _flat128([ms[n] for n in small_names]), _flat128([vs[n] for n in small_names]), name="adamw_small")
    for n, a, b_, c_ in zip(small_names, _unflat128(d_, sshapes), _unflat128(m_, sshapes), _unflat128(v_, sshapes)):
        grads[n], delta[n], new_m[n], new_v[n] = small_g[n], a, b_, c_

    return (loss, grad_x[None], *[grads[n] for n in names], *[delta[n] for n in names],
            *[new_m[n] for n in names], *[new_v[n] for n in names])
```

```python
import functools
import math

import numpy as np
import jax
import jax.numpy as jnp
from jax import lax
from jax.experimental import pallas as pl
from jax.experimental.pallas import tpu as pltpu

F32 = jnp.float32
BF16 = jnp.bfloat16
MESH = pl.DeviceIdType.MESH

D_MODEL = 1024
DEPTH = 2
HG_HEADS = 8
HG_DIM = 128
SW_Q_HEADS = 16
SW_KV_HEADS = 4
SW_HEAD_DIM = 64
SW_GROUP = 4
SW_WINDOW = 128
REL_BUCKETS = 32
REL_MAX_DIST = 128
FFN_DIM = 2816
ALPHA = (2.0 * DEPTH) ** 0.25
LN_EPS = 1e-5
RMS_EPS = 1e-6
ADAM_LR = 0.001
ADAM_B1 = 0.9
ADAM_B2 = 0.999
ADAM_EPS = 1e-08
ADAM_WD = 0.01
ADAM_STEP = 10

VMEM_BYTES_V7X = 64 * 1024 * 1024
VMEM_LIMIT = VMEM_BYTES_V7X - 8 * 1024 * 1024
LANES = 128
SUBLANES = 8

HG_C = 64
HG_RB = 256
ROW_TILE = 256
CONV_R = 1024
N_CHIPS = 4
N_DEV = 8

ANY_SPEC = pl.BlockSpec(memory_space=pl.ANY)


def _after(body, n_in, after):
    if after is None:
        return body, [], ()

    def wrapped(*refs):
        return body(*refs[:n_in], *refs[n_in + 1:])

    return wrapped, [ANY_SPEC], (after,)


def _params(sem=None):
    return pltpu.CompilerParams(dimension_semantics=sem, vmem_limit_bytes=VMEM_LIMIT)


def _tile(n, pref, unit=LANES):
    if n <= pref:
        return n
    best = None
    for t in range(unit, pref + 1, unit):
        if n % t == 0:
            best = t
    assert best is not None, (n, pref, unit)
    return best


def _dot(a, b, ca, cb):
    nb = a.ndim - 2
    batch = tuple(range(nb))
    return lax.dot_general(a.astype(BF16), b.astype(BF16), (((nb + ca,), (nb + cb,)), (batch, batch)),
                           preferred_element_type=F32)


@jax.custom_vjp
def mm(a, b):
    return _dot(a, b, 1, 0)


@jax.custom_vjp
def mm_nt(a, b):
    return _dot(a, b, 1, 1)


@jax.custom_vjp
def mm_tn(a, b):
    return _dot(a, b, 0, 0)


mm.defvjp(lambda a, b: (mm(a, b), (a, b)), lambda r, ct: (mm_nt(ct, r[1]), mm_tn(r[0], ct)))
mm_nt.defvjp(lambda a, b: (mm_nt(a, b), (a, b)), lambda r, ct: (mm(ct, r[1]), mm_tn(ct, r[0])))
mm_tn.defvjp(lambda a, b: (mm_tn(a, b), (a, b)), lambda r, ct: (mm_nt(r[1], ct), mm(r[0], ct)))


def _split2(x):
    hi = x.astype(BF16)
    return hi, (x - hi.astype(F32)).astype(BF16)


@jax.custom_vjp
def _scores(qt, kt):
    return _dot(qt, kt, 1, 1)


def _scores_bwd(r, ct):
    (qh, ql), (kh, kl) = _split2(r[0]), _split2(r[1])
    return _dot(ct, kh, 1, 0) + _dot(ct, kl, 1, 0), _dot(ct, qh, 0, 0) + _dot(ct, ql, 0, 0)


_scores.defvjp(lambda a, b: (_scores(a, b), (a, b)), _scores_bwd)


def _split3(x):
    hi = x.astype(BF16)
    r1 = x - hi.astype(F32)
    mid = r1.astype(BF16)
    lo = (r1 - mid.astype(F32)).astype(BF16)
    return hi, mid, lo


def _cumsum_impl(x):
    ax = x.ndim - 2
    n = x.shape[ax]
    row = lax.broadcasted_iota(jnp.int32, x.shape, ax)
    d = 1
    while d < n:
        x = x + jnp.where(row >= d, pltpu.roll(x, d, ax), 0.0)
        d *= 2
    return x


def _cumsum_rev_impl(x):
    ax = x.ndim - 2
    n = x.shape[ax]
    row = lax.broadcasted_iota(jnp.int32, x.shape, ax)
    d = 1
    while d < n:
        x = x + jnp.where(row < n - d, pltpu.roll(x, n - d, ax), 0.0)
        d *= 2
    return x


@jax.custom_vjp
def _cumsum(x):
    return _cumsum_impl(x)


_cumsum.defvjp(lambda x: (_cumsum_impl(x), None), lambda _, ct: (_cumsum_rev_impl(ct),))


def _matmul(a, b, *, mode, name, out_dtype=F32, add=None, add_scale=1.0, tm=512, tn=1408, tk=1408, after=None,
            split_n=False):
    if mode == "nn":
        (M, K), (K2, N) = a.shape, b.shape
    elif mode == "nt":
        (M, K), (N, K2) = a.shape, b.shape
    else:
        (K, M), (K2, N) = a.shape, b.shape
    assert K == K2, (a.shape, b.shape, mode)
    tm, tn, tk = _tile(M, tm), _tile(N, tn), _tile(K, tk)
    nk = K // tk
    ca, cb = {"nn": (1, 0), "nt": (1, 1), "tn": (0, 0)}[mode]
    a_spec = {"nn": pl.BlockSpec((tm, tk), lambda i, j, k: (i, k)),
              "nt": pl.BlockSpec((tm, tk), lambda i, j, k: (i, k)),
              "tn": pl.BlockSpec((tk, tm), lambda i, j, k: (k, i))}[mode]
    b_spec = {"nn": pl.BlockSpec((tk, tn), lambda i, j, k: (k, j)),
              "nt": pl.BlockSpec((tn, tk), lambda i, j, k: (j, k)),
              "tn": pl.BlockSpec((tk, tn), lambda i, j, k: (k, j))}[mode]
    o_spec = pl.BlockSpec((tm, tn), lambda i, j, k: (i, j))
    has_add = add is not None

    def finish(r, add_ref, o_ref):
        if has_add:
            r = r + add_scale * add_ref[...]
        o_ref[...] = r.astype(out_dtype)

    def body(*refs):
        a_ref, b_ref = refs[:2]
        add_ref = refs[2] if has_add else None
        o_ref = refs[3 if has_add else 2]
        if nk == 1:
            finish(_dot(a_ref[...], b_ref[...], ca, cb), add_ref, o_ref)
            return
        acc_ref = refs[-1]
        k = pl.program_id(2)

        @pl.when(k == 0)
        def _():
            acc_ref[...] = jnp.zeros_like(acc_ref)

        acc_ref[...] += _dot(a_ref[...], b_ref[...], ca, cb)

        @pl.when(k == nk - 1)
        def _():
            finish(acc_ref[...], add_ref, o_ref)

    in_specs = [a_spec, b_spec] + ([o_spec] if has_add else [])
    args = (a, b) + ((add,) if has_add else ())
    body, xs, xa = _after(body, len(args), after)
    in_specs, args = in_specs + xs, args + xa
    out_shape = (M, N)
    if split_n:
        assert not has_add and M == tm
        o_spec = pl.BlockSpec((None, tm, tn), lambda i, j, k: (j, 0, 0))
        out_shape = (N // tn, M, tn)
    return pl.pallas_call(
        body, name=name, grid=(M // tm, N // tn, nk), in_specs=in_specs, out_specs=o_spec,
        out_shape=jax.ShapeDtypeStruct(out_shape, out_dtype),
        scratch_shapes=[pltpu.VMEM((tm, tn), F32)] if nk > 1 else [],
        compiler_params=_params(("parallel", "parallel", "arbitrary")),
    )(*args)


def _ln(z, g, b):
    mu = jnp.mean(z, axis=-1, keepdims=True)
    zc = z - mu
    var = jnp.mean(zc * zc, axis=-1, keepdims=True)
    return zc * lax.rsqrt(var + LN_EPS) * g + b


def _matmul_ln(a, b, h, g, bias, *, name, tgt=None, tm=512):
    (T, K), (K2, Dm) = a.shape, b.shape
    assert K == K2 and h.shape == (T, Dm)
    tm = _tile(T, tm, SUBLANES)
    last = tgt is not None

    def body(*refs):
        a_ref, b_ref, h_ref, g_ref, bias_ref = refs[:5]
        z = ALPHA * h_ref[...] + _dot(a_ref[...], b_ref[...], 1, 0)
        if not last:
            z_ref, y_ref, yb_ref = refs[5:]
            y = _ln(z, g_ref[...], bias_ref[...])
            z_ref[...] = z
            y_ref[...] = y
            yb_ref[...] = y.astype(BF16)
            return
        t_ref, dz_ref, dzb_ref, dg_ref, db_ref, l_ref = refs[5:]

        @pl.when(pl.program_id(0) == 0)
        def _():
            dg_ref[...] = jnp.zeros_like(dg_ref)
            db_ref[...] = jnp.zeros_like(db_ref)
            l_ref[...] = jnp.zeros_like(l_ref)

        y, vjp = jax.vjp(_ln, z, g_ref[...], bias_ref[...])
        e = y - t_ref[...]
        dz, dg, db = vjp(e * (1.0 / Dm))
        l_ref[...] += 0.5 * jnp.sum(jnp.mean(e * e, axis=-1, keepdims=True), axis=0, keepdims=True)
        dz_ref[...] = dz
        dzb_ref[...] = dz.astype(BF16)
        dg_ref[...] += dg
        db_ref[...] += db

    row = pl.BlockSpec((tm, Dm), lambda i: (i, 0))
    vec = pl.BlockSpec((1, Dm), lambda i: (0, 0))
    in_specs = [pl.BlockSpec((tm, K), lambda i: (i, 0)), pl.BlockSpec((K, Dm), lambda i: (0, 0)), row, vec, vec]
    f32, b16 = jax.ShapeDtypeStruct((T, Dm), F32), jax.ShapeDtypeStruct((T, Dm), BF16)
    if not last:
        return pl.pallas_call(
            body, name=name, grid=(T // tm,), in_specs=in_specs, out_specs=[row, row, row], out_shape=[f32, f32, b16],
            compiler_params=_params(("parallel",)),
        )(a, b, h, g, bias)
    return pl.pallas_call(
        body, name=name, grid=(T // tm,), in_specs=in_specs + [row],
        out_specs=[row, row, vec, vec, pl.BlockSpec((SUBLANES, LANES), lambda i: (0, 0))],
        out_shape=[f32, b16, jax.ShapeDtypeStruct((1, Dm), F32), jax.ShapeDtypeStruct((1, Dm), F32),
                   jax.ShapeDtypeStruct((SUBLANES, LANES), F32)],
        compiler_params=_params(("arbitrary",)),
    )(a, b, h, g, bias, tgt)


def _ln_bwd(dy, z, g, b, *, name, after=None):
    T, Dm = z.shape
    tr = _tile(T, ROW_TILE, SUBLANES)

    def body(dy_ref, z_ref, g_ref, b_ref, dz_ref, dzb_ref, dg_ref, db_ref):
        @pl.when(pl.program_id(0) == 0)
        def _():
            dg_ref[...] = jnp.zeros_like(dg_ref)
            db_ref[...] = jnp.zeros_like(db_ref)

        _, vjp = jax.vjp(_ln, z_ref[...], g_ref[...], b_ref[...])
        dz, dg, db = vjp(dy_ref[...])
        dz_ref[...] = dz
        dzb_ref[...] = dz.astype(BF16)
        dg_ref[...] += dg
        db_ref[...] += db

    row = pl.BlockSpec((tr, Dm), lambda i: (i, 0))
    vec = pl.BlockSpec((1, Dm), lambda i: (0, 0))
    body, xs, xa = _after(body, 4, after)
    return pl.pallas_call(
        body, name=name, grid=(T // tr,), in_specs=[row, row, vec, vec] + xs,
        out_specs=[row, row, vec, vec],
        out_shape=[jax.ShapeDtypeStruct((T, Dm), F32), jax.ShapeDtypeStruct((T, Dm), BF16),
                   jax.ShapeDtypeStruct((1, Dm), F32), jax.ShapeDtypeStruct((1, Dm), F32)],
        compiler_params=_params(("arbitrary",)),
    )(dy, z, g, b, *xa)


def _hg_chunk(qr, fr, ir, gr, l0, l1, gw, st):
    C = qr.shape[-2]
    row = lax.broadcasted_iota(jnp.int32, qr.shape, qr.ndim - 2)
    lb = jax.nn.sigmoid(l0 - l1)
    fg = lb + (1.0 - lb) * jax.nn.sigmoid(fr)
    b = _cumsum(jnp.log(fg))
    q = jax.nn.silu(qr)
    k = 1.0 - fg
    bmid = lax.stop_gradient(jnp.sum(jnp.where(row == C // 2 - 1, b, 0.0), axis=-2, keepdims=True))
    bl = jnp.sum(jnp.where(row == C - 1, b, 0.0), axis=-2, keepdims=True)
    o = mm_nt(q * jnp.exp(b), st)
    sc = _scores(q * jnp.exp(b - bmid), k * jnp.exp(bmid - b))
    ti = lax.broadcasted_iota(jnp.int32, (C, C), 0)
    si = lax.broadcasted_iota(jnp.int32, (C, C), 1)
    sc = jnp.where(si <= ti, sc, 0.0)
    o = o + mm(sc, ir)
    st_new = st * jnp.exp(bl) + mm_tn(ir, k * jnp.exp(bl - b))
    on = o * lax.rsqrt(jnp.mean(o * o, axis=-1, keepdims=True) + RMS_EPS)
    return on * gw * jax.nn.silu(gr), st_new


def _heads(ref, rows):
    return jnp.stack([ref[rows, h * HG_DIM:(h + 1) * HG_DIM].astype(F32) for h in range(HG_HEADS)])


def _unheads(x):
    return jnp.concatenate([x[h] for h in range(HG_HEADS)], axis=-1)


def _hgrn_fwd(q, f, i, g, lbl, gw, *, name):
    T, Dm = q.shape
    rb = min(HG_RB, T)
    C = min(HG_C, rb)
    ncb = rb // C

    def body(q_ref, f_ref, i_ref, g_ref, lbl_ref, gw_ref, o_ref, st_ref, s_ref):
        @pl.when(pl.program_id(0) == 0)
        def _():
            s_ref[...] = jnp.zeros_like(s_ref)

        def chunk(ci, carry):
            r0 = pl.multiple_of(ci * C, C)
            rows = pl.ds(r0, C)
            st = s_ref[...]
            st_ref[ci] = st
            out, st_new = _hg_chunk(_heads(q_ref, rows), _heads(f_ref, rows), _heads(i_ref, rows), _heads(g_ref, rows),
                                    _heads(lbl_ref, slice(0, 1)), _heads(lbl_ref, slice(1, 2)), gw_ref[...], st)
            o_ref[rows, :] = _unheads(out).astype(BF16)
            s_ref[...] = st_new
            return carry

        lax.fori_loop(0, ncb, chunk, 0, unroll=True)

    row = pl.BlockSpec((rb, Dm), lambda n: (n, 0))
    return pl.pallas_call(
        body, name=name, grid=(T // rb,),
        in_specs=[row, row, row, row, pl.BlockSpec((2, Dm), lambda n: (0, 0)), pl.BlockSpec((1, HG_DIM), lambda n: (0, 0))],
        out_specs=[row, pl.BlockSpec((ncb, HG_HEADS, HG_DIM, HG_DIM), lambda n: (n, 0, 0, 0))],
        out_shape=[jax.ShapeDtypeStruct((T, Dm), BF16),
                   jax.ShapeDtypeStruct((T // C, HG_HEADS, HG_DIM, HG_DIM), F32)],
        scratch_shapes=[pltpu.VMEM((HG_HEADS, HG_DIM, HG_DIM), F32)],
        compiler_params=_params(("arbitrary",)),
    )(q, f, i, g, lbl, gw)


def _hgrn_bwd(q, f, i, g, lbl, gw, states, dout, *, name, after=None):
    T, Dm = q.shape
    rb = min(HG_RB, T)
    C = min(HG_C, rb)
    ncb = rb // C
    nb = T // rb

    def body(q_ref, f_ref, i_ref, g_ref, lbl_ref, gw_ref, st_ref, do_ref,
             dq_ref, df_ref, di_ref, dg_ref, dlbl_ref, dgw_ref, ds_ref):
        @pl.when(pl.program_id(0) == 0)
        def _():
            ds_ref[...] = jnp.zeros_like(ds_ref)
            dlbl_ref[...] = jnp.zeros_like(dlbl_ref)
            dgw_ref[...] = jnp.zeros_like(dgw_ref)

        def chunk(cj, carry):
            ci = ncb - 1 - cj
            r0 = pl.multiple_of(ci * C, C)
            rows = pl.ds(r0, C)
            _, vjp = jax.vjp(_hg_chunk, _heads(q_ref, rows), _heads(f_ref, rows), _heads(i_ref, rows), _heads(g_ref, rows),
                             _heads(lbl_ref, slice(0, 1)), _heads(lbl_ref, slice(1, 2)), gw_ref[...], st_ref[ci])
            dq, df, di, dg, dl0, dl1, dgw, dst = vjp((_heads(do_ref, rows).astype(F32), ds_ref[...]))
            dq_ref[rows, :] = _unheads(dq).astype(BF16)
            df_ref[rows, :] = _unheads(df).astype(BF16)
            di_ref[rows, :] = _unheads(di).astype(BF16)
            dg_ref[rows, :] = _unheads(dg).astype(BF16)
            dlbl_ref[0:1, :] += _unheads(dl0)
            dlbl_ref[1:2, :] += _unheads(dl1)
            dgw_ref[...] += dgw
            ds_ref[...] = dst
            return carry

        lax.fori_loop(0, ncb, chunk, 0, unroll=True)

    row = pl.BlockSpec((rb, Dm), lambda n: (nb - 1 - n, 0))
    lsp = pl.BlockSpec((2, Dm), lambda n: (0, 0))
    gsp = pl.BlockSpec((1, HG_DIM), lambda n: (0, 0))
    body, xs, xa = _after(body, 8, after)
    return pl.pallas_call(
        body, name=name, grid=(nb,),
        in_specs=[row, row, row, row, lsp, gsp,
                  pl.BlockSpec((ncb, HG_HEADS, HG_DIM, HG_DIM), lambda n: (nb - 1 - n, 0, 0, 0)), row] + xs,
        out_specs=[row, row, row, row, lsp, gsp],
        out_shape=[jax.ShapeDtypeStruct((T, Dm), BF16)] * 4
        + [jax.ShapeDtypeStruct((2, Dm), F32), jax.ShapeDtypeStruct((1, HG_DIM), F32)],
        scratch_shapes=[pltpu.VMEM((HG_HEADS, HG_DIM, HG_DIM), F32)],
        compiler_params=_params(("arbitrary",)),
    )(q, f, i, g, lbl, gw, states, dout, *xa)


CONV_HALO = 2 * SUBLANES


def _conv_rows(u_ref, scr, w, bias, r0, R):
    cur = u_ref[pl.ds(r0, R), :].astype(F32)
    p0 = pl.multiple_of(jnp.maximum(r0 - CONV_HALO, 0), CONV_HALO)
    scr[0:CONV_HALO, :] = jnp.where(r0 > 0, u_ref[pl.ds(p0, CONV_HALO), :].astype(F32), 0.0)
    scr[CONV_HALO:CONV_HALO + R, :] = cur
    s1 = scr[CONV_HALO - 1:CONV_HALO - 1 + R, :]
    s2 = scr[CONV_HALO - 2:CONV_HALO - 2 + R, :]
    return w[0:1, :] * s2 + w[1:2, :] * s1 + w[2:3, :] * cur + bias, cur, s1, s2


def _conv_gate_fwd(ua, ub, wa, wb, ba, bb, *, name):
    T, Fd = ua.shape
    R = min(CONV_R, T)
    tc = LANES

    def body(ua_ref, ub_ref, wa_ref, wb_ref, ba_ref, bb_ref, o_ref, sa, sb):
        wa_, wb_, ba_, bb_ = wa_ref[...], wb_ref[...], ba_ref[...], bb_ref[...]

        def step(ri, carry):
            r0 = pl.multiple_of(ri * R, R)
            ca = _conv_rows(ua_ref, sa, wa_, ba_, r0, R)[0]
            cb = _conv_rows(ub_ref, sb, wb_, bb_, r0, R)[0]
            o_ref[pl.ds(r0, R), :] = (jax.nn.silu(ca) * cb).astype(BF16)
            return carry

        lax.fori_loop(0, T // R, step, 0)

    col = pl.BlockSpec((T, tc), lambda j: (0, j))
    wsp = pl.BlockSpec((3, tc), lambda j: (0, j))
    bsp = pl.BlockSpec((1, tc), lambda j: (0, j))
    return pl.pallas_call(
        body, name=name, grid=(Fd // tc,), in_specs=[col, col, wsp, wsp, bsp, bsp], out_specs=col,
        out_shape=jax.ShapeDtypeStruct((T, Fd), BF16),
        scratch_shapes=[pltpu.VMEM((CONV_HALO + R, tc), F32)] * 2,
        compiler_params=_params(("parallel",)),
    )(ua, ub, wa, wb, ba, bb)


def _conv_gate_bwd(ua, ub, wa, wb, ba, bb, dact, *, name):
    T, Fd = ua.shape
    R = min(CONV_R, T)
    nr = T // R
    tc = LANES

    def body(ua_ref, ub_ref, wa_ref, wb_ref, ba_ref, bb_ref, da_ref,
             dua_ref, dub_ref, dwa_ref, dwb_ref, dba_ref, dbb_ref, sa, sb, sda, sdb):
        wa_, wb_, ba_, bb_ = wa_ref[...], wb_ref[...], ba_ref[...], bb_ref[...]
        sda[R:R + SUBLANES, :] = jnp.zeros((SUBLANES, tc), F32)
        sdb[R:R + SUBLANES, :] = jnp.zeros((SUBLANES, tc), F32)

        def taps(dc, cur, s1, s2):
            return jnp.concatenate([jnp.sum(dc * s2, axis=0, keepdims=True), jnp.sum(dc * s1, axis=0, keepdims=True),
                                    jnp.sum(dc * cur, axis=0, keepdims=True)], axis=0)

        def du_rows(sd, dc, w):
            sd[0:R, :] = dc
            du = w[2:3, :] * dc + w[1:2, :] * sd[1:1 + R, :] + w[0:1, :] * sd[2:2 + R, :]
            sd[R:R + SUBLANES, :] = dc[0:SUBLANES]
            return du

        def step(rj, carry):
            dwa, dwb, dba, dbb = carry
            r0 = pl.multiple_of((nr - 1 - rj) * R, R)
            ca, cura, s1a, s2a = _conv_rows(ua_ref, sa, wa_, ba_, r0, R)
            cb, curb, s1b, s2b = _conv_rows(ub_ref, sb, wb_, bb_, r0, R)
            dact_ = da_ref[pl.ds(r0, R), :].astype(F32)
            sg = jax.nn.sigmoid(ca)
            dca = dact_ * cb * (sg * (1.0 + ca * (1.0 - sg)))
            dcb = dact_ * (ca * sg)
            dua_ref[pl.ds(r0, R), :] = du_rows(sda, dca, wa_).astype(BF16)
            dub_ref[pl.ds(r0, R), :] = du_rows(sdb, dcb, wb_).astype(BF16)
            return (dwa + taps(dca, cura, s1a, s2a), dwb + taps(dcb, curb, s1b, s2b),
                    dba + jnp.sum(dca, axis=0, keepdims=True), dbb + jnp.sum(dcb, axis=0, keepdims=True))

        z3 = jnp.zeros((3, tc), F32)
        z1 = jnp.zeros((1, tc), F32)
        dwa, dwb, dba, dbb = lax.fori_loop(0, nr, step, (z3, z3, z1, z1))
        dwa_ref[...] = dwa
        dwb_ref[...] = dwb
        dba_ref[...] = dba
        dbb_ref[...] = dbb

    col = pl.BlockSpec((T, tc), lambda j: (0, j))
    wsp = pl.BlockSpec((3, tc), lambda j: (0, j))
    bsp = pl.BlockSpec((1, tc), lambda j: (0, j))
    return pl.pallas_call(
        body, name=name, grid=(Fd // tc,), in_specs=[col, col, wsp, wsp, bsp, bsp, col],
        out_specs=[col, col, wsp, wsp, bsp, bsp],
        out_shape=[jax.ShapeDtypeStruct((T, Fd), BF16)] * 2 + [jax.ShapeDtypeStruct((3, Fd), F32)] * 2
        + [jax.ShapeDtypeStruct((1, Fd), F32)] * 2,
        scratch_shapes=[pltpu.VMEM((CONV_HALO + R, tc), F32)] * 2 + [pltpu.VMEM((R + SUBLANES, tc), F32)] * 2,
        compiler_params=_params(("parallel",)),
    )(ua, ub, wa, wb, ba, bb, dact)


def _bucket_index():
    t = np.arange(SW_WINDOW)[:, None] + SW_WINDOW
    s = np.arange(2 * SW_WINDOW)[None, :]
    dist = np.maximum(t - s, 0)
    exact = REL_BUCKETS // 2
    d = np.maximum(dist, 1).astype(np.float32)
    log_b = exact + (np.log(d / np.float32(exact)) / np.float32(math.log(REL_MAX_DIST / exact))
                     * np.float32(REL_BUCKETS - exact)).astype(np.int32)
    bucket = np.where(dist < exact, dist, np.minimum(log_b, REL_BUCKETS - 1))
    return bucket.astype(np.int32).reshape(1, -1)


BIAS_COLS = SW_WINDOW * 2 * SW_WINDOW
BIAS_TILE = 4096


def _bias_from_table(table, bucket, *, name):
    def body(t_ref, idx_ref, o_ref):
        onehot = (lax.broadcasted_iota(jnp.int32, (REL_BUCKETS, BIAS_TILE), 0) == idx_ref[...]).astype(BF16)
        acc = jnp.zeros((SW_Q_HEADS, BIAS_TILE), F32)
        for piece in _split3(t_ref[...]):
            acc = acc + lax.dot_general(piece, onehot, (((0,), (0,)), ((), ())), preferred_element_type=F32)
        o_ref[...] = acc

    return pl.pallas_call(
        body, name=name, grid=(BIAS_COLS // BIAS_TILE,),
        in_specs=[pl.BlockSpec((REL_BUCKETS, SW_Q_HEADS), lambda j: (0, 0)), pl.BlockSpec((1, BIAS_TILE), lambda j: (0, j))],
        out_specs=pl.BlockSpec((SW_Q_HEADS, BIAS_TILE), lambda j: (0, j)),
        out_shape=jax.ShapeDtypeStruct((SW_Q_HEADS, BIAS_COLS), F32),
        compiler_params=_params(("parallel",)),
    )(table, bucket)


def _table_grad(dbias, bucket, *, name):
    def body(d_ref, idx_ref, o_ref):
        @pl.when(pl.program_id(0) == 0)
        def _():
            o_ref[...] = jnp.zeros_like(o_ref)

        onehot = (lax.broadcasted_iota(jnp.int32, (REL_BUCKETS, BIAS_TILE), 0) == idx_ref[...]).astype(BF16)
        acc = jnp.zeros((REL_BUCKETS, SW_Q_HEADS), F32)
        for piece in _split3(d_ref[...]):
            acc = acc + lax.dot_general(onehot, piece, (((1,), (1,)), ((), ())), preferred_element_type=F32)
        o_ref[...] += acc

    return pl.pallas_call(
        body, name=name, grid=(BIAS_COLS // BIAS_TILE,),
        in_specs=[pl.BlockSpec((SW_Q_HEADS, BIAS_TILE), lambda j: (0, j)), pl.BlockSpec((1, BIAS_TILE), lambda j: (0, j))],
        out_specs=pl.BlockSpec((REL_BUCKETS, SW_Q_HEADS), lambda j: (0, 0)),
        out_shape=jax.ShapeDtypeStruct((REL_BUCKETS, SW_Q_HEADS), F32),
        compiler_params=_params(("arbitrary",)),
    )(dbias, bucket)


def _band_mask(n):
    rows = SW_GROUP * SW_WINDOW
    t = (lax.broadcasted_iota(jnp.int32, (rows, 2 * SW_WINDOW), 0) & (SW_WINDOW - 1)) + SW_WINDOW
    s = lax.broadcasted_iota(jnp.int32, (rows, 2 * SW_WINDOW), 1)
    dist = t - s
    return (dist >= 0) & (dist < SW_WINDOW) & ((n > 0) | (s >= SW_WINDOW))


def _head_cols(h):
    return slice(h * SW_HEAD_DIM, (h + 1) * SW_HEAD_DIM)


def _group_inputs(q_ref, bias_ref, sink_ref, g):
    heads = range(g * SW_GROUP, (g + 1) * SW_GROUP)
    q = jnp.concatenate([q_ref[:, _head_cols(h)] for h in heads], axis=0)
    sink = jnp.concatenate([jnp.broadcast_to(sink_ref[:, h:h + 1], (SW_WINDOW, 1)) for h in heads], axis=0)
    bias = bias_ref[g * SW_GROUP:(g + 1) * SW_GROUP].reshape(SW_GROUP * SW_WINDOW, 2 * SW_WINDOW)
    return heads, q, bias, sink


KV_DIM = SW_KV_HEADS * SW_HEAD_DIM


def _kv_pair(kvp_ref, kvc_ref, g):
    ks = slice(g * SW_HEAD_DIM, (g + 1) * SW_HEAD_DIM)
    vs = slice(KV_DIM + g * SW_HEAD_DIM, KV_DIM + (g + 1) * SW_HEAD_DIM)
    kk = jnp.concatenate([kvp_ref[:, ks], kvc_ref[:, ks]], axis=0)
    vv = jnp.concatenate([kvp_ref[:, vs], kvc_ref[:, vs]], axis=0)
    return kk, vv, ks, vs


def _attn_fwd(q1, kv, bias, sinks, *, name):
    T, Dm = q1.shape
    W = SW_WINDOW

    def body(q_ref, kvc_ref, kvp_ref, bias_ref, sink_ref, o_ref):
        mask = _band_mask(pl.program_id(0))
        G = range(SW_KV_HEADS)
        ins = [_group_inputs(q_ref, bias_ref, sink_ref, g) for g in G]
        kvs = [_kv_pair(kvp_ref, kvc_ref, g) for g in G]
        lg = [jnp.where(mask, mm_nt(ins[g][1], kvs[g][0]) * (SW_HEAD_DIM ** -0.5) + ins[g][2], -jnp.inf) for g in G]
        m = [jnp.maximum(jnp.max(lg[g], axis=-1, keepdims=True), ins[g][3]) for g in G]
        p = [jnp.exp(lg[g] - m[g]) for g in G]
        den = [jnp.sum(p[g], axis=-1, keepdims=True) + jnp.exp(ins[g][3] - m[g]) for g in G]
        o = [mm(p[g], kvs[g][1]) / den[g] for g in G]
        for g in G:
            for r, h in enumerate(ins[g][0]):
                o_ref[:, _head_cols(h)] = o[g][r * W:(r + 1) * W].astype(BF16)

    return pl.pallas_call(
        body, name=name, grid=(T // W,),
        in_specs=[pl.BlockSpec((W, Dm), lambda n: (n, 0)),
                  pl.BlockSpec((W, 2 * KV_DIM), lambda n: (n, 0)),
                  pl.BlockSpec((W, 2 * KV_DIM), lambda n: (jnp.maximum(n - 1, 0), 0)),
                  pl.BlockSpec((SW_Q_HEADS, W, 2 * W), lambda n: (0, 0, 0)),
                  pl.BlockSpec((1, SW_Q_HEADS), lambda n: (0, 0))],
        out_specs=pl.BlockSpec((W, Dm), lambda n: (n, 0)),
        out_shape=jax.ShapeDtypeStruct((T, Dm), BF16),
        compiler_params=_params(("parallel",)),
    )(q1, kv, kv, bias, sinks)


def _attn_bwd(q1, kv, bias, sinks, do, *, name):
    T, Dm = q1.shape
    W = SW_WINDOW
    nb = T // W

    def body(q_ref, kvc_ref, kvp_ref, bias_ref, sink_ref, do_ref,
             dq_ref, dkv_ref, dbias_ref, dsink_ref, carry_ref):
        @pl.when(pl.program_id(0) == 0)
        def _():
            carry_ref[...] = jnp.zeros_like(carry_ref)
            dbias_ref[...] = jnp.zeros_like(dbias_ref)
            dsink_ref[...] = jnp.zeros_like(dsink_ref)

        n = nb - 1 - pl.program_id(0)
        mask = _band_mask(n)
        lane = lax.broadcasted_iota(jnp.int32, (1, SW_Q_HEADS), 1)
        sc = SW_HEAD_DIM ** -0.5
        G = range(SW_KV_HEADS)
        ins = [_group_inputs(q_ref, bias_ref, sink_ref, g) for g in G]
        kvs = [_kv_pair(kvp_ref, kvc_ref, g) for g in G]
        do = [jnp.concatenate([do_ref[:, _head_cols(h)] for h in ins[g][0]], axis=0) for g in G]
        lg = [jnp.where(mask, mm_nt(ins[g][1], kvs[g][0]) * sc + ins[g][2], -jnp.inf) for g in G]
        m = [jnp.maximum(jnp.max(lg[g], axis=-1, keepdims=True), ins[g][3]) for g in G]
        p = [jnp.exp(lg[g] - m[g]) for g in G]
        ps = [jnp.exp(ins[g][3] - m[g]) for g in G]
        rden = [1.0 / (jnp.sum(p[g], axis=-1, keepdims=True) + ps[g]) for g in G]
        pn = [p[g] * rden[g] for g in G]
        dpn = [mm_nt(do[g], kvs[g][1]) for g in G]
        delta = [jnp.sum(pn[g] * dpn[g], axis=-1, keepdims=True) for g in G]
        ds = [pn[g] * (dpn[g] - delta[g]) for g in G]
        dsr = [-(ps[g] * rden[g]) * delta[g] for g in G]
        dq = [mm(ds[g], kvs[g][0]) * sc for g in G]
        dkk = [mm_tn(ds[g], ins[g][1]) * sc for g in G]
        dvv = [mm_tn(pn[g], do[g]) for g in G]
        dsink = jnp.zeros((1, SW_Q_HEADS), F32)
        for g in G:
            _, _, ks, vs = kvs[g]
            dbias_ref[g * SW_GROUP:(g + 1) * SW_GROUP] += ds[g].reshape(SW_GROUP, W, 2 * W)
            for r, h in enumerate(ins[g][0]):
                dq_ref[:, _head_cols(h)] = dq[g][r * W:(r + 1) * W].astype(BF16)
                dsink = dsink + jnp.where(lane == h, jnp.sum(dsr[g][r * W:(r + 1) * W], axis=0, keepdims=True), 0.0)
            dkv_ref[:, ks] = (carry_ref[:, ks] + dkk[g][W:]).astype(BF16)
            dkv_ref[:, vs] = (carry_ref[:, vs] + dvv[g][W:]).astype(BF16)
            carry_ref[:, ks] = dkk[g][:W]
            carry_ref[:, vs] = dvv[g][:W]
        dsink_ref[...] += dsink

    rev = lambda n: (nb - 1 - n, 0)
    return pl.pallas_call(
        body, name=name, grid=(nb,),
        in_specs=[pl.BlockSpec((W, Dm), rev),
                  pl.BlockSpec((W, 2 * KV_DIM), rev),
                  pl.BlockSpec((W, 2 * KV_DIM), lambda n: (jnp.maximum(nb - 2 - n, 0), 0)),
                  pl.BlockSpec((SW_Q_HEADS, W, 2 * W), lambda n: (0, 0, 0)),
                  pl.BlockSpec((1, SW_Q_HEADS), lambda n: (0, 0)),
                  pl.BlockSpec((W, Dm), rev)],
        out_specs=[pl.BlockSpec((W, Dm), rev), pl.BlockSpec((W, 2 * KV_DIM), rev),
                   pl.BlockSpec((SW_Q_HEADS, W, 2 * W), lambda n: (0, 0, 0)),
                   pl.BlockSpec((1, SW_Q_HEADS), lambda n: (0, 0))],
        out_shape=[jax.ShapeDtypeStruct((T, Dm), BF16), jax.ShapeDtypeStruct((T, 2 * KV_DIM), BF16),
                   jax.ShapeDtypeStruct((SW_Q_HEADS, W, 2 * W), F32), jax.ShapeDtypeStruct((1, SW_Q_HEADS), F32)],
        scratch_shapes=[pltpu.VMEM((W, 2 * KV_DIM), F32)],
        compiler_params=_params(("arbitrary",)),
    )(q1, kv, kv, bias, sinks, do)


def _ffn_fwd(hb, w, l, after=None):
    ua = _matmul(hb, w["ffn_in_a"][l], mode="nn", out_dtype=BF16, name=f"ffn{l}_up_a", tm=1024, after=after)
    ub = _matmul(hb, w["ffn_in_b"][l], mode="nn", out_dtype=BF16, name=f"ffn{l}_up_b", tm=1024)
    act = _conv_gate_fwd(ua, ub, w["conv_w_a"][l], w["conv_w_b"][l], w["conv_b_a"][l], w["conv_b_b"][l],
                         name=f"ffn{l}_conv_gate")
    return ua, ub, act


def _ffn_bwd(dffb, dh_scaled, hb, ua, ub, act, w, l):
    dact = _matmul(dffb, w["ffn_out"][l], mode="nt", out_dtype=BF16, name=f"ffn{l}_down_dx", tm=1024)
    g_out = _matmul(act, dffb, mode="tn", name=f"ffn{l}_down_dw", tm=1408, tn=1024, tk=1024)
    dua, dub, dwa, dwb, dba, dbb = _conv_gate_bwd(ua, ub, w["conv_w_a"][l], w["conv_w_b"][l], w["conv_b_a"][l],
                                                  w["conv_b_b"][l], dact, name=f"ffn{l}_conv_gate_bwd")
    dh = _matmul(dua, w["ffn_in_a"][l], mode="nt", add=dh_scaled, add_scale=ALPHA, name=f"ffn{l}_up_a_dx",
                 tn=1024, tk=FFN_DIM)
    dh = _matmul(dub, w["ffn_in_b"][l], mode="nt", add=dh, name=f"ffn{l}_up_b_dx", tn=1024, tk=FFN_DIM)
    g_in_a = _matmul(hb, dua, mode="tn", name=f"ffn{l}_up_a_dw", tm=1024, tn=FFN_DIM // 2, tk=1024, split_n=True)
    g_in_b = _matmul(hb, dub, mode="tn", name=f"ffn{l}_up_b_dw", tm=1024, tn=FFN_DIM // 2, tk=1024, split_n=True)
    return dh, dict(ffn_out=g_out, ffn_in_a=g_in_a, ffn_in_b=g_in_b, conv_w_a=dwa, conv_w_b=dwb, conv_b_a=dba, conv_b_b=dbb)


def _local_step(x, tgt, w, more_weights, emit):
    bucket = jnp.asarray(_bucket_index())
    xb = x.astype(BF16)

    pre = [_matmul(xb, w["hg_in"][j], mode="nn", out_dtype=BF16, name=f"hg_in_{j}", tm=1024, tn=1024,
                   after=w.get("token") if j == 0 else None) for j in range(4)]
    og, states = _hgrn_fwd(*pre, w["lb_logits"], w["gnorm"], name="hgrn_fwd")
    z1, h1, h1b = _matmul_ln(og, w["hg_out"], x, w["ln_mix_g"][0], w["ln_mix_b"][0], name="hg_out_ln")
    w = {**w, **more_weights(1, h1b)}
    ua0, ub0, act0 = _ffn_fwd(h1b, w, 0, after=w.get("token"))
    z2, h2, h2b = _matmul_ln(act0, w["ffn_out"][0], h1, w["ln_ffn_g"][0], w["ln_ffn_b"][0], name="ffn0_down_ln")
    kv = _matmul(h2b, w["kv"], mode="nn", name="kv_proj")

    bias = _bias_from_table(w["rel_bias"], bucket, name="rel_bias_expand").reshape(SW_Q_HEADS, SW_WINDOW, 2 * SW_WINDOW)
    q1 = _matmul(h2b, w["sw_q"], mode="nn", name="sw_q")
    o1 = _attn_fwd(q1, kv, bias, w["sinks"], name="attn_fwd")
    z3, h3, h3b = _matmul_ln(o1, w["sw_out"], h2, w["ln_mix_g"][1], w["ln_mix_b"][1], name="sw_out_ln")
    w = {**w, **more_weights(2, h3b)}
    ua1, ub1, act1 = _ffn_fwd(h3b, w, 1)

    g = {}
    dz, dzb, dg_, db_, loss_tile = _matmul_ln(act1, w["ffn_out"][1], h3, w["ln_ffn_g"][1], w["ln_ffn_b"][1], tgt=tgt,
                                              name="ffn1_down_ln_loss")

    g["ln_ffn_g1"], g["ln_ffn_b1"] = dg_, db_
    dh3, gf1 = _ffn_bwd(dzb, dz, h3b, ua1, ub1, act1, w, 1)
    dz, dzb, dg_, db_ = _ln_bwd(dh3, z3, w["ln_mix_g"][1], w["ln_mix_b"][1], name="ln_mix1_bwd")
    g["ln_mix_g1"], g["ln_mix_b1"] = dg_, db_
    do1 = _matmul(dzb, w["sw_out"], mode="nt", out_dtype=BF16, name="sw_out_dx")
    g_sw_out = _matmul(o1, dzb, mode="tn", name="sw_out_dw", tm=1024, tn=1024, tk=1024)
    dq1, dkv, dbias, dsinks = _attn_bwd(q1, kv, bias, w["sinks"], do1, name="attn_bwd")
    g["sinks"] = dsinks
    g["rel_bias"] = _table_grad(dbias.reshape(SW_Q_HEADS, BIAS_COLS), bucket, name="rel_bias_grad")
    dh2 = _matmul(dq1, w["sw_q"], mode="nt", add=dz, add_scale=ALPHA, name="sw_q_dx", tn=1024)
    dh2 = _matmul(dkv, w["kv"], mode="nt", add=dh2, name="kv_dx", tn=1024)
    g_sw_q = _matmul(h2b, dq1, mode="tn", name="sw_q_dw", tm=1024, tn=1024, tk=1024)
    g_kv = _matmul(h2b, dkv, mode="tn", name="kv_dw", tm=1024, tn=512, tk=1024)
    tok = emit(1, dict(sw_q=g_sw_q, sw_out=g_sw_out, kv=g_kv, ffn_in_a=gf1["ffn_in_a"], ffn_in_b=gf1["ffn_in_b"],
                       ffn_out=gf1["ffn_out"]))

    dz, dzb, dg_, db_ = _ln_bwd(dh2, z2, w["ln_ffn_g"][0], w["ln_ffn_b"][0], name="ln_ffn0_bwd", after=tok)
    g["ln_ffn_g0"], g["ln_ffn_b0"] = dg_, db_
    dh1, gf0 = _ffn_bwd(dzb, dz, h1b, ua0, ub0, act0, w, 0)
    dz, dzb, dg_, db_ = _ln_bwd(dh1, z1, w["ln_mix_g"][0], w["ln_mix_b"][0], name="ln_mix0_bwd")
    g["ln_mix_g0"], g["ln_mix_b0"] = dg_, db_
    dog = _matmul(dzb, w["hg_out"], mode="nt", out_dtype=BF16, name="hg_out_dx")
    g_hg_out = _matmul(og, dzb, mode="tn", name="hg_out_dw", tm=1024, tn=1024, tk=1024)
    tok = emit(2, dict(hg_out=g_hg_out, ffn_in_a=gf0["ffn_in_a"], ffn_in_b=gf0["ffn_in_b"], ffn_out=gf0["ffn_out"]))
    dpre = _hgrn_bwd(*pre, w["lb_logits"], w["gnorm"], states, dog, name="hgrn_bwd", after=tok)
    g["lb_logits"], g["gnorm"] = dpre[4], dpre[5]
    tok = emit(3, dict(hg_in=[_matmul(xb, dpre[j], mode="tn", name=f"hg_in_{j}_dw", tm=1024, tn=1024, tk=1024)
                              for j in range(4)]))
    dx = dz
    for j in range(4):
        dx = _matmul(dpre[j], w["hg_in"][j], mode="nt", add=dx, add_scale=ALPHA if j == 0 else 1.0,
                     name=f"hg_in_{j}_dx", tn=1024, after=tok if j == 0 else None)
    g["conv"] = [{k: gf[k] for k in ("conv_w_a", "conv_w_b", "conv_b_a", "conv_b_b")} for gf in (gf0, gf1)]
    return loss_tile, dx, g


def _adamw(wt, ga, gb, m, v, *, name, rows=None, prev=None):
    R, Cc = wt.shape
    r0, n = rows if rows is not None else (0, R)
    tr = _tile(n, 256, SUBLANES) if n % SUBLANES == 0 else n
    assert r0 % tr == 0
    c1 = 1.0 - ADAM_B1 ** ADAM_STEP
    c2 = 1.0 - ADAM_B2 ** ADAM_STEP
    two = gb is not None
    n_in = 5 if two else 4

    def body(*refs):
        if two:
            w_ref, ga_ref, gb_ref, m_ref, v_ref = refs[:5]
            g_ = ga_ref[...] + gb_ref[...]
        else:
            w_ref, ga_ref, m_ref, v_ref = refs[:4]
            g_ = ga_ref[...]
        g_ref, d_ref, nm_ref, nv_ref = refs[-4:]
        nm = ADAM_B1 * m_ref[...] + (1.0 - ADAM_B1) * g_
        nv = ADAM_B2 * v_ref[...] + (1.0 - ADAM_B2) * (g_ * g_)
        g_ref[...] = g_
        d_ref[...] = -ADAM_LR * ((nm / c1) / (jnp.sqrt(nv / c2) + ADAM_EPS) + ADAM_WD * w_ref[...])
        nm_ref[...] = nm
        nv_ref[...] = nv

    full = pl.BlockSpec((tr, Cc), lambda i: (i + r0 // tr, 0))
    part = pl.BlockSpec((tr, Cc), lambda i: (i, 0))
    args = (wt, ga, gb, m, v) if two else (wt, ga, m, v)
    in_specs = [full] + [part] * (n_in - 3) + [full, full]
    aliases = {}
    if prev is not None:
        args, in_specs = args + tuple(prev), in_specs + [ANY_SPEC] * 4
        aliases = {n_in + t: t for t in range(4)}
    return pl.pallas_call(
        body, name=name, grid=(n // tr,), in_specs=in_specs, out_specs=[full] * 4,
        out_shape=[jax.ShapeDtypeStruct((R, Cc), F32)] * 4, input_output_aliases=aliases,
        compiler_params=_params(("parallel",)),
    )(*args)


HBM_SPEC = pl.BlockSpec(memory_space=pltpu.HBM)
SEM_SPEC = pl.BlockSpec(memory_space=pltpu.SEMAPHORE)
VMEM_SPEC = pl.BlockSpec(memory_space=pltpu.VMEM)
DATAFLOW = pltpu.SideEffectType.DATAFLOW_SIDE_EFFECTING


def _in_hbm(a):
    return pltpu.with_memory_space_constraint(a, pltpu.HBM)


def _place():
    return lax.axis_index("x"), lax.axis_index("y"), lax.axis_index("c")


def _other_chips(x, y):
    return [(1 - x, y), (x, 1 - y), (1 - x, 1 - y)]


def _sum8(v, *, name, after=None):
    r = v.shape[0]

    def body(v_ref, all_ref, o_ref, send_sems, recv_sems, local_sem):
        x, y, c = _place()
        me, sibling = (x, y, c), (x, y, 1 - c)
        chips = _other_chips(x, y)

        def rows(px, py, pc):
            return all_ref.at[pl.ds((4 * px + 2 * py + pc) * r, r), :]

        def copy(k, block, to, src=None):
            return pltpu.make_async_remote_copy(
                src_ref=rows(*block) if src is None else src, dst_ref=rows(*block),
                send_sem=send_sems.at[k], recv_sem=recv_sems.at[k], device_id=to, device_id_type=MESH)

        mine = pltpu.make_async_copy(v_ref, rows(*me), local_sem)
        mine.start()
        first = [copy(0, me, sibling, src=v_ref)]
        first += [copy(1 + j, me, (*chip, c), src=v_ref) for j, chip in enumerate(chips)]
        for cp in first:
            cp.start()
        passed = [copy(4 + j, (*chip, c), sibling) for j, chip in enumerate(chips)]
        for j, chip in enumerate(chips):
            copy(1 + j, (*chip, c), me).wait_recv()
            passed[j].start()
        copy(0, sibling, me).wait_recv()
        for j, chip in enumerate(chips):
            copy(4 + j, (*chip, 1 - c), me).wait_recv()
        for cp in first + passed:
            cp.wait_send()
        mine.wait()
        acc = all_ref[pl.ds(0, r), :]
        for d in range(1, N_DEV):
            acc = acc + all_ref[pl.ds(d * r, r), :]
        o_ref[...] = acc

    body, xs, xa = _after(body, 1, after)
    return pl.pallas_call(
        body, name=name, in_specs=[VMEM_SPEC] + xs, out_specs=[VMEM_SPEC, VMEM_SPEC],
        out_shape=[jax.ShapeDtypeStruct((N_DEV * r, LANES), F32), jax.ShapeDtypeStruct((r, LANES), F32)],
        scratch_shapes=[pltpu.SemaphoreType.DMA((7,)), pltpu.SemaphoreType.DMA((7,)), pltpu.SemaphoreType.DMA],
        compiler_params=pltpu.CompilerParams(vmem_limit_bytes=VMEM_LIMIT),
    )(v, *xa)[1]


def _gather_chips(shard, *, name):
    R, Cc = shard.shape
    half = R // 2
    assert half * 2 == R

    def body(s_ref, o_ref, send_sems, recv_sems, local_sem):
        x, y, c = _place()
        sibling = (x, y, 1 - c)
        chips = _other_chips(x, y)

        def part(px, py, pc):
            return o_ref.at[2 * px + py, pl.ds(pc * half, half), :]

        def copy(k, block, to, src=None):
            return pltpu.make_async_remote_copy(
                src_ref=part(*block) if src is None else src, dst_ref=part(*block),
                send_sem=send_sems.at[k], recv_sem=recv_sems.at[k], device_id=to, device_id_type=MESH)

        mine = pltpu.make_async_copy(s_ref, o_ref.at[2 * x + y], local_sem)
        mine.start()
        my_half = s_ref.at[pl.ds(c * half, half), :]
        first = [copy(j, (x, y, c), (*chip, c), src=my_half) for j, chip in enumerate(chips)]
        for cp in first:
            cp.start()
        passed = [copy(3 + j, (*chip, c), sibling) for j, chip in enumerate(chips)]
        for j, chip in enumerate(chips):
            copy(j, (*chip, c), (x, y, c)).wait_recv()
            passed[j].start()
        for j, chip in enumerate(chips):
            copy(3 + j, (*chip, 1 - c), (x, y, c)).wait_recv()
        for cp in first + passed:
            cp.wait_send()
        mine.wait()

    return pl.pallas_call(
        body, name=name, in_specs=[HBM_SPEC], out_specs=HBM_SPEC,
        out_shape=jax.ShapeDtypeStruct((N_CHIPS, R, Cc), shard.dtype),
        scratch_shapes=[pltpu.SemaphoreType.DMA((6,)), pltpu.SemaphoreType.DMA((6,)), pltpu.SemaphoreType.DMA],
    )(shard)


def _swap_sibling(vs, *, name):
    n = len(vs)

    def body(*refs):
        src, dst, send_sems, recv_sems = refs[:n], refs[n:2 * n], refs[2 * n], refs[2 * n + 1]
        x, y, c = _place()
        cps = [pltpu.make_async_remote_copy(src_ref=src[i], dst_ref=dst[i], send_sem=send_sems.at[i],
                                            recv_sem=recv_sems.at[i], device_id=(x, y, 1 - c), device_id_type=MESH)
               for i in range(n)]
        for cp in cps:
            cp.start()
        for cp in cps:
            cp.wait()

    return pl.pallas_call(
        body, name=name, in_specs=[HBM_SPEC] * n, out_specs=[HBM_SPEC] * n,
        out_shape=[jax.ShapeDtypeStruct(v.shape, v.dtype) for v in vs],
        scratch_shapes=[pltpu.SemaphoreType.DMA((n,)), pltpu.SemaphoreType.DMA((n,))],
    )(*vs)


def _half(ref, j, c, half):
    return ref.at[j, pl.ds(c * half, half), :]


def _gather_start(shard, after, *, name):
    R, Cc = shard.shape
    half = R // 2

    def body(src, land, send, recv, src_out, land_out, token):
        x, y, c = _place()
        for k, (px, py) in enumerate(_other_chips(x, y)):
            pltpu.make_async_remote_copy(src_ref=src.at[pl.ds(c * half, half), :], dst_ref=_half(land, 2 * x + y, c, half),
                                         send_sem=send.at[k], recv_sem=recv.at[k], device_id=(px, py, c),
                                         device_id_type=MESH).start()
        token[...] = jnp.zeros_like(token)

    land = lax.empty((N_CHIPS, R, Cc), shard.dtype)
    body, xs, xa = _after(body, 2, after)
    out = pl.pallas_call(
        body, name=name, in_specs=[HBM_SPEC, HBM_SPEC] + xs,
        out_specs=[SEM_SPEC, SEM_SPEC, HBM_SPEC, HBM_SPEC, VMEM_SPEC],
        out_shape=[pltpu.SemaphoreType.DMA((3,)), pltpu.SemaphoreType.DMA((3,)), pltpu.HBM(shard.shape, shard.dtype),
                   pltpu.HBM(land.shape, land.dtype), jax.ShapeDtypeStruct((SUBLANES, LANES), F32)],
        input_output_aliases={0: 2, 1: 3},
        compiler_params=pltpu.CompilerParams(has_side_effects=DATAFLOW),
    )(_in_hbm(shard), _in_hbm(land), *xa)
    return out[:4], out[4]


def _gather_wait(handle, after, *, name):
    send_sems, recv_sems, src, land = handle
    half = src.shape[0] // 2

    def body(src_ref, land_ref, send_ref, recv_ref, after_ref, src_out, land_out):
        x, y, c = _place()
        for k, (px, py) in enumerate(_other_chips(x, y)):
            cp = pltpu.make_async_remote_copy(src_ref=src_ref.at[pl.ds(c * half, half), :],
                                              dst_ref=_half(land_ref, 2 * px + py, c, half), send_sem=send_ref.at[k],
                                              recv_sem=recv_ref.at[k], device_id=(px, py, c), device_id_type=MESH)
            cp.wait_send()
            cp.wait_recv()

    return pl.pallas_call(
        body, name=name, in_specs=[HBM_SPEC, HBM_SPEC, SEM_SPEC, SEM_SPEC, ANY_SPEC], out_specs=[HBM_SPEC, HBM_SPEC],
        out_shape=[pltpu.HBM(src.shape, src.dtype), pltpu.HBM(land.shape, land.dtype)],
        input_output_aliases={0: 0, 1: 1},
        compiler_params=pltpu.CompilerParams(has_side_effects=DATAFLOW),
    )(src, land, send_sems, recv_sems, after)[1]


def _fill_sibling(land, *, name):
    _, R, Cc = land.shape
    half = R // 2

    def body(in_ref, o_ref, send_sems, recv_sems):
        x, y, c = _place()
        chips = _other_chips(x, y)
        cps = [pltpu.make_async_remote_copy(src_ref=_half(in_ref, 2 * px + py, c, half),
                                            dst_ref=_half(o_ref, 2 * px + py, c, half), send_sem=send_sems.at[k],
                                            recv_sem=recv_sems.at[k], device_id=(x, y, 1 - c), device_id_type=MESH)
               for k, (px, py) in enumerate(chips)]
        for cp in cps:
            cp.start()
        for k, (px, py) in enumerate(chips):
            pltpu.make_async_remote_copy(src_ref=_half(in_ref, 2 * px + py, 1 - c, half),
                                         dst_ref=_half(o_ref, 2 * px + py, 1 - c, half), send_sem=send_sems.at[k],
                                         recv_sem=recv_sems.at[k], device_id=(x, y, 1 - c), device_id_type=MESH).wait_recv()
        for cp in cps:
            cp.wait_send()

    return pl.pallas_call(
        body, name=name, in_specs=[HBM_SPEC], out_specs=HBM_SPEC, out_shape=jax.ShapeDtypeStruct(land.shape, land.dtype),
        scratch_shapes=[pltpu.SemaphoreType.DMA((3,)), pltpu.SemaphoreType.DMA((3,))],
        input_output_aliases={0: 0},
    )(land)


def _scatter_copies(src, land, send, recv):
    x, y, c = _place()
    return [pltpu.make_async_remote_copy(src_ref=src[i].at[2 * px + py], dst_ref=land[i].at[k], send_sem=send.at[3 * i + k],
                                         recv_sem=recv.at[3 * i + k], device_id=(px, py, c), device_id_type=MESH)
            for i in range(len(src)) for k, (px, py) in enumerate(_other_chips(x, y))]


def _scatter_start(pieces, *, name):
    n = len(pieces)

    def body(*refs):
        src, land, send, recv, token = refs[:n], refs[n:2 * n], refs[2 * n], refs[2 * n + 1], refs[-1]
        for cp in _scatter_copies(src, land, send, recv):
            cp.start()
        token[...] = jnp.zeros_like(token)

    lands = [lax.empty((3,) + p.shape[1:], p.dtype) for p in pieces]
    sems = pltpu.SemaphoreType.DMA((3 * n,))
    out = pl.pallas_call(
        body, name=name, in_specs=[HBM_SPEC] * (2 * n),
        out_specs=[SEM_SPEC, SEM_SPEC] + [HBM_SPEC] * (2 * n) + [VMEM_SPEC],
        out_shape=[sems, sems] + [pltpu.HBM(a.shape, a.dtype) for a in pieces + lands]
        + [jax.ShapeDtypeStruct((SUBLANES, LANES), F32)],
        input_output_aliases={i: 2 + i for i in range(2 * n)},
        compiler_params=pltpu.CompilerParams(has_side_effects=DATAFLOW),
    )(*[_in_hbm(a) for a in pieces + lands])
    return (out[0], out[1], out[2:2 + n], out[2 + n:2 + 2 * n]), out[-1]


def _scatter_wait(handle, after, *, name):
    send_sems, recv_sems, srcs, lands = handle
    n = len(srcs)

    def body(*refs):
        src, land, send, recv = refs[:n], refs[n:2 * n], refs[2 * n], refs[2 * n + 1]
        for cp in _scatter_copies(src, land, send, recv):
            cp.wait_send()
            cp.wait_recv()

    both = list(srcs) + list(lands)
    out = pl.pallas_call(
        body, name=name, in_specs=[HBM_SPEC] * (2 * n) + [SEM_SPEC, SEM_SPEC, ANY_SPEC], out_specs=[HBM_SPEC] * (2 * n),
        out_shape=[pltpu.HBM(a.shape, a.dtype) for a in both],
        input_output_aliases={i: i for i in range(2 * n)},
        compiler_params=pltpu.CompilerParams(has_side_effects=DATAFLOW),
    )(*both, send_sems, recv_sems, after)
    return out[n:]


def _chip_sum(pieces, got, chip, *, name):
    _, R, Cc = pieces.shape
    tr = _tile(R, 256, SUBLANES)

    def body(chip_ref, a_ref, g_ref, o_ref):
        o_ref[...] = ((a_ref[...] + g_ref[0].astype(F32)) + g_ref[1].astype(F32)) + g_ref[2].astype(F32)

    return pl.pallas_call(
        body, name=name,
        grid_spec=pltpu.PrefetchScalarGridSpec(
            num_scalar_prefetch=1, grid=(R // tr,),
            in_specs=[pl.BlockSpec((None, tr, Cc), lambda i, ch: (ch[0], i, 0)),
                      pl.BlockSpec((3, tr, Cc), lambda i, ch: (0, i, 0))],
            out_specs=pl.BlockSpec((tr, Cc), lambda i, ch: (i, 0))),
        out_shape=jax.ShapeDtypeStruct((R, Cc), F32),
        compiler_params=_params(("parallel",)),
    )(chip, pieces, got)


PACK_COLS = 1024


def _pack_rows(parts):
    return jnp.concatenate([p.reshape(-1, PACK_COLS) for p in parts], axis=0)


def _unpack_rows(block, shapes):
    lead = block.shape[:-2]
    out, off = [], 0
    for s in shapes:
        r = int(np.prod(s)) // PACK_COLS
        out.append(block[..., off:off + r, :].reshape(lead + tuple(s)))
        off += r
    assert off == block.shape[-2]
    return out


def _flat128(parts):
    out = []
    for p in parts:
        v = p.reshape(-1)
        pad = (-v.shape[0]) % LANES
        out.append(jnp.pad(v, (0, pad)) if pad else v)
    v = jnp.concatenate(out)
    pad = (-v.shape[0]) % (SUBLANES * LANES)
    if pad:
        v = jnp.pad(v, (0, pad))
    return v.reshape(-1, LANES)


def _unflat128(block, shapes):
    v = block.reshape(-1)
    out, off = [], 0
    for s in shapes:
        n = int(np.prod(s))
        out.append(v[off:off + n].reshape(s))
        off += n + ((-n) % LANES)
    return out


def kernel(x, hgrn_w_in, hgrn_lb_logits, hgrn_gnorm_w, hgrn_w_out, swa_w_q, swa_sinks, swa_w_out, shared_w_kv, rel_bias, ffn_w_in, ffn_conv_w, ffn_conv_b, ffn_w_out, ln_mix_g, ln_mix_b, ln_ffn_g, ln_ffn_b, loss_target, m_hgrn_w_in, m_hgrn_lb_logits, m_hgrn_gnorm_w, m_hgrn_w_out, m_swa_w_q, m_swa_sinks, m_swa_w_out, m_shared_w_kv, m_rel_bias, m_ffn_w_in, m_ffn_conv_w, m_ffn_conv_b, m_ffn_w_out, m_ln_mix_g, m_ln_mix_b, m_ln_ffn_g, m_ln_ffn_b, v_hgrn_w_in, v_hgrn_lb_logits, v_hgrn_gnorm_w, v_hgrn_w_out, v_swa_w_q, v_swa_sinks, v_swa_w_out, v_shared_w_kv, v_rel_bias, v_ffn_w_in, v_ffn_conv_w, v_ffn_conv_b, v_ffn_w_out, v_ln_mix_g, v_ln_mix_b, v_ln_ffn_g, v_ln_ffn_b):
    xi, yi, ci = _place()
    chip = 2 * xi + yi
    Dm = D_MODEL
    FC = 2 * FFN_DIM // N_CHIPS
    Fo = FFN_DIM // N_CHIPS
    Dq = Dm // N_CHIPS
    bf = lambda a: a.astype(BF16)

    shard0 = _pack_rows([bf(hgrn_w_in), bf(hgrn_w_out)])
    shard1 = _pack_rows([bf(swa_w_q), bf(swa_w_out), bf(shared_w_kv), bf(ffn_w_in[0]), bf(ffn_w_out[0])])
    shard2 = _pack_rows([bf(ffn_w_in[1]), bf(ffn_w_out[1])])
    handle0, token0 = _gather_start(shard0, None, name="gather_w0_start")

    lb_full = lax.dynamic_update_slice(jnp.zeros((2, Dm), F32), hgrn_lb_logits, (0, chip * Dq))
    cw_full = lax.dynamic_update_slice(jnp.zeros((DEPTH, 3, 2 * FFN_DIM), F32), ffn_conv_w, (0, 0, chip * FC))
    only_south = (ci == 0).astype(F32)
    small_in = _sum8(_flat128([lb_full, cw_full]) * only_south, name="gather_small", after=token0)
    lb_full, cw_full = _unflat128(small_in, [(2, Dm), (DEPTH, 3, 2 * FFN_DIM)])

    land0 = _fill_sibling(_gather_wait(handle0, small_in, name="gather_w0_wait"), name="gather_w0_fill")
    all0 = lax.dynamic_update_slice(land0, shard0[None], (chip, 0, 0))
    handle1, token1 = _gather_start(shard1, land0, name="gather_w1_start")
    w_in, w_hg_out = _unpack_rows(all0, [(Dm, Dm), (Dq, Dm)])

    def ffn_weights(w_fi, w_fo, l):
        return {"ffn_in_a": {l: jnp.concatenate([w_fi[0], w_fi[1]], axis=1)},
                "ffn_in_b": {l: jnp.concatenate([w_fi[2], w_fi[3]], axis=1)},
                "ffn_out": {l: w_fo.reshape(FFN_DIM, Dm)}}

    got = {}

    def more_weights(k, after):
        shard = (shard1, shard2)[k - 1]
        land = _gather_wait(got.pop("handle"), after, name=f"gather_w{k}_wait")
        land = _fill_sibling(land, name=f"gather_w{k}_fill")
        allk = lax.dynamic_update_slice(land, shard[None], (chip, 0, 0))
        if k == 1:
            got["handle"], token2 = _gather_start(shard2, land, name="gather_w2_start")
            w_q, w_o, w_kv, w_fi, w_fo = _unpack_rows(allk, [(Dq, Dm), (Dq, Dm), (Dq, 2 * KV_DIM), (Dm, FC), (Fo, Dm)])
            got.update(ffn_weights(w_fi, w_fo, 0))
            return {"sw_q": w_q.reshape(Dm, Dm), "sw_out": w_o.reshape(Dm, Dm), "kv": w_kv.reshape(Dm, 2 * KV_DIM),
                    "token": token2, **{n: got[n] for n in ("ffn_in_a", "ffn_in_b", "ffn_out")}}
        w_fi, w_fo = _unpack_rows(allk, [(Dm, FC), (Fo, Dm)])
        new = ffn_weights(w_fi, w_fo, 1)
        return {n: {**got[n], **new[n]} for n in new}

    got["handle"] = handle1

    w = {
        "hg_in": [w_in[j] for j in range(4)], "hg_out": w_hg_out.reshape(Dm, Dm), "token": token1,
        "lb_logits": lb_full, "gnorm": hgrn_gnorm_w, "sinks": swa_sinks, "rel_bias": rel_bias,
        "conv_w_a": [cw_full[l, :, :FFN_DIM] for l in range(DEPTH)],
        "conv_w_b": [cw_full[l, :, FFN_DIM:] for l in range(DEPTH)],
        "conv_b_a": [ffn_conv_b[l:l + 1, :FFN_DIM] for l in range(DEPTH)],
        "conv_b_b": [ffn_conv_b[l:l + 1, FFN_DIM:] for l in range(DEPTH)],
        "ln_mix_g": [ln_mix_g[l:l + 1] for l in range(DEPTH)], "ln_mix_b": [ln_mix_b[l:l + 1] for l in range(DEPTH)],
        "ln_ffn_g": [ln_ffn_g[l:l + 1] for l in range(DEPTH)], "ln_ffn_b": [ln_ffn_b[l:l + 1] for l in range(DEPTH)],
    }

    sent = {}

    def ffn_pieces(gd):
        return [jnp.concatenate([gd["ffn_in_a"], gd["ffn_in_b"]], axis=0), gd["ffn_out"].reshape(N_CHIPS, Fo, Dm)]

    def emit(k, gd):
        rows4 = lambda a: a.reshape(N_CHIPS, Dq, a.shape[-1])
        if k == 1:
            pieces = [rows4(gd["sw_q"]), rows4(gd["sw_out"]), rows4(gd["kv"])] + ffn_pieces(gd)
        elif k == 2:
            pieces = ffn_pieces(gd) + [rows4(gd["hg_out"])]
        else:
            pieces = [jnp.stack(gd["hg_in"])]
        handle, token = _scatter_start([p.astype(BF16) for p in pieces], name=f"scatter_g{k}_start")
        sent[k] = (handle, pieces)
        return token

    loss_tile, grad_x, g = _local_step(x[0], loss_target[0], w, more_weights, emit)

    wts = dict(hgrn_w_in=hgrn_w_in, hgrn_lb_logits=hgrn_lb_logits, hgrn_gnorm_w=hgrn_gnorm_w, hgrn_w_out=hgrn_w_out,
               swa_w_q=swa_w_q, swa_sinks=swa_sinks, swa_w_out=swa_w_out, shared_w_kv=shared_w_kv, rel_bias=rel_bias,
               ffn_w_in=ffn_w_in, ffn_conv_w=ffn_conv_w, ffn_conv_b=ffn_conv_b, ffn_w_out=ffn_w_out,
               ln_mix_g=ln_mix_g, ln_mix_b=ln_mix_b, ln_ffn_g=ln_ffn_g, ln_ffn_b=ln_ffn_b)
    ms = dict(hgrn_w_in=m_hgrn_w_in, hgrn_lb_logits=m_hgrn_lb_logits, hgrn_gnorm_w=m_hgrn_gnorm_w, hgrn_w_out=m_hgrn_w_out,
              swa_w_q=m_swa_w_q, swa_sinks=m_swa_sinks, swa_w_out=m_swa_w_out, shared_w_kv=m_shared_w_kv, rel_bias=m_rel_bias,
              ffn_w_in=m_ffn_w_in, ffn_conv_w=m_ffn_conv_w, ffn_conv_b=m_ffn_conv_b, ffn_w_out=m_ffn_w_out,
              ln_mix_g=m_ln_mix_g, ln_mix_b=m_ln_mix_b, ln_ffn_g=m_ln_ffn_g, ln_ffn_b=m_ln_ffn_b)
    vs = dict(hgrn_w_in=v_hgrn_w_in, hgrn_lb_logits=v_hgrn_lb_logits, hgrn_gnorm_w=v_hgrn_gnorm_w, hgrn_w_out=v_hgrn_w_out,
              swa_w_q=v_swa_w_q, swa_sinks=v_swa_sinks, swa_w_out=v_swa_w_out, shared_w_kv=v_shared_w_kv, rel_bias=v_rel_bias,
              ffn_w_in=v_ffn_w_in, ffn_conv_w=v_ffn_conv_w, ffn_conv_b=v_ffn_conv_b, ffn_w_out=v_ffn_w_out,
              ln_mix_g=v_ln_mix_g, ln_mix_b=v_ln_mix_b, ln_ffn_g=v_ln_ffn_g, ln_ffn_b=v_ln_ffn_b)
    names = list(wts)
    grads, delta, new_m, new_v = {}, {}, {}, {}

    def update(n, ga, gb, layer=None, prev=None):
        r2 = lambda a: a.reshape(-1, a.shape[-1])
        rows = None if layer is None else (layer * ga.shape[0], ga.shape[0])
        return _adamw(r2(wts[n]), ga, gb, r2(ms[n]), r2(vs[n]), rows=rows, prev=prev,
                      name=f"adamw_{n}" + ("" if layer is None else f"_{layer}"))

    def keep(n, res):
        grads[n], delta[n], new_m[n], new_v[n] = [a.reshape(wts[n].shape) for a in res]

    chip1 = jnp.reshape(chip, (1,)).astype(jnp.int32)
    after = grad_x
    for k in (1, 2, 3):
        handle, pieces = sent[k]
        lands = _scatter_wait(handle, after, name=f"scatter_g{k}_wait")
        parts = [_chip_sum(p, l, chip1, name=f"scatter_g{k}_sum{i}") for i, (p, l) in enumerate(zip(pieces, lands))]
        sibs = _swap_sibling(parts, name=f"scatter_g{k}_swap")
        if k == 1:
            for n, ga, gb in zip(["swa_w_q", "swa_w_out", "shared_w_kv"], parts[:3], sibs[:3]):
                keep(n, update(n, ga, gb))
            ffn_in_1 = update("ffn_w_in", parts[3], sibs[3], layer=1)
            ffn_out_1 = update("ffn_w_out", parts[4], sibs[4], layer=1)
            after = ffn_out_1[3]
        elif k == 2:
            keep("ffn_w_in", update("ffn_w_in", parts[0], sibs[0], layer=0, prev=ffn_in_1))
            keep("ffn_w_out", update("ffn_w_out", parts[1], sibs[1], layer=0, prev=ffn_out_1))
            keep("hgrn_w_out", update("hgrn_w_out", parts[2], sibs[2]))
            after = new_v["hgrn_w_out"]
        else:
            keep("hgrn_w_in", update("hgrn_w_in", parts[0], sibs[0]))

    small_shapes = [(SUBLANES, LANES), (2, Dm), (1, HG_DIM), (1, SW_Q_HEADS), (REL_BUCKETS, SW_Q_HEADS),
                    (DEPTH, 3, 2 * FFN_DIM), (DEPTH, 2 * FFN_DIM)] + [(DEPTH, Dm)] * 4
    gc = g["conv"]
    conv_w_g = jnp.stack([jnp.concatenate([gc[l]["conv_w_a"], gc[l]["conv_w_b"]], axis=1) for l in range(DEPTH)])
    conv_b_g = jnp.concatenate([jnp.concatenate([gc[l]["conv_b_a"], gc[l]["conv_b_b"]], axis=1) for l in range(DEPTH)], axis=0)
    ln_g = [jnp.concatenate([g[f"{n}0"], g[f"{n}1"]], axis=0) for n in ("ln_mix_g", "ln_mix_b", "ln_ffn_g", "ln_ffn_b")]
    small_out = _sum8(_flat128([loss_tile, g["lb_logits"], g["gnorm"], g["sinks"], g["rel_bias"], conv_w_g, conv_b_g] + ln_g),
                      name="sum_small")
    (loss_t, g_lb, g_gn, g_sinks, g_rel, g_cw, g_cb, g_lmg, g_lmb, g_lfg, g_lfb) = _unflat128(small_out, small_shapes)
    loss = loss_t[0, 0]
    g_lb = lax.dynamic_slice_in_dim(g_lb, chip * Dq, Dq, axis=1)
    g_cw = lax.dynamic_slice_in_dim(g_cw, chip * FC, FC, axis=2)
    small_g = dict(hgrn_lb_logits=g_lb, hgrn_gnorm_w=g_gn, swa_sinks=g_sinks, rel_bias=g_rel, ffn_conv_w=g_cw,
                   ffn_conv_b=g_cb, ln_mix_g=g_lmg, ln_mix_b=g_lmb, ln_ffn_g=g_lfg, ln_ffn_b=g_lfb)
    small_names = list(small_g)
    sshapes = [wts[n].shape for n in small_names]
    _, d_, m_, v_ = _adamw(_flat128([wts[n] for n in small_names]), _flat128([small_g[n] for n in small_names]), None,
                           _flat128([ms[n] for n in small_names]), _flat128([vs[n] for n in small_names]), name="adamw_small")
    for n, a, b_, c_ in zip(small_names, _unflat128(d_, sshapes), _unflat128(m_, sshapes), _unflat128(v_, sshapes)):
        grads[n], delta[n], new_m[n], new_v[n] = small_g[n], a, b_, c_

    return (loss, grad_x[None], *[grads[n] for n in names], *[delta[n] for n in names],
            *[new_m[n] for n in names], *[new_v[n] for n in names])
```

```python
import functools
import math

import numpy as np
import jax
import jax.numpy as jnp
from jax import lax
from jax.experimental import pallas as pl
from jax.experimental.pallas import tpu as pltpu

F32 = jnp.float32
BF16 = jnp.bfloat16
MESH = pl.DeviceIdType.MESH

D_MODEL = 1024
DEPTH = 2
HG_HEADS = 8
HG_DIM = 128
SW_Q_HEADS = 16
SW_KV_HEADS = 4
SW_HEAD_DIM = 64
SW_GROUP = 4
SW_WINDOW = 128
REL_BUCKETS = 32
REL_MAX_DIST = 128
FFN_DIM = 2816
ALPHA = (2.0 * DEPTH) ** 0.25
LN_EPS = 1e-5
RMS_EPS = 1e-6
ADAM_LR = 0.001
ADAM_B1 = 0.9
ADAM_B2 = 0.999
ADAM_EPS = 1e-08
ADAM_WD = 0.01
ADAM_STEP = 10

VMEM_BYTES_V7X = 64 * 1024 * 1024
VMEM_LIMIT = VMEM_BYTES_V7X - 8 * 1024 * 1024
LANES = 128
SUBLANES = 8

HG_C = 64
HG_RB = 256
ROW_TILE = 256
CONV_R = 128
N_CHIPS = 4
N_DEV = 8

ANY_SPEC = pl.BlockSpec(memory_space=pl.ANY)


def _after(body, n_in, after):
    if after is None:
        return body, [], ()

    def wrapped(*refs):
        return body(*refs[:n_in], *refs[n_in + 1:])

    return wrapped, [ANY_SPEC], (after,)


def _params(sem=None):
    return pltpu.CompilerParams(dimension_semantics=sem, vmem_limit_bytes=VMEM_LIMIT)


def _tile(n, pref, unit=LANES):
    if n <= pref:
        return n
    best = None
    for t in range(unit, pref + 1, unit):
        if n % t == 0:
            best = t
    assert best is not None, (n, pref, unit)
    return best


def _dot(a, b, ca, cb):
    nb = a.ndim - 2
    batch = tuple(range(nb))
    return lax.dot_general(a.astype(BF16), b.astype(BF16), (((nb + ca,), (nb + cb,)), (batch, batch)),
                           preferred_element_type=F32)


@jax.custom_vjp
def mm(a, b):
    return _dot(a, b, 1, 0)


@jax.custom_vjp
def mm_nt(a, b):
    return _dot(a, b, 1, 1)


@jax.custom_vjp
def mm_tn(a, b):
    return _dot(a, b, 0, 0)


mm.defvjp(lambda a, b: (mm(a, b), (a, b)), lambda r, ct: (mm_nt(ct, r[1]), mm_tn(r[0], ct)))
mm_nt.defvjp(lambda a, b: (mm_nt(a, b), (a, b)), lambda r, ct: (mm(ct, r[1]), mm_tn(ct, r[0])))
mm_tn.defvjp(lambda a, b: (mm_tn(a, b), (a, b)), lambda r, ct: (mm_nt(r[1], ct), mm(r[0], ct)))


def _split2(x):
    hi = x.astype(BF16)
    return hi, (x - hi.astype(F32)).astype(BF16)


@jax.custom_vjp
def _scores(qt, kt):
    return _dot(qt, kt, 1, 1)


def _scores_bwd(r, ct):
    (qh, ql), (kh, kl) = _split2(r[0]), _split2(r[1])
    return _dot(ct, kh, 1, 0) + _dot(ct, kl, 1, 0), _dot(ct, qh, 0, 0) + _dot(ct, ql, 0, 0)


_scores.defvjp(lambda a, b: (_scores(a, b), (a, b)), _scores_bwd)


def _split3(x):
    hi = x.astype(BF16)
    r1 = x - hi.astype(F32)
    mid = r1.astype(BF16)
    lo = (r1 - mid.astype(F32)).astype(BF16)
    return hi, mid, lo


def _cumsum_impl(x):
    ax = x.ndim - 2
    n = x.shape[ax]
    row = lax.broadcasted_iota(jnp.int32, x.shape, ax)
    d = 1
    while d < n:
        x = x + jnp.where(row >= d, pltpu.roll(x, d, ax), 0.0)
        d *= 2
    return x


def _cumsum_rev_impl(x):
    ax = x.ndim - 2
    n = x.shape[ax]
    row = lax.broadcasted_iota(jnp.int32, x.shape, ax)
    d = 1
    while d < n:
        x = x + jnp.where(row < n - d, pltpu.roll(x, n - d, ax), 0.0)
        d *= 2
    return x


@jax.custom_vjp
def _cumsum(x):
    return _cumsum_impl(x)


_cumsum.defvjp(lambda x: (_cumsum_impl(x), None), lambda _, ct: (_cumsum_rev_impl(ct),))


def _matmul(a, b, *, mode, name, out_dtype=F32, add=None, add_scale=1.0, tm=512, tn=1408, tk=1408, after=None,
            split_n=False):
    if mode == "nn":
        (M, K), (K2, N) = a.shape, b.shape
    elif mode == "nt":
        (M, K), (N, K2) = a.shape, b.shape
    else:
        (K, M), (K2, N) = a.shape, b.shape
    assert K == K2, (a.shape, b.shape, mode)
    tm, tn, tk = _tile(M, tm), _tile(N, tn), _tile(K, tk)
    nk = K // tk
    ca, cb = {"nn": (1, 0), "nt": (1, 1), "tn": (0, 0)}[mode]
    a_spec = {"nn": pl.BlockSpec((tm, tk), lambda i, j, k: (i, k)),
              "nt": pl.BlockSpec((tm, tk), lambda i, j, k: (i, k)),
              "tn": pl.BlockSpec((tk, tm), lambda i, j, k: (k, i))}[mode]
    b_spec = {"nn": pl.BlockSpec((tk, tn), lambda i, j, k: (k, j)),
              "nt": pl.BlockSpec((tn, tk), lambda i, j, k: (j, k)),
              "tn": pl.BlockSpec((tk, tn), lambda i, j, k: (k, j))}[mode]
    o_spec = pl.BlockSpec((tm, tn), lambda i, j, k: (i, j))
    has_add = add is not None

    def finish(r, add_ref, o_ref):
        if has_add:
            r = r + add_scale * add_ref[...]
        o_ref[...] = r.astype(out_dtype)

    def body(*refs):
        a_ref, b_ref = refs[:2]
        add_ref = refs[2] if has_add else None
        o_ref = refs[3 if has_add else 2]
        if nk == 1:
            finish(_dot(a_ref[...], b_ref[...], ca, cb), add_ref, o_ref)
            return
        acc_ref = refs[-1]
        k = pl.program_id(2)

        @pl.when(k == 0)
        def _():
            acc_ref[...] = jnp.zeros_like(acc_ref)

        acc_ref[...] += _dot(a_ref[...], b_ref[...], ca, cb)

        @pl.when(k == nk - 1)
        def _():
            finish(acc_ref[...], add_ref, o_ref)

    in_specs = [a_spec, b_spec] + ([o_spec] if has_add else [])
    args = (a, b) + ((add,) if has_add else ())
    body, xs, xa = _after(body, len(args), after)
    in_specs, args = in_specs + xs, args + xa
    out_shape = (M, N)
    if split_n:
        assert not has_add and M == tm
        o_spec = pl.BlockSpec((None, tm, tn), lambda i, j, k: (j, 0, 0))
        out_shape = (N // tn, M, tn)
    return pl.pallas_call(
        body, name=name, grid=(M // tm, N // tn, nk), in_specs=in_specs, out_specs=o_spec,
        out_shape=jax.ShapeDtypeStruct(out_shape, out_dtype),
        scratch_shapes=[pltpu.VMEM((tm, tn), F32)] if nk > 1 else [],
        compiler_params=_params(("parallel", "parallel", "arbitrary")),
    )(*args)


def _ln(z, g, b):
    mu = jnp.mean(z, axis=-1, keepdims=True)
    zc = z - mu
    var = jnp.mean(zc * zc, axis=-1, keepdims=True)
    return zc * lax.rsqrt(var + LN_EPS) * g + b


def _matmul_ln(a, b, h, g, bias, *, name, tgt=None, tm=512):
    (T, K), (K2, Dm) = a.shape, b.shape
    assert K == K2 and h.shape == (T, Dm)
    tm = _tile(T, tm, SUBLANES)
    last = tgt is not None

    def body(*refs):
        a_ref, b_ref, h_ref, g_ref, bias_ref = refs[:5]
        z = ALPHA * h_ref[...] + _dot(a_ref[...], b_ref[...], 1, 0)
        if not last:
            z_ref, y_ref, yb_ref = refs[5:]
            y = _ln(z, g_ref[...], bias_ref[...])
            z_ref[...] = z
            y_ref[...] = y
            yb_ref[...] = y.astype(BF16)
            return
        t_ref, dz_ref, dzb_ref, dg_ref, db_ref, l_ref = refs[5:]

        @pl.when(pl.program_id(0) == 0)
        def _():
            dg_ref[...] = jnp.zeros_like(dg_ref)
            db_ref[...] = jnp.zeros_like(db_ref)
            l_ref[...] = jnp.zeros_like(l_ref)

        y, vjp = jax.vjp(_ln, z, g_ref[...], bias_ref[...])
        e = y - t_ref[...]
        dz, dg, db = vjp(e * (1.0 / Dm))
        l_ref[...] += 0.5 * jnp.sum(jnp.mean(e * e, axis=-1, keepdims=True), axis=0, keepdims=True)
        dz_ref[...] = dz
        dzb_ref[...] = dz.astype(BF16)
        dg_ref[...] += dg
        db_ref[...] += db

    row = pl.BlockSpec((tm, Dm), lambda i: (i, 0))
    vec = pl.BlockSpec((1, Dm), lambda i: (0, 0))
    in_specs = [pl.BlockSpec((tm, K), lambda i: (i, 0)), pl.BlockSpec((K, Dm), lambda i: (0, 0)), row, vec, vec]
    f32, b16 = jax.ShapeDtypeStruct((T, Dm), F32), jax.ShapeDtypeStruct((T, Dm), BF16)
    if not last:
        return pl.pallas_call(
            body, name=name, grid=(T // tm,), in_specs=in_specs, out_specs=[row, row, row], out_shape=[f32, f32, b16],
            compiler_params=_params(("parallel",)),
        )(a, b, h, g, bias)
    return pl.pallas_call(
        body, name=name, grid=(T // tm,), in_specs=in_specs + [row],
        out_specs=[row, row, vec, vec, pl.BlockSpec((SUBLANES, LANES), lambda i: (0, 0))],
        out_shape=[f32, b16, jax.ShapeDtypeStruct((1, Dm), F32), jax.ShapeDtypeStruct((1, Dm), F32),
                   jax.ShapeDtypeStruct((SUBLANES, LANES), F32)],
        compiler_params=_params(("arbitrary",)),
    )(a, b, h, g, bias, tgt)


def _ln_bwd(dy, z, g, b, *, name, after=None):
    T, Dm = z.shape
    tr = _tile(T, ROW_TILE, SUBLANES)

    def body(dy_ref, z_ref, g_ref, b_ref, dz_ref, dzb_ref, dg_ref, db_ref):
        @pl.when(pl.program_id(0) == 0)
        def _():
            dg_ref[...] = jnp.zeros_like(dg_ref)
            db_ref[...] = jnp.zeros_like(db_ref)

        _, vjp = jax.vjp(_ln, z_ref[...], g_ref[...], b_ref[...])
        dz, dg, db = vjp(dy_ref[...])
        dz_ref[...] = dz
        dzb_ref[...] = dz.astype(BF16)
        dg_ref[...] += dg
        db_ref[...] += db

    row = pl.BlockSpec((tr, Dm), lambda i: (i, 0))
    vec = pl.BlockSpec((1, Dm), lambda i: (0, 0))
    body, xs, xa = _after(body, 4, after)
    return pl.pallas_call(
        body, name=name, grid=(T // tr,), in_specs=[row, row, vec, vec] + xs,
        out_specs=[row, row, vec, vec],
        out_shape=[jax.ShapeDtypeStruct((T, Dm), F32), jax.ShapeDtypeStruct((T, Dm), BF16),
                   jax.ShapeDtypeStruct((1, Dm), F32), jax.ShapeDtypeStruct((1, Dm), F32)],
        compiler_params=_params(("arbitrary",)),
    )(dy, z, g, b, *xa)


def _hg_chunk(qr, fr, ir, gr, l0, l1, gw, st):
    C = qr.shape[-2]
    row = lax.broadcasted_iota(jnp.int32, qr.shape, qr.ndim - 2)
    lb = jax.nn.sigmoid(l0 - l1)
    fg = lb + (1.0 - lb) * jax.nn.sigmoid(fr)
    b = _cumsum(jnp.log(fg))
    q = jax.nn.silu(qr)
    k = 1.0 - fg
    bmid = lax.stop_gradient(jnp.sum(jnp.where(row == C // 2 - 1, b, 0.0), axis=-2, keepdims=True))
    bl = jnp.sum(jnp.where(row == C - 1, b, 0.0), axis=-2, keepdims=True)
    o = mm_nt(q * jnp.exp(b), st)
    sc = _scores(q * jnp.exp(b - bmid), k * jnp.exp(bmid - b))
    ti = lax.broadcasted_iota(jnp.int32, (C, C), 0)
    si = lax.broadcasted_iota(jnp.int32, (C, C), 1)
    sc = jnp.where(si <= ti, sc, 0.0)
    o = o + mm(sc, ir)
    st_new = st * jnp.exp(bl) + mm_tn(ir, k * jnp.exp(bl - b))
    on = o * lax.rsqrt(jnp.mean(o * o, axis=-1, keepdims=True) + RMS_EPS)
    return on * gw * jax.nn.silu(gr), st_new


def _heads(ref, rows):
    return jnp.stack([ref[rows, h * HG_DIM:(h + 1) * HG_DIM].astype(F32) for h in range(HG_HEADS)])


def _unheads(x):
    return jnp.concatenate([x[h] for h in range(HG_HEADS)], axis=-1)


def _hgrn_fwd(q, f, i, g, lbl, gw, *, name):
    T, Dm = q.shape
    rb = min(HG_RB, T)
    C = min(HG_C, rb)
    ncb = rb // C

    def body(q_ref, f_ref, i_ref, g_ref, lbl_ref, gw_ref, o_ref, st_ref, s_ref):
        @pl.when(pl.program_id(0) == 0)
        def _():
            s_ref[...] = jnp.zeros_like(s_ref)

        def chunk(ci, carry):
            r0 = pl.multiple_of(ci * C, C)
            rows = pl.ds(r0, C)
            st = s_ref[...]
            st_ref[ci] = st
            out, st_new = _hg_chunk(_heads(q_ref, rows), _heads(f_ref, rows), _heads(i_ref, rows), _heads(g_ref, rows),
                                    _heads(lbl_ref, slice(0, 1)), _heads(lbl_ref, slice(1, 2)), gw_ref[...], st)
            o_ref[rows, :] = _unheads(out).astype(BF16)
            s_ref[...] = st_new
            return carry

        lax.fori_loop(0, ncb, chunk, 0, unroll=True)

    row = pl.BlockSpec((rb, Dm), lambda n: (n, 0))
    return pl.pallas_call(
        body, name=name, grid=(T // rb,),
        in_specs=[row, row, row, row, pl.BlockSpec((2, Dm), lambda n: (0, 0)), pl.BlockSpec((1, HG_DIM), lambda n: (0, 0))],
        out_specs=[row, pl.BlockSpec((ncb, HG_HEADS, HG_DIM, HG_DIM), lambda n: (n, 0, 0, 0))],
        out_shape=[jax.ShapeDtypeStruct((T, Dm), BF16),
                   jax.ShapeDtypeStruct((T // C, HG_HEADS, HG_DIM, HG_DIM), F32)],
        scratch_shapes=[pltpu.VMEM((HG_HEADS, HG_DIM, HG_DIM), F32)],
        compiler_params=_params(("arbitrary",)),
    )(q, f, i, g, lbl, gw)


def _hgrn_bwd(q, f, i, g, lbl, gw, states, dout, *, name, after=None):
    T, Dm = q.shape
    rb = min(HG_RB, T)
    C = min(HG_C, rb)
    ncb = rb // C
    nb = T // rb

    def body(q_ref, f_ref, i_ref, g_ref, lbl_ref, gw_ref, st_ref, do_ref,
             dq_ref, df_ref, di_ref, dg_ref, dlbl_ref, dgw_ref, ds_ref):
        @pl.when(pl.program_id(0) == 0)
        def _():
            ds_ref[...] = jnp.zeros_like(ds_ref)
            dlbl_ref[...] = jnp.zeros_like(dlbl_ref)
            dgw_ref[...] = jnp.zeros_like(dgw_ref)

        def chunk(cj, carry):
            ci = ncb - 1 - cj
            r0 = pl.multiple_of(ci * C, C)
            rows = pl.ds(r0, C)
            _, vjp = jax.vjp(_hg_chunk, _heads(q_ref, rows), _heads(f_ref, rows), _heads(i_ref, rows), _heads(g_ref, rows),
                             _heads(lbl_ref, slice(0, 1)), _heads(lbl_ref, slice(1, 2)), gw_ref[...], st_ref[ci])
            dq, df, di, dg, dl0, dl1, dgw, dst = vjp((_heads(do_ref, rows).astype(F32), ds_ref[...]))
            dq_ref[rows, :] = _unheads(dq).astype(BF16)
            df_ref[rows, :] = _unheads(df).astype(BF16)
            di_ref[rows, :] = _unheads(di).astype(BF16)
            dg_ref[rows, :] = _unheads(dg).astype(BF16)
            dlbl_ref[0:1, :] += _unheads(dl0)
            dlbl_ref[1:2, :] += _unheads(dl1)
            dgw_ref[...] += dgw
            ds_ref[...] = dst
            return carry

        lax.fori_loop(0, ncb, chunk, 0, unroll=True)

    row = pl.BlockSpec((rb, Dm), lambda n: (nb - 1 - n, 0))
    lsp = pl.BlockSpec((2, Dm), lambda n: (0, 0))
    gsp = pl.BlockSpec((1, HG_DIM), lambda n: (0, 0))
    body, xs, xa = _after(body, 8, after)
    return pl.pallas_call(
        body, name=name, grid=(nb,),
        in_specs=[row, row, row, row, lsp, gsp,
                  pl.BlockSpec((ncb, HG_HEADS, HG_DIM, HG_DIM), lambda n: (nb - 1 - n, 0, 0, 0)), row] + xs,
        out_specs=[row, row, row, row, lsp, gsp],
        out_shape=[jax.ShapeDtypeStruct((T, Dm), BF16)] * 4
        + [jax.ShapeDtypeStruct((2, Dm), F32), jax.ShapeDtypeStruct((1, HG_DIM), F32)],
        scratch_shapes=[pltpu.VMEM((HG_HEADS, HG_DIM, HG_DIM), F32)],
        compiler_params=_params(("arbitrary",)),
    )(q, f, i, g, lbl, gw, states, dout, *xa)


CONV_HALO = 2 * SUBLANES


def _conv_rows(u_ref, scr, w, bias, r0, R):
    cur = u_ref[pl.ds(r0, R), :].astype(F32)
    p0 = pl.multiple_of(jnp.maximum(r0 - CONV_HALO, 0), CONV_HALO)
    scr[0:CONV_HALO, :] = jnp.where(r0 > 0, u_ref[pl.ds(p0, CONV_HALO), :].astype(F32), 0.0)
    scr[CONV_HALO:CONV_HALO + R, :] = cur
    s1 = scr[CONV_HALO - 1:CONV_HALO - 1 + R, :]
    s2 = scr[CONV_HALO - 2:CONV_HALO - 2 + R, :]
    return w[0:1, :] * s2 + w[1:2, :] * s1 + w[2:3, :] * cur + bias, cur, s1, s2


def _conv_gate_fwd(ua, ub, wa, wb, ba, bb, *, name):
    T, Fd = ua.shape
    R = min(CONV_R, T)
    tc = LANES

    def body(ua_ref, ub_ref, wa_ref, wb_ref, ba_ref, bb_ref, o_ref, sa, sb):
        wa_, wb_, ba_, bb_ = wa_ref[...], wb_ref[...], ba_ref[...], bb_ref[...]

        def step(ri, carry):
            r0 = pl.multiple_of(ri * R, R)
            ca = _conv_rows(ua_ref, sa, wa_, ba_, r0, R)[0]
            cb = _conv_rows(ub_ref, sb, wb_, bb_, r0, R)[0]
            o_ref[pl.ds(r0, R), :] = (jax.nn.silu(ca) * cb).astype(BF16)
            return carry

        lax.fori_loop(0, T // R, step, 0)

    col = pl.BlockSpec((T, tc), lambda j: (0, j))
    wsp = pl.BlockSpec((3, tc), lambda j: (0, j))
    bsp = pl.BlockSpec((1, tc), lambda j: (0, j))
    return pl.pallas_call(
        body, name=name, grid=(Fd // tc,), in_specs=[col, col, wsp, wsp, bsp, bsp], out_specs=col,
        out_shape=jax.ShapeDtypeStruct((T, Fd), BF16),
        scratch_shapes=[pltpu.VMEM((CONV_HALO + R, tc), F32)] * 2,
        compiler_params=_params(("parallel",)),
    )(ua, ub, wa, wb, ba, bb)


def _conv_gate_bwd(ua, ub, wa, wb, ba, bb, dact, *, name):
    T, Fd = ua.shape
    R = min(CONV_R, T)
    nr = T // R
    tc = LANES

    def body(ua_ref, ub_ref, wa_ref, wb_ref, ba_ref, bb_ref, da_ref,
             dua_ref, dub_ref, dwa_ref, dwb_ref, dba_ref, dbb_ref, sa, sb, sda, sdb):
        wa_, wb_, ba_, bb_ = wa_ref[...], wb_ref[...], ba_ref[...], bb_ref[...]
        sda[R:R + SUBLANES, :] = jnp.zeros((SUBLANES, tc), F32)
        sdb[R:R + SUBLANES, :] = jnp.zeros((SUBLANES, tc), F32)

        def taps(dc, cur, s1, s2):
            return jnp.concatenate([jnp.sum(dc * s2, axis=0, keepdims=True), jnp.sum(dc * s1, axis=0, keepdims=True),
                                    jnp.sum(dc * cur, axis=0, keepdims=True)], axis=0)

        def du_rows(sd, dc, w):
            sd[0:R, :] = dc
            du = w[2:3, :] * dc + w[1:2, :] * sd[1:1 + R, :] + w[0:1, :] * sd[2:2 + R, :]
            sd[R:R + SUBLANES, :] = dc[0:SUBLANES]
            return du

        def step(rj, carry):
            dwa, dwb, dba, dbb = carry
            r0 = pl.multiple_of((nr - 1 - rj) * R, R)
            ca, cura, s1a, s2a = _conv_rows(ua_ref, sa, wa_, ba_, r0, R)
            cb, curb, s1b, s2b = _conv_rows(ub_ref, sb, wb_, bb_, r0, R)
            dact_ = da_ref[pl.ds(r0, R), :].astype(F32)
            sg = jax.nn.sigmoid(ca)
            dca = dact_ * cb * (sg * (1.0 + ca * (1.0 - sg)))
            dcb = dact_ * (ca * sg)
            dua_ref[pl.ds(r0, R), :] = du_rows(sda, dca, wa_).astype(BF16)
            dub_ref[pl.ds(r0, R), :] = du_rows(sdb, dcb, wb_).astype(BF16)
            return (dwa + taps(dca, cura, s1a, s2a), dwb + taps(dcb, curb, s1b, s2b),
                    dba + jnp.sum(dca, axis=0, keepdims=True), dbb + jnp.sum(dcb, axis=0, keepdims=True))

        z3 = jnp.zeros((3, tc), F32)
        z1 = jnp.zeros((1, tc), F32)
        dwa, dwb, dba, dbb = lax.fori_loop(0, nr, step, (z3, z3, z1, z1))
        dwa_ref[...] = dwa
        dwb_ref[...] = dwb
        dba_ref[...] = dba
        dbb_ref[...] = dbb

    col = pl.BlockSpec((T, tc), lambda j: (0, j))
    wsp = pl.BlockSpec((3, tc), lambda j: (0, j))
    bsp = pl.BlockSpec((1, tc), lambda j: (0, j))
    return pl.pallas_call(
        body, name=name, grid=(Fd // tc,), in_specs=[col, col, wsp, wsp, bsp, bsp, col],
        out_specs=[col, col, wsp, wsp, bsp, bsp],
        out_shape=[jax.ShapeDtypeStruct((T, Fd), BF16)] * 2 + [jax.ShapeDtypeStruct((3, Fd), F32)] * 2
        + [jax.ShapeDtypeStruct((1, Fd), F32)] * 2,
        scratch_shapes=[pltpu.VMEM((CONV_HALO + R, tc), F32)] * 2 + [pltpu.VMEM((R + SUBLANES, tc), F32)] * 2,
        compiler_params=_params(("parallel",)),
    )(ua, ub, wa, wb, ba, bb, dact)


def _bucket_index():
    t = np.arange(SW_WINDOW)[:, None] + SW_WINDOW
    s = np.arange(2 * SW_WINDOW)[None, :]
    dist = np.maximum(t - s, 0)
    exact = REL_BUCKETS // 2
    d = np.maximum(dist, 1).astype(np.float32)
    log_b = exact + (np.log(d / np.float32(exact)) / np.float32(math.log(REL_MAX_DIST / exact))
                     * np.float32(REL_BUCKETS - exact)).astype(np.int32)
    bucket = np.where(dist < exact, dist, np.minimum(log_b, REL_BUCKETS - 1))
    return bucket.astype(np.int32).reshape(1, -1)


BIAS_COLS = SW_WINDOW * 2 * SW_WINDOW
BIAS_TILE = 4096


def _bias_from_table(table, bucket, *, name):
    def body(t_ref, idx_ref, o_ref):
        onehot = (lax.broadcasted_iota(jnp.int32, (REL_BUCKETS, BIAS_TILE), 0) == idx_ref[...]).astype(BF16)
        acc = jnp.zeros((SW_Q_HEADS, BIAS_TILE), F32)
        for piece in _split3(t_ref[...]):
            acc = acc + lax.dot_general(piece, onehot, (((0,), (0,)), ((), ())), preferred_element_type=F32)
        o_ref[...] = acc

    return pl.pallas_call(
        body, name=name, grid=(BIAS_COLS // BIAS_TILE,),
        in_specs=[pl.BlockSpec((REL_BUCKETS, SW_Q_HEADS), lambda j: (0, 0)), pl.BlockSpec((1, BIAS_TILE), lambda j: (0, j))],
        out_specs=pl.BlockSpec((SW_Q_HEADS, BIAS_TILE), lambda j: (0, j)),
        out_shape=jax.ShapeDtypeStruct((SW_Q_HEADS, BIAS_COLS), F32),
        compiler_params=_params(("parallel",)),
    )(table, bucket)


def _table_grad(dbias, bucket, *, name):
    def body(d_ref, idx_ref, o_ref):
        @pl.when(pl.program_id(0) == 0)
        def _():
            o_ref[...] = jnp.zeros_like(o_ref)

        onehot = (lax.broadcasted_iota(jnp.int32, (REL_BUCKETS, BIAS_TILE), 0) == idx_ref[...]).astype(BF16)
        acc = jnp.zeros((REL_BUCKETS, SW_Q_HEADS), F32)
        for piece in _split3(d_ref[...]):
            acc = acc + lax.dot_general(onehot, piece, (((1,), (1,)), ((), ())), preferred_element_type=F32)
        o_ref[...] += acc

    return pl.pallas_call(
        body, name=name, grid=(BIAS_COLS // BIAS_TILE,),
        in_specs=[pl.BlockSpec((SW_Q_HEADS, BIAS_TILE), lambda j: (0, j)), pl.BlockSpec((1, BIAS_TILE), lambda j: (0, j))],
        out_specs=pl.BlockSpec((REL_BUCKETS, SW_Q_HEADS), lambda j: (0, 0)),
        out_shape=jax.ShapeDtypeStruct((REL_BUCKETS, SW_Q_HEADS), F32),
        compiler_params=_params(("arbitrary",)),
    )(dbias, bucket)


def _band_mask(n):
    rows = SW_GROUP * SW_WINDOW
    t = (lax.broadcasted_iota(jnp.int32, (rows, 2 * SW_WINDOW), 0) & (SW_WINDOW - 1)) + SW_WINDOW
    s = lax.broadcasted_iota(jnp.int32, (rows, 2 * SW_WINDOW), 1)
    dist = t - s
    return (dist >= 0) & (dist < SW_WINDOW) & ((n > 0) | (s >= SW_WINDOW))


def _head_cols(h):
    return slice(h * SW_HEAD_DIM, (h + 1) * SW_HEAD_DIM)


def _group_inputs(q_ref, bias_ref, sink_ref, g):
    heads = range(g * SW_GROUP, (g + 1) * SW_GROUP)
    q = jnp.concatenate([q_ref[:, _head_cols(h)] for h in heads], axis=0)
    sink = jnp.concatenate([jnp.broadcast_to(sink_ref[:, h:h + 1], (SW_WINDOW, 1)) for h in heads], axis=0)
    bias = bias_ref[g * SW_GROUP:(g + 1) * SW_GROUP].reshape(SW_GROUP * SW_WINDOW, 2 * SW_WINDOW)
    return heads, q, bias, sink


KV_DIM = SW_KV_HEADS * SW_HEAD_DIM


def _kv_pair(kvp_ref, kvc_ref, g):
    ks = slice(g * SW_HEAD_DIM, (g + 1) * SW_HEAD_DIM)
    vs = slice(KV_DIM + g * SW_HEAD_DIM, KV_DIM + (g + 1) * SW_HEAD_DIM)
    kk = jnp.concatenate([kvp_ref[:, ks], kvc_ref[:, ks]], axis=0)
    vv = jnp.concatenate([kvp_ref[:, vs], kvc_ref[:, vs]], axis=0)
    return kk, vv, ks, vs


def _attn_fwd(q1, kv, bias, sinks, *, name):
    T, Dm = q1.shape
    W = SW_WINDOW

    def body(q_ref, kvc_ref, kvp_ref, bias_ref, sink_ref, o_ref):
        mask = _band_mask(pl.program_id(0))
        G = range(SW_KV_HEADS)
        ins = [_group_inputs(q_ref, bias_ref, sink_ref, g) for g in G]
        kvs = [_kv_pair(kvp_ref, kvc_ref, g) for g in G]
        lg = [jnp.where(mask, mm_nt(ins[g][1], kvs[g][0]) * (SW_HEAD_DIM ** -0.5) + ins[g][2], -jnp.inf) for g in G]
        m = [jnp.maximum(jnp.max(lg[g], axis=-1, keepdims=True), ins[g][3]) for g in G]
        p = [jnp.exp(lg[g] - m[g]) for g in G]
        den = [jnp.sum(p[g], axis=-1, keepdims=True) + jnp.exp(ins[g][3] - m[g]) for g in G]
        o = [mm(p[g], kvs[g][1]) / den[g] for g in G]
        for g in G:
            for r, h in enumerate(ins[g][0]):
                o_ref[:, _head_cols(h)] = o[g][r * W:(r + 1) * W].astype(BF16)

    return pl.pallas_call(
        body, name=name, grid=(T // W,),
        in_specs=[pl.BlockSpec((W, Dm), lambda n: (n, 0)),
                  pl.BlockSpec((W, 2 * KV_DIM), lambda n: (n, 0)),
                  pl.BlockSpec((W, 2 * KV_DIM), lambda n: (jnp.maximum(n - 1, 0), 0)),
                  pl.BlockSpec((SW_Q_HEADS, W, 2 * W), lambda n: (0, 0, 0)),
                  pl.BlockSpec((1, SW_Q_HEADS), lambda n: (0, 0))],
        out_specs=pl.BlockSpec((W, Dm), lambda n: (n, 0)),
        out_shape=jax.ShapeDtypeStruct((T, Dm), BF16),
        compiler_params=_params(("parallel",)),
    )(q1, kv, kv, bias, sinks)


def _attn_bwd(q1, kv, bias, sinks, do, *, name):
    T, Dm = q1.shape
    W = SW_WINDOW
    nb = T // W

    def body(q_ref, kvc_ref, kvp_ref, bias_ref, sink_ref, do_ref,
             dq_ref, dkv_ref, dbias_ref, dsink_ref, carry_ref):
        @pl.when(pl.program_id(0) == 0)
        def _():
            carry_ref[...] = jnp.zeros_like(carry_ref)
            dbias_ref[...] = jnp.zeros_like(dbias_ref)
            dsink_ref[...] = jnp.zeros_like(dsink_ref)

        n = nb - 1 - pl.program_id(0)
        mask = _band_mask(n)
        lane = lax.broadcasted_iota(jnp.int32, (1, SW_Q_HEADS), 1)
        sc = SW_HEAD_DIM ** -0.5
        G = range(SW_KV_HEADS)
        ins = [_group_inputs(q_ref, bias_ref, sink_ref, g) for g in G]
        kvs = [_kv_pair(kvp_ref, kvc_ref, g) for g in G]
        do = [jnp.concatenate([do_ref[:, _head_cols(h)] for h in ins[g][0]], axis=0) for g in G]
        lg = [jnp.where(mask, mm_nt(ins[g][1], kvs[g][0]) * sc + ins[g][2], -jnp.inf) for g in G]
        m = [jnp.maximum(jnp.max(lg[g], axis=-1, keepdims=True), ins[g][3]) for g in G]
        p = [jnp.exp(lg[g] - m[g]) for g in G]
        ps = [jnp.exp(ins[g][3] - m[g]) for g in G]
        rden = [1.0 / (jnp.sum(p[g], axis=-1, keepdims=True) + ps[g]) for g in G]
        pn = [p[g] * rden[g] for g in G]
        dpn = [mm_nt(do[g], kvs[g][1]) for g in G]
        delta = [jnp.sum(pn[g] * dpn[g], axis=-1, keepdims=True) for g in G]
        ds = [pn[g] * (dpn[g] - delta[g]) for g in G]
        dsr = [-(ps[g] * rden[g]) * delta[g] for g in G]
        dq = [mm(ds[g], kvs[g][0]) * sc for g in G]
        dkk = [mm_tn(ds[g], ins[g][1]) * sc for g in G]
        dvv = [mm_tn(pn[g], do[g]) for g in G]
        dsink = jnp.zeros((1, SW_Q_HEADS), F32)
        for g in G:
            _, _, ks, vs = kvs[g]
            dbias_ref[g * SW_GROUP:(g + 1) * SW_GROUP] += ds[g].reshape(SW_GROUP, W, 2 * W)
            for r, h in enumerate(ins[g][0]):
                dq_ref[:, _head_cols(h)] = dq[g][r * W:(r + 1) * W].astype(BF16)
                dsink = dsink + jnp.where(lane == h, jnp.sum(dsr[g][r * W:(r + 1) * W], axis=0, keepdims=True), 0.0)
            dkv_ref[:, ks] = (carry_ref[:, ks] + dkk[g][W:]).astype(BF16)
            dkv_ref[:, vs] = (carry_ref[:, vs] + dvv[g][W:]).astype(BF16)
            carry_ref[:, ks] = dkk[g][:W]
            carry_ref[:, vs] = dvv[g][:W]
        dsink_ref[...] += dsink

    rev = lambda n: (nb - 1 - n, 0)
    return pl.pallas_call(
        body, name=name, grid=(nb,),
        in_specs=[pl.BlockSpec((W, Dm), rev),
                  pl.BlockSpec((W, 2 * KV_DIM), rev),
                  pl.BlockSpec((W, 2 * KV_DIM), lambda n: (jnp.maximum(nb - 2 - n, 0), 0)),
                  pl.BlockSpec((SW_Q_HEADS, W, 2 * W), lambda n: (0, 0, 0)),
                  pl.BlockSpec((1, SW_Q_HEADS), lambda n: (0, 0)),
                  pl.BlockSpec((W, Dm), rev)],
        out_specs=[pl.BlockSpec((W, Dm), rev), pl.BlockSpec((W, 2 * KV_DIM), rev),
                   pl.BlockSpec((SW_Q_HEADS, W, 2 * W), lambda n: (0, 0, 0)),
                   pl.BlockSpec((1, SW_Q_HEADS), lambda n: (0, 0))],
        out_shape=[jax.ShapeDtypeStruct((T, Dm), BF16), jax.ShapeDtypeStruct((T, 2 * KV_DIM), BF16),
                   jax.ShapeDtypeStruct((SW_Q_HEADS, W, 2 * W), F32), jax.ShapeDtypeStruct((1, SW_Q_HEADS), F32)],
        scratch_shapes=[pltpu.VMEM((W, 2 * KV_DIM), F32)],
        compiler_params=_params(("arbitrary",)),
    )(q1, kv, kv, bias, sinks, do)


def _ffn_fwd(hb, w, l, after=None):
    ua = _matmul(hb, w["ffn_in_a"][l], mode="nn", out_dtype=BF16, name=f"ffn{l}_up_a", tm=1024, after=after)
    ub = _matmul(hb, w["ffn_in_b"][l], mode="nn", out_dtype=BF16, name=f"ffn{l}_up_b", tm=1024)
    act = _conv_gate_fwd(ua, ub, w["conv_w_a"][l], w["conv_w_b"][l], w["conv_b_a"][l], w["conv_b_b"][l],
                         name=f"ffn{l}_conv_gate")
    return ua, ub, act


def _ffn_bwd(dffb, dh_scaled, hb, ua, ub, act, w, l):
    dact = _matmul(dffb, w["ffn_out"][l], mode="nt", out_dtype=BF16, name=f"ffn{l}_down_dx", tm=1024)
    g_out = _matmul(act, dffb, mode="tn", name=f"ffn{l}_down_dw", tm=1408, tn=1024, tk=1024)
    dua, dub, dwa, dwb, dba, dbb = _conv_gate_bwd(ua, ub, w["conv_w_a"][l], w["conv_w_b"][l], w["conv_b_a"][l],
                                                  w["conv_b_b"][l], dact, name=f"ffn{l}_conv_gate_bwd")
    dh = _matmul(dua, w["ffn_in_a"][l], mode="nt", add=dh_scaled, add_scale=ALPHA, name=f"ffn{l}_up_a_dx",
                 tn=1024, tk=FFN_DIM)
    dh = _matmul(dub, w["ffn_in_b"][l], mode="nt", add=dh, name=f"ffn{l}_up_b_dx", tn=1024, tk=FFN_DIM)
    g_in_a = _matmul(hb, dua, mode="tn", name=f"ffn{l}_up_a_dw", tm=1024, tn=FFN_DIM // 2, tk=1024, split_n=True)
    g_in_b = _matmul(hb, dub, mode="tn", name=f"ffn{l}_up_b_dw", tm=1024, tn=FFN_DIM // 2, tk=1024, split_n=True)
    return dh, dict(ffn_out=g_out, ffn_in_a=g_in_a, ffn_in_b=g_in_b, conv_w_a=dwa, conv_w_b=dwb, conv_b_a=dba, conv_b_b=dbb)


def _local_step(x, tgt, w, more_weights, emit):
    bucket = jnp.asarray(_bucket_index())
    xb = x.astype(BF16)

    pre = [_matmul(xb, w["hg_in"][j], mode="nn", out_dtype=BF16, name=f"hg_in_{j}", tm=1024, tn=1024,
                   after=w.get("token") if j == 0 else None) for j in range(4)]
    og, states = _hgrn_fwd(*pre, w["lb_logits"], w["gnorm"], name="hgrn_fwd")
    z1, h1, h1b = _matmul_ln(og, w["hg_out"], x, w["ln_mix_g"][0], w["ln_mix_b"][0], name="hg_out_ln")
    w = {**w, **more_weights(1, h1b)}
    ua0, ub0, act0 = _ffn_fwd(h1b, w, 0, after=w.get("token"))
    z2, h2, h2b = _matmul_ln(act0, w["ffn_out"][0], h1, w["ln_ffn_g"][0], w["ln_ffn_b"][0], name="ffn0_down_ln")
    kv = _matmul(h2b, w["kv"], mode="nn", name="kv_proj")

    bias = _bias_from_table(w["rel_bias"], bucket, name="rel_bias_expand").reshape(SW_Q_HEADS, SW_WINDOW, 2 * SW_WINDOW)
    q1 = _matmul(h2b, w["sw_q"], mode="nn", name="sw_q")
    o1 = _attn_fwd(q1, kv, bias, w["sinks"], name="attn_fwd")
    z3, h3, h3b = _matmul_ln(o1, w["sw_out"], h2, w["ln_mix_g"][1], w["ln_mix_b"][1], name="sw_out_ln")
    w = {**w, **more_weights(2, h3b)}
    ua1, ub1, act1 = _ffn_fwd(h3b, w, 1)

    g = {}
    dz, dzb, dg_, db_, loss_tile = _matmul_ln(act1, w["ffn_out"][1], h3, w["ln_ffn_g"][1], w["ln_ffn_b"][1], tgt=tgt,
                                              name="ffn1_down_ln_loss")

    g["ln_ffn_g1"], g["ln_ffn_b1"] = dg_, db_
    dh3, gf1 = _ffn_bwd(dzb, dz, h3b, ua1, ub1, act1, w, 1)
    dz, dzb, dg_, db_ = _ln_bwd(dh3, z3, w["ln_mix_g"][1], w["ln_mix_b"][1], name="ln_mix1_bwd")
    g["ln_mix_g1"], g["ln_mix_b1"] = dg_, db_
    do1 = _matmul(dzb, w["sw_out"], mode="nt", out_dtype=BF16, name="sw_out_dx")
    g_sw_out = _matmul(o1, dzb, mode="tn", name="sw_out_dw", tm=1024, tn=1024, tk=1024)
    dq1, dkv, dbias, dsinks = _attn_bwd(q1, kv, bias, w["sinks"], do1, name="attn_bwd")
    g["sinks"] = dsinks
    g["rel_bias"] = _table_grad(dbias.reshape(SW_Q_HEADS, BIAS_COLS), bucket, name="rel_bias_grad")
    dh2 = _matmul(dq1, w["sw_q"], mode="nt", add=dz, add_scale=ALPHA, name="sw_q_dx", tn=1024)
    dh2 = _matmul(dkv, w["kv"], mode="nt", add=dh2, name="kv_dx", tn=1024)
    g_sw_q = _matmul(h2b, dq1, mode="tn", name="sw_q_dw", tm=1024, tn=1024, tk=1024)
    g_kv = _matmul(h2b, dkv, mode="tn", name="kv_dw", tm=1024, tn=512, tk=1024)
    tok = emit(1, dict(sw_q=g_sw_q, sw_out=g_sw_out, kv=g_kv, ffn_in_a=gf1["ffn_in_a"], ffn_in_b=gf1["ffn_in_b"],
                       ffn_out=gf1["ffn_out"]))

    dz, dzb, dg_, db_ = _ln_bwd(dh2, z2, w["ln_ffn_g"][0], w["ln_ffn_b"][0], name="ln_ffn0_bwd", after=tok)
    g["ln_ffn_g0"], g["ln_ffn_b0"] = dg_, db_
    dh1, gf0 = _ffn_bwd(dzb, dz, h1b, ua0, ub0, act0, w, 0)
    dz, dzb, dg_, db_ = _ln_bwd(dh1, z1, w["ln_mix_g"][0], w["ln_mix_b"][0], name="ln_mix0_bwd")
    g["ln_mix_g0"], g["ln_mix_b0"] = dg_, db_
    dog = _matmul(dzb, w["hg_out"], mode="nt", out_dtype=BF16, name="hg_out_dx")
    g_hg_out = _matmul(og, dzb, mode="tn", name="hg_out_dw", tm=1024, tn=1024, tk=1024)
    tok = emit(2, dict(hg_out=g_hg_out, ffn_in_a=gf0["ffn_in_a"], ffn_in_b=gf0["ffn_in_b"], ffn_out=gf0["ffn_out"]))
    dpre = _hgrn_bwd(*pre, w["lb_logits"], w["gnorm"], states, dog, name="hgrn_bwd", after=tok)
    g["lb_logits"], g["gnorm"] = dpre[4], dpre[5]
    tok = emit(3, dict(hg_in=[_matmul(xb, dpre[j], mode="tn", name=f"hg_in_{j}_dw", tm=1024, tn=1024, tk=1024)
                              for j in range(4)]))
    dx = dz
    for j in range(4):
        dx = _matmul(dpre[j], w["hg_in"][j], mode="nt", add=dx, add_scale=ALPHA if j == 0 else 1.0,
                     name=f"hg_in_{j}_dx", tn=1024, after=tok if j == 0 else None)
    g["conv"] = [{k: gf[k] for k in ("conv_w_a", "conv_w_b", "conv_b_a", "conv_b_b")} for gf in (gf0, gf1)]
    return loss_tile, dx, g


def _adamw(wt, ga, gb, m, v, *, name, rows=None, prev=None):
    R, Cc = wt.shape
    r0, n = rows if rows is not None else (0, R)
    tr = _tile(n, 256, SUBLANES) if n % SUBLANES == 0 else n
    assert r0 % tr == 0
    c1 = 1.0 - ADAM_B1 ** ADAM_STEP
    c2 = 1.0 - ADAM_B2 ** ADAM_STEP
    two = gb is not None
    n_in = 5 if two else 4

    def body(*refs):
        if two:
            w_ref, ga_ref, gb_ref, m_ref, v_ref = refs[:5]
            g_ = ga_ref[...] + gb_ref[...]
        else:
            w_ref, ga_ref, m_ref, v_ref = refs[:4]
            g_ = ga_ref[...]
        g_ref, d_ref, nm_ref, nv_ref = refs[-4:]
        nm = ADAM_B1 * m_ref[...] + (1.0 - ADAM_B1) * g_
        nv = ADAM_B2 * v_ref[...] + (1.0 - ADAM_B2) * (g_ * g_)
        g_ref[...] = g_
        d_ref[...] = -ADAM_LR * ((nm / c1) / (jnp.sqrt(nv / c2) + ADAM_EPS) + ADAM_WD * w_ref[...])
        nm_ref[...] = nm
        nv_ref[...] = nv

    full = pl.BlockSpec((tr, Cc), lambda i: (i + r0 // tr, 0))
    part = pl.BlockSpec((tr, Cc), lambda i: (i, 0))
    args = (wt, ga, gb, m, v) if two else (wt, ga, m, v)
    in_specs = [full] + [part] * (n_in - 3) + [full, full]
    aliases = {}
    if prev is not None:
        args, in_specs = args + tuple(prev), in_specs + [ANY_SPEC] * 4
        aliases = {n_in + t: t for t in range(4)}
    return pl.pallas_call(
        body, name=name, grid=(n // tr,), in_specs=in_specs, out_specs=[full] * 4,
        out_shape=[jax.ShapeDtypeStruct((R, Cc), F32)] * 4, input_output_aliases=aliases,
        compiler_params=_params(("parallel",)),
    )(*args)


HBM_SPEC = pl.BlockSpec(memory_space=pltpu.HBM)
SEM_SPEC = pl.BlockSpec(memory_space=pltpu.SEMAPHORE)
VMEM_SPEC = pl.BlockSpec(memory_space=pltpu.VMEM)
DATAFLOW = pltpu.SideEffectType.DATAFLOW_SIDE_EFFECTING


def _in_hbm(a):
    return pltpu.with_memory_space_constraint(a, pltpu.HBM)


def _place():
    return lax.axis_index("x"), lax.axis_index("y"), lax.axis_index("c")


def _other_chips(x, y):
    return [(1 - x, y), (x, 1 - y), (1 - x, 1 - y)]


def _sum8(v, *, name, after=None):
    r = v.shape[0]

    def body(v_ref, all_ref, o_ref, send_sems, recv_sems, local_sem):
        x, y, c = _place()
        me, sibling = (x, y, c), (x, y, 1 - c)
        chips = _other_chips(x, y)

        def rows(px, py, pc):
            return all_ref.at[pl.ds((4 * px + 2 * py + pc) * r, r), :]

        def copy(k, block, to, src=None):
            return pltpu.make_async_remote_copy(
                src_ref=rows(*block) if src is None else src, dst_ref=rows(*block),
                send_sem=send_sems.at[k], recv_sem=recv_sems.at[k], device_id=to, device_id_type=MESH)

        mine = pltpu.make_async_copy(v_ref, rows(*me), local_sem)
        mine.start()
        first = [copy(0, me, sibling, src=v_ref)]
        first += [copy(1 + j, me, (*chip, c), src=v_ref) for j, chip in enumerate(chips)]
        for cp in first:
            cp.start()
        passed = [copy(4 + j, (*chip, c), sibling) for j, chip in enumerate(chips)]
        for j, chip in enumerate(chips):
            copy(1 + j, (*chip, c), me).wait_recv()
            passed[j].start()
        copy(0, sibling, me).wait_recv()
        for j, chip in enumerate(chips):
            copy(4 + j, (*chip, 1 - c), me).wait_recv()
        for cp in first + passed:
            cp.wait_send()
        mine.wait()
        acc = all_ref[pl.ds(0, r), :]
        for d in range(1, N_DEV):
            acc = acc + all_ref[pl.ds(d * r, r), :]
        o_ref[...] = acc

    body, xs, xa = _after(body, 1, after)
    return pl.pallas_call(
        body, name=name, in_specs=[VMEM_SPEC] + xs, out_specs=[VMEM_SPEC, VMEM_SPEC],
        out_shape=[jax.ShapeDtypeStruct((N_DEV * r, LANES), F32), jax.ShapeDtypeStruct((r, LANES), F32)],
        scratch_shapes=[pltpu.SemaphoreType.DMA((7,)), pltpu.SemaphoreType.DMA((7,)), pltpu.SemaphoreType.DMA],
        compiler_params=pltpu.CompilerParams(vmem_limit_bytes=VMEM_LIMIT),
    )(v, *xa)[1]


def _gather_chips(shard, *, name):
    R, Cc = shard.shape
    half = R // 2
    assert half * 2 == R

    def body(s_ref, o_ref, send_sems, recv_sems, local_sem):
        x, y, c = _place()
        sibling = (x, y, 1 - c)
        chips = _other_chips(x, y)

        def part(px, py, pc):
            return o_ref.at[2 * px + py, pl.ds(pc * half, half), :]

        def copy(k, block, to, src=None):
            return pltpu.make_async_remote_copy(
                src_ref=part(*block) if src is None else src, dst_ref=part(*block),
                send_sem=send_sems.at[k], recv_sem=recv_sems.at[k], device_id=to, device_id_type=MESH)

        mine = pltpu.make_async_copy(s_ref, o_ref.at[2 * x + y], local_sem)
        mine.start()
        my_half = s_ref.at[pl.ds(c * half, half), :]
        first = [copy(j, (x, y, c), (*chip, c), src=my_half) for j, chip in enumerate(chips)]
        for cp in first:
            cp.start()
        passed = [copy(3 + j, (*chip, c), sibling) for j, chip in enumerate(chips)]
        for j, chip in enumerate(chips):
            copy(j, (*chip, c), (x, y, c)).wait_recv()
            passed[j].start()
        for j, chip in enumerate(chips):
            copy(3 + j, (*chip, 1 - c), (x, y, c)).wait_recv()
        for cp in first + passed:
            cp.wait_send()
        mine.wait()

    return pl.pallas_call(
        body, name=name, in_specs=[HBM_SPEC], out_specs=HBM_SPEC,
        out_shape=jax.ShapeDtypeStruct((N_CHIPS, R, Cc), shard.dtype),
        scratch_shapes=[pltpu.SemaphoreType.DMA((6,)), pltpu.SemaphoreType.DMA((6,)), pltpu.SemaphoreType.DMA],
    )(shard)


def _swap_sibling(vs, *, name):
    n = len(vs)

    def body(*refs):
        src, dst, send_sems, recv_sems = refs[:n], refs[n:2 * n], refs[2 * n], refs[2 * n + 1]
        x, y, c = _place()
        cps = [pltpu.make_async_remote_copy(src_ref=src[i], dst_ref=dst[i], send_sem=send_sems.at[i],
                                            recv_sem=recv_sems.at[i], device_id=(x, y, 1 - c), device_id_type=MESH)
               for i in range(n)]
        for cp in cps:
            cp.start()
        for cp in cps:
            cp.wait()

    return pl.pallas_call(
        body, name=name, in_specs=[HBM_SPEC] * n, out_specs=[HBM_SPEC] * n,
        out_shape=[jax.ShapeDtypeStruct(v.shape, v.dtype) for v in vs],
        scratch_shapes=[pltpu.SemaphoreType.DMA((n,)), pltpu.SemaphoreType.DMA((n,))],
    )(*vs)


def _half(ref, j, c, half):
    return ref.at[j, pl.ds(c * half, half), :]


def _gather_start(shard, after, *, name):
    R, Cc = shard.shape
    half = R // 2

    def body(src, land, send, recv, src_out, land_out, token):
        x, y, c = _place()
        for k, (px, py) in enumerate(_other_chips(x, y)):
            pltpu.make_async_remote_copy(src_ref=src.at[pl.ds(c * half, half), :], dst_ref=_half(land, 2 * x + y, c, half),
                                         send_sem=send.at[k], recv_sem=recv.at[k], device_id=(px, py, c),
                                         device_id_type=MESH).start()
        token[...] = jnp.zeros_like(token)

    land = lax.empty((N_CHIPS, R, Cc), shard.dtype)
    body, xs, xa = _after(body, 2, after)
    out = pl.pallas_call(
        body, name=name, in_specs=[HBM_SPEC, HBM_SPEC] + xs,
        out_specs=[SEM_SPEC, SEM_SPEC, HBM_SPEC, HBM_SPEC, VMEM_SPEC],
        out_shape=[pltpu.SemaphoreType.DMA((3,)), pltpu.SemaphoreType.DMA((3,)), pltpu.HBM(shard.shape, shard.dtype),
                   pltpu.HBM(land.shape, land.dtype), jax.ShapeDtypeStruct((SUBLANES, LANES), F32)],
        input_output_aliases={0: 2, 1: 3},
        compiler_params=pltpu.CompilerParams(has_side_effects=DATAFLOW),
    )(_in_hbm(shard), _in_hbm(land), *xa)
    return out[:4], out[4]


def _gather_wait(handle, after, *, name):
    send_sems, recv_sems, src, land = handle
    half = src.shape[0] // 2

    def body(src_ref, land_ref, send_ref, recv_ref, after_ref, src_out, land_out):
        x, y, c = _place()
        for k, (px, py) in enumerate(_other_chips(x, y)):
            cp = pltpu.make_async_remote_copy(src_ref=src_ref.at[pl.ds(c * half, half), :],
                                              dst_ref=_half(land_ref, 2 * px + py, c, half), send_sem=send_ref.at[k],
                                              recv_sem=recv_ref.at[k], device_id=(px, py, c), device_id_type=MESH)
            cp.wait_send()
            cp.wait_recv()

    return pl.pallas_call(
        body, name=name, in_specs=[HBM_SPEC, HBM_SPEC, SEM_SPEC, SEM_SPEC, ANY_SPEC], out_specs=[HBM_SPEC, HBM_SPEC],
        out_shape=[pltpu.HBM(src.shape, src.dtype), pltpu.HBM(land.shape, land.dtype)],
        input_output_aliases={0: 0, 1: 1},
        compiler_params=pltpu.CompilerParams(has_side_effects=DATAFLOW),
    )(src, land, send_sems, recv_sems, after)[1]


def _fill_sibling(land, *, name):
    _, R, Cc = land.shape
    half = R // 2

    def body(in_ref, o_ref, send_sems, recv_sems):
        x, y, c = _place()
        chips = _other_chips(x, y)
        cps = [pltpu.make_async_remote_copy(src_ref=_half(in_ref, 2 * px + py, c, half),
                                            dst_ref=_half(o_ref, 2 * px + py, c, half), send_sem=send_sems.at[k],
                                            recv_sem=recv_sems.at[k], device_id=(x, y, 1 - c), device_id_type=MESH)
               for k, (px, py) in enumerate(chips)]
        for cp in cps:
            cp.start()
        for k, (px, py) in enumerate(chips):
            pltpu.make_async_remote_copy(src_ref=_half(in_ref, 2 * px + py, 1 - c, half),
                                         dst_ref=_half(o_ref, 2 * px + py, 1 - c, half), send_sem=send_sems.at[k],
                                         recv_sem=recv_sems.at[k], device_id=(x, y, 1 - c), device_id_type=MESH).wait_recv()
        for cp in cps:
            cp.wait_send()

    return pl.pallas_call(
        body, name=name, in_specs=[HBM_SPEC], out_specs=HBM_SPEC, out_shape=jax.ShapeDtypeStruct(land.shape, land.dtype),
        scratch_shapes=[pltpu.SemaphoreType.DMA((3,)), pltpu.SemaphoreType.DMA((3,))],
        input_output_aliases={0: 0},
    )(land)


def _scatter_copies(src, land, send, recv):
    x, y, c = _place()
    return [pltpu.make_async_remote_copy(src_ref=src[i].at[2 * px + py], dst_ref=land[i].at[k], send_sem=send.at[3 * i + k],
                                         recv_sem=recv.at[3 * i + k], device_id=(px, py, c), device_id_type=MESH)
            for i in range(len(src)) for k, (px, py) in enumerate(_other_chips(x, y))]


def _scatter_start(pieces, *, name):
    n = len(pieces)

    def body(*refs):
        src, land, send, recv, token = refs[:n], refs[n:2 * n], refs[2 * n], refs[2 * n + 1], refs[-1]
        for cp in _scatter_copies(src, land, send, recv):
            cp.start()
        token[...] = jnp.zeros_like(token)

    lands = [lax.empty((3,) + p.shape[1:], p.dtype) for p in pieces]
    sems = pltpu.SemaphoreType.DMA((3 * n,))
    out = pl.pallas_call(
        body, name=name, in_specs=[HBM_SPEC] * (2 * n),
        out_specs=[SEM_SPEC, SEM_SPEC] + [HBM_SPEC] * (2 * n) + [VMEM_SPEC],
        out_shape=[sems, sems] + [pltpu.HBM(a.shape, a.dtype) for a in pieces + lands]
        + [jax.ShapeDtypeStruct((SUBLANES, LANES), F32)],
        input_output_aliases={i: 2 + i for i in range(2 * n)},
        compiler_params=pltpu.CompilerParams(has_side_effects=DATAFLOW),
    )(*[_in_hbm(a) for a in pieces + lands])
    return (out[0], out[1], out[2:2 + n], out[2 + n:2 + 2 * n]), out[-1]


def _scatter_wait(handle, after, *, name):
    send_sems, recv_sems, srcs, lands = handle
    n = len(srcs)

    def body(*refs):
        src, land, send, recv = refs[:n], refs[n:2 * n], refs[2 * n], refs[2 * n + 1]
        for cp in _scatter_copies(src, land, send, recv):
            cp.wait_send()
            cp.wait_recv()

    both = list(srcs) + list(lands)
    out = pl.pallas_call(
        body, name=name, in_specs=[HBM_SPEC] * (2 * n) + [SEM_SPEC, SEM_SPEC, ANY_SPEC], out_specs=[HBM_SPEC] * (2 * n),
        out_shape=[pltpu.HBM(a.shape, a.dtype) for a in both],
        input_output_aliases={i: i for i in range(2 * n)},
        compiler_params=pltpu.CompilerParams(has_side_effects=DATAFLOW),
    )(*both, send_sems, recv_sems, after)
    return out[n:]


def _chip_sum(pieces, got, chip, *, name):
    _, R, Cc = pieces.shape
    tr = _tile(R, 256, SUBLANES)

    def body(chip_ref, a_ref, g_ref, o_ref):
        o_ref[...] = ((a_ref[...] + g_ref[0].astype(F32)) + g_ref[1].astype(F32)) + g_ref[2].astype(F32)

    return pl.pallas_call(
        body, name=name,
        grid_spec=pltpu.PrefetchScalarGridSpec(
            num_scalar_prefetch=1, grid=(R // tr,),
            in_specs=[pl.BlockSpec((None, tr, Cc), lambda i, ch: (ch[0], i, 0)),
                      pl.BlockSpec((3, tr, Cc), lambda i, ch: (0, i, 0))],
            out_specs=pl.BlockSpec((tr, Cc), lambda i, ch: (i, 0))),
        out_shape=jax.ShapeDtypeStruct((R, Cc), F32),
        compiler_params=_params(("parallel",)),
    )(chip, pieces, got)


PACK_COLS = 1024


def _pack_rows(parts):
    return jnp.concatenate([p.reshape(-1, PACK_COLS) for p in parts], axis=0)


def _unpack_rows(block, shapes):
    lead = block.shape[:-2]
    out, off = [], 0
    for s in shapes:
        r = int(np.prod(s)) // PACK_COLS
        out.append(block[..., off:off + r, :].reshape(lead + tuple(s)))
        off += r
    assert off == block.shape[-2]
    return out


def _flat128(parts):
    out = []
    for p in parts:
        v = p.reshape(-1)
        pad = (-v.shape[0]) % LANES
        out.append(jnp.pad(v, (0, pad)) if pad else v)
    v = jnp.concatenate(out)
    pad = (-v.shape[0]) % (SUBLANES * LANES)
    if pad:
        v = jnp.pad(v, (0, pad))
    return v.reshape(-1, LANES)


def _unflat128(block, shapes):
    v = block.reshape(-1)
    out, off = [], 0
    for s in shapes:
        n = int(np.prod(s))
        out.append(v[off:off + n].reshape(s))
        off += n + ((-n) % LANES)
    return out


def kernel(x, hgrn_w_in, hgrn_lb_logits, hgrn_gnorm_w, hgrn_w_out, swa_w_q, swa_sinks, swa_w_out, shared_w_kv, rel_bias, ffn_w_in, ffn_conv_w, ffn_conv_b, ffn_w_out, ln_mix_g, ln_mix_b, ln_ffn_g, ln_ffn_b, loss_target, m_hgrn_w_in, m_hgrn_lb_logits, m_hgrn_gnorm_w, m_hgrn_w_out, m_swa_w_q, m_swa_sinks, m_swa_w_out, m_shared_w_kv, m_rel_bias, m_ffn_w_in, m_ffn_conv_w, m_ffn_conv_b, m_ffn_w_out, m_ln_mix_g, m_ln_mix_b, m_ln_ffn_g, m_ln_ffn_b, v_hgrn_w_in, v_hgrn_lb_logits, v_hgrn_gnorm_w, v_hgrn_w_out, v_swa_w_q, v_swa_sinks, v_swa_w_out, v_shared_w_kv, v_rel_bias, v_ffn_w_in, v_ffn_conv_w, v_ffn_conv_b, v_ffn_w_out, v_ln_mix_g, v_ln_mix_b, v_ln_ffn_g, v_ln_ffn_b):
    xi, yi, ci = _place()
    chip = 2 * xi + yi
    Dm = D_MODEL
    FC = 2 * FFN_DIM // N_CHIPS
    Fo = FFN_DIM // N_CHIPS
    Dq = Dm // N_CHIPS
    bf = lambda a: a.astype(BF16)

    shard0 = _pack_rows([bf(hgrn_w_in), bf(hgrn_w_out)])
    shard1 = _pack_rows([bf(swa_w_q), bf(swa_w_out), bf(shared_w_kv), bf(ffn_w_in[0]), bf(ffn_w_out[0])])
    shard2 = _pack_rows([bf(ffn_w_in[1]), bf(ffn_w_out[1])])
    handle0, token0 = _gather_start(shard0, None, name="gather_w0_start")

    lb_full = lax.dynamic_update_slice(jnp.zeros((2, Dm), F32), hgrn_lb_logits, (0, chip * Dq))
    cw_full = lax.dynamic_update_slice(jnp.zeros((DEPTH, 3, 2 * FFN_DIM), F32), ffn_conv_w, (0, 0, chip * FC))
    only_south = (ci == 0).astype(F32)
    small_in = _sum8(_flat128([lb_full, cw_full]) * only_south, name="gather_small", after=token0)
    lb_full, cw_full = _unflat128(small_in, [(2, Dm), (DEPTH, 3, 2 * FFN_DIM)])

    land0 = _fill_sibling(_gather_wait(handle0, small_in, name="gather_w0_wait"), name="gather_w0_fill")
    all0 = lax.dynamic_update_slice(land0, shard0[None], (chip, 0, 0))
    handle1, token1 = _gather_start(shard1, land0, name="gather_w1_start")
    w_in, w_hg_out = _unpack_rows(all0, [(Dm, Dm), (Dq, Dm)])

    def ffn_weights(w_fi, w_fo, l):
        return {"ffn_in_a": {l: jnp.concatenate([w_fi[0], w_fi[1]], axis=1)},
                "ffn_in_b": {l: jnp.concatenate([w_fi[2], w_fi[3]], axis=1)},
                "ffn_out": {l: w_fo.reshape(FFN_DIM, Dm)}}

    got = {}

    def more_weights(k, after):
        shard = (shard1, shard2)[k - 1]
        land = _gather_wait(got.pop("handle"), after, name=f"gather_w{k}_wait")
        land = _fill_sibling(land, name=f"gather_w{k}_fill")
        allk = lax.dynamic_update_slice(land, shard[None], (chip, 0, 0))
        if k == 1:
            got["handle"], token2 = _gather_start(shard2, land, name="gather_w2_start")
            w_q, w_o, w_kv, w_fi, w_fo = _unpack_rows(allk, [(Dq, Dm), (Dq, Dm), (Dq, 2 * KV_DIM), (Dm, FC), (Fo, Dm)])
            got.update(ffn_weights(w_fi, w_fo, 0))
            return {"sw_q": w_q.reshape(Dm, Dm), "sw_out": w_o.reshape(Dm, Dm), "kv": w_kv.reshape(Dm, 2 * KV_DIM),
                    "token": token2, **{n: got[n] for n in ("ffn_in_a", "ffn_in_b", "ffn_out")}}
        w_fi, w_fo = _unpack_rows(allk, [(Dm, FC), (Fo, Dm)])
        new = ffn_weights(w_fi, w_fo, 1)
        return {n: {**got[n], **new[n]} for n in new}

    got["handle"] = handle1

    w = {
        "hg_in": [w_in[j] for j in range(4)], "hg_out": w_hg_out.reshape(Dm, Dm), "token": token1,
        "lb_logits": lb_full, "gnorm": hgrn_gnorm_w, "sinks": swa_sinks, "rel_bias": rel_bias,
        "conv_w_a": [cw_full[l, :, :FFN_DIM] for l in range(DEPTH)],
        "conv_w_b": [cw_full[l, :, FFN_DIM:] for l in range(DEPTH)],
        "conv_b_a": [ffn_conv_b[l:l + 1, :FFN_DIM] for l in range(DEPTH)],
        "conv_b_b": [ffn_conv_b[l:l + 1, FFN_DIM:] for l in range(DEPTH)],
        "ln_mix_g": [ln_mix_g[l:l + 1] for l in range(DEPTH)], "ln_mix_b": [ln_mix_b[l:l + 1] for l in range(DEPTH)],
        "ln_ffn_g": [ln_ffn_g[l:l + 1] for l in range(DEPTH)], "ln_ffn_b": [ln_ffn_b[l:l + 1] for l in range(DEPTH)],
    }

    sent = {}

    def ffn_pieces(gd):
        return [jnp.concatenate([gd["ffn_in_a"], gd["ffn_in_b"]], axis=0), gd["ffn_out"].reshape(N_CHIPS, Fo, Dm)]

    def emit(k, gd):
        rows4 = lambda a: a.reshape(N_CHIPS, Dq, a.shape[-1])
        if k == 1:
            pieces = [rows4(gd["sw_q"]), rows4(gd["sw_out"]), rows4(gd["kv"])] + ffn_pieces(gd)
        elif k == 2:
            pieces = ffn_pieces(gd) + [rows4(gd["hg_out"])]
        else:
            pieces = [jnp.stack(gd["hg_in"])]
        handle, token = _scatter_start([p.astype(BF16) for p in pieces], name=f"scatter_g{k}_start")
        sent[k] = (handle, pieces)
        return token

    loss_tile, grad_x, g = _local_step(x[0], loss_target[0], w, more_weights, emit)

    wts = dict(hgrn_w_in=hgrn_w_in, hgrn_lb_logits=hgrn_lb_logits, hgrn_gnorm_w=hgrn_gnorm_w, hgrn_w_out=hgrn_w_out,
               swa_w_q=swa_w_q, swa_sinks=swa_sinks, swa_w_out=swa_w_out, shared_w_kv=shared_w_kv, rel_bias=rel_bias,
               ffn_w_in=ffn_w_in, ffn_conv_w=ffn_conv_w, ffn_conv_b=ffn_conv_b, ffn_w_out=ffn_w_out,
               ln_mix_g=ln_mix_g, ln_mix_b=ln_mix_b, ln_ffn_g=ln_ffn_g, ln_ffn_b=ln_ffn_b)
    ms = dict(hgrn_w_in=m_hgrn_w_in, hgrn_lb_logits=m_hgrn_lb_logits, hgrn_gnorm_w=m_hgrn_gnorm_w, hgrn_w_out=m_hgrn_w_out,
              swa_w_q=m_swa_w_q, swa_sinks=m_swa_sinks, swa_w_out=m_swa_w_out, shared_w_kv=m_shared_w_kv, rel_bias=m_rel_bias,
              ffn_w_in=m_ffn_w_in, ffn_conv_w=m_ffn_conv_w, ffn_conv_b=m_ffn_conv_b, ffn_w_out=m_ffn_w_out,
              ln_mix_g=m_ln_mix_g, ln_mix_b=m_ln_mix_b, ln_ffn_g=m_ln_ffn_g, ln_ffn_b=m_ln_ffn_b)
    vs = dict(hgrn_w_in=v_hgrn_w_in, hgrn_lb_logits=v_hgrn_lb_logits, hgrn_gnorm_w=v_hgrn_gnorm_w, hgrn_w_out=v_hgrn_w_out,
              swa_w_q=v_swa_w_q, swa_sinks=v_swa_sinks, swa_w_out=v_swa_w_out, shared_w_kv=v_shared_w_kv, rel_bias=v_rel_bias,
              ffn_w_in=v_ffn_w_in, ffn_conv_w=v_ffn_conv_w, ffn_conv_b=v_ffn_conv_b, ffn_w_out=v_ffn_w_out,
              ln_mix_g=v_ln_mix_g, ln_mix_b=v_ln_mix_b, ln_ffn_g=v_ln_ffn_g, ln_ffn_b=v_ln_ffn_b)
    names = list(wts)
    grads, delta, new_m, new_v = {}, {}, {}, {}

    def update(n, ga, gb, layer=None, prev=None):
        r2 = lambda a: a.reshape(-1, a.shape[-1])
        rows = None if layer is None else (layer * ga.shape[0], ga.shape[0])
        return _adamw(r2(wts[n]), ga, gb, r2(ms[n]), r2(vs[n]), rows=rows, prev=prev,
                      name=f"adamw_{n}" + ("" if layer is None else f"_{layer}"))

    def keep(n, res):
        grads[n], delta[n], new_m[n], new_v[n] = [a.reshape(wts[n].shape) for a in res]

    chip1 = jnp.reshape(chip, (1,)).astype(jnp.int32)
    after = grad_x
    for k in (1, 2, 3):
        handle, pieces = sent[k]
        lands = _scatter_wait(handle, after, name=f"scatter_g{k}_wait")
        parts = [_chip_sum(p, l, chip1, name=f"scatter_g{k}_sum{i}") for i, (p, l) in enumerate(zip(pieces, lands))]
        sibs = _swap_sibling(parts, name=f"scatter_g{k}_swap")
        if k == 1:
            for n, ga, gb in zip(["swa_w_q", "swa_w_out", "shared_w_kv"], parts[:3], sibs[:3]):
                keep(n, update(n, ga, gb))
            ffn_in_1 = update("ffn_w_in", parts[3], sibs[3], layer=1)
            ffn_out_1 = update("ffn_w_out", parts[4], sibs[4], layer=1)
            after = ffn_out_1[3]
        elif k == 2:
            keep("ffn_w_in", update("ffn_w_in", parts[0], sibs[0], layer=0, prev=ffn_in_1))
            keep("ffn_w_out", update("ffn_w_out", parts[1], sibs[1], layer=0, prev=ffn_out_1))
            keep("hgrn_w_out", update("hgrn_w_out", parts[2], sibs[2]))
            after = new_v["hgrn_w_out"]
        else:
            keep("hgrn_w_in", update("hgrn_w_in", parts[0], sibs[0]))

    small_shapes = [(SUBLANES, LANES), (2, Dm), (1, HG_DIM), (1, SW_Q_HEADS), (REL_BUCKETS, SW_Q_HEADS),
                    (DEPTH, 3, 2 * FFN_DIM), (DEPTH, 2 * FFN_DIM)] + [(DEPTH, Dm)] * 4
    gc = g["conv"]
    conv_w_g = jnp.stack([jnp.concatenate([gc[l]["conv_w_a"], gc[l]["conv_w_b"]], axis=1) for l in range(DEPTH)])
    conv_b_g = jnp.concatenate([jnp.concatenate([gc[l]["conv_b_a"], gc[l]["conv_b_b"]], axis=1) for l in range(DEPTH)], axis=0)
    ln_g = [jnp.concatenate([g[f"{n}0"], g[f"{n}1"]], axis=0) for n in ("ln_mix_g", "ln_mix_b", "ln_ffn_g", "ln_ffn_b")]
    small_out = _sum8(_flat128([loss_tile, g["lb_logits"], g["gnorm"], g["sinks"], g["rel_bias"], conv_w_g, conv_b_g] + ln_g),
                      name="sum_small")
    (loss_t, g_lb, g_gn, g_sinks, g_rel, g_cw, g_cb, g_lmg, g_lmb, g_lfg, g_lfb) = _unflat128(small_out, small_shapes)
    loss = loss_t[0, 0]
    g_lb = lax.dynamic_slice_in_dim(g_lb, chip * Dq, Dq, axis=1)
    g_cw = lax.dynamic_slice_in_dim(g_cw, chip * FC, FC, axis=2)
    small_g = dict(hgrn_lb_logits=g_lb, hgrn_gnorm_w=g_gn, swa_sinks=g_sinks, rel_bias=g_rel, ffn_conv_w=g_cw,
                   ffn_conv_b=g_cb, ln_mix_g=g_lmg, ln_mix_b=g_lmb, ln_ffn_g=g_lfg, ln_ffn_b=g_lfb)
    small_names = list(small_g)
    sshapes = [wts[n].shape for n in small_names]
    _, d_, m_, v_ = _adamw(_flat128([wts[n] for n in small_names]), _flat128([small_g[n] for n in small_names]), None,
                           _flat128([ms[n] for n in small_names]), _flat128([vs[n] for n in small_names]), name="adamw_small")
    for n, a, b_, c_ in zip(small_names, _unflat128(d_, sshapes), _unflat128(m_, sshapes), _unflat128(v_, sshapes)):
        grads[n], delta[n], new_m[n], new_v[n] = small_g[n], a, b_, c_

    return (loss, grad_x[None], *[grads[n] for n in names], *[delta[n] for n in names],
            *[new_m[n] for n in names], *[new_v[n] for n in names])
```

```python
import math

import numpy as np
import jax
import jax.numpy as jnp
from jax import lax
from jax.experimental import pallas as pl
from jax.experimental.pallas import tpu as pltpu

F32 = jnp.float32
BF16 = jnp.bfloat16
MESH = pl.DeviceIdType.MESH

D_MODEL = 1024
DEPTH = 2
HG_HEADS = 8
HG_DIM = 128
SW_Q_HEADS = 16
SW_KV_HEADS = 4
SW_HEAD_DIM = 64
SW_GROUP = 4
SW_WINDOW = 128
REL_BUCKETS = 32
REL_MAX_DIST = 128
FFN_DIM = 2816
ALPHA = (2.0 * DEPTH) ** 0.25
LN_EPS = 1e-5
RMS_EPS = 1e-6
ADAM_LR = 0.001
ADAM_B1 = 0.9
ADAM_B2 = 0.999
ADAM_EPS = 1e-08
ADAM_WD = 0.01
ADAM_STEP = 10

VMEM_BYTES_V7X = 64 * 1024 * 1024
VMEM_LIMIT = VMEM_BYTES_V7X - 8 * 1024 * 1024
LANES = 128
SUBLANES = 8

HG_C = 64
HG_RB = 256
ROW_TILE = 256
CONV_R = 128
N_CHIPS = 4
N_DEV = 8

ANY_SPEC = pl.BlockSpec(memory_space=pl.ANY)


def _after(body, n_in, after):
    if after is None:
        return body, [], ()

    def wrapped(*refs):
        return body(*refs[:n_in], *refs[n_in + 1:])

    return wrapped, [ANY_SPEC], (after,)


def _params(sem=None):
    return pltpu.CompilerParams(dimension_semantics=sem, vmem_limit_bytes=VMEM_LIMIT)


def _tile(n, pref, unit=LANES):
    if n <= pref:
        return n
    best = None
    for t in range(unit, pref + 1, unit):
        if n % t == 0:
            best = t
    assert best is not None, (n, pref, unit)
    return best


def _dot(a, b, ca, cb):
    nb = a.ndim - 2
    batch = tuple(range(nb))
    return lax.dot_general(a.astype(BF16), b.astype(BF16), (((nb + ca,), (nb + cb,)), (batch, batch)),
                           preferred_element_type=F32)


@jax.custom_vjp
def mm(a, b):
    return _dot(a, b, 1, 0)


@jax.custom_vjp
def mm_nt(a, b):
    return _dot(a, b, 1, 1)


@jax.custom_vjp
def mm_tn(a, b):
    return _dot(a, b, 0, 0)


mm.defvjp(lambda a, b: (mm(a, b), (a, b)), lambda r, ct: (mm_nt(ct, r[1]), mm_tn(r[0], ct)))
mm_nt.defvjp(lambda a, b: (mm_nt(a, b), (a, b)), lambda r, ct: (mm(ct, r[1]), mm_tn(ct, r[0])))
mm_tn.defvjp(lambda a, b: (mm_tn(a, b), (a, b)), lambda r, ct: (mm_nt(r[1], ct), mm(r[0], ct)))


def _split2(x):
    hi = x.astype(BF16)
    return hi, (x - hi.astype(F32)).astype(BF16)


@jax.custom_vjp
def _scores(qt, kt):
    return _dot(qt, kt, 1, 1)


def _scores_bwd(r, ct):
    (qh, ql), (kh, kl) = _split2(r[0]), _split2(r[1])
    return _dot(ct, kh, 1, 0) + _dot(ct, kl, 1, 0), _dot(ct, qh, 0, 0) + _dot(ct, ql, 0, 0)


_scores.defvjp(lambda a, b: (_scores(a, b), (a, b)), _scores_bwd)


def _split3(x):
    hi = x.astype(BF16)
    r1 = x - hi.astype(F32)
    mid = r1.astype(BF16)
    lo = (r1 - mid.astype(F32)).astype(BF16)
    return hi, mid, lo


def _cumsum_impl(x):
    ax = x.ndim - 2
    n = x.shape[ax]
    row = lax.broadcasted_iota(jnp.int32, x.shape, ax)
    d = 1
    while d < n:
        x = x + jnp.where(row >= d, pltpu.roll(x, d, ax), 0.0)
        d *= 2
    return x


def _cumsum_rev_impl(x):
    ax = x.ndim - 2
    n = x.shape[ax]
    row = lax.broadcasted_iota(jnp.int32, x.shape, ax)
    d = 1
    while d < n:
        x = x + jnp.where(row < n - d, pltpu.roll(x, n - d, ax), 0.0)
        d *= 2
    return x


@jax.custom_vjp
def _cumsum(x):
    return _cumsum_impl(x)


_cumsum.defvjp(lambda x: (_cumsum_impl(x), None), lambda _, ct: (_cumsum_rev_impl(ct),))


def _matmul(a, b, *, mode, name, out_dtype=F32, add=None, add_scale=1.0, tm=512, tn=1408, tk=1408, after=None,
            split_n=False, planes=None):
    P = b.shape[0] if planes else 1
    a2, b2 = a.shape[-2:], b.shape[-2:]
    (M, K) = a2 if mode[0] == "n" else a2[::-1]
    (K2, N) = b2 if mode[1] == "n" else b2[::-1]
    assert K == K2, (a.shape, b.shape, mode)
    assert a.ndim == (3 if planes == "k" else 2) and b.ndim == (3 if planes else 2)
    tm, tn, tk = _tile(M, tm), _tile(N, tn), _tile(K, tk)
    nj, nkp = N // tn, K // tk
    nk = nkp * (P if planes == "k" else 1)
    ca, cb = (1 if mode[0] == "n" else 0), (0 if mode[1] == "n" else 1)
    a_blk, a_idx = ((tk, tm), lambda i, k: (k, i)) if mode[0] == "t" else ((tm, tk), lambda i, k: (i, k))
    b_blk, b_idx = ((tn, tk), lambda k, j: (j, k)) if mode[1] == "t" else ((tk, tn), lambda k, j: (k, j))
    if planes == "k":
        a_spec = pl.BlockSpec((None,) + a_blk, lambda i, j, k: (k // nkp,) + a_idx(i, k % nkp))
        b_spec = pl.BlockSpec((None,) + b_blk, lambda i, j, k: (k // nkp,) + b_idx(k % nkp, j))
    else:
        a_spec = pl.BlockSpec(a_blk, lambda i, j, k: a_idx(i, k))
        b_spec = (pl.BlockSpec((None,) + b_blk, lambda i, j, k: (j // nj,) + b_idx(k, j % nj)) if planes == "n"
                  else pl.BlockSpec(b_blk, lambda i, j, k: b_idx(k, j)))
    if split_n:
        o_spec, out_shape = pl.BlockSpec((None, tm, tn), lambda i, j, k: (j, i, 0)), (P * nj if planes == "n" else nj, M, tn)
    elif planes == "n":
        o_spec, out_shape = pl.BlockSpec((None, tm, tn), lambda i, j, k: (j // nj, i, j % nj)), (P, M, N)
    else:
        o_spec, out_shape = pl.BlockSpec((tm, tn), lambda i, j, k: (i, j)), (M, N)
    has_add = add is not None
    assert not (has_add and (split_n or planes == "n"))

    def finish(r, add_ref, o_ref):
        if has_add:
            r = r + add_scale * add_ref[...]
        o_ref[...] = r.astype(out_dtype)

    def body(*refs):
        a_ref, b_ref = refs[:2]
        add_ref = refs[2] if has_add else None
        o_ref = refs[3 if has_add else 2]
        if nk == 1:
            finish(_dot(a_ref[...], b_ref[...], ca, cb), add_ref, o_ref)
            return
        acc_ref = refs[-1]
        k = pl.program_id(2)

        @pl.when(k == 0)
        def _():
            acc_ref[...] = jnp.zeros_like(acc_ref)

        acc_ref[...] += _dot(a_ref[...], b_ref[...], ca, cb)

        @pl.when(k == nk - 1)
        def _():
            finish(acc_ref[...], add_ref, o_ref)

    in_specs = [a_spec, b_spec] + ([o_spec] if has_add else [])
    args = (a, b) + ((add,) if has_add else ())
    body, xs, xa = _after(body, len(args), after)
    in_specs, args = in_specs + xs, args + xa
    return pl.pallas_call(
        body, name=name, grid=(M // tm, nj * (P if planes == "n" else 1), nk), in_specs=in_specs, out_specs=o_spec,
        out_shape=jax.ShapeDtypeStruct(out_shape, out_dtype),
        scratch_shapes=[pltpu.VMEM((tm, tn), F32)] if nk > 1 else [],
        compiler_params=_params(("parallel", "parallel", "arbitrary")),
    )(*args)


def _ln(z, g, b):
    mu = jnp.mean(z, axis=-1, keepdims=True)
    zc = z - mu
    var = jnp.mean(zc * zc, axis=-1, keepdims=True)
    return zc * lax.rsqrt(var + LN_EPS) * g + b


def _matmul_ln(a, b, h, g, bias, *, name, tgt=None, tm=512, a_t=False):
    (T, K), (K2, Dm) = (a.shape[::-1] if a_t else a.shape), b.shape
    assert K == K2 and h.shape == (T, Dm)
    tm = _tile(T, tm, SUBLANES)
    last = tgt is not None

    def body(*refs):
        a_ref, b_ref, h_ref, g_ref, bias_ref = refs[:5]
        z = ALPHA * h_ref[...] + _dot(a_ref[...], b_ref[...], 0 if a_t else 1, 0)
        if not last:
            z_ref, y_ref, yb_ref = refs[5:]
            y = _ln(z, g_ref[...], bias_ref[...])
            z_ref[...] = z
            y_ref[...] = y
            yb_ref[...] = y.astype(BF16)
            return
        t_ref, dz_ref, dzb_ref, dg_ref, db_ref, l_ref = refs[5:]

        @pl.when(pl.program_id(0) == 0)
        def _():
            dg_ref[...] = jnp.zeros_like(dg_ref)
            db_ref[...] = jnp.zeros_like(db_ref)
            l_ref[...] = jnp.zeros_like(l_ref)

        y, vjp = jax.vjp(_ln, z, g_ref[...], bias_ref[...])
        e = y - t_ref[...]
        dz, dg, db = vjp(e * (1.0 / Dm))
        l_ref[...] += 0.5 * jnp.sum(jnp.mean(e * e, axis=-1, keepdims=True), axis=0, keepdims=True)
        dz_ref[...] = dz
        dzb_ref[...] = dz.astype(BF16)
        dg_ref[...] += dg
        db_ref[...] += db

    row = pl.BlockSpec((tm, Dm), lambda i: (i, 0))
    vec = pl.BlockSpec((1, Dm), lambda i: (0, 0))
    a_spec = pl.BlockSpec((K, tm), lambda i: (0, i)) if a_t else pl.BlockSpec((tm, K), lambda i: (i, 0))
    in_specs = [a_spec, pl.BlockSpec((K, Dm), lambda i: (0, 0)), row, vec, vec]
    f32, b16 = jax.ShapeDtypeStruct((T, Dm), F32), jax.ShapeDtypeStruct((T, Dm), BF16)
    if not last:
        return pl.pallas_call(
            body, name=name, grid=(T // tm,), in_specs=in_specs, out_specs=[row, row, row], out_shape=[f32, f32, b16],
            compiler_params=_params(("parallel",)),
        )(a, b, h, g, bias)
    return pl.pallas_call(
        body, name=name, grid=(T // tm,), in_specs=in_specs + [row],
        out_specs=[row, row, vec, vec, pl.BlockSpec((SUBLANES, LANES), lambda i: (0, 0))],
        out_shape=[f32, b16, jax.ShapeDtypeStruct((1, Dm), F32), jax.ShapeDtypeStruct((1, Dm), F32),
                   jax.ShapeDtypeStruct((SUBLANES, LANES), F32)],
        compiler_params=_params(("arbitrary",)),
    )(a, b, h, g, bias, tgt)


def _ln_bwd(dy, z, g, b, *, name, after=None):
    T, Dm = z.shape
    tr = _tile(T, ROW_TILE, SUBLANES)

    def body(dy_ref, z_ref, g_ref, b_ref, dz_ref, dzb_ref, dg_ref, db_ref):
        @pl.when(pl.program_id(0) == 0)
        def _():
            dg_ref[...] = jnp.zeros_like(dg_ref)
            db_ref[...] = jnp.zeros_like(db_ref)

        _, vjp = jax.vjp(_ln, z_ref[...], g_ref[...], b_ref[...])
        dz, dg, db = vjp(dy_ref[...])
        dz_ref[...] = dz
        dzb_ref[...] = dz.astype(BF16)
        dg_ref[...] += dg
        db_ref[...] += db

    row = pl.BlockSpec((tr, Dm), lambda i: (i, 0))
    vec = pl.BlockSpec((1, Dm), lambda i: (0, 0))
    body, xs, xa = _after(body, 4, after)
    return pl.pallas_call(
        body, name=name, grid=(T // tr,), in_specs=[row, row, vec, vec] + xs,
        out_specs=[row, row, vec, vec],
        out_shape=[jax.ShapeDtypeStruct((T, Dm), F32), jax.ShapeDtypeStruct((T, Dm), BF16),
                   jax.ShapeDtypeStruct((1, Dm), F32), jax.ShapeDtypeStruct((1, Dm), F32)],
        compiler_params=_params(("arbitrary",)),
    )(dy, z, g, b, *xa)


def _hg_chunk(qr, fr, ir, gr, l0, l1, gw, st):
    C = qr.shape[-2]
    row = lax.broadcasted_iota(jnp.int32, qr.shape, qr.ndim - 2)
    lb = jax.nn.sigmoid(l0 - l1)
    fg = lb + (1.0 - lb) * jax.nn.sigmoid(fr)
    b = _cumsum(jnp.log(fg))
    q = jax.nn.silu(qr)
    k = 1.0 - fg
    bmid = lax.stop_gradient(jnp.sum(jnp.where(row == C // 2 - 1, b, 0.0), axis=-2, keepdims=True))
    bl = jnp.sum(jnp.where(row == C - 1, b, 0.0), axis=-2, keepdims=True)
    o = mm_nt(q * jnp.exp(b), st)
    sc = _scores(q * jnp.exp(b - bmid), k * jnp.exp(bmid - b))
    ti = lax.broadcasted_iota(jnp.int32, (C, C), 0)
    si = lax.broadcasted_iota(jnp.int32, (C, C), 1)
    sc = jnp.where(si <= ti, sc, 0.0)
    o = o + mm(sc, ir)
    st_new = st * jnp.exp(bl) + mm_tn(ir, k * jnp.exp(bl - b))
    on = o * lax.rsqrt(jnp.mean(o * o, axis=-1, keepdims=True) + RMS_EPS)
    return on * gw * jax.nn.silu(gr), st_new


def _heads(ref, rows):
    return jnp.stack([ref[rows, h * HG_DIM:(h + 1) * HG_DIM].astype(F32) for h in range(HG_HEADS)])


def _unheads(x):
    return jnp.concatenate([x[h] for h in range(HG_HEADS)], axis=-1)


def _hgrn_fwd(pre, lbl, gw, *, name):
    _, T, Dm = pre.shape
    rb = min(HG_RB, T)
    C = min(HG_C, rb)
    ncb = rb // C

    def body(pre_ref, lbl_ref, gw_ref, o_ref, st_ref, s_ref):
        @pl.when(pl.program_id(0) == 0)
        def _():
            s_ref[...] = jnp.zeros_like(s_ref)

        def chunk(ci, carry):
            r0 = pl.multiple_of(ci * C, C)
            rows = pl.ds(r0, C)
            st = s_ref[...]
            st_ref[ci] = st
            out, st_new = _hg_chunk(*[_heads(pre_ref.at[j], rows) for j in range(4)],
                                    _heads(lbl_ref, slice(0, 1)), _heads(lbl_ref, slice(1, 2)), gw_ref[...], st)
            o_ref[rows, :] = _unheads(out).astype(BF16)
            s_ref[...] = st_new
            return carry

        lax.fori_loop(0, ncb, chunk, 0, unroll=True)

    row = pl.BlockSpec((rb, Dm), lambda n: (n, 0))
    return pl.pallas_call(
        body, name=name, grid=(T // rb,),
        in_specs=[pl.BlockSpec((4, rb, Dm), lambda n: (0, n, 0)), pl.BlockSpec((2, Dm), lambda n: (0, 0)),
                  pl.BlockSpec((1, HG_DIM), lambda n: (0, 0))],
        out_specs=[row, pl.BlockSpec((ncb, HG_HEADS, HG_DIM, HG_DIM), lambda n: (n, 0, 0, 0))],
        out_shape=[jax.ShapeDtypeStruct((T, Dm), BF16),
                   jax.ShapeDtypeStruct((T // C, HG_HEADS, HG_DIM, HG_DIM), F32)],
        scratch_shapes=[pltpu.VMEM((HG_HEADS, HG_DIM, HG_DIM), F32)],
        compiler_params=_params(("arbitrary",)),
    )(pre, lbl, gw)


def _hgrn_bwd(pre, lbl, gw, states, dout, *, name, after=None):
    _, T, Dm = pre.shape
    rb = min(HG_RB, T)
    C = min(HG_C, rb)
    ncb = rb // C
    nb = T // rb

    def body(pre_ref, lbl_ref, gw_ref, st_ref, do_ref, dpre_ref, dlbl_ref, dgw_ref, ds_ref):
        @pl.when(pl.program_id(0) == 0)
        def _():
            ds_ref[...] = jnp.zeros_like(ds_ref)
            dlbl_ref[...] = jnp.zeros_like(dlbl_ref)
            dgw_ref[...] = jnp.zeros_like(dgw_ref)

        def chunk(cj, carry):
            ci = ncb - 1 - cj
            r0 = pl.multiple_of(ci * C, C)
            rows = pl.ds(r0, C)
            _, vjp = jax.vjp(_hg_chunk, *[_heads(pre_ref.at[j], rows) for j in range(4)],
                             _heads(lbl_ref, slice(0, 1)), _heads(lbl_ref, slice(1, 2)), gw_ref[...], st_ref[ci])
            *dpre, dl0, dl1, dgw, dst = vjp((_heads(do_ref, rows), ds_ref[...]))
            for j in range(4):
                dpre_ref[j, rows, :] = _unheads(dpre[j]).astype(BF16)
            dlbl_ref[0:1, :] += _unheads(dl0)
            dlbl_ref[1:2, :] += _unheads(dl1)
            dgw_ref[...] += dgw
            ds_ref[...] = dst
            return carry

        lax.fori_loop(0, ncb, chunk, 0, unroll=True)

    row = pl.BlockSpec((rb, Dm), lambda n: (nb - 1 - n, 0))
    lsp = pl.BlockSpec((2, Dm), lambda n: (0, 0))
    gsp = pl.BlockSpec((1, HG_DIM), lambda n: (0, 0))
    pre_spec = pl.BlockSpec((4, rb, Dm), lambda n: (0, nb - 1 - n, 0))
    body, xs, xa = _after(body, 5, after)
    return pl.pallas_call(
        body, name=name, grid=(nb,),
        in_specs=[pre_spec, lsp, gsp, pl.BlockSpec((ncb, HG_HEADS, HG_DIM, HG_DIM), lambda n: (nb - 1 - n, 0, 0, 0)), row] + xs,
        out_specs=[pre_spec, lsp, gsp],
        out_shape=[jax.ShapeDtypeStruct((4, T, Dm), BF16), jax.ShapeDtypeStruct((2, Dm), F32),
                   jax.ShapeDtypeStruct((1, HG_DIM), F32)],
        scratch_shapes=[pltpu.VMEM((HG_HEADS, HG_DIM, HG_DIM), F32)],
        compiler_params=_params(("arbitrary",)),
    )(pre, lbl, gw, states, dout, *xa)


CONV_HALO = 2 * SUBLANES


def _conv_rows(u_ref, scr, w, bias, r0, R):
    cur = u_ref[pl.ds(r0, R), :].astype(F32)
    p0 = pl.multiple_of(jnp.maximum(r0 - CONV_HALO, 0), CONV_HALO)
    scr[0:CONV_HALO, :] = jnp.where(r0 > 0, u_ref[pl.ds(p0, CONV_HALO), :].astype(F32), 0.0)
    scr[CONV_HALO:CONV_HALO + R, :] = cur
    s1 = scr[CONV_HALO - 1:CONV_HALO - 1 + R, :]
    s2 = scr[CONV_HALO - 2:CONV_HALO - 2 + R, :]
    return w[0:1, :] * s2 + w[1:2, :] * s1 + w[2:3, :] * cur + bias, cur, s1, s2


def _conv_gate_fwd(u, wa, wb, ba, bb, *, name):
    _, T, Fd = u.shape
    R = min(CONV_R, T)
    tc = LANES

    def body(u_ref, wa_ref, wb_ref, ba_ref, bb_ref, o_ref, sa, sb):
        wa_, wb_, ba_, bb_ = wa_ref[...], wb_ref[...], ba_ref[...], bb_ref[...]

        def step(ri, carry):
            r0 = pl.multiple_of(ri * R, R)
            ca = _conv_rows(u_ref.at[0], sa, wa_, ba_, r0, R)[0]
            cb = _conv_rows(u_ref.at[1], sb, wb_, bb_, r0, R)[0]
            o_ref[pl.ds(r0, R), :] = (jax.nn.silu(ca) * cb).astype(BF16)
            return carry

        lax.fori_loop(0, T // R, step, 0)

    col = pl.BlockSpec((T, tc), lambda j: (0, j))
    wsp = pl.BlockSpec((3, tc), lambda j: (0, j))
    bsp = pl.BlockSpec((1, tc), lambda j: (0, j))
    both = pl.BlockSpec((2, T, tc), lambda j: (0, 0, j))
    return pl.pallas_call(
        body, name=name, grid=(Fd // tc,), in_specs=[both, wsp, wsp, bsp, bsp], out_specs=col,
        out_shape=jax.ShapeDtypeStruct((T, Fd), BF16),
        scratch_shapes=[pltpu.VMEM((CONV_HALO + R, tc), F32)] * 2,
        compiler_params=_params(("parallel",)),
    )(u, wa, wb, ba, bb)


def _conv_gate_bwd(u, wa, wb, ba, bb, dact, *, name):
    _, T, Fd = u.shape
    R = min(CONV_R, T)
    nr = T // R
    tc = LANES

    def body(u_ref, wa_ref, wb_ref, ba_ref, bb_ref, da_ref,
             du_ref, dwa_ref, dwb_ref, dba_ref, dbb_ref, sa, sb, sda, sdb):
        wa_, wb_, ba_, bb_ = wa_ref[...], wb_ref[...], ba_ref[...], bb_ref[...]
        sda[R:R + SUBLANES, :] = jnp.zeros((SUBLANES, tc), F32)
        sdb[R:R + SUBLANES, :] = jnp.zeros((SUBLANES, tc), F32)

        def taps(dc, cur, s1, s2):
            return jnp.concatenate([jnp.sum(dc * s2, axis=0, keepdims=True), jnp.sum(dc * s1, axis=0, keepdims=True),
                                    jnp.sum(dc * cur, axis=0, keepdims=True)], axis=0)

        def du_rows(sd, dc, w):
            sd[0:R, :] = dc
            du = w[2:3, :] * dc + w[1:2, :] * sd[1:1 + R, :] + w[0:1, :] * sd[2:2 + R, :]
            sd[R:R + SUBLANES, :] = dc[0:SUBLANES]
            return du

        def step(rj, carry):
            dwa, dwb, dba, dbb = carry
            r0 = pl.multiple_of((nr - 1 - rj) * R, R)
            ca, cura, s1a, s2a = _conv_rows(u_ref.at[0], sa, wa_, ba_, r0, R)
            cb, curb, s1b, s2b = _conv_rows(u_ref.at[1], sb, wb_, bb_, r0, R)
            dact_ = da_ref[pl.ds(r0, R), :].astype(F32)
            sg = jax.nn.sigmoid(ca)
            dca = dact_ * cb * (sg * (1.0 + ca * (1.0 - sg)))
            dcb = dact_ * (ca * sg)
            du_ref[0, pl.ds(r0, R), :] = du_rows(sda, dca, wa_).astype(BF16)
            du_ref[1, pl.ds(r0, R), :] = du_rows(sdb, dcb, wb_).astype(BF16)
            return (dwa + taps(dca, cura, s1a, s2a), dwb + taps(dcb, curb, s1b, s2b),
                    dba + jnp.sum(dca, axis=0, keepdims=True), dbb + jnp.sum(dcb, axis=0, keepdims=True))

        z3 = jnp.zeros((3, tc), F32)
        z1 = jnp.zeros((1, tc), F32)
        dwa, dwb, dba, dbb = lax.fori_loop(0, nr, step, (z3, z3, z1, z1))
        dwa_ref[...] = dwa
        dwb_ref[...] = dwb
        dba_ref[...] = dba
        dbb_ref[...] = dbb

    col = pl.BlockSpec((T, tc), lambda j: (0, j))
    wsp = pl.BlockSpec((3, tc), lambda j: (0, j))
    bsp = pl.BlockSpec((1, tc), lambda j: (0, j))
    both = pl.BlockSpec((2, T, tc), lambda j: (0, 0, j))
    return pl.pallas_call(
        body, name=name, grid=(Fd // tc,), in_specs=[both, wsp, wsp, bsp, bsp, col],
        out_specs=[both, wsp, wsp, bsp, bsp],
        out_shape=[jax.ShapeDtypeStruct((2, T, Fd), BF16)] + [jax.ShapeDtypeStruct((3, Fd), F32)] * 2
        + [jax.ShapeDtypeStruct((1, Fd), F32)] * 2,
        scratch_shapes=[pltpu.VMEM((CONV_HALO + R, tc), F32)] * 2 + [pltpu.VMEM((R + SUBLANES, tc), F32)] * 2,
        compiler_params=_params(("parallel",)),
    )(u, wa, wb, ba, bb, dact)


def _bucket_index():
    t = np.arange(SW_WINDOW)[None, :] + SW_WINDOW
    s = np.arange(2 * SW_WINDOW)[:, None]
    dist = np.maximum(t - s, 0)
    exact = REL_BUCKETS // 2
    d = np.maximum(dist, 1).astype(np.float32)
    log_b = exact + (np.log(d / np.float32(exact)) / np.float32(math.log(REL_MAX_DIST / exact))
                     * np.float32(REL_BUCKETS - exact)).astype(np.int32)
    bucket = np.where(dist < exact, dist, np.minimum(log_b, REL_BUCKETS - 1))
    return bucket.astype(np.int32).reshape(1, -1)


BIAS_COLS = SW_WINDOW * 2 * SW_WINDOW
BIAS_TILE = 4096


def _bias_from_table(table, bucket, *, name):
    def body(t_ref, idx_ref, o_ref):
        onehot = (lax.broadcasted_iota(jnp.int32, (REL_BUCKETS, BIAS_TILE), 0) == idx_ref[...]).astype(BF16)
        acc = jnp.zeros((SW_Q_HEADS, BIAS_TILE), F32)
        for piece in _split3(t_ref[...]):
            acc = acc + lax.dot_general(piece, onehot, (((0,), (0,)), ((), ())), preferred_element_type=F32)
        o_ref[...] = acc

    return pl.pallas_call(
        body, name=name, grid=(BIAS_COLS // BIAS_TILE,),
        in_specs=[pl.BlockSpec((REL_BUCKETS, SW_Q_HEADS), lambda j: (0, 0)), pl.BlockSpec((1, BIAS_TILE), lambda j: (0, j))],
        out_specs=pl.BlockSpec((SW_Q_HEADS, BIAS_TILE), lambda j: (0, j)),
        out_shape=jax.ShapeDtypeStruct((SW_Q_HEADS, BIAS_COLS), F32),
        compiler_params=_params(("parallel",)),
    )(table, bucket)


def _table_grad(dbias, bucket, *, name):
    def body(d_ref, idx_ref, o_ref):
        @pl.when(pl.program_id(0) == 0)
        def _():
            o_ref[...] = jnp.zeros_like(o_ref)

        onehot = (lax.broadcasted_iota(jnp.int32, (REL_BUCKETS, BIAS_TILE), 0) == idx_ref[...]).astype(BF16)
        acc = jnp.zeros((REL_BUCKETS, SW_Q_HEADS), F32)
        for piece in _split3(d_ref[...]):
            acc = acc + lax.dot_general(onehot, piece, (((1,), (1,)), ((), ())), preferred_element_type=F32)
        o_ref[...] += acc

    return pl.pallas_call(
        body, name=name, grid=(BIAS_COLS // BIAS_TILE,),
        in_specs=[pl.BlockSpec((SW_Q_HEADS, BIAS_TILE), lambda j: (0, j)), pl.BlockSpec((1, BIAS_TILE), lambda j: (0, j))],
        out_specs=pl.BlockSpec((REL_BUCKETS, SW_Q_HEADS), lambda j: (0, 0)),
        out_shape=jax.ShapeDtypeStruct((REL_BUCKETS, SW_Q_HEADS), F32),
        compiler_params=_params(("arbitrary",)),
    )(dbias, bucket)


KV_DIM = SW_KV_HEADS * SW_HEAD_DIM
GROUP_ROWS = SW_GROUP * SW_HEAD_DIM
GROUP_LANES = SW_GROUP * SW_WINDOW


def _band_mask(n):
    s = lax.broadcasted_iota(jnp.int32, (2 * SW_WINDOW, GROUP_LANES), 0)
    t = (lax.broadcasted_iota(jnp.int32, (2 * SW_WINDOW, GROUP_LANES), 1) & (SW_WINDOW - 1)) + SW_WINDOW
    dist = t - s
    return (dist >= 0) & (dist < SW_WINDOW) & ((n > 0) | (s >= SW_WINDOW))


def _side_by_side(x_ref, g):
    r0 = g * GROUP_ROWS
    return jnp.concatenate([x_ref[r0 + r * SW_HEAD_DIM:r0 + (r + 1) * SW_HEAD_DIM, :] for r in range(SW_GROUP)], axis=1)


def _group_inputs(bias_ref, sink_ref, g):
    heads = range(g * SW_GROUP, (g + 1) * SW_GROUP)
    bias = jnp.concatenate([bias_ref[h] for h in heads], axis=1)
    sink = jnp.concatenate([jnp.broadcast_to(sink_ref[:, h:h + 1], (1, SW_WINDOW)) for h in heads], axis=1)
    return heads, bias, sink


def _kv_pair(kvp_ref, kvc_ref, g):
    ks = slice(g * SW_HEAD_DIM, (g + 1) * SW_HEAD_DIM)
    vs = slice(KV_DIM + g * SW_HEAD_DIM, KV_DIM + (g + 1) * SW_HEAD_DIM)
    kk = jnp.concatenate([kvp_ref[:, ks], kvc_ref[:, ks]], axis=0)
    vv = jnp.concatenate([kvp_ref[:, vs], kvc_ref[:, vs]], axis=0)
    return kk, vv, ks, vs


def _col_max(x):
    return jnp.max(x, axis=0, keepdims=True)


def _col_sum(x):
    return jnp.sum(x, axis=0, keepdims=True)


def _attn_fwd(qt, kv, bias, sinks, *, name):
    Dm, T = qt.shape
    W = SW_WINDOW

    def body(q_ref, kvc_ref, kvp_ref, bias_ref, sink_ref, o_ref):
        mask = _band_mask(pl.program_id(0))
        G = range(SW_KV_HEADS)
        ins = [_group_inputs(bias_ref, sink_ref, g) for g in G]
        kvs = [_kv_pair(kvp_ref, kvc_ref, g) for g in G]
        q = [_side_by_side(q_ref, g) for g in G]
        lg = [jnp.where(mask, mm(kvs[g][0], q[g]) * (SW_HEAD_DIM ** -0.5) + ins[g][1], -jnp.inf) for g in G]
        m = [jnp.maximum(_col_max(lg[g]), ins[g][2]) for g in G]
        p = [jnp.exp(lg[g] - m[g]) for g in G]
        den = [_col_sum(p[g]) + jnp.exp(ins[g][2] - m[g]) for g in G]
        o = [mm_tn(kvs[g][1], p[g]) / den[g] for g in G]
        for g in G:
            for r in range(SW_GROUP):
                o_ref[g * GROUP_ROWS + r * SW_HEAD_DIM:g * GROUP_ROWS + (r + 1) * SW_HEAD_DIM, :] = (
                    o[g][:, r * W:(r + 1) * W].astype(BF16))

    return pl.pallas_call(
        body, name=name, grid=(T // W,),
        in_specs=[pl.BlockSpec((Dm, W), lambda n: (0, n)),
                  pl.BlockSpec((W, 2 * KV_DIM), lambda n: (n, 0)),
                  pl.BlockSpec((W, 2 * KV_DIM), lambda n: (jnp.maximum(n - 1, 0), 0)),
                  pl.BlockSpec((SW_Q_HEADS, 2 * W, W), lambda n: (0, 0, 0)),
                  pl.BlockSpec((1, SW_Q_HEADS), lambda n: (0, 0))],
        out_specs=pl.BlockSpec((Dm, W), lambda n: (0, n)),
        out_shape=jax.ShapeDtypeStruct((Dm, T), BF16),
        compiler_params=_params(("parallel",)),
    )(qt, kv, kv, bias, sinks)


def _attn_bwd(qt, kv, bias, sinks, dot, *, name):
    Dm, T = qt.shape
    W = SW_WINDOW
    nb = T // W

    def body(q_ref, kvc_ref, kvp_ref, bias_ref, sink_ref, do_ref,
             dq_ref, dkv_ref, dbias_ref, dsink_ref, carry_ref):
        @pl.when(pl.program_id(0) == 0)
        def _():
            carry_ref[...] = jnp.zeros_like(carry_ref)
            dbias_ref[...] = jnp.zeros_like(dbias_ref)
            dsink_ref[...] = jnp.zeros_like(dsink_ref)

        n = nb - 1 - pl.program_id(0)
        mask = _band_mask(n)
        lane = lax.broadcasted_iota(jnp.int32, (1, SW_Q_HEADS), 1)
        sc = SW_HEAD_DIM ** -0.5
        G = range(SW_KV_HEADS)
        ins = [_group_inputs(bias_ref, sink_ref, g) for g in G]
        kvs = [_kv_pair(kvp_ref, kvc_ref, g) for g in G]
        q = [_side_by_side(q_ref, g) for g in G]
        do = [_side_by_side(do_ref, g) for g in G]
        lg = [jnp.where(mask, mm(kvs[g][0], q[g]) * sc + ins[g][1], -jnp.inf) for g in G]
        m = [jnp.maximum(_col_max(lg[g]), ins[g][2]) for g in G]
        p = [jnp.exp(lg[g] - m[g]) for g in G]
        ps = [jnp.exp(ins[g][2] - m[g]) for g in G]
        rden = [1.0 / (_col_sum(p[g]) + ps[g]) for g in G]
        pn = [p[g] * rden[g] for g in G]
        dpn = [mm(kvs[g][1], do[g]) for g in G]
        delta = [_col_sum(pn[g] * dpn[g]) for g in G]
        ds = [pn[g] * (dpn[g] - delta[g]) for g in G]
        dsr = [-(ps[g] * rden[g]) * delta[g] for g in G]
        dq = [mm_tn(kvs[g][0], ds[g]) * sc for g in G]
        dkk = [mm_nt(ds[g], q[g]) * sc for g in G]
        dvv = [mm_nt(pn[g], do[g]) for g in G]
        dsink = jnp.zeros((1, SW_Q_HEADS), F32)
        for g in G:
            _, _, ks, vs = kvs[g]
            for r, h in enumerate(ins[g][0]):
                cols = slice(r * W, (r + 1) * W)
                dbias_ref[h] += ds[g][:, cols]
                dq_ref[g * GROUP_ROWS + r * SW_HEAD_DIM:g * GROUP_ROWS + (r + 1) * SW_HEAD_DIM, :] = dq[g][:, cols].astype(BF16)
                dsink = dsink + jnp.where(lane == h, jnp.sum(dsr[g][:, cols], axis=1, keepdims=True), 0.0)
            dkv_ref[:, ks] = (carry_ref[:, ks] + dkk[g][W:]).astype(BF16)
            dkv_ref[:, vs] = (carry_ref[:, vs] + dvv[g][W:]).astype(BF16)
            carry_ref[:, ks] = dkk[g][:W]
            carry_ref[:, vs] = dvv[g][:W]
        dsink_ref[...] += dsink

    rev = lambda n: (nb - 1 - n, 0)
    revt = lambda n: (0, nb - 1 - n)
    return pl.pallas_call(
        body, name=name, grid=(nb,),
        in_specs=[pl.BlockSpec((Dm, W), revt),
                  pl.BlockSpec((W, 2 * KV_DIM), rev),
                  pl.BlockSpec((W, 2 * KV_DIM), lambda n: (jnp.maximum(nb - 2 - n, 0), 0)),
                  pl.BlockSpec((SW_Q_HEADS, 2 * W, W), lambda n: (0, 0, 0)),
                  pl.BlockSpec((1, SW_Q_HEADS), lambda n: (0, 0)),
                  pl.BlockSpec((Dm, W), revt)],
        out_specs=[pl.BlockSpec((Dm, W), revt), pl.BlockSpec((W, 2 * KV_DIM), rev),
                   pl.BlockSpec((SW_Q_HEADS, 2 * W, W), lambda n: (0, 0, 0)),
                   pl.BlockSpec((1, SW_Q_HEADS), lambda n: (0, 0))],
        out_shape=[jax.ShapeDtypeStruct((Dm, T), BF16), jax.ShapeDtypeStruct((T, 2 * KV_DIM), BF16),
                   jax.ShapeDtypeStruct((SW_Q_HEADS, 2 * W, W), F32), jax.ShapeDtypeStruct((1, SW_Q_HEADS), F32)],
        scratch_shapes=[pltpu.VMEM((W, 2 * KV_DIM), F32)],
        compiler_params=_params(("arbitrary",)),
    )(qt, kv, kv, bias, sinks, dot)


def _ffn_fwd(hb, w, l, after=None):
    u = _matmul(hb, w["ffn_in"][l], mode="nn", planes="n", out_dtype=BF16, name=f"ffn{l}_up", tm=1024, after=after)
    act = _conv_gate_fwd(u, w["conv_w_a"][l], w["conv_w_b"][l], w["conv_b_a"][l], w["conv_b_b"][l],
                         name=f"ffn{l}_conv_gate")
    return u, act


def _ffn_bwd(dffb, dh_scaled, hb, u, act, w, l):
    dact = _matmul(dffb, w["ffn_out"][l], mode="nt", out_dtype=BF16, name=f"ffn{l}_down_dx", tm=1024)
    g_out = _matmul(act, dffb, mode="tn", name=f"ffn{l}_down_dw", tm=1408, tn=1024, tk=1024)
    du, dwa, dwb, dba, dbb = _conv_gate_bwd(u, w["conv_w_a"][l], w["conv_w_b"][l], w["conv_b_a"][l], w["conv_b_b"][l],
                                            dact, name=f"ffn{l}_conv_gate_bwd")
    dh = _matmul(du, w["ffn_in"][l], mode="nt", planes="k", add=dh_scaled, add_scale=ALPHA, name=f"ffn{l}_up_dx",
                 tn=1024, tk=FFN_DIM)
    g_in = _matmul(hb, du, mode="tn", planes="n", name=f"ffn{l}_up_dw", tm=1024, tn=FFN_DIM // 2, tk=1024, split_n=True)
    return dh, dict(ffn_out=g_out, ffn_in=g_in, conv_w_a=dwa, conv_w_b=dwb, conv_b_a=dba, conv_b_b=dbb)


def _local_step(x, tgt, w, more_weights, emit):
    bucket = jnp.asarray(_bucket_index())
    xb = x.astype(BF16)

    pre = _matmul(xb, w["hg_in"], mode="nn", planes="n", out_dtype=BF16, name="hg_in", tm=1024, tn=1024,
                  after=w.get("token"))
    og, states = _hgrn_fwd(pre, w["lb_logits"], w["gnorm"], name="hgrn_fwd")
    z1, h1, h1b = _matmul_ln(og, w["hg_out"], x, w["ln_mix_g"][0], w["ln_mix_b"][0], name="hg_out_ln")
    w = {**w, **more_weights(1, h1b)}
    u0, act0 = _ffn_fwd(h1b, w, 0, after=w.get("token"))
    z2, h2, h2b = _matmul_ln(act0, w["ffn_out"][0], h1, w["ln_ffn_g"][0], w["ln_ffn_b"][0], name="ffn0_down_ln")
    kv = _matmul(h2b, w["kv"], mode="nn", out_dtype=BF16, name="kv_proj")

    bias = _bias_from_table(w["rel_bias"], bucket, name="rel_bias_expand").reshape(SW_Q_HEADS, 2 * SW_WINDOW, SW_WINDOW)
    q1 = _matmul(w["sw_q"], h2b, mode="tt", out_dtype=BF16, name="sw_q", tm=1024, tn=1024)
    o1 = _attn_fwd(q1, kv, bias, w["sinks"], name="attn_fwd")
    z3, h3, h3b = _matmul_ln(o1, w["sw_out"], h2, w["ln_mix_g"][1], w["ln_mix_b"][1], a_t=True, name="sw_out_ln")
    w = {**w, **more_weights(2, h3b)}
    u1, act1 = _ffn_fwd(h3b, w, 1)

    g = {}
    dz, dzb, dg_, db_, loss_tile = _matmul_ln(act1, w["ffn_out"][1], h3, w["ln_ffn_g"][1], w["ln_ffn_b"][1], tgt=tgt,
                                              name="ffn1_down_ln_loss")

    g["ln_ffn_g1"], g["ln_ffn_b1"] = dg_, db_
    dh3, gf1 = _ffn_bwd(dzb, dz, h3b, u1, act1, w, 1)
    dz, dzb, dg_, db_ = _ln_bwd(dh3, z3, w["ln_mix_g"][1], w["ln_mix_b"][1], name="ln_mix1_bwd")
    g["ln_mix_g1"], g["ln_mix_b1"] = dg_, db_
    do1 = _matmul(w["sw_out"], dzb, mode="nt", out_dtype=BF16, name="sw_out_dx", tm=1024, tn=1024)
    g_sw_out = _matmul(o1, dzb, mode="nn", name="sw_out_dw", tm=1024, tn=1024, tk=1024)
    dq1, dkv, dbias, dsinks = _attn_bwd(q1, kv, bias, w["sinks"], do1, name="attn_bwd")
    g["sinks"] = dsinks
    g["rel_bias"] = _table_grad(dbias.reshape(SW_Q_HEADS, BIAS_COLS), bucket, name="rel_bias_grad")
    dh2 = _matmul(dq1, w["sw_q"], mode="tt", add=dz, add_scale=ALPHA, name="sw_q_dx", tn=1024)
    dh2 = _matmul(dkv, w["kv"], mode="nt", add=dh2, name="kv_dx", tn=1024)
    g_sw_q = _matmul(h2b, dq1, mode="tt", name="sw_q_dw", tm=1024, tn=1024, tk=1024)
    g_kv = _matmul(h2b, dkv, mode="tn", name="kv_dw", tm=1024, tn=512, tk=1024)
    tok = emit(1, dict(sw_q=g_sw_q, sw_out=g_sw_out, kv=g_kv, ffn_in=gf1["ffn_in"], ffn_out=gf1["ffn_out"]))

    dz, dzb, dg_, db_ = _ln_bwd(dh2, z2, w["ln_ffn_g"][0], w["ln_ffn_b"][0], name="ln_ffn0_bwd", after=tok)
    g["ln_ffn_g0"], g["ln_ffn_b0"] = dg_, db_
    dh1, gf0 = _ffn_bwd(dzb, dz, h1b, u0, act0, w, 0)
    dz, dzb, dg_, db_ = _ln_bwd(dh1, z1, w["ln_mix_g"][0], w["ln_mix_b"][0], name="ln_mix0_bwd")
    g["ln_mix_g0"], g["ln_mix_b0"] = dg_, db_
    dog = _matmul(dzb, w["hg_out"], mode="nt", out_dtype=BF16, name="hg_out_dx")
    g_hg_out = _matmul(og, dzb, mode="tn", name="hg_out_dw", tm=1024, tn=1024, tk=1024)
    tok = emit(2, dict(hg_out=g_hg_out, ffn_in=gf0["ffn_in"], ffn_out=gf0["ffn_out"]))
    dpre, g["lb_logits"], g["gnorm"] = _hgrn_bwd(pre, w["lb_logits"], w["gnorm"], states, dog, name="hgrn_bwd", after=tok)
    tok = emit(3, dict(hg_in=_matmul(xb, dpre, mode="tn", planes="n", name="hg_in_dw", tm=1024, tn=1024, tk=1024)))
    dx = _matmul(dpre, w["hg_in"], mode="nt", planes="k", add=dz, add_scale=ALPHA, name="hg_in_dx", tn=1024, tk=1024,
                 after=tok)
    g["conv"] = [{k: gf[k] for k in ("conv_w_a", "conv_w_b", "conv_b_a", "conv_b_b")} for gf in (gf0, gf1)]
    return loss_tile, dx, g


def _adamw(wt, ga, gb, m, v, *, name, rows=None, prev=None):
    R, Cc = wt.shape
    r0, n = rows if rows is not None else (0, R)
    tr = _tile(n, 256, SUBLANES) if n % SUBLANES == 0 else n
    assert r0 % tr == 0
    c1 = 1.0 - ADAM_B1 ** ADAM_STEP
    c2 = 1.0 - ADAM_B2 ** ADAM_STEP
    two = gb is not None
    n_in = 5 if two else 4

    def body(*refs):
        if two:
            w_ref, ga_ref, gb_ref, m_ref, v_ref = refs[:5]
            g_ = ga_ref[...] + gb_ref[...]
        else:
            w_ref, ga_ref, m_ref, v_ref = refs[:4]
            g_ = ga_ref[...]
        g_ref, d_ref, nm_ref, nv_ref = refs[-4:]
        nm = ADAM_B1 * m_ref[...] + (1.0 - ADAM_B1) * g_
        nv = ADAM_B2 * v_ref[...] + (1.0 - ADAM_B2) * (g_ * g_)
        g_ref[...] = g_
        d_ref[...] = -ADAM_LR * ((nm / c1) / (jnp.sqrt(nv / c2) + ADAM_EPS) + ADAM_WD * w_ref[...])
        nm_ref[...] = nm
        nv_ref[...] = nv

    full = pl.BlockSpec((tr, Cc), lambda i: (i + r0 // tr, 0))
    part = pl.BlockSpec((tr, Cc), lambda i: (i, 0))
    args = (wt, ga, gb, m, v) if two else (wt, ga, m, v)
    in_specs = [full] + [part] * (n_in - 3) + [full, full]
    aliases = {}
    if prev is not None:
        args, in_specs = args + tuple(prev), in_specs + [ANY_SPEC] * 4
        aliases = {n_in + t: t for t in range(4)}
    return pl.pallas_call(
        body, name=name, grid=(n // tr,), in_specs=in_specs, out_specs=[full] * 4,
        out_shape=[jax.ShapeDtypeStruct((R, Cc), F32)] * 4, input_output_aliases=aliases,
        compiler_params=_params(("parallel",)),
    )(*args)


HBM_SPEC = pl.BlockSpec(memory_space=pltpu.HBM)
SEM_SPEC = pl.BlockSpec(memory_space=pltpu.SEMAPHORE)
VMEM_SPEC = pl.BlockSpec(memory_space=pltpu.VMEM)
DATAFLOW = pltpu.SideEffectType.DATAFLOW_SIDE_EFFECTING


def _in_hbm(a):
    return pltpu.with_memory_space_constraint(a, pltpu.HBM)


def _place():
    return lax.axis_index("x"), lax.axis_index("y"), lax.axis_index("c")


def _other_chips(x, y):
    return [(1 - x, y), (x, 1 - y), (1 - x, 1 - y)]


def _sum8(v, *, name, after=None):
    r = v.shape[0]

    def body(v_ref, all_ref, o_ref, send_sems, recv_sems, local_sem):
        x, y, c = _place()
        me, sibling = (x, y, c), (x, y, 1 - c)
        chips = _other_chips(x, y)

        def rows(px, py, pc):
            return all_ref.at[pl.ds((4 * px + 2 * py + pc) * r, r), :]

        def copy(k, block, to, src=None):
            return pltpu.make_async_remote_copy(
                src_ref=rows(*block) if src is None else src, dst_ref=rows(*block),
                send_sem=send_sems.at[k], recv_sem=recv_sems.at[k], device_id=to, device_id_type=MESH)

        mine = pltpu.make_async_copy(v_ref, rows(*me), local_sem)
        mine.start()
        first = [copy(0, me, sibling, src=v_ref)]
        first += [copy(1 + j, me, (*chip, c), src=v_ref) for j, chip in enumerate(chips)]
        for cp in first:
            cp.start()
        passed = [copy(4 + j, (*chip, c), sibling) for j, chip in enumerate(chips)]
        for j, chip in enumerate(chips):
            copy(1 + j, (*chip, c), me).wait_recv()
            passed[j].start()
        copy(0, sibling, me).wait_recv()
        for j, chip in enumerate(chips):
            copy(4 + j, (*chip, 1 - c), me).wait_recv()
        for cp in first + passed:
            cp.wait_send()
        mine.wait()
        acc = all_ref[pl.ds(0, r), :]
        for d in range(1, N_DEV):
            acc = acc + all_ref[pl.ds(d * r, r), :]
        o_ref[...] = acc

    body, xs, xa = _after(body, 1, after)
    return pl.pallas_call(
        body, name=name, in_specs=[VMEM_SPEC] + xs, out_specs=[VMEM_SPEC, VMEM_SPEC],
        out_shape=[jax.ShapeDtypeStruct((N_DEV * r, LANES), F32), jax.ShapeDtypeStruct((r, LANES), F32)],
        scratch_shapes=[pltpu.SemaphoreType.DMA((7,)), pltpu.SemaphoreType.DMA((7,)), pltpu.SemaphoreType.DMA],
        compiler_params=pltpu.CompilerParams(vmem_limit_bytes=VMEM_LIMIT),
    )(v, *xa)[1]


def _swap_sibling(vs, *, name):
    n = len(vs)

    def body(*refs):
        src, dst, send_sems, recv_sems = refs[:n], refs[n:2 * n], refs[2 * n], refs[2 * n + 1]
        x, y, c = _place()
        cps = [pltpu.make_async_remote_copy(src_ref=src[i], dst_ref=dst[i], send_sem=send_sems.at[i],
                                            recv_sem=recv_sems.at[i], device_id=(x, y, 1 - c), device_id_type=MESH)
               for i in range(n)]
        for cp in cps:
            cp.start()
        for cp in cps:
            cp.wait()

    return pl.pallas_call(
        body, name=name, in_specs=[HBM_SPEC] * n, out_specs=[HBM_SPEC] * n,
        out_shape=[jax.ShapeDtypeStruct(v.shape, v.dtype) for v in vs],
        scratch_shapes=[pltpu.SemaphoreType.DMA((n,)), pltpu.SemaphoreType.DMA((n,))],
    )(*vs)


def _half(ref, j, c, half):
    return ref.at[j, pl.ds(c * half, half), :]


def _gather_start(shard, after, *, name):
    R, Cc = shard.shape
    half = R // 2

    def body(src, land, send, recv, src_out, land_out, token):
        x, y, c = _place()
        for k, (px, py) in enumerate(_other_chips(x, y)):
            pltpu.make_async_remote_copy(src_ref=src.at[pl.ds(c * half, half), :], dst_ref=_half(land, 2 * x + y, c, half),
                                         send_sem=send.at[k], recv_sem=recv.at[k], device_id=(px, py, c),
                                         device_id_type=MESH).start()
        token[...] = jnp.zeros_like(token)

    land = lax.empty((N_CHIPS, R, Cc), shard.dtype)
    body, xs, xa = _after(body, 2, after)
    out = pl.pallas_call(
        body, name=name, in_specs=[HBM_SPEC, HBM_SPEC] + xs,
        out_specs=[SEM_SPEC, SEM_SPEC, HBM_SPEC, HBM_SPEC, VMEM_SPEC],
        out_shape=[pltpu.SemaphoreType.DMA((3,)), pltpu.SemaphoreType.DMA((3,)), pltpu.HBM(shard.shape, shard.dtype),
                   pltpu.HBM(land.shape, land.dtype), jax.ShapeDtypeStruct((SUBLANES, LANES), F32)],
        input_output_aliases={0: 2, 1: 3},
        compiler_params=pltpu.CompilerParams(has_side_effects=DATAFLOW),
    )(_in_hbm(shard), _in_hbm(land), *xa)
    return out[:4], out[4]


def _gather_wait(handle, after, *, name):
    send_sems, recv_sems, src, land = handle
    half = src.shape[0] // 2

    def body(src_ref, land_ref, send_ref, recv_ref, after_ref, src_out, land_out):
        x, y, c = _place()
        for k, (px, py) in enumerate(_other_chips(x, y)):
            cp = pltpu.make_async_remote_copy(src_ref=src_ref.at[pl.ds(c * half, half), :],
                                              dst_ref=_half(land_ref, 2 * px + py, c, half), send_sem=send_ref.at[k],
                                              recv_sem=recv_ref.at[k], device_id=(px, py, c), device_id_type=MESH)
            cp.wait_send()
            cp.wait_recv()

    return pl.pallas_call(
        body, name=name, in_specs=[HBM_SPEC, HBM_SPEC, SEM_SPEC, SEM_SPEC, ANY_SPEC], out_specs=[HBM_SPEC, HBM_SPEC],
        out_shape=[pltpu.HBM(src.shape, src.dtype), pltpu.HBM(land.shape, land.dtype)],
        input_output_aliases={0: 0, 1: 1},
        compiler_params=pltpu.CompilerParams(has_side_effects=DATAFLOW),
    )(src, land, send_sems, recv_sems, after)[1]


def _fill_sibling(land, *, name):
    _, R, Cc = land.shape
    half = R // 2

    def body(in_ref, o_ref, send_sems, recv_sems):
        x, y, c = _place()
        chips = _other_chips(x, y)
        cps = [pltpu.make_async_remote_copy(src_ref=_half(in_ref, 2 * px + py, c, half),
                                            dst_ref=_half(o_ref, 2 * px + py, c, half), send_sem=send_sems.at[k],
                                            recv_sem=recv_sems.at[k], device_id=(x, y, 1 - c), device_id_type=MESH)
               for k, (px, py) in enumerate(chips)]
        for cp in cps:
            cp.start()
        for k, (px, py) in enumerate(chips):
            pltpu.make_async_remote_copy(src_ref=_half(in_ref, 2 * px + py, 1 - c, half),
                                         dst_ref=_half(o_ref, 2 * px + py, 1 - c, half), send_sem=send_sems.at[k],
                                         recv_sem=recv_sems.at[k], device_id=(x, y, 1 - c), device_id_type=MESH).wait_recv()
        for cp in cps:
            cp.wait_send()

    return pl.pallas_call(
        body, name=name, in_specs=[HBM_SPEC], out_specs=HBM_SPEC, out_shape=jax.ShapeDtypeStruct(land.shape, land.dtype),
        scratch_shapes=[pltpu.SemaphoreType.DMA((3,)), pltpu.SemaphoreType.DMA((3,))],
        input_output_aliases={0: 0},
    )(land)


def _scatter_copies(src, land, send, recv):
    x, y, c = _place()
    return [pltpu.make_async_remote_copy(src_ref=src[i].at[2 * px + py], dst_ref=land[i].at[k], send_sem=send.at[3 * i + k],
                                         recv_sem=recv.at[3 * i + k], device_id=(px, py, c), device_id_type=MESH)
            for i in range(len(src)) for k, (px, py) in enumerate(_other_chips(x, y))]


def _scatter_start(pieces, *, name):
    n = len(pieces)

    def body(*refs):
        src, land, send, recv, token = refs[:n], refs[n:2 * n], refs[2 * n], refs[2 * n + 1], refs[-1]
        for cp in _scatter_copies(src, land, send, recv):
            cp.start()
        token[...] = jnp.zeros_like(token)

    lands = [lax.empty((3,) + p.shape[1:], p.dtype) for p in pieces]
    sems = pltpu.SemaphoreType.DMA((3 * n,))
    out = pl.pallas_call(
        body, name=name, in_specs=[HBM_SPEC] * (2 * n),
        out_specs=[SEM_SPEC, SEM_SPEC] + [HBM_SPEC] * (2 * n) + [VMEM_SPEC],
        out_shape=[sems, sems] + [pltpu.HBM(a.shape, a.dtype) for a in pieces + lands]
        + [jax.ShapeDtypeStruct((SUBLANES, LANES), F32)],
        input_output_aliases={i: 2 + i for i in range(2 * n)},
        compiler_params=pltpu.CompilerParams(has_side_effects=DATAFLOW),
    )(*[_in_hbm(a) for a in pieces + lands])
    return (out[0], out[1], out[2:2 + n], out[2 + n:2 + 2 * n]), out[-1]


def _scatter_wait(handle, after, *, name):
    send_sems, recv_sems, srcs, lands = handle
    n = len(srcs)

    def body(*refs):
        src, land, send, recv = refs[:n], refs[n:2 * n], refs[2 * n], refs[2 * n + 1]
        for cp in _scatter_copies(src, land, send, recv):
            cp.wait_send()
            cp.wait_recv()

    both = list(srcs) + list(lands)
    out = pl.pallas_call(
        body, name=name, in_specs=[HBM_SPEC] * (2 * n) + [SEM_SPEC, SEM_SPEC, ANY_SPEC], out_specs=[HBM_SPEC] * (2 * n),
        out_shape=[pltpu.HBM(a.shape, a.dtype) for a in both],
        input_output_aliases={i: i for i in range(2 * n)},
        compiler_params=pltpu.CompilerParams(has_side_effects=DATAFLOW),
    )(*both, send_sems, recv_sems, after)
    return out[n:]


def _chip_sum(pieces, got, chip, *, name):
    _, R, Cc = pieces.shape
    tr = _tile(R, 256, SUBLANES)

    def body(chip_ref, a_ref, g_ref, o_ref):
        o_ref[...] = ((a_ref[...] + g_ref[0].astype(F32)) + g_ref[1].astype(F32)) + g_ref[2].astype(F32)

    return pl.pallas_call(
        body, name=name,
        grid_spec=pltpu.PrefetchScalarGridSpec(
            num_scalar_prefetch=1, grid=(R // tr,),
            in_specs=[pl.BlockSpec((None, tr, Cc), lambda i, ch: (ch[0], i, 0)),
                      pl.BlockSpec((3, tr, Cc), lambda i, ch: (0, i, 0))],
            out_specs=pl.BlockSpec((tr, Cc), lambda i, ch: (i, 0))),
        out_shape=jax.ShapeDtypeStruct((R, Cc), F32),
        compiler_params=_params(("parallel",)),
    )(chip, pieces, got)


PACK_COLS = 1024


def _pack_rows(parts):
    return jnp.concatenate([p.reshape(-1, PACK_COLS) for p in parts], axis=0)


def _unpack_rows(block, shapes):
    lead = block.shape[:-2]
    out, off = [], 0
    for s in shapes:
        r = int(np.prod(s)) // PACK_COLS
        out.append(block[..., off:off + r, :].reshape(lead + tuple(s)))
        off += r
    assert off == block.shape[-2]
    return out


def _flat128(parts):
    out = []
    for p in parts:
        v = p.reshape(-1)
        pad = (-v.shape[0]) % LANES
        out.append(jnp.pad(v, (0, pad)) if pad else v)
    v = jnp.concatenate(out)
    pad = (-v.shape[0]) % (SUBLANES * LANES)
    if pad:
        v = jnp.pad(v, (0, pad))
    return v.reshape(-1, LANES)


def _unflat128(block, shapes):
    v = block.reshape(-1)
    out, off = [], 0
    for s in shapes:
        n = int(np.prod(s))
        out.append(v[off:off + n].reshape(s))
        off += n + ((-n) % LANES)
    return out


def kernel(x, hgrn_w_in, hgrn_lb_logits, hgrn_gnorm_w, hgrn_w_out, swa_w_q, swa_sinks, swa_w_out, shared_w_kv, rel_bias, ffn_w_in, ffn_conv_w, ffn_conv_b, ffn_w_out, ln_mix_g, ln_mix_b, ln_ffn_g, ln_ffn_b, loss_target, m_hgrn_w_in, m_hgrn_lb_logits, m_hgrn_gnorm_w, m_hgrn_w_out, m_swa_w_q, m_swa_sinks, m_swa_w_out, m_shared_w_kv, m_rel_bias, m_ffn_w_in, m_ffn_conv_w, m_ffn_conv_b, m_ffn_w_out, m_ln_mix_g, m_ln_mix_b, m_ln_ffn_g, m_ln_ffn_b, v_hgrn_w_in, v_hgrn_lb_logits, v_hgrn_gnorm_w, v_hgrn_w_out, v_swa_w_q, v_swa_sinks, v_swa_w_out, v_shared_w_kv, v_rel_bias, v_ffn_w_in, v_ffn_conv_w, v_ffn_conv_b, v_ffn_w_out, v_ln_mix_g, v_ln_mix_b, v_ln_ffn_g, v_ln_ffn_b):
    xi, yi, ci = _place()
    chip = 2 * xi + yi
    Dm = D_MODEL
    FC = 2 * FFN_DIM // N_CHIPS
    Fo = FFN_DIM // N_CHIPS
    Dq = Dm // N_CHIPS
    bf = lambda a: a.astype(BF16)

    shard0 = _pack_rows([bf(hgrn_w_in), bf(hgrn_w_out)])
    shard1 = _pack_rows([bf(swa_w_q), bf(swa_w_out), bf(shared_w_kv), bf(ffn_w_in[0]), bf(ffn_w_out[0])])
    shard2 = _pack_rows([bf(ffn_w_in[1]), bf(ffn_w_out[1])])
    handle0, token0 = _gather_start(shard0, None, name="gather_w0_start")

    lb_full = lax.dynamic_update_slice(jnp.zeros((2, Dm), F32), hgrn_lb_logits, (0, chip * Dq))
    cw_full = lax.dynamic_update_slice(jnp.zeros((DEPTH, 3, 2 * FFN_DIM), F32), ffn_conv_w, (0, 0, chip * FC))
    only_south = (ci == 0).astype(F32)
    small_in = _sum8(_flat128([lb_full, cw_full]) * only_south, name="gather_small", after=token0)
    lb_full, cw_full = _unflat128(small_in, [(2, Dm), (DEPTH, 3, 2 * FFN_DIM)])

    land0 = _fill_sibling(_gather_wait(handle0, small_in, name="gather_w0_wait"), name="gather_w0_fill")
    all0 = lax.dynamic_update_slice(land0, shard0[None], (chip, 0, 0))
    handle1, token1 = _gather_start(shard1, land0, name="gather_w1_start")
    w_in, w_hg_out = _unpack_rows(all0, [(Dm, Dm), (Dq, Dm)])

    def ffn_weights(w_fi, w_fo, l):
        halves = jnp.stack([jnp.concatenate([w_fi[0], w_fi[1]], axis=1), jnp.concatenate([w_fi[2], w_fi[3]], axis=1)])
        return {"ffn_in": {l: halves}, "ffn_out": {l: w_fo.reshape(FFN_DIM, Dm)}}

    got = {}

    def more_weights(k, after):
        shard = (shard1, shard2)[k - 1]
        land = _gather_wait(got.pop("handle"), after, name=f"gather_w{k}_wait")
        land = _fill_sibling(land, name=f"gather_w{k}_fill")
        allk = lax.dynamic_update_slice(land, shard[None], (chip, 0, 0))
        if k == 1:
            got["handle"], token2 = _gather_start(shard2, land, name="gather_w2_start")
            w_q, w_o, w_kv, w_fi, w_fo = _unpack_rows(allk, [(Dq, Dm), (Dq, Dm), (Dq, 2 * KV_DIM), (Dm, FC), (Fo, Dm)])
            got.update(ffn_weights(w_fi, w_fo, 0))
            return {"sw_q": w_q.reshape(Dm, Dm), "sw_out": w_o.reshape(Dm, Dm), "kv": w_kv.reshape(Dm, 2 * KV_DIM),
                    "token": token2, **{n: got[n] for n in ("ffn_in", "ffn_out")}}
        w_fi, w_fo = _unpack_rows(allk, [(Dm, FC), (Fo, Dm)])
        new = ffn_weights(w_fi, w_fo, 1)
        return {n: {**got[n], **new[n]} for n in new}

    got["handle"] = handle1

    w = {
        "hg_in": w_in, "hg_out": w_hg_out.reshape(Dm, Dm), "token": token1,
        "lb_logits": lb_full, "gnorm": hgrn_gnorm_w, "sinks": swa_sinks, "rel_bias": rel_bias,
        "conv_w_a": [cw_full[l, :, :FFN_DIM] for l in range(DEPTH)],
        "conv_w_b": [cw_full[l, :, FFN_DIM:] for l in range(DEPTH)],
        "conv_b_a": [ffn_conv_b[l:l + 1, :FFN_DIM] for l in range(DEPTH)],
        "conv_b_b": [ffn_conv_b[l:l + 1, FFN_DIM:] for l in range(DEPTH)],
        "ln_mix_g": [ln_mix_g[l:l + 1] for l in range(DEPTH)], "ln_mix_b": [ln_mix_b[l:l + 1] for l in range(DEPTH)],
        "ln_ffn_g": [ln_ffn_g[l:l + 1] for l in range(DEPTH)], "ln_ffn_b": [ln_ffn_b[l:l + 1] for l in range(DEPTH)],
    }

    sent = {}

    def ffn_pieces(gd):
        return [gd["ffn_in"], gd["ffn_out"].reshape(N_CHIPS, Fo, Dm)]

    def emit(k, gd):
        rows4 = lambda a: a.reshape(N_CHIPS, Dq, a.shape[-1])
        if k == 1:
            pieces = [rows4(gd["sw_q"]), rows4(gd["sw_out"]), rows4(gd["kv"])] + ffn_pieces(gd)
        elif k == 2:
            pieces = ffn_pieces(gd) + [rows4(gd["hg_out"])]
        else:
            pieces = [gd["hg_in"]]
        handle, token = _scatter_start([p.astype(BF16) for p in pieces], name=f"scatter_g{k}_start")
        sent[k] = (handle, pieces)
        return token

    loss_tile, grad_x, g = _local_step(x[0], loss_target[0], w, more_weights, emit)

    wts = dict(hgrn_w_in=hgrn_w_in, hgrn_lb_logits=hgrn_lb_logits, hgrn_gnorm_w=hgrn_gnorm_w, hgrn_w_out=hgrn_w_out,
               swa_w_q=swa_w_q, swa_sinks=swa_sinks, swa_w_out=swa_w_out, shared_w_kv=shared_w_kv, rel_bias=rel_bias,
               ffn_w_in=ffn_w_in, ffn_conv_w=ffn_conv_w, ffn_conv_b=ffn_conv_b, ffn_w_out=ffn_w_out,
               ln_mix_g=ln_mix_g, ln_mix_b=ln_mix_b, ln_ffn_g=ln_ffn_g, ln_ffn_b=ln_ffn_b)
    ms = dict(hgrn_w_in=m_hgrn_w_in, hgrn_lb_logits=m_hgrn_lb_logits, hgrn_gnorm_w=m_hgrn_gnorm_w, hgrn_w_out=m_hgrn_w_out,
              swa_w_q=m_swa_w_q, swa_sinks=m_swa_sinks, swa_w_out=m_swa_w_out, shared_w_kv=m_shared_w_kv, rel_bias=m_rel_bias,
              ffn_w_in=m_ffn_w_in, ffn_conv_w=m_ffn_conv_w, ffn_conv_b=m_ffn_conv_b, ffn_w_out=m_ffn_w_out,
              ln_mix_g=m_ln_mix_g, ln_mix_b=m_ln_mix_b, ln_ffn_g=m_ln_ffn_g, ln_ffn_b=m_ln_ffn_b)
    vs = dict(hgrn_w_in=v_hgrn_w_in, hgrn_lb_logits=v_hgrn_lb_logits, hgrn_gnorm_w=v_hgrn_gnorm_w, hgrn_w_out=v_hgrn_w_out,
              swa_w_q=v_swa_w_q, swa_sinks=v_swa_sinks, swa_w_out=v_swa_w_out, shared_w_kv=v_shared_w_kv, rel_bias=v_rel_bias,
              ffn_w_in=v_ffn_w_in, ffn_conv_w=v_ffn_conv_w, ffn_conv_b=v_ffn_conv_b, ffn_w_out=v_ffn_w_out,
              ln_mix_g=v_ln_mix_g, ln_mix_b=v_ln_mix_b, ln_ffn_g=v_ln_ffn_g, ln_ffn_b=v_ln_ffn_b)
    names = list(wts)
    grads, delta, new_m, new_v = {}, {}, {}, {}

    def update(n, ga, gb, layer=None, prev=None):
        r2 = lambda a: a.reshape(-1, a.shape[-1])
        rows = None if layer is None else (layer * ga.shape[0], ga.shape[0])
        return _adamw(r2(wts[n]), ga, gb, r2(ms[n]), r2(vs[n]), rows=rows, prev=prev,
                      name=f"adamw_{n}" + ("" if layer is None else f"_{layer}"))

    def keep(n, res):
        grads[n], delta[n], new_m[n], new_v[n] = [a.reshape(wts[n].shape) for a in res]

    chip1 = jnp.reshape(chip, (1,)).astype(jnp.int32)
    after = grad_x
    for k in (1, 2, 3):
        handle, pieces = sent[k]
        lands = _scatter_wait(handle, after, name=f"scatter_g{k}_wait")
        parts = [_chip_sum(p, l, chip1, name=f"scatter_g{k}_sum{i}") for i, (p, l) in enumerate(zip(pieces, lands))]
        sibs = _swap_sibling(parts, name=f"scatter_g{k}_swap")
        if k == 1:
            for n, ga, gb in zip(["swa_w_q", "swa_w_out", "shared_w_kv"], parts[:3], sibs[:3]):
                keep(n, update(n, ga, gb))
            ffn_in_1 = update("ffn_w_in", parts[3], sibs[3], layer=1)
            ffn_out_1 = update("ffn_w_out", parts[4], sibs[4], layer=1)
            after = ffn_out_1[3]
        elif k == 2:
            keep("ffn_w_in", update("ffn_w_in", parts[0], sibs[0], layer=0, prev=ffn_in_1))
            keep("ffn_w_out", update("ffn_w_out", parts[1], sibs[1], layer=0, prev=ffn_out_1))
            keep("hgrn_w_out", update("hgrn_w_out", parts[2], sibs[2]))
            after = new_v["hgrn_w_out"]
        else:
            keep("hgrn_w_in", update("hgrn_w_in", parts[0], sibs[0]))

    small_shapes = [(SUBLANES, LANES), (2, Dm), (1, HG_DIM), (1, SW_Q_HEADS), (REL_BUCKETS, SW_Q_HEADS),
                    (DEPTH, 3, 2 * FFN_DIM), (DEPTH, 2 * FFN_DIM)] + [(DEPTH, Dm)] * 4
    gc = g["conv"]
    conv_w_g = jnp.stack([jnp.concatenate([gc[l]["conv_w_a"], gc[l]["conv_w_b"]], axis=1) for l in range(DEPTH)])
    conv_b_g = jnp.concatenate([jnp.concatenate([gc[l]["conv_b_a"], gc[l]["conv_b_b"]], axis=1) for l in range(DEPTH)], axis=0)
    ln_g = [jnp.concatenate([g[f"{n}0"], g[f"{n}1"]], axis=0) for n in ("ln_mix_g", "ln_mix_b", "ln_ffn_g", "ln_ffn_b")]
    small_out = _sum8(_flat128([loss_tile, g["lb_logits"], g["gnorm"], g["sinks"], g["rel_bias"], conv_w_g, conv_b_g] + ln_g),
                      name="sum_small")
    (loss_t, g_lb, g_gn, g_sinks, g_rel, g_cw, g_cb, g_lmg, g_lmb, g_lfg, g_lfb) = _unflat128(small_out, small_shapes)
    loss = loss_t[0, 0]
    g_lb = lax.dynamic_slice_in_dim(g_lb, chip * Dq, Dq, axis=1)
    g_cw = lax.dynamic_slice_in_dim(g_cw, chip * FC, FC, axis=2)
    small_g = dict(hgrn_lb_logits=g_lb, hgrn_gnorm_w=g_gn, swa_sinks=g_sinks, rel_bias=g_rel, ffn_conv_w=g_cw,
                   ffn_conv_b=g_cb, ln_mix_g=g_lmg, ln_mix_b=g_lmb, ln_ffn_g=g_lfg, ln_ffn_b=g_lfb)
    small_names = list(small_g)
    sshapes = [wts[n].shape for n in small_names]
    _, d_, m_, v_ = _adamw(_flat128([wts[n] for n in small_names]), _flat128([small_g[n] for n in small_names]), None,
                           _flat128([ms[n] for n in small_names]), _flat128([vs[n] for n in small_names]), name="adamw_small")
    for n, a, b_, c_ in zip(small_names, _unflat128(d_, sshapes), _unflat128(m_, sshapes), _unflat128(v_, sshapes)):
        grads[n], delta[n], new_m[n], new_v[n] = small_g[n], a, b_, c_

    return (loss, grad_x[None], *[grads[n] for n in names], *[delta[n] for n in names],
            *[new_m[n] for n in names], *[new_v[n] for n in names])
```

```python
import math

import numpy as np
import jax
import jax.numpy as jnp
from jax import lax
from jax.experimental import pallas as pl
from jax.experimental.pallas import tpu as pltpu

F32 = jnp.float32
BF16 = jnp.bfloat16
MESH = pl.DeviceIdType.MESH

D_MODEL = 1024
DEPTH = 2
HG_HEADS = 8
HG_DIM = 128
SW_Q_HEADS = 16
SW_KV_HEADS = 4
SW_HEAD_DIM = 64
SW_GROUP = 4
SW_WINDOW = 128
REL_BUCKETS = 32
REL_MAX_DIST = 128
FFN_DIM = 2816
ALPHA = (2.0 * DEPTH) ** 0.25
LN_EPS = 1e-5
RMS_EPS = 1e-6
ADAM_LR = 0.001
ADAM_B1 = 0.9
ADAM_B2 = 0.999
ADAM_EPS = 1e-08
ADAM_WD = 0.01
ADAM_STEP = 10

VMEM_BYTES_V7X = 64 * 1024 * 1024
VMEM_LIMIT = VMEM_BYTES_V7X - 8 * 1024 * 1024
LANES = 128
SUBLANES = 8

HG_C = 64
HG_RB = 256
ROW_TILE = 256
CONV_R = 128
N_CHIPS = 4
N_DEV = 8

ANY_SPEC = pl.BlockSpec(memory_space=pl.ANY)


def _after(body, n_in, after):
    if after is None:
        return body, [], ()

    def wrapped(*refs):
        return body(*refs[:n_in], *refs[n_in + 1:])

    return wrapped, [ANY_SPEC], (after,)


def _params(sem=None):
    return pltpu.CompilerParams(dimension_semantics=sem, vmem_limit_bytes=VMEM_LIMIT)


def _tile(n, pref, unit=LANES):
    if n <= pref:
        return n
    best = None
    for t in range(unit, pref + 1, unit):
        if n % t == 0:
            best = t
    assert best is not None, (n, pref, unit)
    return best


def _dot(a, b, ca, cb):
    nb = a.ndim - 2
    batch = tuple(range(nb))
    return lax.dot_general(a.astype(BF16), b.astype(BF16), (((nb + ca,), (nb + cb,)), (batch, batch)),
                           preferred_element_type=F32)


@jax.custom_vjp
def mm(a, b):
    return _dot(a, b, 1, 0)


@jax.custom_vjp
def mm_nt(a, b):
    return _dot(a, b, 1, 1)


@jax.custom_vjp
def mm_tn(a, b):
    return _dot(a, b, 0, 0)


mm.defvjp(lambda a, b: (mm(a, b), (a, b)), lambda r, ct: (mm_nt(ct, r[1]), mm_tn(r[0], ct)))
mm_nt.defvjp(lambda a, b: (mm_nt(a, b), (a, b)), lambda r, ct: (mm(ct, r[1]), mm_tn(ct, r[0])))
mm_tn.defvjp(lambda a, b: (mm_tn(a, b), (a, b)), lambda r, ct: (mm_nt(r[1], ct), mm(r[0], ct)))


def _split2(x):
    hi = x.astype(BF16)
    return hi, (x - hi.astype(F32)).astype(BF16)


@jax.custom_vjp
def _scores(qt, kt):
    return _dot(qt, kt, 1, 1)


def _scores_bwd(r, ct):
    (qh, ql), (kh, kl) = _split2(r[0]), _split2(r[1])
    return _dot(ct, kh, 1, 0) + _dot(ct, kl, 1, 0), _dot(ct, qh, 0, 0) + _dot(ct, ql, 0, 0)


_scores.defvjp(lambda a, b: (_scores(a, b), (a, b)), _scores_bwd)


def _split3(x):
    hi = x.astype(BF16)
    r1 = x - hi.astype(F32)
    mid = r1.astype(BF16)
    lo = (r1 - mid.astype(F32)).astype(BF16)
    return hi, mid, lo


def _cumsum_impl(x):
    ax = x.ndim - 2
    n = x.shape[ax]
    row = lax.broadcasted_iota(jnp.int32, x.shape, ax)
    d = 1
    while d < n:
        x = x + jnp.where(row >= d, pltpu.roll(x, d, ax), 0.0)
        d *= 2
    return x


def _cumsum_rev_impl(x):
    ax = x.ndim - 2
    n = x.shape[ax]
    row = lax.broadcasted_iota(jnp.int32, x.shape, ax)
    d = 1
    while d < n:
        x = x + jnp.where(row < n - d, pltpu.roll(x, n - d, ax), 0.0)
        d *= 2
    return x


@jax.custom_vjp
def _cumsum(x):
    return _cumsum_impl(x)


_cumsum.defvjp(lambda x: (_cumsum_impl(x), None), lambda _, ct: (_cumsum_rev_impl(ct),))


def _matmul(a, b, *, mode, name, out_dtype=F32, add=None, add_scale=1.0, tm=512, tn=1408, tk=1408, after=None,
            split_n=False, planes=None):
    P = b.shape[0] if planes else 1
    a2, b2 = a.shape[-2:], b.shape[-2:]
    (M, K) = a2 if mode[0] == "n" else a2[::-1]
    (K2, N) = b2 if mode[1] == "n" else b2[::-1]
    assert K == K2, (a.shape, b.shape, mode)
    assert a.ndim == (3 if planes == "k" else 2) and b.ndim == (3 if planes else 2)
    tm, tn, tk = _tile(M, tm), _tile(N, tn), _tile(K, tk)
    nj, nkp = N // tn, K // tk
    nk = nkp * (P if planes == "k" else 1)
    ca, cb = (1 if mode[0] == "n" else 0), (0 if mode[1] == "n" else 1)
    a_blk, a_idx = ((tk, tm), lambda i, k: (k, i)) if mode[0] == "t" else ((tm, tk), lambda i, k: (i, k))
    b_blk, b_idx = ((tn, tk), lambda k, j: (j, k)) if mode[1] == "t" else ((tk, tn), lambda k, j: (k, j))
    if planes == "k":
        a_spec = pl.BlockSpec((None,) + a_blk, lambda i, j, k: (k // nkp,) + a_idx(i, k % nkp))
        b_spec = pl.BlockSpec((None,) + b_blk, lambda i, j, k: (k // nkp,) + b_idx(k % nkp, j))
    else:
        a_spec = pl.BlockSpec(a_blk, lambda i, j, k: a_idx(i, k))
        b_spec = (pl.BlockSpec((None,) + b_blk, lambda i, j, k: (j // nj,) + b_idx(k, j % nj)) if planes == "n"
                  else pl.BlockSpec(b_blk, lambda i, j, k: b_idx(k, j)))
    if split_n:
        o_spec, out_shape = pl.BlockSpec((None, tm, tn), lambda i, j, k: (j, i, 0)), (P * nj if planes == "n" else nj, M, tn)
    elif planes == "n":
        o_spec, out_shape = pl.BlockSpec((None, tm, tn), lambda i, j, k: (j // nj, i, j % nj)), (P, M, N)
    else:
        o_spec, out_shape = pl.BlockSpec((tm, tn), lambda i, j, k: (i, j)), (M, N)
    has_add = add is not None
    assert not (has_add and (split_n or planes == "n"))

    def finish(r, add_ref, o_ref):
        if has_add:
            r = r + add_scale * add_ref[...]
        o_ref[...] = r.astype(out_dtype)

    def body(*refs):
        a_ref, b_ref = refs[:2]
        add_ref = refs[2] if has_add else None
        o_ref = refs[3 if has_add else 2]
        if nk == 1:
            finish(_dot(a_ref[...], b_ref[...], ca, cb), add_ref, o_ref)
            return
        acc_ref = refs[-1]
        k = pl.program_id(2)

        @pl.when(k == 0)
        def _():
            acc_ref[...] = jnp.zeros_like(acc_ref)

        acc_ref[...] += _dot(a_ref[...], b_ref[...], ca, cb)

        @pl.when(k == nk - 1)
        def _():
            finish(acc_ref[...], add_ref, o_ref)

    in_specs = [a_spec, b_spec] + ([o_spec] if has_add else [])
    args = (a, b) + ((add,) if has_add else ())
    body, xs, xa = _after(body, len(args), after)
    in_specs, args = in_specs + xs, args + xa
    return pl.pallas_call(
        body, name=name, grid=(M // tm, nj * (P if planes == "n" else 1), nk), in_specs=in_specs, out_specs=o_spec,
        out_shape=jax.ShapeDtypeStruct(out_shape, out_dtype),
        scratch_shapes=[pltpu.VMEM((tm, tn), F32)] if nk > 1 else [],
        compiler_params=_params(("parallel", "parallel", "arbitrary")),
    )(*args)


def _ln(z, g, b):
    mu = jnp.mean(z, axis=-1, keepdims=True)
    zc = z - mu
    var = jnp.mean(zc * zc, axis=-1, keepdims=True)
    return zc * lax.rsqrt(var + LN_EPS) * g + b


def _matmul_ln(a, b, h, g, bias, *, name, tgt=None, tm=512, a_t=False):
    (T, K), (K2, Dm) = (a.shape[::-1] if a_t else a.shape), b.shape
    assert K == K2 and h.shape == (T, Dm)
    tm = _tile(T, tm, SUBLANES)
    last = tgt is not None

    def body(*refs):
        a_ref, b_ref, h_ref, g_ref, bias_ref = refs[:5]
        z = ALPHA * h_ref[...] + _dot(a_ref[...], b_ref[...], 0 if a_t else 1, 0)
        if not last:
            z_ref, y_ref, yb_ref = refs[5:]
            y = _ln(z, g_ref[...], bias_ref[...])
            z_ref[...] = z
            y_ref[...] = y
            yb_ref[...] = y.astype(BF16)
            return
        t_ref, dz_ref, dzb_ref, dg_ref, db_ref, l_ref = refs[5:]

        @pl.when(pl.program_id(0) == 0)
        def _():
            dg_ref[...] = jnp.zeros_like(dg_ref)
            db_ref[...] = jnp.zeros_like(db_ref)
            l_ref[...] = jnp.zeros_like(l_ref)

        y, vjp = jax.vjp(_ln, z, g_ref[...], bias_ref[...])
        e = y - t_ref[...]
        dz, dg, db = vjp(e * (1.0 / Dm))
        l_ref[...] += 0.5 * jnp.sum(jnp.mean(e * e, axis=-1, keepdims=True), axis=0, keepdims=True)
        dz_ref[...] = dz
        dzb_ref[...] = dz.astype(BF16)
        dg_ref[...] += dg
        db_ref[...] += db

    row = pl.BlockSpec((tm, Dm), lambda i: (i, 0))
    vec = pl.BlockSpec((1, Dm), lambda i: (0, 0))
    a_spec = pl.BlockSpec((K, tm), lambda i: (0, i)) if a_t else pl.BlockSpec((tm, K), lambda i: (i, 0))
    in_specs = [a_spec, pl.BlockSpec((K, Dm), lambda i: (0, 0)), row, vec, vec]
    f32, b16 = jax.ShapeDtypeStruct((T, Dm), F32), jax.ShapeDtypeStruct((T, Dm), BF16)
    if not last:
        return pl.pallas_call(
            body, name=name, grid=(T // tm,), in_specs=in_specs, out_specs=[row, row, row], out_shape=[f32, f32, b16],
            compiler_params=_params(("parallel",)),
        )(a, b, h, g, bias)
    return pl.pallas_call(
        body, name=name, grid=(T // tm,), in_specs=in_specs + [row],
        out_specs=[row, row, vec, vec, pl.BlockSpec((SUBLANES, LANES), lambda i: (0, 0))],
        out_shape=[f32, b16, jax.ShapeDtypeStruct((1, Dm), F32), jax.ShapeDtypeStruct((1, Dm), F32),
                   jax.ShapeDtypeStruct((SUBLANES, LANES), F32)],
        compiler_params=_params(("arbitrary",)),
    )(a, b, h, g, bias, tgt)


def _ln_bwd_matmul(dy, z, g, b, w, *, name, out_t=False, tm=512, after=None):
    T, Dm = z.shape
    N = w.shape[0]
    tm = _tile(T, tm, LANES if out_t else SUBLANES)

    def body(dy_ref, z_ref, g_ref, b_ref, w_ref, dz_ref, dzb_ref, dg_ref, db_ref, o_ref):
        @pl.when(pl.program_id(0) == 0)
        def _():
            dg_ref[...] = jnp.zeros_like(dg_ref)
            db_ref[...] = jnp.zeros_like(db_ref)

        _, vjp = jax.vjp(_ln, z_ref[...], g_ref[...], b_ref[...])
        dz, dg, db = vjp(dy_ref[...])
        dzb = dz.astype(BF16)
        dz_ref[...] = dz
        dzb_ref[...] = dzb
        dg_ref[...] += dg
        db_ref[...] += db
        o_ref[...] = (_dot(w_ref[...], dzb, 1, 1) if out_t else _dot(dzb, w_ref[...], 1, 1)).astype(BF16)

    row = pl.BlockSpec((tm, Dm), lambda i: (i, 0))
    vec = pl.BlockSpec((1, Dm), lambda i: (0, 0))
    o_spec = pl.BlockSpec((N, tm), lambda i: (0, i)) if out_t else pl.BlockSpec((tm, N), lambda i: (i, 0))
    body, xs, xa = _after(body, 5, after)
    return pl.pallas_call(
        body, name=name, grid=(T // tm,), in_specs=[row, row, vec, vec, pl.BlockSpec((N, Dm), lambda i: (0, 0))] + xs,
        out_specs=[row, row, vec, vec, o_spec],
        out_shape=[jax.ShapeDtypeStruct((T, Dm), F32), jax.ShapeDtypeStruct((T, Dm), BF16),
                   jax.ShapeDtypeStruct((1, Dm), F32), jax.ShapeDtypeStruct((1, Dm), F32),
                   jax.ShapeDtypeStruct((N, T) if out_t else (T, N), BF16)],
        compiler_params=_params(("arbitrary",)),
    )(dy, z, g, b, w, *xa)


def _hg_chunk(qr, fr, ir, gr, l0, l1, gw, st):
    C = qr.shape[-2]
    row = lax.broadcasted_iota(jnp.int32, qr.shape, qr.ndim - 2)
    lb = jax.nn.sigmoid(l0 - l1)
    fg = lb + (1.0 - lb) * jax.nn.sigmoid(fr)
    b = _cumsum(jnp.log(fg))
    q = jax.nn.silu(qr)
    k = 1.0 - fg
    bmid = lax.stop_gradient(jnp.sum(jnp.where(row == C // 2 - 1, b, 0.0), axis=-2, keepdims=True))
    bl = jnp.sum(jnp.where(row == C - 1, b, 0.0), axis=-2, keepdims=True)
    o = mm_nt(q * jnp.exp(b), st)
    sc = _scores(q * jnp.exp(b - bmid), k * jnp.exp(bmid - b))
    ti = lax.broadcasted_iota(jnp.int32, (C, C), 0)
    si = lax.broadcasted_iota(jnp.int32, (C, C), 1)
    sc = jnp.where(si <= ti, sc, 0.0)
    o = o + mm(sc, ir)
    st_new = st * jnp.exp(bl) + mm_tn(ir, k * jnp.exp(bl - b))
    on = o * lax.rsqrt(jnp.mean(o * o, axis=-1, keepdims=True) + RMS_EPS)
    return on * gw * jax.nn.silu(gr), st_new


def _heads(ref, rows):
    return jnp.stack([ref[rows, h * HG_DIM:(h + 1) * HG_DIM].astype(F32) for h in range(HG_HEADS)])


def _unheads(x):
    return jnp.concatenate([x[h] for h in range(HG_HEADS)], axis=-1)


def _hgrn_fwd(pre, lbl, gw, *, name):
    _, T, Dm = pre.shape
    rb = min(HG_RB, T)
    C = min(HG_C, rb)
    ncb = rb // C

    def body(pre_ref, lbl_ref, gw_ref, o_ref, st_ref, s_ref):
        @pl.when(pl.program_id(0) == 0)
        def _():
            s_ref[...] = jnp.zeros_like(s_ref)

        def chunk(ci, carry):
            r0 = pl.multiple_of(ci * C, C)
            rows = pl.ds(r0, C)
            st = s_ref[...]
            st_ref[ci] = st
            out, st_new = _hg_chunk(*[_heads(pre_ref.at[j], rows) for j in range(4)],
                                    _heads(lbl_ref, slice(0, 1)), _heads(lbl_ref, slice(1, 2)), gw_ref[...], st)
            o_ref[rows, :] = _unheads(out).astype(BF16)
            s_ref[...] = st_new
            return carry

        lax.fori_loop(0, ncb, chunk, 0, unroll=True)

    row = pl.BlockSpec((rb, Dm), lambda n: (n, 0))
    return pl.pallas_call(
        body, name=name, grid=(T // rb,),
        in_specs=[pl.BlockSpec((4, rb, Dm), lambda n: (0, n, 0)), pl.BlockSpec((2, Dm), lambda n: (0, 0)),
                  pl.BlockSpec((1, HG_DIM), lambda n: (0, 0))],
        out_specs=[row, pl.BlockSpec((ncb, HG_HEADS, HG_DIM, HG_DIM), lambda n: (n, 0, 0, 0))],
        out_shape=[jax.ShapeDtypeStruct((T, Dm), BF16),
                   jax.ShapeDtypeStruct((T // C, HG_HEADS, HG_DIM, HG_DIM), F32)],
        scratch_shapes=[pltpu.VMEM((HG_HEADS, HG_DIM, HG_DIM), F32)],
        compiler_params=_params(("arbitrary",)),
    )(pre, lbl, gw)


def _hgrn_bwd(pre, lbl, gw, states, dout, *, name, after=None):
    _, T, Dm = pre.shape
    rb = min(HG_RB, T)
    C = min(HG_C, rb)
    ncb = rb // C
    nb = T // rb

    def body(pre_ref, lbl_ref, gw_ref, st_ref, do_ref, dpre_ref, dlbl_ref, dgw_ref, ds_ref):
        @pl.when(pl.program_id(0) == 0)
        def _():
            ds_ref[...] = jnp.zeros_like(ds_ref)
            dlbl_ref[...] = jnp.zeros_like(dlbl_ref)
            dgw_ref[...] = jnp.zeros_like(dgw_ref)

        def chunk(cj, carry):
            ci = ncb - 1 - cj
            r0 = pl.multiple_of(ci * C, C)
            rows = pl.ds(r0, C)
            _, vjp = jax.vjp(_hg_chunk, *[_heads(pre_ref.at[j], rows) for j in range(4)],
                             _heads(lbl_ref, slice(0, 1)), _heads(lbl_ref, slice(1, 2)), gw_ref[...], st_ref[ci])
            *dpre, dl0, dl1, dgw, dst = vjp((_heads(do_ref, rows), ds_ref[...]))
            for j in range(4):
                dpre_ref[j, rows, :] = _unheads(dpre[j]).astype(BF16)
            dlbl_ref[0:1, :] += _unheads(dl0)
            dlbl_ref[1:2, :] += _unheads(dl1)
            dgw_ref[...] += dgw
            ds_ref[...] = dst
            return carry

        lax.fori_loop(0, ncb, chunk, 0, unroll=True)

    row = pl.BlockSpec((rb, Dm), lambda n: (nb - 1 - n, 0))
    lsp = pl.BlockSpec((2, Dm), lambda n: (0, 0))
    gsp = pl.BlockSpec((1, HG_DIM), lambda n: (0, 0))
    pre_spec = pl.BlockSpec((4, rb, Dm), lambda n: (0, nb - 1 - n, 0))
    body, xs, xa = _after(body, 5, after)
    return pl.pallas_call(
        body, name=name, grid=(nb,),
        in_specs=[pre_spec, lsp, gsp, pl.BlockSpec((ncb, HG_HEADS, HG_DIM, HG_DIM), lambda n: (nb - 1 - n, 0, 0, 0)), row] + xs,
        out_specs=[pre_spec, lsp, gsp],
        out_shape=[jax.ShapeDtypeStruct((4, T, Dm), BF16), jax.ShapeDtypeStruct((2, Dm), F32),
                   jax.ShapeDtypeStruct((1, HG_DIM), F32)],
        scratch_shapes=[pltpu.VMEM((HG_HEADS, HG_DIM, HG_DIM), F32)],
        compiler_params=_params(("arbitrary",)),
    )(pre, lbl, gw, states, dout, *xa)


CONV_HALO = 2 * SUBLANES


def _conv_rows(u_ref, scr, w, bias, r0, R):
    cur = u_ref[pl.ds(r0, R), :].astype(F32)
    p0 = pl.multiple_of(jnp.maximum(r0 - CONV_HALO, 0), CONV_HALO)
    scr[0:CONV_HALO, :] = jnp.where(r0 > 0, u_ref[pl.ds(p0, CONV_HALO), :].astype(F32), 0.0)
    scr[CONV_HALO:CONV_HALO + R, :] = cur
    s1 = scr[CONV_HALO - 1:CONV_HALO - 1 + R, :]
    s2 = scr[CONV_HALO - 2:CONV_HALO - 2 + R, :]
    return w[0:1, :] * s2 + w[1:2, :] * s1 + w[2:3, :] * cur + bias, cur, s1, s2


def _conv_gate_fwd(u, wa, wb, ba, bb, *, name):
    _, T, Fd = u.shape
    R = min(CONV_R, T)
    tc = LANES

    def body(u_ref, wa_ref, wb_ref, ba_ref, bb_ref, o_ref, sa, sb):
        wa_, wb_, ba_, bb_ = wa_ref[...], wb_ref[...], ba_ref[...], bb_ref[...]

        def step(ri, carry):
            r0 = pl.multiple_of(ri * R, R)
            ca = _conv_rows(u_ref.at[0], sa, wa_, ba_, r0, R)[0]
            cb = _conv_rows(u_ref.at[1], sb, wb_, bb_, r0, R)[0]
            o_ref[pl.ds(r0, R), :] = (jax.nn.silu(ca) * cb).astype(BF16)
            return carry

        lax.fori_loop(0, T // R, step, 0)

    col = pl.BlockSpec((T, tc), lambda j: (0, j))
    wsp = pl.BlockSpec((3, tc), lambda j: (0, j))
    bsp = pl.BlockSpec((1, tc), lambda j: (0, j))
    both = pl.BlockSpec((2, T, tc), lambda j: (0, 0, j))
    return pl.pallas_call(
        body, name=name, grid=(Fd // tc,), in_specs=[both, wsp, wsp, bsp, bsp], out_specs=col,
        out_shape=jax.ShapeDtypeStruct((T, Fd), BF16),
        scratch_shapes=[pltpu.VMEM((CONV_HALO + R, tc), F32)] * 2,
        compiler_params=_params(("parallel",)),
    )(u, wa, wb, ba, bb)


def _conv_gate_bwd(u, wa, wb, ba, bb, dact, *, name):
    _, T, Fd = u.shape
    R = min(CONV_R, T)
    nr = T // R
    tc = LANES

    def body(u_ref, wa_ref, wb_ref, ba_ref, bb_ref, da_ref,
             du_ref, dwa_ref, dwb_ref, dba_ref, dbb_ref, sa, sb, sda, sdb):
        wa_, wb_, ba_, bb_ = wa_ref[...], wb_ref[...], ba_ref[...], bb_ref[...]
        sda[R:R + SUBLANES, :] = jnp.zeros((SUBLANES, tc), F32)
        sdb[R:R + SUBLANES, :] = jnp.zeros((SUBLANES, tc), F32)

        def taps(dc, cur, s1, s2):
            return jnp.concatenate([jnp.sum(dc * s2, axis=0, keepdims=True), jnp.sum(dc * s1, axis=0, keepdims=True),
                                    jnp.sum(dc * cur, axis=0, keepdims=True)], axis=0)

        def du_rows(sd, dc, w):
            sd[0:R, :] = dc
            du = w[2:3, :] * dc + w[1:2, :] * sd[1:1 + R, :] + w[0:1, :] * sd[2:2 + R, :]
            sd[R:R + SUBLANES, :] = dc[0:SUBLANES]
            return du

        def step(rj, carry):
            dwa, dwb, dba, dbb = carry
            r0 = pl.multiple_of((nr - 1 - rj) * R, R)
            ca, cura, s1a, s2a = _conv_rows(u_ref.at[0], sa, wa_, ba_, r0, R)
            cb, curb, s1b, s2b = _conv_rows(u_ref.at[1], sb, wb_, bb_, r0, R)
            dact_ = da_ref[pl.ds(r0, R), :].astype(F32)
            sg = jax.nn.sigmoid(ca)
            dca = dact_ * cb * (sg * (1.0 + ca * (1.0 - sg)))
            dcb = dact_ * (ca * sg)
            du_ref[0, pl.ds(r0, R), :] = du_rows(sda, dca, wa_).astype(BF16)
            du_ref[1, pl.ds(r0, R), :] = du_rows(sdb, dcb, wb_).astype(BF16)
            return (dwa + taps(dca, cura, s1a, s2a), dwb + taps(dcb, curb, s1b, s2b),
                    dba + jnp.sum(dca, axis=0, keepdims=True), dbb + jnp.sum(dcb, axis=0, keepdims=True))

        z3 = jnp.zeros((3, tc), F32)
        z1 = jnp.zeros((1, tc), F32)
        dwa, dwb, dba, dbb = lax.fori_loop(0, nr, step, (z3, z3, z1, z1))
        dwa_ref[...] = dwa
        dwb_ref[...] = dwb
        dba_ref[...] = dba
        dbb_ref[...] = dbb

    col = pl.BlockSpec((T, tc), lambda j: (0, j))
    wsp = pl.BlockSpec((3, tc), lambda j: (0, j))
    bsp = pl.BlockSpec((1, tc), lambda j: (0, j))
    both = pl.BlockSpec((2, T, tc), lambda j: (0, 0, j))
    return pl.pallas_call(
        body, name=name, grid=(Fd // tc,), in_specs=[both, wsp, wsp, bsp, bsp, col],
        out_specs=[both, wsp, wsp, bsp, bsp],
        out_shape=[jax.ShapeDtypeStruct((2, T, Fd), BF16)] + [jax.ShapeDtypeStruct((3, Fd), F32)] * 2
        + [jax.ShapeDtypeStruct((1, Fd), F32)] * 2,
        scratch_shapes=[pltpu.VMEM((CONV_HALO + R, tc), F32)] * 2 + [pltpu.VMEM((R + SUBLANES, tc), F32)] * 2,
        compiler_params=_params(("parallel",)),
    )(u, wa, wb, ba, bb, dact)


def _bucket_index():
    t = np.arange(SW_WINDOW)[None, :] + SW_WINDOW
    s = np.arange(2 * SW_WINDOW)[:, None]
    dist = np.maximum(t - s, 0)
    exact = REL_BUCKETS // 2
    d = np.maximum(dist, 1).astype(np.float32)
    log_b = exact + (np.log(d / np.float32(exact)) / np.float32(math.log(REL_MAX_DIST / exact))
                     * np.float32(REL_BUCKETS - exact)).astype(np.int32)
    bucket = np.where(dist < exact, dist, np.minimum(log_b, REL_BUCKETS - 1))
    return bucket.astype(np.int32).reshape(1, -1)


BIAS_COLS = SW_WINDOW * 2 * SW_WINDOW
BIAS_TILE = 4096


def _bias_from_table(table, bucket, *, name):
    def body(t_ref, idx_ref, o_ref):
        onehot = (lax.broadcasted_iota(jnp.int32, (REL_BUCKETS, BIAS_TILE), 0) == idx_ref[...]).astype(BF16)
        acc = jnp.zeros((SW_Q_HEADS, BIAS_TILE), F32)
        for piece in _split3(t_ref[...]):
            acc = acc + lax.dot_general(piece, onehot, (((0,), (0,)), ((), ())), preferred_element_type=F32)
        o_ref[...] = acc

    return pl.pallas_call(
        body, name=name, grid=(BIAS_COLS // BIAS_TILE,),
        in_specs=[pl.BlockSpec((REL_BUCKETS, SW_Q_HEADS), lambda j: (0, 0)), pl.BlockSpec((1, BIAS_TILE), lambda j: (0, j))],
        out_specs=pl.BlockSpec((SW_Q_HEADS, BIAS_TILE), lambda j: (0, j)),
        out_shape=jax.ShapeDtypeStruct((SW_Q_HEADS, BIAS_COLS), F32),
        compiler_params=_params(("parallel",)),
    )(table, bucket)


def _table_grad(dbias, bucket, *, name):
    def body(d_ref, idx_ref, o_ref):
        @pl.when(pl.program_id(0) == 0)
        def _():
            o_ref[...] = jnp.zeros_like(o_ref)

        onehot = (lax.broadcasted_iota(jnp.int32, (REL_BUCKETS, BIAS_TILE), 0) == idx_ref[...]).astype(BF16)
        acc = jnp.zeros((REL_BUCKETS, SW_Q_HEADS), F32)
        for piece in _split3(d_ref[...]):
            acc = acc + lax.dot_general(onehot, piece, (((1,), (1,)), ((), ())), preferred_element_type=F32)
        o_ref[...] += acc

    return pl.pallas_call(
        body, name=name, grid=(BIAS_COLS // BIAS_TILE,),
        in_specs=[pl.BlockSpec((SW_Q_HEADS, BIAS_TILE), lambda j: (0, j)), pl.BlockSpec((1, BIAS_TILE), lambda j: (0, j))],
        out_specs=pl.BlockSpec((REL_BUCKETS, SW_Q_HEADS), lambda j: (0, 0)),
        out_shape=jax.ShapeDtypeStruct((REL_BUCKETS, SW_Q_HEADS), F32),
        compiler_params=_params(("arbitrary",)),
    )(dbias, bucket)


KV_DIM = SW_KV_HEADS * SW_HEAD_DIM
GROUP_ROWS = SW_GROUP * SW_HEAD_DIM
GROUP_LANES = SW_GROUP * SW_WINDOW


def _band_mask(n):
    s = lax.broadcasted_iota(jnp.int32, (2 * SW_WINDOW, GROUP_LANES), 0)
    t = (lax.broadcasted_iota(jnp.int32, (2 * SW_WINDOW, GROUP_LANES), 1) & (SW_WINDOW - 1)) + SW_WINDOW
    dist = t - s
    return (dist >= 0) & (dist < SW_WINDOW) & ((n > 0) | (s >= SW_WINDOW))


def _side_by_side(x_ref, g):
    r0 = g * GROUP_ROWS
    return jnp.concatenate([x_ref[r0 + r * SW_HEAD_DIM:r0 + (r + 1) * SW_HEAD_DIM, :] for r in range(SW_GROUP)], axis=1)


def _group_inputs(bias_ref, sink_ref, g):
    heads = range(g * SW_GROUP, (g + 1) * SW_GROUP)
    bias = jnp.concatenate([bias_ref[h] for h in heads], axis=1)
    sink = jnp.concatenate([jnp.broadcast_to(sink_ref[:, h:h + 1], (1, SW_WINDOW)) for h in heads], axis=1)
    return heads, bias, sink


def _kv_pair(kvp_ref, kvc_ref, g):
    ks = slice(g * SW_HEAD_DIM, (g + 1) * SW_HEAD_DIM)
    vs = slice(KV_DIM + g * SW_HEAD_DIM, KV_DIM + (g + 1) * SW_HEAD_DIM)
    kk = jnp.concatenate([kvp_ref[:, ks], kvc_ref[:, ks]], axis=0)
    vv = jnp.concatenate([kvp_ref[:, vs], kvc_ref[:, vs]], axis=0)
    return kk, vv, ks, vs


def _col_max(x):
    return jnp.max(x, axis=0, keepdims=True)


def _col_sum(x):
    return jnp.sum(x, axis=0, keepdims=True)


def _attn_fwd(qt, kv, bias, sinks, *, name):
    Dm, T = qt.shape
    W = SW_WINDOW

    def body(q_ref, kvc_ref, kvp_ref, bias_ref, sink_ref, o_ref):
        mask = _band_mask(pl.program_id(0))
        G = range(SW_KV_HEADS)
        ins = [_group_inputs(bias_ref, sink_ref, g) for g in G]
        kvs = [_kv_pair(kvp_ref, kvc_ref, g) for g in G]
        q = [_side_by_side(q_ref, g) for g in G]
        lg = [jnp.where(mask, mm(kvs[g][0], q[g]) * (SW_HEAD_DIM ** -0.5) + ins[g][1], -jnp.inf) for g in G]
        m = [jnp.maximum(_col_max(lg[g]), ins[g][2]) for g in G]
        p = [jnp.exp(lg[g] - m[g]) for g in G]
        den = [_col_sum(p[g]) + jnp.exp(ins[g][2] - m[g]) for g in G]
        o = [mm_tn(kvs[g][1], p[g]) / den[g] for g in G]
        for g in G:
            for r in range(SW_GROUP):
                o_ref[g * GROUP_ROWS + r * SW_HEAD_DIM:g * GROUP_ROWS + (r + 1) * SW_HEAD_DIM, :] = (
                    o[g][:, r * W:(r + 1) * W].astype(BF16))

    return pl.pallas_call(
        body, name=name, grid=(T // W,),
        in_specs=[pl.BlockSpec((Dm, W), lambda n: (0, n)),
                  pl.BlockSpec((W, 2 * KV_DIM), lambda n: (n, 0)),
                  pl.BlockSpec((W, 2 * KV_DIM), lambda n: (jnp.maximum(n - 1, 0), 0)),
                  pl.BlockSpec((SW_Q_HEADS, 2 * W, W), lambda n: (0, 0, 0)),
                  pl.BlockSpec((1, SW_Q_HEADS), lambda n: (0, 0))],
        out_specs=pl.BlockSpec((Dm, W), lambda n: (0, n)),
        out_shape=jax.ShapeDtypeStruct((Dm, T), BF16),
        compiler_params=_params(("parallel",)),
    )(qt, kv, kv, bias, sinks)


def _attn_bwd(qt, kv, bias, sinks, dot, *, name):
    Dm, T = qt.shape
    W = SW_WINDOW
    nb = T // W

    def body(q_ref, kvc_ref, kvp_ref, bias_ref, sink_ref, do_ref,
             dq_ref, dkv_ref, dbias_ref, dsink_ref, carry_ref):
        @pl.when(pl.program_id(0) == 0)
        def _():
            carry_ref[...] = jnp.zeros_like(carry_ref)
            dbias_ref[...] = jnp.zeros_like(dbias_ref)
            dsink_ref[...] = jnp.zeros_like(dsink_ref)

        n = nb - 1 - pl.program_id(0)
        mask = _band_mask(n)
        lane = lax.broadcasted_iota(jnp.int32, (1, SW_Q_HEADS), 1)
        sc = SW_HEAD_DIM ** -0.5
        G = range(SW_KV_HEADS)
        ins = [_group_inputs(bias_ref, sink_ref, g) for g in G]
        kvs = [_kv_pair(kvp_ref, kvc_ref, g) for g in G]
        q = [_side_by_side(q_ref, g) for g in G]
        do = [_side_by_side(do_ref, g) for g in G]
        lg = [jnp.where(mask, mm(kvs[g][0], q[g]) * sc + ins[g][1], -jnp.inf) for g in G]
        m = [jnp.maximum(_col_max(lg[g]), ins[g][2]) for g in G]
        p = [jnp.exp(lg[g] - m[g]) for g in G]
        ps = [jnp.exp(ins[g][2] - m[g]) for g in G]
        rden = [1.0 / (_col_sum(p[g]) + ps[g]) for g in G]
        pn = [p[g] * rden[g] for g in G]
        dpn = [mm(kvs[g][1], do[g]) for g in G]
        delta = [_col_sum(pn[g] * dpn[g]) for g in G]
        ds = [pn[g] * (dpn[g] - delta[g]) for g in G]
        dsr = [-(ps[g] * rden[g]) * delta[g] for g in G]
        dq = [mm_tn(kvs[g][0], ds[g]) * sc for g in G]
        dkk = [mm_nt(ds[g], q[g]) * sc for g in G]
        dvv = [mm_nt(pn[g], do[g]) for g in G]
        dsink = jnp.zeros((1, SW_Q_HEADS), F32)
        for g in G:
            _, _, ks, vs = kvs[g]
            for r, h in enumerate(ins[g][0]):
                cols = slice(r * W, (r + 1) * W)
                dbias_ref[h] += ds[g][:, cols]
                dq_ref[g * GROUP_ROWS + r * SW_HEAD_DIM:g * GROUP_ROWS + (r + 1) * SW_HEAD_DIM, :] = dq[g][:, cols].astype(BF16)
                dsink = dsink + jnp.where(lane == h, jnp.sum(dsr[g][:, cols], axis=1, keepdims=True), 0.0)
            dkv_ref[:, ks] = (carry_ref[:, ks] + dkk[g][W:]).astype(BF16)
            dkv_ref[:, vs] = (carry_ref[:, vs] + dvv[g][W:]).astype(BF16)
            carry_ref[:, ks] = dkk[g][:W]
            carry_ref[:, vs] = dvv[g][:W]
        dsink_ref[...] += dsink

    rev = lambda n: (nb - 1 - n, 0)
    revt = lambda n: (0, nb - 1 - n)
    return pl.pallas_call(
        body, name=name, grid=(nb,),
        in_specs=[pl.BlockSpec((Dm, W), revt),
                  pl.BlockSpec((W, 2 * KV_DIM), rev),
                  pl.BlockSpec((W, 2 * KV_DIM), lambda n: (jnp.maximum(nb - 2 - n, 0), 0)),
                  pl.BlockSpec((SW_Q_HEADS, 2 * W, W), lambda n: (0, 0, 0)),
                  pl.BlockSpec((1, SW_Q_HEADS), lambda n: (0, 0)),
                  pl.BlockSpec((Dm, W), revt)],
        out_specs=[pl.BlockSpec((Dm, W), revt), pl.BlockSpec((W, 2 * KV_DIM), rev),
                   pl.BlockSpec((SW_Q_HEADS, 2 * W, W), lambda n: (0, 0, 0)),
                   pl.BlockSpec((1, SW_Q_HEADS), lambda n: (0, 0))],
        out_shape=[jax.ShapeDtypeStruct((Dm, T), BF16), jax.ShapeDtypeStruct((T, 2 * KV_DIM), BF16),
                   jax.ShapeDtypeStruct((SW_Q_HEADS, 2 * W, W), F32), jax.ShapeDtypeStruct((1, SW_Q_HEADS), F32)],
        scratch_shapes=[pltpu.VMEM((W, 2 * KV_DIM), F32)],
        compiler_params=_params(("arbitrary",)),
    )(qt, kv, kv, bias, sinks, dot)


def _ffn_fwd(hb, w, l, after=None):
    u = _matmul(hb, w["ffn_in"][l], mode="nn", planes="n", out_dtype=BF16, name=f"ffn{l}_up", tm=1024, after=after)
    act = _conv_gate_fwd(u, w["conv_w_a"][l], w["conv_w_b"][l], w["conv_b_a"][l], w["conv_b_b"][l],
                         name=f"ffn{l}_conv_gate")
    return u, act


def _ffn_bwd(dffb, dh_scaled, hb, u, act, w, l, dact=None):
    if dact is None:
        dact = _matmul(dffb, w["ffn_out"][l], mode="nt", out_dtype=BF16, name=f"ffn{l}_down_dx", tm=1024)
    g_out = _matmul(act, dffb, mode="tn", name=f"ffn{l}_down_dw", tm=1408, tn=1024, tk=1024)
    du, dwa, dwb, dba, dbb = _conv_gate_bwd(u, w["conv_w_a"][l], w["conv_w_b"][l], w["conv_b_a"][l], w["conv_b_b"][l],
                                            dact, name=f"ffn{l}_conv_gate_bwd")
    dh = _matmul(du, w["ffn_in"][l], mode="nt", planes="k", add=dh_scaled, add_scale=ALPHA, name=f"ffn{l}_up_dx",
                 tn=1024, tk=FFN_DIM)
    g_in = _matmul(hb, du, mode="tn", planes="n", name=f"ffn{l}_up_dw", tm=1024, tn=FFN_DIM // 2, tk=1024, split_n=True)
    return dh, dict(ffn_out=g_out, ffn_in=g_in, conv_w_a=dwa, conv_w_b=dwb, conv_b_a=dba, conv_b_b=dbb)


def _local_step(x, tgt, w, more_weights, emit):
    bucket = jnp.asarray(_bucket_index())
    xb = x.astype(BF16)

    pre = _matmul(xb, w["hg_in"], mode="nn", planes="n", out_dtype=BF16, name="hg_in", tm=1024, tn=1024,
                  after=w.get("token"))
    og, states = _hgrn_fwd(pre, w["lb_logits"], w["gnorm"], name="hgrn_fwd")
    z1, h1, h1b = _matmul_ln(og, w["hg_out"], x, w["ln_mix_g"][0], w["ln_mix_b"][0], name="hg_out_ln")
    w = {**w, **more_weights(1, h1b)}
    u0, act0 = _ffn_fwd(h1b, w, 0, after=w.get("token"))
    z2, h2, h2b = _matmul_ln(act0, w["ffn_out"][0], h1, w["ln_ffn_g"][0], w["ln_ffn_b"][0], name="ffn0_down_ln")
    kv = _matmul(h2b, w["kv"], mode="nn", out_dtype=BF16, name="kv_proj")

    bias = _bias_from_table(w["rel_bias"], bucket, name="rel_bias_expand").reshape(SW_Q_HEADS, 2 * SW_WINDOW, SW_WINDOW)
    q1 = _matmul(w["sw_q"], h2b, mode="tt", out_dtype=BF16, name="sw_q", tm=1024, tn=1024)
    o1 = _attn_fwd(q1, kv, bias, w["sinks"], name="attn_fwd")
    z3, h3, h3b = _matmul_ln(o1, w["sw_out"], h2, w["ln_mix_g"][1], w["ln_mix_b"][1], a_t=True, name="sw_out_ln")
    w = {**w, **more_weights(2, h3b)}
    u1, act1 = _ffn_fwd(h3b, w, 1)

    g = {}
    dz, dzb, dg_, db_, loss_tile = _matmul_ln(act1, w["ffn_out"][1], h3, w["ln_ffn_g"][1], w["ln_ffn_b"][1], tgt=tgt,
                                              name="ffn1_down_ln_loss")

    g["ln_ffn_g1"], g["ln_ffn_b1"] = dg_, db_
    dh3, gf1 = _ffn_bwd(dzb, dz, h3b, u1, act1, w, 1)
    dz, dzb, dg_, db_, do1 = _ln_bwd_matmul(dh3, z3, w["ln_mix_g"][1], w["ln_mix_b"][1], w["sw_out"], out_t=True,
                                            name="ln_mix1_bwd_sw_out_dx")
    g["ln_mix_g1"], g["ln_mix_b1"] = dg_, db_
    g_sw_out = _matmul(o1, dzb, mode="nn", name="sw_out_dw", tm=1024, tn=1024, tk=1024)
    dq1, dkv, dbias, dsinks = _attn_bwd(q1, kv, bias, w["sinks"], do1, name="attn_bwd")
    g["sinks"] = dsinks
    g["rel_bias"] = _table_grad(dbias.reshape(SW_Q_HEADS, BIAS_COLS), bucket, name="rel_bias_grad")
    dh2 = _matmul(dq1, w["sw_q"], mode="tt", add=dz, add_scale=ALPHA, name="sw_q_dx", tn=1024)
    dh2 = _matmul(dkv, w["kv"], mode="nt", add=dh2, name="kv_dx", tn=1024)
    g_sw_q = _matmul(h2b, dq1, mode="tt", name="sw_q_dw", tm=1024, tn=1024, tk=1024)
    g_kv = _matmul(h2b, dkv, mode="tn", name="kv_dw", tm=1024, tn=512, tk=1024)
    tok = emit(1, dict(sw_q=g_sw_q, sw_out=g_sw_out, kv=g_kv, ffn_in=gf1["ffn_in"], ffn_out=gf1["ffn_out"]))

    dz, dzb, dg_, db_, dact0 = _ln_bwd_matmul(dh2, z2, w["ln_ffn_g"][0], w["ln_ffn_b"][0], w["ffn_out"][0],
                                              name="ln_ffn0_bwd_down_dx", after=tok)
    g["ln_ffn_g0"], g["ln_ffn_b0"] = dg_, db_
    dh1, gf0 = _ffn_bwd(dzb, dz, h1b, u0, act0, w, 0, dact=dact0)
    dz, dzb, dg_, db_, dog = _ln_bwd_matmul(dh1, z1, w["ln_mix_g"][0], w["ln_mix_b"][0], w["hg_out"],
                                            name="ln_mix0_bwd_hg_out_dx")
    g["ln_mix_g0"], g["ln_mix_b0"] = dg_, db_
    g_hg_out = _matmul(og, dzb, mode="tn", name="hg_out_dw", tm=1024, tn=1024, tk=1024)
    tok = emit(2, dict(hg_out=g_hg_out, ffn_in=gf0["ffn_in"], ffn_out=gf0["ffn_out"]))
    dpre, g["lb_logits"], g["gnorm"] = _hgrn_bwd(pre, w["lb_logits"], w["gnorm"], states, dog, name="hgrn_bwd", after=tok)
    tok = emit(3, dict(hg_in=_matmul(xb, dpre, mode="tn", planes="n", name="hg_in_dw", tm=1024, tn=1024, tk=1024)))
    dx = _matmul(dpre, w["hg_in"], mode="nt", planes="k", add=dz, add_scale=ALPHA, name="hg_in_dx", tn=1024, tk=1024,
                 after=tok)
    g["conv"] = [{k: gf[k] for k in ("conv_w_a", "conv_w_b", "conv_b_a", "conv_b_b")} for gf in (gf0, gf1)]
    return loss_tile, dx, g


def _adamw(wt, ga, gb, m, v, *, name, rows=None, prev=None):
    R, Cc = wt.shape
    r0, n = rows if rows is not None else (0, R)
    tr = _tile(n, 256, SUBLANES) if n % SUBLANES == 0 else n
    assert r0 % tr == 0
    c1 = 1.0 - ADAM_B1 ** ADAM_STEP
    c2 = 1.0 - ADAM_B2 ** ADAM_STEP
    two = gb is not None
    n_in = 5 if two else 4

    def body(*refs):
        if two:
            w_ref, ga_ref, gb_ref, m_ref, v_ref = refs[:5]
            g_ = ga_ref[...] + gb_ref[...]
        else:
            w_ref, ga_ref, m_ref, v_ref = refs[:4]
            g_ = ga_ref[...]
        g_ref, d_ref, nm_ref, nv_ref = refs[-4:]
        nm = ADAM_B1 * m_ref[...] + (1.0 - ADAM_B1) * g_
        nv = ADAM_B2 * v_ref[...] + (1.0 - ADAM_B2) * (g_ * g_)
        g_ref[...] = g_
        d_ref[...] = -ADAM_LR * ((nm / c1) / (jnp.sqrt(nv / c2) + ADAM_EPS) + ADAM_WD * w_ref[...])
        nm_ref[...] = nm
        nv_ref[...] = nv

    full = pl.BlockSpec((tr, Cc), lambda i: (i + r0 // tr, 0))
    part = pl.BlockSpec((tr, Cc), lambda i: (i, 0))
    args = (wt, ga, gb, m, v) if two else (wt, ga, m, v)
    in_specs = [full] + [part] * (n_in - 3) + [full, full]
    aliases = {}
    if prev is not None:
        args, in_specs = args + tuple(prev), in_specs + [ANY_SPEC] * 4
        aliases = {n_in + t: t for t in range(4)}
    return pl.pallas_call(
        body, name=name, grid=(n // tr,), in_specs=in_specs, out_specs=[full] * 4,
        out_shape=[jax.ShapeDtypeStruct((R, Cc), F32)] * 4, input_output_aliases=aliases,
        compiler_params=_params(("parallel",)),
    )(*args)


HBM_SPEC = pl.BlockSpec(memory_space=pltpu.HBM)
SEM_SPEC = pl.BlockSpec(memory_space=pltpu.SEMAPHORE)
VMEM_SPEC = pl.BlockSpec(memory_space=pltpu.VMEM)
DATAFLOW = pltpu.SideEffectType.DATAFLOW_SIDE_EFFECTING


def _in_hbm(a):
    return pltpu.with_memory_space_constraint(a, pltpu.HBM)


def _place():
    return lax.axis_index("x"), lax.axis_index("y"), lax.axis_index("c")


def _other_chips(x, y):
    return [(1 - x, y), (x, 1 - y), (1 - x, 1 - y)]


def _sum8(v, *, name, after=None):
    r = v.shape[0]

    def body(v_ref, all_ref, o_ref, send_sems, recv_sems, local_sem):
        x, y, c = _place()
        me, sibling = (x, y, c), (x, y, 1 - c)
        chips = _other_chips(x, y)

        def rows(px, py, pc):
            return all_ref.at[pl.ds((4 * px + 2 * py + pc) * r, r), :]

        def copy(k, block, to, src=None):
            return pltpu.make_async_remote_copy(
                src_ref=rows(*block) if src is None else src, dst_ref=rows(*block),
                send_sem=send_sems.at[k], recv_sem=recv_sems.at[k], device_id=to, device_id_type=MESH)

        mine = pltpu.make_async_copy(v_ref, rows(*me), local_sem)
        mine.start()
        first = [copy(0, me, sibling, src=v_ref)]
        first += [copy(1 + j, me, (*chip, c), src=v_ref) for j, chip in enumerate(chips)]
        for cp in first:
            cp.start()
        passed = [copy(4 + j, (*chip, c), sibling) for j, chip in enumerate(chips)]
        for j, chip in enumerate(chips):
            copy(1 + j, (*chip, c), me).wait_recv()
            passed[j].start()
        copy(0, sibling, me).wait_recv()
        for j, chip in enumerate(chips):
            copy(4 + j, (*chip, 1 - c), me).wait_recv()
        for cp in first + passed:
            cp.wait_send()
        mine.wait()
        acc = all_ref[pl.ds(0, r), :]
        for d in range(1, N_DEV):
            acc = acc + all_ref[pl.ds(d * r, r), :]
        o_ref[...] = acc

    body, xs, xa = _after(body, 1, after)
    return pl.pallas_call(
        body, name=name, in_specs=[VMEM_SPEC] + xs, out_specs=[VMEM_SPEC, VMEM_SPEC],
        out_shape=[jax.ShapeDtypeStruct((N_DEV * r, LANES), F32), jax.ShapeDtypeStruct((r, LANES), F32)],
        scratch_shapes=[pltpu.SemaphoreType.DMA((7,)), pltpu.SemaphoreType.DMA((7,)), pltpu.SemaphoreType.DMA],
        compiler_params=pltpu.CompilerParams(vmem_limit_bytes=VMEM_LIMIT),
    )(v, *xa)[1]


def _swap_sibling(vs, *, name):
    n = len(vs)

    def body(*refs):
        src, dst, send_sems, recv_sems = refs[:n], refs[n:2 * n], refs[2 * n], refs[2 * n + 1]
        x, y, c = _place()
        cps = [pltpu.make_async_remote_copy(src_ref=src[i], dst_ref=dst[i], send_sem=send_sems.at[i],
                                            recv_sem=recv_sems.at[i], device_id=(x, y, 1 - c), device_id_type=MESH)
               for i in range(n)]
        for cp in cps:
            cp.start()
        for cp in cps:
            cp.wait()

    return pl.pallas_call(
        body, name=name, in_specs=[HBM_SPEC] * n, out_specs=[HBM_SPEC] * n,
        out_shape=[jax.ShapeDtypeStruct(v.shape, v.dtype) for v in vs],
        scratch_shapes=[pltpu.SemaphoreType.DMA((n,)), pltpu.SemaphoreType.DMA((n,))],
    )(*vs)


def _half(ref, j, c, half):
    return ref.at[j, pl.ds(c * half, half), :]


def _gather_start(shard, after, *, name):
    R, Cc = shard.shape
    half = R // 2

    def body(src, land, send, recv, src_out, land_out, token):
        x, y, c = _place()
        for k, (px, py) in enumerate(_other_chips(x, y)):
            pltpu.make_async_remote_copy(src_ref=src.at[pl.ds(c * half, half), :], dst_ref=_half(land, 2 * x + y, c, half),
                                         send_sem=send.at[k], recv_sem=recv.at[k], device_id=(px, py, c),
                                         device_id_type=MESH).start()
        token[...] = jnp.zeros_like(token)

    land = lax.empty((N_CHIPS, R, Cc), shard.dtype)
    body, xs, xa = _after(body, 2, after)
    out = pl.pallas_call(
        body, name=name, in_specs=[HBM_SPEC, HBM_SPEC] + xs,
        out_specs=[SEM_SPEC, SEM_SPEC, HBM_SPEC, HBM_SPEC, VMEM_SPEC],
        out_shape=[pltpu.SemaphoreType.DMA((3,)), pltpu.SemaphoreType.DMA((3,)), pltpu.HBM(shard.shape, shard.dtype),
                   pltpu.HBM(land.shape, land.dtype), jax.ShapeDtypeStruct((SUBLANES, LANES), F32)],
        input_output_aliases={0: 2, 1: 3},
        compiler_params=pltpu.CompilerParams(has_side_effects=DATAFLOW),
    )(_in_hbm(shard), _in_hbm(land), *xa)
    return out[:4], out[4]


def _gather_wait(handle, after, *, name):
    send_sems, recv_sems, src, land = handle
    half = src.shape[0] // 2

    def body(src_ref, land_ref, send_ref, recv_ref, after_ref, src_out, land_out):
        x, y, c = _place()
        for k, (px, py) in enumerate(_other_chips(x, y)):
            cp = pltpu.make_async_remote_copy(src_ref=src_ref.at[pl.ds(c * half, half), :],
                                              dst_ref=_half(land_ref, 2 * px + py, c, half), send_sem=send_ref.at[k],
                                              recv_sem=recv_ref.at[k], device_id=(px, py, c), device_id_type=MESH)
            cp.wait_send()
            cp.wait_recv()

    return pl.pallas_call(
        body, name=name, in_specs=[HBM_SPEC, HBM_SPEC, SEM_SPEC, SEM_SPEC, ANY_SPEC], out_specs=[HBM_SPEC, HBM_SPEC],
        out_shape=[pltpu.HBM(src.shape, src.dtype), pltpu.HBM(land.shape, land.dtype)],
        input_output_aliases={0: 0, 1: 1},
        compiler_params=pltpu.CompilerParams(has_side_effects=DATAFLOW),
    )(src, land, send_sems, recv_sems, after)[1]


def _fill_sibling(land, *, name):
    _, R, Cc = land.shape
    half = R // 2

    def body(in_ref, o_ref, send_sems, recv_sems):
        x, y, c = _place()
        chips = _other_chips(x, y)
        cps = [pltpu.make_async_remote_copy(src_ref=_half(in_ref, 2 * px + py, c, half),
                                            dst_ref=_half(o_ref, 2 * px + py, c, half), send_sem=send_sems.at[k],
                                            recv_sem=recv_sems.at[k], device_id=(x, y, 1 - c), device_id_type=MESH)
               for k, (px, py) in enumerate(chips)]
        for cp in cps:
            cp.start()
        for k, (px, py) in enumerate(chips):
            pltpu.make_async_remote_copy(src_ref=_half(in_ref, 2 * px + py, 1 - c, half),
                                         dst_ref=_half(o_ref, 2 * px + py, 1 - c, half), send_sem=send_sems.at[k],
                                         recv_sem=recv_sems.at[k], device_id=(x, y, 1 - c), device_id_type=MESH).wait_recv()
        for cp in cps:
            cp.wait_send()

    return pl.pallas_call(
        body, name=name, in_specs=[HBM_SPEC], out_specs=HBM_SPEC, out_shape=jax.ShapeDtypeStruct(land.shape, land.dtype),
        scratch_shapes=[pltpu.SemaphoreType.DMA((3,)), pltpu.SemaphoreType.DMA((3,))],
        input_output_aliases={0: 0},
    )(land)


def _scatter_copies(src, land, send, recv):
    x, y, c = _place()
    return [pltpu.make_async_remote_copy(src_ref=src[i].at[2 * px + py], dst_ref=land[i].at[k], send_sem=send.at[3 * i + k],
                                         recv_sem=recv.at[3 * i + k], device_id=(px, py, c), device_id_type=MESH)
            for i in range(len(src)) for k, (px, py) in enumerate(_other_chips(x, y))]


def _scatter_start(pieces, *, name):
    n = len(pieces)

    def body(*refs):
        src, land, send, recv, token = refs[:n], refs[n:2 * n], refs[2 * n], refs[2 * n + 1], refs[-1]
        for cp in _scatter_copies(src, land, send, recv):
            cp.start()
        token[...] = jnp.zeros_like(token)

    lands = [lax.empty((3,) + p.shape[1:], p.dtype) for p in pieces]
    sems = pltpu.SemaphoreType.DMA((3 * n,))
    out = pl.pallas_call(
        body, name=name, in_specs=[HBM_SPEC] * (2 * n),
        out_specs=[SEM_SPEC, SEM_SPEC] + [HBM_SPEC] * (2 * n) + [VMEM_SPEC],
        out_shape=[sems, sems] + [pltpu.HBM(a.shape, a.dtype) for a in pieces + lands]
        + [jax.ShapeDtypeStruct((SUBLANES, LANES), F32)],
        input_output_aliases={i: 2 + i for i in range(2 * n)},
        compiler_params=pltpu.CompilerParams(has_side_effects=DATAFLOW),
    )(*[_in_hbm(a) for a in pieces + lands])
    return (out[0], out[1], out[2:2 + n], out[2 + n:2 + 2 * n]), out[-1]


def _scatter_wait(handle, after, *, name):
    send_sems, recv_sems, srcs, lands = handle
    n = len(srcs)

    def body(*refs):
        src, land, send, recv = refs[:n], refs[n:2 * n], refs[2 * n], refs[2 * n + 1]
        for cp in _scatter_copies(src, land, send, recv):
            cp.wait_send()
            cp.wait_recv()

    both = list(srcs) + list(lands)
    out = pl.pallas_call(
        body, name=name, in_specs=[HBM_SPEC] * (2 * n) + [SEM_SPEC, SEM_SPEC, ANY_SPEC], out_specs=[HBM_SPEC] * (2 * n),
        out_shape=[pltpu.HBM(a.shape, a.dtype) for a in both],
        input_output_aliases={i: i for i in range(2 * n)},
        compiler_params=pltpu.CompilerParams(has_side_effects=DATAFLOW),
    )(*both, send_sems, recv_sems, after)
    return out[n:]


def _chip_sum(pieces, got, chip, *, name):
    _, R, Cc = pieces.shape
    tr = _tile(R, 256, SUBLANES)

    def body(chip_ref, a_ref, g_ref, o_ref):
        o_ref[...] = ((a_ref[...] + g_ref[0].astype(F32)) + g_ref[1].astype(F32)) + g_ref[2].astype(F32)

    return pl.pallas_call(
        body, name=name,
        grid_spec=pltpu.PrefetchScalarGridSpec(
            num_scalar_prefetch=1, grid=(R // tr,),
            in_specs=[pl.BlockSpec((None, tr, Cc), lambda i, ch: (ch[0], i, 0)),
                      pl.BlockSpec((3, tr, Cc), lambda i, ch: (0, i, 0))],
            out_specs=pl.BlockSpec((tr, Cc), lambda i, ch: (i, 0))),
        out_shape=jax.ShapeDtypeStruct((R, Cc), F32),
        compiler_params=_params(("parallel",)),
    )(chip, pieces, got)


PACK_COLS = 1024


def _pack_rows(parts):
    return jnp.concatenate([p.reshape(-1, PACK_COLS) for p in parts], axis=0)


def _unpack_rows(block, shapes):
    lead = block.shape[:-2]
    out, off = [], 0
    for s in shapes:
        r = int(np.prod(s)) // PACK_COLS
        out.append(block[..., off:off + r, :].reshape(lead + tuple(s)))
        off += r
    assert off == block.shape[-2]
    return out


def _flat128(parts):
    out = []
    for p in parts:
        v = p.reshape(-1)
        pad = (-v.shape[0]) % LANES
        out.append(jnp.pad(v, (0, pad)) if pad else v)
    v = jnp.concatenate(out)
    pad = (-v.shape[0]) % (SUBLANES * LANES)
    if pad:
        v = jnp.pad(v, (0, pad))
    return v.reshape(-1, LANES)


def _unflat128(block, shapes):
    v = block.reshape(-1)
    out, off = [], 0
    for s in shapes:
        n = int(np.prod(s))
        out.append(v[off:off + n].reshape(s))
        off += n + ((-n) % LANES)
    return out


def kernel(x, hgrn_w_in, hgrn_lb_logits, hgrn_gnorm_w, hgrn_w_out, swa_w_q, swa_sinks, swa_w_out, shared_w_kv, rel_bias, ffn_w_in, ffn_conv_w, ffn_conv_b, ffn_w_out, ln_mix_g, ln_mix_b, ln_ffn_g, ln_ffn_b, loss_target, m_hgrn_w_in, m_hgrn_lb_logits, m_hgrn_gnorm_w, m_hgrn_w_out, m_swa_w_q, m_swa_sinks, m_swa_w_out, m_shared_w_kv, m_rel_bias, m_ffn_w_in, m_ffn_conv_w, m_ffn_conv_b, m_ffn_w_out, m_ln_mix_g, m_ln_mix_b, m_ln_ffn_g, m_ln_ffn_b, v_hgrn_w_in, v_hgrn_lb_logits, v_hgrn_gnorm_w, v_hgrn_w_out, v_swa_w_q, v_swa_sinks, v_swa_w_out, v_shared_w_kv, v_rel_bias, v_ffn_w_in, v_ffn_conv_w, v_ffn_conv_b, v_ffn_w_out, v_ln_mix_g, v_ln_mix_b, v_ln_ffn_g, v_ln_ffn_b):
    xi, yi, ci = _place()
    chip = 2 * xi + yi
    Dm = D_MODEL
    FC = 2 * FFN_DIM // N_CHIPS
    Fo = FFN_DIM // N_CHIPS
    Dq = Dm // N_CHIPS
    bf = lambda a: a.astype(BF16)

    shard0 = _pack_rows([bf(hgrn_w_in), bf(hgrn_w_out)])
    shard1 = _pack_rows([bf(swa_w_q), bf(swa_w_out), bf(shared_w_kv), bf(ffn_w_in[0]), bf(ffn_w_out[0])])
    shard2 = _pack_rows([bf(ffn_w_in[1]), bf(ffn_w_out[1])])
    handle0, token0 = _gather_start(shard0, None, name="gather_w0_start")

    lb_full = lax.dynamic_update_slice(jnp.zeros((2, Dm), F32), hgrn_lb_logits, (0, chip * Dq))
    cw_full = lax.dynamic_update_slice(jnp.zeros((DEPTH, 3, 2 * FFN_DIM), F32), ffn_conv_w, (0, 0, chip * FC))
    only_south = (ci == 0).astype(F32)
    small_in = _sum8(_flat128([lb_full, cw_full]) * only_south, name="gather_small", after=token0)
    lb_full, cw_full = _unflat128(small_in, [(2, Dm), (DEPTH, 3, 2 * FFN_DIM)])

    land0 = _fill_sibling(_gather_wait(handle0, small_in, name="gather_w0_wait"), name="gather_w0_fill")
    all0 = lax.dynamic_update_slice(land0, shard0[None], (chip, 0, 0))
    handle1, token1 = _gather_start(shard1, land0, name="gather_w1_start")
    w_in, w_hg_out = _unpack_rows(all0, [(Dm, Dm), (Dq, Dm)])

    def ffn_weights(w_fi, w_fo, l):
        halves = jnp.stack([jnp.concatenate([w_fi[0], w_fi[1]], axis=1), jnp.concatenate([w_fi[2], w_fi[3]], axis=1)])
        return {"ffn_in": {l: halves}, "ffn_out": {l: w_fo.reshape(FFN_DIM, Dm)}}

    got = {}

    def more_weights(k, after):
        shard = (shard1, shard2)[k - 1]
        land = _gather_wait(got.pop("handle"), after, name=f"gather_w{k}_wait")
        land = _fill_sibling(land, name=f"gather_w{k}_fill")
        allk = lax.dynamic_update_slice(land, shard[None], (chip, 0, 0))
        if k == 1:
            got["handle"], token2 = _gather_start(shard2, land, name="gather_w2_start")
            w_q, w_o, w_kv, w_fi, w_fo = _unpack_rows(allk, [(Dq, Dm), (Dq, Dm), (Dq, 2 * KV_DIM), (Dm, FC), (Fo, Dm)])
            got.update(ffn_weights(w_fi, w_fo, 0))
            return {"sw_q": w_q.reshape(Dm, Dm), "sw_out": w_o.reshape(Dm, Dm), "kv": w_kv.reshape(Dm, 2 * KV_DIM),
                    "token": token2, **{n: got[n] for n in ("ffn_in", "ffn_out")}}
        w_fi, w_fo = _unpack_rows(allk, [(Dm, FC), (Fo, Dm)])
        new = ffn_weights(w_fi, w_fo, 1)
        return {n: {**got[n], **new[n]} for n in new}

    got["handle"] = handle1

    w = {
        "hg_in": w_in, "hg_out": w_hg_out.reshape(Dm, Dm), "token": token1,
        "lb_logits": lb_full, "gnorm": hgrn_gnorm_w, "sinks": swa_sinks, "rel_bias": rel_bias,
        "conv_w_a": [cw_full[l, :, :FFN_DIM] for l in range(DEPTH)],
        "conv_w_b": [cw_full[l, :, FFN_DIM:] for l in range(DEPTH)],
        "conv_b_a": [ffn_conv_b[l:l + 1, :FFN_DIM] for l in range(DEPTH)],
        "conv_b_b": [ffn_conv_b[l:l + 1, FFN_DIM:] for l in range(DEPTH)],
        "ln_mix_g": [ln_mix_g[l:l + 1] for l in range(DEPTH)], "ln_mix_b": [ln_mix_b[l:l + 1] for l in range(DEPTH)],
        "ln_ffn_g": [ln_ffn_g[l:l + 1] for l in range(DEPTH)], "ln_ffn_b": [ln_ffn_b[l:l + 1] for l in range(DEPTH)],
    }

    sent = {}

    def ffn_pieces(gd):
        return [gd["ffn_in"], gd["ffn_out"].reshape(N_CHIPS, Fo, Dm)]

    def emit(k, gd):
        rows4 = lambda a: a.reshape(N_CHIPS, Dq, a.shape[-1])
        if k == 1:
            pieces = [rows4(gd["sw_q"]), rows4(gd["sw_out"]), rows4(gd["kv"])] + ffn_pieces(gd)
        elif k == 2:
            pieces = ffn_pieces(gd) + [rows4(gd["hg_out"])]
        else:
            pieces = [gd["hg_in"]]
        handle, token = _scatter_start([p.astype(BF16) for p in pieces], name=f"scatter_g{k}_start")
        sent[k] = (handle, pieces)
        return token

    loss_tile, grad_x, g = _local_step(x[0], loss_target[0], w, more_weights, emit)

    wts = dict(hgrn_w_in=hgrn_w_in, hgrn_lb_logits=hgrn_lb_logits, hgrn_gnorm_w=hgrn_gnorm_w, hgrn_w_out=hgrn_w_out,
               swa_w_q=swa_w_q, swa_sinks=swa_sinks, swa_w_out=swa_w_out, shared_w_kv=shared_w_kv, rel_bias=rel_bias,
               ffn_w_in=ffn_w_in, ffn_conv_w=ffn_conv_w, ffn_conv_b=ffn_conv_b, ffn_w_out=ffn_w_out,
               ln_mix_g=ln_mix_g, ln_mix_b=ln_mix_b, ln_ffn_g=ln_ffn_g, ln_ffn_b=ln_ffn_b)
    ms = dict(hgrn_w_in=m_hgrn_w_in, hgrn_lb_logits=m_hgrn_lb_logits, hgrn_gnorm_w=m_hgrn_gnorm_w, hgrn_w_out=m_hgrn_w_out,
              swa_w_q=m_swa_w_q, swa_sinks=m_swa_sinks, swa_w_out=m_swa_w_out, shared_w_kv=m_shared_w_kv, rel_bias=m_rel_bias,
              ffn_w_in=m_ffn_w_in, ffn_conv_w=m_ffn_conv_w, ffn_conv_b=m_ffn_conv_b, ffn_w_out=m_ffn_w_out,
              ln_mix_g=m_ln_mix_g, ln_mix_b=m_ln_mix_b, ln_ffn_g=m_ln_ffn_g, ln_ffn_b=m_ln_ffn_b)
    vs = dict(hgrn_w_in=v_hgrn_w_in, hgrn_lb_logits=v_hgrn_lb_logits, hgrn_gnorm_w=v_hgrn_gnorm_w, hgrn_w_out=v_hgrn_w_out,
              swa_w_q=v_swa_w_q, swa_sinks=v_swa_sinks, swa_w_out=v_swa_w_out, shared_w_kv=v_shared_w_kv, rel_bias=v_rel_bias,
              ffn_w_in=v_ffn_w_in, ffn_conv_w=v_ffn_conv_w, ffn_conv_b=v_ffn_conv_b, ffn_w_out=v_ffn_w_out,
              ln_mix_g=v_ln_mix_g, ln_mix_b=v_ln_mix_b, ln_ffn_g=v_ln_ffn_g, ln_ffn_b=v_ln_ffn_b)
    names = list(wts)
    grads, delta, new_m, new_v = {}, {}, {}, {}

    def update(n, ga, gb, layer=None, prev=None):
        r2 = lambda a: a.reshape(-1, a.shape[-1])
        rows = None if layer is None else (layer * ga.shape[0], ga.shape[0])
        return _adamw(r2(wts[n]), ga, gb, r2(ms[n]), r2(vs[n]), rows=rows, prev=prev,
                      name=f"adamw_{n}" + ("" if layer is None else f"_{layer}"))

    def keep(n, res):
        grads[n], delta[n], new_m[n], new_v[n] = [a.reshape(wts[n].shape) for a in res]

    chip1 = jnp.reshape(chip, (1,)).astype(jnp.int32)
    after = grad_x
    for k in (1, 2, 3):
        handle, pieces = sent[k]
        lands = _scatter_wait(handle, after, name=f"scatter_g{k}_wait")
        parts = [_chip_sum(p, l, chip1, name=f"scatter_g{k}_sum{i}") for i, (p, l) in enumerate(zip(pieces, lands))]
        sibs = _swap_sibling(parts, name=f"scatter_g{k}_swap")
        if k == 1:
            for n, ga, gb in zip(["swa_w_q", "swa_w_out", "shared_w_kv"], parts[:3], sibs[:3]):
                keep(n, update(n, ga, gb))
            ffn_in_1 = update("ffn_w_in", parts[3], sibs[3], layer=1)
            ffn_out_1 = update("ffn_w_out", parts[4], sibs[4], layer=1)
            after = ffn_out_1[3]
        elif k == 2:
            keep("ffn_w_in", update("ffn_w_in", parts[0], sibs[0], layer=0, prev=ffn_in_1))
            keep("ffn_w_out", update("ffn_w_out", parts[1], sibs[1], layer=0, prev=ffn_out_1))
            keep("hgrn_w_out", update("hgrn_w_out", parts[2], sibs[2]))
            after = new_v["hgrn_w_out"]
        else:
            keep("hgrn_w_in", update("hgrn_w_in", parts[0], sibs[0]))

    small_shapes = [(SUBLANES, LANES), (2, Dm), (1, HG_DIM), (1, SW_Q_HEADS), (REL_BUCKETS, SW_Q_HEADS),
                    (DEPTH, 3, 2 * FFN_DIM), (DEPTH, 2 * FFN_DIM)] + [(DEPTH, Dm)] * 4
    gc = g["conv"]
    conv_w_g = jnp.stack([jnp.concatenate([gc[l]["conv_w_a"], gc[l]["conv_w_b"]], axis=1) for l in range(DEPTH)])
    conv_b_g = jnp.concatenate([jnp.concatenate([gc[l]["conv_b_a"], gc[l]["conv_b_b"]], axis=1) for l in range(DEPTH)], axis=0)
    ln_g = [jnp.concatenate([g[f"{n}0"], g[f"{n}1"]], axis=0) for n in ("ln_mix_g", "ln_mix_b", "ln_ffn_g", "ln_ffn_b")]
    small_out = _sum8(_flat128([loss_tile, g["lb_logits"], g["gnorm"], g["sinks"], g["rel_bias"], conv_w_g, conv_b_g] + ln_g),
                      name="sum_small")
    (loss_t, g_lb, g_gn, g_sinks, g_rel, g_cw, g_cb, g_lmg, g_lmb, g_lfg, g_lfb) = _unflat128(small_out, small_shapes)
    loss = loss_t[0, 0]
    g_lb = lax.dynamic_slice_in_dim(g_lb, chip * Dq, Dq, axis=1)
    g_cw = lax.dynamic_slice_in_dim(g_cw, chip * FC, FC, axis=2)
    small_g = dict(hgrn_lb_logits=g_lb, hgrn_gnorm_w=g_gn, swa_sinks=g_sinks, rel_bias=g_rel, ffn_conv_w=g_cw,
                   ffn_conv_b=g_cb, ln_mix_g=g_lmg, ln_mix_b=g_lmb, ln_ffn_g=g_lfg, ln_ffn_b=g_lfb)
    small_names = list(small_g)
    sshapes = [wts[n].shape for n in small_names]
    _, d_, m_, v_ = _adamw(_flat128([wts[n] for n in small_names]), _flat128([small_g[n] for n in small_names]), None,
                           _flat128([ms[n] for n in small_names]), _flat128([vs[n] for n in small_names]), name="adamw_small")
    for n, a, b_, c_ in zip(small_names, _unflat128(d_, sshapes), _unflat128(m_, sshapes), _unflat128(v_, sshapes)):
        grads[n], delta[n], new_m[n], new_v[n] = small_g[n], a, b_, c_

    return (loss, grad_x[None], *[grads[n] for n in names], *[delta[n] for n in names],
            *[new_m[n] for n in names], *[new_v[n] for n in names])
```

```python
import math

import numpy as np
import jax
import jax.numpy as jnp
from jax import lax
from jax.experimental import pallas as pl
from jax.experimental.pallas import tpu as pltpu

F32 = jnp.float32
BF16 = jnp.bfloat16
MESH = pl.DeviceIdType.MESH

D_MODEL = 1024
DEPTH = 2
HG_HEADS = 8
HG_DIM = 128
SW_Q_HEADS = 16
SW_KV_HEADS = 4
SW_HEAD_DIM = 64
SW_GROUP = 4
SW_WINDOW = 128
REL_BUCKETS = 32
REL_MAX_DIST = 128
FFN_DIM = 2816
ALPHA = (2.0 * DEPTH) ** 0.25
LN_EPS = 1e-5
RMS_EPS = 1e-6
ADAM_LR = 0.001
ADAM_B1 = 0.9
ADAM_B2 = 0.999
ADAM_EPS = 1e-08
ADAM_WD = 0.01
ADAM_STEP = 10

VMEM_BYTES_V7X = 64 * 1024 * 1024
VMEM_LIMIT = VMEM_BYTES_V7X - 8 * 1024 * 1024
LANES = 128
SUBLANES = 8

HG_C = 64
HG_RB = 256
ROW_TILE = 256
CONV_R = 128
N_CHIPS = 4
N_DEV = 8

ANY_SPEC = pl.BlockSpec(memory_space=pl.ANY)


def _after(body, n_in, after):
    if after is None:
        return body, [], ()

    def wrapped(*refs):
        return body(*refs[:n_in], *refs[n_in + 1:])

    return wrapped, [ANY_SPEC], (after,)


def _params(sem=None):
    return pltpu.CompilerParams(dimension_semantics=sem, vmem_limit_bytes=VMEM_LIMIT)


def _tile(n, pref, unit=LANES):
    if n <= pref:
        return n
    best = None
    for t in range(unit, pref + 1, unit):
        if n % t == 0:
            best = t
    assert best is not None, (n, pref, unit)
    return best


def _dot(a, b, ca, cb):
    nb = a.ndim - 2
    batch = tuple(range(nb))
    return lax.dot_general(a.astype(BF16), b.astype(BF16), (((nb + ca,), (nb + cb,)), (batch, batch)),
                           preferred_element_type=F32)


@jax.custom_vjp
def mm(a, b):
    return _dot(a, b, 1, 0)


@jax.custom_vjp
def mm_nt(a, b):
    return _dot(a, b, 1, 1)


@jax.custom_vjp
def mm_tn(a, b):
    return _dot(a, b, 0, 0)


mm.defvjp(lambda a, b: (mm(a, b), (a, b)), lambda r, ct: (mm_nt(ct, r[1]), mm_tn(r[0], ct)))
mm_nt.defvjp(lambda a, b: (mm_nt(a, b), (a, b)), lambda r, ct: (mm(ct, r[1]), mm_tn(ct, r[0])))
mm_tn.defvjp(lambda a, b: (mm_tn(a, b), (a, b)), lambda r, ct: (mm_nt(r[1], ct), mm(r[0], ct)))


def _split2(x):
    hi = x.astype(BF16)
    return hi, (x - hi.astype(F32)).astype(BF16)


@jax.custom_vjp
def _scores(qt, kt):
    return _dot(qt, kt, 1, 1)


def _scores_bwd(r, ct):
    (qh, ql), (kh, kl) = _split2(r[0]), _split2(r[1])
    return _dot(ct, kh, 1, 0) + _dot(ct, kl, 1, 0), _dot(ct, qh, 0, 0) + _dot(ct, ql, 0, 0)


_scores.defvjp(lambda a, b: (_scores(a, b), (a, b)), _scores_bwd)


def _split3(x):
    hi = x.astype(BF16)
    r1 = x - hi.astype(F32)
    mid = r1.astype(BF16)
    lo = (r1 - mid.astype(F32)).astype(BF16)
    return hi, mid, lo


def _cumsum_impl(x):
    ax = x.ndim - 2
    n = x.shape[ax]
    row = lax.broadcasted_iota(jnp.int32, x.shape, ax)
    d = 1
    while d < n:
        x = x + jnp.where(row >= d, pltpu.roll(x, d, ax), 0.0)
        d *= 2
    return x


def _cumsum_rev_impl(x):
    ax = x.ndim - 2
    n = x.shape[ax]
    row = lax.broadcasted_iota(jnp.int32, x.shape, ax)
    d = 1
    while d < n:
        x = x + jnp.where(row < n - d, pltpu.roll(x, n - d, ax), 0.0)
        d *= 2
    return x


@jax.custom_vjp
def _cumsum(x):
    return _cumsum_impl(x)


_cumsum.defvjp(lambda x: (_cumsum_impl(x), None), lambda _, ct: (_cumsum_rev_impl(ct),))


def _matmul(a, b, *, mode, name, out_dtype=F32, add=None, add_scale=1.0, tm=512, tn=1408, tk=1408, after=None,
            split_n=False, planes=None):
    P = b.shape[0] if planes else 1
    a2, b2 = a.shape[-2:], b.shape[-2:]
    (M, K) = a2 if mode[0] == "n" else a2[::-1]
    (K2, N) = b2 if mode[1] == "n" else b2[::-1]
    assert K == K2, (a.shape, b.shape, mode)
    assert a.ndim == (3 if planes == "k" else 2) and b.ndim == (3 if planes else 2)
    tm, tn, tk = _tile(M, tm), _tile(N, tn), _tile(K, tk)
    nj, nkp = N // tn, K // tk
    nk = nkp * (P if planes == "k" else 1)
    ca, cb = (1 if mode[0] == "n" else 0), (0 if mode[1] == "n" else 1)
    a_blk, a_idx = ((tk, tm), lambda i, k: (k, i)) if mode[0] == "t" else ((tm, tk), lambda i, k: (i, k))
    b_blk, b_idx = ((tn, tk), lambda k, j: (j, k)) if mode[1] == "t" else ((tk, tn), lambda k, j: (k, j))
    if planes == "k":
        a_spec = pl.BlockSpec((None,) + a_blk, lambda i, j, k: (k // nkp,) + a_idx(i, k % nkp))
        b_spec = pl.BlockSpec((None,) + b_blk, lambda i, j, k: (k // nkp,) + b_idx(k % nkp, j))
    else:
        a_spec = pl.BlockSpec(a_blk, lambda i, j, k: a_idx(i, k))
        b_spec = (pl.BlockSpec((None,) + b_blk, lambda i, j, k: (j // nj,) + b_idx(k, j % nj)) if planes == "n"
                  else pl.BlockSpec(b_blk, lambda i, j, k: b_idx(k, j)))
    if split_n:
        o_spec, out_shape = pl.BlockSpec((None, tm, tn), lambda i, j, k: (j, i, 0)), (P * nj if planes == "n" else nj, M, tn)
    elif planes == "n":
        o_spec, out_shape = pl.BlockSpec((None, tm, tn), lambda i, j, k: (j // nj, i, j % nj)), (P, M, N)
    else:
        o_spec, out_shape = pl.BlockSpec((tm, tn), lambda i, j, k: (i, j)), (M, N)
    has_add = add is not None
    assert not (has_add and (split_n or planes == "n"))

    def finish(r, add_ref, o_ref):
        if has_add:
            r = r + add_scale * add_ref[...]
        o_ref[...] = r.astype(out_dtype)

    def body(*refs):
        a_ref, b_ref = refs[:2]
        add_ref = refs[2] if has_add else None
        o_ref = refs[3 if has_add else 2]
        if nk == 1:
            finish(_dot(a_ref[...], b_ref[...], ca, cb), add_ref, o_ref)
            return
        acc_ref = refs[-1]
        k = pl.program_id(2)

        @pl.when(k == 0)
        def _():
            acc_ref[...] = jnp.zeros_like(acc_ref)

        acc_ref[...] += _dot(a_ref[...], b_ref[...], ca, cb)

        @pl.when(k == nk - 1)
        def _():
            finish(acc_ref[...], add_ref, o_ref)

    in_specs = [a_spec, b_spec] + ([o_spec] if has_add else [])
    args = (a, b) + ((add,) if has_add else ())
    body, xs, xa = _after(body, len(args), after)
    in_specs, args = in_specs + xs, args + xa
    return pl.pallas_call(
        body, name=name, grid=(M // tm, nj * (P if planes == "n" else 1), nk), in_specs=in_specs, out_specs=o_spec,
        out_shape=jax.ShapeDtypeStruct(out_shape, out_dtype),
        scratch_shapes=[pltpu.VMEM((tm, tn), F32)] if nk > 1 else [],
        compiler_params=_params(("parallel", "parallel", "arbitrary")),
    )(*args)


def _ln(z, g, b):
    mu = jnp.mean(z, axis=-1, keepdims=True)
    zc = z - mu
    var = jnp.mean(zc * zc, axis=-1, keepdims=True)
    return zc * lax.rsqrt(var + LN_EPS) * g + b


def _matmul_ln(a, b, h, g, bias, *, name, tgt=None, tm=512, a_t=False):
    (T, K), (K2, Dm) = (a.shape[::-1] if a_t else a.shape), b.shape
    assert K == K2 and h.shape == (T, Dm)
    tm = _tile(T, tm, SUBLANES)
    last = tgt is not None

    def body(*refs):
        a_ref, b_ref, h_ref, g_ref, bias_ref = refs[:5]
        z = ALPHA * h_ref[...] + _dot(a_ref[...], b_ref[...], 0 if a_t else 1, 0)
        if not last:
            z_ref, y_ref, yb_ref = refs[5:]
            y = _ln(z, g_ref[...], bias_ref[...])
            z_ref[...] = z
            y_ref[...] = y
            yb_ref[...] = y.astype(BF16)
            return
        t_ref, dz_ref, dzb_ref, dg_ref, db_ref, l_ref, da_ref = refs[5:]

        @pl.when(pl.program_id(0) == 0)
        def _():
            dg_ref[...] = jnp.zeros_like(dg_ref)
            db_ref[...] = jnp.zeros_like(db_ref)
            l_ref[...] = jnp.zeros_like(l_ref)

        y, vjp = jax.vjp(_ln, z, g_ref[...], bias_ref[...])
        e = y - t_ref[...]
        dz, dg, db = vjp(e * (1.0 / Dm))
        l_ref[...] += 0.5 * jnp.sum(jnp.mean(e * e, axis=-1, keepdims=True), axis=0, keepdims=True)
        dzb = dz.astype(BF16)
        dz_ref[...] = dz
        dzb_ref[...] = dzb
        dg_ref[...] += dg
        db_ref[...] += db
        da_ref[...] = _dot(dzb, b_ref[...], 1, 1).astype(BF16)

    row = pl.BlockSpec((tm, Dm), lambda i: (i, 0))
    vec = pl.BlockSpec((1, Dm), lambda i: (0, 0))
    a_spec = pl.BlockSpec((K, tm), lambda i: (0, i)) if a_t else pl.BlockSpec((tm, K), lambda i: (i, 0))
    in_specs = [a_spec, pl.BlockSpec((K, Dm), lambda i: (0, 0)), row, vec, vec]
    f32, b16 = jax.ShapeDtypeStruct((T, Dm), F32), jax.ShapeDtypeStruct((T, Dm), BF16)
    if not last:
        return pl.pallas_call(
            body, name=name, grid=(T // tm,), in_specs=in_specs, out_specs=[row, row, row], out_shape=[f32, f32, b16],
            compiler_params=_params(("parallel",)),
        )(a, b, h, g, bias)
    assert not a_t
    return pl.pallas_call(
        body, name=name, grid=(T // tm,), in_specs=in_specs + [row],
        out_specs=[row, row, vec, vec, pl.BlockSpec((SUBLANES, LANES), lambda i: (0, 0)), a_spec],
        out_shape=[f32, b16, jax.ShapeDtypeStruct((1, Dm), F32), jax.ShapeDtypeStruct((1, Dm), F32),
                   jax.ShapeDtypeStruct((SUBLANES, LANES), F32), jax.ShapeDtypeStruct((T, K), BF16)],
        compiler_params=_params(("arbitrary",)),
    )(a, b, h, g, bias, tgt)


def _ln_bwd_matmul(dy, z, g, b, w, *, name, out_t=False, tm=512, after=None):
    T, Dm = z.shape
    N = w.shape[0]
    tm = _tile(T, tm, LANES if out_t else SUBLANES)

    def body(dy_ref, z_ref, g_ref, b_ref, w_ref, dz_ref, dzb_ref, dg_ref, db_ref, o_ref):
        @pl.when(pl.program_id(0) == 0)
        def _():
            dg_ref[...] = jnp.zeros_like(dg_ref)
            db_ref[...] = jnp.zeros_like(db_ref)

        _, vjp = jax.vjp(_ln, z_ref[...], g_ref[...], b_ref[...])
        dz, dg, db = vjp(dy_ref[...])
        dzb = dz.astype(BF16)
        dz_ref[...] = dz
        dzb_ref[...] = dzb
        dg_ref[...] += dg
        db_ref[...] += db
        o_ref[...] = (_dot(w_ref[...], dzb, 1, 1) if out_t else _dot(dzb, w_ref[...], 1, 1)).astype(BF16)

    row = pl.BlockSpec((tm, Dm), lambda i: (i, 0))
    vec = pl.BlockSpec((1, Dm), lambda i: (0, 0))
    o_spec = pl.BlockSpec((N, tm), lambda i: (0, i)) if out_t else pl.BlockSpec((tm, N), lambda i: (i, 0))
    body, xs, xa = _after(body, 5, after)
    return pl.pallas_call(
        body, name=name, grid=(T // tm,), in_specs=[row, row, vec, vec, pl.BlockSpec((N, Dm), lambda i: (0, 0))] + xs,
        out_specs=[row, row, vec, vec, o_spec],
        out_shape=[jax.ShapeDtypeStruct((T, Dm), F32), jax.ShapeDtypeStruct((T, Dm), BF16),
                   jax.ShapeDtypeStruct((1, Dm), F32), jax.ShapeDtypeStruct((1, Dm), F32),
                   jax.ShapeDtypeStruct((N, T) if out_t else (T, N), BF16)],
        compiler_params=_params(("arbitrary",)),
    )(dy, z, g, b, w, *xa)


def _hg_chunk(qr, fr, ir, gr, l0, l1, gw, st):
    C = qr.shape[-2]
    row = lax.broadcasted_iota(jnp.int32, qr.shape, qr.ndim - 2)
    lb = jax.nn.sigmoid(l0 - l1)
    fg = lb + (1.0 - lb) * jax.nn.sigmoid(fr)
    b = _cumsum(jnp.log(fg))
    q = jax.nn.silu(qr)
    k = 1.0 - fg
    bmid = lax.stop_gradient(jnp.sum(jnp.where(row == C // 2 - 1, b, 0.0), axis=-2, keepdims=True))
    bl = jnp.sum(jnp.where(row == C - 1, b, 0.0), axis=-2, keepdims=True)
    o = mm_nt(q * jnp.exp(b), st)
    sc = _scores(q * jnp.exp(b - bmid), k * jnp.exp(bmid - b))
    ti = lax.broadcasted_iota(jnp.int32, (C, C), 0)
    si = lax.broadcasted_iota(jnp.int32, (C, C), 1)
    sc = jnp.where(si <= ti, sc, 0.0)
    o = o + mm(sc, ir)
    st_new = st * jnp.exp(bl) + mm_tn(ir, k * jnp.exp(bl - b))
    on = o * lax.rsqrt(jnp.mean(o * o, axis=-1, keepdims=True) + RMS_EPS)
    return on * gw * jax.nn.silu(gr), st_new


def _heads(ref, rows):
    return jnp.stack([ref[rows, h * HG_DIM:(h + 1) * HG_DIM].astype(F32) for h in range(HG_HEADS)])


def _unheads(x):
    return jnp.concatenate([x[h] for h in range(HG_HEADS)], axis=-1)


def _hgrn_fwd(pre, lbl, gw, *, name):
    _, T, Dm = pre.shape
    rb = min(HG_RB, T)
    C = min(HG_C, rb)
    ncb = rb // C

    def body(pre_ref, lbl_ref, gw_ref, o_ref, st_ref, s_ref):
        @pl.when(pl.program_id(0) == 0)
        def _():
            s_ref[...] = jnp.zeros_like(s_ref)

        def chunk(ci, carry):
            r0 = pl.multiple_of(ci * C, C)
            rows = pl.ds(r0, C)
            st = s_ref[...]
            st_ref[ci] = st
            out, st_new = _hg_chunk(*[_heads(pre_ref.at[j], rows) for j in range(4)],
                                    _heads(lbl_ref, slice(0, 1)), _heads(lbl_ref, slice(1, 2)), gw_ref[...], st)
            o_ref[rows, :] = _unheads(out).astype(BF16)
            s_ref[...] = st_new
            return carry

        lax.fori_loop(0, ncb, chunk, 0, unroll=True)

    row = pl.BlockSpec((rb, Dm), lambda n: (n, 0))
    return pl.pallas_call(
        body, name=name, grid=(T // rb,),
        in_specs=[pl.BlockSpec((4, rb, Dm), lambda n: (0, n, 0)), pl.BlockSpec((2, Dm), lambda n: (0, 0)),
                  pl.BlockSpec((1, HG_DIM), lambda n: (0, 0))],
        out_specs=[row, pl.BlockSpec((ncb, HG_HEADS, HG_DIM, HG_DIM), lambda n: (n, 0, 0, 0))],
        out_shape=[jax.ShapeDtypeStruct((T, Dm), BF16),
                   jax.ShapeDtypeStruct((T // C, HG_HEADS, HG_DIM, HG_DIM), F32)],
        scratch_shapes=[pltpu.VMEM((HG_HEADS, HG_DIM, HG_DIM), F32)],
        compiler_params=_params(("arbitrary",)),
    )(pre, lbl, gw)


def _hgrn_bwd(pre, lbl, gw, states, dout, *, name, after=None):
    _, T, Dm = pre.shape
    rb = min(HG_RB, T)
    C = min(HG_C, rb)
    ncb = rb // C
    nb = T // rb

    def body(pre_ref, lbl_ref, gw_ref, st_ref, do_ref, dpre_ref, dlbl_ref, dgw_ref, ds_ref):
        @pl.when(pl.program_id(0) == 0)
        def _():
            ds_ref[...] = jnp.zeros_like(ds_ref)
            dlbl_ref[...] = jnp.zeros_like(dlbl_ref)
            dgw_ref[...] = jnp.zeros_like(dgw_ref)

        def chunk(cj, carry):
            ci = ncb - 1 - cj
            r0 = pl.multiple_of(ci * C, C)
            rows = pl.ds(r0, C)
            _, vjp = jax.vjp(_hg_chunk, *[_heads(pre_ref.at[j], rows) for j in range(4)],
                             _heads(lbl_ref, slice(0, 1)), _heads(lbl_ref, slice(1, 2)), gw_ref[...], st_ref[ci])
            *dpre, dl0, dl1, dgw, dst = vjp((_heads(do_ref, rows), ds_ref[...]))
            for j in range(4):
                dpre_ref[j, rows, :] = _unheads(dpre[j]).astype(BF16)
            dlbl_ref[0:1, :] += _unheads(dl0)
            dlbl_ref[1:2, :] += _unheads(dl1)
            dgw_ref[...] += dgw
            ds_ref[...] = dst
            return carry

        lax.fori_loop(0, ncb, chunk, 0, unroll=True)

    row = pl.BlockSpec((rb, Dm), lambda n: (nb - 1 - n, 0))
    lsp = pl.BlockSpec((2, Dm), lambda n: (0, 0))
    gsp = pl.BlockSpec((1, HG_DIM), lambda n: (0, 0))
    pre_spec = pl.BlockSpec((4, rb, Dm), lambda n: (0, nb - 1 - n, 0))
    body, xs, xa = _after(body, 5, after)
    return pl.pallas_call(
        body, name=name, grid=(nb,),
        in_specs=[pre_spec, lsp, gsp, pl.BlockSpec((ncb, HG_HEADS, HG_DIM, HG_DIM), lambda n: (nb - 1 - n, 0, 0, 0)), row] + xs,
        out_specs=[pre_spec, lsp, gsp],
        out_shape=[jax.ShapeDtypeStruct((4, T, Dm), BF16), jax.ShapeDtypeStruct((2, Dm), F32),
                   jax.ShapeDtypeStruct((1, HG_DIM), F32)],
        scratch_shapes=[pltpu.VMEM((HG_HEADS, HG_DIM, HG_DIM), F32)],
        compiler_params=_params(("arbitrary",)),
    )(pre, lbl, gw, states, dout, *xa)


CONV_HALO = 2 * SUBLANES


def _conv_rows(u_ref, scr, w, bias, r0, R):
    cur = u_ref[pl.ds(r0, R), :].astype(F32)
    p0 = pl.multiple_of(jnp.maximum(r0 - CONV_HALO, 0), CONV_HALO)
    scr[0:CONV_HALO, :] = jnp.where(r0 > 0, u_ref[pl.ds(p0, CONV_HALO), :].astype(F32), 0.0)
    scr[CONV_HALO:CONV_HALO + R, :] = cur
    s1 = scr[CONV_HALO - 1:CONV_HALO - 1 + R, :]
    s2 = scr[CONV_HALO - 2:CONV_HALO - 2 + R, :]
    return w[0:1, :] * s2 + w[1:2, :] * s1 + w[2:3, :] * cur + bias, cur, s1, s2


def _conv_gate_fwd(u, wa, wb, ba, bb, *, name):
    _, T, Fd = u.shape
    R = min(CONV_R, T)
    tc = LANES

    def body(u_ref, wa_ref, wb_ref, ba_ref, bb_ref, o_ref, sa, sb):
        wa_, wb_, ba_, bb_ = wa_ref[...], wb_ref[...], ba_ref[...], bb_ref[...]

        def step(ri, carry):
            r0 = pl.multiple_of(ri * R, R)
            ca = _conv_rows(u_ref.at[0], sa, wa_, ba_, r0, R)[0]
            cb = _conv_rows(u_ref.at[1], sb, wb_, bb_, r0, R)[0]
            o_ref[pl.ds(r0, R), :] = (jax.nn.silu(ca) * cb).astype(BF16)
            return carry

        lax.fori_loop(0, T // R, step, 0)

    col = pl.BlockSpec((T, tc), lambda j: (0, j))
    wsp = pl.BlockSpec((3, tc), lambda j: (0, j))
    bsp = pl.BlockSpec((1, tc), lambda j: (0, j))
    both = pl.BlockSpec((2, T, tc), lambda j: (0, 0, j))
    return pl.pallas_call(
        body, name=name, grid=(Fd // tc,), in_specs=[both, wsp, wsp, bsp, bsp], out_specs=col,
        out_shape=jax.ShapeDtypeStruct((T, Fd), BF16),
        scratch_shapes=[pltpu.VMEM((CONV_HALO + R, tc), F32)] * 2,
        compiler_params=_params(("parallel",)),
    )(u, wa, wb, ba, bb)


def _conv_gate_bwd(u, wa, wb, ba, bb, dact, *, name):
    _, T, Fd = u.shape
    R = min(CONV_R, T)
    nr = T // R
    tc = LANES

    def body(u_ref, wa_ref, wb_ref, ba_ref, bb_ref, da_ref,
             du_ref, dwa_ref, dwb_ref, dba_ref, dbb_ref, sa, sb, sda, sdb):
        wa_, wb_, ba_, bb_ = wa_ref[...], wb_ref[...], ba_ref[...], bb_ref[...]
        sda[R:R + SUBLANES, :] = jnp.zeros((SUBLANES, tc), F32)
        sdb[R:R + SUBLANES, :] = jnp.zeros((SUBLANES, tc), F32)

        def taps(dc, cur, s1, s2):
            return jnp.concatenate([jnp.sum(dc * s2, axis=0, keepdims=True), jnp.sum(dc * s1, axis=0, keepdims=True),
                                    jnp.sum(dc * cur, axis=0, keepdims=True)], axis=0)

        def du_rows(sd, dc, w):
            sd[0:R, :] = dc
            du = w[2:3, :] * dc + w[1:2, :] * sd[1:1 + R, :] + w[0:1, :] * sd[2:2 + R, :]
            sd[R:R + SUBLANES, :] = dc[0:SUBLANES]
            return du

        def step(rj, carry):
            dwa, dwb, dba, dbb = carry
            r0 = pl.multiple_of((nr - 1 - rj) * R, R)
            ca, cura, s1a, s2a = _conv_rows(u_ref.at[0], sa, wa_, ba_, r0, R)
            cb, curb, s1b, s2b = _conv_rows(u_ref.at[1], sb, wb_, bb_, r0, R)
            dact_ = da_ref[pl.ds(r0, R), :].astype(F32)
            sg = jax.nn.sigmoid(ca)
            dca = dact_ * cb * (sg * (1.0 + ca * (1.0 - sg)))
            dcb = dact_ * (ca * sg)
            du_ref[0, pl.ds(r0, R), :] = du_rows(sda, dca, wa_).astype(BF16)
            du_ref[1, pl.ds(r0, R), :] = du_rows(sdb, dcb, wb_).astype(BF16)
            return (dwa + taps(dca, cura, s1a, s2a), dwb + taps(dcb, curb, s1b, s2b),
                    dba + jnp.sum(dca, axis=0, keepdims=True), dbb + jnp.sum(dcb, axis=0, keepdims=True))

        z3 = jnp.zeros((3, tc), F32)
        z1 = jnp.zeros((1, tc), F32)
        dwa, dwb, dba, dbb = lax.fori_loop(0, nr, step, (z3, z3, z1, z1))
        dwa_ref[...] = dwa
        dwb_ref[...] = dwb
        dba_ref[...] = dba
        dbb_ref[...] = dbb

    col = pl.BlockSpec((T, tc), lambda j: (0, j))
    wsp = pl.BlockSpec((3, tc), lambda j: (0, j))
    bsp = pl.BlockSpec((1, tc), lambda j: (0, j))
    both = pl.BlockSpec((2, T, tc), lambda j: (0, 0, j))
    return pl.pallas_call(
        body, name=name, grid=(Fd // tc,), in_specs=[both, wsp, wsp, bsp, bsp, col],
        out_specs=[both, wsp, wsp, bsp, bsp],
        out_shape=[jax.ShapeDtypeStruct((2, T, Fd), BF16)] + [jax.ShapeDtypeStruct((3, Fd), F32)] * 2
        + [jax.ShapeDtypeStruct((1, Fd), F32)] * 2,
        scratch_shapes=[pltpu.VMEM((CONV_HALO + R, tc), F32)] * 2 + [pltpu.VMEM((R + SUBLANES, tc), F32)] * 2,
        compiler_params=_params(("parallel",)),
    )(u, wa, wb, ba, bb, dact)


def _bucket_index():
    t = np.arange(SW_WINDOW)[None, :] + SW_WINDOW
    s = np.arange(2 * SW_WINDOW)[:, None]
    dist = np.maximum(t - s, 0)
    exact = REL_BUCKETS // 2
    d = np.maximum(dist, 1).astype(np.float32)
    log_b = exact + (np.log(d / np.float32(exact)) / np.float32(math.log(REL_MAX_DIST / exact))
                     * np.float32(REL_BUCKETS - exact)).astype(np.int32)
    bucket = np.where(dist < exact, dist, np.minimum(log_b, REL_BUCKETS - 1))
    return bucket.astype(np.int32).reshape(1, -1)


BIAS_COLS = SW_WINDOW * 2 * SW_WINDOW
BIAS_TILE = 4096


def _bias_from_table(table, bucket, *, name):
    def body(t_ref, idx_ref, o_ref):
        onehot = (lax.broadcasted_iota(jnp.int32, (REL_BUCKETS, BIAS_TILE), 0) == idx_ref[...]).astype(BF16)
        acc = jnp.zeros((SW_Q_HEADS, BIAS_TILE), F32)
        for piece in _split3(t_ref[...]):
            acc = acc + lax.dot_general(piece, onehot, (((0,), (0,)), ((), ())), preferred_element_type=F32)
        o_ref[...] = acc

    return pl.pallas_call(
        body, name=name, grid=(BIAS_COLS // BIAS_TILE,),
        in_specs=[pl.BlockSpec((REL_BUCKETS, SW_Q_HEADS), lambda j: (0, 0)), pl.BlockSpec((1, BIAS_TILE), lambda j: (0, j))],
        out_specs=pl.BlockSpec((SW_Q_HEADS, BIAS_TILE), lambda j: (0, j)),
        out_shape=jax.ShapeDtypeStruct((SW_Q_HEADS, BIAS_COLS), F32),
        compiler_params=_params(("parallel",)),
    )(table, bucket)


def _table_grad(dbias, bucket, *, name):
    def body(d_ref, idx_ref, o_ref):
        @pl.when(pl.program_id(0) == 0)
        def _():
            o_ref[...] = jnp.zeros_like(o_ref)

        onehot = (lax.broadcasted_iota(jnp.int32, (REL_BUCKETS, BIAS_TILE), 0) == idx_ref[...]).astype(BF16)
        acc = jnp.zeros((REL_BUCKETS, SW_Q_HEADS), F32)
        for piece in _split3(d_ref[...]):
            acc = acc + lax.dot_general(onehot, piece, (((1,), (1,)), ((), ())), preferred_element_type=F32)
        o_ref[...] += acc

    return pl.pallas_call(
        body, name=name, grid=(BIAS_COLS // BIAS_TILE,),
        in_specs=[pl.BlockSpec((SW_Q_HEADS, BIAS_TILE), lambda j: (0, j)), pl.BlockSpec((1, BIAS_TILE), lambda j: (0, j))],
        out_specs=pl.BlockSpec((REL_BUCKETS, SW_Q_HEADS), lambda j: (0, 0)),
        out_shape=jax.ShapeDtypeStruct((REL_BUCKETS, SW_Q_HEADS), F32),
        compiler_params=_params(("arbitrary",)),
    )(dbias, bucket)


KV_DIM = SW_KV_HEADS * SW_HEAD_DIM
GROUP_ROWS = SW_GROUP * SW_HEAD_DIM
GROUP_LANES = SW_GROUP * SW_WINDOW


def _band_mask(n):
    s = lax.broadcasted_iota(jnp.int32, (2 * SW_WINDOW, GROUP_LANES), 0)
    t = (lax.broadcasted_iota(jnp.int32, (2 * SW_WINDOW, GROUP_LANES), 1) & (SW_WINDOW - 1)) + SW_WINDOW
    dist = t - s
    return (dist >= 0) & (dist < SW_WINDOW) & ((n > 0) | (s >= SW_WINDOW))


def _side_by_side(x_ref, g):
    r0 = g * GROUP_ROWS
    return jnp.concatenate([x_ref[r0 + r * SW_HEAD_DIM:r0 + (r + 1) * SW_HEAD_DIM, :] for r in range(SW_GROUP)], axis=1)


def _group_inputs(bias_ref, sink_ref, g):
    heads = range(g * SW_GROUP, (g + 1) * SW_GROUP)
    bias = jnp.concatenate([bias_ref[h] for h in heads], axis=1)
    sink = jnp.concatenate([jnp.broadcast_to(sink_ref[:, h:h + 1], (1, SW_WINDOW)) for h in heads], axis=1)
    return heads, bias, sink


def _kv_pair(kvp_ref, kvc_ref, g):
    ks = slice(g * SW_HEAD_DIM, (g + 1) * SW_HEAD_DIM)
    vs = slice(KV_DIM + g * SW_HEAD_DIM, KV_DIM + (g + 1) * SW_HEAD_DIM)
    kk = jnp.concatenate([kvp_ref[:, ks], kvc_ref[:, ks]], axis=0)
    vv = jnp.concatenate([kvp_ref[:, vs], kvc_ref[:, vs]], axis=0)
    return kk, vv, ks, vs


def _col_max(x):
    return jnp.max(x, axis=0, keepdims=True)


def _col_sum(x):
    return jnp.sum(x, axis=0, keepdims=True)


def _attn_fwd(qt, kv, bias, sinks, *, name):
    Dm, T = qt.shape
    W = SW_WINDOW

    def body(q_ref, kvc_ref, kvp_ref, bias_ref, sink_ref, o_ref):
        mask = _band_mask(pl.program_id(0))
        G = range(SW_KV_HEADS)
        ins = [_group_inputs(bias_ref, sink_ref, g) for g in G]
        kvs = [_kv_pair(kvp_ref, kvc_ref, g) for g in G]
        q = [_side_by_side(q_ref, g) for g in G]
        lg = [jnp.where(mask, mm(kvs[g][0], q[g]) * (SW_HEAD_DIM ** -0.5) + ins[g][1], -jnp.inf) for g in G]
        m = [jnp.maximum(_col_max(lg[g]), ins[g][2]) for g in G]
        p = [jnp.exp(lg[g] - m[g]) for g in G]
        den = [_col_sum(p[g]) + jnp.exp(ins[g][2] - m[g]) for g in G]
        o = [mm_tn(kvs[g][1], p[g]) / den[g] for g in G]
        for g in G:
            for r in range(SW_GROUP):
                o_ref[g * GROUP_ROWS + r * SW_HEAD_DIM:g * GROUP_ROWS + (r + 1) * SW_HEAD_DIM, :] = (
                    o[g][:, r * W:(r + 1) * W].astype(BF16))

    return pl.pallas_call(
        body, name=name, grid=(T // W,),
        in_specs=[pl.BlockSpec((Dm, W), lambda n: (0, n)),
                  pl.BlockSpec((W, 2 * KV_DIM), lambda n: (n, 0)),
                  pl.BlockSpec((W, 2 * KV_DIM), lambda n: (jnp.maximum(n - 1, 0), 0)),
                  pl.BlockSpec((SW_Q_HEADS, 2 * W, W), lambda n: (0, 0, 0)),
                  pl.BlockSpec((1, SW_Q_HEADS), lambda n: (0, 0))],
        out_specs=pl.BlockSpec((Dm, W), lambda n: (0, n)),
        out_shape=jax.ShapeDtypeStruct((Dm, T), BF16),
        compiler_params=_params(("parallel",)),
    )(qt, kv, kv, bias, sinks)


def _attn_bwd(qt, kv, bias, sinks, dot, *, name):
    Dm, T = qt.shape
    W = SW_WINDOW
    nb = T // W

    def body(q_ref, kvc_ref, kvp_ref, bias_ref, sink_ref, do_ref,
             dq_ref, dkv_ref, dbias_ref, dsink_ref, carry_ref):
        @pl.when(pl.program_id(0) == 0)
        def _():
            carry_ref[...] = jnp.zeros_like(carry_ref)
            dbias_ref[...] = jnp.zeros_like(dbias_ref)
            dsink_ref[...] = jnp.zeros_like(dsink_ref)

        n = nb - 1 - pl.program_id(0)
        mask = _band_mask(n)
        lane = lax.broadcasted_iota(jnp.int32, (1, SW_Q_HEADS), 1)
        sc = SW_HEAD_DIM ** -0.5
        G = range(SW_KV_HEADS)
        ins = [_group_inputs(bias_ref, sink_ref, g) for g in G]
        kvs = [_kv_pair(kvp_ref, kvc_ref, g) for g in G]
        q = [_side_by_side(q_ref, g) for g in G]
        do = [_side_by_side(do_ref, g) for g in G]
        lg = [jnp.where(mask, mm(kvs[g][0], q[g]) * sc + ins[g][1], -jnp.inf) for g in G]
        m = [jnp.maximum(_col_max(lg[g]), ins[g][2]) for g in G]
        p = [jnp.exp(lg[g] - m[g]) for g in G]
        ps = [jnp.exp(ins[g][2] - m[g]) for g in G]
        rden = [1.0 / (_col_sum(p[g]) + ps[g]) for g in G]
        pn = [p[g] * rden[g] for g in G]
        dpn = [mm(kvs[g][1], do[g]) for g in G]
        delta = [_col_sum(pn[g] * dpn[g]) for g in G]
        ds = [pn[g] * (dpn[g] - delta[g]) for g in G]
        dsr = [-(ps[g] * rden[g]) * delta[g] for g in G]
        dq = [mm_tn(kvs[g][0], ds[g]) * sc for g in G]
        dkk = [mm_nt(ds[g], q[g]) * sc for g in G]
        dvv = [mm_nt(pn[g], do[g]) for g in G]
        dsink = jnp.zeros((1, SW_Q_HEADS), F32)
        for g in G:
            _, _, ks, vs = kvs[g]
            for r, h in enumerate(ins[g][0]):
                cols = slice(r * W, (r + 1) * W)
                dbias_ref[h] += ds[g][:, cols]
                dq_ref[g * GROUP_ROWS + r * SW_HEAD_DIM:g * GROUP_ROWS + (r + 1) * SW_HEAD_DIM, :] = dq[g][:, cols].astype(BF16)
                dsink = dsink + jnp.where(lane == h, jnp.sum(dsr[g][:, cols], axis=1, keepdims=True), 0.0)
            dkv_ref[:, ks] = (carry_ref[:, ks] + dkk[g][W:]).astype(BF16)
            dkv_ref[:, vs] = (carry_ref[:, vs] + dvv[g][W:]).astype(BF16)
            carry_ref[:, ks] = dkk[g][:W]
            carry_ref[:, vs] = dvv[g][:W]
        dsink_ref[...] += dsink

    rev = lambda n: (nb - 1 - n, 0)
    revt = lambda n: (0, nb - 1 - n)
    return pl.pallas_call(
        body, name=name, grid=(nb,),
        in_specs=[pl.BlockSpec((Dm, W), revt),
                  pl.BlockSpec((W, 2 * KV_DIM), rev),
                  pl.BlockSpec((W, 2 * KV_DIM), lambda n: (jnp.maximum(nb - 2 - n, 0), 0)),
                  pl.BlockSpec((SW_Q_HEADS, 2 * W, W), lambda n: (0, 0, 0)),
                  pl.BlockSpec((1, SW_Q_HEADS), lambda n: (0, 0)),
                  pl.BlockSpec((Dm, W), revt)],
        out_specs=[pl.BlockSpec((Dm, W), revt), pl.BlockSpec((W, 2 * KV_DIM), rev),
                   pl.BlockSpec((SW_Q_HEADS, 2 * W, W), lambda n: (0, 0, 0)),
                   pl.BlockSpec((1, SW_Q_HEADS), lambda n: (0, 0))],
        out_shape=[jax.ShapeDtypeStruct((Dm, T), BF16), jax.ShapeDtypeStruct((T, 2 * KV_DIM), BF16),
                   jax.ShapeDtypeStruct((SW_Q_HEADS, 2 * W, W), F32), jax.ShapeDtypeStruct((1, SW_Q_HEADS), F32)],
        scratch_shapes=[pltpu.VMEM((W, 2 * KV_DIM), F32)],
        compiler_params=_params(("arbitrary",)),
    )(qt, kv, kv, bias, sinks, dot)


def _ffn_fwd(hb, w, l, after=None):
    u = _matmul(hb, w["ffn_in"][l], mode="nn", planes="n", out_dtype=BF16, name=f"ffn{l}_up", tm=1024, after=after)
    act = _conv_gate_fwd(u, w["conv_w_a"][l], w["conv_w_b"][l], w["conv_b_a"][l], w["conv_b_b"][l],
                         name=f"ffn{l}_conv_gate")
    return u, act


def _ffn_bwd(dffb, dh_scaled, hb, u, act, w, l, dact):
    g_out = _matmul(act, dffb, mode="tn", name=f"ffn{l}_down_dw", tm=1408, tn=1024, tk=1024)
    du, dwa, dwb, dba, dbb = _conv_gate_bwd(u, w["conv_w_a"][l], w["conv_w_b"][l], w["conv_b_a"][l], w["conv_b_b"][l],
                                            dact, name=f"ffn{l}_conv_gate_bwd")
    dh = _matmul(du, w["ffn_in"][l], mode="nt", planes="k", add=dh_scaled, add_scale=ALPHA, name=f"ffn{l}_up_dx",
                 tm=1024, tn=1024, tk=FFN_DIM)
    g_in = _matmul(hb, du, mode="tn", planes="n", name=f"ffn{l}_up_dw", tm=1024, tn=FFN_DIM // 2, tk=1024, split_n=True)
    return dh, dict(ffn_out=g_out, ffn_in=g_in, conv_w_a=dwa, conv_w_b=dwb, conv_b_a=dba, conv_b_b=dbb)


def _local_step(x, tgt, w, more_weights, emit):
    bucket = jnp.asarray(_bucket_index())
    xb = x.astype(BF16)

    pre = _matmul(xb, w["hg_in"], mode="nn", planes="n", out_dtype=BF16, name="hg_in", tm=1024, tn=1024,
                  after=w.get("token"))
    og, states = _hgrn_fwd(pre, w["lb_logits"], w["gnorm"], name="hgrn_fwd")
    z1, h1, h1b = _matmul_ln(og, w["hg_out"], x, w["ln_mix_g"][0], w["ln_mix_b"][0], name="hg_out_ln")
    w = {**w, **more_weights(1, h1b)}
    u0, act0 = _ffn_fwd(h1b, w, 0, after=w.get("token"))
    z2, h2, h2b = _matmul_ln(act0, w["ffn_out"][0], h1, w["ln_ffn_g"][0], w["ln_ffn_b"][0], name="ffn0_down_ln")
    kv = _matmul(h2b, w["kv"], mode="nn", out_dtype=BF16, name="kv_proj")

    bias = _bias_from_table(w["rel_bias"], bucket, name="rel_bias_expand").reshape(SW_Q_HEADS, 2 * SW_WINDOW, SW_WINDOW)
    q1 = _matmul(w["sw_q"], h2b, mode="tt", out_dtype=BF16, name="sw_q", tm=1024, tn=1024)
    o1 = _attn_fwd(q1, kv, bias, w["sinks"], name="attn_fwd")
    z3, h3, h3b = _matmul_ln(o1, w["sw_out"], h2, w["ln_mix_g"][1], w["ln_mix_b"][1], a_t=True, name="sw_out_ln")
    w = {**w, **more_weights(2, h3b)}
    u1, act1 = _ffn_fwd(h3b, w, 1)

    g = {}
    dz, dzb, dg_, db_, loss_tile, dact1 = _matmul_ln(act1, w["ffn_out"][1], h3, w["ln_ffn_g"][1], w["ln_ffn_b"][1],
                                                     tgt=tgt, name="ffn1_down_ln_loss")

    g["ln_ffn_g1"], g["ln_ffn_b1"] = dg_, db_
    dh3, gf1 = _ffn_bwd(dzb, dz, h3b, u1, act1, w, 1, dact1)
    dz, dzb, dg_, db_, do1 = _ln_bwd_matmul(dh3, z3, w["ln_mix_g"][1], w["ln_mix_b"][1], w["sw_out"], out_t=True,
                                            name="ln_mix1_bwd_sw_out_dx")
    g["ln_mix_g1"], g["ln_mix_b1"] = dg_, db_
    g_sw_out = _matmul(o1, dzb, mode="nn", name="sw_out_dw", tm=1024, tn=1024, tk=1024)
    dq1, dkv, dbias, dsinks = _attn_bwd(q1, kv, bias, w["sinks"], do1, name="attn_bwd")
    g["sinks"] = dsinks
    g["rel_bias"] = _table_grad(dbias.reshape(SW_Q_HEADS, BIAS_COLS), bucket, name="rel_bias_grad")
    dh2 = _matmul(dq1, w["sw_q"], mode="tt", add=dz, add_scale=ALPHA, name="sw_q_dx", tn=1024)
    dh2 = _matmul(dkv, w["kv"], mode="nt", add=dh2, name="kv_dx", tn=1024)
    g_sw_q = _matmul(h2b, dq1, mode="tt", name="sw_q_dw", tm=1024, tn=1024, tk=1024)
    g_kv = _matmul(h2b, dkv, mode="tn", name="kv_dw", tm=1024, tn=512, tk=1024)
    tok = emit(1, dict(sw_q=g_sw_q, sw_out=g_sw_out, kv=g_kv, ffn_in=gf1["ffn_in"], ffn_out=gf1["ffn_out"]))

    dz, dzb, dg_, db_, dact0 = _ln_bwd_matmul(dh2, z2, w["ln_ffn_g"][0], w["ln_ffn_b"][0], w["ffn_out"][0],
                                              name="ln_ffn0_bwd_down_dx", after=tok)
    g["ln_ffn_g0"], g["ln_ffn_b0"] = dg_, db_
    dh1, gf0 = _ffn_bwd(dzb, dz, h1b, u0, act0, w, 0, dact0)
    dz, dzb, dg_, db_, dog = _ln_bwd_matmul(dh1, z1, w["ln_mix_g"][0], w["ln_mix_b"][0], w["hg_out"],
                                            name="ln_mix0_bwd_hg_out_dx")
    g["ln_mix_g0"], g["ln_mix_b0"] = dg_, db_
    g_hg_out = _matmul(og, dzb, mode="tn", name="hg_out_dw", tm=1024, tn=1024, tk=1024)
    tok = emit(2, dict(hg_out=g_hg_out, ffn_in=gf0["ffn_in"], ffn_out=gf0["ffn_out"]))
    dpre, g["lb_logits"], g["gnorm"] = _hgrn_bwd(pre, w["lb_logits"], w["gnorm"], states, dog, name="hgrn_bwd", after=tok)
    tok = emit(3, dict(hg_in=_matmul(xb, dpre, mode="tn", planes="n", name="hg_in_dw", tm=1024, tn=1024, tk=1024)))
    dx = _matmul(dpre, w["hg_in"], mode="nt", planes="k", add=dz, add_scale=ALPHA, name="hg_in_dx", tm=1024, tn=1024,
                 tk=1024, after=tok)
    g["conv"] = [{k: gf[k] for k in ("conv_w_a", "conv_w_b", "conv_b_a", "conv_b_b")} for gf in (gf0, gf1)]
    return loss_tile, dx, g


def _adamw(wt, ga, gb, m, v, *, name, rows=None, prev=None):
    R, Cc = wt.shape
    r0, n = rows if rows is not None else (0, R)
    tr = _tile(n, 256, SUBLANES) if n % SUBLANES == 0 else n
    assert r0 % tr == 0
    c1 = 1.0 - ADAM_B1 ** ADAM_STEP
    c2 = 1.0 - ADAM_B2 ** ADAM_STEP
    two = gb is not None
    n_in = 5 if two else 4

    def body(*refs):
        if two:
            w_ref, ga_ref, gb_ref, m_ref, v_ref = refs[:5]
            g_ = ga_ref[...] + gb_ref[...]
        else:
            w_ref, ga_ref, m_ref, v_ref = refs[:4]
            g_ = ga_ref[...]
        g_ref, d_ref, nm_ref, nv_ref = refs[-4:]
        nm = ADAM_B1 * m_ref[...] + (1.0 - ADAM_B1) * g_
        nv = ADAM_B2 * v_ref[...] + (1.0 - ADAM_B2) * (g_ * g_)
        g_ref[...] = g_
        d_ref[...] = -ADAM_LR * ((nm / c1) / (jnp.sqrt(nv / c2) + ADAM_EPS) + ADAM_WD * w_ref[...])
        nm_ref[...] = nm
        nv_ref[...] = nv

    full = pl.BlockSpec((tr, Cc), lambda i: (i + r0 // tr, 0))
    part = pl.BlockSpec((tr, Cc), lambda i: (i, 0))
    args = (wt, ga, gb, m, v) if two else (wt, ga, m, v)
    in_specs = [full] + [part] * (n_in - 3) + [full, full]
    aliases = {}
    if prev is not None:
        args, in_specs = args + tuple(prev), in_specs + [ANY_SPEC] * 4
        aliases = {n_in + t: t for t in range(4)}
    return pl.pallas_call(
        body, name=name, grid=(n // tr,), in_specs=in_specs, out_specs=[full] * 4,
        out_shape=[jax.ShapeDtypeStruct((R, Cc), F32)] * 4, input_output_aliases=aliases,
        compiler_params=_params(("parallel",)),
    )(*args)


HBM_SPEC = pl.BlockSpec(memory_space=pltpu.HBM)
SEM_SPEC = pl.BlockSpec(memory_space=pltpu.SEMAPHORE)
VMEM_SPEC = pl.BlockSpec(memory_space=pltpu.VMEM)
DATAFLOW = pltpu.SideEffectType.DATAFLOW_SIDE_EFFECTING


def _in_hbm(a):
    return pltpu.with_memory_space_constraint(a, pltpu.HBM)


def _place():
    return lax.axis_index("x"), lax.axis_index("y"), lax.axis_index("c")


def _other_chips(x, y):
    return [(1 - x, y), (x, 1 - y), (1 - x, 1 - y)]


def _sum8(v, *, name, after=None):
    r = v.shape[0]

    def body(v_ref, all_ref, o_ref, send_sems, recv_sems, local_sem):
        x, y, c = _place()
        me, sibling = (x, y, c), (x, y, 1 - c)
        chips = _other_chips(x, y)

        def rows(px, py, pc):
            return all_ref.at[pl.ds((4 * px + 2 * py + pc) * r, r), :]

        def copy(k, block, to, src=None):
            return pltpu.make_async_remote_copy(
                src_ref=rows(*block) if src is None else src, dst_ref=rows(*block),
                send_sem=send_sems.at[k], recv_sem=recv_sems.at[k], device_id=to, device_id_type=MESH)

        mine = pltpu.make_async_copy(v_ref, rows(*me), local_sem)
        mine.start()
        first = [copy(0, me, sibling, src=v_ref)]
        first += [copy(1 + j, me, (*chip, c), src=v_ref) for j, chip in enumerate(chips)]
        for cp in first:
            cp.start()
        passed = [copy(4 + j, (*chip, c), sibling) for j, chip in enumerate(chips)]
        for j, chip in enumerate(chips):
            copy(1 + j, (*chip, c), me).wait_recv()
            passed[j].start()
        copy(0, sibling, me).wait_recv()
        for j, chip in enumerate(chips):
            copy(4 + j, (*chip, 1 - c), me).wait_recv()
        for cp in first + passed:
            cp.wait_send()
        mine.wait()
        acc = all_ref[pl.ds(0, r), :]
        for d in range(1, N_DEV):
            acc = acc + all_ref[pl.ds(d * r, r), :]
        o_ref[...] = acc

    body, xs, xa = _after(body, 1, after)
    return pl.pallas_call(
        body, name=name, in_specs=[VMEM_SPEC] + xs, out_specs=[VMEM_SPEC, VMEM_SPEC],
        out_shape=[jax.ShapeDtypeStruct((N_DEV * r, LANES), F32), jax.ShapeDtypeStruct((r, LANES), F32)],
        scratch_shapes=[pltpu.SemaphoreType.DMA((7,)), pltpu.SemaphoreType.DMA((7,)), pltpu.SemaphoreType.DMA],
        compiler_params=pltpu.CompilerParams(vmem_limit_bytes=VMEM_LIMIT),
    )(v, *xa)[1]


def _swap_sibling(vs, *, name):
    n = len(vs)

    def body(*refs):
        src, dst, send_sems, recv_sems = refs[:n], refs[n:2 * n], refs[2 * n], refs[2 * n + 1]
        x, y, c = _place()
        cps = [pltpu.make_async_remote_copy(src_ref=src[i], dst_ref=dst[i], send_sem=send_sems.at[i],
                                            recv_sem=recv_sems.at[i], device_id=(x, y, 1 - c), device_id_type=MESH)
               for i in range(n)]
        for cp in cps:
            cp.start()
        for cp in cps:
            cp.wait()

    return pl.pallas_call(
        body, name=name, in_specs=[HBM_SPEC] * n, out_specs=[HBM_SPEC] * n,
        out_shape=[jax.ShapeDtypeStruct(v.shape, v.dtype) for v in vs],
        scratch_shapes=[pltpu.SemaphoreType.DMA((n,)), pltpu.SemaphoreType.DMA((n,))],
    )(*vs)


def _half(ref, j, c, half):
    return ref.at[j, pl.ds(c * half, half), :]


def _gather_start(shard, after, *, name):
    R, Cc = shard.shape
    half = R // 2

    def body(src, land, send, recv, src_out, land_out, token):
        x, y, c = _place()
        for k, (px, py) in enumerate(_other_chips(x, y)):
            pltpu.make_async_remote_copy(src_ref=src.at[pl.ds(c * half, half), :], dst_ref=_half(land, 2 * x + y, c, half),
                                         send_sem=send.at[k], recv_sem=recv.at[k], device_id=(px, py, c),
                                         device_id_type=MESH).start()
        token[...] = jnp.zeros_like(token)

    land = lax.empty((N_CHIPS, R, Cc), shard.dtype)
    body, xs, xa = _after(body, 2, after)
    out = pl.pallas_call(
        body, name=name, in_specs=[HBM_SPEC, HBM_SPEC] + xs,
        out_specs=[SEM_SPEC, SEM_SPEC, HBM_SPEC, HBM_SPEC, VMEM_SPEC],
        out_shape=[pltpu.SemaphoreType.DMA((3,)), pltpu.SemaphoreType.DMA((3,)), pltpu.HBM(shard.shape, shard.dtype),
                   pltpu.HBM(land.shape, land.dtype), jax.ShapeDtypeStruct((SUBLANES, LANES), F32)],
        input_output_aliases={0: 2, 1: 3},
        compiler_params=pltpu.CompilerParams(has_side_effects=DATAFLOW),
    )(_in_hbm(shard), _in_hbm(land), *xa)
    return out[:4], out[4]


def _gather_wait(handle, after, *, name):
    send_sems, recv_sems, src, land = handle
    half = src.shape[0] // 2

    def body(src_ref, land_ref, send_ref, recv_ref, after_ref, src_out, land_out):
        x, y, c = _place()
        for k, (px, py) in enumerate(_other_chips(x, y)):
            cp = pltpu.make_async_remote_copy(src_ref=src_ref.at[pl.ds(c * half, half), :],
                                              dst_ref=_half(land_ref, 2 * px + py, c, half), send_sem=send_ref.at[k],
                                              recv_sem=recv_ref.at[k], device_id=(px, py, c), device_id_type=MESH)
            cp.wait_send()
            cp.wait_recv()

    return pl.pallas_call(
        body, name=name, in_specs=[HBM_SPEC, HBM_SPEC, SEM_SPEC, SEM_SPEC, ANY_SPEC], out_specs=[HBM_SPEC, HBM_SPEC],
        out_shape=[pltpu.HBM(src.shape, src.dtype), pltpu.HBM(land.shape, land.dtype)],
        input_output_aliases={0: 0, 1: 1},
        compiler_params=pltpu.CompilerParams(has_side_effects=DATAFLOW),
    )(src, land, send_sems, recv_sems, after)[1]


def _fill_sibling(land, *, name):
    _, R, Cc = land.shape
    half = R // 2

    def body(in_ref, o_ref, send_sems, recv_sems):
        x, y, c = _place()
        chips = _other_chips(x, y)
        cps = [pltpu.make_async_remote_copy(src_ref=_half(in_ref, 2 * px + py, c, half),
                                            dst_ref=_half(o_ref, 2 * px + py, c, half), send_sem=send_sems.at[k],
                                            recv_sem=recv_sems.at[k], device_id=(x, y, 1 - c), device_id_type=MESH)
               for k, (px, py) in enumerate(chips)]
        for cp in cps:
            cp.start()
        for k, (px, py) in enumerate(chips):
            pltpu.make_async_remote_copy(src_ref=_half(in_ref, 2 * px + py, 1 - c, half),
                                         dst_ref=_half(o_ref, 2 * px + py, 1 - c, half), send_sem=send_sems.at[k],
                                         recv_sem=recv_sems.at[k], device_id=(x, y, 1 - c), device_id_type=MESH).wait_recv()
        for cp in cps:
            cp.wait_send()

    return pl.pallas_call(
        body, name=name, in_specs=[HBM_SPEC], out_specs=HBM_SPEC, out_shape=jax.ShapeDtypeStruct(land.shape, land.dtype),
        scratch_shapes=[pltpu.SemaphoreType.DMA((3,)), pltpu.SemaphoreType.DMA((3,))],
        input_output_aliases={0: 0},
    )(land)


def _scatter_copies(src, land, send, recv):
    x, y, c = _place()
    return [pltpu.make_async_remote_copy(src_ref=src[i].at[2 * px + py], dst_ref=land[i].at[k], send_sem=send.at[3 * i + k],
                                         recv_sem=recv.at[3 * i + k], device_id=(px, py, c), device_id_type=MESH)
            for i in range(len(src)) for k, (px, py) in enumerate(_other_chips(x, y))]


def _scatter_start(pieces, *, name):
    n = len(pieces)

    def body(*refs):
        src, land, send, recv, token = refs[:n], refs[n:2 * n], refs[2 * n], refs[2 * n + 1], refs[-1]
        for cp in _scatter_copies(src, land, send, recv):
            cp.start()
        token[...] = jnp.zeros_like(token)

    lands = [lax.empty((3,) + p.shape[1:], p.dtype) for p in pieces]
    sems = pltpu.SemaphoreType.DMA((3 * n,))
    out = pl.pallas_call(
        body, name=name, in_specs=[HBM_SPEC] * (2 * n),
        out_specs=[SEM_SPEC, SEM_SPEC] + [HBM_SPEC] * (2 * n) + [VMEM_SPEC],
        out_shape=[sems, sems] + [pltpu.HBM(a.shape, a.dtype) for a in pieces + lands]
        + [jax.ShapeDtypeStruct((SUBLANES, LANES), F32)],
        input_output_aliases={i: 2 + i for i in range(2 * n)},
        compiler_params=pltpu.CompilerParams(has_side_effects=DATAFLOW),
    )(*[_in_hbm(a) for a in pieces + lands])
    return (out[0], out[1], out[2:2 + n], out[2 + n:2 + 2 * n]), out[-1]


def _scatter_wait(handle, after, *, name):
    send_sems, recv_sems, srcs, lands = handle
    n = len(srcs)

    def body(*refs):
        src, land, send, recv = refs[:n], refs[n:2 * n], refs[2 * n], refs[2 * n + 1]
        for cp in _scatter_copies(src, land, send, recv):
            cp.wait_send()
            cp.wait_recv()

    both = list(srcs) + list(lands)
    out = pl.pallas_call(
        body, name=name, in_specs=[HBM_SPEC] * (2 * n) + [SEM_SPEC, SEM_SPEC, ANY_SPEC], out_specs=[HBM_SPEC] * (2 * n),
        out_shape=[pltpu.HBM(a.shape, a.dtype) for a in both],
        input_output_aliases={i: i for i in range(2 * n)},
        compiler_params=pltpu.CompilerParams(has_side_effects=DATAFLOW),
    )(*both, send_sems, recv_sems, after)
    return out[n:]


def _chip_sum(pieces, got, chip, *, name):
    _, R, Cc = pieces.shape
    tr = _tile(R, 256, SUBLANES)

    def body(chip_ref, a_ref, g_ref, o_ref):
        o_ref[...] = ((a_ref[...] + g_ref[0].astype(F32)) + g_ref[1].astype(F32)) + g_ref[2].astype(F32)

    return pl.pallas_call(
        body, name=name,
        grid_spec=pltpu.PrefetchScalarGridSpec(
            num_scalar_prefetch=1, grid=(R // tr,),
            in_specs=[pl.BlockSpec((None, tr, Cc), lambda i, ch: (ch[0], i, 0)),
                      pl.BlockSpec((3, tr, Cc), lambda i, ch: (0, i, 0))],
            out_specs=pl.BlockSpec((tr, Cc), lambda i, ch: (i, 0))),
        out_shape=jax.ShapeDtypeStruct((R, Cc), F32),
        compiler_params=_params(("parallel",)),
    )(chip, pieces, got)


PACK_COLS = 1024


def _pack_rows(parts):
    return jnp.concatenate([p.reshape(-1, PACK_COLS) for p in parts], axis=0)


def _unpack_rows(block, shapes):
    lead = block.shape[:-2]
    out, off = [], 0
    for s in shapes:
        r = int(np.prod(s)) // PACK_COLS
        out.append(block[..., off:off + r, :].reshape(lead + tuple(s)))
        off += r
    assert off == block.shape[-2]
    return out


def _flat128(parts):
    out = []
    for p in parts:
        v = p.reshape(-1)
        pad = (-v.shape[0]) % LANES
        out.append(jnp.pad(v, (0, pad)) if pad else v)
    v = jnp.concatenate(out)
    pad = (-v.shape[0]) % (SUBLANES * LANES)
    if pad:
        v = jnp.pad(v, (0, pad))
    return v.reshape(-1, LANES)


def _unflat128(block, shapes):
    v = block.reshape(-1)
    out, off = [], 0
    for s in shapes:
        n = int(np.prod(s))
        out.append(v[off:off + n].reshape(s))
        off += n + ((-n) % LANES)
    return out


def kernel(x, hgrn_w_in, hgrn_lb_logits, hgrn_gnorm_w, hgrn_w_out, swa_w_q, swa_sinks, swa_w_out, shared_w_kv, rel_bias, ffn_w_in, ffn_conv_w, ffn_conv_b, ffn_w_out, ln_mix_g, ln_mix_b, ln_ffn_g, ln_ffn_b, loss_target, m_hgrn_w_in, m_hgrn_lb_logits, m_hgrn_gnorm_w, m_hgrn_w_out, m_swa_w_q, m_swa_sinks, m_swa_w_out, m_shared_w_kv, m_rel_bias, m_ffn_w_in, m_ffn_conv_w, m_ffn_conv_b, m_ffn_w_out, m_ln_mix_g, m_ln_mix_b, m_ln_ffn_g, m_ln_ffn_b, v_hgrn_w_in, v_hgrn_lb_logits, v_hgrn_gnorm_w, v_hgrn_w_out, v_swa_w_q, v_swa_sinks, v_swa_w_out, v_shared_w_kv, v_rel_bias, v_ffn_w_in, v_ffn_conv_w, v_ffn_conv_b, v_ffn_w_out, v_ln_mix_g, v_ln_mix_b, v_ln_ffn_g, v_ln_ffn_b):
    xi, yi, ci = _place()
    chip = 2 * xi + yi
    Dm = D_MODEL
    FC = 2 * FFN_DIM // N_CHIPS
    Fo = FFN_DIM // N_CHIPS
    Dq = Dm // N_CHIPS
    bf = lambda a: a.astype(BF16)

    shard0 = _pack_rows([bf(hgrn_w_in), bf(hgrn_w_out)])
    shard1 = _pack_rows([bf(swa_w_q), bf(swa_w_out), bf(shared_w_kv), bf(ffn_w_in[0]), bf(ffn_w_out[0])])
    shard2 = _pack_rows([bf(ffn_w_in[1]), bf(ffn_w_out[1])])
    handle0, token0 = _gather_start(shard0, None, name="gather_w0_start")

    lb_full = lax.dynamic_update_slice(jnp.zeros((2, Dm), F32), hgrn_lb_logits, (0, chip * Dq))
    cw_full = lax.dynamic_update_slice(jnp.zeros((DEPTH, 3, 2 * FFN_DIM), F32), ffn_conv_w, (0, 0, chip * FC))
    only_south = (ci == 0).astype(F32)
    small_in = _sum8(_flat128([lb_full, cw_full]) * only_south, name="gather_small", after=token0)
    lb_full, cw_full = _unflat128(small_in, [(2, Dm), (DEPTH, 3, 2 * FFN_DIM)])

    land0 = _fill_sibling(_gather_wait(handle0, small_in, name="gather_w0_wait"), name="gather_w0_fill")
    all0 = lax.dynamic_update_slice(land0, shard0[None], (chip, 0, 0))
    handle1, token1 = _gather_start(shard1, land0, name="gather_w1_start")
    w_in, w_hg_out = _unpack_rows(all0, [(Dm, Dm), (Dq, Dm)])

    def ffn_weights(w_fi, w_fo, l):
        halves = jnp.stack([jnp.concatenate([w_fi[0], w_fi[1]], axis=1), jnp.concatenate([w_fi[2], w_fi[3]], axis=1)])
        return {"ffn_in": {l: halves}, "ffn_out": {l: w_fo.reshape(FFN_DIM, Dm)}}

    got = {}

    def more_weights(k, after):
        shard = (shard1, shard2)[k - 1]
        land = _gather_wait(got.pop("handle"), after, name=f"gather_w{k}_wait")
        land = _fill_sibling(land, name=f"gather_w{k}_fill")
        allk = lax.dynamic_update_slice(land, shard[None], (chip, 0, 0))
        if k == 1:
            got["handle"], token2 = _gather_start(shard2, land, name="gather_w2_start")
            w_q, w_o, w_kv, w_fi, w_fo = _unpack_rows(allk, [(Dq, Dm), (Dq, Dm), (Dq, 2 * KV_DIM), (Dm, FC), (Fo, Dm)])
            got.update(ffn_weights(w_fi, w_fo, 0))
            return {"sw_q": w_q.reshape(Dm, Dm), "sw_out": w_o.reshape(Dm, Dm), "kv": w_kv.reshape(Dm, 2 * KV_DIM),
                    "token": token2, **{n: got[n] for n in ("ffn_in", "ffn_out")}}
        w_fi, w_fo = _unpack_rows(allk, [(Dm, FC), (Fo, Dm)])
        new = ffn_weights(w_fi, w_fo, 1)
        return {n: {**got[n], **new[n]} for n in new}

    got["handle"] = handle1

    w = {
        "hg_in": w_in, "hg_out": w_hg_out.reshape(Dm, Dm), "token": token1,
        "lb_logits": lb_full, "gnorm": hgrn_gnorm_w, "sinks": swa_sinks, "rel_bias": rel_bias,
        "conv_w_a": [cw_full[l, :, :FFN_DIM] for l in range(DEPTH)],
        "conv_w_b": [cw_full[l, :, FFN_DIM:] for l in range(DEPTH)],
        "conv_b_a": [ffn_conv_b[l:l + 1, :FFN_DIM] for l in range(DEPTH)],
        "conv_b_b": [ffn_conv_b[l:l + 1, FFN_DIM:] for l in range(DEPTH)],
        "ln_mix_g": [ln_mix_g[l:l + 1] for l in range(DEPTH)], "ln_mix_b": [ln_mix_b[l:l + 1] for l in range(DEPTH)],
        "ln_ffn_g": [ln_ffn_g[l:l + 1] for l in range(DEPTH)], "ln_ffn_b": [ln_ffn_b[l:l + 1] for l in range(DEPTH)],
    }

    sent = {}

    def ffn_pieces(gd):
        return [gd["ffn_in"], gd["ffn_out"].reshape(N_CHIPS, Fo, Dm)]

    def emit(k, gd):
        rows4 = lambda a: a.reshape(N_CHIPS, Dq, a.shape[-1])
        if k == 1:
            pieces = [rows4(gd["sw_q"]), rows4(gd["sw_out"]), rows4(gd["kv"])] + ffn_pieces(gd)
        elif k == 2:
            pieces = ffn_pieces(gd) + [rows4(gd["hg_out"])]
        else:
            pieces = [gd["hg_in"]]
        handle, token = _scatter_start([p.astype(BF16) for p in pieces], name=f"scatter_g{k}_start")
        sent[k] = (handle, pieces)
        return token

    loss_tile, grad_x, g = _local_step(x[0], loss_target[0], w, more_weights, emit)

    wts = dict(hgrn_w_in=hgrn_w_in, hgrn_lb_logits=hgrn_lb_logits, hgrn_gnorm_w=hgrn_gnorm_w, hgrn_w_out=hgrn_w_out,
               swa_w_q=swa_w_q, swa_sinks=swa_sinks, swa_w_out=swa_w_out, shared_w_kv=shared_w_kv, rel_bias=rel_bias,
               ffn_w_in=ffn_w_in, ffn_conv_w=ffn_conv_w, ffn_conv_b=ffn_conv_b, ffn_w_out=ffn_w_out,
               ln_mix_g=ln_mix_g, ln_mix_b=ln_mix_b, ln_ffn_g=ln_ffn_g, ln_ffn_b=ln_ffn_b)
    ms = dict(hgrn_w_in=m_hgrn_w_in, hgrn_lb_logits=m_hgrn_lb_logits, hgrn_gnorm_w=m_hgrn_gnorm_w, hgrn_w_out=m_hgrn_w_out,
              swa_w_q=m_swa_w_q, swa_sinks=m_swa_sinks, swa_w_out=m_swa_w_out, shared_w_kv=m_shared_w_kv, rel_bias=m_rel_bias,
              ffn_w_in=m_ffn_w_in, ffn_conv_w=m_ffn_conv_w, ffn_conv_b=m_ffn_conv_b, ffn_w_out=m_ffn_w_out,
              ln_mix_g=m_ln_mix_g, ln_mix_b=m_ln_mix_b, ln_ffn_g=m_ln_ffn_g, ln_ffn_b=m_ln_ffn_b)
    vs = dict(hgrn_w_in=v_hgrn_w_in, hgrn_lb_logits=v_hgrn_lb_logits, hgrn_gnorm_w=v_hgrn_gnorm_w, hgrn_w_out=v_hgrn_w_out,
              swa_w_q=v_swa_w_q, swa_sinks=v_swa_sinks, swa_w_out=v_swa_w_out, shared_w_kv=v_shared_w_kv, rel_bias=v_rel_bias,
              ffn_w_in=v_ffn_w_in, ffn_conv_w=v_ffn_conv_w, ffn_conv_b=v_ffn_conv_b, ffn_w_out=v_ffn_w_out,
              ln_mix_g=v_ln_mix_g, ln_mix_b=v_ln_mix_b, ln_ffn_g=v_ln_ffn_g, ln_ffn_b=v_ln_ffn_b)
    names = list(wts)
    grads, delta, new_m, new_v = {}, {}, {}, {}

    def update(n, ga, gb, layer=None, prev=None):
        r2 = lambda a: a.reshape(-1, a.shape[-1])
        rows = None if layer is None else (layer * ga.shape[0], ga.shape[0])
        return _adamw(r2(wts[n]), ga, gb, r2(ms[n]), r2(vs[n]), rows=rows, prev=prev,
                      name=f"adamw_{n}" + ("" if layer is None else f"_{layer}"))

    def keep(n, res):
        grads[n], delta[n], new_m[n], new_v[n] = [a.reshape(wts[n].shape) for a in res]

    chip1 = jnp.reshape(chip, (1,)).astype(jnp.int32)
    after = grad_x
    for k in (1, 2, 3):
        handle, pieces = sent[k]
        lands = _scatter_wait(handle, after, name=f"scatter_g{k}_wait")
        parts = [_chip_sum(p, l, chip1, name=f"scatter_g{k}_sum{i}") for i, (p, l) in enumerate(zip(pieces, lands))]
        sibs = _swap_sibling(parts, name=f"scatter_g{k}_swap")
        if k == 1:
            for n, ga, gb in zip(["swa_w_q", "swa_w_out", "shared_w_kv"], parts[:3], sibs[:3]):
                keep(n, update(n, ga, gb))
            ffn_in_1 = update("ffn_w_in", parts[3], sibs[3], layer=1)
            ffn_out_1 = update("ffn_w_out", parts[4], sibs[4], layer=1)
            after = ffn_out_1[3]
        elif k == 2:
            keep("ffn_w_in", update("ffn_w_in", parts[0], sibs[0], layer=0, prev=ffn_in_1))
            keep("ffn_w_out", update("ffn_w_out", parts[1], sibs[1], layer=0, prev=ffn_out_1))
            keep("hgrn_w_out", update("hgrn_w_out", parts[2], sibs[2]))
            after = new_v["hgrn_w_out"]
        else:
            keep("hgrn_w_in", update("hgrn_w_in", parts[0], sibs[0]))

    small_shapes = [(SUBLANES, LANES), (2, Dm), (1, HG_DIM), (1, SW_Q_HEADS), (REL_BUCKETS, SW_Q_HEADS),
                    (DEPTH, 3, 2 * FFN_DIM), (DEPTH, 2 * FFN_DIM)] + [(DEPTH, Dm)] * 4
    gc = g["conv"]
    conv_w_g = jnp.stack([jnp.concatenate([gc[l]["conv_w_a"], gc[l]["conv_w_b"]], axis=1) for l in range(DEPTH)])
    conv_b_g = jnp.concatenate([jnp.concatenate([gc[l]["conv_b_a"], gc[l]["conv_b_b"]], axis=1) for l in range(DEPTH)], axis=0)
    ln_g = [jnp.concatenate([g[f"{n}0"], g[f"{n}1"]], axis=0) for n in ("ln_mix_g", "ln_mix_b", "ln_ffn_g", "ln_ffn_b")]
    small_out = _sum8(_flat128([loss_tile, g["lb_logits"], g["gnorm"], g["sinks"], g["rel_bias"], conv_w_g, conv_b_g] + ln_g),
                      name="sum_small")
    (loss_t, g_lb, g_gn, g_sinks, g_rel, g_cw, g_cb, g_lmg, g_lmb, g_lfg, g_lfb) = _unflat128(small_out, small_shapes)
    loss = loss_t[0, 0]
    g_lb = lax.dynamic_slice_in_dim(g_lb, chip * Dq, Dq, axis=1)
    g_cw = lax.dynamic_slice_in_dim(g_cw, chip * FC, FC, axis=2)
    small_g = dict(hgrn_lb_logits=g_lb, hgrn_gnorm_w=g_gn, swa_sinks=g_sinks, rel_bias=g_rel, ffn_conv_w=g_cw,
                   ffn_conv_b=g_cb, ln_mix_g=g_lmg, ln_mix_b=g_lmb, ln_ffn_g=g_lfg, ln_ffn_b=g_lfb)
    small_names = list(small_g)
    sshapes = [wts[n].shape for n in small_names]
    _, d_, m_, v_ = _adamw(_flat128([wts[n] for n in small_names]), _flat128([small_g[n] for n in small_names]), None,
                           _flat128([ms[n] for n in small_names]), _flat128([vs[n] for n in small_names]), name="adamw_small")
    for n, a, b_, c_ in zip(small_names, _unflat128(d_, sshapes), _unflat128(m_, sshapes), _unflat128(v_, sshapes)):
        grads[n], delta[n], new_m[n], new_v[n] = small_g[n], a, b_, c_

    return (loss, grad_x[None], *[grads[n] for n in names], *[delta[n] for n in names],
            *[new_m[n] for n in names], *[new_v[n] for n in names])
```

```python
import math

import numpy as np
import jax
import jax.numpy as jnp
from jax import lax
from jax.experimental import pallas as pl
from jax.experimental.pallas import tpu as pltpu

F32 = jnp.float32
BF16 = jnp.bfloat16
MESH = pl.DeviceIdType.MESH

D_MODEL = 1024
DEPTH = 2
HG_HEADS = 8
HG_DIM = 128
SW_Q_HEADS = 16
SW_KV_HEADS = 4
SW_HEAD_DIM = 64
SW_GROUP = 4
SW_WINDOW = 128
REL_BUCKETS = 32
REL_MAX_DIST = 128
FFN_DIM = 2816
ALPHA = (2.0 * DEPTH) ** 0.25
LN_EPS = 1e-5
RMS_EPS = 1e-6
ADAM_LR = 0.001
ADAM_B1 = 0.9
ADAM_B2 = 0.999
ADAM_EPS = 1e-08
ADAM_WD = 0.01
ADAM_STEP = 10

VMEM_BYTES_V7X = 64 * 1024 * 1024
VMEM_LIMIT = VMEM_BYTES_V7X - 8 * 1024 * 1024
LANES = 128
SUBLANES = 8

HG_C = 64
HG_RB = 256
ROW_TILE = 256
CONV_R = 128
N_CHIPS = 4
N_DEV = 8

ANY_SPEC = pl.BlockSpec(memory_space=pl.ANY)


def _after(body, n_in, after):
    if after is None:
        return body, [], ()

    def wrapped(*refs):
        return body(*refs[:n_in], *refs[n_in + 1:])

    return wrapped, [ANY_SPEC], (after,)


def _params(sem=None):
    return pltpu.CompilerParams(dimension_semantics=sem, vmem_limit_bytes=VMEM_LIMIT)


def _tile(n, pref, unit=LANES):
    if n <= pref:
        return n
    best = None
    for t in range(unit, pref + 1, unit):
        if n % t == 0:
            best = t
    assert best is not None, (n, pref, unit)
    return best


def _dot(a, b, ca, cb):
    nb = a.ndim - 2
    batch = tuple(range(nb))
    return lax.dot_general(a.astype(BF16), b.astype(BF16), (((nb + ca,), (nb + cb,)), (batch, batch)),
                           preferred_element_type=F32)


@jax.custom_vjp
def mm(a, b):
    return _dot(a, b, 1, 0)


@jax.custom_vjp
def mm_nt(a, b):
    return _dot(a, b, 1, 1)


@jax.custom_vjp
def mm_tn(a, b):
    return _dot(a, b, 0, 0)


mm.defvjp(lambda a, b: (mm(a, b), (a, b)), lambda r, ct: (mm_nt(ct, r[1]), mm_tn(r[0], ct)))
mm_nt.defvjp(lambda a, b: (mm_nt(a, b), (a, b)), lambda r, ct: (mm(ct, r[1]), mm_tn(ct, r[0])))
mm_tn.defvjp(lambda a, b: (mm_tn(a, b), (a, b)), lambda r, ct: (mm_nt(r[1], ct), mm(r[0], ct)))


def _split2(x):
    hi = x.astype(BF16)
    return hi, (x - hi.astype(F32)).astype(BF16)


@jax.custom_vjp
def _scores(qt, kt):
    return _dot(qt, kt, 1, 1)


def _scores_bwd(r, ct):
    (qh, ql), (kh, kl) = _split2(r[0]), _split2(r[1])
    return _dot(ct, kh, 1, 0) + _dot(ct, kl, 1, 0), _dot(ct, qh, 0, 0) + _dot(ct, ql, 0, 0)


_scores.defvjp(lambda a, b: (_scores(a, b), (a, b)), _scores_bwd)


def _split3(x):
    hi = x.astype(BF16)
    r1 = x - hi.astype(F32)
    mid = r1.astype(BF16)
    lo = (r1 - mid.astype(F32)).astype(BF16)
    return hi, mid, lo


def _cumsum_impl(x):
    ax = x.ndim - 2
    n = x.shape[ax]
    row = lax.broadcasted_iota(jnp.int32, x.shape, ax)
    d = 1
    while d < n:
        x = x + jnp.where(row >= d, pltpu.roll(x, d, ax), 0.0)
        d *= 2
    return x


def _cumsum_rev_impl(x):
    ax = x.ndim - 2
    n = x.shape[ax]
    row = lax.broadcasted_iota(jnp.int32, x.shape, ax)
    d = 1
    while d < n:
        x = x + jnp.where(row < n - d, pltpu.roll(x, n - d, ax), 0.0)
        d *= 2
    return x


@jax.custom_vjp
def _cumsum(x):
    return _cumsum_impl(x)


_cumsum.defvjp(lambda x: (_cumsum_impl(x), None), lambda _, ct: (_cumsum_rev_impl(ct),))


def _matmul(a, b, *, mode, name, out_dtype=F32, add=None, add_scale=1.0, tm=512, tn=1408, tk=1408, after=None,
            split_n=False, planes=None):
    P = b.shape[0] if planes else 1
    a2, b2 = a.shape[-2:], b.shape[-2:]
    (M, K) = a2 if mode[0] == "n" else a2[::-1]
    (K2, N) = b2 if mode[1] == "n" else b2[::-1]
    assert K == K2, (a.shape, b.shape, mode)
    assert a.ndim == (3 if planes == "k" else 2) and b.ndim == (3 if planes else 2)
    tm, tn, tk = _tile(M, tm), _tile(N, tn), _tile(K, tk)
    nj, nkp = N // tn, K // tk
    nk = nkp * (P if planes == "k" else 1)
    ca, cb = (1 if mode[0] == "n" else 0), (0 if mode[1] == "n" else 1)
    a_blk, a_idx = ((tk, tm), lambda i, k: (k, i)) if mode[0] == "t" else ((tm, tk), lambda i, k: (i, k))
    b_blk, b_idx = ((tn, tk), lambda k, j: (j, k)) if mode[1] == "t" else ((tk, tn), lambda k, j: (k, j))
    if planes == "k":
        a_spec = pl.BlockSpec((None,) + a_blk, lambda i, j, k: (k // nkp,) + a_idx(i, k % nkp))
        b_spec = pl.BlockSpec((None,) + b_blk, lambda i, j, k: (k // nkp,) + b_idx(k % nkp, j))
    else:
        a_spec = pl.BlockSpec(a_blk, lambda i, j, k: a_idx(i, k))
        b_spec = (pl.BlockSpec((None,) + b_blk, lambda i, j, k: (j // nj,) + b_idx(k, j % nj)) if planes == "n"
                  else pl.BlockSpec(b_blk, lambda i, j, k: b_idx(k, j)))
    if split_n:
        o_spec, out_shape = pl.BlockSpec((None, tm, tn), lambda i, j, k: (j, i, 0)), (P * nj if planes == "n" else nj, M, tn)
    elif planes == "n":
        o_spec, out_shape = pl.BlockSpec((None, tm, tn), lambda i, j, k: (j // nj, i, j % nj)), (P, M, N)
    else:
        o_spec, out_shape = pl.BlockSpec((tm, tn), lambda i, j, k: (i, j)), (M, N)
    has_add = add is not None
    assert not (has_add and (split_n or planes == "n"))

    def finish(r, add_ref, o_ref):
        if has_add:
            r = r + add_scale * add_ref[...]
        o_ref[...] = r.astype(out_dtype)

    def body(*refs):
        a_ref, b_ref = refs[:2]
        add_ref = refs[2] if has_add else None
        o_ref = refs[3 if has_add else 2]
        if nk == 1:
            finish(_dot(a_ref[...], b_ref[...], ca, cb), add_ref, o_ref)
            return
        acc_ref = refs[-1]
        k = pl.program_id(2)

        @pl.when(k == 0)
        def _():
            acc_ref[...] = jnp.zeros_like(acc_ref)

        acc_ref[...] += _dot(a_ref[...], b_ref[...], ca, cb)

        @pl.when(k == nk - 1)
        def _():
            finish(acc_ref[...], add_ref, o_ref)

    in_specs = [a_spec, b_spec] + ([o_spec] if has_add else [])
    args = (a, b) + ((add,) if has_add else ())
    body, xs, xa = _after(body, len(args), after)
    in_specs, args = in_specs + xs, args + xa
    return pl.pallas_call(
        body, name=name, grid=(M // tm, nj * (P if planes == "n" else 1), nk), in_specs=in_specs, out_specs=o_spec,
        out_shape=jax.ShapeDtypeStruct(out_shape, out_dtype),
        scratch_shapes=[pltpu.VMEM((tm, tn), F32)] if nk > 1 else [],
        compiler_params=_params(("parallel", "parallel", "arbitrary")),
    )(*args)


def _ln(z, g, b):
    mu = jnp.mean(z, axis=-1, keepdims=True)
    zc = z - mu
    var = jnp.mean(zc * zc, axis=-1, keepdims=True)
    return zc * lax.rsqrt(var + LN_EPS) * g + b


def _matmul_ln(a, b, h, g, bias, *, name, tgt=None, tm=512, a_t=False):
    (T, K), (K2, Dm) = (a.shape[::-1] if a_t else a.shape), b.shape
    assert K == K2 and h.shape == (T, Dm)
    tm = _tile(T, tm, SUBLANES)
    last = tgt is not None

    def body(*refs):
        a_ref, b_ref, h_ref, g_ref, bias_ref = refs[:5]
        z = ALPHA * h_ref[...] + _dot(a_ref[...], b_ref[...], 0 if a_t else 1, 0)
        if not last:
            z_ref, y_ref, yb_ref = refs[5:]
            y = _ln(z, g_ref[...], bias_ref[...])
            z_ref[...] = z
            y_ref[...] = y
            yb_ref[...] = y.astype(BF16)
            return
        t_ref, dz_ref, dzb_ref, dg_ref, db_ref, l_ref, da_ref = refs[5:]

        @pl.when(pl.program_id(0) == 0)
        def _():
            dg_ref[...] = jnp.zeros_like(dg_ref)
            db_ref[...] = jnp.zeros_like(db_ref)
            l_ref[...] = jnp.zeros_like(l_ref)

        y, vjp = jax.vjp(_ln, z, g_ref[...], bias_ref[...])
        e = y - t_ref[...]
        dz, dg, db = vjp(e * (1.0 / Dm))
        l_ref[...] += 0.5 * jnp.sum(jnp.mean(e * e, axis=-1, keepdims=True), axis=0, keepdims=True)
        dzb = dz.astype(BF16)
        dz_ref[...] = dz
        dzb_ref[...] = dzb
        dg_ref[...] += dg
        db_ref[...] += db
        da_ref[...] = _dot(dzb, b_ref[...], 1, 1).astype(BF16)

    row = pl.BlockSpec((tm, Dm), lambda i: (i, 0))
    vec = pl.BlockSpec((1, Dm), lambda i: (0, 0))
    a_spec = pl.BlockSpec((K, tm), lambda i: (0, i)) if a_t else pl.BlockSpec((tm, K), lambda i: (i, 0))
    in_specs = [a_spec, pl.BlockSpec((K, Dm), lambda i: (0, 0)), row, vec, vec]
    f32, b16 = jax.ShapeDtypeStruct((T, Dm), F32), jax.ShapeDtypeStruct((T, Dm), BF16)
    if not last:
        return pl.pallas_call(
            body, name=name, grid=(T // tm,), in_specs=in_specs, out_specs=[row, row, row], out_shape=[f32, f32, b16],
            compiler_params=_params(("parallel",)),
        )(a, b, h, g, bias)
    assert not a_t
    return pl.pallas_call(
        body, name=name, grid=(T // tm,), in_specs=in_specs + [row],
        out_specs=[row, row, vec, vec, pl.BlockSpec((SUBLANES, LANES), lambda i: (0, 0)), a_spec],
        out_shape=[f32, b16, jax.ShapeDtypeStruct((1, Dm), F32), jax.ShapeDtypeStruct((1, Dm), F32),
                   jax.ShapeDtypeStruct((SUBLANES, LANES), F32), jax.ShapeDtypeStruct((T, K), BF16)],
        compiler_params=_params(("arbitrary",)),
    )(a, b, h, g, bias, tgt)


def _ln_bwd_matmul(dy, z, g, b, w, *, name, out_t=False, tm=512, after=None):
    T, Dm = z.shape
    N = w.shape[0]
    tm = _tile(T, tm, LANES if out_t else SUBLANES)

    def body(dy_ref, z_ref, g_ref, b_ref, w_ref, dz_ref, dzb_ref, dg_ref, db_ref, o_ref):
        @pl.when(pl.program_id(0) == 0)
        def _():
            dg_ref[...] = jnp.zeros_like(dg_ref)
            db_ref[...] = jnp.zeros_like(db_ref)

        _, vjp = jax.vjp(_ln, z_ref[...], g_ref[...], b_ref[...])
        dz, dg, db = vjp(dy_ref[...])
        dzb = dz.astype(BF16)
        dz_ref[...] = dz
        dzb_ref[...] = dzb
        dg_ref[...] += dg
        db_ref[...] += db
        o_ref[...] = (_dot(w_ref[...], dzb, 1, 1) if out_t else _dot(dzb, w_ref[...], 1, 1)).astype(BF16)

    row = pl.BlockSpec((tm, Dm), lambda i: (i, 0))
    vec = pl.BlockSpec((1, Dm), lambda i: (0, 0))
    o_spec = pl.BlockSpec((N, tm), lambda i: (0, i)) if out_t else pl.BlockSpec((tm, N), lambda i: (i, 0))
    body, xs, xa = _after(body, 5, after)
    return pl.pallas_call(
        body, name=name, grid=(T // tm,), in_specs=[row, row, vec, vec, pl.BlockSpec((N, Dm), lambda i: (0, 0))] + xs,
        out_specs=[row, row, vec, vec, o_spec],
        out_shape=[jax.ShapeDtypeStruct((T, Dm), F32), jax.ShapeDtypeStruct((T, Dm), BF16),
                   jax.ShapeDtypeStruct((1, Dm), F32), jax.ShapeDtypeStruct((1, Dm), F32),
                   jax.ShapeDtypeStruct((N, T) if out_t else (T, N), BF16)],
        compiler_params=_params(("arbitrary",)),
    )(dy, z, g, b, w, *xa)


def _hg_chunk(qr, fr, ir, gr, l0, l1, gw, st):
    C = qr.shape[-2]
    row = lax.broadcasted_iota(jnp.int32, qr.shape, qr.ndim - 2)
    lb = jax.nn.sigmoid(l0 - l1)
    fg = lb + (1.0 - lb) * jax.nn.sigmoid(fr)
    b = _cumsum(jnp.log(fg))
    q = jax.nn.silu(qr)
    k = 1.0 - fg
    bmid = lax.stop_gradient(jnp.sum(jnp.where(row == C // 2 - 1, b, 0.0), axis=-2, keepdims=True))
    bl = jnp.sum(jnp.where(row == C - 1, b, 0.0), axis=-2, keepdims=True)
    o = mm_nt(q * jnp.exp(b), st)
    sc = _scores(q * jnp.exp(b - bmid), k * jnp.exp(bmid - b))
    ti = lax.broadcasted_iota(jnp.int32, (C, C), 0)
    si = lax.broadcasted_iota(jnp.int32, (C, C), 1)
    sc = jnp.where(si <= ti, sc, 0.0)
    o = o + mm(sc, ir)
    st_new = st * jnp.exp(bl) + mm_tn(ir, k * jnp.exp(bl - b))
    on = o * lax.rsqrt(jnp.mean(o * o, axis=-1, keepdims=True) + RMS_EPS)
    return on * gw * jax.nn.silu(gr), st_new


def _heads(ref, rows):
    return jnp.stack([ref[rows, h * HG_DIM:(h + 1) * HG_DIM].astype(F32) for h in range(HG_HEADS)])


def _unheads(x):
    return jnp.concatenate([x[h] for h in range(HG_HEADS)], axis=-1)


def _hgrn_fwd(pre, lbl, gw, *, name):
    _, T, Dm = pre.shape
    rb = min(HG_RB, T)
    C = min(HG_C, rb)
    ncb = rb // C

    def body(pre_ref, lbl_ref, gw_ref, o_ref, st_ref, s_ref):
        @pl.when(pl.program_id(0) == 0)
        def _():
            s_ref[...] = jnp.zeros_like(s_ref)

        def chunk(ci, carry):
            r0 = pl.multiple_of(ci * C, C)
            rows = pl.ds(r0, C)
            st = s_ref[...]
            st_ref[ci] = st
            out, st_new = _hg_chunk(*[_heads(pre_ref.at[j], rows) for j in range(4)],
                                    _heads(lbl_ref, slice(0, 1)), _heads(lbl_ref, slice(1, 2)), gw_ref[...], st)
            o_ref[rows, :] = _unheads(out).astype(BF16)
            s_ref[...] = st_new
            return carry

        lax.fori_loop(0, ncb, chunk, 0, unroll=True)

    row = pl.BlockSpec((rb, Dm), lambda n: (n, 0))
    return pl.pallas_call(
        body, name=name, grid=(T // rb,),
        in_specs=[pl.BlockSpec((4, rb, Dm), lambda n: (0, n, 0)), pl.BlockSpec((2, Dm), lambda n: (0, 0)),
                  pl.BlockSpec((1, HG_DIM), lambda n: (0, 0))],
        out_specs=[row, pl.BlockSpec((ncb, HG_HEADS, HG_DIM, HG_DIM), lambda n: (n, 0, 0, 0))],
        out_shape=[jax.ShapeDtypeStruct((T, Dm), BF16),
                   jax.ShapeDtypeStruct((T // C, HG_HEADS, HG_DIM, HG_DIM), F32)],
        scratch_shapes=[pltpu.VMEM((HG_HEADS, HG_DIM, HG_DIM), F32)],
        compiler_params=_params(("arbitrary",)),
    )(pre, lbl, gw)


def _hgrn_bwd(pre, lbl, gw, states, dout, *, name, after=None):
    _, T, Dm = pre.shape
    rb = min(HG_RB, T)
    C = min(HG_C, rb)
    ncb = rb // C
    nb = T // rb

    def body(pre_ref, lbl_ref, gw_ref, st_ref, do_ref, dpre_ref, dlbl_ref, dgw_ref, ds_ref):
        @pl.when(pl.program_id(0) == 0)
        def _():
            ds_ref[...] = jnp.zeros_like(ds_ref)
            dlbl_ref[...] = jnp.zeros_like(dlbl_ref)
            dgw_ref[...] = jnp.zeros_like(dgw_ref)

        def chunk(cj, carry):
            ci = ncb - 1 - cj
            r0 = pl.multiple_of(ci * C, C)
            rows = pl.ds(r0, C)
            _, vjp = jax.vjp(_hg_chunk, *[_heads(pre_ref.at[j], rows) for j in range(4)],
                             _heads(lbl_ref, slice(0, 1)), _heads(lbl_ref, slice(1, 2)), gw_ref[...], st_ref[ci])
            *dpre, dl0, dl1, dgw, dst = vjp((_heads(do_ref, rows), ds_ref[...]))
            for j in range(4):
                dpre_ref[j, rows, :] = _unheads(dpre[j]).astype(BF16)
            dlbl_ref[0:1, :] += _unheads(dl0)
            dlbl_ref[1:2, :] += _unheads(dl1)
            dgw_ref[...] += dgw
            ds_ref[...] = dst
            return carry

        lax.fori_loop(0, ncb, chunk, 0, unroll=True)

    row = pl.BlockSpec((rb, Dm), lambda n: (nb - 1 - n, 0))
    lsp = pl.BlockSpec((2, Dm), lambda n: (0, 0))
    gsp = pl.BlockSpec((1, HG_DIM), lambda n: (0, 0))
    pre_spec = pl.BlockSpec((4, rb, Dm), lambda n: (0, nb - 1 - n, 0))
    body, xs, xa = _after(body, 5, after)
    return pl.pallas_call(
        body, name=name, grid=(nb,),
        in_specs=[pre_spec, lsp, gsp, pl.BlockSpec((ncb, HG_HEADS, HG_DIM, HG_DIM), lambda n: (nb - 1 - n, 0, 0, 0)), row] + xs,
        out_specs=[pre_spec, lsp, gsp],
        out_shape=[jax.ShapeDtypeStruct((4, T, Dm), BF16), jax.ShapeDtypeStruct((2, Dm), F32),
                   jax.ShapeDtypeStruct((1, HG_DIM), F32)],
        scratch_shapes=[pltpu.VMEM((HG_HEADS, HG_DIM, HG_DIM), F32)],
        compiler_params=_params(("arbitrary",)),
    )(pre, lbl, gw, states, dout, *xa)


CONV_HALO = 2 * SUBLANES


def _conv_rows(u_ref, scr, w, bias, r0, R):
    cur = u_ref[pl.ds(r0, R), :].astype(F32)
    p0 = pl.multiple_of(jnp.maximum(r0 - CONV_HALO, 0), CONV_HALO)
    scr[0:CONV_HALO, :] = jnp.where(r0 > 0, u_ref[pl.ds(p0, CONV_HALO), :].astype(F32), 0.0)
    scr[CONV_HALO:CONV_HALO + R, :] = cur
    s1 = scr[CONV_HALO - 1:CONV_HALO - 1 + R, :]
    s2 = scr[CONV_HALO - 2:CONV_HALO - 2 + R, :]
    return w[0:1, :] * s2 + w[1:2, :] * s1 + w[2:3, :] * cur + bias, cur, s1, s2


def _conv_gate_fwd(u, wa, wb, ba, bb, *, name):
    _, T, Fd = u.shape
    R = min(CONV_R, T)
    tc = LANES

    def body(u_ref, wa_ref, wb_ref, ba_ref, bb_ref, o_ref, sa, sb):
        wa_, wb_, ba_, bb_ = wa_ref[...], wb_ref[...], ba_ref[...], bb_ref[...]

        def step(ri, carry):
            r0 = pl.multiple_of(ri * R, R)
            ca = _conv_rows(u_ref.at[0], sa, wa_, ba_, r0, R)[0]
            cb = _conv_rows(u_ref.at[1], sb, wb_, bb_, r0, R)[0]
            o_ref[pl.ds(r0, R), :] = (jax.nn.silu(ca) * cb).astype(BF16)
            return carry

        lax.fori_loop(0, T // R, step, 0)

    col = pl.BlockSpec((T, tc), lambda j: (0, j))
    wsp = pl.BlockSpec((3, tc), lambda j: (0, j))
    bsp = pl.BlockSpec((1, tc), lambda j: (0, j))
    both = pl.BlockSpec((2, T, tc), lambda j: (0, 0, j))
    return pl.pallas_call(
        body, name=name, grid=(Fd // tc,), in_specs=[both, wsp, wsp, bsp, bsp], out_specs=col,
        out_shape=jax.ShapeDtypeStruct((T, Fd), BF16),
        scratch_shapes=[pltpu.VMEM((CONV_HALO + R, tc), F32)] * 2,
        compiler_params=_params(("parallel",)),
    )(u, wa, wb, ba, bb)


def _conv_gate_bwd(u, wa, wb, ba, bb, dact, *, name):
    _, T, Fd = u.shape
    R = min(CONV_R, T)
    nr = T // R
    tc = LANES

    def body(u_ref, wa_ref, wb_ref, ba_ref, bb_ref, da_ref,
             du_ref, dwa_ref, dwb_ref, dba_ref, dbb_ref, sa, sb, sda, sdb):
        wa_, wb_, ba_, bb_ = wa_ref[...], wb_ref[...], ba_ref[...], bb_ref[...]
        sda[R:R + SUBLANES, :] = jnp.zeros((SUBLANES, tc), F32)
        sdb[R:R + SUBLANES, :] = jnp.zeros((SUBLANES, tc), F32)

        def taps(dc, cur, s1, s2):
            return jnp.concatenate([jnp.sum(dc * s2, axis=0, keepdims=True), jnp.sum(dc * s1, axis=0, keepdims=True),
                                    jnp.sum(dc * cur, axis=0, keepdims=True)], axis=0)

        def du_rows(sd, dc, w):
            sd[0:R, :] = dc
            du = w[2:3, :] * dc + w[1:2, :] * sd[1:1 + R, :] + w[0:1, :] * sd[2:2 + R, :]
            sd[R:R + SUBLANES, :] = dc[0:SUBLANES]
            return du

        def step(rj, carry):
            dwa, dwb, dba, dbb = carry
            r0 = pl.multiple_of((nr - 1 - rj) * R, R)
            ca, cura, s1a, s2a = _conv_rows(u_ref.at[0], sa, wa_, ba_, r0, R)
            cb, curb, s1b, s2b = _conv_rows(u_ref.at[1], sb, wb_, bb_, r0, R)
            dact_ = da_ref[pl.ds(r0, R), :].astype(F32)
            sg = jax.nn.sigmoid(ca)
            dca = dact_ * cb * (sg * (1.0 + ca * (1.0 - sg)))
            dcb = dact_ * (ca * sg)
            du_ref[0, pl.ds(r0, R), :] = du_rows(sda, dca, wa_).astype(BF16)
            du_ref[1, pl.ds(r0, R), :] = du_rows(sdb, dcb, wb_).astype(BF16)
            return (dwa + taps(dca, cura, s1a, s2a), dwb + taps(dcb, curb, s1b, s2b),
                    dba + jnp.sum(dca, axis=0, keepdims=True), dbb + jnp.sum(dcb, axis=0, keepdims=True))

        z3 = jnp.zeros((3, tc), F32)
        z1 = jnp.zeros((1, tc), F32)
        dwa, dwb, dba, dbb = lax.fori_loop(0, nr, step, (z3, z3, z1, z1))
        dwa_ref[...] = dwa
        dwb_ref[...] = dwb
        dba_ref[...] = dba
        dbb_ref[...] = dbb

    col = pl.BlockSpec((T, tc), lambda j: (0, j))
    wsp = pl.BlockSpec((3, tc), lambda j: (0, j))
    bsp = pl.BlockSpec((1, tc), lambda j: (0, j))
    both = pl.BlockSpec((2, T, tc), lambda j: (0, 0, j))
    return pl.pallas_call(
        body, name=name, grid=(Fd // tc,), in_specs=[both, wsp, wsp, bsp, bsp, col],
        out_specs=[both, wsp, wsp, bsp, bsp],
        out_shape=[jax.ShapeDtypeStruct((2, T, Fd), BF16)] + [jax.ShapeDtypeStruct((3, Fd), F32)] * 2
        + [jax.ShapeDtypeStruct((1, Fd), F32)] * 2,
        scratch_shapes=[pltpu.VMEM((CONV_HALO + R, tc), F32)] * 2 + [pltpu.VMEM((R + SUBLANES, tc), F32)] * 2,
        compiler_params=_params(("parallel",)),
    )(u, wa, wb, ba, bb, dact)


def _bucket_index():
    t = np.arange(SW_WINDOW)[None, :] + SW_WINDOW
    s = np.arange(2 * SW_WINDOW)[:, None]
    dist = np.maximum(t - s, 0)
    exact = REL_BUCKETS // 2
    d = np.maximum(dist, 1).astype(np.float32)
    log_b = exact + (np.log(d / np.float32(exact)) / np.float32(math.log(REL_MAX_DIST / exact))
                     * np.float32(REL_BUCKETS - exact)).astype(np.int32)
    bucket = np.where(dist < exact, dist, np.minimum(log_b, REL_BUCKETS - 1))
    return bucket.astype(np.int32).reshape(1, -1)


BIAS_COLS = SW_WINDOW * 2 * SW_WINDOW
BIAS_TILE = 4096


def _bias_from_table(table, bucket, *, name):
    def body(t_ref, idx_ref, o_ref):
        onehot = (lax.broadcasted_iota(jnp.int32, (REL_BUCKETS, BIAS_TILE), 0) == idx_ref[...]).astype(BF16)
        acc = jnp.zeros((SW_Q_HEADS, BIAS_TILE), F32)
        for piece in _split3(t_ref[...]):
            acc = acc + lax.dot_general(piece, onehot, (((0,), (0,)), ((), ())), preferred_element_type=F32)
        o_ref[...] = acc

    return pl.pallas_call(
        body, name=name, grid=(BIAS_COLS // BIAS_TILE,),
        in_specs=[pl.BlockSpec((REL_BUCKETS, SW_Q_HEADS), lambda j: (0, 0)), pl.BlockSpec((1, BIAS_TILE), lambda j: (0, j))],
        out_specs=pl.BlockSpec((SW_Q_HEADS, BIAS_TILE), lambda j: (0, j)),
        out_shape=jax.ShapeDtypeStruct((SW_Q_HEADS, BIAS_COLS), F32),
        compiler_params=_params(("parallel",)),
    )(table, bucket)


def _table_grad(dbias, bucket, *, name):
    def body(d_ref, idx_ref, o_ref):
        @pl.when(pl.program_id(0) == 0)
        def _():
            o_ref[...] = jnp.zeros_like(o_ref)

        onehot = (lax.broadcasted_iota(jnp.int32, (REL_BUCKETS, BIAS_TILE), 0) == idx_ref[...]).astype(BF16)
        acc = jnp.zeros((REL_BUCKETS, SW_Q_HEADS), F32)
        for piece in _split3(d_ref[...]):
            acc = acc + lax.dot_general(onehot, piece, (((1,), (1,)), ((), ())), preferred_element_type=F32)
        o_ref[...] += acc

    return pl.pallas_call(
        body, name=name, grid=(BIAS_COLS // BIAS_TILE,),
        in_specs=[pl.BlockSpec((SW_Q_HEADS, BIAS_TILE), lambda j: (0, j)), pl.BlockSpec((1, BIAS_TILE), lambda j: (0, j))],
        out_specs=pl.BlockSpec((REL_BUCKETS, SW_Q_HEADS), lambda j: (0, 0)),
        out_shape=jax.ShapeDtypeStruct((REL_BUCKETS, SW_Q_HEADS), F32),
        compiler_params=_params(("arbitrary",)),
    )(dbias, bucket)


KV_DIM = SW_KV_HEADS * SW_HEAD_DIM
GROUP_ROWS = SW_GROUP * SW_HEAD_DIM
GROUP_LANES = SW_GROUP * SW_WINDOW


def _band_mask(n):
    s = lax.broadcasted_iota(jnp.int32, (2 * SW_WINDOW, GROUP_LANES), 0)
    t = (lax.broadcasted_iota(jnp.int32, (2 * SW_WINDOW, GROUP_LANES), 1) & (SW_WINDOW - 1)) + SW_WINDOW
    dist = t - s
    return (dist >= 0) & (dist < SW_WINDOW) & ((n > 0) | (s >= SW_WINDOW))


def _side_by_side(x_ref, g):
    r0 = g * GROUP_ROWS
    return jnp.concatenate([x_ref[r0 + r * SW_HEAD_DIM:r0 + (r + 1) * SW_HEAD_DIM, :] for r in range(SW_GROUP)], axis=1)


def _group_inputs(bias_ref, sink_ref, g):
    heads = range(g * SW_GROUP, (g + 1) * SW_GROUP)
    bias = jnp.concatenate([bias_ref[h] for h in heads], axis=1)
    sink = jnp.concatenate([jnp.broadcast_to(sink_ref[:, h:h + 1], (1, SW_WINDOW)) for h in heads], axis=1)
    return heads, bias, sink


def _kv_pair(kvp_ref, kvc_ref, g):
    ks = slice(g * SW_HEAD_DIM, (g + 1) * SW_HEAD_DIM)
    vs = slice(KV_DIM + g * SW_HEAD_DIM, KV_DIM + (g + 1) * SW_HEAD_DIM)
    kk = jnp.concatenate([kvp_ref[:, ks], kvc_ref[:, ks]], axis=0)
    vv = jnp.concatenate([kvp_ref[:, vs], kvc_ref[:, vs]], axis=0)
    return kk, vv, ks, vs


def _col_max(x):
    return jnp.max(x, axis=0, keepdims=True)


def _col_sum(x):
    return jnp.sum(x, axis=0, keepdims=True)


def _attn_fwd(qt, kv, bias, sinks, *, name):
    Dm, T = qt.shape
    W = SW_WINDOW

    def body(q_ref, kvc_ref, kvp_ref, bias_ref, sink_ref, o_ref):
        mask = _band_mask(pl.program_id(0))
        G = range(SW_KV_HEADS)
        ins = [_group_inputs(bias_ref, sink_ref, g) for g in G]
        kvs = [_kv_pair(kvp_ref, kvc_ref, g) for g in G]
        q = [_side_by_side(q_ref, g) for g in G]
        lg = [jnp.where(mask, mm(kvs[g][0], q[g]) * (SW_HEAD_DIM ** -0.5) + ins[g][1], -jnp.inf) for g in G]
        m = [jnp.maximum(_col_max(lg[g]), ins[g][2]) for g in G]
        p = [jnp.exp(lg[g] - m[g]) for g in G]
        den = [_col_sum(p[g]) + jnp.exp(ins[g][2] - m[g]) for g in G]
        o = [mm_tn(kvs[g][1], p[g]) / den[g] for g in G]
        for g in G:
            for r in range(SW_GROUP):
                o_ref[g * GROUP_ROWS + r * SW_HEAD_DIM:g * GROUP_ROWS + (r + 1) * SW_HEAD_DIM, :] = (
                    o[g][:, r * W:(r + 1) * W].astype(BF16))

    return pl.pallas_call(
        body, name=name, grid=(T // W,),
        in_specs=[pl.BlockSpec((Dm, W), lambda n: (0, n)),
                  pl.BlockSpec((W, 2 * KV_DIM), lambda n: (n, 0)),
                  pl.BlockSpec((W, 2 * KV_DIM), lambda n: (jnp.maximum(n - 1, 0), 0)),
                  pl.BlockSpec((SW_Q_HEADS, 2 * W, W), lambda n: (0, 0, 0)),
                  pl.BlockSpec((1, SW_Q_HEADS), lambda n: (0, 0))],
        out_specs=pl.BlockSpec((Dm, W), lambda n: (0, n)),
        out_shape=jax.ShapeDtypeStruct((Dm, T), BF16),
        compiler_params=_params(("parallel",)),
    )(qt, kv, kv, bias, sinks)


def _attn_bwd(qt, kv, bias, sinks, dot, *, name):
    Dm, T = qt.shape
    W = SW_WINDOW
    nb = T // W

    def body(q_ref, kvc_ref, kvp_ref, bias_ref, sink_ref, do_ref,
             dq_ref, dkv_ref, dbias_ref, dsink_ref, carry_ref):
        @pl.when(pl.program_id(0) == 0)
        def _():
            carry_ref[...] = jnp.zeros_like(carry_ref)
            dbias_ref[...] = jnp.zeros_like(dbias_ref)
            dsink_ref[...] = jnp.zeros_like(dsink_ref)

        n = nb - 1 - pl.program_id(0)
        mask = _band_mask(n)
        lane = lax.broadcasted_iota(jnp.int32, (1, SW_Q_HEADS), 1)
        sc = SW_HEAD_DIM ** -0.5
        G = range(SW_KV_HEADS)
        ins = [_group_inputs(bias_ref, sink_ref, g) for g in G]
        kvs = [_kv_pair(kvp_ref, kvc_ref, g) for g in G]
        q = [_side_by_side(q_ref, g) for g in G]
        do = [_side_by_side(do_ref, g) for g in G]
        lg = [jnp.where(mask, mm(kvs[g][0], q[g]) * sc + ins[g][1], -jnp.inf) for g in G]
        m = [jnp.maximum(_col_max(lg[g]), ins[g][2]) for g in G]
        p = [jnp.exp(lg[g] - m[g]) for g in G]
        ps = [jnp.exp(ins[g][2] - m[g]) for g in G]
        rden = [1.0 / (_col_sum(p[g]) + ps[g]) for g in G]
        pn = [p[g] * rden[g] for g in G]
        dpn = [mm(kvs[g][1], do[g]) for g in G]
        delta = [_col_sum(pn[g] * dpn[g]) for g in G]
        ds = [pn[g] * (dpn[g] - delta[g]) for g in G]
        dsr = [-(ps[g] * rden[g]) * delta[g] for g in G]
        dq = [mm_tn(kvs[g][0], ds[g]) * sc for g in G]
        dkk = [mm_nt(ds[g], q[g]) * sc for g in G]
        dvv = [mm_nt(pn[g], do[g]) for g in G]
        dsink = jnp.zeros((1, SW_Q_HEADS), F32)
        for g in G:
            _, _, ks, vs = kvs[g]
            for r, h in enumerate(ins[g][0]):
                cols = slice(r * W, (r + 1) * W)
                dbias_ref[h] += ds[g][:, cols]
                dq_ref[g * GROUP_ROWS + r * SW_HEAD_DIM:g * GROUP_ROWS + (r + 1) * SW_HEAD_DIM, :] = dq[g][:, cols].astype(BF16)
                dsink = dsink + jnp.where(lane == h, jnp.sum(dsr[g][:, cols], axis=1, keepdims=True), 0.0)
            dkv_ref[:, ks] = (carry_ref[:, ks] + dkk[g][W:]).astype(BF16)
            dkv_ref[:, vs] = (carry_ref[:, vs] + dvv[g][W:]).astype(BF16)
            carry_ref[:, ks] = dkk[g][:W]
            carry_ref[:, vs] = dvv[g][:W]
        dsink_ref[...] += dsink

    rev = lambda n: (nb - 1 - n, 0)
    revt = lambda n: (0, nb - 1 - n)
    return pl.pallas_call(
        body, name=name, grid=(nb,),
        in_specs=[pl.BlockSpec((Dm, W), revt),
                  pl.BlockSpec((W, 2 * KV_DIM), rev),
                  pl.BlockSpec((W, 2 * KV_DIM), lambda n: (jnp.maximum(nb - 2 - n, 0), 0)),
                  pl.BlockSpec((SW_Q_HEADS, 2 * W, W), lambda n: (0, 0, 0)),
                  pl.BlockSpec((1, SW_Q_HEADS), lambda n: (0, 0)),
                  pl.BlockSpec((Dm, W), revt)],
        out_specs=[pl.BlockSpec((Dm, W), revt), pl.BlockSpec((W, 2 * KV_DIM), rev),
                   pl.BlockSpec((SW_Q_HEADS, 2 * W, W), lambda n: (0, 0, 0)),
                   pl.BlockSpec((1, SW_Q_HEADS), lambda n: (0, 0))],
        out_shape=[jax.ShapeDtypeStruct((Dm, T), BF16), jax.ShapeDtypeStruct((T, 2 * KV_DIM), BF16),
                   jax.ShapeDtypeStruct((SW_Q_HEADS, 2 * W, W), F32), jax.ShapeDtypeStruct((1, SW_Q_HEADS), F32)],
        scratch_shapes=[pltpu.VMEM((W, 2 * KV_DIM), F32)],
        compiler_params=_params(("arbitrary",)),
    )(qt, kv, kv, bias, sinks, dot)


def _ffn_fwd(hb, w, l, after=None):
    u = _matmul(hb, w["ffn_in"][l], mode="nn", planes="n", out_dtype=BF16, name=f"ffn{l}_up", tm=1024, after=after)
    act = _conv_gate_fwd(u, w["conv_w_a"][l], w["conv_w_b"][l], w["conv_b_a"][l], w["conv_b_b"][l],
                         name=f"ffn{l}_conv_gate")
    return u, act


def _ffn_bwd(dffb, dh_scaled, hb, u, act, w, l, dact):
    g_out = _matmul(act, dffb, mode="tn", name=f"ffn{l}_down_dw", tm=1408, tn=1024, tk=1024)
    du, dwa, dwb, dba, dbb = _conv_gate_bwd(u, w["conv_w_a"][l], w["conv_w_b"][l], w["conv_b_a"][l], w["conv_b_b"][l],
                                            dact, name=f"ffn{l}_conv_gate_bwd")
    dh = _matmul(du, w["ffn_in"][l], mode="nt", planes="k", add=dh_scaled, add_scale=ALPHA, name=f"ffn{l}_up_dx",
                 tm=1024, tn=1024, tk=FFN_DIM)
    g_in = _matmul(hb, du, mode="tn", planes="n", name=f"ffn{l}_up_dw", tm=1024, tn=FFN_DIM // 2, tk=1024, split_n=True)
    return dh, dict(ffn_out=g_out, ffn_in=g_in, conv_w_a=dwa, conv_w_b=dwb, conv_b_a=dba, conv_b_b=dbb)


def _local_step(x, xb, tgt, w, more_weights, emit):
    bucket = jnp.asarray(_bucket_index())

    pre = _matmul(xb, w["hg_in"], mode="nn", planes="n", out_dtype=BF16, name="hg_in", tm=1024, tn=1024,
                  after=w.get("token"))
    og, states = _hgrn_fwd(pre, w["lb_logits"], w["gnorm"], name="hgrn_fwd")
    z1, h1, h1b = _matmul_ln(og, w["hg_out"], x, w["ln_mix_g"][0], w["ln_mix_b"][0], name="hg_out_ln")
    w = {**w, **more_weights(1, h1b)}
    u0, act0 = _ffn_fwd(h1b, w, 0, after=w.get("token"))
    z2, h2, h2b = _matmul_ln(act0, w["ffn_out"][0], h1, w["ln_ffn_g"][0], w["ln_ffn_b"][0], name="ffn0_down_ln")
    kv = _matmul(h2b, w["kv"], mode="nn", out_dtype=BF16, name="kv_proj")

    bias = _bias_from_table(w["rel_bias"], bucket, name="rel_bias_expand").reshape(SW_Q_HEADS, 2 * SW_WINDOW, SW_WINDOW)
    q1 = _matmul(w["sw_q"], h2b, mode="tt", out_dtype=BF16, name="sw_q", tm=1024, tn=1024)
    o1 = _attn_fwd(q1, kv, bias, w["sinks"], name="attn_fwd")
    z3, h3, h3b = _matmul_ln(o1, w["sw_out"], h2, w["ln_mix_g"][1], w["ln_mix_b"][1], a_t=True, name="sw_out_ln")
    w = {**w, **more_weights(2, h3b)}
    u1, act1 = _ffn_fwd(h3b, w, 1)

    g = {}
    dz, dzb, dg_, db_, loss_tile, dact1 = _matmul_ln(act1, w["ffn_out"][1], h3, w["ln_ffn_g"][1], w["ln_ffn_b"][1],
                                                     tgt=tgt, name="ffn1_down_ln_loss")

    g["ln_ffn_g1"], g["ln_ffn_b1"] = dg_, db_
    dh3, gf1 = _ffn_bwd(dzb, dz, h3b, u1, act1, w, 1, dact1)
    dz, dzb, dg_, db_, do1 = _ln_bwd_matmul(dh3, z3, w["ln_mix_g"][1], w["ln_mix_b"][1], w["sw_out"], out_t=True,
                                            name="ln_mix1_bwd_sw_out_dx")
    g["ln_mix_g1"], g["ln_mix_b1"] = dg_, db_
    g_sw_out = _matmul(o1, dzb, mode="nn", name="sw_out_dw", tm=1024, tn=1024, tk=1024)
    dq1, dkv, dbias, dsinks = _attn_bwd(q1, kv, bias, w["sinks"], do1, name="attn_bwd")
    g["sinks"] = dsinks
    g["rel_bias"] = _table_grad(dbias.reshape(SW_Q_HEADS, BIAS_COLS), bucket, name="rel_bias_grad")
    dh2 = _matmul(dq1, w["sw_q"], mode="tt", add=dz, add_scale=ALPHA, name="sw_q_dx", tn=1024)
    dh2 = _matmul(dkv, w["kv"], mode="nt", add=dh2, name="kv_dx", tn=1024)
    g_sw_q = _matmul(h2b, dq1, mode="tt", name="sw_q_dw", tm=1024, tn=1024, tk=1024)
    g_kv = _matmul(h2b, dkv, mode="tn", name="kv_dw", tm=1024, tn=512, tk=1024)
    tok = emit(1, dict(sw_q=g_sw_q, sw_out=g_sw_out, kv=g_kv, ffn_in=gf1["ffn_in"], ffn_out=gf1["ffn_out"]))

    dz, dzb, dg_, db_, dact0 = _ln_bwd_matmul(dh2, z2, w["ln_ffn_g"][0], w["ln_ffn_b"][0], w["ffn_out"][0],
                                              name="ln_ffn0_bwd_down_dx", after=tok)
    g["ln_ffn_g0"], g["ln_ffn_b0"] = dg_, db_
    dh1, gf0 = _ffn_bwd(dzb, dz, h1b, u0, act0, w, 0, dact0)
    dz, dzb, dg_, db_, dog = _ln_bwd_matmul(dh1, z1, w["ln_mix_g"][0], w["ln_mix_b"][0], w["hg_out"],
                                            name="ln_mix0_bwd_hg_out_dx")
    g["ln_mix_g0"], g["ln_mix_b0"] = dg_, db_
    g_hg_out = _matmul(og, dzb, mode="tn", name="hg_out_dw", tm=1024, tn=1024, tk=1024)
    tok = emit(2, dict(hg_out=g_hg_out, ffn_in=gf0["ffn_in"], ffn_out=gf0["ffn_out"]))
    dpre, g["lb_logits"], g["gnorm"] = _hgrn_bwd(pre, w["lb_logits"], w["gnorm"], states, dog, name="hgrn_bwd", after=tok)
    tok = emit(3, dict(hg_in=_matmul(xb, dpre, mode="tn", planes="n", name="hg_in_dw", tm=1024, tn=1024, tk=1024)))
    dx = _matmul(dpre, w["hg_in"], mode="nt", planes="k", add=dz, add_scale=ALPHA, name="hg_in_dx", tm=1024, tn=1024,
                 tk=1024, after=tok)
    g["conv"] = [{k: gf[k] for k in ("conv_w_a", "conv_w_b", "conv_b_a", "conv_b_b")} for gf in (gf0, gf1)]
    return loss_tile, dx, g


def _adamw(wt, ga, gb, m, v, *, name, rows=None, prev=None):
    R, Cc = wt.shape
    r0, n = rows if rows is not None else (0, R)
    tr = _tile(n, 256, SUBLANES) if n % SUBLANES == 0 else n
    assert r0 % tr == 0
    c1 = 1.0 - ADAM_B1 ** ADAM_STEP
    c2 = 1.0 - ADAM_B2 ** ADAM_STEP
    two = gb is not None
    n_in = 5 if two else 4

    def body(*refs):
        if two:
            w_ref, ga_ref, gb_ref, m_ref, v_ref = refs[:5]
            g_ = ga_ref[...] + gb_ref[...]
        else:
            w_ref, ga_ref, m_ref, v_ref = refs[:4]
            g_ = ga_ref[...]
        g_ref, d_ref, nm_ref, nv_ref = refs[-4:]
        nm = ADAM_B1 * m_ref[...] + (1.0 - ADAM_B1) * g_
        nv = ADAM_B2 * v_ref[...] + (1.0 - ADAM_B2) * (g_ * g_)
        g_ref[...] = g_
        d_ref[...] = -ADAM_LR * ((nm / c1) / (jnp.sqrt(nv / c2) + ADAM_EPS) + ADAM_WD * w_ref[...])
        nm_ref[...] = nm
        nv_ref[...] = nv

    full = pl.BlockSpec((tr, Cc), lambda i: (i + r0 // tr, 0))
    part = pl.BlockSpec((tr, Cc), lambda i: (i, 0))
    args = (wt, ga, gb, m, v) if two else (wt, ga, m, v)
    in_specs = [full] + [part] * (n_in - 3) + [full, full]
    aliases = {}
    if prev is not None:
        args, in_specs = args + tuple(prev), in_specs + [ANY_SPEC] * 4
        aliases = {n_in + t: t for t in range(4)}
    return pl.pallas_call(
        body, name=name, grid=(n // tr,), in_specs=in_specs, out_specs=[full] * 4,
        out_shape=[jax.ShapeDtypeStruct((R, Cc), F32)] * 4, input_output_aliases=aliases,
        compiler_params=_params(("parallel",)),
    )(*args)


HBM_SPEC = pl.BlockSpec(memory_space=pltpu.HBM)
SEM_SPEC = pl.BlockSpec(memory_space=pltpu.SEMAPHORE)
VMEM_SPEC = pl.BlockSpec(memory_space=pltpu.VMEM)
DATAFLOW = pltpu.SideEffectType.DATAFLOW_SIDE_EFFECTING


def _in_hbm(a):
    return pltpu.with_memory_space_constraint(a, pltpu.HBM)


def _place():
    return lax.axis_index("x"), lax.axis_index("y"), lax.axis_index("c")


def _other_chips(x, y):
    return [(1 - x, y), (x, 1 - y), (1 - x, 1 - y)]


def _sum8(v, *, name, after=None):
    r = v.shape[0]

    def body(v_ref, all_ref, o_ref, send_sems, recv_sems, local_sem):
        x, y, c = _place()
        me, sibling = (x, y, c), (x, y, 1 - c)
        chips = _other_chips(x, y)

        def rows(px, py, pc):
            return all_ref.at[pl.ds((4 * px + 2 * py + pc) * r, r), :]

        def copy(k, block, to, src=None):
            return pltpu.make_async_remote_copy(
                src_ref=rows(*block) if src is None else src, dst_ref=rows(*block),
                send_sem=send_sems.at[k], recv_sem=recv_sems.at[k], device_id=to, device_id_type=MESH)

        mine = pltpu.make_async_copy(v_ref, rows(*me), local_sem)
        mine.start()
        first = [copy(0, me, sibling, src=v_ref)]
        first += [copy(1 + j, me, (*chip, c), src=v_ref) for j, chip in enumerate(chips)]
        for cp in first:
            cp.start()
        passed = [copy(4 + j, (*chip, c), sibling) for j, chip in enumerate(chips)]
        for j, chip in enumerate(chips):
            copy(1 + j, (*chip, c), me).wait_recv()
            passed[j].start()
        copy(0, sibling, me).wait_recv()
        for j, chip in enumerate(chips):
            copy(4 + j, (*chip, 1 - c), me).wait_recv()
        for cp in first + passed:
            cp.wait_send()
        mine.wait()
        acc = all_ref[pl.ds(0, r), :]
        for d in range(1, N_DEV):
            acc = acc + all_ref[pl.ds(d * r, r), :]
        o_ref[...] = acc

    body, xs, xa = _after(body, 1, after)
    return pl.pallas_call(
        body, name=name, in_specs=[VMEM_SPEC] + xs, out_specs=[VMEM_SPEC, VMEM_SPEC],
        out_shape=[jax.ShapeDtypeStruct((N_DEV * r, LANES), F32), jax.ShapeDtypeStruct((r, LANES), F32)],
        scratch_shapes=[pltpu.SemaphoreType.DMA((7,)), pltpu.SemaphoreType.DMA((7,)), pltpu.SemaphoreType.DMA],
        compiler_params=pltpu.CompilerParams(vmem_limit_bytes=VMEM_LIMIT),
    )(v, *xa)[1]


def _swap_sibling(vs, *, name):
    n = len(vs)

    def body(*refs):
        src, dst, send_sems, recv_sems = refs[:n], refs[n:2 * n], refs[2 * n], refs[2 * n + 1]
        x, y, c = _place()
        cps = [pltpu.make_async_remote_copy(src_ref=src[i], dst_ref=dst[i], send_sem=send_sems.at[i],
                                            recv_sem=recv_sems.at[i], device_id=(x, y, 1 - c), device_id_type=MESH)
               for i in range(n)]
        for cp in cps:
            cp.start()
        for cp in cps:
            cp.wait()

    return pl.pallas_call(
        body, name=name, in_specs=[HBM_SPEC] * n, out_specs=[HBM_SPEC] * n,
        out_shape=[jax.ShapeDtypeStruct(v.shape, v.dtype) for v in vs],
        scratch_shapes=[pltpu.SemaphoreType.DMA((n,)), pltpu.SemaphoreType.DMA((n,))],
    )(*vs)


def _half(ref, j, c, half):
    return ref.at[j, pl.ds(c * half, half), :]


def _gather_start(shard, after, *, name):
    R, Cc = shard.shape
    half = R // 2

    def body(src, land, send, recv, src_out, land_out, token):
        x, y, c = _place()
        for k, (px, py) in enumerate(_other_chips(x, y)):
            pltpu.make_async_remote_copy(src_ref=src.at[pl.ds(c * half, half), :], dst_ref=_half(land, 2 * x + y, c, half),
                                         send_sem=send.at[k], recv_sem=recv.at[k], device_id=(px, py, c),
                                         device_id_type=MESH).start()
        token[...] = jnp.zeros_like(token)

    land = lax.empty((N_CHIPS, R, Cc), shard.dtype)
    body, xs, xa = _after(body, 2, after)
    out = pl.pallas_call(
        body, name=name, in_specs=[HBM_SPEC, HBM_SPEC] + xs,
        out_specs=[SEM_SPEC, SEM_SPEC, HBM_SPEC, HBM_SPEC, VMEM_SPEC],
        out_shape=[pltpu.SemaphoreType.DMA((3,)), pltpu.SemaphoreType.DMA((3,)), pltpu.HBM(shard.shape, shard.dtype),
                   pltpu.HBM(land.shape, land.dtype), jax.ShapeDtypeStruct((SUBLANES, LANES), F32)],
        input_output_aliases={0: 2, 1: 3},
        compiler_params=pltpu.CompilerParams(has_side_effects=DATAFLOW),
    )(_in_hbm(shard), _in_hbm(land), *xa)
    return out[:4], out[4]


def _gather_wait(handle, after, *, name):
    send_sems, recv_sems, src, land = handle
    half = src.shape[0] // 2

    def body(src_ref, land_ref, send_ref, recv_ref, after_ref, src_out, land_out):
        x, y, c = _place()
        for k, (px, py) in enumerate(_other_chips(x, y)):
            cp = pltpu.make_async_remote_copy(src_ref=src_ref.at[pl.ds(c * half, half), :],
                                              dst_ref=_half(land_ref, 2 * px + py, c, half), send_sem=send_ref.at[k],
                                              recv_sem=recv_ref.at[k], device_id=(px, py, c), device_id_type=MESH)
            cp.wait_send()
            cp.wait_recv()

    return pl.pallas_call(
        body, name=name, in_specs=[HBM_SPEC, HBM_SPEC, SEM_SPEC, SEM_SPEC, ANY_SPEC], out_specs=[HBM_SPEC, HBM_SPEC],
        out_shape=[pltpu.HBM(src.shape, src.dtype), pltpu.HBM(land.shape, land.dtype)],
        input_output_aliases={0: 0, 1: 1},
        compiler_params=pltpu.CompilerParams(has_side_effects=DATAFLOW),
    )(src, land, send_sems, recv_sems, after)[1]


def _fill_sibling(land, *, name):
    _, R, Cc = land.shape
    half = R // 2

    def body(in_ref, o_ref, send_sems, recv_sems):
        x, y, c = _place()
        chips = _other_chips(x, y)
        cps = [pltpu.make_async_remote_copy(src_ref=_half(in_ref, 2 * px + py, c, half),
                                            dst_ref=_half(o_ref, 2 * px + py, c, half), send_sem=send_sems.at[k],
                                            recv_sem=recv_sems.at[k], device_id=(x, y, 1 - c), device_id_type=MESH)
               for k, (px, py) in enumerate(chips)]
        for cp in cps:
            cp.start()
        for k, (px, py) in enumerate(chips):
            pltpu.make_async_remote_copy(src_ref=_half(in_ref, 2 * px + py, 1 - c, half),
                                         dst_ref=_half(o_ref, 2 * px + py, 1 - c, half), send_sem=send_sems.at[k],
                                         recv_sem=recv_sems.at[k], device_id=(x, y, 1 - c), device_id_type=MESH).wait_recv()
        for cp in cps:
            cp.wait_send()

    return pl.pallas_call(
        body, name=name, in_specs=[HBM_SPEC], out_specs=HBM_SPEC, out_shape=jax.ShapeDtypeStruct(land.shape, land.dtype),
        scratch_shapes=[pltpu.SemaphoreType.DMA((3,)), pltpu.SemaphoreType.DMA((3,))],
        input_output_aliases={0: 0},
    )(land)


def _scatter_copies(src, land, send, recv):
    x, y, c = _place()
    return [pltpu.make_async_remote_copy(src_ref=src[i].at[2 * px + py], dst_ref=land[i].at[k], send_sem=send.at[3 * i + k],
                                         recv_sem=recv.at[3 * i + k], device_id=(px, py, c), device_id_type=MESH)
            for i in range(len(src)) for k, (px, py) in enumerate(_other_chips(x, y))]


def _scatter_start(pieces, *, name):
    n = len(pieces)

    def body(*refs):
        src, land, send, recv, token = refs[:n], refs[n:2 * n], refs[2 * n], refs[2 * n + 1], refs[-1]
        for cp in _scatter_copies(src, land, send, recv):
            cp.start()
        token[...] = jnp.zeros_like(token)

    lands = [lax.empty((3,) + p.shape[1:], p.dtype) for p in pieces]
    sems = pltpu.SemaphoreType.DMA((3 * n,))
    out = pl.pallas_call(
        body, name=name, in_specs=[HBM_SPEC] * (2 * n),
        out_specs=[SEM_SPEC, SEM_SPEC] + [HBM_SPEC] * (2 * n) + [VMEM_SPEC],
        out_shape=[sems, sems] + [pltpu.HBM(a.shape, a.dtype) for a in pieces + lands]
        + [jax.ShapeDtypeStruct((SUBLANES, LANES), F32)],
        input_output_aliases={i: 2 + i for i in range(2 * n)},
        compiler_params=pltpu.CompilerParams(has_side_effects=DATAFLOW),
    )(*[_in_hbm(a) for a in pieces + lands])
    return (out[0], out[1], out[2:2 + n], out[2 + n:2 + 2 * n]), out[-1]


def _scatter_wait(handle, after, *, name):
    send_sems, recv_sems, srcs, lands = handle
    n = len(srcs)

    def body(*refs):
        src, land, send, recv = refs[:n], refs[n:2 * n], refs[2 * n], refs[2 * n + 1]
        for cp in _scatter_copies(src, land, send, recv):
            cp.wait_send()
            cp.wait_recv()

    both = list(srcs) + list(lands)
    out = pl.pallas_call(
        body, name=name, in_specs=[HBM_SPEC] * (2 * n) + [SEM_SPEC, SEM_SPEC, ANY_SPEC], out_specs=[HBM_SPEC] * (2 * n),
        out_shape=[pltpu.HBM(a.shape, a.dtype) for a in both],
        input_output_aliases={i: i for i in range(2 * n)},
        compiler_params=pltpu.CompilerParams(has_side_effects=DATAFLOW),
    )(*both, send_sems, recv_sems, after)
    return out[n:]


def _to_bf16(x, *, name, after=None):
    T, Dm = x.shape
    tr = _tile(T, 512, 2 * SUBLANES)

    def body(x_ref, o_ref):
        o_ref[...] = x_ref[...].astype(BF16)

    blk = pl.BlockSpec((tr, Dm), lambda i: (i, 0))
    body, xs, xa = _after(body, 1, after)
    return pl.pallas_call(
        body, name=name, grid=(T // tr,), in_specs=[blk] + xs, out_specs=blk, out_shape=jax.ShapeDtypeStruct((T, Dm), BF16),
        compiler_params=_params(("parallel",)),
    )(x, *xa)


def _chip_sum(pieces, got, chip, *, name):
    _, R, Cc = pieces.shape
    tr = _tile(R, 256, SUBLANES)

    def body(chip_ref, a_ref, g_ref, o_ref):
        o_ref[...] = ((a_ref[...] + g_ref[0].astype(F32)) + g_ref[1].astype(F32)) + g_ref[2].astype(F32)

    return pl.pallas_call(
        body, name=name,
        grid_spec=pltpu.PrefetchScalarGridSpec(
            num_scalar_prefetch=1, grid=(R // tr,),
            in_specs=[pl.BlockSpec((None, tr, Cc), lambda i, ch: (ch[0], i, 0)),
                      pl.BlockSpec((3, tr, Cc), lambda i, ch: (0, i, 0))],
            out_specs=pl.BlockSpec((tr, Cc), lambda i, ch: (i, 0))),
        out_shape=jax.ShapeDtypeStruct((R, Cc), F32),
        compiler_params=_params(("parallel",)),
    )(chip, pieces, got)


PACK_COLS = 1024
SMALL_ROWS = 32


def _pack_rows(parts):
    return jnp.concatenate([p.reshape(-1, PACK_COLS) for p in parts], axis=0)


def _unpack_rows(block, shapes):
    lead = block.shape[:-2]
    out, off = [], 0
    for s in shapes:
        r = int(np.prod(s)) // PACK_COLS
        out.append(block[..., off:off + r, :].reshape(lead + tuple(s)))
        off += r
    assert off == block.shape[-2]
    return out


def _flat128(parts):
    out = []
    for p in parts:
        v = p.reshape(-1)
        pad = (-v.shape[0]) % LANES
        out.append(jnp.pad(v, (0, pad)) if pad else v)
    v = jnp.concatenate(out)
    pad = (-v.shape[0]) % (SUBLANES * LANES)
    if pad:
        v = jnp.pad(v, (0, pad))
    return v.reshape(-1, LANES)


def _unflat128(block, shapes):
    v = block.reshape(-1)
    out, off = [], 0
    for s in shapes:
        n = int(np.prod(s))
        out.append(v[off:off + n].reshape(s))
        off += n + ((-n) % LANES)
    return out


def kernel(x, hgrn_w_in, hgrn_lb_logits, hgrn_gnorm_w, hgrn_w_out, swa_w_q, swa_sinks, swa_w_out, shared_w_kv, rel_bias, ffn_w_in, ffn_conv_w, ffn_conv_b, ffn_w_out, ln_mix_g, ln_mix_b, ln_ffn_g, ln_ffn_b, loss_target, m_hgrn_w_in, m_hgrn_lb_logits, m_hgrn_gnorm_w, m_hgrn_w_out, m_swa_w_q, m_swa_sinks, m_swa_w_out, m_shared_w_kv, m_rel_bias, m_ffn_w_in, m_ffn_conv_w, m_ffn_conv_b, m_ffn_w_out, m_ln_mix_g, m_ln_mix_b, m_ln_ffn_g, m_ln_ffn_b, v_hgrn_w_in, v_hgrn_lb_logits, v_hgrn_gnorm_w, v_hgrn_w_out, v_swa_w_q, v_swa_sinks, v_swa_w_out, v_shared_w_kv, v_rel_bias, v_ffn_w_in, v_ffn_conv_w, v_ffn_conv_b, v_ffn_w_out, v_ln_mix_g, v_ln_mix_b, v_ln_ffn_g, v_ln_ffn_b):
    xi, yi, ci = _place()
    chip = 2 * xi + yi
    Dm = D_MODEL
    FC = 2 * FFN_DIM // N_CHIPS
    Fo = FFN_DIM // N_CHIPS
    Dq = Dm // N_CHIPS
    bf = lambda a: a.astype(BF16)

    small = jnp.concatenate([hgrn_lb_logits.reshape(-1), ffn_conv_w.reshape(-1)])
    n_small = small.shape[0]
    bits = jnp.concatenate(_split3(small))
    bits = jnp.pad(bits, (0, SMALL_ROWS * PACK_COLS - 3 * n_small)).reshape(SMALL_ROWS, PACK_COLS)
    shard0 = _pack_rows([bf(hgrn_w_in), bf(hgrn_w_out), bits])
    shard1 = _pack_rows([bf(swa_w_q), bf(swa_w_out), bf(shared_w_kv), bf(ffn_w_in[0]), bf(ffn_w_out[0])])
    shard2 = _pack_rows([bf(ffn_w_in[1]), bf(ffn_w_out[1])])
    handle0, token0 = _gather_start(shard0, None, name="gather_w0_start")
    xb = _to_bf16(x[0], name="x_to_bf16", after=token0)
    corner = lambda a: a[:2 * SUBLANES, :LANES]
    casts_done = corner(xb) + corner(shard1) + corner(shard2)
    land0 = _fill_sibling(_gather_wait(handle0, casts_done, name="gather_w0_wait"), name="gather_w0_fill")
    all0 = lax.dynamic_update_slice(land0, shard0[None], (chip, 0, 0))
    handle1, token1 = _gather_start(shard1, land0, name="gather_w1_start")
    w_in, w_hg_out, small_all = _unpack_rows(all0, [(Dm, Dm), (Dq, Dm), (SMALL_ROWS, PACK_COLS)])
    parts = small_all.reshape(N_CHIPS, -1)[:, :3 * n_small].reshape(N_CHIPS, 3, n_small).astype(F32)
    vals = (parts[:, 0] + parts[:, 1]) + parts[:, 2]
    lb_full = vals[:, :2 * Dq].reshape(N_CHIPS, 2, Dq).transpose(1, 0, 2).reshape(2, Dm)
    cw_full = vals[:, 2 * Dq:].reshape(N_CHIPS, DEPTH, 3, FC).transpose(1, 2, 0, 3).reshape(DEPTH, 3, 2 * FFN_DIM)

    def ffn_weights(w_fi, w_fo, l):
        halves = jnp.stack([jnp.concatenate([w_fi[0], w_fi[1]], axis=1), jnp.concatenate([w_fi[2], w_fi[3]], axis=1)])
        return {"ffn_in": {l: halves}, "ffn_out": {l: w_fo.reshape(FFN_DIM, Dm)}}

    got = {}

    def more_weights(k, after):
        shard = (shard1, shard2)[k - 1]
        land = _gather_wait(got.pop("handle"), after, name=f"gather_w{k}_wait")
        land = _fill_sibling(land, name=f"gather_w{k}_fill")
        allk = lax.dynamic_update_slice(land, shard[None], (chip, 0, 0))
        if k == 1:
            got["handle"], token2 = _gather_start(shard2, land, name="gather_w2_start")
            w_q, w_o, w_kv, w_fi, w_fo = _unpack_rows(allk, [(Dq, Dm), (Dq, Dm), (Dq, 2 * KV_DIM), (Dm, FC), (Fo, Dm)])
            got.update(ffn_weights(w_fi, w_fo, 0))
            return {"sw_q": w_q.reshape(Dm, Dm), "sw_out": w_o.reshape(Dm, Dm), "kv": w_kv.reshape(Dm, 2 * KV_DIM),
                    "token": token2, **{n: got[n] for n in ("ffn_in", "ffn_out")}}
        w_fi, w_fo = _unpack_rows(allk, [(Dm, FC), (Fo, Dm)])
        new = ffn_weights(w_fi, w_fo, 1)
        return {n: {**got[n], **new[n]} for n in new}

    got["handle"] = handle1

    w = {
        "hg_in": w_in, "hg_out": w_hg_out.reshape(Dm, Dm), "token": token1,
        "lb_logits": lb_full, "gnorm": hgrn_gnorm_w, "sinks": swa_sinks, "rel_bias": rel_bias,
        "conv_w_a": [cw_full[l, :, :FFN_DIM] for l in range(DEPTH)],
        "conv_w_b": [cw_full[l, :, FFN_DIM:] for l in range(DEPTH)],
        "conv_b_a": [ffn_conv_b[l:l + 1, :FFN_DIM] for l in range(DEPTH)],
        "conv_b_b": [ffn_conv_b[l:l + 1, FFN_DIM:] for l in range(DEPTH)],
        "ln_mix_g": [ln_mix_g[l:l + 1] for l in range(DEPTH)], "ln_mix_b": [ln_mix_b[l:l + 1] for l in range(DEPTH)],
        "ln_ffn_g": [ln_ffn_g[l:l + 1] for l in range(DEPTH)], "ln_ffn_b": [ln_ffn_b[l:l + 1] for l in range(DEPTH)],
    }

    sent = {}

    def ffn_pieces(gd):
        return [gd["ffn_in"], gd["ffn_out"].reshape(N_CHIPS, Fo, Dm)]

    def emit(k, gd):
        rows4 = lambda a: a.reshape(N_CHIPS, Dq, a.shape[-1])
        if k == 1:
            pieces = [rows4(gd["sw_q"]), rows4(gd["sw_out"]), rows4(gd["kv"])] + ffn_pieces(gd)
        elif k == 2:
            pieces = ffn_pieces(gd) + [rows4(gd["hg_out"])]
        else:
            pieces = [gd["hg_in"]]
        handle, token = _scatter_start([p.astype(BF16) for p in pieces], name=f"scatter_g{k}_start")
        sent[k] = (handle, pieces)
        return token

    loss_tile, grad_x, g = _local_step(x[0], xb, loss_target[0], w, more_weights, emit)

    wts = dict(hgrn_w_in=hgrn_w_in, hgrn_lb_logits=hgrn_lb_logits, hgrn_gnorm_w=hgrn_gnorm_w, hgrn_w_out=hgrn_w_out,
               swa_w_q=swa_w_q, swa_sinks=swa_sinks, swa_w_out=swa_w_out, shared_w_kv=shared_w_kv, rel_bias=rel_bias,
               ffn_w_in=ffn_w_in, ffn_conv_w=ffn_conv_w, ffn_conv_b=ffn_conv_b, ffn_w_out=ffn_w_out,
               ln_mix_g=ln_mix_g, ln_mix_b=ln_mix_b, ln_ffn_g=ln_ffn_g, ln_ffn_b=ln_ffn_b)
    ms = dict(hgrn_w_in=m_hgrn_w_in, hgrn_lb_logits=m_hgrn_lb_logits, hgrn_gnorm_w=m_hgrn_gnorm_w, hgrn_w_out=m_hgrn_w_out,
              swa_w_q=m_swa_w_q, swa_sinks=m_swa_sinks, swa_w_out=m_swa_w_out, shared_w_kv=m_shared_w_kv, rel_bias=m_rel_bias,
              ffn_w_in=m_ffn_w_in, ffn_conv_w=m_ffn_conv_w, ffn_conv_b=m_ffn_conv_b, ffn_w_out=m_ffn_w_out,
              ln_mix_g=m_ln_mix_g, ln_mix_b=m_ln_mix_b, ln_ffn_g=m_ln_ffn_g, ln_ffn_b=m_ln_ffn_b)
    vs = dict(hgrn_w_in=v_hgrn_w_in, hgrn_lb_logits=v_hgrn_lb_logits, hgrn_gnorm_w=v_hgrn_gnorm_w, hgrn_w_out=v_hgrn_w_out,
              swa_w_q=v_swa_w_q, swa_sinks=v_swa_sinks, swa_w_out=v_swa_w_out, shared_w_kv=v_shared_w_kv, rel_bias=v_rel_bias,
              ffn_w_in=v_ffn_w_in, ffn_conv_w=v_ffn_conv_w, ffn_conv_b=v_ffn_conv_b, ffn_w_out=v_ffn_w_out,
              ln_mix_g=v_ln_mix_g, ln_mix_b=v_ln_mix_b, ln_ffn_g=v_ln_ffn_g, ln_ffn_b=v_ln_ffn_b)
    names = list(wts)
    grads, delta, new_m, new_v = {}, {}, {}, {}

    def update(n, ga, gb, layer=None, prev=None):
        r2 = lambda a: a.reshape(-1, a.shape[-1])
        rows = None if layer is None else (layer * ga.shape[0], ga.shape[0])
        return _adamw(r2(wts[n]), ga, gb, r2(ms[n]), r2(vs[n]), rows=rows, prev=prev,
                      name=f"adamw_{n}" + ("" if layer is None else f"_{layer}"))

    def keep(n, res):
        grads[n], delta[n], new_m[n], new_v[n] = [a.reshape(wts[n].shape) for a in res]

    chip1 = jnp.reshape(chip, (1,)).astype(jnp.int32)
    after = grad_x
    for k in (1, 2, 3):
        handle, pieces = sent[k]
        lands = _scatter_wait(handle, after, name=f"scatter_g{k}_wait")
        parts = [_chip_sum(p, l, chip1, name=f"scatter_g{k}_sum{i}") for i, (p, l) in enumerate(zip(pieces, lands))]
        sibs = _swap_sibling(parts, name=f"scatter_g{k}_swap")
        if k == 1:
            for n, ga, gb in zip(["swa_w_q", "swa_w_out", "shared_w_kv"], parts[:3], sibs[:3]):
                keep(n, update(n, ga, gb))
            ffn_in_1 = update("ffn_w_in", parts[3], sibs[3], layer=1)
            ffn_out_1 = update("ffn_w_out", parts[4], sibs[4], layer=1)
            after = ffn_out_1[3]
        elif k == 2:
            keep("ffn_w_in", update("ffn_w_in", parts[0], sibs[0], layer=0, prev=ffn_in_1))
            keep("ffn_w_out", update("ffn_w_out", parts[1], sibs[1], layer=0, prev=ffn_out_1))
            keep("hgrn_w_out", update("hgrn_w_out", parts[2], sibs[2]))
            after = new_v["hgrn_w_out"]
        else:
            keep("hgrn_w_in", update("hgrn_w_in", parts[0], sibs[0]))

    small_shapes = [(SUBLANES, LANES), (2, Dm), (1, HG_DIM), (1, SW_Q_HEADS), (REL_BUCKETS, SW_Q_HEADS),
                    (DEPTH, 3, 2 * FFN_DIM), (DEPTH, 2 * FFN_DIM)] + [(DEPTH, Dm)] * 4
    gc = g["conv"]
    conv_w_g = jnp.stack([jnp.concatenate([gc[l]["conv_w_a"], gc[l]["conv_w_b"]], axis=1) for l in range(DEPTH)])
    conv_b_g = jnp.concatenate([jnp.concatenate([gc[l]["conv_b_a"], gc[l]["conv_b_b"]], axis=1) for l in range(DEPTH)], axis=0)
    ln_g = [jnp.concatenate([g[f"{n}0"], g[f"{n}1"]], axis=0) for n in ("ln_mix_g", "ln_mix_b", "ln_ffn_g", "ln_ffn_b")]
    small_out = _sum8(_flat128([loss_tile, g["lb_logits"], g["gnorm"], g["sinks"], g["rel_bias"], conv_w_g, conv_b_g] + ln_g),
                      name="sum_small")
    (loss_t, g_lb, g_gn, g_sinks, g_rel, g_cw, g_cb, g_lmg, g_lmb, g_lfg, g_lfb) = _unflat128(small_out, small_shapes)
    loss = loss_t[0, 0]
    g_lb = lax.dynamic_slice_in_dim(g_lb, chip * Dq, Dq, axis=1)
    g_cw = lax.dynamic_slice_in_dim(g_cw, chip * FC, FC, axis=2)
    small_g = dict(hgrn_lb_logits=g_lb, hgrn_gnorm_w=g_gn, swa_sinks=g_sinks, rel_bias=g_rel, ffn_conv_w=g_cw,
                   ffn_conv_b=g_cb, ln_mix_g=g_lmg, ln_mix_b=g_lmb, ln_ffn_g=g_lfg, ln_ffn_b=g_lfb)
    small_names = list(small_g)
    sshapes = [wts[n].shape for n in small_names]
    _, d_, m_, v_ = _adamw(_flat128([wts[n] for n in small_names]), _flat128([small_g[n] for n in small_names]), None,
                           _flat128([ms[n] for n in small_names]), _flat128([vs[n] for n in small_names]), name="adamw_small")
    for n, a, b_, c_ in zip(small_names, _unflat128(d_, sshapes), _unflat128(m_, sshapes), _unflat128(v_, sshapes)):
        grads[n], delta[n], new_m[n], new_v[n] = small_g[n], a, b_, c_

    return (loss, grad_x[None], *[grads[n] for n in names], *[delta[n] for n in names],
            *[new_m[n] for n in names], *[new_v[n] for n in names])
```

```python
import math

import numpy as np
import jax
import jax.numpy as jnp
from jax import lax
from jax.experimental import pallas as pl
from jax.experimental.pallas import tpu as pltpu

F32 = jnp.float32
BF16 = jnp.bfloat16
MESH = pl.DeviceIdType.MESH

D_MODEL = 1024
DEPTH = 2
HG_HEADS = 8
HG_DIM = 128
SW_Q_HEADS = 16
SW_KV_HEADS = 4
SW_HEAD_DIM = 64
SW_GROUP = 4
SW_WINDOW = 128
REL_BUCKETS = 32
REL_MAX_DIST = 128
FFN_DIM = 2816
ALPHA = (2.0 * DEPTH) ** 0.25
LN_EPS = 1e-5
RMS_EPS = 1e-6
ADAM_LR = 0.001
ADAM_B1 = 0.9
ADAM_B2 = 0.999
ADAM_EPS = 1e-08
ADAM_WD = 0.01
ADAM_STEP = 10

VMEM_BYTES_V7X = 64 * 1024 * 1024
VMEM_LIMIT = VMEM_BYTES_V7X - 8 * 1024 * 1024
LANES = 128
SUBLANES = 8

HG_C = 64
HG_RB = 256
ROW_TILE = 256
CONV_R = 128
N_CHIPS = 4
N_DEV = 8

ANY_SPEC = pl.BlockSpec(memory_space=pl.ANY)


def _after(body, n_in, after):
    if after is None:
        return body, [], ()

    def wrapped(*refs):
        return body(*refs[:n_in], *refs[n_in + 1:])

    return wrapped, [ANY_SPEC], (after,)


def _params(sem=None):
    return pltpu.CompilerParams(dimension_semantics=sem, vmem_limit_bytes=VMEM_LIMIT)


def _tile(n, pref, unit=LANES):
    if n <= pref:
        return n
    best = None
    for t in range(unit, pref + 1, unit):
        if n % t == 0:
            best = t
    assert best is not None, (n, pref, unit)
    return best


def _dot(a, b, ca, cb):
    nb = a.ndim - 2
    batch = tuple(range(nb))
    return lax.dot_general(a.astype(BF16), b.astype(BF16), (((nb + ca,), (nb + cb,)), (batch, batch)),
                           preferred_element_type=F32)


@jax.custom_vjp
def mm(a, b):
    return _dot(a, b, 1, 0)


@jax.custom_vjp
def mm_nt(a, b):
    return _dot(a, b, 1, 1)


@jax.custom_vjp
def mm_tn(a, b):
    return _dot(a, b, 0, 0)


mm.defvjp(lambda a, b: (mm(a, b), (a, b)), lambda r, ct: (mm_nt(ct, r[1]), mm_tn(r[0], ct)))
mm_nt.defvjp(lambda a, b: (mm_nt(a, b), (a, b)), lambda r, ct: (mm(ct, r[1]), mm_tn(ct, r[0])))
mm_tn.defvjp(lambda a, b: (mm_tn(a, b), (a, b)), lambda r, ct: (mm_nt(r[1], ct), mm(r[0], ct)))


def _split2(x):
    hi = x.astype(BF16)
    return hi, (x - hi.astype(F32)).astype(BF16)


@jax.custom_vjp
def _scores(qt, kt):
    return _dot(qt, kt, 1, 1)


def _scores_bwd(r, ct):
    (qh, ql), (kh, kl) = _split2(r[0]), _split2(r[1])
    return _dot(ct, kh, 1, 0) + _dot(ct, kl, 1, 0), _dot(ct, qh, 0, 0) + _dot(ct, ql, 0, 0)


_scores.defvjp(lambda a, b: (_scores(a, b), (a, b)), _scores_bwd)


def _split3(x):
    hi = x.astype(BF16)
    r1 = x - hi.astype(F32)
    mid = r1.astype(BF16)
    lo = (r1 - mid.astype(F32)).astype(BF16)
    return hi, mid, lo


def _cumsum_impl(x):
    ax = x.ndim - 2
    n = x.shape[ax]
    row = lax.broadcasted_iota(jnp.int32, x.shape, ax)
    d = 1
    while d < n:
        x = x + jnp.where(row >= d, pltpu.roll(x, d, ax), 0.0)
        d *= 2
    return x


def _cumsum_rev_impl(x):
    ax = x.ndim - 2
    n = x.shape[ax]
    row = lax.broadcasted_iota(jnp.int32, x.shape, ax)
    d = 1
    while d < n:
        x = x + jnp.where(row < n - d, pltpu.roll(x, n - d, ax), 0.0)
        d *= 2
    return x


@jax.custom_vjp
def _cumsum(x):
    return _cumsum_impl(x)


_cumsum.defvjp(lambda x: (_cumsum_impl(x), None), lambda _, ct: (_cumsum_rev_impl(ct),))


def _matmul(a, b, *, mode, name, out_dtype=F32, add=None, add_scale=1.0, tm=512, tn=1408, tk=1408, after=None,
            split_n=False, planes=None, also_bf16=False):
    P = b.shape[0] if planes else 1
    a2, b2 = a.shape[-2:], b.shape[-2:]
    (M, K) = a2 if mode[0] == "n" else a2[::-1]
    (K2, N) = b2 if mode[1] == "n" else b2[::-1]
    assert K == K2, (a.shape, b.shape, mode)
    assert a.ndim == (3 if planes == "k" else 2) and b.ndim == (3 if planes else 2)
    tm, tn, tk = _tile(M, tm), _tile(N, tn), _tile(K, tk)
    nj, nkp = N // tn, K // tk
    nk = nkp * (P if planes == "k" else 1)
    ca, cb = (1 if mode[0] == "n" else 0), (0 if mode[1] == "n" else 1)
    a_blk, a_idx = ((tk, tm), lambda i, k: (k, i)) if mode[0] == "t" else ((tm, tk), lambda i, k: (i, k))
    b_blk, b_idx = ((tn, tk), lambda k, j: (j, k)) if mode[1] == "t" else ((tk, tn), lambda k, j: (k, j))
    if planes == "k":
        a_spec = pl.BlockSpec((None,) + a_blk, lambda i, j, k: (k // nkp,) + a_idx(i, k % nkp))
        b_spec = pl.BlockSpec((None,) + b_blk, lambda i, j, k: (k // nkp,) + b_idx(k % nkp, j))
    else:
        a_spec = pl.BlockSpec(a_blk, lambda i, j, k: a_idx(i, k))
        b_spec = (pl.BlockSpec((None,) + b_blk, lambda i, j, k: (j // nj,) + b_idx(k, j % nj)) if planes == "n"
                  else pl.BlockSpec(b_blk, lambda i, j, k: b_idx(k, j)))
    if split_n:
        o_spec, out_shape = pl.BlockSpec((None, tm, tn), lambda i, j, k: (j, i, 0)), (P * nj if planes == "n" else nj, M, tn)
    elif planes == "n":
        o_spec, out_shape = pl.BlockSpec((None, tm, tn), lambda i, j, k: (j // nj, i, j % nj)), (P, M, N)
    else:
        o_spec, out_shape = pl.BlockSpec((tm, tn), lambda i, j, k: (i, j)), (M, N)
    has_add = add is not None
    assert not (has_add and (split_n or planes == "n"))

    def finish(r, add_ref, o_refs):
        if has_add:
            r = r + add_scale * add_ref[...]
        o_refs[0][...] = r.astype(out_dtype)
        if also_bf16:
            o_refs[1][...] = r.astype(BF16)

    def body(*refs):
        a_ref, b_ref = refs[:2]
        add_ref = refs[2] if has_add else None
        first = 3 if has_add else 2
        o_ref = refs[first:first + (2 if also_bf16 else 1)]
        if nk == 1:
            finish(_dot(a_ref[...], b_ref[...], ca, cb), add_ref, o_ref)
            return
        acc_ref = refs[-1]
        k = pl.program_id(2)

        @pl.when(k == 0)
        def _():
            acc_ref[...] = jnp.zeros_like(acc_ref)

        acc_ref[...] += _dot(a_ref[...], b_ref[...], ca, cb)

        @pl.when(k == nk - 1)
        def _():
            finish(acc_ref[...], add_ref, o_ref)

    in_specs = [a_spec, b_spec] + ([o_spec] if has_add else [])
    args = (a, b) + ((add,) if has_add else ())
    body, xs, xa = _after(body, len(args), after)
    in_specs, args = in_specs + xs, args + xa
    out_shapes = [jax.ShapeDtypeStruct(out_shape, out_dtype)] + ([jax.ShapeDtypeStruct(out_shape, BF16)] if also_bf16 else [])
    out = pl.pallas_call(
        body, name=name, grid=(M // tm, nj * (P if planes == "n" else 1), nk), in_specs=in_specs,
        out_specs=[o_spec] * len(out_shapes), out_shape=out_shapes,
        scratch_shapes=[pltpu.VMEM((tm, tn), F32)] if nk > 1 else [],
        compiler_params=_params(("parallel", "parallel", "arbitrary")),
    )(*args)
    return tuple(out) if also_bf16 else out[0]


def _ln(z, g, b):
    mu = jnp.mean(z, axis=-1, keepdims=True)
    zc = z - mu
    var = jnp.mean(zc * zc, axis=-1, keepdims=True)
    return zc * lax.rsqrt(var + LN_EPS) * g + b


def _matmul_ln(a, b, h, g, bias, *, name, tgt=None, tm=512, a_t=False):
    (T, K), (K2, Dm) = (a.shape[::-1] if a_t else a.shape), b.shape
    assert K == K2 and h.shape == (T, Dm)
    tm = _tile(T, tm, SUBLANES)
    last = tgt is not None

    def body(*refs):
        a_ref, b_ref, h_ref, g_ref, bias_ref = refs[:5]
        z = ALPHA * h_ref[...] + _dot(a_ref[...], b_ref[...], 0 if a_t else 1, 0)
        if not last:
            z_ref, y_ref, yb_ref = refs[5:]
            y = _ln(z, g_ref[...], bias_ref[...])
            z_ref[...] = z
            y_ref[...] = y
            yb_ref[...] = y.astype(BF16)
            return
        t_ref, dz_ref, dzb_ref, dgb_ref, l_ref, da_ref = refs[5:]

        @pl.when(pl.program_id(0) == 0)
        def _():
            dgb_ref[...] = jnp.zeros_like(dgb_ref)
            l_ref[...] = jnp.zeros_like(l_ref)

        y, vjp = jax.vjp(_ln, z, g_ref[...], bias_ref[...])
        e = y - t_ref[...]
        dz, dg, db = vjp(e * (1.0 / Dm))
        l_ref[...] += 0.5 * jnp.sum(jnp.mean(e * e, axis=-1, keepdims=True), axis=0, keepdims=True)
        dzb = dz.astype(BF16)
        dz_ref[...] = dz
        dzb_ref[...] = dzb
        dgb_ref[...] += jnp.concatenate([dg, db], axis=0)
        da_ref[...] = _dot(dzb, b_ref[...], 1, 1).astype(BF16)

    row = pl.BlockSpec((tm, Dm), lambda i: (i, 0))
    vec = pl.BlockSpec((1, Dm), lambda i: (0, 0))
    a_spec = pl.BlockSpec((K, tm), lambda i: (0, i)) if a_t else pl.BlockSpec((tm, K), lambda i: (i, 0))
    in_specs = [a_spec, pl.BlockSpec((K, Dm), lambda i: (0, 0)), row, vec, vec]
    f32, b16 = jax.ShapeDtypeStruct((T, Dm), F32), jax.ShapeDtypeStruct((T, Dm), BF16)
    if not last:
        return pl.pallas_call(
            body, name=name, grid=(T // tm,), in_specs=in_specs, out_specs=[row, row, row], out_shape=[f32, f32, b16],
            compiler_params=_params(("parallel",)),
        )(a, b, h, g, bias)
    assert not a_t
    return pl.pallas_call(
        body, name=name, grid=(T // tm,), in_specs=in_specs + [row],
        out_specs=[row, row, pl.BlockSpec((2, Dm), lambda i: (0, 0)), pl.BlockSpec((SUBLANES, LANES), lambda i: (0, 0)), a_spec],
        out_shape=[f32, b16, jax.ShapeDtypeStruct((2, Dm), F32), jax.ShapeDtypeStruct((SUBLANES, LANES), F32),
                   jax.ShapeDtypeStruct((T, K), BF16)],
        compiler_params=_params(("arbitrary",)),
    )(a, b, h, g, bias, tgt)


def _ln_bwd_matmul(dy, z, g, b, w, *, name, out_t=False, tm=512, after=None):
    T, Dm = z.shape
    N = w.shape[0]
    tm = _tile(T, tm, LANES if out_t else SUBLANES)

    def body(dy_ref, z_ref, g_ref, b_ref, w_ref, dz_ref, dzb_ref, dgb_ref, o_ref):
        @pl.when(pl.program_id(0) == 0)
        def _():
            dgb_ref[...] = jnp.zeros_like(dgb_ref)

        _, vjp = jax.vjp(_ln, z_ref[...], g_ref[...], b_ref[...])
        dz, dg, db = vjp(dy_ref[...])
        dzb = dz.astype(BF16)
        dz_ref[...] = dz
        dzb_ref[...] = dzb
        dgb_ref[...] += jnp.concatenate([dg, db], axis=0)
        o_ref[...] = (_dot(w_ref[...], dzb, 1, 1) if out_t else _dot(dzb, w_ref[...], 1, 1)).astype(BF16)

    row = pl.BlockSpec((tm, Dm), lambda i: (i, 0))
    vec = pl.BlockSpec((1, Dm), lambda i: (0, 0))
    o_spec = pl.BlockSpec((N, tm), lambda i: (0, i)) if out_t else pl.BlockSpec((tm, N), lambda i: (i, 0))
    body, xs, xa = _after(body, 5, after)
    return pl.pallas_call(
        body, name=name, grid=(T // tm,), in_specs=[row, row, vec, vec, pl.BlockSpec((N, Dm), lambda i: (0, 0))] + xs,
        out_specs=[row, row, pl.BlockSpec((2, Dm), lambda i: (0, 0)), o_spec],
        out_shape=[jax.ShapeDtypeStruct((T, Dm), F32), jax.ShapeDtypeStruct((T, Dm), BF16),
                   jax.ShapeDtypeStruct((2, Dm), F32), jax.ShapeDtypeStruct((N, T) if out_t else (T, N), BF16)],
        compiler_params=_params(("arbitrary",)),
    )(dy, z, g, b, w, *xa)


def _hg_chunk(qr, fr, ir, gr, l0, l1, gw, st):
    C = qr.shape[-2]
    row = lax.broadcasted_iota(jnp.int32, qr.shape, qr.ndim - 2)
    lb = jax.nn.sigmoid(l0 - l1)
    fg = lb + (1.0 - lb) * jax.nn.sigmoid(fr)
    b = _cumsum(jnp.log(fg))
    q = jax.nn.silu(qr)
    k = 1.0 - fg
    bmid = lax.stop_gradient(jnp.sum(jnp.where(row == C // 2 - 1, b, 0.0), axis=-2, keepdims=True))
    bl = jnp.sum(jnp.where(row == C - 1, b, 0.0), axis=-2, keepdims=True)
    o = mm_nt(q * jnp.exp(b), st)
    sc = _scores(q * jnp.exp(b - bmid), k * jnp.exp(bmid - b))
    ti = lax.broadcasted_iota(jnp.int32, (C, C), 0)
    si = lax.broadcasted_iota(jnp.int32, (C, C), 1)
    sc = jnp.where(si <= ti, sc, 0.0)
    o = o + mm(sc, ir)
    st_new = st * jnp.exp(bl) + mm_tn(ir, k * jnp.exp(bl - b))
    on = o * lax.rsqrt(jnp.mean(o * o, axis=-1, keepdims=True) + RMS_EPS)
    return on * gw * jax.nn.silu(gr), st_new


def _heads(ref, rows):
    return jnp.stack([ref[rows, h * HG_DIM:(h + 1) * HG_DIM].astype(F32) for h in range(HG_HEADS)])


def _unheads(x):
    return jnp.concatenate([x[h] for h in range(HG_HEADS)], axis=-1)


def _hgrn_fwd(pre, lbl, gw, *, name):
    _, T, Dm = pre.shape
    rb = min(HG_RB, T)
    C = min(HG_C, rb)
    ncb = rb // C

    def body(pre_ref, lbl_ref, gw_ref, o_ref, st_ref, s_ref):
        @pl.when(pl.program_id(0) == 0)
        def _():
            s_ref[...] = jnp.zeros_like(s_ref)

        def chunk(ci, carry):
            r0 = pl.multiple_of(ci * C, C)
            rows = pl.ds(r0, C)
            st = s_ref[...]
            st_ref[ci] = st
            out, st_new = _hg_chunk(*[_heads(pre_ref.at[j], rows) for j in range(4)],
                                    _heads(lbl_ref, slice(0, 1)), _heads(lbl_ref, slice(1, 2)), gw_ref[...], st)
            o_ref[rows, :] = _unheads(out).astype(BF16)
            s_ref[...] = st_new
            return carry

        lax.fori_loop(0, ncb, chunk, 0, unroll=True)

    row = pl.BlockSpec((rb, Dm), lambda n: (n, 0))
    return pl.pallas_call(
        body, name=name, grid=(T // rb,),
        in_specs=[pl.BlockSpec((4, rb, Dm), lambda n: (0, n, 0)), pl.BlockSpec((2, Dm), lambda n: (0, 0)),
                  pl.BlockSpec((1, HG_DIM), lambda n: (0, 0))],
        out_specs=[row, pl.BlockSpec((ncb, HG_HEADS, HG_DIM, HG_DIM), lambda n: (n, 0, 0, 0))],
        out_shape=[jax.ShapeDtypeStruct((T, Dm), BF16),
                   jax.ShapeDtypeStruct((T // C, HG_HEADS, HG_DIM, HG_DIM), F32)],
        scratch_shapes=[pltpu.VMEM((HG_HEADS, HG_DIM, HG_DIM), F32)],
        compiler_params=_params(("arbitrary",)),
    )(pre, lbl, gw)


def _hgrn_bwd(pre, lbl, gw, states, dout, *, name, after=None):
    _, T, Dm = pre.shape
    rb = min(HG_RB, T)
    C = min(HG_C, rb)
    ncb = rb // C
    nb = T // rb

    def body(pre_ref, lbl_ref, gw_ref, st_ref, do_ref, dpre_ref, dlbl_ref, dgw_ref, ds_ref):
        @pl.when(pl.program_id(0) == 0)
        def _():
            ds_ref[...] = jnp.zeros_like(ds_ref)
            dlbl_ref[...] = jnp.zeros_like(dlbl_ref)
            dgw_ref[...] = jnp.zeros_like(dgw_ref)

        def chunk(cj, carry):
            ci = ncb - 1 - cj
            r0 = pl.multiple_of(ci * C, C)
            rows = pl.ds(r0, C)
            _, vjp = jax.vjp(_hg_chunk, *[_heads(pre_ref.at[j], rows) for j in range(4)],
                             _heads(lbl_ref, slice(0, 1)), _heads(lbl_ref, slice(1, 2)), gw_ref[...], st_ref[ci])
            *dpre, dl0, dl1, dgw, dst = vjp((_heads(do_ref, rows), ds_ref[...]))
            for j in range(4):
                dpre_ref[j, rows, :] = _unheads(dpre[j]).astype(BF16)
            dlbl_ref[0:1, :] += _unheads(dl0)
            dlbl_ref[1:2, :] += _unheads(dl1)
            dgw_ref[...] += dgw
            ds_ref[...] = dst
            return carry

        lax.fori_loop(0, ncb, chunk, 0, unroll=True)

    row = pl.BlockSpec((rb, Dm), lambda n: (nb - 1 - n, 0))
    lsp = pl.BlockSpec((2, Dm), lambda n: (0, 0))
    gsp = pl.BlockSpec((1, HG_DIM), lambda n: (0, 0))
    pre_spec = pl.BlockSpec((4, rb, Dm), lambda n: (0, nb - 1 - n, 0))
    body, xs, xa = _after(body, 5, after)
    return pl.pallas_call(
        body, name=name, grid=(nb,),
        in_specs=[pre_spec, lsp, gsp, pl.BlockSpec((ncb, HG_HEADS, HG_DIM, HG_DIM), lambda n: (nb - 1 - n, 0, 0, 0)), row] + xs,
        out_specs=[pre_spec, lsp, gsp],
        out_shape=[jax.ShapeDtypeStruct((4, T, Dm), BF16), jax.ShapeDtypeStruct((2, Dm), F32),
                   jax.ShapeDtypeStruct((1, HG_DIM), F32)],
        scratch_shapes=[pltpu.VMEM((HG_HEADS, HG_DIM, HG_DIM), F32)],
        compiler_params=_params(("arbitrary",)),
    )(pre, lbl, gw, states, dout, *xa)


CONV_HALO = 2 * SUBLANES


def _conv_rows(u_ref, scr, w, bias, r0, R):
    cur = u_ref[pl.ds(r0, R), :].astype(F32)
    p0 = pl.multiple_of(jnp.maximum(r0 - CONV_HALO, 0), CONV_HALO)
    scr[0:CONV_HALO, :] = jnp.where(r0 > 0, u_ref[pl.ds(p0, CONV_HALO), :].astype(F32), 0.0)
    scr[CONV_HALO:CONV_HALO + R, :] = cur
    s1 = scr[CONV_HALO - 1:CONV_HALO - 1 + R, :]
    s2 = scr[CONV_HALO - 2:CONV_HALO - 2 + R, :]
    return w[0:1, :] * s2 + w[1:2, :] * s1 + w[2:3, :] * cur + bias, cur, s1, s2


def _conv_gate_fwd(u, wa, wb, ba, bb, *, name):
    _, T, Fd = u.shape
    R = min(CONV_R, T)
    tc = LANES

    def body(u_ref, wa_ref, wb_ref, ba_ref, bb_ref, o_ref, sa, sb):
        wa_, wb_, ba_, bb_ = wa_ref[...], wb_ref[...], ba_ref[...], bb_ref[...]

        def step(ri, carry):
            r0 = pl.multiple_of(ri * R, R)
            ca = _conv_rows(u_ref.at[0], sa, wa_, ba_, r0, R)[0]
            cb = _conv_rows(u_ref.at[1], sb, wb_, bb_, r0, R)[0]
            o_ref[pl.ds(r0, R), :] = (jax.nn.silu(ca) * cb).astype(BF16)
            return carry

        lax.fori_loop(0, T // R, step, 0)

    col = pl.BlockSpec((T, tc), lambda j: (0, j))
    wsp = pl.BlockSpec((3, tc), lambda j: (0, j))
    bsp = pl.BlockSpec((1, tc), lambda j: (0, j))
    both = pl.BlockSpec((2, T, tc), lambda j: (0, 0, j))
    return pl.pallas_call(
        body, name=name, grid=(Fd // tc,), in_specs=[both, wsp, wsp, bsp, bsp], out_specs=col,
        out_shape=jax.ShapeDtypeStruct((T, Fd), BF16),
        scratch_shapes=[pltpu.VMEM((CONV_HALO + R, tc), F32)] * 2,
        compiler_params=_params(("parallel",)),
    )(u, wa, wb, ba, bb)


def _conv_gate_bwd(u, wa, wb, ba, bb, dact, *, name):
    _, T, Fd = u.shape
    R = min(CONV_R, T)
    nr = T // R
    tc = LANES

    def body(u_ref, wa_ref, wb_ref, ba_ref, bb_ref, da_ref,
             du_ref, dp_ref, sa, sb, sda, sdb):
        wa_, wb_, ba_, bb_ = wa_ref[...], wb_ref[...], ba_ref[...], bb_ref[...]
        sda[R:R + SUBLANES, :] = jnp.zeros((SUBLANES, tc), F32)
        sdb[R:R + SUBLANES, :] = jnp.zeros((SUBLANES, tc), F32)

        def taps(dc, cur, s1, s2):
            return jnp.concatenate([jnp.sum(dc * s2, axis=0, keepdims=True), jnp.sum(dc * s1, axis=0, keepdims=True),
                                    jnp.sum(dc * cur, axis=0, keepdims=True)], axis=0)

        def du_rows(sd, dc, w):
            sd[0:R, :] = dc
            du = w[2:3, :] * dc + w[1:2, :] * sd[1:1 + R, :] + w[0:1, :] * sd[2:2 + R, :]
            sd[R:R + SUBLANES, :] = dc[0:SUBLANES]
            return du

        def step(rj, carry):
            dwa, dwb, dba, dbb = carry
            r0 = pl.multiple_of((nr - 1 - rj) * R, R)
            ca, cura, s1a, s2a = _conv_rows(u_ref.at[0], sa, wa_, ba_, r0, R)
            cb, curb, s1b, s2b = _conv_rows(u_ref.at[1], sb, wb_, bb_, r0, R)
            dact_ = da_ref[pl.ds(r0, R), :].astype(F32)
            sg = jax.nn.sigmoid(ca)
            dca = dact_ * cb * (sg * (1.0 + ca * (1.0 - sg)))
            dcb = dact_ * (ca * sg)
            du_ref[0, pl.ds(r0, R), :] = du_rows(sda, dca, wa_).astype(BF16)
            du_ref[1, pl.ds(r0, R), :] = du_rows(sdb, dcb, wb_).astype(BF16)
            return (dwa + taps(dca, cura, s1a, s2a), dwb + taps(dcb, curb, s1b, s2b),
                    dba + jnp.sum(dca, axis=0, keepdims=True), dbb + jnp.sum(dcb, axis=0, keepdims=True))

        z3 = jnp.zeros((3, tc), F32)
        z1 = jnp.zeros((1, tc), F32)
        dwa, dwb, dba, dbb = lax.fori_loop(0, nr, step, (z3, z3, z1, z1))
        dp_ref[0] = jnp.concatenate([dwa, dba], axis=0)
        dp_ref[1] = jnp.concatenate([dwb, dbb], axis=0)

    col = pl.BlockSpec((T, tc), lambda j: (0, j))
    wsp = pl.BlockSpec((3, tc), lambda j: (0, j))
    bsp = pl.BlockSpec((1, tc), lambda j: (0, j))
    both = pl.BlockSpec((2, T, tc), lambda j: (0, 0, j))
    return pl.pallas_call(
        body, name=name, grid=(Fd // tc,), in_specs=[both, wsp, wsp, bsp, bsp, col],
        out_specs=[both, pl.BlockSpec((2, 4, tc), lambda j: (0, 0, j))],
        out_shape=[jax.ShapeDtypeStruct((2, T, Fd), BF16), jax.ShapeDtypeStruct((2, 4, Fd), F32)],
        scratch_shapes=[pltpu.VMEM((CONV_HALO + R, tc), F32)] * 2 + [pltpu.VMEM((R + SUBLANES, tc), F32)] * 2,
        compiler_params=_params(("parallel",)),
    )(u, wa, wb, ba, bb, dact)


def _bucket_index():
    t = np.arange(SW_WINDOW)[None, :] + SW_WINDOW
    s = np.arange(2 * SW_WINDOW)[:, None]
    dist = np.maximum(t - s, 0)
    exact = REL_BUCKETS // 2
    d = np.maximum(dist, 1).astype(np.float32)
    log_b = exact + (np.log(d / np.float32(exact)) / np.float32(math.log(REL_MAX_DIST / exact))
                     * np.float32(REL_BUCKETS - exact)).astype(np.int32)
    bucket = np.where(dist < exact, dist, np.minimum(log_b, REL_BUCKETS - 1))
    return bucket.astype(np.int32).reshape(1, -1)


BIAS_COLS = SW_WINDOW * 2 * SW_WINDOW
BIAS_TILE = 4096


def _bias_from_table(table, bucket, *, name):
    def body(t_ref, idx_ref, o_ref):
        onehot = (lax.broadcasted_iota(jnp.int32, (REL_BUCKETS, BIAS_TILE), 0) == idx_ref[...]).astype(BF16)
        acc = jnp.zeros((SW_Q_HEADS, BIAS_TILE), F32)
        for piece in _split3(t_ref[...]):
            acc = acc + lax.dot_general(piece, onehot, (((0,), (0,)), ((), ())), preferred_element_type=F32)
        o_ref[...] = acc

    return pl.pallas_call(
        body, name=name, grid=(BIAS_COLS // BIAS_TILE,),
        in_specs=[pl.BlockSpec((REL_BUCKETS, SW_Q_HEADS), lambda j: (0, 0)), pl.BlockSpec((1, BIAS_TILE), lambda j: (0, j))],
        out_specs=pl.BlockSpec((SW_Q_HEADS, BIAS_TILE), lambda j: (0, j)),
        out_shape=jax.ShapeDtypeStruct((SW_Q_HEADS, BIAS_COLS), F32),
        compiler_params=_params(("parallel",)),
    )(table, bucket)


def _table_grad(dbias, bucket, *, name):
    def body(d_ref, idx_ref, o_ref):
        @pl.when(pl.program_id(0) == 0)
        def _():
            o_ref[...] = jnp.zeros_like(o_ref)

        onehot = (lax.broadcasted_iota(jnp.int32, (REL_BUCKETS, BIAS_TILE), 0) == idx_ref[...]).astype(BF16)
        acc = jnp.zeros((REL_BUCKETS, SW_Q_HEADS), F32)
        for piece in _split3(d_ref[...]):
            acc = acc + lax.dot_general(onehot, piece, (((1,), (1,)), ((), ())), preferred_element_type=F32)
        o_ref[...] += acc

    return pl.pallas_call(
        body, name=name, grid=(BIAS_COLS // BIAS_TILE,),
        in_specs=[pl.BlockSpec((SW_Q_HEADS, BIAS_TILE), lambda j: (0, j)), pl.BlockSpec((1, BIAS_TILE), lambda j: (0, j))],
        out_specs=pl.BlockSpec((REL_BUCKETS, SW_Q_HEADS), lambda j: (0, 0)),
        out_shape=jax.ShapeDtypeStruct((REL_BUCKETS, SW_Q_HEADS), F32),
        compiler_params=_params(("arbitrary",)),
    )(dbias, bucket)


KV_DIM = SW_KV_HEADS * SW_HEAD_DIM
GROUP_ROWS = SW_GROUP * SW_HEAD_DIM
GROUP_LANES = SW_GROUP * SW_WINDOW


def _band_mask(n):
    s = lax.broadcasted_iota(jnp.int32, (2 * SW_WINDOW, GROUP_LANES), 0)
    t = (lax.broadcasted_iota(jnp.int32, (2 * SW_WINDOW, GROUP_LANES), 1) & (SW_WINDOW - 1)) + SW_WINDOW
    dist = t - s
    return (dist >= 0) & (dist < SW_WINDOW) & ((n > 0) | (s >= SW_WINDOW))


def _side_by_side(x_ref, g):
    r0 = g * GROUP_ROWS
    return jnp.concatenate([x_ref[r0 + r * SW_HEAD_DIM:r0 + (r + 1) * SW_HEAD_DIM, :] for r in range(SW_GROUP)], axis=1)


def _group_inputs(bias_ref, sink_ref, g):
    heads = range(g * SW_GROUP, (g + 1) * SW_GROUP)
    bias = jnp.concatenate([bias_ref[h] for h in heads], axis=1)
    sink = jnp.concatenate([jnp.broadcast_to(sink_ref[:, h:h + 1], (1, SW_WINDOW)) for h in heads], axis=1)
    return heads, bias, sink


def _kv_pair(kvp_ref, kvc_ref, g):
    ks = slice(g * SW_HEAD_DIM, (g + 1) * SW_HEAD_DIM)
    vs = slice(KV_DIM + g * SW_HEAD_DIM, KV_DIM + (g + 1) * SW_HEAD_DIM)
    kk = jnp.concatenate([kvp_ref[:, ks], kvc_ref[:, ks]], axis=0)
    vv = jnp.concatenate([kvp_ref[:, vs], kvc_ref[:, vs]], axis=0)
    return kk, vv, ks, vs


def _col_max(x):
    return jnp.max(x, axis=0, keepdims=True)


def _col_sum(x):
    return jnp.sum(x, axis=0, keepdims=True)


def _attn_fwd(qt, kv, bias, sinks, *, name):
    Dm, T = qt.shape
    W = SW_WINDOW

    def body(q_ref, kvc_ref, kvp_ref, bias_ref, sink_ref, o_ref):
        mask = _band_mask(pl.program_id(0))
        G = range(SW_KV_HEADS)
        ins = [_group_inputs(bias_ref, sink_ref, g) for g in G]
        kvs = [_kv_pair(kvp_ref, kvc_ref, g) for g in G]
        q = [_side_by_side(q_ref, g) for g in G]
        lg = [jnp.where(mask, mm(kvs[g][0], q[g]) * (SW_HEAD_DIM ** -0.5) + ins[g][1], -jnp.inf) for g in G]
        m = [jnp.maximum(_col_max(lg[g]), ins[g][2]) for g in G]
        p = [jnp.exp(lg[g] - m[g]) for g in G]
        den = [_col_sum(p[g]) + jnp.exp(ins[g][2] - m[g]) for g in G]
        o = [mm_tn(kvs[g][1], p[g]) / den[g] for g in G]
        for g in G:
            for r in range(SW_GROUP):
                o_ref[g * GROUP_ROWS + r * SW_HEAD_DIM:g * GROUP_ROWS + (r + 1) * SW_HEAD_DIM, :] = (
                    o[g][:, r * W:(r + 1) * W].astype(BF16))

    return pl.pallas_call(
        body, name=name, grid=(T // W,),
        in_specs=[pl.BlockSpec((Dm, W), lambda n: (0, n)),
                  pl.BlockSpec((W, 2 * KV_DIM), lambda n: (n, 0)),
                  pl.BlockSpec((W, 2 * KV_DIM), lambda n: (jnp.maximum(n - 1, 0), 0)),
                  pl.BlockSpec((SW_Q_HEADS, 2 * W, W), lambda n: (0, 0, 0)),
                  pl.BlockSpec((1, SW_Q_HEADS), lambda n: (0, 0))],
        out_specs=pl.BlockSpec((Dm, W), lambda n: (0, n)),
        out_shape=jax.ShapeDtypeStruct((Dm, T), BF16),
        compiler_params=_params(("parallel",)),
    )(qt, kv, kv, bias, sinks)


def _attn_bwd(qt, kv, bias, sinks, dot, *, name):
    Dm, T = qt.shape
    W = SW_WINDOW
    nb = T // W

    def body(q_ref, kvc_ref, kvp_ref, bias_ref, sink_ref, do_ref,
             dq_ref, dkv_ref, dbias_ref, dsink_ref, carry_ref):
        @pl.when(pl.program_id(0) == 0)
        def _():
            carry_ref[...] = jnp.zeros_like(carry_ref)
            dbias_ref[...] = jnp.zeros_like(dbias_ref)
            dsink_ref[...] = jnp.zeros_like(dsink_ref)

        n = nb - 1 - pl.program_id(0)
        mask = _band_mask(n)
        lane = lax.broadcasted_iota(jnp.int32, (1, SW_Q_HEADS), 1)
        sc = SW_HEAD_DIM ** -0.5
        G = range(SW_KV_HEADS)
        ins = [_group_inputs(bias_ref, sink_ref, g) for g in G]
        kvs = [_kv_pair(kvp_ref, kvc_ref, g) for g in G]
        q = [_side_by_side(q_ref, g) for g in G]
        do = [_side_by_side(do_ref, g) for g in G]
        lg = [jnp.where(mask, mm(kvs[g][0], q[g]) * sc + ins[g][1], -jnp.inf) for g in G]
        m = [jnp.maximum(_col_max(lg[g]), ins[g][2]) for g in G]
        p = [jnp.exp(lg[g] - m[g]) for g in G]
        ps = [jnp.exp(ins[g][2] - m[g]) for g in G]
        rden = [1.0 / (_col_sum(p[g]) + ps[g]) for g in G]
        pn = [p[g] * rden[g] for g in G]
        dpn = [mm(kvs[g][1], do[g]) for g in G]
        delta = [_col_sum(pn[g] * dpn[g]) for g in G]
        ds = [pn[g] * (dpn[g] - delta[g]) for g in G]
        dsr = [-(ps[g] * rden[g]) * delta[g] for g in G]
        dq = [mm_tn(kvs[g][0], ds[g]) * sc for g in G]
        dkk = [mm_nt(ds[g], q[g]) * sc for g in G]
        dvv = [mm_nt(pn[g], do[g]) for g in G]
        dsink = jnp.zeros((1, SW_Q_HEADS), F32)
        for g in G:
            _, _, ks, vs = kvs[g]
            for r, h in enumerate(ins[g][0]):
                cols = slice(r * W, (r + 1) * W)
                dbias_ref[h] += ds[g][:, cols]
                dq_ref[g * GROUP_ROWS + r * SW_HEAD_DIM:g * GROUP_ROWS + (r + 1) * SW_HEAD_DIM, :] = dq[g][:, cols].astype(BF16)
                dsink = dsink + jnp.where(lane == h, jnp.sum(dsr[g][:, cols], axis=1, keepdims=True), 0.0)
            dkv_ref[:, ks] = (carry_ref[:, ks] + dkk[g][W:]).astype(BF16)
            dkv_ref[:, vs] = (carry_ref[:, vs] + dvv[g][W:]).astype(BF16)
            carry_ref[:, ks] = dkk[g][:W]
            carry_ref[:, vs] = dvv[g][:W]
        dsink_ref[...] += dsink

    rev = lambda n: (nb - 1 - n, 0)
    revt = lambda n: (0, nb - 1 - n)
    return pl.pallas_call(
        body, name=name, grid=(nb,),
        in_specs=[pl.BlockSpec((Dm, W), revt),
                  pl.BlockSpec((W, 2 * KV_DIM), rev),
                  pl.BlockSpec((W, 2 * KV_DIM), lambda n: (jnp.maximum(nb - 2 - n, 0), 0)),
                  pl.BlockSpec((SW_Q_HEADS, 2 * W, W), lambda n: (0, 0, 0)),
                  pl.BlockSpec((1, SW_Q_HEADS), lambda n: (0, 0)),
                  pl.BlockSpec((Dm, W), revt)],
        out_specs=[pl.BlockSpec((Dm, W), revt), pl.BlockSpec((W, 2 * KV_DIM), rev),
                   pl.BlockSpec((SW_Q_HEADS, 2 * W, W), lambda n: (0, 0, 0)),
                   pl.BlockSpec((1, SW_Q_HEADS), lambda n: (0, 0))],
        out_shape=[jax.ShapeDtypeStruct((Dm, T), BF16), jax.ShapeDtypeStruct((T, 2 * KV_DIM), BF16),
                   jax.ShapeDtypeStruct((SW_Q_HEADS, 2 * W, W), F32), jax.ShapeDtypeStruct((1, SW_Q_HEADS), F32)],
        scratch_shapes=[pltpu.VMEM((W, 2 * KV_DIM), F32)],
        compiler_params=_params(("arbitrary",)),
    )(qt, kv, kv, bias, sinks, dot)


def _ffn_fwd(hb, w, l, after=None):
    u = _matmul(hb, w["ffn_in"][l], mode="nn", planes="n", out_dtype=BF16, name=f"ffn{l}_up", tm=1024, after=after)
    act = _conv_gate_fwd(u, w["conv_w_a"][l], w["conv_w_b"][l], w["conv_b_a"][l], w["conv_b_b"][l],
                         name=f"ffn{l}_conv_gate")
    return u, act


def _ffn_bwd(dffb, dh_scaled, hb, u, act, w, l, dact):
    g_out = _matmul(act, dffb, mode="tn", name=f"ffn{l}_down_dw", tm=1408, tn=1024, tk=1024, also_bf16=True)
    du, g_conv = _conv_gate_bwd(u, w["conv_w_a"][l], w["conv_w_b"][l], w["conv_b_a"][l], w["conv_b_b"][l],
                                dact, name=f"ffn{l}_conv_gate_bwd")
    dh = _matmul(du, w["ffn_in"][l], mode="nt", planes="k", add=dh_scaled, add_scale=ALPHA, name=f"ffn{l}_up_dx",
                 tm=1024, tn=1024, tk=FFN_DIM)
    g_in = _matmul(hb, du, mode="tn", planes="n", name=f"ffn{l}_up_dw", tm=1024, tn=FFN_DIM // 2, tk=1024, split_n=True, also_bf16=True)
    return dh, dict(ffn_out=g_out, ffn_in=g_in, conv=g_conv)


def _local_step(x, xb, tgt, w, more_weights, emit):
    bucket = jnp.asarray(_bucket_index())

    pre = _matmul(xb, w["hg_in"], mode="nn", planes="n", out_dtype=BF16, name="hg_in", tm=1024, tn=1024,
                  after=w.get("token"))
    og, states = _hgrn_fwd(pre, w["lb_logits"], w["gnorm"], name="hgrn_fwd")
    z1, h1, h1b = _matmul_ln(og, w["hg_out"], x, w["ln_mix_g"][0], w["ln_mix_b"][0], name="hg_out_ln")
    w = {**w, **more_weights(1, h1b)}
    u0, act0 = _ffn_fwd(h1b, w, 0, after=w.get("token"))
    z2, h2, h2b = _matmul_ln(act0, w["ffn_out"][0], h1, w["ln_ffn_g"][0], w["ln_ffn_b"][0], name="ffn0_down_ln")
    kv = _matmul(h2b, w["kv"], mode="nn", out_dtype=BF16, name="kv_proj")

    bias = _bias_from_table(w["rel_bias"], bucket, name="rel_bias_expand").reshape(SW_Q_HEADS, 2 * SW_WINDOW, SW_WINDOW)
    q1 = _matmul(w["sw_q"], h2b, mode="tt", out_dtype=BF16, name="sw_q", tm=1024, tn=1024)
    o1 = _attn_fwd(q1, kv, bias, w["sinks"], name="attn_fwd")
    z3, h3, h3b = _matmul_ln(o1, w["sw_out"], h2, w["ln_mix_g"][1], w["ln_mix_b"][1], a_t=True, name="sw_out_ln")
    w = {**w, **more_weights(2, h3b)}
    u1, act1 = _ffn_fwd(h3b, w, 1)

    g = {}
    dz, dzb, g["ln_ffn1"], loss_tile, dact1 = _matmul_ln(act1, w["ffn_out"][1], h3, w["ln_ffn_g"][1], w["ln_ffn_b"][1],
                                                         tgt=tgt, name="ffn1_down_ln_loss")

    dh3, gf1 = _ffn_bwd(dzb, dz, h3b, u1, act1, w, 1, dact1)
    dz, dzb, g["ln_mix1"], do1 = _ln_bwd_matmul(dh3, z3, w["ln_mix_g"][1], w["ln_mix_b"][1], w["sw_out"], out_t=True,
                                                name="ln_mix1_bwd_sw_out_dx")
    g_sw_out = _matmul(o1, dzb, mode="nn", name="sw_out_dw", tm=1024, tn=1024, tk=1024, also_bf16=True)
    dq1, dkv, dbias, dsinks = _attn_bwd(q1, kv, bias, w["sinks"], do1, name="attn_bwd")
    g["sinks"] = dsinks
    g["rel_bias"] = _table_grad(dbias.reshape(SW_Q_HEADS, BIAS_COLS), bucket, name="rel_bias_grad")
    dh2 = _matmul(dq1, w["sw_q"], mode="tt", add=dz, add_scale=ALPHA, name="sw_q_dx", tn=1024)
    dh2 = _matmul(dkv, w["kv"], mode="nt", add=dh2, name="kv_dx", tn=1024)
    g_sw_q = _matmul(h2b, dq1, mode="tt", name="sw_q_dw", tm=1024, tn=1024, tk=1024, also_bf16=True)
    g_kv = _matmul(h2b, dkv, mode="tn", name="kv_dw", tm=1024, tn=512, tk=1024, also_bf16=True)
    tok = emit(1, dict(sw_q=g_sw_q, sw_out=g_sw_out, kv=g_kv, ffn_in=gf1["ffn_in"], ffn_out=gf1["ffn_out"]))

    dz, dzb, g["ln_ffn0"], dact0 = _ln_bwd_matmul(dh2, z2, w["ln_ffn_g"][0], w["ln_ffn_b"][0], w["ffn_out"][0],
                                                  name="ln_ffn0_bwd_down_dx", after=tok)
    dh1, gf0 = _ffn_bwd(dzb, dz, h1b, u0, act0, w, 0, dact0)
    dz, dzb, g["ln_mix0"], dog = _ln_bwd_matmul(dh1, z1, w["ln_mix_g"][0], w["ln_mix_b"][0], w["hg_out"],
                                                name="ln_mix0_bwd_hg_out_dx")
    g_hg_out = _matmul(og, dzb, mode="tn", name="hg_out_dw", tm=1024, tn=1024, tk=1024, also_bf16=True)
    tok = emit(2, dict(hg_out=g_hg_out, ffn_in=gf0["ffn_in"], ffn_out=gf0["ffn_out"]))
    dpre, g["lb_logits"], g["gnorm"] = _hgrn_bwd(pre, w["lb_logits"], w["gnorm"], states, dog, name="hgrn_bwd", after=tok)
    tok = emit(3, dict(hg_in=_matmul(xb, dpre, mode="tn", planes="n", name="hg_in_dw", tm=1024, tn=1024, tk=1024, also_bf16=True)))
    dx = _matmul(dpre, w["hg_in"], mode="nt", planes="k", add=dz, add_scale=ALPHA, name="hg_in_dx", tm=1024, tn=1024,
                 tk=1024, after=tok)
    g["conv0"], g["conv1"] = gf0["conv"], gf1["conv"]
    return loss_tile, dx, g


def _adamw(wt, ga, gb, m, v, *, name, rows=None, prev=None):
    R, Cc = wt.shape
    r0, n = rows if rows is not None else (0, R)
    tr = _tile(n, 256, SUBLANES) if n % SUBLANES == 0 else n
    assert r0 % tr == 0
    c1 = 1.0 - ADAM_B1 ** ADAM_STEP
    c2 = 1.0 - ADAM_B2 ** ADAM_STEP
    two = gb is not None
    n_in = 5 if two else 4

    def body(*refs):
        if two:
            w_ref, ga_ref, gb_ref, m_ref, v_ref = refs[:5]
            g_ = ga_ref[...] + gb_ref[...]
        else:
            w_ref, ga_ref, m_ref, v_ref = refs[:4]
            g_ = ga_ref[...]
        g_ref, d_ref, nm_ref, nv_ref = refs[-4:]
        nm = ADAM_B1 * m_ref[...] + (1.0 - ADAM_B1) * g_
        nv = ADAM_B2 * v_ref[...] + (1.0 - ADAM_B2) * (g_ * g_)
        g_ref[...] = g_
        d_ref[...] = -ADAM_LR * ((nm / c1) / (jnp.sqrt(nv / c2) + ADAM_EPS) + ADAM_WD * w_ref[...])
        nm_ref[...] = nm
        nv_ref[...] = nv

    full = pl.BlockSpec((tr, Cc), lambda i: (i + r0 // tr, 0))
    part = pl.BlockSpec((tr, Cc), lambda i: (i, 0))
    args = (wt, ga, gb, m, v) if two else (wt, ga, m, v)
    in_specs = [full] + [part] * (n_in - 3) + [full, full]
    aliases = {}
    if prev is not None:
        args, in_specs = args + tuple(prev), in_specs + [ANY_SPEC] * 4
        aliases = {n_in + t: t for t in range(4)}
    return pl.pallas_call(
        body, name=name, grid=(n // tr,), in_specs=in_specs, out_specs=[full] * 4,
        out_shape=[jax.ShapeDtypeStruct((R, Cc), F32)] * 4, input_output_aliases=aliases,
        compiler_params=_params(("parallel",)),
    )(*args)


def _adamw_small(ws, gs, ms, vs, *, name):
    n = len(ws)
    c1 = 1.0 - ADAM_B1 ** ADAM_STEP
    c2 = 1.0 - ADAM_B2 ** ADAM_STEP

    def body(*refs):
        w_refs, g_refs, m_refs, v_refs = (refs[k * n:(k + 1) * n] for k in range(4))
        d_refs, nm_refs, nv_refs = (refs[(4 + k) * n:(5 + k) * n] for k in range(3))
        for i in range(n):
            g_ = g_refs[i][...]
            nm = ADAM_B1 * m_refs[i][...] + (1.0 - ADAM_B1) * g_
            nv = ADAM_B2 * v_refs[i][...] + (1.0 - ADAM_B2) * (g_ * g_)
            d_refs[i][...] = -ADAM_LR * ((nm / c1) / (jnp.sqrt(nv / c2) + ADAM_EPS) + ADAM_WD * w_refs[i][...])
            nm_refs[i][...] = nm
            nv_refs[i][...] = nv

    vm = pl.BlockSpec(memory_space=pltpu.VMEM)
    out = pl.pallas_call(
        body, name=name, in_specs=[vm] * (4 * n), out_specs=[vm] * (3 * n),
        out_shape=[jax.ShapeDtypeStruct(w.shape, F32) for w in ws] * 3,
    )(*ws, *gs, *ms, *vs)
    return out[:n], out[n:2 * n], out[2 * n:]


HBM_SPEC = pl.BlockSpec(memory_space=pltpu.HBM)
SEM_SPEC = pl.BlockSpec(memory_space=pltpu.SEMAPHORE)
VMEM_SPEC = pl.BlockSpec(memory_space=pltpu.VMEM)
DATAFLOW = pltpu.SideEffectType.DATAFLOW_SIDE_EFFECTING


def _in_hbm(a):
    return pltpu.with_memory_space_constraint(a, pltpu.HBM)


def _place():
    return lax.axis_index("x"), lax.axis_index("y"), lax.axis_index("c")


def _other_chips(x, y):
    return [(1 - x, y), (x, 1 - y), (1 - x, 1 - y)]


def _sum8(vs, *, name):
    n = len(vs)

    def body(*refs):
        v_refs, all_refs, o_refs = refs[:n], refs[n:2 * n], refs[2 * n:3 * n]
        send_sems, recv_sems, local_sems = refs[3 * n:]
        x, y, c = _place()
        me, sibling = (x, y, c), (x, y, 1 - c)
        chips = _other_chips(x, y)

        def slot(i, px, py, pc):
            return all_refs[i].at[4 * px + 2 * py + pc]

        def copy(i, k, block, to, src=None):
            return pltpu.make_async_remote_copy(
                src_ref=slot(i, *block) if src is None else src, dst_ref=slot(i, *block),
                send_sem=send_sems.at[7 * i + k], recv_sem=recv_sems.at[7 * i + k], device_id=to, device_id_type=MESH)

        mine = [pltpu.make_async_copy(v_refs[i], slot(i, *me), local_sems.at[i]) for i in range(n)]
        for cp in mine:
            cp.start()
        first = [copy(i, 0, me, sibling, src=v_refs[i]) for i in range(n)]
        first += [copy(i, 1 + j, me, (*chip, c), src=v_refs[i]) for i in range(n) for j, chip in enumerate(chips)]
        for cp in first:
            cp.start()
        passed = []
        for i in range(n):
            for j, chip in enumerate(chips):
                copy(i, 1 + j, (*chip, c), me).wait_recv()
                passed.append(copy(i, 4 + j, (*chip, c), sibling))
                passed[-1].start()
        for i in range(n):
            copy(i, 0, sibling, me).wait_recv()
            for j, chip in enumerate(chips):
                copy(i, 4 + j, (*chip, 1 - c), me).wait_recv()
        for cp in first + passed:
            cp.wait_send()
        for cp in mine:
            cp.wait()
        for i in range(n):
            acc = all_refs[i][0]
            for d in range(1, N_DEV):
                acc = acc + all_refs[i][d]
            o_refs[i][...] = acc

    return pl.pallas_call(
        body, name=name, in_specs=[VMEM_SPEC] * n, out_specs=[VMEM_SPEC] * (2 * n),
        out_shape=[jax.ShapeDtypeStruct((N_DEV,) + v.shape, F32) for v in vs] + [jax.ShapeDtypeStruct(v.shape, F32) for v in vs],
        scratch_shapes=[pltpu.SemaphoreType.DMA((7 * n,)), pltpu.SemaphoreType.DMA((7 * n,)), pltpu.SemaphoreType.DMA((n,))],
        compiler_params=pltpu.CompilerParams(vmem_limit_bytes=VMEM_LIMIT),
    )(*vs)[n:]


def _swap_sibling(vs, *, name):
    n = len(vs)

    def body(*refs):
        src, dst, send_sems, recv_sems = refs[:n], refs[n:2 * n], refs[2 * n], refs[2 * n + 1]
        x, y, c = _place()
        cps = [pltpu.make_async_remote_copy(src_ref=src[i], dst_ref=dst[i], send_sem=send_sems.at[i],
                                            recv_sem=recv_sems.at[i], device_id=(x, y, 1 - c), device_id_type=MESH)
               for i in range(n)]
        for cp in cps:
            cp.start()
        for cp in cps:
            cp.wait()

    return pl.pallas_call(
        body, name=name, in_specs=[HBM_SPEC] * n, out_specs=[HBM_SPEC] * n,
        out_shape=[jax.ShapeDtypeStruct(v.shape, v.dtype) for v in vs],
        scratch_shapes=[pltpu.SemaphoreType.DMA((n,)), pltpu.SemaphoreType.DMA((n,))],
    )(*vs)


def _half(ref, j, c, half):
    return ref.at[j, pl.ds(c * half, half), :]


def _gather_start(shard, after, *, name):
    R, Cc = shard.shape
    half = R // 2

    def body(src, land, send, recv, src_out, land_out, token):
        x, y, c = _place()
        for k, (px, py) in enumerate(_other_chips(x, y)):
            pltpu.make_async_remote_copy(src_ref=src.at[pl.ds(c * half, half), :], dst_ref=_half(land, 2 * x + y, c, half),
                                         send_sem=send.at[k], recv_sem=recv.at[k], device_id=(px, py, c),
                                         device_id_type=MESH).start()
        token[...] = jnp.zeros_like(token)

    land = lax.empty((N_CHIPS, R, Cc), shard.dtype)
    body, xs, xa = _after(body, 2, after)
    out = pl.pallas_call(
        body, name=name, in_specs=[HBM_SPEC, HBM_SPEC] + xs,
        out_specs=[SEM_SPEC, SEM_SPEC, HBM_SPEC, HBM_SPEC, VMEM_SPEC],
        out_shape=[pltpu.SemaphoreType.DMA((3,)), pltpu.SemaphoreType.DMA((3,)), pltpu.HBM(shard.shape, shard.dtype),
                   pltpu.HBM(land.shape, land.dtype), jax.ShapeDtypeStruct((SUBLANES, LANES), F32)],
        input_output_aliases={0: 2, 1: 3},
        compiler_params=pltpu.CompilerParams(has_side_effects=DATAFLOW),
    )(_in_hbm(shard), _in_hbm(land), *xa)
    return out[:4], out[4]


def _gather_wait(handle, after, *, name):
    send_sems, recv_sems, src, land = handle
    half = src.shape[0] // 2

    def body(src_ref, land_ref, send_ref, recv_ref, after_ref, src_out, land_out):
        x, y, c = _place()
        for k, (px, py) in enumerate(_other_chips(x, y)):
            cp = pltpu.make_async_remote_copy(src_ref=src_ref.at[pl.ds(c * half, half), :],
                                              dst_ref=_half(land_ref, 2 * px + py, c, half), send_sem=send_ref.at[k],
                                              recv_sem=recv_ref.at[k], device_id=(px, py, c), device_id_type=MESH)
            cp.wait_send()
            cp.wait_recv()

    return pl.pallas_call(
        body, name=name, in_specs=[HBM_SPEC, HBM_SPEC, SEM_SPEC, SEM_SPEC, ANY_SPEC], out_specs=[HBM_SPEC, HBM_SPEC],
        out_shape=[pltpu.HBM(src.shape, src.dtype), pltpu.HBM(land.shape, land.dtype)],
        input_output_aliases={0: 0, 1: 1},
        compiler_params=pltpu.CompilerParams(has_side_effects=DATAFLOW),
    )(src, land, send_sems, recv_sems, after)[1]


def _fill_sibling(land, *, name):
    _, R, Cc = land.shape
    half = R // 2

    def body(in_ref, o_ref, send_sems, recv_sems):
        x, y, c = _place()
        chips = _other_chips(x, y)
        cps = [pltpu.make_async_remote_copy(src_ref=_half(in_ref, 2 * px + py, c, half),
                                            dst_ref=_half(o_ref, 2 * px + py, c, half), send_sem=send_sems.at[k],
                                            recv_sem=recv_sems.at[k], device_id=(x, y, 1 - c), device_id_type=MESH)
               for k, (px, py) in enumerate(chips)]
        for cp in cps:
            cp.start()
        for k, (px, py) in enumerate(chips):
            pltpu.make_async_remote_copy(src_ref=_half(in_ref, 2 * px + py, 1 - c, half),
                                         dst_ref=_half(o_ref, 2 * px + py, 1 - c, half), send_sem=send_sems.at[k],
                                         recv_sem=recv_sems.at[k], device_id=(x, y, 1 - c), device_id_type=MESH).wait_recv()
        for cp in cps:
            cp.wait_send()

    return pl.pallas_call(
        body, name=name, in_specs=[HBM_SPEC], out_specs=HBM_SPEC, out_shape=jax.ShapeDtypeStruct(land.shape, land.dtype),
        scratch_shapes=[pltpu.SemaphoreType.DMA((3,)), pltpu.SemaphoreType.DMA((3,))],
        input_output_aliases={0: 0},
    )(land)


def _scatter_copies(src, land, send, recv):
    x, y, c = _place()
    return [pltpu.make_async_remote_copy(src_ref=src[i].at[2 * px + py], dst_ref=land[i].at[k], send_sem=send.at[3 * i + k],
                                         recv_sem=recv.at[3 * i + k], device_id=(px, py, c), device_id_type=MESH)
            for i in range(len(src)) for k, (px, py) in enumerate(_other_chips(x, y))]


def _scatter_start(pieces, *, name):
    n = len(pieces)

    def body(*refs):
        src, land, send, recv, token = refs[:n], refs[n:2 * n], refs[2 * n], refs[2 * n + 1], refs[-1]
        for cp in _scatter_copies(src, land, send, recv):
            cp.start()
        token[...] = jnp.zeros_like(token)

    lands = [lax.empty((3,) + p.shape[1:], p.dtype) for p in pieces]
    sems = pltpu.SemaphoreType.DMA((3 * n,))
    out = pl.pallas_call(
        body, name=name, in_specs=[HBM_SPEC] * (2 * n),
        out_specs=[SEM_SPEC, SEM_SPEC] + [HBM_SPEC] * (2 * n) + [VMEM_SPEC],
        out_shape=[sems, sems] + [pltpu.HBM(a.shape, a.dtype) for a in pieces + lands]
        + [jax.ShapeDtypeStruct((SUBLANES, LANES), F32)],
        input_output_aliases={i: 2 + i for i in range(2 * n)},
        compiler_params=pltpu.CompilerParams(has_side_effects=DATAFLOW),
    )(*[_in_hbm(a) for a in pieces + lands])
    return (out[0], out[1], out[2:2 + n], out[2 + n:2 + 2 * n]), out[-1]


def _scatter_wait(handle, after, *, name):
    send_sems, recv_sems, srcs, lands = handle
    n = len(srcs)

    def body(*refs):
        src, land, send, recv = refs[:n], refs[n:2 * n], refs[2 * n], refs[2 * n + 1]
        for cp in _scatter_copies(src, land, send, recv):
            cp.wait_send()
            cp.wait_recv()

    both = list(srcs) + list(lands)
    out = pl.pallas_call(
        body, name=name, in_specs=[HBM_SPEC] * (2 * n) + [SEM_SPEC, SEM_SPEC, ANY_SPEC], out_specs=[HBM_SPEC] * (2 * n),
        out_shape=[pltpu.HBM(a.shape, a.dtype) for a in both],
        input_output_aliases={i: i for i in range(2 * n)},
        compiler_params=pltpu.CompilerParams(has_side_effects=DATAFLOW),
    )(*both, send_sems, recv_sems, after)
    return out[n:]


def _to_bf16(x, *, name, after=None):
    T, Dm = x.shape
    tr = _tile(T, 512, 2 * SUBLANES)

    def body(x_ref, o_ref):
        o_ref[...] = x_ref[...].astype(BF16)

    blk = pl.BlockSpec((tr, Dm), lambda i: (i, 0))
    body, xs, xa = _after(body, 1, after)
    return pl.pallas_call(
        body, name=name, grid=(T // tr,), in_specs=[blk] + xs, out_specs=blk, out_shape=jax.ShapeDtypeStruct((T, Dm), BF16),
        compiler_params=_params(("parallel",)),
    )(x, *xa)


def _chip_sum(pieces, got, chip, *, name):
    _, R, Cc = pieces.shape
    tr = _tile(R, 256, SUBLANES)

    def body(chip_ref, a_ref, g_ref, o_ref):
        o_ref[...] = ((a_ref[...] + g_ref[0].astype(F32)) + g_ref[1].astype(F32)) + g_ref[2].astype(F32)

    return pl.pallas_call(
        body, name=name,
        grid_spec=pltpu.PrefetchScalarGridSpec(
            num_scalar_prefetch=1, grid=(R // tr,),
            in_specs=[pl.BlockSpec((None, tr, Cc), lambda i, ch: (ch[0], i, 0)),
                      pl.BlockSpec((3, tr, Cc), lambda i, ch: (0, i, 0))],
            out_specs=pl.BlockSpec((tr, Cc), lambda i, ch: (i, 0))),
        out_shape=jax.ShapeDtypeStruct((R, Cc), F32),
        compiler_params=_params(("parallel",)),
    )(chip, pieces, got)


PACK_COLS = 1024
SMALL_ROWS = 32


def _pack_rows(parts):
    return jnp.concatenate([p.reshape(-1, PACK_COLS) for p in parts], axis=0)


def _unpack_rows(block, shapes):
    lead = block.shape[:-2]
    out, off = [], 0
    for s in shapes:
        r = int(np.prod(s)) // PACK_COLS
        out.append(block[..., off:off + r, :].reshape(lead + tuple(s)))
        off += r
    assert off == block.shape[-2]
    return out


def kernel(x, hgrn_w_in, hgrn_lb_logits, hgrn_gnorm_w, hgrn_w_out, swa_w_q, swa_sinks, swa_w_out, shared_w_kv, rel_bias, ffn_w_in, ffn_conv_w, ffn_conv_b, ffn_w_out, ln_mix_g, ln_mix_b, ln_ffn_g, ln_ffn_b, loss_target, m_hgrn_w_in, m_hgrn_lb_logits, m_hgrn_gnorm_w, m_hgrn_w_out, m_swa_w_q, m_swa_sinks, m_swa_w_out, m_shared_w_kv, m_rel_bias, m_ffn_w_in, m_ffn_conv_w, m_ffn_conv_b, m_ffn_w_out, m_ln_mix_g, m_ln_mix_b, m_ln_ffn_g, m_ln_ffn_b, v_hgrn_w_in, v_hgrn_lb_logits, v_hgrn_gnorm_w, v_hgrn_w_out, v_swa_w_q, v_swa_sinks, v_swa_w_out, v_shared_w_kv, v_rel_bias, v_ffn_w_in, v_ffn_conv_w, v_ffn_conv_b, v_ffn_w_out, v_ln_mix_g, v_ln_mix_b, v_ln_ffn_g, v_ln_ffn_b):
    xi, yi, ci = _place()
    chip = 2 * xi + yi
    Dm = D_MODEL
    FC = 2 * FFN_DIM // N_CHIPS
    Fo = FFN_DIM // N_CHIPS
    Dq = Dm // N_CHIPS
    bf = lambda a: a.astype(BF16)

    small = jnp.concatenate([hgrn_lb_logits.reshape(-1), ffn_conv_w.reshape(-1)])
    n_small = small.shape[0]
    bits = jnp.concatenate(_split3(small))
    bits = jnp.pad(bits, (0, SMALL_ROWS * PACK_COLS - 3 * n_small)).reshape(SMALL_ROWS, PACK_COLS)
    shard0 = _pack_rows([bf(hgrn_w_in), bf(hgrn_w_out), bits])
    shard1 = _pack_rows([bf(swa_w_q), bf(swa_w_out), bf(shared_w_kv), bf(ffn_w_in[0]), bf(ffn_w_out[0])])
    shard2 = _pack_rows([bf(ffn_w_in[1]), bf(ffn_w_out[1])])
    handle0, token0 = _gather_start(shard0, None, name="gather_w0_start")
    xb = _to_bf16(x[0], name="x_to_bf16", after=token0)
    corner = lambda a: a[:2 * SUBLANES, :LANES]
    casts_done = corner(xb) + corner(shard1) + corner(shard2)
    land0 = _fill_sibling(_gather_wait(handle0, casts_done, name="gather_w0_wait"), name="gather_w0_fill")
    all0 = lax.dynamic_update_slice(land0, shard0[None], (chip, 0, 0))
    handle1, token1 = _gather_start(shard1, land0, name="gather_w1_start")
    w_in, w_hg_out, small_all = _unpack_rows(all0, [(Dm, Dm), (Dq, Dm), (SMALL_ROWS, PACK_COLS)])
    parts = small_all.reshape(N_CHIPS, -1)[:, :3 * n_small].reshape(N_CHIPS, 3, n_small).astype(F32)
    vals = (parts[:, 0] + parts[:, 1]) + parts[:, 2]
    lb_full = vals[:, :2 * Dq].reshape(N_CHIPS, 2, Dq).transpose(1, 0, 2).reshape(2, Dm)
    cw_full = vals[:, 2 * Dq:].reshape(N_CHIPS, DEPTH, 3, FC).transpose(1, 2, 0, 3).reshape(DEPTH, 3, 2 * FFN_DIM)

    def ffn_weights(w_fi, w_fo, l):
        halves = jnp.stack([jnp.concatenate([w_fi[0], w_fi[1]], axis=1), jnp.concatenate([w_fi[2], w_fi[3]], axis=1)])
        return {"ffn_in": {l: halves}, "ffn_out": {l: w_fo.reshape(FFN_DIM, Dm)}}

    got = {}

    def more_weights(k, after):
        shard = (shard1, shard2)[k - 1]
        land = _gather_wait(got.pop("handle"), after, name=f"gather_w{k}_wait")
        land = _fill_sibling(land, name=f"gather_w{k}_fill")
        allk = lax.dynamic_update_slice(land, shard[None], (chip, 0, 0))
        if k == 1:
            got["handle"], token2 = _gather_start(shard2, land, name="gather_w2_start")
            w_q, w_o, w_kv, w_fi, w_fo = _unpack_rows(allk, [(Dq, Dm), (Dq, Dm), (Dq, 2 * KV_DIM), (Dm, FC), (Fo, Dm)])
            got.update(ffn_weights(w_fi, w_fo, 0))
            return {"sw_q": w_q.reshape(Dm, Dm), "sw_out": w_o.reshape(Dm, Dm), "kv": w_kv.reshape(Dm, 2 * KV_DIM),
                    "token": token2, **{n: got[n] for n in ("ffn_in", "ffn_out")}}
        w_fi, w_fo = _unpack_rows(allk, [(Dm, FC), (Fo, Dm)])
        new = ffn_weights(w_fi, w_fo, 1)
        return {n: {**got[n], **new[n]} for n in new}

    got["handle"] = handle1

    w = {
        "hg_in": w_in, "hg_out": w_hg_out.reshape(Dm, Dm), "token": token1,
        "lb_logits": lb_full, "gnorm": hgrn_gnorm_w, "sinks": swa_sinks, "rel_bias": rel_bias,
        "conv_w_a": [cw_full[l, :, :FFN_DIM] for l in range(DEPTH)],
        "conv_w_b": [cw_full[l, :, FFN_DIM:] for l in range(DEPTH)],
        "conv_b_a": [ffn_conv_b[l:l + 1, :FFN_DIM] for l in range(DEPTH)],
        "conv_b_b": [ffn_conv_b[l:l + 1, FFN_DIM:] for l in range(DEPTH)],
        "ln_mix_g": [ln_mix_g[l:l + 1] for l in range(DEPTH)], "ln_mix_b": [ln_mix_b[l:l + 1] for l in range(DEPTH)],
        "ln_ffn_g": [ln_ffn_g[l:l + 1] for l in range(DEPTH)], "ln_ffn_b": [ln_ffn_b[l:l + 1] for l in range(DEPTH)],
    }

    sent = {}

    def emit(k, gd):
        rows4 = lambda a: a.reshape(N_CHIPS, -1, a.shape[-1])
        order = {1: ["sw_q", "sw_out", "kv", "ffn_in", "ffn_out"], 2: ["ffn_in", "ffn_out", "hg_out"], 3: ["hg_in"]}[k]
        as_pieces = lambda a, nme: a if nme in ("ffn_in", "hg_in") else rows4(a)
        handle, token = _scatter_start([as_pieces(gd[nme][1], nme) for nme in order], name=f"scatter_g{k}_start")
        sent[k] = (handle, [as_pieces(gd[nme][0], nme) for nme in order])
        return token

    loss_tile, grad_x, g = _local_step(x[0], xb, loss_target[0], w, more_weights, emit)

    wts = dict(hgrn_w_in=hgrn_w_in, hgrn_lb_logits=hgrn_lb_logits, hgrn_gnorm_w=hgrn_gnorm_w, hgrn_w_out=hgrn_w_out,
               swa_w_q=swa_w_q, swa_sinks=swa_sinks, swa_w_out=swa_w_out, shared_w_kv=shared_w_kv, rel_bias=rel_bias,
               ffn_w_in=ffn_w_in, ffn_conv_w=ffn_conv_w, ffn_conv_b=ffn_conv_b, ffn_w_out=ffn_w_out,
               ln_mix_g=ln_mix_g, ln_mix_b=ln_mix_b, ln_ffn_g=ln_ffn_g, ln_ffn_b=ln_ffn_b)
    ms = dict(hgrn_w_in=m_hgrn_w_in, hgrn_lb_logits=m_hgrn_lb_logits, hgrn_gnorm_w=m_hgrn_gnorm_w, hgrn_w_out=m_hgrn_w_out,
              swa_w_q=m_swa_w_q, swa_sinks=m_swa_sinks, swa_w_out=m_swa_w_out, shared_w_kv=m_shared_w_kv, rel_bias=m_rel_bias,
              ffn_w_in=m_ffn_w_in, ffn_conv_w=m_ffn_conv_w, ffn_conv_b=m_ffn_conv_b, ffn_w_out=m_ffn_w_out,
              ln_mix_g=m_ln_mix_g, ln_mix_b=m_ln_mix_b, ln_ffn_g=m_ln_ffn_g, ln_ffn_b=m_ln_ffn_b)
    vs = dict(hgrn_w_in=v_hgrn_w_in, hgrn_lb_logits=v_hgrn_lb_logits, hgrn_gnorm_w=v_hgrn_gnorm_w, hgrn_w_out=v_hgrn_w_out,
              swa_w_q=v_swa_w_q, swa_sinks=v_swa_sinks, swa_w_out=v_swa_w_out, shared_w_kv=v_shared_w_kv, rel_bias=v_rel_bias,
              ffn_w_in=v_ffn_w_in, ffn_conv_w=v_ffn_conv_w, ffn_conv_b=v_ffn_conv_b, ffn_w_out=v_ffn_w_out,
              ln_mix_g=v_ln_mix_g, ln_mix_b=v_ln_mix_b, ln_ffn_g=v_ln_ffn_g, ln_ffn_b=v_ln_ffn_b)
    names = list(wts)
    grads, delta, new_m, new_v = {}, {}, {}, {}

    def update(n, ga, gb, layer=None, prev=None):
        r2 = lambda a: a.reshape(-1, a.shape[-1])
        rows = None if layer is None else (layer * ga.shape[0], ga.shape[0])
        return _adamw(r2(wts[n]), ga, gb, r2(ms[n]), r2(vs[n]), rows=rows, prev=prev,
                      name=f"adamw_{n}" + ("" if layer is None else f"_{layer}"))

    def keep(n, res):
        grads[n], delta[n], new_m[n], new_v[n] = [a.reshape(wts[n].shape) for a in res]

    chip1 = jnp.reshape(chip, (1,)).astype(jnp.int32)
    after = grad_x
    for k in (1, 2, 3):
        handle, pieces = sent[k]
        lands = _scatter_wait(handle, after, name=f"scatter_g{k}_wait")
        parts = [_chip_sum(p, l, chip1, name=f"scatter_g{k}_sum{i}") for i, (p, l) in enumerate(zip(pieces, lands))]
        sibs = _swap_sibling(parts, name=f"scatter_g{k}_swap")
        if k == 1:
            for n, ga, gb in zip(["swa_w_q", "swa_w_out", "shared_w_kv"], parts[:3], sibs[:3]):
                keep(n, update(n, ga, gb))
            ffn_in_1 = update("ffn_w_in", parts[3], sibs[3], layer=1)
            ffn_out_1 = update("ffn_w_out", parts[4], sibs[4], layer=1)
            after = ffn_out_1[3]
        elif k == 2:
            keep("ffn_w_in", update("ffn_w_in", parts[0], sibs[0], layer=0, prev=ffn_in_1))
            keep("ffn_w_out", update("ffn_w_out", parts[1], sibs[1], layer=0, prev=ffn_out_1))
            keep("hgrn_w_out", update("hgrn_w_out", parts[2], sibs[2]))
            after = new_v["hgrn_w_out"]
        else:
            keep("hgrn_w_in", update("hgrn_w_in", parts[0], sibs[0]))

    small_keys = ["lb_logits", "gnorm", "sinks", "rel_bias", "conv0", "conv1", "ln_mix0", "ln_mix1", "ln_ffn0", "ln_ffn1"]
    flat2 = lambda a: a.reshape(-1, a.shape[-1])
    sums = _sum8([loss_tile] + [flat2(g[k]) for k in small_keys], name="sum_small")
    loss = sums[0][0, 0]
    sg = {k: v.reshape(g[k].shape) for k, v in zip(small_keys, sums[1:])}
    conv = [sg["conv0"], sg["conv1"]]
    g_cw = jnp.stack([jnp.concatenate([conv[l][0, :3], conv[l][1, :3]], axis=1) for l in range(DEPTH)])
    g_cb = jnp.stack([jnp.concatenate([conv[l][0, 3], conv[l][1, 3]], axis=0) for l in range(DEPTH)])
    ln = lambda nme, r: jnp.stack([sg[nme + "0"][r], sg[nme + "1"][r]])
    small_g = dict(hgrn_lb_logits=lax.dynamic_slice_in_dim(sg["lb_logits"], chip * Dq, Dq, axis=1),
                   hgrn_gnorm_w=sg["gnorm"], swa_sinks=sg["sinks"], rel_bias=sg["rel_bias"],
                   ffn_conv_w=lax.dynamic_slice_in_dim(g_cw, chip * FC, FC, axis=2), ffn_conv_b=g_cb,
                   ln_mix_g=ln("ln_mix", 0), ln_mix_b=ln("ln_mix", 1), ln_ffn_g=ln("ln_ffn", 0), ln_ffn_b=ln("ln_ffn", 1))
    small_names = list(small_g)
    d_, m_, v_ = _adamw_small([flat2(wts[n]) for n in small_names], [flat2(small_g[n]) for n in small_names],
                              [flat2(ms[n]) for n in small_names], [flat2(vs[n]) for n in small_names], name="adamw_small")
    for n, a, b_, c_ in zip(small_names, d_, m_, v_):
        shp = wts[n].shape
        grads[n], delta[n], new_m[n], new_v[n] = small_g[n], a.reshape(shp), b_.reshape(shp), c_.reshape(shp)

    return (loss, grad_x[None], *[grads[n] for n in names], *[delta[n] for n in names],
            *[new_m[n] for n in names], *[new_v[n] for n in names])
```

```python
import math

import numpy as np
import jax
import jax.numpy as jnp
from jax import lax
from jax.experimental import pallas as pl
from jax.experimental.pallas import tpu as pltpu

F32 = jnp.float32
BF16 = jnp.bfloat16
MESH = pl.DeviceIdType.MESH

D_MODEL = 1024
DEPTH = 2
HG_HEADS = 8
HG_DIM = 128
SW_Q_HEADS = 16
SW_KV_HEADS = 4
SW_HEAD_DIM = 64
SW_GROUP = 4
SW_WINDOW = 128
REL_BUCKETS = 32
REL_MAX_DIST = 128
FFN_DIM = 2816
ALPHA = (2.0 * DEPTH) ** 0.25
LN_EPS = 1e-5
RMS_EPS = 1e-6
ADAM_LR = 0.001
ADAM_B1 = 0.9
ADAM_B2 = 0.999
ADAM_EPS = 1e-08
ADAM_WD = 0.01
ADAM_STEP = 10

VMEM_BYTES_V7X = 64 * 1024 * 1024
VMEM_LIMIT = VMEM_BYTES_V7X - 8 * 1024 * 1024
LANES = 128
SUBLANES = 8

HG_C = 64
HG_RB = 256
CONV_R = 128
N_CHIPS = 4
N_DEV = 8

ANY_SPEC = pl.BlockSpec(memory_space=pl.ANY)


def _after(body, n_in, after):
    if after is None:
        return body, [], ()

    def wrapped(*refs):
        return body(*refs[:n_in], *refs[n_in + 1:])

    return wrapped, [ANY_SPEC], (after,)


def _params(sem=None):
    return pltpu.CompilerParams(dimension_semantics=sem, vmem_limit_bytes=VMEM_LIMIT)


def _tile(n, pref, unit=LANES):
    if n <= pref:
        return n
    best = None
    for t in range(unit, pref + 1, unit):
        if n % t == 0:
            best = t
    assert best is not None, (n, pref, unit)
    return best


def _dot(a, b, ca, cb):
    nb = a.ndim - 2
    batch = tuple(range(nb))
    return lax.dot_general(a.astype(BF16), b.astype(BF16), (((nb + ca,), (nb + cb,)), (batch, batch)),
                           preferred_element_type=F32)


@jax.custom_vjp
def mm(a, b):
    return _dot(a, b, 1, 0)


@jax.custom_vjp
def mm_nt(a, b):
    return _dot(a, b, 1, 1)


@jax.custom_vjp
def mm_tn(a, b):
    return _dot(a, b, 0, 0)


mm.defvjp(lambda a, b: (mm(a, b), (a, b)), lambda r, ct: (mm_nt(ct, r[1]), mm_tn(r[0], ct)))
mm_nt.defvjp(lambda a, b: (mm_nt(a, b), (a, b)), lambda r, ct: (mm(ct, r[1]), mm_tn(ct, r[0])))
mm_tn.defvjp(lambda a, b: (mm_tn(a, b), (a, b)), lambda r, ct: (mm_nt(r[1], ct), mm(r[0], ct)))


def _split2(x):
    hi = x.astype(BF16)
    return hi, (x - hi.astype(F32)).astype(BF16)


@jax.custom_vjp
def _scores(qt, kt):
    return _dot(qt, kt, 1, 1)


def _scores_bwd(r, ct):
    (qh, ql), (kh, kl) = _split2(r[0]), _split2(r[1])
    return _dot(ct, kh, 1, 0) + _dot(ct, kl, 1, 0), _dot(ct, qh, 0, 0) + _dot(ct, ql, 0, 0)


_scores.defvjp(lambda a, b: (_scores(a, b), (a, b)), _scores_bwd)


def _split3(x):
    hi = x.astype(BF16)
    r1 = x - hi.astype(F32)
    mid = r1.astype(BF16)
    lo = (r1 - mid.astype(F32)).astype(BF16)
    return hi, mid, lo


def _cumsum_impl(x):
    ax = x.ndim - 2
    n = x.shape[ax]
    row = lax.broadcasted_iota(jnp.int32, x.shape, ax)
    d = 1
    while d < n:
        x = x + jnp.where(row >= d, pltpu.roll(x, d, ax), 0.0)
        d *= 2
    return x


def _cumsum_rev_impl(x):
    ax = x.ndim - 2
    n = x.shape[ax]
    row = lax.broadcasted_iota(jnp.int32, x.shape, ax)
    d = 1
    while d < n:
        x = x + jnp.where(row < n - d, pltpu.roll(x, n - d, ax), 0.0)
        d *= 2
    return x


@jax.custom_vjp
def _cumsum(x):
    return _cumsum_impl(x)


_cumsum.defvjp(lambda x: (_cumsum_impl(x), None), lambda _, ct: (_cumsum_rev_impl(ct),))


def _matmul(a, b, *, mode, name, out_dtype=F32, add=None, add_scale=1.0, tm=512, tn=1408, tk=1408, after=None,
            split_n=False, planes=None, also_bf16=False):
    P = b.shape[0] if planes else 1
    a2, b2 = a.shape[-2:], b.shape[-2:]
    (M, K) = a2 if mode[0] == "n" else a2[::-1]
    (K2, N) = b2 if mode[1] == "n" else b2[::-1]
    assert K == K2, (a.shape, b.shape, mode)
    assert a.ndim == (3 if planes == "k" else 2) and b.ndim == (3 if planes else 2)
    tm, tn, tk = _tile(M, tm), _tile(N, tn), _tile(K, tk)
    nj, nkp = N // tn, K // tk
    nk = nkp * (P if planes == "k" else 1)
    ca, cb = (1 if mode[0] == "n" else 0), (0 if mode[1] == "n" else 1)
    a_blk, a_idx = ((tk, tm), lambda i, k: (k, i)) if mode[0] == "t" else ((tm, tk), lambda i, k: (i, k))
    b_blk, b_idx = ((tn, tk), lambda k, j: (j, k)) if mode[1] == "t" else ((tk, tn), lambda k, j: (k, j))
    if planes == "k":
        a_spec = pl.BlockSpec((None,) + a_blk, lambda i, j, k: (k // nkp,) + a_idx(i, k % nkp))
        b_spec = pl.BlockSpec((None,) + b_blk, lambda i, j, k: (k // nkp,) + b_idx(k % nkp, j))
    else:
        a_spec = pl.BlockSpec(a_blk, lambda i, j, k: a_idx(i, k))
        b_spec = (pl.BlockSpec((None,) + b_blk, lambda i, j, k: (j // nj,) + b_idx(k, j % nj)) if planes == "n"
                  else pl.BlockSpec(b_blk, lambda i, j, k: b_idx(k, j)))
    if split_n:
        o_spec, out_shape = pl.BlockSpec((None, tm, tn), lambda i, j, k: (j, i, 0)), (P * nj if planes == "n" else nj, M, tn)
    elif planes == "n":
        o_spec, out_shape = pl.BlockSpec((None, tm, tn), lambda i, j, k: (j // nj, i, j % nj)), (P, M, N)
    else:
        o_spec, out_shape = pl.BlockSpec((tm, tn), lambda i, j, k: (i, j)), (M, N)
    has_add = add is not None
    assert not (has_add and (split_n or planes == "n"))

    def finish(r, add_ref, o_refs):
        if has_add:
            r = r + add_scale * add_ref[...]
        o_refs[0][...] = r.astype(out_dtype)
        if also_bf16:
            o_refs[1][...] = r.astype(BF16)

    def body(*refs):
        a_ref, b_ref = refs[:2]
        add_ref = refs[2] if has_add else None
        first = 3 if has_add else 2
        o_ref = refs[first:first + (2 if also_bf16 else 1)]
        if nk == 1:
            finish(_dot(a_ref[...], b_ref[...], ca, cb), add_ref, o_ref)
            return
        acc_ref = refs[-1]
        k = pl.program_id(2)

        @pl.when(k == 0)
        def _():
            acc_ref[...] = jnp.zeros_like(acc_ref)

        acc_ref[...] += _dot(a_ref[...], b_ref[...], ca, cb)

        @pl.when(k == nk - 1)
        def _():
            finish(acc_ref[...], add_ref, o_ref)

    in_specs = [a_spec, b_spec] + ([o_spec] if has_add else [])
    args = (a, b) + ((add,) if has_add else ())
    body, xs, xa = _after(body, len(args), after)
    in_specs, args = in_specs + xs, args + xa
    out_shapes = [jax.ShapeDtypeStruct(out_shape, out_dtype)] + ([jax.ShapeDtypeStruct(out_shape, BF16)] if also_bf16 else [])
    out = pl.pallas_call(
        body, name=name, grid=(M // tm, nj * (P if planes == "n" else 1), nk), in_specs=in_specs,
        out_specs=[o_spec] * len(out_shapes), out_shape=out_shapes,
        scratch_shapes=[pltpu.VMEM((tm, tn), F32)] if nk > 1 else [],
        compiler_params=_params(("parallel", "parallel", "arbitrary")),
    )(*args)
    return tuple(out) if also_bf16 else out[0]


def _ln(z, g, b):
    mu = jnp.mean(z, axis=-1, keepdims=True)
    zc = z - mu
    var = jnp.mean(zc * zc, axis=-1, keepdims=True)
    return zc * lax.rsqrt(var + LN_EPS) * g + b


def _matmul_ln(a, b, h, g, bias, *, name, tgt=None, tm=512, a_t=False):
    (T, K), (K2, Dm) = (a.shape[::-1] if a_t else a.shape), b.shape
    assert K == K2 and h.shape == (T, Dm)
    tm = _tile(T, tm, SUBLANES)
    last = tgt is not None

    def body(*refs):
        a_ref, b_ref, h_ref, g_ref, bias_ref = refs[:5]
        z = ALPHA * h_ref[...] + _dot(a_ref[...], b_ref[...], 0 if a_t else 1, 0)
        if not last:
            z_ref, y_ref, yb_ref = refs[5:]
            y = _ln(z, g_ref[...], bias_ref[...])
            z_ref[...] = z
            y_ref[...] = y
            yb_ref[...] = y.astype(BF16)
            return
        t_ref, dz_ref, dzb_ref, dgb_ref, l_ref, da_ref = refs[5:]

        @pl.when(pl.program_id(0) == 0)
        def _():
            dgb_ref[...] = jnp.zeros_like(dgb_ref)
            l_ref[...] = jnp.zeros_like(l_ref)

        y, vjp = jax.vjp(_ln, z, g_ref[...], bias_ref[...])
        e = y - t_ref[...]
        dz, dg, db = vjp(e * (1.0 / Dm))
        l_ref[...] += 0.5 * jnp.sum(jnp.mean(e * e, axis=-1, keepdims=True), axis=0, keepdims=True)
        dzb = dz.astype(BF16)
        dz_ref[...] = dz
        dzb_ref[...] = dzb
        dgb_ref[...] += jnp.concatenate([dg, db], axis=0)
        da_ref[...] = _dot(dzb, b_ref[...], 1, 1).astype(BF16)

    row = pl.BlockSpec((tm, Dm), lambda i: (i, 0))
    vec = pl.BlockSpec((1, Dm), lambda i: (0, 0))
    a_spec = pl.BlockSpec((K, tm), lambda i: (0, i)) if a_t else pl.BlockSpec((tm, K), lambda i: (i, 0))
    in_specs = [a_spec, pl.BlockSpec((K, Dm), lambda i: (0, 0)), row, vec, vec]
    f32, b16 = jax.ShapeDtypeStruct((T, Dm), F32), jax.ShapeDtypeStruct((T, Dm), BF16)
    if not last:
        return pl.pallas_call(
            body, name=name, grid=(T // tm,), in_specs=in_specs, out_specs=[row, row, row], out_shape=[f32, f32, b16],
            compiler_params=_params(("parallel",)),
        )(a, b, h, g, bias)
    assert not a_t
    return pl.pallas_call(
        body, name=name, grid=(T // tm,), in_specs=in_specs + [row],
        out_specs=[row, row, pl.BlockSpec((2, Dm), lambda i: (0, 0)), pl.BlockSpec((SUBLANES, LANES), lambda i: (0, 0)), a_spec],
        out_shape=[f32, b16, jax.ShapeDtypeStruct((2, Dm), F32), jax.ShapeDtypeStruct((SUBLANES, LANES), F32),
                   jax.ShapeDtypeStruct((T, K), BF16)],
        compiler_params=_params(("arbitrary",)),
    )(a, b, h, g, bias, tgt)


def _ln_bwd_matmul(dy, z, g, b, w, *, name, out_t=False, tm=512, after=None):
    T, Dm = z.shape
    N = w.shape[0]
    tm = _tile(T, tm, LANES if out_t else SUBLANES)

    def body(dy_ref, z_ref, g_ref, b_ref, w_ref, dz_ref, dzb_ref, dgb_ref, o_ref):
        @pl.when(pl.program_id(0) == 0)
        def _():
            dgb_ref[...] = jnp.zeros_like(dgb_ref)

        _, vjp = jax.vjp(_ln, z_ref[...], g_ref[...], b_ref[...])
        dz, dg, db = vjp(dy_ref[...])
        dzb = dz.astype(BF16)
        dz_ref[...] = dz
        dzb_ref[...] = dzb
        dgb_ref[...] += jnp.concatenate([dg, db], axis=0)
        o_ref[...] = (_dot(w_ref[...], dzb, 1, 1) if out_t else _dot(dzb, w_ref[...], 1, 1)).astype(BF16)

    row = pl.BlockSpec((tm, Dm), lambda i: (i, 0))
    vec = pl.BlockSpec((1, Dm), lambda i: (0, 0))
    o_spec = pl.BlockSpec((N, tm), lambda i: (0, i)) if out_t else pl.BlockSpec((tm, N), lambda i: (i, 0))
    body, xs, xa = _after(body, 5, after)
    return pl.pallas_call(
        body, name=name, grid=(T // tm,), in_specs=[row, row, vec, vec, pl.BlockSpec((N, Dm), lambda i: (0, 0))] + xs,
        out_specs=[row, row, pl.BlockSpec((2, Dm), lambda i: (0, 0)), o_spec],
        out_shape=[jax.ShapeDtypeStruct((T, Dm), F32), jax.ShapeDtypeStruct((T, Dm), BF16),
                   jax.ShapeDtypeStruct((2, Dm), F32), jax.ShapeDtypeStruct((N, T) if out_t else (T, N), BF16)],
        compiler_params=_params(("arbitrary",)),
    )(dy, z, g, b, w, *xa)


def _hg_chunk(qr, fr, ir, gr, l0, l1, gw, st):
    C = qr.shape[-2]
    row = lax.broadcasted_iota(jnp.int32, qr.shape, qr.ndim - 2)
    lb = jax.nn.sigmoid(l0 - l1)
    fg = lb + (1.0 - lb) * jax.nn.sigmoid(fr)
    b = _cumsum(jnp.log(fg))
    q = jax.nn.silu(qr)
    k = 1.0 - fg
    bmid = lax.stop_gradient(jnp.sum(jnp.where(row == C // 2 - 1, b, 0.0), axis=-2, keepdims=True))
    bl = jnp.sum(jnp.where(row == C - 1, b, 0.0), axis=-2, keepdims=True)
    o = mm_nt(q * jnp.exp(b), st)
    sc = _scores(q * jnp.exp(b - bmid), k * jnp.exp(bmid - b))
    ti = lax.broadcasted_iota(jnp.int32, (C, C), 0)
    si = lax.broadcasted_iota(jnp.int32, (C, C), 1)
    sc = jnp.where(si <= ti, sc, 0.0)
    o = o + mm(sc, ir)
    st_new = st * jnp.exp(bl) + mm_tn(ir, k * jnp.exp(bl - b))
    on = o * lax.rsqrt(jnp.mean(o * o, axis=-1, keepdims=True) + RMS_EPS)
    return on * gw * jax.nn.silu(gr), st_new


def _heads(ref, rows):
    return jnp.stack([ref[rows, h * HG_DIM:(h + 1) * HG_DIM].astype(F32) for h in range(HG_HEADS)])


def _unheads(x):
    return jnp.concatenate([x[h] for h in range(HG_HEADS)], axis=-1)


def _hgrn_fwd(pre, lbl, gw, *, name):
    _, T, Dm = pre.shape
    rb = min(HG_RB, T)
    C = min(HG_C, rb)
    ncb = rb // C

    def body(pre_ref, lbl_ref, gw_ref, o_ref, st_ref, s_ref):
        @pl.when(pl.program_id(0) == 0)
        def _():
            s_ref[...] = jnp.zeros_like(s_ref)

        def chunk(ci, carry):
            r0 = pl.multiple_of(ci * C, C)
            rows = pl.ds(r0, C)
            st = s_ref[...]
            st_ref[ci] = st
            out, st_new = _hg_chunk(*[_heads(pre_ref.at[j], rows) for j in range(4)],
                                    _heads(lbl_ref, slice(0, 1)), _heads(lbl_ref, slice(1, 2)), gw_ref[...], st)
            o_ref[rows, :] = _unheads(out).astype(BF16)
            s_ref[...] = st_new
            return carry

        lax.fori_loop(0, ncb, chunk, 0, unroll=True)

    row = pl.BlockSpec((rb, Dm), lambda n: (n, 0))
    return pl.pallas_call(
        body, name=name, grid=(T // rb,),
        in_specs=[pl.BlockSpec((4, rb, Dm), lambda n: (0, n, 0)), pl.BlockSpec((2, Dm), lambda n: (0, 0)),
                  pl.BlockSpec((1, HG_DIM), lambda n: (0, 0))],
        out_specs=[row, pl.BlockSpec((ncb, HG_HEADS, HG_DIM, HG_DIM), lambda n: (n, 0, 0, 0))],
        out_shape=[jax.ShapeDtypeStruct((T, Dm), BF16),
                   jax.ShapeDtypeStruct((T // C, HG_HEADS, HG_DIM, HG_DIM), F32)],
        scratch_shapes=[pltpu.VMEM((HG_HEADS, HG_DIM, HG_DIM), F32)],
        compiler_params=_params(("arbitrary",)),
    )(pre, lbl, gw)


def _hgrn_bwd(pre, lbl, gw, states, dout, *, name, after=None):
    _, T, Dm = pre.shape
    rb = min(HG_RB, T)
    C = min(HG_C, rb)
    ncb = rb // C
    nb = T // rb

    def body(pre_ref, lbl_ref, gw_ref, st_ref, do_ref, dpre_ref, dlbl_ref, dgw_ref, ds_ref):
        @pl.when(pl.program_id(0) == 0)
        def _():
            ds_ref[...] = jnp.zeros_like(ds_ref)
            dlbl_ref[...] = jnp.zeros_like(dlbl_ref)
            dgw_ref[...] = jnp.zeros_like(dgw_ref)

        def chunk(cj, carry):
            ci = ncb - 1 - cj
            r0 = pl.multiple_of(ci * C, C)
            rows = pl.ds(r0, C)
            _, vjp = jax.vjp(_hg_chunk, *[_heads(pre_ref.at[j], rows) for j in range(4)],
                             _heads(lbl_ref, slice(0, 1)), _heads(lbl_ref, slice(1, 2)), gw_ref[...], st_ref[ci])
            *dpre, dl0, dl1, dgw, dst = vjp((_heads(do_ref, rows), ds_ref[...]))
            for j in range(4):
                dpre_ref[j, rows, :] = _unheads(dpre[j]).astype(BF16)
            dlbl_ref[0:1, :] += _unheads(dl0)
            dlbl_ref[1:2, :] += _unheads(dl1)
            dgw_ref[...] += dgw
            ds_ref[...] = dst
            return carry

        lax.fori_loop(0, ncb, chunk, 0, unroll=True)

    row = pl.BlockSpec((rb, Dm), lambda n: (nb - 1 - n, 0))
    lsp = pl.BlockSpec((2, Dm), lambda n: (0, 0))
    gsp = pl.BlockSpec((1, HG_DIM), lambda n: (0, 0))
    pre_spec = pl.BlockSpec((4, rb, Dm), lambda n: (0, nb - 1 - n, 0))
    body, xs, xa = _after(body, 5, after)
    return pl.pallas_call(
        body, name=name, grid=(nb,),
        in_specs=[pre_spec, lsp, gsp, pl.BlockSpec((ncb, HG_HEADS, HG_DIM, HG_DIM), lambda n: (nb - 1 - n, 0, 0, 0)), row] + xs,
        out_specs=[pre_spec, lsp, gsp],
        out_shape=[jax.ShapeDtypeStruct((4, T, Dm), BF16), jax.ShapeDtypeStruct((2, Dm), F32),
                   jax.ShapeDtypeStruct((1, HG_DIM), F32)],
        scratch_shapes=[pltpu.VMEM((HG_HEADS, HG_DIM, HG_DIM), F32)],
        compiler_params=_params(("arbitrary",)),
    )(pre, lbl, gw, states, dout, *xa)


CONV_HALO = 2 * SUBLANES


def _conv_rows(u_ref, scr, w, bias, r0, R):
    cur = u_ref[pl.ds(r0, R), :].astype(F32)
    p0 = pl.multiple_of(jnp.maximum(r0 - CONV_HALO, 0), CONV_HALO)
    scr[0:CONV_HALO, :] = jnp.where(r0 > 0, u_ref[pl.ds(p0, CONV_HALO), :].astype(F32), 0.0)
    scr[CONV_HALO:CONV_HALO + R, :] = cur
    s1 = scr[CONV_HALO - 1:CONV_HALO - 1 + R, :]
    s2 = scr[CONV_HALO - 2:CONV_HALO - 2 + R, :]
    return w[0:1, :] * s2 + w[1:2, :] * s1 + w[2:3, :] * cur + bias, cur, s1, s2


def _halves_spec(T, Fd):
    per = Fd // 2 // LANES
    return pl.BlockSpec((2, None, T, LANES), lambda j: (0, j // per, 0, j % per))


def _conv_gate_fwd(u, wa, wb, ba, bb, *, name):
    T, Fd = u.shape[2], 2 * u.shape[3]
    R = min(CONV_R, T)
    tc = LANES

    def body(u_ref, wa_ref, wb_ref, ba_ref, bb_ref, o_ref, sa, sb):
        wa_, wb_, ba_, bb_ = wa_ref[...], wb_ref[...], ba_ref[...], bb_ref[...]

        def step(ri, carry):
            r0 = pl.multiple_of(ri * R, R)
            ca = _conv_rows(u_ref.at[0], sa, wa_, ba_, r0, R)[0]
            cb = _conv_rows(u_ref.at[1], sb, wb_, bb_, r0, R)[0]
            o_ref[pl.ds(r0, R), :] = (jax.nn.silu(ca) * cb).astype(BF16)
            return carry

        lax.fori_loop(0, T // R, step, 0)

    col = pl.BlockSpec((T, tc), lambda j: (0, j))
    wsp = pl.BlockSpec((3, tc), lambda j: (0, j))
    bsp = pl.BlockSpec((1, tc), lambda j: (0, j))
    both = _halves_spec(T, Fd)
    return pl.pallas_call(
        body, name=name, grid=(Fd // tc,), in_specs=[both, wsp, wsp, bsp, bsp], out_specs=col,
        out_shape=jax.ShapeDtypeStruct((T, Fd), BF16),
        scratch_shapes=[pltpu.VMEM((CONV_HALO + R, tc), F32)] * 2,
        compiler_params=_params(("parallel",)),
    )(u, wa, wb, ba, bb)


def _conv_gate_bwd(u, wa, wb, ba, bb, dact, *, name):
    T, Fd = u.shape[2], 2 * u.shape[3]
    R = min(CONV_R, T)
    nr = T // R
    tc = LANES

    def body(u_ref, wa_ref, wb_ref, ba_ref, bb_ref, da_ref,
             du_ref, dp_ref, sa, sb, sda, sdb):
        wa_, wb_, ba_, bb_ = wa_ref[...], wb_ref[...], ba_ref[...], bb_ref[...]
        sda[R:R + SUBLANES, :] = jnp.zeros((SUBLANES, tc), F32)
        sdb[R:R + SUBLANES, :] = jnp.zeros((SUBLANES, tc), F32)

        def taps(dc, cur, s1, s2):
            return jnp.concatenate([jnp.sum(dc * s2, axis=0, keepdims=True), jnp.sum(dc * s1, axis=0, keepdims=True),
                                    jnp.sum(dc * cur, axis=0, keepdims=True)], axis=0)

        def du_rows(sd, dc, w):
            sd[0:R, :] = dc
            du = w[2:3, :] * dc + w[1:2, :] * sd[1:1 + R, :] + w[0:1, :] * sd[2:2 + R, :]
            sd[R:R + SUBLANES, :] = dc[0:SUBLANES]
            return du

        def step(rj, carry):
            dwa, dwb, dba, dbb = carry
            r0 = pl.multiple_of((nr - 1 - rj) * R, R)
            ca, cura, s1a, s2a = _conv_rows(u_ref.at[0], sa, wa_, ba_, r0, R)
            cb, curb, s1b, s2b = _conv_rows(u_ref.at[1], sb, wb_, bb_, r0, R)
            dact_ = da_ref[pl.ds(r0, R), :].astype(F32)
            sg = jax.nn.sigmoid(ca)
            dca = dact_ * cb * (sg * (1.0 + ca * (1.0 - sg)))
            dcb = dact_ * (ca * sg)
            du_ref[0, pl.ds(r0, R), :] = du_rows(sda, dca, wa_).astype(BF16)
            du_ref[1, pl.ds(r0, R), :] = du_rows(sdb, dcb, wb_).astype(BF16)
            return (dwa + taps(dca, cura, s1a, s2a), dwb + taps(dcb, curb, s1b, s2b),
                    dba + jnp.sum(dca, axis=0, keepdims=True), dbb + jnp.sum(dcb, axis=0, keepdims=True))

        z3 = jnp.zeros((3, tc), F32)
        z1 = jnp.zeros((1, tc), F32)
        dwa, dwb, dba, dbb = lax.fori_loop(0, nr, step, (z3, z3, z1, z1))
        dp_ref[0] = jnp.concatenate([dwa, dba], axis=0)
        dp_ref[1] = jnp.concatenate([dwb, dbb], axis=0)

    col = pl.BlockSpec((T, tc), lambda j: (0, j))
    wsp = pl.BlockSpec((3, tc), lambda j: (0, j))
    bsp = pl.BlockSpec((1, tc), lambda j: (0, j))
    both = _halves_spec(T, Fd)
    return pl.pallas_call(
        body, name=name, grid=(Fd // tc,), in_specs=[both, wsp, wsp, bsp, bsp, col],
        out_specs=[both, pl.BlockSpec((2, 4, tc), lambda j: (0, 0, j))],
        out_shape=[jax.ShapeDtypeStruct(u.shape, BF16), jax.ShapeDtypeStruct((2, 4, Fd), F32)],
        scratch_shapes=[pltpu.VMEM((CONV_HALO + R, tc), F32)] * 2 + [pltpu.VMEM((R + SUBLANES, tc), F32)] * 2,
        compiler_params=_params(("parallel",)),
    )(u, wa, wb, ba, bb, dact)


def _bucket_index():
    t = np.arange(SW_WINDOW)[None, :] + SW_WINDOW
    s = np.arange(2 * SW_WINDOW)[:, None]
    dist = np.maximum(t - s, 0)
    exact = REL_BUCKETS // 2
    d = np.maximum(dist, 1).astype(np.float32)
    log_b = exact + (np.log(d / np.float32(exact)) / np.float32(math.log(REL_MAX_DIST / exact))
                     * np.float32(REL_BUCKETS - exact)).astype(np.int32)
    bucket = np.where(dist < exact, dist, np.minimum(log_b, REL_BUCKETS - 1))
    return bucket.astype(np.int32).reshape(1, -1)


BIAS_COLS = SW_WINDOW * 2 * SW_WINDOW
BIAS_TILE = 4096


def _bias_from_table(table, bucket, *, name):
    def body(t_ref, idx_ref, o_ref):
        onehot = (lax.broadcasted_iota(jnp.int32, (REL_BUCKETS, BIAS_TILE), 0) == idx_ref[...]).astype(BF16)
        acc = jnp.zeros((SW_Q_HEADS, BIAS_TILE), F32)
        for piece in _split3(t_ref[...]):
            acc = acc + lax.dot_general(piece, onehot, (((0,), (0,)), ((), ())), preferred_element_type=F32)
        o_ref[...] = acc

    return pl.pallas_call(
        body, name=name, grid=(BIAS_COLS // BIAS_TILE,),
        in_specs=[pl.BlockSpec((REL_BUCKETS, SW_Q_HEADS), lambda j: (0, 0)), pl.BlockSpec((1, BIAS_TILE), lambda j: (0, j))],
        out_specs=pl.BlockSpec((SW_Q_HEADS, BIAS_TILE), lambda j: (0, j)),
        out_shape=jax.ShapeDtypeStruct((SW_Q_HEADS, BIAS_COLS), F32),
        compiler_params=_params(("parallel",)),
    )(table, bucket)


def _table_grad(dbias, bucket, *, name):
    def body(d_ref, idx_ref, o_ref):
        @pl.when(pl.program_id(0) == 0)
        def _():
            o_ref[...] = jnp.zeros_like(o_ref)

        onehot = (lax.broadcasted_iota(jnp.int32, (REL_BUCKETS, BIAS_TILE), 0) == idx_ref[...]).astype(BF16)
        acc = jnp.zeros((REL_BUCKETS, SW_Q_HEADS), F32)
        for piece in _split3(d_ref[...]):
            acc = acc + lax.dot_general(onehot, piece, (((1,), (1,)), ((), ())), preferred_element_type=F32)
        o_ref[...] += acc

    return pl.pallas_call(
        body, name=name, grid=(BIAS_COLS // BIAS_TILE,),
        in_specs=[pl.BlockSpec((SW_Q_HEADS, BIAS_TILE), lambda j: (0, j)), pl.BlockSpec((1, BIAS_TILE), lambda j: (0, j))],
        out_specs=pl.BlockSpec((REL_BUCKETS, SW_Q_HEADS), lambda j: (0, 0)),
        out_shape=jax.ShapeDtypeStruct((REL_BUCKETS, SW_Q_HEADS), F32),
        compiler_params=_params(("arbitrary",)),
    )(dbias, bucket)


KV_DIM = SW_KV_HEADS * SW_HEAD_DIM
GROUP_ROWS = SW_GROUP * SW_HEAD_DIM
GROUP_LANES = SW_GROUP * SW_WINDOW


def _band_mask(n):
    s = lax.broadcasted_iota(jnp.int32, (2 * SW_WINDOW, GROUP_LANES), 0)
    t = (lax.broadcasted_iota(jnp.int32, (2 * SW_WINDOW, GROUP_LANES), 1) & (SW_WINDOW - 1)) + SW_WINDOW
    dist = t - s
    return (dist >= 0) & (dist < SW_WINDOW) & ((n > 0) | (s >= SW_WINDOW))


def _side_by_side(x_ref, g):
    r0 = g * GROUP_ROWS
    return jnp.concatenate([x_ref[r0 + r * SW_HEAD_DIM:r0 + (r + 1) * SW_HEAD_DIM, :] for r in range(SW_GROUP)], axis=1)


def _group_inputs(bias_ref, sink_ref, g):
    heads = range(g * SW_GROUP, (g + 1) * SW_GROUP)
    bias = jnp.concatenate([bias_ref[h] for h in heads], axis=1)
    sink = jnp.concatenate([jnp.broadcast_to(sink_ref[:, h:h + 1], (1, SW_WINDOW)) for h in heads], axis=1)
    return heads, bias, sink


def _kv_pair(kvp_ref, kvc_ref, g):
    ks = slice(g * SW_HEAD_DIM, (g + 1) * SW_HEAD_DIM)
    vs = slice(KV_DIM + g * SW_HEAD_DIM, KV_DIM + (g + 1) * SW_HEAD_DIM)
    kk = jnp.concatenate([kvp_ref[:, ks], kvc_ref[:, ks]], axis=0)
    vv = jnp.concatenate([kvp_ref[:, vs], kvc_ref[:, vs]], axis=0)
    return kk, vv, ks, vs


def _col_max(x):
    return jnp.max(x, axis=0, keepdims=True)


def _col_sum(x):
    return jnp.sum(x, axis=0, keepdims=True)


def _attn_fwd(qt, kv, bias, sinks, *, name):
    Dm, T = qt.shape
    W = SW_WINDOW

    def body(q_ref, kvc_ref, kvp_ref, bias_ref, sink_ref, o_ref):
        mask = _band_mask(pl.program_id(0))
        G = range(SW_KV_HEADS)
        ins = [_group_inputs(bias_ref, sink_ref, g) for g in G]
        kvs = [_kv_pair(kvp_ref, kvc_ref, g) for g in G]
        q = [_side_by_side(q_ref, g) for g in G]
        lg = [jnp.where(mask, mm(kvs[g][0], q[g]) * (SW_HEAD_DIM ** -0.5) + ins[g][1], -jnp.inf) for g in G]
        m = [jnp.maximum(_col_max(lg[g]), ins[g][2]) for g in G]
        p = [jnp.exp(lg[g] - m[g]) for g in G]
        den = [_col_sum(p[g]) + jnp.exp(ins[g][2] - m[g]) for g in G]
        o = [mm_tn(kvs[g][1], p[g]) / den[g] for g in G]
        for g in G:
            for r in range(SW_GROUP):
                o_ref[g * GROUP_ROWS + r * SW_HEAD_DIM:g * GROUP_ROWS + (r + 1) * SW_HEAD_DIM, :] = (
                    o[g][:, r * W:(r + 1) * W].astype(BF16))

    return pl.pallas_call(
        body, name=name, grid=(T // W,),
        in_specs=[pl.BlockSpec((Dm, W), lambda n: (0, n)),
                  pl.BlockSpec((W, 2 * KV_DIM), lambda n: (n, 0)),
                  pl.BlockSpec((W, 2 * KV_DIM), lambda n: (jnp.maximum(n - 1, 0), 0)),
                  pl.BlockSpec((SW_Q_HEADS, 2 * W, W), lambda n: (0, 0, 0)),
                  pl.BlockSpec((1, SW_Q_HEADS), lambda n: (0, 0))],
        out_specs=pl.BlockSpec((Dm, W), lambda n: (0, n)),
        out_shape=jax.ShapeDtypeStruct((Dm, T), BF16),
        compiler_params=_params(("parallel",)),
    )(qt, kv, kv, bias, sinks)


def _attn_bwd(qt, kv, bias, sinks, dot, *, name):
    Dm, T = qt.shape
    W = SW_WINDOW
    nb = T // W

    def body(q_ref, kvc_ref, kvp_ref, bias_ref, sink_ref, do_ref,
             dq_ref, dkv_ref, dbias_ref, dsink_ref, carry_ref):
        @pl.when(pl.program_id(0) == 0)
        def _():
            carry_ref[...] = jnp.zeros_like(carry_ref)
            dbias_ref[...] = jnp.zeros_like(dbias_ref)
            dsink_ref[...] = jnp.zeros_like(dsink_ref)

        n = nb - 1 - pl.program_id(0)
        mask = _band_mask(n)
        lane = lax.broadcasted_iota(jnp.int32, (1, SW_Q_HEADS), 1)
        sc = SW_HEAD_DIM ** -0.5
        G = range(SW_KV_HEADS)
        ins = [_group_inputs(bias_ref, sink_ref, g) for g in G]
        kvs = [_kv_pair(kvp_ref, kvc_ref, g) for g in G]
        q = [_side_by_side(q_ref, g) for g in G]
        do = [_side_by_side(do_ref, g) for g in G]
        lg = [jnp.where(mask, mm(kvs[g][0], q[g]) * sc + ins[g][1], -jnp.inf) for g in G]
        m = [jnp.maximum(_col_max(lg[g]), ins[g][2]) for g in G]
        p = [jnp.exp(lg[g] - m[g]) for g in G]
        ps = [jnp.exp(ins[g][2] - m[g]) for g in G]
        rden = [1.0 / (_col_sum(p[g]) + ps[g]) for g in G]
        pn = [p[g] * rden[g] for g in G]
        dpn = [mm(kvs[g][1], do[g]) for g in G]
        delta = [_col_sum(pn[g] * dpn[g]) for g in G]
        ds = [pn[g] * (dpn[g] - delta[g]) for g in G]
        dsr = [-(ps[g] * rden[g]) * delta[g] for g in G]
        dq = [mm_tn(kvs[g][0], ds[g]) * sc for g in G]
        dkk = [mm_nt(ds[g], q[g]) * sc for g in G]
        dvv = [mm_nt(pn[g], do[g]) for g in G]
        dsink = jnp.zeros((1, SW_Q_HEADS), F32)
        for g in G:
            _, _, ks, vs = kvs[g]
            for r, h in enumerate(ins[g][0]):
                cols = slice(r * W, (r + 1) * W)
                dbias_ref[h] += ds[g][:, cols]
                dq_ref[g * GROUP_ROWS + r * SW_HEAD_DIM:g * GROUP_ROWS + (r + 1) * SW_HEAD_DIM, :] = dq[g][:, cols].astype(BF16)
                dsink = dsink + jnp.where(lane == h, jnp.sum(dsr[g][:, cols], axis=1, keepdims=True), 0.0)
            dkv_ref[:, ks] = (carry_ref[:, ks] + dkk[g][W:]).astype(BF16)
            dkv_ref[:, vs] = (carry_ref[:, vs] + dvv[g][W:]).astype(BF16)
            carry_ref[:, ks] = dkk[g][:W]
            carry_ref[:, vs] = dvv[g][:W]
        dsink_ref[...] += dsink

    rev = lambda n: (nb - 1 - n, 0)
    revt = lambda n: (0, nb - 1 - n)
    return pl.pallas_call(
        body, name=name, grid=(nb,),
        in_specs=[pl.BlockSpec((Dm, W), revt),
                  pl.BlockSpec((W, 2 * KV_DIM), rev),
                  pl.BlockSpec((W, 2 * KV_DIM), lambda n: (jnp.maximum(nb - 2 - n, 0), 0)),
                  pl.BlockSpec((SW_Q_HEADS, 2 * W, W), lambda n: (0, 0, 0)),
                  pl.BlockSpec((1, SW_Q_HEADS), lambda n: (0, 0)),
                  pl.BlockSpec((Dm, W), revt)],
        out_specs=[pl.BlockSpec((Dm, W), revt), pl.BlockSpec((W, 2 * KV_DIM), rev),
                   pl.BlockSpec((SW_Q_HEADS, 2 * W, W), lambda n: (0, 0, 0)),
                   pl.BlockSpec((1, SW_Q_HEADS), lambda n: (0, 0))],
        out_shape=[jax.ShapeDtypeStruct((Dm, T), BF16), jax.ShapeDtypeStruct((T, 2 * KV_DIM), BF16),
                   jax.ShapeDtypeStruct((SW_Q_HEADS, 2 * W, W), F32), jax.ShapeDtypeStruct((1, SW_Q_HEADS), F32)],
        scratch_shapes=[pltpu.VMEM((W, 2 * KV_DIM), F32)],
        compiler_params=_params(("arbitrary",)),
    )(qt, kv, kv, bias, sinks, dot)


def _ffn_fwd(hb, w, l, after=None):
    u = _matmul(hb, w["ffn_in"][l], mode="nn", planes="n", out_dtype=BF16, name=f"ffn{l}_up", tm=1024, after=after)
    u = u.reshape((2, 2) + u.shape[1:])
    act = _conv_gate_fwd(u, w["conv_w_a"][l], w["conv_w_b"][l], w["conv_b_a"][l], w["conv_b_b"][l],
                         name=f"ffn{l}_conv_gate")
    return u, act


def _ffn_bwd(dffb, dh_scaled, hb, u, act, w, l, dact):
    g_out = _matmul(act, dffb, mode="tn", name=f"ffn{l}_down_dw", tm=1408, tn=1024, tk=1024, also_bf16=True)
    du, g_conv = _conv_gate_bwd(u, w["conv_w_a"][l], w["conv_w_b"][l], w["conv_b_a"][l], w["conv_b_b"][l],
                                dact, name=f"ffn{l}_conv_gate_bwd")
    du = du.reshape((N_CHIPS,) + du.shape[2:])
    dh = _matmul(du, w["ffn_in"][l], mode="nt", planes="k", add=dh_scaled, add_scale=ALPHA, name=f"ffn{l}_up_dx",
                 tm=1024, tn=1024, tk=FFN_DIM // 2)
    g_in = _matmul(hb, du, mode="tn", planes="n", name=f"ffn{l}_up_dw", tm=1024, tn=FFN_DIM // 2, tk=1024, also_bf16=True)
    return dh, dict(ffn_out=g_out, ffn_in=g_in, conv=g_conv)


def _local_step(x, xb, tgt, w, more_weights, emit):
    bucket = jnp.asarray(_bucket_index())

    pre = _matmul(xb, w["hg_in"], mode="nn", planes="n", out_dtype=BF16, name="hg_in", tm=1024, tn=1024,
                  after=w.get("token"))
    og, states = _hgrn_fwd(pre, w["lb_logits"], w["gnorm"], name="hgrn_fwd")
    z1, h1, h1b = _matmul_ln(og, w["hg_out"], x, w["ln_mix_g"][0], w["ln_mix_b"][0], name="hg_out_ln")
    w = {**w, **more_weights(1, h1b)}
    u0, act0 = _ffn_fwd(h1b, w, 0, after=w.get("token"))
    z2, h2, h2b = _matmul_ln(act0, w["ffn_out"][0], h1, w["ln_ffn_g"][0], w["ln_ffn_b"][0], name="ffn0_down_ln")
    kv = _matmul(h2b, w["kv"], mode="nn", out_dtype=BF16, name="kv_proj")

    bias = _bias_from_table(w["rel_bias"], bucket, name="rel_bias_expand").reshape(SW_Q_HEADS, 2 * SW_WINDOW, SW_WINDOW)
    q1 = _matmul(w["sw_q"], h2b, mode="tt", out_dtype=BF16, name="sw_q", tm=1024, tn=1024)
    o1 = _attn_fwd(q1, kv, bias, w["sinks"], name="attn_fwd")
    z3, h3, h3b = _matmul_ln(o1, w["sw_out"], h2, w["ln_mix_g"][1], w["ln_mix_b"][1], a_t=True, name="sw_out_ln")
    w = {**w, **more_weights(2, h3b)}
    u1, act1 = _ffn_fwd(h3b, w, 1)

    g = {}
    dz, dzb, g["ln_ffn1"], loss_tile, dact1 = _matmul_ln(act1, w["ffn_out"][1], h3, w["ln_ffn_g"][1], w["ln_ffn_b"][1],
                                                         tgt=tgt, name="ffn1_down_ln_loss")

    dh3, gf1 = _ffn_bwd(dzb, dz, h3b, u1, act1, w, 1, dact1)
    dz, dzb, g["ln_mix1"], do1 = _ln_bwd_matmul(dh3, z3, w["ln_mix_g"][1], w["ln_mix_b"][1], w["sw_out"], out_t=True,
                                                name="ln_mix1_bwd_sw_out_dx")
    g_sw_out = _matmul(o1, dzb, mode="nn", name="sw_out_dw", tm=1024, tn=1024, tk=1024, also_bf16=True)
    dq1, dkv, dbias, dsinks = _attn_bwd(q1, kv, bias, w["sinks"], do1, name="attn_bwd")
    g["sinks"] = dsinks
    g["rel_bias"] = _table_grad(dbias.reshape(SW_Q_HEADS, BIAS_COLS), bucket, name="rel_bias_grad")
    dh2 = _matmul(dq1, w["sw_q"], mode="tt", add=dz, add_scale=ALPHA, name="sw_q_dx", tn=1024)
    dh2 = _matmul(dkv, w["kv"], mode="nt", add=dh2, name="kv_dx", tn=1024)
    g_sw_q = _matmul(h2b, dq1, mode="tt", name="sw_q_dw", tm=1024, tn=1024, tk=1024, also_bf16=True)
    g_kv = _matmul(h2b, dkv, mode="tn", name="kv_dw", tm=1024, tn=512, tk=1024, also_bf16=True)
    tok = emit(1, dict(sw_q=g_sw_q, sw_out=g_sw_out, kv=g_kv, ffn_in=gf1["ffn_in"], ffn_out=gf1["ffn_out"]))

    dz, dzb, g["ln_ffn0"], dact0 = _ln_bwd_matmul(dh2, z2, w["ln_ffn_g"][0], w["ln_ffn_b"][0], w["ffn_out"][0],
                                                  name="ln_ffn0_bwd_down_dx", after=tok)
    dh1, gf0 = _ffn_bwd(dzb, dz, h1b, u0, act0, w, 0, dact0)
    dz, dzb, g["ln_mix0"], dog = _ln_bwd_matmul(dh1, z1, w["ln_mix_g"][0], w["ln_mix_b"][0], w["hg_out"],
                                                name="ln_mix0_bwd_hg_out_dx")
    g_hg_out = _matmul(og, dzb, mode="tn", name="hg_out_dw", tm=1024, tn=1024, tk=1024, also_bf16=True)
    tok = emit(2, dict(hg_out=g_hg_out, ffn_in=gf0["ffn_in"], ffn_out=gf0["ffn_out"]))
    dpre, g["lb_logits"], g["gnorm"] = _hgrn_bwd(pre, w["lb_logits"], w["gnorm"], states, dog, name="hgrn_bwd", after=tok)
    tok = emit(3, dict(hg_in=_matmul(xb, dpre, mode="tn", planes="n", name="hg_in_dw", tm=1024, tn=1024, tk=1024, also_bf16=True)))
    dx = _matmul(dpre, w["hg_in"], mode="nt", planes="k", add=dz, add_scale=ALPHA, name="hg_in_dx", tm=1024, tn=1024,
                 tk=1024, after=tok)
    g["conv0"], g["conv1"] = gf0["conv"], gf1["conv"]
    return loss_tile, dx, g


def _adamw(wt, ga, gb, m, v, *, name, rows=None, prev=None):
    R, Cc = wt.shape
    r0, n = rows if rows is not None else (0, R)
    tr = _tile(n, 256, SUBLANES) if n % SUBLANES == 0 else n
    assert r0 % tr == 0
    c1 = 1.0 - ADAM_B1 ** ADAM_STEP
    c2 = 1.0 - ADAM_B2 ** ADAM_STEP
    two = gb is not None
    n_in = 5 if two else 4

    def body(*refs):
        if two:
            w_ref, ga_ref, gb_ref, m_ref, v_ref = refs[:5]
            g_ = ga_ref[...] + gb_ref[...]
        else:
            w_ref, ga_ref, m_ref, v_ref = refs[:4]
            g_ = ga_ref[...]
        g_ref, d_ref, nm_ref, nv_ref = refs[-4:]
        nm = ADAM_B1 * m_ref[...] + (1.0 - ADAM_B1) * g_
        nv = ADAM_B2 * v_ref[...] + (1.0 - ADAM_B2) * (g_ * g_)
        g_ref[...] = g_
        d_ref[...] = -ADAM_LR * ((nm / c1) / (jnp.sqrt(nv / c2) + ADAM_EPS) + ADAM_WD * w_ref[...])
        nm_ref[...] = nm
        nv_ref[...] = nv

    full = pl.BlockSpec((tr, Cc), lambda i: (i + r0 // tr, 0))
    part = pl.BlockSpec((tr, Cc), lambda i: (i, 0))
    args = (wt, ga, gb, m, v) if two else (wt, ga, m, v)
    in_specs = [full] + [part] * (n_in - 3) + [full, full]
    aliases = {}
    if prev is not None:
        args, in_specs = args + tuple(prev), in_specs + [ANY_SPEC] * 4
        aliases = {n_in + t: t for t in range(4)}
    return pl.pallas_call(
        body, name=name, grid=(n // tr,), in_specs=in_specs, out_specs=[full] * 4,
        out_shape=[jax.ShapeDtypeStruct((R, Cc), F32)] * 4, input_output_aliases=aliases,
        compiler_params=_params(("parallel",)),
    )(*args)


def _adamw_small(ws, gs, ms, vs, *, name):
    n = len(ws)
    c1 = 1.0 - ADAM_B1 ** ADAM_STEP
    c2 = 1.0 - ADAM_B2 ** ADAM_STEP

    def body(*refs):
        w_refs, g_refs, m_refs, v_refs = (refs[k * n:(k + 1) * n] for k in range(4))
        d_refs, nm_refs, nv_refs = (refs[(4 + k) * n:(5 + k) * n] for k in range(3))
        for i in range(n):
            g_ = g_refs[i][...]
            nm = ADAM_B1 * m_refs[i][...] + (1.0 - ADAM_B1) * g_
            nv = ADAM_B2 * v_refs[i][...] + (1.0 - ADAM_B2) * (g_ * g_)
            d_refs[i][...] = -ADAM_LR * ((nm / c1) / (jnp.sqrt(nv / c2) + ADAM_EPS) + ADAM_WD * w_refs[i][...])
            nm_refs[i][...] = nm
            nv_refs[i][...] = nv

    vm = pl.BlockSpec(memory_space=pltpu.VMEM)
    out = pl.pallas_call(
        body, name=name, in_specs=[vm] * (4 * n), out_specs=[vm] * (3 * n),
        out_shape=[jax.ShapeDtypeStruct(w.shape, F32) for w in ws] * 3,
    )(*ws, *gs, *ms, *vs)
    return out[:n], out[n:2 * n], out[2 * n:]


HBM_SPEC = pl.BlockSpec(memory_space=pltpu.HBM)
SEM_SPEC = pl.BlockSpec(memory_space=pltpu.SEMAPHORE)
VMEM_SPEC = pl.BlockSpec(memory_space=pltpu.VMEM)
DATAFLOW = pltpu.SideEffectType.DATAFLOW_SIDE_EFFECTING


def _in_hbm(a):
    return pltpu.with_memory_space_constraint(a, pltpu.HBM)


def _place():
    return lax.axis_index("x"), lax.axis_index("y"), lax.axis_index("c")


def _other_chips(x, y):
    return [(1 - x, y), (x, 1 - y), (1 - x, 1 - y)]


def _sum8(vs, *, name):
    n = len(vs)

    def body(*refs):
        v_refs, all_refs, o_refs = refs[:n], refs[n:2 * n], refs[2 * n:3 * n]
        send_sems, recv_sems, local_sems = refs[3 * n:]
        x, y, c = _place()
        me, sibling = (x, y, c), (x, y, 1 - c)
        chips = _other_chips(x, y)

        def slot(i, px, py, pc):
            return all_refs[i].at[4 * px + 2 * py + pc]

        def copy(i, k, block, to, src=None):
            return pltpu.make_async_remote_copy(
                src_ref=slot(i, *block) if src is None else src, dst_ref=slot(i, *block),
                send_sem=send_sems.at[7 * i + k], recv_sem=recv_sems.at[7 * i + k], device_id=to, device_id_type=MESH)

        mine = [pltpu.make_async_copy(v_refs[i], slot(i, *me), local_sems.at[i]) for i in range(n)]
        for cp in mine:
            cp.start()
        first = [copy(i, 0, me, sibling, src=v_refs[i]) for i in range(n)]
        first += [copy(i, 1 + j, me, (*chip, c), src=v_refs[i]) for i in range(n) for j, chip in enumerate(chips)]
        for cp in first:
            cp.start()
        passed = []
        for i in range(n):
            for j, chip in enumerate(chips):
                copy(i, 1 + j, (*chip, c), me).wait_recv()
                passed.append(copy(i, 4 + j, (*chip, c), sibling))
                passed[-1].start()
        for i in range(n):
            copy(i, 0, sibling, me).wait_recv()
            for j, chip in enumerate(chips):
                copy(i, 4 + j, (*chip, 1 - c), me).wait_recv()
        for cp in first + passed:
            cp.wait_send()
        for cp in mine:
            cp.wait()
        for i in range(n):
            acc = all_refs[i][0]
            for d in range(1, N_DEV):
                acc = acc + all_refs[i][d]
            o_refs[i][...] = acc

    return pl.pallas_call(
        body, name=name, in_specs=[VMEM_SPEC] * n, out_specs=[VMEM_SPEC] * (2 * n),
        out_shape=[jax.ShapeDtypeStruct((N_DEV,) + v.shape, F32) for v in vs] + [jax.ShapeDtypeStruct(v.shape, F32) for v in vs],
        scratch_shapes=[pltpu.SemaphoreType.DMA((7 * n,)), pltpu.SemaphoreType.DMA((7 * n,)), pltpu.SemaphoreType.DMA((n,))],
        compiler_params=pltpu.CompilerParams(vmem_limit_bytes=VMEM_LIMIT),
    )(*vs)[n:]


def _swap_sibling(vs, *, name):
    n = len(vs)

    def body(*refs):
        src, dst, send_sems, recv_sems = refs[:n], refs[n:2 * n], refs[2 * n], refs[2 * n + 1]
        x, y, c = _place()
        cps = [pltpu.make_async_remote_copy(src_ref=src[i], dst_ref=dst[i], send_sem=send_sems.at[i],
                                            recv_sem=recv_sems.at[i], device_id=(x, y, 1 - c), device_id_type=MESH)
               for i in range(n)]
        for cp in cps:
            cp.start()
        for cp in cps:
            cp.wait()

    return pl.pallas_call(
        body, name=name, in_specs=[HBM_SPEC] * n, out_specs=[HBM_SPEC] * n,
        out_shape=[jax.ShapeDtypeStruct(v.shape, v.dtype) for v in vs],
        scratch_shapes=[pltpu.SemaphoreType.DMA((n,)), pltpu.SemaphoreType.DMA((n,))],
    )(*vs)


def _gather_copies(srcs, lands, send, recv, sibling=False):
    x, y, c = _place()
    out = []
    for i, (src, land) in enumerate(zip(srcs, lands)):
        half = land.shape[1] // 2
        rows = pl.ds(c * half, half)
        for k, (px, py) in enumerate(_other_chips(x, y)):
            if sibling:
                src_ref, dst_ref, to = src.at[2 * px + py, rows], land.at[2 * px + py, rows], (x, y, 1 - c)
            else:
                src_ref, dst_ref, to = src.at[rows], land.at[2 * x + y, rows], (px, py, c)
            out.append(pltpu.make_async_remote_copy(src_ref=src_ref, dst_ref=dst_ref, send_sem=send.at[3 * i + k],
                                                    recv_sem=recv.at[3 * i + k], device_id=to, device_id_type=MESH))
    return out


def _gather_arrivals(lands, send, recv, sibling=False):
    x, y, c = _place()
    out = []
    for i, land in enumerate(lands):
        half = land.shape[1] // 2
        rows = pl.ds(((1 - c) if sibling else c) * half, half)
        for k, (px, py) in enumerate(_other_chips(x, y)):
            part = land.at[2 * px + py, rows]
            out.append(pltpu.make_async_remote_copy(src_ref=part, dst_ref=part, send_sem=send.at[3 * i + k],
                                                    recv_sem=recv.at[3 * i + k],
                                                    device_id=(x, y, 1 - c) if sibling else (px, py, c), device_id_type=MESH))
    return out


def _gather_start(shards, after, *, name):
    n = len(shards)

    def body(*refs):
        srcs, lands, send, recv, token = refs[:n], refs[n:2 * n], refs[2 * n], refs[2 * n + 1], refs[-1]
        for cp in _gather_copies(srcs, lands, send, recv):
            cp.start()
        token[...] = jnp.zeros_like(token)

    lands = [lax.empty((N_CHIPS,) + s.shape, s.dtype) for s in shards]
    sems = pltpu.SemaphoreType.DMA((3 * n,))
    body, xs, xa = _after(body, 2 * n, after)
    out = pl.pallas_call(
        body, name=name, in_specs=[HBM_SPEC] * (2 * n) + xs,
        out_specs=[SEM_SPEC, SEM_SPEC] + [HBM_SPEC] * (2 * n) + [VMEM_SPEC],
        out_shape=[sems, sems] + [pltpu.HBM(a.shape, a.dtype) for a in list(shards) + lands]
        + [jax.ShapeDtypeStruct((SUBLANES, LANES), F32)],
        input_output_aliases={i: 2 + i for i in range(2 * n)},
        compiler_params=pltpu.CompilerParams(has_side_effects=DATAFLOW),
    )(*[_in_hbm(a) for a in list(shards) + lands], *xa)
    return (out[0], out[1], out[2:2 + n], out[2 + n:2 + 2 * n]), out[-1]


def _gather_wait(handle, after, *, name):
    send_sems, recv_sems, srcs, lands = handle
    n = len(srcs)

    def body(*refs):
        srcs_, lands_, send, recv = refs[:n], refs[n:2 * n], refs[2 * n], refs[2 * n + 1]
        for cp in _gather_copies(srcs_, lands_, send, recv):
            cp.wait_send()
        for cp in _gather_arrivals(lands_, send, recv):
            cp.wait_recv()

    both = list(srcs) + list(lands)
    out = pl.pallas_call(
        body, name=name, in_specs=[HBM_SPEC] * (2 * n) + [SEM_SPEC, SEM_SPEC, ANY_SPEC], out_specs=[HBM_SPEC] * (2 * n),
        out_shape=[pltpu.HBM(a.shape, a.dtype) for a in both],
        input_output_aliases={i: i for i in range(2 * n)},
        compiler_params=pltpu.CompilerParams(has_side_effects=DATAFLOW),
    )(*both, send_sems, recv_sems, after)
    return out[n:]


def _fill_sibling(lands, *, name):
    n = len(lands)

    def body(*refs):
        ins, outs, send_sems, recv_sems = refs[:n], refs[n:2 * n], refs[2 * n], refs[2 * n + 1]
        cps = _gather_copies(ins, outs, send_sems, recv_sems, sibling=True)
        for cp in cps:
            cp.start()
        for cp in _gather_arrivals(outs, send_sems, recv_sems, sibling=True):
            cp.wait_recv()
        for cp in cps:
            cp.wait_send()

    return pl.pallas_call(
        body, name=name, in_specs=[HBM_SPEC] * n, out_specs=[HBM_SPEC] * n,
        out_shape=[jax.ShapeDtypeStruct(a.shape, a.dtype) for a in lands],
        scratch_shapes=[pltpu.SemaphoreType.DMA((3 * n,)), pltpu.SemaphoreType.DMA((3 * n,))],
        input_output_aliases={i: i for i in range(n)},
    )(*lands)


def _scatter_copies(src, land, send, recv):
    x, y, c = _place()
    return [pltpu.make_async_remote_copy(src_ref=src[i].at[2 * px + py], dst_ref=land[i].at[k], send_sem=send.at[3 * i + k],
                                         recv_sem=recv.at[3 * i + k], device_id=(px, py, c), device_id_type=MESH)
            for i in range(len(src)) for k, (px, py) in enumerate(_other_chips(x, y))]


def _scatter_start(pieces, *, name):
    n = len(pieces)

    def body(*refs):
        src, land, send, recv, token = refs[:n], refs[n:2 * n], refs[2 * n], refs[2 * n + 1], refs[-1]
        for cp in _scatter_copies(src, land, send, recv):
            cp.start()
        token[...] = jnp.zeros_like(token)

    lands = [lax.empty((3,) + p.shape[1:], p.dtype) for p in pieces]
    sems = pltpu.SemaphoreType.DMA((3 * n,))
    out = pl.pallas_call(
        body, name=name, in_specs=[HBM_SPEC] * (2 * n),
        out_specs=[SEM_SPEC, SEM_SPEC] + [HBM_SPEC] * (2 * n) + [VMEM_SPEC],
        out_shape=[sems, sems] + [pltpu.HBM(a.shape, a.dtype) for a in pieces + lands]
        + [jax.ShapeDtypeStruct((SUBLANES, LANES), F32)],
        input_output_aliases={i: 2 + i for i in range(2 * n)},
        compiler_params=pltpu.CompilerParams(has_side_effects=DATAFLOW),
    )(*[_in_hbm(a) for a in pieces + lands])
    return (out[0], out[1], out[2:2 + n], out[2 + n:2 + 2 * n]), out[-1]


def _scatter_wait(handle, after, *, name):
    send_sems, recv_sems, srcs, lands = handle
    n = len(srcs)

    def body(*refs):
        src, land, send, recv = refs[:n], refs[n:2 * n], refs[2 * n], refs[2 * n + 1]
        for cp in _scatter_copies(src, land, send, recv):
            cp.wait_send()
            cp.wait_recv()

    both = list(srcs) + list(lands)
    out = pl.pallas_call(
        body, name=name, in_specs=[HBM_SPEC] * (2 * n) + [SEM_SPEC, SEM_SPEC, ANY_SPEC], out_specs=[HBM_SPEC] * (2 * n),
        out_shape=[pltpu.HBM(a.shape, a.dtype) for a in both],
        input_output_aliases={i: i for i in range(2 * n)},
        compiler_params=pltpu.CompilerParams(has_side_effects=DATAFLOW),
    )(*both, send_sems, recv_sems, after)
    return out[n:]


def _to_bf16(x, *, name, after=None):
    T, Dm = x.shape
    tr = _tile(T, 512, 2 * SUBLANES)

    def body(x_ref, o_ref):
        o_ref[...] = x_ref[...].astype(BF16)

    blk = pl.BlockSpec((tr, Dm), lambda i: (i, 0))
    body, xs, xa = _after(body, 1, after)
    return pl.pallas_call(
        body, name=name, grid=(T // tr,), in_specs=[blk] + xs, out_specs=blk, out_shape=jax.ShapeDtypeStruct((T, Dm), BF16),
        compiler_params=_params(("parallel",)),
    )(x, *xa)


def _chip_sum(pieces, got, chip, *, name):
    _, R, Cc = pieces.shape
    tr = _tile(R, 256, SUBLANES)

    def body(chip_ref, a_ref, g_ref, o_ref):
        o_ref[...] = ((a_ref[...] + g_ref[0].astype(F32)) + g_ref[1].astype(F32)) + g_ref[2].astype(F32)

    return pl.pallas_call(
        body, name=name,
        grid_spec=pltpu.PrefetchScalarGridSpec(
            num_scalar_prefetch=1, grid=(R // tr,),
            in_specs=[pl.BlockSpec((None, tr, Cc), lambda i, ch: (ch[0], i, 0)),
                      pl.BlockSpec((3, tr, Cc), lambda i, ch: (0, i, 0))],
            out_specs=pl.BlockSpec((tr, Cc), lambda i, ch: (i, 0))),
        out_shape=jax.ShapeDtypeStruct((R, Cc), F32),
        compiler_params=_params(("parallel",)),
    )(chip, pieces, got)


PACK_COLS = 1024
SMALL_ROWS = 32


def kernel(x, hgrn_w_in, hgrn_lb_logits, hgrn_gnorm_w, hgrn_w_out, swa_w_q, swa_sinks, swa_w_out, shared_w_kv, rel_bias, ffn_w_in, ffn_conv_w, ffn_conv_b, ffn_w_out, ln_mix_g, ln_mix_b, ln_ffn_g, ln_ffn_b, loss_target, m_hgrn_w_in, m_hgrn_lb_logits, m_hgrn_gnorm_w, m_hgrn_w_out, m_swa_w_q, m_swa_sinks, m_swa_w_out, m_shared_w_kv, m_rel_bias, m_ffn_w_in, m_ffn_conv_w, m_ffn_conv_b, m_ffn_w_out, m_ln_mix_g, m_ln_mix_b, m_ln_ffn_g, m_ln_ffn_b, v_hgrn_w_in, v_hgrn_lb_logits, v_hgrn_gnorm_w, v_hgrn_w_out, v_swa_w_q, v_swa_sinks, v_swa_w_out, v_shared_w_kv, v_rel_bias, v_ffn_w_in, v_ffn_conv_w, v_ffn_conv_b, v_ffn_w_out, v_ln_mix_g, v_ln_mix_b, v_ln_ffn_g, v_ln_ffn_b):
    xi, yi, ci = _place()
    chip = 2 * xi + yi
    Dm = D_MODEL
    FC = 2 * FFN_DIM // N_CHIPS
    Fo = FFN_DIM // N_CHIPS
    Dq = Dm // N_CHIPS
    bf = lambda a: a.astype(BF16)

    small = jnp.concatenate([hgrn_lb_logits.reshape(-1), ffn_conv_w.reshape(-1)])
    n_small = small.shape[0]
    bits = jnp.concatenate(_split3(small))
    bits = jnp.pad(bits, (0, SMALL_ROWS * PACK_COLS - 3 * n_small)).reshape(SMALL_ROWS, PACK_COLS)
    groups = [[bf(hgrn_w_in[0]), bf(hgrn_w_out[0]), bits],
              [bf(swa_w_q[0]), bf(swa_w_out[0]), bf(shared_w_kv), bf(ffn_w_in[0]), bf(ffn_w_out[0])],
              [bf(ffn_w_in[1]), bf(ffn_w_out[1])]]

    def gathered(k, landed):
        lands = _fill_sibling(landed, name=f"gather_w{k}_fill")
        return [lax.dynamic_update_slice(land, shard[None], (chip,) + (0,) * shard.ndim)
                for land, shard in zip(lands, groups[k])], lands

    handle0, token0 = _gather_start(groups[0], None, name="gather_w0_start")
    xb = _to_bf16(x[0], name="x_to_bf16", after=token0)
    corner = lambda a: a[:2 * SUBLANES, :LANES]
    casts_done = corner(xb) + sum(corner(a) for a in groups[1] + groups[2])
    (w_in, w_hg_out, small_all), lands0 = gathered(0, _gather_wait(handle0, casts_done, name="gather_w0_wait"))
    handle1, token1 = _gather_start(groups[1], lands0[0], name="gather_w1_start")
    parts = small_all.reshape(N_CHIPS, -1)[:, :3 * n_small].reshape(N_CHIPS, 3, n_small).astype(F32)
    vals = (parts[:, 0] + parts[:, 1]) + parts[:, 2]
    lb_full = vals[:, :2 * Dq].reshape(N_CHIPS, 2, Dq).transpose(1, 0, 2).reshape(2, Dm)
    cw_full = vals[:, 2 * Dq:].reshape(N_CHIPS, DEPTH, 3, FC).transpose(1, 2, 0, 3).reshape(DEPTH, 3, 2 * FFN_DIM)

    got = {"handle": handle1}

    def more_weights(k, after):
        ws, lands = gathered(k, _gather_wait(got.pop("handle"), after, name=f"gather_w{k}_wait"))
        if k == 1:
            got["handle"], token2 = _gather_start(groups[2], lands[0], name="gather_w2_start")
            w_q, w_o, w_kv, w_fi, w_fo = ws
            got.update(ffn_in={0: w_fi}, ffn_out={0: w_fo.reshape(FFN_DIM, Dm)})
            return {"sw_q": w_q.reshape(Dm, Dm), "sw_out": w_o.reshape(Dm, Dm), "kv": w_kv.reshape(Dm, 2 * KV_DIM),
                    "token": token2, "ffn_in": got["ffn_in"], "ffn_out": got["ffn_out"]}
        w_fi, w_fo = ws
        return {"ffn_in": {**got["ffn_in"], 1: w_fi}, "ffn_out": {**got["ffn_out"], 1: w_fo.reshape(FFN_DIM, Dm)}}

    w = {
        "hg_in": w_in, "hg_out": w_hg_out.reshape(Dm, Dm), "token": token1,
        "lb_logits": lb_full, "gnorm": hgrn_gnorm_w, "sinks": swa_sinks, "rel_bias": rel_bias,
        "conv_w_a": [cw_full[l, :, :FFN_DIM] for l in range(DEPTH)],
        "conv_w_b": [cw_full[l, :, FFN_DIM:] for l in range(DEPTH)],
        "conv_b_a": [ffn_conv_b[l:l + 1, :FFN_DIM] for l in range(DEPTH)],
        "conv_b_b": [ffn_conv_b[l:l + 1, FFN_DIM:] for l in range(DEPTH)],
        "ln_mix_g": [ln_mix_g[l:l + 1] for l in range(DEPTH)], "ln_mix_b": [ln_mix_b[l:l + 1] for l in range(DEPTH)],
        "ln_ffn_g": [ln_ffn_g[l:l + 1] for l in range(DEPTH)], "ln_ffn_b": [ln_ffn_b[l:l + 1] for l in range(DEPTH)],
    }

    sent = {}

    def emit(k, gd):
        rows4 = lambda a: a.reshape(N_CHIPS, -1, a.shape[-1])
        order = {1: ["sw_q", "sw_out", "kv", "ffn_in", "ffn_out"], 2: ["ffn_in", "ffn_out", "hg_out"], 3: ["hg_in"]}[k]
        as_pieces = lambda a, nme: a if nme in ("ffn_in", "hg_in") else rows4(a)
        handle, token = _scatter_start([as_pieces(gd[nme][1], nme) for nme in order], name=f"scatter_g{k}_start")
        sent[k] = (handle, [as_pieces(gd[nme][0], nme) for nme in order])
        return token

    loss_tile, grad_x, g = _local_step(x[0], xb, loss_target[0], w, more_weights, emit)

    wts = dict(hgrn_w_in=hgrn_w_in, hgrn_lb_logits=hgrn_lb_logits, hgrn_gnorm_w=hgrn_gnorm_w, hgrn_w_out=hgrn_w_out,
               swa_w_q=swa_w_q, swa_sinks=swa_sinks, swa_w_out=swa_w_out, shared_w_kv=shared_w_kv, rel_bias=rel_bias,
               ffn_w_in=ffn_w_in, ffn_conv_w=ffn_conv_w, ffn_conv_b=ffn_conv_b, ffn_w_out=ffn_w_out,
               ln_mix_g=ln_mix_g, ln_mix_b=ln_mix_b, ln_ffn_g=ln_ffn_g, ln_ffn_b=ln_ffn_b)
    ms = dict(hgrn_w_in=m_hgrn_w_in, hgrn_lb_logits=m_hgrn_lb_logits, hgrn_gnorm_w=m_hgrn_gnorm_w, hgrn_w_out=m_hgrn_w_out,
              swa_w_q=m_swa_w_q, swa_sinks=m_swa_sinks, swa_w_out=m_swa_w_out, shared_w_kv=m_shared_w_kv, rel_bias=m_rel_bias,
              ffn_w_in=m_ffn_w_in, ffn_conv_w=m_ffn_conv_w, ffn_conv_b=m_ffn_conv_b, ffn_w_out=m_ffn_w_out,
              ln_mix_g=m_ln_mix_g, ln_mix_b=m_ln_mix_b, ln_ffn_g=m_ln_ffn_g, ln_ffn_b=m_ln_ffn_b)
    vs = dict(hgrn_w_in=v_hgrn_w_in, hgrn_lb_logits=v_hgrn_lb_logits, hgrn_gnorm_w=v_hgrn_gnorm_w, hgrn_w_out=v_hgrn_w_out,
              swa_w_q=v_swa_w_q, swa_sinks=v_swa_sinks, swa_w_out=v_swa_w_out, shared_w_kv=v_shared_w_kv, rel_bias=v_rel_bias,
              ffn_w_in=v_ffn_w_in, ffn_conv_w=v_ffn_conv_w, ffn_conv_b=v_ffn_conv_b, ffn_w_out=v_ffn_w_out,
              ln_mix_g=v_ln_mix_g, ln_mix_b=v_ln_mix_b, ln_ffn_g=v_ln_ffn_g, ln_ffn_b=v_ln_ffn_b)
    names = list(wts)
    grads, delta, new_m, new_v = {}, {}, {}, {}

    def update(n, ga, gb, layer=None, prev=None):
        r2 = lambda a: a.reshape(-1, a.shape[-1])
        rows = None if layer is None else (layer * ga.shape[0], ga.shape[0])
        return _adamw(r2(wts[n]), ga, gb, r2(ms[n]), r2(vs[n]), rows=rows, prev=prev,
                      name=f"adamw_{n}" + ("" if layer is None else f"_{layer}"))

    def keep(n, res):
        grads[n], delta[n], new_m[n], new_v[n] = [a.reshape(wts[n].shape) for a in res]

    chip1 = jnp.reshape(chip, (1,)).astype(jnp.int32)
    after = grad_x
    for k in (1, 2, 3):
        handle, pieces = sent[k]
        lands = _scatter_wait(handle, after, name=f"scatter_g{k}_wait")
        parts = [_chip_sum(p, l, chip1, name=f"scatter_g{k}_sum{i}") for i, (p, l) in enumerate(zip(pieces, lands))]
        sibs = _swap_sibling(parts, name=f"scatter_g{k}_swap")
        if k == 1:
            for n, ga, gb in zip(["swa_w_q", "swa_w_out", "shared_w_kv"], parts[:3], sibs[:3]):
                keep(n, update(n, ga, gb))
            ffn_in_1 = update("ffn_w_in", parts[3], sibs[3], layer=1)
            ffn_out_1 = update("ffn_w_out", parts[4], sibs[4], layer=1)
            after = ffn_out_1[3]
        elif k == 2:
            keep("ffn_w_in", update("ffn_w_in", parts[0], sibs[0], layer=0, prev=ffn_in_1))
            keep("ffn_w_out", update("ffn_w_out", parts[1], sibs[1], layer=0, prev=ffn_out_1))
            keep("hgrn_w_out", update("hgrn_w_out", parts[2], sibs[2]))
            after = new_v["hgrn_w_out"]
        else:
            keep("hgrn_w_in", update("hgrn_w_in", parts[0], sibs[0]))

    small_keys = ["lb_logits", "gnorm", "sinks", "rel_bias", "conv0", "conv1", "ln_mix0", "ln_mix1", "ln_ffn0", "ln_ffn1"]
    flat2 = lambda a: a.reshape(-1, a.shape[-1])
    sums = _sum8([loss_tile] + [flat2(g[k]) for k in small_keys], name="sum_small")
    loss = sums[0][0, 0]
    sg = {k: v.reshape(g[k].shape) for k, v in zip(small_keys, sums[1:])}
    conv = [sg["conv0"], sg["conv1"]]
    g_cw = jnp.stack([jnp.concatenate([conv[l][0, :3], conv[l][1, :3]], axis=1) for l in range(DEPTH)])
    g_cb = jnp.stack([jnp.concatenate([conv[l][0, 3], conv[l][1, 3]], axis=0) for l in range(DEPTH)])
    ln = lambda nme, r: jnp.stack([sg[nme + "0"][r], sg[nme + "1"][r]])
    small_g = dict(hgrn_lb_logits=lax.dynamic_slice_in_dim(sg["lb_logits"], chip * Dq, Dq, axis=1),
                   hgrn_gnorm_w=sg["gnorm"], swa_sinks=sg["sinks"], rel_bias=sg["rel_bias"],
                   ffn_conv_w=lax.dynamic_slice_in_dim(g_cw, chip * FC, FC, axis=2), ffn_conv_b=g_cb,
                   ln_mix_g=ln("ln_mix", 0), ln_mix_b=ln("ln_mix", 1), ln_ffn_g=ln("ln_ffn", 0), ln_ffn_b=ln("ln_ffn", 1))
    small_names = list(small_g)
    d_, m_, v_ = _adamw_small([flat2(wts[n]) for n in small_names], [flat2(small_g[n]) for n in small_names],
                              [flat2(ms[n]) for n in small_names], [flat2(vs[n]) for n in small_names], name="adamw_small")
    for n, a, b_, c_ in zip(small_names, d_, m_, v_):
        shp = wts[n].shape
        grads[n], delta[n], new_m[n], new_v[n] = small_g[n], a.reshape(shp), b_.reshape(shp), c_.reshape(shp)

    return (loss, grad_x[None], *[grads[n] for n in names], *[delta[n] for n in names],
            *[new_m[n] for n in names], *[new_v[n] for n in names])
```

```python
import math

import numpy as np
import jax
import jax.numpy as jnp
from jax import lax
from jax.experimental import pallas as pl
from jax.experimental.pallas import tpu as pltpu

F32 = jnp.float32
BF16 = jnp.bfloat16
MESH = pl.DeviceIdType.MESH

D_MODEL = 1024
DEPTH = 2
HG_HEADS = 8
HG_DIM = 128
SW_Q_HEADS = 16
SW_KV_HEADS = 4
SW_HEAD_DIM = 64
SW_GROUP = 4
SW_WINDOW = 128
REL_BUCKETS = 32
REL_MAX_DIST = 128
FFN_DIM = 2816
ALPHA = (2.0 * DEPTH) ** 0.25
LN_EPS = 1e-5
RMS_EPS = 1e-6
ADAM_LR = 0.001
ADAM_B1 = 0.9
ADAM_B2 = 0.999
ADAM_EPS = 1e-08
ADAM_WD = 0.01
ADAM_STEP = 10

VMEM_BYTES_V7X = 64 * 1024 * 1024
VMEM_LIMIT = VMEM_BYTES_V7X - 8 * 1024 * 1024
LANES = 128
SUBLANES = 8

HG_C = 64
HG_RB = 256
CONV_R = 128
N_CHIPS = 4
N_DEV = 8

ANY_SPEC = pl.BlockSpec(memory_space=pl.ANY)


def _after(body, n_in, after):
    if after is None:
        return body, [], ()

    def wrapped(*refs):
        return body(*refs[:n_in], *refs[n_in + 1:])

    return wrapped, [ANY_SPEC], (after,)


def _params(sem=None):
    return pltpu.CompilerParams(dimension_semantics=sem, vmem_limit_bytes=VMEM_LIMIT)


def _tile(n, pref, unit=LANES):
    if n <= pref:
        return n
    best = None
    for t in range(unit, pref + 1, unit):
        if n % t == 0:
            best = t
    assert best is not None, (n, pref, unit)
    return best


def _dot(a, b, ca, cb):
    nb = a.ndim - 2
    batch = tuple(range(nb))
    return lax.dot_general(a.astype(BF16), b.astype(BF16), (((nb + ca,), (nb + cb,)), (batch, batch)),
                           preferred_element_type=F32)


@jax.custom_vjp
def mm(a, b):
    return _dot(a, b, 1, 0)


@jax.custom_vjp
def mm_nt(a, b):
    return _dot(a, b, 1, 1)


@jax.custom_vjp
def mm_tn(a, b):
    return _dot(a, b, 0, 0)


mm.defvjp(lambda a, b: (mm(a, b), (a, b)), lambda r, ct: (mm_nt(ct, r[1]), mm_tn(r[0], ct)))
mm_nt.defvjp(lambda a, b: (mm_nt(a, b), (a, b)), lambda r, ct: (mm(ct, r[1]), mm_tn(ct, r[0])))
mm_tn.defvjp(lambda a, b: (mm_tn(a, b), (a, b)), lambda r, ct: (mm_nt(r[1], ct), mm(r[0], ct)))


def _split2(x):
    hi = x.astype(BF16)
    return hi, (x - hi.astype(F32)).astype(BF16)


@jax.custom_vjp
def _scores(qt, kt):
    return _dot(qt, kt, 1, 1)


def _scores_bwd(r, ct):
    (qh, ql), (kh, kl) = _split2(r[0]), _split2(r[1])
    return _dot(ct, kh, 1, 0) + _dot(ct, kl, 1, 0), _dot(ct, qh, 0, 0) + _dot(ct, ql, 0, 0)


_scores.defvjp(lambda a, b: (_scores(a, b), (a, b)), _scores_bwd)


def _split3(x):
    hi = x.astype(BF16)
    r1 = x - hi.astype(F32)
    mid = r1.astype(BF16)
    lo = (r1 - mid.astype(F32)).astype(BF16)
    return hi, mid, lo


def _cumsum_impl(x):
    ax = x.ndim - 2
    n = x.shape[ax]
    row = lax.broadcasted_iota(jnp.int32, x.shape, ax)
    d = 1
    while d < n:
        x = x + jnp.where(row >= d, pltpu.roll(x, d, ax), 0.0)
        d *= 2
    return x


def _cumsum_rev_impl(x):
    ax = x.ndim - 2
    n = x.shape[ax]
    row = lax.broadcasted_iota(jnp.int32, x.shape, ax)
    d = 1
    while d < n:
        x = x + jnp.where(row < n - d, pltpu.roll(x, n - d, ax), 0.0)
        d *= 2
    return x


@jax.custom_vjp
def _cumsum(x):
    return _cumsum_impl(x)


_cumsum.defvjp(lambda x: (_cumsum_impl(x), None), lambda _, ct: (_cumsum_rev_impl(ct),))


def _matmul(a, b, *, mode, name, out_dtype=F32, add=None, add_scale=1.0, tm=512, tn=1408, tk=1408, after=None,
            split_n=False, planes=None, also_bf16=False):
    P = b.shape[0] if planes else 1
    a2, b2 = a.shape[-2:], b.shape[-2:]
    (M, K) = a2 if mode[0] == "n" else a2[::-1]
    (K2, N) = b2 if mode[1] == "n" else b2[::-1]
    assert K == K2, (a.shape, b.shape, mode)
    assert a.ndim == (3 if planes == "k" else 2) and b.ndim == (3 if planes else 2)
    tm, tn, tk = _tile(M, tm), _tile(N, tn), _tile(K, tk)
    nj, nkp = N // tn, K // tk
    nk = nkp * (P if planes == "k" else 1)
    ca, cb = (1 if mode[0] == "n" else 0), (0 if mode[1] == "n" else 1)
    a_blk, a_idx = ((tk, tm), lambda i, k: (k, i)) if mode[0] == "t" else ((tm, tk), lambda i, k: (i, k))
    b_blk, b_idx = ((tn, tk), lambda k, j: (j, k)) if mode[1] == "t" else ((tk, tn), lambda k, j: (k, j))
    if planes == "k":
        a_spec = pl.BlockSpec((None,) + a_blk, lambda i, j, k: (k // nkp,) + a_idx(i, k % nkp))
        b_spec = pl.BlockSpec((None,) + b_blk, lambda i, j, k: (k // nkp,) + b_idx(k % nkp, j))
    else:
        a_spec = pl.BlockSpec(a_blk, lambda i, j, k: a_idx(i, k))
        b_spec = (pl.BlockSpec((None,) + b_blk, lambda i, j, k: (j // nj,) + b_idx(k, j % nj)) if planes == "n"
                  else pl.BlockSpec(b_blk, lambda i, j, k: b_idx(k, j)))
    if split_n:
        o_spec, out_shape = pl.BlockSpec((None, tm, tn), lambda i, j, k: (j, i, 0)), (P * nj if planes == "n" else nj, M, tn)
    elif planes == "n":
        o_spec, out_shape = pl.BlockSpec((None, tm, tn), lambda i, j, k: (j // nj, i, j % nj)), (P, M, N)
    else:
        o_spec, out_shape = pl.BlockSpec((tm, tn), lambda i, j, k: (i, j)), (M, N)
    has_add = add is not None
    assert not (has_add and (split_n or planes == "n"))

    def finish(r, add_ref, o_refs):
        if has_add:
            r = r + add_scale * add_ref[...]
        o_refs[0][...] = r.astype(out_dtype)
        if also_bf16:
            o_refs[1][...] = r.astype(BF16)

    def body(*refs):
        a_ref, b_ref = refs[:2]
        add_ref = refs[2] if has_add else None
        first = 3 if has_add else 2
        o_ref = refs[first:first + (2 if also_bf16 else 1)]
        if nk == 1:
            finish(_dot(a_ref[...], b_ref[...], ca, cb), add_ref, o_ref)
            return
        acc_ref = refs[-1]
        k = pl.program_id(2)

        @pl.when(k == 0)
        def _():
            acc_ref[...] = jnp.zeros_like(acc_ref)

        acc_ref[...] += _dot(a_ref[...], b_ref[...], ca, cb)

        @pl.when(k == nk - 1)
        def _():
            finish(acc_ref[...], add_ref, o_ref)

    in_specs = [a_spec, b_spec] + ([o_spec] if has_add else [])
    args = (a, b) + ((add,) if has_add else ())
    body, xs, xa = _after(body, len(args), after)
    in_specs, args = in_specs + xs, args + xa
    out_shapes = [jax.ShapeDtypeStruct(out_shape, out_dtype)] + ([jax.ShapeDtypeStruct(out_shape, BF16)] if also_bf16 else [])
    out = pl.pallas_call(
        body, name=name, grid=(M // tm, nj * (P if planes == "n" else 1), nk), in_specs=in_specs,
        out_specs=[o_spec] * len(out_shapes), out_shape=out_shapes,
        scratch_shapes=[pltpu.VMEM((tm, tn), F32)] if nk > 1 else [],
        compiler_params=_params(("parallel", "parallel", "arbitrary")),
    )(*args)
    return tuple(out) if also_bf16 else out[0]


def _matmul_planes_nt(a, b, add, *, add_scale, name, tm=512, after=None):
    (P, M, K), (P2, N, K2) = a.shape, b.shape
    assert P == P2 and K == K2 and add.shape == (M, N)
    tm = _tile(M, tm, SUBLANES)

    def body(a_ref, b_ref, add_ref, o_ref):
        r = add_scale * add_ref[...]
        for p in range(P):
            r = r + _dot(a_ref[p], b_ref[p], 1, 1)
        o_ref[...] = r

    row = pl.BlockSpec((tm, N), lambda i: (i, 0))
    body, xs, xa = _after(body, 3, after)
    return pl.pallas_call(
        body, name=name, grid=(M // tm,),
        in_specs=[pl.BlockSpec((P, tm, K), lambda i: (0, i, 0)), pl.BlockSpec((P, N, K), lambda i: (0, 0, 0)), row] + xs,
        out_specs=row, out_shape=jax.ShapeDtypeStruct((M, N), F32),
        compiler_params=_params(("parallel",)),
    )(a, b, add, *xa)


def _ln(z, g, b):
    mu = jnp.mean(z, axis=-1, keepdims=True)
    zc = z - mu
    var = jnp.mean(zc * zc, axis=-1, keepdims=True)
    return zc * lax.rsqrt(var + LN_EPS) * g + b


def _matmul_ln(a, b, h, g, bias, *, name, tgt=None, tm=512, a_t=False):
    (T, K), (K2, Dm) = (a.shape[::-1] if a_t else a.shape), b.shape
    assert K == K2 and h.shape == (T, Dm)
    tm = _tile(T, tm, SUBLANES)
    last = tgt is not None

    def body(*refs):
        a_ref, b_ref, h_ref, g_ref, bias_ref = refs[:5]
        z = ALPHA * h_ref[...] + _dot(a_ref[...], b_ref[...], 0 if a_t else 1, 0)
        if not last:
            z_ref, y_ref, yb_ref = refs[5:]
            y = _ln(z, g_ref[...], bias_ref[...])
            z_ref[...] = z
            y_ref[...] = y
            yb_ref[...] = y.astype(BF16)
            return
        t_ref, dz_ref, dzb_ref, dgb_ref, l_ref, da_ref = refs[5:]

        @pl.when(pl.program_id(0) == 0)
        def _():
            dgb_ref[...] = jnp.zeros_like(dgb_ref)
            l_ref[...] = jnp.zeros_like(l_ref)

        y, vjp = jax.vjp(_ln, z, g_ref[...], bias_ref[...])
        e = y - t_ref[...]
        dz, dg, db = vjp(e * (1.0 / Dm))
        l_ref[...] += 0.5 * jnp.sum(jnp.mean(e * e, axis=-1, keepdims=True), axis=0, keepdims=True)
        dzb = dz.astype(BF16)
        dz_ref[...] = dz
        dzb_ref[...] = dzb
        dgb_ref[...] += jnp.concatenate([dg, db], axis=0)
        da_ref[...] = _dot(dzb, b_ref[...], 1, 1).astype(BF16)

    row = pl.BlockSpec((tm, Dm), lambda i: (i, 0))
    vec = pl.BlockSpec((1, Dm), lambda i: (0, 0))
    a_spec = pl.BlockSpec((K, tm), lambda i: (0, i)) if a_t else pl.BlockSpec((tm, K), lambda i: (i, 0))
    in_specs = [a_spec, pl.BlockSpec((K, Dm), lambda i: (0, 0)), row, vec, vec]
    f32, b16 = jax.ShapeDtypeStruct((T, Dm), F32), jax.ShapeDtypeStruct((T, Dm), BF16)
    if not last:
        return pl.pallas_call(
            body, name=name, grid=(T // tm,), in_specs=in_specs, out_specs=[row, row, row], out_shape=[f32, f32, b16],
            compiler_params=_params(("parallel",)),
        )(a, b, h, g, bias)
    assert not a_t
    return pl.pallas_call(
        body, name=name, grid=(T // tm,), in_specs=in_specs + [row],
        out_specs=[row, row, pl.BlockSpec((2, Dm), lambda i: (0, 0)), pl.BlockSpec((SUBLANES, LANES), lambda i: (0, 0)), a_spec],
        out_shape=[f32, b16, jax.ShapeDtypeStruct((2, Dm), F32), jax.ShapeDtypeStruct((SUBLANES, LANES), F32),
                   jax.ShapeDtypeStruct((T, K), BF16)],
        compiler_params=_params(("arbitrary",)),
    )(a, b, h, g, bias, tgt)


def _ln_bwd_matmul(dy, z, g, b, w, *, name, out_t=False, tm=512, after=None):
    T, Dm = z.shape
    N = w.shape[0]
    tm = _tile(T, tm, LANES if out_t else SUBLANES)

    def body(dy_ref, z_ref, g_ref, b_ref, w_ref, dz_ref, dzb_ref, dgb_ref, o_ref):
        @pl.when(pl.program_id(0) == 0)
        def _():
            dgb_ref[...] = jnp.zeros_like(dgb_ref)

        _, vjp = jax.vjp(_ln, z_ref[...], g_ref[...], b_ref[...])
        dz, dg, db = vjp(dy_ref[...])
        dzb = dz.astype(BF16)
        dz_ref[...] = dz
        dzb_ref[...] = dzb
        dgb_ref[...] += jnp.concatenate([dg, db], axis=0)
        o_ref[...] = (_dot(w_ref[...], dzb, 1, 1) if out_t else _dot(dzb, w_ref[...], 1, 1)).astype(BF16)

    row = pl.BlockSpec((tm, Dm), lambda i: (i, 0))
    vec = pl.BlockSpec((1, Dm), lambda i: (0, 0))
    o_spec = pl.BlockSpec((N, tm), lambda i: (0, i)) if out_t else pl.BlockSpec((tm, N), lambda i: (i, 0))
    body, xs, xa = _after(body, 5, after)
    return pl.pallas_call(
        body, name=name, grid=(T // tm,), in_specs=[row, row, vec, vec, pl.BlockSpec((N, Dm), lambda i: (0, 0))] + xs,
        out_specs=[row, row, pl.BlockSpec((2, Dm), lambda i: (0, 0)), o_spec],
        out_shape=[jax.ShapeDtypeStruct((T, Dm), F32), jax.ShapeDtypeStruct((T, Dm), BF16),
                   jax.ShapeDtypeStruct((2, Dm), F32), jax.ShapeDtypeStruct((N, T) if out_t else (T, N), BF16)],
        compiler_params=_params(("arbitrary",)),
    )(dy, z, g, b, w, *xa)


def _hg_chunk(qr, fr, ir, gr, l0, l1, gw, st):
    C = qr.shape[-2]
    row = lax.broadcasted_iota(jnp.int32, qr.shape, qr.ndim - 2)
    lb = jax.nn.sigmoid(l0 - l1)
    fg = lb + (1.0 - lb) * jax.nn.sigmoid(fr)
    b = _cumsum(jnp.log(fg))
    q = jax.nn.silu(qr)
    k = 1.0 - fg
    bmid = lax.stop_gradient(jnp.sum(jnp.where(row == C // 2 - 1, b, 0.0), axis=-2, keepdims=True))
    bl = jnp.sum(jnp.where(row == C - 1, b, 0.0), axis=-2, keepdims=True)
    o = mm_nt(q * jnp.exp(b), st)
    sc = _scores(q * jnp.exp(b - bmid), k * jnp.exp(bmid - b))
    ti = lax.broadcasted_iota(jnp.int32, (C, C), 0)
    si = lax.broadcasted_iota(jnp.int32, (C, C), 1)
    sc = jnp.where(si <= ti, sc, 0.0)
    o = o + mm(sc, ir)
    st_new = st * jnp.exp(bl) + mm_tn(ir, k * jnp.exp(bl - b))
    on = o * lax.rsqrt(jnp.mean(o * o, axis=-1, keepdims=True) + RMS_EPS)
    return on * gw * jax.nn.silu(gr), st_new


def _heads(ref, rows):
    return jnp.stack([ref[rows, h * HG_DIM:(h + 1) * HG_DIM].astype(F32) for h in range(HG_HEADS)])


def _unheads(x):
    return jnp.concatenate([x[h] for h in range(HG_HEADS)], axis=-1)


def _hgrn_fwd(pre, lbl, gw, *, name):
    _, T, Dm = pre.shape
    rb = min(HG_RB, T)
    C = min(HG_C, rb)
    ncb = rb // C

    def body(pre_ref, lbl_ref, gw_ref, o_ref, st_ref, s_ref):
        @pl.when(pl.program_id(0) == 0)
        def _():
            s_ref[...] = jnp.zeros_like(s_ref)

        def chunk(ci, carry):
            r0 = pl.multiple_of(ci * C, C)
            rows = pl.ds(r0, C)
            st = s_ref[...]
            st_ref[ci] = st
            out, st_new = _hg_chunk(*[_heads(pre_ref.at[j], rows) for j in range(4)],
                                    _heads(lbl_ref, slice(0, 1)), _heads(lbl_ref, slice(1, 2)), gw_ref[...], st)
            o_ref[rows, :] = _unheads(out).astype(BF16)
            s_ref[...] = st_new
            return carry

        lax.fori_loop(0, ncb, chunk, 0, unroll=True)

    row = pl.BlockSpec((rb, Dm), lambda n: (n, 0))
    return pl.pallas_call(
        body, name=name, grid=(T // rb,),
        in_specs=[pl.BlockSpec((4, rb, Dm), lambda n: (0, n, 0)), pl.BlockSpec((2, Dm), lambda n: (0, 0)),
                  pl.BlockSpec((1, HG_DIM), lambda n: (0, 0))],
        out_specs=[row, pl.BlockSpec((ncb, HG_HEADS, HG_DIM, HG_DIM), lambda n: (n, 0, 0, 0))],
        out_shape=[jax.ShapeDtypeStruct((T, Dm), BF16),
                   jax.ShapeDtypeStruct((T // C, HG_HEADS, HG_DIM, HG_DIM), F32)],
        scratch_shapes=[pltpu.VMEM((HG_HEADS, HG_DIM, HG_DIM), F32)],
        compiler_params=_params(("arbitrary",)),
    )(pre, lbl, gw)


def _hgrn_bwd(pre, lbl, gw, states, dout, *, name, after=None):
    _, T, Dm = pre.shape
    rb = min(HG_RB, T)
    C = min(HG_C, rb)
    ncb = rb // C
    nb = T // rb

    def body(pre_ref, lbl_ref, gw_ref, st_ref, do_ref, dpre_ref, dlbl_ref, dgw_ref, ds_ref):
        @pl.when(pl.program_id(0) == 0)
        def _():
            ds_ref[...] = jnp.zeros_like(ds_ref)
            dlbl_ref[...] = jnp.zeros_like(dlbl_ref)
            dgw_ref[...] = jnp.zeros_like(dgw_ref)

        def chunk(cj, carry):
            ci = ncb - 1 - cj
            r0 = pl.multiple_of(ci * C, C)
            rows = pl.ds(r0, C)
            _, vjp = jax.vjp(_hg_chunk, *[_heads(pre_ref.at[j], rows) for j in range(4)],
                             _heads(lbl_ref, slice(0, 1)), _heads(lbl_ref, slice(1, 2)), gw_ref[...], st_ref[ci])
            *dpre, dl0, dl1, dgw, dst = vjp((_heads(do_ref, rows), ds_ref[...]))
            for j in range(4):
                dpre_ref[j, rows, :] = _unheads(dpre[j]).astype(BF16)
            dlbl_ref[0:1, :] += _unheads(dl0)
            dlbl_ref[1:2, :] += _unheads(dl1)
            dgw_ref[...] += dgw
            ds_ref[...] = dst
            return carry

        lax.fori_loop(0, ncb, chunk, 0, unroll=True)

    row = pl.BlockSpec((rb, Dm), lambda n: (nb - 1 - n, 0))
    lsp = pl.BlockSpec((2, Dm), lambda n: (0, 0))
    gsp = pl.BlockSpec((1, HG_DIM), lambda n: (0, 0))
    pre_spec = pl.BlockSpec((4, rb, Dm), lambda n: (0, nb - 1 - n, 0))
    body, xs, xa = _after(body, 5, after)
    return pl.pallas_call(
        body, name=name, grid=(nb,),
        in_specs=[pre_spec, lsp, gsp, pl.BlockSpec((ncb, HG_HEADS, HG_DIM, HG_DIM), lambda n: (nb - 1 - n, 0, 0, 0)), row] + xs,
        out_specs=[pre_spec, lsp, gsp],
        out_shape=[jax.ShapeDtypeStruct((4, T, Dm), BF16), jax.ShapeDtypeStruct((2, Dm), F32),
                   jax.ShapeDtypeStruct((1, HG_DIM), F32)],
        scratch_shapes=[pltpu.VMEM((HG_HEADS, HG_DIM, HG_DIM), F32)],
        compiler_params=_params(("arbitrary",)),
    )(pre, lbl, gw, states, dout, *xa)


CONV_HALO = 2 * SUBLANES


def _conv_rows(u_ref, scr, w, bias, r0, R):
    cur = u_ref[pl.ds(r0, R), :].astype(F32)
    p0 = pl.multiple_of(jnp.maximum(r0 - CONV_HALO, 0), CONV_HALO)
    scr[0:CONV_HALO, :] = jnp.where(r0 > 0, u_ref[pl.ds(p0, CONV_HALO), :].astype(F32), 0.0)
    scr[CONV_HALO:CONV_HALO + R, :] = cur
    s1 = scr[CONV_HALO - 1:CONV_HALO - 1 + R, :]
    s2 = scr[CONV_HALO - 2:CONV_HALO - 2 + R, :]
    return w[0:1, :] * s2 + w[1:2, :] * s1 + w[2:3, :] * cur + bias, cur, s1, s2


def _halves_spec(T, Fd):
    per = Fd // 2 // LANES
    return pl.BlockSpec((2, None, T, LANES), lambda j: (0, j // per, 0, j % per))


def _conv_gate_fwd(u, wa, wb, ba, bb, *, name):
    T, Fd = u.shape[2], 2 * u.shape[3]
    R = min(CONV_R, T)
    tc = LANES

    def body(u_ref, wa_ref, wb_ref, ba_ref, bb_ref, o_ref, sa, sb):
        wa_, wb_, ba_, bb_ = wa_ref[...], wb_ref[...], ba_ref[...], bb_ref[...]

        def step(ri, carry):
            r0 = pl.multiple_of(ri * R, R)
            ca = _conv_rows(u_ref.at[0], sa, wa_, ba_, r0, R)[0]
            cb = _conv_rows(u_ref.at[1], sb, wb_, bb_, r0, R)[0]
            o_ref[pl.ds(r0, R), :] = (jax.nn.silu(ca) * cb).astype(BF16)
            return carry

        lax.fori_loop(0, T // R, step, 0)

    col = pl.BlockSpec((T, tc), lambda j: (0, j))
    wsp = pl.BlockSpec((3, tc), lambda j: (0, j))
    bsp = pl.BlockSpec((1, tc), lambda j: (0, j))
    both = _halves_spec(T, Fd)
    return pl.pallas_call(
        body, name=name, grid=(Fd // tc,), in_specs=[both, wsp, wsp, bsp, bsp], out_specs=col,
        out_shape=jax.ShapeDtypeStruct((T, Fd), BF16),
        scratch_shapes=[pltpu.VMEM((CONV_HALO + R, tc), F32)] * 2,
        compiler_params=_params(("parallel",)),
    )(u, wa, wb, ba, bb)


def _conv_gate_bwd(u, wa, wb, ba, bb, dact, *, name):
    T, Fd = u.shape[2], 2 * u.shape[3]
    R = min(CONV_R, T)
    nr = T // R
    tc = LANES

    def body(u_ref, wa_ref, wb_ref, ba_ref, bb_ref, da_ref,
             du_ref, dp_ref, sa, sb, sda, sdb):
        wa_, wb_, ba_, bb_ = wa_ref[...], wb_ref[...], ba_ref[...], bb_ref[...]
        sda[R:R + SUBLANES, :] = jnp.zeros((SUBLANES, tc), F32)
        sdb[R:R + SUBLANES, :] = jnp.zeros((SUBLANES, tc), F32)

        def taps(dc, cur, s1, s2):
            return jnp.concatenate([jnp.sum(dc * s2, axis=0, keepdims=True), jnp.sum(dc * s1, axis=0, keepdims=True),
                                    jnp.sum(dc * cur, axis=0, keepdims=True)], axis=0)

        def du_rows(sd, dc, w):
            sd[0:R, :] = dc
            du = w[2:3, :] * dc + w[1:2, :] * sd[1:1 + R, :] + w[0:1, :] * sd[2:2 + R, :]
            sd[R:R + SUBLANES, :] = dc[0:SUBLANES]
            return du

        def step(rj, carry):
            dwa, dwb, dba, dbb = carry
            r0 = pl.multiple_of((nr - 1 - rj) * R, R)
            ca, cura, s1a, s2a = _conv_rows(u_ref.at[0], sa, wa_, ba_, r0, R)
            cb, curb, s1b, s2b = _conv_rows(u_ref.at[1], sb, wb_, bb_, r0, R)
            dact_ = da_ref[pl.ds(r0, R), :].astype(F32)
            sg = jax.nn.sigmoid(ca)
            dca = dact_ * cb * (sg * (1.0 + ca * (1.0 - sg)))
            dcb = dact_ * (ca * sg)
            du_ref[0, pl.ds(r0, R), :] = du_rows(sda, dca, wa_).astype(BF16)
            du_ref[1, pl.ds(r0, R), :] = du_rows(sdb, dcb, wb_).astype(BF16)
            return (dwa + taps(dca, cura, s1a, s2a), dwb + taps(dcb, curb, s1b, s2b),
                    dba + jnp.sum(dca, axis=0, keepdims=True), dbb + jnp.sum(dcb, axis=0, keepdims=True))

        z3 = jnp.zeros((3, tc), F32)
        z1 = jnp.zeros((1, tc), F32)
        dwa, dwb, dba, dbb = lax.fori_loop(0, nr, step, (z3, z3, z1, z1))
        dp_ref[0] = jnp.concatenate([dwa, dba], axis=0)
        dp_ref[1] = jnp.concatenate([dwb, dbb], axis=0)

    col = pl.BlockSpec((T, tc), lambda j: (0, j))
    wsp = pl.BlockSpec((3, tc), lambda j: (0, j))
    bsp = pl.BlockSpec((1, tc), lambda j: (0, j))
    both = _halves_spec(T, Fd)
    return pl.pallas_call(
        body, name=name, grid=(Fd // tc,), in_specs=[both, wsp, wsp, bsp, bsp, col],
        out_specs=[both, pl.BlockSpec((2, 4, tc), lambda j: (0, 0, j))],
        out_shape=[jax.ShapeDtypeStruct(u.shape, BF16), jax.ShapeDtypeStruct((2, 4, Fd), F32)],
        scratch_shapes=[pltpu.VMEM((CONV_HALO + R, tc), F32)] * 2 + [pltpu.VMEM((R + SUBLANES, tc), F32)] * 2,
        compiler_params=_params(("parallel",)),
    )(u, wa, wb, ba, bb, dact)


def _bucket_index():
    t = np.arange(SW_WINDOW)[None, :] + SW_WINDOW
    s = np.arange(2 * SW_WINDOW)[:, None]
    dist = np.maximum(t - s, 0)
    exact = REL_BUCKETS // 2
    d = np.maximum(dist, 1).astype(np.float32)
    log_b = exact + (np.log(d / np.float32(exact)) / np.float32(math.log(REL_MAX_DIST / exact))
                     * np.float32(REL_BUCKETS - exact)).astype(np.int32)
    bucket = np.where(dist < exact, dist, np.minimum(log_b, REL_BUCKETS - 1))
    return bucket.astype(np.int32).reshape(1, -1)


BIAS_COLS = SW_WINDOW * 2 * SW_WINDOW
BIAS_TILE = 4096


def _bias_from_table(table, bucket, *, name):
    def body(t_ref, idx_ref, o_ref):
        onehot = (lax.broadcasted_iota(jnp.int32, (REL_BUCKETS, BIAS_TILE), 0) == idx_ref[...]).astype(BF16)
        acc = jnp.zeros((SW_Q_HEADS, BIAS_TILE), F32)
        for piece in _split3(t_ref[...]):
            acc = acc + lax.dot_general(piece, onehot, (((0,), (0,)), ((), ())), preferred_element_type=F32)
        o_ref[...] = acc

    return pl.pallas_call(
        body, name=name, grid=(BIAS_COLS // BIAS_TILE,),
        in_specs=[pl.BlockSpec((REL_BUCKETS, SW_Q_HEADS), lambda j: (0, 0)), pl.BlockSpec((1, BIAS_TILE), lambda j: (0, j))],
        out_specs=pl.BlockSpec((SW_Q_HEADS, BIAS_TILE), lambda j: (0, j)),
        out_shape=jax.ShapeDtypeStruct((SW_Q_HEADS, BIAS_COLS), F32),
        compiler_params=_params(("parallel",)),
    )(table, bucket)


def _table_grad(dbias, bucket, *, name):
    def body(d_ref, idx_ref, o_ref):
        @pl.when(pl.program_id(0) == 0)
        def _():
            o_ref[...] = jnp.zeros_like(o_ref)

        onehot = (lax.broadcasted_iota(jnp.int32, (REL_BUCKETS, BIAS_TILE), 0) == idx_ref[...]).astype(BF16)
        acc = jnp.zeros((REL_BUCKETS, SW_Q_HEADS), F32)
        for piece in _split3(d_ref[...]):
            acc = acc + lax.dot_general(onehot, piece, (((1,), (1,)), ((), ())), preferred_element_type=F32)
        o_ref[...] += acc

    return pl.pallas_call(
        body, name=name, grid=(BIAS_COLS // BIAS_TILE,),
        in_specs=[pl.BlockSpec((SW_Q_HEADS, BIAS_TILE), lambda j: (0, j)), pl.BlockSpec((1, BIAS_TILE), lambda j: (0, j))],
        out_specs=pl.BlockSpec((REL_BUCKETS, SW_Q_HEADS), lambda j: (0, 0)),
        out_shape=jax.ShapeDtypeStruct((REL_BUCKETS, SW_Q_HEADS), F32),
        compiler_params=_params(("arbitrary",)),
    )(dbias, bucket)


KV_DIM = SW_KV_HEADS * SW_HEAD_DIM
GROUP_ROWS = SW_GROUP * SW_HEAD_DIM
GROUP_LANES = SW_GROUP * SW_WINDOW


def _band_mask(n):
    s = lax.broadcasted_iota(jnp.int32, (2 * SW_WINDOW, GROUP_LANES), 0)
    t = (lax.broadcasted_iota(jnp.int32, (2 * SW_WINDOW, GROUP_LANES), 1) & (SW_WINDOW - 1)) + SW_WINDOW
    dist = t - s
    return (dist >= 0) & (dist < SW_WINDOW) & ((n > 0) | (s >= SW_WINDOW))


def _side_by_side(x_ref, g):
    r0 = g * GROUP_ROWS
    return jnp.concatenate([x_ref[r0 + r * SW_HEAD_DIM:r0 + (r + 1) * SW_HEAD_DIM, :] for r in range(SW_GROUP)], axis=1)


def _group_inputs(bias_ref, sink_ref, g):
    heads = range(g * SW_GROUP, (g + 1) * SW_GROUP)
    bias = jnp.concatenate([bias_ref[h] for h in heads], axis=1)
    sink = jnp.concatenate([jnp.broadcast_to(sink_ref[:, h:h + 1], (1, SW_WINDOW)) for h in heads], axis=1)
    return heads, bias, sink


def _kv_pair(kvp_ref, kvc_ref, g):
    ks = slice(g * SW_HEAD_DIM, (g + 1) * SW_HEAD_DIM)
    vs = slice(KV_DIM + g * SW_HEAD_DIM, KV_DIM + (g + 1) * SW_HEAD_DIM)
    kk = jnp.concatenate([kvp_ref[:, ks], kvc_ref[:, ks]], axis=0)
    vv = jnp.concatenate([kvp_ref[:, vs], kvc_ref[:, vs]], axis=0)
    return kk, vv, ks, vs


def _col_max(x):
    return jnp.max(x, axis=0, keepdims=True)


def _col_sum(x):
    return jnp.sum(x, axis=0, keepdims=True)


def _attn_fwd(qt, kv, bias, sinks, *, name):
    Dm, T = qt.shape
    W = SW_WINDOW

    def body(q_ref, kvc_ref, kvp_ref, bias_ref, sink_ref, o_ref):
        mask = _band_mask(pl.program_id(0))
        G = range(SW_KV_HEADS)
        ins = [_group_inputs(bias_ref, sink_ref, g) for g in G]
        kvs = [_kv_pair(kvp_ref, kvc_ref, g) for g in G]
        q = [_side_by_side(q_ref, g) for g in G]
        lg = [jnp.where(mask, mm(kvs[g][0], q[g]) * (SW_HEAD_DIM ** -0.5) + ins[g][1], -jnp.inf) for g in G]
        m = [jnp.maximum(_col_max(lg[g]), ins[g][2]) for g in G]
        p = [jnp.exp(lg[g] - m[g]) for g in G]
        den = [_col_sum(p[g]) + jnp.exp(ins[g][2] - m[g]) for g in G]
        o = [mm_tn(kvs[g][1], p[g]) / den[g] for g in G]
        for g in G:
            for r in range(SW_GROUP):
                o_ref[g * GROUP_ROWS + r * SW_HEAD_DIM:g * GROUP_ROWS + (r + 1) * SW_HEAD_DIM, :] = (
                    o[g][:, r * W:(r + 1) * W].astype(BF16))

    return pl.pallas_call(
        body, name=name, grid=(T // W,),
        in_specs=[pl.BlockSpec((Dm, W), lambda n: (0, n)),
                  pl.BlockSpec((W, 2 * KV_DIM), lambda n: (n, 0)),
                  pl.BlockSpec((W, 2 * KV_DIM), lambda n: (jnp.maximum(n - 1, 0), 0)),
                  pl.BlockSpec((SW_Q_HEADS, 2 * W, W), lambda n: (0, 0, 0)),
                  pl.BlockSpec((1, SW_Q_HEADS), lambda n: (0, 0))],
        out_specs=pl.BlockSpec((Dm, W), lambda n: (0, n)),
        out_shape=jax.ShapeDtypeStruct((Dm, T), BF16),
        compiler_params=_params(("parallel",)),
    )(qt, kv, kv, bias, sinks)


def _attn_bwd(qt, kv, bias, sinks, dot, *, name):
    Dm, T = qt.shape
    W = SW_WINDOW
    nb = T // W

    def body(q_ref, kvc_ref, kvp_ref, bias_ref, sink_ref, do_ref,
             dq_ref, dkv_ref, dbias_ref, dsink_ref, carry_ref):
        @pl.when(pl.program_id(0) == 0)
        def _():
            carry_ref[...] = jnp.zeros_like(carry_ref)
            dbias_ref[...] = jnp.zeros_like(dbias_ref)
            dsink_ref[...] = jnp.zeros_like(dsink_ref)

        n = nb - 1 - pl.program_id(0)
        mask = _band_mask(n)
        lane = lax.broadcasted_iota(jnp.int32, (1, SW_Q_HEADS), 1)
        sc = SW_HEAD_DIM ** -0.5
        G = range(SW_KV_HEADS)
        ins = [_group_inputs(bias_ref, sink_ref, g) for g in G]
        kvs = [_kv_pair(kvp_ref, kvc_ref, g) for g in G]
        q = [_side_by_side(q_ref, g) for g in G]
        do = [_side_by_side(do_ref, g) for g in G]
        lg = [jnp.where(mask, mm(kvs[g][0], q[g]) * sc + ins[g][1], -jnp.inf) for g in G]
        m = [jnp.maximum(_col_max(lg[g]), ins[g][2]) for g in G]
        p = [jnp.exp(lg[g] - m[g]) for g in G]
        ps = [jnp.exp(ins[g][2] - m[g]) for g in G]
        rden = [1.0 / (_col_sum(p[g]) + ps[g]) for g in G]
        pn = [p[g] * rden[g] for g in G]
        dpn = [mm(kvs[g][1], do[g]) for g in G]
        delta = [_col_sum(pn[g] * dpn[g]) for g in G]
        ds = [pn[g] * (dpn[g] - delta[g]) for g in G]
        dsr = [-(ps[g] * rden[g]) * delta[g] for g in G]
        dq = [mm_tn(kvs[g][0], ds[g]) * sc for g in G]
        dkk = [mm_nt(ds[g], q[g]) * sc for g in G]
        dvv = [mm_nt(pn[g], do[g]) for g in G]
        dsink = jnp.zeros((1, SW_Q_HEADS), F32)
        for g in G:
            _, _, ks, vs = kvs[g]
            for r, h in enumerate(ins[g][0]):
                cols = slice(r * W, (r + 1) * W)
                dbias_ref[h] += ds[g][:, cols]
                dq_ref[g * GROUP_ROWS + r * SW_HEAD_DIM:g * GROUP_ROWS + (r + 1) * SW_HEAD_DIM, :] = dq[g][:, cols].astype(BF16)
                dsink = dsink + jnp.where(lane == h, jnp.sum(dsr[g][:, cols], axis=1, keepdims=True), 0.0)
            dkv_ref[:, ks] = (carry_ref[:, ks] + dkk[g][W:]).astype(BF16)
            dkv_ref[:, vs] = (carry_ref[:, vs] + dvv[g][W:]).astype(BF16)
            carry_ref[:, ks] = dkk[g][:W]
            carry_ref[:, vs] = dvv[g][:W]
        dsink_ref[...] += dsink

    rev = lambda n: (nb - 1 - n, 0)
    revt = lambda n: (0, nb - 1 - n)
    return pl.pallas_call(
        body, name=name, grid=(nb,),
        in_specs=[pl.BlockSpec((Dm, W), revt),
                  pl.BlockSpec((W, 2 * KV_DIM), rev),
                  pl.BlockSpec((W, 2 * KV_DIM), lambda n: (jnp.maximum(nb - 2 - n, 0), 0)),
                  pl.BlockSpec((SW_Q_HEADS, 2 * W, W), lambda n: (0, 0, 0)),
                  pl.BlockSpec((1, SW_Q_HEADS), lambda n: (0, 0)),
                  pl.BlockSpec((Dm, W), revt)],
        out_specs=[pl.BlockSpec((Dm, W), revt), pl.BlockSpec((W, 2 * KV_DIM), rev),
                   pl.BlockSpec((SW_Q_HEADS, 2 * W, W), lambda n: (0, 0, 0)),
                   pl.BlockSpec((1, SW_Q_HEADS), lambda n: (0, 0))],
        out_shape=[jax.ShapeDtypeStruct((Dm, T), BF16), jax.ShapeDtypeStruct((T, 2 * KV_DIM), BF16),
                   jax.ShapeDtypeStruct((SW_Q_HEADS, 2 * W, W), F32), jax.ShapeDtypeStruct((1, SW_Q_HEADS), F32)],
        scratch_shapes=[pltpu.VMEM((W, 2 * KV_DIM), F32)],
        compiler_params=_params(("arbitrary",)),
    )(qt, kv, kv, bias, sinks, dot)


def _ffn_fwd(hb, w, l, after=None):
    u = _matmul(hb, w["ffn_in"][l], mode="nn", planes="n", out_dtype=BF16, name=f"ffn{l}_up", tm=1024, after=after)
    u = u.reshape((2, 2) + u.shape[1:])
    act = _conv_gate_fwd(u, w["conv_w_a"][l], w["conv_w_b"][l], w["conv_b_a"][l], w["conv_b_b"][l],
                         name=f"ffn{l}_conv_gate")
    return u, act


def _ffn_bwd(dffb, dh_scaled, hb, u, act, w, l, dact):
    g_out = _matmul(act, dffb, mode="tn", name=f"ffn{l}_down_dw", tm=1408, tn=1024, tk=1024, also_bf16=True)
    du, g_conv = _conv_gate_bwd(u, w["conv_w_a"][l], w["conv_w_b"][l], w["conv_b_a"][l], w["conv_b_b"][l],
                                dact, name=f"ffn{l}_conv_gate_bwd")
    du = du.reshape((N_CHIPS,) + du.shape[2:])
    dh = _matmul_planes_nt(du, w["ffn_in"][l], dh_scaled, add_scale=ALPHA, name=f"ffn{l}_up_dx")
    g_in = _matmul(hb, du, mode="tn", planes="n", name=f"ffn{l}_up_dw", tm=1024, tn=FFN_DIM // 2, tk=1024, also_bf16=True)
    return dh, dict(ffn_out=g_out, ffn_in=g_in, conv=g_conv)


def _local_step(x, xb, tgt, w, more_weights, emit):
    bucket = jnp.asarray(_bucket_index())

    pre = _matmul(xb, w["hg_in"], mode="nn", planes="n", out_dtype=BF16, name="hg_in", tm=1024, tn=1024,
                  after=w.get("token"))
    og, states = _hgrn_fwd(pre, w["lb_logits"], w["gnorm"], name="hgrn_fwd")
    z1, h1, h1b = _matmul_ln(og, w["hg_out"], x, w["ln_mix_g"][0], w["ln_mix_b"][0], name="hg_out_ln")
    w = {**w, **more_weights(1, h1b)}
    u0, act0 = _ffn_fwd(h1b, w, 0, after=w.get("token"))
    z2, h2, h2b = _matmul_ln(act0, w["ffn_out"][0], h1, w["ln_ffn_g"][0], w["ln_ffn_b"][0], name="ffn0_down_ln")
    kv = _matmul(h2b, w["kv"], mode="nn", out_dtype=BF16, name="kv_proj")

    bias = _bias_from_table(w["rel_bias"], bucket, name="rel_bias_expand").reshape(SW_Q_HEADS, 2 * SW_WINDOW, SW_WINDOW)
    q1 = _matmul(w["sw_q"], h2b, mode="tt", out_dtype=BF16, name="sw_q", tm=1024, tn=1024)
    o1 = _attn_fwd(q1, kv, bias, w["sinks"], name="attn_fwd")
    z3, h3, h3b = _matmul_ln(o1, w["sw_out"], h2, w["ln_mix_g"][1], w["ln_mix_b"][1], a_t=True, name="sw_out_ln")
    w = {**w, **more_weights(2, h3b)}
    u1, act1 = _ffn_fwd(h3b, w, 1)

    g = {}
    dz, dzb, g["ln_ffn1"], loss_tile, dact1 = _matmul_ln(act1, w["ffn_out"][1], h3, w["ln_ffn_g"][1], w["ln_ffn_b"][1],
                                                         tgt=tgt, name="ffn1_down_ln_loss")

    dh3, gf1 = _ffn_bwd(dzb, dz, h3b, u1, act1, w, 1, dact1)
    dz, dzb, g["ln_mix1"], do1 = _ln_bwd_matmul(dh3, z3, w["ln_mix_g"][1], w["ln_mix_b"][1], w["sw_out"], out_t=True,
                                                name="ln_mix1_bwd_sw_out_dx")
    g_sw_out = _matmul(o1, dzb, mode="nn", name="sw_out_dw", tm=1024, tn=1024, tk=1024, also_bf16=True)
    dq1, dkv, dbias, dsinks = _attn_bwd(q1, kv, bias, w["sinks"], do1, name="attn_bwd")
    g["sinks"] = dsinks
    g["rel_bias"] = _table_grad(dbias.reshape(SW_Q_HEADS, BIAS_COLS), bucket, name="rel_bias_grad")
    dh2 = _matmul(dq1, w["sw_q"], mode="tt", add=dz, add_scale=ALPHA, name="sw_q_dx", tn=1024)
    dh2 = _matmul(dkv, w["kv"], mode="nt", add=dh2, name="kv_dx", tn=1024)
    g_sw_q = _matmul(h2b, dq1, mode="tt", name="sw_q_dw", tm=1024, tn=1024, tk=1024, also_bf16=True)
    g_kv = _matmul(h2b, dkv, mode="tn", name="kv_dw", tm=1024, tn=512, tk=1024, also_bf16=True)
    tok = emit(1, dict(sw_q=g_sw_q, sw_out=g_sw_out, kv=g_kv, ffn_in=gf1["ffn_in"], ffn_out=gf1["ffn_out"]))

    dz, dzb, g["ln_ffn0"], dact0 = _ln_bwd_matmul(dh2, z2, w["ln_ffn_g"][0], w["ln_ffn_b"][0], w["ffn_out"][0],
                                                  name="ln_ffn0_bwd_down_dx", after=tok)
    dh1, gf0 = _ffn_bwd(dzb, dz, h1b, u0, act0, w, 0, dact0)
    dz, dzb, g["ln_mix0"], dog = _ln_bwd_matmul(dh1, z1, w["ln_mix_g"][0], w["ln_mix_b"][0], w["hg_out"],
                                                name="ln_mix0_bwd_hg_out_dx")
    g_hg_out = _matmul(og, dzb, mode="tn", name="hg_out_dw", tm=1024, tn=1024, tk=1024, also_bf16=True)
    tok = emit(2, dict(hg_out=g_hg_out, ffn_in=gf0["ffn_in"], ffn_out=gf0["ffn_out"]))
    dpre, g["lb_logits"], g["gnorm"] = _hgrn_bwd(pre, w["lb_logits"], w["gnorm"], states, dog, name="hgrn_bwd", after=tok)
    tok = emit(3, dict(hg_in=_matmul(xb, dpre, mode="tn", planes="n", name="hg_in_dw", tm=1024, tn=1024, tk=1024, also_bf16=True)))
    dx = _matmul_planes_nt(dpre, w["hg_in"], dz, add_scale=ALPHA, name="hg_in_dx", after=tok)
    g["conv0"], g["conv1"] = gf0["conv"], gf1["conv"]
    return loss_tile, dx, g


def _adamw(wt, ga, gb, m, v, *, name, rows=None, prev=None):
    R, Cc = wt.shape
    r0, n = rows if rows is not None else (0, R)
    tr = _tile(n, 256, SUBLANES) if n % SUBLANES == 0 else n
    assert r0 % tr == 0
    c1 = 1.0 - ADAM_B1 ** ADAM_STEP
    c2 = 1.0 - ADAM_B2 ** ADAM_STEP
    two = gb is not None
    n_in = 5 if two else 4

    def body(*refs):
        if two:
            w_ref, ga_ref, gb_ref, m_ref, v_ref = refs[:5]
            g_ = ga_ref[...] + gb_ref[...]
        else:
            w_ref, ga_ref, m_ref, v_ref = refs[:4]
            g_ = ga_ref[...]
        g_ref, d_ref, nm_ref, nv_ref = refs[-4:]
        nm = ADAM_B1 * m_ref[...] + (1.0 - ADAM_B1) * g_
        nv = ADAM_B2 * v_ref[...] + (1.0 - ADAM_B2) * (g_ * g_)
        g_ref[...] = g_
        d_ref[...] = -ADAM_LR * ((nm / c1) / (jnp.sqrt(nv / c2) + ADAM_EPS) + ADAM_WD * w_ref[...])
        nm_ref[...] = nm
        nv_ref[...] = nv

    full = pl.BlockSpec((tr, Cc), lambda i: (i + r0 // tr, 0))
    part = pl.BlockSpec((tr, Cc), lambda i: (i, 0))
    args = (wt, ga, gb, m, v) if two else (wt, ga, m, v)
    in_specs = [full] + [part] * (n_in - 3) + [full, full]
    aliases = {}
    if prev is not None:
        args, in_specs = args + tuple(prev), in_specs + [ANY_SPEC] * 4
        aliases = {n_in + t: t for t in range(4)}
    return pl.pallas_call(
        body, name=name, grid=(n // tr,), in_specs=in_specs, out_specs=[full] * 4,
        out_shape=[jax.ShapeDtypeStruct((R, Cc), F32)] * 4, input_output_aliases=aliases,
        compiler_params=_params(("parallel",)),
    )(*args)


def _adamw_small(ws, gs, ms, vs, *, name):
    n = len(ws)
    c1 = 1.0 - ADAM_B1 ** ADAM_STEP
    c2 = 1.0 - ADAM_B2 ** ADAM_STEP

    def body(*refs):
        w_refs, g_refs, m_refs, v_refs = (refs[k * n:(k + 1) * n] for k in range(4))
        d_refs, nm_refs, nv_refs = (refs[(4 + k) * n:(5 + k) * n] for k in range(3))
        for i in range(n):
            g_ = g_refs[i][...]
            nm = ADAM_B1 * m_refs[i][...] + (1.0 - ADAM_B1) * g_
            nv = ADAM_B2 * v_refs[i][...] + (1.0 - ADAM_B2) * (g_ * g_)
            d_refs[i][...] = -ADAM_LR * ((nm / c1) / (jnp.sqrt(nv / c2) + ADAM_EPS) + ADAM_WD * w_refs[i][...])
            nm_refs[i][...] = nm
            nv_refs[i][...] = nv

    vm = pl.BlockSpec(memory_space=pltpu.VMEM)
    out = pl.pallas_call(
        body, name=name, in_specs=[vm] * (4 * n), out_specs=[vm] * (3 * n),
        out_shape=[jax.ShapeDtypeStruct(w.shape, F32) for w in ws] * 3,
    )(*ws, *gs, *ms, *vs)
    return out[:n], out[n:2 * n], out[2 * n:]


HBM_SPEC = pl.BlockSpec(memory_space=pltpu.HBM)
SEM_SPEC = pl.BlockSpec(memory_space=pltpu.SEMAPHORE)
VMEM_SPEC = pl.BlockSpec(memory_space=pltpu.VMEM)
DATAFLOW = pltpu.SideEffectType.DATAFLOW_SIDE_EFFECTING


def _in_hbm(a):
    return pltpu.with_memory_space_constraint(a, pltpu.HBM)


def _place():
    return lax.axis_index("x"), lax.axis_index("y"), lax.axis_index("c")


def _other_chips(x, y):
    return [(1 - x, y), (x, 1 - y), (1 - x, 1 - y)]


def _sum8(vs, *, name):
    n = len(vs)

    def body(*refs):
        v_refs, all_refs, o_refs = refs[:n], refs[n:2 * n], refs[2 * n:3 * n]
        send_sems, recv_sems, local_sems = refs[3 * n:]
        x, y, c = _place()
        me, sibling = (x, y, c), (x, y, 1 - c)
        chips = _other_chips(x, y)

        def slot(i, px, py, pc):
            return all_refs[i].at[4 * px + 2 * py + pc]

        def copy(i, k, block, to, src=None):
            return pltpu.make_async_remote_copy(
                src_ref=slot(i, *block) if src is None else src, dst_ref=slot(i, *block),
                send_sem=send_sems.at[7 * i + k], recv_sem=recv_sems.at[7 * i + k], device_id=to, device_id_type=MESH)

        mine = [pltpu.make_async_copy(v_refs[i], slot(i, *me), local_sems.at[i]) for i in range(n)]
        for cp in mine:
            cp.start()
        first = [copy(i, 0, me, sibling, src=v_refs[i]) for i in range(n)]
        first += [copy(i, 1 + j, me, (*chip, c), src=v_refs[i]) for i in range(n) for j, chip in enumerate(chips)]
        for cp in first:
            cp.start()
        passed = []
        for i in range(n):
            for j, chip in enumerate(chips):
                copy(i, 1 + j, (*chip, c), me).wait_recv()
                passed.append(copy(i, 4 + j, (*chip, c), sibling))
                passed[-1].start()
        for i in range(n):
            copy(i, 0, sibling, me).wait_recv()
            for j, chip in enumerate(chips):
                copy(i, 4 + j, (*chip, 1 - c), me).wait_recv()
        for cp in first + passed:
            cp.wait_send()
        for cp in mine:
            cp.wait()
        for i in range(n):
            acc = all_refs[i][0]
            for d in range(1, N_DEV):
                acc = acc + all_refs[i][d]
            o_refs[i][...] = acc

    return pl.pallas_call(
        body, name=name, in_specs=[VMEM_SPEC] * n, out_specs=[VMEM_SPEC] * (2 * n),
        out_shape=[jax.ShapeDtypeStruct((N_DEV,) + v.shape, F32) for v in vs] + [jax.ShapeDtypeStruct(v.shape, F32) for v in vs],
        scratch_shapes=[pltpu.SemaphoreType.DMA((7 * n,)), pltpu.SemaphoreType.DMA((7 * n,)), pltpu.SemaphoreType.DMA((n,))],
        compiler_params=pltpu.CompilerParams(vmem_limit_bytes=VMEM_LIMIT),
    )(*vs)[n:]


def _swap_sibling(vs, *, name):
    n = len(vs)

    def body(*refs):
        src, dst, send_sems, recv_sems = refs[:n], refs[n:2 * n], refs[2 * n], refs[2 * n + 1]
        x, y, c = _place()
        cps = [pltpu.make_async_remote_copy(src_ref=src[i], dst_ref=dst[i], send_sem=send_sems.at[i],
                                            recv_sem=recv_sems.at[i], device_id=(x, y, 1 - c), device_id_type=MESH)
               for i in range(n)]
        for cp in cps:
            cp.start()
        for cp in cps:
            cp.wait()

    return pl.pallas_call(
        body, name=name, in_specs=[HBM_SPEC] * n, out_specs=[HBM_SPEC] * n,
        out_shape=[jax.ShapeDtypeStruct(v.shape, v.dtype) for v in vs],
        scratch_shapes=[pltpu.SemaphoreType.DMA((n,)), pltpu.SemaphoreType.DMA((n,))],
    )(*vs)


def _gather_copies(srcs, lands, send, recv, sibling=False):
    x, y, c = _place()
    out = []
    for i, (src, land) in enumerate(zip(srcs, lands)):
        half = land.shape[1] // 2
        rows = pl.ds(c * half, half)
        for k, (px, py) in enumerate(_other_chips(x, y)):
            if sibling:
                src_ref, dst_ref, to = src.at[2 * px + py, rows], land.at[2 * px + py, rows], (x, y, 1 - c)
            else:
                src_ref, dst_ref, to = src.at[rows], land.at[2 * x + y, rows], (px, py, c)
            out.append(pltpu.make_async_remote_copy(src_ref=src_ref, dst_ref=dst_ref, send_sem=send.at[3 * i + k],
                                                    recv_sem=recv.at[3 * i + k], device_id=to, device_id_type=MESH))
    return out


def _gather_arrivals(lands, send, recv, sibling=False):
    x, y, c = _place()
    out = []
    for i, land in enumerate(lands):
        half = land.shape[1] // 2
        rows = pl.ds(((1 - c) if sibling else c) * half, half)
        for k, (px, py) in enumerate(_other_chips(x, y)):
            part = land.at[2 * px + py, rows]
            out.append(pltpu.make_async_remote_copy(src_ref=part, dst_ref=part, send_sem=send.at[3 * i + k],
                                                    recv_sem=recv.at[3 * i + k],
                                                    device_id=(x, y, 1 - c) if sibling else (px, py, c), device_id_type=MESH))
    return out


def _gather_start(shards, after, *, name):
    n = len(shards)

    def body(*refs):
        srcs, lands, send, recv, token = refs[:n], refs[n:2 * n], refs[2 * n], refs[2 * n + 1], refs[-1]
        for cp in _gather_copies(srcs, lands, send, recv):
            cp.start()
        token[...] = jnp.zeros_like(token)

    lands = [lax.empty((N_CHIPS,) + s.shape, s.dtype) for s in shards]
    sems = pltpu.SemaphoreType.DMA((3 * n,))
    body, xs, xa = _after(body, 2 * n, after)
    out = pl.pallas_call(
        body, name=name, in_specs=[HBM_SPEC] * (2 * n) + xs,
        out_specs=[SEM_SPEC, SEM_SPEC] + [HBM_SPEC] * (2 * n) + [VMEM_SPEC],
        out_shape=[sems, sems] + [pltpu.HBM(a.shape, a.dtype) for a in list(shards) + lands]
        + [jax.ShapeDtypeStruct((SUBLANES, LANES), F32)],
        input_output_aliases={i: 2 + i for i in range(2 * n)},
        compiler_params=pltpu.CompilerParams(has_side_effects=DATAFLOW),
    )(*[_in_hbm(a) for a in list(shards) + lands], *xa)
    return (out[0], out[1], out[2:2 + n], out[2 + n:2 + 2 * n]), out[-1]


def _gather_wait(handle, after, *, name):
    send_sems, recv_sems, srcs, lands = handle
    n = len(srcs)

    def body(*refs):
        srcs_, lands_, send, recv = refs[:n], refs[n:2 * n], refs[2 * n], refs[2 * n + 1]
        for cp in _gather_copies(srcs_, lands_, send, recv):
            cp.wait_send()
        for cp in _gather_arrivals(lands_, send, recv):
            cp.wait_recv()

    both = list(srcs) + list(lands)
    out = pl.pallas_call(
        body, name=name, in_specs=[HBM_SPEC] * (2 * n) + [SEM_SPEC, SEM_SPEC, ANY_SPEC], out_specs=[HBM_SPEC] * (2 * n),
        out_shape=[pltpu.HBM(a.shape, a.dtype) for a in both],
        input_output_aliases={i: i for i in range(2 * n)},
        compiler_params=pltpu.CompilerParams(has_side_effects=DATAFLOW),
    )(*both, send_sems, recv_sems, after)
    return out[n:]


def _fill_sibling(lands, *, name):
    n = len(lands)

    def body(*refs):
        ins, outs, send_sems, recv_sems = refs[:n], refs[n:2 * n], refs[2 * n], refs[2 * n + 1]
        cps = _gather_copies(ins, outs, send_sems, recv_sems, sibling=True)
        for cp in cps:
            cp.start()
        for cp in _gather_arrivals(outs, send_sems, recv_sems, sibling=True):
            cp.wait_recv()
        for cp in cps:
            cp.wait_send()

    return pl.pallas_call(
        body, name=name, in_specs=[HBM_SPEC] * n, out_specs=[HBM_SPEC] * n,
        out_shape=[jax.ShapeDtypeStruct(a.shape, a.dtype) for a in lands],
        scratch_shapes=[pltpu.SemaphoreType.DMA((3 * n,)), pltpu.SemaphoreType.DMA((3 * n,))],
        input_output_aliases={i: i for i in range(n)},
    )(*lands)


def _scatter_copies(src, land, send, recv):
    x, y, c = _place()
    return [pltpu.make_async_remote_copy(src_ref=src[i].at[2 * px + py], dst_ref=land[i].at[k], send_sem=send.at[3 * i + k],
                                         recv_sem=recv.at[3 * i + k], device_id=(px, py, c), device_id_type=MESH)
            for i in range(len(src)) for k, (px, py) in enumerate(_other_chips(x, y))]


def _scatter_start(pieces, *, name):
    n = len(pieces)

    def body(*refs):
        src, land, send, recv, token = refs[:n], refs[n:2 * n], refs[2 * n], refs[2 * n + 1], refs[-1]
        for cp in _scatter_copies(src, land, send, recv):
            cp.start()
        token[...] = jnp.zeros_like(token)

    lands = [lax.empty((3,) + p.shape[1:], p.dtype) for p in pieces]
    sems = pltpu.SemaphoreType.DMA((3 * n,))
    out = pl.pallas_call(
        body, name=name, in_specs=[HBM_SPEC] * (2 * n),
        out_specs=[SEM_SPEC, SEM_SPEC] + [HBM_SPEC] * (2 * n) + [VMEM_SPEC],
        out_shape=[sems, sems] + [pltpu.HBM(a.shape, a.dtype) for a in pieces + lands]
        + [jax.ShapeDtypeStruct((SUBLANES, LANES), F32)],
        input_output_aliases={i: 2 + i for i in range(2 * n)},
        compiler_params=pltpu.CompilerParams(has_side_effects=DATAFLOW),
    )(*[_in_hbm(a) for a in pieces + lands])
    return (out[0], out[1], out[2:2 + n], out[2 + n:2 + 2 * n]), out[-1]


def _scatter_wait(handle, after, *, name):
    send_sems, recv_sems, srcs, lands = handle
    n = len(srcs)

    def body(*refs):
        src, land, send, recv = refs[:n], refs[n:2 * n], refs[2 * n], refs[2 * n + 1]
        for cp in _scatter_copies(src, land, send, recv):
            cp.wait_send()
            cp.wait_recv()

    both = list(srcs) + list(lands)
    out = pl.pallas_call(
        body, name=name, in_specs=[HBM_SPEC] * (2 * n) + [SEM_SPEC, SEM_SPEC, ANY_SPEC], out_specs=[HBM_SPEC] * (2 * n),
        out_shape=[pltpu.HBM(a.shape, a.dtype) for a in both],
        input_output_aliases={i: i for i in range(2 * n)},
        compiler_params=pltpu.CompilerParams(has_side_effects=DATAFLOW),
    )(*both, send_sems, recv_sems, after)
    return out[n:]


def _to_bf16(x, *, name, after=None):
    T, Dm = x.shape
    tr = _tile(T, 512, 2 * SUBLANES)

    def body(x_ref, o_ref):
        o_ref[...] = x_ref[...].astype(BF16)

    blk = pl.BlockSpec((tr, Dm), lambda i: (i, 0))
    body, xs, xa = _after(body, 1, after)
    return pl.pallas_call(
        body, name=name, grid=(T // tr,), in_specs=[blk] + xs, out_specs=blk, out_shape=jax.ShapeDtypeStruct((T, Dm), BF16),
        compiler_params=_params(("parallel",)),
    )(x, *xa)


def _chip_sum(pieces, got, chip, *, name):
    _, R, Cc = pieces.shape
    tr = _tile(R, 256, SUBLANES)

    def body(chip_ref, a_ref, g_ref, o_ref):
        o_ref[...] = ((a_ref[...] + g_ref[0].astype(F32)) + g_ref[1].astype(F32)) + g_ref[2].astype(F32)

    return pl.pallas_call(
        body, name=name,
        grid_spec=pltpu.PrefetchScalarGridSpec(
            num_scalar_prefetch=1, grid=(R // tr,),
            in_specs=[pl.BlockSpec((None, tr, Cc), lambda i, ch: (ch[0], i, 0)),
                      pl.BlockSpec((3, tr, Cc), lambda i, ch: (0, i, 0))],
            out_specs=pl.BlockSpec((tr, Cc), lambda i, ch: (i, 0))),
        out_shape=jax.ShapeDtypeStruct((R, Cc), F32),
        compiler_params=_params(("parallel",)),
    )(chip, pieces, got)


PACK_COLS = 1024
SMALL_ROWS = 32


def kernel(x, hgrn_w_in, hgrn_lb_logits, hgrn_gnorm_w, hgrn_w_out, swa_w_q, swa_sinks, swa_w_out, shared_w_kv, rel_bias, ffn_w_in, ffn_conv_w, ffn_conv_b, ffn_w_out, ln_mix_g, ln_mix_b, ln_ffn_g, ln_ffn_b, loss_target, m_hgrn_w_in, m_hgrn_lb_logits, m_hgrn_gnorm_w, m_hgrn_w_out, m_swa_w_q, m_swa_sinks, m_swa_w_out, m_shared_w_kv, m_rel_bias, m_ffn_w_in, m_ffn_conv_w, m_ffn_conv_b, m_ffn_w_out, m_ln_mix_g, m_ln_mix_b, m_ln_ffn_g, m_ln_ffn_b, v_hgrn_w_in, v_hgrn_lb_logits, v_hgrn_gnorm_w, v_hgrn_w_out, v_swa_w_q, v_swa_sinks, v_swa_w_out, v_shared_w_kv, v_rel_bias, v_ffn_w_in, v_ffn_conv_w, v_ffn_conv_b, v_ffn_w_out, v_ln_mix_g, v_ln_mix_b, v_ln_ffn_g, v_ln_ffn_b):
    xi, yi, ci = _place()
    chip = 2 * xi + yi
    Dm = D_MODEL
    FC = 2 * FFN_DIM // N_CHIPS
    Fo = FFN_DIM // N_CHIPS
    Dq = Dm // N_CHIPS
    bf = lambda a: a.astype(BF16)

    small = jnp.concatenate([hgrn_lb_logits.reshape(-1), ffn_conv_w.reshape(-1)])
    n_small = small.shape[0]
    bits = jnp.concatenate(_split3(small))
    bits = jnp.pad(bits, (0, SMALL_ROWS * PACK_COLS - 3 * n_small)).reshape(SMALL_ROWS, PACK_COLS)
    groups = [[bf(hgrn_w_in[0]), bf(hgrn_w_out[0]), bits],
              [bf(swa_w_q[0]), bf(swa_w_out[0]), bf(shared_w_kv), bf(ffn_w_in[0]), bf(ffn_w_out[0])],
              [bf(ffn_w_in[1]), bf(ffn_w_out[1])]]

    def gathered(k, landed):
        lands = _fill_sibling(landed, name=f"gather_w{k}_fill")
        return [lax.dynamic_update_slice(land, shard[None], (chip,) + (0,) * shard.ndim)
                for land, shard in zip(lands, groups[k])], lands

    handle0, token0 = _gather_start(groups[0], None, name="gather_w0_start")
    xb = _to_bf16(x[0], name="x_to_bf16", after=token0)
    corner = lambda a: a[:2 * SUBLANES, :LANES]
    casts_done = corner(xb) + sum(corner(a) for a in groups[1] + groups[2])
    (w_in, w_hg_out, small_all), lands0 = gathered(0, _gather_wait(handle0, casts_done, name="gather_w0_wait"))
    handle1, token1 = _gather_start(groups[1], lands0[0], name="gather_w1_start")
    parts = small_all.reshape(N_CHIPS, -1)[:, :3 * n_small].reshape(N_CHIPS, 3, n_small).astype(F32)
    vals = (parts[:, 0] + parts[:, 1]) + parts[:, 2]
    lb_full = vals[:, :2 * Dq].reshape(N_CHIPS, 2, Dq).transpose(1, 0, 2).reshape(2, Dm)
    cw_full = vals[:, 2 * Dq:].reshape(N_CHIPS, DEPTH, 3, FC).transpose(1, 2, 0, 3).reshape(DEPTH, 3, 2 * FFN_DIM)

    got = {"handle": handle1}

    def more_weights(k, after):
        ws, lands = gathered(k, _gather_wait(got.pop("handle"), after, name=f"gather_w{k}_wait"))
        if k == 1:
            got["handle"], token2 = _gather_start(groups[2], lands[0], name="gather_w2_start")
            w_q, w_o, w_kv, w_fi, w_fo = ws
            got.update(ffn_in={0: w_fi}, ffn_out={0: w_fo.reshape(FFN_DIM, Dm)})
            return {"sw_q": w_q.reshape(Dm, Dm), "sw_out": w_o.reshape(Dm, Dm), "kv": w_kv.reshape(Dm, 2 * KV_DIM),
                    "token": token2, "ffn_in": got["ffn_in"], "ffn_out": got["ffn_out"]}
        w_fi, w_fo = ws
        return {"ffn_in": {**got["ffn_in"], 1: w_fi}, "ffn_out": {**got["ffn_out"], 1: w_fo.reshape(FFN_DIM, Dm)}}

    w = {
        "hg_in": w_in, "hg_out": w_hg_out.reshape(Dm, Dm), "token": token1,
        "lb_logits": lb_full, "gnorm": hgrn_gnorm_w, "sinks": swa_sinks, "rel_bias": rel_bias,
        "conv_w_a": [cw_full[l, :, :FFN_DIM] for l in range(DEPTH)],
        "conv_w_b": [cw_full[l, :, FFN_DIM:] for l in range(DEPTH)],
        "conv_b_a": [ffn_conv_b[l:l + 1, :FFN_DIM] for l in range(DEPTH)],
        "conv_b_b": [ffn_conv_b[l:l + 1, FFN_DIM:] for l in range(DEPTH)],
        "ln_mix_g": [ln_mix_g[l:l + 1] for l in range(DEPTH)], "ln_mix_b": [ln_mix_b[l:l + 1] for l in range(DEPTH)],
        "ln_ffn_g": [ln_ffn_g[l:l + 1] for l in range(DEPTH)], "ln_ffn_b": [ln_ffn_b[l:l + 1] for l in range(DEPTH)],
    }

    sent = {}

    def emit(k, gd):
        rows4 = lambda a: a.reshape(N_CHIPS, -1, a.shape[-1])
        order = {1: ["sw_q", "sw_out", "kv", "ffn_in", "ffn_out"], 2: ["ffn_in", "ffn_out", "hg_out"], 3: ["hg_in"]}[k]
        as_pieces = lambda a, nme: a if nme in ("ffn_in", "hg_in") else rows4(a)
        handle, token = _scatter_start([as_pieces(gd[nme][1], nme) for nme in order], name=f"scatter_g{k}_start")
        sent[k] = (handle, [as_pieces(gd[nme][0], nme) for nme in order])
        return token

    loss_tile, grad_x, g = _local_step(x[0], xb, loss_target[0], w, more_weights, emit)

    wts = dict(hgrn_w_in=hgrn_w_in, hgrn_lb_logits=hgrn_lb_logits, hgrn_gnorm_w=hgrn_gnorm_w, hgrn_w_out=hgrn_w_out,
               swa_w_q=swa_w_q, swa_sinks=swa_sinks, swa_w_out=swa_w_out, shared_w_kv=shared_w_kv, rel_bias=rel_bias,
               ffn_w_in=ffn_w_in, ffn_conv_w=ffn_conv_w, ffn_conv_b=ffn_conv_b, ffn_w_out=ffn_w_out,
               ln_mix_g=ln_mix_g, ln_mix_b=ln_mix_b, ln_ffn_g=ln_ffn_g, ln_ffn_b=ln_ffn_b)
    ms = dict(hgrn_w_in=m_hgrn_w_in, hgrn_lb_logits=m_hgrn_lb_logits, hgrn_gnorm_w=m_hgrn_gnorm_w, hgrn_w_out=m_hgrn_w_out,
              swa_w_q=m_swa_w_q, swa_sinks=m_swa_sinks, swa_w_out=m_swa_w_out, shared_w_kv=m_shared_w_kv, rel_bias=m_rel_bias,
              ffn_w_in=m_ffn_w_in, ffn_conv_w=m_ffn_conv_w, ffn_conv_b=m_ffn_conv_b, ffn_w_out=m_ffn_w_out,
              ln_mix_g=m_ln_mix_g, ln_mix_b=m_ln_mix_b, ln_ffn_g=m_ln_ffn_g, ln_ffn_b=m_ln_ffn_b)
    vs = dict(hgrn_w_in=v_hgrn_w_in, hgrn_lb_logits=v_hgrn_lb_logits, hgrn_gnorm_w=v_hgrn_gnorm_w, hgrn_w_out=v_hgrn_w_out,
              swa_w_q=v_swa_w_q, swa_sinks=v_swa_sinks, swa_w_out=v_swa_w_out, shared_w_kv=v_shared_w_kv, rel_bias=v_rel_bias,
              ffn_w_in=v_ffn_w_in, ffn_conv_w=v_ffn_conv_w, ffn_conv_b=v_ffn_conv_b, ffn_w_out=v_ffn_w_out,
              ln_mix_g=v_ln_mix_g, ln_mix_b=v_ln_mix_b, ln_ffn_g=v_ln_ffn_g, ln_ffn_b=v_ln_ffn_b)
    names = list(wts)
    grads, delta, new_m, new_v = {}, {}, {}, {}

    def update(n, ga, gb, layer=None, prev=None):
        r2 = lambda a: a.reshape(-1, a.shape[-1])
        rows = None if layer is None else (layer * ga.shape[0], ga.shape[0])
        return _adamw(r2(wts[n]), ga, gb, r2(ms[n]), r2(vs[n]), rows=rows, prev=prev,
                      name=f"adamw_{n}" + ("" if layer is None else f"_{layer}"))

    def keep(n, res):
        grads[n], delta[n], new_m[n], new_v[n] = [a.reshape(wts[n].shape) for a in res]

    chip1 = jnp.reshape(chip, (1,)).astype(jnp.int32)
    after = grad_x
    for k in (1, 2, 3):
        handle, pieces = sent[k]
        lands = _scatter_wait(handle, after, name=f"scatter_g{k}_wait")
        parts = [_chip_sum(p, l, chip1, name=f"scatter_g{k}_sum{i}") for i, (p, l) in enumerate(zip(pieces, lands))]
        sibs = _swap_sibling(parts, name=f"scatter_g{k}_swap")
        if k == 1:
            for n, ga, gb in zip(["swa_w_q", "swa_w_out", "shared_w_kv"], parts[:3], sibs[:3]):
                keep(n, update(n, ga, gb))
            ffn_in_1 = update("ffn_w_in", parts[3], sibs[3], layer=1)
            ffn_out_1 = update("ffn_w_out", parts[4], sibs[4], layer=1)
            after = ffn_out_1[3]
        elif k == 2:
            keep("ffn_w_in", update("ffn_w_in", parts[0], sibs[0], layer=0, prev=ffn_in_1))
            keep("ffn_w_out", update("ffn_w_out", parts[1], sibs[1], layer=0, prev=ffn_out_1))
            keep("hgrn_w_out", update("hgrn_w_out", parts[2], sibs[2]))
            after = new_v["hgrn_w_out"]
        else:
            keep("hgrn_w_in", update("hgrn_w_in", parts[0], sibs[0]))

    small_keys = ["lb_logits", "gnorm", "sinks", "rel_bias", "conv0", "conv1", "ln_mix0", "ln_mix1", "ln_ffn0", "ln_ffn1"]
    flat2 = lambda a: a.reshape(-1, a.shape[-1])
    sums = _sum8([loss_tile] + [flat2(g[k]) for k in small_keys], name="sum_small")
    loss = sums[0][0, 0]
    sg = {k: v.reshape(g[k].shape) for k, v in zip(small_keys, sums[1:])}
    conv = [sg["conv0"], sg["conv1"]]
    g_cw = jnp.stack([jnp.concatenate([conv[l][0, :3], conv[l][1, :3]], axis=1) for l in range(DEPTH)])
    g_cb = jnp.stack([jnp.concatenate([conv[l][0, 3], conv[l][1, 3]], axis=0) for l in range(DEPTH)])
    ln = lambda nme, r: jnp.stack([sg[nme + "0"][r], sg[nme + "1"][r]])
    small_g = dict(hgrn_lb_logits=lax.dynamic_slice_in_dim(sg["lb_logits"], chip * Dq, Dq, axis=1),
                   hgrn_gnorm_w=sg["gnorm"], swa_sinks=sg["sinks"], rel_bias=sg["rel_bias"],
                   ffn_conv_w=lax.dynamic_slice_in_dim(g_cw, chip * FC, FC, axis=2), ffn_conv_b=g_cb,
                   ln_mix_g=ln("ln_mix", 0), ln_mix_b=ln("ln_mix", 1), ln_ffn_g=ln("ln_ffn", 0), ln_ffn_b=ln("ln_ffn", 1))
    small_names = list(small_g)
    d_, m_, v_ = _adamw_small([flat2(wts[n]) for n in small_names], [flat2(small_g[n]) for n in small_names],
                              [flat2(ms[n]) for n in small_names], [flat2(vs[n]) for n in small_names], name="adamw_small")
    for n, a, b_, c_ in zip(small_names, d_, m_, v_):
        shp = wts[n].shape
        grads[n], delta[n], new_m[n], new_v[n] = small_g[n], a.reshape(shp), b_.reshape(shp), c_.reshape(shp)

    return (loss, grad_x[None], *[grads[n] for n in names], *[delta[n] for n in names],
            *[new_m[n] for n in names], *[new_v[n] for n in names])
```

```python
import math

import numpy as np
import jax
import jax.numpy as jnp
from jax import lax
from jax.experimental import pallas as pl
from jax.experimental.pallas import tpu as pltpu

F32 = jnp.float32
BF16 = jnp.bfloat16
MESH = pl.DeviceIdType.MESH

D_MODEL = 1024
DEPTH = 2
HG_HEADS = 8
HG_DIM = 128
SW_Q_HEADS = 16
SW_KV_HEADS = 4
SW_HEAD_DIM = 64
SW_GROUP = 4
SW_WINDOW = 128
REL_BUCKETS = 32
REL_MAX_DIST = 128
FFN_DIM = 2816
ALPHA = (2.0 * DEPTH) ** 0.25
LN_EPS = 1e-5
RMS_EPS = 1e-6
ADAM_LR = 0.001
ADAM_B1 = 0.9
ADAM_B2 = 0.999
ADAM_EPS = 1e-08
ADAM_WD = 0.01
ADAM_STEP = 10

VMEM_BYTES_V7X = 64 * 1024 * 1024
VMEM_LIMIT = VMEM_BYTES_V7X - 8 * 1024 * 1024
LANES = 128
SUBLANES = 8

HG_C = 64
HG_RB = 256
CONV_R = 128
N_CHIPS = 4
N_DEV = 8

ANY_SPEC = pl.BlockSpec(memory_space=pl.ANY)


def _after(body, n_in, after):
    if after is None:
        return body, [], ()

    def wrapped(*refs):
        return body(*refs[:n_in], *refs[n_in + 1:])

    return wrapped, [ANY_SPEC], (after,)


def _params(sem=None):
    return pltpu.CompilerParams(dimension_semantics=sem, vmem_limit_bytes=VMEM_LIMIT)


def _tile(n, pref, unit=LANES):
    if n <= pref:
        return n
    best = None
    for t in range(unit, pref + 1, unit):
        if n % t == 0:
            best = t
    assert best is not None, (n, pref, unit)
    return best


def _dot(a, b, ca, cb):
    nb = a.ndim - 2
    batch = tuple(range(nb))
    return lax.dot_general(a.astype(BF16), b.astype(BF16), (((nb + ca,), (nb + cb,)), (batch, batch)),
                           preferred_element_type=F32)


@jax.custom_vjp
def mm(a, b):
    return _dot(a, b, 1, 0)


@jax.custom_vjp
def mm_nt(a, b):
    return _dot(a, b, 1, 1)


@jax.custom_vjp
def mm_tn(a, b):
    return _dot(a, b, 0, 0)


mm.defvjp(lambda a, b: (mm(a, b), (a, b)), lambda r, ct: (mm_nt(ct, r[1]), mm_tn(r[0], ct)))
mm_nt.defvjp(lambda a, b: (mm_nt(a, b), (a, b)), lambda r, ct: (mm(ct, r[1]), mm_tn(ct, r[0])))
mm_tn.defvjp(lambda a, b: (mm_tn(a, b), (a, b)), lambda r, ct: (mm_nt(r[1], ct), mm(r[0], ct)))


def _split2(x):
    hi = x.astype(BF16)
    return hi, (x - hi.astype(F32)).astype(BF16)


@jax.custom_vjp
def _scores(qt, kt):
    return _dot(qt, kt, 1, 1)


def _scores_bwd(r, ct):
    (qh, ql), (kh, kl) = _split2(r[0]), _split2(r[1])
    return _dot(ct, kh, 1, 0) + _dot(ct, kl, 1, 0), _dot(ct, qh, 0, 0) + _dot(ct, ql, 0, 0)


_scores.defvjp(lambda a, b: (_scores(a, b), (a, b)), _scores_bwd)


def _split3(x):
    hi = x.astype(BF16)
    r1 = x - hi.astype(F32)
    mid = r1.astype(BF16)
    lo = (r1 - mid.astype(F32)).astype(BF16)
    return hi, mid, lo


def _cumsum_impl(x):
    ax = x.ndim - 2
    n = x.shape[ax]
    row = lax.broadcasted_iota(jnp.int32, x.shape, ax)
    d = 1
    while d < n:
        x = x + jnp.where(row >= d, pltpu.roll(x, d, ax), 0.0)
        d *= 2
    return x


def _cumsum_rev_impl(x):
    ax = x.ndim - 2
    n = x.shape[ax]
    row = lax.broadcasted_iota(jnp.int32, x.shape, ax)
    d = 1
    while d < n:
        x = x + jnp.where(row < n - d, pltpu.roll(x, n - d, ax), 0.0)
        d *= 2
    return x


@jax.custom_vjp
def _cumsum(x):
    return _cumsum_impl(x)


_cumsum.defvjp(lambda x: (_cumsum_impl(x), None), lambda _, ct: (_cumsum_rev_impl(ct),))


def _matmul(a, b, *, mode, name, out_dtype=F32, add=None, add_scale=1.0, tm=512, tn=1408, tk=1408, after=None,
            split_n=False, planes=None, also_bf16=False):
    P = b.shape[0] if planes else 1
    a2, b2 = a.shape[-2:], b.shape[-2:]
    (M, K) = a2 if mode[0] == "n" else a2[::-1]
    (K2, N) = b2 if mode[1] == "n" else b2[::-1]
    assert K == K2, (a.shape, b.shape, mode)
    assert a.ndim == (3 if planes == "k" else 2) and b.ndim == (3 if planes else 2)
    tm, tn, tk = _tile(M, tm), _tile(N, tn), _tile(K, tk)
    nj, nkp = N // tn, K // tk
    nk = nkp * (P if planes == "k" else 1)
    ca, cb = (1 if mode[0] == "n" else 0), (0 if mode[1] == "n" else 1)
    a_blk, a_idx = ((tk, tm), lambda i, k: (k, i)) if mode[0] == "t" else ((tm, tk), lambda i, k: (i, k))
    b_blk, b_idx = ((tn, tk), lambda k, j: (j, k)) if mode[1] == "t" else ((tk, tn), lambda k, j: (k, j))
    if planes == "k":
        a_spec = pl.BlockSpec((None,) + a_blk, lambda i, j, k: (k // nkp,) + a_idx(i, k % nkp))
        b_spec = pl.BlockSpec((None,) + b_blk, lambda i, j, k: (k // nkp,) + b_idx(k % nkp, j))
    else:
        a_spec = pl.BlockSpec(a_blk, lambda i, j, k: a_idx(i, k))
        b_spec = (pl.BlockSpec((None,) + b_blk, lambda i, j, k: (j // nj,) + b_idx(k, j % nj)) if planes == "n"
                  else pl.BlockSpec(b_blk, lambda i, j, k: b_idx(k, j)))
    if split_n:
        o_spec, out_shape = pl.BlockSpec((None, tm, tn), lambda i, j, k: (j, i, 0)), (P * nj if planes == "n" else nj, M, tn)
    elif planes == "n":
        o_spec, out_shape = pl.BlockSpec((None, tm, tn), lambda i, j, k: (j // nj, i, j % nj)), (P, M, N)
    else:
        o_spec, out_shape = pl.BlockSpec((tm, tn), lambda i, j, k: (i, j)), (M, N)
    has_add = add is not None
    assert not (has_add and (split_n or planes == "n"))

    def finish(r, add_ref, o_refs):
        if has_add:
            r = r + add_scale * add_ref[...]
        o_refs[0][...] = r.astype(out_dtype)
        if also_bf16:
            o_refs[1][...] = r.astype(BF16)

    def body(*refs):
        a_ref, b_ref = refs[:2]
        add_ref = refs[2] if has_add else None
        first = 3 if has_add else 2
        o_ref = refs[first:first + (2 if also_bf16 else 1)]
        if nk == 1:
            finish(_dot(a_ref[...], b_ref[...], ca, cb), add_ref, o_ref)
            return
        acc_ref = refs[-1]
        k = pl.program_id(2)

        @pl.when(k == 0)
        def _():
            acc_ref[...] = jnp.zeros_like(acc_ref)

        acc_ref[...] += _dot(a_ref[...], b_ref[...], ca, cb)

        @pl.when(k == nk - 1)
        def _():
            finish(acc_ref[...], add_ref, o_ref)

    in_specs = [a_spec, b_spec] + ([o_spec] if has_add else [])
    args = (a, b) + ((add,) if has_add else ())
    body, xs, xa = _after(body, len(args), after)
    in_specs, args = in_specs + xs, args + xa
    out_shapes = [jax.ShapeDtypeStruct(out_shape, out_dtype)] + ([jax.ShapeDtypeStruct(out_shape, BF16)] if also_bf16 else [])
    out = pl.pallas_call(
        body, name=name, grid=(M // tm, nj * (P if planes == "n" else 1), nk), in_specs=in_specs,
        out_specs=[o_spec] * len(out_shapes), out_shape=out_shapes,
        scratch_shapes=[pltpu.VMEM((tm, tn), F32)] if nk > 1 else [],
        compiler_params=_params(("parallel", "parallel", "arbitrary")),
    )(*args)
    return tuple(out) if also_bf16 else out[0]


def _matmul_planes_nt(a, b, add, *, add_scale, name, tm=512, after=None):
    (P, M, K), (P2, N, K2) = a.shape, b.shape
    assert P == P2 and K == K2 and add.shape == (M, N)
    tm = _tile(M, tm, SUBLANES)

    def body(a_ref, b_ref, add_ref, o_ref):
        r = add_scale * add_ref[...]
        for p in range(P):
            r = r + _dot(a_ref[p], b_ref[p], 1, 1)
        o_ref[...] = r

    row = pl.BlockSpec((tm, N), lambda i: (i, 0))
    body, xs, xa = _after(body, 3, after)
    return pl.pallas_call(
        body, name=name, grid=(M // tm,),
        in_specs=[pl.BlockSpec((P, tm, K), lambda i: (0, i, 0)), pl.BlockSpec((P, N, K), lambda i: (0, 0, 0)), row] + xs,
        out_specs=row, out_shape=jax.ShapeDtypeStruct((M, N), F32),
        compiler_params=_params(("parallel",)),
    )(a, b, add, *xa)


def _ln(z, g, b):
    mu = jnp.mean(z, axis=-1, keepdims=True)
    zc = z - mu
    var = jnp.mean(zc * zc, axis=-1, keepdims=True)
    return zc * lax.rsqrt(var + LN_EPS) * g + b


def _matmul_ln(a, b, h, g, bias, *, name, tgt=None, tm=512, a_t=False):
    (T, K), (K2, Dm) = (a.shape[::-1] if a_t else a.shape), b.shape
    assert K == K2 and h.shape == (T, Dm)
    tm = _tile(T, tm, SUBLANES)
    last = tgt is not None

    def body(*refs):
        a_ref, b_ref, h_ref, g_ref, bias_ref = refs[:5]
        z = ALPHA * h_ref[...] + _dot(a_ref[...], b_ref[...], 0 if a_t else 1, 0)
        if not last:
            z_ref, y_ref, yb_ref = refs[5:]
            y = _ln(z, g_ref[...], bias_ref[...])
            z_ref[...] = z
            y_ref[...] = y
            yb_ref[...] = y.astype(BF16)
            return
        t_ref, dz_ref, dzb_ref, dgb_ref, l_ref, da_ref = refs[5:]

        @pl.when(pl.program_id(0) == 0)
        def _():
            dgb_ref[...] = jnp.zeros_like(dgb_ref)
            l_ref[...] = jnp.zeros_like(l_ref)

        y, vjp = jax.vjp(_ln, z, g_ref[...], bias_ref[...])
        e = y - t_ref[...]
        dz, dg, db = vjp(e * (1.0 / Dm))
        l_ref[...] += 0.5 * jnp.sum(jnp.mean(e * e, axis=-1, keepdims=True), axis=0, keepdims=True)
        dzb = dz.astype(BF16)
        dz_ref[...] = dz
        dzb_ref[...] = dzb
        dgb_ref[...] += jnp.concatenate([dg, db], axis=0)
        da_ref[...] = _dot(dzb, b_ref[...], 1, 1).astype(BF16)

    row = pl.BlockSpec((tm, Dm), lambda i: (i, 0))
    vec = pl.BlockSpec((1, Dm), lambda i: (0, 0))
    a_spec = pl.BlockSpec((K, tm), lambda i: (0, i)) if a_t else pl.BlockSpec((tm, K), lambda i: (i, 0))
    in_specs = [a_spec, pl.BlockSpec((K, Dm), lambda i: (0, 0)), row, vec, vec]
    f32, b16 = jax.ShapeDtypeStruct((T, Dm), F32), jax.ShapeDtypeStruct((T, Dm), BF16)
    if not last:
        return pl.pallas_call(
            body, name=name, grid=(T // tm,), in_specs=in_specs, out_specs=[row, row, row], out_shape=[f32, f32, b16],
            compiler_params=_params(("parallel",)),
        )(a, b, h, g, bias)
    assert not a_t
    return pl.pallas_call(
        body, name=name, grid=(T // tm,), in_specs=in_specs + [row],
        out_specs=[row, row, pl.BlockSpec((2, Dm), lambda i: (0, 0)), pl.BlockSpec((SUBLANES, LANES), lambda i: (0, 0)), a_spec],
        out_shape=[f32, b16, jax.ShapeDtypeStruct((2, Dm), F32), jax.ShapeDtypeStruct((SUBLANES, LANES), F32),
                   jax.ShapeDtypeStruct((T, K), BF16)],
        compiler_params=_params(("arbitrary",)),
    )(a, b, h, g, bias, tgt)


def _ln_bwd_matmul(dy, z, g, b, w, *, name, out_t=False, tm=512, after=None):
    T, Dm = z.shape
    N = w.shape[0]
    tm = _tile(T, tm, LANES if out_t else SUBLANES)

    def body(dy_ref, z_ref, g_ref, b_ref, w_ref, dz_ref, dzb_ref, dgb_ref, o_ref):
        @pl.when(pl.program_id(0) == 0)
        def _():
            dgb_ref[...] = jnp.zeros_like(dgb_ref)

        _, vjp = jax.vjp(_ln, z_ref[...], g_ref[...], b_ref[...])
        dz, dg, db = vjp(dy_ref[...])
        dzb = dz.astype(BF16)
        dz_ref[...] = dz
        dzb_ref[...] = dzb
        dgb_ref[...] += jnp.concatenate([dg, db], axis=0)
        o_ref[...] = (_dot(w_ref[...], dzb, 1, 1) if out_t else _dot(dzb, w_ref[...], 1, 1)).astype(BF16)

    row = pl.BlockSpec((tm, Dm), lambda i: (i, 0))
    vec = pl.BlockSpec((1, Dm), lambda i: (0, 0))
    o_spec = pl.BlockSpec((N, tm), lambda i: (0, i)) if out_t else pl.BlockSpec((tm, N), lambda i: (i, 0))
    body, xs, xa = _after(body, 5, after)
    return pl.pallas_call(
        body, name=name, grid=(T // tm,), in_specs=[row, row, vec, vec, pl.BlockSpec((N, Dm), lambda i: (0, 0))] + xs,
        out_specs=[row, row, pl.BlockSpec((2, Dm), lambda i: (0, 0)), o_spec],
        out_shape=[jax.ShapeDtypeStruct((T, Dm), F32), jax.ShapeDtypeStruct((T, Dm), BF16),
                   jax.ShapeDtypeStruct((2, Dm), F32), jax.ShapeDtypeStruct((N, T) if out_t else (T, N), BF16)],
        compiler_params=_params(("arbitrary",)),
    )(dy, z, g, b, w, *xa)


def _hg_chunk(qr, fr, ir, gr, l0, l1, gw, st):
    C = qr.shape[-2]
    row = lax.broadcasted_iota(jnp.int32, qr.shape, qr.ndim - 2)
    lb = jax.nn.sigmoid(l0 - l1)
    fg = lb + (1.0 - lb) * jax.nn.sigmoid(fr)
    b = _cumsum(jnp.log(fg))
    q = jax.nn.silu(qr)
    k = 1.0 - fg
    bmid = lax.stop_gradient(jnp.sum(jnp.where(row == C // 2 - 1, b, 0.0), axis=-2, keepdims=True))
    bl = jnp.sum(jnp.where(row == C - 1, b, 0.0), axis=-2, keepdims=True)
    o = mm_nt(q * jnp.exp(b), st)
    sc = _scores(q * jnp.exp(b - bmid), k * jnp.exp(bmid - b))
    ti = lax.broadcasted_iota(jnp.int32, (C, C), 0)
    si = lax.broadcasted_iota(jnp.int32, (C, C), 1)
    sc = jnp.where(si <= ti, sc, 0.0)
    o = o + mm(sc, ir)
    st_new = st * jnp.exp(bl) + mm_tn(ir, k * jnp.exp(bl - b))
    on = o * lax.rsqrt(jnp.mean(o * o, axis=-1, keepdims=True) + RMS_EPS)
    return on * gw * jax.nn.silu(gr), st_new


def _heads(ref, rows):
    return jnp.stack([ref[rows, h * HG_DIM:(h + 1) * HG_DIM].astype(F32) for h in range(HG_HEADS)])


def _unheads(x):
    return jnp.concatenate([x[h] for h in range(HG_HEADS)], axis=-1)


def _hgrn_fwd(pre, lbl, gw, *, name):
    _, T, Dm = pre.shape
    rb = min(HG_RB, T)
    C = min(HG_C, rb)
    ncb = rb // C

    def body(pre_ref, lbl_ref, gw_ref, o_ref, st_ref, s_ref):
        @pl.when(pl.program_id(0) == 0)
        def _():
            s_ref[...] = jnp.zeros_like(s_ref)

        def chunk(ci, carry):
            r0 = pl.multiple_of(ci * C, C)
            rows = pl.ds(r0, C)
            st = s_ref[...]
            st_ref[ci] = st
            out, st_new = _hg_chunk(*[_heads(pre_ref.at[j], rows) for j in range(4)],
                                    _heads(lbl_ref, slice(0, 1)), _heads(lbl_ref, slice(1, 2)), gw_ref[...], st)
            o_ref[rows, :] = _unheads(out).astype(BF16)
            s_ref[...] = st_new
            return carry

        lax.fori_loop(0, ncb, chunk, 0, unroll=True)

    row = pl.BlockSpec((rb, Dm), lambda n: (n, 0))
    return pl.pallas_call(
        body, name=name, grid=(T // rb,),
        in_specs=[pl.BlockSpec((4, rb, Dm), lambda n: (0, n, 0)), pl.BlockSpec((2, Dm), lambda n: (0, 0)),
                  pl.BlockSpec((1, HG_DIM), lambda n: (0, 0))],
        out_specs=[row, pl.BlockSpec((ncb, HG_HEADS, HG_DIM, HG_DIM), lambda n: (n, 0, 0, 0))],
        out_shape=[jax.ShapeDtypeStruct((T, Dm), BF16),
                   jax.ShapeDtypeStruct((T // C, HG_HEADS, HG_DIM, HG_DIM), F32)],
        scratch_shapes=[pltpu.VMEM((HG_HEADS, HG_DIM, HG_DIM), F32)],
        compiler_params=_params(("arbitrary",)),
    )(pre, lbl, gw)


def _hgrn_bwd(pre, lbl, gw, states, dout, *, name, after=None):
    _, T, Dm = pre.shape
    rb = min(HG_RB, T)
    C = min(HG_C, rb)
    ncb = rb // C
    nb = T // rb

    def body(pre_ref, lbl_ref, gw_ref, st_ref, do_ref, dpre_ref, dlbl_ref, dgw_ref, ds_ref):
        @pl.when(pl.program_id(0) == 0)
        def _():
            ds_ref[...] = jnp.zeros_like(ds_ref)
            dlbl_ref[...] = jnp.zeros_like(dlbl_ref)
            dgw_ref[...] = jnp.zeros_like(dgw_ref)

        def chunk(cj, carry):
            ci = ncb - 1 - cj
            r0 = pl.multiple_of(ci * C, C)
            rows = pl.ds(r0, C)
            _, vjp = jax.vjp(_hg_chunk, *[_heads(pre_ref.at[j], rows) for j in range(4)],
                             _heads(lbl_ref, slice(0, 1)), _heads(lbl_ref, slice(1, 2)), gw_ref[...], st_ref[ci])
            *dpre, dl0, dl1, dgw, dst = vjp((_heads(do_ref, rows), ds_ref[...]))
            for j in range(4):
                dpre_ref[j, rows, :] = _unheads(dpre[j]).astype(BF16)
            dlbl_ref[0:1, :] += _unheads(dl0)
            dlbl_ref[1:2, :] += _unheads(dl1)
            dgw_ref[...] += dgw
            ds_ref[...] = dst
            return carry

        lax.fori_loop(0, ncb, chunk, 0, unroll=True)

    row = pl.BlockSpec((rb, Dm), lambda n: (nb - 1 - n, 0))
    lsp = pl.BlockSpec((2, Dm), lambda n: (0, 0))
    gsp = pl.BlockSpec((1, HG_DIM), lambda n: (0, 0))
    pre_spec = pl.BlockSpec((4, rb, Dm), lambda n: (0, nb - 1 - n, 0))
    body, xs, xa = _after(body, 5, after)
    return pl.pallas_call(
        body, name=name, grid=(nb,),
        in_specs=[pre_spec, lsp, gsp, pl.BlockSpec((ncb, HG_HEADS, HG_DIM, HG_DIM), lambda n: (nb - 1 - n, 0, 0, 0)), row] + xs,
        out_specs=[pre_spec, lsp, gsp],
        out_shape=[jax.ShapeDtypeStruct((4, T, Dm), BF16), jax.ShapeDtypeStruct((2, Dm), F32),
                   jax.ShapeDtypeStruct((1, HG_DIM), F32)],
        scratch_shapes=[pltpu.VMEM((HG_HEADS, HG_DIM, HG_DIM), F32)],
        compiler_params=_params(("arbitrary",)),
    )(pre, lbl, gw, states, dout, *xa)


CONV_HALO = 2 * SUBLANES


def _conv_rows(u_ref, scr, w, bias, r0, R):
    cur = u_ref[pl.ds(r0, R), :].astype(F32)
    p0 = pl.multiple_of(jnp.maximum(r0 - CONV_HALO, 0), CONV_HALO)
    scr[0:CONV_HALO, :] = jnp.where(r0 > 0, u_ref[pl.ds(p0, CONV_HALO), :].astype(F32), 0.0)
    scr[CONV_HALO:CONV_HALO + R, :] = cur
    s1 = scr[CONV_HALO - 1:CONV_HALO - 1 + R, :]
    s2 = scr[CONV_HALO - 2:CONV_HALO - 2 + R, :]
    return w[0:1, :] * s2 + w[1:2, :] * s1 + w[2:3, :] * cur + bias, cur, s1, s2


def _halves_spec(T, Fd):
    per = Fd // 2 // LANES
    return pl.BlockSpec((2, None, T, LANES), lambda j: (0, j // per, 0, j % per))


def _conv_gate_fwd(u, wa, wb, ba, bb, *, name):
    T, Fd = u.shape[2], 2 * u.shape[3]
    R = min(CONV_R, T)
    tc = LANES

    def body(u_ref, wa_ref, wb_ref, ba_ref, bb_ref, o_ref, sa, sb):
        wa_, wb_, ba_, bb_ = wa_ref[...], wb_ref[...], ba_ref[...], bb_ref[...]

        def step(ri, carry):
            r0 = pl.multiple_of(ri * R, R)
            ca = _conv_rows(u_ref.at[0], sa, wa_, ba_, r0, R)[0]
            cb = _conv_rows(u_ref.at[1], sb, wb_, bb_, r0, R)[0]
            o_ref[pl.ds(r0, R), :] = (jax.nn.silu(ca) * cb).astype(BF16)
            return carry

        lax.fori_loop(0, T // R, step, 0)

    col = pl.BlockSpec((T, tc), lambda j: (0, j))
    wsp = pl.BlockSpec((3, tc), lambda j: (0, j))
    bsp = pl.BlockSpec((1, tc), lambda j: (0, j))
    both = _halves_spec(T, Fd)
    return pl.pallas_call(
        body, name=name, grid=(Fd // tc,), in_specs=[both, wsp, wsp, bsp, bsp], out_specs=col,
        out_shape=jax.ShapeDtypeStruct((T, Fd), BF16),
        scratch_shapes=[pltpu.VMEM((CONV_HALO + R, tc), F32)] * 2,
        compiler_params=_params(("parallel",)),
    )(u, wa, wb, ba, bb)


def _conv_gate_bwd(u, wa, wb, ba, bb, dact, *, name):
    T, Fd = u.shape[2], 2 * u.shape[3]
    R = min(CONV_R, T)
    nr = T // R
    tc = LANES

    def body(u_ref, wa_ref, wb_ref, ba_ref, bb_ref, da_ref,
             du_ref, dp_ref, sa, sb, sda, sdb):
        wa_, wb_, ba_, bb_ = wa_ref[...], wb_ref[...], ba_ref[...], bb_ref[...]
        sda[R:R + SUBLANES, :] = jnp.zeros((SUBLANES, tc), F32)
        sdb[R:R + SUBLANES, :] = jnp.zeros((SUBLANES, tc), F32)

        def taps(dc, cur, s1, s2):
            return jnp.concatenate([jnp.sum(dc * s2, axis=0, keepdims=True), jnp.sum(dc * s1, axis=0, keepdims=True),
                                    jnp.sum(dc * cur, axis=0, keepdims=True)], axis=0)

        def du_rows(sd, dc, w):
            sd[0:R, :] = dc
            du = w[2:3, :] * dc + w[1:2, :] * sd[1:1 + R, :] + w[0:1, :] * sd[2:2 + R, :]
            sd[R:R + SUBLANES, :] = dc[0:SUBLANES]
            return du

        def step(rj, carry):
            dwa, dwb, dba, dbb = carry
            r0 = pl.multiple_of((nr - 1 - rj) * R, R)
            ca, cura, s1a, s2a = _conv_rows(u_ref.at[0], sa, wa_, ba_, r0, R)
            cb, curb, s1b, s2b = _conv_rows(u_ref.at[1], sb, wb_, bb_, r0, R)
            dact_ = da_ref[pl.ds(r0, R), :].astype(F32)
            sg = jax.nn.sigmoid(ca)
            dca = dact_ * cb * (sg * (1.0 + ca * (1.0 - sg)))
            dcb = dact_ * (ca * sg)
            du_ref[0, pl.ds(r0, R), :] = du_rows(sda, dca, wa_).astype(BF16)
            du_ref[1, pl.ds(r0, R), :] = du_rows(sdb, dcb, wb_).astype(BF16)
            return (dwa + taps(dca, cura, s1a, s2a), dwb + taps(dcb, curb, s1b, s2b),
                    dba + jnp.sum(dca, axis=0, keepdims=True), dbb + jnp.sum(dcb, axis=0, keepdims=True))

        z3 = jnp.zeros((3, tc), F32)
        z1 = jnp.zeros((1, tc), F32)
        dwa, dwb, dba, dbb = lax.fori_loop(0, nr, step, (z3, z3, z1, z1))
        dp_ref[0] = jnp.concatenate([dwa, dba], axis=0)
        dp_ref[1] = jnp.concatenate([dwb, dbb], axis=0)

    col = pl.BlockSpec((T, tc), lambda j: (0, j))
    wsp = pl.BlockSpec((3, tc), lambda j: (0, j))
    bsp = pl.BlockSpec((1, tc), lambda j: (0, j))
    both = _halves_spec(T, Fd)
    return pl.pallas_call(
        body, name=name, grid=(Fd // tc,), in_specs=[both, wsp, wsp, bsp, bsp, col],
        out_specs=[both, pl.BlockSpec((2, 4, tc), lambda j: (0, 0, j))],
        out_shape=[jax.ShapeDtypeStruct(u.shape, BF16), jax.ShapeDtypeStruct((2, 4, Fd), F32)],
        scratch_shapes=[pltpu.VMEM((CONV_HALO + R, tc), F32)] * 2 + [pltpu.VMEM((R + SUBLANES, tc), F32)] * 2,
        compiler_params=_params(("parallel",)),
    )(u, wa, wb, ba, bb, dact)


def _bucket_index():
    t = np.arange(SW_WINDOW)[None, :] + SW_WINDOW
    s = np.arange(2 * SW_WINDOW)[:, None]
    dist = np.maximum(t - s, 0)
    exact = REL_BUCKETS // 2
    d = np.maximum(dist, 1).astype(np.float32)
    log_b = exact + (np.log(d / np.float32(exact)) / np.float32(math.log(REL_MAX_DIST / exact))
                     * np.float32(REL_BUCKETS - exact)).astype(np.int32)
    bucket = np.where(dist < exact, dist, np.minimum(log_b, REL_BUCKETS - 1))
    return bucket.astype(np.int32).reshape(1, -1)


BIAS_COLS = SW_WINDOW * 2 * SW_WINDOW
BIAS_TILE = 4096


def _bias_from_table(table, bucket, *, name):
    def body(t_ref, idx_ref, o_ref):
        onehot = (lax.broadcasted_iota(jnp.int32, (REL_BUCKETS, BIAS_TILE), 0) == idx_ref[...]).astype(BF16)
        acc = jnp.zeros((SW_Q_HEADS, BIAS_TILE), F32)
        for piece in _split3(t_ref[...]):
            acc = acc + lax.dot_general(piece, onehot, (((0,), (0,)), ((), ())), preferred_element_type=F32)
        o_ref[...] = acc

    return pl.pallas_call(
        body, name=name, grid=(BIAS_COLS // BIAS_TILE,),
        in_specs=[pl.BlockSpec((REL_BUCKETS, SW_Q_HEADS), lambda j: (0, 0)), pl.BlockSpec((1, BIAS_TILE), lambda j: (0, j))],
        out_specs=pl.BlockSpec((SW_Q_HEADS, BIAS_TILE), lambda j: (0, j)),
        out_shape=jax.ShapeDtypeStruct((SW_Q_HEADS, BIAS_COLS), F32),
        compiler_params=_params(("parallel",)),
    )(table, bucket)


def _table_grad(dbias, bucket, *, name):
    def body(d_ref, idx_ref, o_ref):
        @pl.when(pl.program_id(0) == 0)
        def _():
            o_ref[...] = jnp.zeros_like(o_ref)

        onehot = (lax.broadcasted_iota(jnp.int32, (REL_BUCKETS, BIAS_TILE), 0) == idx_ref[...]).astype(BF16)
        acc = jnp.zeros((REL_BUCKETS, SW_Q_HEADS), F32)
        for piece in _split3(d_ref[...]):
            acc = acc + lax.dot_general(onehot, piece, (((1,), (1,)), ((), ())), preferred_element_type=F32)
        o_ref[...] += acc

    return pl.pallas_call(
        body, name=name, grid=(BIAS_COLS // BIAS_TILE,),
        in_specs=[pl.BlockSpec((SW_Q_HEADS, BIAS_TILE), lambda j: (0, j)), pl.BlockSpec((1, BIAS_TILE), lambda j: (0, j))],
        out_specs=pl.BlockSpec((REL_BUCKETS, SW_Q_HEADS), lambda j: (0, 0)),
        out_shape=jax.ShapeDtypeStruct((REL_BUCKETS, SW_Q_HEADS), F32),
        compiler_params=_params(("arbitrary",)),
    )(dbias, bucket)


KV_DIM = SW_KV_HEADS * SW_HEAD_DIM
GROUP_ROWS = SW_GROUP * SW_HEAD_DIM
GROUP_LANES = SW_GROUP * SW_WINDOW


def _band_mask(n):
    s = lax.broadcasted_iota(jnp.int32, (2 * SW_WINDOW, GROUP_LANES), 0)
    t = (lax.broadcasted_iota(jnp.int32, (2 * SW_WINDOW, GROUP_LANES), 1) & (SW_WINDOW - 1)) + SW_WINDOW
    dist = t - s
    return (dist >= 0) & (dist < SW_WINDOW) & ((n > 0) | (s >= SW_WINDOW))


def _side_by_side(x_ref, g):
    r0 = g * GROUP_ROWS
    return jnp.concatenate([x_ref[r0 + r * SW_HEAD_DIM:r0 + (r + 1) * SW_HEAD_DIM, :] for r in range(SW_GROUP)], axis=1)


def _group_inputs(bias_ref, sink_ref, g):
    heads = range(g * SW_GROUP, (g + 1) * SW_GROUP)
    bias = jnp.concatenate([bias_ref[h] for h in heads], axis=1)
    sink = jnp.concatenate([jnp.broadcast_to(sink_ref[:, h:h + 1], (1, SW_WINDOW)) for h in heads], axis=1)
    return heads, bias, sink


def _kv_pair(kvp_ref, kvc_ref, g):
    ks = slice(g * SW_HEAD_DIM, (g + 1) * SW_HEAD_DIM)
    vs = slice(KV_DIM + g * SW_HEAD_DIM, KV_DIM + (g + 1) * SW_HEAD_DIM)
    kk = jnp.concatenate([kvp_ref[:, ks], kvc_ref[:, ks]], axis=0)
    vv = jnp.concatenate([kvp_ref[:, vs], kvc_ref[:, vs]], axis=0)
    return kk, vv, ks, vs


def _col_max(x):
    return jnp.max(x, axis=0, keepdims=True)


def _col_sum(x):
    return jnp.sum(x, axis=0, keepdims=True)


def _attn_fwd(qt, kv, bias, sinks, *, name):
    Dm, T = qt.shape
    W = SW_WINDOW

    def body(q_ref, kvc_ref, kvp_ref, bias_ref, sink_ref, o_ref):
        mask = _band_mask(pl.program_id(0))
        G = range(SW_KV_HEADS)
        ins = [_group_inputs(bias_ref, sink_ref, g) for g in G]
        kvs = [_kv_pair(kvp_ref, kvc_ref, g) for g in G]
        q = [_side_by_side(q_ref, g) for g in G]
        lg = [jnp.where(mask, mm(kvs[g][0], q[g]) * (SW_HEAD_DIM ** -0.5) + ins[g][1], -jnp.inf) for g in G]
        m = [jnp.maximum(_col_max(lg[g]), ins[g][2]) for g in G]
        p = [jnp.exp(lg[g] - m[g]) for g in G]
        den = [_col_sum(p[g]) + jnp.exp(ins[g][2] - m[g]) for g in G]
        o = [mm_tn(kvs[g][1], p[g]) / den[g] for g in G]
        for g in G:
            for r in range(SW_GROUP):
                o_ref[g * GROUP_ROWS + r * SW_HEAD_DIM:g * GROUP_ROWS + (r + 1) * SW_HEAD_DIM, :] = (
                    o[g][:, r * W:(r + 1) * W].astype(BF16))

    return pl.pallas_call(
        body, name=name, grid=(T // W,),
        in_specs=[pl.BlockSpec((Dm, W), lambda n: (0, n)),
                  pl.BlockSpec((W, 2 * KV_DIM), lambda n: (n, 0)),
                  pl.BlockSpec((W, 2 * KV_DIM), lambda n: (jnp.maximum(n - 1, 0), 0)),
                  pl.BlockSpec((SW_Q_HEADS, 2 * W, W), lambda n: (0, 0, 0)),
                  pl.BlockSpec((1, SW_Q_HEADS), lambda n: (0, 0))],
        out_specs=pl.BlockSpec((Dm, W), lambda n: (0, n)),
        out_shape=jax.ShapeDtypeStruct((Dm, T), BF16),
        compiler_params=_params(("parallel",)),
    )(qt, kv, kv, bias, sinks)


def _attn_bwd(qt, kv, bias, sinks, dot, *, name):
    Dm, T = qt.shape
    W = SW_WINDOW
    nb = T // W

    def body(q_ref, kvc_ref, kvp_ref, bias_ref, sink_ref, do_ref,
             dq_ref, dkv_ref, dbias_ref, dsink_ref, carry_ref):
        @pl.when(pl.program_id(0) == 0)
        def _():
            carry_ref[...] = jnp.zeros_like(carry_ref)
            dbias_ref[...] = jnp.zeros_like(dbias_ref)
            dsink_ref[...] = jnp.zeros_like(dsink_ref)

        n = nb - 1 - pl.program_id(0)
        mask = _band_mask(n)
        lane = lax.broadcasted_iota(jnp.int32, (1, SW_Q_HEADS), 1)
        sc = SW_HEAD_DIM ** -0.5
        G = range(SW_KV_HEADS)
        ins = [_group_inputs(bias_ref, sink_ref, g) for g in G]
        kvs = [_kv_pair(kvp_ref, kvc_ref, g) for g in G]
        q = [_side_by_side(q_ref, g) for g in G]
        do = [_side_by_side(do_ref, g) for g in G]
        lg = [jnp.where(mask, mm(kvs[g][0], q[g]) * sc + ins[g][1], -jnp.inf) for g in G]
        m = [jnp.maximum(_col_max(lg[g]), ins[g][2]) for g in G]
        p = [jnp.exp(lg[g] - m[g]) for g in G]
        ps = [jnp.exp(ins[g][2] - m[g]) for g in G]
        rden = [1.0 / (_col_sum(p[g]) + ps[g]) for g in G]
        pn = [p[g] * rden[g] for g in G]
        dpn = [mm(kvs[g][1], do[g]) for g in G]
        delta = [_col_sum(pn[g] * dpn[g]) for g in G]
        ds = [pn[g] * (dpn[g] - delta[g]) for g in G]
        dsr = [-(ps[g] * rden[g]) * delta[g] for g in G]
        dq = [mm_tn(kvs[g][0], ds[g]) * sc for g in G]
        dkk = [mm_nt(ds[g], q[g]) * sc for g in G]
        dvv = [mm_nt(pn[g], do[g]) for g in G]
        dsink = jnp.zeros((1, SW_Q_HEADS), F32)
        for g in G:
            _, _, ks, vs = kvs[g]
            for r, h in enumerate(ins[g][0]):
                cols = slice(r * W, (r + 1) * W)
                dbias_ref[h] += ds[g][:, cols]
                dq_ref[g * GROUP_ROWS + r * SW_HEAD_DIM:g * GROUP_ROWS + (r + 1) * SW_HEAD_DIM, :] = dq[g][:, cols].astype(BF16)
                dsink = dsink + jnp.where(lane == h, jnp.sum(dsr[g][:, cols], axis=1, keepdims=True), 0.0)
            dkv_ref[:, ks] = (carry_ref[:, ks] + dkk[g][W:]).astype(BF16)
            dkv_ref[:, vs] = (carry_ref[:, vs] + dvv[g][W:]).astype(BF16)
            carry_ref[:, ks] = dkk[g][:W]
            carry_ref[:, vs] = dvv[g][:W]
        dsink_ref[...] += dsink

    rev = lambda n: (nb - 1 - n, 0)
    revt = lambda n: (0, nb - 1 - n)
    return pl.pallas_call(
        body, name=name, grid=(nb,),
        in_specs=[pl.BlockSpec((Dm, W), revt),
                  pl.BlockSpec((W, 2 * KV_DIM), rev),
                  pl.BlockSpec((W, 2 * KV_DIM), lambda n: (jnp.maximum(nb - 2 - n, 0), 0)),
                  pl.BlockSpec((SW_Q_HEADS, 2 * W, W), lambda n: (0, 0, 0)),
                  pl.BlockSpec((1, SW_Q_HEADS), lambda n: (0, 0)),
                  pl.BlockSpec((Dm, W), revt)],
        out_specs=[pl.BlockSpec((Dm, W), revt), pl.BlockSpec((W, 2 * KV_DIM), rev),
                   pl.BlockSpec((SW_Q_HEADS, 2 * W, W), lambda n: (0, 0, 0)),
                   pl.BlockSpec((1, SW_Q_HEADS), lambda n: (0, 0))],
        out_shape=[jax.ShapeDtypeStruct((Dm, T), BF16), jax.ShapeDtypeStruct((T, 2 * KV_DIM), BF16),
                   jax.ShapeDtypeStruct((SW_Q_HEADS, 2 * W, W), F32), jax.ShapeDtypeStruct((1, SW_Q_HEADS), F32)],
        scratch_shapes=[pltpu.VMEM((W, 2 * KV_DIM), F32)],
        compiler_params=_params(("arbitrary",)),
    )(qt, kv, kv, bias, sinks, dot)


def _ffn_fwd(hb, w, l, after=None):
    u = _matmul(hb, w["ffn_in"][l], mode="nn", planes="n", out_dtype=BF16, name=f"ffn{l}_up", tm=1024, after=after)
    u = u.reshape((2, 2) + u.shape[1:])
    act = _conv_gate_fwd(u, w["conv_w_a"][l], w["conv_w_b"][l], w["conv_b_a"][l], w["conv_b_b"][l],
                         name=f"ffn{l}_conv_gate")
    return u, act


def _ffn_bwd(dffb, dh_scaled, hb, u, act, w, l, dact):
    g_out = _matmul(act, dffb, mode="tn", name=f"ffn{l}_down_dw", tm=1408, tn=1024, tk=1024, also_bf16=True)
    du, g_conv = _conv_gate_bwd(u, w["conv_w_a"][l], w["conv_w_b"][l], w["conv_b_a"][l], w["conv_b_b"][l],
                                dact, name=f"ffn{l}_conv_gate_bwd")
    du = du.reshape((N_CHIPS,) + du.shape[2:])
    dh = _matmul_planes_nt(du, w["ffn_in"][l], dh_scaled, add_scale=ALPHA, name=f"ffn{l}_up_dx")
    g_in = _matmul(hb, du, mode="tn", planes="n", name=f"ffn{l}_up_dw", tm=1024, tn=FFN_DIM // 2, tk=1024, also_bf16=True)
    return dh, dict(ffn_out=g_out, ffn_in=g_in, conv=g_conv)


def _local_step(x, xb, tgt, w, more_weights, emit):
    bucket = jnp.asarray(_bucket_index())

    pre = _matmul(xb, w["hg_in"], mode="nn", planes="n", out_dtype=BF16, name="hg_in", tm=1024, tn=1024,
                  after=w.get("token"))
    og, states = _hgrn_fwd(pre, w["lb_logits"], w["gnorm"], name="hgrn_fwd")
    z1, h1, h1b = _matmul_ln(og, w["hg_out"], x, w["ln_mix_g"][0], w["ln_mix_b"][0], name="hg_out_ln")
    w = {**w, **more_weights(1, h1b)}
    u0, act0 = _ffn_fwd(h1b, w, 0, after=w.get("token"))
    z2, h2, h2b = _matmul_ln(act0, w["ffn_out"][0], h1, w["ln_ffn_g"][0], w["ln_ffn_b"][0], name="ffn0_down_ln")
    kv = _matmul(h2b, w["kv"], mode="nn", out_dtype=BF16, name="kv_proj")

    bias = _bias_from_table(w["rel_bias"], bucket, name="rel_bias_expand").reshape(SW_Q_HEADS, 2 * SW_WINDOW, SW_WINDOW)
    q1 = _matmul(w["sw_q"], h2b, mode="tt", out_dtype=BF16, name="sw_q", tm=1024, tn=1024)
    o1 = _attn_fwd(q1, kv, bias, w["sinks"], name="attn_fwd")
    z3, h3, h3b = _matmul_ln(o1, w["sw_out"], h2, w["ln_mix_g"][1], w["ln_mix_b"][1], a_t=True, name="sw_out_ln")
    w = {**w, **more_weights(2, h3b)}
    u1, act1 = _ffn_fwd(h3b, w, 1)

    g = {}
    dz, dzb, g["ln_ffn1"], loss_tile, dact1 = _matmul_ln(act1, w["ffn_out"][1], h3, w["ln_ffn_g"][1], w["ln_ffn_b"][1],
                                                         tgt=tgt, name="ffn1_down_ln_loss")

    dh3, gf1 = _ffn_bwd(dzb, dz, h3b, u1, act1, w, 1, dact1)
    dz, dzb, g["ln_mix1"], do1 = _ln_bwd_matmul(dh3, z3, w["ln_mix_g"][1], w["ln_mix_b"][1], w["sw_out"], out_t=True,
                                                name="ln_mix1_bwd_sw_out_dx")
    g_sw_out = _matmul(o1, dzb, mode="nn", name="sw_out_dw", tm=1024, tn=1024, tk=1024, also_bf16=True)
    dq1, dkv, dbias, dsinks = _attn_bwd(q1, kv, bias, w["sinks"], do1, name="attn_bwd")
    g["sinks"] = dsinks
    g["rel_bias"] = _table_grad(dbias.reshape(SW_Q_HEADS, BIAS_COLS), bucket, name="rel_bias_grad")
    dh2 = _matmul(dq1, w["sw_q"], mode="tt", add=dz, add_scale=ALPHA, name="sw_q_dx", tn=1024)
    dh2 = _matmul(dkv, w["kv"], mode="nt", add=dh2, name="kv_dx", tn=1024)
    g_sw_q = _matmul(h2b, dq1, mode="tt", name="sw_q_dw", tm=1024, tn=1024, tk=1024, also_bf16=True)
    g_kv = _matmul(h2b, dkv, mode="tn", name="kv_dw", tm=1024, tn=512, tk=1024, also_bf16=True)
    tok = emit(1, dict(sw_q=g_sw_q, sw_out=g_sw_out, kv=g_kv, ffn_in=gf1["ffn_in"], ffn_out=gf1["ffn_out"]))

    dz, dzb, g["ln_ffn0"], dact0 = _ln_bwd_matmul(dh2, z2, w["ln_ffn_g"][0], w["ln_ffn_b"][0], w["ffn_out"][0],
                                                  name="ln_ffn0_bwd_down_dx", after=tok)
    dh1, gf0 = _ffn_bwd(dzb, dz, h1b, u0, act0, w, 0, dact0)
    dz, dzb, g["ln_mix0"], dog = _ln_bwd_matmul(dh1, z1, w["ln_mix_g"][0], w["ln_mix_b"][0], w["hg_out"],
                                                name="ln_mix0_bwd_hg_out_dx")
    g_hg_out = _matmul(og, dzb, mode="tn", name="hg_out_dw", tm=1024, tn=1024, tk=1024, also_bf16=True)
    tok = emit(2, dict(hg_out=g_hg_out, ffn_in=gf0["ffn_in"], ffn_out=gf0["ffn_out"]))
    dpre, g["lb_logits"], g["gnorm"] = _hgrn_bwd(pre, w["lb_logits"], w["gnorm"], states, dog, name="hgrn_bwd", after=tok)
    tok = emit(3, dict(hg_in=_matmul(xb, dpre, mode="tn", planes="n", name="hg_in_dw", tm=1024, tn=1024, tk=1024, also_bf16=True)))
    dx = _matmul_planes_nt(dpre, w["hg_in"], dz, add_scale=ALPHA, name="hg_in_dx", after=tok)
    g["conv0"], g["conv1"] = gf0["conv"], gf1["conv"]
    return loss_tile, dx, g


def _adamw(wt, ga, gb, m, v, *, name, rows=None, prev=None):
    R, Cc = wt.shape
    r0, n = rows if rows is not None else (0, R)
    tr = _tile(n, 256, SUBLANES) if n % SUBLANES == 0 else n
    assert r0 % tr == 0
    c1 = 1.0 - ADAM_B1 ** ADAM_STEP
    c2 = 1.0 - ADAM_B2 ** ADAM_STEP
    n_in = 5

    def body(*refs):
        w_ref, ga_ref, gb_ref, m_ref, v_ref = refs[:n_in]
        g_ = ga_ref[...] + gb_ref[...]
        g_ref, d_ref, nm_ref, nv_ref = refs[-4:]
        nm = ADAM_B1 * m_ref[...] + (1.0 - ADAM_B1) * g_
        nv = ADAM_B2 * v_ref[...] + (1.0 - ADAM_B2) * (g_ * g_)
        g_ref[...] = g_
        d_ref[...] = -ADAM_LR * ((nm / c1) / (jnp.sqrt(nv / c2) + ADAM_EPS) + ADAM_WD * w_ref[...])
        nm_ref[...] = nm
        nv_ref[...] = nv

    full = pl.BlockSpec((tr, Cc), lambda i: (i + r0 // tr, 0))
    part = pl.BlockSpec((tr, Cc), lambda i: (i, 0))
    args = (wt, ga, gb, m, v)
    in_specs = [full, part, part, full, full]
    aliases = {}
    if prev is not None:
        args, in_specs = args + tuple(prev), in_specs + [ANY_SPEC] * 4
        aliases = {n_in + t: t for t in range(4)}
    return pl.pallas_call(
        body, name=name, grid=(n // tr,), in_specs=in_specs, out_specs=[full] * 4,
        out_shape=[jax.ShapeDtypeStruct((R, Cc), F32)] * 4, input_output_aliases=aliases,
        compiler_params=_params(("parallel",)),
    )(*args)


def _adamw_small(ws, gs, ms, vs, *, name):
    n = len(ws)
    c1 = 1.0 - ADAM_B1 ** ADAM_STEP
    c2 = 1.0 - ADAM_B2 ** ADAM_STEP

    def body(*refs):
        w_refs, g_refs, m_refs, v_refs = (refs[k * n:(k + 1) * n] for k in range(4))
        d_refs, nm_refs, nv_refs = (refs[(4 + k) * n:(5 + k) * n] for k in range(3))
        for i in range(n):
            g_ = g_refs[i][...]
            nm = ADAM_B1 * m_refs[i][...] + (1.0 - ADAM_B1) * g_
            nv = ADAM_B2 * v_refs[i][...] + (1.0 - ADAM_B2) * (g_ * g_)
            d_refs[i][...] = -ADAM_LR * ((nm / c1) / (jnp.sqrt(nv / c2) + ADAM_EPS) + ADAM_WD * w_refs[i][...])
            nm_refs[i][...] = nm
            nv_refs[i][...] = nv

    vm = pl.BlockSpec(memory_space=pltpu.VMEM)
    out = pl.pallas_call(
        body, name=name, in_specs=[vm] * (4 * n), out_specs=[vm] * (3 * n),
        out_shape=[jax.ShapeDtypeStruct(w.shape, F32) for w in ws] * 3,
    )(*ws, *gs, *ms, *vs)
    return out[:n], out[n:2 * n], out[2 * n:]


HBM_SPEC = pl.BlockSpec(memory_space=pltpu.HBM)
SEM_SPEC = pl.BlockSpec(memory_space=pltpu.SEMAPHORE)
VMEM_SPEC = pl.BlockSpec(memory_space=pltpu.VMEM)
DATAFLOW = pltpu.SideEffectType.DATAFLOW_SIDE_EFFECTING


def _in_hbm(a):
    return pltpu.with_memory_space_constraint(a, pltpu.HBM)


def _place():
    return lax.axis_index("x"), lax.axis_index("y"), lax.axis_index("c")


def _other_chips(x, y):
    return [(1 - x, y), (x, 1 - y), (1 - x, 1 - y)]


def _sum8(vs, *, name):
    n = len(vs)

    def body(*refs):
        v_refs, all_refs, o_refs = refs[:n], refs[n:2 * n], refs[2 * n:3 * n]
        send_sems, recv_sems, local_sems = refs[3 * n:]
        x, y, c = _place()
        me, sibling = (x, y, c), (x, y, 1 - c)
        chips = _other_chips(x, y)

        def slot(i, px, py, pc):
            return all_refs[i].at[4 * px + 2 * py + pc]

        def copy(i, k, block, to, src=None):
            return pltpu.make_async_remote_copy(
                src_ref=slot(i, *block) if src is None else src, dst_ref=slot(i, *block),
                send_sem=send_sems.at[7 * i + k], recv_sem=recv_sems.at[7 * i + k], device_id=to, device_id_type=MESH)

        mine = [pltpu.make_async_copy(v_refs[i], slot(i, *me), local_sems.at[i]) for i in range(n)]
        for cp in mine:
            cp.start()
        first = [copy(i, 0, me, sibling, src=v_refs[i]) for i in range(n)]
        first += [copy(i, 1 + j, me, (*chip, c), src=v_refs[i]) for i in range(n) for j, chip in enumerate(chips)]
        for cp in first:
            cp.start()
        passed = []
        for i in range(n):
            for j, chip in enumerate(chips):
                copy(i, 1 + j, (*chip, c), me).wait_recv()
                passed.append(copy(i, 4 + j, (*chip, c), sibling))
                passed[-1].start()
        for i in range(n):
            copy(i, 0, sibling, me).wait_recv()
            for j, chip in enumerate(chips):
                copy(i, 4 + j, (*chip, 1 - c), me).wait_recv()
        for cp in first + passed:
            cp.wait_send()
        for cp in mine:
            cp.wait()
        for i in range(n):
            acc = all_refs[i][0]
            for d in range(1, N_DEV):
                acc = acc + all_refs[i][d]
            o_refs[i][...] = acc

    return pl.pallas_call(
        body, name=name, in_specs=[VMEM_SPEC] * n, out_specs=[VMEM_SPEC] * (2 * n),
        out_shape=[jax.ShapeDtypeStruct((N_DEV,) + v.shape, F32) for v in vs] + [jax.ShapeDtypeStruct(v.shape, F32) for v in vs],
        scratch_shapes=[pltpu.SemaphoreType.DMA((7 * n,)), pltpu.SemaphoreType.DMA((7 * n,)), pltpu.SemaphoreType.DMA((n,))],
        compiler_params=pltpu.CompilerParams(vmem_limit_bytes=VMEM_LIMIT),
    )(*vs)[n:]


def _swap_copies(src, land, send, recv):
    x, y, c = _place()
    return [pltpu.make_async_remote_copy(src_ref=src[i], dst_ref=land[i], send_sem=send.at[i], recv_sem=recv.at[i],
                                         device_id=(x, y, 1 - c), device_id_type=MESH) for i in range(len(src))]


def _swap_start(vs, *, name):
    n = len(vs)

    def body(*refs):
        src, land, send, recv, token = refs[:n], refs[n:2 * n], refs[2 * n], refs[2 * n + 1], refs[-1]
        for cp in _swap_copies(src, land, send, recv):
            cp.start()
        token[...] = jnp.zeros_like(token)

    lands = [lax.empty(v.shape, v.dtype) for v in vs]
    sems = pltpu.SemaphoreType.DMA((n,))
    out = pl.pallas_call(
        body, name=name, in_specs=[HBM_SPEC] * (2 * n),
        out_specs=[SEM_SPEC, SEM_SPEC] + [HBM_SPEC] * (2 * n) + [VMEM_SPEC],
        out_shape=[sems, sems] + [pltpu.HBM(a.shape, a.dtype) for a in list(vs) + lands]
        + [jax.ShapeDtypeStruct((SUBLANES, LANES), F32)],
        input_output_aliases={i: 2 + i for i in range(2 * n)},
        compiler_params=pltpu.CompilerParams(has_side_effects=DATAFLOW),
    )(*[_in_hbm(a) for a in list(vs) + lands])
    return (out[0], out[1], out[2:2 + n], out[2 + n:2 + 2 * n]), out[-1]


def _swap_wait(handle, after, *, name):
    send_sems, recv_sems, srcs, lands = handle
    n = len(srcs)

    def body(*refs):
        src, land, send, recv = refs[:n], refs[n:2 * n], refs[2 * n], refs[2 * n + 1]
        for cp in _swap_copies(src, land, send, recv):
            cp.wait_send()
            cp.wait_recv()

    both = list(srcs) + list(lands)
    out = pl.pallas_call(
        body, name=name, in_specs=[HBM_SPEC] * (2 * n) + [SEM_SPEC, SEM_SPEC, ANY_SPEC], out_specs=[HBM_SPEC] * (2 * n),
        out_shape=[pltpu.HBM(a.shape, a.dtype) for a in both],
        input_output_aliases={i: i for i in range(2 * n)},
        compiler_params=pltpu.CompilerParams(has_side_effects=DATAFLOW),
    )(*both, send_sems, recv_sems, after)
    return out[:n], out[n:]


def _gather_copies(srcs, lands, send, recv, sibling=False):
    x, y, c = _place()
    out = []
    for i, (src, land) in enumerate(zip(srcs, lands)):
        half = land.shape[1] // 2
        rows = pl.ds(c * half, half)
        for k, (px, py) in enumerate(_other_chips(x, y)):
            if sibling:
                src_ref, dst_ref, to = src.at[2 * px + py, rows], land.at[2 * px + py, rows], (x, y, 1 - c)
            else:
                src_ref, dst_ref, to = src.at[rows], land.at[2 * x + y, rows], (px, py, c)
            out.append(pltpu.make_async_remote_copy(src_ref=src_ref, dst_ref=dst_ref, send_sem=send.at[3 * i + k],
                                                    recv_sem=recv.at[3 * i + k], device_id=to, device_id_type=MESH))
    return out


def _gather_arrivals(lands, send, recv, sibling=False):
    x, y, c = _place()
    out = []
    for i, land in enumerate(lands):
        half = land.shape[1] // 2
        rows = pl.ds(((1 - c) if sibling else c) * half, half)
        for k, (px, py) in enumerate(_other_chips(x, y)):
            part = land.at[2 * px + py, rows]
            out.append(pltpu.make_async_remote_copy(src_ref=part, dst_ref=part, send_sem=send.at[3 * i + k],
                                                    recv_sem=recv.at[3 * i + k],
                                                    device_id=(x, y, 1 - c) if sibling else (px, py, c), device_id_type=MESH))
    return out


def _gather_start(shards, after, *, name):
    n = len(shards)

    def body(*refs):
        srcs, lands, send, recv, token = refs[:n], refs[n:2 * n], refs[2 * n], refs[2 * n + 1], refs[-1]
        for cp in _gather_copies(srcs, lands, send, recv):
            cp.start()
        token[...] = jnp.zeros_like(token)

    lands = [lax.empty((N_CHIPS,) + s.shape, s.dtype) for s in shards]
    sems = pltpu.SemaphoreType.DMA((3 * n,))
    body, xs, xa = _after(body, 2 * n, after)
    out = pl.pallas_call(
        body, name=name, in_specs=[HBM_SPEC] * (2 * n) + xs,
        out_specs=[SEM_SPEC, SEM_SPEC] + [HBM_SPEC] * (2 * n) + [VMEM_SPEC],
        out_shape=[sems, sems] + [pltpu.HBM(a.shape, a.dtype) for a in list(shards) + lands]
        + [jax.ShapeDtypeStruct((SUBLANES, LANES), F32)],
        input_output_aliases={i: 2 + i for i in range(2 * n)},
        compiler_params=pltpu.CompilerParams(has_side_effects=DATAFLOW),
    )(*[_in_hbm(a) for a in list(shards) + lands], *xa)
    return (out[0], out[1], out[2:2 + n], out[2 + n:2 + 2 * n]), out[-1]


def _gather_wait(handle, after, *, name):
    send_sems, recv_sems, srcs, lands = handle
    n = len(srcs)

    def body(*refs):
        srcs_, lands_, send, recv = refs[:n], refs[n:2 * n], refs[2 * n], refs[2 * n + 1]
        for cp in _gather_copies(srcs_, lands_, send, recv):
            cp.wait_send()
        for cp in _gather_arrivals(lands_, send, recv):
            cp.wait_recv()

    both = list(srcs) + list(lands)
    out = pl.pallas_call(
        body, name=name, in_specs=[HBM_SPEC] * (2 * n) + [SEM_SPEC, SEM_SPEC, ANY_SPEC], out_specs=[HBM_SPEC] * (2 * n),
        out_shape=[pltpu.HBM(a.shape, a.dtype) for a in both],
        input_output_aliases={i: i for i in range(2 * n)},
        compiler_params=pltpu.CompilerParams(has_side_effects=DATAFLOW),
    )(*both, send_sems, recv_sems, after)
    return out[n:]


def _fill_sibling(lands, *, name):
    n = len(lands)

    def body(*refs):
        ins, outs, send_sems, recv_sems = refs[:n], refs[n:2 * n], refs[2 * n], refs[2 * n + 1]
        cps = _gather_copies(ins, outs, send_sems, recv_sems, sibling=True)
        for cp in cps:
            cp.start()
        for cp in _gather_arrivals(outs, send_sems, recv_sems, sibling=True):
            cp.wait_recv()
        for cp in cps:
            cp.wait_send()

    return pl.pallas_call(
        body, name=name, in_specs=[HBM_SPEC] * n, out_specs=[HBM_SPEC] * n,
        out_shape=[jax.ShapeDtypeStruct(a.shape, a.dtype) for a in lands],
        scratch_shapes=[pltpu.SemaphoreType.DMA((3 * n,)), pltpu.SemaphoreType.DMA((3 * n,))],
        input_output_aliases={i: i for i in range(n)},
    )(*lands)


def _scatter_copies(src, land, send, recv):
    x, y, c = _place()
    return [pltpu.make_async_remote_copy(src_ref=src[i].at[2 * px + py], dst_ref=land[i].at[k], send_sem=send.at[3 * i + k],
                                         recv_sem=recv.at[3 * i + k], device_id=(px, py, c), device_id_type=MESH)
            for i in range(len(src)) for k, (px, py) in enumerate(_other_chips(x, y))]


def _scatter_start(pieces, *, name):
    n = len(pieces)

    def body(*refs):
        src, land, send, recv, token = refs[:n], refs[n:2 * n], refs[2 * n], refs[2 * n + 1], refs[-1]
        for cp in _scatter_copies(src, land, send, recv):
            cp.start()
        token[...] = jnp.zeros_like(token)

    lands = [lax.empty((3,) + p.shape[1:], p.dtype) for p in pieces]
    sems = pltpu.SemaphoreType.DMA((3 * n,))
    out = pl.pallas_call(
        body, name=name, in_specs=[HBM_SPEC] * (2 * n),
        out_specs=[SEM_SPEC, SEM_SPEC] + [HBM_SPEC] * (2 * n) + [VMEM_SPEC],
        out_shape=[sems, sems] + [pltpu.HBM(a.shape, a.dtype) for a in pieces + lands]
        + [jax.ShapeDtypeStruct((SUBLANES, LANES), F32)],
        input_output_aliases={i: 2 + i for i in range(2 * n)},
        compiler_params=pltpu.CompilerParams(has_side_effects=DATAFLOW),
    )(*[_in_hbm(a) for a in pieces + lands])
    return (out[0], out[1], out[2:2 + n], out[2 + n:2 + 2 * n]), out[-1]


def _scatter_wait(handle, after, *, name):
    send_sems, recv_sems, srcs, lands = handle
    n = len(srcs)

    def body(*refs):
        src, land, send, recv = refs[:n], refs[n:2 * n], refs[2 * n], refs[2 * n + 1]
        for cp in _scatter_copies(src, land, send, recv):
            cp.wait_send()
            cp.wait_recv()

    both = list(srcs) + list(lands)
    out = pl.pallas_call(
        body, name=name, in_specs=[HBM_SPEC] * (2 * n) + [SEM_SPEC, SEM_SPEC, ANY_SPEC], out_specs=[HBM_SPEC] * (2 * n),
        out_shape=[pltpu.HBM(a.shape, a.dtype) for a in both],
        input_output_aliases={i: i for i in range(2 * n)},
        compiler_params=pltpu.CompilerParams(has_side_effects=DATAFLOW),
    )(*both, send_sems, recv_sems, after)
    return out[n:]


def _to_bf16(x, *, name, after=None):
    T, Dm = x.shape
    tr = _tile(T, 512, 2 * SUBLANES)

    def body(x_ref, o_ref):
        o_ref[...] = x_ref[...].astype(BF16)

    blk = pl.BlockSpec((tr, Dm), lambda i: (i, 0))
    body, xs, xa = _after(body, 1, after)
    return pl.pallas_call(
        body, name=name, grid=(T // tr,), in_specs=[blk] + xs, out_specs=blk, out_shape=jax.ShapeDtypeStruct((T, Dm), BF16),
        compiler_params=_params(("parallel",)),
    )(x, *xa)


def _chip_sum(pieces, got, chip, *, name):
    _, R, Cc = pieces.shape
    tr = _tile(R, 256, SUBLANES)

    def body(chip_ref, a_ref, g_ref, o_ref):
        o_ref[...] = ((a_ref[...] + g_ref[0].astype(F32)) + g_ref[1].astype(F32)) + g_ref[2].astype(F32)

    return pl.pallas_call(
        body, name=name,
        grid_spec=pltpu.PrefetchScalarGridSpec(
            num_scalar_prefetch=1, grid=(R // tr,),
            in_specs=[pl.BlockSpec((None, tr, Cc), lambda i, ch: (ch[0], i, 0)),
                      pl.BlockSpec((3, tr, Cc), lambda i, ch: (0, i, 0))],
            out_specs=pl.BlockSpec((tr, Cc), lambda i, ch: (i, 0))),
        out_shape=jax.ShapeDtypeStruct((R, Cc), F32),
        compiler_params=_params(("parallel",)),
    )(chip, pieces, got)


PACK_COLS = 1024
SMALL_ROWS = 32


def kernel(x, hgrn_w_in, hgrn_lb_logits, hgrn_gnorm_w, hgrn_w_out, swa_w_q, swa_sinks, swa_w_out, shared_w_kv, rel_bias, ffn_w_in, ffn_conv_w, ffn_conv_b, ffn_w_out, ln_mix_g, ln_mix_b, ln_ffn_g, ln_ffn_b, loss_target, m_hgrn_w_in, m_hgrn_lb_logits, m_hgrn_gnorm_w, m_hgrn_w_out, m_swa_w_q, m_swa_sinks, m_swa_w_out, m_shared_w_kv, m_rel_bias, m_ffn_w_in, m_ffn_conv_w, m_ffn_conv_b, m_ffn_w_out, m_ln_mix_g, m_ln_mix_b, m_ln_ffn_g, m_ln_ffn_b, v_hgrn_w_in, v_hgrn_lb_logits, v_hgrn_gnorm_w, v_hgrn_w_out, v_swa_w_q, v_swa_sinks, v_swa_w_out, v_shared_w_kv, v_rel_bias, v_ffn_w_in, v_ffn_conv_w, v_ffn_conv_b, v_ffn_w_out, v_ln_mix_g, v_ln_mix_b, v_ln_ffn_g, v_ln_ffn_b):
    xi, yi, ci = _place()
    chip = 2 * xi + yi
    Dm = D_MODEL
    FC = 2 * FFN_DIM // N_CHIPS
    Fo = FFN_DIM // N_CHIPS
    Dq = Dm // N_CHIPS
    bf = lambda a: a.astype(BF16)

    small = jnp.concatenate([hgrn_lb_logits.reshape(-1), ffn_conv_w.reshape(-1)])
    n_small = small.shape[0]
    bits = jnp.concatenate(_split3(small))
    bits = jnp.pad(bits, (0, SMALL_ROWS * PACK_COLS - 3 * n_small)).reshape(SMALL_ROWS, PACK_COLS)
    groups = [[bf(hgrn_w_in[0]), bf(hgrn_w_out[0]), bits],
              [bf(swa_w_q[0]), bf(swa_w_out[0]), bf(shared_w_kv), bf(ffn_w_in[0]), bf(ffn_w_out[0])],
              [bf(ffn_w_in[1]), bf(ffn_w_out[1])]]

    def gathered(k, landed):
        lands = _fill_sibling(landed, name=f"gather_w{k}_fill")
        return [lax.dynamic_update_slice(land, shard[None], (chip,) + (0,) * shard.ndim)
                for land, shard in zip(lands, groups[k])], lands

    handle0, token0 = _gather_start(groups[0], None, name="gather_w0_start")
    xb = _to_bf16(x[0], name="x_to_bf16", after=token0)
    corner = lambda a: a[:2 * SUBLANES, :LANES]
    casts_done = corner(xb) + sum(corner(a) for a in groups[1] + groups[2])
    (w_in, w_hg_out, small_all), lands0 = gathered(0, _gather_wait(handle0, casts_done, name="gather_w0_wait"))
    handle1, token1 = _gather_start(groups[1], lands0[0], name="gather_w1_start")
    parts = small_all.reshape(N_CHIPS, -1)[:, :3 * n_small].reshape(N_CHIPS, 3, n_small).astype(F32)
    vals = (parts[:, 0] + parts[:, 1]) + parts[:, 2]
    lb_full = vals[:, :2 * Dq].reshape(N_CHIPS, 2, Dq).transpose(1, 0, 2).reshape(2, Dm)
    cw_full = vals[:, 2 * Dq:].reshape(N_CHIPS, DEPTH, 3, FC).transpose(1, 2, 0, 3).reshape(DEPTH, 3, 2 * FFN_DIM)

    got = {"handle": handle1}

    def more_weights(k, after):
        ws, lands = gathered(k, _gather_wait(got.pop("handle"), after, name=f"gather_w{k}_wait"))
        if k == 1:
            got["handle"], token2 = _gather_start(groups[2], lands[0], name="gather_w2_start")
            w_q, w_o, w_kv, w_fi, w_fo = ws
            got.update(ffn_in={0: w_fi}, ffn_out={0: w_fo.reshape(FFN_DIM, Dm)})
            return {"sw_q": w_q.reshape(Dm, Dm), "sw_out": w_o.reshape(Dm, Dm), "kv": w_kv.reshape(Dm, 2 * KV_DIM),
                    "token": token2, "ffn_in": got["ffn_in"], "ffn_out": got["ffn_out"]}
        w_fi, w_fo = ws
        return {"ffn_in": {**got["ffn_in"], 1: w_fi}, "ffn_out": {**got["ffn_out"], 1: w_fo.reshape(FFN_DIM, Dm)}}

    w = {
        "hg_in": w_in, "hg_out": w_hg_out.reshape(Dm, Dm), "token": token1,
        "lb_logits": lb_full, "gnorm": hgrn_gnorm_w, "sinks": swa_sinks, "rel_bias": rel_bias,
        "conv_w_a": [cw_full[l, :, :FFN_DIM] for l in range(DEPTH)],
        "conv_w_b": [cw_full[l, :, FFN_DIM:] for l in range(DEPTH)],
        "conv_b_a": [ffn_conv_b[l:l + 1, :FFN_DIM] for l in range(DEPTH)],
        "conv_b_b": [ffn_conv_b[l:l + 1, FFN_DIM:] for l in range(DEPTH)],
        "ln_mix_g": [ln_mix_g[l:l + 1] for l in range(DEPTH)], "ln_mix_b": [ln_mix_b[l:l + 1] for l in range(DEPTH)],
        "ln_ffn_g": [ln_ffn_g[l:l + 1] for l in range(DEPTH)], "ln_ffn_b": [ln_ffn_b[l:l + 1] for l in range(DEPTH)],
    }

    sent = {}

    def emit(k, gd):
        rows4 = lambda a: a.reshape(N_CHIPS, -1, a.shape[-1])
        order = {1: ["sw_q", "sw_out", "kv", "ffn_in", "ffn_out"], 2: ["ffn_in", "ffn_out", "hg_out"], 3: ["hg_in"]}[k]
        as_pieces = lambda a, nme: a if nme in ("ffn_in", "hg_in") else rows4(a)
        handle, token = _scatter_start([as_pieces(gd[nme][1], nme) for nme in order], name=f"scatter_g{k}_start")
        sent[k] = (handle, [as_pieces(gd[nme][0], nme) for nme in order])
        return token

    loss_tile, grad_x, g = _local_step(x[0], xb, loss_target[0], w, more_weights, emit)

    wts = dict(hgrn_w_in=hgrn_w_in, hgrn_lb_logits=hgrn_lb_logits, hgrn_gnorm_w=hgrn_gnorm_w, hgrn_w_out=hgrn_w_out,
               swa_w_q=swa_w_q, swa_sinks=swa_sinks, swa_w_out=swa_w_out, shared_w_kv=shared_w_kv, rel_bias=rel_bias,
               ffn_w_in=ffn_w_in, ffn_conv_w=ffn_conv_w, ffn_conv_b=ffn_conv_b, ffn_w_out=ffn_w_out,
               ln_mix_g=ln_mix_g, ln_mix_b=ln_mix_b, ln_ffn_g=ln_ffn_g, ln_ffn_b=ln_ffn_b)
    ms = dict(hgrn_w_in=m_hgrn_w_in, hgrn_lb_logits=m_hgrn_lb_logits, hgrn_gnorm_w=m_hgrn_gnorm_w, hgrn_w_out=m_hgrn_w_out,
              swa_w_q=m_swa_w_q, swa_sinks=m_swa_sinks, swa_w_out=m_swa_w_out, shared_w_kv=m_shared_w_kv, rel_bias=m_rel_bias,
              ffn_w_in=m_ffn_w_in, ffn_conv_w=m_ffn_conv_w, ffn_conv_b=m_ffn_conv_b, ffn_w_out=m_ffn_w_out,
              ln_mix_g=m_ln_mix_g, ln_mix_b=m_ln_mix_b, ln_ffn_g=m_ln_ffn_g, ln_ffn_b=m_ln_ffn_b)
    vs = dict(hgrn_w_in=v_hgrn_w_in, hgrn_lb_logits=v_hgrn_lb_logits, hgrn_gnorm_w=v_hgrn_gnorm_w, hgrn_w_out=v_hgrn_w_out,
              swa_w_q=v_swa_w_q, swa_sinks=v_swa_sinks, swa_w_out=v_swa_w_out, shared_w_kv=v_shared_w_kv, rel_bias=v_rel_bias,
              ffn_w_in=v_ffn_w_in, ffn_conv_w=v_ffn_conv_w, ffn_conv_b=v_ffn_conv_b, ffn_w_out=v_ffn_w_out,
              ln_mix_g=v_ln_mix_g, ln_mix_b=v_ln_mix_b, ln_ffn_g=v_ln_ffn_g, ln_ffn_b=v_ln_ffn_b)
    names = list(wts)
    grads, delta, new_m, new_v = {}, {}, {}, {}

    def update(n, ga, gb, layer=None, prev=None):
        r2 = lambda a: a.reshape(-1, a.shape[-1])
        rows = None if layer is None else (layer * ga.shape[0], ga.shape[0])
        return _adamw(r2(wts[n]), ga, gb, r2(ms[n]), r2(vs[n]), rows=rows, prev=prev,
                      name=f"adamw_{n}" + ("" if layer is None else f"_{layer}"))

    def keep(n, res):
        grads[n], delta[n], new_m[n], new_v[n] = [a.reshape(wts[n].shape) for a in res]

    chip1 = jnp.reshape(chip, (1,)).astype(jnp.int32)
    after, swaps = grad_x, {}
    for k in (1, 2, 3):
        handle, pieces = sent[k]
        lands = _scatter_wait(handle, after, name=f"scatter_g{k}_wait")
        parts = [_chip_sum(p, l, chip1, name=f"scatter_g{k}_sum{i}") for i, (p, l) in enumerate(zip(pieces, lands))]
        swaps[k], after = _swap_start(parts, name=f"scatter_g{k}_swap_start")
    for k in (1, 2, 3):
        parts, sibs = _swap_wait(swaps[k], after, name=f"scatter_g{k}_swap_wait")
        if k == 1:
            for n, ga, gb in zip(["swa_w_q", "swa_w_out", "shared_w_kv"], parts[:3], sibs[:3]):
                keep(n, update(n, ga, gb))
            ffn_in_1 = update("ffn_w_in", parts[3], sibs[3], layer=1)
            ffn_out_1 = update("ffn_w_out", parts[4], sibs[4], layer=1)
            after = ffn_out_1[3]
        elif k == 2:
            keep("ffn_w_in", update("ffn_w_in", parts[0], sibs[0], layer=0, prev=ffn_in_1))
            keep("ffn_w_out", update("ffn_w_out", parts[1], sibs[1], layer=0, prev=ffn_out_1))
            keep("hgrn_w_out", update("hgrn_w_out", parts[2], sibs[2]))
            after = new_v["hgrn_w_out"]
        else:
            keep("hgrn_w_in", update("hgrn_w_in", parts[0], sibs[0]))

    small_keys = ["lb_logits", "gnorm", "sinks", "rel_bias", "conv0", "conv1", "ln_mix0", "ln_mix1", "ln_ffn0", "ln_ffn1"]
    flat2 = lambda a: a.reshape(-1, a.shape[-1])
    sums = _sum8([loss_tile] + [flat2(g[k]) for k in small_keys], name="sum_small")
    loss = sums[0][0, 0]
    sg = {k: v.reshape(g[k].shape) for k, v in zip(small_keys, sums[1:])}
    conv = [sg["conv0"], sg["conv1"]]
    g_cw = jnp.stack([jnp.concatenate([conv[l][0, :3], conv[l][1, :3]], axis=1) for l in range(DEPTH)])
    g_cb = jnp.stack([jnp.concatenate([conv[l][0, 3], conv[l][1, 3]], axis=0) for l in range(DEPTH)])
    ln = lambda nme, r: jnp.stack([sg[nme + "0"][r], sg[nme + "1"][r]])
    small_g = dict(hgrn_lb_logits=lax.dynamic_slice_in_dim(sg["lb_logits"], chip * Dq, Dq, axis=1),
                   hgrn_gnorm_w=sg["gnorm"], swa_sinks=sg["sinks"], rel_bias=sg["rel_bias"],
                   ffn_conv_w=lax.dynamic_slice_in_dim(g_cw, chip * FC, FC, axis=2), ffn_conv_b=g_cb,
                   ln_mix_g=ln("ln_mix", 0), ln_mix_b=ln("ln_mix", 1), ln_ffn_g=ln("ln_ffn", 0), ln_ffn_b=ln("ln_ffn", 1))
    small_names = list(small_g)
    d_, m_, v_ = _adamw_small([flat2(wts[n]) for n in small_names], [flat2(small_g[n]) for n in small_names],
                              [flat2(ms[n]) for n in small_names], [flat2(vs[n]) for n in small_names], name="adamw_small")
    for n, a, b_, c_ in zip(small_names, d_, m_, v_):
        shp = wts[n].shape
        grads[n], delta[n], new_m[n], new_v[n] = small_g[n], a.reshape(shp), b_.reshape(shp), c_.reshape(shp)

    return (loss, grad_x[None], *[grads[n] for n in names], *[delta[n] for n in names],
            *[new_m[n] for n in names], *[new_v[n] for n in names])
```

```python
import math

import numpy as np
import jax
import jax.numpy as jnp
from jax import lax
from jax.experimental import pallas as pl
from jax.experimental.pallas import tpu as pltpu

F32 = jnp.float32
BF16 = jnp.bfloat16
MESH = pl.DeviceIdType.MESH

D_MODEL = 1024
DEPTH = 2
HG_HEADS = 8
HG_DIM = 128
SW_Q_HEADS = 16
SW_KV_HEADS = 4
SW_HEAD_DIM = 64
SW_GROUP = 4
SW_WINDOW = 128
REL_BUCKETS = 32
REL_MAX_DIST = 128
FFN_DIM = 2816
ALPHA = (2.0 * DEPTH) ** 0.25
LN_EPS = 1e-5
RMS_EPS = 1e-6
ADAM_LR = 0.001
ADAM_B1 = 0.9
ADAM_B2 = 0.999
ADAM_EPS = 1e-08
ADAM_WD = 0.01
ADAM_STEP = 10

VMEM_BYTES_V7X = 64 * 1024 * 1024
VMEM_LIMIT = VMEM_BYTES_V7X - 8 * 1024 * 1024
LANES = 128
SUBLANES = 8

HG_C = 64
HG_RB = 256
CONV_R = 128
N_CHIPS = 4
N_DEV = 8

ANY_SPEC = pl.BlockSpec(memory_space=pl.ANY)


def _after(body, n_in, after):
    if after is None:
        return body, [], ()

    def wrapped(*refs):
        return body(*refs[:n_in], *refs[n_in + 1:])

    return wrapped, [ANY_SPEC], (after,)


def _params(sem=None):
    return pltpu.CompilerParams(dimension_semantics=sem, vmem_limit_bytes=VMEM_LIMIT)


def _tile(n, pref, unit=LANES):
    if n <= pref:
        return n
    best = None
    for t in range(unit, pref + 1, unit):
        if n % t == 0:
            best = t
    assert best is not None, (n, pref, unit)
    return best


def _dot(a, b, ca, cb):
    nb = a.ndim - 2
    batch = tuple(range(nb))
    return lax.dot_general(a.astype(BF16), b.astype(BF16), (((nb + ca,), (nb + cb,)), (batch, batch)),
                           preferred_element_type=F32)


@jax.custom_vjp
def mm(a, b):
    return _dot(a, b, 1, 0)


@jax.custom_vjp
def mm_nt(a, b):
    return _dot(a, b, 1, 1)


@jax.custom_vjp
def mm_tn(a, b):
    return _dot(a, b, 0, 0)


mm.defvjp(lambda a, b: (mm(a, b), (a, b)), lambda r, ct: (mm_nt(ct, r[1]), mm_tn(r[0], ct)))
mm_nt.defvjp(lambda a, b: (mm_nt(a, b), (a, b)), lambda r, ct: (mm(ct, r[1]), mm_tn(ct, r[0])))
mm_tn.defvjp(lambda a, b: (mm_tn(a, b), (a, b)), lambda r, ct: (mm_nt(r[1], ct), mm(r[0], ct)))


def _split2(x):
    hi = x.astype(BF16)
    return hi, (x - hi.astype(F32)).astype(BF16)


@jax.custom_vjp
def _scores(qt, kt):
    return _dot(qt, kt, 1, 1)


def _scores_bwd(r, ct):
    (qh, ql), (kh, kl) = _split2(r[0]), _split2(r[1])
    return _dot(ct, kh, 1, 0) + _dot(ct, kl, 1, 0), _dot(ct, qh, 0, 0) + _dot(ct, ql, 0, 0)


_scores.defvjp(lambda a, b: (_scores(a, b), (a, b)), _scores_bwd)


def _split3(x):
    hi = x.astype(BF16)
    r1 = x - hi.astype(F32)
    mid = r1.astype(BF16)
    lo = (r1 - mid.astype(F32)).astype(BF16)
    return hi, mid, lo


def _cumsum_impl(x):
    ax = x.ndim - 2
    n = x.shape[ax]
    row = lax.broadcasted_iota(jnp.int32, x.shape, ax)
    d = 1
    while d < n:
        x = x + jnp.where(row >= d, pltpu.roll(x, d, ax), 0.0)
        d *= 2
    return x


def _cumsum_rev_impl(x):
    ax = x.ndim - 2
    n = x.shape[ax]
    row = lax.broadcasted_iota(jnp.int32, x.shape, ax)
    d = 1
    while d < n:
        x = x + jnp.where(row < n - d, pltpu.roll(x, n - d, ax), 0.0)
        d *= 2
    return x


@jax.custom_vjp
def _cumsum(x):
    return _cumsum_impl(x)


_cumsum.defvjp(lambda x: (_cumsum_impl(x), None), lambda _, ct: (_cumsum_rev_impl(ct),))


def _matmul(a, b, *, mode, name, out_dtype=F32, add=None, add_scale=1.0, tm=512, tn=1408, tk=1408, after=None,
            split_n=False, planes=None, also_bf16=False):
    P = b.shape[0] if planes else 1
    a2, b2 = a.shape[-2:], b.shape[-2:]
    (M, K) = a2 if mode[0] == "n" else a2[::-1]
    (K2, N) = b2 if mode[1] == "n" else b2[::-1]
    assert K == K2, (a.shape, b.shape, mode)
    assert a.ndim == (3 if planes == "k" else 2) and b.ndim == (3 if planes else 2)
    tm, tn, tk = _tile(M, tm), _tile(N, tn), _tile(K, tk)
    nj, nkp = N // tn, K // tk
    nk = nkp * (P if planes == "k" else 1)
    ca, cb = (1 if mode[0] == "n" else 0), (0 if mode[1] == "n" else 1)
    a_blk, a_idx = ((tk, tm), lambda i, k: (k, i)) if mode[0] == "t" else ((tm, tk), lambda i, k: (i, k))
    b_blk, b_idx = ((tn, tk), lambda k, j: (j, k)) if mode[1] == "t" else ((tk, tn), lambda k, j: (k, j))
    if planes == "k":
        a_spec = pl.BlockSpec((None,) + a_blk, lambda i, j, k: (k // nkp,) + a_idx(i, k % nkp))
        b_spec = pl.BlockSpec((None,) + b_blk, lambda i, j, k: (k // nkp,) + b_idx(k % nkp, j))
    else:
        a_spec = pl.BlockSpec(a_blk, lambda i, j, k: a_idx(i, k))
        b_spec = (pl.BlockSpec((None,) + b_blk, lambda i, j, k: (j // nj,) + b_idx(k, j % nj)) if planes == "n"
                  else pl.BlockSpec(b_blk, lambda i, j, k: b_idx(k, j)))
    if split_n:
        o_spec, out_shape = pl.BlockSpec((None, tm, tn), lambda i, j, k: (j, i, 0)), (P * nj if planes == "n" else nj, M, tn)
    elif planes == "n":
        o_spec, out_shape = pl.BlockSpec((None, tm, tn), lambda i, j, k: (j // nj, i, j % nj)), (P, M, N)
    else:
        o_spec, out_shape = pl.BlockSpec((tm, tn), lambda i, j, k: (i, j)), (M, N)
    has_add = add is not None
    assert not (has_add and (split_n or planes == "n"))

    def finish(r, add_ref, o_refs):
        if has_add:
            r = r + add_scale * add_ref[...]
        o_refs[0][...] = r.astype(out_dtype)
        if also_bf16:
            o_refs[1][...] = r.astype(BF16)

    def body(*refs):
        a_ref, b_ref = refs[:2]
        add_ref = refs[2] if has_add else None
        first = 3 if has_add else 2
        o_ref = refs[first:first + (2 if also_bf16 else 1)]
        if nk == 1:
            finish(_dot(a_ref[...], b_ref[...], ca, cb), add_ref, o_ref)
            return
        acc_ref = refs[-1]
        k = pl.program_id(2)

        @pl.when(k == 0)
        def _():
            acc_ref[...] = jnp.zeros_like(acc_ref)

        acc_ref[...] += _dot(a_ref[...], b_ref[...], ca, cb)

        @pl.when(k == nk - 1)
        def _():
            finish(acc_ref[...], add_ref, o_ref)

    in_specs = [a_spec, b_spec] + ([o_spec] if has_add else [])
    args = (a, b) + ((add,) if has_add else ())
    body, xs, xa = _after(body, len(args), after)
    in_specs, args = in_specs + xs, args + xa
    out_shapes = [jax.ShapeDtypeStruct(out_shape, out_dtype)] + ([jax.ShapeDtypeStruct(out_shape, BF16)] if also_bf16 else [])
    out = pl.pallas_call(
        body, name=name, grid=(M // tm, nj * (P if planes == "n" else 1), nk), in_specs=in_specs,
        out_specs=[o_spec] * len(out_shapes), out_shape=out_shapes,
        scratch_shapes=[pltpu.VMEM((tm, tn), F32)] if nk > 1 else [],
        compiler_params=_params(("parallel", "parallel", "arbitrary")),
    )(*args)
    return tuple(out) if also_bf16 else out[0]


def _matmul_planes_nn(a, b, *, name, tm=512, after=None):
    (M, K), (P, K2, N) = a.shape, b.shape
    assert K == K2
    tm = _tile(M, tm, 2 * SUBLANES)

    def body(a_ref, b_ref, o_ref):
        for p in range(P):
            o_ref[p] = _dot(a_ref[...], b_ref[p], 1, 0).astype(BF16)

    body, xs, xa = _after(body, 2, after)
    return pl.pallas_call(
        body, name=name, grid=(M // tm,),
        in_specs=[pl.BlockSpec((tm, K), lambda i: (i, 0)), pl.BlockSpec((P, K, N), lambda i: (0, 0, 0))] + xs,
        out_specs=pl.BlockSpec((P, tm, N), lambda i: (0, i, 0)), out_shape=jax.ShapeDtypeStruct((P, M, N), BF16),
        compiler_params=_params(("parallel",)),
    )(a, b, *xa)


def _matmul_planes_nt(a, b, add, *, add_scale, name, tm=512, after=None):
    (P, M, K), (P2, N, K2) = a.shape, b.shape
    assert P == P2 and K == K2 and add.shape == (M, N)
    tm = _tile(M, tm, SUBLANES)

    def body(a_ref, b_ref, add_ref, o_ref):
        r = add_scale * add_ref[...]
        for p in range(P):
            r = r + _dot(a_ref[p], b_ref[p], 1, 1)
        o_ref[...] = r

    row = pl.BlockSpec((tm, N), lambda i: (i, 0))
    body, xs, xa = _after(body, 3, after)
    return pl.pallas_call(
        body, name=name, grid=(M // tm,),
        in_specs=[pl.BlockSpec((P, tm, K), lambda i: (0, i, 0)), pl.BlockSpec((P, N, K), lambda i: (0, 0, 0)), row] + xs,
        out_specs=row, out_shape=jax.ShapeDtypeStruct((M, N), F32),
        compiler_params=_params(("parallel",)),
    )(a, b, add, *xa)


def _ln(z, g, b):
    mu = jnp.mean(z, axis=-1, keepdims=True)
    zc = z - mu
    var = jnp.mean(zc * zc, axis=-1, keepdims=True)
    return zc * lax.rsqrt(var + LN_EPS) * g + b


def _matmul_ln(a, b, h, g, bias, *, name, tgt=None, tm=512, a_t=False):
    (T, K), (K2, Dm) = (a.shape[::-1] if a_t else a.shape), b.shape
    assert K == K2 and h.shape == (T, Dm)
    tm = _tile(T, tm, SUBLANES)
    last = tgt is not None

    def body(*refs):
        a_ref, b_ref, h_ref, g_ref, bias_ref = refs[:5]
        z = ALPHA * h_ref[...] + _dot(a_ref[...], b_ref[...], 0 if a_t else 1, 0)
        if not last:
            z_ref, y_ref, yb_ref = refs[5:]
            y = _ln(z, g_ref[...], bias_ref[...])
            z_ref[...] = z
            y_ref[...] = y
            yb_ref[...] = y.astype(BF16)
            return
        t_ref, dz_ref, dzb_ref, dgb_ref, l_ref, da_ref = refs[5:]

        @pl.when(pl.program_id(0) == 0)
        def _():
            dgb_ref[...] = jnp.zeros_like(dgb_ref)
            l_ref[...] = jnp.zeros_like(l_ref)

        y, vjp = jax.vjp(_ln, z, g_ref[...], bias_ref[...])
        e = y - t_ref[...]
        dz, dg, db = vjp(e * (1.0 / Dm))
        l_ref[...] += 0.5 * jnp.sum(jnp.mean(e * e, axis=-1, keepdims=True), axis=0, keepdims=True)
        dzb = dz.astype(BF16)
        dz_ref[...] = dz
        dzb_ref[...] = dzb
        dgb_ref[...] += jnp.concatenate([dg, db], axis=0)
        da_ref[...] = _dot(dzb, b_ref[...], 1, 1).astype(BF16)

    row = pl.BlockSpec((tm, Dm), lambda i: (i, 0))
    vec = pl.BlockSpec((1, Dm), lambda i: (0, 0))
    a_spec = pl.BlockSpec((K, tm), lambda i: (0, i)) if a_t else pl.BlockSpec((tm, K), lambda i: (i, 0))
    in_specs = [a_spec, pl.BlockSpec((K, Dm), lambda i: (0, 0)), row, vec, vec]
    f32, b16 = jax.ShapeDtypeStruct((T, Dm), F32), jax.ShapeDtypeStruct((T, Dm), BF16)
    if not last:
        return pl.pallas_call(
            body, name=name, grid=(T // tm,), in_specs=in_specs, out_specs=[row, row, row], out_shape=[f32, f32, b16],
            compiler_params=_params(("parallel",)),
        )(a, b, h, g, bias)
    assert not a_t
    return pl.pallas_call(
        body, name=name, grid=(T // tm,), in_specs=in_specs + [row],
        out_specs=[row, row, pl.BlockSpec((2, Dm), lambda i: (0, 0)), pl.BlockSpec((SUBLANES, LANES), lambda i: (0, 0)), a_spec],
        out_shape=[f32, b16, jax.ShapeDtypeStruct((2, Dm), F32), jax.ShapeDtypeStruct((SUBLANES, LANES), F32),
                   jax.ShapeDtypeStruct((T, K), BF16)],
        compiler_params=_params(("arbitrary",)),
    )(a, b, h, g, bias, tgt)


def _ln_bwd_matmul(dy, z, g, b, w, *, name, out_t=False, tm=512, after=None):
    T, Dm = z.shape
    N = w.shape[0]
    tm = _tile(T, tm, LANES if out_t else SUBLANES)

    def body(dy_ref, z_ref, g_ref, b_ref, w_ref, dz_ref, dzb_ref, dgb_ref, o_ref):
        @pl.when(pl.program_id(0) == 0)
        def _():
            dgb_ref[...] = jnp.zeros_like(dgb_ref)

        _, vjp = jax.vjp(_ln, z_ref[...], g_ref[...], b_ref[...])
        dz, dg, db = vjp(dy_ref[...])
        dzb = dz.astype(BF16)
        dz_ref[...] = dz
        dzb_ref[...] = dzb
        dgb_ref[...] += jnp.concatenate([dg, db], axis=0)
        o_ref[...] = (_dot(w_ref[...], dzb, 1, 1) if out_t else _dot(dzb, w_ref[...], 1, 1)).astype(BF16)

    row = pl.BlockSpec((tm, Dm), lambda i: (i, 0))
    vec = pl.BlockSpec((1, Dm), lambda i: (0, 0))
    o_spec = pl.BlockSpec((N, tm), lambda i: (0, i)) if out_t else pl.BlockSpec((tm, N), lambda i: (i, 0))
    body, xs, xa = _after(body, 5, after)
    return pl.pallas_call(
        body, name=name, grid=(T // tm,), in_specs=[row, row, vec, vec, pl.BlockSpec((N, Dm), lambda i: (0, 0))] + xs,
        out_specs=[row, row, pl.BlockSpec((2, Dm), lambda i: (0, 0)), o_spec],
        out_shape=[jax.ShapeDtypeStruct((T, Dm), F32), jax.ShapeDtypeStruct((T, Dm), BF16),
                   jax.ShapeDtypeStruct((2, Dm), F32), jax.ShapeDtypeStruct((N, T) if out_t else (T, N), BF16)],
        compiler_params=_params(("arbitrary",)),
    )(dy, z, g, b, w, *xa)


def _hg_chunk(qr, fr, ir, gr, l0, l1, gw, st):
    C = qr.shape[-2]
    row = lax.broadcasted_iota(jnp.int32, qr.shape, qr.ndim - 2)
    lb = jax.nn.sigmoid(l0 - l1)
    fg = lb + (1.0 - lb) * jax.nn.sigmoid(fr)
    b = _cumsum(jnp.log(fg))
    q = jax.nn.silu(qr)
    k = 1.0 - fg
    bmid = lax.stop_gradient(jnp.sum(jnp.where(row == C // 2 - 1, b, 0.0), axis=-2, keepdims=True))
    bl = jnp.sum(jnp.where(row == C - 1, b, 0.0), axis=-2, keepdims=True)
    o = mm_nt(q * jnp.exp(b), st)
    sc = _scores(q * jnp.exp(b - bmid), k * jnp.exp(bmid - b))
    ti = lax.broadcasted_iota(jnp.int32, (C, C), 0)
    si = lax.broadcasted_iota(jnp.int32, (C, C), 1)
    sc = jnp.where(si <= ti, sc, 0.0)
    o = o + mm(sc, ir)
    st_new = st * jnp.exp(bl) + mm_tn(ir, k * jnp.exp(bl - b))
    on = o * lax.rsqrt(jnp.mean(o * o, axis=-1, keepdims=True) + RMS_EPS)
    return on * gw * jax.nn.silu(gr), st_new


def _heads(ref, rows):
    return jnp.stack([ref[rows, h * HG_DIM:(h + 1) * HG_DIM].astype(F32) for h in range(HG_HEADS)])


def _unheads(x):
    return jnp.concatenate([x[h] for h in range(HG_HEADS)], axis=-1)


def _hgrn_fwd(pre, lbl, gw, *, name):
    _, T, Dm = pre.shape
    rb = min(HG_RB, T)
    C = min(HG_C, rb)
    ncb = rb // C

    def body(pre_ref, lbl_ref, gw_ref, o_ref, st_ref, s_ref):
        @pl.when(pl.program_id(0) == 0)
        def _():
            s_ref[...] = jnp.zeros_like(s_ref)

        def chunk(ci, carry):
            r0 = pl.multiple_of(ci * C, C)
            rows = pl.ds(r0, C)
            st = s_ref[...]
            st_ref[ci] = st
            out, st_new = _hg_chunk(*[_heads(pre_ref.at[j], rows) for j in range(4)],
                                    _heads(lbl_ref, slice(0, 1)), _heads(lbl_ref, slice(1, 2)), gw_ref[...], st)
            o_ref[rows, :] = _unheads(out).astype(BF16)
            s_ref[...] = st_new
            return carry

        lax.fori_loop(0, ncb, chunk, 0, unroll=True)

    row = pl.BlockSpec((rb, Dm), lambda n: (n, 0))
    return pl.pallas_call(
        body, name=name, grid=(T // rb,),
        in_specs=[pl.BlockSpec((4, rb, Dm), lambda n: (0, n, 0)), pl.BlockSpec((2, Dm), lambda n: (0, 0)),
                  pl.BlockSpec((1, HG_DIM), lambda n: (0, 0))],
        out_specs=[row, pl.BlockSpec((ncb, HG_HEADS, HG_DIM, HG_DIM), lambda n: (n, 0, 0, 0))],
        out_shape=[jax.ShapeDtypeStruct((T, Dm), BF16),
                   jax.ShapeDtypeStruct((T // C, HG_HEADS, HG_DIM, HG_DIM), F32)],
        scratch_shapes=[pltpu.VMEM((HG_HEADS, HG_DIM, HG_DIM), F32)],
        compiler_params=_params(("arbitrary",)),
    )(pre, lbl, gw)


def _hgrn_bwd(pre, lbl, gw, states, dout, *, name, after=None):
    _, T, Dm = pre.shape
    rb = min(HG_RB, T)
    C = min(HG_C, rb)
    ncb = rb // C
    nb = T // rb

    def body(pre_ref, lbl_ref, gw_ref, st_ref, do_ref, dpre_ref, dlbl_ref, dgw_ref, ds_ref):
        @pl.when(pl.program_id(0) == 0)
        def _():
            ds_ref[...] = jnp.zeros_like(ds_ref)
            dlbl_ref[...] = jnp.zeros_like(dlbl_ref)
            dgw_ref[...] = jnp.zeros_like(dgw_ref)

        def chunk(cj, carry):
            ci = ncb - 1 - cj
            r0 = pl.multiple_of(ci * C, C)
            rows = pl.ds(r0, C)
            _, vjp = jax.vjp(_hg_chunk, *[_heads(pre_ref.at[j], rows) for j in range(4)],
                             _heads(lbl_ref, slice(0, 1)), _heads(lbl_ref, slice(1, 2)), gw_ref[...], st_ref[ci])
            *dpre, dl0, dl1, dgw, dst = vjp((_heads(do_ref, rows), ds_ref[...]))
            for j in range(4):
                dpre_ref[j, rows, :] = _unheads(dpre[j]).astype(BF16)
            dlbl_ref[0:1, :] += _unheads(dl0)
            dlbl_ref[1:2, :] += _unheads(dl1)
            dgw_ref[...] += dgw
            ds_ref[...] = dst
            return carry

        lax.fori_loop(0, ncb, chunk, 0, unroll=True)

    row = pl.BlockSpec((rb, Dm), lambda n: (nb - 1 - n, 0))
    lsp = pl.BlockSpec((2, Dm), lambda n: (0, 0))
    gsp = pl.BlockSpec((1, HG_DIM), lambda n: (0, 0))
    pre_spec = pl.BlockSpec((4, rb, Dm), lambda n: (0, nb - 1 - n, 0))
    body, xs, xa = _after(body, 5, after)
    return pl.pallas_call(
        body, name=name, grid=(nb,),
        in_specs=[pre_spec, lsp, gsp, pl.BlockSpec((ncb, HG_HEADS, HG_DIM, HG_DIM), lambda n: (nb - 1 - n, 0, 0, 0)), row] + xs,
        out_specs=[pre_spec, lsp, gsp],
        out_shape=[jax.ShapeDtypeStruct((4, T, Dm), BF16), jax.ShapeDtypeStruct((2, Dm), F32),
                   jax.ShapeDtypeStruct((1, HG_DIM), F32)],
        scratch_shapes=[pltpu.VMEM((HG_HEADS, HG_DIM, HG_DIM), F32)],
        compiler_params=_params(("arbitrary",)),
    )(pre, lbl, gw, states, dout, *xa)


CONV_HALO = 2 * SUBLANES


def _conv_rows(u_ref, scr, w, bias, r0, R):
    cur = u_ref[pl.ds(r0, R), :].astype(F32)
    p0 = pl.multiple_of(jnp.maximum(r0 - CONV_HALO, 0), CONV_HALO)
    scr[0:CONV_HALO, :] = jnp.where(r0 > 0, u_ref[pl.ds(p0, CONV_HALO), :].astype(F32), 0.0)
    scr[CONV_HALO:CONV_HALO + R, :] = cur
    s1 = scr[CONV_HALO - 1:CONV_HALO - 1 + R, :]
    s2 = scr[CONV_HALO - 2:CONV_HALO - 2 + R, :]
    return w[0:1, :] * s2 + w[1:2, :] * s1 + w[2:3, :] * cur + bias, cur, s1, s2


def _halves_spec(T, Fd):
    per = Fd // 2 // LANES
    return pl.BlockSpec((2, None, T, LANES), lambda j: (0, j // per, 0, j % per))


def _conv_gate_fwd(u, wa, wb, ba, bb, *, name):
    T, Fd = u.shape[2], 2 * u.shape[3]
    R = min(CONV_R, T)
    tc = LANES

    def body(u_ref, wa_ref, wb_ref, ba_ref, bb_ref, o_ref, sa, sb):
        wa_, wb_, ba_, bb_ = wa_ref[...], wb_ref[...], ba_ref[...], bb_ref[...]

        def step(ri, carry):
            r0 = pl.multiple_of(ri * R, R)
            ca = _conv_rows(u_ref.at[0], sa, wa_, ba_, r0, R)[0]
            cb = _conv_rows(u_ref.at[1], sb, wb_, bb_, r0, R)[0]
            o_ref[pl.ds(r0, R), :] = (jax.nn.silu(ca) * cb).astype(BF16)
            return carry

        lax.fori_loop(0, T // R, step, 0)

    col = pl.BlockSpec((T, tc), lambda j: (0, j))
    wsp = pl.BlockSpec((3, tc), lambda j: (0, j))
    bsp = pl.BlockSpec((1, tc), lambda j: (0, j))
    both = _halves_spec(T, Fd)
    return pl.pallas_call(
        body, name=name, grid=(Fd // tc,), in_specs=[both, wsp, wsp, bsp, bsp], out_specs=col,
        out_shape=jax.ShapeDtypeStruct((T, Fd), BF16),
        scratch_shapes=[pltpu.VMEM((CONV_HALO + R, tc), F32)] * 2,
        compiler_params=_params(("parallel",)),
    )(u, wa, wb, ba, bb)


def _conv_gate_bwd(u, wa, wb, ba, bb, dact, *, name):
    T, Fd = u.shape[2], 2 * u.shape[3]
    R = min(CONV_R, T)
    nr = T // R
    tc = LANES

    def body(u_ref, wa_ref, wb_ref, ba_ref, bb_ref, da_ref,
             du_ref, dp_ref, sa, sb, sda, sdb):
        wa_, wb_, ba_, bb_ = wa_ref[...], wb_ref[...], ba_ref[...], bb_ref[...]
        sda[R:R + SUBLANES, :] = jnp.zeros((SUBLANES, tc), F32)
        sdb[R:R + SUBLANES, :] = jnp.zeros((SUBLANES, tc), F32)

        def taps(dc, cur, s1, s2):
            return jnp.concatenate([jnp.sum(dc * s2, axis=0, keepdims=True), jnp.sum(dc * s1, axis=0, keepdims=True),
                                    jnp.sum(dc * cur, axis=0, keepdims=True)], axis=0)

        def du_rows(sd, dc, w):
            sd[0:R, :] = dc
            du = w[2:3, :] * dc + w[1:2, :] * sd[1:1 + R, :] + w[0:1, :] * sd[2:2 + R, :]
            sd[R:R + SUBLANES, :] = dc[0:SUBLANES]
            return du

        def step(rj, carry):
            dwa, dwb, dba, dbb = carry
            r0 = pl.multiple_of((nr - 1 - rj) * R, R)
            ca, cura, s1a, s2a = _conv_rows(u_ref.at[0], sa, wa_, ba_, r0, R)
            cb, curb, s1b, s2b = _conv_rows(u_ref.at[1], sb, wb_, bb_, r0, R)
            dact_ = da_ref[pl.ds(r0, R), :].astype(F32)
            sg = jax.nn.sigmoid(ca)
            dca = dact_ * cb * (sg * (1.0 + ca * (1.0 - sg)))
            dcb = dact_ * (ca * sg)
            du_ref[0, pl.ds(r0, R), :] = du_rows(sda, dca, wa_).astype(BF16)
            du_ref[1, pl.ds(r0, R), :] = du_rows(sdb, dcb, wb_).astype(BF16)
            return (dwa + taps(dca, cura, s1a, s2a), dwb + taps(dcb, curb, s1b, s2b),
                    dba + jnp.sum(dca, axis=0, keepdims=True), dbb + jnp.sum(dcb, axis=0, keepdims=True))

        z3 = jnp.zeros((3, tc), F32)
        z1 = jnp.zeros((1, tc), F32)
        dwa, dwb, dba, dbb = lax.fori_loop(0, nr, step, (z3, z3, z1, z1))
        dp_ref[0] = jnp.concatenate([dwa, dba], axis=0)
        dp_ref[1] = jnp.concatenate([dwb, dbb], axis=0)

    col = pl.BlockSpec((T, tc), lambda j: (0, j))
    wsp = pl.BlockSpec((3, tc), lambda j: (0, j))
    bsp = pl.BlockSpec((1, tc), lambda j: (0, j))
    both = _halves_spec(T, Fd)
    return pl.pallas_call(
        body, name=name, grid=(Fd // tc,), in_specs=[both, wsp, wsp, bsp, bsp, col],
        out_specs=[both, pl.BlockSpec((2, 4, tc), lambda j: (0, 0, j))],
        out_shape=[jax.ShapeDtypeStruct(u.shape, BF16), jax.ShapeDtypeStruct((2, 4, Fd), F32)],
        scratch_shapes=[pltpu.VMEM((CONV_HALO + R, tc), F32)] * 2 + [pltpu.VMEM((R + SUBLANES, tc), F32)] * 2,
        compiler_params=_params(("parallel",)),
    )(u, wa, wb, ba, bb, dact)


def _bucket_index():
    t = np.arange(SW_WINDOW)[None, :] + SW_WINDOW
    s = np.arange(2 * SW_WINDOW)[:, None]
    dist = np.maximum(t - s, 0)
    exact = REL_BUCKETS // 2
    d = np.maximum(dist, 1).astype(np.float32)
    log_b = exact + (np.log(d / np.float32(exact)) / np.float32(math.log(REL_MAX_DIST / exact))
                     * np.float32(REL_BUCKETS - exact)).astype(np.int32)
    bucket = np.where(dist < exact, dist, np.minimum(log_b, REL_BUCKETS - 1))
    return bucket.astype(np.int32).reshape(1, -1)


BIAS_COLS = SW_WINDOW * 2 * SW_WINDOW
BIAS_TILE = 4096


def _bias_from_table(table, bucket, *, name):
    def body(t_ref, idx_ref, o_ref):
        onehot = (lax.broadcasted_iota(jnp.int32, (REL_BUCKETS, BIAS_TILE), 0) == idx_ref[...]).astype(BF16)
        acc = jnp.zeros((SW_Q_HEADS, BIAS_TILE), F32)
        for piece in _split3(t_ref[...]):
            acc = acc + lax.dot_general(piece, onehot, (((0,), (0,)), ((), ())), preferred_element_type=F32)
        o_ref[...] = acc

    return pl.pallas_call(
        body, name=name, grid=(BIAS_COLS // BIAS_TILE,),
        in_specs=[pl.BlockSpec((REL_BUCKETS, SW_Q_HEADS), lambda j: (0, 0)), pl.BlockSpec((1, BIAS_TILE), lambda j: (0, j))],
        out_specs=pl.BlockSpec((SW_Q_HEADS, BIAS_TILE), lambda j: (0, j)),
        out_shape=jax.ShapeDtypeStruct((SW_Q_HEADS, BIAS_COLS), F32),
        compiler_params=_params(("parallel",)),
    )(table, bucket)


def _table_grad(dbias, bucket, *, name):
    def body(d_ref, idx_ref, o_ref):
        @pl.when(pl.program_id(0) == 0)
        def _():
            o_ref[...] = jnp.zeros_like(o_ref)

        onehot = (lax.broadcasted_iota(jnp.int32, (REL_BUCKETS, BIAS_TILE), 0) == idx_ref[...]).astype(BF16)
        acc = jnp.zeros((REL_BUCKETS, SW_Q_HEADS), F32)
        for piece in _split3(d_ref[...]):
            acc = acc + lax.dot_general(onehot, piece, (((1,), (1,)), ((), ())), preferred_element_type=F32)
        o_ref[...] += acc

    return pl.pallas_call(
        body, name=name, grid=(BIAS_COLS // BIAS_TILE,),
        in_specs=[pl.BlockSpec((SW_Q_HEADS, BIAS_TILE), lambda j: (0, j)), pl.BlockSpec((1, BIAS_TILE), lambda j: (0, j))],
        out_specs=pl.BlockSpec((REL_BUCKETS, SW_Q_HEADS), lambda j: (0, 0)),
        out_shape=jax.ShapeDtypeStruct((REL_BUCKETS, SW_Q_HEADS), F32),
        compiler_params=_params(("arbitrary",)),
    )(dbias, bucket)


KV_DIM = SW_KV_HEADS * SW_HEAD_DIM
GROUP_ROWS = SW_GROUP * SW_HEAD_DIM
GROUP_LANES = SW_GROUP * SW_WINDOW


def _band_mask(n):
    s = lax.broadcasted_iota(jnp.int32, (2 * SW_WINDOW, GROUP_LANES), 0)
    t = (lax.broadcasted_iota(jnp.int32, (2 * SW_WINDOW, GROUP_LANES), 1) & (SW_WINDOW - 1)) + SW_WINDOW
    dist = t - s
    return (dist >= 0) & (dist < SW_WINDOW) & ((n > 0) | (s >= SW_WINDOW))


def _side_by_side(x_ref, g):
    r0 = g * GROUP_ROWS
    return jnp.concatenate([x_ref[r0 + r * SW_HEAD_DIM:r0 + (r + 1) * SW_HEAD_DIM, :] for r in range(SW_GROUP)], axis=1)


def _group_inputs(bias_ref, sink_ref, g):
    heads = range(g * SW_GROUP, (g + 1) * SW_GROUP)
    bias = jnp.concatenate([bias_ref[h] for h in heads], axis=1)
    sink = jnp.concatenate([jnp.broadcast_to(sink_ref[:, h:h + 1], (1, SW_WINDOW)) for h in heads], axis=1)
    return heads, bias, sink


def _kv_pair(kvp_ref, kvc_ref, g):
    ks = slice(g * SW_HEAD_DIM, (g + 1) * SW_HEAD_DIM)
    vs = slice(KV_DIM + g * SW_HEAD_DIM, KV_DIM + (g + 1) * SW_HEAD_DIM)
    kk = jnp.concatenate([kvp_ref[:, ks], kvc_ref[:, ks]], axis=0)
    vv = jnp.concatenate([kvp_ref[:, vs], kvc_ref[:, vs]], axis=0)
    return kk, vv, ks, vs


def _col_max(x):
    return jnp.max(x, axis=0, keepdims=True)


def _col_sum(x):
    return jnp.sum(x, axis=0, keepdims=True)


def _attn_fwd(qt, kv, bias, sinks, *, name):
    Dm, T = qt.shape
    W = SW_WINDOW

    def body(q_ref, kvc_ref, kvp_ref, bias_ref, sink_ref, o_ref):
        mask = _band_mask(pl.program_id(0))
        G = range(SW_KV_HEADS)
        ins = [_group_inputs(bias_ref, sink_ref, g) for g in G]
        kvs = [_kv_pair(kvp_ref, kvc_ref, g) for g in G]
        q = [_side_by_side(q_ref, g) for g in G]
        lg = [jnp.where(mask, mm(kvs[g][0], q[g]) * (SW_HEAD_DIM ** -0.5) + ins[g][1], -jnp.inf) for g in G]
        m = [jnp.maximum(_col_max(lg[g]), ins[g][2]) for g in G]
        p = [jnp.exp(lg[g] - m[g]) for g in G]
        den = [_col_sum(p[g]) + jnp.exp(ins[g][2] - m[g]) for g in G]
        o = [mm_tn(kvs[g][1], p[g]) / den[g] for g in G]
        for g in G:
            for r in range(SW_GROUP):
                o_ref[g * GROUP_ROWS + r * SW_HEAD_DIM:g * GROUP_ROWS + (r + 1) * SW_HEAD_DIM, :] = (
                    o[g][:, r * W:(r + 1) * W].astype(BF16))

    return pl.pallas_call(
        body, name=name, grid=(T // W,),
        in_specs=[pl.BlockSpec((Dm, W), lambda n: (0, n)),
                  pl.BlockSpec((W, 2 * KV_DIM), lambda n: (n, 0)),
                  pl.BlockSpec((W, 2 * KV_DIM), lambda n: (jnp.maximum(n - 1, 0), 0)),
                  pl.BlockSpec((SW_Q_HEADS, 2 * W, W), lambda n: (0, 0, 0)),
                  pl.BlockSpec((1, SW_Q_HEADS), lambda n: (0, 0))],
        out_specs=pl.BlockSpec((Dm, W), lambda n: (0, n)),
        out_shape=jax.ShapeDtypeStruct((Dm, T), BF16),
        compiler_params=_params(("parallel",)),
    )(qt, kv, kv, bias, sinks)


def _attn_bwd(qt, kv, bias, sinks, dot, *, name):
    Dm, T = qt.shape
    W = SW_WINDOW
    nb = T // W

    def body(q_ref, kvc_ref, kvp_ref, bias_ref, sink_ref, do_ref,
             dq_ref, dkv_ref, dbias_ref, dsink_ref, carry_ref):
        @pl.when(pl.program_id(0) == 0)
        def _():
            carry_ref[...] = jnp.zeros_like(carry_ref)
            dbias_ref[...] = jnp.zeros_like(dbias_ref)
            dsink_ref[...] = jnp.zeros_like(dsink_ref)

        n = nb - 1 - pl.program_id(0)
        mask = _band_mask(n)
        lane = lax.broadcasted_iota(jnp.int32, (1, SW_Q_HEADS), 1)
        sc = SW_HEAD_DIM ** -0.5
        G = range(SW_KV_HEADS)
        ins = [_group_inputs(bias_ref, sink_ref, g) for g in G]
        kvs = [_kv_pair(kvp_ref, kvc_ref, g) for g in G]
        q = [_side_by_side(q_ref, g) for g in G]
        do = [_side_by_side(do_ref, g) for g in G]
        lg = [jnp.where(mask, mm(kvs[g][0], q[g]) * sc + ins[g][1], -jnp.inf) for g in G]
        m = [jnp.maximum(_col_max(lg[g]), ins[g][2]) for g in G]
        p = [jnp.exp(lg[g] - m[g]) for g in G]
        ps = [jnp.exp(ins[g][2] - m[g]) for g in G]
        rden = [1.0 / (_col_sum(p[g]) + ps[g]) for g in G]
        pn = [p[g] * rden[g] for g in G]
        dpn = [mm(kvs[g][1], do[g]) for g in G]
        delta = [_col_sum(pn[g] * dpn[g]) for g in G]
        ds = [pn[g] * (dpn[g] - delta[g]) for g in G]
        dsr = [-(ps[g] * rden[g]) * delta[g] for g in G]
        dq = [mm_tn(kvs[g][0], ds[g]) * sc for g in G]
        dkk = [mm_nt(ds[g], q[g]) * sc for g in G]
        dvv = [mm_nt(pn[g], do[g]) for g in G]
        dsink = jnp.zeros((1, SW_Q_HEADS), F32)
        for g in G:
            _, _, ks, vs = kvs[g]
            for r, h in enumerate(ins[g][0]):
                cols = slice(r * W, (r + 1) * W)
                dbias_ref[h] += ds[g][:, cols]
                dq_ref[g * GROUP_ROWS + r * SW_HEAD_DIM:g * GROUP_ROWS + (r + 1) * SW_HEAD_DIM, :] = dq[g][:, cols].astype(BF16)
                dsink = dsink + jnp.where(lane == h, jnp.sum(dsr[g][:, cols], axis=1, keepdims=True), 0.0)
            dkv_ref[:, ks] = (carry_ref[:, ks] + dkk[g][W:]).astype(BF16)
            dkv_ref[:, vs] = (carry_ref[:, vs] + dvv[g][W:]).astype(BF16)
            carry_ref[:, ks] = dkk[g][:W]
            carry_ref[:, vs] = dvv[g][:W]
        dsink_ref[...] += dsink

    rev = lambda n: (nb - 1 - n, 0)
    revt = lambda n: (0, nb - 1 - n)
    return pl.pallas_call(
        body, name=name, grid=(nb,),
        in_specs=[pl.BlockSpec((Dm, W), revt),
                  pl.BlockSpec((W, 2 * KV_DIM), rev),
                  pl.BlockSpec((W, 2 * KV_DIM), lambda n: (jnp.maximum(nb - 2 - n, 0), 0)),
                  pl.BlockSpec((SW_Q_HEADS, 2 * W, W), lambda n: (0, 0, 0)),
                  pl.BlockSpec((1, SW_Q_HEADS), lambda n: (0, 0)),
                  pl.BlockSpec((Dm, W), revt)],
        out_specs=[pl.BlockSpec((Dm, W), revt), pl.BlockSpec((W, 2 * KV_DIM), rev),
                   pl.BlockSpec((SW_Q_HEADS, 2 * W, W), lambda n: (0, 0, 0)),
                   pl.BlockSpec((1, SW_Q_HEADS), lambda n: (0, 0))],
        out_shape=[jax.ShapeDtypeStruct((Dm, T), BF16), jax.ShapeDtypeStruct((T, 2 * KV_DIM), BF16),
                   jax.ShapeDtypeStruct((SW_Q_HEADS, 2 * W, W), F32), jax.ShapeDtypeStruct((1, SW_Q_HEADS), F32)],
        scratch_shapes=[pltpu.VMEM((W, 2 * KV_DIM), F32)],
        compiler_params=_params(("arbitrary",)),
    )(qt, kv, kv, bias, sinks, dot)


def _ffn_fwd(hb, w, l, after=None):
    u = _matmul_planes_nn(hb, w["ffn_in"][l], name=f"ffn{l}_up", after=after)
    u = u.reshape((2, 2) + u.shape[1:])
    act = _conv_gate_fwd(u, w["conv_w_a"][l], w["conv_w_b"][l], w["conv_b_a"][l], w["conv_b_b"][l],
                         name=f"ffn{l}_conv_gate")
    return u, act


def _ffn_bwd(dffb, dh_scaled, hb, u, act, w, l, dact):
    g_out = _matmul(act, dffb, mode="tn", name=f"ffn{l}_down_dw", tm=1408, tn=1024, tk=1024, also_bf16=True)
    du, g_conv = _conv_gate_bwd(u, w["conv_w_a"][l], w["conv_w_b"][l], w["conv_b_a"][l], w["conv_b_b"][l],
                                dact, name=f"ffn{l}_conv_gate_bwd")
    du = du.reshape((N_CHIPS,) + du.shape[2:])
    dh = _matmul_planes_nt(du, w["ffn_in"][l], dh_scaled, add_scale=ALPHA, name=f"ffn{l}_up_dx")
    g_in = _matmul(hb, du, mode="tn", planes="n", name=f"ffn{l}_up_dw", tm=1024, tn=FFN_DIM // 2, tk=1024, also_bf16=True)
    return dh, dict(ffn_out=g_out, ffn_in=g_in, conv=g_conv)


def _local_step(x, xb, tgt, w, more_weights, emit):
    bucket = jnp.asarray(_bucket_index())

    pre = _matmul_planes_nn(xb, w["hg_in"], name="hg_in", after=w.get("token"))
    og, states = _hgrn_fwd(pre, w["lb_logits"], w["gnorm"], name="hgrn_fwd")
    z1, h1, h1b = _matmul_ln(og, w["hg_out"], x, w["ln_mix_g"][0], w["ln_mix_b"][0], name="hg_out_ln")
    w = {**w, **more_weights(1, h1b)}
    u0, act0 = _ffn_fwd(h1b, w, 0, after=w.get("token"))
    z2, h2, h2b = _matmul_ln(act0, w["ffn_out"][0], h1, w["ln_ffn_g"][0], w["ln_ffn_b"][0], name="ffn0_down_ln")
    kv = _matmul(h2b, w["kv"], mode="nn", out_dtype=BF16, name="kv_proj")

    bias = _bias_from_table(w["rel_bias"], bucket, name="rel_bias_expand").reshape(SW_Q_HEADS, 2 * SW_WINDOW, SW_WINDOW)
    q1 = _matmul(w["sw_q"], h2b, mode="tt", out_dtype=BF16, name="sw_q", tm=1024, tn=1024)
    o1 = _attn_fwd(q1, kv, bias, w["sinks"], name="attn_fwd")
    z3, h3, h3b = _matmul_ln(o1, w["sw_out"], h2, w["ln_mix_g"][1], w["ln_mix_b"][1], a_t=True, name="sw_out_ln")
    w = {**w, **more_weights(2, h3b)}
    u1, act1 = _ffn_fwd(h3b, w, 1)

    g = {}
    dz, dzb, g["ln_ffn1"], loss_tile, dact1 = _matmul_ln(act1, w["ffn_out"][1], h3, w["ln_ffn_g"][1], w["ln_ffn_b"][1],
                                                         tgt=tgt, name="ffn1_down_ln_loss")

    dh3, gf1 = _ffn_bwd(dzb, dz, h3b, u1, act1, w, 1, dact1)
    dz, dzb, g["ln_mix1"], do1 = _ln_bwd_matmul(dh3, z3, w["ln_mix_g"][1], w["ln_mix_b"][1], w["sw_out"], out_t=True,
                                                name="ln_mix1_bwd_sw_out_dx")
    g_sw_out = _matmul(o1, dzb, mode="nn", name="sw_out_dw", tm=1024, tn=1024, tk=1024, also_bf16=True)
    dq1, dkv, dbias, dsinks = _attn_bwd(q1, kv, bias, w["sinks"], do1, name="attn_bwd")
    g["sinks"] = dsinks
    g["rel_bias"] = _table_grad(dbias.reshape(SW_Q_HEADS, BIAS_COLS), bucket, name="rel_bias_grad")
    dh2 = _matmul(dq1, w["sw_q"], mode="tt", add=dz, add_scale=ALPHA, name="sw_q_dx", tn=1024)
    dh2 = _matmul(dkv, w["kv"], mode="nt", add=dh2, name="kv_dx", tn=1024)
    g_sw_q = _matmul(h2b, dq1, mode="tt", name="sw_q_dw", tm=1024, tn=1024, tk=1024, also_bf16=True)
    g_kv = _matmul(h2b, dkv, mode="tn", name="kv_dw", tm=1024, tn=512, tk=1024, also_bf16=True)
    tok = emit(1, dict(sw_q=g_sw_q, sw_out=g_sw_out, kv=g_kv, ffn_in=gf1["ffn_in"], ffn_out=gf1["ffn_out"]))

    dz, dzb, g["ln_ffn0"], dact0 = _ln_bwd_matmul(dh2, z2, w["ln_ffn_g"][0], w["ln_ffn_b"][0], w["ffn_out"][0],
                                                  name="ln_ffn0_bwd_down_dx", after=tok)
    dh1, gf0 = _ffn_bwd(dzb, dz, h1b, u0, act0, w, 0, dact0)
    dz, dzb, g["ln_mix0"], dog = _ln_bwd_matmul(dh1, z1, w["ln_mix_g"][0], w["ln_mix_b"][0], w["hg_out"],
                                                name="ln_mix0_bwd_hg_out_dx")
    g_hg_out = _matmul(og, dzb, mode="tn", name="hg_out_dw", tm=1024, tn=1024, tk=1024, also_bf16=True)
    tok = emit(2, dict(hg_out=g_hg_out, ffn_in=gf0["ffn_in"], ffn_out=gf0["ffn_out"]))
    dpre, g["lb_logits"], g["gnorm"] = _hgrn_bwd(pre, w["lb_logits"], w["gnorm"], states, dog, name="hgrn_bwd", after=tok)
    tok = emit(3, dict(hg_in=_matmul(xb, dpre, mode="tn", planes="n", name="hg_in_dw", tm=1024, tn=1024, tk=1024, also_bf16=True)))
    dx = _matmul_planes_nt(dpre, w["hg_in"], dz, add_scale=ALPHA, name="hg_in_dx", after=tok)
    g["conv0"], g["conv1"] = gf0["conv"], gf1["conv"]
    return loss_tile, dx, g


def _adamw(wt, ga, gb, m, v, *, name, rows=None, prev=None):
    R, Cc = wt.shape
    r0, n = rows if rows is not None else (0, R)
    tr = _tile(n, 256, SUBLANES) if n % SUBLANES == 0 else n
    assert r0 % tr == 0
    c1 = 1.0 - ADAM_B1 ** ADAM_STEP
    c2 = 1.0 - ADAM_B2 ** ADAM_STEP
    n_in = 5

    def body(*refs):
        w_ref, ga_ref, gb_ref, m_ref, v_ref = refs[:n_in]
        g_ = ga_ref[...] + gb_ref[...]
        g_ref, d_ref, nm_ref, nv_ref = refs[-4:]
        nm = ADAM_B1 * m_ref[...] + (1.0 - ADAM_B1) * g_
        nv = ADAM_B2 * v_ref[...] + (1.0 - ADAM_B2) * (g_ * g_)
        g_ref[...] = g_
        d_ref[...] = -ADAM_LR * ((nm / c1) / (jnp.sqrt(nv / c2) + ADAM_EPS) + ADAM_WD * w_ref[...])
        nm_ref[...] = nm
        nv_ref[...] = nv

    full = pl.BlockSpec((tr, Cc), lambda i: (i + r0 // tr, 0))
    part = pl.BlockSpec((tr, Cc), lambda i: (i, 0))
    args = (wt, ga, gb, m, v)
    in_specs = [full, part, part, full, full]
    aliases = {}
    if prev is not None:
        args, in_specs = args + tuple(prev), in_specs + [ANY_SPEC] * 4
        aliases = {n_in + t: t for t in range(4)}
    return pl.pallas_call(
        body, name=name, grid=(n // tr,), in_specs=in_specs, out_specs=[full] * 4,
        out_shape=[jax.ShapeDtypeStruct((R, Cc), F32)] * 4, input_output_aliases=aliases,
        compiler_params=_params(("parallel",)),
    )(*args)


def _adamw_small(ws, gs, ms, vs, *, name):
    n = len(ws)
    c1 = 1.0 - ADAM_B1 ** ADAM_STEP
    c2 = 1.0 - ADAM_B2 ** ADAM_STEP

    def body(*refs):
        w_refs, g_refs, m_refs, v_refs = (refs[k * n:(k + 1) * n] for k in range(4))
        d_refs, nm_refs, nv_refs = (refs[(4 + k) * n:(5 + k) * n] for k in range(3))
        for i in range(n):
            g_ = g_refs[i][...]
            nm = ADAM_B1 * m_refs[i][...] + (1.0 - ADAM_B1) * g_
            nv = ADAM_B2 * v_refs[i][...] + (1.0 - ADAM_B2) * (g_ * g_)
            d_refs[i][...] = -ADAM_LR * ((nm / c1) / (jnp.sqrt(nv / c2) + ADAM_EPS) + ADAM_WD * w_refs[i][...])
            nm_refs[i][...] = nm
            nv_refs[i][...] = nv

    vm = pl.BlockSpec(memory_space=pltpu.VMEM)
    out = pl.pallas_call(
        body, name=name, in_specs=[vm] * (4 * n), out_specs=[vm] * (3 * n),
        out_shape=[jax.ShapeDtypeStruct(w.shape, F32) for w in ws] * 3,
    )(*ws, *gs, *ms, *vs)
    return out[:n], out[n:2 * n], out[2 * n:]


HBM_SPEC = pl.BlockSpec(memory_space=pltpu.HBM)
SEM_SPEC = pl.BlockSpec(memory_space=pltpu.SEMAPHORE)
VMEM_SPEC = pl.BlockSpec(memory_space=pltpu.VMEM)
DATAFLOW = pltpu.SideEffectType.DATAFLOW_SIDE_EFFECTING


def _in_hbm(a):
    return pltpu.with_memory_space_constraint(a, pltpu.HBM)


def _place():
    return lax.axis_index("x"), lax.axis_index("y"), lax.axis_index("c")


def _other_chips(x, y):
    return [(1 - x, y), (x, 1 - y), (1 - x, 1 - y)]


def _sum8(vs, *, name):
    n = len(vs)

    def body(*refs):
        v_refs, all_refs, o_refs = refs[:n], refs[n:2 * n], refs[2 * n:3 * n]
        send_sems, recv_sems, local_sems = refs[3 * n:]
        x, y, c = _place()
        me, sibling = (x, y, c), (x, y, 1 - c)
        chips = _other_chips(x, y)

        def slot(i, px, py, pc):
            return all_refs[i].at[4 * px + 2 * py + pc]

        def copy(i, k, block, to, src=None):
            return pltpu.make_async_remote_copy(
                src_ref=slot(i, *block) if src is None else src, dst_ref=slot(i, *block),
                send_sem=send_sems.at[7 * i + k], recv_sem=recv_sems.at[7 * i + k], device_id=to, device_id_type=MESH)

        mine = [pltpu.make_async_copy(v_refs[i], slot(i, *me), local_sems.at[i]) for i in range(n)]
        for cp in mine:
            cp.start()
        first = [copy(i, 0, me, sibling, src=v_refs[i]) for i in range(n)]
        first += [copy(i, 1 + j, me, (*chip, c), src=v_refs[i]) for i in range(n) for j, chip in enumerate(chips)]
        for cp in first:
            cp.start()
        passed = []
        for i in range(n):
            for j, chip in enumerate(chips):
                copy(i, 1 + j, (*chip, c), me).wait_recv()
                passed.append(copy(i, 4 + j, (*chip, c), sibling))
                passed[-1].start()
        for i in range(n):
            copy(i, 0, sibling, me).wait_recv()
            for j, chip in enumerate(chips):
                copy(i, 4 + j, (*chip, 1 - c), me).wait_recv()
        for cp in first + passed:
            cp.wait_send()
        for cp in mine:
            cp.wait()
        for i in range(n):
            acc = all_refs[i][0]
            for d in range(1, N_DEV):
                acc = acc + all_refs[i][d]
            o_refs[i][...] = acc

    return pl.pallas_call(
        body, name=name, in_specs=[VMEM_SPEC] * n, out_specs=[VMEM_SPEC] * (2 * n),
        out_shape=[jax.ShapeDtypeStruct((N_DEV,) + v.shape, F32) for v in vs] + [jax.ShapeDtypeStruct(v.shape, F32) for v in vs],
        scratch_shapes=[pltpu.SemaphoreType.DMA((7 * n,)), pltpu.SemaphoreType.DMA((7 * n,)), pltpu.SemaphoreType.DMA((n,))],
        compiler_params=pltpu.CompilerParams(vmem_limit_bytes=VMEM_LIMIT),
    )(*vs)[n:]


def _swap_copies(src, land, send, recv):
    x, y, c = _place()
    return [pltpu.make_async_remote_copy(src_ref=src[i], dst_ref=land[i], send_sem=send.at[i], recv_sem=recv.at[i],
                                         device_id=(x, y, 1 - c), device_id_type=MESH) for i in range(len(src))]


def _swap_start(vs, *, name):
    n = len(vs)

    def body(*refs):
        src, land, send, recv, token = refs[:n], refs[n:2 * n], refs[2 * n], refs[2 * n + 1], refs[-1]
        for cp in _swap_copies(src, land, send, recv):
            cp.start()
        token[...] = jnp.zeros_like(token)

    lands = [lax.empty(v.shape, v.dtype) for v in vs]
    sems = pltpu.SemaphoreType.DMA((n,))
    out = pl.pallas_call(
        body, name=name, in_specs=[HBM_SPEC] * (2 * n),
        out_specs=[SEM_SPEC, SEM_SPEC] + [HBM_SPEC] * (2 * n) + [VMEM_SPEC],
        out_shape=[sems, sems] + [pltpu.HBM(a.shape, a.dtype) for a in list(vs) + lands]
        + [jax.ShapeDtypeStruct((SUBLANES, LANES), F32)],
        input_output_aliases={i: 2 + i for i in range(2 * n)},
        compiler_params=pltpu.CompilerParams(has_side_effects=DATAFLOW),
    )(*[_in_hbm(a) for a in list(vs) + lands])
    return (out[0], out[1], out[2:2 + n], out[2 + n:2 + 2 * n]), out[-1]


def _swap_wait(handle, after, *, name):
    send_sems, recv_sems, srcs, lands = handle
    n = len(srcs)

    def body(*refs):
        src, land, send, recv = refs[:n], refs[n:2 * n], refs[2 * n], refs[2 * n + 1]
        for cp in _swap_copies(src, land, send, recv):
            cp.wait_send()
            cp.wait_recv()

    both = list(srcs) + list(lands)
    out = pl.pallas_call(
        body, name=name, in_specs=[HBM_SPEC] * (2 * n) + [SEM_SPEC, SEM_SPEC, ANY_SPEC], out_specs=[HBM_SPEC] * (2 * n),
        out_shape=[pltpu.HBM(a.shape, a.dtype) for a in both],
        input_output_aliases={i: i for i in range(2 * n)},
        compiler_params=pltpu.CompilerParams(has_side_effects=DATAFLOW),
    )(*both, send_sems, recv_sems, after)
    return out[:n], out[n:]


def _gather_copies(srcs, lands, send, recv, sibling=False):
    x, y, c = _place()
    out = []
    for i, (src, land) in enumerate(zip(srcs, lands)):
        half = land.shape[1] // 2
        rows = pl.ds(c * half, half)
        for k, (px, py) in enumerate(_other_chips(x, y)):
            if sibling:
                src_ref, dst_ref, to = src.at[2 * px + py, rows], land.at[2 * px + py, rows], (x, y, 1 - c)
            else:
                src_ref, dst_ref, to = src.at[rows], land.at[2 * x + y, rows], (px, py, c)
            out.append(pltpu.make_async_remote_copy(src_ref=src_ref, dst_ref=dst_ref, send_sem=send.at[3 * i + k],
                                                    recv_sem=recv.at[3 * i + k], device_id=to, device_id_type=MESH))
    return out


def _gather_arrivals(lands, send, recv, sibling=False):
    x, y, c = _place()
    out = []
    for i, land in enumerate(lands):
        half = land.shape[1] // 2
        rows = pl.ds(((1 - c) if sibling else c) * half, half)
        for k, (px, py) in enumerate(_other_chips(x, y)):
            part = land.at[2 * px + py, rows]
            out.append(pltpu.make_async_remote_copy(src_ref=part, dst_ref=part, send_sem=send.at[3 * i + k],
                                                    recv_sem=recv.at[3 * i + k],
                                                    device_id=(x, y, 1 - c) if sibling else (px, py, c), device_id_type=MESH))
    return out


def _own_copies(srcs, lands, sems):
    x, y, _ = _place()
    return [pltpu.make_async_copy(src, land.at[2 * x + y], sems.at[i]) for i, (src, land) in enumerate(zip(srcs, lands))]


def _gather_start(shards, after, *, name):
    n = len(shards)

    def body(*refs):
        srcs, lands, (send, recv, own), token = refs[:n], refs[n:2 * n], refs[2 * n:2 * n + 3], refs[-1]
        for cp in _gather_copies(srcs, lands, send, recv) + _own_copies(srcs, lands, own):
            cp.start()
        token[...] = jnp.zeros_like(token)

    lands = [lax.empty((N_CHIPS,) + s.shape, s.dtype) for s in shards]
    sems = pltpu.SemaphoreType.DMA((3 * n,))
    body, xs, xa = _after(body, 2 * n, after)
    out = pl.pallas_call(
        body, name=name, in_specs=[HBM_SPEC] * (2 * n) + xs,
        out_specs=[SEM_SPEC] * 3 + [HBM_SPEC] * (2 * n) + [VMEM_SPEC],
        out_shape=[sems, sems, pltpu.SemaphoreType.DMA((n,))] + [pltpu.HBM(a.shape, a.dtype) for a in list(shards) + lands]
        + [jax.ShapeDtypeStruct((SUBLANES, LANES), F32)],
        input_output_aliases={i: 3 + i for i in range(2 * n)},
        compiler_params=pltpu.CompilerParams(has_side_effects=DATAFLOW),
    )(*[_in_hbm(a) for a in list(shards) + lands], *xa)
    return (out[:3], out[3:3 + n], out[3 + n:3 + 2 * n]), out[-1]


def _gather_wait(handle, after, *, name):
    sems, srcs, lands = handle
    n = len(srcs)

    def body(*refs):
        srcs_, lands_, (send, recv, own) = refs[:n], refs[n:2 * n], refs[2 * n:2 * n + 3]
        for cp in _gather_copies(srcs_, lands_, send, recv):
            cp.wait_send()
        for cp in _gather_arrivals(lands_, send, recv):
            cp.wait_recv()
        for cp in _own_copies(srcs_, lands_, own):
            cp.wait()

    both = list(srcs) + list(lands)
    out = pl.pallas_call(
        body, name=name, in_specs=[HBM_SPEC] * (2 * n) + [SEM_SPEC] * 3 + [ANY_SPEC], out_specs=[HBM_SPEC] * (2 * n),
        out_shape=[pltpu.HBM(a.shape, a.dtype) for a in both],
        input_output_aliases={i: i for i in range(2 * n)},
        compiler_params=pltpu.CompilerParams(has_side_effects=DATAFLOW),
    )(*both, *sems, after)
    return out[n:]


def _fill_sibling(lands, *, name):
    n = len(lands)

    def body(*refs):
        ins, outs, send_sems, recv_sems = refs[:n], refs[n:2 * n], refs[2 * n], refs[2 * n + 1]
        cps = _gather_copies(ins, outs, send_sems, recv_sems, sibling=True)
        for cp in cps:
            cp.start()
        for cp in _gather_arrivals(outs, send_sems, recv_sems, sibling=True):
            cp.wait_recv()
        for cp in cps:
            cp.wait_send()

    return pl.pallas_call(
        body, name=name, in_specs=[HBM_SPEC] * n, out_specs=[HBM_SPEC] * n,
        out_shape=[jax.ShapeDtypeStruct(a.shape, a.dtype) for a in lands],
        scratch_shapes=[pltpu.SemaphoreType.DMA((3 * n,)), pltpu.SemaphoreType.DMA((3 * n,))],
        input_output_aliases={i: i for i in range(n)},
    )(*lands)


def _scatter_copies(src, land, send, recv):
    x, y, c = _place()
    return [pltpu.make_async_remote_copy(src_ref=src[i].at[2 * px + py], dst_ref=land[i].at[k], send_sem=send.at[3 * i + k],
                                         recv_sem=recv.at[3 * i + k], device_id=(px, py, c), device_id_type=MESH)
            for i in range(len(src)) for k, (px, py) in enumerate(_other_chips(x, y))]


def _scatter_start(pieces, *, name):
    n = len(pieces)

    def body(*refs):
        src, land, send, recv, token = refs[:n], refs[n:2 * n], refs[2 * n], refs[2 * n + 1], refs[-1]
        for cp in _scatter_copies(src, land, send, recv):
            cp.start()
        token[...] = jnp.zeros_like(token)

    lands = [lax.empty((3,) + p.shape[1:], p.dtype) for p in pieces]
    sems = pltpu.SemaphoreType.DMA((3 * n,))
    out = pl.pallas_call(
        body, name=name, in_specs=[HBM_SPEC] * (2 * n),
        out_specs=[SEM_SPEC, SEM_SPEC] + [HBM_SPEC] * (2 * n) + [VMEM_SPEC],
        out_shape=[sems, sems] + [pltpu.HBM(a.shape, a.dtype) for a in pieces + lands]
        + [jax.ShapeDtypeStruct((SUBLANES, LANES), F32)],
        input_output_aliases={i: 2 + i for i in range(2 * n)},
        compiler_params=pltpu.CompilerParams(has_side_effects=DATAFLOW),
    )(*[_in_hbm(a) for a in pieces + lands])
    return (out[0], out[1], out[2:2 + n], out[2 + n:2 + 2 * n]), out[-1]


def _scatter_wait(handle, after, *, name):
    send_sems, recv_sems, srcs, lands = handle
    n = len(srcs)

    def body(*refs):
        src, land, send, recv = refs[:n], refs[n:2 * n], refs[2 * n], refs[2 * n + 1]
        for cp in _scatter_copies(src, land, send, recv):
            cp.wait_send()
            cp.wait_recv()

    both = list(srcs) + list(lands)
    out = pl.pallas_call(
        body, name=name, in_specs=[HBM_SPEC] * (2 * n) + [SEM_SPEC, SEM_SPEC, ANY_SPEC], out_specs=[HBM_SPEC] * (2 * n),
        out_shape=[pltpu.HBM(a.shape, a.dtype) for a in both],
        input_output_aliases={i: i for i in range(2 * n)},
        compiler_params=pltpu.CompilerParams(has_side_effects=DATAFLOW),
    )(*both, send_sems, recv_sems, after)
    return out[n:]


def _to_bf16(x, *, name, after=None):
    T, Dm = x.shape
    tr = _tile(T, 512, 2 * SUBLANES)

    def body(x_ref, o_ref):
        o_ref[...] = x_ref[...].astype(BF16)

    blk = pl.BlockSpec((tr, Dm), lambda i: (i, 0))
    body, xs, xa = _after(body, 1, after)
    return pl.pallas_call(
        body, name=name, grid=(T // tr,), in_specs=[blk] + xs, out_specs=blk, out_shape=jax.ShapeDtypeStruct((T, Dm), BF16),
        compiler_params=_params(("parallel",)),
    )(x, *xa)


def _chip_sum(pieces, got, chip, *, name):
    _, R, Cc = pieces.shape
    tr = _tile(R, 256, SUBLANES)

    def body(chip_ref, a_ref, g_ref, o_ref):
        o_ref[...] = ((a_ref[...] + g_ref[0].astype(F32)) + g_ref[1].astype(F32)) + g_ref[2].astype(F32)

    return pl.pallas_call(
        body, name=name,
        grid_spec=pltpu.PrefetchScalarGridSpec(
            num_scalar_prefetch=1, grid=(R // tr,),
            in_specs=[pl.BlockSpec((None, tr, Cc), lambda i, ch: (ch[0], i, 0)),
                      pl.BlockSpec((3, tr, Cc), lambda i, ch: (0, i, 0))],
            out_specs=pl.BlockSpec((tr, Cc), lambda i, ch: (i, 0))),
        out_shape=jax.ShapeDtypeStruct((R, Cc), F32),
        compiler_params=_params(("parallel",)),
    )(chip, pieces, got)


PACK_COLS = 1024
SMALL_ROWS = 32


def kernel(x, hgrn_w_in, hgrn_lb_logits, hgrn_gnorm_w, hgrn_w_out, swa_w_q, swa_sinks, swa_w_out, shared_w_kv, rel_bias, ffn_w_in, ffn_conv_w, ffn_conv_b, ffn_w_out, ln_mix_g, ln_mix_b, ln_ffn_g, ln_ffn_b, loss_target, m_hgrn_w_in, m_hgrn_lb_logits, m_hgrn_gnorm_w, m_hgrn_w_out, m_swa_w_q, m_swa_sinks, m_swa_w_out, m_shared_w_kv, m_rel_bias, m_ffn_w_in, m_ffn_conv_w, m_ffn_conv_b, m_ffn_w_out, m_ln_mix_g, m_ln_mix_b, m_ln_ffn_g, m_ln_ffn_b, v_hgrn_w_in, v_hgrn_lb_logits, v_hgrn_gnorm_w, v_hgrn_w_out, v_swa_w_q, v_swa_sinks, v_swa_w_out, v_shared_w_kv, v_rel_bias, v_ffn_w_in, v_ffn_conv_w, v_ffn_conv_b, v_ffn_w_out, v_ln_mix_g, v_ln_mix_b, v_ln_ffn_g, v_ln_ffn_b):
    xi, yi, ci = _place()
    chip = 2 * xi + yi
    Dm = D_MODEL
    FC = 2 * FFN_DIM // N_CHIPS
    Fo = FFN_DIM // N_CHIPS
    Dq = Dm // N_CHIPS
    bf = lambda a: a.astype(BF16)

    small = jnp.concatenate([hgrn_lb_logits.reshape(-1), ffn_conv_w.reshape(-1)])
    n_small = small.shape[0]
    bits = jnp.concatenate(_split3(small))
    bits = jnp.pad(bits, (0, SMALL_ROWS * PACK_COLS - 3 * n_small)).reshape(SMALL_ROWS, PACK_COLS)
    groups = [[bf(hgrn_w_in[0]), bf(hgrn_w_out[0]), bits],
              [bf(swa_w_q[0]), bf(swa_w_out[0]), bf(shared_w_kv), bf(ffn_w_in[0]), bf(ffn_w_out[0])],
              [bf(ffn_w_in[1]), bf(ffn_w_out[1])]]

    def gathered(k, landed):
        return _fill_sibling(landed, name=f"gather_w{k}_fill")

    handle0, token0 = _gather_start(groups[0], None, name="gather_w0_start")
    xb = _to_bf16(x[0], name="x_to_bf16", after=token0)
    corner = lambda a: a[:2 * SUBLANES, :LANES]
    casts_done = corner(xb) + sum(corner(a) for a in groups[1] + groups[2])
    w_in, w_hg_out, small_all = gathered(0, _gather_wait(handle0, casts_done, name="gather_w0_wait"))
    handle1, token1 = _gather_start(groups[1], w_in, name="gather_w1_start")
    parts = small_all.reshape(N_CHIPS, -1)[:, :3 * n_small].reshape(N_CHIPS, 3, n_small).astype(F32)
    vals = (parts[:, 0] + parts[:, 1]) + parts[:, 2]
    lb_full = vals[:, :2 * Dq].reshape(N_CHIPS, 2, Dq).transpose(1, 0, 2).reshape(2, Dm)
    cw_full = vals[:, 2 * Dq:].reshape(N_CHIPS, DEPTH, 3, FC).transpose(1, 2, 0, 3).reshape(DEPTH, 3, 2 * FFN_DIM)

    got = {"handle": handle1}

    def more_weights(k, after):
        ws = gathered(k, _gather_wait(got.pop("handle"), after, name=f"gather_w{k}_wait"))
        if k == 1:
            got["handle"], token2 = _gather_start(groups[2], ws[0], name="gather_w2_start")
            w_q, w_o, w_kv, w_fi, w_fo = ws
            got.update(ffn_in={0: w_fi}, ffn_out={0: w_fo.reshape(FFN_DIM, Dm)})
            return {"sw_q": w_q.reshape(Dm, Dm), "sw_out": w_o.reshape(Dm, Dm), "kv": w_kv.reshape(Dm, 2 * KV_DIM),
                    "token": token2, "ffn_in": got["ffn_in"], "ffn_out": got["ffn_out"]}
        w_fi, w_fo = ws
        return {"ffn_in": {**got["ffn_in"], 1: w_fi}, "ffn_out": {**got["ffn_out"], 1: w_fo.reshape(FFN_DIM, Dm)}}

    w = {
        "hg_in": w_in, "hg_out": w_hg_out.reshape(Dm, Dm), "token": token1,
        "lb_logits": lb_full, "gnorm": hgrn_gnorm_w, "sinks": swa_sinks, "rel_bias": rel_bias,
        "conv_w_a": [cw_full[l, :, :FFN_DIM] for l in range(DEPTH)],
        "conv_w_b": [cw_full[l, :, FFN_DIM:] for l in range(DEPTH)],
        "conv_b_a": [ffn_conv_b[l:l + 1, :FFN_DIM] for l in range(DEPTH)],
        "conv_b_b": [ffn_conv_b[l:l + 1, FFN_DIM:] for l in range(DEPTH)],
        "ln_mix_g": [ln_mix_g[l:l + 1] for l in range(DEPTH)], "ln_mix_b": [ln_mix_b[l:l + 1] for l in range(DEPTH)],
        "ln_ffn_g": [ln_ffn_g[l:l + 1] for l in range(DEPTH)], "ln_ffn_b": [ln_ffn_b[l:l + 1] for l in range(DEPTH)],
    }

    sent = {}

    def emit(k, gd):
        rows4 = lambda a: a.reshape(N_CHIPS, -1, a.shape[-1])
        order = {1: ["sw_q", "sw_out", "kv", "ffn_in", "ffn_out"], 2: ["ffn_in", "ffn_out", "hg_out"], 3: ["hg_in"]}[k]
        as_pieces = lambda a, nme: a if nme in ("ffn_in", "hg_in") else rows4(a)
        handle, token = _scatter_start([as_pieces(gd[nme][1], nme) for nme in order], name=f"scatter_g{k}_start")
        sent[k] = (handle, [as_pieces(gd[nme][0], nme) for nme in order])
        return token

    loss_tile, grad_x, g = _local_step(x[0], xb, loss_target[0], w, more_weights, emit)

    wts = dict(hgrn_w_in=hgrn_w_in, hgrn_lb_logits=hgrn_lb_logits, hgrn_gnorm_w=hgrn_gnorm_w, hgrn_w_out=hgrn_w_out,
               swa_w_q=swa_w_q, swa_sinks=swa_sinks, swa_w_out=swa_w_out, shared_w_kv=shared_w_kv, rel_bias=rel_bias,
               ffn_w_in=ffn_w_in, ffn_conv_w=ffn_conv_w, ffn_conv_b=ffn_conv_b, ffn_w_out=ffn_w_out,
               ln_mix_g=ln_mix_g, ln_mix_b=ln_mix_b, ln_ffn_g=ln_ffn_g, ln_ffn_b=ln_ffn_b)
    ms = dict(hgrn_w_in=m_hgrn_w_in, hgrn_lb_logits=m_hgrn_lb_logits, hgrn_gnorm_w=m_hgrn_gnorm_w, hgrn_w_out=m_hgrn_w_out,
              swa_w_q=m_swa_w_q, swa_sinks=m_swa_sinks, swa_w_out=m_swa_w_out, shared_w_kv=m_shared_w_kv, rel_bias=m_rel_bias,
              ffn_w_in=m_ffn_w_in, ffn_conv_w=m_ffn_conv_w, ffn_conv_b=m_ffn_conv_b, ffn_w_out=m_ffn_w_out,
              ln_mix_g=m_ln_mix_g, ln_mix_b=m_ln_mix_b, ln_ffn_g=m_ln_ffn_g, ln_ffn_b=m_ln_ffn_b)
    vs = dict(hgrn_w_in=v_hgrn_w_in, hgrn_lb_logits=v_hgrn_lb_logits, hgrn_gnorm_w=v_hgrn_gnorm_w, hgrn_w_out=v_hgrn_w_out,
              swa_w_q=v_swa_w_q, swa_sinks=v_swa_sinks, swa_w_out=v_swa_w_out, shared_w_kv=v_shared_w_kv, rel_bias=v_rel_bias,
              ffn_w_in=v_ffn_w_in, ffn_conv_w=v_ffn_conv_w, ffn_conv_b=v_ffn_conv_b, ffn_w_out=v_ffn_w_out,
              ln_mix_g=v_ln_mix_g, ln_mix_b=v_ln_mix_b, ln_ffn_g=v_ln_ffn_g, ln_ffn_b=v_ln_ffn_b)
    names = list(wts)
    grads, delta, new_m, new_v = {}, {}, {}, {}

    def update(n, ga, gb, layer=None, prev=None):
        r2 = lambda a: a.reshape(-1, a.shape[-1])
        rows = None if layer is None else (layer * ga.shape[0], ga.shape[0])
        return _adamw(r2(wts[n]), ga, gb, r2(ms[n]), r2(vs[n]), rows=rows, prev=prev,
                      name=f"adamw_{n}" + ("" if layer is None else f"_{layer}"))

    def keep(n, res):
        grads[n], delta[n], new_m[n], new_v[n] = [a.reshape(wts[n].shape) for a in res]

    chip1 = jnp.reshape(chip, (1,)).astype(jnp.int32)
    after, swaps = grad_x, {}
    for k in (1, 2, 3):
        handle, pieces = sent[k]
        lands = _scatter_wait(handle, after, name=f"scatter_g{k}_wait")
        parts = [_chip_sum(p, l, chip1, name=f"scatter_g{k}_sum{i}") for i, (p, l) in enumerate(zip(pieces, lands))]
        swaps[k], after = _swap_start(parts, name=f"scatter_g{k}_swap_start")
    for k in (1, 2, 3):
        parts, sibs = _swap_wait(swaps[k], after, name=f"scatter_g{k}_swap_wait")
        if k == 1:
            for n, ga, gb in zip(["swa_w_q", "swa_w_out", "shared_w_kv"], parts[:3], sibs[:3]):
                keep(n, update(n, ga, gb))
            ffn_in_1 = update("ffn_w_in", parts[3], sibs[3], layer=1)
            ffn_out_1 = update("ffn_w_out", parts[4], sibs[4], layer=1)
            after = ffn_out_1[3]
        elif k == 2:
            keep("ffn_w_in", update("ffn_w_in", parts[0], sibs[0], layer=0, prev=ffn_in_1))
            keep("ffn_w_out", update("ffn_w_out", parts[1], sibs[1], layer=0, prev=ffn_out_1))
            keep("hgrn_w_out", update("hgrn_w_out", parts[2], sibs[2]))
            after = new_v["hgrn_w_out"]
        else:
            keep("hgrn_w_in", update("hgrn_w_in", parts[0], sibs[0]))

    small_keys = ["lb_logits", "gnorm", "sinks", "rel_bias", "conv0", "conv1", "ln_mix0", "ln_mix1", "ln_ffn0", "ln_ffn1"]
    flat2 = lambda a: a.reshape(-1, a.shape[-1])
    sums = _sum8([loss_tile] + [flat2(g[k]) for k in small_keys], name="sum_small")
    loss = sums[0][0, 0]
    sg = {k: v.reshape(g[k].shape) for k, v in zip(small_keys, sums[1:])}
    conv = [sg["conv0"], sg["conv1"]]
    g_cw = jnp.stack([jnp.concatenate([conv[l][0, :3], conv[l][1, :3]], axis=1) for l in range(DEPTH)])
    g_cb = jnp.stack([jnp.concatenate([conv[l][0, 3], conv[l][1, 3]], axis=0) for l in range(DEPTH)])
    ln = lambda nme, r: jnp.stack([sg[nme + "0"][r], sg[nme + "1"][r]])
    small_g = dict(hgrn_lb_logits=lax.dynamic_slice_in_dim(sg["lb_logits"], chip * Dq, Dq, axis=1),
                   hgrn_gnorm_w=sg["gnorm"], swa_sinks=sg["sinks"], rel_bias=sg["rel_bias"],
                   ffn_conv_w=lax.dynamic_slice_in_dim(g_cw, chip * FC, FC, axis=2), ffn_conv_b=g_cb,
                   ln_mix_g=ln("ln_mix", 0), ln_mix_b=ln("ln_mix", 1), ln_ffn_g=ln("ln_ffn", 0), ln_ffn_b=ln("ln_ffn", 1))
    small_names = list(small_g)
    d_, m_, v_ = _adamw_small([flat2(wts[n]) for n in small_names], [flat2(small_g[n]) for n in small_names],
                              [flat2(ms[n]) for n in small_names], [flat2(vs[n]) for n in small_names], name="adamw_small")
    for n, a, b_, c_ in zip(small_names, d_, m_, v_):
        shp = wts[n].shape
        grads[n], delta[n], new_m[n], new_v[n] = small_g[n], a.reshape(shp), b_.reshape(shp), c_.reshape(shp)

    return (loss, grad_x[None], *[grads[n] for n in names], *[delta[n] for n in names],
            *[new_m[n] for n in names], *[new_v[n] for n in names])
```

```python
import math

import numpy as np
import jax
import jax.numpy as jnp
from jax import lax
from jax.experimental import pallas as pl
from jax.experimental.pallas import tpu as pltpu

F32 = jnp.float32
BF16 = jnp.bfloat16
MESH = pl.DeviceIdType.MESH

D_MODEL = 1024
DEPTH = 2
HG_HEADS = 8
HG_DIM = 128
SW_Q_HEADS = 16
SW_KV_HEADS = 4
SW_HEAD_DIM = 64
SW_GROUP = 4
SW_WINDOW = 128
REL_BUCKETS = 32
REL_MAX_DIST = 128
FFN_DIM = 2816
ALPHA = (2.0 * DEPTH) ** 0.25
LN_EPS = 1e-5
RMS_EPS = 1e-6
ADAM_LR = 0.001
ADAM_B1 = 0.9
ADAM_B2 = 0.999
ADAM_EPS = 1e-08
ADAM_WD = 0.01
ADAM_STEP = 10

VMEM_BYTES_V7X = 64 * 1024 * 1024
VMEM_LIMIT = VMEM_BYTES_V7X - 8 * 1024 * 1024
LANES = 128
SUBLANES = 8

HG_C = 64
HG_RB = 256
CONV_R = 128
N_CHIPS = 4
N_DEV = 8

ANY_SPEC = pl.BlockSpec(memory_space=pl.ANY)


def _after(body, n_in, after):
    if after is None:
        return body, [], ()

    def wrapped(*refs):
        return body(*refs[:n_in], *refs[n_in + 1:])

    return wrapped, [ANY_SPEC], (after,)


def _params(sem=None):
    return pltpu.CompilerParams(dimension_semantics=sem, vmem_limit_bytes=VMEM_LIMIT)


def _tile(n, pref, unit=LANES):
    if n <= pref:
        return n
    best = None
    for t in range(unit, pref + 1, unit):
        if n % t == 0:
            best = t
    assert best is not None, (n, pref, unit)
    return best


def _dot(a, b, ca, cb):
    nb = a.ndim - 2
    batch = tuple(range(nb))
    return lax.dot_general(a.astype(BF16), b.astype(BF16), (((nb + ca,), (nb + cb,)), (batch, batch)),
                           preferred_element_type=F32)


@jax.custom_vjp
def mm(a, b):
    return _dot(a, b, 1, 0)


@jax.custom_vjp
def mm_nt(a, b):
    return _dot(a, b, 1, 1)


@jax.custom_vjp
def mm_tn(a, b):
    return _dot(a, b, 0, 0)


mm.defvjp(lambda a, b: (mm(a, b), (a, b)), lambda r, ct: (mm_nt(ct, r[1]), mm_tn(r[0], ct)))
mm_nt.defvjp(lambda a, b: (mm_nt(a, b), (a, b)), lambda r, ct: (mm(ct, r[1]), mm_tn(ct, r[0])))
mm_tn.defvjp(lambda a, b: (mm_tn(a, b), (a, b)), lambda r, ct: (mm_nt(r[1], ct), mm(r[0], ct)))


def _split2(x):
    hi = x.astype(BF16)
    return hi, (x - hi.astype(F32)).astype(BF16)


@jax.custom_vjp
def _scores(qt, kt):
    return _dot(qt, kt, 1, 1)


def _scores_bwd(r, ct):
    (qh, ql), (kh, kl) = _split2(r[0]), _split2(r[1])
    return _dot(ct, kh, 1, 0) + _dot(ct, kl, 1, 0), _dot(ct, qh, 0, 0) + _dot(ct, ql, 0, 0)


_scores.defvjp(lambda a, b: (_scores(a, b), (a, b)), _scores_bwd)


def _split3(x):
    hi = x.astype(BF16)
    r1 = x - hi.astype(F32)
    mid = r1.astype(BF16)
    lo = (r1 - mid.astype(F32)).astype(BF16)
    return hi, mid, lo


def _cumsum_impl(x):
    ax = x.ndim - 2
    n = x.shape[ax]
    row = lax.broadcasted_iota(jnp.int32, x.shape, ax)
    d = 1
    while d < n:
        x = x + jnp.where(row >= d, pltpu.roll(x, d, ax), 0.0)
        d *= 2
    return x


def _cumsum_rev_impl(x):
    ax = x.ndim - 2
    n = x.shape[ax]
    row = lax.broadcasted_iota(jnp.int32, x.shape, ax)
    d = 1
    while d < n:
        x = x + jnp.where(row < n - d, pltpu.roll(x, n - d, ax), 0.0)
        d *= 2
    return x


@jax.custom_vjp
def _cumsum(x):
    return _cumsum_impl(x)


_cumsum.defvjp(lambda x: (_cumsum_impl(x), None), lambda _, ct: (_cumsum_rev_impl(ct),))


def _matmul(a, b, *, mode, name, out_dtype=F32, add=None, add_scale=1.0, tm=512, tn=1408, tk=1408, after=None,
            split_n=False, planes=None, also_bf16=False):
    P = b.shape[0] if planes else 1
    a2, b2 = a.shape[-2:], b.shape[-2:]
    (M, K) = a2 if mode[0] == "n" else a2[::-1]
    (K2, N) = b2 if mode[1] == "n" else b2[::-1]
    assert K == K2, (a.shape, b.shape, mode)
    assert a.ndim == (3 if planes == "k" else 2) and b.ndim == (3 if planes else 2)
    tm, tn, tk = _tile(M, tm), _tile(N, tn), _tile(K, tk)
    nj, nkp = N // tn, K // tk
    nk = nkp * (P if planes == "k" else 1)
    ca, cb = (1 if mode[0] == "n" else 0), (0 if mode[1] == "n" else 1)
    a_blk, a_idx = ((tk, tm), lambda i, k: (k, i)) if mode[0] == "t" else ((tm, tk), lambda i, k: (i, k))
    b_blk, b_idx = ((tn, tk), lambda k, j: (j, k)) if mode[1] == "t" else ((tk, tn), lambda k, j: (k, j))
    if planes == "k":
        a_spec = pl.BlockSpec((None,) + a_blk, lambda i, j, k: (k // nkp,) + a_idx(i, k % nkp))
        b_spec = pl.BlockSpec((None,) + b_blk, lambda i, j, k: (k // nkp,) + b_idx(k % nkp, j))
    else:
        a_spec = pl.BlockSpec(a_blk, lambda i, j, k: a_idx(i, k))
        b_spec = (pl.BlockSpec((None,) + b_blk, lambda i, j, k: (j // nj,) + b_idx(k, j % nj)) if planes == "n"
                  else pl.BlockSpec(b_blk, lambda i, j, k: b_idx(k, j)))
    if split_n:
        o_spec, out_shape = pl.BlockSpec((None, tm, tn), lambda i, j, k: (j, i, 0)), (P * nj if planes == "n" else nj, M, tn)
    elif planes == "n":
        o_spec, out_shape = pl.BlockSpec((None, tm, tn), lambda i, j, k: (j // nj, i, j % nj)), (P, M, N)
    else:
        o_spec, out_shape = pl.BlockSpec((tm, tn), lambda i, j, k: (i, j)), (M, N)
    has_add = add is not None
    assert not (has_add and (split_n or planes == "n"))

    def finish(r, add_ref, o_refs):
        if has_add:
            r = r + add_scale * add_ref[...]
        o_refs[0][...] = r.astype(out_dtype)
        if also_bf16:
            o_refs[1][...] = r.astype(BF16)

    def body(*refs):
        a_ref, b_ref = refs[:2]
        add_ref = refs[2] if has_add else None
        first = 3 if has_add else 2
        o_ref = refs[first:first + (2 if also_bf16 else 1)]
        if nk == 1:
            finish(_dot(a_ref[...], b_ref[...], ca, cb), add_ref, o_ref)
            return
        acc_ref = refs[-1]
        k = pl.program_id(2)

        @pl.when(k == 0)
        def _():
            acc_ref[...] = jnp.zeros_like(acc_ref)

        acc_ref[...] += _dot(a_ref[...], b_ref[...], ca, cb)

        @pl.when(k == nk - 1)
        def _():
            finish(acc_ref[...], add_ref, o_ref)

    in_specs = [a_spec, b_spec] + ([o_spec] if has_add else [])
    args = (a, b) + ((add,) if has_add else ())
    body, xs, xa = _after(body, len(args), after)
    in_specs, args = in_specs + xs, args + xa
    out_shapes = [jax.ShapeDtypeStruct(out_shape, out_dtype)] + ([jax.ShapeDtypeStruct(out_shape, BF16)] if also_bf16 else [])
    out = pl.pallas_call(
        body, name=name, grid=(M // tm, nj * (P if planes == "n" else 1), nk), in_specs=in_specs,
        out_specs=[o_spec] * len(out_shapes), out_shape=out_shapes,
        scratch_shapes=[pltpu.VMEM((tm, tn), F32)] if nk > 1 else [],
        compiler_params=_params(("parallel", "parallel", "arbitrary")),
    )(*args)
    return tuple(out) if also_bf16 else out[0]


def _matmul_planes_nn(a, b, *, name, tm=512, after=None):
    (M, K), (P, K2, N) = a.shape, b.shape
    assert K == K2
    tm = _tile(M, tm, 2 * SUBLANES)

    def body(a_ref, b_ref, o_ref):
        for p in range(P):
            o_ref[p] = _dot(a_ref[...], b_ref[p], 1, 0).astype(BF16)

    body, xs, xa = _after(body, 2, after)
    return pl.pallas_call(
        body, name=name, grid=(M // tm,),
        in_specs=[pl.BlockSpec((tm, K), lambda i: (i, 0)), pl.BlockSpec((P, K, N), lambda i: (0, 0, 0))] + xs,
        out_specs=pl.BlockSpec((P, tm, N), lambda i: (0, i, 0)), out_shape=jax.ShapeDtypeStruct((P, M, N), BF16),
        compiler_params=_params(("parallel",)),
    )(a, b, *xa)


def _matmul_planes_nt(a, b, add, *, add_scale, name, tm=512, after=None):
    (P, M, K), (P2, N, K2) = a.shape, b.shape
    assert P == P2 and K == K2 and add.shape == (M, N)
    tm = _tile(M, tm, SUBLANES)

    def body(a_ref, b_ref, add_ref, o_ref):
        r = add_scale * add_ref[...]
        for p in range(P):
            r = r + _dot(a_ref[p], b_ref[p], 1, 1)
        o_ref[...] = r

    row = pl.BlockSpec((tm, N), lambda i: (i, 0))
    body, xs, xa = _after(body, 3, after)
    return pl.pallas_call(
        body, name=name, grid=(M // tm,),
        in_specs=[pl.BlockSpec((P, tm, K), lambda i: (0, i, 0)), pl.BlockSpec((P, N, K), lambda i: (0, 0, 0)), row] + xs,
        out_specs=row, out_shape=jax.ShapeDtypeStruct((M, N), F32),
        compiler_params=_params(("parallel",)),
    )(a, b, add, *xa)


def _ln(z, g, b):
    mu = jnp.mean(z, axis=-1, keepdims=True)
    zc = z - mu
    var = jnp.mean(zc * zc, axis=-1, keepdims=True)
    return zc * lax.rsqrt(var + LN_EPS) * g + b


def _matmul_ln(a, b, h, g, bias, *, name, tgt=None, tm=512, a_t=False):
    (T, K), (K2, Dm) = (a.shape[::-1] if a_t else a.shape), b.shape
    assert K == K2 and h.shape == (T, Dm)
    tm = _tile(T, tm, SUBLANES)
    last = tgt is not None

    def body(*refs):
        a_ref, b_ref, h_ref, g_ref, bias_ref = refs[:5]
        z = ALPHA * h_ref[...] + _dot(a_ref[...], b_ref[...], 0 if a_t else 1, 0)
        if not last:
            z_ref, y_ref, yb_ref = refs[5:]
            y = _ln(z, g_ref[...], bias_ref[...])
            z_ref[...] = z
            y_ref[...] = y
            yb_ref[...] = y.astype(BF16)
            return
        t_ref, dz_ref, dzb_ref, dgb_ref, l_ref, da_ref = refs[5:]

        @pl.when(pl.program_id(0) == 0)
        def _():
            dgb_ref[...] = jnp.zeros_like(dgb_ref)
            l_ref[...] = jnp.zeros_like(l_ref)

        y, vjp = jax.vjp(_ln, z, g_ref[...], bias_ref[...])
        e = y - t_ref[...]
        dz, dg, db = vjp(e * (1.0 / Dm))
        l_ref[...] += 0.5 * jnp.sum(jnp.mean(e * e, axis=-1, keepdims=True), axis=0, keepdims=True)
        dzb = dz.astype(BF16)
        dz_ref[...] = dz
        dzb_ref[...] = dzb
        dgb_ref[...] += jnp.concatenate([dg, db], axis=0)
        da_ref[...] = _dot(dzb, b_ref[...], 1, 1).astype(BF16)

    row = pl.BlockSpec((tm, Dm), lambda i: (i, 0))
    vec = pl.BlockSpec((1, Dm), lambda i: (0, 0))
    a_spec = pl.BlockSpec((K, tm), lambda i: (0, i)) if a_t else pl.BlockSpec((tm, K), lambda i: (i, 0))
    in_specs = [a_spec, pl.BlockSpec((K, Dm), lambda i: (0, 0)), row, vec, vec]
    f32, b16 = jax.ShapeDtypeStruct((T, Dm), F32), jax.ShapeDtypeStruct((T, Dm), BF16)
    if not last:
        return pl.pallas_call(
            body, name=name, grid=(T // tm,), in_specs=in_specs, out_specs=[row, row, row], out_shape=[f32, f32, b16],
            compiler_params=_params(("parallel",)),
        )(a, b, h, g, bias)
    assert not a_t
    return pl.pallas_call(
        body, name=name, grid=(T // tm,), in_specs=in_specs + [row],
        out_specs=[row, row, pl.BlockSpec((2, Dm), lambda i: (0, 0)), pl.BlockSpec((SUBLANES, LANES), lambda i: (0, 0)), a_spec],
        out_shape=[f32, b16, jax.ShapeDtypeStruct((2, Dm), F32), jax.ShapeDtypeStruct((SUBLANES, LANES), F32),
                   jax.ShapeDtypeStruct((T, K), BF16)],
        compiler_params=_params(("arbitrary",)),
    )(a, b, h, g, bias, tgt)


def _ln_bwd_matmul(dy, z, g, b, w, *, name, out_t=False, tm=512, after=None):
    T, Dm = z.shape
    N = w.shape[0]
    tm = _tile(T, tm, LANES if out_t else SUBLANES)

    def body(dy_ref, z_ref, g_ref, b_ref, w_ref, dz_ref, dzb_ref, dgb_ref, o_ref):
        @pl.when(pl.program_id(0) == 0)
        def _():
            dgb_ref[...] = jnp.zeros_like(dgb_ref)

        _, vjp = jax.vjp(_ln, z_ref[...], g_ref[...], b_ref[...])
        dz, dg, db = vjp(dy_ref[...])
        dzb = dz.astype(BF16)
        dz_ref[...] = dz
        dzb_ref[...] = dzb
        dgb_ref[...] += jnp.concatenate([dg, db], axis=0)
        o_ref[...] = (_dot(w_ref[...], dzb, 1, 1) if out_t else _dot(dzb, w_ref[...], 1, 1)).astype(BF16)

    row = pl.BlockSpec((tm, Dm), lambda i: (i, 0))
    vec = pl.BlockSpec((1, Dm), lambda i: (0, 0))
    o_spec = pl.BlockSpec((N, tm), lambda i: (0, i)) if out_t else pl.BlockSpec((tm, N), lambda i: (i, 0))
    body, xs, xa = _after(body, 5, after)
    return pl.pallas_call(
        body, name=name, grid=(T // tm,), in_specs=[row, row, vec, vec, pl.BlockSpec((N, Dm), lambda i: (0, 0))] + xs,
        out_specs=[row, row, pl.BlockSpec((2, Dm), lambda i: (0, 0)), o_spec],
        out_shape=[jax.ShapeDtypeStruct((T, Dm), F32), jax.ShapeDtypeStruct((T, Dm), BF16),
                   jax.ShapeDtypeStruct((2, Dm), F32), jax.ShapeDtypeStruct((N, T) if out_t else (T, N), BF16)],
        compiler_params=_params(("arbitrary",)),
    )(dy, z, g, b, w, *xa)


def _hg_chunk(qr, fr, ir, gr, l0, l1, gw, st):
    C = qr.shape[-2]
    row = lax.broadcasted_iota(jnp.int32, qr.shape, qr.ndim - 2)
    lb = jax.nn.sigmoid(l0 - l1)
    fg = lb + (1.0 - lb) * jax.nn.sigmoid(fr)
    b = _cumsum(jnp.log(fg))
    q = jax.nn.silu(qr)
    k = 1.0 - fg
    bmid = lax.stop_gradient(jnp.sum(jnp.where(row == C // 2 - 1, b, 0.0), axis=-2, keepdims=True))
    bl = jnp.sum(jnp.where(row == C - 1, b, 0.0), axis=-2, keepdims=True)
    o = mm_nt(q * jnp.exp(b), st)
    sc = _scores(q * jnp.exp(b - bmid), k * jnp.exp(bmid - b))
    ti = lax.broadcasted_iota(jnp.int32, (C, C), 0)
    si = lax.broadcasted_iota(jnp.int32, (C, C), 1)
    sc = jnp.where(si <= ti, sc, 0.0)
    o = o + mm(sc, ir)
    st_new = st * jnp.exp(bl) + mm_tn(ir, k * jnp.exp(bl - b))
    on = o * lax.rsqrt(jnp.mean(o * o, axis=-1, keepdims=True) + RMS_EPS)
    return on * gw * jax.nn.silu(gr), st_new


def _heads(ref, rows):
    return jnp.stack([ref[rows, h * HG_DIM:(h + 1) * HG_DIM].astype(F32) for h in range(HG_HEADS)])


def _unheads(x):
    return jnp.concatenate([x[h] for h in range(HG_HEADS)], axis=-1)


def _hgrn_fwd(pre, lbl, gw, *, name):
    _, T, Dm = pre.shape
    rb = min(HG_RB, T)
    C = min(HG_C, rb)
    ncb = rb // C

    def body(pre_ref, lbl_ref, gw_ref, o_ref, st_ref, s_ref):
        @pl.when(pl.program_id(0) == 0)
        def _():
            s_ref[...] = jnp.zeros_like(s_ref)

        def chunk(ci, carry):
            r0 = pl.multiple_of(ci * C, C)
            rows = pl.ds(r0, C)
            st = s_ref[...]
            st_ref[ci] = st
            out, st_new = _hg_chunk(*[_heads(pre_ref.at[j], rows) for j in range(4)],
                                    _heads(lbl_ref, slice(0, 1)), _heads(lbl_ref, slice(1, 2)), gw_ref[...], st)
            o_ref[rows, :] = _unheads(out).astype(BF16)
            s_ref[...] = st_new
            return carry

        lax.fori_loop(0, ncb, chunk, 0, unroll=True)

    row = pl.BlockSpec((rb, Dm), lambda n: (n, 0))
    return pl.pallas_call(
        body, name=name, grid=(T // rb,),
        in_specs=[pl.BlockSpec((4, rb, Dm), lambda n: (0, n, 0)), pl.BlockSpec((2, Dm), lambda n: (0, 0)),
                  pl.BlockSpec((1, HG_DIM), lambda n: (0, 0))],
        out_specs=[row, pl.BlockSpec((ncb, HG_HEADS, HG_DIM, HG_DIM), lambda n: (n, 0, 0, 0))],
        out_shape=[jax.ShapeDtypeStruct((T, Dm), BF16),
                   jax.ShapeDtypeStruct((T // C, HG_HEADS, HG_DIM, HG_DIM), F32)],
        scratch_shapes=[pltpu.VMEM((HG_HEADS, HG_DIM, HG_DIM), F32)],
        compiler_params=_params(("arbitrary",)),
    )(pre, lbl, gw)


def _hgrn_bwd(pre, lbl, gw, states, dout, *, name, after=None):
    _, T, Dm = pre.shape
    rb = min(HG_RB, T)
    C = min(HG_C, rb)
    ncb = rb // C
    nb = T // rb

    def body(pre_ref, lbl_ref, gw_ref, st_ref, do_ref, dpre_ref, dlbl_ref, dgw_ref, ds_ref):
        @pl.when(pl.program_id(0) == 0)
        def _():
            ds_ref[...] = jnp.zeros_like(ds_ref)
            dlbl_ref[...] = jnp.zeros_like(dlbl_ref)
            dgw_ref[...] = jnp.zeros_like(dgw_ref)

        def chunk(cj, carry):
            ci = ncb - 1 - cj
            r0 = pl.multiple_of(ci * C, C)
            rows = pl.ds(r0, C)
            _, vjp = jax.vjp(_hg_chunk, *[_heads(pre_ref.at[j], rows) for j in range(4)],
                             _heads(lbl_ref, slice(0, 1)), _heads(lbl_ref, slice(1, 2)), gw_ref[...], st_ref[ci])
            *dpre, dl0, dl1, dgw, dst = vjp((_heads(do_ref, rows), ds_ref[...]))
            for j in range(4):
                dpre_ref[j, rows, :] = _unheads(dpre[j]).astype(BF16)
            dlbl_ref[0:1, :] += _unheads(dl0)
            dlbl_ref[1:2, :] += _unheads(dl1)
            dgw_ref[...] += dgw
            ds_ref[...] = dst
            return carry

        lax.fori_loop(0, ncb, chunk, 0, unroll=True)

    row = pl.BlockSpec((rb, Dm), lambda n: (nb - 1 - n, 0))
    lsp = pl.BlockSpec((2, Dm), lambda n: (0, 0))
    gsp = pl.BlockSpec((1, HG_DIM), lambda n: (0, 0))
    pre_spec = pl.BlockSpec((4, rb, Dm), lambda n: (0, nb - 1 - n, 0))
    body, xs, xa = _after(body, 5, after)
    return pl.pallas_call(
        body, name=name, grid=(nb,),
        in_specs=[pre_spec, lsp, gsp, pl.BlockSpec((ncb, HG_HEADS, HG_DIM, HG_DIM), lambda n: (nb - 1 - n, 0, 0, 0)), row] + xs,
        out_specs=[pre_spec, lsp, gsp],
        out_shape=[jax.ShapeDtypeStruct((4, T, Dm), BF16), jax.ShapeDtypeStruct((2, Dm), F32),
                   jax.ShapeDtypeStruct((1, HG_DIM), F32)],
        scratch_shapes=[pltpu.VMEM((HG_HEADS, HG_DIM, HG_DIM), F32)],
        compiler_params=_params(("arbitrary",)),
    )(pre, lbl, gw, states, dout, *xa)


CONV_HALO = 2 * SUBLANES


def _conv_rows(u_ref, scr, w, bias, r0, R):
    cur = u_ref[pl.ds(r0, R), :].astype(F32)
    p0 = pl.multiple_of(jnp.maximum(r0 - CONV_HALO, 0), CONV_HALO)
    scr[0:CONV_HALO, :] = jnp.where(r0 > 0, u_ref[pl.ds(p0, CONV_HALO), :].astype(F32), 0.0)
    scr[CONV_HALO:CONV_HALO + R, :] = cur
    s1 = scr[CONV_HALO - 1:CONV_HALO - 1 + R, :]
    s2 = scr[CONV_HALO - 2:CONV_HALO - 2 + R, :]
    return w[0:1, :] * s2 + w[1:2, :] * s1 + w[2:3, :] * cur + bias, cur, s1, s2


def _halves_spec(T, Fd):
    per = Fd // 2 // LANES
    return pl.BlockSpec((2, None, T, LANES), lambda j: (0, j // per, 0, j % per))


def _conv_gate_fwd(u, wa, wb, ba, bb, *, name):
    T, Fd = u.shape[2], 2 * u.shape[3]
    R = min(CONV_R, T)
    tc = LANES

    def body(u_ref, wa_ref, wb_ref, ba_ref, bb_ref, o_ref, sa, sb):
        wa_, wb_, ba_, bb_ = wa_ref[...], wb_ref[...], ba_ref[...], bb_ref[...]

        def step(ri, carry):
            r0 = pl.multiple_of(ri * R, R)
            ca = _conv_rows(u_ref.at[0], sa, wa_, ba_, r0, R)[0]
            cb = _conv_rows(u_ref.at[1], sb, wb_, bb_, r0, R)[0]
            o_ref[pl.ds(r0, R), :] = (jax.nn.silu(ca) * cb).astype(BF16)
            return carry

        lax.fori_loop(0, T // R, step, 0)

    col = pl.BlockSpec((T, tc), lambda j: (0, j))
    wsp = pl.BlockSpec((3, tc), lambda j: (0, j))
    bsp = pl.BlockSpec((1, tc), lambda j: (0, j))
    both = _halves_spec(T, Fd)
    return pl.pallas_call(
        body, name=name, grid=(Fd // tc,), in_specs=[both, wsp, wsp, bsp, bsp], out_specs=col,
        out_shape=jax.ShapeDtypeStruct((T, Fd), BF16),
        scratch_shapes=[pltpu.VMEM((CONV_HALO + R, tc), F32)] * 2,
        compiler_params=_params(("parallel",)),
    )(u, wa, wb, ba, bb)


def _conv_gate_bwd(u, wa, wb, ba, bb, dact, *, name):
    T, Fd = u.shape[2], 2 * u.shape[3]
    R = min(CONV_R, T)
    nr = T // R
    tc = LANES

    def body(u_ref, wa_ref, wb_ref, ba_ref, bb_ref, da_ref,
             du_ref, dp_ref, sa, sb, sda, sdb):
        wa_, wb_, ba_, bb_ = wa_ref[...], wb_ref[...], ba_ref[...], bb_ref[...]
        sda[R:R + SUBLANES, :] = jnp.zeros((SUBLANES, tc), F32)
        sdb[R:R + SUBLANES, :] = jnp.zeros((SUBLANES, tc), F32)

        def taps(dc, cur, s1, s2):
            return jnp.concatenate([jnp.sum(dc * s2, axis=0, keepdims=True), jnp.sum(dc * s1, axis=0, keepdims=True),
                                    jnp.sum(dc * cur, axis=0, keepdims=True)], axis=0)

        def du_rows(sd, dc, w):
            sd[0:R, :] = dc
            du = w[2:3, :] * dc + w[1:2, :] * sd[1:1 + R, :] + w[0:1, :] * sd[2:2 + R, :]
            sd[R:R + SUBLANES, :] = dc[0:SUBLANES]
            return du

        def step(rj, carry):
            dwa, dwb, dba, dbb = carry
            r0 = pl.multiple_of((nr - 1 - rj) * R, R)
            ca, cura, s1a, s2a = _conv_rows(u_ref.at[0], sa, wa_, ba_, r0, R)
            cb, curb, s1b, s2b = _conv_rows(u_ref.at[1], sb, wb_, bb_, r0, R)
            dact_ = da_ref[pl.ds(r0, R), :].astype(F32)
            sg = jax.nn.sigmoid(ca)
            dca = dact_ * cb * (sg * (1.0 + ca * (1.0 - sg)))
            dcb = dact_ * (ca * sg)
            du_ref[0, pl.ds(r0, R), :] = du_rows(sda, dca, wa_).astype(BF16)
            du_ref[1, pl.ds(r0, R), :] = du_rows(sdb, dcb, wb_).astype(BF16)
            return (dwa + taps(dca, cura, s1a, s2a), dwb + taps(dcb, curb, s1b, s2b),
                    dba + jnp.sum(dca, axis=0, keepdims=True), dbb + jnp.sum(dcb, axis=0, keepdims=True))

        z3 = jnp.zeros((3, tc), F32)
        z1 = jnp.zeros((1, tc), F32)
        dwa, dwb, dba, dbb = lax.fori_loop(0, nr, step, (z3, z3, z1, z1))
        dp_ref[0] = jnp.concatenate([dwa, dba], axis=0)
        dp_ref[1] = jnp.concatenate([dwb, dbb], axis=0)

    col = pl.BlockSpec((T, tc), lambda j: (0, j))
    wsp = pl.BlockSpec((3, tc), lambda j: (0, j))
    bsp = pl.BlockSpec((1, tc), lambda j: (0, j))
    both = _halves_spec(T, Fd)
    return pl.pallas_call(
        body, name=name, grid=(Fd // tc,), in_specs=[both, wsp, wsp, bsp, bsp, col],
        out_specs=[both, pl.BlockSpec((2, 4, tc), lambda j: (0, 0, j))],
        out_shape=[jax.ShapeDtypeStruct(u.shape, BF16), jax.ShapeDtypeStruct((2, 4, Fd), F32)],
        scratch_shapes=[pltpu.VMEM((CONV_HALO + R, tc), F32)] * 2 + [pltpu.VMEM((R + SUBLANES, tc), F32)] * 2,
        compiler_params=_params(("parallel",)),
    )(u, wa, wb, ba, bb, dact)


def _bucket_index():
    t = np.arange(SW_WINDOW)[None, :] + SW_WINDOW
    s = np.arange(2 * SW_WINDOW)[:, None]
    dist = np.maximum(t - s, 0)
    exact = REL_BUCKETS // 2
    d = np.maximum(dist, 1).astype(np.float32)
    log_b = exact + (np.log(d / np.float32(exact)) / np.float32(math.log(REL_MAX_DIST / exact))
                     * np.float32(REL_BUCKETS - exact)).astype(np.int32)
    bucket = np.where(dist < exact, dist, np.minimum(log_b, REL_BUCKETS - 1))
    return bucket.astype(np.int32).reshape(1, -1)


BIAS_COLS = SW_WINDOW * 2 * SW_WINDOW
BIAS_TILE = 4096


def _bias_from_table(table, bucket, *, name):
    def body(t_ref, idx_ref, o_ref):
        onehot = (lax.broadcasted_iota(jnp.int32, (REL_BUCKETS, BIAS_TILE), 0) == idx_ref[...]).astype(BF16)
        acc = jnp.zeros((SW_Q_HEADS, BIAS_TILE), F32)
        for piece in _split3(t_ref[...]):
            acc = acc + lax.dot_general(piece, onehot, (((0,), (0,)), ((), ())), preferred_element_type=F32)
        o_ref[...] = acc

    return pl.pallas_call(
        body, name=name, grid=(BIAS_COLS // BIAS_TILE,),
        in_specs=[pl.BlockSpec((REL_BUCKETS, SW_Q_HEADS), lambda j: (0, 0)), pl.BlockSpec((1, BIAS_TILE), lambda j: (0, j))],
        out_specs=pl.BlockSpec((SW_Q_HEADS, BIAS_TILE), lambda j: (0, j)),
        out_shape=jax.ShapeDtypeStruct((SW_Q_HEADS, BIAS_COLS), F32),
        compiler_params=_params(("parallel",)),
    )(table, bucket)


def _table_grad(dbias, bucket, *, name):
    def body(d_ref, idx_ref, o_ref):
        @pl.when(pl.program_id(0) == 0)
        def _():
            o_ref[...] = jnp.zeros_like(o_ref)

        onehot = (lax.broadcasted_iota(jnp.int32, (REL_BUCKETS, BIAS_TILE), 0) == idx_ref[...]).astype(BF16)
        acc = jnp.zeros((REL_BUCKETS, SW_Q_HEADS), F32)
        for piece in _split3(d_ref[...]):
            acc = acc + lax.dot_general(onehot, piece, (((1,), (1,)), ((), ())), preferred_element_type=F32)
        o_ref[...] += acc

    return pl.pallas_call(
        body, name=name, grid=(BIAS_COLS // BIAS_TILE,),
        in_specs=[pl.BlockSpec((SW_Q_HEADS, BIAS_TILE), lambda j: (0, j)), pl.BlockSpec((1, BIAS_TILE), lambda j: (0, j))],
        out_specs=pl.BlockSpec((REL_BUCKETS, SW_Q_HEADS), lambda j: (0, 0)),
        out_shape=jax.ShapeDtypeStruct((REL_BUCKETS, SW_Q_HEADS), F32),
        compiler_params=_params(("arbitrary",)),
    )(dbias, bucket)


KV_DIM = SW_KV_HEADS * SW_HEAD_DIM
GROUP_ROWS = SW_GROUP * SW_HEAD_DIM
GROUP_LANES = SW_GROUP * SW_WINDOW


def _band_mask(n):
    s = lax.broadcasted_iota(jnp.int32, (2 * SW_WINDOW, GROUP_LANES), 0)
    t = (lax.broadcasted_iota(jnp.int32, (2 * SW_WINDOW, GROUP_LANES), 1) & (SW_WINDOW - 1)) + SW_WINDOW
    dist = t - s
    return (dist >= 0) & (dist < SW_WINDOW) & ((n > 0) | (s >= SW_WINDOW))


def _side_by_side(x_ref, g):
    r0 = g * GROUP_ROWS
    return jnp.concatenate([x_ref[r0 + r * SW_HEAD_DIM:r0 + (r + 1) * SW_HEAD_DIM, :] for r in range(SW_GROUP)], axis=1)


def _group_inputs(bias_ref, sink_ref, g):
    heads = range(g * SW_GROUP, (g + 1) * SW_GROUP)
    bias = jnp.concatenate([bias_ref[h] for h in heads], axis=1)
    sink = jnp.concatenate([jnp.broadcast_to(sink_ref[:, h:h + 1], (1, SW_WINDOW)) for h in heads], axis=1)
    return heads, bias, sink


def _kv_pair(kvp_ref, kvc_ref, g):
    ks = slice(g * SW_HEAD_DIM, (g + 1) * SW_HEAD_DIM)
    vs = slice(KV_DIM + g * SW_HEAD_DIM, KV_DIM + (g + 1) * SW_HEAD_DIM)
    kk = jnp.concatenate([kvp_ref[:, ks], kvc_ref[:, ks]], axis=0)
    vv = jnp.concatenate([kvp_ref[:, vs], kvc_ref[:, vs]], axis=0)
    return kk, vv, ks, vs


def _col_max(x):
    return jnp.max(x, axis=0, keepdims=True)


def _col_sum(x):
    return jnp.sum(x, axis=0, keepdims=True)


def _attn_fwd(qt, kv, bias, sinks, *, name):
    Dm, T = qt.shape
    W = SW_WINDOW

    def body(q_ref, kvc_ref, kvp_ref, bias_ref, sink_ref, o_ref):
        mask = _band_mask(pl.program_id(0))
        G = range(SW_KV_HEADS)
        ins = [_group_inputs(bias_ref, sink_ref, g) for g in G]
        kvs = [_kv_pair(kvp_ref, kvc_ref, g) for g in G]
        q = [_side_by_side(q_ref, g) for g in G]
        lg = [jnp.where(mask, mm(kvs[g][0], q[g]) * (SW_HEAD_DIM ** -0.5) + ins[g][1], -jnp.inf) for g in G]
        m = [jnp.maximum(_col_max(lg[g]), ins[g][2]) for g in G]
        p = [jnp.exp(lg[g] - m[g]) for g in G]
        den = [_col_sum(p[g]) + jnp.exp(ins[g][2] - m[g]) for g in G]
        o = [mm_tn(kvs[g][1], p[g]) / den[g] for g in G]
        for g in G:
            for r in range(SW_GROUP):
                o_ref[g * GROUP_ROWS + r * SW_HEAD_DIM:g * GROUP_ROWS + (r + 1) * SW_HEAD_DIM, :] = (
                    o[g][:, r * W:(r + 1) * W].astype(BF16))

    return pl.pallas_call(
        body, name=name, grid=(T // W,),
        in_specs=[pl.BlockSpec((Dm, W), lambda n: (0, n)),
                  pl.BlockSpec((W, 2 * KV_DIM), lambda n: (n, 0)),
                  pl.BlockSpec((W, 2 * KV_DIM), lambda n: (jnp.maximum(n - 1, 0), 0)),
                  pl.BlockSpec((SW_Q_HEADS, 2 * W, W), lambda n: (0, 0, 0)),
                  pl.BlockSpec((1, SW_Q_HEADS), lambda n: (0, 0))],
        out_specs=pl.BlockSpec((Dm, W), lambda n: (0, n)),
        out_shape=jax.ShapeDtypeStruct((Dm, T), BF16),
        compiler_params=_params(("parallel",)),
    )(qt, kv, kv, bias, sinks)


def _attn_bwd(qt, kv, bias, sinks, dot, *, name):
    Dm, T = qt.shape
    W = SW_WINDOW
    nb = T // W

    def body(q_ref, kvc_ref, kvp_ref, bias_ref, sink_ref, do_ref,
             dq_ref, dkv_ref, dbias_ref, dsink_ref, carry_ref):
        @pl.when(pl.program_id(0) == 0)
        def _():
            carry_ref[...] = jnp.zeros_like(carry_ref)
            dbias_ref[...] = jnp.zeros_like(dbias_ref)
            dsink_ref[...] = jnp.zeros_like(dsink_ref)

        n = nb - 1 - pl.program_id(0)
        mask = _band_mask(n)
        lane = lax.broadcasted_iota(jnp.int32, (1, SW_Q_HEADS), 1)
        sc = SW_HEAD_DIM ** -0.5
        G = range(SW_KV_HEADS)
        ins = [_group_inputs(bias_ref, sink_ref, g) for g in G]
        kvs = [_kv_pair(kvp_ref, kvc_ref, g) for g in G]
        q = [_side_by_side(q_ref, g) for g in G]
        do = [_side_by_side(do_ref, g) for g in G]
        lg = [jnp.where(mask, mm(kvs[g][0], q[g]) * sc + ins[g][1], -jnp.inf) for g in G]
        m = [jnp.maximum(_col_max(lg[g]), ins[g][2]) for g in G]
        p = [jnp.exp(lg[g] - m[g]) for g in G]
        ps = [jnp.exp(ins[g][2] - m[g]) for g in G]
        rden = [1.0 / (_col_sum(p[g]) + ps[g]) for g in G]
        pn = [p[g] * rden[g] for g in G]
        dpn = [mm(kvs[g][1], do[g]) for g in G]
        delta = [_col_sum(pn[g] * dpn[g]) for g in G]
        ds = [pn[g] * (dpn[g] - delta[g]) for g in G]
        dsr = [-(ps[g] * rden[g]) * delta[g] for g in G]
        dq = [mm_tn(kvs[g][0], ds[g]) * sc for g in G]
        dkk = [mm_nt(ds[g], q[g]) * sc for g in G]
        dvv = [mm_nt(pn[g], do[g]) for g in G]
        dsink = jnp.zeros((1, SW_Q_HEADS), F32)
        for g in G:
            _, _, ks, vs = kvs[g]
            for r, h in enumerate(ins[g][0]):
                cols = slice(r * W, (r + 1) * W)
                dbias_ref[h] += ds[g][:, cols]
                dq_ref[g * GROUP_ROWS + r * SW_HEAD_DIM:g * GROUP_ROWS + (r + 1) * SW_HEAD_DIM, :] = dq[g][:, cols].astype(BF16)
                dsink = dsink + jnp.where(lane == h, jnp.sum(dsr[g][:, cols], axis=1, keepdims=True), 0.0)
            dkv_ref[:, ks] = (carry_ref[:, ks] + dkk[g][W:]).astype(BF16)
            dkv_ref[:, vs] = (carry_ref[:, vs] + dvv[g][W:]).astype(BF16)
            carry_ref[:, ks] = dkk[g][:W]
            carry_ref[:, vs] = dvv[g][:W]
        dsink_ref[...] += dsink

    rev = lambda n: (nb - 1 - n, 0)
    revt = lambda n: (0, nb - 1 - n)
    return pl.pallas_call(
        body, name=name, grid=(nb,),
        in_specs=[pl.BlockSpec((Dm, W), revt),
                  pl.BlockSpec((W, 2 * KV_DIM), rev),
                  pl.BlockSpec((W, 2 * KV_DIM), lambda n: (jnp.maximum(nb - 2 - n, 0), 0)),
                  pl.BlockSpec((SW_Q_HEADS, 2 * W, W), lambda n: (0, 0, 0)),
                  pl.BlockSpec((1, SW_Q_HEADS), lambda n: (0, 0)),
                  pl.BlockSpec((Dm, W), revt)],
        out_specs=[pl.BlockSpec((Dm, W), revt), pl.BlockSpec((W, 2 * KV_DIM), rev),
                   pl.BlockSpec((SW_Q_HEADS, 2 * W, W), lambda n: (0, 0, 0)),
                   pl.BlockSpec((1, SW_Q_HEADS), lambda n: (0, 0))],
        out_shape=[jax.ShapeDtypeStruct((Dm, T), BF16), jax.ShapeDtypeStruct((T, 2 * KV_DIM), BF16),
                   jax.ShapeDtypeStruct((SW_Q_HEADS, 2 * W, W), F32), jax.ShapeDtypeStruct((1, SW_Q_HEADS), F32)],
        scratch_shapes=[pltpu.VMEM((W, 2 * KV_DIM), F32)],
        compiler_params=_params(("arbitrary",)),
    )(qt, kv, kv, bias, sinks, dot)


def _ffn_fwd(hb, w, l, after=None):
    u = _matmul_planes_nn(hb, w["ffn_in"][l], name=f"ffn{l}_up", after=after)
    u = u.reshape((2, 2) + u.shape[1:])
    act = _conv_gate_fwd(u, w["conv_w_a"][l], w["conv_w_b"][l], w["conv_b_a"][l], w["conv_b_b"][l],
                         name=f"ffn{l}_conv_gate")
    return u, act


def _ffn_bwd(dffb, dh_scaled, hb, u, act, w, l, dact):
    g_out = _matmul(act, dffb, mode="tn", name=f"ffn{l}_down_dw", tm=1408, tn=1024, tk=1024, also_bf16=True)
    du, g_conv = _conv_gate_bwd(u, w["conv_w_a"][l], w["conv_w_b"][l], w["conv_b_a"][l], w["conv_b_b"][l],
                                dact, name=f"ffn{l}_conv_gate_bwd")
    du = du.reshape((N_CHIPS,) + du.shape[2:])
    dh = _matmul_planes_nt(du, w["ffn_in"][l], dh_scaled, add_scale=ALPHA, name=f"ffn{l}_up_dx")
    g_in = _matmul(hb, du, mode="tn", planes="n", name=f"ffn{l}_up_dw", tm=1024, tn=FFN_DIM // 2, tk=1024, also_bf16=True)
    return dh, dict(ffn_out=g_out, ffn_in=g_in, conv=g_conv)


def _local_step(x, xb, tgt, w, more_weights, emit):
    bucket = jnp.asarray(_bucket_index())

    pre = _matmul_planes_nn(xb, w["hg_in"], name="hg_in", after=w.get("token"))
    og, states = _hgrn_fwd(pre, w["lb_logits"], w["gnorm"], name="hgrn_fwd")
    z1, h1, h1b = _matmul_ln(og, w["hg_out"], x, w["ln_mix_g"][0], w["ln_mix_b"][0], name="hg_out_ln")
    w = {**w, **more_weights(1, h1b)}
    u0, act0 = _ffn_fwd(h1b, w, 0, after=w.get("token"))
    z2, h2, h2b = _matmul_ln(act0, w["ffn_out"][0], h1, w["ln_ffn_g"][0], w["ln_ffn_b"][0], name="ffn0_down_ln")
    kv = _matmul(h2b, w["kv"], mode="nn", out_dtype=BF16, name="kv_proj")

    bias = _bias_from_table(w["rel_bias"], bucket, name="rel_bias_expand").reshape(SW_Q_HEADS, 2 * SW_WINDOW, SW_WINDOW)
    q1 = _matmul(w["sw_q"], h2b, mode="tt", out_dtype=BF16, name="sw_q", tm=1024, tn=1024)
    o1 = _attn_fwd(q1, kv, bias, w["sinks"], name="attn_fwd")
    z3, h3, h3b = _matmul_ln(o1, w["sw_out"], h2, w["ln_mix_g"][1], w["ln_mix_b"][1], a_t=True, name="sw_out_ln")
    w = {**w, **more_weights(2, h3b)}
    u1, act1 = _ffn_fwd(h3b, w, 1)

    g = {}
    dz, dzb, g["ln_ffn1"], loss_tile, dact1 = _matmul_ln(act1, w["ffn_out"][1], h3, w["ln_ffn_g"][1], w["ln_ffn_b"][1],
                                                         tgt=tgt, name="ffn1_down_ln_loss")

    dh3, gf1 = _ffn_bwd(dzb, dz, h3b, u1, act1, w, 1, dact1)
    dz, dzb, g["ln_mix1"], do1 = _ln_bwd_matmul(dh3, z3, w["ln_mix_g"][1], w["ln_mix_b"][1], w["sw_out"], out_t=True,
                                                name="ln_mix1_bwd_sw_out_dx")
    g_sw_out = _matmul(o1, dzb, mode="nn", name="sw_out_dw", tm=1024, tn=1024, tk=1024, also_bf16=True)
    dq1, dkv, dbias, dsinks = _attn_bwd(q1, kv, bias, w["sinks"], do1, name="attn_bwd")
    g["sinks"] = dsinks
    g["rel_bias"] = _table_grad(dbias.reshape(SW_Q_HEADS, BIAS_COLS), bucket, name="rel_bias_grad")
    dh2 = _matmul(dq1, w["sw_q"], mode="tt", add=dz, add_scale=ALPHA, name="sw_q_dx", tn=1024)
    dh2 = _matmul(dkv, w["kv"], mode="nt", add=dh2, name="kv_dx", tn=1024)
    g_sw_q = _matmul(h2b, dq1, mode="tt", name="sw_q_dw", tm=1024, tn=1024, tk=1024, also_bf16=True)
    g_kv = _matmul(h2b, dkv, mode="tn", name="kv_dw", tm=1024, tn=512, tk=1024, also_bf16=True)
    tok = emit(1, dict(sw_q=g_sw_q, sw_out=g_sw_out, kv=g_kv, ffn_in=gf1["ffn_in"], ffn_out=gf1["ffn_out"]))

    dz, dzb, g["ln_ffn0"], dact0 = _ln_bwd_matmul(dh2, z2, w["ln_ffn_g"][0], w["ln_ffn_b"][0], w["ffn_out"][0],
                                                  name="ln_ffn0_bwd_down_dx", after=tok)
    dh1, gf0 = _ffn_bwd(dzb, dz, h1b, u0, act0, w, 0, dact0)
    dz, dzb, g["ln_mix0"], dog = _ln_bwd_matmul(dh1, z1, w["ln_mix_g"][0], w["ln_mix_b"][0], w["hg_out"],
                                                name="ln_mix0_bwd_hg_out_dx")
    g_hg_out = _matmul(og, dzb, mode="tn", name="hg_out_dw", tm=1024, tn=1024, tk=1024, also_bf16=True)
    tok = emit(2, dict(hg_out=g_hg_out, ffn_in=gf0["ffn_in"], ffn_out=gf0["ffn_out"]))
    dpre, g["lb_logits"], g["gnorm"] = _hgrn_bwd(pre, w["lb_logits"], w["gnorm"], states, dog, name="hgrn_bwd", after=tok)
    tok = emit(3, dict(hg_in=_matmul(xb, dpre, mode="tn", planes="n", name="hg_in_dw", tm=1024, tn=1024, tk=1024, also_bf16=True)))
    dx = _matmul_planes_nt(dpre, w["hg_in"], dz, add_scale=ALPHA, name="hg_in_dx", after=tok)
    g["conv0"], g["conv1"] = gf0["conv"], gf1["conv"]
    return loss_tile, dx, g


def _adamw(wt, ga, gb, m, v, *, name, rows=None, prev=None):
    R, Cc = wt.shape
    r0, n = rows if rows is not None else (0, R)
    tr = _tile(n, 256, SUBLANES) if n % SUBLANES == 0 else n
    assert r0 % tr == 0
    c1 = 1.0 - ADAM_B1 ** ADAM_STEP
    c2 = 1.0 - ADAM_B2 ** ADAM_STEP
    n_in = 5

    def body(*refs):
        w_ref, ga_ref, gb_ref, m_ref, v_ref = refs[:n_in]
        g_ = ga_ref[...] + gb_ref[...]
        g_ref, d_ref, nm_ref, nv_ref = refs[-4:]
        nm = ADAM_B1 * m_ref[...] + (1.0 - ADAM_B1) * g_
        nv = ADAM_B2 * v_ref[...] + (1.0 - ADAM_B2) * (g_ * g_)
        g_ref[...] = g_
        d_ref[...] = -ADAM_LR * ((nm / c1) / (jnp.sqrt(nv / c2) + ADAM_EPS) + ADAM_WD * w_ref[...])
        nm_ref[...] = nm
        nv_ref[...] = nv

    full = pl.BlockSpec((tr, Cc), lambda i: (i + r0 // tr, 0))
    part = pl.BlockSpec((tr, Cc), lambda i: (i, 0))
    args = (wt, ga, gb, m, v)
    in_specs = [full, part, part, full, full]
    aliases = {}
    if prev is not None:
        args, in_specs = args + tuple(prev), in_specs + [ANY_SPEC] * 4
        aliases = {n_in + t: t for t in range(4)}
    return pl.pallas_call(
        body, name=name, grid=(n // tr,), in_specs=in_specs, out_specs=[full] * 4,
        out_shape=[jax.ShapeDtypeStruct((R, Cc), F32)] * 4, input_output_aliases=aliases,
        compiler_params=_params(("parallel",)),
    )(*args)


def _adamw_small(ws, gs, ms, vs, *, name):
    n = len(ws)
    c1 = 1.0 - ADAM_B1 ** ADAM_STEP
    c2 = 1.0 - ADAM_B2 ** ADAM_STEP

    def body(*refs):
        w_refs, g_refs, m_refs, v_refs = (refs[k * n:(k + 1) * n] for k in range(4))
        d_refs, nm_refs, nv_refs = (refs[(4 + k) * n:(5 + k) * n] for k in range(3))
        for i in range(n):
            g_ = g_refs[i][...]
            nm = ADAM_B1 * m_refs[i][...] + (1.0 - ADAM_B1) * g_
            nv = ADAM_B2 * v_refs[i][...] + (1.0 - ADAM_B2) * (g_ * g_)
            d_refs[i][...] = -ADAM_LR * ((nm / c1) / (jnp.sqrt(nv / c2) + ADAM_EPS) + ADAM_WD * w_refs[i][...])
            nm_refs[i][...] = nm
            nv_refs[i][...] = nv

    vm = pl.BlockSpec(memory_space=pltpu.VMEM)
    out = pl.pallas_call(
        body, name=name, in_specs=[vm] * (4 * n), out_specs=[vm] * (3 * n),
        out_shape=[jax.ShapeDtypeStruct(w.shape, F32) for w in ws] * 3,
    )(*ws, *gs, *ms, *vs)
    return out[:n], out[n:2 * n], out[2 * n:]


HBM_SPEC = pl.BlockSpec(memory_space=pltpu.HBM)
SEM_SPEC = pl.BlockSpec(memory_space=pltpu.SEMAPHORE)
VMEM_SPEC = pl.BlockSpec(memory_space=pltpu.VMEM)
DATAFLOW = pltpu.SideEffectType.DATAFLOW_SIDE_EFFECTING


def _in_hbm(a):
    return pltpu.with_memory_space_constraint(a, pltpu.HBM)


def _place():
    return lax.axis_index("x"), lax.axis_index("y"), lax.axis_index("c")


def _other_chips(x, y):
    return [(1 - x, y), (x, 1 - y), (1 - x, 1 - y)]


def _sum8(vs, *, name):
    n = len(vs)

    def body(*refs):
        v_refs, all_refs, o_refs = refs[:n], refs[n:2 * n], refs[2 * n:3 * n]
        send_sems, recv_sems, local_sems = refs[3 * n:]
        x, y, c = _place()
        me, sibling = (x, y, c), (x, y, 1 - c)
        chips = _other_chips(x, y)

        def slot(i, px, py, pc):
            return all_refs[i].at[4 * px + 2 * py + pc]

        def copy(i, k, block, to, src=None):
            return pltpu.make_async_remote_copy(
                src_ref=slot(i, *block) if src is None else src, dst_ref=slot(i, *block),
                send_sem=send_sems.at[7 * i + k], recv_sem=recv_sems.at[7 * i + k], device_id=to, device_id_type=MESH)

        mine = [pltpu.make_async_copy(v_refs[i], slot(i, *me), local_sems.at[i]) for i in range(n)]
        for cp in mine:
            cp.start()
        first = [copy(i, 0, me, sibling, src=v_refs[i]) for i in range(n)]
        first += [copy(i, 1 + j, me, (*chip, c), src=v_refs[i]) for i in range(n) for j, chip in enumerate(chips)]
        for cp in first:
            cp.start()
        passed = []
        for i in range(n):
            for j, chip in enumerate(chips):
                copy(i, 1 + j, (*chip, c), me).wait_recv()
                passed.append(copy(i, 4 + j, (*chip, c), sibling))
                passed[-1].start()
        for i in range(n):
            copy(i, 0, sibling, me).wait_recv()
            for j, chip in enumerate(chips):
                copy(i, 4 + j, (*chip, 1 - c), me).wait_recv()
        for cp in first + passed:
            cp.wait_send()
        for cp in mine:
            cp.wait()
        for i in range(n):
            acc = all_refs[i][0]
            for d in range(1, N_DEV):
                acc = acc + all_refs[i][d]
            o_refs[i][...] = acc

    return pl.pallas_call(
        body, name=name, in_specs=[VMEM_SPEC] * n, out_specs=[VMEM_SPEC] * (2 * n),
        out_shape=[jax.ShapeDtypeStruct((N_DEV,) + v.shape, F32) for v in vs] + [jax.ShapeDtypeStruct(v.shape, F32) for v in vs],
        scratch_shapes=[pltpu.SemaphoreType.DMA((7 * n,)), pltpu.SemaphoreType.DMA((7 * n,)), pltpu.SemaphoreType.DMA((n,))],
        compiler_params=pltpu.CompilerParams(vmem_limit_bytes=VMEM_LIMIT),
    )(*vs)[n:]


def _swap_copies(src, land, send, recv):
    x, y, c = _place()
    return [pltpu.make_async_remote_copy(src_ref=src[i], dst_ref=land[i], send_sem=send.at[i], recv_sem=recv.at[i],
                                         device_id=(x, y, 1 - c), device_id_type=MESH) for i in range(len(src))]


def _swap_start(vs, *, name):
    n = len(vs)

    def body(*refs):
        src, land, send, recv, token = refs[:n], refs[n:2 * n], refs[2 * n], refs[2 * n + 1], refs[-1]
        for cp in _swap_copies(src, land, send, recv):
            cp.start()
        token[...] = jnp.zeros_like(token)

    lands = [lax.empty(v.shape, v.dtype) for v in vs]
    sems = pltpu.SemaphoreType.DMA((n,))
    out = pl.pallas_call(
        body, name=name, in_specs=[HBM_SPEC] * (2 * n),
        out_specs=[SEM_SPEC, SEM_SPEC] + [HBM_SPEC] * (2 * n) + [VMEM_SPEC],
        out_shape=[sems, sems] + [pltpu.HBM(a.shape, a.dtype) for a in list(vs) + lands]
        + [jax.ShapeDtypeStruct((SUBLANES, LANES), F32)],
        input_output_aliases={i: 2 + i for i in range(2 * n)},
        compiler_params=pltpu.CompilerParams(has_side_effects=DATAFLOW),
    )(*[_in_hbm(a) for a in list(vs) + lands])
    return (out[0], out[1], out[2:2 + n], out[2 + n:2 + 2 * n]), out[-1]


def _swap_wait(handle, after, *, name):
    send_sems, recv_sems, srcs, lands = handle
    n = len(srcs)

    def body(*refs):
        src, land, send, recv = refs[:n], refs[n:2 * n], refs[2 * n], refs[2 * n + 1]
        for cp in _swap_copies(src, land, send, recv):
            cp.wait_send()
            cp.wait_recv()

    both = list(srcs) + list(lands)
    out = pl.pallas_call(
        body, name=name, in_specs=[HBM_SPEC] * (2 * n) + [SEM_SPEC, SEM_SPEC, ANY_SPEC], out_specs=[HBM_SPEC] * (2 * n),
        out_shape=[pltpu.HBM(a.shape, a.dtype) for a in both],
        input_output_aliases={i: i for i in range(2 * n)},
        compiler_params=pltpu.CompilerParams(has_side_effects=DATAFLOW),
    )(*both, send_sems, recv_sems, after)
    return out[:n], out[n:]


def _gather_copies(srcs, lands, send, recv, sibling=False):
    x, y, c = _place()
    out = []
    for i, (src, land) in enumerate(zip(srcs, lands)):
        half = land.shape[1] // 2
        rows = pl.ds(c * half, half)
        for k, (px, py) in enumerate(_other_chips(x, y)):
            if sibling:
                src_ref, dst_ref, to = src.at[2 * px + py, rows], land.at[2 * px + py, rows], (x, y, 1 - c)
            else:
                src_ref, dst_ref, to = src.at[rows], land.at[2 * x + y, rows], (px, py, c)
            out.append(pltpu.make_async_remote_copy(src_ref=src_ref, dst_ref=dst_ref, send_sem=send.at[3 * i + k],
                                                    recv_sem=recv.at[3 * i + k], device_id=to, device_id_type=MESH))
    return out


def _gather_arrivals(lands, send, recv, sibling=False):
    x, y, c = _place()
    out = []
    for i, land in enumerate(lands):
        half = land.shape[1] // 2
        rows = pl.ds(((1 - c) if sibling else c) * half, half)
        for k, (px, py) in enumerate(_other_chips(x, y)):
            part = land.at[2 * px + py, rows]
            out.append(pltpu.make_async_remote_copy(src_ref=part, dst_ref=part, send_sem=send.at[3 * i + k],
                                                    recv_sem=recv.at[3 * i + k],
                                                    device_id=(x, y, 1 - c) if sibling else (px, py, c), device_id_type=MESH))
    return out


def _own_copies(srcs, lands, sems):
    x, y, _ = _place()
    return [pltpu.make_async_copy(src, land.at[2 * x + y], sems.at[i]) for i, (src, land) in enumerate(zip(srcs, lands))]


def _gather_start(shards, after, *, name, own_too):
    n = len(shards)

    def body(*refs):
        srcs, lands, (send, recv, own), token = refs[:n], refs[n:2 * n], refs[2 * n:2 * n + 3], refs[-1]
        for cp in _gather_copies(srcs, lands, send, recv) + (_own_copies(srcs, lands, own) if own_too else []):
            cp.start()
        token[...] = jnp.zeros_like(token)

    lands = [lax.empty((N_CHIPS,) + s.shape, s.dtype) for s in shards]
    sems = pltpu.SemaphoreType.DMA((3 * n,))
    body, xs, xa = _after(body, 2 * n, after)
    out = pl.pallas_call(
        body, name=name, in_specs=[HBM_SPEC] * (2 * n) + xs,
        out_specs=[SEM_SPEC] * 3 + [HBM_SPEC] * (2 * n) + [VMEM_SPEC],
        out_shape=[sems, sems, pltpu.SemaphoreType.DMA((n,))] + [pltpu.HBM(a.shape, a.dtype) for a in list(shards) + lands]
        + [jax.ShapeDtypeStruct((SUBLANES, LANES), F32)],
        input_output_aliases={i: 3 + i for i in range(2 * n)},
        compiler_params=pltpu.CompilerParams(has_side_effects=DATAFLOW),
    )(*[_in_hbm(a) for a in list(shards) + lands], *xa)
    return (out[:3], out[3:3 + n], out[3 + n:3 + 2 * n], own_too), out[-1]


def _gather_wait(handle, after, *, name):
    sems, srcs, lands, own_too = handle
    n = len(srcs)

    def body(*refs):
        srcs_, lands_, (send, recv, own) = refs[:n], refs[n:2 * n], refs[2 * n:2 * n + 3]
        for cp in _gather_copies(srcs_, lands_, send, recv):
            cp.wait_send()
        for cp in _gather_arrivals(lands_, send, recv):
            cp.wait_recv()
        for cp in _own_copies(srcs_, lands_, own) if own_too else []:
            cp.wait()

    both = list(srcs) + list(lands)
    out = pl.pallas_call(
        body, name=name, in_specs=[HBM_SPEC] * (2 * n) + [SEM_SPEC] * 3 + [ANY_SPEC], out_specs=[HBM_SPEC] * (2 * n),
        out_shape=[pltpu.HBM(a.shape, a.dtype) for a in both],
        input_output_aliases={i: i for i in range(2 * n)},
        compiler_params=pltpu.CompilerParams(has_side_effects=DATAFLOW),
    )(*both, *sems, after)
    return out[n:]


def _fill_sibling(lands, *, name):
    n = len(lands)

    def body(*refs):
        ins, outs, send_sems, recv_sems = refs[:n], refs[n:2 * n], refs[2 * n], refs[2 * n + 1]
        cps = _gather_copies(ins, outs, send_sems, recv_sems, sibling=True)
        for cp in cps:
            cp.start()
        for cp in _gather_arrivals(outs, send_sems, recv_sems, sibling=True):
            cp.wait_recv()
        for cp in cps:
            cp.wait_send()

    return pl.pallas_call(
        body, name=name, in_specs=[HBM_SPEC] * n, out_specs=[HBM_SPEC] * n,
        out_shape=[jax.ShapeDtypeStruct(a.shape, a.dtype) for a in lands],
        scratch_shapes=[pltpu.SemaphoreType.DMA((3 * n,)), pltpu.SemaphoreType.DMA((3 * n,))],
        input_output_aliases={i: i for i in range(n)},
    )(*lands)


def _scatter_copies(src, land, send, recv):
    x, y, c = _place()
    return [pltpu.make_async_remote_copy(src_ref=src[i].at[2 * px + py], dst_ref=land[i].at[k], send_sem=send.at[3 * i + k],
                                         recv_sem=recv.at[3 * i + k], device_id=(px, py, c), device_id_type=MESH)
            for i in range(len(src)) for k, (px, py) in enumerate(_other_chips(x, y))]


def _scatter_start(pieces, *, name):
    n = len(pieces)

    def body(*refs):
        src, land, send, recv, token = refs[:n], refs[n:2 * n], refs[2 * n], refs[2 * n + 1], refs[-1]
        for cp in _scatter_copies(src, land, send, recv):
            cp.start()
        token[...] = jnp.zeros_like(token)

    lands = [lax.empty((3,) + p.shape[1:], p.dtype) for p in pieces]
    sems = pltpu.SemaphoreType.DMA((3 * n,))
    out = pl.pallas_call(
        body, name=name, in_specs=[HBM_SPEC] * (2 * n),
        out_specs=[SEM_SPEC, SEM_SPEC] + [HBM_SPEC] * (2 * n) + [VMEM_SPEC],
        out_shape=[sems, sems] + [pltpu.HBM(a.shape, a.dtype) for a in pieces + lands]
        + [jax.ShapeDtypeStruct((SUBLANES, LANES), F32)],
        input_output_aliases={i: 2 + i for i in range(2 * n)},
        compiler_params=pltpu.CompilerParams(has_side_effects=DATAFLOW),
    )(*[_in_hbm(a) for a in pieces + lands])
    return (out[0], out[1], out[2:2 + n], out[2 + n:2 + 2 * n]), out[-1]


def _scatter_wait(handle, after, *, name):
    send_sems, recv_sems, srcs, lands = handle
    n = len(srcs)

    def body(*refs):
        src, land, send, recv = refs[:n], refs[n:2 * n], refs[2 * n], refs[2 * n + 1]
        for cp in _scatter_copies(src, land, send, recv):
            cp.wait_send()
            cp.wait_recv()

    both = list(srcs) + list(lands)
    out = pl.pallas_call(
        body, name=name, in_specs=[HBM_SPEC] * (2 * n) + [SEM_SPEC, SEM_SPEC, ANY_SPEC], out_specs=[HBM_SPEC] * (2 * n),
        out_shape=[pltpu.HBM(a.shape, a.dtype) for a in both],
        input_output_aliases={i: i for i in range(2 * n)},
        compiler_params=pltpu.CompilerParams(has_side_effects=DATAFLOW),
    )(*both, send_sems, recv_sems, after)
    return out[n:]


def _to_bf16(x, *, name, after=None):
    T, Dm = x.shape
    tr = _tile(T, 512, 2 * SUBLANES)

    def body(x_ref, o_ref):
        o_ref[...] = x_ref[...].astype(BF16)

    blk = pl.BlockSpec((tr, Dm), lambda i: (i, 0))
    body, xs, xa = _after(body, 1, after)
    return pl.pallas_call(
        body, name=name, grid=(T // tr,), in_specs=[blk] + xs, out_specs=blk, out_shape=jax.ShapeDtypeStruct((T, Dm), BF16),
        compiler_params=_params(("parallel",)),
    )(x, *xa)


def _chip_sum(pieces, got, chip, *, name):
    _, R, Cc = pieces.shape
    tr = _tile(R, 256, SUBLANES)

    def body(chip_ref, a_ref, g_ref, o_ref):
        o_ref[...] = ((a_ref[...] + g_ref[0].astype(F32)) + g_ref[1].astype(F32)) + g_ref[2].astype(F32)

    return pl.pallas_call(
        body, name=name,
        grid_spec=pltpu.PrefetchScalarGridSpec(
            num_scalar_prefetch=1, grid=(R // tr,),
            in_specs=[pl.BlockSpec((None, tr, Cc), lambda i, ch: (ch[0], i, 0)),
                      pl.BlockSpec((3, tr, Cc), lambda i, ch: (0, i, 0))],
            out_specs=pl.BlockSpec((tr, Cc), lambda i, ch: (i, 0))),
        out_shape=jax.ShapeDtypeStruct((R, Cc), F32),
        compiler_params=_params(("parallel",)),
    )(chip, pieces, got)


PACK_COLS = 1024
SMALL_ROWS = 32


def kernel(x, hgrn_w_in, hgrn_lb_logits, hgrn_gnorm_w, hgrn_w_out, swa_w_q, swa_sinks, swa_w_out, shared_w_kv, rel_bias, ffn_w_in, ffn_conv_w, ffn_conv_b, ffn_w_out, ln_mix_g, ln_mix_b, ln_ffn_g, ln_ffn_b, loss_target, m_hgrn_w_in, m_hgrn_lb_logits, m_hgrn_gnorm_w, m_hgrn_w_out, m_swa_w_q, m_swa_sinks, m_swa_w_out, m_shared_w_kv, m_rel_bias, m_ffn_w_in, m_ffn_conv_w, m_ffn_conv_b, m_ffn_w_out, m_ln_mix_g, m_ln_mix_b, m_ln_ffn_g, m_ln_ffn_b, v_hgrn_w_in, v_hgrn_lb_logits, v_hgrn_gnorm_w, v_hgrn_w_out, v_swa_w_q, v_swa_sinks, v_swa_w_out, v_shared_w_kv, v_rel_bias, v_ffn_w_in, v_ffn_conv_w, v_ffn_conv_b, v_ffn_w_out, v_ln_mix_g, v_ln_mix_b, v_ln_ffn_g, v_ln_ffn_b):
    xi, yi, ci = _place()
    chip = 2 * xi + yi
    Dm = D_MODEL
    FC = 2 * FFN_DIM // N_CHIPS
    Fo = FFN_DIM // N_CHIPS
    Dq = Dm // N_CHIPS
    bf = lambda a: a.astype(BF16)

    small = jnp.concatenate([hgrn_lb_logits.reshape(-1), ffn_conv_w.reshape(-1)])
    n_small = small.shape[0]
    bits = jnp.concatenate(_split3(small))
    bits = jnp.pad(bits, (0, SMALL_ROWS * PACK_COLS - 3 * n_small)).reshape(SMALL_ROWS, PACK_COLS)
    groups = [[bf(hgrn_w_in[0]), bf(hgrn_w_out[0]), bits],
              [bf(swa_w_q[0]), bf(swa_w_out[0]), bf(shared_w_kv), bf(ffn_w_in[0]), bf(ffn_w_out[0])],
              [bf(ffn_w_in[1]), bf(ffn_w_out[1])]]

    def gathered(k, landed):
        lands = _fill_sibling(landed, name=f"gather_w{k}_fill")
        if k > 0:
            return lands
        return [lax.dynamic_update_slice(land, shard[None], (chip,) + (0,) * shard.ndim)
                for land, shard in zip(lands, groups[0])]

    handle0, token0 = _gather_start(groups[0], None, name="gather_w0_start", own_too=False)
    xb = _to_bf16(x[0], name="x_to_bf16", after=token0)
    corner = lambda a: a[:2 * SUBLANES, :LANES]
    casts_done = corner(xb) + sum(corner(a) for a in groups[1] + groups[2])
    w_in, w_hg_out, small_all = gathered(0, _gather_wait(handle0, casts_done, name="gather_w0_wait"))
    handle1, token1 = _gather_start(groups[1], w_in, name="gather_w1_start", own_too=True)
    parts = small_all.reshape(N_CHIPS, -1)[:, :3 * n_small].reshape(N_CHIPS, 3, n_small).astype(F32)
    vals = (parts[:, 0] + parts[:, 1]) + parts[:, 2]
    lb_full = vals[:, :2 * Dq].reshape(N_CHIPS, 2, Dq).transpose(1, 0, 2).reshape(2, Dm)
    cw_full = vals[:, 2 * Dq:].reshape(N_CHIPS, DEPTH, 3, FC).transpose(1, 2, 0, 3).reshape(DEPTH, 3, 2 * FFN_DIM)

    got = {"handle": handle1}

    def more_weights(k, after):
        ws = gathered(k, _gather_wait(got.pop("handle"), after, name=f"gather_w{k}_wait"))
        if k == 1:
            got["handle"], token2 = _gather_start(groups[2], ws[0], name="gather_w2_start", own_too=True)
            w_q, w_o, w_kv, w_fi, w_fo = ws
            got.update(ffn_in={0: w_fi}, ffn_out={0: w_fo.reshape(FFN_DIM, Dm)})
            return {"sw_q": w_q.reshape(Dm, Dm), "sw_out": w_o.reshape(Dm, Dm), "kv": w_kv.reshape(Dm, 2 * KV_DIM),
                    "token": token2, "ffn_in": got["ffn_in"], "ffn_out": got["ffn_out"]}
        w_fi, w_fo = ws
        return {"ffn_in": {**got["ffn_in"], 1: w_fi}, "ffn_out": {**got["ffn_out"], 1: w_fo.reshape(FFN_DIM, Dm)}}

    w = {
        "hg_in": w_in, "hg_out": w_hg_out.reshape(Dm, Dm), "token": token1,
        "lb_logits": lb_full, "gnorm": hgrn_gnorm_w, "sinks": swa_sinks, "rel_bias": rel_bias,
        "conv_w_a": [cw_full[l, :, :FFN_DIM] for l in range(DEPTH)],
        "conv_w_b": [cw_full[l, :, FFN_DIM:] for l in range(DEPTH)],
        "conv_b_a": [ffn_conv_b[l:l + 1, :FFN_DIM] for l in range(DEPTH)],
        "conv_b_b": [ffn_conv_b[l:l + 1, FFN_DIM:] for l in range(DEPTH)],
        "ln_mix_g": [ln_mix_g[l:l + 1] for l in range(DEPTH)], "ln_mix_b": [ln_mix_b[l:l + 1] for l in range(DEPTH)],
        "ln_ffn_g": [ln_ffn_g[l:l + 1] for l in range(DEPTH)], "ln_ffn_b": [ln_ffn_b[l:l + 1] for l in range(DEPTH)],
    }

    sent = {}

    def emit(k, gd):
        rows4 = lambda a: a.reshape(N_CHIPS, -1, a.shape[-1])
        order = {1: ["sw_q", "sw_out", "kv", "ffn_in", "ffn_out"], 2: ["ffn_in", "ffn_out", "hg_out"], 3: ["hg_in"]}[k]
        as_pieces = lambda a, nme: a if nme in ("ffn_in", "hg_in") else rows4(a)
        handle, token = _scatter_start([as_pieces(gd[nme][1], nme) for nme in order], name=f"scatter_g{k}_start")
        sent[k] = (handle, [as_pieces(gd[nme][0], nme) for nme in order])
        return token

    loss_tile, grad_x, g = _local_step(x[0], xb, loss_target[0], w, more_weights, emit)

    wts = dict(hgrn_w_in=hgrn_w_in, hgrn_lb_logits=hgrn_lb_logits, hgrn_gnorm_w=hgrn_gnorm_w, hgrn_w_out=hgrn_w_out,
               swa_w_q=swa_w_q, swa_sinks=swa_sinks, swa_w_out=swa_w_out, shared_w_kv=shared_w_kv, rel_bias=rel_bias,
               ffn_w_in=ffn_w_in, ffn_conv_w=ffn_conv_w, ffn_conv_b=ffn_conv_b, ffn_w_out=ffn_w_out,
               ln_mix_g=ln_mix_g, ln_mix_b=ln_mix_b, ln_ffn_g=ln_ffn_g, ln_ffn_b=ln_ffn_b)
    ms = dict(hgrn_w_in=m_hgrn_w_in, hgrn_lb_logits=m_hgrn_lb_logits, hgrn_gnorm_w=m_hgrn_gnorm_w, hgrn_w_out=m_hgrn_w_out,
              swa_w_q=m_swa_w_q, swa_sinks=m_swa_sinks, swa_w_out=m_swa_w_out, shared_w_kv=m_shared_w_kv, rel_bias=m_rel_bias,
              ffn_w_in=m_ffn_w_in, ffn_conv_w=m_ffn_conv_w, ffn_conv_b=m_ffn_conv_b, ffn_w_out=m_ffn_w_out,
              ln_mix_g=m_ln_mix_g, ln_mix_b=m_ln_mix_b, ln_ffn_g=m_ln_ffn_g, ln_ffn_b=m_ln_ffn_b)
    vs = dict(hgrn_w_in=v_hgrn_w_in, hgrn_lb_logits=v_hgrn_lb_logits, hgrn_gnorm_w=v_hgrn_gnorm_w, hgrn_w_out=v_hgrn_w_out,
              swa_w_q=v_swa_w_q, swa_sinks=v_swa_sinks, swa_w_out=v_swa_w_out, shared_w_kv=v_shared_w_kv, rel_bias=v_rel_bias,
              ffn_w_in=v_ffn_w_in, ffn_conv_w=v_ffn_conv_w, ffn_conv_b=v_ffn_conv_b, ffn_w_out=v_ffn_w_out,
              ln_mix_g=v_ln_mix_g, ln_mix_b=v_ln_mix_b, ln_ffn_g=v_ln_ffn_g, ln_ffn_b=v_ln_ffn_b)
    names = list(wts)
    grads, delta, new_m, new_v = {}, {}, {}, {}

    def update(n, ga, gb, layer=None, prev=None):
        r2 = lambda a: a.reshape(-1, a.shape[-1])
        rows = None if layer is None else (layer * ga.shape[0], ga.shape[0])
        return _adamw(r2(wts[n]), ga, gb, r2(ms[n]), r2(vs[n]), rows=rows, prev=prev,
                      name=f"adamw_{n}" + ("" if layer is None else f"_{layer}"))

    def keep(n, res):
        grads[n], delta[n], new_m[n], new_v[n] = [a.reshape(wts[n].shape) for a in res]

    chip1 = jnp.reshape(chip, (1,)).astype(jnp.int32)
    after, swaps = grad_x, {}
    for k in (1, 2, 3):
        handle, pieces = sent[k]
        lands = _scatter_wait(handle, after, name=f"scatter_g{k}_wait")
        parts = [_chip_sum(p, l, chip1, name=f"scatter_g{k}_sum{i}") for i, (p, l) in enumerate(zip(pieces, lands))]
        swaps[k], after = _swap_start(parts, name=f"scatter_g{k}_swap_start")
    for k in (1, 2, 3):
        parts, sibs = _swap_wait(swaps[k], after, name=f"scatter_g{k}_swap_wait")
        if k == 1:
            for n, ga, gb in zip(["swa_w_q", "swa_w_out", "shared_w_kv"], parts[:3], sibs[:3]):
                keep(n, update(n, ga, gb))
            ffn_in_1 = update("ffn_w_in", parts[3], sibs[3], layer=1)
            ffn_out_1 = update("ffn_w_out", parts[4], sibs[4], layer=1)
            after = ffn_out_1[3]
        elif k == 2:
            keep("ffn_w_in", update("ffn_w_in", parts[0], sibs[0], layer=0, prev=ffn_in_1))
            keep("ffn_w_out", update("ffn_w_out", parts[1], sibs[1], layer=0, prev=ffn_out_1))
            keep("hgrn_w_out", update("hgrn_w_out", parts[2], sibs[2]))
            after = new_v["hgrn_w_out"]
        else:
            keep("hgrn_w_in", update("hgrn_w_in", parts[0], sibs[0]))

    small_keys = ["lb_logits", "gnorm", "sinks", "rel_bias", "conv0", "conv1", "ln_mix0", "ln_mix1", "ln_ffn0", "ln_ffn1"]
    flat2 = lambda a: a.reshape(-1, a.shape[-1])
    sums = _sum8([loss_tile] + [flat2(g[k]) for k in small_keys], name="sum_small")
    loss = sums[0][0, 0]
    sg = {k: v.reshape(g[k].shape) for k, v in zip(small_keys, sums[1:])}
    conv = [sg["conv0"], sg["conv1"]]
    g_cw = jnp.stack([jnp.concatenate([conv[l][0, :3], conv[l][1, :3]], axis=1) for l in range(DEPTH)])
    g_cb = jnp.stack([jnp.concatenate([conv[l][0, 3], conv[l][1, 3]], axis=0) for l in range(DEPTH)])
    ln = lambda nme, r: jnp.stack([sg[nme + "0"][r], sg[nme + "1"][r]])
    small_g = dict(hgrn_lb_logits=lax.dynamic_slice_in_dim(sg["lb_logits"], chip * Dq, Dq, axis=1),
                   hgrn_gnorm_w=sg["gnorm"], swa_sinks=sg["sinks"], rel_bias=sg["rel_bias"],
                   ffn_conv_w=lax.dynamic_slice_in_dim(g_cw, chip * FC, FC, axis=2), ffn_conv_b=g_cb,
                   ln_mix_g=ln("ln_mix", 0), ln_mix_b=ln("ln_mix", 1), ln_ffn_g=ln("ln_ffn", 0), ln_ffn_b=ln("ln_ffn", 1))
    small_names = list(small_g)
    d_, m_, v_ = _adamw_small([flat2(wts[n]) for n in small_names], [flat2(small_g[n]) for n in small_names],
                              [flat2(ms[n]) for n in small_names], [flat2(vs[n]) for n in small_names], name="adamw_small")
    for n, a, b_, c_ in zip(small_names, d_, m_, v_):
        shp = wts[n].shape
        grads[n], delta[n], new_m[n], new_v[n] = small_g[n], a.reshape(shp), b_.reshape(shp), c_.reshape(shp)

    return (loss, grad_x[None], *[grads[n] for n in names], *[delta[n] for n in names],
            *[new_m[n] for n in names], *[new_v[n] for n in names])
```

```python
import math

import numpy as np
import jax
import jax.numpy as jnp
from jax import lax
from jax.experimental import pallas as pl
from jax.experimental.pallas import tpu as pltpu

F32 = jnp.float32
BF16 = jnp.bfloat16
MESH = pl.DeviceIdType.MESH

D_MODEL = 1024
DEPTH = 2
HG_HEADS = 8
HG_DIM = 128
SW_Q_HEADS = 16
SW_KV_HEADS = 4
SW_HEAD_DIM = 64
SW_GROUP = 4
SW_WINDOW = 128
REL_BUCKETS = 32
REL_MAX_DIST = 128
FFN_DIM = 2816
ALPHA = (2.0 * DEPTH) ** 0.25
LN_EPS = 1e-5
RMS_EPS = 1e-6
ADAM_LR = 0.001
ADAM_B1 = 0.9
ADAM_B2 = 0.999
ADAM_EPS = 1e-08
ADAM_WD = 0.01
ADAM_STEP = 10

VMEM_BYTES_V7X = 64 * 1024 * 1024
VMEM_LIMIT = VMEM_BYTES_V7X - 8 * 1024 * 1024
LANES = 128
SUBLANES = 8

HG_C = 64
HG_RB = 256
CONV_R = 128
N_CHIPS = 4
N_DEV = 8

ANY_SPEC = pl.BlockSpec(memory_space=pl.ANY)


def _after(body, n_in, after):
    if after is None:
        return body, [], ()

    def wrapped(*refs):
        return body(*refs[:n_in], *refs[n_in + 1:])

    return wrapped, [ANY_SPEC], (after,)


def _params(sem=None):
    return pltpu.CompilerParams(dimension_semantics=sem, vmem_limit_bytes=VMEM_LIMIT)


def _tile(n, pref, unit=LANES):
    if n <= pref:
        return n
    best = None
    for t in range(unit, pref + 1, unit):
        if n % t == 0:
            best = t
    assert best is not None, (n, pref, unit)
    return best


def _dot(a, b, ca, cb):
    nb = a.ndim - 2
    batch = tuple(range(nb))
    return lax.dot_general(a.astype(BF16), b.astype(BF16), (((nb + ca,), (nb + cb,)), (batch, batch)),
                           preferred_element_type=F32)


@jax.custom_vjp
def mm(a, b):
    return _dot(a, b, 1, 0)


@jax.custom_vjp
def mm_nt(a, b):
    return _dot(a, b, 1, 1)


@jax.custom_vjp
def mm_tn(a, b):
    return _dot(a, b, 0, 0)


mm.defvjp(lambda a, b: (mm(a, b), (a, b)), lambda r, ct: (mm_nt(ct, r[1]), mm_tn(r[0], ct)))
mm_nt.defvjp(lambda a, b: (mm_nt(a, b), (a, b)), lambda r, ct: (mm(ct, r[1]), mm_tn(ct, r[0])))
mm_tn.defvjp(lambda a, b: (mm_tn(a, b), (a, b)), lambda r, ct: (mm_nt(r[1], ct), mm(r[0], ct)))


def _split2(x):
    hi = x.astype(BF16)
    return hi, (x - hi.astype(F32)).astype(BF16)


@jax.custom_vjp
def _scores(qt, kt):
    return _dot(qt, kt, 1, 1)


def _scores_bwd(r, ct):
    (qh, ql), (kh, kl) = _split2(r[0]), _split2(r[1])
    return _dot(ct, kh, 1, 0) + _dot(ct, kl, 1, 0), _dot(ct, qh, 0, 0) + _dot(ct, ql, 0, 0)


_scores.defvjp(lambda a, b: (_scores(a, b), (a, b)), _scores_bwd)


def _split3(x):
    hi = x.astype(BF16)
    r1 = x - hi.astype(F32)
    mid = r1.astype(BF16)
    lo = (r1 - mid.astype(F32)).astype(BF16)
    return hi, mid, lo


def _cumsum_impl(x):
    ax = x.ndim - 2
    n = x.shape[ax]
    row = lax.broadcasted_iota(jnp.int32, x.shape, ax)
    d = 1
    while d < n:
        x = x + jnp.where(row >= d, pltpu.roll(x, d, ax), 0.0)
        d *= 2
    return x


def _cumsum_rev_impl(x):
    ax = x.ndim - 2
    n = x.shape[ax]
    row = lax.broadcasted_iota(jnp.int32, x.shape, ax)
    d = 1
    while d < n:
        x = x + jnp.where(row < n - d, pltpu.roll(x, n - d, ax), 0.0)
        d *= 2
    return x


@jax.custom_vjp
def _cumsum(x):
    return _cumsum_impl(x)


_cumsum.defvjp(lambda x: (_cumsum_impl(x), None), lambda _, ct: (_cumsum_rev_impl(ct),))


def _matmul(a, b, *, mode, name, out_dtype=F32, add=None, add_scale=1.0, tm=512, tn=1408, tk=1408, after=None,
            split_n=False, planes=None, also_bf16=False):
    P = b.shape[0] if planes else 1
    a2, b2 = a.shape[-2:], b.shape[-2:]
    (M, K) = a2 if mode[0] == "n" else a2[::-1]
    (K2, N) = b2 if mode[1] == "n" else b2[::-1]
    assert K == K2, (a.shape, b.shape, mode)
    assert a.ndim == (3 if planes == "k" else 2) and b.ndim == (3 if planes else 2)
    tm, tn, tk = _tile(M, tm), _tile(N, tn), _tile(K, tk)
    nj, nkp = N // tn, K // tk
    nk = nkp * (P if planes == "k" else 1)
    ca, cb = (1 if mode[0] == "n" else 0), (0 if mode[1] == "n" else 1)
    a_blk, a_idx = ((tk, tm), lambda i, k: (k, i)) if mode[0] == "t" else ((tm, tk), lambda i, k: (i, k))
    b_blk, b_idx = ((tn, tk), lambda k, j: (j, k)) if mode[1] == "t" else ((tk, tn), lambda k, j: (k, j))
    if planes == "k":
        a_spec = pl.BlockSpec((None,) + a_blk, lambda i, j, k: (k // nkp,) + a_idx(i, k % nkp))
        b_spec = pl.BlockSpec((None,) + b_blk, lambda i, j, k: (k // nkp,) + b_idx(k % nkp, j))
    else:
        a_spec = pl.BlockSpec(a_blk, lambda i, j, k: a_idx(i, k))
        b_spec = (pl.BlockSpec((None,) + b_blk, lambda i, j, k: (j // nj,) + b_idx(k, j % nj)) if planes == "n"
                  else pl.BlockSpec(b_blk, lambda i, j, k: b_idx(k, j)))
    if split_n:
        o_spec, out_shape = pl.BlockSpec((None, tm, tn), lambda i, j, k: (j, i, 0)), (P * nj if planes == "n" else nj, M, tn)
    elif planes == "n":
        o_spec, out_shape = pl.BlockSpec((None, tm, tn), lambda i, j, k: (j // nj, i, j % nj)), (P, M, N)
    else:
        o_spec, out_shape = pl.BlockSpec((tm, tn), lambda i, j, k: (i, j)), (M, N)
    has_add = add is not None
    assert not (has_add and (split_n or planes == "n"))

    def finish(r, add_ref, o_refs):
        if has_add:
            r = r + add_scale * add_ref[...]
        o_refs[0][...] = r.astype(out_dtype)
        if also_bf16:
            o_refs[1][...] = r.astype(BF16)

    def body(*refs):
        a_ref, b_ref = refs[:2]
        add_ref = refs[2] if has_add else None
        first = 3 if has_add else 2
        o_ref = refs[first:first + (2 if also_bf16 else 1)]
        if nk == 1:
            finish(_dot(a_ref[...], b_ref[...], ca, cb), add_ref, o_ref)
            return
        acc_ref = refs[-1]
        k = pl.program_id(2)

        @pl.when(k == 0)
        def _():
            acc_ref[...] = jnp.zeros_like(acc_ref)

        acc_ref[...] += _dot(a_ref[...], b_ref[...], ca, cb)

        @pl.when(k == nk - 1)
        def _():
            finish(acc_ref[...], add_ref, o_ref)

    in_specs = [a_spec, b_spec] + ([o_spec] if has_add else [])
    args = (a, b) + ((add,) if has_add else ())
    body, xs, xa = _after(body, len(args), after)
    in_specs, args = in_specs + xs, args + xa
    out_shapes = [jax.ShapeDtypeStruct(out_shape, out_dtype)] + ([jax.ShapeDtypeStruct(out_shape, BF16)] if also_bf16 else [])
    out = pl.pallas_call(
        body, name=name, grid=(M // tm, nj * (P if planes == "n" else 1), nk), in_specs=in_specs,
        out_specs=[o_spec] * len(out_shapes), out_shape=out_shapes,
        scratch_shapes=[pltpu.VMEM((tm, tn), F32)] if nk > 1 else [],
        compiler_params=_params(("parallel", "parallel", "arbitrary")),
    )(*args)
    return tuple(out) if also_bf16 else out[0]


def _matmul_planes_nn(a, b, *, name, tm=512, after=None):
    (M, K), (P, K2, N) = a.shape, b.shape
    assert K == K2
    tm = _tile(M, tm, 2 * SUBLANES)

    def body(a_ref, b_ref, o_ref):
        for p in range(P):
            o_ref[p] = _dot(a_ref[...], b_ref[p], 1, 0).astype(BF16)

    body, xs, xa = _after(body, 2, after)
    return pl.pallas_call(
        body, name=name, grid=(M // tm,),
        in_specs=[pl.BlockSpec((tm, K), lambda i: (i, 0)), pl.BlockSpec((P, K, N), lambda i: (0, 0, 0))] + xs,
        out_specs=pl.BlockSpec((P, tm, N), lambda i: (0, i, 0)), out_shape=jax.ShapeDtypeStruct((P, M, N), BF16),
        compiler_params=_params(("parallel",)),
    )(a, b, *xa)


def _matmul_planes_nt(a, b, add, *, add_scale, name, tm=512, after=None):
    (P, M, K), (P2, N, K2) = a.shape, b.shape
    assert P == P2 and K == K2 and add.shape == (M, N)
    tm = _tile(M, tm, SUBLANES)

    def body(a_ref, b_ref, add_ref, o_ref):
        r = add_scale * add_ref[...]
        for p in range(P):
            r = r + _dot(a_ref[p], b_ref[p], 1, 1)
        o_ref[...] = r

    row = pl.BlockSpec((tm, N), lambda i: (i, 0))
    body, xs, xa = _after(body, 3, after)
    return pl.pallas_call(
        body, name=name, grid=(M // tm,),
        in_specs=[pl.BlockSpec((P, tm, K), lambda i: (0, i, 0)), pl.BlockSpec((P, N, K), lambda i: (0, 0, 0)), row] + xs,
        out_specs=row, out_shape=jax.ShapeDtypeStruct((M, N), F32),
        compiler_params=_params(("parallel",)),
    )(a, b, add, *xa)


def _ln(z, g, b):
    mu = jnp.mean(z, axis=-1, keepdims=True)
    zc = z - mu
    var = jnp.mean(zc * zc, axis=-1, keepdims=True)
    return zc * lax.rsqrt(var + LN_EPS) * g + b


def _matmul_ln(a, b, h, g, bias, *, name, tgt=None, tm=512, a_t=False):
    (T, K), (K2, Dm) = (a.shape[::-1] if a_t else a.shape), b.shape
    assert K == K2 and h.shape == (T, Dm)
    tm = _tile(T, tm, SUBLANES)
    last = tgt is not None

    def body(*refs):
        a_ref, b_ref, h_ref, g_ref, bias_ref = refs[:5]
        z = ALPHA * h_ref[...] + _dot(a_ref[...], b_ref[...], 0 if a_t else 1, 0)
        if not last:
            z_ref, y_ref, yb_ref = refs[5:]
            y = _ln(z, g_ref[...], bias_ref[...])
            z_ref[...] = z
            y_ref[...] = y
            yb_ref[...] = y.astype(BF16)
            return
        t_ref, dz_ref, dzb_ref, dgb_ref, l_ref, da_ref = refs[5:]

        @pl.when(pl.program_id(0) == 0)
        def _():
            dgb_ref[...] = jnp.zeros_like(dgb_ref)
            l_ref[...] = jnp.zeros_like(l_ref)

        y, vjp = jax.vjp(_ln, z, g_ref[...], bias_ref[...])
        e = y - t_ref[...]
        dz, dg, db = vjp(e * (1.0 / Dm))
        l_ref[...] += 0.5 * jnp.sum(jnp.mean(e * e, axis=-1, keepdims=True), axis=0, keepdims=True)
        dzb = dz.astype(BF16)
        dz_ref[...] = dz
        dzb_ref[...] = dzb
        dgb_ref[...] += jnp.concatenate([dg, db], axis=0)
        da_ref[...] = _dot(dzb, b_ref[...], 1, 1).astype(BF16)

    row = pl.BlockSpec((tm, Dm), lambda i: (i, 0))
    vec = pl.BlockSpec((1, Dm), lambda i: (0, 0))
    a_spec = pl.BlockSpec((K, tm), lambda i: (0, i)) if a_t else pl.BlockSpec((tm, K), lambda i: (i, 0))
    in_specs = [a_spec, pl.BlockSpec((K, Dm), lambda i: (0, 0)), row, vec, vec]
    f32, b16 = jax.ShapeDtypeStruct((T, Dm), F32), jax.ShapeDtypeStruct((T, Dm), BF16)
    if not last:
        return pl.pallas_call(
            body, name=name, grid=(T // tm,), in_specs=in_specs, out_specs=[row, row, row], out_shape=[f32, f32, b16],
            compiler_params=_params(("parallel",)),
        )(a, b, h, g, bias)
    assert not a_t
    return pl.pallas_call(
        body, name=name, grid=(T // tm,), in_specs=in_specs + [row],
        out_specs=[row, row, pl.BlockSpec((2, Dm), lambda i: (0, 0)), pl.BlockSpec((SUBLANES, LANES), lambda i: (0, 0)), a_spec],
        out_shape=[f32, b16, jax.ShapeDtypeStruct((2, Dm), F32), jax.ShapeDtypeStruct((SUBLANES, LANES), F32),
                   jax.ShapeDtypeStruct((T, K), BF16)],
        compiler_params=_params(("arbitrary",)),
    )(a, b, h, g, bias, tgt)


def _ln_bwd_matmul(dy, z, g, b, w, *, name, out_t=False, tm=512, after=None):
    T, Dm = z.shape
    N = w.shape[0]
    tm = _tile(T, tm, LANES if out_t else SUBLANES)

    def body(dy_ref, z_ref, g_ref, b_ref, w_ref, dz_ref, dzb_ref, dgb_ref, o_ref):
        @pl.when(pl.program_id(0) == 0)
        def _():
            dgb_ref[...] = jnp.zeros_like(dgb_ref)

        _, vjp = jax.vjp(_ln, z_ref[...], g_ref[...], b_ref[...])
        dz, dg, db = vjp(dy_ref[...])
        dzb = dz.astype(BF16)
        dz_ref[...] = dz
        dzb_ref[...] = dzb
        dgb_ref[...] += jnp.concatenate([dg, db], axis=0)
        o_ref[...] = (_dot(w_ref[...], dzb, 1, 1) if out_t else _dot(dzb, w_ref[...], 1, 1)).astype(BF16)

    row = pl.BlockSpec((tm, Dm), lambda i: (i, 0))
    vec = pl.BlockSpec((1, Dm), lambda i: (0, 0))
    o_spec = pl.BlockSpec((N, tm), lambda i: (0, i)) if out_t else pl.BlockSpec((tm, N), lambda i: (i, 0))
    body, xs, xa = _after(body, 5, after)
    return pl.pallas_call(
        body, name=name, grid=(T // tm,), in_specs=[row, row, vec, vec, pl.BlockSpec((N, Dm), lambda i: (0, 0))] + xs,
        out_specs=[row, row, pl.BlockSpec((2, Dm), lambda i: (0, 0)), o_spec],
        out_shape=[jax.ShapeDtypeStruct((T, Dm), F32), jax.ShapeDtypeStruct((T, Dm), BF16),
                   jax.ShapeDtypeStruct((2, Dm), F32), jax.ShapeDtypeStruct((N, T) if out_t else (T, N), BF16)],
        compiler_params=_params(("arbitrary",)),
    )(dy, z, g, b, w, *xa)


def _hg_chunk(qr, fr, ir, gr, l0, l1, gw, st):
    C = qr.shape[-2]
    row = lax.broadcasted_iota(jnp.int32, qr.shape, qr.ndim - 2)
    lb = jax.nn.sigmoid(l0 - l1)
    fg = lb + (1.0 - lb) * jax.nn.sigmoid(fr)
    b = _cumsum(jnp.log(fg))
    q = jax.nn.silu(qr)
    k = 1.0 - fg
    bmid = lax.stop_gradient(jnp.sum(jnp.where(row == C // 2 - 1, b, 0.0), axis=-2, keepdims=True))
    bl = jnp.sum(jnp.where(row == C - 1, b, 0.0), axis=-2, keepdims=True)
    o = mm_nt(q * jnp.exp(b), st)
    sc = _scores(q * jnp.exp(b - bmid), k * jnp.exp(bmid - b))
    ti = lax.broadcasted_iota(jnp.int32, (C, C), 0)
    si = lax.broadcasted_iota(jnp.int32, (C, C), 1)
    sc = jnp.where(si <= ti, sc, 0.0)
    o = o + mm(sc, ir)
    st_new = st * jnp.exp(bl) + mm_tn(ir, k * jnp.exp(bl - b))
    on = o * lax.rsqrt(jnp.mean(o * o, axis=-1, keepdims=True) + RMS_EPS)
    return on * gw * jax.nn.silu(gr), st_new


def _heads(ref, rows):
    return jnp.stack([ref[rows, h * HG_DIM:(h + 1) * HG_DIM].astype(F32) for h in range(HG_HEADS)])


def _unheads(x):
    return jnp.concatenate([x[h] for h in range(HG_HEADS)], axis=-1)


def _hgrn_fwd(pre, lbl, gw, *, name):
    _, T, Dm = pre.shape
    rb = min(HG_RB, T)
    C = min(HG_C, rb)
    ncb = rb // C

    def body(pre_ref, lbl_ref, gw_ref, o_ref, st_ref, s_ref):
        @pl.when(pl.program_id(0) == 0)
        def _():
            s_ref[...] = jnp.zeros_like(s_ref)

        def chunk(ci, carry):
            r0 = pl.multiple_of(ci * C, C)
            rows = pl.ds(r0, C)
            st = s_ref[...]
            st_ref[ci] = st
            out, st_new = _hg_chunk(*[_heads(pre_ref.at[j], rows) for j in range(4)],
                                    _heads(lbl_ref, slice(0, 1)), _heads(lbl_ref, slice(1, 2)), gw_ref[...], st)
            o_ref[rows, :] = _unheads(out).astype(BF16)
            s_ref[...] = st_new
            return carry

        lax.fori_loop(0, ncb, chunk, 0, unroll=True)

    row = pl.BlockSpec((rb, Dm), lambda n: (n, 0))
    return pl.pallas_call(
        body, name=name, grid=(T // rb,),
        in_specs=[pl.BlockSpec((4, rb, Dm), lambda n: (0, n, 0)), pl.BlockSpec((2, Dm), lambda n: (0, 0)),
                  pl.BlockSpec((1, HG_DIM), lambda n: (0, 0))],
        out_specs=[row, pl.BlockSpec((ncb, HG_HEADS, HG_DIM, HG_DIM), lambda n: (n, 0, 0, 0))],
        out_shape=[jax.ShapeDtypeStruct((T, Dm), BF16),
                   jax.ShapeDtypeStruct((T // C, HG_HEADS, HG_DIM, HG_DIM), F32)],
        scratch_shapes=[pltpu.VMEM((HG_HEADS, HG_DIM, HG_DIM), F32)],
        compiler_params=_params(("arbitrary",)),
    )(pre, lbl, gw)


def _hgrn_bwd(pre, lbl, gw, states, dout, *, name, after=None):
    _, T, Dm = pre.shape
    rb = min(HG_RB, T)
    C = min(HG_C, rb)
    ncb = rb // C
    nb = T // rb

    def body(pre_ref, lbl_ref, gw_ref, st_ref, do_ref, dpre_ref, dlbl_ref, dgw_ref, ds_ref):
        @pl.when(pl.program_id(0) == 0)
        def _():
            ds_ref[...] = jnp.zeros_like(ds_ref)
            dlbl_ref[...] = jnp.zeros_like(dlbl_ref)
            dgw_ref[...] = jnp.zeros_like(dgw_ref)

        def chunk(cj, carry):
            ci = ncb - 1 - cj
            r0 = pl.multiple_of(ci * C, C)
            rows = pl.ds(r0, C)
            _, vjp = jax.vjp(_hg_chunk, *[_heads(pre_ref.at[j], rows) for j in range(4)],
                             _heads(lbl_ref, slice(0, 1)), _heads(lbl_ref, slice(1, 2)), gw_ref[...], st_ref[ci])
            *dpre, dl0, dl1, dgw, dst = vjp((_heads(do_ref, rows), ds_ref[...]))
            for j in range(4):
                dpre_ref[j, rows, :] = _unheads(dpre[j]).astype(BF16)
            dlbl_ref[0:1, :] += _unheads(dl0)
            dlbl_ref[1:2, :] += _unheads(dl1)
            dgw_ref[...] += dgw
            ds_ref[...] = dst
            return carry

        lax.fori_loop(0, ncb, chunk, 0, unroll=True)

    row = pl.BlockSpec((rb, Dm), lambda n: (nb - 1 - n, 0))
    lsp = pl.BlockSpec((2, Dm), lambda n: (0, 0))
    gsp = pl.BlockSpec((1, HG_DIM), lambda n: (0, 0))
    pre_spec = pl.BlockSpec((4, rb, Dm), lambda n: (0, nb - 1 - n, 0))
    body, xs, xa = _after(body, 5, after)
    return pl.pallas_call(
        body, name=name, grid=(nb,),
        in_specs=[pre_spec, lsp, gsp, pl.BlockSpec((ncb, HG_HEADS, HG_DIM, HG_DIM), lambda n: (nb - 1 - n, 0, 0, 0)), row] + xs,
        out_specs=[pre_spec, lsp, gsp],
        out_shape=[jax.ShapeDtypeStruct((4, T, Dm), BF16), jax.ShapeDtypeStruct((2, Dm), F32),
                   jax.ShapeDtypeStruct((1, HG_DIM), F32)],
        scratch_shapes=[pltpu.VMEM((HG_HEADS, HG_DIM, HG_DIM), F32)],
        compiler_params=_params(("arbitrary",)),
    )(pre, lbl, gw, states, dout, *xa)


CONV_HALO = 2 * SUBLANES


def _conv_rows(u_ref, scr, w, bias, r0, R):
    cur = u_ref[pl.ds(r0, R), :].astype(F32)
    p0 = pl.multiple_of(jnp.maximum(r0 - CONV_HALO, 0), CONV_HALO)
    scr[0:CONV_HALO, :] = jnp.where(r0 > 0, u_ref[pl.ds(p0, CONV_HALO), :].astype(F32), 0.0)
    scr[CONV_HALO:CONV_HALO + R, :] = cur
    s1 = scr[CONV_HALO - 1:CONV_HALO - 1 + R, :]
    s2 = scr[CONV_HALO - 2:CONV_HALO - 2 + R, :]
    return w[0:1, :] * s2 + w[1:2, :] * s1 + w[2:3, :] * cur + bias, cur, s1, s2


def _halves_spec(T, Fd):
    per = Fd // 2 // LANES
    return pl.BlockSpec((2, None, T, LANES), lambda j: (0, j // per, 0, j % per))


def _conv_gate_fwd(u, wa, wb, ba, bb, *, name):
    T, Fd = u.shape[2], 2 * u.shape[3]
    R = min(CONV_R, T)
    tc = LANES

    def body(u_ref, wa_ref, wb_ref, ba_ref, bb_ref, o_ref, sa, sb):
        wa_, wb_, ba_, bb_ = wa_ref[...], wb_ref[...], ba_ref[...], bb_ref[...]

        def step(ri, carry):
            r0 = pl.multiple_of(ri * R, R)
            ca = _conv_rows(u_ref.at[0], sa, wa_, ba_, r0, R)[0]
            cb = _conv_rows(u_ref.at[1], sb, wb_, bb_, r0, R)[0]
            o_ref[pl.ds(r0, R), :] = (jax.nn.silu(ca) * cb).astype(BF16)
            return carry

        lax.fori_loop(0, T // R, step, 0)

    col = pl.BlockSpec((T, tc), lambda j: (0, j))
    wsp = pl.BlockSpec((3, tc), lambda j: (0, j))
    bsp = pl.BlockSpec((1, tc), lambda j: (0, j))
    both = _halves_spec(T, Fd)
    return pl.pallas_call(
        body, name=name, grid=(Fd // tc,), in_specs=[both, wsp, wsp, bsp, bsp], out_specs=col,
        out_shape=jax.ShapeDtypeStruct((T, Fd), BF16),
        scratch_shapes=[pltpu.VMEM((CONV_HALO + R, tc), F32)] * 2,
        compiler_params=_params(("parallel",)),
    )(u, wa, wb, ba, bb)


def _conv_gate_bwd(u, wa, wb, ba, bb, dact, *, name):
    T, Fd = u.shape[2], 2 * u.shape[3]
    R = min(CONV_R, T)
    nr = T // R
    tc = LANES

    def body(u_ref, wa_ref, wb_ref, ba_ref, bb_ref, da_ref,
             du_ref, dp_ref, sa, sb, sda, sdb):
        wa_, wb_, ba_, bb_ = wa_ref[...], wb_ref[...], ba_ref[...], bb_ref[...]
        sda[R:R + SUBLANES, :] = jnp.zeros((SUBLANES, tc), F32)
        sdb[R:R + SUBLANES, :] = jnp.zeros((SUBLANES, tc), F32)

        def taps(dc, cur, s1, s2):
            return jnp.concatenate([jnp.sum(dc * s2, axis=0, keepdims=True), jnp.sum(dc * s1, axis=0, keepdims=True),
                                    jnp.sum(dc * cur, axis=0, keepdims=True)], axis=0)

        def du_rows(sd, dc, w):
            sd[0:R, :] = dc
            du = w[2:3, :] * dc + w[1:2, :] * sd[1:1 + R, :] + w[0:1, :] * sd[2:2 + R, :]
            sd[R:R + SUBLANES, :] = dc[0:SUBLANES]
            return du

        def step(rj, carry):
            dwa, dwb, dba, dbb = carry
            r0 = pl.multiple_of((nr - 1 - rj) * R, R)
            ca, cura, s1a, s2a = _conv_rows(u_ref.at[0], sa, wa_, ba_, r0, R)
            cb, curb, s1b, s2b = _conv_rows(u_ref.at[1], sb, wb_, bb_, r0, R)
            dact_ = da_ref[pl.ds(r0, R), :].astype(F32)
            sg = jax.nn.sigmoid(ca)
            dca = dact_ * cb * (sg * (1.0 + ca * (1.0 - sg)))
            dcb = dact_ * (ca * sg)
            du_ref[0, pl.ds(r0, R), :] = du_rows(sda, dca, wa_).astype(BF16)
            du_ref[1, pl.ds(r0, R), :] = du_rows(sdb, dcb, wb_).astype(BF16)
            return (dwa + taps(dca, cura, s1a, s2a), dwb + taps(dcb, curb, s1b, s2b),
                    dba + jnp.sum(dca, axis=0, keepdims=True), dbb + jnp.sum(dcb, axis=0, keepdims=True))

        z3 = jnp.zeros((3, tc), F32)
        z1 = jnp.zeros((1, tc), F32)
        dwa, dwb, dba, dbb = lax.fori_loop(0, nr, step, (z3, z3, z1, z1))
        dp_ref[0] = jnp.concatenate([dwa, dba], axis=0)
        dp_ref[1] = jnp.concatenate([dwb, dbb], axis=0)

    col = pl.BlockSpec((T, tc), lambda j: (0, j))
    wsp = pl.BlockSpec((3, tc), lambda j: (0, j))
    bsp = pl.BlockSpec((1, tc), lambda j: (0, j))
    both = _halves_spec(T, Fd)
    return pl.pallas_call(
        body, name=name, grid=(Fd // tc,), in_specs=[both, wsp, wsp, bsp, bsp, col],
        out_specs=[both, pl.BlockSpec((2, 4, tc), lambda j: (0, 0, j))],
        out_shape=[jax.ShapeDtypeStruct(u.shape, BF16), jax.ShapeDtypeStruct((2, 4, Fd), F32)],
        scratch_shapes=[pltpu.VMEM((CONV_HALO + R, tc), F32)] * 2 + [pltpu.VMEM((R + SUBLANES, tc), F32)] * 2,
        compiler_params=_params(("parallel",)),
    )(u, wa, wb, ba, bb, dact)


def _bucket_index():
    t = np.arange(SW_WINDOW)[None, :] + SW_WINDOW
    s = np.arange(2 * SW_WINDOW)[:, None]
    dist = np.maximum(t - s, 0)
    exact = REL_BUCKETS // 2
    d = np.maximum(dist, 1).astype(np.float32)
    log_b = exact + (np.log(d / np.float32(exact)) / np.float32(math.log(REL_MAX_DIST / exact))
                     * np.float32(REL_BUCKETS - exact)).astype(np.int32)
    bucket = np.where(dist < exact, dist, np.minimum(log_b, REL_BUCKETS - 1))
    return bucket.astype(np.int32).reshape(1, -1)


BIAS_COLS = SW_WINDOW * 2 * SW_WINDOW
BIAS_TILE = 4096


def _bias_from_table(table, bucket, *, name):
    def body(t_ref, idx_ref, o_ref):
        onehot = (lax.broadcasted_iota(jnp.int32, (REL_BUCKETS, BIAS_TILE), 0) == idx_ref[...]).astype(BF16)
        acc = jnp.zeros((SW_Q_HEADS, BIAS_TILE), F32)
        for piece in _split3(t_ref[...]):
            acc = acc + lax.dot_general(piece, onehot, (((0,), (0,)), ((), ())), preferred_element_type=F32)
        o_ref[...] = acc

    return pl.pallas_call(
        body, name=name, grid=(BIAS_COLS // BIAS_TILE,),
        in_specs=[pl.BlockSpec((REL_BUCKETS, SW_Q_HEADS), lambda j: (0, 0)), pl.BlockSpec((1, BIAS_TILE), lambda j: (0, j))],
        out_specs=pl.BlockSpec((SW_Q_HEADS, BIAS_TILE), lambda j: (0, j)),
        out_shape=jax.ShapeDtypeStruct((SW_Q_HEADS, BIAS_COLS), F32),
        compiler_params=_params(("parallel",)),
    )(table, bucket)


def _table_grad(dbias, bucket, *, name):
    def body(d_ref, idx_ref, o_ref):
        @pl.when(pl.program_id(0) == 0)
        def _():
            o_ref[...] = jnp.zeros_like(o_ref)

        onehot = (lax.broadcasted_iota(jnp.int32, (REL_BUCKETS, BIAS_TILE), 0) == idx_ref[...]).astype(BF16)
        acc = jnp.zeros((REL_BUCKETS, SW_Q_HEADS), F32)
        for piece in _split3(d_ref[...]):
            acc = acc + lax.dot_general(onehot, piece, (((1,), (1,)), ((), ())), preferred_element_type=F32)
        o_ref[...] += acc

    return pl.pallas_call(
        body, name=name, grid=(BIAS_COLS // BIAS_TILE,),
        in_specs=[pl.BlockSpec((SW_Q_HEADS, BIAS_TILE), lambda j: (0, j)), pl.BlockSpec((1, BIAS_TILE), lambda j: (0, j))],
        out_specs=pl.BlockSpec((REL_BUCKETS, SW_Q_HEADS), lambda j: (0, 0)),
        out_shape=jax.ShapeDtypeStruct((REL_BUCKETS, SW_Q_HEADS), F32),
        compiler_params=_params(("arbitrary",)),
    )(dbias, bucket)


KV_DIM = SW_KV_HEADS * SW_HEAD_DIM
GROUP_ROWS = SW_GROUP * SW_HEAD_DIM
GROUP_LANES = SW_GROUP * SW_WINDOW


def _band_mask(n):
    s = lax.broadcasted_iota(jnp.int32, (2 * SW_WINDOW, GROUP_LANES), 0)
    t = (lax.broadcasted_iota(jnp.int32, (2 * SW_WINDOW, GROUP_LANES), 1) & (SW_WINDOW - 1)) + SW_WINDOW
    dist = t - s
    return (dist >= 0) & (dist < SW_WINDOW) & ((n > 0) | (s >= SW_WINDOW))


def _side_by_side(x_ref, g):
    r0 = g * GROUP_ROWS
    return jnp.concatenate([x_ref[r0 + r * SW_HEAD_DIM:r0 + (r + 1) * SW_HEAD_DIM, :] for r in range(SW_GROUP)], axis=1)


def _group_inputs(bias_ref, sink_ref, g):
    heads = range(g * SW_GROUP, (g + 1) * SW_GROUP)
    bias = jnp.concatenate([bias_ref[h] for h in heads], axis=1)
    sink = jnp.concatenate([jnp.broadcast_to(sink_ref[:, h:h + 1], (1, SW_WINDOW)) for h in heads], axis=1)
    return heads, bias, sink


def _kv_pair(kvp_ref, kvc_ref, g):
    ks = slice(g * SW_HEAD_DIM, (g + 1) * SW_HEAD_DIM)
    vs = slice(KV_DIM + g * SW_HEAD_DIM, KV_DIM + (g + 1) * SW_HEAD_DIM)
    kk = jnp.concatenate([kvp_ref[:, ks], kvc_ref[:, ks]], axis=0)
    vv = jnp.concatenate([kvp_ref[:, vs], kvc_ref[:, vs]], axis=0)
    return kk, vv, ks, vs


def _col_max(x):
    return jnp.max(x, axis=0, keepdims=True)


def _col_sum(x):
    return jnp.sum(x, axis=0, keepdims=True)


def _attn_fwd(qt, kv, bias, sinks, *, name):
    Dm, T = qt.shape
    W = SW_WINDOW

    def body(q_ref, kvc_ref, kvp_ref, bias_ref, sink_ref, o_ref):
        mask = _band_mask(pl.program_id(0))
        G = range(SW_KV_HEADS)
        ins = [_group_inputs(bias_ref, sink_ref, g) for g in G]
        kvs = [_kv_pair(kvp_ref, kvc_ref, g) for g in G]
        q = [_side_by_side(q_ref, g) for g in G]
        lg = [jnp.where(mask, mm(kvs[g][0], q[g]) * (SW_HEAD_DIM ** -0.5) + ins[g][1], -jnp.inf) for g in G]
        m = [jnp.maximum(_col_max(lg[g]), ins[g][2]) for g in G]
        p = [jnp.exp(lg[g] - m[g]) for g in G]
        den = [_col_sum(p[g]) + jnp.exp(ins[g][2] - m[g]) for g in G]
        o = [mm_tn(kvs[g][1], p[g]) / den[g] for g in G]
        for g in G:
            for r in range(SW_GROUP):
                o_ref[g * GROUP_ROWS + r * SW_HEAD_DIM:g * GROUP_ROWS + (r + 1) * SW_HEAD_DIM, :] = (
                    o[g][:, r * W:(r + 1) * W].astype(BF16))

    return pl.pallas_call(
        body, name=name, grid=(T // W,),
        in_specs=[pl.BlockSpec((Dm, W), lambda n: (0, n)),
                  pl.BlockSpec((W, 2 * KV_DIM), lambda n: (n, 0)),
                  pl.BlockSpec((W, 2 * KV_DIM), lambda n: (jnp.maximum(n - 1, 0), 0)),
                  pl.BlockSpec((SW_Q_HEADS, 2 * W, W), lambda n: (0, 0, 0)),
                  pl.BlockSpec((1, SW_Q_HEADS), lambda n: (0, 0))],
        out_specs=pl.BlockSpec((Dm, W), lambda n: (0, n)),
        out_shape=jax.ShapeDtypeStruct((Dm, T), BF16),
        compiler_params=_params(("parallel",)),
    )(qt, kv, kv, bias, sinks)


def _attn_bwd(qt, kv, bias, sinks, dot, *, name):
    Dm, T = qt.shape
    W = SW_WINDOW
    nb = T // W

    def body(q_ref, kvc_ref, kvp_ref, bias_ref, sink_ref, do_ref,
             dq_ref, dkv_ref, dbias_ref, dsink_ref, carry_ref):
        @pl.when(pl.program_id(0) == 0)
        def _():
            carry_ref[...] = jnp.zeros_like(carry_ref)
            dbias_ref[...] = jnp.zeros_like(dbias_ref)
            dsink_ref[...] = jnp.zeros_like(dsink_ref)

        n = nb - 1 - pl.program_id(0)
        mask = _band_mask(n)
        lane = lax.broadcasted_iota(jnp.int32, (1, SW_Q_HEADS), 1)
        sc = SW_HEAD_DIM ** -0.5
        G = range(SW_KV_HEADS)
        ins = [_group_inputs(bias_ref, sink_ref, g) for g in G]
        kvs = [_kv_pair(kvp_ref, kvc_ref, g) for g in G]
        q = [_side_by_side(q_ref, g) for g in G]
        do = [_side_by_side(do_ref, g) for g in G]
        lg = [jnp.where(mask, mm(kvs[g][0], q[g]) * sc + ins[g][1], -jnp.inf) for g in G]
        m = [jnp.maximum(_col_max(lg[g]), ins[g][2]) for g in G]
        p = [jnp.exp(lg[g] - m[g]) for g in G]
        ps = [jnp.exp(ins[g][2] - m[g]) for g in G]
        rden = [1.0 / (_col_sum(p[g]) + ps[g]) for g in G]
        pn = [p[g] * rden[g] for g in G]
        dpn = [mm(kvs[g][1], do[g]) for g in G]
        delta = [_col_sum(pn[g] * dpn[g]) for g in G]
        ds = [pn[g] * (dpn[g] - delta[g]) for g in G]
        dsr = [-(ps[g] * rden[g]) * delta[g] for g in G]
        dq = [mm_tn(kvs[g][0], ds[g]) * sc for g in G]
        dkk = [mm_nt(ds[g], q[g]) * sc for g in G]
        dvv = [mm_nt(pn[g], do[g]) for g in G]
        dsink = jnp.zeros((1, SW_Q_HEADS), F32)
        for g in G:
            _, _, ks, vs = kvs[g]
            for r, h in enumerate(ins[g][0]):
                cols = slice(r * W, (r + 1) * W)
                dbias_ref[h] += ds[g][:, cols]
                dq_ref[g * GROUP_ROWS + r * SW_HEAD_DIM:g * GROUP_ROWS + (r + 1) * SW_HEAD_DIM, :] = dq[g][:, cols].astype(BF16)
                dsink = dsink + jnp.where(lane == h, jnp.sum(dsr[g][:, cols], axis=1, keepdims=True), 0.0)
            dkv_ref[:, ks] = (carry_ref[:, ks] + dkk[g][W:]).astype(BF16)
            dkv_ref[:, vs] = (carry_ref[:, vs] + dvv[g][W:]).astype(BF16)
            carry_ref[:, ks] = dkk[g][:W]
            carry_ref[:, vs] = dvv[g][:W]
        dsink_ref[...] += dsink

    rev = lambda n: (nb - 1 - n, 0)
    revt = lambda n: (0, nb - 1 - n)
    return pl.pallas_call(
        body, name=name, grid=(nb,),
        in_specs=[pl.BlockSpec((Dm, W), revt),
                  pl.BlockSpec((W, 2 * KV_DIM), rev),
                  pl.BlockSpec((W, 2 * KV_DIM), lambda n: (jnp.maximum(nb - 2 - n, 0), 0)),
                  pl.BlockSpec((SW_Q_HEADS, 2 * W, W), lambda n: (0, 0, 0)),
                  pl.BlockSpec((1, SW_Q_HEADS), lambda n: (0, 0)),
                  pl.BlockSpec((Dm, W), revt)],
        out_specs=[pl.BlockSpec((Dm, W), revt), pl.BlockSpec((W, 2 * KV_DIM), rev),
                   pl.BlockSpec((SW_Q_HEADS, 2 * W, W), lambda n: (0, 0, 0)),
                   pl.BlockSpec((1, SW_Q_HEADS), lambda n: (0, 0))],
        out_shape=[jax.ShapeDtypeStruct((Dm, T), BF16), jax.ShapeDtypeStruct((T, 2 * KV_DIM), BF16),
                   jax.ShapeDtypeStruct((SW_Q_HEADS, 2 * W, W), F32), jax.ShapeDtypeStruct((1, SW_Q_HEADS), F32)],
        scratch_shapes=[pltpu.VMEM((W, 2 * KV_DIM), F32)],
        compiler_params=_params(("arbitrary",)),
    )(qt, kv, kv, bias, sinks, dot)


def _ffn_fwd(hb, w, l, after=None):
    u = _matmul_planes_nn(hb, w["ffn_in"][l], name=f"ffn{l}_up", after=after)
    u = u.reshape((2, 2) + u.shape[1:])
    act = _conv_gate_fwd(u, w["conv_w_a"][l], w["conv_w_b"][l], w["conv_b_a"][l], w["conv_b_b"][l],
                         name=f"ffn{l}_conv_gate")
    return u, act


def _ffn_bwd(dffb, dh_scaled, hb, u, act, w, l, dact):
    g_out = _matmul(act, dffb, mode="tn", name=f"ffn{l}_down_dw", tm=1408, tn=1024, tk=2048, also_bf16=True)
    du, g_conv = _conv_gate_bwd(u, w["conv_w_a"][l], w["conv_w_b"][l], w["conv_b_a"][l], w["conv_b_b"][l],
                                dact, name=f"ffn{l}_conv_gate_bwd")
    du = du.reshape((N_CHIPS,) + du.shape[2:])
    dh = _matmul_planes_nt(du, w["ffn_in"][l], dh_scaled, add_scale=ALPHA, name=f"ffn{l}_up_dx")
    g_in = _matmul(hb, du, mode="tn", planes="n", name=f"ffn{l}_up_dw", tm=1024, tn=FFN_DIM // 2, tk=2048, also_bf16=True)
    return dh, dict(ffn_out=g_out, ffn_in=g_in, conv=g_conv)


def _local_step(x, xb, tgt, w, more_weights, emit):
    bucket = jnp.asarray(_bucket_index())

    pre = _matmul_planes_nn(xb, w["hg_in"], name="hg_in", after=w.get("token"))
    og, states = _hgrn_fwd(pre, w["lb_logits"], w["gnorm"], name="hgrn_fwd")
    z1, h1, h1b = _matmul_ln(og, w["hg_out"], x, w["ln_mix_g"][0], w["ln_mix_b"][0], name="hg_out_ln")
    w = {**w, **more_weights(1, h1b)}
    u0, act0 = _ffn_fwd(h1b, w, 0, after=w.get("token"))
    z2, h2, h2b = _matmul_ln(act0, w["ffn_out"][0], h1, w["ln_ffn_g"][0], w["ln_ffn_b"][0], name="ffn0_down_ln")
    kv = _matmul(h2b, w["kv"], mode="nn", out_dtype=BF16, name="kv_proj")

    bias = _bias_from_table(w["rel_bias"], bucket, name="rel_bias_expand").reshape(SW_Q_HEADS, 2 * SW_WINDOW, SW_WINDOW)
    q1 = _matmul(w["sw_q"], h2b, mode="tt", out_dtype=BF16, name="sw_q", tm=1024, tn=1024)
    o1 = _attn_fwd(q1, kv, bias, w["sinks"], name="attn_fwd")
    z3, h3, h3b = _matmul_ln(o1, w["sw_out"], h2, w["ln_mix_g"][1], w["ln_mix_b"][1], a_t=True, name="sw_out_ln")
    w = {**w, **more_weights(2, h3b)}
    u1, act1 = _ffn_fwd(h3b, w, 1)

    g = {}
    dz, dzb, g["ln_ffn1"], loss_tile, dact1 = _matmul_ln(act1, w["ffn_out"][1], h3, w["ln_ffn_g"][1], w["ln_ffn_b"][1],
                                                         tgt=tgt, name="ffn1_down_ln_loss")

    dh3, gf1 = _ffn_bwd(dzb, dz, h3b, u1, act1, w, 1, dact1)
    dz, dzb, g["ln_mix1"], do1 = _ln_bwd_matmul(dh3, z3, w["ln_mix_g"][1], w["ln_mix_b"][1], w["sw_out"], out_t=True,
                                                name="ln_mix1_bwd_sw_out_dx")
    g_sw_out = _matmul(o1, dzb, mode="nn", name="sw_out_dw", tm=1024, tn=1024, tk=2048, also_bf16=True)
    dq1, dkv, dbias, dsinks = _attn_bwd(q1, kv, bias, w["sinks"], do1, name="attn_bwd")
    g["sinks"] = dsinks
    g["rel_bias"] = _table_grad(dbias.reshape(SW_Q_HEADS, BIAS_COLS), bucket, name="rel_bias_grad")
    dh2 = _matmul(dq1, w["sw_q"], mode="tt", add=dz, add_scale=ALPHA, name="sw_q_dx", tn=1024)
    dh2 = _matmul(dkv, w["kv"], mode="nt", add=dh2, name="kv_dx", tn=1024)
    g_sw_q = _matmul(h2b, dq1, mode="tt", name="sw_q_dw", tm=1024, tn=1024, tk=2048, also_bf16=True)
    g_kv = _matmul(h2b, dkv, mode="tn", name="kv_dw", tm=1024, tn=512, tk=2048, also_bf16=True)
    tok = emit(1, dict(sw_q=g_sw_q, sw_out=g_sw_out, kv=g_kv, ffn_in=gf1["ffn_in"], ffn_out=gf1["ffn_out"]))

    dz, dzb, g["ln_ffn0"], dact0 = _ln_bwd_matmul(dh2, z2, w["ln_ffn_g"][0], w["ln_ffn_b"][0], w["ffn_out"][0],
                                                  name="ln_ffn0_bwd_down_dx", after=tok)
    dh1, gf0 = _ffn_bwd(dzb, dz, h1b, u0, act0, w, 0, dact0)
    dz, dzb, g["ln_mix0"], dog = _ln_bwd_matmul(dh1, z1, w["ln_mix_g"][0], w["ln_mix_b"][0], w["hg_out"],
                                                name="ln_mix0_bwd_hg_out_dx")
    g_hg_out = _matmul(og, dzb, mode="tn", name="hg_out_dw", tm=1024, tn=1024, tk=2048, also_bf16=True)
    tok = emit(2, dict(hg_out=g_hg_out, ffn_in=gf0["ffn_in"], ffn_out=gf0["ffn_out"]))
    dpre, g["lb_logits"], g["gnorm"] = _hgrn_bwd(pre, w["lb_logits"], w["gnorm"], states, dog, name="hgrn_bwd", after=tok)
    tok = emit(3, dict(hg_in=_matmul(xb, dpre, mode="tn", planes="n", name="hg_in_dw", tm=1024, tn=1024, tk=2048, also_bf16=True)))
    dx = _matmul_planes_nt(dpre, w["hg_in"], dz, add_scale=ALPHA, name="hg_in_dx", after=tok)
    g["conv0"], g["conv1"] = gf0["conv"], gf1["conv"]
    return loss_tile, dx, g


def _adamw(wt, ga, gb, m, v, *, name, rows=None, prev=None):
    R, Cc = wt.shape
    r0, n = rows if rows is not None else (0, R)
    tr = _tile(n, 256, SUBLANES) if n % SUBLANES == 0 else n
    assert r0 % tr == 0
    c1 = 1.0 - ADAM_B1 ** ADAM_STEP
    c2 = 1.0 - ADAM_B2 ** ADAM_STEP
    n_in = 5

    def body(*refs):
        w_ref, ga_ref, gb_ref, m_ref, v_ref = refs[:n_in]
        g_ = ga_ref[...] + gb_ref[...]
        g_ref, d_ref, nm_ref, nv_ref = refs[-4:]
        nm = ADAM_B1 * m_ref[...] + (1.0 - ADAM_B1) * g_
        nv = ADAM_B2 * v_ref[...] + (1.0 - ADAM_B2) * (g_ * g_)
        g_ref[...] = g_
        d_ref[...] = -ADAM_LR * ((nm / c1) / (jnp.sqrt(nv / c2) + ADAM_EPS) + ADAM_WD * w_ref[...])
        nm_ref[...] = nm
        nv_ref[...] = nv

    full = pl.BlockSpec((tr, Cc), lambda i: (i + r0 // tr, 0))
    part = pl.BlockSpec((tr, Cc), lambda i: (i, 0))
    args = (wt, ga, gb, m, v)
    in_specs = [full, part, part, full, full]
    aliases = {}
    if prev is not None:
        args, in_specs = args + tuple(prev), in_specs + [ANY_SPEC] * 4
        aliases = {n_in + t: t for t in range(4)}
    return pl.pallas_call(
        body, name=name, grid=(n // tr,), in_specs=in_specs, out_specs=[full] * 4,
        out_shape=[jax.ShapeDtypeStruct((R, Cc), F32)] * 4, input_output_aliases=aliases,
        compiler_params=_params(("parallel",)),
    )(*args)


def _adamw_small(ws, gs, ms, vs, *, name):
    n = len(ws)
    c1 = 1.0 - ADAM_B1 ** ADAM_STEP
    c2 = 1.0 - ADAM_B2 ** ADAM_STEP

    def body(*refs):
        w_refs, g_refs, m_refs, v_refs = (refs[k * n:(k + 1) * n] for k in range(4))
        d_refs, nm_refs, nv_refs = (refs[(4 + k) * n:(5 + k) * n] for k in range(3))
        for i in range(n):
            g_ = g_refs[i][...]
            nm = ADAM_B1 * m_refs[i][...] + (1.0 - ADAM_B1) * g_
            nv = ADAM_B2 * v_refs[i][...] + (1.0 - ADAM_B2) * (g_ * g_)
            d_refs[i][...] = -ADAM_LR * ((nm / c1) / (jnp.sqrt(nv / c2) + ADAM_EPS) + ADAM_WD * w_refs[i][...])
            nm_refs[i][...] = nm
            nv_refs[i][...] = nv

    vm = pl.BlockSpec(memory_space=pltpu.VMEM)
    out = pl.pallas_call(
        body, name=name, in_specs=[vm] * (4 * n), out_specs=[vm] * (3 * n),
        out_shape=[jax.ShapeDtypeStruct(w.shape, F32) for w in ws] * 3,
    )(*ws, *gs, *ms, *vs)
    return out[:n], out[n:2 * n], out[2 * n:]


HBM_SPEC = pl.BlockSpec(memory_space=pltpu.HBM)
SEM_SPEC = pl.BlockSpec(memory_space=pltpu.SEMAPHORE)
VMEM_SPEC = pl.BlockSpec(memory_space=pltpu.VMEM)
DATAFLOW = pltpu.SideEffectType.DATAFLOW_SIDE_EFFECTING


def _in_hbm(a):
    return pltpu.with_memory_space_constraint(a, pltpu.HBM)


def _place():
    return lax.axis_index("x"), lax.axis_index("y"), lax.axis_index("c")


def _other_chips(x, y):
    return [(1 - x, y), (x, 1 - y), (1 - x, 1 - y)]


def _sum8(vs, *, name):
    n = len(vs)

    def body(*refs):
        v_refs, all_refs, o_refs = refs[:n], refs[n:2 * n], refs[2 * n:3 * n]
        send_sems, recv_sems, local_sems = refs[3 * n:]
        x, y, c = _place()
        me, sibling = (x, y, c), (x, y, 1 - c)
        chips = _other_chips(x, y)

        def slot(i, px, py, pc):
            return all_refs[i].at[4 * px + 2 * py + pc]

        def copy(i, k, block, to, src=None):
            return pltpu.make_async_remote_copy(
                src_ref=slot(i, *block) if src is None else src, dst_ref=slot(i, *block),
                send_sem=send_sems.at[7 * i + k], recv_sem=recv_sems.at[7 * i + k], device_id=to, device_id_type=MESH)

        mine = [pltpu.make_async_copy(v_refs[i], slot(i, *me), local_sems.at[i]) for i in range(n)]
        for cp in mine:
            cp.start()
        first = [copy(i, 0, me, sibling, src=v_refs[i]) for i in range(n)]
        first += [copy(i, 1 + j, me, (*chip, c), src=v_refs[i]) for i in range(n) for j, chip in enumerate(chips)]
        for cp in first:
            cp.start()
        passed = []
        for i in range(n):
            for j, chip in enumerate(chips):
                copy(i, 1 + j, (*chip, c), me).wait_recv()
                passed.append(copy(i, 4 + j, (*chip, c), sibling))
                passed[-1].start()
        for i in range(n):
            copy(i, 0, sibling, me).wait_recv()
            for j, chip in enumerate(chips):
                copy(i, 4 + j, (*chip, 1 - c), me).wait_recv()
        for cp in first + passed:
            cp.wait_send()
        for cp in mine:
            cp.wait()
        for i in range(n):
            acc = all_refs[i][0]
            for d in range(1, N_DEV):
                acc = acc + all_refs[i][d]
            o_refs[i][...] = acc

    return pl.pallas_call(
        body, name=name, in_specs=[VMEM_SPEC] * n, out_specs=[VMEM_SPEC] * (2 * n),
        out_shape=[jax.ShapeDtypeStruct((N_DEV,) + v.shape, F32) for v in vs] + [jax.ShapeDtypeStruct(v.shape, F32) for v in vs],
        scratch_shapes=[pltpu.SemaphoreType.DMA((7 * n,)), pltpu.SemaphoreType.DMA((7 * n,)), pltpu.SemaphoreType.DMA((n,))],
        compiler_params=pltpu.CompilerParams(vmem_limit_bytes=VMEM_LIMIT),
    )(*vs)[n:]


def _swap_copies(src, land, send, recv):
    x, y, c = _place()
    return [pltpu.make_async_remote_copy(src_ref=src[i], dst_ref=land[i], send_sem=send.at[i], recv_sem=recv.at[i],
                                         device_id=(x, y, 1 - c), device_id_type=MESH) for i in range(len(src))]


def _swap_start(vs, *, name):
    n = len(vs)

    def body(*refs):
        src, land, send, recv, token = refs[:n], refs[n:2 * n], refs[2 * n], refs[2 * n + 1], refs[-1]
        for cp in _swap_copies(src, land, send, recv):
            cp.start()
        token[...] = jnp.zeros_like(token)

    lands = [lax.empty(v.shape, v.dtype) for v in vs]
    sems = pltpu.SemaphoreType.DMA((n,))
    out = pl.pallas_call(
        body, name=name, in_specs=[HBM_SPEC] * (2 * n),
        out_specs=[SEM_SPEC, SEM_SPEC] + [HBM_SPEC] * (2 * n) + [VMEM_SPEC],
        out_shape=[sems, sems] + [pltpu.HBM(a.shape, a.dtype) for a in list(vs) + lands]
        + [jax.ShapeDtypeStruct((SUBLANES, LANES), F32)],
        input_output_aliases={i: 2 + i for i in range(2 * n)},
        compiler_params=pltpu.CompilerParams(has_side_effects=DATAFLOW),
    )(*[_in_hbm(a) for a in list(vs) + lands])
    return (out[0], out[1], out[2:2 + n], out[2 + n:2 + 2 * n]), out[-1]


def _swap_wait(handle, after, *, name):
    send_sems, recv_sems, srcs, lands = handle
    n = len(srcs)

    def body(*refs):
        src, land, send, recv = refs[:n], refs[n:2 * n], refs[2 * n], refs[2 * n + 1]
        for cp in _swap_copies(src, land, send, recv):
            cp.wait_send()
            cp.wait_recv()

    both = list(srcs) + list(lands)
    out = pl.pallas_call(
        body, name=name, in_specs=[HBM_SPEC] * (2 * n) + [SEM_SPEC, SEM_SPEC, ANY_SPEC], out_specs=[HBM_SPEC] * (2 * n),
        out_shape=[pltpu.HBM(a.shape, a.dtype) for a in both],
        input_output_aliases={i: i for i in range(2 * n)},
        compiler_params=pltpu.CompilerParams(has_side_effects=DATAFLOW),
    )(*both, send_sems, recv_sems, after)
    return out[:n], out[n:]


def _gather_copies(srcs, lands, send, recv, sibling=False):
    x, y, c = _place()
    out = []
    for i, (src, land) in enumerate(zip(srcs, lands)):
        half = land.shape[1] // 2
        rows = pl.ds(c * half, half)
        for k, (px, py) in enumerate(_other_chips(x, y)):
            if sibling:
                src_ref, dst_ref, to = src.at[2 * px + py, rows], land.at[2 * px + py, rows], (x, y, 1 - c)
            else:
                src_ref, dst_ref, to = src.at[rows], land.at[2 * x + y, rows], (px, py, c)
            out.append(pltpu.make_async_remote_copy(src_ref=src_ref, dst_ref=dst_ref, send_sem=send.at[3 * i + k],
                                                    recv_sem=recv.at[3 * i + k], device_id=to, device_id_type=MESH))
    return out


def _gather_arrivals(lands, send, recv, sibling=False):
    x, y, c = _place()
    out = []
    for i, land in enumerate(lands):
        half = land.shape[1] // 2
        rows = pl.ds(((1 - c) if sibling else c) * half, half)
        for k, (px, py) in enumerate(_other_chips(x, y)):
            part = land.at[2 * px + py, rows]
            out.append(pltpu.make_async_remote_copy(src_ref=part, dst_ref=part, send_sem=send.at[3 * i + k],
                                                    recv_sem=recv.at[3 * i + k],
                                                    device_id=(x, y, 1 - c) if sibling else (px, py, c), device_id_type=MESH))
    return out


def _own_copies(srcs, lands, sems):
    x, y, _ = _place()
    return [pltpu.make_async_copy(src, land.at[2 * x + y], sems.at[i]) for i, (src, land) in enumerate(zip(srcs, lands))]


def _gather_start(shards, after, *, name, own_too):
    n = len(shards)

    def body(*refs):
        srcs, lands, (send, recv, own), token = refs[:n], refs[n:2 * n], refs[2 * n:2 * n + 3], refs[-1]
        for cp in _gather_copies(srcs, lands, send, recv) + (_own_copies(srcs, lands, own) if own_too else []):
            cp.start()
        token[...] = jnp.zeros_like(token)

    lands = [lax.empty((N_CHIPS,) + s.shape, s.dtype) for s in shards]
    sems = pltpu.SemaphoreType.DMA((3 * n,))
    body, xs, xa = _after(body, 2 * n, after)
    out = pl.pallas_call(
        body, name=name, in_specs=[HBM_SPEC] * (2 * n) + xs,
        out_specs=[SEM_SPEC] * 3 + [HBM_SPEC] * (2 * n) + [VMEM_SPEC],
        out_shape=[sems, sems, pltpu.SemaphoreType.DMA((n,))] + [pltpu.HBM(a.shape, a.dtype) for a in list(shards) + lands]
        + [jax.ShapeDtypeStruct((SUBLANES, LANES), F32)],
        input_output_aliases={i: 3 + i for i in range(2 * n)},
        compiler_params=pltpu.CompilerParams(has_side_effects=DATAFLOW),
    )(*[_in_hbm(a) for a in list(shards) + lands], *xa)
    return (out[:3], out[3:3 + n], out[3 + n:3 + 2 * n], own_too), out[-1]


def _gather_wait(handle, after, *, name):
    sems, srcs, lands, own_too = handle
    n = len(srcs)

    def body(*refs):
        srcs_, lands_, (send, recv, own) = refs[:n], refs[n:2 * n], refs[2 * n:2 * n + 3]
        for cp in _gather_copies(srcs_, lands_, send, recv):
            cp.wait_send()
        for cp in _gather_arrivals(lands_, send, recv):
            cp.wait_recv()
        for cp in _own_copies(srcs_, lands_, own) if own_too else []:
            cp.wait()

    both = list(srcs) + list(lands)
    out = pl.pallas_call(
        body, name=name, in_specs=[HBM_SPEC] * (2 * n) + [SEM_SPEC] * 3 + [ANY_SPEC], out_specs=[HBM_SPEC] * (2 * n),
        out_shape=[pltpu.HBM(a.shape, a.dtype) for a in both],
        input_output_aliases={i: i for i in range(2 * n)},
        compiler_params=pltpu.CompilerParams(has_side_effects=DATAFLOW),
    )(*both, *sems, after)
    return out[n:]


def _fill_sibling(lands, *, name):
    n = len(lands)

    def body(*refs):
        ins, outs, send_sems, recv_sems = refs[:n], refs[n:2 * n], refs[2 * n], refs[2 * n + 1]
        cps = _gather_copies(ins, outs, send_sems, recv_sems, sibling=True)
        for cp in cps:
            cp.start()
        for cp in _gather_arrivals(outs, send_sems, recv_sems, sibling=True):
            cp.wait_recv()
        for cp in cps:
            cp.wait_send()

    return pl.pallas_call(
        body, name=name, in_specs=[HBM_SPEC] * n, out_specs=[HBM_SPEC] * n,
        out_shape=[jax.ShapeDtypeStruct(a.shape, a.dtype) for a in lands],
        scratch_shapes=[pltpu.SemaphoreType.DMA((3 * n,)), pltpu.SemaphoreType.DMA((3 * n,))],
        input_output_aliases={i: i for i in range(n)},
    )(*lands)


def _scatter_copies(src, land, send, recv):
    x, y, c = _place()
    return [pltpu.make_async_remote_copy(src_ref=src[i].at[2 * px + py], dst_ref=land[i].at[k], send_sem=send.at[3 * i + k],
                                         recv_sem=recv.at[3 * i + k], device_id=(px, py, c), device_id_type=MESH)
            for i in range(len(src)) for k, (px, py) in enumerate(_other_chips(x, y))]


def _scatter_start(pieces, *, name):
    n = len(pieces)

    def body(*refs):
        src, land, send, recv, token = refs[:n], refs[n:2 * n], refs[2 * n], refs[2 * n + 1], refs[-1]
        for cp in _scatter_copies(src, land, send, recv):
            cp.start()
        token[...] = jnp.zeros_like(token)

    lands = [lax.empty((3,) + p.shape[1:], p.dtype) for p in pieces]
    sems = pltpu.SemaphoreType.DMA((3 * n,))
    out = pl.pallas_call(
        body, name=name, in_specs=[HBM_SPEC] * (2 * n),
        out_specs=[SEM_SPEC, SEM_SPEC] + [HBM_SPEC] * (2 * n) + [VMEM_SPEC],
        out_shape=[sems, sems] + [pltpu.HBM(a.shape, a.dtype) for a in pieces + lands]
        + [jax.ShapeDtypeStruct((SUBLANES, LANES), F32)],
        input_output_aliases={i: 2 + i for i in range(2 * n)},
        compiler_params=pltpu.CompilerParams(has_side_effects=DATAFLOW),
    )(*[_in_hbm(a) for a in pieces + lands])
    return (out[0], out[1], out[2:2 + n], out[2 + n:2 + 2 * n]), out[-1]


def _scatter_wait(handle, after, *, name):
    send_sems, recv_sems, srcs, lands = handle
    n = len(srcs)

    def body(*refs):
        src, land, send, recv = refs[:n], refs[n:2 * n], refs[2 * n], refs[2 * n + 1]
        for cp in _scatter_copies(src, land, send, recv):
            cp.wait_send()
            cp.wait_recv()

    both = list(srcs) + list(lands)
    out = pl.pallas_call(
        body, name=name, in_specs=[HBM_SPEC] * (2 * n) + [SEM_SPEC, SEM_SPEC, ANY_SPEC], out_specs=[HBM_SPEC] * (2 * n),
        out_shape=[pltpu.HBM(a.shape, a.dtype) for a in both],
        input_output_aliases={i: i for i in range(2 * n)},
        compiler_params=pltpu.CompilerParams(has_side_effects=DATAFLOW),
    )(*both, send_sems, recv_sems, after)
    return out[n:]


def _to_bf16(x, *, name, after=None):
    T, Dm = x.shape
    tr = _tile(T, 512, 2 * SUBLANES)

    def body(x_ref, o_ref):
        o_ref[...] = x_ref[...].astype(BF16)

    blk = pl.BlockSpec((tr, Dm), lambda i: (i, 0))
    body, xs, xa = _after(body, 1, after)
    return pl.pallas_call(
        body, name=name, grid=(T // tr,), in_specs=[blk] + xs, out_specs=blk, out_shape=jax.ShapeDtypeStruct((T, Dm), BF16),
        compiler_params=_params(("parallel",)),
    )(x, *xa)


def _chip_sum(pieces, got, chip, *, name):
    _, R, Cc = pieces.shape
    tr = _tile(R, 256, SUBLANES)

    def body(chip_ref, a_ref, g_ref, o_ref):
        o_ref[...] = ((a_ref[...] + g_ref[0].astype(F32)) + g_ref[1].astype(F32)) + g_ref[2].astype(F32)

    return pl.pallas_call(
        body, name=name,
        grid_spec=pltpu.PrefetchScalarGridSpec(
            num_scalar_prefetch=1, grid=(R // tr,),
            in_specs=[pl.BlockSpec((None, tr, Cc), lambda i, ch: (ch[0], i, 0)),
                      pl.BlockSpec((3, tr, Cc), lambda i, ch: (0, i, 0))],
            out_specs=pl.BlockSpec((tr, Cc), lambda i, ch: (i, 0))),
        out_shape=jax.ShapeDtypeStruct((R, Cc), F32),
        compiler_params=_params(("parallel",)),
    )(chip, pieces, got)


PACK_COLS = 1024
SMALL_ROWS = 32


def kernel(x, hgrn_w_in, hgrn_lb_logits, hgrn_gnorm_w, hgrn_w_out, swa_w_q, swa_sinks, swa_w_out, shared_w_kv, rel_bias, ffn_w_in, ffn_conv_w, ffn_conv_b, ffn_w_out, ln_mix_g, ln_mix_b, ln_ffn_g, ln_ffn_b, loss_target, m_hgrn_w_in, m_hgrn_lb_logits, m_hgrn_gnorm_w, m_hgrn_w_out, m_swa_w_q, m_swa_sinks, m_swa_w_out, m_shared_w_kv, m_rel_bias, m_ffn_w_in, m_ffn_conv_w, m_ffn_conv_b, m_ffn_w_out, m_ln_mix_g, m_ln_mix_b, m_ln_ffn_g, m_ln_ffn_b, v_hgrn_w_in, v_hgrn_lb_logits, v_hgrn_gnorm_w, v_hgrn_w_out, v_swa_w_q, v_swa_sinks, v_swa_w_out, v_shared_w_kv, v_rel_bias, v_ffn_w_in, v_ffn_conv_w, v_ffn_conv_b, v_ffn_w_out, v_ln_mix_g, v_ln_mix_b, v_ln_ffn_g, v_ln_ffn_b):
    xi, yi, ci = _place()
    chip = 2 * xi + yi
    Dm = D_MODEL
    FC = 2 * FFN_DIM // N_CHIPS
    Fo = FFN_DIM // N_CHIPS
    Dq = Dm // N_CHIPS
    bf = lambda a: a.astype(BF16)

    small = jnp.concatenate([hgrn_lb_logits.reshape(-1), ffn_conv_w.reshape(-1)])
    n_small = small.shape[0]
    bits = jnp.concatenate(_split3(small))
    bits = jnp.pad(bits, (0, SMALL_ROWS * PACK_COLS - 3 * n_small)).reshape(SMALL_ROWS, PACK_COLS)
    groups = [[bf(hgrn_w_in[0]), bf(hgrn_w_out[0]), bits],
              [bf(swa_w_q[0]), bf(swa_w_out[0]), bf(shared_w_kv), bf(ffn_w_in[0]), bf(ffn_w_out[0])],
              [bf(ffn_w_in[1]), bf(ffn_w_out[1])]]

    def gathered(k, landed):
        lands = _fill_sibling(landed, name=f"gather_w{k}_fill")
        if k > 0:
            return lands
        return [lax.dynamic_update_slice(land, shard[None], (chip,) + (0,) * shard.ndim)
                for land, shard in zip(lands, groups[0])]

    handle0, token0 = _gather_start(groups[0], None, name="gather_w0_start", own_too=False)
    xb = _to_bf16(x[0], name="x_to_bf16", after=token0)
    corner = lambda a: a[:2 * SUBLANES, :LANES]
    casts_done = corner(xb) + sum(corner(a) for a in groups[1] + groups[2])
    w_in, w_hg_out, small_all = gathered(0, _gather_wait(handle0, casts_done, name="gather_w0_wait"))
    handle1, token1 = _gather_start(groups[1], w_in, name="gather_w1_start", own_too=True)
    parts = small_all.reshape(N_CHIPS, -1)[:, :3 * n_small].reshape(N_CHIPS, 3, n_small).astype(F32)
    vals = (parts[:, 0] + parts[:, 1]) + parts[:, 2]
    lb_full = vals[:, :2 * Dq].reshape(N_CHIPS, 2, Dq).transpose(1, 0, 2).reshape(2, Dm)
    cw_full = vals[:, 2 * Dq:].reshape(N_CHIPS, DEPTH, 3, FC).transpose(1, 2, 0, 3).reshape(DEPTH, 3, 2 * FFN_DIM)

    got = {"handle": handle1}

    def more_weights(k, after):
        ws = gathered(k, _gather_wait(got.pop("handle"), after, name=f"gather_w{k}_wait"))
        if k == 1:
            got["handle"], token2 = _gather_start(groups[2], ws[0], name="gather_w2_start", own_too=True)
            w_q, w_o, w_kv, w_fi, w_fo = ws
            got.update(ffn_in={0: w_fi}, ffn_out={0: w_fo.reshape(FFN_DIM, Dm)})
            return {"sw_q": w_q.reshape(Dm, Dm), "sw_out": w_o.reshape(Dm, Dm), "kv": w_kv.reshape(Dm, 2 * KV_DIM),
                    "token": token2, "ffn_in": got["ffn_in"], "ffn_out": got["ffn_out"]}
        w_fi, w_fo = ws
        return {"ffn_in": {**got["ffn_in"], 1: w_fi}, "ffn_out": {**got["ffn_out"], 1: w_fo.reshape(FFN_DIM, Dm)}}

    w = {
        "hg_in": w_in, "hg_out": w_hg_out.reshape(Dm, Dm), "token": token1,
        "lb_logits": lb_full, "gnorm": hgrn_gnorm_w, "sinks": swa_sinks, "rel_bias": rel_bias,
        "conv_w_a": [cw_full[l, :, :FFN_DIM] for l in range(DEPTH)],
        "conv_w_b": [cw_full[l, :, FFN_DIM:] for l in range(DEPTH)],
        "conv_b_a": [ffn_conv_b[l:l + 1, :FFN_DIM] for l in range(DEPTH)],
        "conv_b_b": [ffn_conv_b[l:l + 1, FFN_DIM:] for l in range(DEPTH)],
        "ln_mix_g": [ln_mix_g[l:l + 1] for l in range(DEPTH)], "ln_mix_b": [ln_mix_b[l:l + 1] for l in range(DEPTH)],
        "ln_ffn_g": [ln_ffn_g[l:l + 1] for l in range(DEPTH)], "ln_ffn_b": [ln_ffn_b[l:l + 1] for l in range(DEPTH)],
    }

    sent = {}

    def emit(k, gd):
        rows4 = lambda a: a.reshape(N_CHIPS, -1, a.shape[-1])
        order = {1: ["sw_q", "sw_out", "kv", "ffn_in", "ffn_out"], 2: ["ffn_in", "ffn_out", "hg_out"], 3: ["hg_in"]}[k]
        as_pieces = lambda a, nme: a if nme in ("ffn_in", "hg_in") else rows4(a)
        handle, token = _scatter_start([as_pieces(gd[nme][1], nme) for nme in order], name=f"scatter_g{k}_start")
        sent[k] = (handle, [as_pieces(gd[nme][0], nme) for nme in order])
        return token

    loss_tile, grad_x, g = _local_step(x[0], xb, loss_target[0], w, more_weights, emit)

    wts = dict(hgrn_w_in=hgrn_w_in, hgrn_lb_logits=hgrn_lb_logits, hgrn_gnorm_w=hgrn_gnorm_w, hgrn_w_out=hgrn_w_out,
               swa_w_q=swa_w_q, swa_sinks=swa_sinks, swa_w_out=swa_w_out, shared_w_kv=shared_w_kv, rel_bias=rel_bias,
               ffn_w_in=ffn_w_in, ffn_conv_w=ffn_conv_w, ffn_conv_b=ffn_conv_b, ffn_w_out=ffn_w_out,
               ln_mix_g=ln_mix_g, ln_mix_b=ln_mix_b, ln_ffn_g=ln_ffn_g, ln_ffn_b=ln_ffn_b)
    ms = dict(hgrn_w_in=m_hgrn_w_in, hgrn_lb_logits=m_hgrn_lb_logits, hgrn_gnorm_w=m_hgrn_gnorm_w, hgrn_w_out=m_hgrn_w_out,
              swa_w_q=m_swa_w_q, swa_sinks=m_swa_sinks, swa_w_out=m_swa_w_out, shared_w_kv=m_shared_w_kv, rel_bias=m_rel_bias,
              ffn_w_in=m_ffn_w_in, ffn_conv_w=m_ffn_conv_w, ffn_conv_b=m_ffn_conv_b, ffn_w_out=m_ffn_w_out,
              ln_mix_g=m_ln_mix_g, ln_mix_b=m_ln_mix_b, ln_ffn_g=m_ln_ffn_g, ln_ffn_b=m_ln_ffn_b)
    vs = dict(hgrn_w_in=v_hgrn_w_in, hgrn_lb_logits=v_hgrn_lb_logits, hgrn_gnorm_w=v_hgrn_gnorm_w, hgrn_w_out=v_hgrn_w_out,
              swa_w_q=v_swa_w_q, swa_sinks=v_swa_sinks, swa_w_out=v_swa_w_out, shared_w_kv=v_shared_w_kv, rel_bias=v_rel_bias,
              ffn_w_in=v_ffn_w_in, ffn_conv_w=v_ffn_conv_w, ffn_conv_b=v_ffn_conv_b, ffn_w_out=v_ffn_w_out,
              ln_mix_g=v_ln_mix_g, ln_mix_b=v_ln_mix_b, ln_ffn_g=v_ln_ffn_g, ln_ffn_b=v_ln_ffn_b)
    names = list(wts)
    grads, delta, new_m, new_v = {}, {}, {}, {}

    def update(n, ga, gb, layer=None, prev=None):
        r2 = lambda a: a.reshape(-1, a.shape[-1])
        rows = None if layer is None else (layer * ga.shape[0], ga.shape[0])
        return _adamw(r2(wts[n]), ga, gb, r2(ms[n]), r2(vs[n]), rows=rows, prev=prev,
                      name=f"adamw_{n}" + ("" if layer is None else f"_{layer}"))

    def keep(n, res):
        grads[n], delta[n], new_m[n], new_v[n] = [a.reshape(wts[n].shape) for a in res]

    chip1 = jnp.reshape(chip, (1,)).astype(jnp.int32)
    after, swaps = grad_x, {}
    for k in (1, 2, 3):
        handle, pieces = sent[k]
        lands = _scatter_wait(handle, after, name=f"scatter_g{k}_wait")
        parts = [_chip_sum(p, l, chip1, name=f"scatter_g{k}_sum{i}") for i, (p, l) in enumerate(zip(pieces, lands))]
        swaps[k], after = _swap_start(parts, name=f"scatter_g{k}_swap_start")
    for k in (1, 2, 3):
        parts, sibs = _swap_wait(swaps[k], after, name=f"scatter_g{k}_swap_wait")
        if k == 1:
            for n, ga, gb in zip(["swa_w_q", "swa_w_out", "shared_w_kv"], parts[:3], sibs[:3]):
                keep(n, update(n, ga, gb))
            ffn_in_1 = update("ffn_w_in", parts[3], sibs[3], layer=1)
            ffn_out_1 = update("ffn_w_out", parts[4], sibs[4], layer=1)
            after = ffn_out_1[3]
        elif k == 2:
            keep("ffn_w_in", update("ffn_w_in", parts[0], sibs[0], layer=0, prev=ffn_in_1))
            keep("ffn_w_out", update("ffn_w_out", parts[1], sibs[1], layer=0, prev=ffn_out_1))
            keep("hgrn_w_out", update("hgrn_w_out", parts[2], sibs[2]))
            after = new_v["hgrn_w_out"]
        else:
            keep("hgrn_w_in", update("hgrn_w_in", parts[0], sibs[0]))

    small_keys = ["lb_logits", "gnorm", "sinks", "rel_bias", "conv0", "conv1", "ln_mix0", "ln_mix1", "ln_ffn0", "ln_ffn1"]
    flat2 = lambda a: a.reshape(-1, a.shape[-1])
    sums = _sum8([loss_tile] + [flat2(g[k]) for k in small_keys], name="sum_small")
    loss = sums[0][0, 0]
    sg = {k: v.reshape(g[k].shape) for k, v in zip(small_keys, sums[1:])}
    conv = [sg["conv0"], sg["conv1"]]
    g_cw = jnp.stack([jnp.concatenate([conv[l][0, :3], conv[l][1, :3]], axis=1) for l in range(DEPTH)])
    g_cb = jnp.stack([jnp.concatenate([conv[l][0, 3], conv[l][1, 3]], axis=0) for l in range(DEPTH)])
    ln = lambda nme, r: jnp.stack([sg[nme + "0"][r], sg[nme + "1"][r]])
    small_g = dict(hgrn_lb_logits=lax.dynamic_slice_in_dim(sg["lb_logits"], chip * Dq, Dq, axis=1),
                   hgrn_gnorm_w=sg["gnorm"], swa_sinks=sg["sinks"], rel_bias=sg["rel_bias"],
                   ffn_conv_w=lax.dynamic_slice_in_dim(g_cw, chip * FC, FC, axis=2), ffn_conv_b=g_cb,
                   ln_mix_g=ln("ln_mix", 0), ln_mix_b=ln("ln_mix", 1), ln_ffn_g=ln("ln_ffn", 0), ln_ffn_b=ln("ln_ffn", 1))
    small_names = list(small_g)
    d_, m_, v_ = _adamw_small([flat2(wts[n]) for n in small_names], [flat2(small_g[n]) for n in small_names],
                              [flat2(ms[n]) for n in small_names], [flat2(vs[n]) for n in small_names], name="adamw_small")
    for n, a, b_, c_ in zip(small_names, d_, m_, v_):
        shp = wts[n].shape
        grads[n], delta[n], new_m[n], new_v[n] = small_g[n], a.reshape(shp), b_.reshape(shp), c_.reshape(shp)

    return (loss, grad_x[None], *[grads[n] for n in names], *[delta[n] for n in names],
            *[new_m[n] for n in names], *[new_v[n] for n in names])
```

```python
import math

import numpy as np
import jax
import jax.numpy as jnp
from jax import lax
from jax.experimental import pallas as pl
from jax.experimental.pallas import tpu as pltpu

F32 = jnp.float32
BF16 = jnp.bfloat16
MESH = pl.DeviceIdType.MESH

D_MODEL = 1024
DEPTH = 2
HG_HEADS = 8
HG_DIM = 128
SW_Q_HEADS = 16
SW_KV_HEADS = 4
SW_HEAD_DIM = 64
SW_GROUP = 4
SW_WINDOW = 128
REL_BUCKETS = 32
REL_MAX_DIST = 128
FFN_DIM = 2816
ALPHA = (2.0 * DEPTH) ** 0.25
LN_EPS = 1e-5
RMS_EPS = 1e-6
ADAM_LR = 0.001
ADAM_B1 = 0.9
ADAM_B2 = 0.999
ADAM_EPS = 1e-08
ADAM_WD = 0.01
ADAM_STEP = 10

VMEM_BYTES_V7X = 64 * 1024 * 1024
VMEM_LIMIT = VMEM_BYTES_V7X - 8 * 1024 * 1024
LANES = 128
SUBLANES = 8

HG_C = 64
HG_RB = 512
CONV_R = 128
N_CHIPS = 4
N_DEV = 8

ANY_SPEC = pl.BlockSpec(memory_space=pl.ANY)


def _after(body, n_in, after):
    if after is None:
        return body, [], ()

    def wrapped(*refs):
        return body(*refs[:n_in], *refs[n_in + 1:])

    return wrapped, [ANY_SPEC], (after,)


def _params(sem=None):
    return pltpu.CompilerParams(dimension_semantics=sem, vmem_limit_bytes=VMEM_LIMIT)


def _tile(n, pref, unit=LANES):
    if n <= pref:
        return n
    best = None
    for t in range(unit, pref + 1, unit):
        if n % t == 0:
            best = t
    assert best is not None, (n, pref, unit)
    return best


def _dot(a, b, ca, cb):
    nb = a.ndim - 2
    batch = tuple(range(nb))
    return lax.dot_general(a.astype(BF16), b.astype(BF16), (((nb + ca,), (nb + cb,)), (batch, batch)),
                           preferred_element_type=F32)


@jax.custom_vjp
def mm(a, b):
    return _dot(a, b, 1, 0)


@jax.custom_vjp
def mm_nt(a, b):
    return _dot(a, b, 1, 1)


@jax.custom_vjp
def mm_tn(a, b):
    return _dot(a, b, 0, 0)


mm.defvjp(lambda a, b: (mm(a, b), (a, b)), lambda r, ct: (mm_nt(ct, r[1]), mm_tn(r[0], ct)))
mm_nt.defvjp(lambda a, b: (mm_nt(a, b), (a, b)), lambda r, ct: (mm(ct, r[1]), mm_tn(ct, r[0])))
mm_tn.defvjp(lambda a, b: (mm_tn(a, b), (a, b)), lambda r, ct: (mm_nt(r[1], ct), mm(r[0], ct)))


def _split2(x):
    hi = x.astype(BF16)
    return hi, (x - hi.astype(F32)).astype(BF16)


@jax.custom_vjp
def _scores(qt, kt):
    return _dot(qt, kt, 1, 1)


def _scores_bwd(r, ct):
    (qh, ql), (kh, kl) = _split2(r[0]), _split2(r[1])
    return _dot(ct, kh, 1, 0) + _dot(ct, kl, 1, 0), _dot(ct, qh, 0, 0) + _dot(ct, ql, 0, 0)


_scores.defvjp(lambda a, b: (_scores(a, b), (a, b)), _scores_bwd)


def _split3(x):
    hi = x.astype(BF16)
    r1 = x - hi.astype(F32)
    mid = r1.astype(BF16)
    lo = (r1 - mid.astype(F32)).astype(BF16)
    return hi, mid, lo


def _cumsum_impl(x):
    ax = x.ndim - 2
    n = x.shape[ax]
    row = lax.broadcasted_iota(jnp.int32, x.shape, ax)
    d = 1
    while d < n:
        x = x + jnp.where(row >= d, pltpu.roll(x, d, ax), 0.0)
        d *= 2
    return x


def _cumsum_rev_impl(x):
    ax = x.ndim - 2
    n = x.shape[ax]
    row = lax.broadcasted_iota(jnp.int32, x.shape, ax)
    d = 1
    while d < n:
        x = x + jnp.where(row < n - d, pltpu.roll(x, n - d, ax), 0.0)
        d *= 2
    return x


@jax.custom_vjp
def _cumsum(x):
    return _cumsum_impl(x)


_cumsum.defvjp(lambda x: (_cumsum_impl(x), None), lambda _, ct: (_cumsum_rev_impl(ct),))


def _matmul(a, b, *, mode, name, out_dtype=F32, add=None, add_scale=1.0, tm=512, tn=1408, tk=1408, after=None,
            planes=None, also_bf16=False):
    assert planes in (None, "n")
    P = b.shape[0] if planes else 1
    a2, b2 = a.shape, b.shape[-2:]
    (M, K) = a2 if mode[0] == "n" else a2[::-1]
    (K2, N) = b2 if mode[1] == "n" else b2[::-1]
    assert K == K2, (a.shape, b.shape, mode)
    assert b.ndim == (3 if planes else 2)
    tm, tn, tk = _tile(M, tm), _tile(N, tn), _tile(K, tk)
    nj, nk = N // tn, K // tk
    ca, cb = (1 if mode[0] == "n" else 0), (0 if mode[1] == "n" else 1)
    a_blk, a_idx = ((tk, tm), lambda i, k: (k, i)) if mode[0] == "t" else ((tm, tk), lambda i, k: (i, k))
    b_blk, b_idx = ((tn, tk), lambda k, j: (j, k)) if mode[1] == "t" else ((tk, tn), lambda k, j: (k, j))
    a_spec = pl.BlockSpec(a_blk, lambda i, j, k: a_idx(i, k))
    if planes:
        b_spec = pl.BlockSpec((None,) + b_blk, lambda i, j, k: (j // nj,) + b_idx(k, j % nj))
        o_spec, out_shape = pl.BlockSpec((None, tm, tn), lambda i, j, k: (j // nj, i, j % nj)), (P, M, N)
    else:
        b_spec = pl.BlockSpec(b_blk, lambda i, j, k: b_idx(k, j))
        o_spec, out_shape = pl.BlockSpec((tm, tn), lambda i, j, k: (i, j)), (M, N)
    has_add = add is not None
    assert not (has_add and planes)

    def finish(r, add_ref, o_refs):
        if has_add:
            r = r + add_scale * add_ref[...]
        o_refs[0][...] = r.astype(out_dtype)
        if also_bf16:
            o_refs[1][...] = r.astype(BF16)

    def body(*refs):
        a_ref, b_ref = refs[:2]
        add_ref = refs[2] if has_add else None
        first = 3 if has_add else 2
        o_ref = refs[first:first + (2 if also_bf16 else 1)]
        if nk == 1:
            finish(_dot(a_ref[...], b_ref[...], ca, cb), add_ref, o_ref)
            return
        acc_ref = refs[-1]
        k = pl.program_id(2)

        @pl.when(k == 0)
        def _():
            acc_ref[...] = jnp.zeros_like(acc_ref)

        acc_ref[...] += _dot(a_ref[...], b_ref[...], ca, cb)

        @pl.when(k == nk - 1)
        def _():
            finish(acc_ref[...], add_ref, o_ref)

    in_specs = [a_spec, b_spec] + ([o_spec] if has_add else [])
    args = (a, b) + ((add,) if has_add else ())
    body, xs, xa = _after(body, len(args), after)
    in_specs, args = in_specs + xs, args + xa
    out_shapes = [jax.ShapeDtypeStruct(out_shape, out_dtype)] + ([jax.ShapeDtypeStruct(out_shape, BF16)] if also_bf16 else [])
    out = pl.pallas_call(
        body, name=name, grid=(M // tm, nj * (P if planes == "n" else 1), nk), in_specs=in_specs,
        out_specs=[o_spec] * len(out_shapes), out_shape=out_shapes,
        scratch_shapes=[pltpu.VMEM((tm, tn), F32)] if nk > 1 else [],
        compiler_params=_params(("parallel", "parallel", "arbitrary")),
    )(*args)
    return tuple(out) if also_bf16 else out[0]


def _matmul_planes_nn(a, b, *, name, tm=512, after=None):
    (M, K), (P, K2, N) = a.shape, b.shape
    assert K == K2
    tm = _tile(M, tm, 2 * SUBLANES)

    def body(a_ref, b_ref, o_ref):
        for p in range(P):
            o_ref[p] = _dot(a_ref[...], b_ref[p], 1, 0).astype(BF16)

    body, xs, xa = _after(body, 2, after)
    return pl.pallas_call(
        body, name=name, grid=(M // tm,),
        in_specs=[pl.BlockSpec((tm, K), lambda i: (i, 0)), pl.BlockSpec((P, K, N), lambda i: (0, 0, 0))] + xs,
        out_specs=pl.BlockSpec((P, tm, N), lambda i: (0, i, 0)), out_shape=jax.ShapeDtypeStruct((P, M, N), BF16),
        compiler_params=_params(("parallel",)),
    )(a, b, *xa)


def _matmul_planes_nt(a, b, add, *, add_scale, name, tm=512, after=None):
    (P, M, K), (P2, N, K2) = a.shape, b.shape
    assert P == P2 and K == K2 and add.shape == (M, N)
    tm = _tile(M, tm, SUBLANES)

    def body(a_ref, b_ref, add_ref, o_ref):
        r = add_scale * add_ref[...]
        for p in range(P):
            r = r + _dot(a_ref[p], b_ref[p], 1, 1)
        o_ref[...] = r

    row = pl.BlockSpec((tm, N), lambda i: (i, 0))
    body, xs, xa = _after(body, 3, after)
    return pl.pallas_call(
        body, name=name, grid=(M // tm,),
        in_specs=[pl.BlockSpec((P, tm, K), lambda i: (0, i, 0)), pl.BlockSpec((P, N, K), lambda i: (0, 0, 0)), row] + xs,
        out_specs=row, out_shape=jax.ShapeDtypeStruct((M, N), F32),
        compiler_params=_params(("parallel",)),
    )(a, b, add, *xa)


def _ln(z, g, b):
    mu = jnp.mean(z, axis=-1, keepdims=True)
    zc = z - mu
    var = jnp.mean(zc * zc, axis=-1, keepdims=True)
    return zc * lax.rsqrt(var + LN_EPS) * g + b


def _matmul_ln(a, b, h, g, bias, *, name, tgt=None, tm=512, a_t=False):
    (T, K), (K2, Dm) = (a.shape[::-1] if a_t else a.shape), b.shape
    assert K == K2 and h.shape == (T, Dm)
    tm = _tile(T, tm, SUBLANES)
    last = tgt is not None

    def body(*refs):
        a_ref, b_ref, h_ref, g_ref, bias_ref = refs[:5]
        z = ALPHA * h_ref[...] + _dot(a_ref[...], b_ref[...], 0 if a_t else 1, 0)
        if not last:
            z_ref, y_ref, yb_ref = refs[5:]
            y = _ln(z, g_ref[...], bias_ref[...])
            z_ref[...] = z
            y_ref[...] = y
            yb_ref[...] = y.astype(BF16)
            return
        t_ref, dz_ref, dzb_ref, dgb_ref, l_ref, da_ref = refs[5:]

        @pl.when(pl.program_id(0) == 0)
        def _():
            dgb_ref[...] = jnp.zeros_like(dgb_ref)
            l_ref[...] = jnp.zeros_like(l_ref)

        y, vjp = jax.vjp(_ln, z, g_ref[...], bias_ref[...])
        e = y - t_ref[...]
        dz, dg, db = vjp(e * (1.0 / Dm))
        l_ref[...] += 0.5 * jnp.sum(jnp.mean(e * e, axis=-1, keepdims=True), axis=0, keepdims=True)
        dzb = dz.astype(BF16)
        dz_ref[...] = dz
        dzb_ref[...] = dzb
        dgb_ref[...] += jnp.concatenate([dg, db], axis=0)
        da_ref[...] = _dot(dzb, b_ref[...], 1, 1).astype(BF16)

    row = pl.BlockSpec((tm, Dm), lambda i: (i, 0))
    vec = pl.BlockSpec((1, Dm), lambda i: (0, 0))
    a_spec = pl.BlockSpec((K, tm), lambda i: (0, i)) if a_t else pl.BlockSpec((tm, K), lambda i: (i, 0))
    in_specs = [a_spec, pl.BlockSpec((K, Dm), lambda i: (0, 0)), row, vec, vec]
    f32, b16 = jax.ShapeDtypeStruct((T, Dm), F32), jax.ShapeDtypeStruct((T, Dm), BF16)
    if not last:
        return pl.pallas_call(
            body, name=name, grid=(T // tm,), in_specs=in_specs, out_specs=[row, row, row], out_shape=[f32, f32, b16],
            compiler_params=_params(("parallel",)),
        )(a, b, h, g, bias)
    assert not a_t
    return pl.pallas_call(
        body, name=name, grid=(T // tm,), in_specs=in_specs + [row],
        out_specs=[row, row, pl.BlockSpec((2, Dm), lambda i: (0, 0)), pl.BlockSpec((SUBLANES, LANES), lambda i: (0, 0)), a_spec],
        out_shape=[f32, b16, jax.ShapeDtypeStruct((2, Dm), F32), jax.ShapeDtypeStruct((SUBLANES, LANES), F32),
                   jax.ShapeDtypeStruct((T, K), BF16)],
        compiler_params=_params(("arbitrary",)),
    )(a, b, h, g, bias, tgt)


def _ln_bwd_matmul(dy, z, g, b, w, *, name, out_t=False, tm=512, after=None):
    T, Dm = z.shape
    N = w.shape[0]
    tm = _tile(T, tm, LANES if out_t else SUBLANES)

    def body(dy_ref, z_ref, g_ref, b_ref, w_ref, dz_ref, dzb_ref, dgb_ref, o_ref):
        @pl.when(pl.program_id(0) == 0)
        def _():
            dgb_ref[...] = jnp.zeros_like(dgb_ref)

        _, vjp = jax.vjp(_ln, z_ref[...], g_ref[...], b_ref[...])
        dz, dg, db = vjp(dy_ref[...])
        dzb = dz.astype(BF16)
        dz_ref[...] = dz
        dzb_ref[...] = dzb
        dgb_ref[...] += jnp.concatenate([dg, db], axis=0)
        o_ref[...] = (_dot(w_ref[...], dzb, 1, 1) if out_t else _dot(dzb, w_ref[...], 1, 1)).astype(BF16)

    row = pl.BlockSpec((tm, Dm), lambda i: (i, 0))
    vec = pl.BlockSpec((1, Dm), lambda i: (0, 0))
    o_spec = pl.BlockSpec((N, tm), lambda i: (0, i)) if out_t else pl.BlockSpec((tm, N), lambda i: (i, 0))
    body, xs, xa = _after(body, 5, after)
    return pl.pallas_call(
        body, name=name, grid=(T // tm,), in_specs=[row, row, vec, vec, pl.BlockSpec((N, Dm), lambda i: (0, 0))] + xs,
        out_specs=[row, row, pl.BlockSpec((2, Dm), lambda i: (0, 0)), o_spec],
        out_shape=[jax.ShapeDtypeStruct((T, Dm), F32), jax.ShapeDtypeStruct((T, Dm), BF16),
                   jax.ShapeDtypeStruct((2, Dm), F32), jax.ShapeDtypeStruct((N, T) if out_t else (T, N), BF16)],
        compiler_params=_params(("arbitrary",)),
    )(dy, z, g, b, w, *xa)


def _hg_chunk(qr, fr, ir, gr, l0, l1, gw, st):
    C = qr.shape[-2]
    row = lax.broadcasted_iota(jnp.int32, qr.shape, qr.ndim - 2)
    lb = jax.nn.sigmoid(l0 - l1)
    fg = lb + (1.0 - lb) * jax.nn.sigmoid(fr)
    b = _cumsum(jnp.log(fg))
    q = jax.nn.silu(qr)
    k = 1.0 - fg
    bmid = lax.stop_gradient(jnp.sum(jnp.where(row == C // 2 - 1, b, 0.0), axis=-2, keepdims=True))
    bl = jnp.sum(jnp.where(row == C - 1, b, 0.0), axis=-2, keepdims=True)
    o = mm_nt(q * jnp.exp(b), st)
    sc = _scores(q * jnp.exp(b - bmid), k * jnp.exp(bmid - b))
    ti = lax.broadcasted_iota(jnp.int32, (C, C), 0)
    si = lax.broadcasted_iota(jnp.int32, (C, C), 1)
    sc = jnp.where(si <= ti, sc, 0.0)
    o = o + mm(sc, ir)
    st_new = st * jnp.exp(bl) + mm_tn(ir, k * jnp.exp(bl - b))
    on = o * lax.rsqrt(jnp.mean(o * o, axis=-1, keepdims=True) + RMS_EPS)
    return on * gw * jax.nn.silu(gr), st_new


def _heads(ref, rows):
    return jnp.stack([ref[rows, h * HG_DIM:(h + 1) * HG_DIM].astype(F32) for h in range(HG_HEADS)])


def _unheads(x):
    return jnp.concatenate([x[h] for h in range(HG_HEADS)], axis=-1)


def _hgrn_fwd(pre, lbl, gw, *, name):
    _, T, Dm = pre.shape
    rb = min(HG_RB, T)
    C = min(HG_C, rb)
    ncb = rb // C

    def body(pre_ref, lbl_ref, gw_ref, o_ref, st_ref, s_ref):
        @pl.when(pl.program_id(0) == 0)
        def _():
            s_ref[...] = jnp.zeros_like(s_ref)

        def chunk(ci, carry):
            r0 = pl.multiple_of(ci * C, C)
            rows = pl.ds(r0, C)
            st = s_ref[...]
            st_ref[ci] = st
            out, st_new = _hg_chunk(*[_heads(pre_ref.at[j], rows) for j in range(4)],
                                    _heads(lbl_ref, slice(0, 1)), _heads(lbl_ref, slice(1, 2)), gw_ref[...], st)
            o_ref[rows, :] = _unheads(out).astype(BF16)
            s_ref[...] = st_new
            return carry

        lax.fori_loop(0, ncb, chunk, 0, unroll=True)

    row = pl.BlockSpec((rb, Dm), lambda n: (n, 0))
    return pl.pallas_call(
        body, name=name, grid=(T // rb,),
        in_specs=[pl.BlockSpec((4, rb, Dm), lambda n: (0, n, 0)), pl.BlockSpec((2, Dm), lambda n: (0, 0)),
                  pl.BlockSpec((1, HG_DIM), lambda n: (0, 0))],
        out_specs=[row, pl.BlockSpec((ncb, HG_HEADS, HG_DIM, HG_DIM), lambda n: (n, 0, 0, 0))],
        out_shape=[jax.ShapeDtypeStruct((T, Dm), BF16),
                   jax.ShapeDtypeStruct((T // C, HG_HEADS, HG_DIM, HG_DIM), F32)],
        scratch_shapes=[pltpu.VMEM((HG_HEADS, HG_DIM, HG_DIM), F32)],
        compiler_params=_params(("arbitrary",)),
    )(pre, lbl, gw)


def _hgrn_bwd(pre, lbl, gw, states, dout, *, name, after=None):
    _, T, Dm = pre.shape
    rb = min(HG_RB, T)
    C = min(HG_C, rb)
    ncb = rb // C
    nb = T // rb

    def body(pre_ref, lbl_ref, gw_ref, st_ref, do_ref, dpre_ref, dlbl_ref, dgw_ref, ds_ref):
        @pl.when(pl.program_id(0) == 0)
        def _():
            ds_ref[...] = jnp.zeros_like(ds_ref)
            dlbl_ref[...] = jnp.zeros_like(dlbl_ref)
            dgw_ref[...] = jnp.zeros_like(dgw_ref)

        def chunk(cj, carry):
            ci = ncb - 1 - cj
            r0 = pl.multiple_of(ci * C, C)
            rows = pl.ds(r0, C)
            _, vjp = jax.vjp(_hg_chunk, *[_heads(pre_ref.at[j], rows) for j in range(4)],
                             _heads(lbl_ref, slice(0, 1)), _heads(lbl_ref, slice(1, 2)), gw_ref[...], st_ref[ci])
            *dpre, dl0, dl1, dgw, dst = vjp((_heads(do_ref, rows), ds_ref[...]))
            for j in range(4):
                dpre_ref[j, rows, :] = _unheads(dpre[j]).astype(BF16)
            dlbl_ref[0:1, :] += _unheads(dl0)
            dlbl_ref[1:2, :] += _unheads(dl1)
            dgw_ref[...] += dgw
            ds_ref[...] = dst
            return carry

        lax.fori_loop(0, ncb, chunk, 0, unroll=True)

    row = pl.BlockSpec((rb, Dm), lambda n: (nb - 1 - n, 0))
    lsp = pl.BlockSpec((2, Dm), lambda n: (0, 0))
    gsp = pl.BlockSpec((1, HG_DIM), lambda n: (0, 0))
    pre_spec = pl.BlockSpec((4, rb, Dm), lambda n: (0, nb - 1 - n, 0))
    body, xs, xa = _after(body, 5, after)
    return pl.pallas_call(
        body, name=name, grid=(nb,),
        in_specs=[pre_spec, lsp, gsp, pl.BlockSpec((ncb, HG_HEADS, HG_DIM, HG_DIM), lambda n: (nb - 1 - n, 0, 0, 0)), row] + xs,
        out_specs=[pre_spec, lsp, gsp],
        out_shape=[jax.ShapeDtypeStruct((4, T, Dm), BF16), jax.ShapeDtypeStruct((2, Dm), F32),
                   jax.ShapeDtypeStruct((1, HG_DIM), F32)],
        scratch_shapes=[pltpu.VMEM((HG_HEADS, HG_DIM, HG_DIM), F32)],
        compiler_params=_params(("arbitrary",)),
    )(pre, lbl, gw, states, dout, *xa)


CONV_HALO = 2 * SUBLANES


def _conv_rows(u_ref, scr, w, bias, r0, R):
    cur = u_ref[pl.ds(r0, R), :].astype(F32)
    p0 = pl.multiple_of(jnp.maximum(r0 - CONV_HALO, 0), CONV_HALO)
    scr[0:CONV_HALO, :] = jnp.where(r0 > 0, u_ref[pl.ds(p0, CONV_HALO), :].astype(F32), 0.0)
    scr[CONV_HALO:CONV_HALO + R, :] = cur
    s1 = scr[CONV_HALO - 1:CONV_HALO - 1 + R, :]
    s2 = scr[CONV_HALO - 2:CONV_HALO - 2 + R, :]
    return w[0:1, :] * s2 + w[1:2, :] * s1 + w[2:3, :] * cur + bias, cur, s1, s2


def _halves_spec(T, Fd):
    per = Fd // 2 // LANES
    return pl.BlockSpec((2, None, T, LANES), lambda j: (0, j // per, 0, j % per))


def _conv_gate_fwd(u, wa, wb, ba, bb, *, name):
    T, Fd = u.shape[2], 2 * u.shape[3]
    R = min(CONV_R, T)
    tc = LANES

    def body(u_ref, wa_ref, wb_ref, ba_ref, bb_ref, o_ref, sa, sb):
        wa_, wb_, ba_, bb_ = wa_ref[...], wb_ref[...], ba_ref[...], bb_ref[...]

        def step(ri, carry):
            r0 = pl.multiple_of(ri * R, R)
            ca = _conv_rows(u_ref.at[0], sa, wa_, ba_, r0, R)[0]
            cb = _conv_rows(u_ref.at[1], sb, wb_, bb_, r0, R)[0]
            o_ref[pl.ds(r0, R), :] = (jax.nn.silu(ca) * cb).astype(BF16)
            return carry

        lax.fori_loop(0, T // R, step, 0)

    col = pl.BlockSpec((T, tc), lambda j: (0, j))
    wsp = pl.BlockSpec((3, tc), lambda j: (0, j))
    bsp = pl.BlockSpec((1, tc), lambda j: (0, j))
    both = _halves_spec(T, Fd)
    return pl.pallas_call(
        body, name=name, grid=(Fd // tc,), in_specs=[both, wsp, wsp, bsp, bsp], out_specs=col,
        out_shape=jax.ShapeDtypeStruct((T, Fd), BF16),
        scratch_shapes=[pltpu.VMEM((CONV_HALO + R, tc), F32)] * 2,
        compiler_params=_params(("parallel",)),
    )(u, wa, wb, ba, bb)


def _conv_gate_bwd(u, wa, wb, ba, bb, dact, *, name):
    T, Fd = u.shape[2], 2 * u.shape[3]
    R = min(CONV_R, T)
    nr = T // R
    tc = LANES

    def body(u_ref, wa_ref, wb_ref, ba_ref, bb_ref, da_ref,
             du_ref, dp_ref, sa, sb, sda, sdb):
        wa_, wb_, ba_, bb_ = wa_ref[...], wb_ref[...], ba_ref[...], bb_ref[...]
        sda[R:R + SUBLANES, :] = jnp.zeros((SUBLANES, tc), F32)
        sdb[R:R + SUBLANES, :] = jnp.zeros((SUBLANES, tc), F32)

        def taps(dc, cur, s1, s2):
            return jnp.concatenate([jnp.sum(dc * s2, axis=0, keepdims=True), jnp.sum(dc * s1, axis=0, keepdims=True),
                                    jnp.sum(dc * cur, axis=0, keepdims=True)], axis=0)

        def du_rows(sd, dc, w):
            sd[0:R, :] = dc
            du = w[2:3, :] * dc + w[1:2, :] * sd[1:1 + R, :] + w[0:1, :] * sd[2:2 + R, :]
            sd[R:R + SUBLANES, :] = dc[0:SUBLANES]
            return du

        def step(rj, carry):
            dwa, dwb, dba, dbb = carry
            r0 = pl.multiple_of((nr - 1 - rj) * R, R)
            ca, cura, s1a, s2a = _conv_rows(u_ref.at[0], sa, wa_, ba_, r0, R)
            cb, curb, s1b, s2b = _conv_rows(u_ref.at[1], sb, wb_, bb_, r0, R)
            dact_ = da_ref[pl.ds(r0, R), :].astype(F32)
            sg = jax.nn.sigmoid(ca)
            dca = dact_ * cb * (sg * (1.0 + ca * (1.0 - sg)))
            dcb = dact_ * (ca * sg)
            du_ref[0, pl.ds(r0, R), :] = du_rows(sda, dca, wa_).astype(BF16)
            du_ref[1, pl.ds(r0, R), :] = du_rows(sdb, dcb, wb_).astype(BF16)
            return (dwa + taps(dca, cura, s1a, s2a), dwb + taps(dcb, curb, s1b, s2b),
                    dba + jnp.sum(dca, axis=0, keepdims=True), dbb + jnp.sum(dcb, axis=0, keepdims=True))

        z3 = jnp.zeros((3, tc), F32)
        z1 = jnp.zeros((1, tc), F32)
        dwa, dwb, dba, dbb = lax.fori_loop(0, nr, step, (z3, z3, z1, z1))
        dp_ref[0] = jnp.concatenate([dwa, dba], axis=0)
        dp_ref[1] = jnp.concatenate([dwb, dbb], axis=0)

    col = pl.BlockSpec((T, tc), lambda j: (0, j))
    wsp = pl.BlockSpec((3, tc), lambda j: (0, j))
    bsp = pl.BlockSpec((1, tc), lambda j: (0, j))
    both = _halves_spec(T, Fd)
    return pl.pallas_call(
        body, name=name, grid=(Fd // tc,), in_specs=[both, wsp, wsp, bsp, bsp, col],
        out_specs=[both, pl.BlockSpec((2, 4, tc), lambda j: (0, 0, j))],
        out_shape=[jax.ShapeDtypeStruct(u.shape, BF16), jax.ShapeDtypeStruct((2, 4, Fd), F32)],
        scratch_shapes=[pltpu.VMEM((CONV_HALO + R, tc), F32)] * 2 + [pltpu.VMEM((R + SUBLANES, tc), F32)] * 2,
        compiler_params=_params(("parallel",)),
    )(u, wa, wb, ba, bb, dact)


def _bucket_index():
    t = np.arange(SW_WINDOW)[None, :] + SW_WINDOW
    s = np.arange(2 * SW_WINDOW)[:, None]
    dist = np.maximum(t - s, 0)
    exact = REL_BUCKETS // 2
    d = np.maximum(dist, 1).astype(np.float32)
    log_b = exact + (np.log(d / np.float32(exact)) / np.float32(math.log(REL_MAX_DIST / exact))
                     * np.float32(REL_BUCKETS - exact)).astype(np.int32)
    bucket = np.where(dist < exact, dist, np.minimum(log_b, REL_BUCKETS - 1))
    return bucket.astype(np.int32).reshape(1, -1)


BIAS_COLS = SW_WINDOW * 2 * SW_WINDOW
BIAS_TILE = 4096


def _bias_from_table(table, bucket, *, name):
    def body(t_ref, idx_ref, o_ref):
        onehot = (lax.broadcasted_iota(jnp.int32, (REL_BUCKETS, BIAS_TILE), 0) == idx_ref[...]).astype(BF16)
        acc = jnp.zeros((SW_Q_HEADS, BIAS_TILE), F32)
        for piece in _split3(t_ref[...]):
            acc = acc + lax.dot_general(piece, onehot, (((0,), (0,)), ((), ())), preferred_element_type=F32)
        o_ref[...] = acc

    return pl.pallas_call(
        body, name=name, grid=(BIAS_COLS // BIAS_TILE,),
        in_specs=[pl.BlockSpec((REL_BUCKETS, SW_Q_HEADS), lambda j: (0, 0)), pl.BlockSpec((1, BIAS_TILE), lambda j: (0, j))],
        out_specs=pl.BlockSpec((SW_Q_HEADS, BIAS_TILE), lambda j: (0, j)),
        out_shape=jax.ShapeDtypeStruct((SW_Q_HEADS, BIAS_COLS), F32),
        compiler_params=_params(("parallel",)),
    )(table, bucket)


def _table_grad(dbias, bucket, *, name):
    def body(d_ref, idx_ref, o_ref):
        @pl.when(pl.program_id(0) == 0)
        def _():
            o_ref[...] = jnp.zeros_like(o_ref)

        onehot = (lax.broadcasted_iota(jnp.int32, (REL_BUCKETS, BIAS_TILE), 0) == idx_ref[...]).astype(BF16)
        acc = jnp.zeros((REL_BUCKETS, SW_Q_HEADS), F32)
        for piece in _split3(d_ref[...]):
            acc = acc + lax.dot_general(onehot, piece, (((1,), (1,)), ((), ())), preferred_element_type=F32)
        o_ref[...] += acc

    return pl.pallas_call(
        body, name=name, grid=(BIAS_COLS // BIAS_TILE,),
        in_specs=[pl.BlockSpec((SW_Q_HEADS, BIAS_TILE), lambda j: (0, j)), pl.BlockSpec((1, BIAS_TILE), lambda j: (0, j))],
        out_specs=pl.BlockSpec((REL_BUCKETS, SW_Q_HEADS), lambda j: (0, 0)),
        out_shape=jax.ShapeDtypeStruct((REL_BUCKETS, SW_Q_HEADS), F32),
        compiler_params=_params(("arbitrary",)),
    )(dbias, bucket)


KV_DIM = SW_KV_HEADS * SW_HEAD_DIM
GROUP_ROWS = SW_GROUP * SW_HEAD_DIM
GROUP_LANES = SW_GROUP * SW_WINDOW


def _band_mask(n):
    s = lax.broadcasted_iota(jnp.int32, (2 * SW_WINDOW, GROUP_LANES), 0)
    t = (lax.broadcasted_iota(jnp.int32, (2 * SW_WINDOW, GROUP_LANES), 1) & (SW_WINDOW - 1)) + SW_WINDOW
    dist = t - s
    return (dist >= 0) & (dist < SW_WINDOW) & ((n > 0) | (s >= SW_WINDOW))


def _side_by_side(x_ref, g):
    r0 = g * GROUP_ROWS
    return jnp.concatenate([x_ref[r0 + r * SW_HEAD_DIM:r0 + (r + 1) * SW_HEAD_DIM, :] for r in range(SW_GROUP)], axis=1)


def _group_inputs(bias_ref, sink_ref, g):
    heads = range(g * SW_GROUP, (g + 1) * SW_GROUP)
    bias = jnp.concatenate([bias_ref[h] for h in heads], axis=1)
    sink = jnp.concatenate([jnp.broadcast_to(sink_ref[:, h:h + 1], (1, SW_WINDOW)) for h in heads], axis=1)
    return heads, bias, sink


def _kv_pair(kvp_ref, kvc_ref, g):
    ks = slice(g * SW_HEAD_DIM, (g + 1) * SW_HEAD_DIM)
    vs = slice(KV_DIM + g * SW_HEAD_DIM, KV_DIM + (g + 1) * SW_HEAD_DIM)
    kk = jnp.concatenate([kvp_ref[:, ks], kvc_ref[:, ks]], axis=0)
    vv = jnp.concatenate([kvp_ref[:, vs], kvc_ref[:, vs]], axis=0)
    return kk, vv, ks, vs


def _col_max(x):
    return jnp.max(x, axis=0, keepdims=True)


def _col_sum(x):
    return jnp.sum(x, axis=0, keepdims=True)


def _attn_fwd(qt, kv, bias, sinks, *, name):
    Dm, T = qt.shape
    W = SW_WINDOW

    def body(q_ref, kvc_ref, kvp_ref, bias_ref, sink_ref, o_ref):
        mask = _band_mask(pl.program_id(0))
        G = range(SW_KV_HEADS)
        ins = [_group_inputs(bias_ref, sink_ref, g) for g in G]
        kvs = [_kv_pair(kvp_ref, kvc_ref, g) for g in G]
        q = [_side_by_side(q_ref, g) for g in G]
        lg = [jnp.where(mask, mm(kvs[g][0], q[g]) * (SW_HEAD_DIM ** -0.5) + ins[g][1], -jnp.inf) for g in G]
        m = [jnp.maximum(_col_max(lg[g]), ins[g][2]) for g in G]
        p = [jnp.exp(lg[g] - m[g]) for g in G]
        den = [_col_sum(p[g]) + jnp.exp(ins[g][2] - m[g]) for g in G]
        o = [mm_tn(kvs[g][1], p[g]) / den[g] for g in G]
        for g in G:
            for r in range(SW_GROUP):
                o_ref[g * GROUP_ROWS + r * SW_HEAD_DIM:g * GROUP_ROWS + (r + 1) * SW_HEAD_DIM, :] = (
                    o[g][:, r * W:(r + 1) * W].astype(BF16))

    return pl.pallas_call(
        body, name=name, grid=(T // W,),
        in_specs=[pl.BlockSpec((Dm, W), lambda n: (0, n)),
                  pl.BlockSpec((W, 2 * KV_DIM), lambda n: (n, 0)),
                  pl.BlockSpec((W, 2 * KV_DIM), lambda n: (jnp.maximum(n - 1, 0), 0)),
                  pl.BlockSpec((SW_Q_HEADS, 2 * W, W), lambda n: (0, 0, 0)),
                  pl.BlockSpec((1, SW_Q_HEADS), lambda n: (0, 0))],
        out_specs=pl.BlockSpec((Dm, W), lambda n: (0, n)),
        out_shape=jax.ShapeDtypeStruct((Dm, T), BF16),
        compiler_params=_params(("parallel",)),
    )(qt, kv, kv, bias, sinks)


def _attn_bwd(qt, kv, bias, sinks, dot, *, name):
    Dm, T = qt.shape
    W = SW_WINDOW
    nb = T // W

    def body(q_ref, kvc_ref, kvp_ref, bias_ref, sink_ref, do_ref,
             dq_ref, dkv_ref, dbias_ref, dsink_ref, carry_ref):
        @pl.when(pl.program_id(0) == 0)
        def _():
            carry_ref[...] = jnp.zeros_like(carry_ref)
            dbias_ref[...] = jnp.zeros_like(dbias_ref)
            dsink_ref[...] = jnp.zeros_like(dsink_ref)

        n = nb - 1 - pl.program_id(0)
        mask = _band_mask(n)
        lane = lax.broadcasted_iota(jnp.int32, (1, SW_Q_HEADS), 1)
        sc = SW_HEAD_DIM ** -0.5
        G = range(SW_KV_HEADS)
        ins = [_group_inputs(bias_ref, sink_ref, g) for g in G]
        kvs = [_kv_pair(kvp_ref, kvc_ref, g) for g in G]
        q = [_side_by_side(q_ref, g) for g in G]
        do = [_side_by_side(do_ref, g) for g in G]
        lg = [jnp.where(mask, mm(kvs[g][0], q[g]) * sc + ins[g][1], -jnp.inf) for g in G]
        m = [jnp.maximum(_col_max(lg[g]), ins[g][2]) for g in G]
        p = [jnp.exp(lg[g] - m[g]) for g in G]
        ps = [jnp.exp(ins[g][2] - m[g]) for g in G]
        rden = [1.0 / (_col_sum(p[g]) + ps[g]) for g in G]
        pn = [p[g] * rden[g] for g in G]
        dpn = [mm(kvs[g][1], do[g]) for g in G]
        delta = [_col_sum(pn[g] * dpn[g]) for g in G]
        ds = [pn[g] * (dpn[g] - delta[g]) for g in G]
        dsr = [-(ps[g] * rden[g]) * delta[g] for g in G]
        dq = [mm_tn(kvs[g][0], ds[g]) * sc for g in G]
        dkk = [mm_nt(ds[g], q[g]) * sc for g in G]
        dvv = [mm_nt(pn[g], do[g]) for g in G]
        dsink = jnp.zeros((1, SW_Q_HEADS), F32)
        for g in G:
            _, _, ks, vs = kvs[g]
            for r, h in enumerate(ins[g][0]):
                cols = slice(r * W, (r + 1) * W)
                dbias_ref[h] += ds[g][:, cols]
                dq_ref[g * GROUP_ROWS + r * SW_HEAD_DIM:g * GROUP_ROWS + (r + 1) * SW_HEAD_DIM, :] = dq[g][:, cols].astype(BF16)
                dsink = dsink + jnp.where(lane == h, jnp.sum(dsr[g][:, cols], axis=1, keepdims=True), 0.0)
            dkv_ref[:, ks] = (carry_ref[:, ks] + dkk[g][W:]).astype(BF16)
            dkv_ref[:, vs] = (carry_ref[:, vs] + dvv[g][W:]).astype(BF16)
            carry_ref[:, ks] = dkk[g][:W]
            carry_ref[:, vs] = dvv[g][:W]
        dsink_ref[...] += dsink

    rev = lambda n: (nb - 1 - n, 0)
    revt = lambda n: (0, nb - 1 - n)
    return pl.pallas_call(
        body, name=name, grid=(nb,),
        in_specs=[pl.BlockSpec((Dm, W), revt),
                  pl.BlockSpec((W, 2 * KV_DIM), rev),
                  pl.BlockSpec((W, 2 * KV_DIM), lambda n: (jnp.maximum(nb - 2 - n, 0), 0)),
                  pl.BlockSpec((SW_Q_HEADS, 2 * W, W), lambda n: (0, 0, 0)),
                  pl.BlockSpec((1, SW_Q_HEADS), lambda n: (0, 0)),
                  pl.BlockSpec((Dm, W), revt)],
        out_specs=[pl.BlockSpec((Dm, W), revt), pl.BlockSpec((W, 2 * KV_DIM), rev),
                   pl.BlockSpec((SW_Q_HEADS, 2 * W, W), lambda n: (0, 0, 0)),
                   pl.BlockSpec((1, SW_Q_HEADS), lambda n: (0, 0))],
        out_shape=[jax.ShapeDtypeStruct((Dm, T), BF16), jax.ShapeDtypeStruct((T, 2 * KV_DIM), BF16),
                   jax.ShapeDtypeStruct((SW_Q_HEADS, 2 * W, W), F32), jax.ShapeDtypeStruct((1, SW_Q_HEADS), F32)],
        scratch_shapes=[pltpu.VMEM((W, 2 * KV_DIM), F32)],
        compiler_params=_params(("arbitrary",)),
    )(qt, kv, kv, bias, sinks, dot)


def _ffn_fwd(hb, w, l, after=None):
    u = _matmul_planes_nn(hb, w["ffn_in"][l], name=f"ffn{l}_up", after=after)
    u = u.reshape((2, 2) + u.shape[1:])
    act = _conv_gate_fwd(u, w["conv_w_a"][l], w["conv_w_b"][l], w["conv_b_a"][l], w["conv_b_b"][l],
                         name=f"ffn{l}_conv_gate")
    return u, act


def _ffn_bwd(dffb, dh_scaled, hb, u, act, w, l, dact):
    g_out = _matmul(act, dffb, mode="tn", name=f"ffn{l}_down_dw", tm=1408, tn=1024, tk=2048, also_bf16=True)
    du, g_conv = _conv_gate_bwd(u, w["conv_w_a"][l], w["conv_w_b"][l], w["conv_b_a"][l], w["conv_b_b"][l],
                                dact, name=f"ffn{l}_conv_gate_bwd")
    du = du.reshape((N_CHIPS,) + du.shape[2:])
    dh = _matmul_planes_nt(du, w["ffn_in"][l], dh_scaled, add_scale=ALPHA, name=f"ffn{l}_up_dx")
    g_in = _matmul(hb, du, mode="tn", planes="n", name=f"ffn{l}_up_dw", tm=1024, tn=FFN_DIM // 2, tk=2048, also_bf16=True)
    return dh, dict(ffn_out=g_out, ffn_in=g_in, conv=g_conv)


def _local_step(x, xb, tgt, w, more_weights, emit):
    bucket = jnp.asarray(_bucket_index())

    pre = _matmul_planes_nn(xb, w["hg_in"], name="hg_in", tm=1024, after=w.get("token"))
    og, states = _hgrn_fwd(pre, w["lb_logits"], w["gnorm"], name="hgrn_fwd")
    z1, h1, h1b = _matmul_ln(og, w["hg_out"], x, w["ln_mix_g"][0], w["ln_mix_b"][0], name="hg_out_ln")
    w = {**w, **more_weights(1, h1b)}
    u0, act0 = _ffn_fwd(h1b, w, 0, after=w.get("token"))
    z2, h2, h2b = _matmul_ln(act0, w["ffn_out"][0], h1, w["ln_ffn_g"][0], w["ln_ffn_b"][0], name="ffn0_down_ln")
    kv = _matmul(h2b, w["kv"], mode="nn", out_dtype=BF16, name="kv_proj")

    bias = _bias_from_table(w["rel_bias"], bucket, name="rel_bias_expand").reshape(SW_Q_HEADS, 2 * SW_WINDOW, SW_WINDOW)
    q1 = _matmul(w["sw_q"], h2b, mode="tt", out_dtype=BF16, name="sw_q", tm=1024, tn=1024)
    o1 = _attn_fwd(q1, kv, bias, w["sinks"], name="attn_fwd")
    z3, h3, h3b = _matmul_ln(o1, w["sw_out"], h2, w["ln_mix_g"][1], w["ln_mix_b"][1], a_t=True, name="sw_out_ln")
    w = {**w, **more_weights(2, h3b)}
    u1, act1 = _ffn_fwd(h3b, w, 1)

    g = {}
    dz, dzb, g["ln_ffn1"], loss_tile, dact1 = _matmul_ln(act1, w["ffn_out"][1], h3, w["ln_ffn_g"][1], w["ln_ffn_b"][1],
                                                         tgt=tgt, name="ffn1_down_ln_loss")

    dh3, gf1 = _ffn_bwd(dzb, dz, h3b, u1, act1, w, 1, dact1)
    dz, dzb, g["ln_mix1"], do1 = _ln_bwd_matmul(dh3, z3, w["ln_mix_g"][1], w["ln_mix_b"][1], w["sw_out"], out_t=True,
                                                name="ln_mix1_bwd_sw_out_dx")
    g_sw_out = _matmul(o1, dzb, mode="nn", name="sw_out_dw", tm=1024, tn=1024, tk=2048, also_bf16=True)
    dq1, dkv, dbias, dsinks = _attn_bwd(q1, kv, bias, w["sinks"], do1, name="attn_bwd")
    g["sinks"] = dsinks
    g["rel_bias"] = _table_grad(dbias.reshape(SW_Q_HEADS, BIAS_COLS), bucket, name="rel_bias_grad")
    dh2 = _matmul(dq1, w["sw_q"], mode="tt", add=dz, add_scale=ALPHA, name="sw_q_dx", tn=1024)
    dh2 = _matmul(dkv, w["kv"], mode="nt", add=dh2, name="kv_dx", tn=1024)
    g_sw_q = _matmul(h2b, dq1, mode="tt", name="sw_q_dw", tm=1024, tn=1024, tk=2048, also_bf16=True)
    g_kv = _matmul(h2b, dkv, mode="tn", name="kv_dw", tm=1024, tn=512, tk=2048, also_bf16=True)
    tok = emit(1, dict(sw_q=g_sw_q, sw_out=g_sw_out, kv=g_kv, ffn_in=gf1["ffn_in"], ffn_out=gf1["ffn_out"]))

    dz, dzb, g["ln_ffn0"], dact0 = _ln_bwd_matmul(dh2, z2, w["ln_ffn_g"][0], w["ln_ffn_b"][0], w["ffn_out"][0],
                                                  name="ln_ffn0_bwd_down_dx", after=tok)
    dh1, gf0 = _ffn_bwd(dzb, dz, h1b, u0, act0, w, 0, dact0)
    dz, dzb, g["ln_mix0"], dog = _ln_bwd_matmul(dh1, z1, w["ln_mix_g"][0], w["ln_mix_b"][0], w["hg_out"],
                                                name="ln_mix0_bwd_hg_out_dx")
    g_hg_out = _matmul(og, dzb, mode="tn", name="hg_out_dw", tm=1024, tn=1024, tk=2048, also_bf16=True)
    tok = emit(2, dict(hg_out=g_hg_out, ffn_in=gf0["ffn_in"], ffn_out=gf0["ffn_out"]))
    dpre, g["lb_logits"], g["gnorm"] = _hgrn_bwd(pre, w["lb_logits"], w["gnorm"], states, dog, name="hgrn_bwd", after=tok)
    tok = emit(3, dict(hg_in=_matmul(xb, dpre, mode="tn", planes="n", name="hg_in_dw", tm=1024, tn=1024, tk=2048, also_bf16=True)))
    dx = _matmul_planes_nt(dpre, w["hg_in"], dz, add_scale=ALPHA, name="hg_in_dx", after=tok)
    g["conv0"], g["conv1"] = gf0["conv"], gf1["conv"]
    return loss_tile, dx, g


def _adamw(wt, ga, gb, m, v, *, name, rows=None, prev=None):
    R, Cc = wt.shape
    r0, n = rows if rows is not None else (0, R)
    tr = _tile(n, 256, SUBLANES) if n % SUBLANES == 0 else n
    assert r0 % tr == 0
    c1 = 1.0 - ADAM_B1 ** ADAM_STEP
    c2 = 1.0 - ADAM_B2 ** ADAM_STEP
    n_in = 5

    def body(*refs):
        w_ref, ga_ref, gb_ref, m_ref, v_ref = refs[:n_in]
        g_ = ga_ref[...] + gb_ref[...]
        g_ref, d_ref, nm_ref, nv_ref = refs[-4:]
        nm = ADAM_B1 * m_ref[...] + (1.0 - ADAM_B1) * g_
        nv = ADAM_B2 * v_ref[...] + (1.0 - ADAM_B2) * (g_ * g_)
        g_ref[...] = g_
        d_ref[...] = -ADAM_LR * ((nm / c1) / (jnp.sqrt(nv / c2) + ADAM_EPS) + ADAM_WD * w_ref[...])
        nm_ref[...] = nm
        nv_ref[...] = nv

    full = pl.BlockSpec((tr, Cc), lambda i: (i + r0 // tr, 0))
    part = pl.BlockSpec((tr, Cc), lambda i: (i, 0))
    args = (wt, ga, gb, m, v)
    in_specs = [full, part, part, full, full]
    aliases = {}
    if prev is not None:
        args, in_specs = args + tuple(prev), in_specs + [ANY_SPEC] * 4
        aliases = {n_in + t: t for t in range(4)}
    return pl.pallas_call(
        body, name=name, grid=(n // tr,), in_specs=in_specs, out_specs=[full] * 4,
        out_shape=[jax.ShapeDtypeStruct((R, Cc), F32)] * 4, input_output_aliases=aliases,
        compiler_params=_params(("parallel",)),
    )(*args)


def _adamw_small(ws, gs, ms, vs, *, name):
    n = len(ws)
    c1 = 1.0 - ADAM_B1 ** ADAM_STEP
    c2 = 1.0 - ADAM_B2 ** ADAM_STEP

    def body(*refs):
        w_refs, g_refs, m_refs, v_refs = (refs[k * n:(k + 1) * n] for k in range(4))
        d_refs, nm_refs, nv_refs = (refs[(4 + k) * n:(5 + k) * n] for k in range(3))
        for i in range(n):
            g_ = g_refs[i][...]
            nm = ADAM_B1 * m_refs[i][...] + (1.0 - ADAM_B1) * g_
            nv = ADAM_B2 * v_refs[i][...] + (1.0 - ADAM_B2) * (g_ * g_)
            d_refs[i][...] = -ADAM_LR * ((nm / c1) / (jnp.sqrt(nv / c2) + ADAM_EPS) + ADAM_WD * w_refs[i][...])
            nm_refs[i][...] = nm
            nv_refs[i][...] = nv

    vm = pl.BlockSpec(memory_space=pltpu.VMEM)
    out = pl.pallas_call(
        body, name=name, in_specs=[vm] * (4 * n), out_specs=[vm] * (3 * n),
        out_shape=[jax.ShapeDtypeStruct(w.shape, F32) for w in ws] * 3,
    )(*ws, *gs, *ms, *vs)
    return out[:n], out[n:2 * n], out[2 * n:]


HBM_SPEC = pl.BlockSpec(memory_space=pltpu.HBM)
SEM_SPEC = pl.BlockSpec(memory_space=pltpu.SEMAPHORE)
VMEM_SPEC = pl.BlockSpec(memory_space=pltpu.VMEM)
DATAFLOW = pltpu.SideEffectType.DATAFLOW_SIDE_EFFECTING


def _in_hbm(a):
    return pltpu.with_memory_space_constraint(a, pltpu.HBM)


def _place():
    return lax.axis_index("x"), lax.axis_index("y"), lax.axis_index("c")


def _other_chips(x, y):
    return [(1 - x, y), (x, 1 - y), (1 - x, 1 - y)]


def _sum8(vs, *, name):
    n = len(vs)

    def body(*refs):
        v_refs, all_refs, o_refs = refs[:n], refs[n:2 * n], refs[2 * n:3 * n]
        send_sems, recv_sems, local_sems = refs[3 * n:]
        x, y, c = _place()
        me, sibling = (x, y, c), (x, y, 1 - c)
        chips = _other_chips(x, y)

        def slot(i, px, py, pc):
            return all_refs[i].at[4 * px + 2 * py + pc]

        def copy(i, k, block, to, src=None):
            return pltpu.make_async_remote_copy(
                src_ref=slot(i, *block) if src is None else src, dst_ref=slot(i, *block),
                send_sem=send_sems.at[7 * i + k], recv_sem=recv_sems.at[7 * i + k], device_id=to, device_id_type=MESH)

        mine = [pltpu.make_async_copy(v_refs[i], slot(i, *me), local_sems.at[i]) for i in range(n)]
        for cp in mine:
            cp.start()
        first = [copy(i, 0, me, sibling, src=v_refs[i]) for i in range(n)]
        first += [copy(i, 1 + j, me, (*chip, c), src=v_refs[i]) for i in range(n) for j, chip in enumerate(chips)]
        for cp in first:
            cp.start()
        passed = []
        for i in range(n):
            for j, chip in enumerate(chips):
                copy(i, 1 + j, (*chip, c), me).wait_recv()
                passed.append(copy(i, 4 + j, (*chip, c), sibling))
                passed[-1].start()
        for i in range(n):
            copy(i, 0, sibling, me).wait_recv()
            for j, chip in enumerate(chips):
                copy(i, 4 + j, (*chip, 1 - c), me).wait_recv()
        for cp in first + passed:
            cp.wait_send()
        for cp in mine:
            cp.wait()
        for i in range(n):
            acc = all_refs[i][0]
            for d in range(1, N_DEV):
                acc = acc + all_refs[i][d]
            o_refs[i][...] = acc

    return pl.pallas_call(
        body, name=name, in_specs=[VMEM_SPEC] * n, out_specs=[VMEM_SPEC] * (2 * n),
        out_shape=[jax.ShapeDtypeStruct((N_DEV,) + v.shape, F32) for v in vs] + [jax.ShapeDtypeStruct(v.shape, F32) for v in vs],
        scratch_shapes=[pltpu.SemaphoreType.DMA((7 * n,)), pltpu.SemaphoreType.DMA((7 * n,)), pltpu.SemaphoreType.DMA((n,))],
        compiler_params=pltpu.CompilerParams(vmem_limit_bytes=VMEM_LIMIT),
    )(*vs)[n:]


def _swap_copies(src, land, send, recv):
    x, y, c = _place()
    return [pltpu.make_async_remote_copy(src_ref=src[i], dst_ref=land[i], send_sem=send.at[i], recv_sem=recv.at[i],
                                         device_id=(x, y, 1 - c), device_id_type=MESH) for i in range(len(src))]


def _swap_start(vs, *, name):
    n = len(vs)

    def body(*refs):
        src, land, send, recv, token = refs[:n], refs[n:2 * n], refs[2 * n], refs[2 * n + 1], refs[-1]
        for cp in _swap_copies(src, land, send, recv):
            cp.start()
        token[...] = jnp.zeros_like(token)

    lands = [lax.empty(v.shape, v.dtype) for v in vs]
    sems = pltpu.SemaphoreType.DMA((n,))
    out = pl.pallas_call(
        body, name=name, in_specs=[HBM_SPEC] * (2 * n),
        out_specs=[SEM_SPEC, SEM_SPEC] + [HBM_SPEC] * (2 * n) + [VMEM_SPEC],
        out_shape=[sems, sems] + [pltpu.HBM(a.shape, a.dtype) for a in list(vs) + lands]
        + [jax.ShapeDtypeStruct((SUBLANES, LANES), F32)],
        input_output_aliases={i: 2 + i for i in range(2 * n)},
        compiler_params=pltpu.CompilerParams(has_side_effects=DATAFLOW),
    )(*[_in_hbm(a) for a in list(vs) + lands])
    return (out[0], out[1], out[2:2 + n], out[2 + n:2 + 2 * n]), out[-1]


def _swap_wait(handle, after, *, name):
    send_sems, recv_sems, srcs, lands = handle
    n = len(srcs)

    def body(*refs):
        src, land, send, recv = refs[:n], refs[n:2 * n], refs[2 * n], refs[2 * n + 1]
        for cp in _swap_copies(src, land, send, recv):
            cp.wait_send()
            cp.wait_recv()

    both = list(srcs) + list(lands)
    out = pl.pallas_call(
        body, name=name, in_specs=[HBM_SPEC] * (2 * n) + [SEM_SPEC, SEM_SPEC, ANY_SPEC], out_specs=[HBM_SPEC] * (2 * n),
        out_shape=[pltpu.HBM(a.shape, a.dtype) for a in both],
        input_output_aliases={i: i for i in range(2 * n)},
        compiler_params=pltpu.CompilerParams(has_side_effects=DATAFLOW),
    )(*both, send_sems, recv_sems, after)
    return out[:n], out[n:]


def _gather_copies(srcs, lands, send, recv, sibling=False):
    x, y, c = _place()
    out = []
    for i, (src, land) in enumerate(zip(srcs, lands)):
        half = land.shape[1] // 2
        rows = pl.ds(c * half, half)
        for k, (px, py) in enumerate(_other_chips(x, y)):
            if sibling:
                src_ref, dst_ref, to = src.at[2 * px + py, rows], land.at[2 * px + py, rows], (x, y, 1 - c)
            else:
                src_ref, dst_ref, to = src.at[rows], land.at[2 * x + y, rows], (px, py, c)
            out.append(pltpu.make_async_remote_copy(src_ref=src_ref, dst_ref=dst_ref, send_sem=send.at[3 * i + k],
                                                    recv_sem=recv.at[3 * i + k], device_id=to, device_id_type=MESH))
    return out


def _gather_arrivals(lands, send, recv, sibling=False):
    x, y, c = _place()
    out = []
    for i, land in enumerate(lands):
        half = land.shape[1] // 2
        rows = pl.ds(((1 - c) if sibling else c) * half, half)
        for k, (px, py) in enumerate(_other_chips(x, y)):
            part = land.at[2 * px + py, rows]
            out.append(pltpu.make_async_remote_copy(src_ref=part, dst_ref=part, send_sem=send.at[3 * i + k],
                                                    recv_sem=recv.at[3 * i + k],
                                                    device_id=(x, y, 1 - c) if sibling else (px, py, c), device_id_type=MESH))
    return out


def _own_copies(srcs, lands, sems):
    x, y, _ = _place()
    return [pltpu.make_async_copy(src, land.at[2 * x + y], sems.at[i]) for i, (src, land) in enumerate(zip(srcs, lands))]


def _gather_start(shards, after, *, name, own_too):
    n = len(shards)

    def body(*refs):
        srcs, lands, (send, recv, own), token = refs[:n], refs[n:2 * n], refs[2 * n:2 * n + 3], refs[-1]
        for cp in _gather_copies(srcs, lands, send, recv) + (_own_copies(srcs, lands, own) if own_too else []):
            cp.start()
        token[...] = jnp.zeros_like(token)

    lands = [lax.empty((N_CHIPS,) + s.shape, s.dtype) for s in shards]
    sems = pltpu.SemaphoreType.DMA((3 * n,))
    body, xs, xa = _after(body, 2 * n, after)
    out = pl.pallas_call(
        body, name=name, in_specs=[HBM_SPEC] * (2 * n) + xs,
        out_specs=[SEM_SPEC] * 3 + [HBM_SPEC] * (2 * n) + [VMEM_SPEC],
        out_shape=[sems, sems, pltpu.SemaphoreType.DMA((n,))] + [pltpu.HBM(a.shape, a.dtype) for a in list(shards) + lands]
        + [jax.ShapeDtypeStruct((SUBLANES, LANES), F32)],
        input_output_aliases={i: 3 + i for i in range(2 * n)},
        compiler_params=pltpu.CompilerParams(has_side_effects=DATAFLOW),
    )(*[_in_hbm(a) for a in list(shards) + lands], *xa)
    return (out[:3], out[3:3 + n], out[3 + n:3 + 2 * n], own_too), out[-1]


def _gather_wait(handle, after, *, name):
    sems, srcs, lands, own_too = handle
    n = len(srcs)

    def body(*refs):
        srcs_, lands_, (send, recv, own) = refs[:n], refs[n:2 * n], refs[2 * n:2 * n + 3]
        for cp in _gather_copies(srcs_, lands_, send, recv):
            cp.wait_send()
        for cp in _gather_arrivals(lands_, send, recv):
            cp.wait_recv()
        for cp in _own_copies(srcs_, lands_, own) if own_too else []:
            cp.wait()

    both = list(srcs) + list(lands)
    out = pl.pallas_call(
        body, name=name, in_specs=[HBM_SPEC] * (2 * n) + [SEM_SPEC] * 3 + [ANY_SPEC], out_specs=[HBM_SPEC] * (2 * n),
        out_shape=[pltpu.HBM(a.shape, a.dtype) for a in both],
        input_output_aliases={i: i for i in range(2 * n)},
        compiler_params=pltpu.CompilerParams(has_side_effects=DATAFLOW),
    )(*both, *sems, after)
    return out[n:]


def _fill_sibling(lands, *, name):
    n = len(lands)

    def body(*refs):
        ins, outs, send_sems, recv_sems = refs[:n], refs[n:2 * n], refs[2 * n], refs[2 * n + 1]
        cps = _gather_copies(ins, outs, send_sems, recv_sems, sibling=True)
        for cp in cps:
            cp.start()
        for cp in _gather_arrivals(outs, send_sems, recv_sems, sibling=True):
            cp.wait_recv()
        for cp in cps:
            cp.wait_send()

    return pl.pallas_call(
        body, name=name, in_specs=[HBM_SPEC] * n, out_specs=[HBM_SPEC] * n,
        out_shape=[jax.ShapeDtypeStruct(a.shape, a.dtype) for a in lands],
        scratch_shapes=[pltpu.SemaphoreType.DMA((3 * n,)), pltpu.SemaphoreType.DMA((3 * n,))],
        input_output_aliases={i: i for i in range(n)},
    )(*lands)


def _scatter_copies(src, land, send, recv):
    x, y, c = _place()
    return [pltpu.make_async_remote_copy(src_ref=src[i].at[2 * px + py], dst_ref=land[i].at[k], send_sem=send.at[3 * i + k],
                                         recv_sem=recv.at[3 * i + k], device_id=(px, py, c), device_id_type=MESH)
            for i in range(len(src)) for k, (px, py) in enumerate(_other_chips(x, y))]


def _scatter_start(pieces, *, name):
    n = len(pieces)

    def body(*refs):
        src, land, send, recv, token = refs[:n], refs[n:2 * n], refs[2 * n], refs[2 * n + 1], refs[-1]
        for cp in _scatter_copies(src, land, send, recv):
            cp.start()
        token[...] = jnp.zeros_like(token)

    lands = [lax.empty((3,) + p.shape[1:], p.dtype) for p in pieces]
    sems = pltpu.SemaphoreType.DMA((3 * n,))
    out = pl.pallas_call(
        body, name=name, in_specs=[HBM_SPEC] * (2 * n),
        out_specs=[SEM_SPEC, SEM_SPEC] + [HBM_SPEC] * (2 * n) + [VMEM_SPEC],
        out_shape=[sems, sems] + [pltpu.HBM(a.shape, a.dtype) for a in pieces + lands]
        + [jax.ShapeDtypeStruct((SUBLANES, LANES), F32)],
        input_output_aliases={i: 2 + i for i in range(2 * n)},
        compiler_params=pltpu.CompilerParams(has_side_effects=DATAFLOW),
    )(*[_in_hbm(a) for a in pieces + lands])
    return (out[0], out[1], out[2:2 + n], out[2 + n:2 + 2 * n]), out[-1]


def _scatter_wait(handle, after, *, name):
    send_sems, recv_sems, srcs, lands = handle
    n = len(srcs)

    def body(*refs):
        src, land, send, recv = refs[:n], refs[n:2 * n], refs[2 * n], refs[2 * n + 1]
        for cp in _scatter_copies(src, land, send, recv):
            cp.wait_send()
            cp.wait_recv()

    both = list(srcs) + list(lands)
    out = pl.pallas_call(
        body, name=name, in_specs=[HBM_SPEC] * (2 * n) + [SEM_SPEC, SEM_SPEC, ANY_SPEC], out_specs=[HBM_SPEC] * (2 * n),
        out_shape=[pltpu.HBM(a.shape, a.dtype) for a in both],
        input_output_aliases={i: i for i in range(2 * n)},
        compiler_params=pltpu.CompilerParams(has_side_effects=DATAFLOW),
    )(*both, send_sems, recv_sems, after)
    return out[n:]


def _to_bf16(x, *, name, after=None):
    T, Dm = x.shape
    tr = _tile(T, 512, 2 * SUBLANES)

    def body(x_ref, o_ref):
        o_ref[...] = x_ref[...].astype(BF16)

    blk = pl.BlockSpec((tr, Dm), lambda i: (i, 0))
    body, xs, xa = _after(body, 1, after)
    return pl.pallas_call(
        body, name=name, grid=(T // tr,), in_specs=[blk] + xs, out_specs=blk, out_shape=jax.ShapeDtypeStruct((T, Dm), BF16),
        compiler_params=_params(("parallel",)),
    )(x, *xa)


def _chip_sum(pieces, got, chip, *, name):
    _, R, Cc = pieces.shape
    tr = _tile(R, 256, SUBLANES)

    def body(chip_ref, a_ref, g_ref, o_ref):
        o_ref[...] = ((a_ref[...] + g_ref[0].astype(F32)) + g_ref[1].astype(F32)) + g_ref[2].astype(F32)

    return pl.pallas_call(
        body, name=name,
        grid_spec=pltpu.PrefetchScalarGridSpec(
            num_scalar_prefetch=1, grid=(R // tr,),
            in_specs=[pl.BlockSpec((None, tr, Cc), lambda i, ch: (ch[0], i, 0)),
                      pl.BlockSpec((3, tr, Cc), lambda i, ch: (0, i, 0))],
            out_specs=pl.BlockSpec((tr, Cc), lambda i, ch: (i, 0))),
        out_shape=jax.ShapeDtypeStruct((R, Cc), F32),
        compiler_params=_params(("parallel",)),
    )(chip, pieces, got)


PACK_COLS = 1024
SMALL_ROWS = 32


def kernel(x, hgrn_w_in, hgrn_lb_logits, hgrn_gnorm_w, hgrn_w_out, swa_w_q, swa_sinks, swa_w_out, shared_w_kv, rel_bias, ffn_w_in, ffn_conv_w, ffn_conv_b, ffn_w_out, ln_mix_g, ln_mix_b, ln_ffn_g, ln_ffn_b, loss_target, m_hgrn_w_in, m_hgrn_lb_logits, m_hgrn_gnorm_w, m_hgrn_w_out, m_swa_w_q, m_swa_sinks, m_swa_w_out, m_shared_w_kv, m_rel_bias, m_ffn_w_in, m_ffn_conv_w, m_ffn_conv_b, m_ffn_w_out, m_ln_mix_g, m_ln_mix_b, m_ln_ffn_g, m_ln_ffn_b, v_hgrn_w_in, v_hgrn_lb_logits, v_hgrn_gnorm_w, v_hgrn_w_out, v_swa_w_q, v_swa_sinks, v_swa_w_out, v_shared_w_kv, v_rel_bias, v_ffn_w_in, v_ffn_conv_w, v_ffn_conv_b, v_ffn_w_out, v_ln_mix_g, v_ln_mix_b, v_ln_ffn_g, v_ln_ffn_b):
    xi, yi, ci = _place()
    chip = 2 * xi + yi
    Dm = D_MODEL
    FC = 2 * FFN_DIM // N_CHIPS
    Fo = FFN_DIM // N_CHIPS
    Dq = Dm // N_CHIPS
    bf = lambda a: a.astype(BF16)

    small = jnp.concatenate([hgrn_lb_logits.reshape(-1), ffn_conv_w.reshape(-1)])
    n_small = small.shape[0]
    bits = jnp.concatenate(_split3(small))
    bits = jnp.pad(bits, (0, SMALL_ROWS * PACK_COLS - 3 * n_small)).reshape(SMALL_ROWS, PACK_COLS)
    groups = [[bf(hgrn_w_in[0]), bf(hgrn_w_out[0]), bits],
              [bf(swa_w_q[0]), bf(swa_w_out[0]), bf(shared_w_kv), bf(ffn_w_in[0]), bf(ffn_w_out[0])],
              [bf(ffn_w_in[1]), bf(ffn_w_out[1])]]

    def gathered(k, landed):
        lands = _fill_sibling(landed, name=f"gather_w{k}_fill")
        if k > 0:
            return lands
        return [lax.dynamic_update_slice(land, shard[None], (chip,) + (0,) * shard.ndim)
                for land, shard in zip(lands, groups[0])]

    handle0, token0 = _gather_start(groups[0], None, name="gather_w0_start", own_too=False)
    xb = _to_bf16(x[0], name="x_to_bf16", after=token0)
    corner = lambda a: a[:2 * SUBLANES, :LANES]
    casts_done = corner(xb) + sum(corner(a) for a in groups[1] + groups[2])
    w_in, w_hg_out, small_all = gathered(0, _gather_wait(handle0, casts_done, name="gather_w0_wait"))
    handle1, token1 = _gather_start(groups[1], w_in, name="gather_w1_start", own_too=True)
    parts = small_all.reshape(N_CHIPS, -1)[:, :3 * n_small].reshape(N_CHIPS, 3, n_small).astype(F32)
    vals = (parts[:, 0] + parts[:, 1]) + parts[:, 2]
    lb_full = vals[:, :2 * Dq].reshape(N_CHIPS, 2, Dq).transpose(1, 0, 2).reshape(2, Dm)
    cw_full = vals[:, 2 * Dq:].reshape(N_CHIPS, DEPTH, 3, FC).transpose(1, 2, 0, 3).reshape(DEPTH, 3, 2 * FFN_DIM)

    got = {"handle": handle1}

    def more_weights(k, after):
        ws = gathered(k, _gather_wait(got.pop("handle"), after, name=f"gather_w{k}_wait"))
        if k == 1:
            got["handle"], token2 = _gather_start(groups[2], ws[0], name="gather_w2_start", own_too=True)
            w_q, w_o, w_kv, w_fi, w_fo = ws
            got.update(ffn_in={0: w_fi}, ffn_out={0: w_fo.reshape(FFN_DIM, Dm)})
            return {"sw_q": w_q.reshape(Dm, Dm), "sw_out": w_o.reshape(Dm, Dm), "kv": w_kv.reshape(Dm, 2 * KV_DIM),
                    "token": token2, "ffn_in": got["ffn_in"], "ffn_out": got["ffn_out"]}
        w_fi, w_fo = ws
        return {"ffn_in": {**got["ffn_in"], 1: w_fi}, "ffn_out": {**got["ffn_out"], 1: w_fo.reshape(FFN_DIM, Dm)}}

    w = {
        "hg_in": w_in, "hg_out": w_hg_out.reshape(Dm, Dm), "token": token1,
        "lb_logits": lb_full, "gnorm": hgrn_gnorm_w, "sinks": swa_sinks, "rel_bias": rel_bias,
        "conv_w_a": [cw_full[l, :, :FFN_DIM] for l in range(DEPTH)],
        "conv_w_b": [cw_full[l, :, FFN_DIM:] for l in range(DEPTH)],
        "conv_b_a": [ffn_conv_b[l:l + 1, :FFN_DIM] for l in range(DEPTH)],
        "conv_b_b": [ffn_conv_b[l:l + 1, FFN_DIM:] for l in range(DEPTH)],
        "ln_mix_g": [ln_mix_g[l:l + 1] for l in range(DEPTH)], "ln_mix_b": [ln_mix_b[l:l + 1] for l in range(DEPTH)],
        "ln_ffn_g": [ln_ffn_g[l:l + 1] for l in range(DEPTH)], "ln_ffn_b": [ln_ffn_b[l:l + 1] for l in range(DEPTH)],
    }

    sent = {}

    def emit(k, gd):
        rows4 = lambda a: a.reshape(N_CHIPS, -1, a.shape[-1])
        order = {1: ["sw_q", "sw_out", "kv", "ffn_in", "ffn_out"], 2: ["ffn_in", "ffn_out", "hg_out"], 3: ["hg_in"]}[k]
        as_pieces = lambda a, nme: a if nme in ("ffn_in", "hg_in") else rows4(a)
        handle, token = _scatter_start([as_pieces(gd[nme][1], nme) for nme in order], name=f"scatter_g{k}_start")
        sent[k] = (handle, [as_pieces(gd[nme][0], nme) for nme in order])
        return token

    loss_tile, grad_x, g = _local_step(x[0], xb, loss_target[0], w, more_weights, emit)

    wts = dict(hgrn_w_in=hgrn_w_in, hgrn_lb_logits=hgrn_lb_logits, hgrn_gnorm_w=hgrn_gnorm_w, hgrn_w_out=hgrn_w_out,
               swa_w_q=swa_w_q, swa_sinks=swa_sinks, swa_w_out=swa_w_out, shared_w_kv=shared_w_kv, rel_bias=rel_bias,
               ffn_w_in=ffn_w_in, ffn_conv_w=ffn_conv_w, ffn_conv_b=ffn_conv_b, ffn_w_out=ffn_w_out,
               ln_mix_g=ln_mix_g, ln_mix_b=ln_mix_b, ln_ffn_g=ln_ffn_g, ln_ffn_b=ln_ffn_b)
    ms = dict(hgrn_w_in=m_hgrn_w_in, hgrn_lb_logits=m_hgrn_lb_logits, hgrn_gnorm_w=m_hgrn_gnorm_w, hgrn_w_out=m_hgrn_w_out,
              swa_w_q=m_swa_w_q, swa_sinks=m_swa_sinks, swa_w_out=m_swa_w_out, shared_w_kv=m_shared_w_kv, rel_bias=m_rel_bias,
              ffn_w_in=m_ffn_w_in, ffn_conv_w=m_ffn_conv_w, ffn_conv_b=m_ffn_conv_b, ffn_w_out=m_ffn_w_out,
              ln_mix_g=m_ln_mix_g, ln_mix_b=m_ln_mix_b, ln_ffn_g=m_ln_ffn_g, ln_ffn_b=m_ln_ffn_b)
    vs = dict(hgrn_w_in=v_hgrn_w_in, hgrn_lb_logits=v_hgrn_lb_logits, hgrn_gnorm_w=v_hgrn_gnorm_w, hgrn_w_out=v_hgrn_w_out,
              swa_w_q=v_swa_w_q, swa_sinks=v_swa_sinks, swa_w_out=v_swa_w_out, shared_w_kv=v_shared_w_kv, rel_bias=v_rel_bias,
              ffn_w_in=v_ffn_w_in, ffn_conv_w=v_ffn_conv_w, ffn_conv_b=v_ffn_conv_b, ffn_w_out=v_ffn_w_out,
              ln_mix_g=v_ln_mix_g, ln_mix_b=v_ln_mix_b, ln_ffn_g=v_ln_ffn_g, ln_ffn_b=v_ln_ffn_b)
    names = list(wts)
    grads, delta, new_m, new_v = {}, {}, {}, {}

    def update(n, ga, gb, layer=None, prev=None):
        r2 = lambda a: a.reshape(-1, a.shape[-1])
        rows = None if layer is None else (layer * ga.shape[0], ga.shape[0])
        return _adamw(r2(wts[n]), ga, gb, r2(ms[n]), r2(vs[n]), rows=rows, prev=prev,
                      name=f"adamw_{n}" + ("" if layer is None else f"_{layer}"))

    def keep(n, res):
        grads[n], delta[n], new_m[n], new_v[n] = [a.reshape(wts[n].shape) for a in res]

    chip1 = jnp.reshape(chip, (1,)).astype(jnp.int32)
    after, swaps = grad_x, {}
    for k in (1, 2, 3):
        handle, pieces = sent[k]
        lands = _scatter_wait(handle, after, name=f"scatter_g{k}_wait")
        parts = [_chip_sum(p, l, chip1, name=f"scatter_g{k}_sum{i}") for i, (p, l) in enumerate(zip(pieces, lands))]
        swaps[k], after = _swap_start(parts, name=f"scatter_g{k}_swap_start")
    for k in (1, 2, 3):
        parts, sibs = _swap_wait(swaps[k], after, name=f"scatter_g{k}_swap_wait")
        if k == 1:
            for n, ga, gb in zip(["swa_w_q", "swa_w_out", "shared_w_kv"], parts[:3], sibs[:3]):
                keep(n, update(n, ga, gb))
            ffn_in_1 = update("ffn_w_in", parts[3], sibs[3], layer=1)
            ffn_out_1 = update("ffn_w_out", parts[4], sibs[4], layer=1)
            after = ffn_out_1[3]
        elif k == 2:
            keep("ffn_w_in", update("ffn_w_in", parts[0], sibs[0], layer=0, prev=ffn_in_1))
            keep("ffn_w_out", update("ffn_w_out", parts[1], sibs[1], layer=0, prev=ffn_out_1))
            keep("hgrn_w_out", update("hgrn_w_out", parts[2], sibs[2]))
            after = new_v["hgrn_w_out"]
        else:
            keep("hgrn_w_in", update("hgrn_w_in", parts[0], sibs[0]))

    small_keys = ["lb_logits", "gnorm", "sinks", "rel_bias", "conv0", "conv1", "ln_mix0", "ln_mix1", "ln_ffn0", "ln_ffn1"]
    flat2 = lambda a: a.reshape(-1, a.shape[-1])
    sums = _sum8([loss_tile] + [flat2(g[k]) for k in small_keys], name="sum_small")
    loss = sums[0][0, 0]
    sg = {k: v.reshape(g[k].shape) for k, v in zip(small_keys, sums[1:])}
    conv = [sg["conv0"], sg["conv1"]]
    g_cw = jnp.stack([jnp.concatenate([conv[l][0, :3], conv[l][1, :3]], axis=1) for l in range(DEPTH)])
    g_cb = jnp.stack([jnp.concatenate([conv[l][0, 3], conv[l][1, 3]], axis=0) for l in range(DEPTH)])
    ln = lambda nme, r: jnp.stack([sg[nme + "0"][r], sg[nme + "1"][r]])
    small_g = dict(hgrn_lb_logits=lax.dynamic_slice_in_dim(sg["lb_logits"], chip * Dq, Dq, axis=1),
                   hgrn_gnorm_w=sg["gnorm"], swa_sinks=sg["sinks"], rel_bias=sg["rel_bias"],
                   ffn_conv_w=lax.dynamic_slice_in_dim(g_cw, chip * FC, FC, axis=2), ffn_conv_b=g_cb,
                   ln_mix_g=ln("ln_mix", 0), ln_mix_b=ln("ln_mix", 1), ln_ffn_g=ln("ln_ffn", 0), ln_ffn_b=ln("ln_ffn", 1))
    small_names = list(small_g)
    d_, m_, v_ = _adamw_small([flat2(wts[n]) for n in small_names], [flat2(small_g[n]) for n in small_names],
                              [flat2(ms[n]) for n in small_names], [flat2(vs[n]) for n in small_names], name="adamw_small")
    for n, a, b_, c_ in zip(small_names, d_, m_, v_):
        shp = wts[n].shape
        grads[n], delta[n], new_m[n], new_v[n] = small_g[n], a.reshape(shp), b_.reshape(shp), c_.reshape(shp)

    return (loss, grad_x[None], *[grads[n] for n in names], *[delta[n] for n in names],
            *[new_m[n] for n in names], *[new_v[n] for n in names])
```

```python
import math

import numpy as np
import jax
import jax.numpy as jnp
from jax import lax
from jax.experimental import pallas as pl
from jax.experimental.pallas import tpu as pltpu

F32 = jnp.float32
BF16 = jnp.bfloat16
MESH = pl.DeviceIdType.MESH

D_MODEL = 1024
DEPTH = 2
HG_HEADS = 8
HG_DIM = 128
SW_Q_HEADS = 16
SW_KV_HEADS = 4
SW_HEAD_DIM = 64
SW_GROUP = 4
SW_WINDOW = 128
REL_BUCKETS = 32
REL_MAX_DIST = 128
FFN_DIM = 2816
ALPHA = (2.0 * DEPTH) ** 0.25
LN_EPS = 1e-5
RMS_EPS = 1e-6
ADAM_LR = 0.001
ADAM_B1 = 0.9
ADAM_B2 = 0.999
ADAM_EPS = 1e-08
ADAM_WD = 0.01
ADAM_STEP = 10

VMEM_BYTES_V7X = 64 * 1024 * 1024
VMEM_LIMIT = VMEM_BYTES_V7X - 8 * 1024 * 1024
LANES = 128
SUBLANES = 8

HG_C = 64
HG_RB = 256
CONV_R = 128
N_CHIPS = 4
N_DEV = 8

ANY_SPEC = pl.BlockSpec(memory_space=pl.ANY)


def _after(body, n_in, after):
    if after is None:
        return body, [], ()

    def wrapped(*refs):
        return body(*refs[:n_in], *refs[n_in + 1:])

    return wrapped, [ANY_SPEC], (after,)


def _params(sem=None):
    return pltpu.CompilerParams(dimension_semantics=sem, vmem_limit_bytes=VMEM_LIMIT)


def _tile(n, pref, unit=LANES):
    if n <= pref:
        return n
    best = None
    for t in range(unit, pref + 1, unit):
        if n % t == 0:
            best = t
    assert best is not None, (n, pref, unit)
    return best


def _dot(a, b, ca, cb):
    nb = a.ndim - 2
    batch = tuple(range(nb))
    return lax.dot_general(a.astype(BF16), b.astype(BF16), (((nb + ca,), (nb + cb,)), (batch, batch)),
                           preferred_element_type=F32)


@jax.custom_vjp
def mm(a, b):
    return _dot(a, b, 1, 0)


@jax.custom_vjp
def mm_nt(a, b):
    return _dot(a, b, 1, 1)


@jax.custom_vjp
def mm_tn(a, b):
    return _dot(a, b, 0, 0)


mm.defvjp(lambda a, b: (mm(a, b), (a, b)), lambda r, ct: (mm_nt(ct, r[1]), mm_tn(r[0], ct)))
mm_nt.defvjp(lambda a, b: (mm_nt(a, b), (a, b)), lambda r, ct: (mm(ct, r[1]), mm_tn(ct, r[0])))
mm_tn.defvjp(lambda a, b: (mm_tn(a, b), (a, b)), lambda r, ct: (mm_nt(r[1], ct), mm(r[0], ct)))


def _split2(x):
    hi = x.astype(BF16)
    return hi, (x - hi.astype(F32)).astype(BF16)


@jax.custom_vjp
def _scores(qt, kt):
    return _dot(qt, kt, 1, 1)


def _scores_bwd(r, ct):
    (qh, ql), (kh, kl) = _split2(r[0]), _split2(r[1])
    return _dot(ct, kh, 1, 0) + _dot(ct, kl, 1, 0), _dot(ct, qh, 0, 0) + _dot(ct, ql, 0, 0)


_scores.defvjp(lambda a, b: (_scores(a, b), (a, b)), _scores_bwd)


def _split3(x):
    hi = x.astype(BF16)
    r1 = x - hi.astype(F32)
    mid = r1.astype(BF16)
    lo = (r1 - mid.astype(F32)).astype(BF16)
    return hi, mid, lo


def _cumsum_impl(x):
    ax = x.ndim - 2
    n = x.shape[ax]
    row = lax.broadcasted_iota(jnp.int32, x.shape, ax)
    d = 1
    while d < n:
        x = x + jnp.where(row >= d, pltpu.roll(x, d, ax), 0.0)
        d *= 2
    return x


def _cumsum_rev_impl(x):
    ax = x.ndim - 2
    n = x.shape[ax]
    row = lax.broadcasted_iota(jnp.int32, x.shape, ax)
    d = 1
    while d < n:
        x = x + jnp.where(row < n - d, pltpu.roll(x, n - d, ax), 0.0)
        d *= 2
    return x


@jax.custom_vjp
def _cumsum(x):
    return _cumsum_impl(x)


_cumsum.defvjp(lambda x: (_cumsum_impl(x), None), lambda _, ct: (_cumsum_rev_impl(ct),))


def _matmul(a, b, *, mode, name, out_dtype=F32, add=None, add_scale=1.0, tm=512, tn=1408, tk=1408, after=None,
            planes=None, also_bf16=False):
    assert planes in (None, "n")
    P = b.shape[0] if planes else 1
    a2, b2 = a.shape, b.shape[-2:]
    (M, K) = a2 if mode[0] == "n" else a2[::-1]
    (K2, N) = b2 if mode[1] == "n" else b2[::-1]
    assert K == K2, (a.shape, b.shape, mode)
    assert b.ndim == (3 if planes else 2)
    tm, tn, tk = _tile(M, tm), _tile(N, tn), _tile(K, tk)
    nj, nk = N // tn, K // tk
    ca, cb = (1 if mode[0] == "n" else 0), (0 if mode[1] == "n" else 1)
    a_blk, a_idx = ((tk, tm), lambda i, k: (k, i)) if mode[0] == "t" else ((tm, tk), lambda i, k: (i, k))
    b_blk, b_idx = ((tn, tk), lambda k, j: (j, k)) if mode[1] == "t" else ((tk, tn), lambda k, j: (k, j))
    a_spec = pl.BlockSpec(a_blk, lambda i, j, k: a_idx(i, k))
    if planes:
        b_spec = pl.BlockSpec((None,) + b_blk, lambda i, j, k: (j // nj,) + b_idx(k, j % nj))
        o_spec, out_shape = pl.BlockSpec((None, tm, tn), lambda i, j, k: (j // nj, i, j % nj)), (P, M, N)
    else:
        b_spec = pl.BlockSpec(b_blk, lambda i, j, k: b_idx(k, j))
        o_spec, out_shape = pl.BlockSpec((tm, tn), lambda i, j, k: (i, j)), (M, N)
    has_add = add is not None
    assert not (has_add and planes)

    def finish(r, add_ref, o_refs):
        if has_add:
            r = r + add_scale * add_ref[...]
        o_refs[0][...] = r.astype(out_dtype)
        if also_bf16:
            o_refs[1][...] = r.astype(BF16)

    def body(*refs):
        a_ref, b_ref = refs[:2]
        add_ref = refs[2] if has_add else None
        first = 3 if has_add else 2
        o_ref = refs[first:first + (2 if also_bf16 else 1)]
        if nk == 1:
            finish(_dot(a_ref[...], b_ref[...], ca, cb), add_ref, o_ref)
            return
        acc_ref = refs[-1]
        k = pl.program_id(2)

        @pl.when(k == 0)
        def _():
            acc_ref[...] = jnp.zeros_like(acc_ref)

        acc_ref[...] += _dot(a_ref[...], b_ref[...], ca, cb)

        @pl.when(k == nk - 1)
        def _():
            finish(acc_ref[...], add_ref, o_ref)

    in_specs = [a_spec, b_spec] + ([o_spec] if has_add else [])
    args = (a, b) + ((add,) if has_add else ())
    body, xs, xa = _after(body, len(args), after)
    in_specs, args = in_specs + xs, args + xa
    out_shapes = [jax.ShapeDtypeStruct(out_shape, out_dtype)] + ([jax.ShapeDtypeStruct(out_shape, BF16)] if also_bf16 else [])
    out = pl.pallas_call(
        body, name=name, grid=(M // tm, nj * (P if planes == "n" else 1), nk), in_specs=in_specs,
        out_specs=[o_spec] * len(out_shapes), out_shape=out_shapes,
        scratch_shapes=[pltpu.VMEM((tm, tn), F32)] if nk > 1 else [],
        compiler_params=_params(("parallel", "parallel", "arbitrary")),
    )(*args)
    return tuple(out) if also_bf16 else out[0]


def _matmul_planes_nn(a, b, *, name, tm=512, after=None):
    (M, K), (P, K2, N) = a.shape, b.shape
    assert K == K2
    tm = _tile(M, tm, 2 * SUBLANES)

    def body(a_ref, b_ref, o_ref):
        for p in range(P):
            o_ref[p] = _dot(a_ref[...], b_ref[p], 1, 0).astype(BF16)

    body, xs, xa = _after(body, 2, after)
    return pl.pallas_call(
        body, name=name, grid=(M // tm,),
        in_specs=[pl.BlockSpec((tm, K), lambda i: (i, 0)), pl.BlockSpec((P, K, N), lambda i: (0, 0, 0))] + xs,
        out_specs=pl.BlockSpec((P, tm, N), lambda i: (0, i, 0)), out_shape=jax.ShapeDtypeStruct((P, M, N), BF16),
        compiler_params=_params(("parallel",)),
    )(a, b, *xa)


def _matmul_planes_nt(a, b, add, *, add_scale, name, tm=512, after=None):
    (P, M, K), (P2, N, K2) = a.shape, b.shape
    assert P == P2 and K == K2 and add.shape == (M, N)
    tm = _tile(M, tm, SUBLANES)

    def body(a_ref, b_ref, add_ref, o_ref):
        r = add_scale * add_ref[...]
        for p in range(P):
            r = r + _dot(a_ref[p], b_ref[p], 1, 1)
        o_ref[...] = r

    row = pl.BlockSpec((tm, N), lambda i: (i, 0))
    body, xs, xa = _after(body, 3, after)
    return pl.pallas_call(
        body, name=name, grid=(M // tm,),
        in_specs=[pl.BlockSpec((P, tm, K), lambda i: (0, i, 0)), pl.BlockSpec((P, N, K), lambda i: (0, 0, 0)), row] + xs,
        out_specs=row, out_shape=jax.ShapeDtypeStruct((M, N), F32),
        compiler_params=_params(("parallel",)),
    )(a, b, add, *xa)


def _ln(z, g, b):
    mu = jnp.mean(z, axis=-1, keepdims=True)
    zc = z - mu
    var = jnp.mean(zc * zc, axis=-1, keepdims=True)
    return zc * lax.rsqrt(var + LN_EPS) * g + b


def _matmul_ln(a, b, h, g, bias, *, name, tgt=None, tm=512, a_t=False):
    (T, K), (K2, Dm) = (a.shape[::-1] if a_t else a.shape), b.shape
    assert K == K2 and h.shape == (T, Dm)
    tm = _tile(T, tm, SUBLANES)
    last = tgt is not None

    def body(*refs):
        a_ref, b_ref, h_ref, g_ref, bias_ref = refs[:5]
        z = ALPHA * h_ref[...] + _dot(a_ref[...], b_ref[...], 0 if a_t else 1, 0)
        if not last:
            z_ref, y_ref, yb_ref = refs[5:]
            y = _ln(z, g_ref[...], bias_ref[...])
            z_ref[...] = z
            y_ref[...] = y
            yb_ref[...] = y.astype(BF16)
            return
        t_ref, dz_ref, dzb_ref, dgb_ref, l_ref, da_ref = refs[5:]

        @pl.when(pl.program_id(0) == 0)
        def _():
            dgb_ref[...] = jnp.zeros_like(dgb_ref)
            l_ref[...] = jnp.zeros_like(l_ref)

        y, vjp = jax.vjp(_ln, z, g_ref[...], bias_ref[...])
        e = y - t_ref[...]
        dz, dg, db = vjp(e * (1.0 / Dm))
        l_ref[...] += 0.5 * jnp.sum(jnp.mean(e * e, axis=-1, keepdims=True), axis=0, keepdims=True)
        dzb = dz.astype(BF16)
        dz_ref[...] = dz
        dzb_ref[...] = dzb
        dgb_ref[...] += jnp.concatenate([dg, db], axis=0)
        da_ref[...] = _dot(dzb, b_ref[...], 1, 1).astype(BF16)

    row = pl.BlockSpec((tm, Dm), lambda i: (i, 0))
    vec = pl.BlockSpec((1, Dm), lambda i: (0, 0))
    a_spec = pl.BlockSpec((K, tm), lambda i: (0, i)) if a_t else pl.BlockSpec((tm, K), lambda i: (i, 0))
    in_specs = [a_spec, pl.BlockSpec((K, Dm), lambda i: (0, 0)), row, vec, vec]
    f32, b16 = jax.ShapeDtypeStruct((T, Dm), F32), jax.ShapeDtypeStruct((T, Dm), BF16)
    if not last:
        return pl.pallas_call(
            body, name=name, grid=(T // tm,), in_specs=in_specs, out_specs=[row, row, row], out_shape=[f32, f32, b16],
            compiler_params=_params(("parallel",)),
        )(a, b, h, g, bias)
    assert not a_t
    return pl.pallas_call(
        body, name=name, grid=(T // tm,), in_specs=in_specs + [row],
        out_specs=[row, row, pl.BlockSpec((2, Dm), lambda i: (0, 0)), pl.BlockSpec((SUBLANES, LANES), lambda i: (0, 0)), a_spec],
        out_shape=[f32, b16, jax.ShapeDtypeStruct((2, Dm), F32), jax.ShapeDtypeStruct((SUBLANES, LANES), F32),
                   jax.ShapeDtypeStruct((T, K), BF16)],
        compiler_params=_params(("arbitrary",)),
    )(a, b, h, g, bias, tgt)


def _ln_bwd_matmul(dy, z, g, b, w, *, name, out_t=False, tm=512, after=None):
    T, Dm = z.shape
    N = w.shape[0]
    tm = _tile(T, tm, LANES if out_t else SUBLANES)

    def body(dy_ref, z_ref, g_ref, b_ref, w_ref, dz_ref, dzb_ref, dgb_ref, o_ref):
        @pl.when(pl.program_id(0) == 0)
        def _():
            dgb_ref[...] = jnp.zeros_like(dgb_ref)

        _, vjp = jax.vjp(_ln, z_ref[...], g_ref[...], b_ref[...])
        dz, dg, db = vjp(dy_ref[...])
        dzb = dz.astype(BF16)
        dz_ref[...] = dz
        dzb_ref[...] = dzb
        dgb_ref[...] += jnp.concatenate([dg, db], axis=0)
        o_ref[...] = (_dot(w_ref[...], dzb, 1, 1) if out_t else _dot(dzb, w_ref[...], 1, 1)).astype(BF16)

    row = pl.BlockSpec((tm, Dm), lambda i: (i, 0))
    vec = pl.BlockSpec((1, Dm), lambda i: (0, 0))
    o_spec = pl.BlockSpec((N, tm), lambda i: (0, i)) if out_t else pl.BlockSpec((tm, N), lambda i: (i, 0))
    body, xs, xa = _after(body, 5, after)
    return pl.pallas_call(
        body, name=name, grid=(T // tm,), in_specs=[row, row, vec, vec, pl.BlockSpec((N, Dm), lambda i: (0, 0))] + xs,
        out_specs=[row, row, pl.BlockSpec((2, Dm), lambda i: (0, 0)), o_spec],
        out_shape=[jax.ShapeDtypeStruct((T, Dm), F32), jax.ShapeDtypeStruct((T, Dm), BF16),
                   jax.ShapeDtypeStruct((2, Dm), F32), jax.ShapeDtypeStruct((N, T) if out_t else (T, N), BF16)],
        compiler_params=_params(("arbitrary",)),
    )(dy, z, g, b, w, *xa)


def _hg_chunk(qr, fr, ir, gr, l0, l1, gw, st):
    C = qr.shape[-2]
    row = lax.broadcasted_iota(jnp.int32, qr.shape, qr.ndim - 2)
    lb = jax.nn.sigmoid(l0 - l1)
    fg = lb + (1.0 - lb) * jax.nn.sigmoid(fr)
    b = _cumsum(jnp.log(fg))
    q = jax.nn.silu(qr)
    k = 1.0 - fg
    bmid = lax.stop_gradient(jnp.sum(jnp.where(row == C // 2 - 1, b, 0.0), axis=-2, keepdims=True))
    bl = jnp.sum(jnp.where(row == C - 1, b, 0.0), axis=-2, keepdims=True)
    o = mm_nt(q * jnp.exp(b), st)
    sc = _scores(q * jnp.exp(b - bmid), k * jnp.exp(bmid - b))
    ti = lax.broadcasted_iota(jnp.int32, (C, C), 0)
    si = lax.broadcasted_iota(jnp.int32, (C, C), 1)
    sc = jnp.where(si <= ti, sc, 0.0)
    o = o + mm(sc, ir)
    st_new = st * jnp.exp(bl) + mm_tn(ir, k * jnp.exp(bl - b))
    on = o * lax.rsqrt(jnp.mean(o * o, axis=-1, keepdims=True) + RMS_EPS)
    return on * gw * jax.nn.silu(gr), st_new


def _heads(ref, rows):
    return jnp.stack([ref[rows, h * HG_DIM:(h + 1) * HG_DIM].astype(F32) for h in range(HG_HEADS)])


def _unheads(x):
    return jnp.concatenate([x[h] for h in range(HG_HEADS)], axis=-1)


def _hgrn_fwd(pre, lbl, gw, *, name):
    _, T, Dm = pre.shape
    rb = min(HG_RB, T)
    C = min(HG_C, rb)
    ncb = rb // C

    def body(pre_ref, lbl_ref, gw_ref, o_ref, st_ref, s_ref):
        @pl.when(pl.program_id(0) == 0)
        def _():
            s_ref[...] = jnp.zeros_like(s_ref)

        def chunk(ci, carry):
            r0 = pl.multiple_of(ci * C, C)
            rows = pl.ds(r0, C)
            st = s_ref[...]
            st_ref[ci] = st
            out, st_new = _hg_chunk(*[_heads(pre_ref.at[j], rows) for j in range(4)],
                                    _heads(lbl_ref, slice(0, 1)), _heads(lbl_ref, slice(1, 2)), gw_ref[...], st)
            o_ref[rows, :] = _unheads(out).astype(BF16)
            s_ref[...] = st_new
            return carry

        lax.fori_loop(0, ncb, chunk, 0, unroll=True)

    row = pl.BlockSpec((rb, Dm), lambda n: (n, 0))
    return pl.pallas_call(
        body, name=name, grid=(T // rb,),
        in_specs=[pl.BlockSpec((4, rb, Dm), lambda n: (0, n, 0)), pl.BlockSpec((2, Dm), lambda n: (0, 0)),
                  pl.BlockSpec((1, HG_DIM), lambda n: (0, 0))],
        out_specs=[row, pl.BlockSpec((ncb, HG_HEADS, HG_DIM, HG_DIM), lambda n: (n, 0, 0, 0))],
        out_shape=[jax.ShapeDtypeStruct((T, Dm), BF16),
                   jax.ShapeDtypeStruct((T // C, HG_HEADS, HG_DIM, HG_DIM), F32)],
        scratch_shapes=[pltpu.VMEM((HG_HEADS, HG_DIM, HG_DIM), F32)],
        compiler_params=_params(("arbitrary",)),
    )(pre, lbl, gw)


def _hgrn_bwd(pre, lbl, gw, states, dout, *, name, after=None):
    _, T, Dm = pre.shape
    rb = min(HG_RB, T)
    C = min(HG_C, rb)
    ncb = rb // C
    nb = T // rb

    def body(pre_ref, lbl_ref, gw_ref, st_ref, do_ref, dpre_ref, dlbl_ref, dgw_ref, ds_ref):
        @pl.when(pl.program_id(0) == 0)
        def _():
            ds_ref[...] = jnp.zeros_like(ds_ref)
            dlbl_ref[...] = jnp.zeros_like(dlbl_ref)
            dgw_ref[...] = jnp.zeros_like(dgw_ref)

        def chunk(cj, carry):
            ci = ncb - 1 - cj
            r0 = pl.multiple_of(ci * C, C)
            rows = pl.ds(r0, C)
            _, vjp = jax.vjp(_hg_chunk, *[_heads(pre_ref.at[j], rows) for j in range(4)],
                             _heads(lbl_ref, slice(0, 1)), _heads(lbl_ref, slice(1, 2)), gw_ref[...], st_ref[ci])
            *dpre, dl0, dl1, dgw, dst = vjp((_heads(do_ref, rows), ds_ref[...]))
            for j in range(4):
                dpre_ref[j, rows, :] = _unheads(dpre[j]).astype(BF16)
            dlbl_ref[0:1, :] += _unheads(dl0)
            dlbl_ref[1:2, :] += _unheads(dl1)
            dgw_ref[...] += dgw
            ds_ref[...] = dst
            return carry

        lax.fori_loop(0, ncb, chunk, 0, unroll=True)

    row = pl.BlockSpec((rb, Dm), lambda n: (nb - 1 - n, 0))
    lsp = pl.BlockSpec((2, Dm), lambda n: (0, 0))
    gsp = pl.BlockSpec((1, HG_DIM), lambda n: (0, 0))
    pre_spec = pl.BlockSpec((4, rb, Dm), lambda n: (0, nb - 1 - n, 0))
    body, xs, xa = _after(body, 5, after)
    return pl.pallas_call(
        body, name=name, grid=(nb,),
        in_specs=[pre_spec, lsp, gsp, pl.BlockSpec((ncb, HG_HEADS, HG_DIM, HG_DIM), lambda n: (nb - 1 - n, 0, 0, 0)), row] + xs,
        out_specs=[pre_spec, lsp, gsp],
        out_shape=[jax.ShapeDtypeStruct((4, T, Dm), BF16), jax.ShapeDtypeStruct((2, Dm), F32),
                   jax.ShapeDtypeStruct((1, HG_DIM), F32)],
        scratch_shapes=[pltpu.VMEM((HG_HEADS, HG_DIM, HG_DIM), F32)],
        compiler_params=_params(("arbitrary",)),
    )(pre, lbl, gw, states, dout, *xa)


CONV_HALO = 2 * SUBLANES


def _conv_rows(u_ref, scr, w, bias, r0, R):
    cur = u_ref[pl.ds(r0, R), :].astype(F32)
    p0 = pl.multiple_of(jnp.maximum(r0 - CONV_HALO, 0), CONV_HALO)
    scr[0:CONV_HALO, :] = jnp.where(r0 > 0, u_ref[pl.ds(p0, CONV_HALO), :].astype(F32), 0.0)
    scr[CONV_HALO:CONV_HALO + R, :] = cur
    s1 = scr[CONV_HALO - 1:CONV_HALO - 1 + R, :]
    s2 = scr[CONV_HALO - 2:CONV_HALO - 2 + R, :]
    return w[0:1, :] * s2 + w[1:2, :] * s1 + w[2:3, :] * cur + bias, cur, s1, s2


def _halves_spec(T, Fd):
    per = Fd // 2 // LANES
    return pl.BlockSpec((2, None, T, LANES), lambda j: (0, j // per, 0, j % per))


def _conv_gate_fwd(u, wa, wb, ba, bb, *, name):
    T, Fd = u.shape[2], 2 * u.shape[3]
    R = min(CONV_R, T)
    tc = LANES

    def body(u_ref, wa_ref, wb_ref, ba_ref, bb_ref, o_ref, sa, sb):
        wa_, wb_, ba_, bb_ = wa_ref[...], wb_ref[...], ba_ref[...], bb_ref[...]

        def step(ri, carry):
            r0 = pl.multiple_of(ri * R, R)
            ca = _conv_rows(u_ref.at[0], sa, wa_, ba_, r0, R)[0]
            cb = _conv_rows(u_ref.at[1], sb, wb_, bb_, r0, R)[0]
            o_ref[pl.ds(r0, R), :] = (jax.nn.silu(ca) * cb).astype(BF16)
            return carry

        lax.fori_loop(0, T // R, step, 0)

    col = pl.BlockSpec((T, tc), lambda j: (0, j))
    wsp = pl.BlockSpec((3, tc), lambda j: (0, j))
    bsp = pl.BlockSpec((1, tc), lambda j: (0, j))
    both = _halves_spec(T, Fd)
    return pl.pallas_call(
        body, name=name, grid=(Fd // tc,), in_specs=[both, wsp, wsp, bsp, bsp], out_specs=col,
        out_shape=jax.ShapeDtypeStruct((T, Fd), BF16),
        scratch_shapes=[pltpu.VMEM((CONV_HALO + R, tc), F32)] * 2,
        compiler_params=_params(("parallel",)),
    )(u, wa, wb, ba, bb)


def _conv_gate_bwd(u, wa, wb, ba, bb, dact, *, name):
    T, Fd = u.shape[2], 2 * u.shape[3]
    R = min(CONV_R, T)
    nr = T // R
    tc = LANES

    def body(u_ref, wa_ref, wb_ref, ba_ref, bb_ref, da_ref,
             du_ref, dp_ref, sa, sb, sda, sdb):
        wa_, wb_, ba_, bb_ = wa_ref[...], wb_ref[...], ba_ref[...], bb_ref[...]
        sda[R:R + SUBLANES, :] = jnp.zeros((SUBLANES, tc), F32)
        sdb[R:R + SUBLANES, :] = jnp.zeros((SUBLANES, tc), F32)

        def taps(dc, cur, s1, s2):
            return jnp.concatenate([jnp.sum(dc * s2, axis=0, keepdims=True), jnp.sum(dc * s1, axis=0, keepdims=True),
                                    jnp.sum(dc * cur, axis=0, keepdims=True)], axis=0)

        def du_rows(sd, dc, w):
            sd[0:R, :] = dc
            du = w[2:3, :] * dc + w[1:2, :] * sd[1:1 + R, :] + w[0:1, :] * sd[2:2 + R, :]
            sd[R:R + SUBLANES, :] = dc[0:SUBLANES]
            return du

        def step(rj, carry):
            dwa, dwb, dba, dbb = carry
            r0 = pl.multiple_of((nr - 1 - rj) * R, R)
            ca, cura, s1a, s2a = _conv_rows(u_ref.at[0], sa, wa_, ba_, r0, R)
            cb, curb, s1b, s2b = _conv_rows(u_ref.at[1], sb, wb_, bb_, r0, R)
            dact_ = da_ref[pl.ds(r0, R), :].astype(F32)
            sg = jax.nn.sigmoid(ca)
            dca = dact_ * cb * (sg * (1.0 + ca * (1.0 - sg)))
            dcb = dact_ * (ca * sg)
            du_ref[0, pl.ds(r0, R), :] = du_rows(sda, dca, wa_).astype(BF16)
            du_ref[1, pl.ds(r0, R), :] = du_rows(sdb, dcb, wb_).astype(BF16)
            return (dwa + taps(dca, cura, s1a, s2a), dwb + taps(dcb, curb, s1b, s2b),
                    dba + jnp.sum(dca, axis=0, keepdims=True), dbb + jnp.sum(dcb, axis=0, keepdims=True))

        z3 = jnp.zeros((3, tc), F32)
        z1 = jnp.zeros((1, tc), F32)
        dwa, dwb, dba, dbb = lax.fori_loop(0, nr, step, (z3, z3, z1, z1))
        dp_ref[0] = jnp.concatenate([dwa, dba], axis=0)
        dp_ref[1] = jnp.concatenate([dwb, dbb], axis=0)

    col = pl.BlockSpec((T, tc), lambda j: (0, j))
    wsp = pl.BlockSpec((3, tc), lambda j: (0, j))
    bsp = pl.BlockSpec((1, tc), lambda j: (0, j))
    both = _halves_spec(T, Fd)
    return pl.pallas_call(
        body, name=name, grid=(Fd // tc,), in_specs=[both, wsp, wsp, bsp, bsp, col],
        out_specs=[both, pl.BlockSpec((2, 4, tc), lambda j: (0, 0, j))],
        out_shape=[jax.ShapeDtypeStruct(u.shape, BF16), jax.ShapeDtypeStruct((2, 4, Fd), F32)],
        scratch_shapes=[pltpu.VMEM((CONV_HALO + R, tc), F32)] * 2 + [pltpu.VMEM((R + SUBLANES, tc), F32)] * 2,
        compiler_params=_params(("parallel",)),
    )(u, wa, wb, ba, bb, dact)


def _bucket_index():
    t = np.arange(SW_WINDOW)[None, :] + SW_WINDOW
    s = np.arange(2 * SW_WINDOW)[:, None]
    dist = np.maximum(t - s, 0)
    exact = REL_BUCKETS // 2
    d = np.maximum(dist, 1).astype(np.float32)
    log_b = exact + (np.log(d / np.float32(exact)) / np.float32(math.log(REL_MAX_DIST / exact))
                     * np.float32(REL_BUCKETS - exact)).astype(np.int32)
    bucket = np.where(dist < exact, dist, np.minimum(log_b, REL_BUCKETS - 1))
    return bucket.astype(np.int32).reshape(1, -1)


BIAS_COLS = SW_WINDOW * 2 * SW_WINDOW
BIAS_TILE = 4096


def _bias_from_table(table, bucket, *, name):
    def body(t_ref, idx_ref, o_ref):
        onehot = (lax.broadcasted_iota(jnp.int32, (REL_BUCKETS, BIAS_TILE), 0) == idx_ref[...]).astype(BF16)
        acc = jnp.zeros((SW_Q_HEADS, BIAS_TILE), F32)
        for piece in _split3(t_ref[...]):
            acc = acc + lax.dot_general(piece, onehot, (((0,), (0,)), ((), ())), preferred_element_type=F32)
        o_ref[...] = acc

    return pl.pallas_call(
        body, name=name, grid=(BIAS_COLS // BIAS_TILE,),
        in_specs=[pl.BlockSpec((REL_BUCKETS, SW_Q_HEADS), lambda j: (0, 0)), pl.BlockSpec((1, BIAS_TILE), lambda j: (0, j))],
        out_specs=pl.BlockSpec((SW_Q_HEADS, BIAS_TILE), lambda j: (0, j)),
        out_shape=jax.ShapeDtypeStruct((SW_Q_HEADS, BIAS_COLS), F32),
        compiler_params=_params(("parallel",)),
    )(table, bucket)


def _table_grad(dbias, bucket, *, name):
    def body(d_ref, idx_ref, o_ref):
        @pl.when(pl.program_id(0) == 0)
        def _():
            o_ref[...] = jnp.zeros_like(o_ref)

        onehot = (lax.broadcasted_iota(jnp.int32, (REL_BUCKETS, BIAS_TILE), 0) == idx_ref[...]).astype(BF16)
        acc = jnp.zeros((REL_BUCKETS, SW_Q_HEADS), F32)
        for piece in _split3(d_ref[...]):
            acc = acc + lax.dot_general(onehot, piece, (((1,), (1,)), ((), ())), preferred_element_type=F32)
        o_ref[...] += acc

    return pl.pallas_call(
        body, name=name, grid=(BIAS_COLS // BIAS_TILE,),
        in_specs=[pl.BlockSpec((SW_Q_HEADS, BIAS_TILE), lambda j: (0, j)), pl.BlockSpec((1, BIAS_TILE), lambda j: (0, j))],
        out_specs=pl.BlockSpec((REL_BUCKETS, SW_Q_HEADS), lambda j: (0, 0)),
        out_shape=jax.ShapeDtypeStruct((REL_BUCKETS, SW_Q_HEADS), F32),
        compiler_params=_params(("arbitrary",)),
    )(dbias, bucket)


KV_DIM = SW_KV_HEADS * SW_HEAD_DIM
GROUP_ROWS = SW_GROUP * SW_HEAD_DIM
GROUP_LANES = SW_GROUP * SW_WINDOW


def _band_mask(n):
    s = lax.broadcasted_iota(jnp.int32, (2 * SW_WINDOW, GROUP_LANES), 0)
    t = (lax.broadcasted_iota(jnp.int32, (2 * SW_WINDOW, GROUP_LANES), 1) & (SW_WINDOW - 1)) + SW_WINDOW
    dist = t - s
    return (dist >= 0) & (dist < SW_WINDOW) & ((n > 0) | (s >= SW_WINDOW))


def _side_by_side(x_ref, g):
    r0 = g * GROUP_ROWS
    return jnp.concatenate([x_ref[r0 + r * SW_HEAD_DIM:r0 + (r + 1) * SW_HEAD_DIM, :] for r in range(SW_GROUP)], axis=1)


def _group_inputs(bias_ref, sink_ref, g):
    heads = range(g * SW_GROUP, (g + 1) * SW_GROUP)
    bias = jnp.concatenate([bias_ref[h] for h in heads], axis=1)
    sink = jnp.concatenate([jnp.broadcast_to(sink_ref[:, h:h + 1], (1, SW_WINDOW)) for h in heads], axis=1)
    return heads, bias, sink


def _kv_pair(kvp_ref, kvc_ref, g):
    ks = slice(g * SW_HEAD_DIM, (g + 1) * SW_HEAD_DIM)
    vs = slice(KV_DIM + g * SW_HEAD_DIM, KV_DIM + (g + 1) * SW_HEAD_DIM)
    kk = jnp.concatenate([kvp_ref[:, ks], kvc_ref[:, ks]], axis=0)
    vv = jnp.concatenate([kvp_ref[:, vs], kvc_ref[:, vs]], axis=0)
    return kk, vv, ks, vs


def _col_max(x):
    return jnp.max(x, axis=0, keepdims=True)


def _col_sum(x):
    return jnp.sum(x, axis=0, keepdims=True)


def _attn_fwd(qt, kv, bias, sinks, *, name):
    Dm, T = qt.shape
    W = SW_WINDOW

    def body(q_ref, kvc_ref, kvp_ref, bias_ref, sink_ref, o_ref):
        mask = _band_mask(pl.program_id(0))
        G = range(SW_KV_HEADS)
        ins = [_group_inputs(bias_ref, sink_ref, g) for g in G]
        kvs = [_kv_pair(kvp_ref, kvc_ref, g) for g in G]
        q = [_side_by_side(q_ref, g) for g in G]
        lg = [jnp.where(mask, mm(kvs[g][0], q[g]) * (SW_HEAD_DIM ** -0.5) + ins[g][1], -jnp.inf) for g in G]
        m = [jnp.maximum(_col_max(lg[g]), ins[g][2]) for g in G]
        p = [jnp.exp(lg[g] - m[g]) for g in G]
        den = [_col_sum(p[g]) + jnp.exp(ins[g][2] - m[g]) for g in G]
        o = [mm_tn(kvs[g][1], p[g]) / den[g] for g in G]
        for g in G:
            for r in range(SW_GROUP):
                o_ref[g * GROUP_ROWS + r * SW_HEAD_DIM:g * GROUP_ROWS + (r + 1) * SW_HEAD_DIM, :] = (
                    o[g][:, r * W:(r + 1) * W].astype(BF16))

    return pl.pallas_call(
        body, name=name, grid=(T // W,),
        in_specs=[pl.BlockSpec((Dm, W), lambda n: (0, n)),
                  pl.BlockSpec((W, 2 * KV_DIM), lambda n: (n, 0)),
                  pl.BlockSpec((W, 2 * KV_DIM), lambda n: (jnp.maximum(n - 1, 0), 0)),
                  pl.BlockSpec((SW_Q_HEADS, 2 * W, W), lambda n: (0, 0, 0)),
                  pl.BlockSpec((1, SW_Q_HEADS), lambda n: (0, 0))],
        out_specs=pl.BlockSpec((Dm, W), lambda n: (0, n)),
        out_shape=jax.ShapeDtypeStruct((Dm, T), BF16),
        compiler_params=_params(("parallel",)),
    )(qt, kv, kv, bias, sinks)


def _attn_bwd(qt, kv, bias, sinks, dot, *, name):
    Dm, T = qt.shape
    W = SW_WINDOW
    nb = T // W

    def body(q_ref, kvc_ref, kvp_ref, bias_ref, sink_ref, do_ref,
             dq_ref, dkv_ref, dbias_ref, dsink_ref, carry_ref):
        @pl.when(pl.program_id(0) == 0)
        def _():
            carry_ref[...] = jnp.zeros_like(carry_ref)
            dbias_ref[...] = jnp.zeros_like(dbias_ref)
            dsink_ref[...] = jnp.zeros_like(dsink_ref)

        n = nb - 1 - pl.program_id(0)
        mask = _band_mask(n)
        lane = lax.broadcasted_iota(jnp.int32, (1, SW_Q_HEADS), 1)
        sc = SW_HEAD_DIM ** -0.5
        G = range(SW_KV_HEADS)
        ins = [_group_inputs(bias_ref, sink_ref, g) for g in G]
        kvs = [_kv_pair(kvp_ref, kvc_ref, g) for g in G]
        q = [_side_by_side(q_ref, g) for g in G]
        do = [_side_by_side(do_ref, g) for g in G]
        lg = [jnp.where(mask, mm(kvs[g][0], q[g]) * sc + ins[g][1], -jnp.inf) for g in G]
        m = [jnp.maximum(_col_max(lg[g]), ins[g][2]) for g in G]
        p = [jnp.exp(lg[g] - m[g]) for g in G]
        ps = [jnp.exp(ins[g][2] - m[g]) for g in G]
        rden = [1.0 / (_col_sum(p[g]) + ps[g]) for g in G]
        pn = [p[g] * rden[g] for g in G]
        dpn = [mm(kvs[g][1], do[g]) for g in G]
        delta = [_col_sum(pn[g] * dpn[g]) for g in G]
        ds = [pn[g] * (dpn[g] - delta[g]) for g in G]
        dsr = [-(ps[g] * rden[g]) * delta[g] for g in G]
        dq = [mm_tn(kvs[g][0], ds[g]) * sc for g in G]
        dkk = [mm_nt(ds[g], q[g]) * sc for g in G]
        dvv = [mm_nt(pn[g], do[g]) for g in G]
        dsink = jnp.zeros((1, SW_Q_HEADS), F32)
        for g in G:
            _, _, ks, vs = kvs[g]
            for r, h in enumerate(ins[g][0]):
                cols = slice(r * W, (r + 1) * W)
                dbias_ref[h] += ds[g][:, cols]
                dq_ref[g * GROUP_ROWS + r * SW_HEAD_DIM:g * GROUP_ROWS + (r + 1) * SW_HEAD_DIM, :] = dq[g][:, cols].astype(BF16)
                dsink = dsink + jnp.where(lane == h, jnp.sum(dsr[g][:, cols], axis=1, keepdims=True), 0.0)
            dkv_ref[:, ks] = (carry_ref[:, ks] + dkk[g][W:]).astype(BF16)
            dkv_ref[:, vs] = (carry_ref[:, vs] + dvv[g][W:]).astype(BF16)
            carry_ref[:, ks] = dkk[g][:W]
            carry_ref[:, vs] = dvv[g][:W]
        dsink_ref[...] += dsink

    rev = lambda n: (nb - 1 - n, 0)
    revt = lambda n: (0, nb - 1 - n)
    return pl.pallas_call(
        body, name=name, grid=(nb,),
        in_specs=[pl.BlockSpec((Dm, W), revt),
                  pl.BlockSpec((W, 2 * KV_DIM), rev),
                  pl.BlockSpec((W, 2 * KV_DIM), lambda n: (jnp.maximum(nb - 2 - n, 0), 0)),
                  pl.BlockSpec((SW_Q_HEADS, 2 * W, W), lambda n: (0, 0, 0)),
                  pl.BlockSpec((1, SW_Q_HEADS), lambda n: (0, 0)),
                  pl.BlockSpec((Dm, W), revt)],
        out_specs=[pl.BlockSpec((Dm, W), revt), pl.BlockSpec((W, 2 * KV_DIM), rev),
                   pl.BlockSpec((SW_Q_HEADS, 2 * W, W), lambda n: (0, 0, 0)),
                   pl.BlockSpec((1, SW_Q_HEADS), lambda n: (0, 0))],
        out_shape=[jax.ShapeDtypeStruct((Dm, T), BF16), jax.ShapeDtypeStruct((T, 2 * KV_DIM), BF16),
                   jax.ShapeDtypeStruct((SW_Q_HEADS, 2 * W, W), F32), jax.ShapeDtypeStruct((1, SW_Q_HEADS), F32)],
        scratch_shapes=[pltpu.VMEM((W, 2 * KV_DIM), F32)],
        compiler_params=_params(("arbitrary",)),
    )(qt, kv, kv, bias, sinks, dot)


def _ffn_fwd(hb, w, l, after=None):
    u = _matmul_planes_nn(hb, w["ffn_in"][l], name=f"ffn{l}_up", after=after)
    u = u.reshape((2, 2) + u.shape[1:])
    act = _conv_gate_fwd(u, w["conv_w_a"][l], w["conv_w_b"][l], w["conv_b_a"][l], w["conv_b_b"][l],
                         name=f"ffn{l}_conv_gate")
    return u, act


def _ffn_bwd(dffb, dh_scaled, hb, u, act, w, l, dact):
    g_out = _matmul(act, dffb, mode="tn", name=f"ffn{l}_down_dw", tm=1408, tn=1024, tk=2048, also_bf16=True)
    du, g_conv = _conv_gate_bwd(u, w["conv_w_a"][l], w["conv_w_b"][l], w["conv_b_a"][l], w["conv_b_b"][l],
                                dact, name=f"ffn{l}_conv_gate_bwd")
    du = du.reshape((N_CHIPS,) + du.shape[2:])
    dh = _matmul_planes_nt(du, w["ffn_in"][l], dh_scaled, add_scale=ALPHA, name=f"ffn{l}_up_dx")
    g_in = _matmul(hb, du, mode="tn", planes="n", name=f"ffn{l}_up_dw", tm=1024, tn=FFN_DIM // 2, tk=2048, also_bf16=True)
    return dh, dict(ffn_out=g_out, ffn_in=g_in, conv=g_conv)


def _local_step(x, xb, tgt, w, more_weights, emit):
    bucket = jnp.asarray(_bucket_index())

    pre = _matmul_planes_nn(xb, w["hg_in"], name="hg_in", tm=1024, after=w.get("token"))
    og, states = _hgrn_fwd(pre, w["lb_logits"], w["gnorm"], name="hgrn_fwd")
    z1, h1, h1b = _matmul_ln(og, w["hg_out"], x, w["ln_mix_g"][0], w["ln_mix_b"][0], tm=1024, name="hg_out_ln")
    w = {**w, **more_weights(1, h1b)}
    u0, act0 = _ffn_fwd(h1b, w, 0, after=w.get("token"))
    z2, h2, h2b = _matmul_ln(act0, w["ffn_out"][0], h1, w["ln_ffn_g"][0], w["ln_ffn_b"][0], name="ffn0_down_ln")
    kv = _matmul(h2b, w["kv"], mode="nn", out_dtype=BF16, name="kv_proj")

    bias = _bias_from_table(w["rel_bias"], bucket, name="rel_bias_expand").reshape(SW_Q_HEADS, 2 * SW_WINDOW, SW_WINDOW)
    q1 = _matmul(w["sw_q"], h2b, mode="tt", out_dtype=BF16, name="sw_q", tm=1024, tn=1024)
    o1 = _attn_fwd(q1, kv, bias, w["sinks"], name="attn_fwd")
    z3, h3, h3b = _matmul_ln(o1, w["sw_out"], h2, w["ln_mix_g"][1], w["ln_mix_b"][1], a_t=True, tm=1024, name="sw_out_ln")
    w = {**w, **more_weights(2, h3b)}
    u1, act1 = _ffn_fwd(h3b, w, 1)

    g = {}
    dz, dzb, g["ln_ffn1"], loss_tile, dact1 = _matmul_ln(act1, w["ffn_out"][1], h3, w["ln_ffn_g"][1], w["ln_ffn_b"][1],
                                                         tgt=tgt, name="ffn1_down_ln_loss")

    dh3, gf1 = _ffn_bwd(dzb, dz, h3b, u1, act1, w, 1, dact1)
    dz, dzb, g["ln_mix1"], do1 = _ln_bwd_matmul(dh3, z3, w["ln_mix_g"][1], w["ln_mix_b"][1], w["sw_out"], out_t=True,
                                                name="ln_mix1_bwd_sw_out_dx")
    g_sw_out = _matmul(o1, dzb, mode="nn", name="sw_out_dw", tm=1024, tn=1024, tk=2048, also_bf16=True)
    dq1, dkv, dbias, dsinks = _attn_bwd(q1, kv, bias, w["sinks"], do1, name="attn_bwd")
    g["sinks"] = dsinks
    g["rel_bias"] = _table_grad(dbias.reshape(SW_Q_HEADS, BIAS_COLS), bucket, name="rel_bias_grad")
    dh2 = _matmul(dq1, w["sw_q"], mode="tt", add=dz, add_scale=ALPHA, name="sw_q_dx", tn=1024)
    dh2 = _matmul(dkv, w["kv"], mode="nt", add=dh2, name="kv_dx", tn=1024)
    g_sw_q = _matmul(h2b, dq1, mode="tt", name="sw_q_dw", tm=1024, tn=1024, tk=2048, also_bf16=True)
    g_kv = _matmul(h2b, dkv, mode="tn", name="kv_dw", tm=1024, tn=512, tk=2048, also_bf16=True)
    tok = emit(1, dict(sw_q=g_sw_q, sw_out=g_sw_out, kv=g_kv, ffn_in=gf1["ffn_in"], ffn_out=gf1["ffn_out"]))

    dz, dzb, g["ln_ffn0"], dact0 = _ln_bwd_matmul(dh2, z2, w["ln_ffn_g"][0], w["ln_ffn_b"][0], w["ffn_out"][0],
                                                  name="ln_ffn0_bwd_down_dx", after=tok)
    dh1, gf0 = _ffn_bwd(dzb, dz, h1b, u0, act0, w, 0, dact0)
    dz, dzb, g["ln_mix0"], dog = _ln_bwd_matmul(dh1, z1, w["ln_mix_g"][0], w["ln_mix_b"][0], w["hg_out"],
                                                name="ln_mix0_bwd_hg_out_dx")
    g_hg_out = _matmul(og, dzb, mode="tn", name="hg_out_dw", tm=1024, tn=1024, tk=2048, also_bf16=True)
    tok = emit(2, dict(hg_out=g_hg_out, ffn_in=gf0["ffn_in"], ffn_out=gf0["ffn_out"]))
    dpre, g["lb_logits"], g["gnorm"] = _hgrn_bwd(pre, w["lb_logits"], w["gnorm"], states, dog, name="hgrn_bwd", after=tok)
    tok = emit(3, dict(hg_in=_matmul(xb, dpre, mode="tn", planes="n", name="hg_in_dw", tm=1024, tn=1024, tk=2048, also_bf16=True)))
    dx = _matmul_planes_nt(dpre, w["hg_in"], dz, add_scale=ALPHA, name="hg_in_dx", after=tok)
    g["conv0"], g["conv1"] = gf0["conv"], gf1["conv"]
    return loss_tile, dx, g


def _adamw(wt, ga, gb, m, v, *, name, rows=None, prev=None):
    R, Cc = wt.shape
    r0, n = rows if rows is not None else (0, R)
    tr = _tile(n, 256, SUBLANES) if n % SUBLANES == 0 else n
    assert r0 % tr == 0
    c1 = 1.0 - ADAM_B1 ** ADAM_STEP
    c2 = 1.0 - ADAM_B2 ** ADAM_STEP
    n_in = 5

    def body(*refs):
        w_ref, ga_ref, gb_ref, m_ref, v_ref = refs[:n_in]
        g_ = ga_ref[...] + gb_ref[...]
        g_ref, d_ref, nm_ref, nv_ref = refs[-4:]
        nm = ADAM_B1 * m_ref[...] + (1.0 - ADAM_B1) * g_
        nv = ADAM_B2 * v_ref[...] + (1.0 - ADAM_B2) * (g_ * g_)
        g_ref[...] = g_
        d_ref[...] = -ADAM_LR * ((nm / c1) / (jnp.sqrt(nv / c2) + ADAM_EPS) + ADAM_WD * w_ref[...])
        nm_ref[...] = nm
        nv_ref[...] = nv

    full = pl.BlockSpec((tr, Cc), lambda i: (i + r0 // tr, 0))
    part = pl.BlockSpec((tr, Cc), lambda i: (i, 0))
    args = (wt, ga, gb, m, v)
    in_specs = [full, part, part, full, full]
    aliases = {}
    if prev is not None:
        args, in_specs = args + tuple(prev), in_specs + [ANY_SPEC] * 4
        aliases = {n_in + t: t for t in range(4)}
    return pl.pallas_call(
        body, name=name, grid=(n // tr,), in_specs=in_specs, out_specs=[full] * 4,
        out_shape=[jax.ShapeDtypeStruct((R, Cc), F32)] * 4, input_output_aliases=aliases,
        compiler_params=_params(("parallel",)),
    )(*args)


def _adamw_small(ws, gs, ms, vs, *, name):
    n = len(ws)
    c1 = 1.0 - ADAM_B1 ** ADAM_STEP
    c2 = 1.0 - ADAM_B2 ** ADAM_STEP

    def body(*refs):
        w_refs, g_refs, m_refs, v_refs = (refs[k * n:(k + 1) * n] for k in range(4))
        d_refs, nm_refs, nv_refs = (refs[(4 + k) * n:(5 + k) * n] for k in range(3))
        for i in range(n):
            g_ = g_refs[i][...]
            nm = ADAM_B1 * m_refs[i][...] + (1.0 - ADAM_B1) * g_
            nv = ADAM_B2 * v_refs[i][...] + (1.0 - ADAM_B2) * (g_ * g_)
            d_refs[i][...] = -ADAM_LR * ((nm / c1) / (jnp.sqrt(nv / c2) + ADAM_EPS) + ADAM_WD * w_refs[i][...])
            nm_refs[i][...] = nm
            nv_refs[i][...] = nv

    vm = pl.BlockSpec(memory_space=pltpu.VMEM)
    out = pl.pallas_call(
        body, name=name, in_specs=[vm] * (4 * n), out_specs=[vm] * (3 * n),
        out_shape=[jax.ShapeDtypeStruct(w.shape, F32) for w in ws] * 3,
    )(*ws, *gs, *ms, *vs)
    return out[:n], out[n:2 * n], out[2 * n:]


HBM_SPEC = pl.BlockSpec(memory_space=pltpu.HBM)
SEM_SPEC = pl.BlockSpec(memory_space=pltpu.SEMAPHORE)
VMEM_SPEC = pl.BlockSpec(memory_space=pltpu.VMEM)
DATAFLOW = pltpu.SideEffectType.DATAFLOW_SIDE_EFFECTING


def _in_hbm(a):
    return pltpu.with_memory_space_constraint(a, pltpu.HBM)


def _place():
    return lax.axis_index("x"), lax.axis_index("y"), lax.axis_index("c")


def _other_chips(x, y):
    return [(1 - x, y), (x, 1 - y), (1 - x, 1 - y)]


def _sum8(vs, *, name):
    n = len(vs)

    def body(*refs):
        v_refs, all_refs, o_refs = refs[:n], refs[n:2 * n], refs[2 * n:3 * n]
        send_sems, recv_sems, local_sems = refs[3 * n:]
        x, y, c = _place()
        me, sibling = (x, y, c), (x, y, 1 - c)
        chips = _other_chips(x, y)

        def slot(i, px, py, pc):
            return all_refs[i].at[4 * px + 2 * py + pc]

        def copy(i, k, block, to, src=None):
            return pltpu.make_async_remote_copy(
                src_ref=slot(i, *block) if src is None else src, dst_ref=slot(i, *block),
                send_sem=send_sems.at[7 * i + k], recv_sem=recv_sems.at[7 * i + k], device_id=to, device_id_type=MESH)

        mine = [pltpu.make_async_copy(v_refs[i], slot(i, *me), local_sems.at[i]) for i in range(n)]
        for cp in mine:
            cp.start()
        first = [copy(i, 0, me, sibling, src=v_refs[i]) for i in range(n)]
        first += [copy(i, 1 + j, me, (*chip, c), src=v_refs[i]) for i in range(n) for j, chip in enumerate(chips)]
        for cp in first:
            cp.start()
        passed = []
        for i in range(n):
            for j, chip in enumerate(chips):
                copy(i, 1 + j, (*chip, c), me).wait_recv()
                passed.append(copy(i, 4 + j, (*chip, c), sibling))
                passed[-1].start()
        for i in range(n):
            copy(i, 0, sibling, me).wait_recv()
            for j, chip in enumerate(chips):
                copy(i, 4 + j, (*chip, 1 - c), me).wait_recv()
        for cp in first + passed:
            cp.wait_send()
        for cp in mine:
            cp.wait()
        for i in range(n):
            acc = all_refs[i][0]
            for d in range(1, N_DEV):
                acc = acc + all_refs[i][d]
            o_refs[i][...] = acc

    return pl.pallas_call(
        body, name=name, in_specs=[VMEM_SPEC] * n, out_specs=[VMEM_SPEC] * (2 * n),
        out_shape=[jax.ShapeDtypeStruct((N_DEV,) + v.shape, F32) for v in vs] + [jax.ShapeDtypeStruct(v.shape, F32) for v in vs],
        scratch_shapes=[pltpu.SemaphoreType.DMA((7 * n,)), pltpu.SemaphoreType.DMA((7 * n,)), pltpu.SemaphoreType.DMA((n,))],
        compiler_params=pltpu.CompilerParams(vmem_limit_bytes=VMEM_LIMIT),
    )(*vs)[n:]


def _swap_copies(src, land, send, recv):
    x, y, c = _place()
    return [pltpu.make_async_remote_copy(src_ref=src[i], dst_ref=land[i], send_sem=send.at[i], recv_sem=recv.at[i],
                                         device_id=(x, y, 1 - c), device_id_type=MESH) for i in range(len(src))]


def _swap_start(vs, *, name):
    n = len(vs)

    def body(*refs):
        src, land, send, recv, token = refs[:n], refs[n:2 * n], refs[2 * n], refs[2 * n + 1], refs[-1]
        for cp in _swap_copies(src, land, send, recv):
            cp.start()
        token[...] = jnp.zeros_like(token)

    lands = [lax.empty(v.shape, v.dtype) for v in vs]
    sems = pltpu.SemaphoreType.DMA((n,))
    out = pl.pallas_call(
        body, name=name, in_specs=[HBM_SPEC] * (2 * n),
        out_specs=[SEM_SPEC, SEM_SPEC] + [HBM_SPEC] * (2 * n) + [VMEM_SPEC],
        out_shape=[sems, sems] + [pltpu.HBM(a.shape, a.dtype) for a in list(vs) + lands]
        + [jax.ShapeDtypeStruct((SUBLANES, LANES), F32)],
        input_output_aliases={i: 2 + i for i in range(2 * n)},
        compiler_params=pltpu.CompilerParams(has_side_effects=DATAFLOW),
    )(*[_in_hbm(a) for a in list(vs) + lands])
    return (out[0], out[1], out[2:2 + n], out[2 + n:2 + 2 * n]), out[-1]


def _swap_wait(handle, after, *, name):
    send_sems, recv_sems, srcs, lands = handle
    n = len(srcs)

    def body(*refs):
        src, land, send, recv = refs[:n], refs[n:2 * n], refs[2 * n], refs[2 * n + 1]
        for cp in _swap_copies(src, land, send, recv):
            cp.wait_send()
            cp.wait_recv()

    both = list(srcs) + list(lands)
    out = pl.pallas_call(
        body, name=name, in_specs=[HBM_SPEC] * (2 * n) + [SEM_SPEC, SEM_SPEC, ANY_SPEC], out_specs=[HBM_SPEC] * (2 * n),
        out_shape=[pltpu.HBM(a.shape, a.dtype) for a in both],
        input_output_aliases={i: i for i in range(2 * n)},
        compiler_params=pltpu.CompilerParams(has_side_effects=DATAFLOW),
    )(*both, send_sems, recv_sems, after)
    return out[:n], out[n:]


def _gather_copies(srcs, lands, send, recv, sibling=False):
    x, y, c = _place()
    out = []
    for i, (src, land) in enumerate(zip(srcs, lands)):
        half = land.shape[1] // 2
        rows = pl.ds(c * half, half)
        for k, (px, py) in enumerate(_other_chips(x, y)):
            if sibling:
                src_ref, dst_ref, to = src.at[2 * px + py, rows], land.at[2 * px + py, rows], (x, y, 1 - c)
            else:
                src_ref, dst_ref, to = src.at[rows], land.at[2 * x + y, rows], (px, py, c)
            out.append(pltpu.make_async_remote_copy(src_ref=src_ref, dst_ref=dst_ref, send_sem=send.at[3 * i + k],
                                                    recv_sem=recv.at[3 * i + k], device_id=to, device_id_type=MESH))
    return out


def _gather_arrivals(lands, send, recv, sibling=False):
    x, y, c = _place()
    out = []
    for i, land in enumerate(lands):
        half = land.shape[1] // 2
        rows = pl.ds(((1 - c) if sibling else c) * half, half)
        for k, (px, py) in enumerate(_other_chips(x, y)):
            part = land.at[2 * px + py, rows]
            out.append(pltpu.make_async_remote_copy(src_ref=part, dst_ref=part, send_sem=send.at[3 * i + k],
                                                    recv_sem=recv.at[3 * i + k],
                                                    device_id=(x, y, 1 - c) if sibling else (px, py, c), device_id_type=MESH))
    return out


def _own_copies(srcs, lands, sems):
    x, y, _ = _place()
    return [pltpu.make_async_copy(src, land.at[2 * x + y], sems.at[i]) for i, (src, land) in enumerate(zip(srcs, lands))]


def _gather_start(shards, after, *, name, own_too):
    n = len(shards)

    def body(*refs):
        srcs, lands, (send, recv, own), token = refs[:n], refs[n:2 * n], refs[2 * n:2 * n + 3], refs[-1]
        for cp in _gather_copies(srcs, lands, send, recv) + (_own_copies(srcs, lands, own) if own_too else []):
            cp.start()
        token[...] = jnp.zeros_like(token)

    lands = [lax.empty((N_CHIPS,) + s.shape, s.dtype) for s in shards]
    sems = pltpu.SemaphoreType.DMA((3 * n,))
    body, xs, xa = _after(body, 2 * n, after)
    out = pl.pallas_call(
        body, name=name, in_specs=[HBM_SPEC] * (2 * n) + xs,
        out_specs=[SEM_SPEC] * 3 + [HBM_SPEC] * (2 * n) + [VMEM_SPEC],
        out_shape=[sems, sems, pltpu.SemaphoreType.DMA((n,))] + [pltpu.HBM(a.shape, a.dtype) for a in list(shards) + lands]
        + [jax.ShapeDtypeStruct((SUBLANES, LANES), F32)],
        input_output_aliases={i: 3 + i for i in range(2 * n)},
        compiler_params=pltpu.CompilerParams(has_side_effects=DATAFLOW),
    )(*[_in_hbm(a) for a in list(shards) + lands], *xa)
    return (out[:3], out[3:3 + n], out[3 + n:3 + 2 * n], own_too), out[-1]


def _gather_wait(handle, after, *, name):
    sems, srcs, lands, own_too = handle
    n = len(srcs)

    def body(*refs):
        srcs_, lands_, (send, recv, own) = refs[:n], refs[n:2 * n], refs[2 * n:2 * n + 3]
        for cp in _gather_copies(srcs_, lands_, send, recv):
            cp.wait_send()
        for cp in _gather_arrivals(lands_, send, recv):
            cp.wait_recv()
        for cp in _own_copies(srcs_, lands_, own) if own_too else []:
            cp.wait()

    both = list(srcs) + list(lands)
    out = pl.pallas_call(
        body, name=name, in_specs=[HBM_SPEC] * (2 * n) + [SEM_SPEC] * 3 + [ANY_SPEC], out_specs=[HBM_SPEC] * (2 * n),
        out_shape=[pltpu.HBM(a.shape, a.dtype) for a in both],
        input_output_aliases={i: i for i in range(2 * n)},
        compiler_params=pltpu.CompilerParams(has_side_effects=DATAFLOW),
    )(*both, *sems, after)
    return out[n:]


def _fill_sibling(lands, *, name):
    n = len(lands)

    def body(*refs):
        ins, outs, send_sems, recv_sems = refs[:n], refs[n:2 * n], refs[2 * n], refs[2 * n + 1]
        cps = _gather_copies(ins, outs, send_sems, recv_sems, sibling=True)
        for cp in cps:
            cp.start()
        for cp in _gather_arrivals(outs, send_sems, recv_sems, sibling=True):
            cp.wait_recv()
        for cp in cps:
            cp.wait_send()

    return pl.pallas_call(
        body, name=name, in_specs=[HBM_SPEC] * n, out_specs=[HBM_SPEC] * n,
        out_shape=[jax.ShapeDtypeStruct(a.shape, a.dtype) for a in lands],
        scratch_shapes=[pltpu.SemaphoreType.DMA((3 * n,)), pltpu.SemaphoreType.DMA((3 * n,))],
        input_output_aliases={i: i for i in range(n)},
    )(*lands)


def _scatter_copies(src, land, send, recv):
    x, y, c = _place()
    return [pltpu.make_async_remote_copy(src_ref=src[i].at[2 * px + py], dst_ref=land[i].at[k], send_sem=send.at[3 * i + k],
                                         recv_sem=recv.at[3 * i + k], device_id=(px, py, c), device_id_type=MESH)
            for i in range(len(src)) for k, (px, py) in enumerate(_other_chips(x, y))]


def _scatter_start(pieces, *, name):
    n = len(pieces)

    def body(*refs):
        src, land, send, recv, token = refs[:n], refs[n:2 * n], refs[2 * n], refs[2 * n + 1], refs[-1]
        for cp in _scatter_copies(src, land, send, recv):
            cp.start()
        token[...] = jnp.zeros_like(token)

    lands = [lax.empty((3,) + p.shape[1:], p.dtype) for p in pieces]
    sems = pltpu.SemaphoreType.DMA((3 * n,))
    out = pl.pallas_call(
        body, name=name, in_specs=[HBM_SPEC] * (2 * n),
        out_specs=[SEM_SPEC, SEM_SPEC] + [HBM_SPEC] * (2 * n) + [VMEM_SPEC],
        out_shape=[sems, sems] + [pltpu.HBM(a.shape, a.dtype) for a in pieces + lands]
        + [jax.ShapeDtypeStruct((SUBLANES, LANES), F32)],
        input_output_aliases={i: 2 + i for i in range(2 * n)},
        compiler_params=pltpu.CompilerParams(has_side_effects=DATAFLOW),
    )(*[_in_hbm(a) for a in pieces + lands])
    return (out[0], out[1], out[2:2 + n], out[2 + n:2 + 2 * n]), out[-1]


def _scatter_wait(handle, after, *, name):
    send_sems, recv_sems, srcs, lands = handle
    n = len(srcs)

    def body(*refs):
        src, land, send, recv = refs[:n], refs[n:2 * n], refs[2 * n], refs[2 * n + 1]
        for cp in _scatter_copies(src, land, send, recv):
            cp.wait_send()
            cp.wait_recv()

    both = list(srcs) + list(lands)
    out = pl.pallas_call(
        body, name=name, in_specs=[HBM_SPEC] * (2 * n) + [SEM_SPEC, SEM_SPEC, ANY_SPEC], out_specs=[HBM_SPEC] * (2 * n),
        out_shape=[pltpu.HBM(a.shape, a.dtype) for a in both],
        input_output_aliases={i: i for i in range(2 * n)},
        compiler_params=pltpu.CompilerParams(has_side_effects=DATAFLOW),
    )(*both, send_sems, recv_sems, after)
    return out[n:]


def _to_bf16(x, *, name, after=None):
    T, Dm = x.shape
    tr = _tile(T, 512, 2 * SUBLANES)

    def body(x_ref, o_ref):
        o_ref[...] = x_ref[...].astype(BF16)

    blk = pl.BlockSpec((tr, Dm), lambda i: (i, 0))
    body, xs, xa = _after(body, 1, after)
    return pl.pallas_call(
        body, name=name, grid=(T // tr,), in_specs=[blk] + xs, out_specs=blk, out_shape=jax.ShapeDtypeStruct((T, Dm), BF16),
        compiler_params=_params(("parallel",)),
    )(x, *xa)


def _chip_sum(pieces, got, chip, *, name):
    _, R, Cc = pieces.shape
    tr = _tile(R, 512, 2 * SUBLANES)

    def body(chip_ref, a_ref, g_ref, o_ref):
        o_ref[...] = ((a_ref[...] + g_ref[0].astype(F32)) + g_ref[1].astype(F32)) + g_ref[2].astype(F32)

    return pl.pallas_call(
        body, name=name,
        grid_spec=pltpu.PrefetchScalarGridSpec(
            num_scalar_prefetch=1, grid=(R // tr,),
            in_specs=[pl.BlockSpec((None, tr, Cc), lambda i, ch: (ch[0], i, 0)),
                      pl.BlockSpec((3, tr, Cc), lambda i, ch: (0, i, 0))],
            out_specs=pl.BlockSpec((tr, Cc), lambda i, ch: (i, 0))),
        out_shape=jax.ShapeDtypeStruct((R, Cc), F32),
        compiler_params=_params(("parallel",)),
    )(chip, pieces, got)


PACK_COLS = 1024
SMALL_ROWS = 32


def kernel(x, hgrn_w_in, hgrn_lb_logits, hgrn_gnorm_w, hgrn_w_out, swa_w_q, swa_sinks, swa_w_out, shared_w_kv, rel_bias, ffn_w_in, ffn_conv_w, ffn_conv_b, ffn_w_out, ln_mix_g, ln_mix_b, ln_ffn_g, ln_ffn_b, loss_target, m_hgrn_w_in, m_hgrn_lb_logits, m_hgrn_gnorm_w, m_hgrn_w_out, m_swa_w_q, m_swa_sinks, m_swa_w_out, m_shared_w_kv, m_rel_bias, m_ffn_w_in, m_ffn_conv_w, m_ffn_conv_b, m_ffn_w_out, m_ln_mix_g, m_ln_mix_b, m_ln_ffn_g, m_ln_ffn_b, v_hgrn_w_in, v_hgrn_lb_logits, v_hgrn_gnorm_w, v_hgrn_w_out, v_swa_w_q, v_swa_sinks, v_swa_w_out, v_shared_w_kv, v_rel_bias, v_ffn_w_in, v_ffn_conv_w, v_ffn_conv_b, v_ffn_w_out, v_ln_mix_g, v_ln_mix_b, v_ln_ffn_g, v_ln_ffn_b):
    xi, yi, ci = _place()
    chip = 2 * xi + yi
    Dm = D_MODEL
    FC = 2 * FFN_DIM // N_CHIPS
    Fo = FFN_DIM // N_CHIPS
    Dq = Dm // N_CHIPS
    bf = lambda a: a.astype(BF16)

    small = jnp.concatenate([hgrn_lb_logits.reshape(-1), ffn_conv_w.reshape(-1)])
    n_small = small.shape[0]
    bits = jnp.concatenate(_split3(small))
    bits = jnp.pad(bits, (0, SMALL_ROWS * PACK_COLS - 3 * n_small)).reshape(SMALL_ROWS, PACK_COLS)
    groups = [[bf(hgrn_w_in[0]), bf(hgrn_w_out[0]), bits],
              [bf(swa_w_q[0]), bf(swa_w_out[0]), bf(shared_w_kv), bf(ffn_w_in[0]), bf(ffn_w_out[0])],
              [bf(ffn_w_in[1]), bf(ffn_w_out[1])]]

    def gathered(k, landed):
        lands = _fill_sibling(landed, name=f"gather_w{k}_fill")
        if k > 0:
            return lands
        return [lax.dynamic_update_slice(land, shard[None], (chip,) + (0,) * shard.ndim)
                for land, shard in zip(lands, groups[0])]

    handle0, token0 = _gather_start(groups[0], None, name="gather_w0_start", own_too=False)
    xb = _to_bf16(x[0], name="x_to_bf16", after=token0)
    corner = lambda a: a[:2 * SUBLANES, :LANES]
    casts_done = corner(xb) + sum(corner(a) for a in groups[1] + groups[2])
    w_in, w_hg_out, small_all = gathered(0, _gather_wait(handle0, casts_done, name="gather_w0_wait"))
    handle1, token1 = _gather_start(groups[1], w_in, name="gather_w1_start", own_too=True)
    parts = small_all.reshape(N_CHIPS, -1)[:, :3 * n_small].reshape(N_CHIPS, 3, n_small).astype(F32)
    vals = (parts[:, 0] + parts[:, 1]) + parts[:, 2]
    lb_full = vals[:, :2 * Dq].reshape(N_CHIPS, 2, Dq).transpose(1, 0, 2).reshape(2, Dm)
    cw_full = vals[:, 2 * Dq:].reshape(N_CHIPS, DEPTH, 3, FC).transpose(1, 2, 0, 3).reshape(DEPTH, 3, 2 * FFN_DIM)

    got = {"handle": handle1}

    def more_weights(k, after):
        ws = gathered(k, _gather_wait(got.pop("handle"), after, name=f"gather_w{k}_wait"))
        if k == 1:
            got["handle"], token2 = _gather_start(groups[2], ws[0], name="gather_w2_start", own_too=True)
            w_q, w_o, w_kv, w_fi, w_fo = ws
            got.update(ffn_in={0: w_fi}, ffn_out={0: w_fo.reshape(FFN_DIM, Dm)})
            return {"sw_q": w_q.reshape(Dm, Dm), "sw_out": w_o.reshape(Dm, Dm), "kv": w_kv.reshape(Dm, 2 * KV_DIM),
                    "token": token2, "ffn_in": got["ffn_in"], "ffn_out": got["ffn_out"]}
        w_fi, w_fo = ws
        return {"ffn_in": {**got["ffn_in"], 1: w_fi}, "ffn_out": {**got["ffn_out"], 1: w_fo.reshape(FFN_DIM, Dm)}}

    w = {
        "hg_in": w_in, "hg_out": w_hg_out.reshape(Dm, Dm), "token": token1,
        "lb_logits": lb_full, "gnorm": hgrn_gnorm_w, "sinks": swa_sinks, "rel_bias": rel_bias,
        "conv_w_a": [cw_full[l, :, :FFN_DIM] for l in range(DEPTH)],
        "conv_w_b": [cw_full[l, :, FFN_DIM:] for l in range(DEPTH)],
        "conv_b_a": [ffn_conv_b[l:l + 1, :FFN_DIM] for l in range(DEPTH)],
        "conv_b_b": [ffn_conv_b[l:l + 1, FFN_DIM:] for l in range(DEPTH)],
        "ln_mix_g": [ln_mix_g[l:l + 1] for l in range(DEPTH)], "ln_mix_b": [ln_mix_b[l:l + 1] for l in range(DEPTH)],
        "ln_ffn_g": [ln_ffn_g[l:l + 1] for l in range(DEPTH)], "ln_ffn_b": [ln_ffn_b[l:l + 1] for l in range(DEPTH)],
    }

    sent = {}

    def emit(k, gd):
        rows4 = lambda a: a.reshape(N_CHIPS, -1, a.shape[-1])
        order = {1: ["sw_q", "sw_out", "kv", "ffn_in", "ffn_out"], 2: ["ffn_in", "ffn_out", "hg_out"], 3: ["hg_in"]}[k]
        as_pieces = lambda a, nme: a if nme in ("ffn_in", "hg_in") else rows4(a)
        handle, token = _scatter_start([as_pieces(gd[nme][1], nme) for nme in order], name=f"scatter_g{k}_start")
        sent[k] = (handle, [as_pieces(gd[nme][0], nme) for nme in order])
        return token

    loss_tile, grad_x, g = _local_step(x[0], xb, loss_target[0], w, more_weights, emit)

    wts = dict(hgrn_w_in=hgrn_w_in, hgrn_lb_logits=hgrn_lb_logits, hgrn_gnorm_w=hgrn_gnorm_w, hgrn_w_out=hgrn_w_out,
               swa_w_q=swa_w_q, swa_sinks=swa_sinks, swa_w_out=swa_w_out, shared_w_kv=shared_w_kv, rel_bias=rel_bias,
               ffn_w_in=ffn_w_in, ffn_conv_w=ffn_conv_w, ffn_conv_b=ffn_conv_b, ffn_w_out=ffn_w_out,
               ln_mix_g=ln_mix_g, ln_mix_b=ln_mix_b, ln_ffn_g=ln_ffn_g, ln_ffn_b=ln_ffn_b)
    ms = dict(hgrn_w_in=m_hgrn_w_in, hgrn_lb_logits=m_hgrn_lb_logits, hgrn_gnorm_w=m_hgrn_gnorm_w, hgrn_w_out=m_hgrn_w_out,
              swa_w_q=m_swa_w_q, swa_sinks=m_swa_sinks, swa_w_out=m_swa_w_out, shared_w_kv=m_shared_w_kv, rel_bias=m_rel_bias,
              ffn_w_in=m_ffn_w_in, ffn_conv_w=m_ffn_conv_w, ffn_conv_b=m_ffn_conv_b, ffn_w_out=m_ffn_w_out,
              ln_mix_g=m_ln_mix_g, ln_mix_b=m_ln_mix_b, ln_ffn_g=m_ln_ffn_g, ln_ffn_b=m_ln_ffn_b)
    vs = dict(hgrn_w_in=v_hgrn_w_in, hgrn_lb_logits=v_hgrn_lb_logits, hgrn_gnorm_w=v_hgrn_gnorm_w, hgrn_w_out=v_hgrn_w_out,
              swa_w_q=v_swa_w_q, swa_sinks=v_swa_sinks, swa_w_out=v_swa_w_out, shared_w_kv=v_shared_w_kv, rel_bias=v_rel_bias,
              ffn_w_in=v_ffn_w_in, ffn_conv_w=v_ffn_conv_w, ffn_conv_b=v_ffn_conv_b, ffn_w_out=v_ffn_w_out,
              ln_mix_g=v_ln_mix_g, ln_mix_b=v_ln_mix_b, ln_ffn_g=v_ln_ffn_g, ln_ffn_b=v_ln_ffn_b)
    names = list(wts)
    grads, delta, new_m, new_v = {}, {}, {}, {}

    def update(n, ga, gb, layer=None, prev=None):
        r2 = lambda a: a.reshape(-1, a.shape[-1])
        rows = None if layer is None else (layer * ga.shape[0], ga.shape[0])
        return _adamw(r2(wts[n]), ga, gb, r2(ms[n]), r2(vs[n]), rows=rows, prev=prev,
                      name=f"adamw_{n}" + ("" if layer is None else f"_{layer}"))

    def keep(n, res):
        grads[n], delta[n], new_m[n], new_v[n] = [a.reshape(wts[n].shape) for a in res]

    chip1 = jnp.reshape(chip, (1,)).astype(jnp.int32)
    after, swaps = grad_x, {}
    for k in (1, 2, 3):
        handle, pieces = sent[k]
        lands = _scatter_wait(handle, after, name=f"scatter_g{k}_wait")
        parts = [_chip_sum(p, l, chip1, name=f"scatter_g{k}_sum{i}") for i, (p, l) in enumerate(zip(pieces, lands))]
        swaps[k], after = _swap_start(parts, name=f"scatter_g{k}_swap_start")
    for k in (1, 2, 3):
        parts, sibs = _swap_wait(swaps[k], after, name=f"scatter_g{k}_swap_wait")
        if k == 1:
            for n, ga, gb in zip(["swa_w_q", "swa_w_out", "shared_w_kv"], parts[:3], sibs[:3]):
                keep(n, update(n, ga, gb))
            ffn_in_1 = update("ffn_w_in", parts[3], sibs[3], layer=1)
            ffn_out_1 = update("ffn_w_out", parts[4], sibs[4], layer=1)
            after = ffn_out_1[3]
        elif k == 2:
            keep("ffn_w_in", update("ffn_w_in", parts[0], sibs[0], layer=0, prev=ffn_in_1))
            keep("ffn_w_out", update("ffn_w_out", parts[1], sibs[1], layer=0, prev=ffn_out_1))
            keep("hgrn_w_out", update("hgrn_w_out", parts[2], sibs[2]))
            after = new_v["hgrn_w_out"]
        else:
            keep("hgrn_w_in", update("hgrn_w_in", parts[0], sibs[0]))

    small_keys = ["lb_logits", "gnorm", "sinks", "rel_bias", "conv0", "conv1", "ln_mix0", "ln_mix1", "ln_ffn0", "ln_ffn1"]
    flat2 = lambda a: a.reshape(-1, a.shape[-1])
    sums = _sum8([loss_tile] + [flat2(g[k]) for k in small_keys], name="sum_small")
    loss = sums[0][0, 0]
    sg = {k: v.reshape(g[k].shape) for k, v in zip(small_keys, sums[1:])}
    conv = [sg["conv0"], sg["conv1"]]
    g_cw = jnp.stack([jnp.concatenate([conv[l][0, :3], conv[l][1, :3]], axis=1) for l in range(DEPTH)])
    g_cb = jnp.stack([jnp.concatenate([conv[l][0, 3], conv[l][1, 3]], axis=0) for l in range(DEPTH)])
    ln = lambda nme, r: jnp.stack([sg[nme + "0"][r], sg[nme + "1"][r]])
    small_g = dict(hgrn_lb_logits=lax.dynamic_slice_in_dim(sg["lb_logits"], chip * Dq, Dq, axis=1),
                   hgrn_gnorm_w=sg["gnorm"], swa_sinks=sg["sinks"], rel_bias=sg["rel_bias"],
                   ffn_conv_w=lax.dynamic_slice_in_dim(g_cw, chip * FC, FC, axis=2), ffn_conv_b=g_cb,
                   ln_mix_g=ln("ln_mix", 0), ln_mix_b=ln("ln_mix", 1), ln_ffn_g=ln("ln_ffn", 0), ln_ffn_b=ln("ln_ffn", 1))
    small_names = list(small_g)
    d_, m_, v_ = _adamw_small([flat2(wts[n]) for n in small_names], [flat2(small_g[n]) for n in small_names],
                              [flat2(ms[n]) for n in small_names], [flat2(vs[n]) for n in small_names], name="adamw_small")
    for n, a, b_, c_ in zip(small_names, d_, m_, v_):
        shp = wts[n].shape
        grads[n], delta[n], new_m[n], new_v[n] = small_g[n], a.reshape(shp), b_.reshape(shp), c_.reshape(shp)

    return (loss, grad_x[None], *[grads[n] for n in names], *[delta[n] for n in names],
            *[new_m[n] for n in names], *[new_v[n] for n in names])
```

```python
import math

import numpy as np
import jax
import jax.numpy as jnp
from jax import lax
from jax.experimental import pallas as pl
from jax.experimental.pallas import tpu as pltpu

F32 = jnp.float32
BF16 = jnp.bfloat16
MESH = pl.DeviceIdType.MESH

D_MODEL = 1024
DEPTH = 2
HG_HEADS = 8
HG_DIM = 128
SW_Q_HEADS = 16
SW_KV_HEADS = 4
SW_HEAD_DIM = 64
SW_GROUP = 4
SW_WINDOW = 128
REL_BUCKETS = 32
REL_MAX_DIST = 128
FFN_DIM = 2816
ALPHA = (2.0 * DEPTH) ** 0.25
LN_EPS = 1e-5
RMS_EPS = 1e-6
ADAM_LR = 0.001
ADAM_B1 = 0.9
ADAM_B2 = 0.999
ADAM_EPS = 1e-08
ADAM_WD = 0.01
ADAM_STEP = 10

VMEM_BYTES_V7X = 64 * 1024 * 1024
VMEM_LIMIT = VMEM_BYTES_V7X - 8 * 1024 * 1024
LANES = 128
SUBLANES = 8

HG_C = 64
HG_RB = 256
CONV_R = 128
N_CHIPS = 4
N_DEV = 8

ANY_SPEC = pl.BlockSpec(memory_space=pl.ANY)


def _after(body, n_in, after):
    if after is None:
        return body, [], ()

    def wrapped(*refs):
        return body(*refs[:n_in], *refs[n_in + 1:])

    return wrapped, [ANY_SPEC], (after,)


def _params(sem=None):
    return pltpu.CompilerParams(dimension_semantics=sem, vmem_limit_bytes=VMEM_LIMIT)


def _tile(n, pref, unit=LANES):
    if n <= pref:
        return n
    best = None
    for t in range(unit, pref + 1, unit):
        if n % t == 0:
            best = t
    assert best is not None, (n, pref, unit)
    return best


def _dot(a, b, ca, cb):
    nb = a.ndim - 2
    batch = tuple(range(nb))
    return lax.dot_general(a.astype(BF16), b.astype(BF16), (((nb + ca,), (nb + cb,)), (batch, batch)),
                           preferred_element_type=F32)


@jax.custom_vjp
def mm(a, b):
    return _dot(a, b, 1, 0)


@jax.custom_vjp
def mm_nt(a, b):
    return _dot(a, b, 1, 1)


@jax.custom_vjp
def mm_tn(a, b):
    return _dot(a, b, 0, 0)


mm.defvjp(lambda a, b: (mm(a, b), (a, b)), lambda r, ct: (mm_nt(ct, r[1]), mm_tn(r[0], ct)))
mm_nt.defvjp(lambda a, b: (mm_nt(a, b), (a, b)), lambda r, ct: (mm(ct, r[1]), mm_tn(ct, r[0])))
mm_tn.defvjp(lambda a, b: (mm_tn(a, b), (a, b)), lambda r, ct: (mm_nt(r[1], ct), mm(r[0], ct)))


def _split2(x):
    hi = x.astype(BF16)
    return hi, (x - hi.astype(F32)).astype(BF16)


@jax.custom_vjp
def _scores(qt, kt):
    return _dot(qt, kt, 1, 1)


def _scores_bwd(r, ct):
    (qh, ql), (kh, kl) = _split2(r[0]), _split2(r[1])
    return _dot(ct, kh, 1, 0) + _dot(ct, kl, 1, 0), _dot(ct, qh, 0, 0) + _dot(ct, ql, 0, 0)


_scores.defvjp(lambda a, b: (_scores(a, b), (a, b)), _scores_bwd)


def _split3(x):
    hi = x.astype(BF16)
    r1 = x - hi.astype(F32)
    mid = r1.astype(BF16)
    lo = (r1 - mid.astype(F32)).astype(BF16)
    return hi, mid, lo


def _cumsum_impl(x):
    ax = x.ndim - 2
    n = x.shape[ax]
    row = lax.broadcasted_iota(jnp.int32, x.shape, ax)
    d = 1
    while d < n:
        x = x + jnp.where(row >= d, pltpu.roll(x, d, ax), 0.0)
        d *= 2
    return x


def _cumsum_rev_impl(x):
    ax = x.ndim - 2
    n = x.shape[ax]
    row = lax.broadcasted_iota(jnp.int32, x.shape, ax)
    d = 1
    while d < n:
        x = x + jnp.where(row < n - d, pltpu.roll(x, n - d, ax), 0.0)
        d *= 2
    return x


@jax.custom_vjp
def _cumsum(x):
    return _cumsum_impl(x)


_cumsum.defvjp(lambda x: (_cumsum_impl(x), None), lambda _, ct: (_cumsum_rev_impl(ct),))


def _matmul(a, b, *, mode, name, out_dtype=F32, add=None, add_scale=1.0, tm=512, tn=1408, tk=1408, after=None,
            planes=None, also_bf16=False):
    assert planes in (None, "n")
    P = b.shape[0] if planes else 1
    a2, b2 = a.shape, b.shape[-2:]
    (M, K) = a2 if mode[0] == "n" else a2[::-1]
    (K2, N) = b2 if mode[1] == "n" else b2[::-1]
    assert K == K2, (a.shape, b.shape, mode)
    assert b.ndim == (3 if planes else 2)
    tm, tn, tk = _tile(M, tm), _tile(N, tn), _tile(K, tk)
    nj, nk = N // tn, K // tk
    ca, cb = (1 if mode[0] == "n" else 0), (0 if mode[1] == "n" else 1)
    a_blk, a_idx = ((tk, tm), lambda i, k: (k, i)) if mode[0] == "t" else ((tm, tk), lambda i, k: (i, k))
    b_blk, b_idx = ((tn, tk), lambda k, j: (j, k)) if mode[1] == "t" else ((tk, tn), lambda k, j: (k, j))
    a_spec = pl.BlockSpec(a_blk, lambda i, j, k: a_idx(i, k))
    if planes:
        b_spec = pl.BlockSpec((None,) + b_blk, lambda i, j, k: (j // nj,) + b_idx(k, j % nj))
        o_spec, out_shape = pl.BlockSpec((None, tm, tn), lambda i, j, k: (j // nj, i, j % nj)), (P, M, N)
    else:
        b_spec = pl.BlockSpec(b_blk, lambda i, j, k: b_idx(k, j))
        o_spec, out_shape = pl.BlockSpec((tm, tn), lambda i, j, k: (i, j)), (M, N)
    has_add = add is not None
    assert not (has_add and planes)

    def finish(r, add_ref, o_refs):
        if has_add:
            r = r + add_scale * add_ref[...]
        o_refs[0][...] = r.astype(out_dtype)
        if also_bf16:
            o_refs[1][...] = r.astype(BF16)

    def body(*refs):
        a_ref, b_ref = refs[:2]
        add_ref = refs[2] if has_add else None
        first = 3 if has_add else 2
        o_ref = refs[first:first + (2 if also_bf16 else 1)]
        if nk == 1:
            finish(_dot(a_ref[...], b_ref[...], ca, cb), add_ref, o_ref)
            return
        acc_ref = refs[-1]
        k = pl.program_id(2)

        @pl.when(k == 0)
        def _():
            acc_ref[...] = jnp.zeros_like(acc_ref)

        acc_ref[...] += _dot(a_ref[...], b_ref[...], ca, cb)

        @pl.when(k == nk - 1)
        def _():
            finish(acc_ref[...], add_ref, o_ref)

    in_specs = [a_spec, b_spec] + ([o_spec] if has_add else [])
    args = (a, b) + ((add,) if has_add else ())
    body, xs, xa = _after(body, len(args), after)
    in_specs, args = in_specs + xs, args + xa
    out_shapes = [jax.ShapeDtypeStruct(out_shape, out_dtype)] + ([jax.ShapeDtypeStruct(out_shape, BF16)] if also_bf16 else [])
    out = pl.pallas_call(
        body, name=name, grid=(M // tm, nj * (P if planes == "n" else 1), nk), in_specs=in_specs,
        out_specs=[o_spec] * len(out_shapes), out_shape=out_shapes,
        scratch_shapes=[pltpu.VMEM((tm, tn), F32)] if nk > 1 else [],
        compiler_params=_params(("parallel", "parallel", "arbitrary")),
    )(*args)
    return tuple(out) if also_bf16 else out[0]


def _matmul_planes_nn(a, b, *, name, tm=512, after=None):
    (M, K), (P, K2, N) = a.shape, b.shape
    assert K == K2
    tm = _tile(M, tm, 2 * SUBLANES)

    def body(a_ref, b_ref, o_ref):
        for p in range(P):
            o_ref[p] = _dot(a_ref[...], b_ref[p], 1, 0).astype(BF16)

    body, xs, xa = _after(body, 2, after)
    return pl.pallas_call(
        body, name=name, grid=(M // tm,),
        in_specs=[pl.BlockSpec((tm, K), lambda i: (i, 0)), pl.BlockSpec((P, K, N), lambda i: (0, 0, 0))] + xs,
        out_specs=pl.BlockSpec((P, tm, N), lambda i: (0, i, 0)), out_shape=jax.ShapeDtypeStruct((P, M, N), BF16),
        compiler_params=_params(("parallel",)),
    )(a, b, *xa)


def _matmul_planes_nt(a, b, add, *, add_scale, name, tm=512, after=None):
    (P, M, K), (P2, N, K2) = a.shape, b.shape
    assert P == P2 and K == K2 and add.shape == (M, N)
    tm = _tile(M, tm, SUBLANES)

    def body(a_ref, b_ref, add_ref, o_ref):
        r = add_scale * add_ref[...]
        for p in range(P):
            r = r + _dot(a_ref[p], b_ref[p], 1, 1)
        o_ref[...] = r

    row = pl.BlockSpec((tm, N), lambda i: (i, 0))
    body, xs, xa = _after(body, 3, after)
    return pl.pallas_call(
        body, name=name, grid=(M // tm,),
        in_specs=[pl.BlockSpec((P, tm, K), lambda i: (0, i, 0)), pl.BlockSpec((P, N, K), lambda i: (0, 0, 0)), row] + xs,
        out_specs=row, out_shape=jax.ShapeDtypeStruct((M, N), F32),
        compiler_params=_params(("parallel",)),
    )(a, b, add, *xa)


def _ln(z, g, b):
    mu = jnp.mean(z, axis=-1, keepdims=True)
    zc = z - mu
    var = jnp.mean(zc * zc, axis=-1, keepdims=True)
    return zc * lax.rsqrt(var + LN_EPS) * g + b


def _matmul_ln(a, b, h, g, bias, *, name, tgt=None, tm=512, a_t=False):
    (T, K), (K2, Dm) = (a.shape[::-1] if a_t else a.shape), b.shape
    assert K == K2 and h.shape == (T, Dm)
    tm = _tile(T, tm, SUBLANES)
    last = tgt is not None

    def body(*refs):
        a_ref, b_ref, h_ref, g_ref, bias_ref = refs[:5]
        z = ALPHA * h_ref[...] + _dot(a_ref[...], b_ref[...], 0 if a_t else 1, 0)
        if not last:
            z_ref, y_ref, yb_ref = refs[5:]
            y = _ln(z, g_ref[...], bias_ref[...])
            z_ref[...] = z
            y_ref[...] = y
            yb_ref[...] = y.astype(BF16)
            return
        t_ref, dz_ref, dzb_ref, dgb_ref, l_ref, da_ref = refs[5:]

        @pl.when(pl.program_id(0) == 0)
        def _():
            dgb_ref[...] = jnp.zeros_like(dgb_ref)
            l_ref[...] = jnp.zeros_like(l_ref)

        y, vjp = jax.vjp(_ln, z, g_ref[...], bias_ref[...])
        e = y - t_ref[...]
        dz, dg, db = vjp(e * (1.0 / Dm))
        l_ref[...] += 0.5 * jnp.sum(jnp.mean(e * e, axis=-1, keepdims=True), axis=0, keepdims=True)
        dzb = dz.astype(BF16)
        dz_ref[...] = dz
        dzb_ref[...] = dzb
        dgb_ref[...] += jnp.concatenate([dg, db], axis=0)
        da_ref[...] = _dot(dzb, b_ref[...], 1, 1).astype(BF16)

    row = pl.BlockSpec((tm, Dm), lambda i: (i, 0))
    vec = pl.BlockSpec((1, Dm), lambda i: (0, 0))
    a_spec = pl.BlockSpec((K, tm), lambda i: (0, i)) if a_t else pl.BlockSpec((tm, K), lambda i: (i, 0))
    in_specs = [a_spec, pl.BlockSpec((K, Dm), lambda i: (0, 0)), row, vec, vec]
    f32, b16 = jax.ShapeDtypeStruct((T, Dm), F32), jax.ShapeDtypeStruct((T, Dm), BF16)
    if not last:
        return pl.pallas_call(
            body, name=name, grid=(T // tm,), in_specs=in_specs, out_specs=[row, row, row], out_shape=[f32, f32, b16],
            compiler_params=_params(("parallel",)),
        )(a, b, h, g, bias)
    assert not a_t
    return pl.pallas_call(
        body, name=name, grid=(T // tm,), in_specs=in_specs + [row],
        out_specs=[row, row, pl.BlockSpec((2, Dm), lambda i: (0, 0)), pl.BlockSpec((SUBLANES, LANES), lambda i: (0, 0)), a_spec],
        out_shape=[f32, b16, jax.ShapeDtypeStruct((2, Dm), F32), jax.ShapeDtypeStruct((SUBLANES, LANES), F32),
                   jax.ShapeDtypeStruct((T, K), BF16)],
        compiler_params=_params(("arbitrary",)),
    )(a, b, h, g, bias, tgt)


def _ln_bwd_matmul(dy, z, g, b, w, *, name, out_t=False, tm=512, after=None):
    T, Dm = z.shape
    N = w.shape[0]
    tm = _tile(T, tm, LANES if out_t else SUBLANES)

    def body(dy_ref, z_ref, g_ref, b_ref, w_ref, dz_ref, dzb_ref, dgb_ref, o_ref):
        @pl.when(pl.program_id(0) == 0)
        def _():
            dgb_ref[...] = jnp.zeros_like(dgb_ref)

        _, vjp = jax.vjp(_ln, z_ref[...], g_ref[...], b_ref[...])
        dz, dg, db = vjp(dy_ref[...])
        dzb = dz.astype(BF16)
        dz_ref[...] = dz
        dzb_ref[...] = dzb
        dgb_ref[...] += jnp.concatenate([dg, db], axis=0)
        o_ref[...] = (_dot(w_ref[...], dzb, 1, 1) if out_t else _dot(dzb, w_ref[...], 1, 1)).astype(BF16)

    row = pl.BlockSpec((tm, Dm), lambda i: (i, 0))
    vec = pl.BlockSpec((1, Dm), lambda i: (0, 0))
    o_spec = pl.BlockSpec((N, tm), lambda i: (0, i)) if out_t else pl.BlockSpec((tm, N), lambda i: (i, 0))
    body, xs, xa = _after(body, 5, after)
    return pl.pallas_call(
        body, name=name, grid=(T // tm,), in_specs=[row, row, vec, vec, pl.BlockSpec((N, Dm), lambda i: (0, 0))] + xs,
        out_specs=[row, row, pl.BlockSpec((2, Dm), lambda i: (0, 0)), o_spec],
        out_shape=[jax.ShapeDtypeStruct((T, Dm), F32), jax.ShapeDtypeStruct((T, Dm), BF16),
                   jax.ShapeDtypeStruct((2, Dm), F32), jax.ShapeDtypeStruct((N, T) if out_t else (T, N), BF16)],
        compiler_params=_params(("arbitrary",)),
    )(dy, z, g, b, w, *xa)


def _hg_chunk(qr, fr, ir, gr, l0, l1, gw, st):
    C = qr.shape[-2]
    row = lax.broadcasted_iota(jnp.int32, qr.shape, qr.ndim - 2)
    lb = jax.nn.sigmoid(l0 - l1)
    fg = lb + (1.0 - lb) * jax.nn.sigmoid(fr)
    b = _cumsum(jnp.log(fg))
    q = jax.nn.silu(qr)
    k = 1.0 - fg
    bmid = lax.stop_gradient(jnp.sum(jnp.where(row == C // 2 - 1, b, 0.0), axis=-2, keepdims=True))
    bl = jnp.sum(jnp.where(row == C - 1, b, 0.0), axis=-2, keepdims=True)
    o = mm_nt(q * jnp.exp(b), st)
    sc = _scores(q * jnp.exp(b - bmid), k * jnp.exp(bmid - b))
    ti = lax.broadcasted_iota(jnp.int32, (C, C), 0)
    si = lax.broadcasted_iota(jnp.int32, (C, C), 1)
    sc = jnp.where(si <= ti, sc, 0.0)
    o = o + mm(sc, ir)
    st_new = st * jnp.exp(bl) + mm_tn(ir, k * jnp.exp(bl - b))
    on = o * lax.rsqrt(jnp.mean(o * o, axis=-1, keepdims=True) + RMS_EPS)
    return on * gw * jax.nn.silu(gr), st_new


def _heads(ref, rows):
    return jnp.stack([ref[rows, h * HG_DIM:(h + 1) * HG_DIM].astype(F32) for h in range(HG_HEADS)])


def _unheads(x):
    return jnp.concatenate([x[h] for h in range(HG_HEADS)], axis=-1)


def _hgrn_fwd(pre, lbl, gw, *, name):
    _, T, Dm = pre.shape
    rb = min(HG_RB, T)
    C = min(HG_C, rb)
    ncb = rb // C

    def body(pre_ref, lbl_ref, gw_ref, o_ref, st_ref, s_ref):
        @pl.when(pl.program_id(0) == 0)
        def _():
            s_ref[...] = jnp.zeros_like(s_ref)

        def chunk(ci, carry):
            r0 = pl.multiple_of(ci * C, C)
            rows = pl.ds(r0, C)
            st = s_ref[...]
            st_ref[ci] = st
            out, st_new = _hg_chunk(*[_heads(pre_ref.at[j], rows) for j in range(4)],
                                    _heads(lbl_ref, slice(0, 1)), _heads(lbl_ref, slice(1, 2)), gw_ref[...], st)
            o_ref[rows, :] = _unheads(out).astype(BF16)
            s_ref[...] = st_new
            return carry

        lax.fori_loop(0, ncb, chunk, 0, unroll=True)

    row = pl.BlockSpec((rb, Dm), lambda n: (n, 0))
    return pl.pallas_call(
        body, name=name, grid=(T // rb,),
        in_specs=[pl.BlockSpec((4, rb, Dm), lambda n: (0, n, 0)), pl.BlockSpec((2, Dm), lambda n: (0, 0)),
                  pl.BlockSpec((1, HG_DIM), lambda n: (0, 0))],
        out_specs=[row, pl.BlockSpec((ncb, HG_HEADS, HG_DIM, HG_DIM), lambda n: (n, 0, 0, 0))],
        out_shape=[jax.ShapeDtypeStruct((T, Dm), BF16),
                   jax.ShapeDtypeStruct((T // C, HG_HEADS, HG_DIM, HG_DIM), F32)],
        scratch_shapes=[pltpu.VMEM((HG_HEADS, HG_DIM, HG_DIM), F32)],
        compiler_params=_params(("arbitrary",)),
    )(pre, lbl, gw)


def _hgrn_bwd(pre, lbl, gw, states, dout, *, name, after=None):
    _, T, Dm = pre.shape
    rb = min(HG_RB, T)
    C = min(HG_C, rb)
    ncb = rb // C
    nb = T // rb

    def body(pre_ref, lbl_ref, gw_ref, st_ref, do_ref, dpre_ref, dlbl_ref, dgw_ref, ds_ref):
        @pl.when(pl.program_id(0) == 0)
        def _():
            ds_ref[...] = jnp.zeros_like(ds_ref)
            dlbl_ref[...] = jnp.zeros_like(dlbl_ref)
            dgw_ref[...] = jnp.zeros_like(dgw_ref)

        def chunk(cj, carry):
            ci = ncb - 1 - cj
            r0 = pl.multiple_of(ci * C, C)
            rows = pl.ds(r0, C)
            _, vjp = jax.vjp(_hg_chunk, *[_heads(pre_ref.at[j], rows) for j in range(4)],
                             _heads(lbl_ref, slice(0, 1)), _heads(lbl_ref, slice(1, 2)), gw_ref[...], st_ref[ci])
            *dpre, dl0, dl1, dgw, dst = vjp((_heads(do_ref, rows), ds_ref[...]))
            for j in range(4):
                dpre_ref[j, rows, :] = _unheads(dpre[j]).astype(BF16)
            dlbl_ref[0:1, :] += _unheads(dl0)
            dlbl_ref[1:2, :] += _unheads(dl1)
            dgw_ref[...] += dgw
            ds_ref[...] = dst
            return carry

        lax.fori_loop(0, ncb, chunk, 0, unroll=True)

    row = pl.BlockSpec((rb, Dm), lambda n: (nb - 1 - n, 0))
    lsp = pl.BlockSpec((2, Dm), lambda n: (0, 0))
    gsp = pl.BlockSpec((1, HG_DIM), lambda n: (0, 0))
    pre_spec = pl.BlockSpec((4, rb, Dm), lambda n: (0, nb - 1 - n, 0))
    body, xs, xa = _after(body, 5, after)
    return pl.pallas_call(
        body, name=name, grid=(nb,),
        in_specs=[pre_spec, lsp, gsp, pl.BlockSpec((ncb, HG_HEADS, HG_DIM, HG_DIM), lambda n: (nb - 1 - n, 0, 0, 0)), row] + xs,
        out_specs=[pre_spec, lsp, gsp],
        out_shape=[jax.ShapeDtypeStruct((4, T, Dm), BF16), jax.ShapeDtypeStruct((2, Dm), F32),
                   jax.ShapeDtypeStruct((1, HG_DIM), F32)],
        scratch_shapes=[pltpu.VMEM((HG_HEADS, HG_DIM, HG_DIM), F32)],
        compiler_params=_params(("arbitrary",)),
    )(pre, lbl, gw, states, dout, *xa)


CONV_HALO = 2 * SUBLANES


def _conv_rows(u_ref, scr, w, bias, r0, R):
    cur = u_ref[pl.ds(r0, R), :].astype(F32)
    p0 = pl.multiple_of(jnp.maximum(r0 - CONV_HALO, 0), CONV_HALO)
    scr[0:CONV_HALO, :] = jnp.where(r0 > 0, u_ref[pl.ds(p0, CONV_HALO), :].astype(F32), 0.0)
    scr[CONV_HALO:CONV_HALO + R, :] = cur
    s1 = scr[CONV_HALO - 1:CONV_HALO - 1 + R, :]
    s2 = scr[CONV_HALO - 2:CONV_HALO - 2 + R, :]
    return w[0:1, :] * s2 + w[1:2, :] * s1 + w[2:3, :] * cur + bias, cur, s1, s2


def _halves_spec(T, Fd):
    per = Fd // 2 // LANES
    return pl.BlockSpec((2, None, T, LANES), lambda j: (0, j // per, 0, j % per))


def _conv_gate_fwd(u, wa, wb, ba, bb, *, name):
    T, Fd = u.shape[2], 2 * u.shape[3]
    R = min(CONV_R, T)
    tc = LANES

    def body(u_ref, wa_ref, wb_ref, ba_ref, bb_ref, o_ref, sa, sb):
        wa_, wb_, ba_, bb_ = wa_ref[...], wb_ref[...], ba_ref[...], bb_ref[...]

        def step(ri, carry):
            r0 = pl.multiple_of(ri * R, R)
            ca = _conv_rows(u_ref.at[0], sa, wa_, ba_, r0, R)[0]
            cb = _conv_rows(u_ref.at[1], sb, wb_, bb_, r0, R)[0]
            o_ref[pl.ds(r0, R), :] = (jax.nn.silu(ca) * cb).astype(BF16)
            return carry

        lax.fori_loop(0, T // R, step, 0)

    col = pl.BlockSpec((T, tc), lambda j: (0, j))
    wsp = pl.BlockSpec((3, tc), lambda j: (0, j))
    bsp = pl.BlockSpec((1, tc), lambda j: (0, j))
    both = _halves_spec(T, Fd)
    return pl.pallas_call(
        body, name=name, grid=(Fd // tc,), in_specs=[both, wsp, wsp, bsp, bsp], out_specs=col,
        out_shape=jax.ShapeDtypeStruct((T, Fd), BF16),
        scratch_shapes=[pltpu.VMEM((CONV_HALO + R, tc), F32)] * 2,
        compiler_params=_params(("parallel",)),
    )(u, wa, wb, ba, bb)


def _conv_gate_bwd(u, wa, wb, ba, bb, dact, *, name):
    T, Fd = u.shape[2], 2 * u.shape[3]
    R = min(CONV_R, T)
    nr = T // R
    tc = LANES

    def body(u_ref, wa_ref, wb_ref, ba_ref, bb_ref, da_ref,
             du_ref, dp_ref, sa, sb, sda, sdb):
        wa_, wb_, ba_, bb_ = wa_ref[...], wb_ref[...], ba_ref[...], bb_ref[...]
        sda[R:R + SUBLANES, :] = jnp.zeros((SUBLANES, tc), F32)
        sdb[R:R + SUBLANES, :] = jnp.zeros((SUBLANES, tc), F32)

        def taps(dc, cur, s1, s2):
            return jnp.concatenate([jnp.sum(dc * s2, axis=0, keepdims=True), jnp.sum(dc * s1, axis=0, keepdims=True),
                                    jnp.sum(dc * cur, axis=0, keepdims=True)], axis=0)

        def du_rows(sd, dc, w):
            sd[0:R, :] = dc
            du = w[2:3, :] * dc + w[1:2, :] * sd[1:1 + R, :] + w[0:1, :] * sd[2:2 + R, :]
            sd[R:R + SUBLANES, :] = dc[0:SUBLANES]
            return du

        def step(rj, carry):
            dwa, dwb, dba, dbb = carry
            r0 = pl.multiple_of((nr - 1 - rj) * R, R)
            ca, cura, s1a, s2a = _conv_rows(u_ref.at[0], sa, wa_, ba_, r0, R)
            cb, curb, s1b, s2b = _conv_rows(u_ref.at[1], sb, wb_, bb_, r0, R)
            dact_ = da_ref[pl.ds(r0, R), :].astype(F32)
            sg = jax.nn.sigmoid(ca)
            dca = dact_ * cb * (sg * (1.0 + ca * (1.0 - sg)))
            dcb = dact_ * (ca * sg)
            du_ref[0, pl.ds(r0, R), :] = du_rows(sda, dca, wa_).astype(BF16)
            du_ref[1, pl.ds(r0, R), :] = du_rows(sdb, dcb, wb_).astype(BF16)
            return (dwa + taps(dca, cura, s1a, s2a), dwb + taps(dcb, curb, s1b, s2b),
                    dba + jnp.sum(dca, axis=0, keepdims=True), dbb + jnp.sum(dcb, axis=0, keepdims=True))

        z3 = jnp.zeros((3, tc), F32)
        z1 = jnp.zeros((1, tc), F32)
        dwa, dwb, dba, dbb = lax.fori_loop(0, nr, step, (z3, z3, z1, z1))
        dp_ref[0] = jnp.concatenate([dwa, dba], axis=0)
        dp_ref[1] = jnp.concatenate([dwb, dbb], axis=0)

    col = pl.BlockSpec((T, tc), lambda j: (0, j))
    wsp = pl.BlockSpec((3, tc), lambda j: (0, j))
    bsp = pl.BlockSpec((1, tc), lambda j: (0, j))
    both = _halves_spec(T, Fd)
    return pl.pallas_call(
        body, name=name, grid=(Fd // tc,), in_specs=[both, wsp, wsp, bsp, bsp, col],
        out_specs=[both, pl.BlockSpec((2, 4, tc), lambda j: (0, 0, j))],
        out_shape=[jax.ShapeDtypeStruct(u.shape, BF16), jax.ShapeDtypeStruct((2, 4, Fd), F32)],
        scratch_shapes=[pltpu.VMEM((CONV_HALO + R, tc), F32)] * 2 + [pltpu.VMEM((R + SUBLANES, tc), F32)] * 2,
        compiler_params=_params(("parallel",)),
    )(u, wa, wb, ba, bb, dact)


def _bucket_index():
    t = np.arange(SW_WINDOW)[None, :] + SW_WINDOW
    s = np.arange(2 * SW_WINDOW)[:, None]
    dist = np.maximum(t - s, 0)
    exact = REL_BUCKETS // 2
    d = np.maximum(dist, 1).astype(np.float32)
    log_b = exact + (np.log(d / np.float32(exact)) / np.float32(math.log(REL_MAX_DIST / exact))
                     * np.float32(REL_BUCKETS - exact)).astype(np.int32)
    bucket = np.where(dist < exact, dist, np.minimum(log_b, REL_BUCKETS - 1))
    return bucket.astype(np.int32).reshape(1, -1)


BIAS_COLS = SW_WINDOW * 2 * SW_WINDOW
BIAS_TILE = 4096


def _bias_from_table(table, bucket, *, name):
    def body(t_ref, idx_ref, o_ref):
        onehot = (lax.broadcasted_iota(jnp.int32, (REL_BUCKETS, BIAS_TILE), 0) == idx_ref[...]).astype(BF16)
        acc = jnp.zeros((SW_Q_HEADS, BIAS_TILE), F32)
        for piece in _split3(t_ref[...]):
            acc = acc + lax.dot_general(piece, onehot, (((0,), (0,)), ((), ())), preferred_element_type=F32)
        o_ref[...] = acc

    return pl.pallas_call(
        body, name=name, grid=(BIAS_COLS // BIAS_TILE,),
        in_specs=[pl.BlockSpec((REL_BUCKETS, SW_Q_HEADS), lambda j: (0, 0)), pl.BlockSpec((1, BIAS_TILE), lambda j: (0, j))],
        out_specs=pl.BlockSpec((SW_Q_HEADS, BIAS_TILE), lambda j: (0, j)),
        out_shape=jax.ShapeDtypeStruct((SW_Q_HEADS, BIAS_COLS), F32),
        compiler_params=_params(("parallel",)),
    )(table, bucket)


def _table_grad(dbias, bucket, *, name):
    def body(d_ref, idx_ref, o_ref):
        @pl.when(pl.program_id(0) == 0)
        def _():
            o_ref[...] = jnp.zeros_like(o_ref)

        onehot = (lax.broadcasted_iota(jnp.int32, (REL_BUCKETS, BIAS_TILE), 0) == idx_ref[...]).astype(BF16)
        acc = jnp.zeros((REL_BUCKETS, SW_Q_HEADS), F32)
        for piece in _split3(d_ref[...]):
            acc = acc + lax.dot_general(onehot, piece, (((1,), (1,)), ((), ())), preferred_element_type=F32)
        o_ref[...] += acc

    return pl.pallas_call(
        body, name=name, grid=(BIAS_COLS // BIAS_TILE,),
        in_specs=[pl.BlockSpec((SW_Q_HEADS, BIAS_TILE), lambda j: (0, j)), pl.BlockSpec((1, BIAS_TILE), lambda j: (0, j))],
        out_specs=pl.BlockSpec((REL_BUCKETS, SW_Q_HEADS), lambda j: (0, 0)),
        out_shape=jax.ShapeDtypeStruct((REL_BUCKETS, SW_Q_HEADS), F32),
        compiler_params=_params(("arbitrary",)),
    )(dbias, bucket)


KV_DIM = SW_KV_HEADS * SW_HEAD_DIM
GROUP_ROWS = SW_GROUP * SW_HEAD_DIM
GROUP_LANES = SW_GROUP * SW_WINDOW


def _band_mask(n):
    s = lax.broadcasted_iota(jnp.int32, (2 * SW_WINDOW, GROUP_LANES), 0)
    t = (lax.broadcasted_iota(jnp.int32, (2 * SW_WINDOW, GROUP_LANES), 1) & (SW_WINDOW - 1)) + SW_WINDOW
    dist = t - s
    return (dist >= 0) & (dist < SW_WINDOW) & ((n > 0) | (s >= SW_WINDOW))


def _side_by_side(x_ref, g):
    r0 = g * GROUP_ROWS
    return jnp.concatenate([x_ref[r0 + r * SW_HEAD_DIM:r0 + (r + 1) * SW_HEAD_DIM, :] for r in range(SW_GROUP)], axis=1)


def _group_inputs(bias_ref, sink_ref, g):
    heads = range(g * SW_GROUP, (g + 1) * SW_GROUP)
    bias = jnp.concatenate([bias_ref[h] for h in heads], axis=1)
    sink = jnp.concatenate([jnp.broadcast_to(sink_ref[:, h:h + 1], (1, SW_WINDOW)) for h in heads], axis=1)
    return heads, bias, sink


def _kv_pair(kvp_ref, kvc_ref, g):
    ks = slice(g * SW_HEAD_DIM, (g + 1) * SW_HEAD_DIM)
    vs = slice(KV_DIM + g * SW_HEAD_DIM, KV_DIM + (g + 1) * SW_HEAD_DIM)
    kk = jnp.concatenate([kvp_ref[:, ks], kvc_ref[:, ks]], axis=0)
    vv = jnp.concatenate([kvp_ref[:, vs], kvc_ref[:, vs]], axis=0)
    return kk, vv, ks, vs


def _col_max(x):
    return jnp.max(x, axis=0, keepdims=True)


def _col_sum(x):
    return jnp.sum(x, axis=0, keepdims=True)


def _attn_fwd(qt, kv, bias, sinks, *, name):
    Dm, T = qt.shape
    W = SW_WINDOW

    def body(q_ref, kvc_ref, kvp_ref, bias_ref, sink_ref, o_ref):
        mask = _band_mask(pl.program_id(0))
        G = range(SW_KV_HEADS)
        ins = [_group_inputs(bias_ref, sink_ref, g) for g in G]
        kvs = [_kv_pair(kvp_ref, kvc_ref, g) for g in G]
        q = [_side_by_side(q_ref, g) for g in G]
        lg = [jnp.where(mask, mm(kvs[g][0], q[g]) * (SW_HEAD_DIM ** -0.5) + ins[g][1], -jnp.inf) for g in G]
        m = [jnp.maximum(_col_max(lg[g]), ins[g][2]) for g in G]
        p = [jnp.exp(lg[g] - m[g]) for g in G]
        den = [_col_sum(p[g]) + jnp.exp(ins[g][2] - m[g]) for g in G]
        o = [mm_tn(kvs[g][1], p[g]) / den[g] for g in G]
        for g in G:
            for r in range(SW_GROUP):
                o_ref[g * GROUP_ROWS + r * SW_HEAD_DIM:g * GROUP_ROWS + (r + 1) * SW_HEAD_DIM, :] = (
                    o[g][:, r * W:(r + 1) * W].astype(BF16))

    return pl.pallas_call(
        body, name=name, grid=(T // W,),
        in_specs=[pl.BlockSpec((Dm, W), lambda n: (0, n)),
                  pl.BlockSpec((W, 2 * KV_DIM), lambda n: (n, 0)),
                  pl.BlockSpec((W, 2 * KV_DIM), lambda n: (jnp.maximum(n - 1, 0), 0)),
                  pl.BlockSpec((SW_Q_HEADS, 2 * W, W), lambda n: (0, 0, 0)),
                  pl.BlockSpec((1, SW_Q_HEADS), lambda n: (0, 0))],
        out_specs=pl.BlockSpec((Dm, W), lambda n: (0, n)),
        out_shape=jax.ShapeDtypeStruct((Dm, T), BF16),
        compiler_params=_params(("parallel",)),
    )(qt, kv, kv, bias, sinks)


def _attn_bwd(qt, kv, bias, sinks, dot, *, name):
    Dm, T = qt.shape
    W = SW_WINDOW
    nb = T // W

    def body(q_ref, kvc_ref, kvp_ref, bias_ref, sink_ref, do_ref,
             dq_ref, dkv_ref, dbias_ref, dsink_ref, carry_ref):
        @pl.when(pl.program_id(0) == 0)
        def _():
            carry_ref[...] = jnp.zeros_like(carry_ref)
            dbias_ref[...] = jnp.zeros_like(dbias_ref)
            dsink_ref[...] = jnp.zeros_like(dsink_ref)

        n = nb - 1 - pl.program_id(0)
        mask = _band_mask(n)
        lane = lax.broadcasted_iota(jnp.int32, (1, SW_Q_HEADS), 1)
        sc = SW_HEAD_DIM ** -0.5
        G = range(SW_KV_HEADS)
        ins = [_group_inputs(bias_ref, sink_ref, g) for g in G]
        kvs = [_kv_pair(kvp_ref, kvc_ref, g) for g in G]
        q = [_side_by_side(q_ref, g) for g in G]
        do = [_side_by_side(do_ref, g) for g in G]
        lg = [jnp.where(mask, mm(kvs[g][0], q[g]) * sc + ins[g][1], -jnp.inf) for g in G]
        m = [jnp.maximum(_col_max(lg[g]), ins[g][2]) for g in G]
        p = [jnp.exp(lg[g] - m[g]) for g in G]
        ps = [jnp.exp(ins[g][2] - m[g]) for g in G]
        rden = [1.0 / (_col_sum(p[g]) + ps[g]) for g in G]
        pn = [p[g] * rden[g] for g in G]
        dpn = [mm(kvs[g][1], do[g]) for g in G]
        delta = [_col_sum(pn[g] * dpn[g]) for g in G]
        ds = [pn[g] * (dpn[g] - delta[g]) for g in G]
        dsr = [-(ps[g] * rden[g]) * delta[g] for g in G]
        dq = [mm_tn(kvs[g][0], ds[g]) * sc for g in G]
        dkk = [mm_nt(ds[g], q[g]) * sc for g in G]
        dvv = [mm_nt(pn[g], do[g]) for g in G]
        dsink = jnp.zeros((1, SW_Q_HEADS), F32)
        for g in G:
            _, _, ks, vs = kvs[g]
            for r, h in enumerate(ins[g][0]):
                cols = slice(r * W, (r + 1) * W)
                dbias_ref[h] += ds[g][:, cols]
                dq_ref[g * GROUP_ROWS + r * SW_HEAD_DIM:g * GROUP_ROWS + (r + 1) * SW_HEAD_DIM, :] = dq[g][:, cols].astype(BF16)
                dsink = dsink + jnp.where(lane == h, jnp.sum(dsr[g][:, cols], axis=1, keepdims=True), 0.0)
            dkv_ref[:, ks] = (carry_ref[:, ks] + dkk[g][W:]).astype(BF16)
            dkv_ref[:, vs] = (carry_ref[:, vs] + dvv[g][W:]).astype(BF16)
            carry_ref[:, ks] = dkk[g][:W]
            carry_ref[:, vs] = dvv[g][:W]
        dsink_ref[...] += dsink

    rev = lambda n: (nb - 1 - n, 0)
    revt = lambda n: (0, nb - 1 - n)
    return pl.pallas_call(
        body, name=name, grid=(nb,),
        in_specs=[pl.BlockSpec((Dm, W), revt),
                  pl.BlockSpec((W, 2 * KV_DIM), rev),
                  pl.BlockSpec((W, 2 * KV_DIM), lambda n: (jnp.maximum(nb - 2 - n, 0), 0)),
                  pl.BlockSpec((SW_Q_HEADS, 2 * W, W), lambda n: (0, 0, 0)),
                  pl.BlockSpec((1, SW_Q_HEADS), lambda n: (0, 0)),
                  pl.BlockSpec((Dm, W), revt)],
        out_specs=[pl.BlockSpec((Dm, W), revt), pl.BlockSpec((W, 2 * KV_DIM), rev),
                   pl.BlockSpec((SW_Q_HEADS, 2 * W, W), lambda n: (0, 0, 0)),
                   pl.BlockSpec((1, SW_Q_HEADS), lambda n: (0, 0))],
        out_shape=[jax.ShapeDtypeStruct((Dm, T), BF16), jax.ShapeDtypeStruct((T, 2 * KV_DIM), BF16),
                   jax.ShapeDtypeStruct((SW_Q_HEADS, 2 * W, W), F32), jax.ShapeDtypeStruct((1, SW_Q_HEADS), F32)],
        scratch_shapes=[pltpu.VMEM((W, 2 * KV_DIM), F32)],
        compiler_params=_params(("arbitrary",)),
    )(qt, kv, kv, bias, sinks, dot)


def _ffn_fwd(hb, w, l, after=None):
    u = _matmul_planes_nn(hb, w["ffn_in"][l], name=f"ffn{l}_up", after=after)
    u = u.reshape((2, 2) + u.shape[1:])
    act = _conv_gate_fwd(u, w["conv_w_a"][l], w["conv_w_b"][l], w["conv_b_a"][l], w["conv_b_b"][l],
                         name=f"ffn{l}_conv_gate")
    return u, act


def _ffn_bwd(dffb, dh_scaled, hb, u, act, w, l, dact):
    g_out = _matmul(act, dffb, mode="tn", name=f"ffn{l}_down_dw", tm=1408, tn=1024, tk=2048, also_bf16=True)
    du, g_conv = _conv_gate_bwd(u, w["conv_w_a"][l], w["conv_w_b"][l], w["conv_b_a"][l], w["conv_b_b"][l],
                                dact, name=f"ffn{l}_conv_gate_bwd")
    du = du.reshape((N_CHIPS,) + du.shape[2:])
    dh = _matmul_planes_nt(du, w["ffn_in"][l], dh_scaled, add_scale=ALPHA, name=f"ffn{l}_up_dx")
    g_in = _matmul(hb, du, mode="tn", planes="n", name=f"ffn{l}_up_dw", tm=1024, tn=FFN_DIM // 2, tk=2048, also_bf16=True)
    return dh, dict(ffn_out=g_out, ffn_in=g_in, conv=g_conv)


def _local_step(x, xb, tgt, w, more_weights, emit):
    bucket = jnp.asarray(_bucket_index())

    pre = _matmul_planes_nn(xb, w["hg_in"], name="hg_in", tm=1024, after=w.get("token"))
    og, states = _hgrn_fwd(pre, w["lb_logits"], w["gnorm"], name="hgrn_fwd")
    z1, h1, h1b = _matmul_ln(og, w["hg_out"], x, w["ln_mix_g"][0], w["ln_mix_b"][0], tm=1024, name="hg_out_ln")
    w = {**w, **more_weights(1, h1b)}
    u0, act0 = _ffn_fwd(h1b, w, 0, after=w.get("token"))
    z2, h2, h2b = _matmul_ln(act0, w["ffn_out"][0], h1, w["ln_ffn_g"][0], w["ln_ffn_b"][0], name="ffn0_down_ln")
    kv = _matmul(h2b, w["kv"], mode="nn", out_dtype=BF16, name="kv_proj", tm=1024)

    bias = _bias_from_table(w["rel_bias"], bucket, name="rel_bias_expand").reshape(SW_Q_HEADS, 2 * SW_WINDOW, SW_WINDOW)
    q1 = _matmul(w["sw_q"], h2b, mode="tt", out_dtype=BF16, name="sw_q", tm=1024, tn=1024)
    o1 = _attn_fwd(q1, kv, bias, w["sinks"], name="attn_fwd")
    z3, h3, h3b = _matmul_ln(o1, w["sw_out"], h2, w["ln_mix_g"][1], w["ln_mix_b"][1], a_t=True, tm=1024, name="sw_out_ln")
    w = {**w, **more_weights(2, h3b)}
    u1, act1 = _ffn_fwd(h3b, w, 1)

    g = {}
    dz, dzb, g["ln_ffn1"], loss_tile, dact1 = _matmul_ln(act1, w["ffn_out"][1], h3, w["ln_ffn_g"][1], w["ln_ffn_b"][1],
                                                         tgt=tgt, name="ffn1_down_ln_loss")

    dh3, gf1 = _ffn_bwd(dzb, dz, h3b, u1, act1, w, 1, dact1)
    dz, dzb, g["ln_mix1"], do1 = _ln_bwd_matmul(dh3, z3, w["ln_mix_g"][1], w["ln_mix_b"][1], w["sw_out"], out_t=True,
                                                name="ln_mix1_bwd_sw_out_dx")
    g_sw_out = _matmul(o1, dzb, mode="nn", name="sw_out_dw", tm=1024, tn=1024, tk=2048, also_bf16=True)
    dq1, dkv, dbias, dsinks = _attn_bwd(q1, kv, bias, w["sinks"], do1, name="attn_bwd")
    g["sinks"] = dsinks
    g["rel_bias"] = _table_grad(dbias.reshape(SW_Q_HEADS, BIAS_COLS), bucket, name="rel_bias_grad")
    dh2 = _matmul(dq1, w["sw_q"], mode="tt", add=dz, add_scale=ALPHA, name="sw_q_dx", tn=1024)
    dh2 = _matmul(dkv, w["kv"], mode="nt", add=dh2, name="kv_dx", tn=1024)
    g_sw_q = _matmul(h2b, dq1, mode="tt", name="sw_q_dw", tm=1024, tn=1024, tk=2048, also_bf16=True)
    g_kv = _matmul(h2b, dkv, mode="tn", name="kv_dw", tm=1024, tn=512, tk=2048, also_bf16=True)
    tok = emit(1, dict(sw_q=g_sw_q, sw_out=g_sw_out, kv=g_kv, ffn_in=gf1["ffn_in"], ffn_out=gf1["ffn_out"]))

    dz, dzb, g["ln_ffn0"], dact0 = _ln_bwd_matmul(dh2, z2, w["ln_ffn_g"][0], w["ln_ffn_b"][0], w["ffn_out"][0],
                                                  name="ln_ffn0_bwd_down_dx", after=tok)
    dh1, gf0 = _ffn_bwd(dzb, dz, h1b, u0, act0, w, 0, dact0)
    dz, dzb, g["ln_mix0"], dog = _ln_bwd_matmul(dh1, z1, w["ln_mix_g"][0], w["ln_mix_b"][0], w["hg_out"],
                                                name="ln_mix0_bwd_hg_out_dx")
    g_hg_out = _matmul(og, dzb, mode="tn", name="hg_out_dw", tm=1024, tn=1024, tk=2048, also_bf16=True)
    tok = emit(2, dict(hg_out=g_hg_out, ffn_in=gf0["ffn_in"], ffn_out=gf0["ffn_out"]))
    dpre, g["lb_logits"], g["gnorm"] = _hgrn_bwd(pre, w["lb_logits"], w["gnorm"], states, dog, name="hgrn_bwd", after=tok)
    tok = emit(3, dict(hg_in=_matmul(xb, dpre, mode="tn", planes="n", name="hg_in_dw", tm=1024, tn=1024, tk=2048, also_bf16=True)))
    dx = _matmul_planes_nt(dpre, w["hg_in"], dz, add_scale=ALPHA, name="hg_in_dx", after=tok)
    g["conv0"], g["conv1"] = gf0["conv"], gf1["conv"]
    return loss_tile, dx, g


def _adamw(wt, ga, gb, m, v, *, name, rows=None, prev=None):
    R, Cc = wt.shape
    r0, n = rows if rows is not None else (0, R)
    tr = _tile(n, 256, SUBLANES) if n % SUBLANES == 0 else n
    assert r0 % tr == 0
    c1 = 1.0 - ADAM_B1 ** ADAM_STEP
    c2 = 1.0 - ADAM_B2 ** ADAM_STEP
    n_in = 5

    def body(*refs):
        w_ref, ga_ref, gb_ref, m_ref, v_ref = refs[:n_in]
        g_ = ga_ref[...] + gb_ref[...]
        g_ref, d_ref, nm_ref, nv_ref = refs[-4:]
        nm = ADAM_B1 * m_ref[...] + (1.0 - ADAM_B1) * g_
        nv = ADAM_B2 * v_ref[...] + (1.0 - ADAM_B2) * (g_ * g_)
        g_ref[...] = g_
        d_ref[...] = -ADAM_LR * ((nm / c1) / (jnp.sqrt(nv / c2) + ADAM_EPS) + ADAM_WD * w_ref[...])
        nm_ref[...] = nm
        nv_ref[...] = nv

    full = pl.BlockSpec((tr, Cc), lambda i: (i + r0 // tr, 0))
    part = pl.BlockSpec((tr, Cc), lambda i: (i, 0))
    args = (wt, ga, gb, m, v)
    in_specs = [full, part, part, full, full]
    aliases = {}
    if prev is not None:
        args, in_specs = args + tuple(prev), in_specs + [ANY_SPEC] * 4
        aliases = {n_in + t: t for t in range(4)}
    return pl.pallas_call(
        body, name=name, grid=(n // tr,), in_specs=in_specs, out_specs=[full] * 4,
        out_shape=[jax.ShapeDtypeStruct((R, Cc), F32)] * 4, input_output_aliases=aliases,
        compiler_params=_params(("parallel",)),
    )(*args)


def _adamw_small(ws, gs, ms, vs, *, name):
    n = len(ws)
    c1 = 1.0 - ADAM_B1 ** ADAM_STEP
    c2 = 1.0 - ADAM_B2 ** ADAM_STEP

    def body(*refs):
        w_refs, g_refs, m_refs, v_refs = (refs[k * n:(k + 1) * n] for k in range(4))
        d_refs, nm_refs, nv_refs = (refs[(4 + k) * n:(5 + k) * n] for k in range(3))
        for i in range(n):
            g_ = g_refs[i][...]
            nm = ADAM_B1 * m_refs[i][...] + (1.0 - ADAM_B1) * g_
            nv = ADAM_B2 * v_refs[i][...] + (1.0 - ADAM_B2) * (g_ * g_)
            d_refs[i][...] = -ADAM_LR * ((nm / c1) / (jnp.sqrt(nv / c2) + ADAM_EPS) + ADAM_WD * w_refs[i][...])
            nm_refs[i][...] = nm
            nv_refs[i][...] = nv

    vm = pl.BlockSpec(memory_space=pltpu.VMEM)
    out = pl.pallas_call(
        body, name=name, in_specs=[vm] * (4 * n), out_specs=[vm] * (3 * n),
        out_shape=[jax.ShapeDtypeStruct(w.shape, F32) for w in ws] * 3,
    )(*ws, *gs, *ms, *vs)
    return out[:n], out[n:2 * n], out[2 * n:]


HBM_SPEC = pl.BlockSpec(memory_space=pltpu.HBM)
SEM_SPEC = pl.BlockSpec(memory_space=pltpu.SEMAPHORE)
VMEM_SPEC = pl.BlockSpec(memory_space=pltpu.VMEM)
DATAFLOW = pltpu.SideEffectType.DATAFLOW_SIDE_EFFECTING


def _in_hbm(a):
    return pltpu.with_memory_space_constraint(a, pltpu.HBM)


def _place():
    return lax.axis_index("x"), lax.axis_index("y"), lax.axis_index("c")


def _other_chips(x, y):
    return [(1 - x, y), (x, 1 - y), (1 - x, 1 - y)]


def _sum8(vs, *, name):
    n = len(vs)

    def body(*refs):
        v_refs, all_refs, o_refs = refs[:n], refs[n:2 * n], refs[2 * n:3 * n]
        send_sems, recv_sems, local_sems = refs[3 * n:]
        x, y, c = _place()
        me, sibling = (x, y, c), (x, y, 1 - c)
        chips = _other_chips(x, y)

        def slot(i, px, py, pc):
            return all_refs[i].at[4 * px + 2 * py + pc]

        def copy(i, k, block, to, src=None):
            return pltpu.make_async_remote_copy(
                src_ref=slot(i, *block) if src is None else src, dst_ref=slot(i, *block),
                send_sem=send_sems.at[7 * i + k], recv_sem=recv_sems.at[7 * i + k], device_id=to, device_id_type=MESH)

        mine = [pltpu.make_async_copy(v_refs[i], slot(i, *me), local_sems.at[i]) for i in range(n)]
        for cp in mine:
            cp.start()
        first = [copy(i, 0, me, sibling, src=v_refs[i]) for i in range(n)]
        first += [copy(i, 1 + j, me, (*chip, c), src=v_refs[i]) for i in range(n) for j, chip in enumerate(chips)]
        for cp in first:
            cp.start()
        passed = []
        for i in range(n):
            for j, chip in enumerate(chips):
                copy(i, 1 + j, (*chip, c), me).wait_recv()
                passed.append(copy(i, 4 + j, (*chip, c), sibling))
                passed[-1].start()
        for i in range(n):
            copy(i, 0, sibling, me).wait_recv()
            for j, chip in enumerate(chips):
                copy(i, 4 + j, (*chip, 1 - c), me).wait_recv()
        for cp in first + passed:
            cp.wait_send()
        for cp in mine:
            cp.wait()
        for i in range(n):
            acc = all_refs[i][0]
            for d in range(1, N_DEV):
                acc = acc + all_refs[i][d]
            o_refs[i][...] = acc

    return pl.pallas_call(
        body, name=name, in_specs=[VMEM_SPEC] * n, out_specs=[VMEM_SPEC] * (2 * n),
        out_shape=[jax.ShapeDtypeStruct((N_DEV,) + v.shape, F32) for v in vs] + [jax.ShapeDtypeStruct(v.shape, F32) for v in vs],
        scratch_shapes=[pltpu.SemaphoreType.DMA((7 * n,)), pltpu.SemaphoreType.DMA((7 * n,)), pltpu.SemaphoreType.DMA((n,))],
        compiler_params=pltpu.CompilerParams(vmem_limit_bytes=VMEM_LIMIT),
    )(*vs)[n:]


def _swap_copies(src, land, send, recv):
    x, y, c = _place()
    return [pltpu.make_async_remote_copy(src_ref=src[i], dst_ref=land[i], send_sem=send.at[i], recv_sem=recv.at[i],
                                         device_id=(x, y, 1 - c), device_id_type=MESH) for i in range(len(src))]


def _swap_start(vs, *, name):
    n = len(vs)

    def body(*refs):
        src, land, send, recv, token = refs[:n], refs[n:2 * n], refs[2 * n], refs[2 * n + 1], refs[-1]
        for cp in _swap_copies(src, land, send, recv):
            cp.start()
        token[...] = jnp.zeros_like(token)

    lands = [lax.empty(v.shape, v.dtype) for v in vs]
    sems = pltpu.SemaphoreType.DMA((n,))
    out = pl.pallas_call(
        body, name=name, in_specs=[HBM_SPEC] * (2 * n),
        out_specs=[SEM_SPEC, SEM_SPEC] + [HBM_SPEC] * (2 * n) + [VMEM_SPEC],
        out_shape=[sems, sems] + [pltpu.HBM(a.shape, a.dtype) for a in list(vs) + lands]
        + [jax.ShapeDtypeStruct((SUBLANES, LANES), F32)],
        input_output_aliases={i: 2 + i for i in range(2 * n)},
        compiler_params=pltpu.CompilerParams(has_side_effects=DATAFLOW),
    )(*[_in_hbm(a) for a in list(vs) + lands])
    return (out[0], out[1], out[2:2 + n], out[2 + n:2 + 2 * n]), out[-1]


def _swap_wait(handle, after, *, name):
    send_sems, recv_sems, srcs, lands = handle
    n = len(srcs)

    def body(*refs):
        src, land, send, recv = refs[:n], refs[n:2 * n], refs[2 * n], refs[2 * n + 1]
        for cp in _swap_copies(src, land, send, recv):
            cp.wait_send()
            cp.wait_recv()

    both = list(srcs) + list(lands)
    out = pl.pallas_call(
        body, name=name, in_specs=[HBM_SPEC] * (2 * n) + [SEM_SPEC, SEM_SPEC, ANY_SPEC], out_specs=[HBM_SPEC] * (2 * n),
        out_shape=[pltpu.HBM(a.shape, a.dtype) for a in both],
        input_output_aliases={i: i for i in range(2 * n)},
        compiler_params=pltpu.CompilerParams(has_side_effects=DATAFLOW),
    )(*both, send_sems, recv_sems, after)
    return out[:n], out[n:]


def _gather_copies(srcs, lands, send, recv, sibling=False):
    x, y, c = _place()
    out = []
    for i, (src, land) in enumerate(zip(srcs, lands)):
        half = land.shape[1] // 2
        rows = pl.ds(c * half, half)
        for k, (px, py) in enumerate(_other_chips(x, y)):
            if sibling:
                src_ref, dst_ref, to = src.at[2 * px + py, rows], land.at[2 * px + py, rows], (x, y, 1 - c)
            else:
                src_ref, dst_ref, to = src.at[rows], land.at[2 * x + y, rows], (px, py, c)
            out.append(pltpu.make_async_remote_copy(src_ref=src_ref, dst_ref=dst_ref, send_sem=send.at[3 * i + k],
                                                    recv_sem=recv.at[3 * i + k], device_id=to, device_id_type=MESH))
    return out


def _gather_arrivals(lands, send, recv, sibling=False):
    x, y, c = _place()
    out = []
    for i, land in enumerate(lands):
        half = land.shape[1] // 2
        rows = pl.ds(((1 - c) if sibling else c) * half, half)
        for k, (px, py) in enumerate(_other_chips(x, y)):
            part = land.at[2 * px + py, rows]
            out.append(pltpu.make_async_remote_copy(src_ref=part, dst_ref=part, send_sem=send.at[3 * i + k],
                                                    recv_sem=recv.at[3 * i + k],
                                                    device_id=(x, y, 1 - c) if sibling else (px, py, c), device_id_type=MESH))
    return out


def _own_copies(srcs, lands, sems):
    x, y, _ = _place()
    return [pltpu.make_async_copy(src, land.at[2 * x + y], sems.at[i]) for i, (src, land) in enumerate(zip(srcs, lands))]


def _gather_start(shards, after, *, name, own_too):
    n = len(shards)

    def body(*refs):
        srcs, lands, (send, recv, own), token = refs[:n], refs[n:2 * n], refs[2 * n:2 * n + 3], refs[-1]
        for cp in _gather_copies(srcs, lands, send, recv) + (_own_copies(srcs, lands, own) if own_too else []):
            cp.start()
        token[...] = jnp.zeros_like(token)

    lands = [lax.empty((N_CHIPS,) + s.shape, s.dtype) for s in shards]
    sems = pltpu.SemaphoreType.DMA((3 * n,))
    body, xs, xa = _after(body, 2 * n, after)
    out = pl.pallas_call(
        body, name=name, in_specs=[HBM_SPEC] * (2 * n) + xs,
        out_specs=[SEM_SPEC] * 3 + [HBM_SPEC] * (2 * n) + [VMEM_SPEC],
        out_shape=[sems, sems, pltpu.SemaphoreType.DMA((n,))] + [pltpu.HBM(a.shape, a.dtype) for a in list(shards) + lands]
        + [jax.ShapeDtypeStruct((SUBLANES, LANES), F32)],
        input_output_aliases={i: 3 + i for i in range(2 * n)},
        compiler_params=pltpu.CompilerParams(has_side_effects=DATAFLOW),
    )(*[_in_hbm(a) for a in list(shards) + lands], *xa)
    return (out[:3], out[3:3 + n], out[3 + n:3 + 2 * n], own_too), out[-1]


def _gather_wait(handle, after, *, name):
    sems, srcs, lands, own_too = handle
    n = len(srcs)

    def body(*refs):
        srcs_, lands_, (send, recv, own) = refs[:n], refs[n:2 * n], refs[2 * n:2 * n + 3]
        for cp in _gather_copies(srcs_, lands_, send, recv):
            cp.wait_send()
        for cp in _gather_arrivals(lands_, send, recv):
            cp.wait_recv()
        for cp in _own_copies(srcs_, lands_, own) if own_too else []:
            cp.wait()

    both = list(srcs) + list(lands)
    out = pl.pallas_call(
        body, name=name, in_specs=[HBM_SPEC] * (2 * n) + [SEM_SPEC] * 3 + [ANY_SPEC], out_specs=[HBM_SPEC] * (2 * n),
        out_shape=[pltpu.HBM(a.shape, a.dtype) for a in both],
        input_output_aliases={i: i for i in range(2 * n)},
        compiler_params=pltpu.CompilerParams(has_side_effects=DATAFLOW),
    )(*both, *sems, after)
    return out[n:]


def _fill_sibling(lands, *, name):
    n = len(lands)

    def body(*refs):
        ins, outs, send_sems, recv_sems = refs[:n], refs[n:2 * n], refs[2 * n], refs[2 * n + 1]
        cps = _gather_copies(ins, outs, send_sems, recv_sems, sibling=True)
        for cp in cps:
            cp.start()
        for cp in _gather_arrivals(outs, send_sems, recv_sems, sibling=True):
            cp.wait_recv()
        for cp in cps:
            cp.wait_send()

    return pl.pallas_call(
        body, name=name, in_specs=[HBM_SPEC] * n, out_specs=[HBM_SPEC] * n,
        out_shape=[jax.ShapeDtypeStruct(a.shape, a.dtype) for a in lands],
        scratch_shapes=[pltpu.SemaphoreType.DMA((3 * n,)), pltpu.SemaphoreType.DMA((3 * n,))],
        input_output_aliases={i: i for i in range(n)},
    )(*lands)


def _scatter_copies(src, land, send, recv):
    x, y, c = _place()
    return [pltpu.make_async_remote_copy(src_ref=src[i].at[2 * px + py], dst_ref=land[i].at[k], send_sem=send.at[3 * i + k],
                                         recv_sem=recv.at[3 * i + k], device_id=(px, py, c), device_id_type=MESH)
            for i in range(len(src)) for k, (px, py) in enumerate(_other_chips(x, y))]


def _scatter_start(pieces, *, name):
    n = len(pieces)

    def body(*refs):
        src, land, send, recv, token = refs[:n], refs[n:2 * n], refs[2 * n], refs[2 * n + 1], refs[-1]
        for cp in _scatter_copies(src, land, send, recv):
            cp.start()
        token[...] = jnp.zeros_like(token)

    lands = [lax.empty((3,) + p.shape[1:], p.dtype) for p in pieces]
    sems = pltpu.SemaphoreType.DMA((3 * n,))
    out = pl.pallas_call(
        body, name=name, in_specs=[HBM_SPEC] * (2 * n),
        out_specs=[SEM_SPEC, SEM_SPEC] + [HBM_SPEC] * (2 * n) + [VMEM_SPEC],
        out_shape=[sems, sems] + [pltpu.HBM(a.shape, a.dtype) for a in pieces + lands]
        + [jax.ShapeDtypeStruct((SUBLANES, LANES), F32)],
        input_output_aliases={i: 2 + i for i in range(2 * n)},
        compiler_params=pltpu.CompilerParams(has_side_effects=DATAFLOW),
    )(*[_in_hbm(a) for a in pieces + lands])
    return (out[0], out[1], out[2:2 + n], out[2 + n:2 + 2 * n]), out[-1]


def _scatter_wait(handle, after, *, name):
    send_sems, recv_sems, srcs, lands = handle
    n = len(srcs)

    def body(*refs):
        src, land, send, recv = refs[:n], refs[n:2 * n], refs[2 * n], refs[2 * n + 1]
        for cp in _scatter_copies(src, land, send, recv):
            cp.wait_send()
            cp.wait_recv()

    both = list(srcs) + list(lands)
    out = pl.pallas_call(
        body, name=name, in_specs=[HBM_SPEC] * (2 * n) + [SEM_SPEC, SEM_SPEC, ANY_SPEC], out_specs=[HBM_SPEC] * (2 * n),
        out_shape=[pltpu.HBM(a.shape, a.dtype) for a in both],
        input_output_aliases={i: i for i in range(2 * n)},
        compiler_params=pltpu.CompilerParams(has_side_effects=DATAFLOW),
    )(*both, send_sems, recv_sems, after)
    return out[n:]


def _to_bf16(x, *, name, after=None):
    T, Dm = x.shape
    tr = _tile(T, 512, 2 * SUBLANES)

    def body(x_ref, o_ref):
        o_ref[...] = x_ref[...].astype(BF16)

    blk = pl.BlockSpec((tr, Dm), lambda i: (i, 0))
    body, xs, xa = _after(body, 1, after)
    return pl.pallas_call(
        body, name=name, grid=(T // tr,), in_specs=[blk] + xs, out_specs=blk, out_shape=jax.ShapeDtypeStruct((T, Dm), BF16),
        compiler_params=_params(("parallel",)),
    )(x, *xa)


def _chip_sum(pieces, got, chip, *, name):
    _, R, Cc = pieces.shape
    tr = _tile(R, 512, 2 * SUBLANES)

    def body(chip_ref, a_ref, g_ref, o_ref):
        o_ref[...] = ((a_ref[...] + g_ref[0].astype(F32)) + g_ref[1].astype(F32)) + g_ref[2].astype(F32)

    return pl.pallas_call(
        body, name=name,
        grid_spec=pltpu.PrefetchScalarGridSpec(
            num_scalar_prefetch=1, grid=(R // tr,),
            in_specs=[pl.BlockSpec((None, tr, Cc), lambda i, ch: (ch[0], i, 0)),
                      pl.BlockSpec((3, tr, Cc), lambda i, ch: (0, i, 0))],
            out_specs=pl.BlockSpec((tr, Cc), lambda i, ch: (i, 0))),
        out_shape=jax.ShapeDtypeStruct((R, Cc), F32),
        compiler_params=_params(("parallel",)),
    )(chip, pieces, got)


PACK_COLS = 1024
SMALL_ROWS = 32


def kernel(x, hgrn_w_in, hgrn_lb_logits, hgrn_gnorm_w, hgrn_w_out, swa_w_q, swa_sinks, swa_w_out, shared_w_kv, rel_bias, ffn_w_in, ffn_conv_w, ffn_conv_b, ffn_w_out, ln_mix_g, ln_mix_b, ln_ffn_g, ln_ffn_b, loss_target, m_hgrn_w_in, m_hgrn_lb_logits, m_hgrn_gnorm_w, m_hgrn_w_out, m_swa_w_q, m_swa_sinks, m_swa_w_out, m_shared_w_kv, m_rel_bias, m_ffn_w_in, m_ffn_conv_w, m_ffn_conv_b, m_ffn_w_out, m_ln_mix_g, m_ln_mix_b, m_ln_ffn_g, m_ln_ffn_b, v_hgrn_w_in, v_hgrn_lb_logits, v_hgrn_gnorm_w, v_hgrn_w_out, v_swa_w_q, v_swa_sinks, v_swa_w_out, v_shared_w_kv, v_rel_bias, v_ffn_w_in, v_ffn_conv_w, v_ffn_conv_b, v_ffn_w_out, v_ln_mix_g, v_ln_mix_b, v_ln_ffn_g, v_ln_ffn_b):
    xi, yi, ci = _place()
    chip = 2 * xi + yi
    Dm = D_MODEL
    FC = 2 * FFN_DIM // N_CHIPS
    Fo = FFN_DIM // N_CHIPS
    Dq = Dm // N_CHIPS
    bf = lambda a: a.astype(BF16)

    small = jnp.concatenate([hgrn_lb_logits.reshape(-1), ffn_conv_w.reshape(-1)])
    n_small = small.shape[0]
    bits = jnp.concatenate(_split3(small))
    bits = jnp.pad(bits, (0, SMALL_ROWS * PACK_COLS - 3 * n_small)).reshape(SMALL_ROWS, PACK_COLS)
    groups = [[bf(hgrn_w_in[0]), bf(hgrn_w_out[0]), bits],
              [bf(swa_w_q[0]), bf(swa_w_out[0]), bf(shared_w_kv), bf(ffn_w_in[0]), bf(ffn_w_out[0])],
              [bf(ffn_w_in[1]), bf(ffn_w_out[1])]]

    def gathered(k, landed):
        lands = _fill_sibling(landed, name=f"gather_w{k}_fill")
        if k > 0:
            return lands
        return [lax.dynamic_update_slice(land, shard[None], (chip,) + (0,) * shard.ndim)
                for land, shard in zip(lands, groups[0])]

    handle0, token0 = _gather_start(groups[0], None, name="gather_w0_start", own_too=False)
    xb = _to_bf16(x[0], name="x_to_bf16", after=token0)
    corner = lambda a: a[:2 * SUBLANES, :LANES]
    casts_done = corner(xb) + sum(corner(a) for a in groups[1] + groups[2])
    w_in, w_hg_out, small_all = gathered(0, _gather_wait(handle0, casts_done, name="gather_w0_wait"))
    handle1, token1 = _gather_start(groups[1], w_in, name="gather_w1_start", own_too=True)
    parts = small_all.reshape(N_CHIPS, -1)[:, :3 * n_small].reshape(N_CHIPS, 3, n_small).astype(F32)
    vals = (parts[:, 0] + parts[:, 1]) + parts[:, 2]
    lb_full = vals[:, :2 * Dq].reshape(N_CHIPS, 2, Dq).transpose(1, 0, 2).reshape(2, Dm)
    cw_full = vals[:, 2 * Dq:].reshape(N_CHIPS, DEPTH, 3, FC).transpose(1, 2, 0, 3).reshape(DEPTH, 3, 2 * FFN_DIM)

    got = {"handle": handle1}

    def more_weights(k, after):
        ws = gathered(k, _gather_wait(got.pop("handle"), after, name=f"gather_w{k}_wait"))
        if k == 1:
            got["handle"], token2 = _gather_start(groups[2], ws[0], name="gather_w2_start", own_too=True)
            w_q, w_o, w_kv, w_fi, w_fo = ws
            got.update(ffn_in={0: w_fi}, ffn_out={0: w_fo.reshape(FFN_DIM, Dm)})
            return {"sw_q": w_q.reshape(Dm, Dm), "sw_out": w_o.reshape(Dm, Dm), "kv": w_kv.reshape(Dm, 2 * KV_DIM),
                    "token": token2, "ffn_in": got["ffn_in"], "ffn_out": got["ffn_out"]}
        w_fi, w_fo = ws
        return {"ffn_in": {**got["ffn_in"], 1: w_fi}, "ffn_out": {**got["ffn_out"], 1: w_fo.reshape(FFN_DIM, Dm)}}

    w = {
        "hg_in": w_in, "hg_out": w_hg_out.reshape(Dm, Dm), "token": token1,
        "lb_logits": lb_full, "gnorm": hgrn_gnorm_w, "sinks": swa_sinks, "rel_bias": rel_bias,
        "conv_w_a": [cw_full[l, :, :FFN_DIM] for l in range(DEPTH)],
        "conv_w_b": [cw_full[l, :, FFN_DIM:] for l in range(DEPTH)],
        "conv_b_a": [ffn_conv_b[l:l + 1, :FFN_DIM] for l in range(DEPTH)],
        "conv_b_b": [ffn_conv_b[l:l + 1, FFN_DIM:] for l in range(DEPTH)],
        "ln_mix_g": [ln_mix_g[l:l + 1] for l in range(DEPTH)], "ln_mix_b": [ln_mix_b[l:l + 1] for l in range(DEPTH)],
        "ln_ffn_g": [ln_ffn_g[l:l + 1] for l in range(DEPTH)], "ln_ffn_b": [ln_ffn_b[l:l + 1] for l in range(DEPTH)],
    }

    sent = {}

    def emit(k, gd):
        rows4 = lambda a: a.reshape(N_CHIPS, -1, a.shape[-1])
        order = {1: ["sw_q", "sw_out", "kv", "ffn_in", "ffn_out"], 2: ["ffn_in", "ffn_out", "hg_out"], 3: ["hg_in"]}[k]
        as_pieces = lambda a, nme: a if nme in ("ffn_in", "hg_in") else rows4(a)
        handle, token = _scatter_start([as_pieces(gd[nme][1], nme) for nme in order], name=f"scatter_g{k}_start")
        sent[k] = (handle, [as_pieces(gd[nme][0], nme) for nme in order])
        return token

    loss_tile, grad_x, g = _local_step(x[0], xb, loss_target[0], w, more_weights, emit)

    wts = dict(hgrn_w_in=hgrn_w_in, hgrn_lb_logits=hgrn_lb_logits, hgrn_gnorm_w=hgrn_gnorm_w, hgrn_w_out=hgrn_w_out,
               swa_w_q=swa_w_q, swa_sinks=swa_sinks, swa_w_out=swa_w_out, shared_w_kv=shared_w_kv, rel_bias=rel_bias,
               ffn_w_in=ffn_w_in, ffn_conv_w=ffn_conv_w, ffn_conv_b=ffn_conv_b, ffn_w_out=ffn_w_out,
               ln_mix_g=ln_mix_g, ln_mix_b=ln_mix_b, ln_ffn_g=ln_ffn_g, ln_ffn_b=ln_ffn_b)
    ms = dict(hgrn_w_in=m_hgrn_w_in, hgrn_lb_logits=m_hgrn_lb_logits, hgrn_gnorm_w=m_hgrn_gnorm_w, hgrn_w_out=m_hgrn_w_out,
              swa_w_q=m_swa_w_q, swa_sinks=m_swa_sinks, swa_w_out=m_swa_w_out, shared_w_kv=m_shared_w_kv, rel_bias=m_rel_bias,
              ffn_w_in=m_ffn_w_in, ffn_conv_w=m_ffn_conv_w, ffn_conv_b=m_ffn_conv_b, ffn_w_out=m_ffn_w_out,
              ln_mix_g=m_ln_mix_g, ln_mix_b=m_ln_mix_b, ln_ffn_g=m_ln_ffn_g, ln_ffn_b=m_ln_ffn_b)
    vs = dict(hgrn_w_in=v_hgrn_w_in, hgrn_lb_logits=v_hgrn_lb_logits, hgrn_gnorm_w=v_hgrn_gnorm_w, hgrn_w_out=v_hgrn_w_out,
              swa_w_q=v_swa_w_q, swa_sinks=v_swa_sinks, swa_w_out=v_swa_w_out, shared_w_kv=v_shared_w_kv, rel_bias=v_rel_bias,
              ffn_w_in=v_ffn_w_in, ffn_conv_w=v_ffn_conv_w, ffn_conv_b=v_ffn_conv_b, ffn_w_out=v_ffn_w_out,
              ln_mix_g=v_ln_mix_g, ln_mix_b=v_ln_mix_b, ln_ffn_g=v_ln_ffn_g, ln_ffn_b=v_ln_ffn_b)
    names = list(wts)
    grads, delta, new_m, new_v = {}, {}, {}, {}

    def update(n, ga, gb, layer=None, prev=None):
        r2 = lambda a: a.reshape(-1, a.shape[-1])
        rows = None if layer is None else (layer * ga.shape[0], ga.shape[0])
        return _adamw(r2(wts[n]), ga, gb, r2(ms[n]), r2(vs[n]), rows=rows, prev=prev,
                      name=f"adamw_{n}" + ("" if layer is None else f"_{layer}"))

    def keep(n, res):
        grads[n], delta[n], new_m[n], new_v[n] = [a.reshape(wts[n].shape) for a in res]

    chip1 = jnp.reshape(chip, (1,)).astype(jnp.int32)
    after, swaps = grad_x, {}
    for k in (1, 2, 3):
        handle, pieces = sent[k]
        lands = _scatter_wait(handle, after, name=f"scatter_g{k}_wait")
        parts = [_chip_sum(p, l, chip1, name=f"scatter_g{k}_sum{i}") for i, (p, l) in enumerate(zip(pieces, lands))]
        swaps[k], after = _swap_start(parts, name=f"scatter_g{k}_swap_start")
    for k in (1, 2, 3):
        parts, sibs = _swap_wait(swaps[k], after, name=f"scatter_g{k}_swap_wait")
        if k == 1:
            for n, ga, gb in zip(["swa_w_q", "swa_w_out", "shared_w_kv"], parts[:3], sibs[:3]):
                keep(n, update(n, ga, gb))
            ffn_in_1 = update("ffn_w_in", parts[3], sibs[3], layer=1)
            ffn_out_1 = update("ffn_w_out", parts[4], sibs[4], layer=1)
            after = ffn_out_1[3]
        elif k == 2:
            keep("ffn_w_in", update("ffn_w_in", parts[0], sibs[0], layer=0, prev=ffn_in_1))
            keep("ffn_w_out", update("ffn_w_out", parts[1], sibs[1], layer=0, prev=ffn_out_1))
            keep("hgrn_w_out", update("hgrn_w_out", parts[2], sibs[2]))
            after = new_v["hgrn_w_out"]
        else:
            keep("hgrn_w_in", update("hgrn_w_in", parts[0], sibs[0]))

    small_keys = ["lb_logits", "gnorm", "sinks", "rel_bias", "conv0", "conv1", "ln_mix0", "ln_mix1", "ln_ffn0", "ln_ffn1"]
    flat2 = lambda a: a.reshape(-1, a.shape[-1])
    sums = _sum8([loss_tile] + [flat2(g[k]) for k in small_keys], name="sum_small")
    loss = sums[0][0, 0]
    sg = {k: v.reshape(g[k].shape) for k, v in zip(small_keys, sums[1:])}
    conv = [sg["conv0"], sg["conv1"]]
    g_cw = jnp.stack([jnp.concatenate([conv[l][0, :3], conv[l][1, :3]], axis=1) for l in range(DEPTH)])
    g_cb = jnp.stack([jnp.concatenate([conv[l][0, 3], conv[l][1, 3]], axis=0) for l in range(DEPTH)])
    ln = lambda nme, r: jnp.stack([sg[nme + "0"][r], sg[nme + "1"][r]])
    small_g = dict(hgrn_lb_logits=lax.dynamic_slice_in_dim(sg["lb_logits"], chip * Dq, Dq, axis=1),
                   hgrn_gnorm_w=sg["gnorm"], swa_sinks=sg["sinks"], rel_bias=sg["rel_bias"],
                   ffn_conv_w=lax.dynamic_slice_in_dim(g_cw, chip * FC, FC, axis=2), ffn_conv_b=g_cb,
                   ln_mix_g=ln("ln_mix", 0), ln_mix_b=ln("ln_mix", 1), ln_ffn_g=ln("ln_ffn", 0), ln_ffn_b=ln("ln_ffn", 1))
    small_names = list(small_g)
    d_, m_, v_ = _adamw_small([flat2(wts[n]) for n in small_names], [flat2(small_g[n]) for n in small_names],
                              [flat2(ms[n]) for n in small_names], [flat2(vs[n]) for n in small_names], name="adamw_small")
    for n, a, b_, c_ in zip(small_names, d_, m_, v_):
        shp = wts[n].shape
        grads[n], delta[n], new_m[n], new_v[n] = small_g[n], a.reshape(shp), b_.reshape(shp), c_.reshape(shp)

    return (loss, grad_x[None], *[grads[n] for n in names], *[delta[n] for n in names],
            *[new_m[n] for n in names], *[new_v[n] for n in names])
```

```python
import math

import numpy as np
import jax
import jax.numpy as jnp
from jax import lax
from jax.experimental import pallas as pl
from jax.experimental.pallas import tpu as pltpu

F32 = jnp.float32
BF16 = jnp.bfloat16
MESH = pl.DeviceIdType.MESH

D_MODEL = 1024
DEPTH = 2
HG_HEADS = 8
HG_DIM = 128
SW_Q_HEADS = 16
SW_KV_HEADS = 4
SW_HEAD_DIM = 64
SW_GROUP = 4
SW_WINDOW = 128
REL_BUCKETS = 32
REL_MAX_DIST = 128
FFN_DIM = 2816
ALPHA = (2.0 * DEPTH) ** 0.25
LN_EPS = 1e-5
RMS_EPS = 1e-6
ADAM_LR = 0.001
ADAM_B1 = 0.9
ADAM_B2 = 0.999
ADAM_EPS = 1e-08
ADAM_WD = 0.01
ADAM_STEP = 10

VMEM_BYTES_V7X = 64 * 1024 * 1024
VMEM_LIMIT = VMEM_BYTES_V7X - 8 * 1024 * 1024
LANES = 128
SUBLANES = 8

HG_C = 64
HG_RB = 256
CONV_R = 128
N_CHIPS = 4
N_DEV = 8

ANY_SPEC = pl.BlockSpec(memory_space=pl.ANY)


def _after(body, n_in, after):
    if after is None:
        return body, [], ()

    def wrapped(*refs):
        return body(*refs[:n_in], *refs[n_in + 1:])

    return wrapped, [ANY_SPEC], (after,)


def _params(sem=None):
    return pltpu.CompilerParams(dimension_semantics=sem, vmem_limit_bytes=VMEM_LIMIT)


def _tile(n, pref, unit=LANES):
    if n <= pref:
        return n
    best = None
    for t in range(unit, pref + 1, unit):
        if n % t == 0:
            best = t
    assert best is not None, (n, pref, unit)
    return best


def _dot(a, b, ca, cb):
    nb = a.ndim - 2
    batch = tuple(range(nb))
    return lax.dot_general(a.astype(BF16), b.astype(BF16), (((nb + ca,), (nb + cb,)), (batch, batch)),
                           preferred_element_type=F32)


@jax.custom_vjp
def mm(a, b):
    return _dot(a, b, 1, 0)


@jax.custom_vjp
def mm_nt(a, b):
    return _dot(a, b, 1, 1)


@jax.custom_vjp
def mm_tn(a, b):
    return _dot(a, b, 0, 0)


mm.defvjp(lambda a, b: (mm(a, b), (a, b)), lambda r, ct: (mm_nt(ct, r[1]), mm_tn(r[0], ct)))
mm_nt.defvjp(lambda a, b: (mm_nt(a, b), (a, b)), lambda r, ct: (mm(ct, r[1]), mm_tn(ct, r[0])))
mm_tn.defvjp(lambda a, b: (mm_tn(a, b), (a, b)), lambda r, ct: (mm_nt(r[1], ct), mm(r[0], ct)))


def _split2(x):
    hi = x.astype(BF16)
    return hi, (x - hi.astype(F32)).astype(BF16)


@jax.custom_vjp
def _scores(qt, kt):
    return _dot(qt, kt, 1, 1)


def _scores_bwd(r, ct):
    (qh, ql), (kh, kl) = _split2(r[0]), _split2(r[1])
    return _dot(ct, kh, 1, 0) + _dot(ct, kl, 1, 0), _dot(ct, qh, 0, 0) + _dot(ct, ql, 0, 0)


_scores.defvjp(lambda a, b: (_scores(a, b), (a, b)), _scores_bwd)


def _split3(x):
    hi = x.astype(BF16)
    r1 = x - hi.astype(F32)
    mid = r1.astype(BF16)
    lo = (r1 - mid.astype(F32)).astype(BF16)
    return hi, mid, lo


def _cumsum_impl(x):
    ax = x.ndim - 2
    n = x.shape[ax]
    row = lax.broadcasted_iota(jnp.int32, x.shape, ax)
    d = 1
    while d < n:
        x = x + jnp.where(row >= d, pltpu.roll(x, d, ax), 0.0)
        d *= 2
    return x


def _cumsum_rev_impl(x):
    ax = x.ndim - 2
    n = x.shape[ax]
    row = lax.broadcasted_iota(jnp.int32, x.shape, ax)
    d = 1
    while d < n:
        x = x + jnp.where(row < n - d, pltpu.roll(x, n - d, ax), 0.0)
        d *= 2
    return x


@jax.custom_vjp
def _cumsum(x):
    return _cumsum_impl(x)


_cumsum.defvjp(lambda x: (_cumsum_impl(x), None), lambda _, ct: (_cumsum_rev_impl(ct),))


def _matmul(a, b, *, mode, name, out_dtype=F32, add=None, add_scale=1.0, tm=512, tn=1408, tk=1408, after=None,
            planes=None, also_bf16=False):
    assert planes in (None, "n")
    P = b.shape[0] if planes else 1
    a2, b2 = a.shape, b.shape[-2:]
    (M, K) = a2 if mode[0] == "n" else a2[::-1]
    (K2, N) = b2 if mode[1] == "n" else b2[::-1]
    assert K == K2, (a.shape, b.shape, mode)
    assert b.ndim == (3 if planes else 2)
    tm, tn, tk = _tile(M, tm), _tile(N, tn), _tile(K, tk)
    nj, nk = N // tn, K // tk
    ca, cb = (1 if mode[0] == "n" else 0), (0 if mode[1] == "n" else 1)
    a_blk, a_idx = ((tk, tm), lambda i, k: (k, i)) if mode[0] == "t" else ((tm, tk), lambda i, k: (i, k))
    b_blk, b_idx = ((tn, tk), lambda k, j: (j, k)) if mode[1] == "t" else ((tk, tn), lambda k, j: (k, j))
    a_spec = pl.BlockSpec(a_blk, lambda i, j, k: a_idx(i, k))
    if planes:
        b_spec = pl.BlockSpec((None,) + b_blk, lambda i, j, k: (j // nj,) + b_idx(k, j % nj))
        o_spec, out_shape = pl.BlockSpec((None, tm, tn), lambda i, j, k: (j // nj, i, j % nj)), (P, M, N)
    else:
        b_spec = pl.BlockSpec(b_blk, lambda i, j, k: b_idx(k, j))
        o_spec, out_shape = pl.BlockSpec((tm, tn), lambda i, j, k: (i, j)), (M, N)
    has_add = add is not None
    assert not (has_add and planes)

    def finish(r, add_ref, o_refs):
        if has_add:
            r = r + add_scale * add_ref[...]
        o_refs[0][...] = r.astype(out_dtype)
        if also_bf16:
            o_refs[1][...] = r.astype(BF16)

    def body(*refs):
        a_ref, b_ref = refs[:2]
        add_ref = refs[2] if has_add else None
        first = 3 if has_add else 2
        o_ref = refs[first:first + (2 if also_bf16 else 1)]
        if nk == 1:
            finish(_dot(a_ref[...], b_ref[...], ca, cb), add_ref, o_ref)
            return
        acc_ref = refs[-1]
        k = pl.program_id(2)

        @pl.when(k == 0)
        def _():
            acc_ref[...] = jnp.zeros_like(acc_ref)

        acc_ref[...] += _dot(a_ref[...], b_ref[...], ca, cb)

        @pl.when(k == nk - 1)
        def _():
            finish(acc_ref[...], add_ref, o_ref)

    in_specs = [a_spec, b_spec] + ([o_spec] if has_add else [])
    args = (a, b) + ((add,) if has_add else ())
    body, xs, xa = _after(body, len(args), after)
    in_specs, args = in_specs + xs, args + xa
    out_shapes = [jax.ShapeDtypeStruct(out_shape, out_dtype)] + ([jax.ShapeDtypeStruct(out_shape, BF16)] if also_bf16 else [])
    out = pl.pallas_call(
        body, name=name, grid=(M // tm, nj * (P if planes == "n" else 1), nk), in_specs=in_specs,
        out_specs=[o_spec] * len(out_shapes), out_shape=out_shapes,
        scratch_shapes=[pltpu.VMEM((tm, tn), F32)] if nk > 1 else [],
        compiler_params=_params(("parallel", "parallel", "arbitrary")),
    )(*args)
    return tuple(out) if also_bf16 else out[0]


def _matmul_planes_nn(a, b, *, name, tm=512, after=None):
    (M, K), (P, K2, N) = a.shape, b.shape
    assert K == K2
    tm = _tile(M, tm, 2 * SUBLANES)

    def body(a_ref, b_ref, o_ref):
        for p in range(P):
            o_ref[p] = _dot(a_ref[...], b_ref[p], 1, 0).astype(BF16)

    body, xs, xa = _after(body, 2, after)
    return pl.pallas_call(
        body, name=name, grid=(M // tm,),
        in_specs=[pl.BlockSpec((tm, K), lambda i: (i, 0)), pl.BlockSpec((P, K, N), lambda i: (0, 0, 0))] + xs,
        out_specs=pl.BlockSpec((P, tm, N), lambda i: (0, i, 0)), out_shape=jax.ShapeDtypeStruct((P, M, N), BF16),
        compiler_params=_params(("parallel",)),
    )(a, b, *xa)


def _matmul_planes_nt(a, b, add, *, add_scale, name, tm=512, after=None):
    (P, M, K), (P2, N, K2) = a.shape, b.shape
    assert P == P2 and K == K2 and add.shape == (M, N)
    tm = _tile(M, tm, SUBLANES)

    def body(a_ref, b_ref, add_ref, o_ref):
        r = add_scale * add_ref[...]
        for p in range(P):
            r = r + _dot(a_ref[p], b_ref[p], 1, 1)
        o_ref[...] = r

    row = pl.BlockSpec((tm, N), lambda i: (i, 0))
    body, xs, xa = _after(body, 3, after)
    return pl.pallas_call(
        body, name=name, grid=(M // tm,),
        in_specs=[pl.BlockSpec((P, tm, K), lambda i: (0, i, 0)), pl.BlockSpec((P, N, K), lambda i: (0, 0, 0)), row] + xs,
        out_specs=row, out_shape=jax.ShapeDtypeStruct((M, N), F32),
        compiler_params=_params(("parallel",)),
    )(a, b, add, *xa)


def _ln(z, g, b):
    mu = jnp.mean(z, axis=-1, keepdims=True)
    zc = z - mu
    var = jnp.mean(zc * zc, axis=-1, keepdims=True)
    return zc * lax.rsqrt(var + LN_EPS) * g + b


def _matmul_ln(a, b, h, g, bias, *, name, tgt=None, tm=512, a_t=False):
    (T, K), (K2, Dm) = (a.shape[::-1] if a_t else a.shape), b.shape
    assert K == K2 and h.shape == (T, Dm)
    tm = _tile(T, tm, SUBLANES)
    last = tgt is not None

    def body(*refs):
        a_ref, b_ref, h_ref, g_ref, bias_ref = refs[:5]
        z = ALPHA * h_ref[...] + _dot(a_ref[...], b_ref[...], 0 if a_t else 1, 0)
        if not last:
            z_ref, y_ref, yb_ref = refs[5:]
            y = _ln(z, g_ref[...], bias_ref[...])
            z_ref[...] = z
            y_ref[...] = y
            yb_ref[...] = y.astype(BF16)
            return
        t_ref, dz_ref, dzb_ref, dgb_ref, l_ref, da_ref = refs[5:]

        @pl.when(pl.program_id(0) == 0)
        def _():
            dgb_ref[...] = jnp.zeros_like(dgb_ref)
            l_ref[...] = jnp.zeros_like(l_ref)

        y, vjp = jax.vjp(_ln, z, g_ref[...], bias_ref[...])
        e = y - t_ref[...]
        dz, dg, db = vjp(e * (1.0 / Dm))
        l_ref[...] += 0.5 * jnp.sum(jnp.mean(e * e, axis=-1, keepdims=True), axis=0, keepdims=True)
        dzb = dz.astype(BF16)
        dz_ref[...] = dz
        dzb_ref[...] = dzb
        dgb_ref[...] += jnp.concatenate([dg, db], axis=0)
        da_ref[...] = _dot(dzb, b_ref[...], 1, 1).astype(BF16)

    row = pl.BlockSpec((tm, Dm), lambda i: (i, 0))
    vec = pl.BlockSpec((1, Dm), lambda i: (0, 0))
    a_spec = pl.BlockSpec((K, tm), lambda i: (0, i)) if a_t else pl.BlockSpec((tm, K), lambda i: (i, 0))
    in_specs = [a_spec, pl.BlockSpec((K, Dm), lambda i: (0, 0)), row, vec, vec]
    f32, b16 = jax.ShapeDtypeStruct((T, Dm), F32), jax.ShapeDtypeStruct((T, Dm), BF16)
    if not last:
        return pl.pallas_call(
            body, name=name, grid=(T // tm,), in_specs=in_specs, out_specs=[row, row, row], out_shape=[f32, f32, b16],
            compiler_params=_params(("parallel",)),
        )(a, b, h, g, bias)
    assert not a_t
    return pl.pallas_call(
        body, name=name, grid=(T // tm,), in_specs=in_specs + [row],
        out_specs=[row, row, pl.BlockSpec((2, Dm), lambda i: (0, 0)), pl.BlockSpec((SUBLANES, LANES), lambda i: (0, 0)), a_spec],
        out_shape=[f32, b16, jax.ShapeDtypeStruct((2, Dm), F32), jax.ShapeDtypeStruct((SUBLANES, LANES), F32),
                   jax.ShapeDtypeStruct((T, K), BF16)],
        compiler_params=_params(("arbitrary",)),
    )(a, b, h, g, bias, tgt)


def _ln_bwd_matmul(dy, z, g, b, w, *, name, out_t=False, tm=512, after=None):
    T, Dm = z.shape
    N = w.shape[0]
    tm = _tile(T, tm, LANES if out_t else SUBLANES)

    def body(dy_ref, z_ref, g_ref, b_ref, w_ref, dz_ref, dzb_ref, dgb_ref, o_ref):
        @pl.when(pl.program_id(0) == 0)
        def _():
            dgb_ref[...] = jnp.zeros_like(dgb_ref)

        _, vjp = jax.vjp(_ln, z_ref[...], g_ref[...], b_ref[...])
        dz, dg, db = vjp(dy_ref[...])
        dzb = dz.astype(BF16)
        dz_ref[...] = dz
        dzb_ref[...] = dzb
        dgb_ref[...] += jnp.concatenate([dg, db], axis=0)
        o_ref[...] = (_dot(w_ref[...], dzb, 1, 1) if out_t else _dot(dzb, w_ref[...], 1, 1)).astype(BF16)

    row = pl.BlockSpec((tm, Dm), lambda i: (i, 0))
    vec = pl.BlockSpec((1, Dm), lambda i: (0, 0))
    o_spec = pl.BlockSpec((N, tm), lambda i: (0, i)) if out_t else pl.BlockSpec((tm, N), lambda i: (i, 0))
    body, xs, xa = _after(body, 5, after)
    return pl.pallas_call(
        body, name=name, grid=(T // tm,), in_specs=[row, row, vec, vec, pl.BlockSpec((N, Dm), lambda i: (0, 0))] + xs,
        out_specs=[row, row, pl.BlockSpec((2, Dm), lambda i: (0, 0)), o_spec],
        out_shape=[jax.ShapeDtypeStruct((T, Dm), F32), jax.ShapeDtypeStruct((T, Dm), BF16),
                   jax.ShapeDtypeStruct((2, Dm), F32), jax.ShapeDtypeStruct((N, T) if out_t else (T, N), BF16)],
        compiler_params=_params(("arbitrary",)),
    )(dy, z, g, b, w, *xa)


def _hg_chunk(qr, fr, ir, gr, l0, l1, gw, st):
    C = qr.shape[-2]
    row = lax.broadcasted_iota(jnp.int32, qr.shape, qr.ndim - 2)
    lb = jax.nn.sigmoid(l0 - l1)
    fg = lb + (1.0 - lb) * jax.nn.sigmoid(fr)
    b = _cumsum(jnp.log(fg))
    q = jax.nn.silu(qr)
    k = 1.0 - fg
    bmid = lax.stop_gradient(jnp.sum(jnp.where(row == C // 2 - 1, b, 0.0), axis=-2, keepdims=True))
    bl = jnp.sum(jnp.where(row == C - 1, b, 0.0), axis=-2, keepdims=True)
    o = mm_nt(q * jnp.exp(b), st)
    sc = _scores(q * jnp.exp(b - bmid), k * jnp.exp(bmid - b))
    ti = lax.broadcasted_iota(jnp.int32, (C, C), 0)
    si = lax.broadcasted_iota(jnp.int32, (C, C), 1)
    sc = jnp.where(si <= ti, sc, 0.0)
    o = o + mm(sc, ir)
    st_new = st * jnp.exp(bl) + mm_tn(ir, k * jnp.exp(bl - b))
    on = o * lax.rsqrt(jnp.mean(o * o, axis=-1, keepdims=True) + RMS_EPS)
    return on * gw * jax.nn.silu(gr), st_new


def _heads(ref, rows):
    return jnp.stack([ref[rows, h * HG_DIM:(h + 1) * HG_DIM].astype(F32) for h in range(HG_HEADS)])


def _unheads(x):
    return jnp.concatenate([x[h] for h in range(HG_HEADS)], axis=-1)


def _hgrn_fwd(pre, lbl, gw, *, name):
    _, T, Dm = pre.shape
    rb = min(HG_RB, T)
    C = min(HG_C, rb)
    ncb = rb // C

    def body(pre_ref, lbl_ref, gw_ref, o_ref, st_ref, s_ref):
        @pl.when(pl.program_id(0) == 0)
        def _():
            s_ref[...] = jnp.zeros_like(s_ref)

        def chunk(ci, carry):
            r0 = pl.multiple_of(ci * C, C)
            rows = pl.ds(r0, C)
            st = s_ref[...]
            st_ref[ci] = st
            out, st_new = _hg_chunk(*[_heads(pre_ref.at[j], rows) for j in range(4)],
                                    _heads(lbl_ref, slice(0, 1)), _heads(lbl_ref, slice(1, 2)), gw_ref[...], st)
            o_ref[rows, :] = _unheads(out).astype(BF16)
            s_ref[...] = st_new
            return carry

        lax.fori_loop(0, ncb, chunk, 0, unroll=True)

    row = pl.BlockSpec((rb, Dm), lambda n: (n, 0))
    return pl.pallas_call(
        body, name=name, grid=(T // rb,),
        in_specs=[pl.BlockSpec((4, rb, Dm), lambda n: (0, n, 0)), pl.BlockSpec((2, Dm), lambda n: (0, 0)),
                  pl.BlockSpec((1, HG_DIM), lambda n: (0, 0))],
        out_specs=[row, pl.BlockSpec((ncb, HG_HEADS, HG_DIM, HG_DIM), lambda n: (n, 0, 0, 0))],
        out_shape=[jax.ShapeDtypeStruct((T, Dm), BF16),
                   jax.ShapeDtypeStruct((T // C, HG_HEADS, HG_DIM, HG_DIM), F32)],
        scratch_shapes=[pltpu.VMEM((HG_HEADS, HG_DIM, HG_DIM), F32)],
        compiler_params=_params(("arbitrary",)),
    )(pre, lbl, gw)


def _hgrn_bwd(pre, lbl, gw, states, dout, *, name, after=None):
    _, T, Dm = pre.shape
    rb = min(HG_RB, T)
    C = min(HG_C, rb)
    ncb = rb // C
    nb = T // rb

    def body(pre_ref, lbl_ref, gw_ref, st_ref, do_ref, dpre_ref, dlbl_ref, dgw_ref, ds_ref):
        @pl.when(pl.program_id(0) == 0)
        def _():
            ds_ref[...] = jnp.zeros_like(ds_ref)
            dlbl_ref[...] = jnp.zeros_like(dlbl_ref)
            dgw_ref[...] = jnp.zeros_like(dgw_ref)

        def chunk(cj, carry):
            ci = ncb - 1 - cj
            r0 = pl.multiple_of(ci * C, C)
            rows = pl.ds(r0, C)
            _, vjp = jax.vjp(_hg_chunk, *[_heads(pre_ref.at[j], rows) for j in range(4)],
                             _heads(lbl_ref, slice(0, 1)), _heads(lbl_ref, slice(1, 2)), gw_ref[...], st_ref[ci])
            *dpre, dl0, dl1, dgw, dst = vjp((_heads(do_ref, rows), ds_ref[...]))
            for j in range(4):
                dpre_ref[j, rows, :] = _unheads(dpre[j]).astype(BF16)
            dlbl_ref[0:1, :] += _unheads(dl0)
            dlbl_ref[1:2, :] += _unheads(dl1)
            dgw_ref[...] += dgw
            ds_ref[...] = dst
            return carry

        lax.fori_loop(0, ncb, chunk, 0, unroll=True)

    row = pl.BlockSpec((rb, Dm), lambda n: (nb - 1 - n, 0))
    lsp = pl.BlockSpec((2, Dm), lambda n: (0, 0))
    gsp = pl.BlockSpec((1, HG_DIM), lambda n: (0, 0))
    pre_spec = pl.BlockSpec((4, rb, Dm), lambda n: (0, nb - 1 - n, 0))
    body, xs, xa = _after(body, 5, after)
    return pl.pallas_call(
        body, name=name, grid=(nb,),
        in_specs=[pre_spec, lsp, gsp, pl.BlockSpec((ncb, HG_HEADS, HG_DIM, HG_DIM), lambda n: (nb - 1 - n, 0, 0, 0)), row] + xs,
        out_specs=[pre_spec, lsp, gsp],
        out_shape=[jax.ShapeDtypeStruct((4, T, Dm), BF16), jax.ShapeDtypeStruct((2, Dm), F32),
                   jax.ShapeDtypeStruct((1, HG_DIM), F32)],
        scratch_shapes=[pltpu.VMEM((HG_HEADS, HG_DIM, HG_DIM), F32)],
        compiler_params=_params(("arbitrary",)),
    )(pre, lbl, gw, states, dout, *xa)


CONV_HALO = 2 * SUBLANES


def _conv_rows(u_ref, scr, w, bias, r0, R):
    cur = u_ref[pl.ds(r0, R), :].astype(F32)
    p0 = pl.multiple_of(jnp.maximum(r0 - CONV_HALO, 0), CONV_HALO)
    scr[0:CONV_HALO, :] = jnp.where(r0 > 0, u_ref[pl.ds(p0, CONV_HALO), :].astype(F32), 0.0)
    scr[CONV_HALO:CONV_HALO + R, :] = cur
    s1 = scr[CONV_HALO - 1:CONV_HALO - 1 + R, :]
    s2 = scr[CONV_HALO - 2:CONV_HALO - 2 + R, :]
    return w[0:1, :] * s2 + w[1:2, :] * s1 + w[2:3, :] * cur + bias, cur, s1, s2


def _halves_spec(T, Fd):
    per = Fd // 2 // LANES
    return pl.BlockSpec((2, None, T, LANES), lambda j: (0, j // per, 0, j % per))


def _conv_gate_fwd(u, wa, wb, ba, bb, *, name):
    T, Fd = u.shape[2], 2 * u.shape[3]
    R = min(CONV_R, T)
    tc = LANES

    def body(u_ref, wa_ref, wb_ref, ba_ref, bb_ref, o_ref, sa, sb):
        wa_, wb_, ba_, bb_ = wa_ref[...], wb_ref[...], ba_ref[...], bb_ref[...]

        def step(ri, carry):
            r0 = pl.multiple_of(ri * R, R)
            ca = _conv_rows(u_ref.at[0], sa, wa_, ba_, r0, R)[0]
            cb = _conv_rows(u_ref.at[1], sb, wb_, bb_, r0, R)[0]
            o_ref[pl.ds(r0, R), :] = (jax.nn.silu(ca) * cb).astype(BF16)
            return carry

        lax.fori_loop(0, T // R, step, 0)

    col = pl.BlockSpec((T, tc), lambda j: (0, j))
    wsp = pl.BlockSpec((3, tc), lambda j: (0, j))
    bsp = pl.BlockSpec((1, tc), lambda j: (0, j))
    both = _halves_spec(T, Fd)
    return pl.pallas_call(
        body, name=name, grid=(Fd // tc,), in_specs=[both, wsp, wsp, bsp, bsp], out_specs=col,
        out_shape=jax.ShapeDtypeStruct((T, Fd), BF16),
        scratch_shapes=[pltpu.VMEM((CONV_HALO + R, tc), F32)] * 2,
        compiler_params=_params(("parallel",)),
    )(u, wa, wb, ba, bb)


def _conv_gate_bwd(u, wa, wb, ba, bb, dact, *, name):
    T, Fd = u.shape[2], 2 * u.shape[3]
    R = min(CONV_R, T)
    nr = T // R
    tc = LANES

    def body(u_ref, wa_ref, wb_ref, ba_ref, bb_ref, da_ref,
             du_ref, dp_ref, sa, sb, sda, sdb):
        wa_, wb_, ba_, bb_ = wa_ref[...], wb_ref[...], ba_ref[...], bb_ref[...]
        sda[R:R + SUBLANES, :] = jnp.zeros((SUBLANES, tc), F32)
        sdb[R:R + SUBLANES, :] = jnp.zeros((SUBLANES, tc), F32)

        def taps(dc, cur, s1, s2):
            return jnp.concatenate([jnp.sum(dc * s2, axis=0, keepdims=True), jnp.sum(dc * s1, axis=0, keepdims=True),
                                    jnp.sum(dc * cur, axis=0, keepdims=True)], axis=0)

        def du_rows(sd, dc, w):
            sd[0:R, :] = dc
            du = w[2:3, :] * dc + w[1:2, :] * sd[1:1 + R, :] + w[0:1, :] * sd[2:2 + R, :]
            sd[R:R + SUBLANES, :] = dc[0:SUBLANES]
            return du

        def step(rj, carry):
            dwa, dwb, dba, dbb = carry
            r0 = pl.multiple_of((nr - 1 - rj) * R, R)
            ca, cura, s1a, s2a = _conv_rows(u_ref.at[0], sa, wa_, ba_, r0, R)
            cb, curb, s1b, s2b = _conv_rows(u_ref.at[1], sb, wb_, bb_, r0, R)
            dact_ = da_ref[pl.ds(r0, R), :].astype(F32)
            sg = jax.nn.sigmoid(ca)
            dca = dact_ * cb * (sg * (1.0 + ca * (1.0 - sg)))
            dcb = dact_ * (ca * sg)
            du_ref[0, pl.ds(r0, R), :] = du_rows(sda, dca, wa_).astype(BF16)
            du_ref[1, pl.ds(r0, R), :] = du_rows(sdb, dcb, wb_).astype(BF16)
            return (dwa + taps(dca, cura, s1a, s2a), dwb + taps(dcb, curb, s1b, s2b),
                    dba + jnp.sum(dca, axis=0, keepdims=True), dbb + jnp.sum(dcb, axis=0, keepdims=True))

        z3 = jnp.zeros((3, tc), F32)
        z1 = jnp.zeros((1, tc), F32)
        dwa, dwb, dba, dbb = lax.fori_loop(0, nr, step, (z3, z3, z1, z1))
        dp_ref[0] = jnp.concatenate([dwa, dba], axis=0)
        dp_ref[1] = jnp.concatenate([dwb, dbb], axis=0)

    col = pl.BlockSpec((T, tc), lambda j: (0, j))
    wsp = pl.BlockSpec((3, tc), lambda j: (0, j))
    bsp = pl.BlockSpec((1, tc), lambda j: (0, j))
    both = _halves_spec(T, Fd)
    return pl.pallas_call(
        body, name=name, grid=(Fd // tc,), in_specs=[both, wsp, wsp, bsp, bsp, col],
        out_specs=[both, pl.BlockSpec((2, 4, tc), lambda j: (0, 0, j))],
        out_shape=[jax.ShapeDtypeStruct(u.shape, BF16), jax.ShapeDtypeStruct((2, 4, Fd), F32)],
        scratch_shapes=[pltpu.VMEM((CONV_HALO + R, tc), F32)] * 2 + [pltpu.VMEM((R + SUBLANES, tc), F32)] * 2,
        compiler_params=_params(("parallel",)),
    )(u, wa, wb, ba, bb, dact)


def _bucket_index():
    t = np.arange(SW_WINDOW)[None, :] + SW_WINDOW
    s = np.arange(2 * SW_WINDOW)[:, None]
    dist = np.maximum(t - s, 0)
    exact = REL_BUCKETS // 2
    d = np.maximum(dist, 1).astype(np.float32)
    log_b = exact + (np.log(d / np.float32(exact)) / np.float32(math.log(REL_MAX_DIST / exact))
                     * np.float32(REL_BUCKETS - exact)).astype(np.int32)
    bucket = np.where(dist < exact, dist, np.minimum(log_b, REL_BUCKETS - 1))
    return bucket.astype(np.int32).reshape(1, -1)


BIAS_COLS = SW_WINDOW * 2 * SW_WINDOW
BIAS_TILE = 4096


def _bias_from_table(table, bucket, *, name):
    def body(t_ref, idx_ref, o_ref):
        onehot = (lax.broadcasted_iota(jnp.int32, (REL_BUCKETS, BIAS_TILE), 0) == idx_ref[...]).astype(BF16)
        acc = jnp.zeros((SW_Q_HEADS, BIAS_TILE), F32)
        for piece in _split3(t_ref[...]):
            acc = acc + lax.dot_general(piece, onehot, (((0,), (0,)), ((), ())), preferred_element_type=F32)
        o_ref[...] = acc

    return pl.pallas_call(
        body, name=name, grid=(BIAS_COLS // BIAS_TILE,),
        in_specs=[pl.BlockSpec((REL_BUCKETS, SW_Q_HEADS), lambda j: (0, 0)), pl.BlockSpec((1, BIAS_TILE), lambda j: (0, j))],
        out_specs=pl.BlockSpec((SW_Q_HEADS, BIAS_TILE), lambda j: (0, j)),
        out_shape=jax.ShapeDtypeStruct((SW_Q_HEADS, BIAS_COLS), F32),
        compiler_params=_params(("parallel",)),
    )(table, bucket)


def _table_grad(dbias, bucket, *, name):
    def body(d_ref, idx_ref, o_ref):
        @pl.when(pl.program_id(0) == 0)
        def _():
            o_ref[...] = jnp.zeros_like(o_ref)

        onehot = (lax.broadcasted_iota(jnp.int32, (REL_BUCKETS, BIAS_TILE), 0) == idx_ref[...]).astype(BF16)
        acc = jnp.zeros((REL_BUCKETS, SW_Q_HEADS), F32)
        for piece in _split3(d_ref[...]):
            acc = acc + lax.dot_general(onehot, piece, (((1,), (1,)), ((), ())), preferred_element_type=F32)
        o_ref[...] += acc

    return pl.pallas_call(
        body, name=name, grid=(BIAS_COLS // BIAS_TILE,),
        in_specs=[pl.BlockSpec((SW_Q_HEADS, BIAS_TILE), lambda j: (0, j)), pl.BlockSpec((1, BIAS_TILE), lambda j: (0, j))],
        out_specs=pl.BlockSpec((REL_BUCKETS, SW_Q_HEADS), lambda j: (0, 0)),
        out_shape=jax.ShapeDtypeStruct((REL_BUCKETS, SW_Q_HEADS), F32),
        compiler_params=_params(("arbitrary",)),
    )(dbias, bucket)


KV_DIM = SW_KV_HEADS * SW_HEAD_DIM
GROUP_ROWS = SW_GROUP * SW_HEAD_DIM
GROUP_LANES = SW_GROUP * SW_WINDOW


def _band_mask(n):
    s = lax.broadcasted_iota(jnp.int32, (2 * SW_WINDOW, GROUP_LANES), 0)
    t = (lax.broadcasted_iota(jnp.int32, (2 * SW_WINDOW, GROUP_LANES), 1) & (SW_WINDOW - 1)) + SW_WINDOW
    dist = t - s
    return (dist >= 0) & (dist < SW_WINDOW) & ((n > 0) | (s >= SW_WINDOW))


def _side_by_side(x_ref, g):
    r0 = g * GROUP_ROWS
    return jnp.concatenate([x_ref[r0 + r * SW_HEAD_DIM:r0 + (r + 1) * SW_HEAD_DIM, :] for r in range(SW_GROUP)], axis=1)


def _group_inputs(bias_ref, sink_ref, g):
    heads = range(g * SW_GROUP, (g + 1) * SW_GROUP)
    bias = jnp.concatenate([bias_ref[h] for h in heads], axis=1)
    sink = jnp.concatenate([jnp.broadcast_to(sink_ref[:, h:h + 1], (1, SW_WINDOW)) for h in heads], axis=1)
    return heads, bias, sink


def _kv_pair(kvp_ref, kvc_ref, g):
    ks = slice(g * SW_HEAD_DIM, (g + 1) * SW_HEAD_DIM)
    vs = slice(KV_DIM + g * SW_HEAD_DIM, KV_DIM + (g + 1) * SW_HEAD_DIM)
    kk = jnp.concatenate([kvp_ref[:, ks], kvc_ref[:, ks]], axis=0)
    vv = jnp.concatenate([kvp_ref[:, vs], kvc_ref[:, vs]], axis=0)
    return kk, vv, ks, vs


def _col_max(x):
    return jnp.max(x, axis=0, keepdims=True)


def _col_sum(x):
    return jnp.sum(x, axis=0, keepdims=True)


def _attn_fwd(qt, kv, bias, sinks, *, name):
    Dm, T = qt.shape
    W = SW_WINDOW

    def body(q_ref, kvc_ref, kvp_ref, bias_ref, sink_ref, o_ref):
        mask = _band_mask(pl.program_id(0))
        G = range(SW_KV_HEADS)
        ins = [_group_inputs(bias_ref, sink_ref, g) for g in G]
        kvs = [_kv_pair(kvp_ref, kvc_ref, g) for g in G]
        q = [_side_by_side(q_ref, g) for g in G]
        lg = [jnp.where(mask, mm(kvs[g][0], q[g]) * (SW_HEAD_DIM ** -0.5) + ins[g][1], -jnp.inf) for g in G]
        m = [jnp.maximum(_col_max(lg[g]), ins[g][2]) for g in G]
        p = [jnp.exp(lg[g] - m[g]) for g in G]
        den = [_col_sum(p[g]) + jnp.exp(ins[g][2] - m[g]) for g in G]
        o = [mm_tn(kvs[g][1], p[g]) / den[g] for g in G]
        for g in G:
            for r in range(SW_GROUP):
                o_ref[g * GROUP_ROWS + r * SW_HEAD_DIM:g * GROUP_ROWS + (r + 1) * SW_HEAD_DIM, :] = (
                    o[g][:, r * W:(r + 1) * W].astype(BF16))

    return pl.pallas_call(
        body, name=name, grid=(T // W,),
        in_specs=[pl.BlockSpec((Dm, W), lambda n: (0, n)),
                  pl.BlockSpec((W, 2 * KV_DIM), lambda n: (n, 0)),
                  pl.BlockSpec((W, 2 * KV_DIM), lambda n: (jnp.maximum(n - 1, 0), 0)),
                  pl.BlockSpec((SW_Q_HEADS, 2 * W, W), lambda n: (0, 0, 0)),
                  pl.BlockSpec((1, SW_Q_HEADS), lambda n: (0, 0))],
        out_specs=pl.BlockSpec((Dm, W), lambda n: (0, n)),
        out_shape=jax.ShapeDtypeStruct((Dm, T), BF16),
        compiler_params=_params(("parallel",)),
    )(qt, kv, kv, bias, sinks)


def _attn_bwd(qt, kv, bias, sinks, dot, *, name):
    Dm, T = qt.shape
    W = SW_WINDOW
    nb = T // W

    def body(q_ref, kvc_ref, kvp_ref, bias_ref, sink_ref, do_ref,
             dq_ref, dkv_ref, dbias_ref, dsink_ref, carry_ref):
        @pl.when(pl.program_id(0) == 0)
        def _():
            carry_ref[...] = jnp.zeros_like(carry_ref)
            dbias_ref[...] = jnp.zeros_like(dbias_ref)
            dsink_ref[...] = jnp.zeros_like(dsink_ref)

        n = nb - 1 - pl.program_id(0)
        mask = _band_mask(n)
        lane = lax.broadcasted_iota(jnp.int32, (1, SW_Q_HEADS), 1)
        sc = SW_HEAD_DIM ** -0.5
        G = range(SW_KV_HEADS)
        ins = [_group_inputs(bias_ref, sink_ref, g) for g in G]
        kvs = [_kv_pair(kvp_ref, kvc_ref, g) for g in G]
        q = [_side_by_side(q_ref, g) for g in G]
        do = [_side_by_side(do_ref, g) for g in G]
        lg = [jnp.where(mask, mm(kvs[g][0], q[g]) * sc + ins[g][1], -jnp.inf) for g in G]
        m = [jnp.maximum(_col_max(lg[g]), ins[g][2]) for g in G]
        p = [jnp.exp(lg[g] - m[g]) for g in G]
        ps = [jnp.exp(ins[g][2] - m[g]) for g in G]
        rden = [1.0 / (_col_sum(p[g]) + ps[g]) for g in G]
        pn = [p[g] * rden[g] for g in G]
        dpn = [mm(kvs[g][1], do[g]) for g in G]
        delta = [_col_sum(pn[g] * dpn[g]) for g in G]
        ds = [pn[g] * (dpn[g] - delta[g]) for g in G]
        dsr = [-(ps[g] * rden[g]) * delta[g] for g in G]
        dq = [mm_tn(kvs[g][0], ds[g]) * sc for g in G]
        dkk = [mm_nt(ds[g], q[g]) * sc for g in G]
        dvv = [mm_nt(pn[g], do[g]) for g in G]
        dsink = jnp.zeros((1, SW_Q_HEADS), F32)
        for g in G:
            _, _, ks, vs = kvs[g]
            for r, h in enumerate(ins[g][0]):
                cols = slice(r * W, (r + 1) * W)
                dbias_ref[h] += ds[g][:, cols]
                dq_ref[g * GROUP_ROWS + r * SW_HEAD_DIM:g * GROUP_ROWS + (r + 1) * SW_HEAD_DIM, :] = dq[g][:, cols].astype(BF16)
                dsink = dsink + jnp.where(lane == h, jnp.sum(dsr[g][:, cols], axis=1, keepdims=True), 0.0)
            dkv_ref[:, ks] = (carry_ref[:, ks] + dkk[g][W:]).astype(BF16)
            dkv_ref[:, vs] = (carry_ref[:, vs] + dvv[g][W:]).astype(BF16)
            carry_ref[:, ks] = dkk[g][:W]
            carry_ref[:, vs] = dvv[g][:W]
        dsink_ref[...] += dsink

    rev = lambda n: (nb - 1 - n, 0)
    revt = lambda n: (0, nb - 1 - n)
    return pl.pallas_call(
        body, name=name, grid=(nb,),
        in_specs=[pl.BlockSpec((Dm, W), revt),
                  pl.BlockSpec((W, 2 * KV_DIM), rev),
                  pl.BlockSpec((W, 2 * KV_DIM), lambda n: (jnp.maximum(nb - 2 - n, 0), 0)),
                  pl.BlockSpec((SW_Q_HEADS, 2 * W, W), lambda n: (0, 0, 0)),
                  pl.BlockSpec((1, SW_Q_HEADS), lambda n: (0, 0)),
                  pl.BlockSpec((Dm, W), revt)],
        out_specs=[pl.BlockSpec((Dm, W), revt), pl.BlockSpec((W, 2 * KV_DIM), rev),
                   pl.BlockSpec((SW_Q_HEADS, 2 * W, W), lambda n: (0, 0, 0)),
                   pl.BlockSpec((1, SW_Q_HEADS), lambda n: (0, 0))],
        out_shape=[jax.ShapeDtypeStruct((Dm, T), BF16), jax.ShapeDtypeStruct((T, 2 * KV_DIM), BF16),
                   jax.ShapeDtypeStruct((SW_Q_HEADS, 2 * W, W), F32), jax.ShapeDtypeStruct((1, SW_Q_HEADS), F32)],
        scratch_shapes=[pltpu.VMEM((W, 2 * KV_DIM), F32)],
        compiler_params=_params(("arbitrary",)),
    )(qt, kv, kv, bias, sinks, dot)


def _ffn_fwd(hb, w, l, after=None):
    u = _matmul_planes_nn(hb, w["ffn_in"][l], name=f"ffn{l}_up", after=after)
    u = u.reshape((2, 2) + u.shape[1:])
    act = _conv_gate_fwd(u, w["conv_w_a"][l], w["conv_w_b"][l], w["conv_b_a"][l], w["conv_b_b"][l],
                         name=f"ffn{l}_conv_gate")
    return u, act


def _ffn_bwd(dffb, dh_scaled, hb, u, act, w, l, dact):
    g_out = _matmul(act, dffb, mode="tn", name=f"ffn{l}_down_dw", tm=1408, tn=1024, tk=2048, also_bf16=True)
    du, g_conv = _conv_gate_bwd(u, w["conv_w_a"][l], w["conv_w_b"][l], w["conv_b_a"][l], w["conv_b_b"][l],
                                dact, name=f"ffn{l}_conv_gate_bwd")
    du = du.reshape((N_CHIPS,) + du.shape[2:])
    dh = _matmul_planes_nt(du, w["ffn_in"][l], dh_scaled, add_scale=ALPHA, name=f"ffn{l}_up_dx")
    g_in = _matmul(hb, du, mode="tn", planes="n", name=f"ffn{l}_up_dw", tm=1024, tn=FFN_DIM // 2, tk=2048, also_bf16=True)
    return dh, dict(ffn_out=g_out, ffn_in=g_in, conv=g_conv)


def _local_step(x, xb, tgt, w, more_weights, emit):
    bucket = jnp.asarray(_bucket_index())

    pre = _matmul_planes_nn(xb, w["hg_in"], name="hg_in", tm=1024, after=w.get("token"))
    og, states = _hgrn_fwd(pre, w["lb_logits"], w["gnorm"], name="hgrn_fwd")
    z1, h1, h1b = _matmul_ln(og, w["hg_out"], x, w["ln_mix_g"][0], w["ln_mix_b"][0], tm=1024, name="hg_out_ln")
    w = {**w, **more_weights(1, h1b)}
    u0, act0 = _ffn_fwd(h1b, w, 0, after=w.get("token"))
    z2, h2, h2b = _matmul_ln(act0, w["ffn_out"][0], h1, w["ln_ffn_g"][0], w["ln_ffn_b"][0], name="ffn0_down_ln")
    kv = _matmul(h2b, w["kv"], mode="nn", out_dtype=BF16, name="kv_proj", tm=1024)

    bias = _bias_from_table(w["rel_bias"], bucket, name="rel_bias_expand").reshape(SW_Q_HEADS, 2 * SW_WINDOW, SW_WINDOW)
    q1 = _matmul(w["sw_q"], h2b, mode="tt", out_dtype=BF16, name="sw_q", tm=1024, tn=1024)
    o1 = _attn_fwd(q1, kv, bias, w["sinks"], name="attn_fwd")
    z3, h3, h3b = _matmul_ln(o1, w["sw_out"], h2, w["ln_mix_g"][1], w["ln_mix_b"][1], a_t=True, tm=1024, name="sw_out_ln")
    w = {**w, **more_weights(2, h3b)}
    u1, act1 = _ffn_fwd(h3b, w, 1)

    g = {}
    dz, dzb, g["ln_ffn1"], loss_tile, dact1 = _matmul_ln(act1, w["ffn_out"][1], h3, w["ln_ffn_g"][1], w["ln_ffn_b"][1],
                                                         tgt=tgt, name="ffn1_down_ln_loss")

    dh3, gf1 = _ffn_bwd(dzb, dz, h3b, u1, act1, w, 1, dact1)
    dz, dzb, g["ln_mix1"], do1 = _ln_bwd_matmul(dh3, z3, w["ln_mix_g"][1], w["ln_mix_b"][1], w["sw_out"], out_t=True,
                                                name="ln_mix1_bwd_sw_out_dx")
    g_sw_out = _matmul(o1, dzb, mode="nn", name="sw_out_dw", tm=1024, tn=1024, tk=2048, also_bf16=True)
    dq1, dkv, dbias, dsinks = _attn_bwd(q1, kv, bias, w["sinks"], do1, name="attn_bwd")
    g["sinks"] = dsinks
    g["rel_bias"] = _table_grad(dbias.reshape(SW_Q_HEADS, BIAS_COLS), bucket, name="rel_bias_grad")
    dh2 = _matmul(dq1, w["sw_q"], mode="tt", add=dz, add_scale=ALPHA, name="sw_q_dx", tm=1024, tn=1024)
    dh2 = _matmul(dkv, w["kv"], mode="nt", add=dh2, name="kv_dx", tm=1024, tn=1024)
    g_sw_q = _matmul(h2b, dq1, mode="tt", name="sw_q_dw", tm=1024, tn=1024, tk=2048, also_bf16=True)
    g_kv = _matmul(h2b, dkv, mode="tn", name="kv_dw", tm=1024, tn=512, tk=2048, also_bf16=True)
    tok = emit(1, dict(sw_q=g_sw_q, sw_out=g_sw_out, kv=g_kv, ffn_in=gf1["ffn_in"], ffn_out=gf1["ffn_out"]))

    dz, dzb, g["ln_ffn0"], dact0 = _ln_bwd_matmul(dh2, z2, w["ln_ffn_g"][0], w["ln_ffn_b"][0], w["ffn_out"][0],
                                                  name="ln_ffn0_bwd_down_dx", after=tok)
    dh1, gf0 = _ffn_bwd(dzb, dz, h1b, u0, act0, w, 0, dact0)
    dz, dzb, g["ln_mix0"], dog = _ln_bwd_matmul(dh1, z1, w["ln_mix_g"][0], w["ln_mix_b"][0], w["hg_out"],
                                                name="ln_mix0_bwd_hg_out_dx")
    g_hg_out = _matmul(og, dzb, mode="tn", name="hg_out_dw", tm=1024, tn=1024, tk=2048, also_bf16=True)
    tok = emit(2, dict(hg_out=g_hg_out, ffn_in=gf0["ffn_in"], ffn_out=gf0["ffn_out"]))
    dpre, g["lb_logits"], g["gnorm"] = _hgrn_bwd(pre, w["lb_logits"], w["gnorm"], states, dog, name="hgrn_bwd", after=tok)
    tok = emit(3, dict(hg_in=_matmul(xb, dpre, mode="tn", planes="n", name="hg_in_dw", tm=1024, tn=1024, tk=2048, also_bf16=True)))
    dx = _matmul_planes_nt(dpre, w["hg_in"], dz, add_scale=ALPHA, name="hg_in_dx", after=tok)
    g["conv0"], g["conv1"] = gf0["conv"], gf1["conv"]
    return loss_tile, dx, g


def _adamw(wt, ga, gb, m, v, *, name, rows=None, prev=None):
    R, Cc = wt.shape
    r0, n = rows if rows is not None else (0, R)
    tr = _tile(n, 256, SUBLANES) if n % SUBLANES == 0 else n
    assert r0 % tr == 0
    c1 = 1.0 - ADAM_B1 ** ADAM_STEP
    c2 = 1.0 - ADAM_B2 ** ADAM_STEP
    n_in = 5

    def body(*refs):
        w_ref, ga_ref, gb_ref, m_ref, v_ref = refs[:n_in]
        g_ = ga_ref[...] + gb_ref[...]
        g_ref, d_ref, nm_ref, nv_ref = refs[-4:]
        nm = ADAM_B1 * m_ref[...] + (1.0 - ADAM_B1) * g_
        nv = ADAM_B2 * v_ref[...] + (1.0 - ADAM_B2) * (g_ * g_)
        g_ref[...] = g_
        d_ref[...] = -ADAM_LR * ((nm / c1) / (jnp.sqrt(nv / c2) + ADAM_EPS) + ADAM_WD * w_ref[...])
        nm_ref[...] = nm
        nv_ref[...] = nv

    full = pl.BlockSpec((tr, Cc), lambda i: (i + r0 // tr, 0))
    part = pl.BlockSpec((tr, Cc), lambda i: (i, 0))
    args = (wt, ga, gb, m, v)
    in_specs = [full, part, part, full, full]
    aliases = {}
    if prev is not None:
        args, in_specs = args + tuple(prev), in_specs + [ANY_SPEC] * 4
        aliases = {n_in + t: t for t in range(4)}
    return pl.pallas_call(
        body, name=name, grid=(n // tr,), in_specs=in_specs, out_specs=[full] * 4,
        out_shape=[jax.ShapeDtypeStruct((R, Cc), F32)] * 4, input_output_aliases=aliases,
        compiler_params=_params(("parallel",)),
    )(*args)


def _adamw_small(ws, gs, ms, vs, *, name):
    n = len(ws)
    c1 = 1.0 - ADAM_B1 ** ADAM_STEP
    c2 = 1.0 - ADAM_B2 ** ADAM_STEP

    def body(*refs):
        w_refs, g_refs, m_refs, v_refs = (refs[k * n:(k + 1) * n] for k in range(4))
        d_refs, nm_refs, nv_refs = (refs[(4 + k) * n:(5 + k) * n] for k in range(3))
        for i in range(n):
            g_ = g_refs[i][...]
            nm = ADAM_B1 * m_refs[i][...] + (1.0 - ADAM_B1) * g_
            nv = ADAM_B2 * v_refs[i][...] + (1.0 - ADAM_B2) * (g_ * g_)
            d_refs[i][...] = -ADAM_LR * ((nm / c1) / (jnp.sqrt(nv / c2) + ADAM_EPS) + ADAM_WD * w_refs[i][...])
            nm_refs[i][...] = nm
            nv_refs[i][...] = nv

    vm = pl.BlockSpec(memory_space=pltpu.VMEM)
    out = pl.pallas_call(
        body, name=name, in_specs=[vm] * (4 * n), out_specs=[vm] * (3 * n),
        out_shape=[jax.ShapeDtypeStruct(w.shape, F32) for w in ws] * 3,
    )(*ws, *gs, *ms, *vs)
    return out[:n], out[n:2 * n], out[2 * n:]


HBM_SPEC = pl.BlockSpec(memory_space=pltpu.HBM)
SEM_SPEC = pl.BlockSpec(memory_space=pltpu.SEMAPHORE)
VMEM_SPEC = pl.BlockSpec(memory_space=pltpu.VMEM)
DATAFLOW = pltpu.SideEffectType.DATAFLOW_SIDE_EFFECTING


def _in_hbm(a):
    return pltpu.with_memory_space_constraint(a, pltpu.HBM)


def _place():
    return lax.axis_index("x"), lax.axis_index("y"), lax.axis_index("c")


def _other_chips(x, y):
    return [(1 - x, y), (x, 1 - y), (1 - x, 1 - y)]


def _sum8(vs, *, name):
    n = len(vs)

    def body(*refs):
        v_refs, all_refs, o_refs = refs[:n], refs[n:2 * n], refs[2 * n:3 * n]
        send_sems, recv_sems, local_sems = refs[3 * n:]
        x, y, c = _place()
        me, sibling = (x, y, c), (x, y, 1 - c)
        chips = _other_chips(x, y)

        def slot(i, px, py, pc):
            return all_refs[i].at[4 * px + 2 * py + pc]

        def copy(i, k, block, to, src=None):
            return pltpu.make_async_remote_copy(
                src_ref=slot(i, *block) if src is None else src, dst_ref=slot(i, *block),
                send_sem=send_sems.at[7 * i + k], recv_sem=recv_sems.at[7 * i + k], device_id=to, device_id_type=MESH)

        mine = [pltpu.make_async_copy(v_refs[i], slot(i, *me), local_sems.at[i]) for i in range(n)]
        for cp in mine:
            cp.start()
        first = [copy(i, 0, me, sibling, src=v_refs[i]) for i in range(n)]
        first += [copy(i, 1 + j, me, (*chip, c), src=v_refs[i]) for i in range(n) for j, chip in enumerate(chips)]
        for cp in first:
            cp.start()
        passed = []
        for i in range(n):
            for j, chip in enumerate(chips):
                copy(i, 1 + j, (*chip, c), me).wait_recv()
                passed.append(copy(i, 4 + j, (*chip, c), sibling))
                passed[-1].start()
        for i in range(n):
            copy(i, 0, sibling, me).wait_recv()
            for j, chip in enumerate(chips):
                copy(i, 4 + j, (*chip, 1 - c), me).wait_recv()
        for cp in first + passed:
            cp.wait_send()
        for cp in mine:
            cp.wait()
        for i in range(n):
            acc = all_refs[i][0]
            for d in range(1, N_DEV):
                acc = acc + all_refs[i][d]
            o_refs[i][...] = acc

    return pl.pallas_call(
        body, name=name, in_specs=[VMEM_SPEC] * n, out_specs=[VMEM_SPEC] * (2 * n),
        out_shape=[jax.ShapeDtypeStruct((N_DEV,) + v.shape, F32) for v in vs] + [jax.ShapeDtypeStruct(v.shape, F32) for v in vs],
        scratch_shapes=[pltpu.SemaphoreType.DMA((7 * n,)), pltpu.SemaphoreType.DMA((7 * n,)), pltpu.SemaphoreType.DMA((n,))],
        compiler_params=pltpu.CompilerParams(vmem_limit_bytes=VMEM_LIMIT),
    )(*vs)[n:]


def _swap_copies(src, land, send, recv):
    x, y, c = _place()
    return [pltpu.make_async_remote_copy(src_ref=src[i], dst_ref=land[i], send_sem=send.at[i], recv_sem=recv.at[i],
                                         device_id=(x, y, 1 - c), device_id_type=MESH) for i in range(len(src))]


def _swap_start(vs, *, name):
    n = len(vs)

    def body(*refs):
        src, land, send, recv, token = refs[:n], refs[n:2 * n], refs[2 * n], refs[2 * n + 1], refs[-1]
        for cp in _swap_copies(src, land, send, recv):
            cp.start()
        token[...] = jnp.zeros_like(token)

    lands = [lax.empty(v.shape, v.dtype) for v in vs]
    sems = pltpu.SemaphoreType.DMA((n,))
    out = pl.pallas_call(
        body, name=name, in_specs=[HBM_SPEC] * (2 * n),
        out_specs=[SEM_SPEC, SEM_SPEC] + [HBM_SPEC] * (2 * n) + [VMEM_SPEC],
        out_shape=[sems, sems] + [pltpu.HBM(a.shape, a.dtype) for a in list(vs) + lands]
        + [jax.ShapeDtypeStruct((SUBLANES, LANES), F32)],
        input_output_aliases={i: 2 + i for i in range(2 * n)},
        compiler_params=pltpu.CompilerParams(has_side_effects=DATAFLOW),
    )(*[_in_hbm(a) for a in list(vs) + lands])
    return (out[0], out[1], out[2:2 + n], out[2 + n:2 + 2 * n]), out[-1]


def _swap_wait(handle, after, *, name):
    send_sems, recv_sems, srcs, lands = handle
    n = len(srcs)

    def body(*refs):
        src, land, send, recv = refs[:n], refs[n:2 * n], refs[2 * n], refs[2 * n + 1]
        for cp in _swap_copies(src, land, send, recv):
            cp.wait_send()
            cp.wait_recv()

    both = list(srcs) + list(lands)
    out = pl.pallas_call(
        body, name=name, in_specs=[HBM_SPEC] * (2 * n) + [SEM_SPEC, SEM_SPEC, ANY_SPEC], out_specs=[HBM_SPEC] * (2 * n),
        out_shape=[pltpu.HBM(a.shape, a.dtype) for a in both],
        input_output_aliases={i: i for i in range(2 * n)},
        compiler_params=pltpu.CompilerParams(has_side_effects=DATAFLOW),
    )(*both, send_sems, recv_sems, after)
    return out[:n], out[n:]


def _gather_copies(srcs, lands, send, recv, sibling=False):
    x, y, c = _place()
    out = []
    for i, (src, land) in enumerate(zip(srcs, lands)):
        half = land.shape[1] // 2
        rows = pl.ds(c * half, half)
        for k, (px, py) in enumerate(_other_chips(x, y)):
            if sibling:
                src_ref, dst_ref, to = src.at[2 * px + py, rows], land.at[2 * px + py, rows], (x, y, 1 - c)
            else:
                src_ref, dst_ref, to = src.at[rows], land.at[2 * x + y, rows], (px, py, c)
            out.append(pltpu.make_async_remote_copy(src_ref=src_ref, dst_ref=dst_ref, send_sem=send.at[3 * i + k],
                                                    recv_sem=recv.at[3 * i + k], device_id=to, device_id_type=MESH))
    return out


def _gather_arrivals(lands, send, recv, sibling=False):
    x, y, c = _place()
    out = []
    for i, land in enumerate(lands):
        half = land.shape[1] // 2
        rows = pl.ds(((1 - c) if sibling else c) * half, half)
        for k, (px, py) in enumerate(_other_chips(x, y)):
            part = land.at[2 * px + py, rows]
            out.append(pltpu.make_async_remote_copy(src_ref=part, dst_ref=part, send_sem=send.at[3 * i + k],
                                                    recv_sem=recv.at[3 * i + k],
                                                    device_id=(x, y, 1 - c) if sibling else (px, py, c), device_id_type=MESH))
    return out


def _own_copies(srcs, lands, sems):
    x, y, _ = _place()
    return [pltpu.make_async_copy(src, land.at[2 * x + y], sems.at[i]) for i, (src, land) in enumerate(zip(srcs, lands))]


def _gather_start(shards, after, *, name, own_too):
    n = len(shards)

    def body(*refs):
        srcs, lands, (send, recv, own), token = refs[:n], refs[n:2 * n], refs[2 * n:2 * n + 3], refs[-1]
        for cp in _gather_copies(srcs, lands, send, recv) + (_own_copies(srcs, lands, own) if own_too else []):
            cp.start()
        token[...] = jnp.zeros_like(token)

    lands = [lax.empty((N_CHIPS,) + s.shape, s.dtype) for s in shards]
    sems = pltpu.SemaphoreType.DMA((3 * n,))
    body, xs, xa = _after(body, 2 * n, after)
    out = pl.pallas_call(
        body, name=name, in_specs=[HBM_SPEC] * (2 * n) + xs,
        out_specs=[SEM_SPEC] * 3 + [HBM_SPEC] * (2 * n) + [VMEM_SPEC],
        out_shape=[sems, sems, pltpu.SemaphoreType.DMA((n,))] + [pltpu.HBM(a.shape, a.dtype) for a in list(shards) + lands]
        + [jax.ShapeDtypeStruct((SUBLANES, LANES), F32)],
        input_output_aliases={i: 3 + i for i in range(2 * n)},
        compiler_params=pltpu.CompilerParams(has_side_effects=DATAFLOW),
    )(*[_in_hbm(a) for a in list(shards) + lands], *xa)
    return (out[:3], out[3:3 + n], out[3 + n:3 + 2 * n], own_too), out[-1]


def _gather_wait(handle, after, *, name):
    sems, srcs, lands, own_too = handle
    n = len(srcs)

    def body(*refs):
        srcs_, lands_, (send, recv, own) = refs[:n], refs[n:2 * n], refs[2 * n:2 * n + 3]
        for cp in _gather_copies(srcs_, lands_, send, recv):
            cp.wait_send()
        for cp in _gather_arrivals(lands_, send, recv):
            cp.wait_recv()
        for cp in _own_copies(srcs_, lands_, own) if own_too else []:
            cp.wait()

    both = list(srcs) + list(lands)
    out = pl.pallas_call(
        body, name=name, in_specs=[HBM_SPEC] * (2 * n) + [SEM_SPEC] * 3 + [ANY_SPEC], out_specs=[HBM_SPEC] * (2 * n),
        out_shape=[pltpu.HBM(a.shape, a.dtype) for a in both],
        input_output_aliases={i: i for i in range(2 * n)},
        compiler_params=pltpu.CompilerParams(has_side_effects=DATAFLOW),
    )(*both, *sems, after)
    return out[n:]


def _fill_sibling(lands, *, name):
    n = len(lands)

    def body(*refs):
        ins, outs, send_sems, recv_sems = refs[:n], refs[n:2 * n], refs[2 * n], refs[2 * n + 1]
        cps = _gather_copies(ins, outs, send_sems, recv_sems, sibling=True)
        for cp in cps:
            cp.start()
        for cp in _gather_arrivals(outs, send_sems, recv_sems, sibling=True):
            cp.wait_recv()
        for cp in cps:
            cp.wait_send()

    return pl.pallas_call(
        body, name=name, in_specs=[HBM_SPEC] * n, out_specs=[HBM_SPEC] * n,
        out_shape=[jax.ShapeDtypeStruct(a.shape, a.dtype) for a in lands],
        scratch_shapes=[pltpu.SemaphoreType.DMA((3 * n,)), pltpu.SemaphoreType.DMA((3 * n,))],
        input_output_aliases={i: i for i in range(n)},
    )(*lands)


def _scatter_copies(src, land, send, recv):
    x, y, c = _place()
    return [pltpu.make_async_remote_copy(src_ref=src[i].at[2 * px + py], dst_ref=land[i].at[k], send_sem=send.at[3 * i + k],
                                         recv_sem=recv.at[3 * i + k], device_id=(px, py, c), device_id_type=MESH)
            for i in range(len(src)) for k, (px, py) in enumerate(_other_chips(x, y))]


def _scatter_start(pieces, *, name):
    n = len(pieces)

    def body(*refs):
        src, land, send, recv, token = refs[:n], refs[n:2 * n], refs[2 * n], refs[2 * n + 1], refs[-1]
        for cp in _scatter_copies(src, land, send, recv):
            cp.start()
        token[...] = jnp.zeros_like(token)

    lands = [lax.empty((3,) + p.shape[1:], p.dtype) for p in pieces]
    sems = pltpu.SemaphoreType.DMA((3 * n,))
    out = pl.pallas_call(
        body, name=name, in_specs=[HBM_SPEC] * (2 * n),
        out_specs=[SEM_SPEC, SEM_SPEC] + [HBM_SPEC] * (2 * n) + [VMEM_SPEC],
        out_shape=[sems, sems] + [pltpu.HBM(a.shape, a.dtype) for a in pieces + lands]
        + [jax.ShapeDtypeStruct((SUBLANES, LANES), F32)],
        input_output_aliases={i: 2 + i for i in range(2 * n)},
        compiler_params=pltpu.CompilerParams(has_side_effects=DATAFLOW),
    )(*[_in_hbm(a) for a in pieces + lands])
    return (out[0], out[1], out[2:2 + n], out[2 + n:2 + 2 * n]), out[-1]


def _scatter_wait(handle, after, *, name):
    send_sems, recv_sems, srcs, lands = handle
    n = len(srcs)

    def body(*refs):
        src, land, send, recv = refs[:n], refs[n:2 * n], refs[2 * n], refs[2 * n + 1]
        for cp in _scatter_copies(src, land, send, recv):
            cp.wait_send()
            cp.wait_recv()

    both = list(srcs) + list(lands)
    out = pl.pallas_call(
        body, name=name, in_specs=[HBM_SPEC] * (2 * n) + [SEM_SPEC, SEM_SPEC, ANY_SPEC], out_specs=[HBM_SPEC] * (2 * n),
        out_shape=[pltpu.HBM(a.shape, a.dtype) for a in both],
        input_output_aliases={i: i for i in range(2 * n)},
        compiler_params=pltpu.CompilerParams(has_side_effects=DATAFLOW),
    )(*both, send_sems, recv_sems, after)
    return out[n:]


def _to_bf16(x, *, name, after=None):
    T, Dm = x.shape
    tr = _tile(T, 512, 2 * SUBLANES)

    def body(x_ref, o_ref):
        o_ref[...] = x_ref[...].astype(BF16)

    blk = pl.BlockSpec((tr, Dm), lambda i: (i, 0))
    body, xs, xa = _after(body, 1, after)
    return pl.pallas_call(
        body, name=name, grid=(T // tr,), in_specs=[blk] + xs, out_specs=blk, out_shape=jax.ShapeDtypeStruct((T, Dm), BF16),
        compiler_params=_params(("parallel",)),
    )(x, *xa)


def _chip_sum(pieces, got, chip, *, name):
    _, R, Cc = pieces.shape
    tr = _tile(R, 512, 2 * SUBLANES)

    def body(chip_ref, a_ref, g_ref, o_ref):
        o_ref[...] = ((a_ref[...] + g_ref[0].astype(F32)) + g_ref[1].astype(F32)) + g_ref[2].astype(F32)

    return pl.pallas_call(
        body, name=name,
        grid_spec=pltpu.PrefetchScalarGridSpec(
            num_scalar_prefetch=1, grid=(R // tr,),
            in_specs=[pl.BlockSpec((None, tr, Cc), lambda i, ch: (ch[0], i, 0)),
                      pl.BlockSpec((3, tr, Cc), lambda i, ch: (0, i, 0))],
            out_specs=pl.BlockSpec((tr, Cc), lambda i, ch: (i, 0))),
        out_shape=jax.ShapeDtypeStruct((R, Cc), F32),
        compiler_params=_params(("parallel",)),
    )(chip, pieces, got)


PACK_COLS = 1024
SMALL_ROWS = 32


def kernel(x, hgrn_w_in, hgrn_lb_logits, hgrn_gnorm_w, hgrn_w_out, swa_w_q, swa_sinks, swa_w_out, shared_w_kv, rel_bias, ffn_w_in, ffn_conv_w, ffn_conv_b, ffn_w_out, ln_mix_g, ln_mix_b, ln_ffn_g, ln_ffn_b, loss_target, m_hgrn_w_in, m_hgrn_lb_logits, m_hgrn_gnorm_w, m_hgrn_w_out, m_swa_w_q, m_swa_sinks, m_swa_w_out, m_shared_w_kv, m_rel_bias, m_ffn_w_in, m_ffn_conv_w, m_ffn_conv_b, m_ffn_w_out, m_ln_mix_g, m_ln_mix_b, m_ln_ffn_g, m_ln_ffn_b, v_hgrn_w_in, v_hgrn_lb_logits, v_hgrn_gnorm_w, v_hgrn_w_out, v_swa_w_q, v_swa_sinks, v_swa_w_out, v_shared_w_kv, v_rel_bias, v_ffn_w_in, v_ffn_conv_w, v_ffn_conv_b, v_ffn_w_out, v_ln_mix_g, v_ln_mix_b, v_ln_ffn_g, v_ln_ffn_b):
    xi, yi, ci = _place()
    chip = 2 * xi + yi
    Dm = D_MODEL
    FC = 2 * FFN_DIM // N_CHIPS
    Fo = FFN_DIM // N_CHIPS
    Dq = Dm // N_CHIPS
    bf = lambda a: a.astype(BF16)

    small = jnp.concatenate([hgrn_lb_logits.reshape(-1), ffn_conv_w.reshape(-1)])
    n_small = small.shape[0]
    bits = jnp.concatenate(_split3(small))
    bits = jnp.pad(bits, (0, SMALL_ROWS * PACK_COLS - 3 * n_small)).reshape(SMALL_ROWS, PACK_COLS)
    groups = [[bf(hgrn_w_in[0]), bf(hgrn_w_out[0]), bits],
              [bf(swa_w_q[0]), bf(swa_w_out[0]), bf(shared_w_kv), bf(ffn_w_in[0]), bf(ffn_w_out[0])],
              [bf(ffn_w_in[1]), bf(ffn_w_out[1])]]

    def gathered(k, landed):
        lands = _fill_sibling(landed, name=f"gather_w{k}_fill")
        if k > 0:
            return lands
        return [lax.dynamic_update_slice(land, shard[None], (chip,) + (0,) * shard.ndim)
                for land, shard in zip(lands, groups[0])]

    handle0, token0 = _gather_start(groups[0], None, name="gather_w0_start", own_too=False)
    xb = _to_bf16(x[0], name="x_to_bf16", after=token0)
    corner = lambda a: a[:2 * SUBLANES, :LANES]
    casts_done = corner(xb) + sum(corner(a) for a in groups[1] + groups[2])
    w_in, w_hg_out, small_all = gathered(0, _gather_wait(handle0, casts_done, name="gather_w0_wait"))
    handle1, token1 = _gather_start(groups[1], w_in, name="gather_w1_start", own_too=True)
    parts = small_all.reshape(N_CHIPS, -1)[:, :3 * n_small].reshape(N_CHIPS, 3, n_small).astype(F32)
    vals = (parts[:, 0] + parts[:, 1]) + parts[:, 2]
    lb_full = vals[:, :2 * Dq].reshape(N_CHIPS, 2, Dq).transpose(1, 0, 2).reshape(2, Dm)
    cw_full = vals[:, 2 * Dq:].reshape(N_CHIPS, DEPTH, 3, FC).transpose(1, 2, 0, 3).reshape(DEPTH, 3, 2 * FFN_DIM)

    got = {"handle": handle1}

    def more_weights(k, after):
        ws = gathered(k, _gather_wait(got.pop("handle"), after, name=f"gather_w{k}_wait"))
        if k == 1:
            got["handle"], token2 = _gather_start(groups[2], ws[0], name="gather_w2_start", own_too=True)
            w_q, w_o, w_kv, w_fi, w_fo = ws
            got.update(ffn_in={0: w_fi}, ffn_out={0: w_fo.reshape(FFN_DIM, Dm)})
            return {"sw_q": w_q.reshape(Dm, Dm), "sw_out": w_o.reshape(Dm, Dm), "kv": w_kv.reshape(Dm, 2 * KV_DIM),
                    "token": token2, "ffn_in": got["ffn_in"], "ffn_out": got["ffn_out"]}
        w_fi, w_fo = ws
        return {"ffn_in": {**got["ffn_in"], 1: w_fi}, "ffn_out": {**got["ffn_out"], 1: w_fo.reshape(FFN_DIM, Dm)}}

    w = {
        "hg_in": w_in, "hg_out": w_hg_out.reshape(Dm, Dm), "token": token1,
        "lb_logits": lb_full, "gnorm": hgrn_gnorm_w, "sinks": swa_sinks, "rel_bias": rel_bias,
        "conv_w_a": [cw_full[l, :, :FFN_DIM] for l in range(DEPTH)],
        "conv_w_b": [cw_full[l, :, FFN_DIM:] for l in range(DEPTH)],
        "conv_b_a": [ffn_conv_b[l:l + 1, :FFN_DIM] for l in range(DEPTH)],
        "conv_b_b": [ffn_conv_b[l:l + 1, FFN_DIM:] for l in range(DEPTH)],
        "ln_mix_g": [ln_mix_g[l:l + 1] for l in range(DEPTH)], "ln_mix_b": [ln_mix_b[l:l + 1] for l in range(DEPTH)],
        "ln_ffn_g": [ln_ffn_g[l:l + 1] for l in range(DEPTH)], "ln_ffn_b": [ln_ffn_b[l:l + 1] for l in range(DEPTH)],
    }

    sent = {}

    def emit(k, gd):
        rows4 = lambda a: a.reshape(N_CHIPS, -1, a.shape[-1])
        order = {1: ["sw_q", "sw_out", "kv", "ffn_in", "ffn_out"], 2: ["ffn_in", "ffn_out", "hg_out"], 3: ["hg_in"]}[k]
        as_pieces = lambda a, nme: a if nme in ("ffn_in", "hg_in") else rows4(a)
        handle, token = _scatter_start([as_pieces(gd[nme][1], nme) for nme in order], name=f"scatter_g{k}_start")
        sent[k] = (handle, [as_pieces(gd[nme][0], nme) for nme in order])
        return token

    loss_tile, grad_x, g = _local_step(x[0], xb, loss_target[0], w, more_weights, emit)

    wts = dict(hgrn_w_in=hgrn_w_in, hgrn_lb_logits=hgrn_lb_logits, hgrn_gnorm_w=hgrn_gnorm_w, hgrn_w_out=hgrn_w_out,
               swa_w_q=swa_w_q, swa_sinks=swa_sinks, swa_w_out=swa_w_out, shared_w_kv=shared_w_kv, rel_bias=rel_bias,
               ffn_w_in=ffn_w_in, ffn_conv_w=ffn_conv_w, ffn_conv_b=ffn_conv_b, ffn_w_out=ffn_w_out,
               ln_mix_g=ln_mix_g, ln_mix_b=ln_mix_b, ln_ffn_g=ln_ffn_g, ln_ffn_b=ln_ffn_b)
    ms = dict(hgrn_w_in=m_hgrn_w_in, hgrn_lb_logits=m_hgrn_lb_logits, hgrn_gnorm_w=m_hgrn_gnorm_w, hgrn_w_out=m_hgrn_w_out,
              swa_w_q=m_swa_w_q, swa_sinks=m_swa_sinks, swa_w_out=m_swa_w_out, shared_w_kv=m_shared_w_kv, rel_bias=m_rel_bias,
              ffn_w_in=m_ffn_w_in, ffn_conv_w=m_ffn_conv_w, ffn_conv_b=m_ffn_conv_b, ffn_w_out=m_ffn_w_out,
              ln_mix_g=m_ln_mix_g, ln_mix_b=m_ln_mix_b, ln_ffn_g=m_ln_ffn_g, ln_ffn_b=m_ln_ffn_b)
    vs = dict(hgrn_w_in=v_hgrn_w_in, hgrn_lb_logits=v_hgrn_lb_logits, hgrn_gnorm_w=v_hgrn_gnorm_w, hgrn_w_out=v_hgrn_w_out,
              swa_w_q=v_swa_w_q, swa_sinks=v_swa_sinks, swa_w_out=v_swa_w_out, shared_w_kv=v_shared_w_kv, rel_bias=v_rel_bias,
              ffn_w_in=v_ffn_w_in, ffn_conv_w=v_ffn_conv_w, ffn_conv_b=v_ffn_conv_b, ffn_w_out=v_ffn_w_out,
              ln_mix_g=v_ln_mix_g, ln_mix_b=v_ln_mix_b, ln_ffn_g=v_ln_ffn_g, ln_ffn_b=v_ln_ffn_b)
    names = list(wts)
    grads, delta, new_m, new_v = {}, {}, {}, {}

    def update(n, ga, gb, layer=None, prev=None):
        r2 = lambda a: a.reshape(-1, a.shape[-1])
        rows = None if layer is None else (layer * ga.shape[0], ga.shape[0])
        return _adamw(r2(wts[n]), ga, gb, r2(ms[n]), r2(vs[n]), rows=rows, prev=prev,
                      name=f"adamw_{n}" + ("" if layer is None else f"_{layer}"))

    def keep(n, res):
        grads[n], delta[n], new_m[n], new_v[n] = [a.reshape(wts[n].shape) for a in res]

    chip1 = jnp.reshape(chip, (1,)).astype(jnp.int32)
    after, swaps = grad_x, {}
    for k in (1, 2, 3):
        handle, pieces = sent[k]
        lands = _scatter_wait(handle, after, name=f"scatter_g{k}_wait")
        parts = [_chip_sum(p, l, chip1, name=f"scatter_g{k}_sum{i}") for i, (p, l) in enumerate(zip(pieces, lands))]
        swaps[k], after = _swap_start(parts, name=f"scatter_g{k}_swap_start")
    for k in (1, 2, 3):
        parts, sibs = _swap_wait(swaps[k], after, name=f"scatter_g{k}_swap_wait")
        if k == 1:
            for n, ga, gb in zip(["swa_w_q", "swa_w_out", "shared_w_kv"], parts[:3], sibs[:3]):
                keep(n, update(n, ga, gb))
            ffn_in_1 = update("ffn_w_in", parts[3], sibs[3], layer=1)
            ffn_out_1 = update("ffn_w_out", parts[4], sibs[4], layer=1)
            after = ffn_out_1[3]
        elif k == 2:
            keep("ffn_w_in", update("ffn_w_in", parts[0], sibs[0], layer=0, prev=ffn_in_1))
            keep("ffn_w_out", update("ffn_w_out", parts[1], sibs[1], layer=0, prev=ffn_out_1))
            keep("hgrn_w_out", update("hgrn_w_out", parts[2], sibs[2]))
            after = new_v["hgrn_w_out"]
        else:
            keep("hgrn_w_in", update("hgrn_w_in", parts[0], sibs[0]))

    small_keys = ["lb_logits", "gnorm", "sinks", "rel_bias", "conv0", "conv1", "ln_mix0", "ln_mix1", "ln_ffn0", "ln_ffn1"]
    flat2 = lambda a: a.reshape(-1, a.shape[-1])
    sums = _sum8([loss_tile] + [flat2(g[k]) for k in small_keys], name="sum_small")
    loss = sums[0][0, 0]
    sg = {k: v.reshape(g[k].shape) for k, v in zip(small_keys, sums[1:])}
    conv = [sg["conv0"], sg["conv1"]]
    g_cw = jnp.stack([jnp.concatenate([conv[l][0, :3], conv[l][1, :3]], axis=1) for l in range(DEPTH)])
    g_cb = jnp.stack([jnp.concatenate([conv[l][0, 3], conv[l][1, 3]], axis=0) for l in range(DEPTH)])
    ln = lambda nme, r: jnp.stack([sg[nme + "0"][r], sg[nme + "1"][r]])
    small_g = dict(hgrn_lb_logits=lax.dynamic_slice_in_dim(sg["lb_logits"], chip * Dq, Dq, axis=1),
                   hgrn_gnorm_w=sg["gnorm"], swa_sinks=sg["sinks"], rel_bias=sg["rel_bias"],
                   ffn_conv_w=lax.dynamic_slice_in_dim(g_cw, chip * FC, FC, axis=2), ffn_conv_b=g_cb,
                   ln_mix_g=ln("ln_mix", 0), ln_mix_b=ln("ln_mix", 1), ln_ffn_g=ln("ln_ffn", 0), ln_ffn_b=ln("ln_ffn", 1))
    small_names = list(small_g)
    d_, m_, v_ = _adamw_small([flat2(wts[n]) for n in small_names], [flat2(small_g[n]) for n in small_names],
                              [flat2(ms[n]) for n in small_names], [flat2(vs[n]) for n in small_names], name="adamw_small")
    for n, a, b_, c_ in zip(small_names, d_, m_, v_):
        shp = wts[n].shape
        grads[n], delta[n], new_m[n], new_v[n] = small_g[n], a.reshape(shp), b_.reshape(shp), c_.reshape(shp)

    return (loss, grad_x[None], *[grads[n] for n in names], *[delta[n] for n in names],
            *[new_m[n] for n in names], *[new_v[n] for n in names])
```
